```python
import math
import jax, jax.numpy as jnp
from jax import lax
import numpy as np

D_MODEL = 1024
BATCH = 8
SEQ = 2048
DEPTH = 2

N_A_LAYERS = DEPTH // 2
N_B_LAYERS = DEPTH - N_A_LAYERS
S5_GROUP = 16
S5_GROUPS = D_MODEL // S5_GROUP
S5_STATE = 64
DT_MIN = 1e-3
DT_MAX = 1e-1
LAMBDA_RE_MAX = -1e-4
HEAD_DIM = 64
N_Q_HEADS = D_MODEL // HEAD_DIM
N_KV_HEADS = 4
Q_PER_KV = N_Q_HEADS // N_KV_HEADS
WINDOW = 128
BLOCK = 128
D_FF = 4 * D_MODEL
NORM_EPS = 1e-5

kernel_name = "yoco_s5_swa_sink_hybrid"


def rmsnorm(x, g):
    x32 = x.astype(jnp.float32)
    y = x32 * lax.rsqrt(jnp.mean(x32 * x32, axis=-1, keepdims=True) + NORM_EPS)
    return (y * g.astype(jnp.float32)).astype(x.dtype)


def s5_mixer(u, a_re, a_im, log_dt, b_re, b_im, c_re, c_im, d_skip, w_glu, b_glu):
    bsz, seq, _ = u.shape
    f32 = jnp.float32
    u32 = u.astype(f32)
    ug = u32.reshape(bsz, seq, S5_GROUPS, S5_GROUP)
    lam = lax.complex(jnp.minimum(a_re.astype(f32), LAMBDA_RE_MAX), a_im.astype(f32))
    dt = jnp.exp(log_dt.astype(f32))[:, None]
    lam_bar = jnp.exp(lam * dt)
    b_c = lax.complex(b_re.astype(f32), b_im.astype(f32))
    b_bar = ((lam_bar - 1.0) / lam)[:, :, None] * b_c
    bu = lax.complex(jnp.einsum('blgc,gpc->blgp', ug, jnp.real(b_bar)),
                     jnp.einsum('blgc,gpc->blgp', ug, jnp.imag(b_bar)))
    a = jnp.broadcast_to(lam_bar[None, None], (1, seq, S5_GROUPS, S5_STATE))

    def combine(left, right):
        a_l, b_l = left
        a_r, b_r = right
        return a_r * a_l, a_r * b_l + b_r

    _, states = lax.associative_scan(combine, (a, bu), axis=1)
    y = (jnp.einsum('blgp,gcp->blgc', jnp.real(states), c_re.astype(f32))
         - jnp.einsum('blgp,gcp->blgc', jnp.imag(states), c_im.astype(f32)))
    y = y.reshape(bsz, seq, D_MODEL) + d_skip.astype(f32) * u32
    y = jax.nn.gelu(y).astype(u.dtype)
    z = y @ w_glu + b_glu
    val, gate = jnp.split(z, 2, axis=-1)
    return val * jax.nn.sigmoid(gate)


def shared_kv(h, g_kv, w_kv, b_kv):
    bsz, seq, _ = h.shape
    kv = rmsnorm(h, g_kv) @ w_kv + b_kv
    k, v = jnp.split(kv, 2, axis=-1)
    return (k.reshape(bsz, seq, N_KV_HEADS, HEAD_DIM),
            v.reshape(bsz, seq, N_KV_HEADS, HEAD_DIM))


def window_blocks(t, nb):
    bsz = t.shape[0]
    tb = t.reshape(bsz, nb, BLOCK, N_KV_HEADS, HEAD_DIM)
    prev = jnp.concatenate([jnp.zeros_like(tb[:, :1]), tb[:, :-1]], axis=1)
    return jnp.moveaxis(jnp.concatenate([prev, tb], axis=2), 1, 0)


def swa_sink_attention(hn, k, v, w_q, b_q, sinks, w_o, b_o):
    bsz, seq, _ = hn.shape
    nb = seq // BLOCK
    f32 = jnp.float32
    scale = 1.0 / math.sqrt(HEAD_DIM)
    q = (hn @ w_q + b_q).reshape(bsz, nb, BLOCK, N_KV_HEADS, Q_PER_KV, HEAD_DIM)
    q = jnp.moveaxis(q, 1, 0)
    kw = window_blocks(k, nb)
    vw = window_blocks(v, nb)
    qi = jnp.arange(BLOCK)[:, None]
    kj = jnp.arange(2 * BLOCK)[None, :]
    diff = qi + BLOCK - kj
    band = (diff >= 0) & (diff < WINDOW)
    sink = sinks.astype(f32).reshape(N_KV_HEADS, Q_PER_KV)[None, :, :, None]

    def block_fn(args):
        n, qb, kb, vb = args
        s = jnp.einsum('bqkgd,bskd->bkgqs', qb.astype(f32), kb.astype(f32)) * scale
        valid = band & ((n - 1) * BLOCK + kj >= 0)
        s = jnp.where(valid, s, -jnp.inf)
        m = jnp.maximum(jnp.max(s, axis=-1), sink)
        p = jnp.exp(s - m[..., None])
        denom = jnp.sum(p, axis=-1) + jnp.exp(sink - m)
        o = jnp.einsum('bkgqs,bskd->bqkgd', p / denom[..., None], vb.astype(f32))
        return o.astype(hn.dtype)

    o = lax.map(block_fn, (jnp.arange(nb), q, kw, vw))
    o = jnp.moveaxis(o, 0, 1).reshape(bsz, seq, N_Q_HEADS * HEAD_DIM)
    return o @ w_o + b_o


def sq_relu_mlp(h, w_in, w_out):
    return jnp.square(jax.nn.relu(h @ w_in)) @ w_out


def _fwd_setup_inputs(seed: int = 0) -> dict:
    key = jax.random.key(seed)
    ks = jax.random.split(key, 26)
    f32 = jnp.float32
    nrm = lambda k, shape, s: jax.random.normal(k, shape, f32) * s
    G, P, GC, D = S5_GROUPS, S5_STATE, S5_GROUP, D_MODEL
    HQ = N_Q_HEADS * HEAD_DIM
    HKV = N_KV_HEADS * HEAD_DIM
    x = jax.random.normal(ks[0], (BATCH, SEQ, D), f32)
    norm_mix = 1.0 + nrm(ks[1], (DEPTH, D), 0.02)
    norm_mlp = 1.0 + nrm(ks[2], (DEPTH, D), 0.02)
    norm_kv = 1.0 + nrm(ks[3], (D,), 0.02)
    norm_final = 1.0 + nrm(ks[4], (D,), 0.02)
    s5_a_re = -0.5 + nrm(ks[5], (N_A_LAYERS, G, P), 0.01)
    s5_a_im = (jnp.pi * jnp.arange(P, dtype=f32))[None, None, :] + nrm(ks[6], (N_A_LAYERS, G, P), 0.01)
    s5_log_dt = jax.random.uniform(ks[7], (N_A_LAYERS, G), f32, math.log(DT_MIN), math.log(DT_MAX))
    s5_b_re = nrm(ks[8], (N_A_LAYERS, G, P, GC), (2.0 * GC) ** -0.5)
    s5_b_im = nrm(ks[9], (N_A_LAYERS, G, P, GC), (2.0 * GC) ** -0.5)
    s5_c_re = nrm(ks[10], (N_A_LAYERS, G, GC, P), P ** -0.5)
    s5_c_im = nrm(ks[11], (N_A_LAYERS, G, GC, P), P ** -0.5)
    s5_d = nrm(ks[12], (N_A_LAYERS, D), 1.0)
    s5_w_glu = nrm(ks[13], (N_A_LAYERS, D, 2 * D), D ** -0.5)
    s5_b_glu = nrm(ks[14], (N_A_LAYERS, 2 * D), 0.01)
    w_kv = nrm(ks[15], (D, 2 * HKV), D ** -0.5)
    b_kv = nrm(ks[16], (2 * HKV,), 0.01)
    w_q = nrm(ks[17], (N_B_LAYERS, D, HQ), D ** -0.5)
    b_q = nrm(ks[18], (N_B_LAYERS, HQ), 0.01)
    sinks = nrm(ks[19], (N_B_LAYERS, N_Q_HEADS), 0.5)
    w_o = nrm(ks[20], (N_B_LAYERS, HQ, D), HQ ** -0.5)
    b_o = nrm(ks[21], (N_B_LAYERS, D), 0.01)
    w_mlp_in = nrm(ks[22], (DEPTH, D, D_FF), D ** -0.5)
    w_mlp_out = nrm(ks[23], (DEPTH, D_FF, D), D_FF ** -0.5)
    return {"x": x, "norm_mix": norm_mix, "norm_mlp": norm_mlp, "norm_kv": norm_kv,
            "norm_final": norm_final, "s5_a_re": s5_a_re, "s5_a_im": s5_a_im,
            "s5_log_dt": s5_log_dt, "s5_b_re": s5_b_re, "s5_b_im": s5_b_im,
            "s5_c_re": s5_c_re, "s5_c_im": s5_c_im, "s5_d": s5_d, "s5_w_glu": s5_w_glu,
            "s5_b_glu": s5_b_glu, "w_kv": w_kv, "b_kv": b_kv, "w_q": w_q, "b_q": b_q,
            "sinks": sinks, "w_o": w_o, "b_o": b_o, "w_mlp_in": w_mlp_in,
            "w_mlp_out": w_mlp_out}


def _fwd_reference(x, norm_mix, norm_mlp, norm_kv, norm_final, s5_a_re, s5_a_im, s5_log_dt,
              s5_b_re, s5_b_im, s5_c_re, s5_c_im, s5_d, s5_w_glu, s5_b_glu, w_kv, b_kv,
              w_q, b_q, sinks, w_o, b_o, w_mlp_in, w_mlp_out):
    h = x
    k = v = None
    for layer in range(DEPTH):
        if layer < N_A_LAYERS:
            hn = rmsnorm(h, norm_mix[layer])
            h = h + s5_mixer(hn, s5_a_re[layer], s5_a_im[layer], s5_log_dt[layer],
                             s5_b_re[layer], s5_b_im[layer], s5_c_re[layer], s5_c_im[layer],
                             s5_d[layer], s5_w_glu[layer], s5_b_glu[layer])
        else:
            if layer == N_A_LAYERS:
                k, v = shared_kv(h, norm_kv, w_kv, b_kv)
            bi = layer - N_A_LAYERS
            hn = rmsnorm(h, norm_mix[layer])
            h = h + swa_sink_attention(hn, k, v, w_q[bi], b_q[bi], sinks[bi], w_o[bi], b_o[bi])
        h = h + sq_relu_mlp(rmsnorm(h, norm_mlp[layer]), w_mlp_in[layer], w_mlp_out[layer])
    return rmsnorm(h, norm_final)


import jax as _jax
import jax.numpy as _jnp

TWIN_FORMAT = 'train_step'
FWD_PARAMS = ['x', 'norm_mix', 'norm_mlp', 'norm_kv', 'norm_final', 's5_a_re', 's5_a_im', 's5_log_dt', 's5_b_re', 's5_b_im', 's5_c_re', 's5_c_im', 's5_d', 's5_w_glu', 's5_b_glu', 'w_kv', 'b_kv', 'w_q', 'b_q', 'sinks', 'w_o', 'b_o', 'w_mlp_in', 'w_mlp_out']
TWIN_WEIGHTS = ['norm_mix', 'norm_mlp', 'norm_kv', 'norm_final', 's5_a_re', 's5_a_im', 's5_log_dt', 's5_b_re', 's5_b_im', 's5_c_re', 's5_c_im', 's5_d', 's5_w_glu', 's5_b_glu', 'w_kv', 'b_kv', 'w_q', 'b_q', 'sinks', 'w_o', 'b_o', 'w_mlp_in', 'w_mlp_out']
TWIN_DIFF_INPUT = 'x'
TWIN_INPUTS = ['x', 'norm_mix', 'norm_mlp', 'norm_kv', 'norm_final', 's5_a_re', 's5_a_im', 's5_log_dt', 's5_b_re', 's5_b_im', 's5_c_re', 's5_c_im', 's5_d', 's5_w_glu', 's5_b_glu', 'w_kv', 'b_kv', 'w_q', 'b_q', 'sinks', 'w_o', 'b_o', 'w_mlp_in', 'w_mlp_out', 'loss_target', 'm_norm_mix', 'm_norm_mlp', 'm_norm_kv', 'm_norm_final', 'm_s5_a_re', 'm_s5_a_im', 'm_s5_log_dt', 'm_s5_b_re', 'm_s5_b_im', 'm_s5_c_re', 'm_s5_c_im', 'm_s5_d', 'm_s5_w_glu', 'm_s5_b_glu', 'm_w_kv', 'm_b_kv', 'm_w_q', 'm_b_q', 'm_sinks', 'm_w_o', 'm_b_o', 'm_w_mlp_in', 'm_w_mlp_out', 'v_norm_mix', 'v_norm_mlp', 'v_norm_kv', 'v_norm_final', 'v_s5_a_re', 'v_s5_a_im', 'v_s5_log_dt', 'v_s5_b_re', 'v_s5_b_im', 'v_s5_c_re', 'v_s5_c_im', 'v_s5_d', 'v_s5_w_glu', 'v_s5_b_glu', 'v_w_kv', 'v_b_kv', 'v_w_q', 'v_b_q', 'v_sinks', 'v_w_o', 'v_b_o', 'v_w_mlp_in', 'v_w_mlp_out']
TWIN_OUTPUTS = ['loss', 'grad_x', 'grad_norm_mix', 'grad_norm_mlp', 'grad_norm_kv', 'grad_norm_final', 'grad_s5_a_re', 'grad_s5_a_im', 'grad_s5_log_dt', 'grad_s5_b_re', 'grad_s5_b_im', 'grad_s5_c_re', 'grad_s5_c_im', 'grad_s5_d', 'grad_s5_w_glu', 'grad_s5_b_glu', 'grad_w_kv', 'grad_b_kv', 'grad_w_q', 'grad_b_q', 'grad_sinks', 'grad_w_o', 'grad_b_o', 'grad_w_mlp_in', 'grad_w_mlp_out', 'delta_norm_mix', 'delta_norm_mlp', 'delta_norm_kv', 'delta_norm_final', 'delta_s5_a_re', 'delta_s5_a_im', 'delta_s5_log_dt', 'delta_s5_b_re', 'delta_s5_b_im', 'delta_s5_c_re', 'delta_s5_c_im', 'delta_s5_d', 'delta_s5_w_glu', 'delta_s5_b_glu', 'delta_w_kv', 'delta_b_kv', 'delta_w_q', 'delta_b_q', 'delta_sinks', 'delta_w_o', 'delta_b_o', 'delta_w_mlp_in', 'delta_w_mlp_out', 'new_m_norm_mix', 'new_m_norm_mlp', 'new_m_norm_kv', 'new_m_norm_final', 'new_m_s5_a_re', 'new_m_s5_a_im', 'new_m_s5_log_dt', 'new_m_s5_b_re', 'new_m_s5_b_im', 'new_m_s5_c_re', 'new_m_s5_c_im', 'new_m_s5_d', 'new_m_s5_w_glu', 'new_m_s5_b_glu', 'new_m_w_kv', 'new_m_b_kv', 'new_m_w_q', 'new_m_b_q', 'new_m_sinks', 'new_m_w_o', 'new_m_b_o', 'new_m_w_mlp_in', 'new_m_w_mlp_out', 'new_v_norm_mix', 'new_v_norm_mlp', 'new_v_norm_kv', 'new_v_norm_final', 'new_v_s5_a_re', 'new_v_s5_a_im', 'new_v_s5_log_dt', 'new_v_s5_b_re', 'new_v_s5_b_im', 'new_v_s5_c_re', 'new_v_s5_c_im', 'new_v_s5_d', 'new_v_s5_w_glu', 'new_v_s5_b_glu', 'new_v_w_kv', 'new_v_b_kv', 'new_v_w_q', 'new_v_b_q', 'new_v_sinks', 'new_v_w_o', 'new_v_b_o', 'new_v_w_mlp_in', 'new_v_w_mlp_out']
TWIN_LEAF_KINDS = {'loss': 'loss', 'grad_x': 'grad_x', 'grad_norm_mix': 'grad_w', 'grad_norm_mlp': 'grad_w', 'grad_norm_kv': 'grad_w', 'grad_norm_final': 'grad_w', 'grad_s5_a_re': 'grad_w', 'grad_s5_a_im': 'grad_w', 'grad_s5_log_dt': 'grad_w', 'grad_s5_b_re': 'grad_w', 'grad_s5_b_im': 'grad_w', 'grad_s5_c_re': 'grad_w', 'grad_s5_c_im': 'grad_w', 'grad_s5_d': 'grad_w', 'grad_s5_w_glu': 'grad_w', 'grad_s5_b_glu': 'grad_w', 'grad_w_kv': 'grad_w', 'grad_b_kv': 'grad_w', 'grad_w_q': 'grad_w', 'grad_b_q': 'grad_w', 'grad_sinks': 'grad_w', 'grad_w_o': 'grad_w', 'grad_b_o': 'grad_w', 'grad_w_mlp_in': 'grad_w', 'grad_w_mlp_out': 'grad_w', 'delta_norm_mix': 'delta_w', 'delta_norm_mlp': 'delta_w', 'delta_norm_kv': 'delta_w', 'delta_norm_final': 'delta_w', 'delta_s5_a_re': 'delta_w', 'delta_s5_a_im': 'delta_w', 'delta_s5_log_dt': 'delta_w', 'delta_s5_b_re': 'delta_w', 'delta_s5_b_im': 'delta_w', 'delta_s5_c_re': 'delta_w', 'delta_s5_c_im': 'delta_w', 'delta_s5_d': 'delta_w', 'delta_s5_w_glu': 'delta_w', 'delta_s5_b_glu': 'delta_w', 'delta_w_kv': 'delta_w', 'delta_b_kv': 'delta_w', 'delta_w_q': 'delta_w', 'delta_b_q': 'delta_w', 'delta_sinks': 'delta_w', 'delta_w_o': 'delta_w', 'delta_b_o': 'delta_w', 'delta_w_mlp_in': 'delta_w', 'delta_w_mlp_out': 'delta_w', 'new_m_norm_mix': 'new_m', 'new_m_norm_mlp': 'new_m', 'new_m_norm_kv': 'new_m', 'new_m_norm_final': 'new_m', 'new_m_s5_a_re': 'new_m', 'new_m_s5_a_im': 'new_m', 'new_m_s5_log_dt': 'new_m', 'new_m_s5_b_re': 'new_m', 'new_m_s5_b_im': 'new_m', 'new_m_s5_c_re': 'new_m', 'new_m_s5_c_im': 'new_m', 'new_m_s5_d': 'new_m', 'new_m_s5_w_glu': 'new_m', 'new_m_s5_b_glu': 'new_m', 'new_m_w_kv': 'new_m', 'new_m_b_kv': 'new_m', 'new_m_w_q': 'new_m', 'new_m_b_q': 'new_m', 'new_m_sinks': 'new_m', 'new_m_w_o': 'new_m', 'new_m_b_o': 'new_m', 'new_m_w_mlp_in': 'new_m', 'new_m_w_mlp_out': 'new_m', 'new_v_norm_mix': 'new_v', 'new_v_norm_mlp': 'new_v', 'new_v_norm_kv': 'new_v', 'new_v_norm_final': 'new_v', 'new_v_s5_a_re': 'new_v', 'new_v_s5_a_im': 'new_v', 'new_v_s5_log_dt': 'new_v', 'new_v_s5_b_re': 'new_v', 'new_v_s5_b_im': 'new_v', 'new_v_s5_c_re': 'new_v', 'new_v_s5_c_im': 'new_v', 'new_v_s5_d': 'new_v', 'new_v_s5_w_glu': 'new_v', 'new_v_s5_b_glu': 'new_v', 'new_v_w_kv': 'new_v', 'new_v_b_kv': 'new_v', 'new_v_w_q': 'new_v', 'new_v_b_q': 'new_v', 'new_v_sinks': 'new_v', 'new_v_w_o': 'new_v', 'new_v_b_o': 'new_v', 'new_v_w_mlp_in': 'new_v', 'new_v_w_mlp_out': 'new_v'}


def _forward(args):
    return _fwd_reference(*[args[k] for k in FWD_PARAMS])


def _output_shape():
    out = _jax.eval_shape(lambda: _forward(_fwd_setup_inputs(0)))
    return out.shape, out.dtype

N_MICROBATCH = 1
ADAM_LR = 0.001
ADAM_B1 = 0.9
ADAM_B2 = 0.999
ADAM_EPS = 1e-08
ADAM_WD = 0.01
ADAM_STEP = 10
PER_EXAMPLE_BATCH_AXIS = {'x': 0, 'loss_target': 0}
SHARED_INPUTS = []
_WEIGHT_DTYPES = {'norm_mix': _jnp.float32, 'norm_mlp': _jnp.float32, 'norm_kv': _jnp.float32, 'norm_final': _jnp.float32, 's5_a_re': _jnp.float32, 's5_a_im': _jnp.float32, 's5_log_dt': _jnp.float32, 's5_b_re': _jnp.float32, 's5_b_im': _jnp.float32, 's5_c_re': _jnp.float32, 's5_c_im': _jnp.float32, 's5_d': _jnp.float32, 's5_w_glu': _jnp.float32, 's5_b_glu': _jnp.float32, 'w_kv': _jnp.float32, 'b_kv': _jnp.float32, 'w_q': _jnp.float32, 'b_q': _jnp.float32, 'sinks': _jnp.float32, 'w_o': _jnp.float32, 'b_o': _jnp.float32, 'w_mlp_in': _jnp.float32, 'w_mlp_out': _jnp.float32}
MOMENT_SCALE = {'norm_mix': 4.641567e-02, 'norm_mlp': 1.040618e-01, 'norm_kv': 3.686399e-02, 'norm_final': 1.652504e+01, 's5_a_re': 3.334422e-03, 's5_a_im': 3.686216e-03, 's5_log_dt': 1.768181e+00, 's5_b_re': 2.196486e-03, 's5_b_im': 2.196650e-03, 's5_c_re': 3.157099e-03, 's5_c_im': 3.183585e-03, 's5_d': 5.626145e-02, 's5_w_glu': 3.805204e-02, 's5_b_glu': 7.629246e-02, 'w_kv': 5.848630e-02, 'b_kv': 1.788762e-01, 'w_q': 1.233668e-02, 'b_q': 1.223764e-02, 'sinks': 1.492095e-02, 'w_o': 4.506545e-02, 'b_o': 1.319298e-01, 'w_mlp_in': 5.135164e-02, 'w_mlp_out': 1.032441e-01}


def _to_microbatches(a, axis):
    t = _jnp.moveaxis(a, axis, 0)
    t = t.reshape((N_MICROBATCH, t.shape[0] // N_MICROBATCH) + t.shape[1:])
    return _jnp.moveaxis(t, 1, axis + 1)


def setup_inputs(seed: int = 0) -> dict:
    inp = _fwd_setup_inputs(seed)
    key = _jax.random.fold_in(_jax.random.key(seed), 7919)
    shape, _ = _output_shape()
    out = dict(inp)
    out["loss_target"] = _jax.random.normal(_jax.random.fold_in(key, 0), shape, _jnp.float32)
    for i, name in enumerate(TWIN_WEIGHTS):
        w = inp[name].astype(_jnp.float32)
        if MOMENT_SCALE is None:
            s = _jnp.sqrt(_jnp.mean(_jnp.square(w)) + 1e-30)
        else:
            s = MOMENT_SCALE[name]
        km, kv = _jax.random.split(_jax.random.fold_in(key, i + 1))
        out[name] = w
        out["m_" + name] = s * _jax.random.normal(km, w.shape, _jnp.float32)
        out["v_" + name] = (s * s) * _jax.random.uniform(kv, w.shape, _jnp.float32, 0.5, 1.5)
    if N_MICROBATCH > 1:
        for name, axis in PER_EXAMPLE_BATCH_AXIS.items():
            out[name] = _to_microbatches(out[name], axis)
    return {'x': out['x'], 'norm_mix': out['norm_mix'], 'norm_mlp': out['norm_mlp'], 'norm_kv': out['norm_kv'], 'norm_final': out['norm_final'], 's5_a_re': out['s5_a_re'], 's5_a_im': out['s5_a_im'], 's5_log_dt': out['s5_log_dt'], 's5_b_re': out['s5_b_re'], 's5_b_im': out['s5_b_im'], 's5_c_re': out['s5_c_re'], 's5_c_im': out['s5_c_im'], 's5_d': out['s5_d'], 's5_w_glu': out['s5_w_glu'], 's5_b_glu': out['s5_b_glu'], 'w_kv': out['w_kv'], 'b_kv': out['b_kv'], 'w_q': out['w_q'], 'b_q': out['b_q'], 'sinks': out['sinks'], 'w_o': out['w_o'], 'b_o': out['b_o'], 'w_mlp_in': out['w_mlp_in'], 'w_mlp_out': out['w_mlp_out'], 'loss_target': out['loss_target'], 'm_norm_mix': out['m_norm_mix'], 'm_norm_mlp': out['m_norm_mlp'], 'm_norm_kv': out['m_norm_kv'], 'm_norm_final': out['m_norm_final'], 'm_s5_a_re': out['m_s5_a_re'], 'm_s5_a_im': out['m_s5_a_im'], 'm_s5_log_dt': out['m_s5_log_dt'], 'm_s5_b_re': out['m_s5_b_re'], 'm_s5_b_im': out['m_s5_b_im'], 'm_s5_c_re': out['m_s5_c_re'], 'm_s5_c_im': out['m_s5_c_im'], 'm_s5_d': out['m_s5_d'], 'm_s5_w_glu': out['m_s5_w_glu'], 'm_s5_b_glu': out['m_s5_b_glu'], 'm_w_kv': out['m_w_kv'], 'm_b_kv': out['m_b_kv'], 'm_w_q': out['m_w_q'], 'm_b_q': out['m_b_q'], 'm_sinks': out['m_sinks'], 'm_w_o': out['m_w_o'], 'm_b_o': out['m_b_o'], 'm_w_mlp_in': out['m_w_mlp_in'], 'm_w_mlp_out': out['m_w_mlp_out'], 'v_norm_mix': out['v_norm_mix'], 'v_norm_mlp': out['v_norm_mlp'], 'v_norm_kv': out['v_norm_kv'], 'v_norm_final': out['v_norm_final'], 'v_s5_a_re': out['v_s5_a_re'], 'v_s5_a_im': out['v_s5_a_im'], 'v_s5_log_dt': out['v_s5_log_dt'], 'v_s5_b_re': out['v_s5_b_re'], 'v_s5_b_im': out['v_s5_b_im'], 'v_s5_c_re': out['v_s5_c_re'], 'v_s5_c_im': out['v_s5_c_im'], 'v_s5_d': out['v_s5_d'], 'v_s5_w_glu': out['v_s5_w_glu'], 'v_s5_b_glu': out['v_s5_b_glu'], 'v_w_kv': out['v_w_kv'], 'v_b_kv': out['v_b_kv'], 'v_w_q': out['v_w_q'], 'v_b_q': out['v_b_q'], 'v_sinks': out['v_sinks'], 'v_w_o': out['v_w_o'], 'v_b_o': out['v_b_o'], 'v_w_mlp_in': out['v_w_mlp_in'], 'v_w_mlp_out': out['v_w_mlp_out']}


def _loss(weights, diff, rest, loss_target):
    with _jax.named_scope("forward"):
        args = {**rest, TWIN_DIFF_INPUT: diff, **{k: w.astype(_WEIGHT_DTYPES[k]) for k, w in weights.items()}}
        y = _forward(args)
    with _jax.named_scope("loss_head"):
        err = _jnp.square(y.astype(_jnp.float32) - loss_target)
        return 0.5 * _jnp.sum(_jnp.mean(err, axis=-1)) if err.ndim else 0.5 * err


def _adamw(w, g, m, v):
    m = ADAM_B1 * m + (1.0 - ADAM_B1) * g
    v = ADAM_B2 * v + (1.0 - ADAM_B2) * _jnp.square(g)
    m_hat = m / (1.0 - ADAM_B1 ** ADAM_STEP)
    v_hat = v / (1.0 - ADAM_B2 ** ADAM_STEP)
    delta = -ADAM_LR * (m_hat / (_jnp.sqrt(v_hat) + ADAM_EPS) + ADAM_WD * w)
    return delta, m, v


def reference(x, norm_mix, norm_mlp, norm_kv, norm_final, s5_a_re, s5_a_im, s5_log_dt, s5_b_re, s5_b_im, s5_c_re, s5_c_im, s5_d, s5_w_glu, s5_b_glu, w_kv, b_kv, w_q, b_q, sinks, w_o, b_o, w_mlp_in, w_mlp_out, loss_target, m_norm_mix, m_norm_mlp, m_norm_kv, m_norm_final, m_s5_a_re, m_s5_a_im, m_s5_log_dt, m_s5_b_re, m_s5_b_im, m_s5_c_re, m_s5_c_im, m_s5_d, m_s5_w_glu, m_s5_b_glu, m_w_kv, m_b_kv, m_w_q, m_b_q, m_sinks, m_w_o, m_b_o, m_w_mlp_in, m_w_mlp_out, v_norm_mix, v_norm_mlp, v_norm_kv, v_norm_final, v_s5_a_re, v_s5_a_im, v_s5_log_dt, v_s5_b_re, v_s5_b_im, v_s5_c_re, v_s5_c_im, v_s5_d, v_s5_w_glu, v_s5_b_glu, v_w_kv, v_b_kv, v_w_q, v_b_q, v_sinks, v_w_o, v_b_o, v_w_mlp_in, v_w_mlp_out):
    given = dict(x=x, norm_mix=norm_mix, norm_mlp=norm_mlp, norm_kv=norm_kv, norm_final=norm_final, s5_a_re=s5_a_re, s5_a_im=s5_a_im, s5_log_dt=s5_log_dt, s5_b_re=s5_b_re, s5_b_im=s5_b_im, s5_c_re=s5_c_re, s5_c_im=s5_c_im, s5_d=s5_d, s5_w_glu=s5_w_glu, s5_b_glu=s5_b_glu, w_kv=w_kv, b_kv=b_kv, w_q=w_q, b_q=b_q, sinks=sinks, w_o=w_o, b_o=b_o, w_mlp_in=w_mlp_in, w_mlp_out=w_mlp_out, loss_target=loss_target, m_norm_mix=m_norm_mix, m_norm_mlp=m_norm_mlp, m_norm_kv=m_norm_kv, m_norm_final=m_norm_final, m_s5_a_re=m_s5_a_re, m_s5_a_im=m_s5_a_im, m_s5_log_dt=m_s5_log_dt, m_s5_b_re=m_s5_b_re, m_s5_b_im=m_s5_b_im, m_s5_c_re=m_s5_c_re, m_s5_c_im=m_s5_c_im, m_s5_d=m_s5_d, m_s5_w_glu=m_s5_w_glu, m_s5_b_glu=m_s5_b_glu, m_w_kv=m_w_kv, m_b_kv=m_b_kv, m_w_q=m_w_q, m_b_q=m_b_q, m_sinks=m_sinks, m_w_o=m_w_o, m_b_o=m_b_o, m_w_mlp_in=m_w_mlp_in, m_w_mlp_out=m_w_mlp_out, v_norm_mix=v_norm_mix, v_norm_mlp=v_norm_mlp, v_norm_kv=v_norm_kv, v_norm_final=v_norm_final, v_s5_a_re=v_s5_a_re, v_s5_a_im=v_s5_a_im, v_s5_log_dt=v_s5_log_dt, v_s5_b_re=v_s5_b_re, v_s5_b_im=v_s5_b_im, v_s5_c_re=v_s5_c_re, v_s5_c_im=v_s5_c_im, v_s5_d=v_s5_d, v_s5_w_glu=v_s5_w_glu, v_s5_b_glu=v_s5_b_glu, v_w_kv=v_w_kv, v_b_kv=v_b_kv, v_w_q=v_w_q, v_b_q=v_b_q, v_sinks=v_sinks, v_w_o=v_w_o, v_b_o=v_b_o, v_w_mlp_in=v_w_mlp_in, v_w_mlp_out=v_w_mlp_out)
    weights = {n: given[n] for n in TWIN_WEIGHTS}
    shared = {n: given[n] for n in SHARED_INPUTS}
    per_example = {n: given[n] for n in ['x']}
    grad_fn = _jax.value_and_grad(_loss, argnums=(0, 1))

    def one_microbatch(ex, loss_target):
        ex = dict(ex)
        diff = ex.pop(TWIN_DIFF_INPUT)
        return grad_fn(weights, diff, {**shared, **ex}, loss_target)

    if N_MICROBATCH == 1:
        loss, (grad_w, grad_x) = one_microbatch(per_example, given["loss_target"])
    else:
        def body(carry, xs):
            loss_sum, grad_sum = carry
            l_k, (gw_k, gx_k) = one_microbatch(xs[0], xs[1])
            with _jax.named_scope("update"):
                return (loss_sum + l_k, _jax.tree.map(_jnp.add, grad_sum, gw_k)), gx_k

        init = (_jnp.zeros((), _jnp.float32), _jax.tree.map(_jnp.zeros_like, weights))
        (loss, grad_w), grad_x = _jax.lax.scan(body, init, (per_example, given["loss_target"]))
    with _jax.named_scope("update"):
        delta_w, new_m, new_v = {}, {}, {}
        for n in TWIN_WEIGHTS:
            delta_w[n], new_m[n], new_v[n] = _adamw(weights[n], grad_w[n], given["m_" + n], given["v_" + n])
    return (loss, grad_x, *[grad_w[n] for n in TWIN_WEIGHTS], *[delta_w[n] for n in TWIN_WEIGHTS],
            *[new_m[n] for n in TWIN_WEIGHTS], *[new_v[n] for n in TWIN_WEIGHTS])
```

```python
import functools
import math

import jax
import jax.numpy as jnp
from jax import lax
from jax.experimental import pallas as pl
from jax.experimental.pallas import tpu as pltpu

F32 = jnp.float32
BF16 = jnp.bfloat16

D_MODEL = 1024
S5_GROUPS = 64
S5_GROUP = 16
S5_STATE = 64
N_KV = 4
N_Q = 16
HEAD_DIM = 64
BLOCK = 128
NORM_EPS = 1e-5
LAMBDA_RE_MAX = -1e-4
ADAM_LR, ADAM_B1, ADAM_B2, ADAM_EPS, ADAM_WD, ADAM_STEP = 0.001, 0.9, 0.999, 1e-08, 0.01, 10

VMEM_LIMIT_BYTES = 56 * 1024 * 1024
S5_CHUNK = 256
S5_BLOCKS = 4
MESH = pl.DeviceIdType.MESH


def _params(sem=None):
    return pltpu.CompilerParams(dimension_semantics=sem, vmem_limit_bytes=VMEM_LIMIT_BYTES)


def _sds(shape, dtype):
    return jax.ShapeDtypeStruct(shape, dtype)


def _rms_hat(xv):
    r = lax.rsqrt(jnp.mean(xv * xv, axis=-1, keepdims=True) + NORM_EPS)
    return xv * r, r


def rms_fwd(name, x, gains, out_dtypes, tm=256):
    n_rows, d = x.shape
    ng = len(gains)

    def body(x_ref, *refs):
        xh, _ = _rms_hat(x_ref[...])
        for g_ref, o_ref in zip(refs[:ng], refs[ng:]):
            o_ref[...] = (xh * g_ref[...]).astype(o_ref.dtype)

    row = pl.BlockSpec((tm, d), lambda i: (i, 0))
    vec = pl.BlockSpec((1, d), lambda i: (0, 0))
    return pl.pallas_call(
        body, grid=(n_rows // tm,), in_specs=[row] + [vec] * ng, out_specs=[row] * ng,
        out_shape=[_sds((n_rows, d), dt) for dt in out_dtypes], name=name,
        compiler_params=_params(("parallel",)))(x, *gains)


def rms_bwd(name, x, dys, gains, res, tm=256):
    n_rows, d = x.shape
    ng = len(gains)

    def body(x_ref, res_ref, *refs):
        dy_refs, g_refs = refs[:ng], refs[ng:2 * ng]
        dx_ref, dxb_ref, cs_ref = refs[2 * ng:2 * ng + 3]
        dg_refs = refs[2 * ng + 3:]
        i = pl.program_id(0)
        xh, r = _rms_hat(x_ref[...])
        dxh = jnp.zeros_like(xh)
        dgs = []
        for dy_ref, g_ref in zip(dy_refs, g_refs):
            dy = dy_ref[...].astype(F32)
            dxh = dxh + dy * g_ref[...]
            dgs.append(jnp.sum(dy * xh, axis=0, keepdims=True))
        dx = r * (dxh - xh * jnp.mean(dxh * xh, axis=-1, keepdims=True)) + res_ref[...]
        dx_ref[...] = dx
        dxb_ref[...] = dx.astype(BF16)
        cs = jnp.sum(dx, axis=0, keepdims=True)

        @pl.when(i == 0)
        def _():
            cs_ref[...] = jnp.zeros_like(cs_ref)
            for dg_ref in dg_refs:
                dg_ref[...] = jnp.zeros_like(dg_ref)

        cs_ref[...] += cs
        for dg_ref, dg in zip(dg_refs, dgs):
            dg_ref[...] += dg

    row = pl.BlockSpec((tm, d), lambda i: (i, 0))
    vec = pl.BlockSpec((1, d), lambda i: (0, 0))
    return pl.pallas_call(
        body, grid=(n_rows // tm,), in_specs=[row, row] + [row] * ng + [vec] * ng,
        out_specs=[row, row, vec] + [vec] * ng,
        out_shape=[_sds((n_rows, d), F32), _sds((n_rows, d), BF16), _sds((1, d), F32)] + [_sds((1, d), F32)] * ng,
        name=name, compiler_params=_params(("arbitrary",)))(x, res, *dys, *gains)


def mm_nn(name, a, w, col_offsets, n_out, epilogue, out_dtypes, extras=(), rowvecs=(), tm=1024, tn=512):
    m, k = a.shape
    tm, tn = min(tm, m), min(tn, n_out)
    nw, ne, nr = len(col_offsets), len(extras), len(rowvecs)

    def body(a_ref, *refs):
        w_refs, e_refs, r_refs = refs[:nw], refs[nw:nw + ne], refs[nw + ne:nw + ne + nr]
        o_refs = refs[nw + ne + nr:]
        av = a_ref[...]
        accs = [jnp.dot(av, w_ref[...], preferred_element_type=F32) for w_ref in w_refs]
        outs = epilogue(accs, [e[...] for e in e_refs], [r[...] for r in r_refs])
        for o_ref, o in zip(o_refs, outs):
            o_ref[...] = o.astype(o_ref.dtype)

    def wspec(off):
        return pl.BlockSpec((k, tn), lambda j, i, off=off: (0, off // tn + j))

    def rspec(off):
        return pl.BlockSpec((1, tn), lambda j, i, off=off: (0, off // tn + j))

    tile = pl.BlockSpec((tm, tn), lambda j, i: (i, j))
    in_specs = ([pl.BlockSpec((tm, k), lambda j, i: (i, 0))] + [wspec(o) for o in col_offsets]
                + [tile] * ne + [rspec(o) for _, o in rowvecs])
    return pl.pallas_call(
        body, grid=(n_out // tn, m // tm), in_specs=in_specs, out_specs=[tile] * len(out_dtypes),
        out_shape=[_sds((m, n_out), dt) for dt in out_dtypes], name=name,
        compiler_params=_params(("parallel", "parallel")))(a, *([w] * nw), *extras, *[r for r, _ in rowvecs])


def mm_nt(name, g, w, epilogue, out_dtypes, extras=(), tm=512, tk=512):
    m, n = g.shape
    k = w.shape[0]
    tm, tk = min(tm, m), min(tk, k)
    ne = len(extras)

    def body(g_ref, w_ref, *refs):
        e_refs, o_refs = refs[:ne], refs[ne:]
        acc = lax.dot_general(g_ref[...], w_ref[...], (((1,), (1,)), ((), ())), preferred_element_type=F32)
        outs = epilogue(acc, [e[...] for e in e_refs])
        for o_ref, o in zip(o_refs, outs):
            o_ref[...] = o.astype(o_ref.dtype)

    tile = pl.BlockSpec((tm, tk), lambda i, j: (i, j))
    return pl.pallas_call(
        body, grid=(m // tm, k // tk),
        in_specs=[pl.BlockSpec((tm, n), lambda i, j: (i, 0)), pl.BlockSpec((tk, n), lambda i, j: (j, 0))] + [tile] * ne,
        out_specs=[tile] * len(out_dtypes), out_shape=[_sds((m, k), dt) for dt in out_dtypes], name=name,
        compiler_params=_params(("parallel", "parallel")))(g, w, *extras)


def mm_tn(name, a, g, tk=512, tn=512):
    m, k = a.shape
    n = g.shape[1]
    tk, tn = min(tk, k), min(tn, n)

    def body(a_ref, g_ref, o_ref):
        acc = lax.dot_general(a_ref[...], g_ref[...], (((0,), (0,)), ((), ())), preferred_element_type=F32)
        o_ref[...] = acc.astype(o_ref.dtype)

    return pl.pallas_call(
        body, grid=(k // tk, n // tn),
        in_specs=[pl.BlockSpec((m, tk), lambda i, j: (0, i)), pl.BlockSpec((m, tn), lambda i, j: (0, j))],
        out_specs=pl.BlockSpec((tk, tn), lambda i, j: (i, j)), out_shape=_sds((k, n), BF16), name=name,
        compiler_params=_params(("parallel", "parallel")))(a, g)


def _masked_rows(dst_ref, val, tc):
    for half in range(2):
        v = val[:, half * 128:(half + 1) * 128]
        col = lax.broadcasted_iota(jnp.int32, v.shape, 1) // 32 + 4 * half
        for s8 in range(8):
            rows = jnp.where(col == s8, v, 0.0) if s8 // 4 == half else jnp.zeros_like(v)
            dst_ref.at[half][pl.ds(s8, tc, stride=8), :] = rows


def _staged(ref):
    return jnp.concatenate([ref[0], ref[1]], axis=1)


def _stage(ref, val):
    ref[0] = val[:, 0:128]
    ref[1] = val[:, 128:256]


def _gather_rows(src_ref, tc):
    halves = []
    for half in range(2):
        col = lax.broadcasted_iota(jnp.int32, (tc, 128), 1) // 32 + 4 * half
        out = jnp.zeros((tc, 128), F32)
        for s8 in range(4 * half, 4 * half + 4):
            out = jnp.where(col == s8, src_ref.at[half][pl.ds(s8, tc, stride=8), :], out)
        halves.append(out)
    return jnp.concatenate(halves, axis=1)


def _gelu(x):
    c = math.sqrt(2.0 / math.pi)
    return 0.5 * x * (1.0 + jnp.tanh(c * (x + 0.044715 * x * x * x)))


def _gelu_grad(x):
    c = math.sqrt(2.0 / math.pi)
    t = jnp.tanh(c * (x + 0.044715 * x * x * x))
    return 0.5 * (1.0 + t) + 0.5 * x * (1.0 - t * t) * c * (1.0 + 3.0 * 0.044715 * x * x)


def s5_fwd(u, d_skip, rb, rc, lam_r, lam_i):
    n_rows = u.shape[0]
    tc = min(S5_CHUNK, n_rows)
    nc = n_rows // tc

    def body(u_ref, d_ref, rb_ref, rc_ref, lr_ref, li_ref, ge_ref, y2_ref, cs_ref, lhs, bux, yrows, carry):
        i = pl.program_id(0)

        @pl.when(i == 0)
        def _():
            carry[...] = jnp.zeros_like(carry)

        cs_ref[0] = carry[...]
        for blk in range(S5_BLOCKS):
            _masked_rows(lhs, u_ref[:, blk * 256:(blk + 1) * 256], tc)
            bux[blk] = jnp.dot(_staged(lhs).astype(BF16), rb_ref[blk], preferred_element_type=F32)
        lam = [(lr_ref[blk], li_ref[blk]) for blk in range(S5_BLOCKS)]

        def step(t, c):
            r0 = pl.multiple_of(t * 8, 8)
            new = []
            for blk in range(S5_BLOCKS):
                xr, xi = c[2 * blk], c[2 * blk + 1]
                lr, li = lam[blk]
                nr = lr * xr - li * xi + bux[blk, pl.ds(r0, 8), 0:128]
                ni = lr * xi + li * xr + bux[blk, pl.ds(r0, 8), 128:256]
                bux[blk, pl.ds(r0, 8), 0:128] = nr
                bux[blk, pl.ds(r0, 8), 128:256] = ni
                new += [nr, ni]
            return tuple(new)

        c0 = []
        for blk in range(S5_BLOCKS):
            c0 += [carry[blk, :, 0:128], carry[blk, :, 128:256]]
        cn = lax.fori_loop(0, tc, step, tuple(c0))
        for blk in range(S5_BLOCKS):
            carry[blk, :, 0:128] = cn[2 * blk]
            carry[blk, :, 128:256] = cn[2 * blk + 1]
        for blk in range(S5_BLOCKS):
            _stage(yrows, jnp.dot(bux[blk].astype(BF16), rc_ref[blk], preferred_element_type=F32))
            sl = slice(blk * 256, (blk + 1) * 256)
            y2 = _gather_rows(yrows, tc) + d_ref[:, sl] * u_ref[:, sl]
            y2_ref[:, sl] = y2
            ge_ref[:, sl] = _gelu(y2).astype(BF16)

    row = pl.BlockSpec((tc, D_MODEL), lambda i: (i, 0))
    mat = pl.BlockSpec((S5_BLOCKS, 256, 256), lambda i: (0, 0, 0))
    lamspec = pl.BlockSpec((S5_BLOCKS, 8, 128), lambda i: (0, 0, 0))
    return pl.pallas_call(
        body, grid=(nc,),
        in_specs=[row, pl.BlockSpec((1, D_MODEL), lambda i: (0, 0)), mat, mat, lamspec, lamspec],
        out_specs=[row, row, pl.BlockSpec((1, S5_BLOCKS, 8, 256), lambda i: (i, 0, 0, 0))],
        out_shape=[_sds((n_rows, D_MODEL), BF16), _sds((n_rows, D_MODEL), F32), _sds((nc, S5_BLOCKS, 8, 256), F32)],
        scratch_shapes=[pltpu.VMEM((2, 8 * tc, 128), F32), pltpu.VMEM((S5_BLOCKS, 8 * tc, 256), F32),
                        pltpu.VMEM((2, 8 * tc, 128), F32), pltpu.VMEM((S5_BLOCKS, 8, 256), F32)],
        name="s5_fwd", compiler_params=_params(("arbitrary",)))(u, d_skip, rb, rc, lam_r, lam_i)


def s5_bwd(u, dy2, d_skip, cs, rb, rbt, rct, lam_r, lam_i):
    n_rows = u.shape[0]
    tc = min(S5_CHUNK, n_rows)
    nc = n_rows // tc

    def body(u_ref, dy_ref, d_ref, cs_ref, rb_ref, rbt_ref, rct_ref, lr_ref, li_ref,
             du_ref, dd_ref, drb_ref, drc_ref, dlr_ref, dli_ref, tmp, lhsu, lhsd, xs, adj, acarry):
        i = pl.program_id(0)

        @pl.when(i == 0)
        def _():
            acarry[...] = jnp.zeros_like(acarry)
            dd_ref[...] = jnp.zeros_like(dd_ref)
            drb_ref[...] = jnp.zeros_like(drb_ref)
            drc_ref[...] = jnp.zeros_like(drc_ref)
            dlr_ref[...] = jnp.zeros_like(dlr_ref)
            dli_ref[...] = jnp.zeros_like(dli_ref)

        dd_ref[...] += jnp.sum(dy_ref[...] * u_ref[...], axis=0, keepdims=True)
        for blk in range(S5_BLOCKS):
            sl = slice(blk * 256, (blk + 1) * 256)
            _masked_rows(tmp, u_ref[:, sl], tc)
            lhsu[blk] = _staged(tmp).astype(BF16)
            xs[blk] = jnp.dot(lhsu[blk], rb_ref[blk], preferred_element_type=F32)
            _masked_rows(tmp, dy_ref[:, sl], tc)
            lhsd[blk] = _staged(tmp).astype(BF16)
            adj[blk] = jnp.dot(lhsd[blk], rct_ref[blk], preferred_element_type=F32)
        lam = [(lr_ref[blk], li_ref[blk]) for blk in range(S5_BLOCKS)]

        def fstep(t, c):
            r0 = pl.multiple_of(t * 8, 8)
            new = []
            for blk in range(S5_BLOCKS):
                xr, xi = c[2 * blk], c[2 * blk + 1]
                lr, li = lam[blk]
                nr = lr * xr - li * xi + xs[blk, pl.ds(r0, 8), 0:128]
                ni = lr * xi + li * xr + xs[blk, pl.ds(r0, 8), 128:256]
                xs[blk, pl.ds(r0, 8), 0:128] = nr
                xs[blk, pl.ds(r0, 8), 128:256] = ni
                new += [nr, ni]
            return tuple(new)

        c0 = []
        for blk in range(S5_BLOCKS):
            c0 += [cs_ref[0, blk, :, 0:128], cs_ref[0, blk, :, 128:256]]
        lax.fori_loop(0, tc, fstep, tuple(c0))

        def bstep(k, c):
            t = tc - 1 - k
            r0 = pl.multiple_of(t * 8, 8)
            rp = pl.multiple_of(jnp.maximum(t - 1, 0) * 8, 8)
            first = t == 0
            new_a, new_g = [], []
            for blk in range(S5_BLOCKS):
                ar, ai = c[0][2 * blk], c[0][2 * blk + 1]
                glr, gli = c[1][2 * blk], c[1][2 * blk + 1]
                lr, li = lam[blk]
                nr = lr * ar + li * ai + adj[blk, pl.ds(r0, 8), 0:128]
                ni = lr * ai - li * ar + adj[blk, pl.ds(r0, 8), 128:256]
                adj[blk, pl.ds(r0, 8), 0:128] = nr
                adj[blk, pl.ds(r0, 8), 128:256] = ni
                pr = jnp.where(first, cs_ref[0, blk, :, 0:128], xs[blk, pl.ds(rp, 8), 0:128])
                pi = jnp.where(first, cs_ref[0, blk, :, 128:256], xs[blk, pl.ds(rp, 8), 128:256])
                new_a += [nr, ni]
                new_g += [glr + nr * pr + ni * pi, gli + ni * pr - nr * pi]
            return tuple(new_a), tuple(new_g)

        a0, g0 = [], []
        for blk in range(S5_BLOCKS):
            a0 += [acarry[blk, :, 0:128], acarry[blk, :, 128:256]]
            g0 += [dlr_ref[blk], dli_ref[blk]]
        an, gn = lax.fori_loop(0, tc, bstep, (tuple(a0), tuple(g0)))
        for blk in range(S5_BLOCKS):
            acarry[blk, :, 0:128] = an[2 * blk]
            acarry[blk, :, 128:256] = an[2 * blk + 1]
            dlr_ref[blk] = gn[2 * blk]
            dli_ref[blk] = gn[2 * blk + 1]
        for blk in range(S5_BLOCKS):
            sl = slice(blk * 256, (blk + 1) * 256)
            ab = adj[blk].astype(BF16)
            _stage(tmp, jnp.dot(ab, rbt_ref[blk], preferred_element_type=F32))
            du_ref[:, sl] = _gather_rows(tmp, tc) + d_ref[:, sl] * dy_ref[:, sl]
            drb_ref[blk] += lax.dot_general(lhsu[blk], ab, (((0,), (0,)), ((), ())), preferred_element_type=F32)
            drc_ref[blk] += lax.dot_general(xs[blk].astype(BF16), lhsd[blk], (((0,), (0,)), ((), ())),
                                            preferred_element_type=F32)

    rev = pl.BlockSpec((tc, D_MODEL), lambda i: (nc - 1 - i, 0))
    vec = pl.BlockSpec((1, D_MODEL), lambda i: (0, 0))
    mat = pl.BlockSpec((S5_BLOCKS, 256, 256), lambda i: (0, 0, 0))
    lamspec = pl.BlockSpec((S5_BLOCKS, 8, 128), lambda i: (0, 0, 0))
    big = pltpu.VMEM((S5_BLOCKS, 8 * tc, 256), F32)
    bigb = pltpu.VMEM((S5_BLOCKS, 8 * tc, 256), BF16)
    return pl.pallas_call(
        body, grid=(nc,),
        in_specs=[rev, rev, vec, pl.BlockSpec((1, S5_BLOCKS, 8, 256), lambda i: (nc - 1 - i, 0, 0, 0)),
                  mat, mat, mat, lamspec, lamspec],
        out_specs=[rev, vec, mat, mat, lamspec, lamspec],
        out_shape=[_sds((n_rows, D_MODEL), F32), _sds((1, D_MODEL), F32), _sds((S5_BLOCKS, 256, 256), F32),
                   _sds((S5_BLOCKS, 256, 256), F32), _sds((S5_BLOCKS, 8, 128), F32), _sds((S5_BLOCKS, 8, 128), F32)],
        scratch_shapes=[pltpu.VMEM((2, 8 * tc, 128), F32), bigb, bigb, big, big, pltpu.VMEM((S5_BLOCKS, 8, 256), F32)],
        name="s5_bwd", compiler_params=_params(("arbitrary",)))(u, dy2, d_skip, cs, rb, rbt, rct, lam_r, lam_i)


def _s5_discretise(a_re, a_im, log_dt, b_re, b_im):
    lam = lax.complex(jnp.minimum(a_re, LAMBDA_RE_MAX), a_im)
    dt = jnp.exp(log_dt)[:, None]
    lam_bar = jnp.exp(lam * dt)
    b_bar = ((lam_bar - 1.0) / lam)[:, :, None] * lax.complex(b_re, b_im)
    return jnp.real(lam_bar), jnp.imag(lam_bar), jnp.real(b_bar), jnp.imag(b_bar)


def _s5_matrices(bbar_re, bbar_im, c_re, c_im):
    eye2 = jnp.eye(2, dtype=F32)
    bst = jnp.stack([bbar_re, bbar_im]).reshape(2, S5_BLOCKS, 8, 2, S5_STATE, S5_GROUP)
    bt = jnp.transpose(bst, (1, 2, 3, 5, 0, 4))
    rb = (bt[:, :, :, :, :, None, :] * eye2[None, None, :, None, None, :, None]).reshape(S5_BLOCKS, 256, 256)
    cst = jnp.stack([c_re, -c_im]).reshape(2, S5_BLOCKS, 8, 2, S5_GROUP, S5_STATE)
    ct = jnp.transpose(cst, (1, 0, 5, 2, 3, 4))
    rc = (ct[:, :, None, :, :, :, :] * eye2[None, None, :, None, None, :, None]).reshape(S5_BLOCKS, 256, 256)
    return rb, rc


def _s5_matrix_grads(drb, drc):
    x = drb.reshape(S5_BLOCKS, 8, 2, S5_GROUP, 2, 2, S5_STATE)
    db = jnp.stack([x[:, :, 0, :, :, 0, :], x[:, :, 1, :, :, 1, :]], axis=2)
    db = jnp.transpose(db, (4, 0, 1, 2, 5, 3)).reshape(2, S5_GROUPS, S5_STATE, S5_GROUP)
    y = drc.reshape(S5_BLOCKS, 2, 2, S5_STATE, 8, 2, S5_GROUP)
    dc = jnp.stack([y[:, :, 0, :, :, 0, :], y[:, :, 1, :, :, 1, :]], axis=4)
    dc = jnp.transpose(dc, (1, 0, 3, 4, 5, 2)).reshape(2, S5_GROUPS, S5_GROUP, S5_STATE)
    return db[0], db[1], dc[0], -dc[1]


NEG = -1e30


def _attn_masks(n):
    qi = lax.broadcasted_iota(jnp.int32, (BLOCK, BLOCK), 0)
    kj = lax.broadcasted_iota(jnp.int32, (BLOCK, BLOCK), 1)
    return jnp.logical_and(kj > qi, n > 0), kj <= qi


def _attn_probs(qh, kp, kc, sink, mask_p, mask_c):
    scale = 1.0 / math.sqrt(HEAD_DIM)
    nt = (((1,), (1,)), ((), ()))
    sp = jnp.where(mask_p, lax.dot_general(qh, kp, nt, preferred_element_type=F32) * scale, NEG)
    sc = jnp.where(mask_c, lax.dot_general(qh, kc, nt, preferred_element_type=F32) * scale, NEG)
    m = jnp.maximum(jnp.maximum(jnp.max(sp, axis=-1, keepdims=True), jnp.max(sc, axis=-1, keepdims=True)), sink)
    pp = jnp.exp(sp - m)
    pc = jnp.exp(sc - m)
    ps = jnp.exp(sink - m)
    denom = jnp.sum(pp, axis=-1, keepdims=True) + jnp.sum(pc, axis=-1, keepdims=True) + ps
    return pp / denom, pc / denom, ps / denom


def attn_fwd(q, kv, sinks):
    n_rows = q.shape[0]
    nb = n_rows // BLOCK

    def body(sink_ref, q_ref, kvp_ref, kvc_ref, o_ref):
        n = pl.program_id(0)
        mask_p, mask_c = _attn_masks(n)
        outs = []
        for h in range(N_Q):
            kh = h // (N_Q // N_KV)
            ks, vs = slice(kh * HEAD_DIM, (kh + 1) * HEAD_DIM), slice((N_KV + kh) * HEAD_DIM, (N_KV + kh + 1) * HEAD_DIM)
            qh = q_ref[:, h * HEAD_DIM:(h + 1) * HEAD_DIM]
            pp, pc, _ = _attn_probs(qh, kvp_ref[:, ks], kvc_ref[:, ks], sink_ref[h], mask_p, mask_c)
            outs.append(jnp.dot(pp.astype(BF16), kvp_ref[:, vs], preferred_element_type=F32)
                        + jnp.dot(pc.astype(BF16), kvc_ref[:, vs], preferred_element_type=F32))
        o_ref[...] = jnp.concatenate(outs, axis=1).astype(BF16)

    kvw = 2 * N_KV * HEAD_DIM
    return pl.pallas_call(
        body, grid=(nb,),
        in_specs=[pl.BlockSpec(memory_space=pltpu.SMEM), pl.BlockSpec((BLOCK, D_MODEL), lambda n: (n, 0)),
                  pl.BlockSpec((BLOCK, kvw), lambda n: (jnp.maximum(n - 1, 0), 0)), pl.BlockSpec((BLOCK, kvw), lambda n: (n, 0))],
        out_specs=pl.BlockSpec((BLOCK, D_MODEL), lambda n: (n, 0)), out_shape=_sds((n_rows, D_MODEL), BF16),
        name="attn_fwd", compiler_params=_params(("parallel",)))(sinks, q, kv, kv)


def attn_bwd(q, kv, do, sinks):
    n_rows = q.shape[0]
    nb = n_rows // BLOCK
    kvw = 2 * N_KV * HEAD_DIM
    tn = (((0,), (0,)), ((), ()))
    nt = (((1,), (1,)), ((), ()))
    scale = 1.0 / math.sqrt(HEAD_DIM)

    def body(sink_ref, q_ref, kvp_ref, kvc_ref, do_ref, dq_ref, dbq_ref, dprev_ref, dcur_ref, dsink_ref):
        n = pl.program_id(0)
        mask_p, mask_c = _attn_masks(n)
        lane = lax.broadcasted_iota(jnp.int32, (1, 128), 1)
        dqs, dsink = [], jnp.zeros((1, 128), F32)
        dkp, dkc, dvp, dvc = [], [], [], []
        for h in range(N_Q):
            kh, first = h // (N_Q // N_KV), h % (N_Q // N_KV) == 0
            ks, vs = slice(kh * HEAD_DIM, (kh + 1) * HEAD_DIM), slice((N_KV + kh) * HEAD_DIM, (N_KV + kh + 1) * HEAD_DIM)
            qh = q_ref[:, h * HEAD_DIM:(h + 1) * HEAD_DIM]
            doh = do_ref[:, h * HEAD_DIM:(h + 1) * HEAD_DIM]
            kp, kc, vp, vc = kvp_ref[:, ks], kvc_ref[:, ks], kvp_ref[:, vs], kvc_ref[:, vs]
            pp, pc, ps = _attn_probs(qh, kp, kc, sink_ref[h], mask_p, mask_c)
            dpp = lax.dot_general(doh, vp, nt, preferred_element_type=F32)
            dpc = lax.dot_general(doh, vc, nt, preferred_element_type=F32)
            delta = jnp.sum(pp * dpp, axis=-1, keepdims=True) + jnp.sum(pc * dpc, axis=-1, keepdims=True)
            dsp = (pp * (dpp - delta) * scale).astype(BF16)
            dsc = (pc * (dpc - delta) * scale).astype(BF16)
            dsink = dsink + jnp.where(lane == h, -jnp.sum(ps * delta), 0.0)
            dqs.append(jnp.dot(dsp, kp, preferred_element_type=F32) + jnp.dot(dsc, kc, preferred_element_type=F32))
            terms = [lax.dot_general(dsp, qh, tn, preferred_element_type=F32),
                     lax.dot_general(dsc, qh, tn, preferred_element_type=F32),
                     lax.dot_general(pp.astype(BF16), doh, tn, preferred_element_type=F32),
                     lax.dot_general(pc.astype(BF16), doh, tn, preferred_element_type=F32)]
            for acc, term in zip((dkp, dkc, dvp, dvc), terms):
                if first:
                    acc.append(term)
                else:
                    acc[-1] = acc[-1] + term
        dq = jnp.concatenate(dqs, axis=1)
        dq_ref[...] = dq.astype(BF16)
        dprev_ref[0] = jnp.concatenate(dkp + dvp, axis=1)
        dcur_ref[0] = jnp.concatenate(dkc + dvc, axis=1)

        @pl.when(n == 0)
        def _():
            dbq_ref[...] = jnp.zeros_like(dbq_ref)
            dsink_ref[...] = jnp.zeros_like(dsink_ref)

        dbq_ref[...] += jnp.sum(dq, axis=0, keepdims=True)
        dsink_ref[...] += dsink

    blk = pl.BlockSpec((BLOCK, D_MODEL), lambda n: (n, 0))
    part = pl.BlockSpec((1, BLOCK, kvw), lambda n: (n, 0, 0))
    return pl.pallas_call(
        body, grid=(nb,),
        in_specs=[pl.BlockSpec(memory_space=pltpu.SMEM), blk,
                  pl.BlockSpec((BLOCK, kvw), lambda n: (jnp.maximum(n - 1, 0), 0)), pl.BlockSpec((BLOCK, kvw), lambda n: (n, 0)), blk],
        out_specs=[blk, pl.BlockSpec((1, D_MODEL), lambda n: (0, 0)), part, part, pl.BlockSpec((1, 128), lambda n: (0, 0))],
        out_shape=[_sds((n_rows, D_MODEL), BF16), _sds((1, D_MODEL), F32), _sds((nb, BLOCK, kvw), F32),
                   _sds((nb, BLOCK, kvw), F32), _sds((1, 128), F32)],
        name="attn_bwd", compiler_params=_params(("arbitrary",)))(sinks, q, kv, kv, do)


def kv_combine(dprev, dcur):
    nb, _, kvw = dprev.shape

    def body(dcur_ref, dnext_ref, dkv_ref, db_ref):
        m = pl.program_id(0)
        dkv = dcur_ref[0] + jnp.where(m + 1 < nb, dnext_ref[0], 0.0)
        dkv_ref[...] = dkv.astype(BF16)

        @pl.when(m == 0)
        def _():
            db_ref[...] = jnp.zeros_like(db_ref)

        db_ref[...] += jnp.sum(dkv, axis=0, keepdims=True)

    return pl.pallas_call(
        body, grid=(nb,),
        in_specs=[pl.BlockSpec((1, BLOCK, kvw), lambda m: (m, 0, 0)),
                  pl.BlockSpec((1, BLOCK, kvw), lambda m: (jnp.minimum(m + 1, nb - 1), 0, 0))],
        out_specs=[pl.BlockSpec((BLOCK, kvw), lambda m: (m, 0)), pl.BlockSpec((1, kvw), lambda m: (0, 0))],
        out_shape=[_sds((nb * BLOCK, kvw), BF16), _sds((1, kvw), F32)],
        name="kv_combine", compiler_params=_params(("arbitrary",)))(dcur, dprev)


def glu_bwd(dout, val, gate, tm=256):
    n_rows, d = dout.shape

    def body(do_ref, v_ref, g_ref, dz_ref, db_ref):
        i = pl.program_id(0)
        sg = jax.nn.sigmoid(g_ref[...])
        dval = do_ref[...] * sg
        dgate = do_ref[...] * v_ref[...] * sg * (1.0 - sg)
        dz = jnp.concatenate([dval, dgate], axis=1)
        dz_ref[...] = dz.astype(BF16)

        @pl.when(i == 0)
        def _():
            db_ref[...] = jnp.zeros_like(db_ref)

        db_ref[...] += jnp.sum(dz, axis=0, keepdims=True)

    row = pl.BlockSpec((tm, d), lambda i: (i, 0))
    return pl.pallas_call(
        body, grid=(n_rows // tm,), in_specs=[row, row, row],
        out_specs=[pl.BlockSpec((tm, 2 * d), lambda i: (i, 0)), pl.BlockSpec((1, 2 * d), lambda i: (0, 0))],
        out_shape=[_sds((n_rows, 2 * d), BF16), _sds((1, 2 * d), F32)],
        name="glu_bwd", compiler_params=_params(("arbitrary",)))(dout, val, gate)


def final_loss(h, target, gain, tm=256):
    n_rows, d = h.shape

    def body(h_ref, t_ref, g_ref, loss_ref, dh_ref, dhb_ref, dg_ref):
        i = pl.program_id(0)
        xh, r = _rms_hat(h_ref[...])
        err = xh * g_ref[...] - t_ref[...]
        dy = err * (1.0 / d)
        dxh = dy * g_ref[...]
        dx = r * (dxh - xh * jnp.mean(dxh * xh, axis=-1, keepdims=True))
        dh_ref[...] = dx
        dhb_ref[...] = dx.astype(BF16)

        @pl.when(i == 0)
        def _():
            loss_ref[...] = jnp.zeros_like(loss_ref)
            dg_ref[...] = jnp.zeros_like(dg_ref)

        loss_ref[...] += jnp.full((8, 128), 0.5 * jnp.sum(jnp.mean(err * err, axis=-1, keepdims=True)), F32)
        dg_ref[...] += jnp.sum(dy * xh, axis=0, keepdims=True)

    row = pl.BlockSpec((tm, d), lambda i: (i, 0))
    vec = pl.BlockSpec((1, d), lambda i: (0, 0))
    return pl.pallas_call(
        body, grid=(n_rows // tm,), in_specs=[row, row, vec],
        out_specs=[pl.BlockSpec((8, 128), lambda i: (0, 0)), row, row, vec],
        out_shape=[_sds((8, 128), F32), _sds((n_rows, d), F32), _sds((n_rows, d), BF16), _sds((1, d), F32)],
        name="final_loss", compiler_params=_params(("arbitrary",)))(h, target, gain)


def adamw(name, w, g, m, v, tm=256):
    n_rows, d = w.shape
    tm = tm if n_rows % tm == 0 else n_rows

    def body(w_ref, g_ref, m_ref, v_ref, d_ref, nm_ref, nv_ref):
        gv = g_ref[...]
        nm = ADAM_B1 * m_ref[...] + (1.0 - ADAM_B1) * gv
        nv = ADAM_B2 * v_ref[...] + (1.0 - ADAM_B2) * (gv * gv)
        m_hat = nm / (1.0 - ADAM_B1 ** ADAM_STEP)
        v_hat = nv / (1.0 - ADAM_B2 ** ADAM_STEP)
        d_ref[...] = -ADAM_LR * (m_hat / (jnp.sqrt(v_hat) + ADAM_EPS) + ADAM_WD * w_ref[...])
        nm_ref[...] = nm
        nv_ref[...] = nv

    row = pl.BlockSpec((tm, d), lambda i: (i, 0))
    return pl.pallas_call(
        body, grid=(n_rows // tm,), in_specs=[row] * 4, out_specs=[row] * 3,
        out_shape=[_sds((n_rows, d), F32)] * 3, name=name, compiler_params=_params(("parallel",)))(w, g, m, v)


def _position():
    x, y, c = lax.axis_index("x"), lax.axis_index("y"), lax.axis_index("c")
    others = [(1 - x, y), (x, 1 - y), (1 - x, 1 - y)]
    return x, y, c, others


def _window(ref, kind, chip, half, shard_shape):
    r, n = shard_shape
    if kind == "col":
        return ref.at[pl.ds(pl.multiple_of(half * (r // 2), 16), r // 2), pl.ds(pl.multiple_of(chip * n, 128), n)]
    return ref.at[pl.ds(pl.multiple_of(chip * r, 16), r), pl.ds(pl.multiple_of(half * (n // 2), 128), n // 2)]


def _half(ref, kind, half, shape):
    r, n = shape
    if kind == "col":
        return ref.at[pl.ds(pl.multiple_of(half * (r // 2), 16), r // 2), :]
    return ref.at[:, pl.ds(pl.multiple_of(half * (n // 2), 128), n // 2)]


def gather_weights(items):
    nt = len(items)
    shapes = [tuple(a.shape) for a, _ in items]
    kinds = [k for _, k in items]

    def full_shape(i):
        r, n = shapes[i]
        return (r, 4 * n) if kinds[i] == "col" else (4 * r, n)

    def body(*refs):
        in_refs, out_refs = refs[:nt], refs[nt:2 * nt]
        stage_refs = refs[2 * nt:3 * nt]
        send_sems, recv_sems, local_sems = refs[3 * nt:]
        x, y, c, others = _position()
        me = 2 * x + y
        pending = []
        for t in range(nt):
            r, n = shapes[t]
            stage_refs[t][...] = in_refs[t][...].astype(BF16)
            if kinds[t] == "col":
                own = out_refs[t].at[:, pl.ds(pl.multiple_of(me * n, 128), n)]
            else:
                own = out_refs[t].at[pl.ds(pl.multiple_of(me * r, 16), r), :]
            local = pltpu.make_async_copy(stage_refs[t], own, local_sems.at[t])
            local.start()
            pending.append(local)
        sends = []
        for t in range(nt):
            src = _half(stage_refs[t], kinds[t], c, shapes[t])
            for j, (ox, oy) in enumerate(others):
                cp = pltpu.make_async_remote_copy(
                    src_ref=src, dst_ref=_window(out_refs[t], kinds[t], me, c, shapes[t]),
                    send_sem=send_sems.at[t, j], recv_sem=recv_sems.at[t, j], device_id=(ox, oy, c), device_id_type=MESH)
                cp.start()
                sends.append(cp)
        for t in range(nt):
            for j, (ox, oy) in enumerate(others):
                landed = _window(out_refs[t], kinds[t], 2 * ox + oy, c, shapes[t])
                pltpu.make_async_remote_copy(
                    src_ref=landed, dst_ref=landed, send_sem=send_sems.at[t, j], recv_sem=recv_sems.at[t, j],
                    device_id=(ox, oy, c), device_id_type=MESH).wait_recv()
                fw = pltpu.make_async_remote_copy(
                    src_ref=landed, dst_ref=landed, send_sem=send_sems.at[t, 3 + j], recv_sem=recv_sems.at[t, 3 + j],
                    device_id=(x, y, 1 - c), device_id_type=MESH)
                fw.start()
                sends.append(fw)
        for t in range(nt):
            for j, (ox, oy) in enumerate(others):
                got = _window(out_refs[t], kinds[t], 2 * ox + oy, 1 - c, shapes[t])
                pltpu.make_async_remote_copy(
                    src_ref=got, dst_ref=got, send_sem=send_sems.at[t, 3 + j], recv_sem=recv_sems.at[t, 3 + j],
                    device_id=(x, y, 1 - c), device_id_type=MESH).wait_recv()
        for cp in sends:
            cp.wait_send()
        for local in pending:
            local.wait()

    vm = pl.BlockSpec(memory_space=pltpu.VMEM)
    hbm = pl.BlockSpec(memory_space=pl.ANY)
    return pl.pallas_call(
        body, in_specs=[vm] * nt, out_specs=[hbm] * nt,
        out_shape=[_sds(full_shape(i), BF16) for i in range(nt)],
        scratch_shapes=[pltpu.VMEM(shapes[i], BF16) for i in range(nt)]
        + [pltpu.SemaphoreType.DMA((nt, 6)), pltpu.SemaphoreType.DMA((nt, 6)), pltpu.SemaphoreType.DMA((nt,))],
        name="gather_weights", compiler_params=_params())(*[a for a, _ in items])


def swap_halves(grads, kinds):
    nt = len(grads)
    shapes = [tuple(g.shape) for g in grads]

    def body(*refs):
        in_refs, out_refs = refs[:nt], refs[nt:2 * nt]
        send_sems, recv_sems = refs[2 * nt:]
        x, y, c, _ = _position()
        cps = []
        for t in range(nt):
            cp = pltpu.make_async_remote_copy(
                src_ref=_half(in_refs[t], kinds[t], 1 - c, shapes[t]), dst_ref=_half(out_refs[t], kinds[t], 1 - c, shapes[t]),
                send_sem=send_sems.at[t], recv_sem=recv_sems.at[t], device_id=(x, y, 1 - c), device_id_type=MESH)
            cp.start()
            cps.append(cp)
        for t in range(nt):
            mine = _half(out_refs[t], kinds[t], c, shapes[t])
            pltpu.make_async_remote_copy(
                src_ref=mine, dst_ref=mine, send_sem=send_sems.at[t], recv_sem=recv_sems.at[t],
                device_id=(x, y, 1 - c), device_id_type=MESH).wait_recv()
        for cp in cps:
            cp.wait_send()

    hbm = pl.BlockSpec(memory_space=pl.ANY)
    return pl.pallas_call(
        body, in_specs=[hbm] * nt, out_specs=[hbm] * nt, out_shape=[_sds(s, BF16) for s in shapes],
        scratch_shapes=[pltpu.SemaphoreType.DMA((nt,)), pltpu.SemaphoreType.DMA((nt,))],
        name="swap_halves", compiler_params=_params())(*grads)


def _half_spec(kind, shape, tiles):
    r, n = shape
    if kind == "col":
        tn = n // tiles
        return pl.BlockSpec((r // 2, tn), lambda i, s: (s[0], i))
    tm = r // tiles
    return pl.BlockSpec((tm, n // 2), lambda i, s: (i, s[0]))


def add_halves(name, mine, landed, kind, where, tiles=4):
    shape = tuple(mine.shape)
    r, n = shape
    out_shape = (r // 2, n) if kind == "col" else (r, n // 2)
    out_spec = (pl.BlockSpec((r // 2, n // tiles), lambda i, s: (0, i)) if kind == "col"
                else pl.BlockSpec((r // tiles, n // 2), lambda i, s: (i, 0)))

    def body(s_ref, a_ref, b_ref, o_ref):
        o_ref[...] = (a_ref[...].astype(F32) + b_ref[...].astype(F32)).astype(BF16)

    spec = _half_spec(kind, shape, tiles)
    return pl.pallas_call(
        body, grid_spec=pltpu.PrefetchScalarGridSpec(num_scalar_prefetch=1, grid=(tiles,), in_specs=[spec, spec],
                                                     out_specs=out_spec),
        out_shape=_sds(out_shape, BF16), name=name, compiler_params=_params(("parallel",)))(where, mine, landed)


def exchange_shards(parts, kinds, shard_shapes):
    nt = len(parts)

    def piece_shape(t):
        r, n = shard_shapes[t]
        return (r // 2, n) if kinds[t] == "col" else (r, n // 2)

    def piece(ref, t, chip):
        r, n = shard_shapes[t]
        if kinds[t] == "col":
            return ref.at[:, pl.ds(pl.multiple_of(chip * n, 128), n)]
        return ref.at[pl.ds(pl.multiple_of(chip * r, 16), r), :]

    def body(*refs):
        in_refs, out_refs = refs[:nt], refs[nt:2 * nt]
        send_sems, recv_sems = refs[2 * nt:]
        x, y, c, others = _position()
        cps = []
        for t in range(nt):
            for j, (ox, oy) in enumerate(others):
                cp = pltpu.make_async_remote_copy(
                    src_ref=piece(in_refs[t], t, 2 * ox + oy), dst_ref=out_refs[t].at[j],
                    send_sem=send_sems.at[t, j], recv_sem=recv_sems.at[t, j], device_id=(ox, oy, c), device_id_type=MESH)
                cp.start()
                cps.append(cp)
        for t in range(nt):
            for j, (ox, oy) in enumerate(others):
                pltpu.make_async_remote_copy(
                    src_ref=out_refs[t].at[j], dst_ref=out_refs[t].at[j], send_sem=send_sems.at[t, j],
                    recv_sem=recv_sems.at[t, j], device_id=(ox, oy, c), device_id_type=MESH).wait_recv()
        for cp in cps:
            cp.wait_send()

    hbm = pl.BlockSpec(memory_space=pl.ANY)
    return pl.pallas_call(
        body, in_specs=[hbm] * nt, out_specs=[hbm] * nt,
        out_shape=[_sds((3,) + piece_shape(t), BF16) for t in range(nt)],
        scratch_shapes=[pltpu.SemaphoreType.DMA((nt, 3)), pltpu.SemaphoreType.DMA((nt, 3))],
        name="exchange_shards", compiler_params=_params())(*parts)


def sum_shards(name, part, landed, kind, shard_shape, where, tiles=2):
    r, n = shard_shape
    if kind == "col":
        shape, tm = (r // 2, n), r // 2 // tiles
        own = pl.BlockSpec((tm, n), lambda i, s: (i, s[1]))
    else:
        shape, tm = (r, n // 2), r // tiles
        own = pl.BlockSpec((tm, n // 2), lambda i, s: (s[1] * tiles + i, 0))

    def body(s_ref, a_ref, l_ref, o_ref):
        o_ref[...] = ((a_ref[...].astype(F32) + l_ref[0].astype(F32)) + l_ref[1].astype(F32)) + l_ref[2].astype(F32)

    return pl.pallas_call(
        body, grid_spec=pltpu.PrefetchScalarGridSpec(
            num_scalar_prefetch=1, grid=(tiles,),
            in_specs=[own, pl.BlockSpec((3, tm, shape[1]), lambda i, s: (0, i, 0))],
            out_specs=pl.BlockSpec((tm, shape[1]), lambda i, s: (i, 0))),
        out_shape=_sds(shape, F32), name=name, compiler_params=_params(("parallel",)))(where, part, landed)


def share_halves(halves, kinds, shard_shapes):
    nt = len(halves)

    def body(*refs):
        in_refs, out_refs = refs[:nt], refs[nt:2 * nt]
        send_sems, recv_sems, local_sems = refs[2 * nt:]
        x, y, c, _ = _position()
        cps = []
        for t in range(nt):
            dst = _half(out_refs[t], kinds[t], c, shard_shapes[t])
            local = pltpu.make_async_copy(in_refs[t], dst, local_sems.at[t])
            local.start()
            cp = pltpu.make_async_remote_copy(
                src_ref=in_refs[t], dst_ref=dst, send_sem=send_sems.at[t], recv_sem=recv_sems.at[t],
                device_id=(x, y, 1 - c), device_id_type=MESH)
            cp.start()
            cps.append((local, cp))
        for t in range(nt):
            other = _half(out_refs[t], kinds[t], 1 - c, shard_shapes[t])
            pltpu.make_async_remote_copy(
                src_ref=other, dst_ref=other, send_sem=send_sems.at[t], recv_sem=recv_sems.at[t],
                device_id=(x, y, 1 - c), device_id_type=MESH).wait_recv()
        for local, cp in cps:
            cp.wait_send()
            local.wait()

    hbm = pl.BlockSpec(memory_space=pl.ANY)
    return pl.pallas_call(
        body, in_specs=[hbm] * nt, out_specs=[hbm] * nt, out_shape=[_sds(s, F32) for s in shard_shapes],
        scratch_shapes=[pltpu.SemaphoreType.DMA((nt,)), pltpu.SemaphoreType.DMA((nt,)), pltpu.SemaphoreType.DMA((nt,))],
        name="share_halves", compiler_params=_params())(*halves)


def all_reduce_small(name, buf):
    shape = tuple(buf.shape)

    def body(in_ref, out_ref, land, send_sems, recv_sems):
        x, y, c, _ = _position()
        out_ref[...] = in_ref[...]
        for s, peer in enumerate([(x, y, 1 - c), (1 - x, y, c), (x, 1 - y, c)]):
            cp = pltpu.make_async_remote_copy(
                src_ref=out_ref, dst_ref=land.at[s], send_sem=send_sems.at[s], recv_sem=recv_sems.at[s],
                device_id=peer, device_id_type=MESH)
            cp.start()
            cp.wait()
            out_ref[...] = out_ref[...] + land[s]

    vm = pl.BlockSpec(memory_space=pltpu.VMEM)
    return pl.pallas_call(
        body, in_specs=[vm], out_specs=vm, out_shape=_sds(shape, F32),
        scratch_shapes=[pltpu.VMEM((3,) + shape, F32), pltpu.SemaphoreType.DMA((3,)), pltpu.SemaphoreType.DMA((3,))],
        name=name, compiler_params=_params())(buf)


def _pack(arrays):
    flat = jnp.concatenate([a.reshape(-1).astype(F32) for a in arrays])
    pad = (-flat.shape[0]) % 1024
    return jnp.pad(flat, (0, pad)).reshape(-1, 128)


def _unpack(buf, like):
    flat = buf.reshape(-1)
    out, off = [], 0
    for a in like:
        size = math.prod(a.shape)
        out.append(flat[off:off + size].reshape(a.shape))
        off += size
    return out


def _local_step(x, target, small, full):
    d = D_MODEL
    dff = full["w_in0"].shape[1]
    first = lambda accs, e, r: [accs[0]]

    lam_r, lam_i, bbar_re, bbar_im = small["s5_disc"]
    rb, rc = _s5_matrices(bbar_re, bbar_im, small["s5_c_re"], small["s5_c_im"])
    rb16, rc16 = rb.astype(BF16), rc.astype(BF16)
    lr_t, li_t = lam_r.reshape(S5_BLOCKS, 8, 128), lam_i.reshape(S5_BLOCKS, 8, 128)
    (u,) = rms_fwd("norm_mix0", x, [small["norm_mix0"]], [F32])
    ge, y2, cs = s5_fwd(u, small["s5_d"], rb16, rc16, lr_t, li_t)
    h1, val, gate = mm_nn(
        "glu", ge, full["w_glu"], [0, d], d,
        lambda accs, e, r: [e[0] + (accs[0] + r[0]) * jax.nn.sigmoid(accs[1] + r[1]), accs[0] + r[0], accs[1] + r[1]],
        [F32, F32, F32], extras=[x], rowvecs=[(small["s5_b_glu"], 0), (small["s5_b_glu"], d)])

    def mlp_fwd(tag, h, gain, w_in, w_out):
        (n,) = rms_fwd("norm_mlp" + tag, h, [gain], [BF16])
        a, r = mm_nn("mlp_in" + tag, n, w_in, [0], dff,
                     lambda accs, e, rv: [accs[0], jnp.square(jnp.maximum(accs[0], 0.0))], [F32, BF16])
        (h_out,) = mm_nn("mlp_out" + tag, r, w_out, [0], d, lambda accs, e, rv: [e[0] + accs[0]], [F32], extras=[h])
        return h_out, (n, a, r)

    h2, mlp0 = mlp_fwd("0", h1, small["norm_mlp0"], full["w_in0"], full["w_out0"])

    nkv, n2 = rms_fwd("norm_kv_mix1", h2, [small["norm_kv"], small["norm_mix1"]], [BF16, BF16])
    kvw = 2 * N_KV * HEAD_DIM
    (kv,) = mm_nn("kv_proj", nkv, full["w_kv"], [0], kvw, lambda accs, e, r: [accs[0] + r[0]], [BF16],
                  rowvecs=[(small["b_kv"], 0)])
    (q,) = mm_nn("q_proj", n2, full["w_q"], [0], d, lambda accs, e, r: [accs[0] + r[0]], [BF16],
                 rowvecs=[(small["b_q"], 0)])
    sinks = small["sinks"].reshape(N_Q)
    o = attn_fwd(q, kv, sinks)
    (h3,) = mm_nn("o_proj", o, full["w_o"], [0], d, lambda accs, e, r: [e[0] + accs[0] + r[0]], [F32],
                  extras=[h2], rowvecs=[(small["b_o"], 0)])
    h4, mlp1 = mlp_fwd("1", h3, small["norm_mlp1"], full["w_in1"], full["w_out1"])
    loss_tile, dh, dhb, dg_final = final_loss(h4, target, small["norm_final"])

    grads_small, grads_full = {"norm_final": dg_final}, {}
    ident = lambda acc, e: [acc]

    def mlp_bwd(tag, dh, dhb, h_in, gain, w_in, w_out, saved):
        n, a, r = saved
        grads_full["w_out" + tag] = mm_tn("dw_out" + tag, r, dhb)
        (da,) = mm_nt("mlp_da" + tag, dhb, w_out, lambda acc, e: [acc * 2.0 * jnp.maximum(e[0], 0.0)], [BF16], extras=[a])
        grads_full["w_in" + tag] = mm_tn("dw_in" + tag, n, da)
        (dn,) = mm_nt("mlp_dn" + tag, da, w_in, ident, [F32])
        dx, dxb, colsum, dg = rms_bwd("norm_mlp_bwd" + tag, h_in, [dn], [gain], dh)
        grads_small["norm_mlp" + tag] = dg
        return dx, dxb, colsum

    dh3, dh3b, colsum3 = mlp_bwd("1", dh, dhb, h3, small["norm_mlp1"], full["w_in1"], full["w_out1"], mlp1)
    grads_small["b_o"] = colsum3
    grads_full["w_o"] = mm_tn("dw_o", o, dh3b)
    (do,) = mm_nt("attn_do", dh3b, full["w_o"], ident, [BF16])
    dq, dbq, dprev, dcur, dsink = attn_bwd(q, kv, do, sinks)
    dkv, dbkv = kv_combine(dprev, dcur)
    grads_small["b_q"], grads_small["b_kv"], grads_small["sinks"] = dbq, dbkv, dsink[:, :N_Q]
    grads_full["w_q"] = mm_tn("dw_q", n2, dq)
    grads_full["w_kv"] = mm_tn("dw_kv", nkv, dkv)
    (dn2,) = mm_nt("attn_dn", dq, full["w_q"], ident, [F32])
    (dnkv,) = mm_nt("kv_dn", dkv, full["w_kv"], ident, [F32])
    dh2, dh2b, _, dg_mix1, dg_kv = rms_bwd("norm_kv_mix1_bwd", h2, [dn2, dnkv], [small["norm_mix1"], small["norm_kv"]], dh3)
    grads_small["norm_mix1"], grads_small["norm_kv"] = dg_mix1, dg_kv
    dh1, _, _ = mlp_bwd("0", dh2, dh2b, h1, small["norm_mlp0"], full["w_in0"], full["w_out0"], mlp0)

    dz, db_glu = glu_bwd(dh1, val, gate)
    grads_small["s5_b_glu"] = db_glu
    grads_full["w_glu"] = mm_tn("dw_glu", ge, dz)
    (dy2,) = mm_nt("glu_dy", dz, full["w_glu"], lambda acc, e: [acc * _gelu_grad(e[0])], [F32], extras=[y2])
    rbt16, rct16 = jnp.swapaxes(rb16, 1, 2), jnp.swapaxes(rc16, 1, 2)
    du, dd, drb, drc, dlr, dli = s5_bwd(u, dy2, small["s5_d"], cs, rb16, rbt16, rct16, lr_t, li_t)
    grads_small["s5_d"] = dd
    grads_small["s5_mats"] = (drb, drc, dlr, dli)
    grad_x, _, _, dg_mix0 = rms_bwd("norm_mix0_bwd", x, [du], [small["norm_mix0"]], dh1)
    grads_small["norm_mix0"] = dg_mix0
    return loss_tile, grad_x, grads_small, grads_full


FULL_KINDS = {"w_glu": "col", "w_kv": "row", "w_q": "row", "w_o": "row", "w_in0": "col", "w_in1": "col",
              "w_out0": "row", "w_out1": "row"}
SMALL_NAMES = ["norm_mix", "norm_mlp", "norm_kv", "norm_final", "s5_a_re", "s5_a_im", "s5_log_dt", "s5_b_re", "s5_b_im",
               "s5_c_re", "s5_c_im", "s5_d", "s5_b_glu", "b_kv", "b_q", "sinks", "b_o"]
BIG_NAMES = ["s5_w_glu", "w_kv", "w_q", "w_o", "w_mlp_in", "w_mlp_out"]
WEIGHT_ORDER = ["norm_mix", "norm_mlp", "norm_kv", "norm_final", "s5_a_re", "s5_a_im", "s5_log_dt", "s5_b_re", "s5_b_im",
                "s5_c_re", "s5_c_im", "s5_d", "s5_w_glu", "s5_b_glu", "w_kv", "b_kv", "w_q", "b_q", "sinks", "w_o", "b_o",
                "w_mlp_in", "w_mlp_out"]


def kernel(x, norm_mix, norm_mlp, norm_kv, norm_final, s5_a_re, s5_a_im, s5_log_dt, s5_b_re, s5_b_im, s5_c_re, s5_c_im, s5_d, s5_w_glu, s5_b_glu, w_kv, b_kv, w_q, b_q, sinks, w_o, b_o, w_mlp_in, w_mlp_out, loss_target, m_norm_mix, m_norm_mlp, m_norm_kv, m_norm_final, m_s5_a_re, m_s5_a_im, m_s5_log_dt, m_s5_b_re, m_s5_b_im, m_s5_c_re, m_s5_c_im, m_s5_d, m_s5_w_glu, m_s5_b_glu, m_w_kv, m_b_kv, m_w_q, m_b_q, m_sinks, m_w_o, m_b_o, m_w_mlp_in, m_w_mlp_out, v_norm_mix, v_norm_mlp, v_norm_kv, v_norm_final, v_s5_a_re, v_s5_a_im, v_s5_log_dt, v_s5_b_re, v_s5_b_im, v_s5_c_re, v_s5_c_im, v_s5_d, v_s5_w_glu, v_s5_b_glu, v_w_kv, v_b_kv, v_w_q, v_b_q, v_sinks, v_w_o, v_b_o, v_w_mlp_in, v_w_mlp_out):
    env = dict(locals())
    w = {n: env[n] for n in WEIGHT_ORDER}
    mom = {n: env["m_" + n] for n in WEIGHT_ORDER}
    var = {n: env["v_" + n] for n in WEIGHT_ORDER}
    d = D_MODEL
    xi, yi, ci = lax.axis_index("x"), lax.axis_index("y"), lax.axis_index("c")
    chip = 2 * xi + yi
    where = jnp.stack([ci, chip]).astype(jnp.int32)

    dsh, bsh = s5_d.shape[1], s5_b_glu.shape[1]
    placed = jnp.concatenate([
        lax.dynamic_update_slice(jnp.zeros((4 * dsh,), F32), s5_d[0], (chip * dsh,)),
        lax.dynamic_update_slice(jnp.zeros((4 * bsh,), F32), s5_b_glu[0], (chip * bsh,))])
    placed = jnp.where(ci == 0, placed, 0.0).reshape(-1, 128)
    gathered = all_reduce_small("gather_vectors", placed).reshape(-1)
    d_full, bglu_full = gathered[:4 * dsh].reshape(1, -1), gathered[4 * dsh:].reshape(1, -1)

    items = [(s5_w_glu[0], "col"), (w_kv, "row"), (w_q[0], "row"), (w_o[0], "row"),
             (w_mlp_in[0], "col"), (w_mlp_in[1], "col"), (w_mlp_out[0], "row"), (w_mlp_out[1], "row")]
    names = ["w_glu", "w_kv", "w_q", "w_o", "w_in0", "w_in1", "w_out0", "w_out1"]
    full = dict(zip(names, gather_weights(items)))

    disc = lambda *p: _s5_discretise(p[0], p[1], p[2], p[3], p[4])
    disc_args = (s5_a_re[0], s5_a_im[0], s5_log_dt[0], s5_b_re[0], s5_b_im[0])
    disc_out, disc_vjp = jax.vjp(disc, *disc_args)
    small = {
        "norm_mix0": norm_mix[0:1], "norm_mix1": norm_mix[1:2], "norm_mlp0": norm_mlp[0:1], "norm_mlp1": norm_mlp[1:2],
        "norm_kv": norm_kv.reshape(1, d), "norm_final": norm_final.reshape(1, d), "s5_disc": disc_out,
        "s5_c_re": s5_c_re[0], "s5_c_im": s5_c_im[0], "s5_d": d_full, "s5_b_glu": bglu_full,
        "b_kv": b_kv.reshape(1, -1), "b_q": b_q, "sinks": sinks, "b_o": b_o,
    }
    loss_tile, grad_x, gs, gf = _local_step(x[0], loss_target[0], small, full)

    drb, drc, dlr, dli = gs["s5_mats"]
    dbbar_re, dbbar_im, dc_re, dc_im = _s5_matrix_grads(drb, drc)
    pieces = [loss_tile[0:1, 0:1], gs["norm_mix0"], gs["norm_mix1"], gs["norm_mlp0"], gs["norm_mlp1"], gs["norm_kv"],
              gs["norm_final"], dlr, dli, dbbar_re, dbbar_im, dc_re, dc_im, gs["s5_d"], gs["s5_b_glu"], gs["b_kv"],
              gs["b_q"], gs["sinks"], gs["b_o"]]
    summed = _unpack(all_reduce_small("reduce_small", _pack(pieces)), pieces)
    (loss, g_mix0, g_mix1, g_mlp0, g_mlp1, g_kv, g_final, dlr, dli, dbbar_re, dbbar_im, dc_re, dc_im, g_d, g_bglu,
     g_bkv, g_bq, g_sinks, g_bo) = summed
    g_are, g_aim, g_dt, g_bre, g_bim = disc_vjp((dlr.reshape(S5_GROUPS, S5_STATE), dli.reshape(S5_GROUPS, S5_STATE),
                                                  dbbar_re, dbbar_im))
    grads = {
        "norm_mix": jnp.concatenate([g_mix0, g_mix1]), "norm_mlp": jnp.concatenate([g_mlp0, g_mlp1]),
        "norm_kv": g_kv.reshape(d), "norm_final": g_final.reshape(d), "s5_a_re": g_are[None], "s5_a_im": g_aim[None],
        "s5_log_dt": g_dt[None], "s5_b_re": g_bre[None], "s5_b_im": g_bim[None], "s5_c_re": dc_re[None],
        "s5_c_im": dc_im[None], "s5_d": lax.dynamic_slice(g_d, (0, chip * dsh), (1, dsh)),
        "s5_b_glu": lax.dynamic_slice(g_bglu, (0, chip * bsh), (1, bsh)), "b_kv": g_bkv.reshape(-1), "b_q": g_bq,
        "sinks": g_sinks, "b_o": g_bo,
    }

    kinds = [FULL_KINDS[n] for n in names]
    shard_shapes = [tuple(a.shape) for a, _ in items]
    partial = [gf[n] for n in names]
    landed = swap_halves(partial, kinds)
    chip_sums = [add_halves("add_halves_" + n, partial[t], landed[t], kinds[t], where) for t, n in enumerate(names)]
    arrived = exchange_shards(chip_sums, kinds, shard_shapes)
    halves = [sum_shards("sum_shards_" + n, chip_sums[t], arrived[t], kinds[t], shard_shapes[t], where)
              for t, n in enumerate(names)]
    reduced = dict(zip(names, share_halves(halves, kinds, shard_shapes)))
    grads["s5_w_glu"] = reduced["w_glu"][None]
    grads["w_kv"] = reduced["w_kv"]
    grads["w_q"] = reduced["w_q"][None]
    grads["w_o"] = reduced["w_o"][None]
    grads["w_mlp_in"] = jnp.stack([reduced["w_in0"], reduced["w_in1"]])
    grads["w_mlp_out"] = jnp.stack([reduced["w_out0"], reduced["w_out1"]])

    delta, new_m, new_v = {}, {}, {}
    for n in BIG_NAMES:
        flat = lambda a: a.reshape(-1, a.shape[-1])
        dl, nm, nv = adamw("adamw_" + n, flat(w[n]), flat(grads[n]), flat(mom[n]), flat(var[n]))
        delta[n], new_m[n], new_v[n] = dl.reshape(w[n].shape), nm.reshape(w[n].shape), nv.reshape(w[n].shape)
    sw, sg, sm, sv = ([t[n] for n in SMALL_NAMES] for t in (w, grads, mom, var))
    dl, nm, nv = adamw("adamw_small", _pack(sw), _pack(sg), _pack(sm), _pack(sv))
    for n, a, b, c_ in zip(SMALL_NAMES, _unpack(dl, sw), _unpack(nm, sw), _unpack(nv, sw)):
        delta[n], new_m[n], new_v[n] = a, b, c_

    out = [loss.reshape(()), grad_x[None]]
    for table in (grads, delta, new_m, new_v):
        out += [table[n].reshape(w[n].shape) for n in WEIGHT_ORDER]
    return tuple(out)
```

```python
import functools
import math

import jax
import jax.numpy as jnp
from jax import lax
from jax.experimental import pallas as pl
from jax.experimental.pallas import tpu as pltpu

F32 = jnp.float32
BF16 = jnp.bfloat16

D_MODEL = 1024
S5_GROUPS = 64
S5_GROUP = 16
S5_STATE = 64
N_KV = 4
N_Q = 16
HEAD_DIM = 64
BLOCK = 128
NORM_EPS = 1e-5
LAMBDA_RE_MAX = -1e-4
ADAM_LR, ADAM_B1, ADAM_B2, ADAM_EPS, ADAM_WD, ADAM_STEP = 0.001, 0.9, 0.999, 1e-08, 0.01, 10

VMEM_LIMIT_BYTES = 56 * 1024 * 1024
S5_CHUNK = 256
S5_BLOCKS = 4
MESH = pl.DeviceIdType.MESH


def _params(sem=None):
    return pltpu.CompilerParams(dimension_semantics=sem, vmem_limit_bytes=VMEM_LIMIT_BYTES)


def _sds(shape, dtype):
    return jax.ShapeDtypeStruct(shape, dtype)


def _rms_hat(xv):
    r = lax.rsqrt(jnp.mean(xv * xv, axis=-1, keepdims=True) + NORM_EPS)
    return xv * r, r


def rms_fwd(name, x, gains, out_dtypes, tm=256):
    n_rows, d = x.shape
    ng = len(gains)

    def body(x_ref, *refs):
        xh, _ = _rms_hat(x_ref[...])
        for g_ref, o_ref in zip(refs[:ng], refs[ng:]):
            o_ref[...] = (xh * g_ref[...]).astype(o_ref.dtype)

    row = pl.BlockSpec((tm, d), lambda i: (i, 0))
    vec = pl.BlockSpec((1, d), lambda i: (0, 0))
    return pl.pallas_call(
        body, grid=(n_rows // tm,), in_specs=[row] + [vec] * ng, out_specs=[row] * ng,
        out_shape=[_sds((n_rows, d), dt) for dt in out_dtypes], name=name,
        compiler_params=_params(("parallel",)))(x, *gains)


def rms_bwd(name, x, dys, gains, res, tm=256):
    n_rows, d = x.shape
    ng = len(gains)

    def body(x_ref, res_ref, *refs):
        dy_refs, g_refs = refs[:ng], refs[ng:2 * ng]
        dx_ref, dxb_ref, cs_ref = refs[2 * ng:2 * ng + 3]
        dg_refs = refs[2 * ng + 3:]
        i = pl.program_id(0)
        xh, r = _rms_hat(x_ref[...])
        dxh = jnp.zeros_like(xh)
        dgs = []
        for dy_ref, g_ref in zip(dy_refs, g_refs):
            dy = dy_ref[...].astype(F32)
            dxh = dxh + dy * g_ref[...]
            dgs.append(jnp.sum(dy * xh, axis=0, keepdims=True))
        dx = r * (dxh - xh * jnp.mean(dxh * xh, axis=-1, keepdims=True)) + res_ref[...]
        dx_ref[...] = dx
        dxb_ref[...] = dx.astype(BF16)
        cs = jnp.sum(dx, axis=0, keepdims=True)

        @pl.when(i == 0)
        def _():
            cs_ref[...] = jnp.zeros_like(cs_ref)
            for dg_ref in dg_refs:
                dg_ref[...] = jnp.zeros_like(dg_ref)

        cs_ref[...] += cs
        for dg_ref, dg in zip(dg_refs, dgs):
            dg_ref[...] += dg

    row = pl.BlockSpec((tm, d), lambda i: (i, 0))
    vec = pl.BlockSpec((1, d), lambda i: (0, 0))
    return pl.pallas_call(
        body, grid=(n_rows // tm,), in_specs=[row, row] + [row] * ng + [vec] * ng,
        out_specs=[row, row, vec] + [vec] * ng,
        out_shape=[_sds((n_rows, d), F32), _sds((n_rows, d), BF16), _sds((1, d), F32)] + [_sds((1, d), F32)] * ng,
        name=name, compiler_params=_params(("arbitrary",)))(x, res, *dys, *gains)


def mm_nn(name, a, w, col_offsets, n_out, epilogue, out_dtypes, extras=(), rowvecs=(), tm=1024, tn=512):
    m, k = a.shape
    tm, tn = min(tm, m), min(tn, n_out)
    nw, ne, nr = len(col_offsets), len(extras), len(rowvecs)

    def body(a_ref, *refs):
        w_refs, e_refs, r_refs = refs[:nw], refs[nw:nw + ne], refs[nw + ne:nw + ne + nr]
        o_refs = refs[nw + ne + nr:]
        av = a_ref[...]
        accs = [jnp.dot(av, w_ref[...], preferred_element_type=F32) for w_ref in w_refs]
        outs = epilogue(accs, [e[...] for e in e_refs], [r[...] for r in r_refs])
        for o_ref, o in zip(o_refs, outs):
            o_ref[...] = o.astype(o_ref.dtype)

    def wspec(off):
        return pl.BlockSpec((k, tn), lambda j, i, off=off: (0, off // tn + j))

    def rspec(off):
        return pl.BlockSpec((1, tn), lambda j, i, off=off: (0, off // tn + j))

    tile = pl.BlockSpec((tm, tn), lambda j, i: (i, j))
    in_specs = ([pl.BlockSpec((tm, k), lambda j, i: (i, 0))] + [wspec(o) for o in col_offsets]
                + [tile] * ne + [rspec(o) for _, o in rowvecs])
    return pl.pallas_call(
        body, grid=(n_out // tn, m // tm), in_specs=in_specs, out_specs=[tile] * len(out_dtypes),
        out_shape=[_sds((m, n_out), dt) for dt in out_dtypes], name=name,
        compiler_params=_params(("parallel", "parallel")))(a, *([w] * nw), *extras, *[r for r, _ in rowvecs])


def mm_nt(name, g, w, epilogue, out_dtypes, extras=(), tm=512, tk=512):
    m, n = g.shape
    k = w.shape[0]
    tm, tk = min(tm, m), min(tk, k)
    ne = len(extras)

    def body(g_ref, w_ref, *refs):
        e_refs, o_refs = refs[:ne], refs[ne:]
        acc = lax.dot_general(g_ref[...], w_ref[...], (((1,), (1,)), ((), ())), preferred_element_type=F32)
        outs = epilogue(acc, [e[...] for e in e_refs])
        for o_ref, o in zip(o_refs, outs):
            o_ref[...] = o.astype(o_ref.dtype)

    tile = pl.BlockSpec((tm, tk), lambda i, j: (i, j))
    return pl.pallas_call(
        body, grid=(m // tm, k // tk),
        in_specs=[pl.BlockSpec((tm, n), lambda i, j: (i, 0)), pl.BlockSpec((tk, n), lambda i, j: (j, 0))] + [tile] * ne,
        out_specs=[tile] * len(out_dtypes), out_shape=[_sds((m, k), dt) for dt in out_dtypes], name=name,
        compiler_params=_params(("parallel", "parallel")))(g, w, *extras)


def mm_tn(name, a, g, tk=512, tn=512):
    m, k = a.shape
    n = g.shape[1]
    tk, tn = min(tk, k), min(tn, n)

    def body(a_ref, g_ref, o_ref):
        acc = lax.dot_general(a_ref[...], g_ref[...], (((0,), (0,)), ((), ())), preferred_element_type=F32)
        o_ref[...] = acc.astype(o_ref.dtype)

    return pl.pallas_call(
        body, grid=(k // tk, n // tn),
        in_specs=[pl.BlockSpec((m, tk), lambda i, j: (0, i)), pl.BlockSpec((m, tn), lambda i, j: (0, j))],
        out_specs=pl.BlockSpec((tk, tn), lambda i, j: (i, j)), out_shape=_sds((k, n), BF16), name=name,
        compiler_params=_params(("parallel", "parallel")))(a, g)


def _masked_rows(dst_ref, val, tc):
    for half in range(2):
        v = val[:, half * 128:(half + 1) * 128]
        col = lax.broadcasted_iota(jnp.int32, v.shape, 1) // 32 + 4 * half
        for s8 in range(8):
            rows = jnp.where(col == s8, v, 0.0) if s8 // 4 == half else jnp.zeros_like(v)
            dst_ref.at[half][pl.ds(s8, tc, stride=8), :] = rows


def _staged(ref):
    return jnp.concatenate([ref[0], ref[1]], axis=1)


def _stage(ref, val):
    ref[0] = val[:, 0:128]
    ref[1] = val[:, 128:256]


def _gather_rows(src_ref, tc):
    halves = []
    for half in range(2):
        col = lax.broadcasted_iota(jnp.int32, (tc, 128), 1) // 32 + 4 * half
        out = jnp.zeros((tc, 128), F32)
        for s8 in range(4 * half, 4 * half + 4):
            out = jnp.where(col == s8, src_ref.at[half][pl.ds(s8, tc, stride=8), :], out)
        halves.append(out)
    return jnp.concatenate(halves, axis=1)


def _gelu(x):
    c = math.sqrt(2.0 / math.pi)
    return 0.5 * x * (1.0 + jnp.tanh(c * (x + 0.044715 * x * x * x)))


def _gelu_grad(x):
    c = math.sqrt(2.0 / math.pi)
    t = jnp.tanh(c * (x + 0.044715 * x * x * x))
    return 0.5 * (1.0 + t) + 0.5 * x * (1.0 - t * t) * c * (1.0 + 3.0 * 0.044715 * x * x)


def s5_fwd(u, d_skip, rb, rc, lam_r, lam_i):
    n_rows = u.shape[0]
    tc = min(S5_CHUNK, n_rows)
    nc = n_rows // tc

    def body(u_ref, d_ref, rb_ref, rc_ref, lr_ref, li_ref, ge_ref, y2_ref, cs_ref, lhs, bux, yrows, carry):
        i = pl.program_id(0)

        @pl.when(i == 0)
        def _():
            carry[...] = jnp.zeros_like(carry)

        cs_ref[0] = carry[...]
        for blk in range(S5_BLOCKS):
            _masked_rows(lhs, u_ref[:, blk * 256:(blk + 1) * 256], tc)
            bux[blk] = jnp.dot(_staged(lhs).astype(BF16), rb_ref[blk], preferred_element_type=F32)
        lam = [(lr_ref[blk], li_ref[blk]) for blk in range(S5_BLOCKS)]

        def step(t, c):
            r0 = pl.multiple_of(t * 8, 8)
            new = []
            for blk in range(S5_BLOCKS):
                xr, xi = c[2 * blk], c[2 * blk + 1]
                lr, li = lam[blk]
                nr = lr * xr - li * xi + bux[blk, pl.ds(r0, 8), 0:128]
                ni = lr * xi + li * xr + bux[blk, pl.ds(r0, 8), 128:256]
                bux[blk, pl.ds(r0, 8), 0:128] = nr
                bux[blk, pl.ds(r0, 8), 128:256] = ni
                new += [nr, ni]
            return tuple(new)

        c0 = []
        for blk in range(S5_BLOCKS):
            c0 += [carry[blk, :, 0:128], carry[blk, :, 128:256]]
        cn = lax.fori_loop(0, tc, step, tuple(c0))
        for blk in range(S5_BLOCKS):
            carry[blk, :, 0:128] = cn[2 * blk]
            carry[blk, :, 128:256] = cn[2 * blk + 1]
        for blk in range(S5_BLOCKS):
            _stage(yrows, jnp.dot(bux[blk].astype(BF16), rc_ref[blk], preferred_element_type=F32))
            sl = slice(blk * 256, (blk + 1) * 256)
            y2 = _gather_rows(yrows, tc) + d_ref[:, sl] * u_ref[:, sl]
            y2_ref[:, sl] = y2
            ge_ref[:, sl] = _gelu(y2).astype(BF16)

    row = pl.BlockSpec((tc, D_MODEL), lambda i: (i, 0))
    mat = pl.BlockSpec((S5_BLOCKS, 256, 256), lambda i: (0, 0, 0))
    lamspec = pl.BlockSpec((S5_BLOCKS, 8, 128), lambda i: (0, 0, 0))
    return pl.pallas_call(
        body, grid=(nc,),
        in_specs=[row, pl.BlockSpec((1, D_MODEL), lambda i: (0, 0)), mat, mat, lamspec, lamspec],
        out_specs=[row, row, pl.BlockSpec((1, S5_BLOCKS, 8, 256), lambda i: (i, 0, 0, 0))],
        out_shape=[_sds((n_rows, D_MODEL), BF16), _sds((n_rows, D_MODEL), F32), _sds((nc, S5_BLOCKS, 8, 256), F32)],
        scratch_shapes=[pltpu.VMEM((2, 8 * tc, 128), F32), pltpu.VMEM((S5_BLOCKS, 8 * tc, 256), F32),
                        pltpu.VMEM((2, 8 * tc, 128), F32), pltpu.VMEM((S5_BLOCKS, 8, 256), F32)],
        name="s5_fwd", compiler_params=_params(("arbitrary",)))(u, d_skip, rb, rc, lam_r, lam_i)


def s5_bwd(u, dy2, d_skip, cs, rb, rbt, rct, lam_r, lam_i):
    n_rows = u.shape[0]
    tc = min(S5_CHUNK, n_rows)
    nc = n_rows // tc

    def body(u_ref, dy_ref, d_ref, cs_ref, rb_ref, rbt_ref, rct_ref, lr_ref, li_ref,
             du_ref, dd_ref, drb_ref, drc_ref, dlr_ref, dli_ref, tmp, lhsu, lhsd, xs, adj, acarry):
        i = pl.program_id(0)

        @pl.when(i == 0)
        def _():
            acarry[...] = jnp.zeros_like(acarry)
            dd_ref[...] = jnp.zeros_like(dd_ref)
            drb_ref[...] = jnp.zeros_like(drb_ref)
            drc_ref[...] = jnp.zeros_like(drc_ref)
            dlr_ref[...] = jnp.zeros_like(dlr_ref)
            dli_ref[...] = jnp.zeros_like(dli_ref)

        dd_ref[...] += jnp.sum(dy_ref[...] * u_ref[...], axis=0, keepdims=True)
        for blk in range(S5_BLOCKS):
            sl = slice(blk * 256, (blk + 1) * 256)
            _masked_rows(tmp, u_ref[:, sl], tc)
            lhsu[blk] = _staged(tmp).astype(BF16)
            xs[blk] = jnp.dot(lhsu[blk], rb_ref[blk], preferred_element_type=F32)
            _masked_rows(tmp, dy_ref[:, sl], tc)
            lhsd[blk] = _staged(tmp).astype(BF16)
            adj[blk] = jnp.dot(lhsd[blk], rct_ref[blk], preferred_element_type=F32)
        lam = [(lr_ref[blk], li_ref[blk]) for blk in range(S5_BLOCKS)]

        def fstep(t, c):
            r0 = pl.multiple_of(t * 8, 8)
            new = []
            for blk in range(S5_BLOCKS):
                xr, xi = c[2 * blk], c[2 * blk + 1]
                lr, li = lam[blk]
                nr = lr * xr - li * xi + xs[blk, pl.ds(r0, 8), 0:128]
                ni = lr * xi + li * xr + xs[blk, pl.ds(r0, 8), 128:256]
                xs[blk, pl.ds(r0, 8), 0:128] = nr
                xs[blk, pl.ds(r0, 8), 128:256] = ni
                new += [nr, ni]
            return tuple(new)

        c0 = []
        for blk in range(S5_BLOCKS):
            c0 += [cs_ref[0, blk, :, 0:128], cs_ref[0, blk, :, 128:256]]
        lax.fori_loop(0, tc, fstep, tuple(c0))

        def bstep(k, c):
            t = tc - 1 - k
            r0 = pl.multiple_of(t * 8, 8)
            rp = pl.multiple_of(jnp.maximum(t - 1, 0) * 8, 8)
            first = t == 0
            new_a, new_g = [], []
            for blk in range(S5_BLOCKS):
                ar, ai = c[0][2 * blk], c[0][2 * blk + 1]
                glr, gli = c[1][2 * blk], c[1][2 * blk + 1]
                lr, li = lam[blk]
                nr = lr * ar + li * ai + adj[blk, pl.ds(r0, 8), 0:128]
                ni = lr * ai - li * ar + adj[blk, pl.ds(r0, 8), 128:256]
                adj[blk, pl.ds(r0, 8), 0:128] = nr
                adj[blk, pl.ds(r0, 8), 128:256] = ni
                pr = jnp.where(first, cs_ref[0, blk, :, 0:128], xs[blk, pl.ds(rp, 8), 0:128])
                pi = jnp.where(first, cs_ref[0, blk, :, 128:256], xs[blk, pl.ds(rp, 8), 128:256])
                new_a += [nr, ni]
                new_g += [glr + nr * pr + ni * pi, gli + ni * pr - nr * pi]
            return tuple(new_a), tuple(new_g)

        a0, g0 = [], []
        for blk in range(S5_BLOCKS):
            a0 += [acarry[blk, :, 0:128], acarry[blk, :, 128:256]]
            g0 += [dlr_ref[blk], dli_ref[blk]]
        an, gn = lax.fori_loop(0, tc, bstep, (tuple(a0), tuple(g0)))
        for blk in range(S5_BLOCKS):
            acarry[blk, :, 0:128] = an[2 * blk]
            acarry[blk, :, 128:256] = an[2 * blk + 1]
            dlr_ref[blk] = gn[2 * blk]
            dli_ref[blk] = gn[2 * blk + 1]
        for blk in range(S5_BLOCKS):
            sl = slice(blk * 256, (blk + 1) * 256)
            ab = adj[blk].astype(BF16)
            _stage(tmp, jnp.dot(ab, rbt_ref[blk], preferred_element_type=F32))
            du_ref[:, sl] = _gather_rows(tmp, tc) + d_ref[:, sl] * dy_ref[:, sl]
            drb_ref[blk] += lax.dot_general(lhsu[blk], ab, (((0,), (0,)), ((), ())), preferred_element_type=F32)
            drc_ref[blk] += lax.dot_general(xs[blk].astype(BF16), lhsd[blk], (((0,), (0,)), ((), ())),
                                            preferred_element_type=F32)

    rev = pl.BlockSpec((tc, D_MODEL), lambda i: (nc - 1 - i, 0))
    vec = pl.BlockSpec((1, D_MODEL), lambda i: (0, 0))
    mat = pl.BlockSpec((S5_BLOCKS, 256, 256), lambda i: (0, 0, 0))
    lamspec = pl.BlockSpec((S5_BLOCKS, 8, 128), lambda i: (0, 0, 0))
    big = pltpu.VMEM((S5_BLOCKS, 8 * tc, 256), F32)
    bigb = pltpu.VMEM((S5_BLOCKS, 8 * tc, 256), BF16)
    return pl.pallas_call(
        body, grid=(nc,),
        in_specs=[rev, rev, vec, pl.BlockSpec((1, S5_BLOCKS, 8, 256), lambda i: (nc - 1 - i, 0, 0, 0)),
                  mat, mat, mat, lamspec, lamspec],
        out_specs=[rev, vec, mat, mat, lamspec, lamspec],
        out_shape=[_sds((n_rows, D_MODEL), F32), _sds((1, D_MODEL), F32), _sds((S5_BLOCKS, 256, 256), F32),
                   _sds((S5_BLOCKS, 256, 256), F32), _sds((S5_BLOCKS, 8, 128), F32), _sds((S5_BLOCKS, 8, 128), F32)],
        scratch_shapes=[pltpu.VMEM((2, 8 * tc, 128), F32), bigb, bigb, big, big, pltpu.VMEM((S5_BLOCKS, 8, 256), F32)],
        name="s5_bwd", compiler_params=_params(("arbitrary",)))(u, dy2, d_skip, cs, rb, rbt, rct, lam_r, lam_i)


def _s5_discretise(a_re, a_im, log_dt, b_re, b_im):
    lam = lax.complex(jnp.minimum(a_re, LAMBDA_RE_MAX), a_im)
    dt = jnp.exp(log_dt)[:, None]
    lam_bar = jnp.exp(lam * dt)
    b_bar = ((lam_bar - 1.0) / lam)[:, :, None] * lax.complex(b_re, b_im)
    return jnp.real(lam_bar), jnp.imag(lam_bar), jnp.real(b_bar), jnp.imag(b_bar)


def _s5_matrices(bbar_re, bbar_im, c_re, c_im):
    eye2 = jnp.eye(2, dtype=F32)
    bst = jnp.stack([bbar_re, bbar_im]).reshape(2, S5_BLOCKS, 8, 2, S5_STATE, S5_GROUP)
    bt = jnp.transpose(bst, (1, 2, 3, 5, 0, 4))
    rb = (bt[:, :, :, :, :, None, :] * eye2[None, None, :, None, None, :, None]).reshape(S5_BLOCKS, 256, 256)
    cst = jnp.stack([c_re, -c_im]).reshape(2, S5_BLOCKS, 8, 2, S5_GROUP, S5_STATE)
    ct = jnp.transpose(cst, (1, 0, 5, 2, 3, 4))
    rc = (ct[:, :, None, :, :, :, :] * eye2[None, None, :, None, None, :, None]).reshape(S5_BLOCKS, 256, 256)
    return rb, rc


def _s5_matrix_grads(drb, drc):
    x = drb.reshape(S5_BLOCKS, 8, 2, S5_GROUP, 2, 2, S5_STATE)
    db = jnp.stack([x[:, :, 0, :, :, 0, :], x[:, :, 1, :, :, 1, :]], axis=2)
    db = jnp.transpose(db, (4, 0, 1, 2, 5, 3)).reshape(2, S5_GROUPS, S5_STATE, S5_GROUP)
    y = drc.reshape(S5_BLOCKS, 2, 2, S5_STATE, 8, 2, S5_GROUP)
    dc = jnp.stack([y[:, :, 0, :, :, 0, :], y[:, :, 1, :, :, 1, :]], axis=4)
    dc = jnp.transpose(dc, (1, 0, 3, 4, 5, 2)).reshape(2, S5_GROUPS, S5_GROUP, S5_STATE)
    return db[0], db[1], dc[0], -dc[1]


NEG = -1e30


GROUP = N_Q // N_KV


def _attn_masks(n):
    qi = lax.broadcasted_iota(jnp.int32, (GROUP * BLOCK, BLOCK), 0) % BLOCK
    kj = lax.broadcasted_iota(jnp.int32, (GROUP * BLOCK, BLOCK), 1)
    return jnp.logical_and(kj > qi, n > 0), kj <= qi


def _stack_heads(ref, kh):
    return jnp.concatenate([ref[:, (GROUP * kh + g) * HEAD_DIM:(GROUP * kh + g + 1) * HEAD_DIM] for g in range(GROUP)], axis=0)


def _unstack_heads(val):
    return jnp.concatenate([val[g * BLOCK:(g + 1) * BLOCK] for g in range(GROUP)], axis=1)


def _sink_column(sink_ref, kh):
    grp = lax.broadcasted_iota(jnp.int32, (GROUP * BLOCK, 1), 0) // BLOCK
    col = jnp.zeros((GROUP * BLOCK, 1), F32)
    for g in range(GROUP):
        col = jnp.where(grp == g, sink_ref[GROUP * kh + g], col)
    return col, grp


def _attn_exp(q4, kp, kc, sink, mask_p, mask_c):
    scale = 1.0 / math.sqrt(HEAD_DIM)
    nt = (((1,), (1,)), ((), ()))
    sp = jnp.where(mask_p, lax.dot_general(q4, kp, nt, preferred_element_type=F32) * scale, NEG)
    sc = jnp.where(mask_c, lax.dot_general(q4, kc, nt, preferred_element_type=F32) * scale, NEG)
    m = jnp.maximum(jnp.maximum(jnp.max(sp, axis=-1, keepdims=True), jnp.max(sc, axis=-1, keepdims=True)), sink)
    pp = jnp.exp(sp - m)
    pc = jnp.exp(sc - m)
    ps = jnp.exp(sink - m)
    inv = 1.0 / (jnp.sum(pp, axis=-1, keepdims=True) + jnp.sum(pc, axis=-1, keepdims=True) + ps)
    return pp, pc, ps, inv


def attn_fwd(q, kv, sinks):
    n_rows = q.shape[0]
    nb = n_rows // BLOCK

    def body(sink_ref, q_ref, kvp_ref, kvc_ref, o_ref):
        n = pl.program_id(0)
        mask_p, mask_c = _attn_masks(n)
        outs = []
        for kh in range(N_KV):
            ks, vs = slice(kh * HEAD_DIM, (kh + 1) * HEAD_DIM), slice((N_KV + kh) * HEAD_DIM, (N_KV + kh + 1) * HEAD_DIM)
            sink, _ = _sink_column(sink_ref, kh)
            pp, pc, _, inv = _attn_exp(_stack_heads(q_ref, kh), kvp_ref[:, ks], kvc_ref[:, ks], sink, mask_p, mask_c)
            o4 = (jnp.dot(pp.astype(BF16), kvp_ref[:, vs], preferred_element_type=F32)
                  + jnp.dot(pc.astype(BF16), kvc_ref[:, vs], preferred_element_type=F32)) * inv
            outs.append(_unstack_heads(o4))
        o_ref[...] = jnp.concatenate(outs, axis=1).astype(BF16)

    kvw = 2 * N_KV * HEAD_DIM
    return pl.pallas_call(
        body, grid=(nb,),
        in_specs=[pl.BlockSpec(memory_space=pltpu.SMEM), pl.BlockSpec((BLOCK, D_MODEL), lambda n: (n, 0)),
                  pl.BlockSpec((BLOCK, kvw), lambda n: (jnp.maximum(n - 1, 0), 0)), pl.BlockSpec((BLOCK, kvw), lambda n: (n, 0))],
        out_specs=pl.BlockSpec((BLOCK, D_MODEL), lambda n: (n, 0)), out_shape=_sds((n_rows, D_MODEL), BF16),
        name="attn_fwd", compiler_params=_params(("parallel",)))(sinks, q, kv, kv)


def attn_bwd(q, kv, do, sinks):
    n_rows = q.shape[0]
    nb = n_rows // BLOCK
    kvw = 2 * N_KV * HEAD_DIM
    tn = (((0,), (0,)), ((), ()))
    nt = (((1,), (1,)), ((), ()))
    scale = 1.0 / math.sqrt(HEAD_DIM)

    def body(sink_ref, q_ref, kvp_ref, kvc_ref, do_ref, dq_ref, dbq_ref, dprev_ref, dcur_ref, dsink_ref):
        n = pl.program_id(0)
        mask_p, mask_c = _attn_masks(n)
        lane = lax.broadcasted_iota(jnp.int32, (1, 128), 1)
        dqs, dsink = [], jnp.zeros((1, 128), F32)
        dkp, dkc, dvp, dvc = [], [], [], []
        for kh in range(N_KV):
            ks, vs = slice(kh * HEAD_DIM, (kh + 1) * HEAD_DIM), slice((N_KV + kh) * HEAD_DIM, (N_KV + kh + 1) * HEAD_DIM)
            q4, do4 = _stack_heads(q_ref, kh), _stack_heads(do_ref, kh)
            kp, kc, vp, vc = kvp_ref[:, ks], kvc_ref[:, ks], kvp_ref[:, vs], kvc_ref[:, vs]
            sink, grp = _sink_column(sink_ref, kh)
            pp, pc, ps, inv = _attn_exp(q4, kp, kc, sink, mask_p, mask_c)
            pp, pc = pp * inv, pc * inv
            dpp = lax.dot_general(do4, vp, nt, preferred_element_type=F32)
            dpc = lax.dot_general(do4, vc, nt, preferred_element_type=F32)
            delta = jnp.sum(pp * dpp, axis=-1, keepdims=True) + jnp.sum(pc * dpc, axis=-1, keepdims=True)
            dsp = (pp * (dpp - delta) * scale).astype(BF16)
            dsc = (pc * (dpc - delta) * scale).astype(BF16)
            dsk = ps * inv * delta
            for g in range(GROUP):
                dsink = dsink + jnp.where(lane == GROUP * kh + g, -jnp.sum(jnp.where(grp == g, dsk, 0.0)), 0.0)
            dqs.append(_unstack_heads(jnp.dot(dsp, kp, preferred_element_type=F32)
                                      + jnp.dot(dsc, kc, preferred_element_type=F32)))
            dkp.append(lax.dot_general(dsp, q4, tn, preferred_element_type=F32))
            dkc.append(lax.dot_general(dsc, q4, tn, preferred_element_type=F32))
            dvp.append(lax.dot_general(pp.astype(BF16), do4, tn, preferred_element_type=F32))
            dvc.append(lax.dot_general(pc.astype(BF16), do4, tn, preferred_element_type=F32))
        dq = jnp.concatenate(dqs, axis=1)
        dq_ref[...] = dq.astype(BF16)
        dprev_ref[0] = jnp.concatenate(dkp + dvp, axis=1)
        dcur_ref[0] = jnp.concatenate(dkc + dvc, axis=1)

        @pl.when(n == 0)
        def _():
            dbq_ref[...] = jnp.zeros_like(dbq_ref)
            dsink_ref[...] = jnp.zeros_like(dsink_ref)

        dbq_ref[...] += jnp.sum(dq, axis=0, keepdims=True)
        dsink_ref[...] += dsink

    blk = pl.BlockSpec((BLOCK, D_MODEL), lambda n: (n, 0))
    part = pl.BlockSpec((1, BLOCK, kvw), lambda n: (n, 0, 0))
    return pl.pallas_call(
        body, grid=(nb,),
        in_specs=[pl.BlockSpec(memory_space=pltpu.SMEM), blk,
                  pl.BlockSpec((BLOCK, kvw), lambda n: (jnp.maximum(n - 1, 0), 0)), pl.BlockSpec((BLOCK, kvw), lambda n: (n, 0)), blk],
        out_specs=[blk, pl.BlockSpec((1, D_MODEL), lambda n: (0, 0)), part, part, pl.BlockSpec((1, 128), lambda n: (0, 0))],
        out_shape=[_sds((n_rows, D_MODEL), BF16), _sds((1, D_MODEL), F32), _sds((nb, BLOCK, kvw), F32),
                   _sds((nb, BLOCK, kvw), F32), _sds((1, 128), F32)],
        name="attn_bwd", compiler_params=_params(("arbitrary",)))(sinks, q, kv, kv, do)


def kv_combine(dprev, dcur):
    nb, _, kvw = dprev.shape

    def body(dcur_ref, dnext_ref, dkv_ref, db_ref):
        m = pl.program_id(0)
        dkv = dcur_ref[0] + jnp.where(m + 1 < nb, dnext_ref[0], 0.0)
        dkv_ref[...] = dkv.astype(BF16)

        @pl.when(m == 0)
        def _():
            db_ref[...] = jnp.zeros_like(db_ref)

        db_ref[...] += jnp.sum(dkv, axis=0, keepdims=True)

    return pl.pallas_call(
        body, grid=(nb,),
        in_specs=[pl.BlockSpec((1, BLOCK, kvw), lambda m: (m, 0, 0)),
                  pl.BlockSpec((1, BLOCK, kvw), lambda m: (jnp.minimum(m + 1, nb - 1), 0, 0))],
        out_specs=[pl.BlockSpec((BLOCK, kvw), lambda m: (m, 0)), pl.BlockSpec((1, kvw), lambda m: (0, 0))],
        out_shape=[_sds((nb * BLOCK, kvw), BF16), _sds((1, kvw), F32)],
        name="kv_combine", compiler_params=_params(("arbitrary",)))(dcur, dprev)


def glu_bwd(dout, val, gate, tm=256):
    n_rows, d = dout.shape

    def body(do_ref, v_ref, g_ref, dz_ref, db_ref):
        i = pl.program_id(0)
        sg = jax.nn.sigmoid(g_ref[...])
        dval = do_ref[...] * sg
        dgate = do_ref[...] * v_ref[...] * sg * (1.0 - sg)
        dz = jnp.concatenate([dval, dgate], axis=1)
        dz_ref[...] = dz.astype(BF16)

        @pl.when(i == 0)
        def _():
            db_ref[...] = jnp.zeros_like(db_ref)

        db_ref[...] += jnp.sum(dz, axis=0, keepdims=True)

    row = pl.BlockSpec((tm, d), lambda i: (i, 0))
    return pl.pallas_call(
        body, grid=(n_rows // tm,), in_specs=[row, row, row],
        out_specs=[pl.BlockSpec((tm, 2 * d), lambda i: (i, 0)), pl.BlockSpec((1, 2 * d), lambda i: (0, 0))],
        out_shape=[_sds((n_rows, 2 * d), BF16), _sds((1, 2 * d), F32)],
        name="glu_bwd", compiler_params=_params(("arbitrary",)))(dout, val, gate)


def final_loss(h, target, gain, tm=256):
    n_rows, d = h.shape

    def body(h_ref, t_ref, g_ref, loss_ref, dh_ref, dhb_ref, dg_ref):
        i = pl.program_id(0)
        xh, r = _rms_hat(h_ref[...])
        err = xh * g_ref[...] - t_ref[...]
        dy = err * (1.0 / d)
        dxh = dy * g_ref[...]
        dx = r * (dxh - xh * jnp.mean(dxh * xh, axis=-1, keepdims=True))
        dh_ref[...] = dx
        dhb_ref[...] = dx.astype(BF16)

        @pl.when(i == 0)
        def _():
            loss_ref[...] = jnp.zeros_like(loss_ref)
            dg_ref[...] = jnp.zeros_like(dg_ref)

        loss_ref[...] += jnp.full((8, 128), 0.5 * jnp.sum(jnp.mean(err * err, axis=-1, keepdims=True)), F32)
        dg_ref[...] += jnp.sum(dy * xh, axis=0, keepdims=True)

    row = pl.BlockSpec((tm, d), lambda i: (i, 0))
    vec = pl.BlockSpec((1, d), lambda i: (0, 0))
    return pl.pallas_call(
        body, grid=(n_rows // tm,), in_specs=[row, row, vec],
        out_specs=[pl.BlockSpec((8, 128), lambda i: (0, 0)), row, row, vec],
        out_shape=[_sds((8, 128), F32), _sds((n_rows, d), F32), _sds((n_rows, d), BF16), _sds((1, d), F32)],
        name="final_loss", compiler_params=_params(("arbitrary",)))(h, target, gain)


def adamw(name, w, g, m, v, tm=256):
    n_rows, d = w.shape
    tm = tm if n_rows % tm == 0 else n_rows

    def body(w_ref, g_ref, m_ref, v_ref, d_ref, nm_ref, nv_ref):
        gv = g_ref[...]
        nm = ADAM_B1 * m_ref[...] + (1.0 - ADAM_B1) * gv
        nv = ADAM_B2 * v_ref[...] + (1.0 - ADAM_B2) * (gv * gv)
        m_hat = nm / (1.0 - ADAM_B1 ** ADAM_STEP)
        v_hat = nv / (1.0 - ADAM_B2 ** ADAM_STEP)
        d_ref[...] = -ADAM_LR * (m_hat / (jnp.sqrt(v_hat) + ADAM_EPS) + ADAM_WD * w_ref[...])
        nm_ref[...] = nm
        nv_ref[...] = nv

    row = pl.BlockSpec((tm, d), lambda i: (i, 0))
    return pl.pallas_call(
        body, grid=(n_rows // tm,), in_specs=[row] * 4, out_specs=[row] * 3,
        out_shape=[_sds((n_rows, d), F32)] * 3, name=name, compiler_params=_params(("parallel",)))(w, g, m, v)


def _position():
    x, y, c = lax.axis_index("x"), lax.axis_index("y"), lax.axis_index("c")
    others = [(1 - x, y), (x, 1 - y), (1 - x, 1 - y)]
    return x, y, c, others


def _window(ref, kind, chip, half, shard_shape):
    r, n = shard_shape
    if kind == "col":
        return ref.at[pl.ds(pl.multiple_of(half * (r // 2), 16), r // 2), pl.ds(pl.multiple_of(chip * n, 128), n)]
    return ref.at[pl.ds(pl.multiple_of(chip * r, 16), r), pl.ds(pl.multiple_of(half * (n // 2), 128), n // 2)]


def _half(ref, kind, half, shape):
    r, n = shape
    if kind == "col":
        return ref.at[pl.ds(pl.multiple_of(half * (r // 2), 16), r // 2), :]
    return ref.at[:, pl.ds(pl.multiple_of(half * (n // 2), 128), n // 2)]


def gather_weights(arrays, entries):
    na, nt = len(arrays), len(entries)
    shapes = [tuple(arrays[a].shape[1:]) for a, _, _ in entries]
    kinds = [k for _, _, k in entries]

    def full_shape(i):
        r, n = shapes[i]
        return (r, 4 * n) if kinds[i] == "col" else (4 * r, n)

    def body(*refs):
        in_refs, out_refs = refs[:na], refs[na:na + nt]
        stage_refs = refs[na + nt:na + 2 * nt]
        send_sems, recv_sems, local_sems = refs[na + 2 * nt:]
        x, y, c, others = _position()
        me = 2 * x + y
        pending = []
        for t in range(nt):
            r, n = shapes[t]
            stage_refs[t][...] = in_refs[entries[t][0]][entries[t][1]].astype(BF16)
            if kinds[t] == "col":
                own = out_refs[t].at[:, pl.ds(pl.multiple_of(me * n, 128), n)]
            else:
                own = out_refs[t].at[pl.ds(pl.multiple_of(me * r, 16), r), :]
            local = pltpu.make_async_copy(stage_refs[t], own, local_sems.at[t])
            local.start()
            pending.append(local)
        sends = []
        for t in range(nt):
            src = _half(stage_refs[t], kinds[t], c, shapes[t])
            for j, (ox, oy) in enumerate(others):
                cp = pltpu.make_async_remote_copy(
                    src_ref=src, dst_ref=_window(out_refs[t], kinds[t], me, c, shapes[t]),
                    send_sem=send_sems.at[t, j], recv_sem=recv_sems.at[t, j], device_id=(ox, oy, c), device_id_type=MESH)
                cp.start()
                sends.append(cp)
        for t in range(nt):
            for j, (ox, oy) in enumerate(others):
                landed = _window(out_refs[t], kinds[t], 2 * ox + oy, c, shapes[t])
                pltpu.make_async_remote_copy(
                    src_ref=landed, dst_ref=landed, send_sem=send_sems.at[t, j], recv_sem=recv_sems.at[t, j],
                    device_id=(ox, oy, c), device_id_type=MESH).wait_recv()
                fw = pltpu.make_async_remote_copy(
                    src_ref=landed, dst_ref=landed, send_sem=send_sems.at[t, 3 + j], recv_sem=recv_sems.at[t, 3 + j],
                    device_id=(x, y, 1 - c), device_id_type=MESH)
                fw.start()
                sends.append(fw)
        for t in range(nt):
            for j, (ox, oy) in enumerate(others):
                got = _window(out_refs[t], kinds[t], 2 * ox + oy, 1 - c, shapes[t])
                pltpu.make_async_remote_copy(
                    src_ref=got, dst_ref=got, send_sem=send_sems.at[t, 3 + j], recv_sem=recv_sems.at[t, 3 + j],
                    device_id=(x, y, 1 - c), device_id_type=MESH).wait_recv()
        for cp in sends:
            cp.wait_send()
        for local in pending:
            local.wait()

    vm = pl.BlockSpec(memory_space=pltpu.VMEM)
    hbm = pl.BlockSpec(memory_space=pl.ANY)
    return pl.pallas_call(
        body, in_specs=[vm] * na, out_specs=[hbm] * nt,
        out_shape=[_sds(full_shape(i), BF16) for i in range(nt)],
        scratch_shapes=[pltpu.VMEM(shapes[i], BF16) for i in range(nt)]
        + [pltpu.SemaphoreType.DMA((nt, 6)), pltpu.SemaphoreType.DMA((nt, 6)), pltpu.SemaphoreType.DMA((nt,))],
        name="gather_weights", compiler_params=_params())(*arrays)


def swap_halves(grads, kinds):
    nt = len(grads)
    shapes = [tuple(g.shape) for g in grads]

    def body(*refs):
        in_refs, out_refs = refs[:nt], refs[nt:2 * nt]
        send_sems, recv_sems = refs[2 * nt:]
        x, y, c, _ = _position()
        cps = []
        for t in range(nt):
            cp = pltpu.make_async_remote_copy(
                src_ref=_half(in_refs[t], kinds[t], 1 - c, shapes[t]), dst_ref=_half(out_refs[t], kinds[t], 1 - c, shapes[t]),
                send_sem=send_sems.at[t], recv_sem=recv_sems.at[t], device_id=(x, y, 1 - c), device_id_type=MESH)
            cp.start()
            cps.append(cp)
        for t in range(nt):
            mine = _half(out_refs[t], kinds[t], c, shapes[t])
            pltpu.make_async_remote_copy(
                src_ref=mine, dst_ref=mine, send_sem=send_sems.at[t], recv_sem=recv_sems.at[t],
                device_id=(x, y, 1 - c), device_id_type=MESH).wait_recv()
        for cp in cps:
            cp.wait_send()

    hbm = pl.BlockSpec(memory_space=pl.ANY)
    return pl.pallas_call(
        body, in_specs=[hbm] * nt, out_specs=[hbm] * nt, out_shape=[_sds(s, BF16) for s in shapes],
        scratch_shapes=[pltpu.SemaphoreType.DMA((nt,)), pltpu.SemaphoreType.DMA((nt,))],
        name="swap_halves", compiler_params=_params())(*grads)


def _half_spec(kind, shape, tiles):
    r, n = shape
    if kind == "col":
        tn = n // tiles
        return pl.BlockSpec((r // 2, tn), lambda i, s: (s[0], i))
    tm = r // tiles
    return pl.BlockSpec((tm, n // 2), lambda i, s: (i, s[0]))


def add_halves(name, mine, landed, kind, where, tiles=4):
    shape = tuple(mine.shape)
    r, n = shape
    out_shape = (r // 2, n) if kind == "col" else (r, n // 2)
    out_spec = (pl.BlockSpec((r // 2, n // tiles), lambda i, s: (0, i)) if kind == "col"
                else pl.BlockSpec((r // tiles, n // 2), lambda i, s: (i, 0)))

    def body(s_ref, a_ref, b_ref, o_ref):
        o_ref[...] = (a_ref[...].astype(F32) + b_ref[...].astype(F32)).astype(BF16)

    spec = _half_spec(kind, shape, tiles)
    return pl.pallas_call(
        body, grid_spec=pltpu.PrefetchScalarGridSpec(num_scalar_prefetch=1, grid=(tiles,), in_specs=[spec, spec],
                                                     out_specs=out_spec),
        out_shape=_sds(out_shape, BF16), name=name, compiler_params=_params(("parallel",)))(where, mine, landed)


def exchange_shards(parts, kinds, shard_shapes):
    nt = len(parts)

    def piece_shape(t):
        r, n = shard_shapes[t]
        return (r // 2, n) if kinds[t] == "col" else (r, n // 2)

    def piece(ref, t, chip):
        r, n = shard_shapes[t]
        if kinds[t] == "col":
            return ref.at[:, pl.ds(pl.multiple_of(chip * n, 128), n)]
        return ref.at[pl.ds(pl.multiple_of(chip * r, 16), r), :]

    def body(*refs):
        in_refs, out_refs = refs[:nt], refs[nt:2 * nt]
        send_sems, recv_sems = refs[2 * nt:]
        x, y, c, others = _position()
        cps = []
        for t in range(nt):
            for j, (ox, oy) in enumerate(others):
                cp = pltpu.make_async_remote_copy(
                    src_ref=piece(in_refs[t], t, 2 * ox + oy), dst_ref=out_refs[t].at[j],
                    send_sem=send_sems.at[t, j], recv_sem=recv_sems.at[t, j], device_id=(ox, oy, c), device_id_type=MESH)
                cp.start()
                cps.append(cp)
        for t in range(nt):
            for j, (ox, oy) in enumerate(others):
                pltpu.make_async_remote_copy(
                    src_ref=out_refs[t].at[j], dst_ref=out_refs[t].at[j], send_sem=send_sems.at[t, j],
                    recv_sem=recv_sems.at[t, j], device_id=(ox, oy, c), device_id_type=MESH).wait_recv()
        for cp in cps:
            cp.wait_send()

    hbm = pl.BlockSpec(memory_space=pl.ANY)
    return pl.pallas_call(
        body, in_specs=[hbm] * nt, out_specs=[hbm] * nt,
        out_shape=[_sds((3,) + piece_shape(t), BF16) for t in range(nt)],
        scratch_shapes=[pltpu.SemaphoreType.DMA((nt, 3)), pltpu.SemaphoreType.DMA((nt, 3))],
        name="exchange_shards", compiler_params=_params())(*parts)


def sum_shards(name, part, landed, kind, shard_shape, where, layer, n_layers, into=None, tiles=2):
    r, n = shard_shape
    if kind == "col":
        tm, width = r // 2 // tiles, n
        own = pl.BlockSpec((tm, n), lambda i, s: (i, s[1]))
        out = pl.BlockSpec((None, tm, n), lambda i, s: (layer, s[0] * tiles + i, 0))
    else:
        tm, width = r // tiles, n // 2
        own = pl.BlockSpec((tm, n // 2), lambda i, s: (s[1] * tiles + i, 0))
        out = pl.BlockSpec((None, tm, n // 2), lambda i, s: (layer, i, s[0]))

    def body(s_ref, a_ref, l_ref, *o_refs):
        o_refs[-1][...] = ((a_ref[...].astype(F32) + l_ref[0].astype(F32)) + l_ref[1].astype(F32)) + l_ref[2].astype(F32)

    in_specs = [own, pl.BlockSpec((3, tm, width), lambda i, s: (0, i, 0))]
    args, aliases = [where, part, landed], {}
    if into is not None:
        in_specs.append(pl.BlockSpec(memory_space=pl.ANY))
        args.append(into)
        aliases = {3: 0}
    return pl.pallas_call(
        body, grid_spec=pltpu.PrefetchScalarGridSpec(num_scalar_prefetch=1, grid=(tiles,), in_specs=in_specs, out_specs=out),
        out_shape=_sds((n_layers, r, n), F32), input_output_aliases=aliases, name=name,
        compiler_params=_params(("parallel",)))(*args)


def share_halves(arrays, entries):
    na, nt = len(arrays), len(entries)

    def body(*refs):
        out_refs = refs[na:2 * na]
        send_sems, recv_sems = refs[2 * na:]
        x, y, c, _ = _position()
        cps = []
        for t, (a, layer, kind) in enumerate(entries):
            shape = tuple(arrays[a].shape[1:])
            mine = _half(out_refs[a].at[layer], kind, c, shape)
            cp = pltpu.make_async_remote_copy(
                src_ref=mine, dst_ref=mine, send_sem=send_sems.at[t], recv_sem=recv_sems.at[t],
                device_id=(x, y, 1 - c), device_id_type=MESH)
            cp.start()
            cps.append(cp)
        for t, (a, layer, kind) in enumerate(entries):
            shape = tuple(arrays[a].shape[1:])
            other = _half(out_refs[a].at[layer], kind, 1 - c, shape)
            pltpu.make_async_remote_copy(
                src_ref=other, dst_ref=other, send_sem=send_sems.at[t], recv_sem=recv_sems.at[t],
                device_id=(x, y, 1 - c), device_id_type=MESH).wait_recv()
        for cp in cps:
            cp.wait_send()

    hbm = pl.BlockSpec(memory_space=pl.ANY)
    return pl.pallas_call(
        body, in_specs=[hbm] * na, out_specs=[hbm] * na, out_shape=[_sds(a.shape, F32) for a in arrays],
        input_output_aliases={i: i for i in range(na)},
        scratch_shapes=[pltpu.SemaphoreType.DMA((nt,)), pltpu.SemaphoreType.DMA((nt,))],
        name="share_halves", compiler_params=_params())(*arrays)


def all_reduce_small(name, buf):
    shape = tuple(buf.shape)

    def body(in_ref, out_ref, land, send_sems, recv_sems):
        x, y, c, _ = _position()
        out_ref[...] = in_ref[...]
        for s, peer in enumerate([(x, y, 1 - c), (1 - x, y, c), (x, 1 - y, c)]):
            cp = pltpu.make_async_remote_copy(
                src_ref=out_ref, dst_ref=land.at[s], send_sem=send_sems.at[s], recv_sem=recv_sems.at[s],
                device_id=peer, device_id_type=MESH)
            cp.start()
            cp.wait()
            out_ref[...] = out_ref[...] + land[s]

    vm = pl.BlockSpec(memory_space=pltpu.VMEM)
    return pl.pallas_call(
        body, in_specs=[vm], out_specs=vm, out_shape=_sds(shape, F32),
        scratch_shapes=[pltpu.VMEM((3,) + shape, F32), pltpu.SemaphoreType.DMA((3,)), pltpu.SemaphoreType.DMA((3,))],
        name=name, compiler_params=_params())(buf)


def _pack(arrays):
    flat = jnp.concatenate([a.reshape(-1).astype(F32) for a in arrays])
    pad = (-flat.shape[0]) % 1024
    return jnp.pad(flat, (0, pad)).reshape(-1, 128)


def _unpack(buf, like):
    flat = buf.reshape(-1)
    out, off = [], 0
    for a in like:
        size = math.prod(a.shape)
        out.append(flat[off:off + size].reshape(a.shape))
        off += size
    return out


def _local_step(x, target, small, full):
    d = D_MODEL
    dff = full["w_in0"].shape[1]
    first = lambda accs, e, r: [accs[0]]

    lam_r, lam_i, bbar_re, bbar_im = small["s5_disc"]
    rb, rc = _s5_matrices(bbar_re, bbar_im, small["s5_c_re"], small["s5_c_im"])
    rb16, rc16 = rb.astype(BF16), rc.astype(BF16)
    lr_t, li_t = lam_r.reshape(S5_BLOCKS, 8, 128), lam_i.reshape(S5_BLOCKS, 8, 128)
    (u,) = rms_fwd("norm_mix0", x, [small["norm_mix0"]], [F32])
    ge, y2, cs = s5_fwd(u, small["s5_d"], rb16, rc16, lr_t, li_t)
    h1, val, gate = mm_nn(
        "glu", ge, full["w_glu"], [0, d], d,
        lambda accs, e, r: [e[0] + (accs[0] + r[0]) * jax.nn.sigmoid(accs[1] + r[1]), accs[0] + r[0], accs[1] + r[1]],
        [F32, F32, F32], extras=[x], rowvecs=[(small["s5_b_glu"], 0), (small["s5_b_glu"], d)])

    def mlp_fwd(tag, h, gain, w_in, w_out):
        (n,) = rms_fwd("norm_mlp" + tag, h, [gain], [BF16])
        a, r = mm_nn("mlp_in" + tag, n, w_in, [0], dff,
                     lambda accs, e, rv: [accs[0], jnp.square(jnp.maximum(accs[0], 0.0))], [F32, BF16])
        (h_out,) = mm_nn("mlp_out" + tag, r, w_out, [0], d, lambda accs, e, rv: [e[0] + accs[0]], [F32], extras=[h])
        return h_out, (n, a, r)

    h2, mlp0 = mlp_fwd("0", h1, small["norm_mlp0"], full["w_in0"], full["w_out0"])

    nkv, n2 = rms_fwd("norm_kv_mix1", h2, [small["norm_kv"], small["norm_mix1"]], [BF16, BF16])
    kvw = 2 * N_KV * HEAD_DIM
    (kv,) = mm_nn("kv_proj", nkv, full["w_kv"], [0], kvw, lambda accs, e, r: [accs[0] + r[0]], [BF16],
                  rowvecs=[(small["b_kv"], 0)])
    (q,) = mm_nn("q_proj", n2, full["w_q"], [0], d, lambda accs, e, r: [accs[0] + r[0]], [BF16],
                 rowvecs=[(small["b_q"], 0)])
    sinks = small["sinks"].reshape(N_Q)
    o = attn_fwd(q, kv, sinks)
    (h3,) = mm_nn("o_proj", o, full["w_o"], [0], d, lambda accs, e, r: [e[0] + accs[0] + r[0]], [F32],
                  extras=[h2], rowvecs=[(small["b_o"], 0)])
    h4, mlp1 = mlp_fwd("1", h3, small["norm_mlp1"], full["w_in1"], full["w_out1"])
    loss_tile, dh, dhb, dg_final = final_loss(h4, target, small["norm_final"])

    grads_small, grads_full = {"norm_final": dg_final}, {}
    ident = lambda acc, e: [acc]

    def mlp_bwd(tag, dh, dhb, h_in, gain, w_in, w_out, saved):
        n, a, r = saved
        grads_full["w_out" + tag] = mm_tn("dw_out" + tag, r, dhb)
        (da,) = mm_nt("mlp_da" + tag, dhb, w_out, lambda acc, e: [acc * 2.0 * jnp.maximum(e[0], 0.0)], [BF16], extras=[a])
        grads_full["w_in" + tag] = mm_tn("dw_in" + tag, n, da)
        (dn,) = mm_nt("mlp_dn" + tag, da, w_in, ident, [F32])
        dx, dxb, colsum, dg = rms_bwd("norm_mlp_bwd" + tag, h_in, [dn], [gain], dh)
        grads_small["norm_mlp" + tag] = dg
        return dx, dxb, colsum

    dh3, dh3b, colsum3 = mlp_bwd("1", dh, dhb, h3, small["norm_mlp1"], full["w_in1"], full["w_out1"], mlp1)
    grads_small["b_o"] = colsum3
    grads_full["w_o"] = mm_tn("dw_o", o, dh3b)
    (do,) = mm_nt("attn_do", dh3b, full["w_o"], ident, [BF16])
    dq, dbq, dprev, dcur, dsink = attn_bwd(q, kv, do, sinks)
    dkv, dbkv = kv_combine(dprev, dcur)
    grads_small["b_q"], grads_small["b_kv"], grads_small["sinks"] = dbq, dbkv, dsink[:, :N_Q]
    grads_full["w_q"] = mm_tn("dw_q", n2, dq)
    grads_full["w_kv"] = mm_tn("dw_kv", nkv, dkv)
    (dn2,) = mm_nt("attn_dn", dq, full["w_q"], ident, [F32])
    (dnkv,) = mm_nt("kv_dn", dkv, full["w_kv"], ident, [F32])
    dh2, dh2b, _, dg_mix1, dg_kv = rms_bwd("norm_kv_mix1_bwd", h2, [dn2, dnkv], [small["norm_mix1"], small["norm_kv"]], dh3)
    grads_small["norm_mix1"], grads_small["norm_kv"] = dg_mix1, dg_kv
    dh1, _, _ = mlp_bwd("0", dh2, dh2b, h1, small["norm_mlp0"], full["w_in0"], full["w_out0"], mlp0)

    dz, db_glu = glu_bwd(dh1, val, gate)
    grads_small["s5_b_glu"] = db_glu
    grads_full["w_glu"] = mm_tn("dw_glu", ge, dz)
    (dy2,) = mm_nt("glu_dy", dz, full["w_glu"], lambda acc, e: [acc * _gelu_grad(e[0])], [F32], extras=[y2])
    rbt16, rct16 = jnp.swapaxes(rb16, 1, 2), jnp.swapaxes(rc16, 1, 2)
    du, dd, drb, drc, dlr, dli = s5_bwd(u, dy2, small["s5_d"], cs, rb16, rbt16, rct16, lr_t, li_t)
    grads_small["s5_d"] = dd
    grads_small["s5_mats"] = (drb, drc, dlr, dli)
    grad_x, _, _, dg_mix0 = rms_bwd("norm_mix0_bwd", x, [du], [small["norm_mix0"]], dh1)
    grads_small["norm_mix0"] = dg_mix0
    return loss_tile, grad_x, grads_small, grads_full


FULL_KINDS = {"w_glu": "col", "w_kv": "row", "w_q": "row", "w_o": "row", "w_in0": "col", "w_in1": "col",
              "w_out0": "row", "w_out1": "row"}
SMALL_NAMES = ["norm_mix", "norm_mlp", "norm_kv", "norm_final", "s5_a_re", "s5_a_im", "s5_log_dt", "s5_b_re", "s5_b_im",
               "s5_c_re", "s5_c_im", "s5_d", "s5_b_glu", "b_kv", "b_q", "sinks", "b_o"]
BIG_NAMES = ["s5_w_glu", "w_kv", "w_q", "w_o", "w_mlp_in", "w_mlp_out"]
WEIGHT_ORDER = ["norm_mix", "norm_mlp", "norm_kv", "norm_final", "s5_a_re", "s5_a_im", "s5_log_dt", "s5_b_re", "s5_b_im",
                "s5_c_re", "s5_c_im", "s5_d", "s5_w_glu", "s5_b_glu", "w_kv", "b_kv", "w_q", "b_q", "sinks", "w_o", "b_o",
                "w_mlp_in", "w_mlp_out"]


def kernel(x, norm_mix, norm_mlp, norm_kv, norm_final, s5_a_re, s5_a_im, s5_log_dt, s5_b_re, s5_b_im, s5_c_re, s5_c_im, s5_d, s5_w_glu, s5_b_glu, w_kv, b_kv, w_q, b_q, sinks, w_o, b_o, w_mlp_in, w_mlp_out, loss_target, m_norm_mix, m_norm_mlp, m_norm_kv, m_norm_final, m_s5_a_re, m_s5_a_im, m_s5_log_dt, m_s5_b_re, m_s5_b_im, m_s5_c_re, m_s5_c_im, m_s5_d, m_s5_w_glu, m_s5_b_glu, m_w_kv, m_b_kv, m_w_q, m_b_q, m_sinks, m_w_o, m_b_o, m_w_mlp_in, m_w_mlp_out, v_norm_mix, v_norm_mlp, v_norm_kv, v_norm_final, v_s5_a_re, v_s5_a_im, v_s5_log_dt, v_s5_b_re, v_s5_b_im, v_s5_c_re, v_s5_c_im, v_s5_d, v_s5_w_glu, v_s5_b_glu, v_w_kv, v_b_kv, v_w_q, v_b_q, v_sinks, v_w_o, v_b_o, v_w_mlp_in, v_w_mlp_out):
    env = dict(locals())
    w = {n: env[n] for n in WEIGHT_ORDER}
    mom = {n: env["m_" + n] for n in WEIGHT_ORDER}
    var = {n: env["v_" + n] for n in WEIGHT_ORDER}
    d = D_MODEL
    xi, yi, ci = lax.axis_index("x"), lax.axis_index("y"), lax.axis_index("c")
    chip = 2 * xi + yi
    where = jnp.stack([ci, chip]).astype(jnp.int32)

    dsh, bsh = s5_d.shape[1], s5_b_glu.shape[1]
    placed = jnp.concatenate([
        lax.dynamic_update_slice(jnp.zeros((4 * dsh,), F32), s5_d[0], (chip * dsh,)),
        lax.dynamic_update_slice(jnp.zeros((4 * bsh,), F32), s5_b_glu[0], (chip * bsh,))])
    placed = jnp.where(ci == 0, placed, 0.0).reshape(-1, 128)
    gathered = all_reduce_small("gather_vectors", placed).reshape(-1)
    d_full, bglu_full = gathered[:4 * dsh].reshape(1, -1), gathered[4 * dsh:].reshape(1, -1)

    big = [s5_w_glu, w_kv[None], w_q, w_o, w_mlp_in, w_mlp_out]
    entries = [(0, 0, "col"), (1, 0, "row"), (2, 0, "row"), (3, 0, "row"), (4, 0, "col"), (4, 1, "col"),
               (5, 0, "row"), (5, 1, "row")]
    names = ["w_glu", "w_kv", "w_q", "w_o", "w_in0", "w_in1", "w_out0", "w_out1"]
    full = dict(zip(names, gather_weights(big, entries)))

    disc = lambda *p: _s5_discretise(p[0], p[1], p[2], p[3], p[4])
    disc_args = (s5_a_re[0], s5_a_im[0], s5_log_dt[0], s5_b_re[0], s5_b_im[0])
    disc_out, disc_vjp = jax.vjp(disc, *disc_args)
    small = {
        "norm_mix0": norm_mix[0:1], "norm_mix1": norm_mix[1:2], "norm_mlp0": norm_mlp[0:1], "norm_mlp1": norm_mlp[1:2],
        "norm_kv": norm_kv.reshape(1, d), "norm_final": norm_final.reshape(1, d), "s5_disc": disc_out,
        "s5_c_re": s5_c_re[0], "s5_c_im": s5_c_im[0], "s5_d": d_full, "s5_b_glu": bglu_full,
        "b_kv": b_kv.reshape(1, -1), "b_q": b_q, "sinks": sinks, "b_o": b_o,
    }
    loss_tile, grad_x, gs, gf = _local_step(x[0], loss_target[0], small, full)

    drb, drc, dlr, dli = gs["s5_mats"]
    dbbar_re, dbbar_im, dc_re, dc_im = _s5_matrix_grads(drb, drc)
    pieces = [loss_tile[0:1, 0:1], gs["norm_mix0"], gs["norm_mix1"], gs["norm_mlp0"], gs["norm_mlp1"], gs["norm_kv"],
              gs["norm_final"], dlr, dli, dbbar_re, dbbar_im, dc_re, dc_im, gs["s5_d"], gs["s5_b_glu"], gs["b_kv"],
              gs["b_q"], gs["sinks"], gs["b_o"]]
    summed = _unpack(all_reduce_small("reduce_small", _pack(pieces)), pieces)
    (loss, g_mix0, g_mix1, g_mlp0, g_mlp1, g_kv, g_final, dlr, dli, dbbar_re, dbbar_im, dc_re, dc_im, g_d, g_bglu,
     g_bkv, g_bq, g_sinks, g_bo) = summed
    g_are, g_aim, g_dt, g_bre, g_bim = disc_vjp((dlr.reshape(S5_GROUPS, S5_STATE), dli.reshape(S5_GROUPS, S5_STATE),
                                                  dbbar_re, dbbar_im))
    grads = {
        "norm_mix": jnp.concatenate([g_mix0, g_mix1]), "norm_mlp": jnp.concatenate([g_mlp0, g_mlp1]),
        "norm_kv": g_kv.reshape(d), "norm_final": g_final.reshape(d), "s5_a_re": g_are[None], "s5_a_im": g_aim[None],
        "s5_log_dt": g_dt[None], "s5_b_re": g_bre[None], "s5_b_im": g_bim[None], "s5_c_re": dc_re[None],
        "s5_c_im": dc_im[None], "s5_d": lax.dynamic_slice(g_d, (0, chip * dsh), (1, dsh)),
        "s5_b_glu": lax.dynamic_slice(g_bglu, (0, chip * bsh), (1, bsh)), "b_kv": g_bkv.reshape(-1), "b_q": g_bq,
        "sinks": g_sinks, "b_o": g_bo,
    }

    kinds = [k for _, _, k in entries]
    shard_shapes = [tuple(big[a].shape[1:]) for a, _, _ in entries]
    partial = [gf[n] for n in names]
    landed = swap_halves(partial, kinds)
    chip_sums = [add_halves("add_halves_" + n, partial[t], landed[t], kinds[t], where) for t, n in enumerate(names)]
    arrived = exchange_shards(chip_sums, kinds, shard_shapes)
    reduced = [None] * len(big)
    for t, (a, layer, kind) in enumerate(entries):
        reduced[a] = sum_shards("sum_shards_" + names[t], chip_sums[t], arrived[t], kind, shard_shapes[t], where,
                                layer, big[a].shape[0], into=reduced[a])
    reduced = share_halves(reduced, entries)
    for n, g in zip(BIG_NAMES, reduced):
        grads[n] = g.reshape(w[n].shape)

    delta, new_m, new_v = {}, {}, {}
    for n in BIG_NAMES:
        flat = lambda a: a.reshape(-1, a.shape[-1])
        dl, nm, nv = adamw("adamw_" + n, flat(w[n]), flat(grads[n]), flat(mom[n]), flat(var[n]))
        delta[n], new_m[n], new_v[n] = dl.reshape(w[n].shape), nm.reshape(w[n].shape), nv.reshape(w[n].shape)
    sw, sg, sm, sv = ([t[n] for n in SMALL_NAMES] for t in (w, grads, mom, var))
    dl, nm, nv = adamw("adamw_small", _pack(sw), _pack(sg), _pack(sm), _pack(sv))
    for n, a, b, c_ in zip(SMALL_NAMES, _unpack(dl, sw), _unpack(nm, sw), _unpack(nv, sw)):
        delta[n], new_m[n], new_v[n] = a, b, c_

    out = [loss.reshape(()), grad_x[None]]
    for table in (grads, delta, new_m, new_v):
        out += [table[n].reshape(w[n].shape) for n in WEIGHT_ORDER]
    return tuple(out)
```

```python
import functools
import math

import jax
import jax.numpy as jnp
from jax import lax
from jax.experimental import pallas as pl
from jax.experimental.pallas import tpu as pltpu

F32 = jnp.float32
BF16 = jnp.bfloat16

D_MODEL = 1024
S5_GROUPS = 64
S5_GROUP = 16
S5_STATE = 64
N_KV = 4
N_Q = 16
HEAD_DIM = 64
BLOCK = 128
NORM_EPS = 1e-5
LAMBDA_RE_MAX = -1e-4
ADAM_LR, ADAM_B1, ADAM_B2, ADAM_EPS, ADAM_WD, ADAM_STEP = 0.001, 0.9, 0.999, 1e-08, 0.01, 10

VMEM_LIMIT_BYTES = 56 * 1024 * 1024
S5_CHUNK = 256
S5_BLOCKS = 4
MESH = pl.DeviceIdType.MESH


def _params(sem=None):
    return pltpu.CompilerParams(dimension_semantics=sem, vmem_limit_bytes=VMEM_LIMIT_BYTES)


def _sds(shape, dtype):
    return jax.ShapeDtypeStruct(shape, dtype)


def _rms_hat(xv):
    r = lax.rsqrt(jnp.mean(xv * xv, axis=-1, keepdims=True) + NORM_EPS)
    return xv * r, r


def rms_fwd(name, x, gains, out_dtypes, tm=256):
    n_rows, d = x.shape
    ng = len(gains)

    def body(x_ref, *refs):
        xh, _ = _rms_hat(x_ref[...])
        for g_ref, o_ref in zip(refs[:ng], refs[ng:]):
            o_ref[...] = (xh * g_ref[...]).astype(o_ref.dtype)

    row = pl.BlockSpec((tm, d), lambda i: (i, 0))
    vec = pl.BlockSpec((1, d), lambda i: (0, 0))
    return pl.pallas_call(
        body, grid=(n_rows // tm,), in_specs=[row] + [vec] * ng, out_specs=[row] * ng,
        out_shape=[_sds((n_rows, d), dt) for dt in out_dtypes], name=name,
        compiler_params=_params(("parallel",)))(x, *gains)


def rms_bwd(name, x, dys, gains, res, tm=256):
    n_rows, d = x.shape
    ng = len(gains)

    def body(x_ref, res_ref, *refs):
        dy_refs, g_refs = refs[:ng], refs[ng:2 * ng]
        dx_ref, dxb_ref, cs_ref = refs[2 * ng:2 * ng + 3]
        dg_refs = refs[2 * ng + 3:]
        i = pl.program_id(0)
        xh, r = _rms_hat(x_ref[...])
        dxh = jnp.zeros_like(xh)
        dgs = []
        for dy_ref, g_ref in zip(dy_refs, g_refs):
            dy = dy_ref[...].astype(F32)
            dxh = dxh + dy * g_ref[...]
            dgs.append(jnp.sum(dy * xh, axis=0, keepdims=True))
        dx = r * (dxh - xh * jnp.mean(dxh * xh, axis=-1, keepdims=True)) + res_ref[...]
        dx_ref[...] = dx
        dxb_ref[...] = dx.astype(BF16)
        cs = jnp.sum(dx, axis=0, keepdims=True)

        @pl.when(i == 0)
        def _():
            cs_ref[...] = jnp.zeros_like(cs_ref)
            for dg_ref in dg_refs:
                dg_ref[...] = jnp.zeros_like(dg_ref)

        cs_ref[...] += cs
        for dg_ref, dg in zip(dg_refs, dgs):
            dg_ref[...] += dg

    row = pl.BlockSpec((tm, d), lambda i: (i, 0))
    vec = pl.BlockSpec((1, d), lambda i: (0, 0))
    return pl.pallas_call(
        body, grid=(n_rows // tm,), in_specs=[row, row] + [row] * ng + [vec] * ng,
        out_specs=[row, row, vec] + [vec] * ng,
        out_shape=[_sds((n_rows, d), F32), _sds((n_rows, d), BF16), _sds((1, d), F32)] + [_sds((1, d), F32)] * ng,
        name=name, compiler_params=_params(("arbitrary",)))(x, res, *dys, *gains)


def mm_nn(name, a, w, col_offsets, n_out, epilogue, out_dtypes, extras=(), rowvecs=(), tm=1024, tn=512):
    m, k = a.shape
    tm, tn = min(tm, m), min(tn, n_out)
    nw, ne, nr = len(col_offsets), len(extras), len(rowvecs)

    def body(a_ref, *refs):
        w_refs, e_refs, r_refs = refs[:nw], refs[nw:nw + ne], refs[nw + ne:nw + ne + nr]
        o_refs = refs[nw + ne + nr:]
        av = a_ref[...]
        accs = [jnp.dot(av, w_ref[...], preferred_element_type=F32) for w_ref in w_refs]
        outs = epilogue(accs, [e[...] for e in e_refs], [r[...] for r in r_refs])
        for o_ref, o in zip(o_refs, outs):
            o_ref[...] = o.astype(o_ref.dtype)

    def wspec(off):
        return pl.BlockSpec((k, tn), lambda j, i, off=off: (0, off // tn + j))

    def rspec(off):
        return pl.BlockSpec((1, tn), lambda j, i, off=off: (0, off // tn + j))

    tile = pl.BlockSpec((tm, tn), lambda j, i: (i, j))
    in_specs = ([pl.BlockSpec((tm, k), lambda j, i: (i, 0))] + [wspec(o) for o in col_offsets]
                + [tile] * ne + [rspec(o) for _, o in rowvecs])
    return pl.pallas_call(
        body, grid=(n_out // tn, m // tm), in_specs=in_specs, out_specs=[tile] * len(out_dtypes),
        out_shape=[_sds((m, n_out), dt) for dt in out_dtypes], name=name,
        compiler_params=_params(("parallel", "parallel")))(a, *([w] * nw), *extras, *[r for r, _ in rowvecs])


def mm_nt(name, g, w, epilogue, out_dtypes, extras=(), tm=512, tk=512):
    m, n = g.shape
    k = w.shape[0]
    tm, tk = min(tm, m), min(tk, k)
    ne = len(extras)

    def body(g_ref, w_ref, *refs):
        e_refs, o_refs = refs[:ne], refs[ne:]
        acc = lax.dot_general(g_ref[...], w_ref[...], (((1,), (1,)), ((), ())), preferred_element_type=F32)
        outs = epilogue(acc, [e[...] for e in e_refs])
        for o_ref, o in zip(o_refs, outs):
            o_ref[...] = o.astype(o_ref.dtype)

    tile = pl.BlockSpec((tm, tk), lambda i, j: (i, j))
    return pl.pallas_call(
        body, grid=(m // tm, k // tk),
        in_specs=[pl.BlockSpec((tm, n), lambda i, j: (i, 0)), pl.BlockSpec((tk, n), lambda i, j: (j, 0))] + [tile] * ne,
        out_specs=[tile] * len(out_dtypes), out_shape=[_sds((m, k), dt) for dt in out_dtypes], name=name,
        compiler_params=_params(("parallel", "parallel")))(g, w, *extras)


def mm_tn(name, a, g, tk=512, tn=512):
    m, k = a.shape
    n = g.shape[1]
    tk, tn = min(tk, k), min(tn, n)

    def body(a_ref, g_ref, o_ref):
        acc = lax.dot_general(a_ref[...], g_ref[...], (((0,), (0,)), ((), ())), preferred_element_type=F32)
        o_ref[...] = acc.astype(o_ref.dtype)

    return pl.pallas_call(
        body, grid=(k // tk, n // tn),
        in_specs=[pl.BlockSpec((m, tk), lambda i, j: (0, i)), pl.BlockSpec((m, tn), lambda i, j: (0, j))],
        out_specs=pl.BlockSpec((tk, tn), lambda i, j: (i, j)), out_shape=_sds((k, n), BF16), name=name,
        compiler_params=_params(("parallel", "parallel")))(a, g)


def _masked_rows(dst_ref, val, tc):
    for half in range(2):
        v = val[:, half * 128:(half + 1) * 128]
        col = lax.broadcasted_iota(jnp.int32, v.shape, 1) // 32 + 4 * half
        for s8 in range(8):
            rows = jnp.where(col == s8, v, 0.0) if s8 // 4 == half else jnp.zeros_like(v)
            dst_ref.at[half][pl.ds(s8, tc, stride=8), :] = rows


def _staged(ref):
    return jnp.concatenate([ref[0], ref[1]], axis=1)


def _stage(ref, val):
    ref[0] = val[:, 0:128]
    ref[1] = val[:, 128:256]


def _gather_rows(src_ref, tc):
    halves = []
    for half in range(2):
        col = lax.broadcasted_iota(jnp.int32, (tc, 128), 1) // 32 + 4 * half
        out = jnp.zeros((tc, 128), F32)
        for s8 in range(4 * half, 4 * half + 4):
            out = jnp.where(col == s8, src_ref.at[half][pl.ds(s8, tc, stride=8), :], out)
        halves.append(out)
    return jnp.concatenate(halves, axis=1)


def _gelu(x):
    c = math.sqrt(2.0 / math.pi)
    return 0.5 * x * (1.0 + jnp.tanh(c * (x + 0.044715 * x * x * x)))


def _gelu_grad(x):
    c = math.sqrt(2.0 / math.pi)
    t = jnp.tanh(c * (x + 0.044715 * x * x * x))
    return 0.5 * (1.0 + t) + 0.5 * x * (1.0 - t * t) * c * (1.0 + 3.0 * 0.044715 * x * x)


def s5_fwd(u, d_skip, rb, rc, lam_r, lam_i):
    n_rows = u.shape[0]
    tc = min(S5_CHUNK, n_rows)
    nc = n_rows // tc

    def body(u_ref, d_ref, rb_ref, rc_ref, lr_ref, li_ref, ge_ref, y2_ref, cs_ref, lhs, bux, yrows, carry):
        i = pl.program_id(0)

        @pl.when(i == 0)
        def _():
            carry[...] = jnp.zeros_like(carry)

        cs_ref[0] = carry[...]
        for blk in range(S5_BLOCKS):
            _masked_rows(lhs, u_ref[:, blk * 256:(blk + 1) * 256], tc)
            bux[blk] = jnp.dot(_staged(lhs).astype(BF16), rb_ref[blk], preferred_element_type=F32)
        lam = [(lr_ref[blk], li_ref[blk]) for blk in range(S5_BLOCKS)]

        def step(t, c):
            r0 = pl.multiple_of(t * 8, 8)
            new = []
            for blk in range(S5_BLOCKS):
                xr, xi = c[2 * blk], c[2 * blk + 1]
                lr, li = lam[blk]
                nr = lr * xr - li * xi + bux[blk, pl.ds(r0, 8), 0:128]
                ni = lr * xi + li * xr + bux[blk, pl.ds(r0, 8), 128:256]
                bux[blk, pl.ds(r0, 8), 0:128] = nr
                bux[blk, pl.ds(r0, 8), 128:256] = ni
                new += [nr, ni]
            return tuple(new)

        c0 = []
        for blk in range(S5_BLOCKS):
            c0 += [carry[blk, :, 0:128], carry[blk, :, 128:256]]
        cn = lax.fori_loop(0, tc, step, tuple(c0))
        for blk in range(S5_BLOCKS):
            carry[blk, :, 0:128] = cn[2 * blk]
            carry[blk, :, 128:256] = cn[2 * blk + 1]
        for blk in range(S5_BLOCKS):
            _stage(yrows, jnp.dot(bux[blk].astype(BF16), rc_ref[blk], preferred_element_type=F32))
            sl = slice(blk * 256, (blk + 1) * 256)
            y2 = _gather_rows(yrows, tc) + d_ref[:, sl] * u_ref[:, sl]
            y2_ref[:, sl] = y2
            ge_ref[:, sl] = _gelu(y2).astype(BF16)

    row = pl.BlockSpec((tc, D_MODEL), lambda i: (i, 0))
    mat = pl.BlockSpec((S5_BLOCKS, 256, 256), lambda i: (0, 0, 0))
    lamspec = pl.BlockSpec((S5_BLOCKS, 8, 128), lambda i: (0, 0, 0))
    return pl.pallas_call(
        body, grid=(nc,),
        in_specs=[row, pl.BlockSpec((1, D_MODEL), lambda i: (0, 0)), mat, mat, lamspec, lamspec],
        out_specs=[row, row, pl.BlockSpec((1, S5_BLOCKS, 8, 256), lambda i: (i, 0, 0, 0))],
        out_shape=[_sds((n_rows, D_MODEL), BF16), _sds((n_rows, D_MODEL), F32), _sds((nc, S5_BLOCKS, 8, 256), F32)],
        scratch_shapes=[pltpu.VMEM((2, 8 * tc, 128), F32), pltpu.VMEM((S5_BLOCKS, 8 * tc, 256), F32),
                        pltpu.VMEM((2, 8 * tc, 128), F32), pltpu.VMEM((S5_BLOCKS, 8, 256), F32)],
        name="s5_fwd", compiler_params=_params(("arbitrary",)))(u, d_skip, rb, rc, lam_r, lam_i)


def s5_bwd(u, dy2, d_skip, cs, rb, rbt, rct, lam_r, lam_i):
    n_rows = u.shape[0]
    tc = min(S5_CHUNK, n_rows)
    nc = n_rows // tc

    def body(u_ref, dy_ref, d_ref, cs_ref, rb_ref, rbt_ref, rct_ref, lr_ref, li_ref,
             du_ref, dd_ref, drb_ref, drc_ref, dlr_ref, dli_ref, tmp, lhsu, lhsd, xs, adj, acarry):
        i = pl.program_id(0)

        @pl.when(i == 0)
        def _():
            acarry[...] = jnp.zeros_like(acarry)
            dd_ref[...] = jnp.zeros_like(dd_ref)
            drb_ref[...] = jnp.zeros_like(drb_ref)
            drc_ref[...] = jnp.zeros_like(drc_ref)
            dlr_ref[...] = jnp.zeros_like(dlr_ref)
            dli_ref[...] = jnp.zeros_like(dli_ref)

        dd_ref[...] += jnp.sum(dy_ref[...] * u_ref[...], axis=0, keepdims=True)
        for blk in range(S5_BLOCKS):
            sl = slice(blk * 256, (blk + 1) * 256)
            _masked_rows(tmp, u_ref[:, sl], tc)
            lhsu[blk] = _staged(tmp).astype(BF16)
            xs[blk] = jnp.dot(lhsu[blk], rb_ref[blk], preferred_element_type=F32)
            _masked_rows(tmp, dy_ref[:, sl], tc)
            lhsd[blk] = _staged(tmp).astype(BF16)
            adj[blk] = jnp.dot(lhsd[blk], rct_ref[blk], preferred_element_type=F32)
        lam = [(lr_ref[blk], li_ref[blk]) for blk in range(S5_BLOCKS)]

        def fstep(t, c):
            r0 = pl.multiple_of(t * 8, 8)
            new = []
            for blk in range(S5_BLOCKS):
                xr, xi = c[2 * blk], c[2 * blk + 1]
                lr, li = lam[blk]
                nr = lr * xr - li * xi + xs[blk, pl.ds(r0, 8), 0:128]
                ni = lr * xi + li * xr + xs[blk, pl.ds(r0, 8), 128:256]
                xs[blk, pl.ds(r0, 8), 0:128] = nr
                xs[blk, pl.ds(r0, 8), 128:256] = ni
                new += [nr, ni]
            return tuple(new)

        c0 = []
        for blk in range(S5_BLOCKS):
            c0 += [cs_ref[0, blk, :, 0:128], cs_ref[0, blk, :, 128:256]]
        lax.fori_loop(0, tc, fstep, tuple(c0))

        def bstep(k, c):
            t = tc - 1 - k
            r0 = pl.multiple_of(t * 8, 8)
            rp = pl.multiple_of(jnp.maximum(t - 1, 0) * 8, 8)
            first = t == 0
            new_a, new_g = [], []
            for blk in range(S5_BLOCKS):
                ar, ai = c[0][2 * blk], c[0][2 * blk + 1]
                glr, gli = c[1][2 * blk], c[1][2 * blk + 1]
                lr, li = lam[blk]
                nr = lr * ar + li * ai + adj[blk, pl.ds(r0, 8), 0:128]
                ni = lr * ai - li * ar + adj[blk, pl.ds(r0, 8), 128:256]
                adj[blk, pl.ds(r0, 8), 0:128] = nr
                adj[blk, pl.ds(r0, 8), 128:256] = ni
                pr = jnp.where(first, cs_ref[0, blk, :, 0:128], xs[blk, pl.ds(rp, 8), 0:128])
                pi = jnp.where(first, cs_ref[0, blk, :, 128:256], xs[blk, pl.ds(rp, 8), 128:256])
                new_a += [nr, ni]
                new_g += [glr + nr * pr + ni * pi, gli + ni * pr - nr * pi]
            return tuple(new_a), tuple(new_g)

        a0, g0 = [], []
        for blk in range(S5_BLOCKS):
            a0 += [acarry[blk, :, 0:128], acarry[blk, :, 128:256]]
            g0 += [dlr_ref[blk], dli_ref[blk]]
        an, gn = lax.fori_loop(0, tc, bstep, (tuple(a0), tuple(g0)))
        for blk in range(S5_BLOCKS):
            acarry[blk, :, 0:128] = an[2 * blk]
            acarry[blk, :, 128:256] = an[2 * blk + 1]
            dlr_ref[blk] = gn[2 * blk]
            dli_ref[blk] = gn[2 * blk + 1]
        for blk in range(S5_BLOCKS):
            sl = slice(blk * 256, (blk + 1) * 256)
            ab = adj[blk].astype(BF16)
            _stage(tmp, jnp.dot(ab, rbt_ref[blk], preferred_element_type=F32))
            du_ref[:, sl] = _gather_rows(tmp, tc) + d_ref[:, sl] * dy_ref[:, sl]
            drb_ref[blk] += lax.dot_general(lhsu[blk], ab, (((0,), (0,)), ((), ())), preferred_element_type=F32)
            drc_ref[blk] += lax.dot_general(xs[blk].astype(BF16), lhsd[blk], (((0,), (0,)), ((), ())),
                                            preferred_element_type=F32)

    rev = pl.BlockSpec((tc, D_MODEL), lambda i: (nc - 1 - i, 0))
    vec = pl.BlockSpec((1, D_MODEL), lambda i: (0, 0))
    mat = pl.BlockSpec((S5_BLOCKS, 256, 256), lambda i: (0, 0, 0))
    lamspec = pl.BlockSpec((S5_BLOCKS, 8, 128), lambda i: (0, 0, 0))
    big = pltpu.VMEM((S5_BLOCKS, 8 * tc, 256), F32)
    bigb = pltpu.VMEM((S5_BLOCKS, 8 * tc, 256), BF16)
    return pl.pallas_call(
        body, grid=(nc,),
        in_specs=[rev, rev, vec, pl.BlockSpec((1, S5_BLOCKS, 8, 256), lambda i: (nc - 1 - i, 0, 0, 0)),
                  mat, mat, mat, lamspec, lamspec],
        out_specs=[rev, vec, mat, mat, lamspec, lamspec],
        out_shape=[_sds((n_rows, D_MODEL), F32), _sds((1, D_MODEL), F32), _sds((S5_BLOCKS, 256, 256), F32),
                   _sds((S5_BLOCKS, 256, 256), F32), _sds((S5_BLOCKS, 8, 128), F32), _sds((S5_BLOCKS, 8, 128), F32)],
        scratch_shapes=[pltpu.VMEM((2, 8 * tc, 128), F32), bigb, bigb, big, big, pltpu.VMEM((S5_BLOCKS, 8, 256), F32)],
        name="s5_bwd", compiler_params=_params(("arbitrary",)))(u, dy2, d_skip, cs, rb, rbt, rct, lam_r, lam_i)


def _s5_discretise(a_re, a_im, log_dt, b_re, b_im):
    lam = lax.complex(jnp.minimum(a_re, LAMBDA_RE_MAX), a_im)
    dt = jnp.exp(log_dt)[:, None]
    lam_bar = jnp.exp(lam * dt)
    b_bar = ((lam_bar - 1.0) / lam)[:, :, None] * lax.complex(b_re, b_im)
    return jnp.real(lam_bar), jnp.imag(lam_bar), jnp.real(b_bar), jnp.imag(b_bar)


def _s5_matrices(bbar_re, bbar_im, c_re, c_im):
    eye2 = jnp.eye(2, dtype=F32)
    bst = jnp.stack([bbar_re, bbar_im]).reshape(2, S5_BLOCKS, 8, 2, S5_STATE, S5_GROUP)
    bt = jnp.transpose(bst, (1, 2, 3, 5, 0, 4))
    rb = (bt[:, :, :, :, :, None, :] * eye2[None, None, :, None, None, :, None]).reshape(S5_BLOCKS, 256, 256)
    cst = jnp.stack([c_re, -c_im]).reshape(2, S5_BLOCKS, 8, 2, S5_GROUP, S5_STATE)
    ct = jnp.transpose(cst, (1, 0, 5, 2, 3, 4))
    rc = (ct[:, :, None, :, :, :, :] * eye2[None, None, :, None, None, :, None]).reshape(S5_BLOCKS, 256, 256)
    return rb, rc


def _s5_matrix_grads(drb, drc):
    x = drb.reshape(S5_BLOCKS, 8, 2, S5_GROUP, 2, 2, S5_STATE)
    db = jnp.stack([x[:, :, 0, :, :, 0, :], x[:, :, 1, :, :, 1, :]], axis=2)
    db = jnp.transpose(db, (4, 0, 1, 2, 5, 3)).reshape(2, S5_GROUPS, S5_STATE, S5_GROUP)
    y = drc.reshape(S5_BLOCKS, 2, 2, S5_STATE, 8, 2, S5_GROUP)
    dc = jnp.stack([y[:, :, 0, :, :, 0, :], y[:, :, 1, :, :, 1, :]], axis=4)
    dc = jnp.transpose(dc, (1, 0, 3, 4, 5, 2)).reshape(2, S5_GROUPS, S5_GROUP, S5_STATE)
    return db[0], db[1], dc[0], -dc[1]


NEG = -1e30


GROUP = N_Q // N_KV


def _attn_masks(n):
    qi = lax.broadcasted_iota(jnp.int32, (GROUP * BLOCK, BLOCK), 0) % BLOCK
    kj = lax.broadcasted_iota(jnp.int32, (GROUP * BLOCK, BLOCK), 1)
    return jnp.logical_and(kj > qi, n > 0), kj <= qi


def _stack_heads(ref, kh):
    return jnp.concatenate([ref[:, (GROUP * kh + g) * HEAD_DIM:(GROUP * kh + g + 1) * HEAD_DIM] for g in range(GROUP)], axis=0)


def _unstack_heads(val):
    return jnp.concatenate([val[g * BLOCK:(g + 1) * BLOCK] for g in range(GROUP)], axis=1)


def _sink_column(sink_ref, kh):
    grp = lax.broadcasted_iota(jnp.int32, (GROUP * BLOCK, 1), 0) // BLOCK
    col = jnp.zeros((GROUP * BLOCK, 1), F32)
    for g in range(GROUP):
        col = jnp.where(grp == g, sink_ref[GROUP * kh + g], col)
    return col, grp


def _attn_exp(q4, kp, kc, sink, mask_p, mask_c):
    scale = 1.0 / math.sqrt(HEAD_DIM)
    nt = (((1,), (1,)), ((), ()))
    sp = jnp.where(mask_p, lax.dot_general(q4, kp, nt, preferred_element_type=F32) * scale, NEG)
    sc = jnp.where(mask_c, lax.dot_general(q4, kc, nt, preferred_element_type=F32) * scale, NEG)
    m = jnp.maximum(jnp.maximum(jnp.max(sp, axis=-1, keepdims=True), jnp.max(sc, axis=-1, keepdims=True)), sink)
    pp = jnp.exp(sp - m)
    pc = jnp.exp(sc - m)
    ps = jnp.exp(sink - m)
    inv = 1.0 / (jnp.sum(pp, axis=-1, keepdims=True) + jnp.sum(pc, axis=-1, keepdims=True) + ps)
    return pp, pc, ps, inv


def attn_fwd(q, kv, sinks):
    n_rows = q.shape[0]
    nb = n_rows // BLOCK

    def body(sink_ref, q_ref, kvp_ref, kvc_ref, o_ref):
        n = pl.program_id(0)
        mask_p, mask_c = _attn_masks(n)
        outs = []
        for kh in range(N_KV):
            ks, vs = slice(kh * HEAD_DIM, (kh + 1) * HEAD_DIM), slice((N_KV + kh) * HEAD_DIM, (N_KV + kh + 1) * HEAD_DIM)
            sink, _ = _sink_column(sink_ref, kh)
            pp, pc, _, inv = _attn_exp(_stack_heads(q_ref, kh), kvp_ref[:, ks], kvc_ref[:, ks], sink, mask_p, mask_c)
            o4 = (jnp.dot(pp.astype(BF16), kvp_ref[:, vs], preferred_element_type=F32)
                  + jnp.dot(pc.astype(BF16), kvc_ref[:, vs], preferred_element_type=F32)) * inv
            outs.append(_unstack_heads(o4))
        o_ref[...] = jnp.concatenate(outs, axis=1).astype(BF16)

    kvw = 2 * N_KV * HEAD_DIM
    return pl.pallas_call(
        body, grid=(nb,),
        in_specs=[pl.BlockSpec(memory_space=pltpu.SMEM), pl.BlockSpec((BLOCK, D_MODEL), lambda n: (n, 0)),
                  pl.BlockSpec((BLOCK, kvw), lambda n: (jnp.maximum(n - 1, 0), 0)), pl.BlockSpec((BLOCK, kvw), lambda n: (n, 0))],
        out_specs=pl.BlockSpec((BLOCK, D_MODEL), lambda n: (n, 0)), out_shape=_sds((n_rows, D_MODEL), BF16),
        name="attn_fwd", compiler_params=_params(("parallel",)))(sinks, q, kv, kv)


def attn_bwd(q, kv, do, sinks):
    n_rows = q.shape[0]
    nb = n_rows // BLOCK
    kvw = 2 * N_KV * HEAD_DIM
    tn = (((0,), (0,)), ((), ()))
    nt = (((1,), (1,)), ((), ()))
    scale = 1.0 / math.sqrt(HEAD_DIM)

    def body(sink_ref, q_ref, kvp_ref, kvc_ref, do_ref, dq_ref, dbq_ref, dprev_ref, dcur_ref, dsink_ref):
        n = pl.program_id(0)
        mask_p, mask_c = _attn_masks(n)
        lane = lax.broadcasted_iota(jnp.int32, (1, 128), 1)
        dqs, dsink = [], jnp.zeros((1, 128), F32)
        dkp, dkc, dvp, dvc = [], [], [], []
        for kh in range(N_KV):
            ks, vs = slice(kh * HEAD_DIM, (kh + 1) * HEAD_DIM), slice((N_KV + kh) * HEAD_DIM, (N_KV + kh + 1) * HEAD_DIM)
            q4, do4 = _stack_heads(q_ref, kh), _stack_heads(do_ref, kh)
            kp, kc, vp, vc = kvp_ref[:, ks], kvc_ref[:, ks], kvp_ref[:, vs], kvc_ref[:, vs]
            sink, grp = _sink_column(sink_ref, kh)
            pp, pc, ps, inv = _attn_exp(q4, kp, kc, sink, mask_p, mask_c)
            pp, pc = pp * inv, pc * inv
            dpp = lax.dot_general(do4, vp, nt, preferred_element_type=F32)
            dpc = lax.dot_general(do4, vc, nt, preferred_element_type=F32)
            delta = jnp.sum(pp * dpp, axis=-1, keepdims=True) + jnp.sum(pc * dpc, axis=-1, keepdims=True)
            dsp = (pp * (dpp - delta) * scale).astype(BF16)
            dsc = (pc * (dpc - delta) * scale).astype(BF16)
            dsk = ps * inv * delta
            for g in range(GROUP):
                dsink = dsink + jnp.where(lane == GROUP * kh + g, -jnp.sum(jnp.where(grp == g, dsk, 0.0)), 0.0)
            dqs.append(_unstack_heads(jnp.dot(dsp, kp, preferred_element_type=F32)
                                      + jnp.dot(dsc, kc, preferred_element_type=F32)))
            dkp.append(lax.dot_general(dsp, q4, tn, preferred_element_type=F32))
            dkc.append(lax.dot_general(dsc, q4, tn, preferred_element_type=F32))
            dvp.append(lax.dot_general(pp.astype(BF16), do4, tn, preferred_element_type=F32))
            dvc.append(lax.dot_general(pc.astype(BF16), do4, tn, preferred_element_type=F32))
        dq = jnp.concatenate(dqs, axis=1)
        dq_ref[...] = dq.astype(BF16)
        dprev_ref[0] = jnp.concatenate(dkp + dvp, axis=1)
        dcur_ref[0] = jnp.concatenate(dkc + dvc, axis=1)

        @pl.when(n == 0)
        def _():
            dbq_ref[...] = jnp.zeros_like(dbq_ref)
            dsink_ref[...] = jnp.zeros_like(dsink_ref)

        dbq_ref[...] += jnp.sum(dq, axis=0, keepdims=True)
        dsink_ref[...] += dsink

    blk = pl.BlockSpec((BLOCK, D_MODEL), lambda n: (n, 0))
    part = pl.BlockSpec((1, BLOCK, kvw), lambda n: (n, 0, 0))
    return pl.pallas_call(
        body, grid=(nb,),
        in_specs=[pl.BlockSpec(memory_space=pltpu.SMEM), blk,
                  pl.BlockSpec((BLOCK, kvw), lambda n: (jnp.maximum(n - 1, 0), 0)), pl.BlockSpec((BLOCK, kvw), lambda n: (n, 0)), blk],
        out_specs=[blk, pl.BlockSpec((1, D_MODEL), lambda n: (0, 0)), part, part, pl.BlockSpec((1, 128), lambda n: (0, 0))],
        out_shape=[_sds((n_rows, D_MODEL), BF16), _sds((1, D_MODEL), F32), _sds((nb, BLOCK, kvw), F32),
                   _sds((nb, BLOCK, kvw), F32), _sds((1, 128), F32)],
        name="attn_bwd", compiler_params=_params(("arbitrary",)))(sinks, q, kv, kv, do)


def kv_combine(dprev, dcur):
    nb, _, kvw = dprev.shape

    def body(dcur_ref, dnext_ref, dkv_ref, db_ref):
        m = pl.program_id(0)
        dkv = dcur_ref[0] + jnp.where(m + 1 < nb, dnext_ref[0], 0.0)
        dkv_ref[...] = dkv.astype(BF16)

        @pl.when(m == 0)
        def _():
            db_ref[...] = jnp.zeros_like(db_ref)

        db_ref[...] += jnp.sum(dkv, axis=0, keepdims=True)

    return pl.pallas_call(
        body, grid=(nb,),
        in_specs=[pl.BlockSpec((1, BLOCK, kvw), lambda m: (m, 0, 0)),
                  pl.BlockSpec((1, BLOCK, kvw), lambda m: (jnp.minimum(m + 1, nb - 1), 0, 0))],
        out_specs=[pl.BlockSpec((BLOCK, kvw), lambda m: (m, 0)), pl.BlockSpec((1, kvw), lambda m: (0, 0))],
        out_shape=[_sds((nb * BLOCK, kvw), BF16), _sds((1, kvw), F32)],
        name="kv_combine", compiler_params=_params(("arbitrary",)))(dcur, dprev)


def glu_bwd(dout, val, gate, tm=256):
    n_rows, d = dout.shape

    def body(do_ref, v_ref, g_ref, dz_ref, db_ref):
        i = pl.program_id(0)
        sg = jax.nn.sigmoid(g_ref[...])
        dval = do_ref[...] * sg
        dgate = do_ref[...] * v_ref[...] * sg * (1.0 - sg)
        dz = jnp.concatenate([dval, dgate], axis=1)
        dz_ref[...] = dz.astype(BF16)

        @pl.when(i == 0)
        def _():
            db_ref[...] = jnp.zeros_like(db_ref)

        db_ref[...] += jnp.sum(dz, axis=0, keepdims=True)

    row = pl.BlockSpec((tm, d), lambda i: (i, 0))
    return pl.pallas_call(
        body, grid=(n_rows // tm,), in_specs=[row, row, row],
        out_specs=[pl.BlockSpec((tm, 2 * d), lambda i: (i, 0)), pl.BlockSpec((1, 2 * d), lambda i: (0, 0))],
        out_shape=[_sds((n_rows, 2 * d), BF16), _sds((1, 2 * d), F32)],
        name="glu_bwd", compiler_params=_params(("arbitrary",)))(dout, val, gate)


def final_loss(h, target, gain, tm=256):
    n_rows, d = h.shape

    def body(h_ref, t_ref, g_ref, loss_ref, dh_ref, dhb_ref, dg_ref):
        i = pl.program_id(0)
        xh, r = _rms_hat(h_ref[...])
        err = xh * g_ref[...] - t_ref[...]
        dy = err * (1.0 / d)
        dxh = dy * g_ref[...]
        dx = r * (dxh - xh * jnp.mean(dxh * xh, axis=-1, keepdims=True))
        dh_ref[...] = dx
        dhb_ref[...] = dx.astype(BF16)

        @pl.when(i == 0)
        def _():
            loss_ref[...] = jnp.zeros_like(loss_ref)
            dg_ref[...] = jnp.zeros_like(dg_ref)

        loss_ref[...] += jnp.full((8, 128), 0.5 * jnp.sum(jnp.mean(err * err, axis=-1, keepdims=True)), F32)
        dg_ref[...] += jnp.sum(dy * xh, axis=0, keepdims=True)

    row = pl.BlockSpec((tm, d), lambda i: (i, 0))
    vec = pl.BlockSpec((1, d), lambda i: (0, 0))
    return pl.pallas_call(
        body, grid=(n_rows // tm,), in_specs=[row, row, vec],
        out_specs=[pl.BlockSpec((8, 128), lambda i: (0, 0)), row, row, vec],
        out_shape=[_sds((8, 128), F32), _sds((n_rows, d), F32), _sds((n_rows, d), BF16), _sds((1, d), F32)],
        name="final_loss", compiler_params=_params(("arbitrary",)))(h, target, gain)


def adamw(name, w, g, m, v, tm=256):
    n_rows, d = w.shape
    tm = tm if n_rows % tm == 0 else n_rows

    def body(w_ref, g_ref, m_ref, v_ref, d_ref, nm_ref, nv_ref):
        gv = g_ref[...]
        nm = ADAM_B1 * m_ref[...] + (1.0 - ADAM_B1) * gv
        nv = ADAM_B2 * v_ref[...] + (1.0 - ADAM_B2) * (gv * gv)
        m_hat = nm / (1.0 - ADAM_B1 ** ADAM_STEP)
        v_hat = nv / (1.0 - ADAM_B2 ** ADAM_STEP)
        d_ref[...] = -ADAM_LR * (m_hat / (jnp.sqrt(v_hat) + ADAM_EPS) + ADAM_WD * w_ref[...])
        nm_ref[...] = nm
        nv_ref[...] = nv

    row = pl.BlockSpec((tm, d), lambda i: (i, 0))
    return pl.pallas_call(
        body, grid=(n_rows // tm,), in_specs=[row] * 4, out_specs=[row] * 3,
        out_shape=[_sds((n_rows, d), F32)] * 3, name=name, compiler_params=_params(("parallel",)))(w, g, m, v)


def _position():
    x, y, c = lax.axis_index("x"), lax.axis_index("y"), lax.axis_index("c")
    others = [(1 - x, y), (x, 1 - y), (1 - x, 1 - y)]
    return x, y, c, others


def _window(ref, kind, chip, half, shard_shape):
    r, n = shard_shape
    if kind == "col":
        return ref.at[pl.ds(pl.multiple_of(half * (r // 2), 16), r // 2), pl.ds(pl.multiple_of(chip * n, 128), n)]
    return ref.at[pl.ds(pl.multiple_of(chip * r, 16), r), pl.ds(pl.multiple_of(half * (n // 2), 128), n // 2)]


def _half(ref, kind, half, shape):
    r, n = shape
    if kind == "col":
        return ref.at[pl.ds(pl.multiple_of(half * (r // 2), 16), r // 2), :]
    return ref.at[:, pl.ds(pl.multiple_of(half * (n // 2), 128), n // 2)]


def swap_halves(name, grads, kinds):
    nt = len(grads)
    shapes = [tuple(g.shape) for g in grads]

    def body(*refs):
        in_refs, out_refs = refs[:nt], refs[nt:2 * nt]
        send_sems, recv_sems = refs[2 * nt:]
        x, y, c, _ = _position()
        cps = []
        for t in range(nt):
            cp = pltpu.make_async_remote_copy(
                src_ref=_half(in_refs[t], kinds[t], 1 - c, shapes[t]), dst_ref=_half(out_refs[t], kinds[t], 1 - c, shapes[t]),
                send_sem=send_sems.at[t], recv_sem=recv_sems.at[t], device_id=(x, y, 1 - c), device_id_type=MESH)
            cp.start()
            cps.append(cp)
        for t in range(nt):
            mine = _half(out_refs[t], kinds[t], c, shapes[t])
            pltpu.make_async_remote_copy(
                src_ref=mine, dst_ref=mine, send_sem=send_sems.at[t], recv_sem=recv_sems.at[t],
                device_id=(x, y, 1 - c), device_id_type=MESH).wait_recv()
        for cp in cps:
            cp.wait_send()

    hbm = pl.BlockSpec(memory_space=pl.ANY)
    return pl.pallas_call(
        body, in_specs=[hbm] * nt, out_specs=[hbm] * nt, out_shape=[_sds(s, BF16) for s in shapes],
        scratch_shapes=[pltpu.SemaphoreType.DMA((nt,)), pltpu.SemaphoreType.DMA((nt,))],
        name=name, compiler_params=_params())(*grads)


def _half_spec(kind, shape, tiles):
    r, n = shape
    if kind == "col":
        tn = n // tiles
        return pl.BlockSpec((r // 2, tn), lambda i, s: (s[0], i))
    tm = r // tiles
    return pl.BlockSpec((tm, n // 2), lambda i, s: (i, s[0]))


def add_halves(name, mine, landed, kind, where, tiles=4):
    shape = tuple(mine.shape)
    r, n = shape
    out_shape = (r // 2, n) if kind == "col" else (r, n // 2)
    out_spec = (pl.BlockSpec((r // 2, n // tiles), lambda i, s: (0, i)) if kind == "col"
                else pl.BlockSpec((r // tiles, n // 2), lambda i, s: (i, 0)))

    def body(s_ref, a_ref, b_ref, o_ref):
        o_ref[...] = (a_ref[...].astype(F32) + b_ref[...].astype(F32)).astype(BF16)

    spec = _half_spec(kind, shape, tiles)
    return pl.pallas_call(
        body, grid_spec=pltpu.PrefetchScalarGridSpec(num_scalar_prefetch=1, grid=(tiles,), in_specs=[spec, spec],
                                                     out_specs=out_spec),
        out_shape=_sds(out_shape, BF16), name=name, compiler_params=_params(("parallel",)))(where, mine, landed)


def sum_shards(name, part, landed, kind, shard_shape, where, layer, n_layers, into=None, tiles=2):
    r, n = shard_shape
    if kind == "col":
        tm, width = r // 2 // tiles, n
        own = pl.BlockSpec((tm, n), lambda i, s: (i, s[1]))
        out = pl.BlockSpec((None, tm, n), lambda i, s: (layer, s[0] * tiles + i, 0))
    else:
        tm, width = r // tiles, n // 2
        own = pl.BlockSpec((tm, n // 2), lambda i, s: (s[1] * tiles + i, 0))
        out = pl.BlockSpec((None, tm, n // 2), lambda i, s: (layer, i, s[0]))

    def body(s_ref, a_ref, l_ref, *o_refs):
        o_refs[-1][...] = ((a_ref[...].astype(F32) + l_ref[0].astype(F32)) + l_ref[1].astype(F32)) + l_ref[2].astype(F32)

    in_specs = [own, pl.BlockSpec((3, tm, width), lambda i, s: (0, i, 0))]
    args, aliases = [where, part, landed], {}
    if into is not None:
        in_specs.append(pl.BlockSpec(memory_space=pl.ANY))
        args.append(into)
        aliases = {3: 0}
    return pl.pallas_call(
        body, grid_spec=pltpu.PrefetchScalarGridSpec(num_scalar_prefetch=1, grid=(tiles,), in_specs=in_specs, out_specs=out),
        out_shape=_sds((n_layers, r, n), F32), input_output_aliases=aliases, name=name,
        compiler_params=_params(("parallel",)))(*args)


def share_halves(arrays, entries):
    na, nt = len(arrays), len(entries)

    def body(*refs):
        out_refs = refs[na:2 * na]
        send_sems, recv_sems = refs[2 * na:]
        x, y, c, _ = _position()
        cps = []
        for t, (a, layer, kind) in enumerate(entries):
            shape = tuple(arrays[a].shape[1:])
            mine = _half(out_refs[a].at[layer], kind, c, shape)
            cp = pltpu.make_async_remote_copy(
                src_ref=mine, dst_ref=mine, send_sem=send_sems.at[t], recv_sem=recv_sems.at[t],
                device_id=(x, y, 1 - c), device_id_type=MESH)
            cp.start()
            cps.append(cp)
        for t, (a, layer, kind) in enumerate(entries):
            shape = tuple(arrays[a].shape[1:])
            other = _half(out_refs[a].at[layer], kind, 1 - c, shape)
            pltpu.make_async_remote_copy(
                src_ref=other, dst_ref=other, send_sem=send_sems.at[t], recv_sem=recv_sems.at[t],
                device_id=(x, y, 1 - c), device_id_type=MESH).wait_recv()
        for cp in cps:
            cp.wait_send()

    hbm = pl.BlockSpec(memory_space=pl.ANY)
    return pl.pallas_call(
        body, in_specs=[hbm] * na, out_specs=[hbm] * na, out_shape=[_sds(a.shape, F32) for a in arrays],
        input_output_aliases={i: i for i in range(na)},
        scratch_shapes=[pltpu.SemaphoreType.DMA((nt,)), pltpu.SemaphoreType.DMA((nt,))],
        name="share_halves", compiler_params=_params())(*arrays)


HBM_SPEC = pl.BlockSpec(memory_space=pltpu.HBM)
SEM_SPEC = pl.BlockSpec(memory_space=pltpu.SEMAPHORE)
ANY_SPEC = pl.BlockSpec(memory_space=pl.ANY)


def _split_params():
    return pltpu.CompilerParams(has_side_effects=pltpu.SideEffectType.DATAFLOW_SIDE_EFFECTING,
                                vmem_limit_bytes=VMEM_LIMIT_BYTES)


def _in_hbm(a):
    return pltpu.with_memory_space_constraint(a, pltpu.HBM)


def cast_place(name, shard, layer, kind, where, tiles=2):
    _, r, n = shard.shape
    tm = r // tiles
    if kind == "col":
        full, out = (r, 4 * n), pl.BlockSpec((tm, n), lambda i, s: (i, s[1]))
    else:
        full, out = (4 * r, n), pl.BlockSpec((tm, n), lambda i, s: (s[1] * tiles + i, 0))

    def body(s_ref, w_ref, o_ref):
        o_ref[...] = w_ref[...].astype(BF16)

    return pl.pallas_call(
        body, grid_spec=pltpu.PrefetchScalarGridSpec(
            num_scalar_prefetch=1, grid=(tiles,), in_specs=[pl.BlockSpec((None, tm, n), lambda i, s: (layer, i, 0))],
            out_specs=out),
        out_shape=_sds(full, BF16), name=name, compiler_params=_params(("parallel",)))(where, shard)


def gather_start(name, fulls, kinds, shard_shapes, after):
    nt = len(fulls)
    na = 0 if after is None else 1

    def body(*refs):
        full_refs = refs[:nt]
        send_sems, recv_sems, token = refs[nt + na], refs[nt + na + 1], refs[-1]
        x, y, c, others = _position()
        for t in range(nt):
            mine = _window(full_refs[t], kinds[t], 2 * x + y, c, shard_shapes[t])
            for j, (ox, oy) in enumerate(others):
                pltpu.make_async_remote_copy(
                    src_ref=mine, dst_ref=mine, send_sem=send_sems.at[3 * t + j], recv_sem=recv_sems.at[3 * t + j],
                    device_id=(ox, oy, c), device_id_type=MESH).start()
        token[...] = jnp.zeros_like(token)

    sems = pltpu.SemaphoreType.DMA((3 * nt,))
    out = pl.pallas_call(
        body, name=name, in_specs=[HBM_SPEC] * nt + [ANY_SPEC] * na,
        out_specs=(SEM_SPEC, SEM_SPEC, *[HBM_SPEC] * nt, pl.BlockSpec(memory_space=pltpu.VMEM)),
        out_shape=(sems, sems, *[pltpu.HBM(f.shape, f.dtype) for f in fulls], _sds((8, 128), F32)),
        input_output_aliases={t: 2 + t for t in range(nt)}, compiler_params=_split_params(),
    )(*[_in_hbm(f) for f in fulls], *([] if after is None else [after]))
    return out[0], out[1], list(out[2:2 + nt]), out[-1]


def gather_wait(name, send_sems, recv_sems, fulls, kinds, shard_shapes, after):
    nt = len(fulls)

    def body(*refs):
        full_refs, send_ref, recv_ref = refs[:nt], refs[nt], refs[nt + 1]
        x, y, c, others = _position()
        for t in range(nt):
            mine = _window(full_refs[t], kinds[t], 2 * x + y, c, shard_shapes[t])
            for j, (ox, oy) in enumerate(others):
                cp = pltpu.make_async_remote_copy(
                    src_ref=mine, dst_ref=_window(full_refs[t], kinds[t], 2 * ox + oy, c, shard_shapes[t]),
                    send_sem=send_ref.at[3 * t + j], recv_sem=recv_ref.at[3 * t + j],
                    device_id=(ox, oy, c), device_id_type=MESH)
                cp.wait_send()
                cp.wait_recv()

    out = pl.pallas_call(
        body, name=name, in_specs=[HBM_SPEC] * nt + [SEM_SPEC, SEM_SPEC, HBM_SPEC], out_specs=[HBM_SPEC] * nt,
        out_shape=[pltpu.HBM(f.shape, f.dtype) for f in fulls], input_output_aliases={t: t for t in range(nt)},
        compiler_params=_split_params())(*fulls, send_sems, recv_sems, _in_hbm(after))
    return list(out)


def forward_halves(name, fulls, kinds, shard_shapes):
    nt = len(fulls)

    def body(*refs):
        out_refs = refs[nt:2 * nt]
        send_sems, recv_sems = refs[2 * nt:]
        x, y, c, others = _position()
        cps = []
        for t in range(nt):
            for j, (ox, oy) in enumerate(others):
                landed = _window(out_refs[t], kinds[t], 2 * ox + oy, c, shard_shapes[t])
                cp = pltpu.make_async_remote_copy(
                    src_ref=landed, dst_ref=landed, send_sem=send_sems.at[3 * t + j], recv_sem=recv_sems.at[3 * t + j],
                    device_id=(x, y, 1 - c), device_id_type=MESH)
                cp.start()
                cps.append(cp)
        for t in range(nt):
            for j, (ox, oy) in enumerate(others):
                got = _window(out_refs[t], kinds[t], 2 * ox + oy, 1 - c, shard_shapes[t])
                pltpu.make_async_remote_copy(
                    src_ref=got, dst_ref=got, send_sem=send_sems.at[3 * t + j], recv_sem=recv_sems.at[3 * t + j],
                    device_id=(x, y, 1 - c), device_id_type=MESH).wait_recv()
        for cp in cps:
            cp.wait_send()

    out = pl.pallas_call(
        body, in_specs=[ANY_SPEC] * nt, out_specs=[ANY_SPEC] * nt, out_shape=[_sds(f.shape, f.dtype) for f in fulls],
        input_output_aliases={t: t for t in range(nt)},
        scratch_shapes=[pltpu.SemaphoreType.DMA((3 * nt,)), pltpu.SemaphoreType.DMA((3 * nt,))],
        name=name, compiler_params=_params())(*fulls)
    return list(out)


def _piece(ref, kind, chip, shard_shape):
    r, n = shard_shape
    if kind == "col":
        return ref.at[:, pl.ds(pl.multiple_of(chip * n, 128), n)]
    return ref.at[pl.ds(pl.multiple_of(chip * r, 16), r), :]


def _piece_shape(kind, shard_shape):
    r, n = shard_shape
    return (r // 2, n) if kind == "col" else (r, n // 2)


def exchange_start(name, parts, kinds, shard_shapes):
    nt = len(parts)
    lands = [lax.empty((3,) + _piece_shape(kinds[t], shard_shapes[t]), BF16) for t in range(nt)]

    def body(*refs):
        part_refs, land_refs = refs[:nt], refs[nt:2 * nt]
        send_sems, recv_sems, token = refs[2 * nt], refs[2 * nt + 1], refs[-1]
        x, y, c, others = _position()
        for t in range(nt):
            for j, (ox, oy) in enumerate(others):
                pltpu.make_async_remote_copy(
                    src_ref=_piece(part_refs[t], kinds[t], 2 * ox + oy, shard_shapes[t]), dst_ref=land_refs[t].at[j],
                    send_sem=send_sems.at[3 * t + j], recv_sem=recv_sems.at[3 * t + j],
                    device_id=(ox, oy, c), device_id_type=MESH).start()
        token[...] = jnp.zeros_like(token)

    sems = pltpu.SemaphoreType.DMA((3 * nt,))
    both = list(parts) + lands
    out = pl.pallas_call(
        body, name=name, in_specs=[HBM_SPEC] * (2 * nt),
        out_specs=(SEM_SPEC, SEM_SPEC, *[HBM_SPEC] * (2 * nt), pl.BlockSpec(memory_space=pltpu.VMEM)),
        out_shape=(sems, sems, *[pltpu.HBM(a.shape, a.dtype) for a in both], _sds((8, 128), F32)),
        input_output_aliases={t: 2 + t for t in range(2 * nt)}, compiler_params=_split_params(),
    )(*[_in_hbm(a) for a in both])
    return out[0], out[1], list(out[2:2 + nt]), list(out[2 + nt:2 + 2 * nt]), out[-1]


def exchange_wait(name, send_sems, recv_sems, parts, lands, kinds, shard_shapes, after):
    nt = len(parts)

    def body(*refs):
        part_refs, land_refs = refs[:nt], refs[nt:2 * nt]
        send_ref, recv_ref = refs[2 * nt], refs[2 * nt + 1]
        x, y, c, others = _position()
        for t in range(nt):
            for j, (ox, oy) in enumerate(others):
                cp = pltpu.make_async_remote_copy(
                    src_ref=_piece(part_refs[t], kinds[t], 2 * ox + oy, shard_shapes[t]), dst_ref=land_refs[t].at[j],
                    send_sem=send_ref.at[3 * t + j], recv_sem=recv_ref.at[3 * t + j],
                    device_id=(ox, oy, c), device_id_type=MESH)
                cp.wait_send()
                cp.wait_recv()

    both = list(parts) + list(lands)
    out = pl.pallas_call(
        body, name=name, in_specs=[HBM_SPEC] * (2 * nt) + [SEM_SPEC, SEM_SPEC, HBM_SPEC], out_specs=[HBM_SPEC] * (2 * nt),
        out_shape=[pltpu.HBM(a.shape, a.dtype) for a in both], input_output_aliases={t: t for t in range(2 * nt)},
        compiler_params=_split_params())(*both, send_sems, recv_sems, _in_hbm(after))
    return list(out[:nt]), list(out[nt:])


def all_reduce_small(name, buf):
    shape = tuple(buf.shape)

    def body(in_ref, out_ref, land, send_sems, recv_sems):
        x, y, c, _ = _position()
        out_ref[...] = in_ref[...]
        for s, peer in enumerate([(x, y, 1 - c), (1 - x, y, c), (x, 1 - y, c)]):
            cp = pltpu.make_async_remote_copy(
                src_ref=out_ref, dst_ref=land.at[s], send_sem=send_sems.at[s], recv_sem=recv_sems.at[s],
                device_id=peer, device_id_type=MESH)
            cp.start()
            cp.wait()
            out_ref[...] = out_ref[...] + land[s]

    vm = pl.BlockSpec(memory_space=pltpu.VMEM)
    return pl.pallas_call(
        body, in_specs=[vm], out_specs=vm, out_shape=_sds(shape, F32),
        scratch_shapes=[pltpu.VMEM((3,) + shape, F32), pltpu.SemaphoreType.DMA((3,)), pltpu.SemaphoreType.DMA((3,))],
        name=name, compiler_params=_params())(buf)


def _pack(arrays):
    flat = jnp.concatenate([a.reshape(-1).astype(F32) for a in arrays])
    pad = (-flat.shape[0]) % 1024
    return jnp.pad(flat, (0, pad)).reshape(-1, 128)


def _unpack(buf, like):
    flat = buf.reshape(-1)
    out, off = [], 0
    for a in like:
        size = math.prod(a.shape)
        out.append(flat[off:off + size].reshape(a.shape))
        off += size
    return out


def _local_step(x, target, small, need, emit):
    d = D_MODEL
    full = {}

    def after_token(vec, token):
        return vec if token is None else vec + token[0:1, 0:1]

    lam_r, lam_i, bbar_re, bbar_im = small["s5_disc"]
    rb, rc = _s5_matrices(bbar_re, bbar_im, small["s5_c_re"], small["s5_c_im"])
    rb16, rc16 = rb.astype(BF16), rc.astype(BF16)
    lr_t, li_t = lam_r.reshape(S5_BLOCKS, 8, 128), lam_i.reshape(S5_BLOCKS, 8, 128)
    (u,) = rms_fwd("norm_mix0", x, [small["norm_mix0"]], [F32])
    ge, y2, cs = s5_fwd(u, small["s5_d"], rb16, rc16, lr_t, li_t)
    full.update(need("glu", ge))
    h1, val, gate = mm_nn(
        "glu", ge, full["w_glu"], [0, d], d,
        lambda accs, e, r: [e[0] + (accs[0] + r[0]) * jax.nn.sigmoid(accs[1] + r[1]), accs[0] + r[0], accs[1] + r[1]],
        [F32, F32, F32], extras=[x], rowvecs=[(small["s5_b_glu"], 0), (small["s5_b_glu"], d)])

    def mlp_fwd(tag, h, gain, w_in, w_out):
        (n,) = rms_fwd("norm_mlp" + tag, h, [gain], [BF16])
        a, r = mm_nn("mlp_in" + tag, n, w_in, [0], w_in.shape[1],
                     lambda accs, e, rv: [accs[0], jnp.square(jnp.maximum(accs[0], 0.0))], [F32, BF16])
        (h_out,) = mm_nn("mlp_out" + tag, r, w_out, [0], d, lambda accs, e, rv: [e[0] + accs[0]], [F32], extras=[h])
        return h_out, (n, a, r)

    full.update(need("mlp0", h1))
    h2, mlp0 = mlp_fwd("0", h1, small["norm_mlp0"], full["w_in0"], full["w_out0"])

    full.update(need("rest", h2))
    nkv, n2 = rms_fwd("norm_kv_mix1", h2, [small["norm_kv"], small["norm_mix1"]], [BF16, BF16])
    kvw = 2 * N_KV * HEAD_DIM
    (kv,) = mm_nn("kv_proj", nkv, full["w_kv"], [0], kvw, lambda accs, e, r: [accs[0] + r[0]], [BF16],
                  rowvecs=[(small["b_kv"], 0)])
    (q,) = mm_nn("q_proj", n2, full["w_q"], [0], d, lambda accs, e, r: [accs[0] + r[0]], [BF16],
                 rowvecs=[(small["b_q"], 0)])
    sinks = small["sinks"].reshape(N_Q)
    o = attn_fwd(q, kv, sinks)
    (h3,) = mm_nn("o_proj", o, full["w_o"], [0], d, lambda accs, e, r: [e[0] + accs[0] + r[0]], [F32],
                  extras=[h2], rowvecs=[(small["b_o"], 0)])
    h4, mlp1 = mlp_fwd("1", h3, small["norm_mlp1"], full["w_in1"], full["w_out1"])
    loss_tile, dh, dhb, dg_final = final_loss(h4, target, small["norm_final"])

    grads_small, grads_full = {"norm_final": dg_final}, {}
    ident = lambda acc, e: [acc]
    layer1 = ["w_out1", "w_in1", "w_o", "w_q", "w_kv"]
    layer0 = ["w_out0", "w_in0", "w_glu"]

    def mlp_bwd(tag, dh, dhb, h_in, gain, w_in, w_out, saved):
        n, a, r = saved
        grads_full["w_out" + tag] = mm_tn("dw_out" + tag, r, dhb)
        (da,) = mm_nt("mlp_da" + tag, dhb, w_out, lambda acc, e: [acc * 2.0 * jnp.maximum(e[0], 0.0)], [BF16], extras=[a])
        grads_full["w_in" + tag] = mm_tn("dw_in" + tag, n, da)
        (dn,) = mm_nt("mlp_dn" + tag, da, w_in, ident, [F32])
        dx, dxb, colsum, dg = rms_bwd("norm_mlp_bwd" + tag, h_in, [dn], [gain], dh)
        grads_small["norm_mlp" + tag] = dg
        return dx, dxb, colsum

    dh3, dh3b, colsum3 = mlp_bwd("1", dh, dhb, h3, small["norm_mlp1"], full["w_in1"], full["w_out1"], mlp1)
    grads_small["b_o"] = colsum3
    grads_full["w_o"] = mm_tn("dw_o", o, dh3b)
    (do,) = mm_nt("attn_do", dh3b, full["w_o"], ident, [BF16])
    dq, dbq, dprev, dcur, dsink = attn_bwd(q, kv, do, sinks)
    dkv, dbkv = kv_combine(dprev, dcur)
    grads_small["b_q"], grads_small["b_kv"], grads_small["sinks"] = dbq, dbkv, dsink[:, :N_Q]
    grads_full["w_q"] = mm_tn("dw_q", n2, dq)
    grads_full["w_kv"] = mm_tn("dw_kv", nkv, dkv)
    (dn2,) = mm_nt("attn_dn", dq, full["w_q"], ident, [F32])
    (dnkv,) = mm_nt("kv_dn", dkv, full["w_kv"], ident, [F32])
    token = emit("layer1", {n: grads_full[n] for n in layer1})
    dh2, dh2b, _, dg_mix1, dg_kv = rms_bwd("norm_kv_mix1_bwd", h2, [dn2, dnkv],
                                           [after_token(small["norm_mix1"], token), small["norm_kv"]], dh3)
    grads_small["norm_mix1"], grads_small["norm_kv"] = dg_mix1, dg_kv
    dh1, _, _ = mlp_bwd("0", dh2, dh2b, h1, small["norm_mlp0"], full["w_in0"], full["w_out0"], mlp0)

    dz, db_glu = glu_bwd(dh1, val, gate)
    grads_small["s5_b_glu"] = db_glu
    grads_full["w_glu"] = mm_tn("dw_glu", ge, dz)
    token = emit("layer0", {n: grads_full[n] for n in layer0})
    (dy2,) = mm_nt("glu_dy", dz, full["w_glu"], lambda acc, e: [acc * _gelu_grad(e[0])], [F32], extras=[y2])
    rbt16, rct16 = jnp.swapaxes(rb16, 1, 2), jnp.swapaxes(rc16, 1, 2)
    du, dd, drb, drc, dlr, dli = s5_bwd(u, dy2, after_token(small["s5_d"], token), cs, rb16, rbt16, rct16, lr_t, li_t)
    grads_small["s5_d"] = dd
    grads_small["s5_mats"] = (drb, drc, dlr, dli)
    grad_x, _, _, dg_mix0 = rms_bwd("norm_mix0_bwd", x, [du], [small["norm_mix0"]], dh1)
    grads_small["norm_mix0"] = dg_mix0
    return loss_tile, grad_x, grads_small


SMALL_NAMES = ["norm_mix", "norm_mlp", "norm_kv", "norm_final", "s5_a_re", "s5_a_im", "s5_log_dt", "s5_b_re", "s5_b_im",
               "s5_c_re", "s5_c_im", "s5_d", "s5_b_glu", "b_kv", "b_q", "sinks", "b_o"]
BIG_NAMES = ["s5_w_glu", "w_kv", "w_q", "w_o", "w_mlp_in", "w_mlp_out"]
WEIGHT_ORDER = ["norm_mix", "norm_mlp", "norm_kv", "norm_final", "s5_a_re", "s5_a_im", "s5_log_dt", "s5_b_re", "s5_b_im",
                "s5_c_re", "s5_c_im", "s5_d", "s5_w_glu", "s5_b_glu", "w_kv", "b_kv", "w_q", "b_q", "sinks", "w_o", "b_o",
                "w_mlp_in", "w_mlp_out"]


def kernel(x, norm_mix, norm_mlp, norm_kv, norm_final, s5_a_re, s5_a_im, s5_log_dt, s5_b_re, s5_b_im, s5_c_re, s5_c_im, s5_d, s5_w_glu, s5_b_glu, w_kv, b_kv, w_q, b_q, sinks, w_o, b_o, w_mlp_in, w_mlp_out, loss_target, m_norm_mix, m_norm_mlp, m_norm_kv, m_norm_final, m_s5_a_re, m_s5_a_im, m_s5_log_dt, m_s5_b_re, m_s5_b_im, m_s5_c_re, m_s5_c_im, m_s5_d, m_s5_w_glu, m_s5_b_glu, m_w_kv, m_b_kv, m_w_q, m_b_q, m_sinks, m_w_o, m_b_o, m_w_mlp_in, m_w_mlp_out, v_norm_mix, v_norm_mlp, v_norm_kv, v_norm_final, v_s5_a_re, v_s5_a_im, v_s5_log_dt, v_s5_b_re, v_s5_b_im, v_s5_c_re, v_s5_c_im, v_s5_d, v_s5_w_glu, v_s5_b_glu, v_w_kv, v_b_kv, v_w_q, v_b_q, v_sinks, v_w_o, v_b_o, v_w_mlp_in, v_w_mlp_out):
    env = dict(locals())
    w = {n: env[n] for n in WEIGHT_ORDER}
    mom = {n: env["m_" + n] for n in WEIGHT_ORDER}
    var = {n: env["v_" + n] for n in WEIGHT_ORDER}
    d = D_MODEL
    xi, yi, ci = lax.axis_index("x"), lax.axis_index("y"), lax.axis_index("c")
    chip = 2 * xi + yi
    where = jnp.stack([ci, chip]).astype(jnp.int32)

    dsh, bsh = s5_d.shape[1], s5_b_glu.shape[1]
    placed = jnp.concatenate([
        lax.dynamic_update_slice(jnp.zeros((4 * dsh,), F32), s5_d[0], (chip * dsh,)),
        lax.dynamic_update_slice(jnp.zeros((4 * bsh,), F32), s5_b_glu[0], (chip * bsh,))])
    placed = jnp.where(ci == 0, placed, 0.0).reshape(-1, 128)
    gathered = all_reduce_small("gather_vectors", placed).reshape(-1)
    d_full, bglu_full = gathered[:4 * dsh].reshape(1, -1), gathered[4 * dsh:].reshape(1, -1)

    big = [s5_w_glu, w_kv[None], w_q, w_o, w_mlp_in, w_mlp_out]
    entries = [(0, 0, "col"), (1, 0, "row"), (2, 0, "row"), (3, 0, "row"), (4, 0, "col"), (4, 1, "col"),
               (5, 0, "row"), (5, 1, "row")]
    names = ["w_glu", "w_kv", "w_q", "w_o", "w_in0", "w_in1", "w_out0", "w_out1"]
    kinds = dict(zip(names, [k for _, _, k in entries]))
    shard_shapes = dict(zip(names, [tuple(big[a].shape[1:]) for a, _, _ in entries]))

    placed_w = {n: cast_place("cast_" + n, big[a], layer, kind, where) for n, (a, layer, kind) in zip(names, entries)}
    gather_groups = {"glu": ["w_glu"], "mlp0": ["w_in0", "w_out0"], "rest": ["w_kv", "w_q", "w_o", "w_in1", "w_out1"]}
    started, token = {}, None
    for group, members in gather_groups.items():
        send, recv, thru, token = gather_start(
            "gather_start_" + group, [placed_w[n] for n in members], [kinds[n] for n in members],
            [shard_shapes[n] for n in members], token)
        started[group] = (send, recv, thru)

    def need(group, after):
        members = gather_groups[group]
        ks, shapes = [kinds[n] for n in members], [shard_shapes[n] for n in members]
        send, recv, thru = started[group]
        landed = gather_wait("gather_wait_" + group, send, recv, thru, ks, shapes, after)
        return dict(zip(members, forward_halves("forward_halves_" + group, landed, ks, shapes)))

    exchanging = {}

    def emit(group, partial):
        members = list(partial)
        ks, shapes = [kinds[n] for n in members], [shard_shapes[n] for n in members]
        landed = swap_halves("swap_halves_" + group, [partial[n] for n in members], ks)
        sums = [add_halves("add_halves_" + n, partial[n], landed[t], ks[t], where) for t, n in enumerate(members)]
        send, recv, parts, lands, tok = exchange_start("exchange_start_" + group, sums, ks, shapes)
        exchanging[group] = (members, send, recv, parts, lands)
        return tok

    disc = lambda *p: _s5_discretise(p[0], p[1], p[2], p[3], p[4])
    disc_args = (s5_a_re[0], s5_a_im[0], s5_log_dt[0], s5_b_re[0], s5_b_im[0])
    disc_out, disc_vjp = jax.vjp(disc, *disc_args)
    small = {
        "norm_mix0": norm_mix[0:1] + token[0:1, 0:1], "norm_mix1": norm_mix[1:2], "norm_mlp0": norm_mlp[0:1], "norm_mlp1": norm_mlp[1:2],
        "norm_kv": norm_kv.reshape(1, d), "norm_final": norm_final.reshape(1, d), "s5_disc": disc_out,
        "s5_c_re": s5_c_re[0], "s5_c_im": s5_c_im[0], "s5_d": d_full, "s5_b_glu": bglu_full,
        "b_kv": b_kv.reshape(1, -1), "b_q": b_q, "sinks": sinks, "b_o": b_o,
    }
    loss_tile, grad_x, gs = _local_step(x[0], loss_target[0], small, need, emit)

    drb, drc, dlr, dli = gs["s5_mats"]
    dbbar_re, dbbar_im, dc_re, dc_im = _s5_matrix_grads(drb, drc)
    pieces = [loss_tile[0:1, 0:1], gs["norm_mix0"], gs["norm_mix1"], gs["norm_mlp0"], gs["norm_mlp1"], gs["norm_kv"],
              gs["norm_final"], dlr, dli, dbbar_re, dbbar_im, dc_re, dc_im, gs["s5_d"], gs["s5_b_glu"], gs["b_kv"],
              gs["b_q"], gs["sinks"], gs["b_o"]]
    summed_buf = all_reduce_small("reduce_small", _pack(pieces))
    summed = _unpack(summed_buf, pieces)
    (loss, g_mix0, g_mix1, g_mlp0, g_mlp1, g_kv, g_final, dlr, dli, dbbar_re, dbbar_im, dc_re, dc_im, g_d, g_bglu,
     g_bkv, g_bq, g_sinks, g_bo) = summed
    g_are, g_aim, g_dt, g_bre, g_bim = disc_vjp((dlr.reshape(S5_GROUPS, S5_STATE), dli.reshape(S5_GROUPS, S5_STATE),
                                                  dbbar_re, dbbar_im))
    grads = {
        "norm_mix": jnp.concatenate([g_mix0, g_mix1]), "norm_mlp": jnp.concatenate([g_mlp0, g_mlp1]),
        "norm_kv": g_kv.reshape(d), "norm_final": g_final.reshape(d), "s5_a_re": g_are[None], "s5_a_im": g_aim[None],
        "s5_log_dt": g_dt[None], "s5_b_re": g_bre[None], "s5_b_im": g_bim[None], "s5_c_re": dc_re[None],
        "s5_c_im": dc_im[None], "s5_d": lax.dynamic_slice(g_d, (0, chip * dsh), (1, dsh)),
        "s5_b_glu": lax.dynamic_slice(g_bglu, (0, chip * bsh), (1, bsh)), "b_kv": g_bkv.reshape(-1), "b_q": g_bq,
        "sinks": g_sinks, "b_o": g_bo,
    }

    reduced = [None] * len(big)
    where_of = dict(zip(names, entries))
    for group, after in (("layer1", grad_x), ("layer0", summed_buf)):
        members, send, recv, parts, lands = exchanging[group]
        ks, shapes = [kinds[n] for n in members], [shard_shapes[n] for n in members]
        parts, lands = exchange_wait("exchange_wait_" + group, send, recv, parts, lands, ks, shapes, after)
        for t, n in enumerate(members):
            a, layer, kind = where_of[n]
            reduced[a] = sum_shards("sum_shards_" + n, parts[t], lands[t], kind, shapes[t], where, layer,
                                    big[a].shape[0], into=reduced[a])
    reduced = share_halves(reduced, entries)
    for n, g in zip(BIG_NAMES, reduced):
        grads[n] = g.reshape(w[n].shape)

    delta, new_m, new_v = {}, {}, {}
    for n in BIG_NAMES:
        flat = lambda a: a.reshape(-1, a.shape[-1])
        dl, nm, nv = adamw("adamw_" + n, flat(w[n]), flat(grads[n]), flat(mom[n]), flat(var[n]))
        delta[n], new_m[n], new_v[n] = dl.reshape(w[n].shape), nm.reshape(w[n].shape), nv.reshape(w[n].shape)
    sw, sg, sm, sv = ([t[n] for n in SMALL_NAMES] for t in (w, grads, mom, var))
    dl, nm, nv = adamw("adamw_small", _pack(sw), _pack(sg), _pack(sm), _pack(sv))
    for n, a, b, c_ in zip(SMALL_NAMES, _unpack(dl, sw), _unpack(nm, sw), _unpack(nv, sw)):
        delta[n], new_m[n], new_v[n] = a, b, c_

    out = [loss.reshape(()), grad_x[None]]
    for table in (grads, delta, new_m, new_v):
        out += [table[n].reshape(w[n].shape) for n in WEIGHT_ORDER]
    return tuple(out)
```

```python
import functools
import math

import jax
import jax.numpy as jnp
from jax import lax
from jax.experimental import pallas as pl
from jax.experimental.pallas import tpu as pltpu

F32 = jnp.float32
BF16 = jnp.bfloat16

D_MODEL = 1024
S5_GROUPS = 64
S5_GROUP = 16
S5_STATE = 64
N_KV = 4
N_Q = 16
HEAD_DIM = 64
BLOCK = 128
NORM_EPS = 1e-5
LAMBDA_RE_MAX = -1e-4
ADAM_LR, ADAM_B1, ADAM_B2, ADAM_EPS, ADAM_WD, ADAM_STEP = 0.001, 0.9, 0.999, 1e-08, 0.01, 10

VMEM_LIMIT_BYTES = 56 * 1024 * 1024
S5_CHUNK = 256
S5_BLOCKS = 4
MESH = pl.DeviceIdType.MESH


def _params(sem=None):
    return pltpu.CompilerParams(dimension_semantics=sem, vmem_limit_bytes=VMEM_LIMIT_BYTES)


def _sds(shape, dtype):
    return jax.ShapeDtypeStruct(shape, dtype)


def _rms_hat(xv):
    r = lax.rsqrt(jnp.mean(xv * xv, axis=-1, keepdims=True) + NORM_EPS)
    return xv * r, r


def rms_fwd(name, x, gains, out_dtypes, tm=256):
    n_rows, d = x.shape
    ng = len(gains)

    def body(x_ref, *refs):
        xh, _ = _rms_hat(x_ref[...])
        for g_ref, o_ref in zip(refs[:ng], refs[ng:]):
            o_ref[...] = (xh * g_ref[...]).astype(o_ref.dtype)

    row = pl.BlockSpec((tm, d), lambda i: (i, 0))
    vec = pl.BlockSpec((1, d), lambda i: (0, 0))
    return pl.pallas_call(
        body, grid=(n_rows // tm,), in_specs=[row] + [vec] * ng, out_specs=[row] * ng,
        out_shape=[_sds((n_rows, d), dt) for dt in out_dtypes], name=name,
        compiler_params=_params(("parallel",)))(x, *gains)


def rms_bwd(name, x, dys, gains, res, tm=256):
    n_rows, d = x.shape
    ng = len(gains)

    def body(x_ref, res_ref, *refs):
        dy_refs, g_refs = refs[:ng], refs[ng:2 * ng]
        dx_ref, dxb_ref, cs_ref = refs[2 * ng:2 * ng + 3]
        dg_refs = refs[2 * ng + 3:]
        i = pl.program_id(0)
        xh, r = _rms_hat(x_ref[...])
        dxh = jnp.zeros_like(xh)
        dgs = []
        for dy_ref, g_ref in zip(dy_refs, g_refs):
            dy = dy_ref[...].astype(F32)
            dxh = dxh + dy * g_ref[...]
            dgs.append(jnp.sum(dy * xh, axis=0, keepdims=True))
        dx = r * (dxh - xh * jnp.mean(dxh * xh, axis=-1, keepdims=True)) + res_ref[...]
        dx_ref[...] = dx
        dxb_ref[...] = dx.astype(BF16)
        cs = jnp.sum(dx, axis=0, keepdims=True)

        @pl.when(i == 0)
        def _():
            cs_ref[...] = jnp.zeros_like(cs_ref)
            for dg_ref in dg_refs:
                dg_ref[...] = jnp.zeros_like(dg_ref)

        cs_ref[...] += cs
        for dg_ref, dg in zip(dg_refs, dgs):
            dg_ref[...] += dg

    row = pl.BlockSpec((tm, d), lambda i: (i, 0))
    vec = pl.BlockSpec((1, d), lambda i: (0, 0))
    return pl.pallas_call(
        body, grid=(n_rows // tm,), in_specs=[row, row] + [row] * ng + [vec] * ng,
        out_specs=[row, row, vec] + [vec] * ng,
        out_shape=[_sds((n_rows, d), F32), _sds((n_rows, d), BF16), _sds((1, d), F32)] + [_sds((1, d), F32)] * ng,
        name=name, compiler_params=_params(("arbitrary",)))(x, res, *dys, *gains)


def mm_nn(name, a, w, col_offsets, n_out, epilogue, out_dtypes, extras=(), rowvecs=(), tm=1024, tn=512):
    m, k = a.shape
    tm, tn = min(tm, m), min(tn, n_out)
    nw, ne, nr = len(col_offsets), len(extras), len(rowvecs)

    def body(a_ref, *refs):
        w_refs, e_refs, r_refs = refs[:nw], refs[nw:nw + ne], refs[nw + ne:nw + ne + nr]
        o_refs = refs[nw + ne + nr:]
        av = a_ref[...]
        accs = [jnp.dot(av, w_ref[...], preferred_element_type=F32) for w_ref in w_refs]
        outs = epilogue(accs, [e[...] for e in e_refs], [r[...] for r in r_refs])
        for o_ref, o in zip(o_refs, outs):
            o_ref[...] = o.astype(o_ref.dtype)

    def wspec(off):
        return pl.BlockSpec((k, tn), lambda j, i, off=off: (0, off // tn + j))

    def rspec(off):
        return pl.BlockSpec((1, tn), lambda j, i, off=off: (0, off // tn + j))

    tile = pl.BlockSpec((tm, tn), lambda j, i: (i, j))
    in_specs = ([pl.BlockSpec((tm, k), lambda j, i: (i, 0))] + [wspec(o) for o in col_offsets]
                + [tile] * ne + [rspec(o) for _, o in rowvecs])
    return pl.pallas_call(
        body, grid=(n_out // tn, m // tm), in_specs=in_specs, out_specs=[tile] * len(out_dtypes),
        out_shape=[_sds((m, n_out), dt) for dt in out_dtypes], name=name,
        compiler_params=_params(("parallel", "parallel")))(a, *([w] * nw), *extras, *[r for r, _ in rowvecs])


def mm_nt(name, g, w, epilogue, out_dtypes, extras=(), tm=512, tk=512):
    m, n = g.shape
    k = w.shape[0]
    tm, tk = min(tm, m), min(tk, k)
    ne = len(extras)

    def body(g_ref, w_ref, *refs):
        e_refs, o_refs = refs[:ne], refs[ne:]
        acc = lax.dot_general(g_ref[...], w_ref[...], (((1,), (1,)), ((), ())), preferred_element_type=F32)
        outs = epilogue(acc, [e[...] for e in e_refs])
        for o_ref, o in zip(o_refs, outs):
            o_ref[...] = o.astype(o_ref.dtype)

    tile = pl.BlockSpec((tm, tk), lambda i, j: (i, j))
    return pl.pallas_call(
        body, grid=(m // tm, k // tk),
        in_specs=[pl.BlockSpec((tm, n), lambda i, j: (i, 0)), pl.BlockSpec((tk, n), lambda i, j: (j, 0))] + [tile] * ne,
        out_specs=[tile] * len(out_dtypes), out_shape=[_sds((m, k), dt) for dt in out_dtypes], name=name,
        compiler_params=_params(("parallel", "parallel")))(g, w, *extras)


def mm_tn(name, a, g, tk=512, tn=512):
    m, k = a.shape
    n = g.shape[1]
    tk, tn = min(tk, k), min(tn, n)

    def body(a_ref, g_ref, o_ref):
        acc = lax.dot_general(a_ref[...], g_ref[...], (((0,), (0,)), ((), ())), preferred_element_type=F32)
        o_ref[...] = acc.astype(o_ref.dtype)

    return pl.pallas_call(
        body, grid=(k // tk, n // tn),
        in_specs=[pl.BlockSpec((m, tk), lambda i, j: (0, i)), pl.BlockSpec((m, tn), lambda i, j: (0, j))],
        out_specs=pl.BlockSpec((tk, tn), lambda i, j: (i, j)), out_shape=_sds((k, n), BF16), name=name,
        compiler_params=_params(("parallel", "parallel")))(a, g)


def _masked_rows(dst_ref, val, tc):
    for half in range(2):
        v = val[:, half * 128:(half + 1) * 128]
        col = lax.broadcasted_iota(jnp.int32, v.shape, 1) // 32 + 4 * half
        for s8 in range(8):
            rows = jnp.where(col == s8, v, 0.0) if s8 // 4 == half else jnp.zeros_like(v)
            dst_ref.at[half][pl.ds(s8, tc, stride=8), :] = rows


def _staged(ref):
    return jnp.concatenate([ref[0], ref[1]], axis=1)


def _stage(ref, val):
    ref[0] = val[:, 0:128]
    ref[1] = val[:, 128:256]


def _gather_rows(src_ref, tc):
    halves = []
    for half in range(2):
        col = lax.broadcasted_iota(jnp.int32, (tc, 128), 1) // 32 + 4 * half
        out = jnp.zeros((tc, 128), F32)
        for s8 in range(4 * half, 4 * half + 4):
            out = jnp.where(col == s8, src_ref.at[half][pl.ds(s8, tc, stride=8), :], out)
        halves.append(out)
    return jnp.concatenate(halves, axis=1)


def _gelu(x):
    c = math.sqrt(2.0 / math.pi)
    return 0.5 * x * (1.0 + jnp.tanh(c * (x + 0.044715 * x * x * x)))


def _gelu_grad(x):
    c = math.sqrt(2.0 / math.pi)
    t = jnp.tanh(c * (x + 0.044715 * x * x * x))
    return 0.5 * (1.0 + t) + 0.5 * x * (1.0 - t * t) * c * (1.0 + 3.0 * 0.044715 * x * x)


def s5_fwd(u, d_skip, rb, rc, lam_r, lam_i):
    n_rows = u.shape[0]
    tc = min(S5_CHUNK, n_rows)
    nc = n_rows // tc

    def body(u_ref, d_ref, rb_ref, rc_ref, lr_ref, li_ref, ge_ref, y2_ref, cs_ref, lhs, bux, yrows, carry):
        i = pl.program_id(0)

        @pl.when(i == 0)
        def _():
            carry[...] = jnp.zeros_like(carry)

        cs_ref[0] = carry[...]
        for blk in range(S5_BLOCKS):
            _masked_rows(lhs, u_ref[:, blk * 256:(blk + 1) * 256], tc)
            bux[blk] = jnp.dot(_staged(lhs).astype(BF16), rb_ref[blk], preferred_element_type=F32)
        lam = [(lr_ref[blk], li_ref[blk]) for blk in range(S5_BLOCKS)]

        def step(t, c):
            r0 = pl.multiple_of(t * 8, 8)
            new = []
            for blk in range(S5_BLOCKS):
                xr, xi = c[2 * blk], c[2 * blk + 1]
                lr, li = lam[blk]
                nr = lr * xr - li * xi + bux[blk, pl.ds(r0, 8), 0:128]
                ni = lr * xi + li * xr + bux[blk, pl.ds(r0, 8), 128:256]
                bux[blk, pl.ds(r0, 8), 0:128] = nr
                bux[blk, pl.ds(r0, 8), 128:256] = ni
                new += [nr, ni]
            return tuple(new)

        c0 = []
        for blk in range(S5_BLOCKS):
            c0 += [carry[blk, :, 0:128], carry[blk, :, 128:256]]
        cn = lax.fori_loop(0, tc, step, tuple(c0))
        for blk in range(S5_BLOCKS):
            carry[blk, :, 0:128] = cn[2 * blk]
            carry[blk, :, 128:256] = cn[2 * blk + 1]
        for blk in range(S5_BLOCKS):
            _stage(yrows, jnp.dot(bux[blk].astype(BF16), rc_ref[blk], preferred_element_type=F32))
            sl = slice(blk * 256, (blk + 1) * 256)
            y2 = _gather_rows(yrows, tc) + d_ref[:, sl] * u_ref[:, sl]
            y2_ref[:, sl] = y2
            ge_ref[:, sl] = _gelu(y2).astype(BF16)

    row = pl.BlockSpec((tc, D_MODEL), lambda i: (i, 0))
    mat = pl.BlockSpec((S5_BLOCKS, 256, 256), lambda i: (0, 0, 0))
    lamspec = pl.BlockSpec((S5_BLOCKS, 8, 128), lambda i: (0, 0, 0))
    return pl.pallas_call(
        body, grid=(nc,),
        in_specs=[row, pl.BlockSpec((1, D_MODEL), lambda i: (0, 0)), mat, mat, lamspec, lamspec],
        out_specs=[row, row, pl.BlockSpec((1, S5_BLOCKS, 8, 256), lambda i: (i, 0, 0, 0))],
        out_shape=[_sds((n_rows, D_MODEL), BF16), _sds((n_rows, D_MODEL), F32), _sds((nc, S5_BLOCKS, 8, 256), F32)],
        scratch_shapes=[pltpu.VMEM((2, 8 * tc, 128), F32), pltpu.VMEM((S5_BLOCKS, 8 * tc, 256), F32),
                        pltpu.VMEM((2, 8 * tc, 128), F32), pltpu.VMEM((S5_BLOCKS, 8, 256), F32)],
        name="s5_fwd", compiler_params=_params(("arbitrary",)))(u, d_skip, rb, rc, lam_r, lam_i)


def s5_bwd(u, dy2, d_skip, cs, rb, rbt, rct, lam_r, lam_i):
    n_rows = u.shape[0]
    tc = min(S5_CHUNK, n_rows)
    nc = n_rows // tc

    def body(u_ref, dy_ref, d_ref, cs_ref, rb_ref, rbt_ref, rct_ref, lr_ref, li_ref,
             du_ref, dd_ref, drb_ref, drc_ref, dlr_ref, dli_ref, tmp, lhsu, lhsd, xs, adj, acarry):
        i = pl.program_id(0)

        @pl.when(i == 0)
        def _():
            acarry[...] = jnp.zeros_like(acarry)
            dd_ref[...] = jnp.zeros_like(dd_ref)
            drb_ref[...] = jnp.zeros_like(drb_ref)
            drc_ref[...] = jnp.zeros_like(drc_ref)
            dlr_ref[...] = jnp.zeros_like(dlr_ref)
            dli_ref[...] = jnp.zeros_like(dli_ref)

        dd_ref[...] += jnp.sum(dy_ref[...] * u_ref[...], axis=0, keepdims=True)
        for blk in range(S5_BLOCKS):
            sl = slice(blk * 256, (blk + 1) * 256)
            _masked_rows(tmp, u_ref[:, sl], tc)
            lhsu[blk] = _staged(tmp).astype(BF16)
            xs[blk] = jnp.dot(lhsu[blk], rb_ref[blk], preferred_element_type=F32)
            _masked_rows(tmp, dy_ref[:, sl], tc)
            lhsd[blk] = _staged(tmp).astype(BF16)
            adj[blk] = jnp.dot(lhsd[blk], rct_ref[blk], preferred_element_type=F32)
        lam = [(lr_ref[blk], li_ref[blk]) for blk in range(S5_BLOCKS)]

        def fstep(t, c):
            r0 = pl.multiple_of(t * 8, 8)
            new = []
            for blk in range(S5_BLOCKS):
                xr, xi = c[2 * blk], c[2 * blk + 1]
                lr, li = lam[blk]
                nr = lr * xr - li * xi + xs[blk, pl.ds(r0, 8), 0:128]
                ni = lr * xi + li * xr + xs[blk, pl.ds(r0, 8), 128:256]
                xs[blk, pl.ds(r0, 8), 0:128] = nr
                xs[blk, pl.ds(r0, 8), 128:256] = ni
                new += [nr, ni]
            return tuple(new)

        c0 = []
        for blk in range(S5_BLOCKS):
            c0 += [cs_ref[0, blk, :, 0:128], cs_ref[0, blk, :, 128:256]]
        lax.fori_loop(0, tc, fstep, tuple(c0))

        def bstep(k, c):
            t = tc - 1 - k
            r0 = pl.multiple_of(t * 8, 8)
            rp = pl.multiple_of(jnp.maximum(t - 1, 0) * 8, 8)
            first = t == 0
            new_a, new_g = [], []
            for blk in range(S5_BLOCKS):
                ar, ai = c[0][2 * blk], c[0][2 * blk + 1]
                glr, gli = c[1][2 * blk], c[1][2 * blk + 1]
                lr, li = lam[blk]
                nr = lr * ar + li * ai + adj[blk, pl.ds(r0, 8), 0:128]
                ni = lr * ai - li * ar + adj[blk, pl.ds(r0, 8), 128:256]
                adj[blk, pl.ds(r0, 8), 0:128] = nr
                adj[blk, pl.ds(r0, 8), 128:256] = ni
                pr = jnp.where(first, cs_ref[0, blk, :, 0:128], xs[blk, pl.ds(rp, 8), 0:128])
                pi = jnp.where(first, cs_ref[0, blk, :, 128:256], xs[blk, pl.ds(rp, 8), 128:256])
                new_a += [nr, ni]
                new_g += [glr + nr * pr + ni * pi, gli + ni * pr - nr * pi]
            return tuple(new_a), tuple(new_g)

        a0, g0 = [], []
        for blk in range(S5_BLOCKS):
            a0 += [acarry[blk, :, 0:128], acarry[blk, :, 128:256]]
            g0 += [dlr_ref[blk], dli_ref[blk]]
        an, gn = lax.fori_loop(0, tc, bstep, (tuple(a0), tuple(g0)))
        for blk in range(S5_BLOCKS):
            acarry[blk, :, 0:128] = an[2 * blk]
            acarry[blk, :, 128:256] = an[2 * blk + 1]
            dlr_ref[blk] = gn[2 * blk]
            dli_ref[blk] = gn[2 * blk + 1]
        for blk in range(S5_BLOCKS):
            sl = slice(blk * 256, (blk + 1) * 256)
            ab = adj[blk].astype(BF16)
            _stage(tmp, jnp.dot(ab, rbt_ref[blk], preferred_element_type=F32))
            du_ref[:, sl] = _gather_rows(tmp, tc) + d_ref[:, sl] * dy_ref[:, sl]
            drb_ref[blk] += lax.dot_general(lhsu[blk], ab, (((0,), (0,)), ((), ())), preferred_element_type=F32)
            drc_ref[blk] += lax.dot_general(xs[blk].astype(BF16), lhsd[blk], (((0,), (0,)), ((), ())),
                                            preferred_element_type=F32)

    rev = pl.BlockSpec((tc, D_MODEL), lambda i: (nc - 1 - i, 0))
    vec = pl.BlockSpec((1, D_MODEL), lambda i: (0, 0))
    mat = pl.BlockSpec((S5_BLOCKS, 256, 256), lambda i: (0, 0, 0))
    lamspec = pl.BlockSpec((S5_BLOCKS, 8, 128), lambda i: (0, 0, 0))
    big = pltpu.VMEM((S5_BLOCKS, 8 * tc, 256), F32)
    bigb = pltpu.VMEM((S5_BLOCKS, 8 * tc, 256), BF16)
    return pl.pallas_call(
        body, grid=(nc,),
        in_specs=[rev, rev, vec, pl.BlockSpec((1, S5_BLOCKS, 8, 256), lambda i: (nc - 1 - i, 0, 0, 0)),
                  mat, mat, mat, lamspec, lamspec],
        out_specs=[rev, vec, mat, mat, lamspec, lamspec],
        out_shape=[_sds((n_rows, D_MODEL), F32), _sds((1, D_MODEL), F32), _sds((S5_BLOCKS, 256, 256), F32),
                   _sds((S5_BLOCKS, 256, 256), F32), _sds((S5_BLOCKS, 8, 128), F32), _sds((S5_BLOCKS, 8, 128), F32)],
        scratch_shapes=[pltpu.VMEM((2, 8 * tc, 128), F32), bigb, bigb, big, big, pltpu.VMEM((S5_BLOCKS, 8, 256), F32)],
        name="s5_bwd", compiler_params=_params(("arbitrary",)))(u, dy2, d_skip, cs, rb, rbt, rct, lam_r, lam_i)


def _s5_discretise(a_re, a_im, log_dt, b_re, b_im):
    lam = lax.complex(jnp.minimum(a_re, LAMBDA_RE_MAX), a_im)
    dt = jnp.exp(log_dt)[:, None]
    lam_bar = jnp.exp(lam * dt)
    b_bar = ((lam_bar - 1.0) / lam)[:, :, None] * lax.complex(b_re, b_im)
    return jnp.real(lam_bar), jnp.imag(lam_bar), jnp.real(b_bar), jnp.imag(b_bar)


def _s5_matrices(bbar_re, bbar_im, c_re, c_im):
    eye2 = jnp.eye(2, dtype=F32)
    bst = jnp.stack([bbar_re, bbar_im]).reshape(2, S5_BLOCKS, 8, 2, S5_STATE, S5_GROUP)
    bt = jnp.transpose(bst, (1, 2, 3, 5, 0, 4))
    rb = (bt[:, :, :, :, :, None, :] * eye2[None, None, :, None, None, :, None]).reshape(S5_BLOCKS, 256, 256)
    cst = jnp.stack([c_re, -c_im]).reshape(2, S5_BLOCKS, 8, 2, S5_GROUP, S5_STATE)
    ct = jnp.transpose(cst, (1, 0, 5, 2, 3, 4))
    rc = (ct[:, :, None, :, :, :, :] * eye2[None, None, :, None, None, :, None]).reshape(S5_BLOCKS, 256, 256)
    return rb, rc


def _s5_matrix_grads(drb, drc):
    x = drb.reshape(S5_BLOCKS, 8, 2, S5_GROUP, 2, 2, S5_STATE)
    db = jnp.stack([x[:, :, 0, :, :, 0, :], x[:, :, 1, :, :, 1, :]], axis=2)
    db = jnp.transpose(db, (4, 0, 1, 2, 5, 3)).reshape(2, S5_GROUPS, S5_STATE, S5_GROUP)
    y = drc.reshape(S5_BLOCKS, 2, 2, S5_STATE, 8, 2, S5_GROUP)
    dc = jnp.stack([y[:, :, 0, :, :, 0, :], y[:, :, 1, :, :, 1, :]], axis=4)
    dc = jnp.transpose(dc, (1, 0, 3, 4, 5, 2)).reshape(2, S5_GROUPS, S5_GROUP, S5_STATE)
    return db[0], db[1], dc[0], -dc[1]


NEG = -1e30


GROUP = N_Q // N_KV


def _attn_masks(n):
    qi = lax.broadcasted_iota(jnp.int32, (GROUP * BLOCK, BLOCK), 0) % BLOCK
    kj = lax.broadcasted_iota(jnp.int32, (GROUP * BLOCK, BLOCK), 1)
    return jnp.logical_and(kj > qi, n > 0), kj <= qi


def _stack_heads(ref, kh):
    return jnp.concatenate([ref[:, (GROUP * kh + g) * HEAD_DIM:(GROUP * kh + g + 1) * HEAD_DIM] for g in range(GROUP)], axis=0)


def _unstack_heads(val):
    return jnp.concatenate([val[g * BLOCK:(g + 1) * BLOCK] for g in range(GROUP)], axis=1)


def _sink_column(sink_ref, kh):
    grp = lax.broadcasted_iota(jnp.int32, (GROUP * BLOCK, 1), 0) // BLOCK
    col = jnp.zeros((GROUP * BLOCK, 1), F32)
    for g in range(GROUP):
        col = jnp.where(grp == g, sink_ref[GROUP * kh + g], col)
    return col, grp


def _attn_exp(q4, kp, kc, sink, mask_p, mask_c):
    scale = 1.0 / math.sqrt(HEAD_DIM)
    nt = (((1,), (1,)), ((), ()))
    sp = jnp.where(mask_p, lax.dot_general(q4, kp, nt, preferred_element_type=F32) * scale, NEG)
    sc = jnp.where(mask_c, lax.dot_general(q4, kc, nt, preferred_element_type=F32) * scale, NEG)
    m = jnp.maximum(jnp.maximum(jnp.max(sp, axis=-1, keepdims=True), jnp.max(sc, axis=-1, keepdims=True)), sink)
    pp = jnp.exp(sp - m)
    pc = jnp.exp(sc - m)
    ps = jnp.exp(sink - m)
    inv = 1.0 / (jnp.sum(pp, axis=-1, keepdims=True) + jnp.sum(pc, axis=-1, keepdims=True) + ps)
    return pp, pc, ps, inv


def attn_fwd(q, kv, sinks):
    n_rows = q.shape[0]
    nb = n_rows // BLOCK

    def body(sink_ref, q_ref, kvp_ref, kvc_ref, o_ref):
        n = pl.program_id(0)
        mask_p, mask_c = _attn_masks(n)
        outs = []
        for kh in range(N_KV):
            ks, vs = slice(kh * HEAD_DIM, (kh + 1) * HEAD_DIM), slice((N_KV + kh) * HEAD_DIM, (N_KV + kh + 1) * HEAD_DIM)
            sink, _ = _sink_column(sink_ref, kh)
            pp, pc, _, inv = _attn_exp(_stack_heads(q_ref, kh), kvp_ref[:, ks], kvc_ref[:, ks], sink, mask_p, mask_c)
            o4 = (jnp.dot(pp.astype(BF16), kvp_ref[:, vs], preferred_element_type=F32)
                  + jnp.dot(pc.astype(BF16), kvc_ref[:, vs], preferred_element_type=F32)) * inv
            outs.append(_unstack_heads(o4))
        o_ref[...] = jnp.concatenate(outs, axis=1).astype(BF16)

    kvw = 2 * N_KV * HEAD_DIM
    return pl.pallas_call(
        body, grid=(nb,),
        in_specs=[pl.BlockSpec(memory_space=pltpu.SMEM), pl.BlockSpec((BLOCK, D_MODEL), lambda n: (n, 0)),
                  pl.BlockSpec((BLOCK, kvw), lambda n: (jnp.maximum(n - 1, 0), 0)), pl.BlockSpec((BLOCK, kvw), lambda n: (n, 0))],
        out_specs=pl.BlockSpec((BLOCK, D_MODEL), lambda n: (n, 0)), out_shape=_sds((n_rows, D_MODEL), BF16),
        name="attn_fwd", compiler_params=_params(("parallel",)))(sinks, q, kv, kv)


def attn_bwd(q, kv, do, sinks):
    n_rows = q.shape[0]
    nb = n_rows // BLOCK
    kvw = 2 * N_KV * HEAD_DIM
    tn = (((0,), (0,)), ((), ()))
    nt = (((1,), (1,)), ((), ()))
    scale = 1.0 / math.sqrt(HEAD_DIM)

    def body(sink_ref, q_ref, kvp_ref, kvc_ref, do_ref, dq_ref, dbq_ref, dprev_ref, dcur_ref, dsink_ref):
        n = pl.program_id(0)
        mask_p, mask_c = _attn_masks(n)
        lane = lax.broadcasted_iota(jnp.int32, (1, 128), 1)
        dqs, dsink = [], jnp.zeros((1, 128), F32)
        dkp, dkc, dvp, dvc = [], [], [], []
        for kh in range(N_KV):
            ks, vs = slice(kh * HEAD_DIM, (kh + 1) * HEAD_DIM), slice((N_KV + kh) * HEAD_DIM, (N_KV + kh + 1) * HEAD_DIM)
            q4, do4 = _stack_heads(q_ref, kh), _stack_heads(do_ref, kh)
            kp, kc, vp, vc = kvp_ref[:, ks], kvc_ref[:, ks], kvp_ref[:, vs], kvc_ref[:, vs]
            sink, grp = _sink_column(sink_ref, kh)
            pp, pc, ps, inv = _attn_exp(q4, kp, kc, sink, mask_p, mask_c)
            pp, pc = pp * inv, pc * inv
            dpp = lax.dot_general(do4, vp, nt, preferred_element_type=F32)
            dpc = lax.dot_general(do4, vc, nt, preferred_element_type=F32)
            delta = jnp.sum(pp * dpp, axis=-1, keepdims=True) + jnp.sum(pc * dpc, axis=-1, keepdims=True)
            dsp = (pp * (dpp - delta) * scale).astype(BF16)
            dsc = (pc * (dpc - delta) * scale).astype(BF16)
            dsk = ps * inv * delta
            for g in range(GROUP):
                dsink = dsink + jnp.where(lane == GROUP * kh + g, -jnp.sum(jnp.where(grp == g, dsk, 0.0)), 0.0)
            dqs.append(_unstack_heads(jnp.dot(dsp, kp, preferred_element_type=F32)
                                      + jnp.dot(dsc, kc, preferred_element_type=F32)))
            dkp.append(lax.dot_general(dsp, q4, tn, preferred_element_type=F32))
            dkc.append(lax.dot_general(dsc, q4, tn, preferred_element_type=F32))
            dvp.append(lax.dot_general(pp.astype(BF16), do4, tn, preferred_element_type=F32))
            dvc.append(lax.dot_general(pc.astype(BF16), do4, tn, preferred_element_type=F32))
        dq = jnp.concatenate(dqs, axis=1)
        dq_ref[...] = dq.astype(BF16)
        dprev_ref[0] = jnp.concatenate(dkp + dvp, axis=1)
        dcur_ref[0] = jnp.concatenate(dkc + dvc, axis=1)

        @pl.when(n == 0)
        def _():
            dbq_ref[...] = jnp.zeros_like(dbq_ref)
            dsink_ref[...] = jnp.zeros_like(dsink_ref)

        dbq_ref[...] += jnp.sum(dq, axis=0, keepdims=True)
        dsink_ref[...] += dsink

    blk = pl.BlockSpec((BLOCK, D_MODEL), lambda n: (n, 0))
    part = pl.BlockSpec((1, BLOCK, kvw), lambda n: (n, 0, 0))
    return pl.pallas_call(
        body, grid=(nb,),
        in_specs=[pl.BlockSpec(memory_space=pltpu.SMEM), blk,
                  pl.BlockSpec((BLOCK, kvw), lambda n: (jnp.maximum(n - 1, 0), 0)), pl.BlockSpec((BLOCK, kvw), lambda n: (n, 0)), blk],
        out_specs=[blk, pl.BlockSpec((1, D_MODEL), lambda n: (0, 0)), part, part, pl.BlockSpec((1, 128), lambda n: (0, 0))],
        out_shape=[_sds((n_rows, D_MODEL), BF16), _sds((1, D_MODEL), F32), _sds((nb, BLOCK, kvw), F32),
                   _sds((nb, BLOCK, kvw), F32), _sds((1, 128), F32)],
        name="attn_bwd", compiler_params=_params(("arbitrary",)))(sinks, q, kv, kv, do)


def kv_combine(dprev, dcur):
    nb, _, kvw = dprev.shape

    def body(dcur_ref, dnext_ref, dkv_ref, db_ref):
        m = pl.program_id(0)
        dkv = dcur_ref[0] + jnp.where(m + 1 < nb, dnext_ref[0], 0.0)
        dkv_ref[...] = dkv.astype(BF16)

        @pl.when(m == 0)
        def _():
            db_ref[...] = jnp.zeros_like(db_ref)

        db_ref[...] += jnp.sum(dkv, axis=0, keepdims=True)

    return pl.pallas_call(
        body, grid=(nb,),
        in_specs=[pl.BlockSpec((1, BLOCK, kvw), lambda m: (m, 0, 0)),
                  pl.BlockSpec((1, BLOCK, kvw), lambda m: (jnp.minimum(m + 1, nb - 1), 0, 0))],
        out_specs=[pl.BlockSpec((BLOCK, kvw), lambda m: (m, 0)), pl.BlockSpec((1, kvw), lambda m: (0, 0))],
        out_shape=[_sds((nb * BLOCK, kvw), BF16), _sds((1, kvw), F32)],
        name="kv_combine", compiler_params=_params(("arbitrary",)))(dcur, dprev)


def glu_bwd(dout, val, gate, tm=256):
    n_rows, d = dout.shape

    def body(do_ref, v_ref, g_ref, dz_ref, db_ref):
        i = pl.program_id(0)
        sg = jax.nn.sigmoid(g_ref[...])
        dval = do_ref[...] * sg
        dgate = do_ref[...] * v_ref[...] * sg * (1.0 - sg)
        dz = jnp.concatenate([dval, dgate], axis=1)
        dz_ref[...] = dz.astype(BF16)

        @pl.when(i == 0)
        def _():
            db_ref[...] = jnp.zeros_like(db_ref)

        db_ref[...] += jnp.sum(dz, axis=0, keepdims=True)

    row = pl.BlockSpec((tm, d), lambda i: (i, 0))
    return pl.pallas_call(
        body, grid=(n_rows // tm,), in_specs=[row, row, row],
        out_specs=[pl.BlockSpec((tm, 2 * d), lambda i: (i, 0)), pl.BlockSpec((1, 2 * d), lambda i: (0, 0))],
        out_shape=[_sds((n_rows, 2 * d), BF16), _sds((1, 2 * d), F32)],
        name="glu_bwd", compiler_params=_params(("arbitrary",)))(dout, val, gate)


def final_loss(h, target, gain, tm=256):
    n_rows, d = h.shape

    def body(h_ref, t_ref, g_ref, loss_ref, dh_ref, dhb_ref, dg_ref):
        i = pl.program_id(0)
        xh, r = _rms_hat(h_ref[...])
        err = xh * g_ref[...] - t_ref[...]
        dy = err * (1.0 / d)
        dxh = dy * g_ref[...]
        dx = r * (dxh - xh * jnp.mean(dxh * xh, axis=-1, keepdims=True))
        dh_ref[...] = dx
        dhb_ref[...] = dx.astype(BF16)

        @pl.when(i == 0)
        def _():
            loss_ref[...] = jnp.zeros_like(loss_ref)
            dg_ref[...] = jnp.zeros_like(dg_ref)

        loss_ref[...] += jnp.full((8, 128), 0.5 * jnp.sum(jnp.mean(err * err, axis=-1, keepdims=True)), F32)
        dg_ref[...] += jnp.sum(dy * xh, axis=0, keepdims=True)

    row = pl.BlockSpec((tm, d), lambda i: (i, 0))
    vec = pl.BlockSpec((1, d), lambda i: (0, 0))
    return pl.pallas_call(
        body, grid=(n_rows // tm,), in_specs=[row, row, vec],
        out_specs=[pl.BlockSpec((8, 128), lambda i: (0, 0)), row, row, vec],
        out_shape=[_sds((8, 128), F32), _sds((n_rows, d), F32), _sds((n_rows, d), BF16), _sds((1, d), F32)],
        name="final_loss", compiler_params=_params(("arbitrary",)))(h, target, gain)


def adamw(name, w, g, m, v, tm=256):
    n_rows, d = w.shape
    tm = tm if n_rows % tm == 0 else n_rows

    def body(w_ref, g_ref, m_ref, v_ref, d_ref, nm_ref, nv_ref):
        gv = g_ref[...]
        nm = ADAM_B1 * m_ref[...] + (1.0 - ADAM_B1) * gv
        nv = ADAM_B2 * v_ref[...] + (1.0 - ADAM_B2) * (gv * gv)
        m_hat = nm / (1.0 - ADAM_B1 ** ADAM_STEP)
        v_hat = nv / (1.0 - ADAM_B2 ** ADAM_STEP)
        d_ref[...] = -ADAM_LR * (m_hat / (jnp.sqrt(v_hat) + ADAM_EPS) + ADAM_WD * w_ref[...])
        nm_ref[...] = nm
        nv_ref[...] = nv

    row = pl.BlockSpec((tm, d), lambda i: (i, 0))
    return pl.pallas_call(
        body, grid=(n_rows // tm,), in_specs=[row] * 4, out_specs=[row] * 3,
        out_shape=[_sds((n_rows, d), F32)] * 3, name=name, compiler_params=_params(("parallel",)))(w, g, m, v)


def _position():
    x, y, c = lax.axis_index("x"), lax.axis_index("y"), lax.axis_index("c")
    others = [(1 - x, y), (x, 1 - y), (1 - x, 1 - y)]
    return x, y, c, others


def _window(ref, kind, chip, half, shard_shape):
    r, n = shard_shape
    if kind == "col":
        return ref.at[pl.ds(pl.multiple_of(half * (r // 2), 16), r // 2), pl.ds(pl.multiple_of(chip * n, 128), n)]
    return ref.at[pl.ds(pl.multiple_of(chip * r, 16), r), pl.ds(pl.multiple_of(half * (n // 2), 128), n // 2)]


def _half(ref, kind, half, shape):
    r, n = shape
    if kind == "col":
        return ref.at[pl.ds(pl.multiple_of(half * (r // 2), 16), r // 2), :]
    return ref.at[:, pl.ds(pl.multiple_of(half * (n // 2), 128), n // 2)]


def swap_halves(name, grads, kinds):
    nt = len(grads)
    shapes = [tuple(g.shape) for g in grads]

    def body(*refs):
        in_refs, out_refs = refs[:nt], refs[nt:2 * nt]
        send_sems, recv_sems = refs[2 * nt:]
        x, y, c, _ = _position()
        cps = []
        for t in range(nt):
            cp = pltpu.make_async_remote_copy(
                src_ref=_half(in_refs[t], kinds[t], 1 - c, shapes[t]), dst_ref=_half(out_refs[t], kinds[t], 1 - c, shapes[t]),
                send_sem=send_sems.at[t], recv_sem=recv_sems.at[t], device_id=(x, y, 1 - c), device_id_type=MESH)
            cp.start()
            cps.append(cp)
        for t in range(nt):
            mine = _half(out_refs[t], kinds[t], c, shapes[t])
            pltpu.make_async_remote_copy(
                src_ref=mine, dst_ref=mine, send_sem=send_sems.at[t], recv_sem=recv_sems.at[t],
                device_id=(x, y, 1 - c), device_id_type=MESH).wait_recv()
        for cp in cps:
            cp.wait_send()

    hbm = pl.BlockSpec(memory_space=pl.ANY)
    return pl.pallas_call(
        body, in_specs=[hbm] * nt, out_specs=[hbm] * nt, out_shape=[_sds(s, BF16) for s in shapes],
        scratch_shapes=[pltpu.SemaphoreType.DMA((nt,)), pltpu.SemaphoreType.DMA((nt,))],
        name=name, compiler_params=_params())(*grads)


def _half_spec(kind, shape, tiles):
    r, n = shape
    if kind == "col":
        tn = n // tiles
        return pl.BlockSpec((r // 2, tn), lambda i, s: (s[0], i))
    tm = r // tiles
    return pl.BlockSpec((tm, n // 2), lambda i, s: (i, s[0]))


def add_halves(name, mine, landed, kind, where, tiles=4):
    shape = tuple(mine.shape)
    r, n = shape
    out_shape = (r // 2, n) if kind == "col" else (r, n // 2)
    out_spec = (pl.BlockSpec((r // 2, n // tiles), lambda i, s: (0, i)) if kind == "col"
                else pl.BlockSpec((r // tiles, n // 2), lambda i, s: (i, 0)))

    def body(s_ref, a_ref, b_ref, o_ref):
        o_ref[...] = (a_ref[...].astype(F32) + b_ref[...].astype(F32)).astype(BF16)

    spec = _half_spec(kind, shape, tiles)
    return pl.pallas_call(
        body, grid_spec=pltpu.PrefetchScalarGridSpec(num_scalar_prefetch=1, grid=(tiles,), in_specs=[spec, spec],
                                                     out_specs=out_spec),
        out_shape=_sds(out_shape, BF16), name=name, compiler_params=_params(("parallel",)))(where, mine, landed)


def sum_shards(name, part, landed, kind, shard_shape, where, layer, n_layers, into=None, tiles=2):
    r, n = shard_shape
    if kind == "col":
        tm, width = r // 2 // tiles, n
        own = pl.BlockSpec((tm, n), lambda i, s: (i, s[1]))
        out = pl.BlockSpec((None, tm, n), lambda i, s: (layer, s[0] * tiles + i, 0))
    else:
        tm, width = r // tiles, n // 2
        own = pl.BlockSpec((tm, n // 2), lambda i, s: (s[1] * tiles + i, 0))
        out = pl.BlockSpec((None, tm, n // 2), lambda i, s: (layer, i, s[0]))

    def body(s_ref, a_ref, l_ref, *o_refs):
        o_refs[-1][...] = ((a_ref[...].astype(F32) + l_ref[0].astype(F32)) + l_ref[1].astype(F32)) + l_ref[2].astype(F32)

    in_specs = [own, pl.BlockSpec((3, tm, width), lambda i, s: (0, i, 0))]
    args, aliases = [where, part, landed], {}
    if into is not None:
        in_specs.append(pl.BlockSpec(memory_space=pl.ANY))
        args.append(into)
        aliases = {3: 0}
    return pl.pallas_call(
        body, grid_spec=pltpu.PrefetchScalarGridSpec(num_scalar_prefetch=1, grid=(tiles,), in_specs=in_specs, out_specs=out),
        out_shape=_sds((n_layers, r, n), F32), input_output_aliases=aliases, name=name,
        compiler_params=_params(("parallel",)))(*args)


def share_halves(arrays, entries):
    na, nt = len(arrays), len(entries)

    def body(*refs):
        out_refs = refs[na:2 * na]
        send_sems, recv_sems = refs[2 * na:]
        x, y, c, _ = _position()
        cps = []
        for t, (a, layer, kind) in enumerate(entries):
            shape = tuple(arrays[a].shape[1:])
            mine = _half(out_refs[a].at[layer], kind, c, shape)
            cp = pltpu.make_async_remote_copy(
                src_ref=mine, dst_ref=mine, send_sem=send_sems.at[t], recv_sem=recv_sems.at[t],
                device_id=(x, y, 1 - c), device_id_type=MESH)
            cp.start()
            cps.append(cp)
        for t, (a, layer, kind) in enumerate(entries):
            shape = tuple(arrays[a].shape[1:])
            other = _half(out_refs[a].at[layer], kind, 1 - c, shape)
            pltpu.make_async_remote_copy(
                src_ref=other, dst_ref=other, send_sem=send_sems.at[t], recv_sem=recv_sems.at[t],
                device_id=(x, y, 1 - c), device_id_type=MESH).wait_recv()
        for cp in cps:
            cp.wait_send()

    hbm = pl.BlockSpec(memory_space=pl.ANY)
    return pl.pallas_call(
        body, in_specs=[hbm] * na, out_specs=[hbm] * na, out_shape=[_sds(a.shape, F32) for a in arrays],
        input_output_aliases={i: i for i in range(na)},
        scratch_shapes=[pltpu.SemaphoreType.DMA((nt,)), pltpu.SemaphoreType.DMA((nt,))],
        name="share_halves", compiler_params=_params())(*arrays)


HBM_SPEC = pl.BlockSpec(memory_space=pltpu.HBM)
SEM_SPEC = pl.BlockSpec(memory_space=pltpu.SEMAPHORE)
ANY_SPEC = pl.BlockSpec(memory_space=pl.ANY)


def _split_params():
    return pltpu.CompilerParams(has_side_effects=pltpu.SideEffectType.DATAFLOW_SIDE_EFFECTING,
                                vmem_limit_bytes=VMEM_LIMIT_BYTES)


def _in_hbm(a):
    return pltpu.with_memory_space_constraint(a, pltpu.HBM)


def cast_place(name, shard, layer, kind, where, tiles=2):
    _, r, n = shard.shape
    tm = r // tiles
    if kind == "col":
        full, out = (r, 4 * n), pl.BlockSpec((tm, n), lambda i, s: (i, s[1]))
    else:
        full, out = (4 * r, n), pl.BlockSpec((tm, n), lambda i, s: (s[1] * tiles + i, 0))

    def body(s_ref, w_ref, o_ref):
        o_ref[...] = w_ref[...].astype(BF16)

    return pl.pallas_call(
        body, grid_spec=pltpu.PrefetchScalarGridSpec(
            num_scalar_prefetch=1, grid=(tiles,), in_specs=[pl.BlockSpec((None, tm, n), lambda i, s: (layer, i, 0))],
            out_specs=out),
        out_shape=_sds(full, BF16), name=name, compiler_params=_params(("parallel",)))(where, shard)


def gather_start(name, fulls, kinds, shard_shapes, after):
    nt = len(fulls)
    na = 0 if after is None else 1

    def body(*refs):
        full_refs = refs[:nt]
        send_sems, recv_sems, token = refs[nt + na], refs[nt + na + 1], refs[-1]
        x, y, c, others = _position()
        for t in range(nt):
            mine = _window(full_refs[t], kinds[t], 2 * x + y, c, shard_shapes[t])
            for j, (ox, oy) in enumerate(others):
                pltpu.make_async_remote_copy(
                    src_ref=mine, dst_ref=mine, send_sem=send_sems.at[3 * t + j], recv_sem=recv_sems.at[3 * t + j],
                    device_id=(ox, oy, c), device_id_type=MESH).start()
        token[...] = jnp.zeros_like(token)

    sems = pltpu.SemaphoreType.DMA((3 * nt,))
    out = pl.pallas_call(
        body, name=name, in_specs=[HBM_SPEC] * nt + [ANY_SPEC] * na,
        out_specs=(SEM_SPEC, SEM_SPEC, *[HBM_SPEC] * nt, pl.BlockSpec(memory_space=pltpu.VMEM)),
        out_shape=(sems, sems, *[pltpu.HBM(f.shape, f.dtype) for f in fulls], _sds((8, 128), F32)),
        input_output_aliases={t: 2 + t for t in range(nt)}, compiler_params=_split_params(),
    )(*[_in_hbm(f) for f in fulls], *([] if after is None else [after]))
    return out[0], out[1], list(out[2:2 + nt]), out[-1]


def gather_wait(name, send_sems, recv_sems, fulls, kinds, shard_shapes, after):
    nt = len(fulls)

    def body(*refs):
        full_refs, send_ref, recv_ref = refs[:nt], refs[nt], refs[nt + 1]
        x, y, c, others = _position()
        for t in range(nt):
            mine = _window(full_refs[t], kinds[t], 2 * x + y, c, shard_shapes[t])
            for j, (ox, oy) in enumerate(others):
                cp = pltpu.make_async_remote_copy(
                    src_ref=mine, dst_ref=_window(full_refs[t], kinds[t], 2 * ox + oy, c, shard_shapes[t]),
                    send_sem=send_ref.at[3 * t + j], recv_sem=recv_ref.at[3 * t + j],
                    device_id=(ox, oy, c), device_id_type=MESH)
                cp.wait_send()
                cp.wait_recv()

    out = pl.pallas_call(
        body, name=name, in_specs=[HBM_SPEC] * nt + [SEM_SPEC, SEM_SPEC, HBM_SPEC], out_specs=[HBM_SPEC] * nt,
        out_shape=[pltpu.HBM(f.shape, f.dtype) for f in fulls], input_output_aliases={t: t for t in range(nt)},
        compiler_params=_split_params())(*fulls, send_sems, recv_sems, _in_hbm(after))
    return list(out)


def forward_halves(name, fulls, kinds, shard_shapes):
    nt = len(fulls)

    def body(*refs):
        out_refs = refs[nt:2 * nt]
        send_sems, recv_sems = refs[2 * nt:]
        x, y, c, others = _position()
        cps = []
        for t in range(nt):
            for j, (ox, oy) in enumerate(others):
                landed = _window(out_refs[t], kinds[t], 2 * ox + oy, c, shard_shapes[t])
                cp = pltpu.make_async_remote_copy(
                    src_ref=landed, dst_ref=landed, send_sem=send_sems.at[3 * t + j], recv_sem=recv_sems.at[3 * t + j],
                    device_id=(x, y, 1 - c), device_id_type=MESH)
                cp.start()
                cps.append(cp)
        for t in range(nt):
            for j, (ox, oy) in enumerate(others):
                got = _window(out_refs[t], kinds[t], 2 * ox + oy, 1 - c, shard_shapes[t])
                pltpu.make_async_remote_copy(
                    src_ref=got, dst_ref=got, send_sem=send_sems.at[3 * t + j], recv_sem=recv_sems.at[3 * t + j],
                    device_id=(x, y, 1 - c), device_id_type=MESH).wait_recv()
        for cp in cps:
            cp.wait_send()

    out = pl.pallas_call(
        body, in_specs=[ANY_SPEC] * nt, out_specs=[ANY_SPEC] * nt, out_shape=[_sds(f.shape, f.dtype) for f in fulls],
        input_output_aliases={t: t for t in range(nt)},
        scratch_shapes=[pltpu.SemaphoreType.DMA((3 * nt,)), pltpu.SemaphoreType.DMA((3 * nt,))],
        name=name, compiler_params=_params())(*fulls)
    return list(out)


def _piece(ref, kind, chip, shard_shape):
    r, n = shard_shape
    if kind == "col":
        return ref.at[:, pl.ds(pl.multiple_of(chip * n, 128), n)]
    return ref.at[pl.ds(pl.multiple_of(chip * r, 16), r), :]


def _piece_shape(kind, shard_shape):
    r, n = shard_shape
    return (r // 2, n) if kind == "col" else (r, n // 2)


def exchange_start(name, parts, kinds, shard_shapes):
    nt = len(parts)
    lands = [lax.empty((3,) + _piece_shape(kinds[t], shard_shapes[t]), BF16) for t in range(nt)]

    def body(*refs):
        part_refs, land_refs = refs[:nt], refs[nt:2 * nt]
        send_sems, recv_sems, token = refs[2 * nt], refs[2 * nt + 1], refs[-1]
        x, y, c, others = _position()
        for t in range(nt):
            for j, (ox, oy) in enumerate(others):
                pltpu.make_async_remote_copy(
                    src_ref=_piece(part_refs[t], kinds[t], 2 * ox + oy, shard_shapes[t]), dst_ref=land_refs[t].at[j],
                    send_sem=send_sems.at[3 * t + j], recv_sem=recv_sems.at[3 * t + j],
                    device_id=(ox, oy, c), device_id_type=MESH).start()
        token[...] = jnp.zeros_like(token)

    sems = pltpu.SemaphoreType.DMA((3 * nt,))
    both = list(parts) + lands
    out = pl.pallas_call(
        body, name=name, in_specs=[HBM_SPEC] * (2 * nt),
        out_specs=(SEM_SPEC, SEM_SPEC, *[HBM_SPEC] * (2 * nt), pl.BlockSpec(memory_space=pltpu.VMEM)),
        out_shape=(sems, sems, *[pltpu.HBM(a.shape, a.dtype) for a in both], _sds((8, 128), F32)),
        input_output_aliases={t: 2 + t for t in range(2 * nt)}, compiler_params=_split_params(),
    )(*[_in_hbm(a) for a in both])
    return out[0], out[1], list(out[2:2 + nt]), list(out[2 + nt:2 + 2 * nt]), out[-1]


def exchange_wait(name, send_sems, recv_sems, parts, lands, kinds, shard_shapes, after):
    nt = len(parts)

    def body(*refs):
        part_refs, land_refs = refs[:nt], refs[nt:2 * nt]
        send_ref, recv_ref = refs[2 * nt], refs[2 * nt + 1]
        x, y, c, others = _position()
        for t in range(nt):
            for j, (ox, oy) in enumerate(others):
                cp = pltpu.make_async_remote_copy(
                    src_ref=_piece(part_refs[t], kinds[t], 2 * ox + oy, shard_shapes[t]), dst_ref=land_refs[t].at[j],
                    send_sem=send_ref.at[3 * t + j], recv_sem=recv_ref.at[3 * t + j],
                    device_id=(ox, oy, c), device_id_type=MESH)
                cp.wait_send()
                cp.wait_recv()

    both = list(parts) + list(lands)
    out = pl.pallas_call(
        body, name=name, in_specs=[HBM_SPEC] * (2 * nt) + [SEM_SPEC, SEM_SPEC, HBM_SPEC], out_specs=[HBM_SPEC] * (2 * nt),
        out_shape=[pltpu.HBM(a.shape, a.dtype) for a in both], input_output_aliases={t: t for t in range(2 * nt)},
        compiler_params=_split_params())(*both, send_sems, recv_sems, _in_hbm(after))
    return list(out[:nt]), list(out[nt:])


def all_reduce_small(name, buf):
    shape = tuple(buf.shape)

    def body(in_ref, out_ref, land, send_sems, recv_sems):
        x, y, c, _ = _position()
        out_ref[...] = in_ref[...]
        for s, peer in enumerate([(x, y, 1 - c), (1 - x, y, c), (x, 1 - y, c)]):
            cp = pltpu.make_async_remote_copy(
                src_ref=out_ref, dst_ref=land.at[s], send_sem=send_sems.at[s], recv_sem=recv_sems.at[s],
                device_id=peer, device_id_type=MESH)
            cp.start()
            cp.wait()
            out_ref[...] = out_ref[...] + land[s]

    vm = pl.BlockSpec(memory_space=pltpu.VMEM)
    return pl.pallas_call(
        body, in_specs=[vm], out_specs=vm, out_shape=_sds(shape, F32),
        scratch_shapes=[pltpu.VMEM((3,) + shape, F32), pltpu.SemaphoreType.DMA((3,)), pltpu.SemaphoreType.DMA((3,))],
        name=name, compiler_params=_params())(buf)


def _pack(arrays):
    flat = jnp.concatenate([a.reshape(-1).astype(F32) for a in arrays])
    pad = (-flat.shape[0]) % 1024
    return jnp.pad(flat, (0, pad)).reshape(-1, 128)


def _unpack(buf, like):
    flat = buf.reshape(-1)
    out, off = [], 0
    for a in like:
        size = math.prod(a.shape)
        out.append(flat[off:off + size].reshape(a.shape))
        off += size
    return out


def _local_step(x, target, small, need, emit):
    d = D_MODEL
    full = {}

    def after_token(vec, token):
        return vec if token is None else vec + token[0:1, 0:1]

    lam_r, lam_i, bbar_re, bbar_im = small["s5_disc"]
    rb, rc = _s5_matrices(bbar_re, bbar_im, small["s5_c_re"], small["s5_c_im"])
    rb16, rc16 = rb.astype(BF16), rc.astype(BF16)
    lr_t, li_t = lam_r.reshape(S5_BLOCKS, 8, 128), lam_i.reshape(S5_BLOCKS, 8, 128)
    (u,) = rms_fwd("norm_mix0", x, [small["norm_mix0"]], [F32])
    ge, y2, cs = s5_fwd(u, small["s5_d"], rb16, rc16, lr_t, li_t)
    full.update(need("glu", ge))
    h1, val, gate = mm_nn(
        "glu", ge, full["w_glu"], [0, d], d,
        lambda accs, e, r: [e[0] + (accs[0] + r[0]) * jax.nn.sigmoid(accs[1] + r[1]), accs[0] + r[0], accs[1] + r[1]],
        [F32, F32, F32], extras=[x], rowvecs=[(small["s5_b_glu"], 0), (small["s5_b_glu"], d)])

    def mlp_fwd(tag, h, gain, w_in, w_out):
        (n,) = rms_fwd("norm_mlp" + tag, h, [gain], [BF16])
        a, r = mm_nn("mlp_in" + tag, n, w_in, [0], w_in.shape[1],
                     lambda accs, e, rv: [accs[0], jnp.square(jnp.maximum(accs[0], 0.0))], [F32, BF16], tm=2048)
        (h_out,) = mm_nn("mlp_out" + tag, r, w_out, [0], d, lambda accs, e, rv: [e[0] + accs[0]], [F32], extras=[h])
        return h_out, (n, a, r)

    full.update(need("mlp0", h1))
    h2, mlp0 = mlp_fwd("0", h1, small["norm_mlp0"], full["w_in0"], full["w_out0"])

    full.update(need("rest", h2))
    nkv, n2 = rms_fwd("norm_kv_mix1", h2, [small["norm_kv"], small["norm_mix1"]], [BF16, BF16])
    kvw = 2 * N_KV * HEAD_DIM
    (kv,) = mm_nn("kv_proj", nkv, full["w_kv"], [0], kvw, lambda accs, e, r: [accs[0] + r[0]], [BF16],
                  rowvecs=[(small["b_kv"], 0)])
    (q,) = mm_nn("q_proj", n2, full["w_q"], [0], d, lambda accs, e, r: [accs[0] + r[0]], [BF16],
                 rowvecs=[(small["b_q"], 0)])
    sinks = small["sinks"].reshape(N_Q)
    o = attn_fwd(q, kv, sinks)
    (h3,) = mm_nn("o_proj", o, full["w_o"], [0], d, lambda accs, e, r: [e[0] + accs[0] + r[0]], [F32],
                  extras=[h2], rowvecs=[(small["b_o"], 0)])
    h4, mlp1 = mlp_fwd("1", h3, small["norm_mlp1"], full["w_in1"], full["w_out1"])
    loss_tile, dh, dhb, dg_final = final_loss(h4, target, small["norm_final"])

    grads_small, grads_full = {"norm_final": dg_final}, {}
    ident = lambda acc, e: [acc]
    layer1 = ["w_out1", "w_in1", "w_o", "w_q", "w_kv"]
    layer0 = ["w_out0", "w_in0", "w_glu"]

    def mlp_bwd(tag, dh, dhb, h_in, gain, w_in, w_out, saved):
        n, a, r = saved
        grads_full["w_out" + tag] = mm_tn("dw_out" + tag, r, dhb, tn=1024)
        (da,) = mm_nt("mlp_da" + tag, dhb, w_out, lambda acc, e: [acc * 2.0 * jnp.maximum(e[0], 0.0)], [BF16], extras=[a],
                      tm=2048)
        grads_full["w_in" + tag] = mm_tn("dw_in" + tag, n, da, tn=1024)
        (dn,) = mm_nt("mlp_dn" + tag, da, w_in, ident, [F32])
        dx, dxb, colsum, dg = rms_bwd("norm_mlp_bwd" + tag, h_in, [dn], [gain], dh)
        grads_small["norm_mlp" + tag] = dg
        return dx, dxb, colsum

    dh3, dh3b, colsum3 = mlp_bwd("1", dh, dhb, h3, small["norm_mlp1"], full["w_in1"], full["w_out1"], mlp1)
    grads_small["b_o"] = colsum3
    grads_full["w_o"] = mm_tn("dw_o", o, dh3b)
    (do,) = mm_nt("attn_do", dh3b, full["w_o"], ident, [BF16])
    dq, dbq, dprev, dcur, dsink = attn_bwd(q, kv, do, sinks)
    dkv, dbkv = kv_combine(dprev, dcur)
    grads_small["b_q"], grads_small["b_kv"], grads_small["sinks"] = dbq, dbkv, dsink[:, :N_Q]
    grads_full["w_q"] = mm_tn("dw_q", n2, dq)
    grads_full["w_kv"] = mm_tn("dw_kv", nkv, dkv)
    (dn2,) = mm_nt("attn_dn", dq, full["w_q"], ident, [F32])
    (dnkv,) = mm_nt("kv_dn", dkv, full["w_kv"], ident, [F32])
    token = emit("layer1", {n: grads_full[n] for n in layer1})
    dh2, dh2b, _, dg_mix1, dg_kv = rms_bwd("norm_kv_mix1_bwd", h2, [dn2, dnkv],
                                           [after_token(small["norm_mix1"], token), small["norm_kv"]], dh3)
    grads_small["norm_mix1"], grads_small["norm_kv"] = dg_mix1, dg_kv
    dh1, _, _ = mlp_bwd("0", dh2, dh2b, h1, small["norm_mlp0"], full["w_in0"], full["w_out0"], mlp0)

    dz, db_glu = glu_bwd(dh1, val, gate)
    grads_small["s5_b_glu"] = db_glu
    grads_full["w_glu"] = mm_tn("dw_glu", ge, dz, tn=1024)
    token = emit("layer0", {n: grads_full[n] for n in layer0})
    (dy2,) = mm_nt("glu_dy", dz, full["w_glu"], lambda acc, e: [acc * _gelu_grad(e[0])], [F32], extras=[y2])
    rbt16, rct16 = jnp.swapaxes(rb16, 1, 2), jnp.swapaxes(rc16, 1, 2)
    du, dd, drb, drc, dlr, dli = s5_bwd(u, dy2, after_token(small["s5_d"], token), cs, rb16, rbt16, rct16, lr_t, li_t)
    grads_small["s5_d"] = dd
    grads_small["s5_mats"] = (drb, drc, dlr, dli)
    grad_x, _, _, dg_mix0 = rms_bwd("norm_mix0_bwd", x, [du], [small["norm_mix0"]], dh1)
    grads_small["norm_mix0"] = dg_mix0
    return loss_tile, grad_x, grads_small


SMALL_NAMES = ["norm_mix", "norm_mlp", "norm_kv", "norm_final", "s5_a_re", "s5_a_im", "s5_log_dt", "s5_b_re", "s5_b_im",
               "s5_c_re", "s5_c_im", "s5_d", "s5_b_glu", "b_kv", "b_q", "sinks", "b_o"]
BIG_NAMES = ["s5_w_glu", "w_kv", "w_q", "w_o", "w_mlp_in", "w_mlp_out"]
WEIGHT_ORDER = ["norm_mix", "norm_mlp", "norm_kv", "norm_final", "s5_a_re", "s5_a_im", "s5_log_dt", "s5_b_re", "s5_b_im",
                "s5_c_re", "s5_c_im", "s5_d", "s5_w_glu", "s5_b_glu", "w_kv", "b_kv", "w_q", "b_q", "sinks", "w_o", "b_o",
                "w_mlp_in", "w_mlp_out"]


def kernel(x, norm_mix, norm_mlp, norm_kv, norm_final, s5_a_re, s5_a_im, s5_log_dt, s5_b_re, s5_b_im, s5_c_re, s5_c_im, s5_d, s5_w_glu, s5_b_glu, w_kv, b_kv, w_q, b_q, sinks, w_o, b_o, w_mlp_in, w_mlp_out, loss_target, m_norm_mix, m_norm_mlp, m_norm_kv, m_norm_final, m_s5_a_re, m_s5_a_im, m_s5_log_dt, m_s5_b_re, m_s5_b_im, m_s5_c_re, m_s5_c_im, m_s5_d, m_s5_w_glu, m_s5_b_glu, m_w_kv, m_b_kv, m_w_q, m_b_q, m_sinks, m_w_o, m_b_o, m_w_mlp_in, m_w_mlp_out, v_norm_mix, v_norm_mlp, v_norm_kv, v_norm_final, v_s5_a_re, v_s5_a_im, v_s5_log_dt, v_s5_b_re, v_s5_b_im, v_s5_c_re, v_s5_c_im, v_s5_d, v_s5_w_glu, v_s5_b_glu, v_w_kv, v_b_kv, v_w_q, v_b_q, v_sinks, v_w_o, v_b_o, v_w_mlp_in, v_w_mlp_out):
    env = dict(locals())
    w = {n: env[n] for n in WEIGHT_ORDER}
    mom = {n: env["m_" + n] for n in WEIGHT_ORDER}
    var = {n: env["v_" + n] for n in WEIGHT_ORDER}
    d = D_MODEL
    xi, yi, ci = lax.axis_index("x"), lax.axis_index("y"), lax.axis_index("c")
    chip = 2 * xi + yi
    where = jnp.stack([ci, chip]).astype(jnp.int32)

    dsh, bsh = s5_d.shape[1], s5_b_glu.shape[1]
    placed = jnp.concatenate([
        lax.dynamic_update_slice(jnp.zeros((4 * dsh,), F32), s5_d[0], (chip * dsh,)),
        lax.dynamic_update_slice(jnp.zeros((4 * bsh,), F32), s5_b_glu[0], (chip * bsh,))])
    placed = jnp.where(ci == 0, placed, 0.0).reshape(-1, 128)
    gathered_rows = all_reduce_small("gather_vectors", placed)
    gathered = gathered_rows.reshape(-1)
    d_full, bglu_full = gathered[:4 * dsh].reshape(1, -1), gathered[4 * dsh:].reshape(1, -1)

    big = [s5_w_glu, w_kv[None], w_q, w_o, w_mlp_in, w_mlp_out]
    entries = [(0, 0, "col"), (1, 0, "row"), (2, 0, "row"), (3, 0, "row"), (4, 0, "col"), (4, 1, "col"),
               (5, 0, "row"), (5, 1, "row")]
    names = ["w_glu", "w_kv", "w_q", "w_o", "w_in0", "w_in1", "w_out0", "w_out1"]
    kinds = dict(zip(names, [k for _, _, k in entries]))
    shard_shapes = dict(zip(names, [tuple(big[a].shape[1:]) for a, _, _ in entries]))

    placed_w = {n: cast_place("cast_" + n, big[a], layer, kind, where) for n, (a, layer, kind) in zip(names, entries)}
    gather_groups = {"glu": ["w_glu"], "mlp0": ["w_in0", "w_out0"], "rest": ["w_kv", "w_q", "w_o", "w_in1", "w_out1"]}
    started, token = {}, gathered_rows
    for group, members in gather_groups.items():
        send, recv, thru, token = gather_start(
            "gather_start_" + group, [placed_w[n] for n in members], [kinds[n] for n in members],
            [shard_shapes[n] for n in members], token)
        started[group] = (send, recv, thru)

    def need(group, after):
        members = gather_groups[group]
        ks, shapes = [kinds[n] for n in members], [shard_shapes[n] for n in members]
        send, recv, thru = started[group]
        landed = gather_wait("gather_wait_" + group, send, recv, thru, ks, shapes, after)
        return dict(zip(members, forward_halves("forward_halves_" + group, landed, ks, shapes)))

    exchanging = {}

    def emit(group, partial):
        members = list(partial)
        ks, shapes = [kinds[n] for n in members], [shard_shapes[n] for n in members]
        landed = swap_halves("swap_halves_" + group, [partial[n] for n in members], ks)
        sums = [add_halves("add_halves_" + n, partial[n], landed[t], ks[t], where) for t, n in enumerate(members)]
        send, recv, parts, lands, tok = exchange_start("exchange_start_" + group, sums, ks, shapes)
        exchanging[group] = (members, send, recv, parts, lands)
        return tok

    disc = lambda *p: _s5_discretise(p[0], p[1], p[2], p[3], p[4])
    disc_args = (s5_a_re[0], s5_a_im[0], s5_log_dt[0], s5_b_re[0], s5_b_im[0])
    disc_out, disc_vjp = jax.vjp(disc, *disc_args)
    small = {
        "norm_mix0": norm_mix[0:1] + token[0:1, 0:1], "norm_mix1": norm_mix[1:2], "norm_mlp0": norm_mlp[0:1], "norm_mlp1": norm_mlp[1:2],
        "norm_kv": norm_kv.reshape(1, d), "norm_final": norm_final.reshape(1, d), "s5_disc": disc_out,
        "s5_c_re": s5_c_re[0], "s5_c_im": s5_c_im[0], "s5_d": d_full, "s5_b_glu": bglu_full,
        "b_kv": b_kv.reshape(1, -1), "b_q": b_q, "sinks": sinks, "b_o": b_o,
    }
    loss_tile, grad_x, gs = _local_step(x[0], loss_target[0], small, need, emit)

    drb, drc, dlr, dli = gs["s5_mats"]
    dbbar_re, dbbar_im, dc_re, dc_im = _s5_matrix_grads(drb, drc)
    pieces = [loss_tile[0:1, 0:1], gs["norm_mix0"], gs["norm_mix1"], gs["norm_mlp0"], gs["norm_mlp1"], gs["norm_kv"],
              gs["norm_final"], dlr, dli, dbbar_re, dbbar_im, dc_re, dc_im, gs["s5_d"], gs["s5_b_glu"], gs["b_kv"],
              gs["b_q"], gs["sinks"], gs["b_o"]]
    summed_buf = all_reduce_small("reduce_small", _pack(pieces))
    summed = _unpack(summed_buf, pieces)
    (loss, g_mix0, g_mix1, g_mlp0, g_mlp1, g_kv, g_final, dlr, dli, dbbar_re, dbbar_im, dc_re, dc_im, g_d, g_bglu,
     g_bkv, g_bq, g_sinks, g_bo) = summed
    g_are, g_aim, g_dt, g_bre, g_bim = disc_vjp((dlr.reshape(S5_GROUPS, S5_STATE), dli.reshape(S5_GROUPS, S5_STATE),
                                                  dbbar_re, dbbar_im))
    grads = {
        "norm_mix": jnp.concatenate([g_mix0, g_mix1]), "norm_mlp": jnp.concatenate([g_mlp0, g_mlp1]),
        "norm_kv": g_kv.reshape(d), "norm_final": g_final.reshape(d), "s5_a_re": g_are[None], "s5_a_im": g_aim[None],
        "s5_log_dt": g_dt[None], "s5_b_re": g_bre[None], "s5_b_im": g_bim[None], "s5_c_re": dc_re[None],
        "s5_c_im": dc_im[None], "s5_d": lax.dynamic_slice(g_d, (0, chip * dsh), (1, dsh)),
        "s5_b_glu": lax.dynamic_slice(g_bglu, (0, chip * bsh), (1, bsh)), "b_kv": g_bkv.reshape(-1), "b_q": g_bq,
        "sinks": g_sinks, "b_o": g_bo,
    }

    reduced = [None] * len(big)
    where_of = dict(zip(names, entries))
    for group, after in (("layer1", grad_x), ("layer0", summed_buf)):
        members, send, recv, parts, lands = exchanging[group]
        ks, shapes = [kinds[n] for n in members], [shard_shapes[n] for n in members]
        parts, lands = exchange_wait("exchange_wait_" + group, send, recv, parts, lands, ks, shapes, after)
        for t, n in enumerate(members):
            a, layer, kind = where_of[n]
            reduced[a] = sum_shards("sum_shards_" + n, parts[t], lands[t], kind, shapes[t], where, layer,
                                    big[a].shape[0], into=reduced[a])
    reduced = share_halves(reduced, entries)
    for n, g in zip(BIG_NAMES, reduced):
        grads[n] = g.reshape(w[n].shape)

    delta, new_m, new_v = {}, {}, {}
    for n in BIG_NAMES:
        flat = lambda a: a.reshape(-1, a.shape[-1])
        dl, nm, nv = adamw("adamw_" + n, flat(w[n]), flat(grads[n]), flat(mom[n]), flat(var[n]))
        delta[n], new_m[n], new_v[n] = dl.reshape(w[n].shape), nm.reshape(w[n].shape), nv.reshape(w[n].shape)
    sw, sg, sm, sv = ([t[n] for n in SMALL_NAMES] for t in (w, grads, mom, var))
    dl, nm, nv = adamw("adamw_small", _pack(sw), _pack(sg), _pack(sm), _pack(sv))
    for n, a, b, c_ in zip(SMALL_NAMES, _unpack(dl, sw), _unpack(nm, sw), _unpack(nv, sw)):
        delta[n], new_m[n], new_v[n] = a, b, c_

    out = [loss.reshape(()), grad_x[None]]
    for table in (grads, delta, new_m, new_v):
        out += [table[n].reshape(w[n].shape) for n in WEIGHT_ORDER]
    return tuple(out)
```

```python
import functools
import math

import jax
import jax.numpy as jnp
from jax import lax
from jax.experimental import pallas as pl
from jax.experimental.pallas import tpu as pltpu

F32 = jnp.float32
BF16 = jnp.bfloat16

D_MODEL = 1024
S5_GROUPS = 64
S5_GROUP = 16
S5_STATE = 64
N_KV = 4
N_Q = 16
HEAD_DIM = 64
BLOCK = 128
NORM_EPS = 1e-5
LAMBDA_RE_MAX = -1e-4
ADAM_LR, ADAM_B1, ADAM_B2, ADAM_EPS, ADAM_WD, ADAM_STEP = 0.001, 0.9, 0.999, 1e-08, 0.01, 10

VMEM_LIMIT_BYTES = 56 * 1024 * 1024
S5_CHUNK = 256
S5_BLOCKS = 4
MESH = pl.DeviceIdType.MESH


def _params(sem=None):
    return pltpu.CompilerParams(dimension_semantics=sem, vmem_limit_bytes=VMEM_LIMIT_BYTES)


def _sds(shape, dtype):
    return jax.ShapeDtypeStruct(shape, dtype)


def _rms_hat(xv):
    r = lax.rsqrt(jnp.mean(xv * xv, axis=-1, keepdims=True) + NORM_EPS)
    return xv * r, r


def rms_fwd(name, x, gains, out_dtypes, tm=256):
    n_rows, d = x.shape
    ng = len(gains)

    def body(x_ref, *refs):
        xh, _ = _rms_hat(x_ref[...])
        for g_ref, o_ref in zip(refs[:ng], refs[ng:]):
            o_ref[...] = (xh * g_ref[...]).astype(o_ref.dtype)

    row = pl.BlockSpec((tm, d), lambda i: (i, 0))
    vec = pl.BlockSpec((1, d), lambda i: (0, 0))
    return pl.pallas_call(
        body, grid=(n_rows // tm,), in_specs=[row] + [vec] * ng, out_specs=[row] * ng,
        out_shape=[_sds((n_rows, d), dt) for dt in out_dtypes], name=name,
        compiler_params=_params(("parallel",)))(x, *gains)


def rms_bwd(name, x, dys, gains, res, tm=256):
    n_rows, d = x.shape
    ng = len(gains)

    def body(x_ref, res_ref, *refs):
        dy_refs, g_refs = refs[:ng], refs[ng:2 * ng]
        dx_ref, dxb_ref, cs_ref = refs[2 * ng:2 * ng + 3]
        dg_refs = refs[2 * ng + 3:]
        i = pl.program_id(0)
        xh, r = _rms_hat(x_ref[...])
        dxh = jnp.zeros_like(xh)
        dgs = []
        for dy_ref, g_ref in zip(dy_refs, g_refs):
            dy = dy_ref[...].astype(F32)
            dxh = dxh + dy * g_ref[...]
            dgs.append(jnp.sum(dy * xh, axis=0, keepdims=True))
        dx = r * (dxh - xh * jnp.mean(dxh * xh, axis=-1, keepdims=True)) + res_ref[...]
        dx_ref[...] = dx
        dxb_ref[...] = dx.astype(BF16)
        cs = jnp.sum(dx, axis=0, keepdims=True)

        @pl.when(i == 0)
        def _():
            cs_ref[...] = jnp.zeros_like(cs_ref)
            for dg_ref in dg_refs:
                dg_ref[...] = jnp.zeros_like(dg_ref)

        cs_ref[...] += cs
        for dg_ref, dg in zip(dg_refs, dgs):
            dg_ref[...] += dg

    row = pl.BlockSpec((tm, d), lambda i: (i, 0))
    vec = pl.BlockSpec((1, d), lambda i: (0, 0))
    return pl.pallas_call(
        body, grid=(n_rows // tm,), in_specs=[row, row] + [row] * ng + [vec] * ng,
        out_specs=[row, row, vec] + [vec] * ng,
        out_shape=[_sds((n_rows, d), F32), _sds((n_rows, d), BF16), _sds((1, d), F32)] + [_sds((1, d), F32)] * ng,
        name=name, compiler_params=_params(("arbitrary",)))(x, res, *dys, *gains)


def mm_nn(name, a, w, col_offsets, n_out, epilogue, out_dtypes, extras=(), rowvecs=(), tm=1024, tn=512):
    m, k = a.shape
    tm, tn = min(tm, m), min(tn, n_out)
    nw, ne, nr = len(col_offsets), len(extras), len(rowvecs)

    def body(a_ref, *refs):
        w_refs, e_refs, r_refs = refs[:nw], refs[nw:nw + ne], refs[nw + ne:nw + ne + nr]
        o_refs = refs[nw + ne + nr:]
        av = a_ref[...]
        accs = [jnp.dot(av, w_ref[...], preferred_element_type=F32) for w_ref in w_refs]
        outs = epilogue(accs, [e[...] for e in e_refs], [r[...] for r in r_refs])
        for o_ref, o in zip(o_refs, outs):
            o_ref[...] = o.astype(o_ref.dtype)

    def wspec(off):
        return pl.BlockSpec((k, tn), lambda j, i, off=off: (0, off // tn + j))

    def rspec(off):
        return pl.BlockSpec((1, tn), lambda j, i, off=off: (0, off // tn + j))

    tile = pl.BlockSpec((tm, tn), lambda j, i: (i, j))
    in_specs = ([pl.BlockSpec((tm, k), lambda j, i: (i, 0))] + [wspec(o) for o in col_offsets]
                + [tile] * ne + [rspec(o) for _, o in rowvecs])
    return pl.pallas_call(
        body, grid=(n_out // tn, m // tm), in_specs=in_specs, out_specs=[tile] * len(out_dtypes),
        out_shape=[_sds((m, n_out), dt) for dt in out_dtypes], name=name,
        compiler_params=_params(("parallel", "parallel")))(a, *([w] * nw), *extras, *[r for r, _ in rowvecs])


def mm_nt(name, g, w, epilogue, out_dtypes, extras=(), tm=512, tk=512):
    m, n = g.shape
    k = w.shape[0]
    tm, tk = min(tm, m), min(tk, k)
    ne = len(extras)

    def body(g_ref, w_ref, *refs):
        e_refs, o_refs = refs[:ne], refs[ne:]
        acc = lax.dot_general(g_ref[...], w_ref[...], (((1,), (1,)), ((), ())), preferred_element_type=F32)
        outs = epilogue(acc, [e[...] for e in e_refs])
        for o_ref, o in zip(o_refs, outs):
            o_ref[...] = o.astype(o_ref.dtype)

    tile = pl.BlockSpec((tm, tk), lambda i, j: (i, j))
    return pl.pallas_call(
        body, grid=(m // tm, k // tk),
        in_specs=[pl.BlockSpec((tm, n), lambda i, j: (i, 0)), pl.BlockSpec((tk, n), lambda i, j: (j, 0))] + [tile] * ne,
        out_specs=[tile] * len(out_dtypes), out_shape=[_sds((m, k), dt) for dt in out_dtypes], name=name,
        compiler_params=_params(("parallel", "parallel")))(g, w, *extras)


def mm_tn(name, a, g, tk=512, tn=512):
    m, k = a.shape
    n = g.shape[1]
    tk, tn = min(tk, k), min(tn, n)

    def body(a_ref, g_ref, o_ref):
        acc = lax.dot_general(a_ref[...], g_ref[...], (((0,), (0,)), ((), ())), preferred_element_type=F32)
        o_ref[...] = acc.astype(o_ref.dtype)

    return pl.pallas_call(
        body, grid=(k // tk, n // tn),
        in_specs=[pl.BlockSpec((m, tk), lambda i, j: (0, i)), pl.BlockSpec((m, tn), lambda i, j: (0, j))],
        out_specs=pl.BlockSpec((tk, tn), lambda i, j: (i, j)), out_shape=_sds((k, n), BF16), name=name,
        compiler_params=_params(("parallel", "parallel")))(a, g)


def _row_mask(tc):
    row = lax.broadcasted_iota(jnp.int32, (8 * tc, 256), 0) % 8
    col = lax.broadcasted_iota(jnp.int32, (8 * tc, 256), 1) // 32
    return row == col


def _expand_rows(expand_ref, val, mask):
    rep = jnp.dot(expand_ref[...], val.astype(BF16), preferred_element_type=F32)
    return jnp.where(mask, rep, 0.0).astype(BF16)


def _staged(ref):
    return jnp.concatenate([ref[0], ref[1]], axis=1)


def _stage(ref, val):
    ref[0] = val[:, 0:128]
    ref[1] = val[:, 128:256]


def _gather_rows(src_ref, tc):
    halves = []
    for half in range(2):
        col = lax.broadcasted_iota(jnp.int32, (tc, 128), 1) // 32 + 4 * half
        out = jnp.zeros((tc, 128), F32)
        for s8 in range(4 * half, 4 * half + 4):
            out = jnp.where(col == s8, src_ref.at[half][pl.ds(s8, tc, stride=8), :], out)
        halves.append(out)
    return jnp.concatenate(halves, axis=1)


def _gelu(x):
    c = math.sqrt(2.0 / math.pi)
    return 0.5 * x * (1.0 + jnp.tanh(c * (x + 0.044715 * x * x * x)))


def _gelu_grad(x):
    c = math.sqrt(2.0 / math.pi)
    t = jnp.tanh(c * (x + 0.044715 * x * x * x))
    return 0.5 * (1.0 + t) + 0.5 * x * (1.0 - t * t) * c * (1.0 + 3.0 * 0.044715 * x * x)


def _expansion(tc):
    return (jnp.arange(8 * tc)[:, None] // 8 == jnp.arange(tc)[None, :]).astype(BF16)


def s5_fwd(u, d_skip, rb, rc, lam_r, lam_i):
    n_rows = u.shape[0]
    tc = min(S5_CHUNK, n_rows)
    nc = n_rows // tc

    def body(u_ref, d_ref, ex_ref, rb_ref, rc_ref, lr_ref, li_ref, ge_ref, y2_ref, cs_ref, bux, yrows, carry):
        i = pl.program_id(0)

        @pl.when(i == 0)
        def _():
            carry[...] = jnp.zeros_like(carry)

        cs_ref[0] = carry[...]
        mask = _row_mask(tc)
        for blk in range(S5_BLOCKS):
            lhs = _expand_rows(ex_ref, u_ref[:, blk * 256:(blk + 1) * 256], mask)
            bux[blk] = jnp.dot(lhs, rb_ref[blk], preferred_element_type=F32)
        lam = [(lr_ref[blk], li_ref[blk]) for blk in range(S5_BLOCKS)]

        def step(t, c):
            r0 = pl.multiple_of(t * 8, 8)
            new = []
            for blk in range(S5_BLOCKS):
                xr, xi = c[2 * blk], c[2 * blk + 1]
                lr, li = lam[blk]
                nr = lr * xr - li * xi + bux[blk, pl.ds(r0, 8), 0:128]
                ni = lr * xi + li * xr + bux[blk, pl.ds(r0, 8), 128:256]
                bux[blk, pl.ds(r0, 8), 0:128] = nr
                bux[blk, pl.ds(r0, 8), 128:256] = ni
                new += [nr, ni]
            return tuple(new)

        c0 = []
        for blk in range(S5_BLOCKS):
            c0 += [carry[blk, :, 0:128], carry[blk, :, 128:256]]
        cn = lax.fori_loop(0, tc, step, tuple(c0), unroll=4)
        for blk in range(S5_BLOCKS):
            carry[blk, :, 0:128] = cn[2 * blk]
            carry[blk, :, 128:256] = cn[2 * blk + 1]
        for blk in range(S5_BLOCKS):
            _stage(yrows, jnp.dot(bux[blk].astype(BF16), rc_ref[blk], preferred_element_type=F32))
            sl = slice(blk * 256, (blk + 1) * 256)
            y2 = _gather_rows(yrows, tc) + d_ref[:, sl] * u_ref[:, sl]
            y2_ref[:, sl] = y2
            ge_ref[:, sl] = _gelu(y2).astype(BF16)

    row = pl.BlockSpec((tc, D_MODEL), lambda i: (i, 0))
    mat = pl.BlockSpec((S5_BLOCKS, 256, 256), lambda i: (0, 0, 0))
    lamspec = pl.BlockSpec((S5_BLOCKS, 8, 128), lambda i: (0, 0, 0))
    return pl.pallas_call(
        body, grid=(nc,),
        in_specs=[row, pl.BlockSpec((1, D_MODEL), lambda i: (0, 0)), pl.BlockSpec((8 * tc, tc), lambda i: (0, 0)),
                  mat, mat, lamspec, lamspec],
        out_specs=[row, row, pl.BlockSpec((1, S5_BLOCKS, 8, 256), lambda i: (i, 0, 0, 0))],
        out_shape=[_sds((n_rows, D_MODEL), BF16), _sds((n_rows, D_MODEL), F32), _sds((nc, S5_BLOCKS, 8, 256), F32)],
        scratch_shapes=[pltpu.VMEM((S5_BLOCKS, 8 * tc, 256), F32), pltpu.VMEM((2, 8 * tc, 128), F32),
                        pltpu.VMEM((S5_BLOCKS, 8, 256), F32)],
        name="s5_fwd", compiler_params=_params(("arbitrary",)))(u, d_skip, _expansion(tc), rb, rc, lam_r, lam_i)


def s5_bwd(u, dy2, d_skip, cs, rb, rbt, rct, lam_r, lam_i):
    n_rows = u.shape[0]
    tc = min(S5_CHUNK, n_rows)
    nc = n_rows // tc

    def body(u_ref, dy_ref, d_ref, cs_ref, ex_ref, rb_ref, rbt_ref, rct_ref, lr_ref, li_ref,
             du_ref, dd_ref, drb_ref, drc_ref, dlr_ref, dli_ref, tmp, lhsu, lhsd, xs, adj, acarry):
        i = pl.program_id(0)

        @pl.when(i == 0)
        def _():
            acarry[...] = jnp.zeros_like(acarry)
            dd_ref[...] = jnp.zeros_like(dd_ref)
            drb_ref[...] = jnp.zeros_like(drb_ref)
            drc_ref[...] = jnp.zeros_like(drc_ref)
            dlr_ref[...] = jnp.zeros_like(dlr_ref)
            dli_ref[...] = jnp.zeros_like(dli_ref)

        dd_ref[...] += jnp.sum(dy_ref[...] * u_ref[...], axis=0, keepdims=True)
        mask = _row_mask(tc)
        for blk in range(S5_BLOCKS):
            sl = slice(blk * 256, (blk + 1) * 256)
            lhsu[blk] = _expand_rows(ex_ref, u_ref[:, sl], mask)
            xs[blk] = jnp.dot(lhsu[blk], rb_ref[blk], preferred_element_type=F32)
            lhsd[blk] = _expand_rows(ex_ref, dy_ref[:, sl], mask)
            adj[blk] = jnp.dot(lhsd[blk], rct_ref[blk], preferred_element_type=F32)
        lam = [(lr_ref[blk], li_ref[blk]) for blk in range(S5_BLOCKS)]

        def fstep(t, c):
            r0 = pl.multiple_of(t * 8, 8)
            new = []
            for blk in range(S5_BLOCKS):
                xr, xi = c[2 * blk], c[2 * blk + 1]
                lr, li = lam[blk]
                nr = lr * xr - li * xi + xs[blk, pl.ds(r0, 8), 0:128]
                ni = lr * xi + li * xr + xs[blk, pl.ds(r0, 8), 128:256]
                xs[blk, pl.ds(r0, 8), 0:128] = nr
                xs[blk, pl.ds(r0, 8), 128:256] = ni
                new += [nr, ni]
            return tuple(new)

        c0 = []
        for blk in range(S5_BLOCKS):
            c0 += [cs_ref[0, blk, :, 0:128], cs_ref[0, blk, :, 128:256]]
        lax.fori_loop(0, tc, fstep, tuple(c0), unroll=4)

        def bstep(k, c):
            t = tc - 1 - k
            r0 = pl.multiple_of(t * 8, 8)
            rp = pl.multiple_of(jnp.maximum(t - 1, 0) * 8, 8)
            first = t == 0
            new_a, new_g = [], []
            for blk in range(S5_BLOCKS):
                ar, ai = c[0][2 * blk], c[0][2 * blk + 1]
                glr, gli = c[1][2 * blk], c[1][2 * blk + 1]
                lr, li = lam[blk]
                nr = lr * ar + li * ai + adj[blk, pl.ds(r0, 8), 0:128]
                ni = lr * ai - li * ar + adj[blk, pl.ds(r0, 8), 128:256]
                adj[blk, pl.ds(r0, 8), 0:128] = nr
                adj[blk, pl.ds(r0, 8), 128:256] = ni
                pr = jnp.where(first, cs_ref[0, blk, :, 0:128], xs[blk, pl.ds(rp, 8), 0:128])
                pi = jnp.where(first, cs_ref[0, blk, :, 128:256], xs[blk, pl.ds(rp, 8), 128:256])
                new_a += [nr, ni]
                new_g += [glr + nr * pr + ni * pi, gli + ni * pr - nr * pi]
            return tuple(new_a), tuple(new_g)

        a0, g0 = [], []
        for blk in range(S5_BLOCKS):
            a0 += [acarry[blk, :, 0:128], acarry[blk, :, 128:256]]
            g0 += [dlr_ref[blk], dli_ref[blk]]
        an, gn = lax.fori_loop(0, tc, bstep, (tuple(a0), tuple(g0)), unroll=2)
        for blk in range(S5_BLOCKS):
            acarry[blk, :, 0:128] = an[2 * blk]
            acarry[blk, :, 128:256] = an[2 * blk + 1]
            dlr_ref[blk] = gn[2 * blk]
            dli_ref[blk] = gn[2 * blk + 1]
        for blk in range(S5_BLOCKS):
            sl = slice(blk * 256, (blk + 1) * 256)
            ab = adj[blk].astype(BF16)
            _stage(tmp, jnp.dot(ab, rbt_ref[blk], preferred_element_type=F32))
            du_ref[:, sl] = _gather_rows(tmp, tc) + d_ref[:, sl] * dy_ref[:, sl]
            drb_ref[blk] += lax.dot_general(lhsu[blk], ab, (((0,), (0,)), ((), ())), preferred_element_type=F32)
            drc_ref[blk] += lax.dot_general(xs[blk].astype(BF16), lhsd[blk], (((0,), (0,)), ((), ())),
                                            preferred_element_type=F32)

    rev = pl.BlockSpec((tc, D_MODEL), lambda i: (nc - 1 - i, 0))
    vec = pl.BlockSpec((1, D_MODEL), lambda i: (0, 0))
    mat = pl.BlockSpec((S5_BLOCKS, 256, 256), lambda i: (0, 0, 0))
    lamspec = pl.BlockSpec((S5_BLOCKS, 8, 128), lambda i: (0, 0, 0))
    big = pltpu.VMEM((S5_BLOCKS, 8 * tc, 256), F32)
    bigb = pltpu.VMEM((S5_BLOCKS, 8 * tc, 256), BF16)
    return pl.pallas_call(
        body, grid=(nc,),
        in_specs=[rev, rev, vec, pl.BlockSpec((1, S5_BLOCKS, 8, 256), lambda i: (nc - 1 - i, 0, 0, 0)),
                  pl.BlockSpec((8 * tc, tc), lambda i: (0, 0)), mat, mat, mat, lamspec, lamspec],
        out_specs=[rev, vec, mat, mat, lamspec, lamspec],
        out_shape=[_sds((n_rows, D_MODEL), F32), _sds((1, D_MODEL), F32), _sds((S5_BLOCKS, 256, 256), F32),
                   _sds((S5_BLOCKS, 256, 256), F32), _sds((S5_BLOCKS, 8, 128), F32), _sds((S5_BLOCKS, 8, 128), F32)],
        scratch_shapes=[pltpu.VMEM((2, 8 * tc, 128), F32), bigb, bigb, big, big, pltpu.VMEM((S5_BLOCKS, 8, 256), F32)],
        name="s5_bwd", compiler_params=_params(("arbitrary",)))(u, dy2, d_skip, cs, _expansion(tc), rb, rbt, rct, lam_r, lam_i)


def _s5_discretise(a_re, a_im, log_dt, b_re, b_im):
    lam = lax.complex(jnp.minimum(a_re, LAMBDA_RE_MAX), a_im)
    dt = jnp.exp(log_dt)[:, None]
    lam_bar = jnp.exp(lam * dt)
    b_bar = ((lam_bar - 1.0) / lam)[:, :, None] * lax.complex(b_re, b_im)
    return jnp.real(lam_bar), jnp.imag(lam_bar), jnp.real(b_bar), jnp.imag(b_bar)


def _s5_matrices(bbar_re, bbar_im, c_re, c_im):
    eye2 = jnp.eye(2, dtype=F32)
    bst = jnp.stack([bbar_re, bbar_im]).reshape(2, S5_BLOCKS, 8, 2, S5_STATE, S5_GROUP)
    bt = jnp.transpose(bst, (1, 2, 3, 5, 0, 4))
    rb = (bt[:, :, :, :, :, None, :] * eye2[None, None, :, None, None, :, None]).reshape(S5_BLOCKS, 256, 256)
    cst = jnp.stack([c_re, -c_im]).reshape(2, S5_BLOCKS, 8, 2, S5_GROUP, S5_STATE)
    ct = jnp.transpose(cst, (1, 0, 5, 2, 3, 4))
    rc = (ct[:, :, None, :, :, :, :] * eye2[None, None, :, None, None, :, None]).reshape(S5_BLOCKS, 256, 256)
    return rb, rc


def _s5_matrix_grads(drb, drc):
    x = drb.reshape(S5_BLOCKS, 8, 2, S5_GROUP, 2, 2, S5_STATE)
    db = jnp.stack([x[:, :, 0, :, :, 0, :], x[:, :, 1, :, :, 1, :]], axis=2)
    db = jnp.transpose(db, (4, 0, 1, 2, 5, 3)).reshape(2, S5_GROUPS, S5_STATE, S5_GROUP)
    y = drc.reshape(S5_BLOCKS, 2, 2, S5_STATE, 8, 2, S5_GROUP)
    dc = jnp.stack([y[:, :, 0, :, :, 0, :], y[:, :, 1, :, :, 1, :]], axis=4)
    dc = jnp.transpose(dc, (1, 0, 3, 4, 5, 2)).reshape(2, S5_GROUPS, S5_GROUP, S5_STATE)
    return db[0], db[1], dc[0], -dc[1]


NEG = -1e30


GROUP = N_Q // N_KV


def _attn_masks(n):
    qi = lax.broadcasted_iota(jnp.int32, (GROUP * BLOCK, BLOCK), 0) % BLOCK
    kj = lax.broadcasted_iota(jnp.int32, (GROUP * BLOCK, BLOCK), 1)
    return jnp.logical_and(kj > qi, n > 0), kj <= qi


def _stack_heads(ref, kh):
    return jnp.concatenate([ref[:, (GROUP * kh + g) * HEAD_DIM:(GROUP * kh + g + 1) * HEAD_DIM] for g in range(GROUP)], axis=0)


def _unstack_heads(val):
    return jnp.concatenate([val[g * BLOCK:(g + 1) * BLOCK] for g in range(GROUP)], axis=1)


def _sink_column(sink_ref, kh):
    grp = lax.broadcasted_iota(jnp.int32, (GROUP * BLOCK, 1), 0) // BLOCK
    col = jnp.zeros((GROUP * BLOCK, 1), F32)
    for g in range(GROUP):
        col = jnp.where(grp == g, sink_ref[GROUP * kh + g], col)
    return col, grp


def _attn_exp(q4, kp, kc, sink, mask_p, mask_c):
    scale = 1.0 / math.sqrt(HEAD_DIM)
    nt = (((1,), (1,)), ((), ()))
    sp = jnp.where(mask_p, lax.dot_general(q4, kp, nt, preferred_element_type=F32) * scale, NEG)
    sc = jnp.where(mask_c, lax.dot_general(q4, kc, nt, preferred_element_type=F32) * scale, NEG)
    m = jnp.maximum(jnp.maximum(jnp.max(sp, axis=-1, keepdims=True), jnp.max(sc, axis=-1, keepdims=True)), sink)
    pp = jnp.exp(sp - m)
    pc = jnp.exp(sc - m)
    ps = jnp.exp(sink - m)
    inv = 1.0 / (jnp.sum(pp, axis=-1, keepdims=True) + jnp.sum(pc, axis=-1, keepdims=True) + ps)
    return pp, pc, ps, inv


def attn_fwd(q, kv, sinks):
    n_rows = q.shape[0]
    nb = n_rows // BLOCK

    def body(sink_ref, q_ref, kvp_ref, kvc_ref, o_ref):
        n = pl.program_id(0)
        mask_p, mask_c = _attn_masks(n)
        outs = []
        for kh in range(N_KV):
            ks, vs = slice(kh * HEAD_DIM, (kh + 1) * HEAD_DIM), slice((N_KV + kh) * HEAD_DIM, (N_KV + kh + 1) * HEAD_DIM)
            sink, _ = _sink_column(sink_ref, kh)
            pp, pc, _, inv = _attn_exp(_stack_heads(q_ref, kh), kvp_ref[:, ks], kvc_ref[:, ks], sink, mask_p, mask_c)
            o4 = (jnp.dot(pp.astype(BF16), kvp_ref[:, vs], preferred_element_type=F32)
                  + jnp.dot(pc.astype(BF16), kvc_ref[:, vs], preferred_element_type=F32)) * inv
            outs.append(_unstack_heads(o4))
        o_ref[...] = jnp.concatenate(outs, axis=1).astype(BF16)

    kvw = 2 * N_KV * HEAD_DIM
    return pl.pallas_call(
        body, grid=(nb,),
        in_specs=[pl.BlockSpec(memory_space=pltpu.SMEM), pl.BlockSpec((BLOCK, D_MODEL), lambda n: (n, 0)),
                  pl.BlockSpec((BLOCK, kvw), lambda n: (jnp.maximum(n - 1, 0), 0)), pl.BlockSpec((BLOCK, kvw), lambda n: (n, 0))],
        out_specs=pl.BlockSpec((BLOCK, D_MODEL), lambda n: (n, 0)), out_shape=_sds((n_rows, D_MODEL), BF16),
        name="attn_fwd", compiler_params=_params(("parallel",)))(sinks, q, kv, kv)


def attn_bwd(q, kv, do, sinks):
    n_rows = q.shape[0]
    nb = n_rows // BLOCK
    kvw = 2 * N_KV * HEAD_DIM
    tn = (((0,), (0,)), ((), ()))
    nt = (((1,), (1,)), ((), ()))
    scale = 1.0 / math.sqrt(HEAD_DIM)

    def body(sink_ref, q_ref, kvp_ref, kvc_ref, do_ref, dq_ref, dbq_ref, dprev_ref, dcur_ref, dsink_ref):
        n = pl.program_id(0)
        mask_p, mask_c = _attn_masks(n)
        lane = lax.broadcasted_iota(jnp.int32, (1, 128), 1)
        dqs, dsink = [], jnp.zeros((1, 128), F32)
        dkp, dkc, dvp, dvc = [], [], [], []
        for kh in range(N_KV):
            ks, vs = slice(kh * HEAD_DIM, (kh + 1) * HEAD_DIM), slice((N_KV + kh) * HEAD_DIM, (N_KV + kh + 1) * HEAD_DIM)
            q4, do4 = _stack_heads(q_ref, kh), _stack_heads(do_ref, kh)
            kp, kc, vp, vc = kvp_ref[:, ks], kvc_ref[:, ks], kvp_ref[:, vs], kvc_ref[:, vs]
            sink, grp = _sink_column(sink_ref, kh)
            pp, pc, ps, inv = _attn_exp(q4, kp, kc, sink, mask_p, mask_c)
            pp, pc = pp * inv, pc * inv
            dpp = lax.dot_general(do4, vp, nt, preferred_element_type=F32)
            dpc = lax.dot_general(do4, vc, nt, preferred_element_type=F32)
            delta = jnp.sum(pp * dpp, axis=-1, keepdims=True) + jnp.sum(pc * dpc, axis=-1, keepdims=True)
            dsp = (pp * (dpp - delta) * scale).astype(BF16)
            dsc = (pc * (dpc - delta) * scale).astype(BF16)
            dsk = ps * inv * delta
            for g in range(GROUP):
                dsink = dsink + jnp.where(lane == GROUP * kh + g, -jnp.sum(jnp.where(grp == g, dsk, 0.0)), 0.0)
            dqs.append(_unstack_heads(jnp.dot(dsp, kp, preferred_element_type=F32)
                                      + jnp.dot(dsc, kc, preferred_element_type=F32)))
            dkp.append(lax.dot_general(dsp, q4, tn, preferred_element_type=F32))
            dkc.append(lax.dot_general(dsc, q4, tn, preferred_element_type=F32))
            dvp.append(lax.dot_general(pp.astype(BF16), do4, tn, preferred_element_type=F32))
            dvc.append(lax.dot_general(pc.astype(BF16), do4, tn, preferred_element_type=F32))
        dq = jnp.concatenate(dqs, axis=1)
        dq_ref[...] = dq.astype(BF16)
        dprev_ref[0] = jnp.concatenate(dkp + dvp, axis=1)
        dcur_ref[0] = jnp.concatenate(dkc + dvc, axis=1)

        @pl.when(n == 0)
        def _():
            dbq_ref[...] = jnp.zeros_like(dbq_ref)
            dsink_ref[...] = jnp.zeros_like(dsink_ref)

        dbq_ref[...] += jnp.sum(dq, axis=0, keepdims=True)
        dsink_ref[...] += dsink

    blk = pl.BlockSpec((BLOCK, D_MODEL), lambda n: (n, 0))
    part = pl.BlockSpec((1, BLOCK, kvw), lambda n: (n, 0, 0))
    return pl.pallas_call(
        body, grid=(nb,),
        in_specs=[pl.BlockSpec(memory_space=pltpu.SMEM), blk,
                  pl.BlockSpec((BLOCK, kvw), lambda n: (jnp.maximum(n - 1, 0), 0)), pl.BlockSpec((BLOCK, kvw), lambda n: (n, 0)), blk],
        out_specs=[blk, pl.BlockSpec((1, D_MODEL), lambda n: (0, 0)), part, part, pl.BlockSpec((1, 128), lambda n: (0, 0))],
        out_shape=[_sds((n_rows, D_MODEL), BF16), _sds((1, D_MODEL), F32), _sds((nb, BLOCK, kvw), F32),
                   _sds((nb, BLOCK, kvw), F32), _sds((1, 128), F32)],
        name="attn_bwd", compiler_params=_params(("arbitrary",)))(sinks, q, kv, kv, do)


def kv_combine(dprev, dcur):
    nb, _, kvw = dprev.shape

    def body(dcur_ref, dnext_ref, dkv_ref, db_ref):
        m = pl.program_id(0)
        dkv = dcur_ref[0] + jnp.where(m + 1 < nb, dnext_ref[0], 0.0)
        dkv_ref[...] = dkv.astype(BF16)

        @pl.when(m == 0)
        def _():
            db_ref[...] = jnp.zeros_like(db_ref)

        db_ref[...] += jnp.sum(dkv, axis=0, keepdims=True)

    return pl.pallas_call(
        body, grid=(nb,),
        in_specs=[pl.BlockSpec((1, BLOCK, kvw), lambda m: (m, 0, 0)),
                  pl.BlockSpec((1, BLOCK, kvw), lambda m: (jnp.minimum(m + 1, nb - 1), 0, 0))],
        out_specs=[pl.BlockSpec((BLOCK, kvw), lambda m: (m, 0)), pl.BlockSpec((1, kvw), lambda m: (0, 0))],
        out_shape=[_sds((nb * BLOCK, kvw), BF16), _sds((1, kvw), F32)],
        name="kv_combine", compiler_params=_params(("arbitrary",)))(dcur, dprev)


def glu_bwd(dout, val, gate, tm=256):
    n_rows, d = dout.shape

    def body(do_ref, v_ref, g_ref, dz_ref, db_ref):
        i = pl.program_id(0)
        sg = jax.nn.sigmoid(g_ref[...])
        dval = do_ref[...] * sg
        dgate = do_ref[...] * v_ref[...] * sg * (1.0 - sg)
        dz = jnp.concatenate([dval, dgate], axis=1)
        dz_ref[...] = dz.astype(BF16)

        @pl.when(i == 0)
        def _():
            db_ref[...] = jnp.zeros_like(db_ref)

        db_ref[...] += jnp.sum(dz, axis=0, keepdims=True)

    row = pl.BlockSpec((tm, d), lambda i: (i, 0))
    return pl.pallas_call(
        body, grid=(n_rows // tm,), in_specs=[row, row, row],
        out_specs=[pl.BlockSpec((tm, 2 * d), lambda i: (i, 0)), pl.BlockSpec((1, 2 * d), lambda i: (0, 0))],
        out_shape=[_sds((n_rows, 2 * d), BF16), _sds((1, 2 * d), F32)],
        name="glu_bwd", compiler_params=_params(("arbitrary",)))(dout, val, gate)


def final_loss(h, target, gain, tm=256):
    n_rows, d = h.shape

    def body(h_ref, t_ref, g_ref, loss_ref, dh_ref, dhb_ref, dg_ref):
        i = pl.program_id(0)
        xh, r = _rms_hat(h_ref[...])
        err = xh * g_ref[...] - t_ref[...]
        dy = err * (1.0 / d)
        dxh = dy * g_ref[...]
        dx = r * (dxh - xh * jnp.mean(dxh * xh, axis=-1, keepdims=True))
        dh_ref[...] = dx
        dhb_ref[...] = dx.astype(BF16)

        @pl.when(i == 0)
        def _():
            loss_ref[...] = jnp.zeros_like(loss_ref)
            dg_ref[...] = jnp.zeros_like(dg_ref)

        loss_ref[...] += jnp.full((8, 128), 0.5 * jnp.sum(jnp.mean(err * err, axis=-1, keepdims=True)), F32)
        dg_ref[...] += jnp.sum(dy * xh, axis=0, keepdims=True)

    row = pl.BlockSpec((tm, d), lambda i: (i, 0))
    vec = pl.BlockSpec((1, d), lambda i: (0, 0))
    return pl.pallas_call(
        body, grid=(n_rows // tm,), in_specs=[row, row, vec],
        out_specs=[pl.BlockSpec((8, 128), lambda i: (0, 0)), row, row, vec],
        out_shape=[_sds((8, 128), F32), _sds((n_rows, d), F32), _sds((n_rows, d), BF16), _sds((1, d), F32)],
        name="final_loss", compiler_params=_params(("arbitrary",)))(h, target, gain)


def adamw(name, w, g, m, v, tm=256):
    n_rows, d = w.shape
    tm = tm if n_rows % tm == 0 else n_rows

    def body(w_ref, g_ref, m_ref, v_ref, d_ref, nm_ref, nv_ref):
        gv = g_ref[...]
        nm = ADAM_B1 * m_ref[...] + (1.0 - ADAM_B1) * gv
        nv = ADAM_B2 * v_ref[...] + (1.0 - ADAM_B2) * (gv * gv)
        m_hat = nm / (1.0 - ADAM_B1 ** ADAM_STEP)
        v_hat = nv / (1.0 - ADAM_B2 ** ADAM_STEP)
        d_ref[...] = -ADAM_LR * (m_hat / (jnp.sqrt(v_hat) + ADAM_EPS) + ADAM_WD * w_ref[...])
        nm_ref[...] = nm
        nv_ref[...] = nv

    row = pl.BlockSpec((tm, d), lambda i: (i, 0))
    return pl.pallas_call(
        body, grid=(n_rows // tm,), in_specs=[row] * 4, out_specs=[row] * 3,
        out_shape=[_sds((n_rows, d), F32)] * 3, name=name, compiler_params=_params(("parallel",)))(w, g, m, v)


def _position():
    x, y, c = lax.axis_index("x"), lax.axis_index("y"), lax.axis_index("c")
    others = [(1 - x, y), (x, 1 - y), (1 - x, 1 - y)]
    return x, y, c, others


def _window(ref, kind, chip, half, shard_shape):
    r, n = shard_shape
    if kind == "col":
        return ref.at[pl.ds(pl.multiple_of(half * (r // 2), 16), r // 2), pl.ds(pl.multiple_of(chip * n, 128), n)]
    return ref.at[pl.ds(pl.multiple_of(chip * r, 16), r), pl.ds(pl.multiple_of(half * (n // 2), 128), n // 2)]


def _half(ref, kind, half, shape):
    r, n = shape
    if kind == "col":
        return ref.at[pl.ds(pl.multiple_of(half * (r // 2), 16), r // 2), :]
    return ref.at[:, pl.ds(pl.multiple_of(half * (n // 2), 128), n // 2)]


def swap_halves(name, grads, kinds):
    nt = len(grads)
    shapes = [tuple(g.shape) for g in grads]

    def body(*refs):
        in_refs, out_refs = refs[:nt], refs[nt:2 * nt]
        send_sems, recv_sems = refs[2 * nt:]
        x, y, c, _ = _position()
        cps = []
        for t in range(nt):
            cp = pltpu.make_async_remote_copy(
                src_ref=_half(in_refs[t], kinds[t], 1 - c, shapes[t]), dst_ref=_half(out_refs[t], kinds[t], 1 - c, shapes[t]),
                send_sem=send_sems.at[t], recv_sem=recv_sems.at[t], device_id=(x, y, 1 - c), device_id_type=MESH)
            cp.start()
            cps.append(cp)
        for t in range(nt):
            mine = _half(out_refs[t], kinds[t], c, shapes[t])
            pltpu.make_async_remote_copy(
                src_ref=mine, dst_ref=mine, send_sem=send_sems.at[t], recv_sem=recv_sems.at[t],
                device_id=(x, y, 1 - c), device_id_type=MESH).wait_recv()
        for cp in cps:
            cp.wait_send()

    hbm = pl.BlockSpec(memory_space=pl.ANY)
    return pl.pallas_call(
        body, in_specs=[hbm] * nt, out_specs=[hbm] * nt, out_shape=[_sds(s, BF16) for s in shapes],
        scratch_shapes=[pltpu.SemaphoreType.DMA((nt,)), pltpu.SemaphoreType.DMA((nt,))],
        name=name, compiler_params=_params())(*grads)


def _half_spec(kind, shape, tiles):
    r, n = shape
    if kind == "col":
        tn = n // tiles
        return pl.BlockSpec((r // 2, tn), lambda i, s: (s[0], i))
    tm = r // tiles
    return pl.BlockSpec((tm, n // 2), lambda i, s: (i, s[0]))


def add_halves(name, mine, landed, kind, where, tiles=4):
    shape = tuple(mine.shape)
    r, n = shape
    out_shape = (r // 2, n) if kind == "col" else (r, n // 2)
    out_spec = (pl.BlockSpec((r // 2, n // tiles), lambda i, s: (0, i)) if kind == "col"
                else pl.BlockSpec((r // tiles, n // 2), lambda i, s: (i, 0)))

    def body(s_ref, a_ref, b_ref, o_ref):
        o_ref[...] = (a_ref[...].astype(F32) + b_ref[...].astype(F32)).astype(BF16)

    spec = _half_spec(kind, shape, tiles)
    return pl.pallas_call(
        body, grid_spec=pltpu.PrefetchScalarGridSpec(num_scalar_prefetch=1, grid=(tiles,), in_specs=[spec, spec],
                                                     out_specs=out_spec),
        out_shape=_sds(out_shape, BF16), name=name, compiler_params=_params(("parallel",)))(where, mine, landed)


def sum_shards(name, part, landed, kind, shard_shape, where, layer, n_layers, into=None, tiles=2):
    r, n = shard_shape
    if kind == "col":
        tm, width = r // 2 // tiles, n
        own = pl.BlockSpec((tm, n), lambda i, s: (i, s[1]))
        out = pl.BlockSpec((None, tm, n), lambda i, s: (layer, s[0] * tiles + i, 0))
    else:
        tm, width = r // tiles, n // 2
        own = pl.BlockSpec((tm, n // 2), lambda i, s: (s[1] * tiles + i, 0))
        out = pl.BlockSpec((None, tm, n // 2), lambda i, s: (layer, i, s[0]))

    def body(s_ref, a_ref, l_ref, *o_refs):
        o_refs[-1][...] = ((a_ref[...].astype(F32) + l_ref[0].astype(F32)) + l_ref[1].astype(F32)) + l_ref[2].astype(F32)

    in_specs = [own, pl.BlockSpec((3, tm, width), lambda i, s: (0, i, 0))]
    args, aliases = [where, part, landed], {}
    if into is not None:
        in_specs.append(pl.BlockSpec(memory_space=pl.ANY))
        args.append(into)
        aliases = {3: 0}
    return pl.pallas_call(
        body, grid_spec=pltpu.PrefetchScalarGridSpec(num_scalar_prefetch=1, grid=(tiles,), in_specs=in_specs, out_specs=out),
        out_shape=_sds((n_layers, r, n), F32), input_output_aliases=aliases, name=name,
        compiler_params=_params(("parallel",)))(*args)


def share_halves(arrays, entries):
    na, nt = len(arrays), len(entries)

    def body(*refs):
        out_refs = refs[na:2 * na]
        send_sems, recv_sems = refs[2 * na:]
        x, y, c, _ = _position()
        cps = []
        for t, (a, layer, kind) in enumerate(entries):
            shape = tuple(arrays[a].shape[1:])
            mine = _half(out_refs[a].at[layer], kind, c, shape)
            cp = pltpu.make_async_remote_copy(
                src_ref=mine, dst_ref=mine, send_sem=send_sems.at[t], recv_sem=recv_sems.at[t],
                device_id=(x, y, 1 - c), device_id_type=MESH)
            cp.start()
            cps.append(cp)
        for t, (a, layer, kind) in enumerate(entries):
            shape = tuple(arrays[a].shape[1:])
            other = _half(out_refs[a].at[layer], kind, 1 - c, shape)
            pltpu.make_async_remote_copy(
                src_ref=other, dst_ref=other, send_sem=send_sems.at[t], recv_sem=recv_sems.at[t],
                device_id=(x, y, 1 - c), device_id_type=MESH).wait_recv()
        for cp in cps:
            cp.wait_send()

    hbm = pl.BlockSpec(memory_space=pl.ANY)
    return pl.pallas_call(
        body, in_specs=[hbm] * na, out_specs=[hbm] * na, out_shape=[_sds(a.shape, F32) for a in arrays],
        input_output_aliases={i: i for i in range(na)},
        scratch_shapes=[pltpu.SemaphoreType.DMA((nt,)), pltpu.SemaphoreType.DMA((nt,))],
        name="share_halves", compiler_params=_params())(*arrays)


HBM_SPEC = pl.BlockSpec(memory_space=pltpu.HBM)
SEM_SPEC = pl.BlockSpec(memory_space=pltpu.SEMAPHORE)
ANY_SPEC = pl.BlockSpec(memory_space=pl.ANY)


def _split_params():
    return pltpu.CompilerParams(has_side_effects=pltpu.SideEffectType.DATAFLOW_SIDE_EFFECTING,
                                vmem_limit_bytes=VMEM_LIMIT_BYTES)


def _in_hbm(a):
    return pltpu.with_memory_space_constraint(a, pltpu.HBM)


def cast_place(name, shard, layer, kind, where, tiles=2):
    _, r, n = shard.shape
    tm = r // tiles
    if kind == "col":
        full, out = (r, 4 * n), pl.BlockSpec((tm, n), lambda i, s: (i, s[1]))
    else:
        full, out = (4 * r, n), pl.BlockSpec((tm, n), lambda i, s: (s[1] * tiles + i, 0))

    def body(s_ref, w_ref, o_ref):
        o_ref[...] = w_ref[...].astype(BF16)

    return pl.pallas_call(
        body, grid_spec=pltpu.PrefetchScalarGridSpec(
            num_scalar_prefetch=1, grid=(tiles,), in_specs=[pl.BlockSpec((None, tm, n), lambda i, s: (layer, i, 0))],
            out_specs=out),
        out_shape=_sds(full, BF16), name=name, compiler_params=_params(("parallel",)))(where, shard)


def gather_start(name, fulls, kinds, shard_shapes, after):
    nt = len(fulls)
    na = 0 if after is None else 1

    def body(*refs):
        full_refs = refs[:nt]
        send_sems, recv_sems, token = refs[nt + na], refs[nt + na + 1], refs[-1]
        x, y, c, others = _position()
        for t in range(nt):
            mine = _window(full_refs[t], kinds[t], 2 * x + y, c, shard_shapes[t])
            for j, (ox, oy) in enumerate(others):
                pltpu.make_async_remote_copy(
                    src_ref=mine, dst_ref=mine, send_sem=send_sems.at[3 * t + j], recv_sem=recv_sems.at[3 * t + j],
                    device_id=(ox, oy, c), device_id_type=MESH).start()
        token[...] = jnp.zeros_like(token)

    sems = pltpu.SemaphoreType.DMA((3 * nt,))
    out = pl.pallas_call(
        body, name=name, in_specs=[HBM_SPEC] * nt + [ANY_SPEC] * na,
        out_specs=(SEM_SPEC, SEM_SPEC, *[HBM_SPEC] * nt, pl.BlockSpec(memory_space=pltpu.VMEM)),
        out_shape=(sems, sems, *[pltpu.HBM(f.shape, f.dtype) for f in fulls], _sds((8, 128), F32)),
        input_output_aliases={t: 2 + t for t in range(nt)}, compiler_params=_split_params(),
    )(*[_in_hbm(f) for f in fulls], *([] if after is None else [after]))
    return out[0], out[1], list(out[2:2 + nt]), out[-1]


def gather_wait(name, send_sems, recv_sems, fulls, kinds, shard_shapes, after):
    nt = len(fulls)

    def body(*refs):
        full_refs, send_ref, recv_ref = refs[:nt], refs[nt], refs[nt + 1]
        x, y, c, others = _position()
        for t in range(nt):
            mine = _window(full_refs[t], kinds[t], 2 * x + y, c, shard_shapes[t])
            for j, (ox, oy) in enumerate(others):
                cp = pltpu.make_async_remote_copy(
                    src_ref=mine, dst_ref=_window(full_refs[t], kinds[t], 2 * ox + oy, c, shard_shapes[t]),
                    send_sem=send_ref.at[3 * t + j], recv_sem=recv_ref.at[3 * t + j],
                    device_id=(ox, oy, c), device_id_type=MESH)
                cp.wait_send()
                cp.wait_recv()

    out = pl.pallas_call(
        body, name=name, in_specs=[HBM_SPEC] * nt + [SEM_SPEC, SEM_SPEC, HBM_SPEC], out_specs=[HBM_SPEC] * nt,
        out_shape=[pltpu.HBM(f.shape, f.dtype) for f in fulls], input_output_aliases={t: t for t in range(nt)},
        compiler_params=_split_params())(*fulls, send_sems, recv_sems, _in_hbm(after))
    return list(out)


def forward_halves(name, fulls, kinds, shard_shapes):
    nt = len(fulls)

    def body(*refs):
        out_refs = refs[nt:2 * nt]
        send_sems, recv_sems = refs[2 * nt:]
        x, y, c, others = _position()
        cps = []
        for t in range(nt):
            for j, (ox, oy) in enumerate(others):
                landed = _window(out_refs[t], kinds[t], 2 * ox + oy, c, shard_shapes[t])
                cp = pltpu.make_async_remote_copy(
                    src_ref=landed, dst_ref=landed, send_sem=send_sems.at[3 * t + j], recv_sem=recv_sems.at[3 * t + j],
                    device_id=(x, y, 1 - c), device_id_type=MESH)
                cp.start()
                cps.append(cp)
        for t in range(nt):
            for j, (ox, oy) in enumerate(others):
                got = _window(out_refs[t], kinds[t], 2 * ox + oy, 1 - c, shard_shapes[t])
                pltpu.make_async_remote_copy(
                    src_ref=got, dst_ref=got, send_sem=send_sems.at[3 * t + j], recv_sem=recv_sems.at[3 * t + j],
                    device_id=(x, y, 1 - c), device_id_type=MESH).wait_recv()
        for cp in cps:
            cp.wait_send()

    out = pl.pallas_call(
        body, in_specs=[ANY_SPEC] * nt, out_specs=[ANY_SPEC] * nt, out_shape=[_sds(f.shape, f.dtype) for f in fulls],
        input_output_aliases={t: t for t in range(nt)},
        scratch_shapes=[pltpu.SemaphoreType.DMA((3 * nt,)), pltpu.SemaphoreType.DMA((3 * nt,))],
        name=name, compiler_params=_params())(*fulls)
    return list(out)


def _piece(ref, kind, chip, shard_shape):
    r, n = shard_shape
    if kind == "col":
        return ref.at[:, pl.ds(pl.multiple_of(chip * n, 128), n)]
    return ref.at[pl.ds(pl.multiple_of(chip * r, 16), r), :]


def _piece_shape(kind, shard_shape):
    r, n = shard_shape
    return (r // 2, n) if kind == "col" else (r, n // 2)


def exchange_start(name, parts, kinds, shard_shapes):
    nt = len(parts)
    lands = [lax.empty((3,) + _piece_shape(kinds[t], shard_shapes[t]), BF16) for t in range(nt)]

    def body(*refs):
        part_refs, land_refs = refs[:nt], refs[nt:2 * nt]
        send_sems, recv_sems, token = refs[2 * nt], refs[2 * nt + 1], refs[-1]
        x, y, c, others = _position()
        for t in range(nt):
            for j, (ox, oy) in enumerate(others):
                pltpu.make_async_remote_copy(
                    src_ref=_piece(part_refs[t], kinds[t], 2 * ox + oy, shard_shapes[t]), dst_ref=land_refs[t].at[j],
                    send_sem=send_sems.at[3 * t + j], recv_sem=recv_sems.at[3 * t + j],
                    device_id=(ox, oy, c), device_id_type=MESH).start()
        token[...] = jnp.zeros_like(token)

    sems = pltpu.SemaphoreType.DMA((3 * nt,))
    both = list(parts) + lands
    out = pl.pallas_call(
        body, name=name, in_specs=[HBM_SPEC] * (2 * nt),
        out_specs=(SEM_SPEC, SEM_SPEC, *[HBM_SPEC] * (2 * nt), pl.BlockSpec(memory_space=pltpu.VMEM)),
        out_shape=(sems, sems, *[pltpu.HBM(a.shape, a.dtype) for a in both], _sds((8, 128), F32)),
        input_output_aliases={t: 2 + t for t in range(2 * nt)}, compiler_params=_split_params(),
    )(*[_in_hbm(a) for a in both])
    return out[0], out[1], list(out[2:2 + nt]), list(out[2 + nt:2 + 2 * nt]), out[-1]


def exchange_wait(name, send_sems, recv_sems, parts, lands, kinds, shard_shapes, after):
    nt = len(parts)

    def body(*refs):
        part_refs, land_refs = refs[:nt], refs[nt:2 * nt]
        send_ref, recv_ref = refs[2 * nt], refs[2 * nt + 1]
        x, y, c, others = _position()
        for t in range(nt):
            for j, (ox, oy) in enumerate(others):
                cp = pltpu.make_async_remote_copy(
                    src_ref=_piece(part_refs[t], kinds[t], 2 * ox + oy, shard_shapes[t]), dst_ref=land_refs[t].at[j],
                    send_sem=send_ref.at[3 * t + j], recv_sem=recv_ref.at[3 * t + j],
                    device_id=(ox, oy, c), device_id_type=MESH)
                cp.wait_send()
                cp.wait_recv()

    both = list(parts) + list(lands)
    out = pl.pallas_call(
        body, name=name, in_specs=[HBM_SPEC] * (2 * nt) + [SEM_SPEC, SEM_SPEC, HBM_SPEC], out_specs=[HBM_SPEC] * (2 * nt),
        out_shape=[pltpu.HBM(a.shape, a.dtype) for a in both], input_output_aliases={t: t for t in range(2 * nt)},
        compiler_params=_split_params())(*both, send_sems, recv_sems, _in_hbm(after))
    return list(out[:nt]), list(out[nt:])


def all_reduce_small(name, buf):
    rows, width = buf.shape
    half = rows // 2

    def body(in_ref, out_ref, land, send_sems, recv_sems):
        x, y, c, _ = _position()
        mine = pl.ds(pl.multiple_of(c * half, 8), half)
        other = pl.ds(pl.multiple_of((1 - c) * half, 8), half)
        out_ref[mine, :] = in_ref[mine, :]
        steps = [(in_ref.at[other], (x, y, 1 - c)), (out_ref.at[mine], (1 - x, y, c)), (out_ref.at[mine], (x, 1 - y, c))]
        for s, (src, peer) in enumerate(steps):
            cp = pltpu.make_async_remote_copy(
                src_ref=src, dst_ref=land.at[s], send_sem=send_sems.at[s], recv_sem=recv_sems.at[s],
                device_id=peer, device_id_type=MESH)
            cp.start()
            cp.wait()
            out_ref[mine, :] = out_ref[mine, :] + land[s]
        cp = pltpu.make_async_remote_copy(
            src_ref=out_ref.at[mine], dst_ref=out_ref.at[mine], send_sem=send_sems.at[3], recv_sem=recv_sems.at[3],
            device_id=(x, y, 1 - c), device_id_type=MESH)
        cp.start()
        cp.wait()

    vm = pl.BlockSpec(memory_space=pltpu.VMEM)
    return pl.pallas_call(
        body, in_specs=[vm], out_specs=vm, out_shape=_sds((rows, width), F32),
        scratch_shapes=[pltpu.VMEM((3, half, width), F32), pltpu.SemaphoreType.DMA((4,)), pltpu.SemaphoreType.DMA((4,))],
        name=name, compiler_params=_params())(buf)


def _pack(arrays):
    flat = jnp.concatenate([a.reshape(-1).astype(F32) for a in arrays])
    pad = (-flat.shape[0]) % 2048
    return jnp.pad(flat, (0, pad)).reshape(-1, 128)


def _unpack(buf, like):
    flat = buf.reshape(-1)
    out, off = [], 0
    for a in like:
        size = math.prod(a.shape)
        out.append(flat[off:off + size].reshape(a.shape))
        off += size
    return out


def _local_step(x, target, small, need, emit):
    d = D_MODEL
    full = {}

    def after_token(vec, token):
        return vec if token is None else vec + token[0:1, 0:1]

    lam_r, lam_i, bbar_re, bbar_im = small["s5_disc"]
    rb, rc = _s5_matrices(bbar_re, bbar_im, small["s5_c_re"], small["s5_c_im"])
    rb16, rc16 = rb.astype(BF16), rc.astype(BF16)
    lr_t, li_t = lam_r.reshape(S5_BLOCKS, 8, 128), lam_i.reshape(S5_BLOCKS, 8, 128)
    (u,) = rms_fwd("norm_mix0", x, [small["norm_mix0"]], [F32])
    ge, y2, cs = s5_fwd(u, small["s5_d"], rb16, rc16, lr_t, li_t)
    full.update(need("glu", ge))
    h1, val, gate = mm_nn(
        "glu", ge, full["w_glu"], [0, d], d,
        lambda accs, e, r: [e[0] + (accs[0] + r[0]) * jax.nn.sigmoid(accs[1] + r[1]), accs[0] + r[0], accs[1] + r[1]],
        [F32, F32, F32], extras=[x], rowvecs=[(small["s5_b_glu"], 0), (small["s5_b_glu"], d)])

    def mlp_fwd(tag, h, gain, w_in, w_out):
        (n,) = rms_fwd("norm_mlp" + tag, h, [gain], [BF16])
        a, r = mm_nn("mlp_in" + tag, n, w_in, [0], w_in.shape[1],
                     lambda accs, e, rv: [accs[0], jnp.square(jnp.maximum(accs[0], 0.0))], [F32, BF16], tm=2048)
        (h_out,) = mm_nn("mlp_out" + tag, r, w_out, [0], d, lambda accs, e, rv: [e[0] + accs[0]], [F32], extras=[h])
        return h_out, (n, a, r)

    full.update(need("mlp0", h1))
    h2, mlp0 = mlp_fwd("0", h1, small["norm_mlp0"], full["w_in0"], full["w_out0"])

    full.update(need("rest", h2))
    nkv, n2 = rms_fwd("norm_kv_mix1", h2, [small["norm_kv"], small["norm_mix1"]], [BF16, BF16])
    kvw = 2 * N_KV * HEAD_DIM
    (kv,) = mm_nn("kv_proj", nkv, full["w_kv"], [0], kvw, lambda accs, e, r: [accs[0] + r[0]], [BF16],
                  rowvecs=[(small["b_kv"], 0)])
    (q,) = mm_nn("q_proj", n2, full["w_q"], [0], d, lambda accs, e, r: [accs[0] + r[0]], [BF16],
                 rowvecs=[(small["b_q"], 0)])
    sinks = small["sinks"].reshape(N_Q)
    o = attn_fwd(q, kv, sinks)
    (h3,) = mm_nn("o_proj", o, full["w_o"], [0], d, lambda accs, e, r: [e[0] + accs[0] + r[0]], [F32],
                  extras=[h2], rowvecs=[(small["b_o"], 0)])
    h4, mlp1 = mlp_fwd("1", h3, small["norm_mlp1"], full["w_in1"], full["w_out1"])
    loss_tile, dh, dhb, dg_final = final_loss(h4, target, small["norm_final"])

    grads_small, grads_full = {"norm_final": dg_final}, {}
    ident = lambda acc, e: [acc]
    layer1 = ["w_out1", "w_in1", "w_o", "w_q", "w_kv"]
    layer0 = ["w_out0", "w_in0", "w_glu"]

    def mlp_bwd(tag, dh, dhb, h_in, gain, w_in, w_out, saved):
        n, a, r = saved
        grads_full["w_out" + tag] = mm_tn("dw_out" + tag, r, dhb, tn=1024)
        (da,) = mm_nt("mlp_da" + tag, dhb, w_out, lambda acc, e: [acc * 2.0 * jnp.maximum(e[0], 0.0)], [BF16], extras=[a],
                      tm=2048)
        grads_full["w_in" + tag] = mm_tn("dw_in" + tag, n, da, tn=1024)
        (dn,) = mm_nt("mlp_dn" + tag, da, w_in, ident, [F32])
        dx, dxb, colsum, dg = rms_bwd("norm_mlp_bwd" + tag, h_in, [dn], [gain], dh)
        grads_small["norm_mlp" + tag] = dg
        return dx, dxb, colsum

    dh3, dh3b, colsum3 = mlp_bwd("1", dh, dhb, h3, small["norm_mlp1"], full["w_in1"], full["w_out1"], mlp1)
    grads_small["b_o"] = colsum3
    grads_full["w_o"] = mm_tn("dw_o", o, dh3b)
    (do,) = mm_nt("attn_do", dh3b, full["w_o"], ident, [BF16])
    dq, dbq, dprev, dcur, dsink = attn_bwd(q, kv, do, sinks)
    dkv, dbkv = kv_combine(dprev, dcur)
    grads_small["b_q"], grads_small["b_kv"], grads_small["sinks"] = dbq, dbkv, dsink[:, :N_Q]
    grads_full["w_q"] = mm_tn("dw_q", n2, dq)
    grads_full["w_kv"] = mm_tn("dw_kv", nkv, dkv)
    (dn2,) = mm_nt("attn_dn", dq, full["w_q"], ident, [F32])
    (dnkv,) = mm_nt("kv_dn", dkv, full["w_kv"], ident, [F32])
    token = emit("layer1", {n: grads_full[n] for n in layer1})
    dh2, dh2b, _, dg_mix1, dg_kv = rms_bwd("norm_kv_mix1_bwd", h2, [dn2, dnkv],
                                           [after_token(small["norm_mix1"], token), small["norm_kv"]], dh3)
    grads_small["norm_mix1"], grads_small["norm_kv"] = dg_mix1, dg_kv
    dh1, _, _ = mlp_bwd("0", dh2, dh2b, h1, small["norm_mlp0"], full["w_in0"], full["w_out0"], mlp0)

    dz, db_glu = glu_bwd(dh1, val, gate)
    grads_small["s5_b_glu"] = db_glu
    grads_full["w_glu"] = mm_tn("dw_glu", ge, dz, tn=1024)
    token = emit("layer0", {n: grads_full[n] for n in layer0})
    (dy2,) = mm_nt("glu_dy", dz, full["w_glu"], lambda acc, e: [acc * _gelu_grad(e[0])], [F32], extras=[y2])
    rbt16, rct16 = jnp.swapaxes(rb16, 1, 2), jnp.swapaxes(rc16, 1, 2)
    du, dd, drb, drc, dlr, dli = s5_bwd(u, dy2, after_token(small["s5_d"], token), cs, rb16, rbt16, rct16, lr_t, li_t)
    grads_small["s5_d"] = dd
    grads_small["s5_mats"] = (drb, drc, dlr, dli)
    grad_x, _, _, dg_mix0 = rms_bwd("norm_mix0_bwd", x, [du], [small["norm_mix0"]], dh1)
    grads_small["norm_mix0"] = dg_mix0
    return loss_tile, grad_x, grads_small


SMALL_NAMES = ["norm_mix", "norm_mlp", "norm_kv", "norm_final", "s5_a_re", "s5_a_im", "s5_log_dt", "s5_b_re", "s5_b_im",
               "s5_c_re", "s5_c_im", "s5_d", "s5_b_glu", "b_kv", "b_q", "sinks", "b_o"]
BIG_NAMES = ["s5_w_glu", "w_kv", "w_q", "w_o", "w_mlp_in", "w_mlp_out"]
WEIGHT_ORDER = ["norm_mix", "norm_mlp", "norm_kv", "norm_final", "s5_a_re", "s5_a_im", "s5_log_dt", "s5_b_re", "s5_b_im",
                "s5_c_re", "s5_c_im", "s5_d", "s5_w_glu", "s5_b_glu", "w_kv", "b_kv", "w_q", "b_q", "sinks", "w_o", "b_o",
                "w_mlp_in", "w_mlp_out"]


def kernel(x, norm_mix, norm_mlp, norm_kv, norm_final, s5_a_re, s5_a_im, s5_log_dt, s5_b_re, s5_b_im, s5_c_re, s5_c_im, s5_d, s5_w_glu, s5_b_glu, w_kv, b_kv, w_q, b_q, sinks, w_o, b_o, w_mlp_in, w_mlp_out, loss_target, m_norm_mix, m_norm_mlp, m_norm_kv, m_norm_final, m_s5_a_re, m_s5_a_im, m_s5_log_dt, m_s5_b_re, m_s5_b_im, m_s5_c_re, m_s5_c_im, m_s5_d, m_s5_w_glu, m_s5_b_glu, m_w_kv, m_b_kv, m_w_q, m_b_q, m_sinks, m_w_o, m_b_o, m_w_mlp_in, m_w_mlp_out, v_norm_mix, v_norm_mlp, v_norm_kv, v_norm_final, v_s5_a_re, v_s5_a_im, v_s5_log_dt, v_s5_b_re, v_s5_b_im, v_s5_c_re, v_s5_c_im, v_s5_d, v_s5_w_glu, v_s5_b_glu, v_w_kv, v_b_kv, v_w_q, v_b_q, v_sinks, v_w_o, v_b_o, v_w_mlp_in, v_w_mlp_out):
    env = dict(locals())
    w = {n: env[n] for n in WEIGHT_ORDER}
    mom = {n: env["m_" + n] for n in WEIGHT_ORDER}
    var = {n: env["v_" + n] for n in WEIGHT_ORDER}
    d = D_MODEL
    xi, yi, ci = lax.axis_index("x"), lax.axis_index("y"), lax.axis_index("c")
    chip = 2 * xi + yi
    where = jnp.stack([ci, chip]).astype(jnp.int32)

    dsh, bsh = s5_d.shape[1], s5_b_glu.shape[1]
    placed = jnp.concatenate([
        lax.dynamic_update_slice(jnp.zeros((4 * dsh,), F32), s5_d[0], (chip * dsh,)),
        lax.dynamic_update_slice(jnp.zeros((4 * bsh,), F32), s5_b_glu[0], (chip * bsh,))])
    placed = jnp.pad(placed, (0, (-placed.shape[0]) % 2048))
    placed = jnp.where(ci == 0, placed, 0.0).reshape(-1, 128)
    gathered_rows = all_reduce_small("gather_vectors", placed)
    gathered = gathered_rows.reshape(-1)
    d_full, bglu_full = gathered[:4 * dsh].reshape(1, -1), gathered[4 * dsh:].reshape(1, -1)

    big = [s5_w_glu, w_kv[None], w_q, w_o, w_mlp_in, w_mlp_out]
    entries = [(0, 0, "col"), (1, 0, "row"), (2, 0, "row"), (3, 0, "row"), (4, 0, "col"), (4, 1, "col"),
               (5, 0, "row"), (5, 1, "row")]
    names = ["w_glu", "w_kv", "w_q", "w_o", "w_in0", "w_in1", "w_out0", "w_out1"]
    kinds = dict(zip(names, [k for _, _, k in entries]))
    shard_shapes = dict(zip(names, [tuple(big[a].shape[1:]) for a, _, _ in entries]))

    placed_w = {n: cast_place("cast_" + n, big[a], layer, kind, where) for n, (a, layer, kind) in zip(names, entries)}
    gather_groups = {"glu": ["w_glu"], "mlp0": ["w_in0", "w_out0"], "rest": ["w_kv", "w_q", "w_o", "w_in1", "w_out1"]}
    started, token = {}, gathered_rows
    for group, members in gather_groups.items():
        send, recv, thru, token = gather_start(
            "gather_start_" + group, [placed_w[n] for n in members], [kinds[n] for n in members],
            [shard_shapes[n] for n in members], token)
        started[group] = (send, recv, thru)

    def need(group, after):
        members = gather_groups[group]
        ks, shapes = [kinds[n] for n in members], [shard_shapes[n] for n in members]
        send, recv, thru = started[group]
        landed = gather_wait("gather_wait_" + group, send, recv, thru, ks, shapes, after)
        return dict(zip(members, forward_halves("forward_halves_" + group, landed, ks, shapes)))

    exchanging = {}

    def emit(group, partial):
        members = list(partial)
        ks, shapes = [kinds[n] for n in members], [shard_shapes[n] for n in members]
        landed = swap_halves("swap_halves_" + group, [partial[n] for n in members], ks)
        sums = [add_halves("add_halves_" + n, partial[n], landed[t], ks[t], where) for t, n in enumerate(members)]
        send, recv, parts, lands, tok = exchange_start("exchange_start_" + group, sums, ks, shapes)
        exchanging[group] = (members, send, recv, parts, lands)
        return tok

    disc = lambda *p: _s5_discretise(p[0], p[1], p[2], p[3], p[4])
    disc_args = (s5_a_re[0], s5_a_im[0], s5_log_dt[0], s5_b_re[0], s5_b_im[0])
    disc_out, disc_vjp = jax.vjp(disc, *disc_args)
    small = {
        "norm_mix0": norm_mix[0:1] + token[0:1, 0:1], "norm_mix1": norm_mix[1:2], "norm_mlp0": norm_mlp[0:1], "norm_mlp1": norm_mlp[1:2],
        "norm_kv": norm_kv.reshape(1, d), "norm_final": norm_final.reshape(1, d), "s5_disc": disc_out,
        "s5_c_re": s5_c_re[0], "s5_c_im": s5_c_im[0], "s5_d": d_full, "s5_b_glu": bglu_full,
        "b_kv": b_kv.reshape(1, -1), "b_q": b_q, "sinks": sinks, "b_o": b_o,
    }
    loss_tile, grad_x, gs = _local_step(x[0], loss_target[0], small, need, emit)

    drb, drc, dlr, dli = gs["s5_mats"]
    dbbar_re, dbbar_im, dc_re, dc_im = _s5_matrix_grads(drb, drc)
    pieces = [loss_tile[0:1, 0:1], gs["norm_mix0"], gs["norm_mix1"], gs["norm_mlp0"], gs["norm_mlp1"], gs["norm_kv"],
              gs["norm_final"], dlr, dli, dbbar_re, dbbar_im, dc_re, dc_im, gs["s5_d"], gs["s5_b_glu"], gs["b_kv"],
              gs["b_q"], gs["sinks"], gs["b_o"]]
    summed_buf = all_reduce_small("reduce_small", _pack(pieces))
    summed = _unpack(summed_buf, pieces)
    (loss, g_mix0, g_mix1, g_mlp0, g_mlp1, g_kv, g_final, dlr, dli, dbbar_re, dbbar_im, dc_re, dc_im, g_d, g_bglu,
     g_bkv, g_bq, g_sinks, g_bo) = summed
    g_are, g_aim, g_dt, g_bre, g_bim = disc_vjp((dlr.reshape(S5_GROUPS, S5_STATE), dli.reshape(S5_GROUPS, S5_STATE),
                                                  dbbar_re, dbbar_im))
    grads = {
        "norm_mix": jnp.concatenate([g_mix0, g_mix1]), "norm_mlp": jnp.concatenate([g_mlp0, g_mlp1]),
        "norm_kv": g_kv.reshape(d), "norm_final": g_final.reshape(d), "s5_a_re": g_are[None], "s5_a_im": g_aim[None],
        "s5_log_dt": g_dt[None], "s5_b_re": g_bre[None], "s5_b_im": g_bim[None], "s5_c_re": dc_re[None],
        "s5_c_im": dc_im[None], "s5_d": lax.dynamic_slice(g_d, (0, chip * dsh), (1, dsh)),
        "s5_b_glu": lax.dynamic_slice(g_bglu, (0, chip * bsh), (1, bsh)), "b_kv": g_bkv.reshape(-1), "b_q": g_bq,
        "sinks": g_sinks, "b_o": g_bo,
    }

    reduced = [None] * len(big)
    where_of = dict(zip(names, entries))
    for group, after in (("layer1", grad_x), ("layer0", summed_buf)):
        members, send, recv, parts, lands = exchanging[group]
        ks, shapes = [kinds[n] for n in members], [shard_shapes[n] for n in members]
        parts, lands = exchange_wait("exchange_wait_" + group, send, recv, parts, lands, ks, shapes, after)
        for t, n in enumerate(members):
            a, layer, kind = where_of[n]
            reduced[a] = sum_shards("sum_shards_" + n, parts[t], lands[t], kind, shapes[t], where, layer,
                                    big[a].shape[0], into=reduced[a])
    reduced = share_halves(reduced, entries)
    for n, g in zip(BIG_NAMES, reduced):
        grads[n] = g.reshape(w[n].shape)

    delta, new_m, new_v = {}, {}, {}
    for n in BIG_NAMES:
        flat = lambda a: a.reshape(-1, a.shape[-1])
        dl, nm, nv = adamw("adamw_" + n, flat(w[n]), flat(grads[n]), flat(mom[n]), flat(var[n]))
        delta[n], new_m[n], new_v[n] = dl.reshape(w[n].shape), nm.reshape(w[n].shape), nv.reshape(w[n].shape)
    sw, sg, sm, sv = ([t[n] for n in SMALL_NAMES] for t in (w, grads, mom, var))
    dl, nm, nv = adamw("adamw_small", _pack(sw), _pack(sg), _pack(sm), _pack(sv))
    for n, a, b, c_ in zip(SMALL_NAMES, _unpack(dl, sw), _unpack(nm, sw), _unpack(nv, sw)):
        delta[n], new_m[n], new_v[n] = a, b, c_

    out = [loss.reshape(()), grad_x[None]]
    for table in (grads, delta, new_m, new_v):
        out += [table[n].reshape(w[n].shape) for n in WEIGHT_ORDER]
    return tuple(out)
```

```python
import functools
import math

import jax
import jax.numpy as jnp
from jax import lax
from jax.experimental import pallas as pl
from jax.experimental.pallas import tpu as pltpu

F32 = jnp.float32
BF16 = jnp.bfloat16

D_MODEL = 1024
S5_GROUPS = 64
S5_GROUP = 16
S5_STATE = 64
N_KV = 4
N_Q = 16
HEAD_DIM = 64
BLOCK = 128
NORM_EPS = 1e-5
LAMBDA_RE_MAX = -1e-4
ADAM_LR, ADAM_B1, ADAM_B2, ADAM_EPS, ADAM_WD, ADAM_STEP = 0.001, 0.9, 0.999, 1e-08, 0.01, 10

VMEM_LIMIT_BYTES = 56 * 1024 * 1024
S5_CHUNK = 256
S5_BLOCKS = 4
MESH = pl.DeviceIdType.MESH


def _params(sem=None):
    return pltpu.CompilerParams(dimension_semantics=sem, vmem_limit_bytes=VMEM_LIMIT_BYTES)


def _sds(shape, dtype):
    return jax.ShapeDtypeStruct(shape, dtype)


def _rms_hat(xv):
    r = lax.rsqrt(jnp.mean(xv * xv, axis=-1, keepdims=True) + NORM_EPS)
    return xv * r, r


def rms_fwd(name, x, gains, out_dtypes, tm=256):
    n_rows, d = x.shape
    ng = len(gains)

    def body(x_ref, *refs):
        xh, _ = _rms_hat(x_ref[...])
        for g_ref, o_ref in zip(refs[:ng], refs[ng:]):
            o_ref[...] = (xh * g_ref[...]).astype(o_ref.dtype)

    row = pl.BlockSpec((tm, d), lambda i: (i, 0))
    vec = pl.BlockSpec((1, d), lambda i: (0, 0))
    return pl.pallas_call(
        body, grid=(n_rows // tm,), in_specs=[row] + [vec] * ng, out_specs=[row] * ng,
        out_shape=[_sds((n_rows, d), dt) for dt in out_dtypes], name=name,
        compiler_params=_params(("parallel",)))(x, *gains)


def rms_bwd(name, x, dys, gains, res, tm=256):
    n_rows, d = x.shape
    ng = len(gains)

    def body(x_ref, res_ref, *refs):
        dy_refs, g_refs = refs[:ng], refs[ng:2 * ng]
        dx_ref, dxb_ref, cs_ref = refs[2 * ng:2 * ng + 3]
        dg_refs = refs[2 * ng + 3:]
        i = pl.program_id(0)
        xh, r = _rms_hat(x_ref[...])
        dxh = jnp.zeros_like(xh)
        dgs = []
        for dy_ref, g_ref in zip(dy_refs, g_refs):
            dy = dy_ref[...].astype(F32)
            dxh = dxh + dy * g_ref[...]
            dgs.append(jnp.sum(dy * xh, axis=0, keepdims=True))
        dx = r * (dxh - xh * jnp.mean(dxh * xh, axis=-1, keepdims=True)) + res_ref[...]
        dx_ref[...] = dx
        dxb_ref[...] = dx.astype(BF16)
        cs = jnp.sum(dx, axis=0, keepdims=True)

        @pl.when(i == 0)
        def _():
            cs_ref[...] = jnp.zeros_like(cs_ref)
            for dg_ref in dg_refs:
                dg_ref[...] = jnp.zeros_like(dg_ref)

        cs_ref[...] += cs
        for dg_ref, dg in zip(dg_refs, dgs):
            dg_ref[...] += dg

    row = pl.BlockSpec((tm, d), lambda i: (i, 0))
    vec = pl.BlockSpec((1, d), lambda i: (0, 0))
    return pl.pallas_call(
        body, grid=(n_rows // tm,), in_specs=[row, row] + [row] * ng + [vec] * ng,
        out_specs=[row, row, vec] + [vec] * ng,
        out_shape=[_sds((n_rows, d), F32), _sds((n_rows, d), BF16), _sds((1, d), F32)] + [_sds((1, d), F32)] * ng,
        name=name, compiler_params=_params(("arbitrary",)))(x, res, *dys, *gains)


def mm_nn(name, a, w, col_offsets, n_out, epilogue, out_dtypes, extras=(), rowvecs=(), tm=1024, tn=512):
    m, k = a.shape
    tm, tn = min(tm, m), min(tn, n_out)
    nw, ne, nr = len(col_offsets), len(extras), len(rowvecs)

    def body(a_ref, *refs):
        w_refs, e_refs, r_refs = refs[:nw], refs[nw:nw + ne], refs[nw + ne:nw + ne + nr]
        o_refs = refs[nw + ne + nr:]
        av = a_ref[...]
        accs = [jnp.dot(av, w_ref[...], preferred_element_type=F32) for w_ref in w_refs]
        outs = epilogue(accs, [e[...] for e in e_refs], [r[...] for r in r_refs])
        for o_ref, o in zip(o_refs, outs):
            o_ref[...] = o.astype(o_ref.dtype)

    def wspec(off):
        return pl.BlockSpec((k, tn), lambda j, i, off=off: (0, off // tn + j))

    def rspec(off):
        return pl.BlockSpec((1, tn), lambda j, i, off=off: (0, off // tn + j))

    tile = pl.BlockSpec((tm, tn), lambda j, i: (i, j))
    in_specs = ([pl.BlockSpec((tm, k), lambda j, i: (i, 0))] + [wspec(o) for o in col_offsets]
                + [tile] * ne + [rspec(o) for _, o in rowvecs])
    return pl.pallas_call(
        body, grid=(n_out // tn, m // tm), in_specs=in_specs, out_specs=[tile] * len(out_dtypes),
        out_shape=[_sds((m, n_out), dt) for dt in out_dtypes], name=name,
        compiler_params=_params(("parallel", "parallel")))(a, *([w] * nw), *extras, *[r for r, _ in rowvecs])


def mm_nt(name, g, w, epilogue, out_dtypes, extras=(), tm=512, tk=512):
    m, n = g.shape
    k = w.shape[0]
    tm, tk = min(tm, m), min(tk, k)
    ne = len(extras)

    def body(g_ref, w_ref, *refs):
        e_refs, o_refs = refs[:ne], refs[ne:]
        acc = lax.dot_general(g_ref[...], w_ref[...], (((1,), (1,)), ((), ())), preferred_element_type=F32)
        outs = epilogue(acc, [e[...] for e in e_refs])
        for o_ref, o in zip(o_refs, outs):
            o_ref[...] = o.astype(o_ref.dtype)

    tile = pl.BlockSpec((tm, tk), lambda i, j: (i, j))
    return pl.pallas_call(
        body, grid=(m // tm, k // tk),
        in_specs=[pl.BlockSpec((tm, n), lambda i, j: (i, 0)), pl.BlockSpec((tk, n), lambda i, j: (j, 0))] + [tile] * ne,
        out_specs=[tile] * len(out_dtypes), out_shape=[_sds((m, k), dt) for dt in out_dtypes], name=name,
        compiler_params=_params(("parallel", "parallel")))(g, w, *extras)


def mm_tn(name, a, g, tk=512, tn=512):
    m, k = a.shape
    n = g.shape[1]
    tk, tn = min(tk, k), min(tn, n)

    def body(a_ref, g_ref, o_ref):
        acc = lax.dot_general(a_ref[...], g_ref[...], (((0,), (0,)), ((), ())), preferred_element_type=F32)
        o_ref[...] = acc.astype(o_ref.dtype)

    return pl.pallas_call(
        body, grid=(k // tk, n // tn),
        in_specs=[pl.BlockSpec((m, tk), lambda i, j: (0, i)), pl.BlockSpec((m, tn), lambda i, j: (0, j))],
        out_specs=pl.BlockSpec((tk, tn), lambda i, j: (i, j)), out_shape=_sds((k, n), BF16), name=name,
        compiler_params=_params(("parallel", "parallel")))(a, g)


def _row_mask(tc):
    row = lax.broadcasted_iota(jnp.int32, (8 * tc, 256), 0) % 8
    col = lax.broadcasted_iota(jnp.int32, (8 * tc, 256), 1) // 32
    return row == col


def _expand_rows(expand_ref, val, mask):
    rep = jnp.dot(expand_ref[...], val.astype(BF16), preferred_element_type=F32)
    return jnp.where(mask, rep, 0.0).astype(BF16)


def _staged(ref):
    return jnp.concatenate([ref[0], ref[1]], axis=1)


def _stage(ref, val):
    ref[0] = val[:, 0:128]
    ref[1] = val[:, 128:256]


def _gather_rows(src_ref, tc):
    halves = []
    for half in range(2):
        col = lax.broadcasted_iota(jnp.int32, (tc, 128), 1) // 32 + 4 * half
        out = jnp.zeros((tc, 128), F32)
        for s8 in range(4 * half, 4 * half + 4):
            out = jnp.where(col == s8, src_ref.at[half][pl.ds(s8, tc, stride=8), :], out)
        halves.append(out)
    return jnp.concatenate(halves, axis=1)


def _gelu(x):
    c = math.sqrt(2.0 / math.pi)
    return 0.5 * x * (1.0 + jnp.tanh(c * (x + 0.044715 * x * x * x)))


def _gelu_grad(x):
    c = math.sqrt(2.0 / math.pi)
    t = jnp.tanh(c * (x + 0.044715 * x * x * x))
    return 0.5 * (1.0 + t) + 0.5 * x * (1.0 - t * t) * c * (1.0 + 3.0 * 0.044715 * x * x)


def _expansion(tc):
    return (jnp.arange(8 * tc)[:, None] // 8 == jnp.arange(tc)[None, :]).astype(BF16)


def s5_fwd(u, d_skip, rb, rc, lam_r, lam_i):
    n_rows = u.shape[0]
    tc = min(S5_CHUNK, n_rows)
    nc = n_rows // tc

    def body(u_ref, d_ref, ex_ref, rb_ref, rc_ref, lr_ref, li_ref, ge_ref, y2_ref, cs_ref, bux, yrows, carry):
        i = pl.program_id(0)

        @pl.when(i == 0)
        def _():
            carry[...] = jnp.zeros_like(carry)

        cs_ref[0] = carry[...]
        mask = _row_mask(tc)
        for blk in range(S5_BLOCKS):
            lhs = _expand_rows(ex_ref, u_ref[:, blk * 256:(blk + 1) * 256], mask)
            bux[blk] = jnp.dot(lhs, rb_ref[blk], preferred_element_type=F32)
        lam = [(lr_ref[blk], li_ref[blk]) for blk in range(S5_BLOCKS)]

        def step(t, c):
            r0 = pl.multiple_of(t * 8, 8)
            new = []
            for blk in range(S5_BLOCKS):
                xr, xi = c[2 * blk], c[2 * blk + 1]
                lr, li = lam[blk]
                nr = lr * xr - li * xi + bux[blk, pl.ds(r0, 8), 0:128]
                ni = lr * xi + li * xr + bux[blk, pl.ds(r0, 8), 128:256]
                bux[blk, pl.ds(r0, 8), 0:128] = nr
                bux[blk, pl.ds(r0, 8), 128:256] = ni
                new += [nr, ni]
            return tuple(new)

        c0 = []
        for blk in range(S5_BLOCKS):
            c0 += [carry[blk, :, 0:128], carry[blk, :, 128:256]]
        cn = lax.fori_loop(0, tc, step, tuple(c0), unroll=4)
        for blk in range(S5_BLOCKS):
            carry[blk, :, 0:128] = cn[2 * blk]
            carry[blk, :, 128:256] = cn[2 * blk + 1]
        for blk in range(S5_BLOCKS):
            _stage(yrows, jnp.dot(bux[blk].astype(BF16), rc_ref[blk], preferred_element_type=F32))
            sl = slice(blk * 256, (blk + 1) * 256)
            y2 = _gather_rows(yrows, tc) + d_ref[:, sl] * u_ref[:, sl]
            y2_ref[:, sl] = y2
            ge_ref[:, sl] = _gelu(y2).astype(BF16)

    row = pl.BlockSpec((tc, D_MODEL), lambda i: (i, 0))
    mat = pl.BlockSpec((S5_BLOCKS, 256, 256), lambda i: (0, 0, 0))
    lamspec = pl.BlockSpec((S5_BLOCKS, 8, 128), lambda i: (0, 0, 0))
    return pl.pallas_call(
        body, grid=(nc,),
        in_specs=[row, pl.BlockSpec((1, D_MODEL), lambda i: (0, 0)), pl.BlockSpec((8 * tc, tc), lambda i: (0, 0)),
                  mat, mat, lamspec, lamspec],
        out_specs=[row, row, pl.BlockSpec((1, S5_BLOCKS, 8, 256), lambda i: (i, 0, 0, 0))],
        out_shape=[_sds((n_rows, D_MODEL), BF16), _sds((n_rows, D_MODEL), F32), _sds((nc, S5_BLOCKS, 8, 256), F32)],
        scratch_shapes=[pltpu.VMEM((S5_BLOCKS, 8 * tc, 256), F32), pltpu.VMEM((2, 8 * tc, 128), F32),
                        pltpu.VMEM((S5_BLOCKS, 8, 256), F32)],
        name="s5_fwd", compiler_params=_params(("arbitrary",)))(u, d_skip, _expansion(tc), rb, rc, lam_r, lam_i)


def s5_bwd(u, dy2, d_skip, cs, rb, rbt, rct, lam_r, lam_i):
    n_rows = u.shape[0]
    tc = min(S5_CHUNK, n_rows)
    nc = n_rows // tc

    def body(u_ref, dy_ref, d_ref, cs_ref, ex_ref, rb_ref, rbt_ref, rct_ref, lr_ref, li_ref,
             du_ref, dd_ref, drb_ref, drc_ref, dlr_ref, dli_ref, tmp, lhsu, lhsd, xs, adj, acarry):
        i = pl.program_id(0)

        @pl.when(i == 0)
        def _():
            acarry[...] = jnp.zeros_like(acarry)
            dd_ref[...] = jnp.zeros_like(dd_ref)
            drb_ref[...] = jnp.zeros_like(drb_ref)
            drc_ref[...] = jnp.zeros_like(drc_ref)
            dlr_ref[...] = jnp.zeros_like(dlr_ref)
            dli_ref[...] = jnp.zeros_like(dli_ref)

        dd_ref[...] += jnp.sum(dy_ref[...] * u_ref[...], axis=0, keepdims=True)
        mask = _row_mask(tc)
        for blk in range(S5_BLOCKS):
            sl = slice(blk * 256, (blk + 1) * 256)
            lhsu[blk] = _expand_rows(ex_ref, u_ref[:, sl], mask)
            xs[blk] = jnp.dot(lhsu[blk], rb_ref[blk], preferred_element_type=F32)
            lhsd[blk] = _expand_rows(ex_ref, dy_ref[:, sl], mask)
            adj[blk] = jnp.dot(lhsd[blk], rct_ref[blk], preferred_element_type=F32)
        lam = [(lr_ref[blk], li_ref[blk]) for blk in range(S5_BLOCKS)]

        def fstep(t, c):
            r0 = pl.multiple_of(t * 8, 8)
            new = []
            for blk in range(S5_BLOCKS):
                xr, xi = c[2 * blk], c[2 * blk + 1]
                lr, li = lam[blk]
                nr = lr * xr - li * xi + xs[blk, pl.ds(r0, 8), 0:128]
                ni = lr * xi + li * xr + xs[blk, pl.ds(r0, 8), 128:256]
                xs[blk, pl.ds(r0, 8), 0:128] = nr
                xs[blk, pl.ds(r0, 8), 128:256] = ni
                new += [nr, ni]
            return tuple(new)

        c0 = []
        for blk in range(S5_BLOCKS):
            c0 += [cs_ref[0, blk, :, 0:128], cs_ref[0, blk, :, 128:256]]
        lax.fori_loop(0, tc, fstep, tuple(c0), unroll=4)

        def bstep(k, c):
            t = tc - 1 - k
            r0 = pl.multiple_of(t * 8, 8)
            rp = pl.multiple_of(jnp.maximum(t - 1, 0) * 8, 8)
            first = t == 0
            new_a, new_g = [], []
            for blk in range(S5_BLOCKS):
                ar, ai = c[0][2 * blk], c[0][2 * blk + 1]
                glr, gli = c[1][2 * blk], c[1][2 * blk + 1]
                lr, li = lam[blk]
                nr = lr * ar + li * ai + adj[blk, pl.ds(r0, 8), 0:128]
                ni = lr * ai - li * ar + adj[blk, pl.ds(r0, 8), 128:256]
                adj[blk, pl.ds(r0, 8), 0:128] = nr
                adj[blk, pl.ds(r0, 8), 128:256] = ni
                pr = jnp.where(first, cs_ref[0, blk, :, 0:128], xs[blk, pl.ds(rp, 8), 0:128])
                pi = jnp.where(first, cs_ref[0, blk, :, 128:256], xs[blk, pl.ds(rp, 8), 128:256])
                new_a += [nr, ni]
                new_g += [glr + nr * pr + ni * pi, gli + ni * pr - nr * pi]
            return tuple(new_a), tuple(new_g)

        a0, g0 = [], []
        for blk in range(S5_BLOCKS):
            a0 += [acarry[blk, :, 0:128], acarry[blk, :, 128:256]]
            g0 += [dlr_ref[blk], dli_ref[blk]]
        an, gn = lax.fori_loop(0, tc, bstep, (tuple(a0), tuple(g0)), unroll=2)
        for blk in range(S5_BLOCKS):
            acarry[blk, :, 0:128] = an[2 * blk]
            acarry[blk, :, 128:256] = an[2 * blk + 1]
            dlr_ref[blk] = gn[2 * blk]
            dli_ref[blk] = gn[2 * blk + 1]
        for blk in range(S5_BLOCKS):
            sl = slice(blk * 256, (blk + 1) * 256)
            ab = adj[blk].astype(BF16)
            _stage(tmp, jnp.dot(ab, rbt_ref[blk], preferred_element_type=F32))
            du_ref[:, sl] = _gather_rows(tmp, tc) + d_ref[:, sl] * dy_ref[:, sl]
            drb_ref[blk] += lax.dot_general(lhsu[blk], ab, (((0,), (0,)), ((), ())), preferred_element_type=F32)
            drc_ref[blk] += lax.dot_general(lhsd[blk], xs[blk].astype(BF16), (((0,), (0,)), ((), ())),
                                            preferred_element_type=F32)

    rev = pl.BlockSpec((tc, D_MODEL), lambda i: (nc - 1 - i, 0))
    vec = pl.BlockSpec((1, D_MODEL), lambda i: (0, 0))
    mat = pl.BlockSpec((S5_BLOCKS, 256, 256), lambda i: (0, 0, 0))
    lamspec = pl.BlockSpec((S5_BLOCKS, 8, 128), lambda i: (0, 0, 0))
    big = pltpu.VMEM((S5_BLOCKS, 8 * tc, 256), F32)
    bigb = pltpu.VMEM((S5_BLOCKS, 8 * tc, 256), BF16)
    return pl.pallas_call(
        body, grid=(nc,),
        in_specs=[rev, rev, vec, pl.BlockSpec((1, S5_BLOCKS, 8, 256), lambda i: (nc - 1 - i, 0, 0, 0)),
                  pl.BlockSpec((8 * tc, tc), lambda i: (0, 0)), mat, mat, mat, lamspec, lamspec],
        out_specs=[rev, vec, mat, mat, lamspec, lamspec],
        out_shape=[_sds((n_rows, D_MODEL), F32), _sds((1, D_MODEL), F32), _sds((S5_BLOCKS, 256, 256), F32),
                   _sds((S5_BLOCKS, 256, 256), F32), _sds((S5_BLOCKS, 8, 128), F32), _sds((S5_BLOCKS, 8, 128), F32)],
        scratch_shapes=[pltpu.VMEM((2, 8 * tc, 128), F32), bigb, bigb, big, big, pltpu.VMEM((S5_BLOCKS, 8, 256), F32)],
        name="s5_bwd", compiler_params=_params(("arbitrary",)))(u, dy2, d_skip, cs, _expansion(tc), rb, rbt, rct, lam_r, lam_i)


def _s5_discretise(a_re, a_im, log_dt, b_re, b_im):
    lam = lax.complex(jnp.minimum(a_re, LAMBDA_RE_MAX), a_im)
    dt = jnp.exp(log_dt)[:, None]
    lam_bar = jnp.exp(lam * dt)
    b_bar = ((lam_bar - 1.0) / lam)[:, :, None] * lax.complex(b_re, b_im)
    return jnp.real(lam_bar), jnp.imag(lam_bar), jnp.real(b_bar), jnp.imag(b_bar)


def _s5_matrices(bbar_re, bbar_im, c_re, c_im):
    eye2 = jnp.eye(2, dtype=F32)
    bst = jnp.stack([bbar_re, bbar_im]).reshape(2, S5_BLOCKS, 8, 2, S5_STATE, S5_GROUP)
    bt = jnp.transpose(bst, (1, 2, 3, 5, 0, 4))
    rb = (bt[:, :, :, :, :, None, :] * eye2[None, None, :, None, None, :, None]).reshape(S5_BLOCKS, 256, 256)
    cst = jnp.stack([c_re, -c_im]).reshape(2, S5_BLOCKS, 8, 2, S5_GROUP, S5_STATE)
    ct = jnp.transpose(cst, (1, 0, 5, 2, 3, 4))
    rc = (ct[:, :, None, :, :, :, :] * eye2[None, None, :, None, None, :, None]).reshape(S5_BLOCKS, 256, 256)
    return rb, rc


def s5_compact(drb, drct, dlr, dli):
    def body(drb_ref, drct_ref, dlr_ref, dli_ref, o_ref):
        even = (lax.broadcasted_iota(jnp.int32, (256, 64), 0) // S5_GROUP) % 2 == 0
        for blk in range(S5_BLOCKS):
            for k, ref in enumerate((drb_ref, drct_ref)):
                m = ref[blk]
                re = jnp.where(even, m[:, 0:64], m[:, 64:128])
                im = jnp.where(even, m[:, 128:192], m[:, 192:256])
                o_ref[pl.ds(k * 1024 + blk * 256, 256), :] = jnp.concatenate([re, im], axis=1)
            o_ref[pl.ds(2048 + blk * 8, 8), :] = dlr_ref[blk]
            o_ref[pl.ds(2080 + blk * 8, 8), :] = dli_ref[blk]

    vm = pl.BlockSpec(memory_space=pltpu.VMEM)
    return pl.pallas_call(body, in_specs=[vm] * 4, out_specs=vm, out_shape=_sds((2112, 128), F32), name="s5_compact",
                          compiler_params=_params())(drb, drct, dlr, dli)


def _s5_unpack(mats):
    bm = mats[0:1024].reshape(S5_GROUPS, S5_GROUP, 128)
    cm = mats[1024:2048].reshape(S5_GROUPS, S5_GROUP, 128)
    swap = lambda t: jnp.transpose(t, (0, 2, 1))
    return (swap(bm[:, :, 0:64]), swap(bm[:, :, 64:128]), cm[:, :, 0:64], -cm[:, :, 64:128],
            mats[2048:2080].reshape(S5_GROUPS, S5_STATE), mats[2080:2112].reshape(S5_GROUPS, S5_STATE))


NEG = -1e30


GROUP = N_Q // N_KV


def _attn_masks(n):
    qi = lax.broadcasted_iota(jnp.int32, (GROUP * BLOCK, BLOCK), 0) % BLOCK
    kj = lax.broadcasted_iota(jnp.int32, (GROUP * BLOCK, BLOCK), 1)
    return jnp.logical_and(kj > qi, n > 0), kj <= qi


def _stack_heads(ref, kh):
    return jnp.concatenate([ref[:, (GROUP * kh + g) * HEAD_DIM:(GROUP * kh + g + 1) * HEAD_DIM] for g in range(GROUP)], axis=0)


def _unstack_heads(val):
    return jnp.concatenate([val[g * BLOCK:(g + 1) * BLOCK] for g in range(GROUP)], axis=1)


def _sink_column(sink_ref, kh):
    grp = lax.broadcasted_iota(jnp.int32, (GROUP * BLOCK, 1), 0) // BLOCK
    col = jnp.zeros((GROUP * BLOCK, 1), F32)
    for g in range(GROUP):
        col = jnp.where(grp == g, sink_ref[GROUP * kh + g], col)
    return col, grp


def _attn_exp(q4, kp, kc, sink, mask_p, mask_c):
    scale = 1.0 / math.sqrt(HEAD_DIM)
    nt = (((1,), (1,)), ((), ()))
    sp = jnp.where(mask_p, lax.dot_general(q4, kp, nt, preferred_element_type=F32) * scale, NEG)
    sc = jnp.where(mask_c, lax.dot_general(q4, kc, nt, preferred_element_type=F32) * scale, NEG)
    m = jnp.maximum(jnp.maximum(jnp.max(sp, axis=-1, keepdims=True), jnp.max(sc, axis=-1, keepdims=True)), sink)
    pp = jnp.exp(sp - m)
    pc = jnp.exp(sc - m)
    ps = jnp.exp(sink - m)
    inv = 1.0 / (jnp.sum(pp, axis=-1, keepdims=True) + jnp.sum(pc, axis=-1, keepdims=True) + ps)
    return pp, pc, ps, inv


def attn_fwd(q, kv, sinks):
    n_rows = q.shape[0]
    nb = n_rows // BLOCK

    def body(sink_ref, q_ref, kvp_ref, kvc_ref, o_ref):
        n = pl.program_id(0)
        mask_p, mask_c = _attn_masks(n)
        outs = []
        for kh in range(N_KV):
            ks, vs = slice(kh * HEAD_DIM, (kh + 1) * HEAD_DIM), slice((N_KV + kh) * HEAD_DIM, (N_KV + kh + 1) * HEAD_DIM)
            sink, _ = _sink_column(sink_ref, kh)
            pp, pc, _, inv = _attn_exp(_stack_heads(q_ref, kh), kvp_ref[:, ks], kvc_ref[:, ks], sink, mask_p, mask_c)
            o4 = (jnp.dot(pp.astype(BF16), kvp_ref[:, vs], preferred_element_type=F32)
                  + jnp.dot(pc.astype(BF16), kvc_ref[:, vs], preferred_element_type=F32)) * inv
            outs.append(_unstack_heads(o4))
        o_ref[...] = jnp.concatenate(outs, axis=1).astype(BF16)

    kvw = 2 * N_KV * HEAD_DIM
    return pl.pallas_call(
        body, grid=(nb,),
        in_specs=[pl.BlockSpec(memory_space=pltpu.SMEM), pl.BlockSpec((BLOCK, D_MODEL), lambda n: (n, 0)),
                  pl.BlockSpec((BLOCK, kvw), lambda n: (jnp.maximum(n - 1, 0), 0)), pl.BlockSpec((BLOCK, kvw), lambda n: (n, 0))],
        out_specs=pl.BlockSpec((BLOCK, D_MODEL), lambda n: (n, 0)), out_shape=_sds((n_rows, D_MODEL), BF16),
        name="attn_fwd", compiler_params=_params(("parallel",)))(sinks, q, kv, kv)


def attn_bwd(q, kv, do, sinks):
    n_rows = q.shape[0]
    nb = n_rows // BLOCK
    kvw = 2 * N_KV * HEAD_DIM
    tn = (((0,), (0,)), ((), ()))
    nt = (((1,), (1,)), ((), ()))
    scale = 1.0 / math.sqrt(HEAD_DIM)

    def body(sink_ref, q_ref, kvp_ref, kvc_ref, do_ref, dq_ref, dbq_ref, dprev_ref, dcur_ref, dsink_ref):
        n = pl.program_id(0)
        mask_p, mask_c = _attn_masks(n)
        lane = lax.broadcasted_iota(jnp.int32, (1, D_MODEL), 1)
        dqs, dsink = [], jnp.zeros((1, D_MODEL), F32)
        dkp, dkc, dvp, dvc = [], [], [], []
        for kh in range(N_KV):
            ks, vs = slice(kh * HEAD_DIM, (kh + 1) * HEAD_DIM), slice((N_KV + kh) * HEAD_DIM, (N_KV + kh + 1) * HEAD_DIM)
            q4, do4 = _stack_heads(q_ref, kh), _stack_heads(do_ref, kh)
            kp, kc, vp, vc = kvp_ref[:, ks], kvc_ref[:, ks], kvp_ref[:, vs], kvc_ref[:, vs]
            sink, grp = _sink_column(sink_ref, kh)
            pp, pc, ps, inv = _attn_exp(q4, kp, kc, sink, mask_p, mask_c)
            pp, pc = pp * inv, pc * inv
            dpp = lax.dot_general(do4, vp, nt, preferred_element_type=F32)
            dpc = lax.dot_general(do4, vc, nt, preferred_element_type=F32)
            delta = jnp.sum(pp * dpp, axis=-1, keepdims=True) + jnp.sum(pc * dpc, axis=-1, keepdims=True)
            dsp = (pp * (dpp - delta) * scale).astype(BF16)
            dsc = (pc * (dpc - delta) * scale).astype(BF16)
            dsk = ps * inv * delta
            for g in range(GROUP):
                dsink = dsink + jnp.where(lane == GROUP * kh + g, -jnp.sum(jnp.where(grp == g, dsk, 0.0)), 0.0)
            dqs.append(_unstack_heads(jnp.dot(dsp, kp, preferred_element_type=F32)
                                      + jnp.dot(dsc, kc, preferred_element_type=F32)))
            dkp.append(lax.dot_general(dsp, q4, tn, preferred_element_type=F32))
            dkc.append(lax.dot_general(dsc, q4, tn, preferred_element_type=F32))
            dvp.append(lax.dot_general(pp.astype(BF16), do4, tn, preferred_element_type=F32))
            dvc.append(lax.dot_general(pc.astype(BF16), do4, tn, preferred_element_type=F32))
        dq = jnp.concatenate(dqs, axis=1)
        dq_ref[...] = dq.astype(BF16)
        dprev_ref[0] = jnp.concatenate(dkp + dvp, axis=1)
        dcur_ref[0] = jnp.concatenate(dkc + dvc, axis=1)

        @pl.when(n == 0)
        def _():
            dbq_ref[...] = jnp.zeros_like(dbq_ref)
            dsink_ref[...] = jnp.zeros_like(dsink_ref)

        dbq_ref[...] += jnp.sum(dq, axis=0, keepdims=True)
        dsink_ref[...] += dsink

    blk = pl.BlockSpec((BLOCK, D_MODEL), lambda n: (n, 0))
    part = pl.BlockSpec((1, BLOCK, kvw), lambda n: (n, 0, 0))
    return pl.pallas_call(
        body, grid=(nb,),
        in_specs=[pl.BlockSpec(memory_space=pltpu.SMEM), blk,
                  pl.BlockSpec((BLOCK, kvw), lambda n: (jnp.maximum(n - 1, 0), 0)), pl.BlockSpec((BLOCK, kvw), lambda n: (n, 0)), blk],
        out_specs=[blk, pl.BlockSpec((1, D_MODEL), lambda n: (0, 0)), part, part, pl.BlockSpec((1, D_MODEL), lambda n: (0, 0))],
        out_shape=[_sds((n_rows, D_MODEL), BF16), _sds((1, D_MODEL), F32), _sds((nb, BLOCK, kvw), F32),
                   _sds((nb, BLOCK, kvw), F32), _sds((1, D_MODEL), F32)],
        name="attn_bwd", compiler_params=_params(("arbitrary",)))(sinks, q, kv, kv, do)


def kv_combine(dprev, dcur):
    nb, _, kvw = dprev.shape

    def body(dcur_ref, dnext_ref, dkv_ref, db_ref):
        m = pl.program_id(0)
        dkv = dcur_ref[0] + jnp.where(m + 1 < nb, dnext_ref[0], 0.0)
        dkv_ref[...] = dkv.astype(BF16)

        @pl.when(m == 0)
        def _():
            db_ref[...] = jnp.zeros_like(db_ref)

        db_ref[:, 0:kvw] += jnp.sum(dkv, axis=0, keepdims=True)

    return pl.pallas_call(
        body, grid=(nb,),
        in_specs=[pl.BlockSpec((1, BLOCK, kvw), lambda m: (m, 0, 0)),
                  pl.BlockSpec((1, BLOCK, kvw), lambda m: (jnp.minimum(m + 1, nb - 1), 0, 0))],
        out_specs=[pl.BlockSpec((BLOCK, kvw), lambda m: (m, 0)), pl.BlockSpec((1, D_MODEL), lambda m: (0, 0))],
        out_shape=[_sds((nb * BLOCK, kvw), BF16), _sds((1, D_MODEL), F32)],
        name="kv_combine", compiler_params=_params(("arbitrary",)))(dcur, dprev)


def glu_bwd(dout, val, gate, tm=256):
    n_rows, d = dout.shape

    def body(do_ref, v_ref, g_ref, dz_ref, db_ref):
        i = pl.program_id(0)
        sg = jax.nn.sigmoid(g_ref[...])
        dval = do_ref[...] * sg
        dgate = do_ref[...] * v_ref[...] * sg * (1.0 - sg)
        dz_ref[...] = jnp.concatenate([dval, dgate], axis=1).astype(BF16)

        @pl.when(i == 0)
        def _():
            db_ref[...] = jnp.zeros_like(db_ref)

        db_ref[0:1, :] += jnp.sum(dval, axis=0, keepdims=True)
        db_ref[1:2, :] += jnp.sum(dgate, axis=0, keepdims=True)

    row = pl.BlockSpec((tm, d), lambda i: (i, 0))
    return pl.pallas_call(
        body, grid=(n_rows // tm,), in_specs=[row, row, row],
        out_specs=[pl.BlockSpec((tm, 2 * d), lambda i: (i, 0)), pl.BlockSpec((2, d), lambda i: (0, 0))],
        out_shape=[_sds((n_rows, 2 * d), BF16), _sds((2, d), F32)],
        name="glu_bwd", compiler_params=_params(("arbitrary",)))(dout, val, gate)


def final_loss(h, target, gain, tm=256):
    n_rows, d = h.shape

    def body(h_ref, t_ref, g_ref, loss_ref, dh_ref, dhb_ref, dg_ref):
        i = pl.program_id(0)
        xh, r = _rms_hat(h_ref[...])
        err = xh * g_ref[...] - t_ref[...]
        dy = err * (1.0 / d)
        dxh = dy * g_ref[...]
        dx = r * (dxh - xh * jnp.mean(dxh * xh, axis=-1, keepdims=True))
        dh_ref[...] = dx
        dhb_ref[...] = dx.astype(BF16)

        @pl.when(i == 0)
        def _():
            loss_ref[...] = jnp.zeros_like(loss_ref)
            dg_ref[...] = jnp.zeros_like(dg_ref)

        loss_ref[...] += jnp.full((1, d), 0.5 * jnp.sum(jnp.mean(err * err, axis=-1, keepdims=True)), F32)
        dg_ref[...] += jnp.sum(dy * xh, axis=0, keepdims=True)

    row = pl.BlockSpec((tm, d), lambda i: (i, 0))
    vec = pl.BlockSpec((1, d), lambda i: (0, 0))
    return pl.pallas_call(
        body, grid=(n_rows // tm,), in_specs=[row, row, vec],
        out_specs=[vec, row, row, vec],
        out_shape=[_sds((1, d), F32), _sds((n_rows, d), F32), _sds((n_rows, d), BF16), _sds((1, d), F32)],
        name="final_loss", compiler_params=_params(("arbitrary",)))(h, target, gain)


def adamw(name, w, g, m, v, tm=256):
    n_rows, d = w.shape
    tm = tm if n_rows % tm == 0 else n_rows

    def body(w_ref, g_ref, m_ref, v_ref, d_ref, nm_ref, nv_ref):
        gv = g_ref[...]
        nm = ADAM_B1 * m_ref[...] + (1.0 - ADAM_B1) * gv
        nv = ADAM_B2 * v_ref[...] + (1.0 - ADAM_B2) * (gv * gv)
        m_hat = nm / (1.0 - ADAM_B1 ** ADAM_STEP)
        v_hat = nv / (1.0 - ADAM_B2 ** ADAM_STEP)
        d_ref[...] = -ADAM_LR * (m_hat / (jnp.sqrt(v_hat) + ADAM_EPS) + ADAM_WD * w_ref[...])
        nm_ref[...] = nm
        nv_ref[...] = nv

    row = pl.BlockSpec((tm, d), lambda i: (i, 0))
    return pl.pallas_call(
        body, grid=(n_rows // tm,), in_specs=[row] * 4, out_specs=[row] * 3,
        out_shape=[_sds((n_rows, d), F32)] * 3, name=name, compiler_params=_params(("parallel",)))(w, g, m, v)


def _adam_update(w, g, m, v):
    nm = ADAM_B1 * m + (1.0 - ADAM_B1) * g
    nv = ADAM_B2 * v + (1.0 - ADAM_B2) * (g * g)
    m_hat = nm / (1.0 - ADAM_B1 ** ADAM_STEP)
    v_hat = nv / (1.0 - ADAM_B2 ** ADAM_STEP)
    return -ADAM_LR * (m_hat / (jnp.sqrt(v_hat) + ADAM_EPS) + ADAM_WD * w), nm, nv


def adamw_native(name, ws, gs, ms, vs):
    n = len(ws)

    def body(*refs):
        w_refs, g_refs, m_refs, v_refs = refs[:n], refs[n:2 * n], refs[2 * n:3 * n], refs[3 * n:4 * n]
        d_refs, nm_refs, nv_refs = refs[4 * n:5 * n], refs[5 * n:6 * n], refs[6 * n:7 * n]
        for k in range(n):
            dl, nm, nv = _adam_update(w_refs[k][...], g_refs[k][...], m_refs[k][...], v_refs[k][...])
            d_refs[k][...] = dl
            nm_refs[k][...] = nm
            nv_refs[k][...] = nv

    vm = pl.BlockSpec(memory_space=pltpu.VMEM)
    shapes = [_sds(w.shape, F32) for w in ws]
    out = pl.pallas_call(body, in_specs=[vm] * (4 * n), out_specs=[vm] * (3 * n), out_shape=shapes * 3, name=name,
                         compiler_params=_params())(*ws, *gs, *ms, *vs)
    return list(out[:n]), list(out[n:2 * n]), list(out[2 * n:])


VEC_ROWS = {"norm_mix": 0, "norm_mlp": 2, "norm_kv": 4, "norm_final": 5, "s5_d": 6, "b_q": 7, "b_o": 8, "s5_b_glu": 9,
            "b_kv": 11, "sinks": 12, "loss": 13}


def split_vectors(where, vecs, d_shard, glu_shard):
    kvw = 2 * N_KV * HEAD_DIM
    shapes = {"norm_mix": (2, D_MODEL), "norm_mlp": (2, D_MODEL), "norm_kv": (1, D_MODEL), "norm_final": (1, D_MODEL),
              "s5_d": (1, d_shard), "b_q": (1, D_MODEL), "b_o": (1, D_MODEL), "s5_b_glu": (1, glu_shard), "b_kv": (1, kvw),
              "sinks": (1, N_Q), "loss": (1, 128)}
    names = list(shapes)

    def body(where_ref, v_ref, *o_refs):
        chip = where_ref[1]
        for name, o_ref in zip(names, o_refs):
            r0, (r, n) = VEC_ROWS[name], shapes[name]
            if name == "s5_d":
                g = jnp.zeros((1, n), F32)
                for j in range(4):
                    g = jnp.where(chip == j, v_ref[r0:r0 + 1, j * n:(j + 1) * n], g)
            elif name == "s5_b_glu":
                g = jnp.zeros((1, n), F32)
                for j in range(4):
                    row, col = r0 + (j * n) // D_MODEL, (j * n) % D_MODEL
                    g = jnp.where(chip == j, v_ref[row:row + 1, col:col + n], g)
            else:
                g = v_ref[r0:r0 + r, 0:n]
            o_ref[...] = g

    vm = pl.BlockSpec(memory_space=pltpu.VMEM)
    out = pl.pallas_call(body, in_specs=[pl.BlockSpec(memory_space=pltpu.SMEM), vm], out_specs=[vm] * len(names),
                         out_shape=[_sds(shapes[n], F32) for n in names], name="split_vectors",
                         compiler_params=_params())(where, vecs)
    return dict(zip(names, out))


def _position():
    x, y, c = lax.axis_index("x"), lax.axis_index("y"), lax.axis_index("c")
    others = [(1 - x, y), (x, 1 - y), (1 - x, 1 - y)]
    return x, y, c, others


def _window(ref, kind, chip, half, shard_shape):
    r, n = shard_shape
    if kind == "col":
        return ref.at[pl.ds(pl.multiple_of(half * (r // 2), 16), r // 2), pl.ds(pl.multiple_of(chip * n, 128), n)]
    return ref.at[pl.ds(pl.multiple_of(chip * r, 16), r), pl.ds(pl.multiple_of(half * (n // 2), 128), n // 2)]


def _half(ref, kind, half, shape):
    r, n = shape
    if kind == "col":
        return ref.at[pl.ds(pl.multiple_of(half * (r // 2), 16), r // 2), :]
    return ref.at[:, pl.ds(pl.multiple_of(half * (n // 2), 128), n // 2)]


def swap_halves(name, grads, kinds):
    nt = len(grads)
    shapes = [tuple(g.shape) for g in grads]

    def body(*refs):
        in_refs, out_refs = refs[:nt], refs[nt:2 * nt]
        send_sems, recv_sems = refs[2 * nt:]
        x, y, c, _ = _position()
        cps = []
        for t in range(nt):
            cp = pltpu.make_async_remote_copy(
                src_ref=_half(in_refs[t], kinds[t], 1 - c, shapes[t]), dst_ref=_half(out_refs[t], kinds[t], 1 - c, shapes[t]),
                send_sem=send_sems.at[t], recv_sem=recv_sems.at[t], device_id=(x, y, 1 - c), device_id_type=MESH)
            cp.start()
            cps.append(cp)
        for t in range(nt):
            mine = _half(out_refs[t], kinds[t], c, shapes[t])
            pltpu.make_async_remote_copy(
                src_ref=mine, dst_ref=mine, send_sem=send_sems.at[t], recv_sem=recv_sems.at[t],
                device_id=(x, y, 1 - c), device_id_type=MESH).wait_recv()
        for cp in cps:
            cp.wait_send()

    hbm = pl.BlockSpec(memory_space=pl.ANY)
    return pl.pallas_call(
        body, in_specs=[hbm] * nt, out_specs=[hbm] * nt, out_shape=[_sds(s, BF16) for s in shapes],
        scratch_shapes=[pltpu.SemaphoreType.DMA((nt,)), pltpu.SemaphoreType.DMA((nt,))],
        name=name, compiler_params=_params())(*grads)


def _half_spec(kind, shape, tiles):
    r, n = shape
    if kind == "col":
        tn = n // tiles
        return pl.BlockSpec((r // 2, tn), lambda i, s: (s[0], i))
    tm = r // tiles
    return pl.BlockSpec((tm, n // 2), lambda i, s: (i, s[0]))


def add_halves(name, mine, landed, kind, where, tiles=4):
    shape = tuple(mine.shape)
    r, n = shape
    out_shape = (r // 2, n) if kind == "col" else (r, n // 2)
    out_spec = (pl.BlockSpec((r // 2, n // tiles), lambda i, s: (0, i)) if kind == "col"
                else pl.BlockSpec((r // tiles, n // 2), lambda i, s: (i, 0)))

    def body(s_ref, a_ref, b_ref, o_ref):
        o_ref[...] = (a_ref[...].astype(F32) + b_ref[...].astype(F32)).astype(BF16)

    spec = _half_spec(kind, shape, tiles)
    return pl.pallas_call(
        body, grid_spec=pltpu.PrefetchScalarGridSpec(num_scalar_prefetch=1, grid=(tiles,), in_specs=[spec, spec],
                                                     out_specs=out_spec),
        out_shape=_sds(out_shape, BF16), name=name, compiler_params=_params(("parallel",)))(where, mine, landed)


def sum_shards(name, part, landed, kind, shard_shape, where, layer, n_layers, into=None, tiles=2):
    r, n = shard_shape
    if kind == "col":
        tm, width = r // 2 // tiles, n
        own = pl.BlockSpec((tm, n), lambda i, s: (i, s[1]))
        out = pl.BlockSpec((None, tm, n), lambda i, s: (layer, s[0] * tiles + i, 0))
    else:
        tm, width = r // tiles, n // 2
        own = pl.BlockSpec((tm, n // 2), lambda i, s: (s[1] * tiles + i, 0))
        out = pl.BlockSpec((None, tm, n // 2), lambda i, s: (layer, i, s[0]))

    def body(s_ref, a_ref, l_ref, *o_refs):
        o_refs[-1][...] = ((a_ref[...].astype(F32) + l_ref[0].astype(F32)) + l_ref[1].astype(F32)) + l_ref[2].astype(F32)

    in_specs = [own, pl.BlockSpec((3, tm, width), lambda i, s: (0, i, 0))]
    args, aliases = [where, part, landed], {}
    if into is not None:
        in_specs.append(pl.BlockSpec(memory_space=pl.ANY))
        args.append(into)
        aliases = {3: 0}
    return pl.pallas_call(
        body, grid_spec=pltpu.PrefetchScalarGridSpec(num_scalar_prefetch=1, grid=(tiles,), in_specs=in_specs, out_specs=out),
        out_shape=_sds((n_layers, r, n), F32), input_output_aliases=aliases, name=name,
        compiler_params=_params(("parallel",)))(*args)


def share_halves(arrays, entries):
    na, nt = len(arrays), len(entries)

    def body(*refs):
        out_refs = refs[na:2 * na]
        send_sems, recv_sems = refs[2 * na:]
        x, y, c, _ = _position()
        cps = []
        for t, (a, layer, kind) in enumerate(entries):
            shape = tuple(arrays[a].shape[1:])
            mine = _half(out_refs[a].at[layer], kind, c, shape)
            cp = pltpu.make_async_remote_copy(
                src_ref=mine, dst_ref=mine, send_sem=send_sems.at[t], recv_sem=recv_sems.at[t],
                device_id=(x, y, 1 - c), device_id_type=MESH)
            cp.start()
            cps.append(cp)
        for t, (a, layer, kind) in enumerate(entries):
            shape = tuple(arrays[a].shape[1:])
            other = _half(out_refs[a].at[layer], kind, 1 - c, shape)
            pltpu.make_async_remote_copy(
                src_ref=other, dst_ref=other, send_sem=send_sems.at[t], recv_sem=recv_sems.at[t],
                device_id=(x, y, 1 - c), device_id_type=MESH).wait_recv()
        for cp in cps:
            cp.wait_send()

    hbm = pl.BlockSpec(memory_space=pl.ANY)
    return pl.pallas_call(
        body, in_specs=[hbm] * na, out_specs=[hbm] * na, out_shape=[_sds(a.shape, F32) for a in arrays],
        input_output_aliases={i: i for i in range(na)},
        scratch_shapes=[pltpu.SemaphoreType.DMA((nt,)), pltpu.SemaphoreType.DMA((nt,))],
        name="share_halves", compiler_params=_params())(*arrays)


HBM_SPEC = pl.BlockSpec(memory_space=pltpu.HBM)
SEM_SPEC = pl.BlockSpec(memory_space=pltpu.SEMAPHORE)
ANY_SPEC = pl.BlockSpec(memory_space=pl.ANY)


def _split_params():
    return pltpu.CompilerParams(has_side_effects=pltpu.SideEffectType.DATAFLOW_SIDE_EFFECTING,
                                vmem_limit_bytes=VMEM_LIMIT_BYTES)


def _in_hbm(a):
    return pltpu.with_memory_space_constraint(a, pltpu.HBM)


def cast_place(name, shard, layer, kind, where, tiles=2):
    _, r, n = shard.shape
    tm = r // tiles
    if kind == "col":
        full, out = (r, 4 * n), pl.BlockSpec((tm, n), lambda i, s: (i, s[1]))
    else:
        full, out = (4 * r, n), pl.BlockSpec((tm, n), lambda i, s: (s[1] * tiles + i, 0))

    def body(s_ref, w_ref, o_ref):
        o_ref[...] = w_ref[...].astype(BF16)

    return pl.pallas_call(
        body, grid_spec=pltpu.PrefetchScalarGridSpec(
            num_scalar_prefetch=1, grid=(tiles,), in_specs=[pl.BlockSpec((None, tm, n), lambda i, s: (layer, i, 0))],
            out_specs=out),
        out_shape=_sds(full, BF16), name=name, compiler_params=_params(("parallel",)))(where, shard)


def gather_start(name, fulls, kinds, shard_shapes, after):
    nt = len(fulls)
    na = 0 if after is None else 1

    def body(*refs):
        full_refs = refs[:nt]
        send_sems, recv_sems, token = refs[nt + na], refs[nt + na + 1], refs[-1]
        x, y, c, others = _position()
        for t in range(nt):
            mine = _window(full_refs[t], kinds[t], 2 * x + y, c, shard_shapes[t])
            for j, (ox, oy) in enumerate(others):
                pltpu.make_async_remote_copy(
                    src_ref=mine, dst_ref=mine, send_sem=send_sems.at[3 * t + j], recv_sem=recv_sems.at[3 * t + j],
                    device_id=(ox, oy, c), device_id_type=MESH).start()
        token[...] = jnp.zeros_like(token)

    sems = pltpu.SemaphoreType.DMA((3 * nt,))
    out = pl.pallas_call(
        body, name=name, in_specs=[HBM_SPEC] * nt + [ANY_SPEC] * na,
        out_specs=(SEM_SPEC, SEM_SPEC, *[HBM_SPEC] * nt, pl.BlockSpec(memory_space=pltpu.VMEM)),
        out_shape=(sems, sems, *[pltpu.HBM(f.shape, f.dtype) for f in fulls], _sds((8, 128), F32)),
        input_output_aliases={t: 2 + t for t in range(nt)}, compiler_params=_split_params(),
    )(*[_in_hbm(f) for f in fulls], *([] if after is None else [after]))
    return out[0], out[1], list(out[2:2 + nt]), out[-1]


def gather_wait(name, send_sems, recv_sems, fulls, kinds, shard_shapes, after):
    nt = len(fulls)

    def body(*refs):
        full_refs, send_ref, recv_ref = refs[:nt], refs[nt], refs[nt + 1]
        x, y, c, others = _position()
        for t in range(nt):
            mine = _window(full_refs[t], kinds[t], 2 * x + y, c, shard_shapes[t])
            for j, (ox, oy) in enumerate(others):
                cp = pltpu.make_async_remote_copy(
                    src_ref=mine, dst_ref=_window(full_refs[t], kinds[t], 2 * ox + oy, c, shard_shapes[t]),
                    send_sem=send_ref.at[3 * t + j], recv_sem=recv_ref.at[3 * t + j],
                    device_id=(ox, oy, c), device_id_type=MESH)
                cp.wait_send()
                cp.wait_recv()

    out = pl.pallas_call(
        body, name=name, in_specs=[HBM_SPEC] * nt + [SEM_SPEC, SEM_SPEC, HBM_SPEC], out_specs=[HBM_SPEC] * nt,
        out_shape=[pltpu.HBM(f.shape, f.dtype) for f in fulls], input_output_aliases={t: t for t in range(nt)},
        compiler_params=_split_params())(*fulls, send_sems, recv_sems, _in_hbm(after))
    return list(out)


def forward_halves(name, fulls, kinds, shard_shapes):
    nt = len(fulls)

    def body(*refs):
        out_refs = refs[nt:2 * nt]
        send_sems, recv_sems = refs[2 * nt:]
        x, y, c, others = _position()
        cps = []
        for t in range(nt):
            for j, (ox, oy) in enumerate(others):
                landed = _window(out_refs[t], kinds[t], 2 * ox + oy, c, shard_shapes[t])
                cp = pltpu.make_async_remote_copy(
                    src_ref=landed, dst_ref=landed, send_sem=send_sems.at[3 * t + j], recv_sem=recv_sems.at[3 * t + j],
                    device_id=(x, y, 1 - c), device_id_type=MESH)
                cp.start()
                cps.append(cp)
        for t in range(nt):
            for j, (ox, oy) in enumerate(others):
                got = _window(out_refs[t], kinds[t], 2 * ox + oy, 1 - c, shard_shapes[t])
                pltpu.make_async_remote_copy(
                    src_ref=got, dst_ref=got, send_sem=send_sems.at[3 * t + j], recv_sem=recv_sems.at[3 * t + j],
                    device_id=(x, y, 1 - c), device_id_type=MESH).wait_recv()
        for cp in cps:
            cp.wait_send()

    out = pl.pallas_call(
        body, in_specs=[ANY_SPEC] * nt, out_specs=[ANY_SPEC] * nt, out_shape=[_sds(f.shape, f.dtype) for f in fulls],
        input_output_aliases={t: t for t in range(nt)},
        scratch_shapes=[pltpu.SemaphoreType.DMA((3 * nt,)), pltpu.SemaphoreType.DMA((3 * nt,))],
        name=name, compiler_params=_params())(*fulls)
    return list(out)


def _piece(ref, kind, chip, shard_shape):
    r, n = shard_shape
    if kind == "col":
        return ref.at[:, pl.ds(pl.multiple_of(chip * n, 128), n)]
    return ref.at[pl.ds(pl.multiple_of(chip * r, 16), r), :]


def _piece_shape(kind, shard_shape):
    r, n = shard_shape
    return (r // 2, n) if kind == "col" else (r, n // 2)


def exchange_start(name, parts, kinds, shard_shapes):
    nt = len(parts)
    lands = [lax.empty((3,) + _piece_shape(kinds[t], shard_shapes[t]), BF16) for t in range(nt)]

    def body(*refs):
        part_refs, land_refs = refs[:nt], refs[nt:2 * nt]
        send_sems, recv_sems, token = refs[2 * nt], refs[2 * nt + 1], refs[-1]
        x, y, c, others = _position()
        for t in range(nt):
            for j, (ox, oy) in enumerate(others):
                pltpu.make_async_remote_copy(
                    src_ref=_piece(part_refs[t], kinds[t], 2 * ox + oy, shard_shapes[t]), dst_ref=land_refs[t].at[j],
                    send_sem=send_sems.at[3 * t + j], recv_sem=recv_sems.at[3 * t + j],
                    device_id=(ox, oy, c), device_id_type=MESH).start()
        token[...] = jnp.zeros_like(token)

    sems = pltpu.SemaphoreType.DMA((3 * nt,))
    both = list(parts) + lands
    out = pl.pallas_call(
        body, name=name, in_specs=[HBM_SPEC] * (2 * nt),
        out_specs=(SEM_SPEC, SEM_SPEC, *[HBM_SPEC] * (2 * nt), pl.BlockSpec(memory_space=pltpu.VMEM)),
        out_shape=(sems, sems, *[pltpu.HBM(a.shape, a.dtype) for a in both], _sds((8, 128), F32)),
        input_output_aliases={t: 2 + t for t in range(2 * nt)}, compiler_params=_split_params(),
    )(*[_in_hbm(a) for a in both])
    return out[0], out[1], list(out[2:2 + nt]), list(out[2 + nt:2 + 2 * nt]), out[-1]


def exchange_wait(name, send_sems, recv_sems, parts, lands, kinds, shard_shapes, after):
    nt = len(parts)

    def body(*refs):
        part_refs, land_refs = refs[:nt], refs[nt:2 * nt]
        send_ref, recv_ref = refs[2 * nt], refs[2 * nt + 1]
        x, y, c, others = _position()
        for t in range(nt):
            for j, (ox, oy) in enumerate(others):
                cp = pltpu.make_async_remote_copy(
                    src_ref=_piece(part_refs[t], kinds[t], 2 * ox + oy, shard_shapes[t]), dst_ref=land_refs[t].at[j],
                    send_sem=send_ref.at[3 * t + j], recv_sem=recv_ref.at[3 * t + j],
                    device_id=(ox, oy, c), device_id_type=MESH)
                cp.wait_send()
                cp.wait_recv()

    both = list(parts) + list(lands)
    out = pl.pallas_call(
        body, name=name, in_specs=[HBM_SPEC] * (2 * nt) + [SEM_SPEC, SEM_SPEC, HBM_SPEC], out_specs=[HBM_SPEC] * (2 * nt),
        out_shape=[pltpu.HBM(a.shape, a.dtype) for a in both], input_output_aliases={t: t for t in range(2 * nt)},
        compiler_params=_split_params())(*both, send_sems, recv_sems, _in_hbm(after))
    return list(out[:nt]), list(out[nt:])


def all_reduce_small(name, bufs):
    n = len(bufs)
    halves = [b.shape[0] // 2 for b in bufs]

    def body(*refs):
        in_refs, out_refs, lands = refs[:n], refs[n:2 * n], refs[2 * n:3 * n]
        send_sems, recv_sems = refs[3 * n:]
        x, y, c, _ = _position()
        mine = [pl.ds(pl.multiple_of(c * h, 8), h) for h in halves]
        other = [pl.ds(pl.multiple_of((1 - c) * h, 8), h) for h in halves]
        for k in range(n):
            out_refs[k][mine[k], :] = in_refs[k][mine[k], :]
        for s, peer in enumerate([(x, y, 1 - c), (1 - x, y, c), (x, 1 - y, c)]):
            cps = []
            for k in range(n):
                src = in_refs[k].at[other[k]] if s == 0 else out_refs[k].at[mine[k]]
                cp = pltpu.make_async_remote_copy(
                    src_ref=src, dst_ref=lands[k].at[s], send_sem=send_sems.at[4 * k + s], recv_sem=recv_sems.at[4 * k + s],
                    device_id=peer, device_id_type=MESH)
                cp.start()
                cps.append(cp)
            for k, cp in enumerate(cps):
                cp.wait()
                out_refs[k][mine[k], :] = out_refs[k][mine[k], :] + lands[k][s]
        cps = []
        for k in range(n):
            cp = pltpu.make_async_remote_copy(
                src_ref=out_refs[k].at[mine[k]], dst_ref=out_refs[k].at[mine[k]], send_sem=send_sems.at[4 * k + 3],
                recv_sem=recv_sems.at[4 * k + 3], device_id=(x, y, 1 - c), device_id_type=MESH)
            cp.start()
            cps.append(cp)
        for cp in cps:
            cp.wait()

    vm = pl.BlockSpec(memory_space=pltpu.VMEM)
    out = pl.pallas_call(
        body, in_specs=[vm] * n, out_specs=[vm] * n, out_shape=[_sds(b.shape, F32) for b in bufs],
        scratch_shapes=[pltpu.VMEM((3, h, b.shape[1]), F32) for h, b in zip(halves, bufs)]
        + [pltpu.SemaphoreType.DMA((4 * n,)), pltpu.SemaphoreType.DMA((4 * n,))],
        name=name, compiler_params=_params())(*bufs)
    return list(out)


def _local_step(x, target, small, need, emit):
    d = D_MODEL
    full = {}

    def after_token(vec, token):
        return vec if token is None else vec + token[0:1, 0:1]

    lam_r, lam_i, bbar_re, bbar_im = small["s5_disc"]
    rb, rc = _s5_matrices(bbar_re, bbar_im, small["s5_c_re"], small["s5_c_im"])
    rb16, rc16 = rb.astype(BF16), rc.astype(BF16)
    lr_t, li_t = lam_r.reshape(S5_BLOCKS, 8, 128), lam_i.reshape(S5_BLOCKS, 8, 128)
    (u,) = rms_fwd("norm_mix0", x, [small["norm_mix0"]], [F32])
    ge, y2, cs = s5_fwd(u, small["s5_d"], rb16, rc16, lr_t, li_t)
    full.update(need("glu", ge))
    h1, val, gate = mm_nn(
        "glu", ge, full["w_glu"], [0, d], d,
        lambda accs, e, r: [e[0] + (accs[0] + r[0]) * jax.nn.sigmoid(accs[1] + r[1]), accs[0] + r[0], accs[1] + r[1]],
        [F32, F32, F32], extras=[x], rowvecs=[(small["s5_b_glu"], 0), (small["s5_b_glu"], d)])

    def mlp_fwd(tag, h, gain, w_in, w_out):
        (n,) = rms_fwd("norm_mlp" + tag, h, [gain], [BF16])
        a, r = mm_nn("mlp_in" + tag, n, w_in, [0], w_in.shape[1],
                     lambda accs, e, rv: [accs[0], jnp.square(jnp.maximum(accs[0], 0.0))], [F32, BF16], tm=2048)
        (h_out,) = mm_nn("mlp_out" + tag, r, w_out, [0], d, lambda accs, e, rv: [e[0] + accs[0]], [F32], extras=[h])
        return h_out, (n, a, r)

    full.update(need("mlp0", h1))
    h2, mlp0 = mlp_fwd("0", h1, small["norm_mlp0"], full["w_in0"], full["w_out0"])

    full.update(need("rest", h2))
    nkv, n2 = rms_fwd("norm_kv_mix1", h2, [small["norm_kv"], small["norm_mix1"]], [BF16, BF16])
    kvw = 2 * N_KV * HEAD_DIM
    (kv,) = mm_nn("kv_proj", nkv, full["w_kv"], [0], kvw, lambda accs, e, r: [accs[0] + r[0]], [BF16],
                  rowvecs=[(small["b_kv"], 0)])
    (q,) = mm_nn("q_proj", n2, full["w_q"], [0], d, lambda accs, e, r: [accs[0] + r[0]], [BF16],
                 rowvecs=[(small["b_q"], 0)])
    sinks = small["sinks"].reshape(N_Q)
    o = attn_fwd(q, kv, sinks)
    (h3,) = mm_nn("o_proj", o, full["w_o"], [0], d, lambda accs, e, r: [e[0] + accs[0] + r[0]], [F32],
                  extras=[h2], rowvecs=[(small["b_o"], 0)])
    h4, mlp1 = mlp_fwd("1", h3, small["norm_mlp1"], full["w_in1"], full["w_out1"])
    loss_tile, dh, dhb, dg_final = final_loss(h4, target, small["norm_final"])

    grads_small, grads_full = {"norm_final": dg_final}, {}
    ident = lambda acc, e: [acc]
    layer1 = ["w_out1", "w_in1", "w_o", "w_q", "w_kv"]
    layer0 = ["w_out0", "w_in0", "w_glu"]

    def mlp_bwd(tag, dh, dhb, h_in, gain, w_in, w_out, saved):
        n, a, r = saved
        grads_full["w_out" + tag] = mm_tn("dw_out" + tag, r, dhb, tn=1024)
        (da,) = mm_nt("mlp_da" + tag, dhb, w_out, lambda acc, e: [acc * 2.0 * jnp.maximum(e[0], 0.0)], [BF16], extras=[a],
                      tm=2048)
        grads_full["w_in" + tag] = mm_tn("dw_in" + tag, n, da, tn=1024)
        (dn,) = mm_nt("mlp_dn" + tag, da, w_in, ident, [F32])
        dx, dxb, colsum, dg = rms_bwd("norm_mlp_bwd" + tag, h_in, [dn], [gain], dh)
        grads_small["norm_mlp" + tag] = dg
        return dx, dxb, colsum

    dh3, dh3b, colsum3 = mlp_bwd("1", dh, dhb, h3, small["norm_mlp1"], full["w_in1"], full["w_out1"], mlp1)
    grads_small["b_o"] = colsum3
    grads_full["w_o"] = mm_tn("dw_o", o, dh3b)
    (do,) = mm_nt("attn_do", dh3b, full["w_o"], ident, [BF16])
    dq, dbq, dprev, dcur, dsink = attn_bwd(q, kv, do, sinks)
    dkv, dbkv = kv_combine(dprev, dcur)
    grads_small["b_q"], grads_small["b_kv"], grads_small["sinks"] = dbq, dbkv, dsink
    grads_full["w_q"] = mm_tn("dw_q", n2, dq)
    grads_full["w_kv"] = mm_tn("dw_kv", nkv, dkv)
    (dn2,) = mm_nt("attn_dn", dq, full["w_q"], ident, [F32])
    (dnkv,) = mm_nt("kv_dn", dkv, full["w_kv"], ident, [F32])
    token = emit("layer1", {n: grads_full[n] for n in layer1})
    dh2, dh2b, _, dg_mix1, dg_kv = rms_bwd("norm_kv_mix1_bwd", h2, [dn2, dnkv],
                                           [after_token(small["norm_mix1"], token), small["norm_kv"]], dh3)
    grads_small["norm_mix1"], grads_small["norm_kv"] = dg_mix1, dg_kv
    dh1, _, _ = mlp_bwd("0", dh2, dh2b, h1, small["norm_mlp0"], full["w_in0"], full["w_out0"], mlp0)

    dz, db_glu = glu_bwd(dh1, val, gate)
    grads_small["s5_b_glu"] = db_glu
    grads_full["w_glu"] = mm_tn("dw_glu", ge, dz, tn=1024)
    token = emit("layer0", {n: grads_full[n] for n in layer0})
    (dy2,) = mm_nt("glu_dy", dz, full["w_glu"], lambda acc, e: [acc * _gelu_grad(e[0])], [F32], extras=[y2])
    rbt16, rct16 = jnp.swapaxes(rb16, 1, 2), jnp.swapaxes(rc16, 1, 2)
    du, dd, drb, drc, dlr, dli = s5_bwd(u, dy2, after_token(small["s5_d"], token), cs, rb16, rbt16, rct16, lr_t, li_t)
    grads_small["s5_d"] = dd
    grads_small["s5_mats"] = (drb, drc, dlr, dli)
    grad_x, _, _, dg_mix0 = rms_bwd("norm_mix0_bwd", x, [du], [small["norm_mix0"]], dh1)
    grads_small["norm_mix0"] = dg_mix0
    return loss_tile, grad_x, grads_small


SMALL_NAMES = ["norm_mix", "norm_mlp", "norm_kv", "norm_final", "s5_a_re", "s5_a_im", "s5_log_dt", "s5_b_re", "s5_b_im",
               "s5_c_re", "s5_c_im", "s5_d", "s5_b_glu", "b_kv", "b_q", "sinks", "b_o"]
BIG_NAMES = ["s5_w_glu", "w_kv", "w_q", "w_o", "w_mlp_in", "w_mlp_out"]
WEIGHT_ORDER = ["norm_mix", "norm_mlp", "norm_kv", "norm_final", "s5_a_re", "s5_a_im", "s5_log_dt", "s5_b_re", "s5_b_im",
                "s5_c_re", "s5_c_im", "s5_d", "s5_w_glu", "s5_b_glu", "w_kv", "b_kv", "w_q", "b_q", "sinks", "w_o", "b_o",
                "w_mlp_in", "w_mlp_out"]


def kernel(x, norm_mix, norm_mlp, norm_kv, norm_final, s5_a_re, s5_a_im, s5_log_dt, s5_b_re, s5_b_im, s5_c_re, s5_c_im, s5_d, s5_w_glu, s5_b_glu, w_kv, b_kv, w_q, b_q, sinks, w_o, b_o, w_mlp_in, w_mlp_out, loss_target, m_norm_mix, m_norm_mlp, m_norm_kv, m_norm_final, m_s5_a_re, m_s5_a_im, m_s5_log_dt, m_s5_b_re, m_s5_b_im, m_s5_c_re, m_s5_c_im, m_s5_d, m_s5_w_glu, m_s5_b_glu, m_w_kv, m_b_kv, m_w_q, m_b_q, m_sinks, m_w_o, m_b_o, m_w_mlp_in, m_w_mlp_out, v_norm_mix, v_norm_mlp, v_norm_kv, v_norm_final, v_s5_a_re, v_s5_a_im, v_s5_log_dt, v_s5_b_re, v_s5_b_im, v_s5_c_re, v_s5_c_im, v_s5_d, v_s5_w_glu, v_s5_b_glu, v_w_kv, v_b_kv, v_w_q, v_b_q, v_sinks, v_w_o, v_b_o, v_w_mlp_in, v_w_mlp_out):
    env = dict(locals())
    w = {n: env[n] for n in WEIGHT_ORDER}
    mom = {n: env["m_" + n] for n in WEIGHT_ORDER}
    var = {n: env["v_" + n] for n in WEIGHT_ORDER}
    d = D_MODEL
    xi, yi, ci = lax.axis_index("x"), lax.axis_index("y"), lax.axis_index("c")
    chip = 2 * xi + yi
    where = jnp.stack([ci, chip]).astype(jnp.int32)

    dsh, bsh = s5_d.shape[1], s5_b_glu.shape[1]
    placed = jnp.concatenate([
        lax.dynamic_update_slice(jnp.zeros((4 * dsh,), F32), s5_d[0], (chip * dsh,)),
        lax.dynamic_update_slice(jnp.zeros((4 * bsh,), F32), s5_b_glu[0], (chip * bsh,))])
    placed = jnp.pad(placed, (0, (-placed.shape[0]) % 2048))
    placed = jnp.where(ci == 0, placed, 0.0).reshape(-1, 128)
    (gathered_rows,) = all_reduce_small("gather_vectors", [placed])
    gathered = gathered_rows.reshape(-1)
    d_full, bglu_full = gathered[:4 * dsh].reshape(1, -1), gathered[4 * dsh:].reshape(1, -1)

    big = [s5_w_glu, w_kv[None], w_q, w_o, w_mlp_in, w_mlp_out]
    entries = [(0, 0, "col"), (1, 0, "row"), (2, 0, "row"), (3, 0, "row"), (4, 0, "col"), (4, 1, "col"),
               (5, 0, "row"), (5, 1, "row")]
    names = ["w_glu", "w_kv", "w_q", "w_o", "w_in0", "w_in1", "w_out0", "w_out1"]
    kinds = dict(zip(names, [k for _, _, k in entries]))
    shard_shapes = dict(zip(names, [tuple(big[a].shape[1:]) for a, _, _ in entries]))

    placed_w = {n: cast_place("cast_" + n, big[a], layer, kind, where) for n, (a, layer, kind) in zip(names, entries)}
    gather_groups = {"glu": ["w_glu"], "mlp0": ["w_in0", "w_out0"], "rest": ["w_kv", "w_q", "w_o", "w_in1", "w_out1"]}
    started, token = {}, gathered_rows
    for group, members in gather_groups.items():
        send, recv, thru, token = gather_start(
            "gather_start_" + group, [placed_w[n] for n in members], [kinds[n] for n in members],
            [shard_shapes[n] for n in members], token)
        started[group] = (send, recv, thru)

    def need(group, after):
        members = gather_groups[group]
        ks, shapes = [kinds[n] for n in members], [shard_shapes[n] for n in members]
        send, recv, thru = started[group]
        landed = gather_wait("gather_wait_" + group, send, recv, thru, ks, shapes, after)
        return dict(zip(members, forward_halves("forward_halves_" + group, landed, ks, shapes)))

    exchanging = {}

    def emit(group, partial):
        members = list(partial)
        ks, shapes = [kinds[n] for n in members], [shard_shapes[n] for n in members]
        landed = swap_halves("swap_halves_" + group, [partial[n] for n in members], ks)
        sums = [add_halves("add_halves_" + n, partial[n], landed[t], ks[t], where) for t, n in enumerate(members)]
        send, recv, parts, lands, tok = exchange_start("exchange_start_" + group, sums, ks, shapes)
        exchanging[group] = (members, send, recv, parts, lands)
        return tok

    disc = lambda *p: _s5_discretise(p[0], p[1], p[2], p[3], p[4])
    disc_args = (s5_a_re[0], s5_a_im[0], s5_log_dt[0], s5_b_re[0], s5_b_im[0])
    disc_out, disc_vjp = jax.vjp(disc, *disc_args)
    small = {
        "norm_mix0": norm_mix[0:1] + token[0:1, 0:1], "norm_mix1": norm_mix[1:2], "norm_mlp0": norm_mlp[0:1], "norm_mlp1": norm_mlp[1:2],
        "norm_kv": norm_kv.reshape(1, d), "norm_final": norm_final.reshape(1, d), "s5_disc": disc_out,
        "s5_c_re": s5_c_re[0], "s5_c_im": s5_c_im[0], "s5_d": d_full, "s5_b_glu": bglu_full,
        "b_kv": b_kv.reshape(1, -1), "b_q": b_q, "sinks": sinks, "b_o": b_o,
    }
    loss_row, grad_x, gs = _local_step(x[0], loss_target[0], small, need, emit)

    mats = s5_compact(*gs["s5_mats"])
    rows = [gs["norm_mix0"], gs["norm_mix1"], gs["norm_mlp0"], gs["norm_mlp1"], gs["norm_kv"], gs["norm_final"], gs["s5_d"],
            gs["b_q"], gs["b_o"], gs["s5_b_glu"], gs["b_kv"], gs["sinks"], loss_row, jnp.zeros((2, d), F32)]
    vecs, mats = all_reduce_small("reduce_small", [jnp.concatenate(rows, axis=0), mats])
    grads = split_vectors(where, vecs, dsh, bsh)
    loss = grads.pop("loss")[0, 0]
    dbbar_re, dbbar_im, dc_re, dc_im, dlr, dli = _s5_unpack(mats)
    g_are, g_aim, g_dt, g_bre, g_bim = disc_vjp((dlr, dli, dbbar_re, dbbar_im))
    grads.update({"s5_a_re": g_are[None], "s5_a_im": g_aim[None], "s5_log_dt": g_dt[None], "s5_b_re": g_bre[None],
                  "s5_b_im": g_bim[None], "s5_c_re": dc_re[None], "s5_c_im": dc_im[None]})

    reduced = [None] * len(big)
    where_of = dict(zip(names, entries))
    for group, after in (("layer1", grad_x), ("layer0", mats)):
        members, send, recv, parts, lands = exchanging[group]
        ks, shapes = [kinds[n] for n in members], [shard_shapes[n] for n in members]
        parts, lands = exchange_wait("exchange_wait_" + group, send, recv, parts, lands, ks, shapes, after)
        for t, n in enumerate(members):
            a, layer, kind = where_of[n]
            reduced[a] = sum_shards("sum_shards_" + n, parts[t], lands[t], kind, shapes[t], where, layer,
                                    big[a].shape[0], into=reduced[a])
    reduced = share_halves(reduced, entries)
    for n, g in zip(BIG_NAMES, reduced):
        grads[n] = g.reshape(w[n].shape)

    delta, new_m, new_v = {}, {}, {}
    for n in BIG_NAMES:
        flat = lambda a: a.reshape(-1, a.shape[-1])
        dl, nm, nv = adamw("adamw_" + n, flat(w[n]), flat(grads[n]), flat(mom[n]), flat(var[n]))
        delta[n], new_m[n], new_v[n] = dl.reshape(w[n].shape), nm.reshape(w[n].shape), nv.reshape(w[n].shape)
    as_rows = lambda a: a.reshape(1, -1) if a.ndim == 1 else a
    sw, sg, sm, sv = ([as_rows(t[n]) for n in SMALL_NAMES] for t in (w, grads, mom, var))
    for n, a, b, c_ in zip(SMALL_NAMES, *adamw_native("adamw_small", sw, sg, sm, sv)):
        delta[n], new_m[n], new_v[n] = a, b, c_

    out = [loss.reshape(()), grad_x[None]]
    for table in (grads, delta, new_m, new_v):
        out += [table[n].reshape(w[n].shape) for n in WEIGHT_ORDER]
    return tuple(out)
```

```python
import functools
import math

import jax
import jax.numpy as jnp
from jax import lax
from jax.experimental import pallas as pl
from jax.experimental.pallas import tpu as pltpu

F32 = jnp.float32
BF16 = jnp.bfloat16

D_MODEL = 1024
S5_GROUPS = 64
S5_GROUP = 16
S5_STATE = 64
N_KV = 4
N_Q = 16
HEAD_DIM = 64
BLOCK = 128
NORM_EPS = 1e-5
LAMBDA_RE_MAX = -1e-4
ADAM_LR, ADAM_B1, ADAM_B2, ADAM_EPS, ADAM_WD, ADAM_STEP = 0.001, 0.9, 0.999, 1e-08, 0.01, 10

VMEM_LIMIT_BYTES = 56 * 1024 * 1024
S5_CHUNK = 256
S5_BLOCKS = 4
MESH = pl.DeviceIdType.MESH


def _params(sem=None):
    return pltpu.CompilerParams(dimension_semantics=sem, vmem_limit_bytes=VMEM_LIMIT_BYTES)


def _sds(shape, dtype):
    return jax.ShapeDtypeStruct(shape, dtype)


def _rms_hat(xv):
    r = lax.rsqrt(jnp.mean(xv * xv, axis=-1, keepdims=True) + NORM_EPS)
    return xv * r, r


def rms_fwd(name, x, gains, out_dtypes, tm=256):
    n_rows, d = x.shape
    ng = len(gains)

    def body(x_ref, *refs):
        xh, _ = _rms_hat(x_ref[...])
        for g_ref, o_ref in zip(refs[:ng], refs[ng:]):
            o_ref[...] = (xh * g_ref[...]).astype(o_ref.dtype)

    row = pl.BlockSpec((tm, d), lambda i: (i, 0))
    vec = pl.BlockSpec((1, d), lambda i: (0, 0))
    return pl.pallas_call(
        body, grid=(n_rows // tm,), in_specs=[row] + [vec] * ng, out_specs=[row] * ng,
        out_shape=[_sds((n_rows, d), dt) for dt in out_dtypes], name=name,
        compiler_params=_params(("parallel",)))(x, *gains)


def rms_bwd(name, x, dys, gains, res, tm=256):
    n_rows, d = x.shape
    ng = len(gains)

    def body(x_ref, res_ref, *refs):
        dy_refs, g_refs = refs[:ng], refs[ng:2 * ng]
        dx_ref, dxb_ref, cs_ref = refs[2 * ng:2 * ng + 3]
        dg_refs = refs[2 * ng + 3:]
        i = pl.program_id(0)
        xh, r = _rms_hat(x_ref[...])
        dxh = jnp.zeros_like(xh)
        dgs = []
        for dy_ref, g_ref in zip(dy_refs, g_refs):
            dy = dy_ref[...].astype(F32)
            dxh = dxh + dy * g_ref[...]
            dgs.append(jnp.sum(dy * xh, axis=0, keepdims=True))
        dx = r * (dxh - xh * jnp.mean(dxh * xh, axis=-1, keepdims=True)) + res_ref[...]
        dx_ref[...] = dx
        dxb_ref[...] = dx.astype(BF16)
        cs = jnp.sum(dx, axis=0, keepdims=True)

        @pl.when(i == 0)
        def _():
            cs_ref[...] = jnp.zeros_like(cs_ref)
            for dg_ref in dg_refs:
                dg_ref[...] = jnp.zeros_like(dg_ref)

        cs_ref[...] += cs
        for dg_ref, dg in zip(dg_refs, dgs):
            dg_ref[...] += dg

    row = pl.BlockSpec((tm, d), lambda i: (i, 0))
    vec = pl.BlockSpec((1, d), lambda i: (0, 0))
    return pl.pallas_call(
        body, grid=(n_rows // tm,), in_specs=[row, row] + [row] * ng + [vec] * ng,
        out_specs=[row, row, vec] + [vec] * ng,
        out_shape=[_sds((n_rows, d), F32), _sds((n_rows, d), BF16), _sds((1, d), F32)] + [_sds((1, d), F32)] * ng,
        name=name, compiler_params=_params(("arbitrary",)))(x, res, *dys, *gains)


def mm_nn(name, a, w, col_offsets, n_out, epilogue, out_dtypes, extras=(), rowvecs=(), tm=1024, tn=512):
    m, k = a.shape
    tm, tn = min(tm, m), min(tn, n_out)
    nw, ne, nr = len(col_offsets), len(extras), len(rowvecs)

    def body(a_ref, *refs):
        w_refs, e_refs, r_refs = refs[:nw], refs[nw:nw + ne], refs[nw + ne:nw + ne + nr]
        o_refs = refs[nw + ne + nr:]
        av = a_ref[...]
        accs = [jnp.dot(av, w_ref[...], preferred_element_type=F32) for w_ref in w_refs]
        outs = epilogue(accs, [e[...] for e in e_refs], [r[...] for r in r_refs])
        for o_ref, o in zip(o_refs, outs):
            o_ref[...] = o.astype(o_ref.dtype)

    def wspec(off):
        return pl.BlockSpec((k, tn), lambda j, i, off=off: (0, off // tn + j))

    def rspec(off):
        return pl.BlockSpec((1, tn), lambda j, i, off=off: (0, off // tn + j))

    tile = pl.BlockSpec((tm, tn), lambda j, i: (i, j))
    in_specs = ([pl.BlockSpec((tm, k), lambda j, i: (i, 0))] + [wspec(o) for o in col_offsets]
                + [tile] * ne + [rspec(o) for _, o in rowvecs])
    return pl.pallas_call(
        body, grid=(n_out // tn, m // tm), in_specs=in_specs, out_specs=[tile] * len(out_dtypes),
        out_shape=[_sds((m, n_out), dt) for dt in out_dtypes], name=name,
        compiler_params=_params(("parallel", "parallel")))(a, *([w] * nw), *extras, *[r for r, _ in rowvecs])


def mm_nt(name, g, w, epilogue, out_dtypes, extras=(), rowvecs=(), n_sums=0, tm=512, tk=512):
    m, n = g.shape
    k = w.shape[0]
    tm, tk = min(tm, m), min(tk, k)
    ne, nr, no = len(extras), len(rowvecs), len(out_dtypes)

    def body(g_ref, w_ref, *refs):
        e_refs, r_refs, o_refs, s_refs = refs[:ne], refs[ne:ne + nr], refs[ne + nr:ne + nr + no], refs[ne + nr + no:]
        acc = lax.dot_general(g_ref[...], w_ref[...], (((1,), (1,)), ((), ())), preferred_element_type=F32)
        outs = epilogue(acc, [e[...] for e in e_refs], [r[...] for r in r_refs])
        for o_ref, o in zip(o_refs, outs[:no]):
            o_ref[...] = o.astype(o_ref.dtype)
        if n_sums:
            @pl.when(pl.program_id(0) == 0)
            def _():
                for s_ref in s_refs:
                    s_ref[...] = jnp.zeros_like(s_ref)

            for s_ref, val in zip(s_refs, outs[no:]):
                s_ref[...] += val

    tile = pl.BlockSpec((tm, tk), lambda i, j: (i, j))
    vec = pl.BlockSpec((1, tk), lambda i, j: (0, j))
    sem = ("arbitrary", "parallel") if n_sums else ("parallel", "parallel")
    return pl.pallas_call(
        body, grid=(m // tm, k // tk),
        in_specs=[pl.BlockSpec((tm, n), lambda i, j: (i, 0)), pl.BlockSpec((tk, n), lambda i, j: (j, 0))]
        + [tile] * ne + [vec] * nr,
        out_specs=[tile] * no + [vec] * n_sums,
        out_shape=[_sds((m, k), dt) for dt in out_dtypes] + [_sds((1, k), F32)] * n_sums, name=name,
        compiler_params=_params(sem))(g, w, *extras, *rowvecs)


def mm_tn(name, a, g, tk=512, tn=512):
    m, k = a.shape
    n = g.shape[1]
    tk, tn = min(tk, k), min(tn, n)

    def body(a_ref, g_ref, o_ref):
        acc = lax.dot_general(a_ref[...], g_ref[...], (((0,), (0,)), ((), ())), preferred_element_type=F32)
        o_ref[...] = acc.astype(o_ref.dtype)

    return pl.pallas_call(
        body, grid=(k // tk, n // tn),
        in_specs=[pl.BlockSpec((m, tk), lambda i, j: (0, i)), pl.BlockSpec((m, tn), lambda i, j: (0, j))],
        out_specs=pl.BlockSpec((tk, tn), lambda i, j: (i, j)), out_shape=_sds((k, n), BF16), name=name,
        compiler_params=_params(("parallel", "parallel")))(a, g)


def _row_mask(tc):
    row = lax.broadcasted_iota(jnp.int32, (8 * tc, 256), 0) % 8
    col = lax.broadcasted_iota(jnp.int32, (8 * tc, 256), 1) // 32
    return row == col


def _expand_rows(expand_ref, val, mask):
    rep = jnp.dot(expand_ref[...], val.astype(BF16), preferred_element_type=F32)
    return jnp.where(mask, rep, 0.0).astype(BF16)


def _staged(ref):
    return jnp.concatenate([ref[0], ref[1]], axis=1)


def _stage(ref, val):
    ref[0] = val[:, 0:128]
    ref[1] = val[:, 128:256]


def _gather_rows(src_ref, tc):
    halves = []
    for half in range(2):
        col = lax.broadcasted_iota(jnp.int32, (tc, 128), 1) // 32 + 4 * half
        out = jnp.zeros((tc, 128), F32)
        for s8 in range(4 * half, 4 * half + 4):
            out = jnp.where(col == s8, src_ref.at[half][pl.ds(s8, tc, stride=8), :], out)
        halves.append(out)
    return jnp.concatenate(halves, axis=1)


def _gelu(x):
    c = math.sqrt(2.0 / math.pi)
    return 0.5 * x * (1.0 + jnp.tanh(c * (x + 0.044715 * x * x * x)))


def _gelu_grad(x):
    c = math.sqrt(2.0 / math.pi)
    t = jnp.tanh(c * (x + 0.044715 * x * x * x))
    return 0.5 * (1.0 + t) + 0.5 * x * (1.0 - t * t) * c * (1.0 + 3.0 * 0.044715 * x * x)


def _expansion(tc):
    return (jnp.arange(8 * tc)[:, None] // 8 == jnp.arange(tc)[None, :]).astype(BF16)


def s5_fwd(u, d_skip, rb, rc, lam_r, lam_i):
    n_rows = u.shape[0]
    tc = min(S5_CHUNK, n_rows)
    nc = n_rows // tc

    def body(u_ref, d_ref, ex_ref, rb_ref, rc_ref, lr_ref, li_ref, ge_ref, y2_ref, cs_ref, bux, yrows, carry):
        i = pl.program_id(0)

        @pl.when(i == 0)
        def _():
            carry[...] = jnp.zeros_like(carry)

        cs_ref[0] = carry[...]
        mask = _row_mask(tc)
        for blk in range(S5_BLOCKS):
            lhs = _expand_rows(ex_ref, u_ref[:, blk * 256:(blk + 1) * 256], mask)
            bux[blk] = jnp.dot(lhs, rb_ref[blk], preferred_element_type=F32)
        lam = [(lr_ref[blk], li_ref[blk]) for blk in range(S5_BLOCKS)]

        def step(t, c):
            r0 = pl.multiple_of(t * 8, 8)
            new = []
            for blk in range(S5_BLOCKS):
                xr, xi = c[2 * blk], c[2 * blk + 1]
                lr, li = lam[blk]
                nr = lr * xr - li * xi + bux[blk, pl.ds(r0, 8), 0:128]
                ni = lr * xi + li * xr + bux[blk, pl.ds(r0, 8), 128:256]
                bux[blk, pl.ds(r0, 8), 0:128] = nr
                bux[blk, pl.ds(r0, 8), 128:256] = ni
                new += [nr, ni]
            return tuple(new)

        c0 = []
        for blk in range(S5_BLOCKS):
            c0 += [carry[blk, :, 0:128], carry[blk, :, 128:256]]
        cn = lax.fori_loop(0, tc, step, tuple(c0), unroll=4)
        for blk in range(S5_BLOCKS):
            carry[blk, :, 0:128] = cn[2 * blk]
            carry[blk, :, 128:256] = cn[2 * blk + 1]
        for blk in range(S5_BLOCKS):
            _stage(yrows, jnp.dot(bux[blk].astype(BF16), rc_ref[blk], preferred_element_type=F32))
            sl = slice(blk * 256, (blk + 1) * 256)
            y2 = _gather_rows(yrows, tc) + d_ref[:, sl] * u_ref[:, sl]
            y2_ref[:, sl] = y2
            ge_ref[:, sl] = _gelu(y2).astype(BF16)

    row = pl.BlockSpec((tc, D_MODEL), lambda i: (i, 0))
    mat = pl.BlockSpec((S5_BLOCKS, 256, 256), lambda i: (0, 0, 0))
    lamspec = pl.BlockSpec((S5_BLOCKS, 8, 128), lambda i: (0, 0, 0))
    return pl.pallas_call(
        body, grid=(nc,),
        in_specs=[row, pl.BlockSpec((1, D_MODEL), lambda i: (0, 0)), pl.BlockSpec((8 * tc, tc), lambda i: (0, 0)),
                  mat, mat, lamspec, lamspec],
        out_specs=[row, row, pl.BlockSpec((1, S5_BLOCKS, 8, 256), lambda i: (i, 0, 0, 0))],
        out_shape=[_sds((n_rows, D_MODEL), BF16), _sds((n_rows, D_MODEL), F32), _sds((nc, S5_BLOCKS, 8, 256), F32)],
        scratch_shapes=[pltpu.VMEM((S5_BLOCKS, 8 * tc, 256), F32), pltpu.VMEM((2, 8 * tc, 128), F32),
                        pltpu.VMEM((S5_BLOCKS, 8, 256), F32)],
        name="s5_fwd", compiler_params=_params(("arbitrary",)))(u, d_skip, _expansion(tc), rb, rc, lam_r, lam_i)


def s5_bwd(u, dy2, d_skip, cs, rb, rbt, rct, lam_r, lam_i):
    n_rows = u.shape[0]
    tc = min(S5_CHUNK, n_rows)
    nc = n_rows // tc

    def body(u_ref, dy_ref, d_ref, cs_ref, ex_ref, rb_ref, rbt_ref, rct_ref, lr_ref, li_ref,
             du_ref, dd_ref, drb_ref, drc_ref, dlr_ref, dli_ref, tmp, lhsu, lhsd, xs, adj, acarry):
        i = pl.program_id(0)

        @pl.when(i == 0)
        def _():
            acarry[...] = jnp.zeros_like(acarry)
            dd_ref[...] = jnp.zeros_like(dd_ref)
            drb_ref[...] = jnp.zeros_like(drb_ref)
            drc_ref[...] = jnp.zeros_like(drc_ref)
            dlr_ref[...] = jnp.zeros_like(dlr_ref)
            dli_ref[...] = jnp.zeros_like(dli_ref)

        dd_ref[...] += jnp.sum(dy_ref[...] * u_ref[...], axis=0, keepdims=True)
        mask = _row_mask(tc)
        for blk in range(S5_BLOCKS):
            sl = slice(blk * 256, (blk + 1) * 256)
            lhsu[blk] = _expand_rows(ex_ref, u_ref[:, sl], mask)
            xs[blk] = jnp.dot(lhsu[blk], rb_ref[blk], preferred_element_type=F32)
            lhsd[blk] = _expand_rows(ex_ref, dy_ref[:, sl], mask)
            adj[blk] = jnp.dot(lhsd[blk], rct_ref[blk], preferred_element_type=F32)
        lam = [(lr_ref[blk], li_ref[blk]) for blk in range(S5_BLOCKS)]

        def fstep(t, c):
            r0 = pl.multiple_of(t * 8, 8)
            new = []
            for blk in range(S5_BLOCKS):
                xr, xi = c[2 * blk], c[2 * blk + 1]
                lr, li = lam[blk]
                nr = lr * xr - li * xi + xs[blk, pl.ds(r0, 8), 0:128]
                ni = lr * xi + li * xr + xs[blk, pl.ds(r0, 8), 128:256]
                xs[blk, pl.ds(r0, 8), 0:128] = nr
                xs[blk, pl.ds(r0, 8), 128:256] = ni
                new += [nr, ni]
            return tuple(new)

        c0 = []
        for blk in range(S5_BLOCKS):
            c0 += [cs_ref[0, blk, :, 0:128], cs_ref[0, blk, :, 128:256]]
        lax.fori_loop(0, tc, fstep, tuple(c0), unroll=4)

        def bstep(k, c):
            t = tc - 1 - k
            r0 = pl.multiple_of(t * 8, 8)
            rp = pl.multiple_of(jnp.maximum(t - 1, 0) * 8, 8)
            first = t == 0
            new_a, new_g = [], []
            for blk in range(S5_BLOCKS):
                ar, ai = c[0][2 * blk], c[0][2 * blk + 1]
                glr, gli = c[1][2 * blk], c[1][2 * blk + 1]
                lr, li = lam[blk]
                nr = lr * ar + li * ai + adj[blk, pl.ds(r0, 8), 0:128]
                ni = lr * ai - li * ar + adj[blk, pl.ds(r0, 8), 128:256]
                adj[blk, pl.ds(r0, 8), 0:128] = nr
                adj[blk, pl.ds(r0, 8), 128:256] = ni
                pr = jnp.where(first, cs_ref[0, blk, :, 0:128], xs[blk, pl.ds(rp, 8), 0:128])
                pi = jnp.where(first, cs_ref[0, blk, :, 128:256], xs[blk, pl.ds(rp, 8), 128:256])
                new_a += [nr, ni]
                new_g += [glr + nr * pr + ni * pi, gli + ni * pr - nr * pi]
            return tuple(new_a), tuple(new_g)

        a0, g0 = [], []
        for blk in range(S5_BLOCKS):
            a0 += [acarry[blk, :, 0:128], acarry[blk, :, 128:256]]
            g0 += [dlr_ref[blk], dli_ref[blk]]
        an, gn = lax.fori_loop(0, tc, bstep, (tuple(a0), tuple(g0)), unroll=2)
        for blk in range(S5_BLOCKS):
            acarry[blk, :, 0:128] = an[2 * blk]
            acarry[blk, :, 128:256] = an[2 * blk + 1]
            dlr_ref[blk] = gn[2 * blk]
            dli_ref[blk] = gn[2 * blk + 1]
        for blk in range(S5_BLOCKS):
            sl = slice(blk * 256, (blk + 1) * 256)
            ab = adj[blk].astype(BF16)
            _stage(tmp, jnp.dot(ab, rbt_ref[blk], preferred_element_type=F32))
            du_ref[:, sl] = _gather_rows(tmp, tc) + d_ref[:, sl] * dy_ref[:, sl]
            drb_ref[blk] += lax.dot_general(lhsu[blk], ab, (((0,), (0,)), ((), ())), preferred_element_type=F32)
            drc_ref[blk] += lax.dot_general(lhsd[blk], xs[blk].astype(BF16), (((0,), (0,)), ((), ())),
                                            preferred_element_type=F32)

    rev = pl.BlockSpec((tc, D_MODEL), lambda i: (nc - 1 - i, 0))
    vec = pl.BlockSpec((1, D_MODEL), lambda i: (0, 0))
    mat = pl.BlockSpec((S5_BLOCKS, 256, 256), lambda i: (0, 0, 0))
    lamspec = pl.BlockSpec((S5_BLOCKS, 8, 128), lambda i: (0, 0, 0))
    big = pltpu.VMEM((S5_BLOCKS, 8 * tc, 256), F32)
    bigb = pltpu.VMEM((S5_BLOCKS, 8 * tc, 256), BF16)
    return pl.pallas_call(
        body, grid=(nc,),
        in_specs=[rev, rev, vec, pl.BlockSpec((1, S5_BLOCKS, 8, 256), lambda i: (nc - 1 - i, 0, 0, 0)),
                  pl.BlockSpec((8 * tc, tc), lambda i: (0, 0)), mat, mat, mat, lamspec, lamspec],
        out_specs=[rev, vec, mat, mat, lamspec, lamspec],
        out_shape=[_sds((n_rows, D_MODEL), F32), _sds((1, D_MODEL), F32), _sds((S5_BLOCKS, 256, 256), F32),
                   _sds((S5_BLOCKS, 256, 256), F32), _sds((S5_BLOCKS, 8, 128), F32), _sds((S5_BLOCKS, 8, 128), F32)],
        scratch_shapes=[pltpu.VMEM((2, 8 * tc, 128), F32), bigb, bigb, big, big, pltpu.VMEM((S5_BLOCKS, 8, 256), F32)],
        name="s5_bwd", compiler_params=_params(("arbitrary",)))(u, dy2, d_skip, cs, _expansion(tc), rb, rbt, rct, lam_r, lam_i)


def _s5_discretise(a_re, a_im, log_dt, b_re, b_im):
    lam = lax.complex(jnp.minimum(a_re, LAMBDA_RE_MAX), a_im)
    dt = jnp.exp(log_dt)[:, None]
    lam_bar = jnp.exp(lam * dt)
    b_bar = ((lam_bar - 1.0) / lam)[:, :, None] * lax.complex(b_re, b_im)
    return jnp.real(lam_bar), jnp.imag(lam_bar), jnp.real(b_bar), jnp.imag(b_bar)


def _s5_matrices(bbar_re, bbar_im, c_re, c_im):
    eye2 = jnp.eye(2, dtype=F32)
    bst = jnp.stack([bbar_re, bbar_im]).reshape(2, S5_BLOCKS, 8, 2, S5_STATE, S5_GROUP)
    bt = jnp.transpose(bst, (1, 2, 3, 5, 0, 4))
    rb = (bt[:, :, :, :, :, None, :] * eye2[None, None, :, None, None, :, None]).reshape(S5_BLOCKS, 256, 256)
    cst = jnp.stack([c_re, -c_im]).reshape(2, S5_BLOCKS, 8, 2, S5_GROUP, S5_STATE)
    ct = jnp.transpose(cst, (1, 0, 5, 2, 3, 4))
    rc = (ct[:, :, None, :, :, :, :] * eye2[None, None, :, None, None, :, None]).reshape(S5_BLOCKS, 256, 256)
    return rb, rc


def s5_compact(drb, drct, dlr, dli):
    def body(drb_ref, drct_ref, dlr_ref, dli_ref, o_ref):
        even = (lax.broadcasted_iota(jnp.int32, (256, 64), 0) // S5_GROUP) % 2 == 0
        for blk in range(S5_BLOCKS):
            for k, ref in enumerate((drb_ref, drct_ref)):
                m = ref[blk]
                re = jnp.where(even, m[:, 0:64], m[:, 64:128])
                im = jnp.where(even, m[:, 128:192], m[:, 192:256])
                o_ref[pl.ds(k * 1024 + blk * 256, 256), :] = jnp.concatenate([re, im], axis=1)
            o_ref[pl.ds(2048 + blk * 8, 8), :] = dlr_ref[blk]
            o_ref[pl.ds(2080 + blk * 8, 8), :] = dli_ref[blk]

    vm = pl.BlockSpec(memory_space=pltpu.VMEM)
    return pl.pallas_call(body, in_specs=[vm] * 4, out_specs=vm, out_shape=_sds((2112, 128), F32), name="s5_compact",
                          compiler_params=_params())(drb, drct, dlr, dli)


def _s5_unpack(mats):
    bm = mats[0:1024].reshape(S5_GROUPS, S5_GROUP, 128)
    cm = mats[1024:2048].reshape(S5_GROUPS, S5_GROUP, 128)
    swap = lambda t: jnp.transpose(t, (0, 2, 1))
    return (swap(bm[:, :, 0:64]), swap(bm[:, :, 64:128]), cm[:, :, 0:64], -cm[:, :, 64:128],
            mats[2048:2080].reshape(S5_GROUPS, S5_STATE), mats[2080:2112].reshape(S5_GROUPS, S5_STATE))


NEG = -1e30


GROUP = N_Q // N_KV


def _attn_masks(n):
    qi = lax.broadcasted_iota(jnp.int32, (GROUP * BLOCK, BLOCK), 0) % BLOCK
    kj = lax.broadcasted_iota(jnp.int32, (GROUP * BLOCK, BLOCK), 1)
    return jnp.logical_and(kj > qi, n > 0), kj <= qi


def _stack_heads(ref, kh):
    return jnp.concatenate([ref[:, (GROUP * kh + g) * HEAD_DIM:(GROUP * kh + g + 1) * HEAD_DIM] for g in range(GROUP)], axis=0)


def _unstack_heads(val):
    return jnp.concatenate([val[g * BLOCK:(g + 1) * BLOCK] for g in range(GROUP)], axis=1)


def _sink_column(sink_ref, kh):
    grp = lax.broadcasted_iota(jnp.int32, (GROUP * BLOCK, 1), 0) // BLOCK
    col = jnp.zeros((GROUP * BLOCK, 1), F32)
    for g in range(GROUP):
        col = jnp.where(grp == g, sink_ref[GROUP * kh + g], col)
    return col, grp


def _attn_exp(q4, kp, kc, sink, mask_p, mask_c):
    scale = 1.0 / math.sqrt(HEAD_DIM)
    nt = (((1,), (1,)), ((), ()))
    sp = jnp.where(mask_p, lax.dot_general(q4, kp, nt, preferred_element_type=F32) * scale, NEG)
    sc = jnp.where(mask_c, lax.dot_general(q4, kc, nt, preferred_element_type=F32) * scale, NEG)
    m = jnp.maximum(jnp.maximum(jnp.max(sp, axis=-1, keepdims=True), jnp.max(sc, axis=-1, keepdims=True)), sink)
    pp = jnp.exp(sp - m)
    pc = jnp.exp(sc - m)
    ps = jnp.exp(sink - m)
    inv = 1.0 / (jnp.sum(pp, axis=-1, keepdims=True) + jnp.sum(pc, axis=-1, keepdims=True) + ps)
    return pp, pc, ps, inv


def attn_fwd(q, kv, sinks):
    n_rows = q.shape[0]
    nb = n_rows // BLOCK

    def body(sink_ref, q_ref, kvp_ref, kvc_ref, o_ref):
        n = pl.program_id(0)
        mask_p, mask_c = _attn_masks(n)
        outs = []
        for kh in range(N_KV):
            ks, vs = slice(kh * HEAD_DIM, (kh + 1) * HEAD_DIM), slice((N_KV + kh) * HEAD_DIM, (N_KV + kh + 1) * HEAD_DIM)
            sink, _ = _sink_column(sink_ref, kh)
            pp, pc, _, inv = _attn_exp(_stack_heads(q_ref, kh), kvp_ref[:, ks], kvc_ref[:, ks], sink, mask_p, mask_c)
            o4 = (jnp.dot(pp.astype(BF16), kvp_ref[:, vs], preferred_element_type=F32)
                  + jnp.dot(pc.astype(BF16), kvc_ref[:, vs], preferred_element_type=F32)) * inv
            outs.append(_unstack_heads(o4))
        o_ref[...] = jnp.concatenate(outs, axis=1).astype(BF16)

    kvw = 2 * N_KV * HEAD_DIM
    return pl.pallas_call(
        body, grid=(nb,),
        in_specs=[pl.BlockSpec(memory_space=pltpu.SMEM), pl.BlockSpec((BLOCK, D_MODEL), lambda n: (n, 0)),
                  pl.BlockSpec((BLOCK, kvw), lambda n: (jnp.maximum(n - 1, 0), 0)), pl.BlockSpec((BLOCK, kvw), lambda n: (n, 0))],
        out_specs=pl.BlockSpec((BLOCK, D_MODEL), lambda n: (n, 0)), out_shape=_sds((n_rows, D_MODEL), BF16),
        name="attn_fwd", compiler_params=_params(("parallel",)))(sinks, q, kv, kv)


def attn_bwd(q, kv, do, sinks):
    n_rows = q.shape[0]
    nb = n_rows // BLOCK
    kvw = 2 * N_KV * HEAD_DIM
    tn = (((0,), (0,)), ((), ()))
    nt = (((1,), (1,)), ((), ()))
    scale = 1.0 / math.sqrt(HEAD_DIM)

    def body(sink_ref, q_ref, kvp_ref, kvc_ref, do_ref, dq_ref, dbq_ref, dprev_ref, dcur_ref, dsink_ref):
        n = pl.program_id(0)
        mask_p, mask_c = _attn_masks(n)
        lane = lax.broadcasted_iota(jnp.int32, (1, D_MODEL), 1)
        dqs, dsink = [], jnp.zeros((1, D_MODEL), F32)
        dkp, dkc, dvp, dvc = [], [], [], []
        for kh in range(N_KV):
            ks, vs = slice(kh * HEAD_DIM, (kh + 1) * HEAD_DIM), slice((N_KV + kh) * HEAD_DIM, (N_KV + kh + 1) * HEAD_DIM)
            q4, do4 = _stack_heads(q_ref, kh), _stack_heads(do_ref, kh)
            kp, kc, vp, vc = kvp_ref[:, ks], kvc_ref[:, ks], kvp_ref[:, vs], kvc_ref[:, vs]
            sink, grp = _sink_column(sink_ref, kh)
            pp, pc, ps, inv = _attn_exp(q4, kp, kc, sink, mask_p, mask_c)
            pp, pc = pp * inv, pc * inv
            dpp = lax.dot_general(do4, vp, nt, preferred_element_type=F32)
            dpc = lax.dot_general(do4, vc, nt, preferred_element_type=F32)
            delta = jnp.sum(pp * dpp, axis=-1, keepdims=True) + jnp.sum(pc * dpc, axis=-1, keepdims=True)
            dsp = (pp * (dpp - delta) * scale).astype(BF16)
            dsc = (pc * (dpc - delta) * scale).astype(BF16)
            dsk = ps * inv * delta
            for g in range(GROUP):
                dsink = dsink + jnp.where(lane == GROUP * kh + g, -jnp.sum(jnp.where(grp == g, dsk, 0.0)), 0.0)
            dqs.append(_unstack_heads(jnp.dot(dsp, kp, preferred_element_type=F32)
                                      + jnp.dot(dsc, kc, preferred_element_type=F32)))
            dkp.append(lax.dot_general(dsp, q4, tn, preferred_element_type=F32))
            dkc.append(lax.dot_general(dsc, q4, tn, preferred_element_type=F32))
            dvp.append(lax.dot_general(pp.astype(BF16), do4, tn, preferred_element_type=F32))
            dvc.append(lax.dot_general(pc.astype(BF16), do4, tn, preferred_element_type=F32))
        dq = jnp.concatenate(dqs, axis=1)
        dq_ref[...] = dq.astype(BF16)
        dprev_ref[0] = jnp.concatenate(dkp + dvp, axis=1)
        dcur_ref[0] = jnp.concatenate(dkc + dvc, axis=1)

        @pl.when(n == 0)
        def _():
            dbq_ref[...] = jnp.zeros_like(dbq_ref)
            dsink_ref[...] = jnp.zeros_like(dsink_ref)

        dbq_ref[...] += jnp.sum(dq, axis=0, keepdims=True)
        dsink_ref[...] += dsink

    blk = pl.BlockSpec((BLOCK, D_MODEL), lambda n: (n, 0))
    part = pl.BlockSpec((1, BLOCK, kvw), lambda n: (n, 0, 0))
    return pl.pallas_call(
        body, grid=(nb,),
        in_specs=[pl.BlockSpec(memory_space=pltpu.SMEM), blk,
                  pl.BlockSpec((BLOCK, kvw), lambda n: (jnp.maximum(n - 1, 0), 0)), pl.BlockSpec((BLOCK, kvw), lambda n: (n, 0)), blk],
        out_specs=[blk, pl.BlockSpec((1, D_MODEL), lambda n: (0, 0)), part, part, pl.BlockSpec((1, D_MODEL), lambda n: (0, 0))],
        out_shape=[_sds((n_rows, D_MODEL), BF16), _sds((1, D_MODEL), F32), _sds((nb, BLOCK, kvw), F32),
                   _sds((nb, BLOCK, kvw), F32), _sds((1, D_MODEL), F32)],
        name="attn_bwd", compiler_params=_params(("arbitrary",)))(sinks, q, kv, kv, do)


def kv_combine(dprev, dcur):
    nb, _, kvw = dprev.shape

    def body(dcur_ref, dnext_ref, dkv_ref, db_ref):
        m = pl.program_id(0)
        dkv = dcur_ref[0] + jnp.where(m + 1 < nb, dnext_ref[0], 0.0)
        dkv_ref[...] = dkv.astype(BF16)

        @pl.when(m == 0)
        def _():
            db_ref[...] = jnp.zeros_like(db_ref)

        db_ref[:, 0:kvw] += jnp.sum(dkv, axis=0, keepdims=True)

    return pl.pallas_call(
        body, grid=(nb,),
        in_specs=[pl.BlockSpec((1, BLOCK, kvw), lambda m: (m, 0, 0)),
                  pl.BlockSpec((1, BLOCK, kvw), lambda m: (jnp.minimum(m + 1, nb - 1), 0, 0))],
        out_specs=[pl.BlockSpec((BLOCK, kvw), lambda m: (m, 0)), pl.BlockSpec((1, D_MODEL), lambda m: (0, 0))],
        out_shape=[_sds((nb * BLOCK, kvw), BF16), _sds((1, D_MODEL), F32)],
        name="kv_combine", compiler_params=_params(("arbitrary",)))(dcur, dprev)


def glu_bwd(dout, val, gate, tm=256):
    n_rows, d = dout.shape

    def body(do_ref, v_ref, g_ref, dz_ref, db_ref):
        i = pl.program_id(0)
        sg = jax.nn.sigmoid(g_ref[...])
        dval = do_ref[...] * sg
        dgate = do_ref[...] * v_ref[...] * sg * (1.0 - sg)
        dz_ref[...] = jnp.concatenate([dval, dgate], axis=1).astype(BF16)

        @pl.when(i == 0)
        def _():
            db_ref[...] = jnp.zeros_like(db_ref)

        db_ref[0:1, :] += jnp.sum(dval, axis=0, keepdims=True)
        db_ref[1:2, :] += jnp.sum(dgate, axis=0, keepdims=True)

    row = pl.BlockSpec((tm, d), lambda i: (i, 0))
    return pl.pallas_call(
        body, grid=(n_rows // tm,), in_specs=[row, row, row],
        out_specs=[pl.BlockSpec((tm, 2 * d), lambda i: (i, 0)), pl.BlockSpec((2, d), lambda i: (0, 0))],
        out_shape=[_sds((n_rows, 2 * d), BF16), _sds((2, d), F32)],
        name="glu_bwd", compiler_params=_params(("arbitrary",)))(dout, val, gate)


def final_loss(h, target, gain, tm=256):
    n_rows, d = h.shape

    def body(h_ref, t_ref, g_ref, loss_ref, dh_ref, dhb_ref, dg_ref):
        i = pl.program_id(0)
        xh, r = _rms_hat(h_ref[...])
        err = xh * g_ref[...] - t_ref[...]
        dy = err * (1.0 / d)
        dxh = dy * g_ref[...]
        dx = r * (dxh - xh * jnp.mean(dxh * xh, axis=-1, keepdims=True))
        dh_ref[...] = dx
        dhb_ref[...] = dx.astype(BF16)

        @pl.when(i == 0)
        def _():
            loss_ref[...] = jnp.zeros_like(loss_ref)
            dg_ref[...] = jnp.zeros_like(dg_ref)

        loss_ref[...] += jnp.full((1, d), 0.5 * jnp.sum(jnp.mean(err * err, axis=-1, keepdims=True)), F32)
        dg_ref[...] += jnp.sum(dy * xh, axis=0, keepdims=True)

    row = pl.BlockSpec((tm, d), lambda i: (i, 0))
    vec = pl.BlockSpec((1, d), lambda i: (0, 0))
    return pl.pallas_call(
        body, grid=(n_rows // tm,), in_specs=[row, row, vec],
        out_specs=[vec, row, row, vec],
        out_shape=[_sds((1, d), F32), _sds((n_rows, d), F32), _sds((n_rows, d), BF16), _sds((1, d), F32)],
        name="final_loss", compiler_params=_params(("arbitrary",)))(h, target, gain)


def adamw(name, w, g, m, v, tm=256):
    n_rows, d = w.shape
    tm = tm if n_rows % tm == 0 else n_rows

    def body(w_ref, g_ref, m_ref, v_ref, d_ref, nm_ref, nv_ref):
        gv = g_ref[...]
        nm = ADAM_B1 * m_ref[...] + (1.0 - ADAM_B1) * gv
        nv = ADAM_B2 * v_ref[...] + (1.0 - ADAM_B2) * (gv * gv)
        m_hat = nm / (1.0 - ADAM_B1 ** ADAM_STEP)
        v_hat = nv / (1.0 - ADAM_B2 ** ADAM_STEP)
        d_ref[...] = -ADAM_LR * (m_hat / (jnp.sqrt(v_hat) + ADAM_EPS) + ADAM_WD * w_ref[...])
        nm_ref[...] = nm
        nv_ref[...] = nv

    row = pl.BlockSpec((tm, d), lambda i: (i, 0))
    return pl.pallas_call(
        body, grid=(n_rows // tm,), in_specs=[row] * 4, out_specs=[row] * 3,
        out_shape=[_sds((n_rows, d), F32)] * 3, name=name, compiler_params=_params(("parallel",)))(w, g, m, v)


def _adam_update(w, g, m, v):
    nm = ADAM_B1 * m + (1.0 - ADAM_B1) * g
    nv = ADAM_B2 * v + (1.0 - ADAM_B2) * (g * g)
    m_hat = nm / (1.0 - ADAM_B1 ** ADAM_STEP)
    v_hat = nv / (1.0 - ADAM_B2 ** ADAM_STEP)
    return -ADAM_LR * (m_hat / (jnp.sqrt(v_hat) + ADAM_EPS) + ADAM_WD * w), nm, nv


def adamw_native(name, ws, gs, ms, vs):
    n = len(ws)

    def body(*refs):
        w_refs, g_refs, m_refs, v_refs = refs[:n], refs[n:2 * n], refs[2 * n:3 * n], refs[3 * n:4 * n]
        d_refs, nm_refs, nv_refs = refs[4 * n:5 * n], refs[5 * n:6 * n], refs[6 * n:7 * n]
        for k in range(n):
            dl, nm, nv = _adam_update(w_refs[k][...], g_refs[k][...], m_refs[k][...], v_refs[k][...])
            d_refs[k][...] = dl
            nm_refs[k][...] = nm
            nv_refs[k][...] = nv

    vm = pl.BlockSpec(memory_space=pltpu.VMEM)
    shapes = [_sds(w.shape, F32) for w in ws]
    out = pl.pallas_call(body, in_specs=[vm] * (4 * n), out_specs=[vm] * (3 * n), out_shape=shapes * 3, name=name,
                         compiler_params=_params())(*ws, *gs, *ms, *vs)
    return list(out[:n]), list(out[n:2 * n]), list(out[2 * n:])


VEC_ROWS = {"norm_mix": 0, "norm_mlp": 2, "norm_kv": 4, "norm_final": 5, "s5_d": 6, "b_q": 7, "b_o": 8, "s5_b_glu": 9,
            "b_kv": 11, "sinks": 12, "loss": 13}


def split_vectors(where, vecs, d_shard, glu_shard):
    kvw = 2 * N_KV * HEAD_DIM
    shapes = {"norm_mix": (2, D_MODEL), "norm_mlp": (2, D_MODEL), "norm_kv": (1, D_MODEL), "norm_final": (1, D_MODEL),
              "s5_d": (1, d_shard), "b_q": (1, D_MODEL), "b_o": (1, D_MODEL), "s5_b_glu": (1, glu_shard), "b_kv": (1, kvw),
              "sinks": (1, N_Q), "loss": (1, 128)}
    names = list(shapes)

    def body(where_ref, v_ref, *o_refs):
        chip = where_ref[1]
        for name, o_ref in zip(names, o_refs):
            r0, (r, n) = VEC_ROWS[name], shapes[name]
            if name == "s5_d":
                g = jnp.zeros((1, n), F32)
                for j in range(4):
                    g = jnp.where(chip == j, v_ref[r0:r0 + 1, j * n:(j + 1) * n], g)
            elif name == "s5_b_glu":
                g = jnp.zeros((1, n), F32)
                for j in range(4):
                    row, col = r0 + (j * n) // D_MODEL, (j * n) % D_MODEL
                    g = jnp.where(chip == j, v_ref[row:row + 1, col:col + n], g)
            else:
                g = v_ref[r0:r0 + r, 0:n]
            o_ref[...] = g

    vm = pl.BlockSpec(memory_space=pltpu.VMEM)
    out = pl.pallas_call(body, in_specs=[pl.BlockSpec(memory_space=pltpu.SMEM), vm], out_specs=[vm] * len(names),
                         out_shape=[_sds(shapes[n], F32) for n in names], name="split_vectors",
                         compiler_params=_params())(where, vecs)
    return dict(zip(names, out))


def _position():
    x, y, c = lax.axis_index("x"), lax.axis_index("y"), lax.axis_index("c")
    others = [(1 - x, y), (x, 1 - y), (1 - x, 1 - y)]
    return x, y, c, others


def _window(ref, kind, chip, half, shard_shape):
    r, n = shard_shape
    if kind == "col":
        return ref.at[pl.ds(pl.multiple_of(half * (r // 2), 16), r // 2), pl.ds(pl.multiple_of(chip * n, 128), n)]
    return ref.at[pl.ds(pl.multiple_of(chip * r, 16), r), pl.ds(pl.multiple_of(half * (n // 2), 128), n // 2)]


def _half(ref, kind, half, shape):
    r, n = shape
    if kind == "col":
        return ref.at[pl.ds(pl.multiple_of(half * (r // 2), 16), r // 2), :]
    return ref.at[:, pl.ds(pl.multiple_of(half * (n // 2), 128), n // 2)]


def swap_halves(name, grads, kinds):
    nt = len(grads)
    shapes = [tuple(g.shape) for g in grads]

    def body(*refs):
        in_refs, out_refs = refs[:nt], refs[nt:2 * nt]
        send_sems, recv_sems = refs[2 * nt:]
        x, y, c, _ = _position()
        cps = []
        for t in range(nt):
            cp = pltpu.make_async_remote_copy(
                src_ref=_half(in_refs[t], kinds[t], 1 - c, shapes[t]), dst_ref=_half(out_refs[t], kinds[t], 1 - c, shapes[t]),
                send_sem=send_sems.at[t], recv_sem=recv_sems.at[t], device_id=(x, y, 1 - c), device_id_type=MESH)
            cp.start()
            cps.append(cp)
        for t in range(nt):
            mine = _half(out_refs[t], kinds[t], c, shapes[t])
            pltpu.make_async_remote_copy(
                src_ref=mine, dst_ref=mine, send_sem=send_sems.at[t], recv_sem=recv_sems.at[t],
                device_id=(x, y, 1 - c), device_id_type=MESH).wait_recv()
        for cp in cps:
            cp.wait_send()

    hbm = pl.BlockSpec(memory_space=pl.ANY)
    return pl.pallas_call(
        body, in_specs=[hbm] * nt, out_specs=[hbm] * nt, out_shape=[_sds(s, BF16) for s in shapes],
        scratch_shapes=[pltpu.SemaphoreType.DMA((nt,)), pltpu.SemaphoreType.DMA((nt,))],
        name=name, compiler_params=_params())(*grads)


def _half_spec(kind, shape, tiles):
    r, n = shape
    if kind == "col":
        tn = n // tiles
        return pl.BlockSpec((r // 2, tn), lambda i, s: (s[0], i))
    tm = r // tiles
    return pl.BlockSpec((tm, n // 2), lambda i, s: (i, s[0]))


def add_halves(name, mine, landed, kind, where, tiles=4):
    shape = tuple(mine.shape)
    r, n = shape
    out_shape = (r // 2, n) if kind == "col" else (r, n // 2)
    out_spec = (pl.BlockSpec((r // 2, n // tiles), lambda i, s: (0, i)) if kind == "col"
                else pl.BlockSpec((r // tiles, n // 2), lambda i, s: (i, 0)))

    def body(s_ref, a_ref, b_ref, o_ref):
        o_ref[...] = (a_ref[...].astype(F32) + b_ref[...].astype(F32)).astype(BF16)

    spec = _half_spec(kind, shape, tiles)
    return pl.pallas_call(
        body, grid_spec=pltpu.PrefetchScalarGridSpec(num_scalar_prefetch=1, grid=(tiles,), in_specs=[spec, spec],
                                                     out_specs=out_spec),
        out_shape=_sds(out_shape, BF16), name=name, compiler_params=_params(("parallel",)))(where, mine, landed)


def sum_shards(name, part, landed, kind, shard_shape, where, layer, n_layers, into=None, tiles=2):
    r, n = shard_shape
    if kind == "col":
        tm, width = r // 2 // tiles, n
        own = pl.BlockSpec((tm, n), lambda i, s: (i, s[1]))
        out = pl.BlockSpec((None, tm, n), lambda i, s: (layer, s[0] * tiles + i, 0))
    else:
        tm, width = r // tiles, n // 2
        own = pl.BlockSpec((tm, n // 2), lambda i, s: (s[1] * tiles + i, 0))
        out = pl.BlockSpec((None, tm, n // 2), lambda i, s: (layer, i, s[0]))

    def body(s_ref, a_ref, l_ref, *o_refs):
        o_refs[-1][...] = ((a_ref[...].astype(F32) + l_ref[0].astype(F32)) + l_ref[1].astype(F32)) + l_ref[2].astype(F32)

    in_specs = [own, pl.BlockSpec((3, tm, width), lambda i, s: (0, i, 0))]
    args, aliases = [where, part, landed], {}
    if into is not None:
        in_specs.append(pl.BlockSpec(memory_space=pl.ANY))
        args.append(into)
        aliases = {3: 0}
    return pl.pallas_call(
        body, grid_spec=pltpu.PrefetchScalarGridSpec(num_scalar_prefetch=1, grid=(tiles,), in_specs=in_specs, out_specs=out),
        out_shape=_sds((n_layers, r, n), F32), input_output_aliases=aliases, name=name,
        compiler_params=_params(("parallel",)))(*args)


def share_halves(arrays, entries):
    na, nt = len(arrays), len(entries)

    def body(*refs):
        out_refs = refs[na:2 * na]
        send_sems, recv_sems = refs[2 * na:]
        x, y, c, _ = _position()
        cps = []
        for t, (a, layer, kind) in enumerate(entries):
            shape = tuple(arrays[a].shape[1:])
            mine = _half(out_refs[a].at[layer], kind, c, shape)
            cp = pltpu.make_async_remote_copy(
                src_ref=mine, dst_ref=mine, send_sem=send_sems.at[t], recv_sem=recv_sems.at[t],
                device_id=(x, y, 1 - c), device_id_type=MESH)
            cp.start()
            cps.append(cp)
        for t, (a, layer, kind) in enumerate(entries):
            shape = tuple(arrays[a].shape[1:])
            other = _half(out_refs[a].at[layer], kind, 1 - c, shape)
            pltpu.make_async_remote_copy(
                src_ref=other, dst_ref=other, send_sem=send_sems.at[t], recv_sem=recv_sems.at[t],
                device_id=(x, y, 1 - c), device_id_type=MESH).wait_recv()
        for cp in cps:
            cp.wait_send()

    hbm = pl.BlockSpec(memory_space=pl.ANY)
    return pl.pallas_call(
        body, in_specs=[hbm] * na, out_specs=[hbm] * na, out_shape=[_sds(a.shape, F32) for a in arrays],
        input_output_aliases={i: i for i in range(na)},
        scratch_shapes=[pltpu.SemaphoreType.DMA((nt,)), pltpu.SemaphoreType.DMA((nt,))],
        name="share_halves", compiler_params=_params())(*arrays)


HBM_SPEC = pl.BlockSpec(memory_space=pltpu.HBM)
SEM_SPEC = pl.BlockSpec(memory_space=pltpu.SEMAPHORE)
ANY_SPEC = pl.BlockSpec(memory_space=pl.ANY)


def _split_params():
    return pltpu.CompilerParams(has_side_effects=pltpu.SideEffectType.DATAFLOW_SIDE_EFFECTING,
                                vmem_limit_bytes=VMEM_LIMIT_BYTES)


def _in_hbm(a):
    return pltpu.with_memory_space_constraint(a, pltpu.HBM)


def cast_place(name, shard, layer, kind, where, tiles=2):
    _, r, n = shard.shape
    tm = r // tiles
    if kind == "col":
        full, out = (r, 4 * n), pl.BlockSpec((tm, n), lambda i, s: (i, s[1]))
    else:
        full, out = (4 * r, n), pl.BlockSpec((tm, n), lambda i, s: (s[1] * tiles + i, 0))

    def body(s_ref, w_ref, o_ref):
        o_ref[...] = w_ref[...].astype(BF16)

    return pl.pallas_call(
        body, grid_spec=pltpu.PrefetchScalarGridSpec(
            num_scalar_prefetch=1, grid=(tiles,), in_specs=[pl.BlockSpec((None, tm, n), lambda i, s: (layer, i, 0))],
            out_specs=out),
        out_shape=_sds(full, BF16), name=name, compiler_params=_params(("parallel",)))(where, shard)


def gather_start(name, fulls, kinds, shard_shapes, after):
    nt = len(fulls)
    na = 0 if after is None else 1

    def body(*refs):
        full_refs = refs[:nt]
        send_sems, recv_sems, token = refs[nt + na], refs[nt + na + 1], refs[-1]
        x, y, c, others = _position()
        for t in range(nt):
            mine = _window(full_refs[t], kinds[t], 2 * x + y, c, shard_shapes[t])
            for j, (ox, oy) in enumerate(others):
                pltpu.make_async_remote_copy(
                    src_ref=mine, dst_ref=mine, send_sem=send_sems.at[3 * t + j], recv_sem=recv_sems.at[3 * t + j],
                    device_id=(ox, oy, c), device_id_type=MESH).start()
        token[...] = jnp.zeros_like(token)

    sems = pltpu.SemaphoreType.DMA((3 * nt,))
    out = pl.pallas_call(
        body, name=name, in_specs=[HBM_SPEC] * nt + [ANY_SPEC] * na,
        out_specs=(SEM_SPEC, SEM_SPEC, *[HBM_SPEC] * nt, pl.BlockSpec(memory_space=pltpu.VMEM)),
        out_shape=(sems, sems, *[pltpu.HBM(f.shape, f.dtype) for f in fulls], _sds((8, 128), F32)),
        input_output_aliases={t: 2 + t for t in range(nt)}, compiler_params=_split_params(),
    )(*[_in_hbm(f) for f in fulls], *([] if after is None else [after]))
    return out[0], out[1], list(out[2:2 + nt]), out[-1]


def gather_wait(name, send_sems, recv_sems, fulls, kinds, shard_shapes, after):
    nt = len(fulls)

    def body(*refs):
        full_refs, send_ref, recv_ref = refs[:nt], refs[nt], refs[nt + 1]
        x, y, c, others = _position()
        for t in range(nt):
            mine = _window(full_refs[t], kinds[t], 2 * x + y, c, shard_shapes[t])
            for j, (ox, oy) in enumerate(others):
                cp = pltpu.make_async_remote_copy(
                    src_ref=mine, dst_ref=_window(full_refs[t], kinds[t], 2 * ox + oy, c, shard_shapes[t]),
                    send_sem=send_ref.at[3 * t + j], recv_sem=recv_ref.at[3 * t + j],
                    device_id=(ox, oy, c), device_id_type=MESH)
                cp.wait_send()
                cp.wait_recv()

    out = pl.pallas_call(
        body, name=name, in_specs=[HBM_SPEC] * nt + [SEM_SPEC, SEM_SPEC, HBM_SPEC], out_specs=[HBM_SPEC] * nt,
        out_shape=[pltpu.HBM(f.shape, f.dtype) for f in fulls], input_output_aliases={t: t for t in range(nt)},
        compiler_params=_split_params())(*fulls, send_sems, recv_sems, _in_hbm(after))
    return list(out)


def forward_halves(name, fulls, kinds, shard_shapes):
    nt = len(fulls)

    def body(*refs):
        out_refs = refs[nt:2 * nt]
        send_sems, recv_sems = refs[2 * nt:]
        x, y, c, others = _position()
        cps = []
        for t in range(nt):
            for j, (ox, oy) in enumerate(others):
                landed = _window(out_refs[t], kinds[t], 2 * ox + oy, c, shard_shapes[t])
                cp = pltpu.make_async_remote_copy(
                    src_ref=landed, dst_ref=landed, send_sem=send_sems.at[3 * t + j], recv_sem=recv_sems.at[3 * t + j],
                    device_id=(x, y, 1 - c), device_id_type=MESH)
                cp.start()
                cps.append(cp)
        for t in range(nt):
            for j, (ox, oy) in enumerate(others):
                got = _window(out_refs[t], kinds[t], 2 * ox + oy, 1 - c, shard_shapes[t])
                pltpu.make_async_remote_copy(
                    src_ref=got, dst_ref=got, send_sem=send_sems.at[3 * t + j], recv_sem=recv_sems.at[3 * t + j],
                    device_id=(x, y, 1 - c), device_id_type=MESH).wait_recv()
        for cp in cps:
            cp.wait_send()

    out = pl.pallas_call(
        body, in_specs=[ANY_SPEC] * nt, out_specs=[ANY_SPEC] * nt, out_shape=[_sds(f.shape, f.dtype) for f in fulls],
        input_output_aliases={t: t for t in range(nt)},
        scratch_shapes=[pltpu.SemaphoreType.DMA((3 * nt,)), pltpu.SemaphoreType.DMA((3 * nt,))],
        name=name, compiler_params=_params())(*fulls)
    return list(out)


def _piece(ref, kind, chip, shard_shape):
    r, n = shard_shape
    if kind == "col":
        return ref.at[:, pl.ds(pl.multiple_of(chip * n, 128), n)]
    return ref.at[pl.ds(pl.multiple_of(chip * r, 16), r), :]


def _piece_shape(kind, shard_shape):
    r, n = shard_shape
    return (r // 2, n) if kind == "col" else (r, n // 2)


def exchange_start(name, parts, kinds, shard_shapes):
    nt = len(parts)
    lands = [lax.empty((3,) + _piece_shape(kinds[t], shard_shapes[t]), BF16) for t in range(nt)]

    def body(*refs):
        part_refs, land_refs = refs[:nt], refs[nt:2 * nt]
        send_sems, recv_sems, token = refs[2 * nt], refs[2 * nt + 1], refs[-1]
        x, y, c, others = _position()
        for t in range(nt):
            for j, (ox, oy) in enumerate(others):
                pltpu.make_async_remote_copy(
                    src_ref=_piece(part_refs[t], kinds[t], 2 * ox + oy, shard_shapes[t]), dst_ref=land_refs[t].at[j],
                    send_sem=send_sems.at[3 * t + j], recv_sem=recv_sems.at[3 * t + j],
                    device_id=(ox, oy, c), device_id_type=MESH).start()
        token[...] = jnp.zeros_like(token)

    sems = pltpu.SemaphoreType.DMA((3 * nt,))
    both = list(parts) + lands
    out = pl.pallas_call(
        body, name=name, in_specs=[HBM_SPEC] * (2 * nt),
        out_specs=(SEM_SPEC, SEM_SPEC, *[HBM_SPEC] * (2 * nt), pl.BlockSpec(memory_space=pltpu.VMEM)),
        out_shape=(sems, sems, *[pltpu.HBM(a.shape, a.dtype) for a in both], _sds((8, 128), F32)),
        input_output_aliases={t: 2 + t for t in range(2 * nt)}, compiler_params=_split_params(),
    )(*[_in_hbm(a) for a in both])
    return out[0], out[1], list(out[2:2 + nt]), list(out[2 + nt:2 + 2 * nt]), out[-1]


def exchange_wait(name, send_sems, recv_sems, parts, lands, kinds, shard_shapes, after):
    nt = len(parts)

    def body(*refs):
        part_refs, land_refs = refs[:nt], refs[nt:2 * nt]
        send_ref, recv_ref = refs[2 * nt], refs[2 * nt + 1]
        x, y, c, others = _position()
        for t in range(nt):
            for j, (ox, oy) in enumerate(others):
                cp = pltpu.make_async_remote_copy(
                    src_ref=_piece(part_refs[t], kinds[t], 2 * ox + oy, shard_shapes[t]), dst_ref=land_refs[t].at[j],
                    send_sem=send_ref.at[3 * t + j], recv_sem=recv_ref.at[3 * t + j],
                    device_id=(ox, oy, c), device_id_type=MESH)
                cp.wait_send()
                cp.wait_recv()

    both = list(parts) + list(lands)
    out = pl.pallas_call(
        body, name=name, in_specs=[HBM_SPEC] * (2 * nt) + [SEM_SPEC, SEM_SPEC, HBM_SPEC], out_specs=[HBM_SPEC] * (2 * nt),
        out_shape=[pltpu.HBM(a.shape, a.dtype) for a in both], input_output_aliases={t: t for t in range(2 * nt)},
        compiler_params=_split_params())(*both, send_sems, recv_sems, _in_hbm(after))
    return list(out[:nt]), list(out[nt:])


def all_reduce_small(name, bufs):
    n = len(bufs)
    halves = [b.shape[0] // 2 for b in bufs]

    def body(*refs):
        in_refs, out_refs, lands = refs[:n], refs[n:2 * n], refs[2 * n:3 * n]
        send_sems, recv_sems = refs[3 * n:]
        x, y, c, _ = _position()
        mine = [pl.ds(pl.multiple_of(c * h, 8), h) for h in halves]
        other = [pl.ds(pl.multiple_of((1 - c) * h, 8), h) for h in halves]
        for k in range(n):
            out_refs[k][mine[k], :] = in_refs[k][mine[k], :]
        for s, peer in enumerate([(x, y, 1 - c), (1 - x, y, c), (x, 1 - y, c)]):
            cps = []
            for k in range(n):
                src = in_refs[k].at[other[k]] if s == 0 else out_refs[k].at[mine[k]]
                cp = pltpu.make_async_remote_copy(
                    src_ref=src, dst_ref=lands[k].at[s], send_sem=send_sems.at[4 * k + s], recv_sem=recv_sems.at[4 * k + s],
                    device_id=peer, device_id_type=MESH)
                cp.start()
                cps.append(cp)
            for k, cp in enumerate(cps):
                cp.wait()
                out_refs[k][mine[k], :] = out_refs[k][mine[k], :] + lands[k][s]
        cps = []
        for k in range(n):
            cp = pltpu.make_async_remote_copy(
                src_ref=out_refs[k].at[mine[k]], dst_ref=out_refs[k].at[mine[k]], send_sem=send_sems.at[4 * k + 3],
                recv_sem=recv_sems.at[4 * k + 3], device_id=(x, y, 1 - c), device_id_type=MESH)
            cp.start()
            cps.append(cp)
        for cp in cps:
            cp.wait()

    vm = pl.BlockSpec(memory_space=pltpu.VMEM)
    out = pl.pallas_call(
        body, in_specs=[vm] * n, out_specs=[vm] * n, out_shape=[_sds(b.shape, F32) for b in bufs],
        scratch_shapes=[pltpu.VMEM((3, h, b.shape[1]), F32) for h, b in zip(halves, bufs)]
        + [pltpu.SemaphoreType.DMA((4 * n,)), pltpu.SemaphoreType.DMA((4 * n,))],
        name=name, compiler_params=_params())(*bufs)
    return list(out)


def _local_step(x, target, small, need, emit):
    d = D_MODEL
    full = {}

    def after_token(vec, token):
        return vec if token is None else vec + token[0:1, 0:1]

    lam_r, lam_i, bbar_re, bbar_im = small["s5_disc"]
    rb, rc = _s5_matrices(bbar_re, bbar_im, small["s5_c_re"], small["s5_c_im"])
    rb16, rc16 = rb.astype(BF16), rc.astype(BF16)
    lr_t, li_t = lam_r.reshape(S5_BLOCKS, 8, 128), lam_i.reshape(S5_BLOCKS, 8, 128)
    (u,) = rms_fwd("norm_mix0", x, [small["norm_mix0"]], [F32])
    ge, y2, cs = s5_fwd(u, small["s5_d"], rb16, rc16, lr_t, li_t)
    full.update(need("glu", ge))

    def norm_rows(h, gains):
        xh, _ = _rms_hat(h)
        return [xh * g for g in gains]

    def glu_epilogue(accs, e, r):
        v, gt = accs[0] + r[0], accs[1] + r[1]
        h = e[0] + v * jax.nn.sigmoid(gt)
        return [h, v, gt] + norm_rows(h, r[2:])

    h1, val, gate, n1 = mm_nn(
        "glu", ge, full["w_glu"], [0, d], d, glu_epilogue, [F32, F32, F32, BF16], extras=[x],
        rowvecs=[(small["s5_b_glu"], 0), (small["s5_b_glu"], d), (small["norm_mlp0"], 0)], tm=512, tn=d)

    def mlp_fwd(tag, h, n, w_in, w_out, next_gains):
        (r,) = mm_nn("mlp_in" + tag, n, w_in, [0], w_in.shape[1],
                     lambda accs, e, rv: [jnp.square(jnp.maximum(accs[0], 0.0))], [BF16], tm=2048)

        def epilogue(accs, e, rv):
            h_out = e[0] + accs[0]
            return [h_out] + norm_rows(h_out, rv)

        outs = mm_nn("mlp_out" + tag, r, w_out, [0], d, epilogue, [F32] + [BF16] * len(next_gains), extras=[h],
                     rowvecs=[(g, 0) for g in next_gains], tm=512, tn=d)
        return outs[0], outs[1:], (n, r)

    full.update(need("mlp0", h1))
    h2, (nkv, n2), mlp0 = mlp_fwd("0", h1, n1, full["w_in0"], full["w_out0"], [small["norm_kv"], small["norm_mix1"]])

    full.update(need("rest", h2))
    kvw = 2 * N_KV * HEAD_DIM
    (kv,) = mm_nn("kv_proj", nkv, full["w_kv"], [0], kvw, lambda accs, e, r: [accs[0] + r[0]], [BF16],
                  rowvecs=[(small["b_kv"], 0)])
    (q,) = mm_nn("q_proj", n2, full["w_q"], [0], d, lambda accs, e, r: [accs[0] + r[0]], [BF16],
                 rowvecs=[(small["b_q"], 0)])
    sinks = small["sinks"].reshape(N_Q)
    o = attn_fwd(q, kv, sinks)
    def o_epilogue(accs, e, r):
        h_out = e[0] + accs[0] + r[0]
        return [h_out] + norm_rows(h_out, r[1:])

    h3, n3 = mm_nn("o_proj", o, full["w_o"], [0], d, o_epilogue, [F32, BF16], extras=[h2],
                   rowvecs=[(small["b_o"], 0), (small["norm_mlp1"], 0)], tm=512, tn=d)
    h4, _, mlp1 = mlp_fwd("1", h3, n3, full["w_in1"], full["w_out1"], [])
    loss_tile, dh, dhb, dg_final = final_loss(h4, target, small["norm_final"])

    grads_small, grads_full = {"norm_final": dg_final}, {}
    ident = lambda acc, e, r: [acc]
    layer1 = ["w_out1", "w_in1", "w_o", "w_q", "w_kv"]
    layer0 = ["w_out0", "w_in0", "w_glu"]

    def norm_bwd_rows(x_rows, res, dys, gains):
        xh, r = _rms_hat(x_rows)
        dxh = sum(dy * g for dy, g in zip(dys, gains))
        dx = r * (dxh - xh * jnp.mean(dxh * xh, axis=-1, keepdims=True)) + res
        return dx, [jnp.sum(dy * xh, axis=0, keepdims=True) for dy in dys]

    def mlp_bwd(tag, dh, dhb, h_in, gain, w_in, w_out, saved):
        n, r = saved
        grads_full["w_out" + tag] = mm_tn("dw_out" + tag, r, dhb, tn=1024)
        (da,) = mm_nt("mlp_da" + tag, dhb, w_out, lambda acc, e, rv: [acc * 2.0 * jnp.sqrt(e[0].astype(F32))], [BF16],
                      extras=[r], tm=2048)
        grads_full["w_in" + tag] = mm_tn("dw_in" + tag, n, da, tn=1024)

        def epilogue(acc, e, rv):
            dx, dgs = norm_bwd_rows(e[0], e[1], [acc], rv)
            return [dx, dx, jnp.sum(dx, axis=0, keepdims=True)] + dgs

        dx, dxb, colsum, dg = mm_nt("mlp_dn" + tag, da, w_in, epilogue, [F32, BF16], extras=[h_in, dh], rowvecs=[gain],
                                    n_sums=2, tm=512, tk=d)
        grads_small["norm_mlp" + tag] = dg
        return dx, dxb, colsum

    dh3, dh3b, colsum3 = mlp_bwd("1", dh, dhb, h3, small["norm_mlp1"], full["w_in1"], full["w_out1"], mlp1)
    grads_small["b_o"] = colsum3
    grads_full["w_o"] = mm_tn("dw_o", o, dh3b)
    (do,) = mm_nt("attn_do", dh3b, full["w_o"], ident, [BF16])
    dq, dbq, dprev, dcur, dsink = attn_bwd(q, kv, do, sinks)
    dkv, dbkv = kv_combine(dprev, dcur)
    grads_small["b_q"], grads_small["b_kv"], grads_small["sinks"] = dbq, dbkv, dsink
    grads_full["w_q"] = mm_tn("dw_q", n2, dq)
    grads_full["w_kv"] = mm_tn("dw_kv", nkv, dkv)
    (dnkv,) = mm_nt("kv_dn", dkv, full["w_kv"], ident, [F32])
    token = emit("layer1", {n: grads_full[n] for n in layer1})

    def attn_dn_epilogue(acc, e, rv):
        dx, dgs = norm_bwd_rows(e[0], e[1], [acc, e[2]], rv)
        return [dx, dx] + dgs

    dh2, dh2b, dg_mix1, dg_kv = mm_nt("attn_dn", dq, full["w_q"], attn_dn_epilogue, [F32, BF16], extras=[h2, dh3, dnkv],
                                      rowvecs=[after_token(small["norm_mix1"], token), small["norm_kv"]], n_sums=2,
                                      tm=512, tk=d)
    grads_small["norm_mix1"], grads_small["norm_kv"] = dg_mix1, dg_kv
    dh1, _, _ = mlp_bwd("0", dh2, dh2b, h1, small["norm_mlp0"], full["w_in0"], full["w_out0"], mlp0)

    dz, db_glu = glu_bwd(dh1, val, gate)
    grads_small["s5_b_glu"] = db_glu
    grads_full["w_glu"] = mm_tn("dw_glu", ge, dz, tn=1024)
    token = emit("layer0", {n: grads_full[n] for n in layer0})
    (dy2,) = mm_nt("glu_dy", dz, full["w_glu"], lambda acc, e, rv: [acc * _gelu_grad(e[0])], [F32], extras=[y2])
    rbt16, rct16 = jnp.swapaxes(rb16, 1, 2), jnp.swapaxes(rc16, 1, 2)
    du, dd, drb, drc, dlr, dli = s5_bwd(u, dy2, after_token(small["s5_d"], token), cs, rb16, rbt16, rct16, lr_t, li_t)
    grads_small["s5_d"] = dd
    grads_small["s5_mats"] = (drb, drc, dlr, dli)
    grad_x, _, _, dg_mix0 = rms_bwd("norm_mix0_bwd", x, [du], [small["norm_mix0"]], dh1)
    grads_small["norm_mix0"] = dg_mix0
    return loss_tile, grad_x, grads_small


SMALL_NAMES = ["norm_mix", "norm_mlp", "norm_kv", "norm_final", "s5_a_re", "s5_a_im", "s5_log_dt", "s5_b_re", "s5_b_im",
               "s5_c_re", "s5_c_im", "s5_d", "s5_b_glu", "b_kv", "b_q", "sinks", "b_o"]
BIG_NAMES = ["s5_w_glu", "w_kv", "w_q", "w_o", "w_mlp_in", "w_mlp_out"]
WEIGHT_ORDER = ["norm_mix", "norm_mlp", "norm_kv", "norm_final", "s5_a_re", "s5_a_im", "s5_log_dt", "s5_b_re", "s5_b_im",
                "s5_c_re", "s5_c_im", "s5_d", "s5_w_glu", "s5_b_glu", "w_kv", "b_kv", "w_q", "b_q", "sinks", "w_o", "b_o",
                "w_mlp_in", "w_mlp_out"]


def kernel(x, norm_mix, norm_mlp, norm_kv, norm_final, s5_a_re, s5_a_im, s5_log_dt, s5_b_re, s5_b_im, s5_c_re, s5_c_im, s5_d, s5_w_glu, s5_b_glu, w_kv, b_kv, w_q, b_q, sinks, w_o, b_o, w_mlp_in, w_mlp_out, loss_target, m_norm_mix, m_norm_mlp, m_norm_kv, m_norm_final, m_s5_a_re, m_s5_a_im, m_s5_log_dt, m_s5_b_re, m_s5_b_im, m_s5_c_re, m_s5_c_im, m_s5_d, m_s5_w_glu, m_s5_b_glu, m_w_kv, m_b_kv, m_w_q, m_b_q, m_sinks, m_w_o, m_b_o, m_w_mlp_in, m_w_mlp_out, v_norm_mix, v_norm_mlp, v_norm_kv, v_norm_final, v_s5_a_re, v_s5_a_im, v_s5_log_dt, v_s5_b_re, v_s5_b_im, v_s5_c_re, v_s5_c_im, v_s5_d, v_s5_w_glu, v_s5_b_glu, v_w_kv, v_b_kv, v_w_q, v_b_q, v_sinks, v_w_o, v_b_o, v_w_mlp_in, v_w_mlp_out):
    env = dict(locals())
    w = {n: env[n] for n in WEIGHT_ORDER}
    mom = {n: env["m_" + n] for n in WEIGHT_ORDER}
    var = {n: env["v_" + n] for n in WEIGHT_ORDER}
    d = D_MODEL
    xi, yi, ci = lax.axis_index("x"), lax.axis_index("y"), lax.axis_index("c")
    chip = 2 * xi + yi
    where = jnp.stack([ci, chip]).astype(jnp.int32)

    dsh, bsh = s5_d.shape[1], s5_b_glu.shape[1]
    placed = jnp.concatenate([
        lax.dynamic_update_slice(jnp.zeros((4 * dsh,), F32), s5_d[0], (chip * dsh,)),
        lax.dynamic_update_slice(jnp.zeros((4 * bsh,), F32), s5_b_glu[0], (chip * bsh,))])
    placed = jnp.pad(placed, (0, (-placed.shape[0]) % 2048))
    placed = jnp.where(ci == 0, placed, 0.0).reshape(-1, 128)
    (gathered_rows,) = all_reduce_small("gather_vectors", [placed])
    gathered = gathered_rows.reshape(-1)
    d_full, bglu_full = gathered[:4 * dsh].reshape(1, -1), gathered[4 * dsh:].reshape(1, -1)

    big = [s5_w_glu, w_kv[None], w_q, w_o, w_mlp_in, w_mlp_out]
    entries = [(0, 0, "col"), (1, 0, "row"), (2, 0, "row"), (3, 0, "row"), (4, 0, "col"), (4, 1, "col"),
               (5, 0, "row"), (5, 1, "row")]
    names = ["w_glu", "w_kv", "w_q", "w_o", "w_in0", "w_in1", "w_out0", "w_out1"]
    kinds = dict(zip(names, [k for _, _, k in entries]))
    shard_shapes = dict(zip(names, [tuple(big[a].shape[1:]) for a, _, _ in entries]))

    placed_w = {n: cast_place("cast_" + n, big[a], layer, kind, where) for n, (a, layer, kind) in zip(names, entries)}
    gather_groups = {"glu": ["w_glu"], "mlp0": ["w_in0", "w_out0"], "rest": ["w_kv", "w_q", "w_o", "w_in1", "w_out1"]}
    started, token = {}, gathered_rows
    for group, members in gather_groups.items():
        send, recv, thru, token = gather_start(
            "gather_start_" + group, [placed_w[n] for n in members], [kinds[n] for n in members],
            [shard_shapes[n] for n in members], token)
        started[group] = (send, recv, thru)

    def need(group, after):
        members = gather_groups[group]
        ks, shapes = [kinds[n] for n in members], [shard_shapes[n] for n in members]
        send, recv, thru = started[group]
        landed = gather_wait("gather_wait_" + group, send, recv, thru, ks, shapes, after)
        return dict(zip(members, forward_halves("forward_halves_" + group, landed, ks, shapes)))

    exchanging = {}

    def emit(group, partial):
        members = list(partial)
        ks, shapes = [kinds[n] for n in members], [shard_shapes[n] for n in members]
        landed = swap_halves("swap_halves_" + group, [partial[n] for n in members], ks)
        sums = [add_halves("add_halves_" + n, partial[n], landed[t], ks[t], where) for t, n in enumerate(members)]
        send, recv, parts, lands, tok = exchange_start("exchange_start_" + group, sums, ks, shapes)
        exchanging[group] = (members, send, recv, parts, lands)
        return tok

    disc = lambda *p: _s5_discretise(p[0], p[1], p[2], p[3], p[4])
    disc_args = (s5_a_re[0], s5_a_im[0], s5_log_dt[0], s5_b_re[0], s5_b_im[0])
    disc_out, disc_vjp = jax.vjp(disc, *disc_args)
    small = {
        "norm_mix0": norm_mix[0:1] + token[0:1, 0:1], "norm_mix1": norm_mix[1:2], "norm_mlp0": norm_mlp[0:1], "norm_mlp1": norm_mlp[1:2],
        "norm_kv": norm_kv.reshape(1, d), "norm_final": norm_final.reshape(1, d), "s5_disc": disc_out,
        "s5_c_re": s5_c_re[0], "s5_c_im": s5_c_im[0], "s5_d": d_full, "s5_b_glu": bglu_full,
        "b_kv": b_kv.reshape(1, -1), "b_q": b_q, "sinks": sinks, "b_o": b_o,
    }
    loss_row, grad_x, gs = _local_step(x[0], loss_target[0], small, need, emit)

    mats = s5_compact(*gs["s5_mats"])
    rows = [gs["norm_mix0"], gs["norm_mix1"], gs["norm_mlp0"], gs["norm_mlp1"], gs["norm_kv"], gs["norm_final"], gs["s5_d"],
            gs["b_q"], gs["b_o"], gs["s5_b_glu"], gs["b_kv"], gs["sinks"], loss_row, jnp.zeros((2, d), F32)]
    vecs, mats = all_reduce_small("reduce_small", [jnp.concatenate(rows, axis=0), mats])
    grads = split_vectors(where, vecs, dsh, bsh)
    loss = grads.pop("loss")[0, 0]
    dbbar_re, dbbar_im, dc_re, dc_im, dlr, dli = _s5_unpack(mats)
    g_are, g_aim, g_dt, g_bre, g_bim = disc_vjp((dlr, dli, dbbar_re, dbbar_im))
    grads.update({"s5_a_re": g_are[None], "s5_a_im": g_aim[None], "s5_log_dt": g_dt[None], "s5_b_re": g_bre[None],
                  "s5_b_im": g_bim[None], "s5_c_re": dc_re[None], "s5_c_im": dc_im[None]})

    reduced = [None] * len(big)
    where_of = dict(zip(names, entries))
    for group, after in (("layer1", grad_x), ("layer0", mats)):
        members, send, recv, parts, lands = exchanging[group]
        ks, shapes = [kinds[n] for n in members], [shard_shapes[n] for n in members]
        parts, lands = exchange_wait("exchange_wait_" + group, send, recv, parts, lands, ks, shapes, after)
        for t, n in enumerate(members):
            a, layer, kind = where_of[n]
            reduced[a] = sum_shards("sum_shards_" + n, parts[t], lands[t], kind, shapes[t], where, layer,
                                    big[a].shape[0], into=reduced[a])
    reduced = share_halves(reduced, entries)
    for n, g in zip(BIG_NAMES, reduced):
        grads[n] = g.reshape(w[n].shape)

    delta, new_m, new_v = {}, {}, {}
    for n in BIG_NAMES:
        flat = lambda a: a.reshape(-1, a.shape[-1])
        dl, nm, nv = adamw("adamw_" + n, flat(w[n]), flat(grads[n]), flat(mom[n]), flat(var[n]))
        delta[n], new_m[n], new_v[n] = dl.reshape(w[n].shape), nm.reshape(w[n].shape), nv.reshape(w[n].shape)
    as_rows = lambda a: a.reshape(1, -1) if a.ndim == 1 else a
    sw, sg, sm, sv = ([as_rows(t[n]) for n in SMALL_NAMES] for t in (w, grads, mom, var))
    for n, a, b, c_ in zip(SMALL_NAMES, *adamw_native("adamw_small", sw, sg, sm, sv)):
        delta[n], new_m[n], new_v[n] = a, b, c_

    out = [loss.reshape(()), grad_x[None]]
    for table in (grads, delta, new_m, new_v):
        out += [table[n].reshape(w[n].shape) for n in WEIGHT_ORDER]
    return tuple(out)
```

```python
import functools
import math

import jax
import jax.numpy as jnp
from jax import lax
from jax.experimental import pallas as pl
from jax.experimental.pallas import tpu as pltpu

F32 = jnp.float32
BF16 = jnp.bfloat16

D_MODEL = 1024
S5_GROUPS = 64
S5_GROUP = 16
S5_STATE = 64
N_KV = 4
N_Q = 16
HEAD_DIM = 64
BLOCK = 128
NORM_EPS = 1e-5
LAMBDA_RE_MAX = -1e-4
ADAM_LR, ADAM_B1, ADAM_B2, ADAM_EPS, ADAM_WD, ADAM_STEP = 0.001, 0.9, 0.999, 1e-08, 0.01, 10

VMEM_LIMIT_BYTES = 56 * 1024 * 1024
S5_CHUNK = 256
S5_BLOCKS = 4
MESH = pl.DeviceIdType.MESH


def _params(sem=None):
    return pltpu.CompilerParams(dimension_semantics=sem, vmem_limit_bytes=VMEM_LIMIT_BYTES)


def _sds(shape, dtype):
    return jax.ShapeDtypeStruct(shape, dtype)


def _rms_hat(xv):
    r = lax.rsqrt(jnp.mean(xv * xv, axis=-1, keepdims=True) + NORM_EPS)
    return xv * r, r


def rms_fwd(name, x, gains, out_dtypes, tm=256):
    n_rows, d = x.shape
    ng = len(gains)

    def body(x_ref, *refs):
        xh, _ = _rms_hat(x_ref[...])
        for g_ref, o_ref in zip(refs[:ng], refs[ng:]):
            o_ref[...] = (xh * g_ref[...]).astype(o_ref.dtype)

    row = pl.BlockSpec((tm, d), lambda i: (i, 0))
    vec = pl.BlockSpec((1, d), lambda i: (0, 0))
    return pl.pallas_call(
        body, grid=(n_rows // tm,), in_specs=[row] + [vec] * ng, out_specs=[row] * ng,
        out_shape=[_sds((n_rows, d), dt) for dt in out_dtypes], name=name,
        compiler_params=_params(("parallel",)))(x, *gains)


def rms_bwd(name, x, dys, gains, res, tm=256):
    n_rows, d = x.shape
    ng = len(gains)

    def body(x_ref, res_ref, *refs):
        dy_refs, g_refs = refs[:ng], refs[ng:2 * ng]
        dx_ref, dxb_ref, cs_ref = refs[2 * ng:2 * ng + 3]
        dg_refs = refs[2 * ng + 3:]
        i = pl.program_id(0)
        xh, r = _rms_hat(x_ref[...])
        dxh = jnp.zeros_like(xh)
        dgs = []
        for dy_ref, g_ref in zip(dy_refs, g_refs):
            dy = dy_ref[...].astype(F32)
            dxh = dxh + dy * g_ref[...]
            dgs.append(jnp.sum(dy * xh, axis=0, keepdims=True))
        dx = r * (dxh - xh * jnp.mean(dxh * xh, axis=-1, keepdims=True)) + res_ref[...]
        dx_ref[...] = dx
        dxb_ref[...] = dx.astype(BF16)
        cs = jnp.sum(dx, axis=0, keepdims=True)

        @pl.when(i == 0)
        def _():
            cs_ref[...] = jnp.zeros_like(cs_ref)
            for dg_ref in dg_refs:
                dg_ref[...] = jnp.zeros_like(dg_ref)

        cs_ref[...] += cs
        for dg_ref, dg in zip(dg_refs, dgs):
            dg_ref[...] += dg

    row = pl.BlockSpec((tm, d), lambda i: (i, 0))
    vec = pl.BlockSpec((1, d), lambda i: (0, 0))
    return pl.pallas_call(
        body, grid=(n_rows // tm,), in_specs=[row, row] + [row] * ng + [vec] * ng,
        out_specs=[row, row, vec] + [vec] * ng,
        out_shape=[_sds((n_rows, d), F32), _sds((n_rows, d), BF16), _sds((1, d), F32)] + [_sds((1, d), F32)] * ng,
        name=name, compiler_params=_params(("arbitrary",)))(x, res, *dys, *gains)


def mm_nn(name, a, w, col_offsets, n_out, epilogue, out_dtypes, extras=(), rowvecs=(), tm=1024, tn=512):
    m, k = a.shape
    tm, tn = min(tm, m), min(tn, n_out)
    nw, ne, nr = len(col_offsets), len(extras), len(rowvecs)

    def body(a_ref, *refs):
        w_refs, e_refs, r_refs = refs[:nw], refs[nw:nw + ne], refs[nw + ne:nw + ne + nr]
        o_refs = refs[nw + ne + nr:]
        av = a_ref[...]
        accs = [jnp.dot(av, w_ref[...], preferred_element_type=F32) for w_ref in w_refs]
        outs = epilogue(accs, [e[...] for e in e_refs], [r[...] for r in r_refs])
        for o_ref, o in zip(o_refs, outs):
            o_ref[...] = o.astype(o_ref.dtype)

    def wspec(off):
        return pl.BlockSpec((k, tn), lambda j, i, off=off: (0, off // tn + j))

    def rspec(off):
        return pl.BlockSpec((1, tn), lambda j, i, off=off: (0, off // tn + j))

    tile = pl.BlockSpec((tm, tn), lambda j, i: (i, j))
    in_specs = ([pl.BlockSpec((tm, k), lambda j, i: (i, 0))] + [wspec(o) for o in col_offsets]
                + [tile] * ne + [rspec(o) for _, o in rowvecs])
    return pl.pallas_call(
        body, grid=(n_out // tn, m // tm), in_specs=in_specs, out_specs=[tile] * len(out_dtypes),
        out_shape=[_sds((m, n_out), dt) for dt in out_dtypes], name=name,
        compiler_params=_params(("parallel", "parallel")))(a, *([w] * nw), *extras, *[r for r, _ in rowvecs])


def mm_nt(name, g, w, epilogue, out_dtypes, extras=(), rowvecs=(), n_sums=0, tm=512, tk=512):
    m, n = g.shape
    k = w.shape[0]
    tm, tk = min(tm, m), min(tk, k)
    ne, nr, no = len(extras), len(rowvecs), len(out_dtypes)

    def body(g_ref, w_ref, *refs):
        e_refs, r_refs, o_refs, s_refs = refs[:ne], refs[ne:ne + nr], refs[ne + nr:ne + nr + no], refs[ne + nr + no:]
        acc = lax.dot_general(g_ref[...], w_ref[...], (((1,), (1,)), ((), ())), preferred_element_type=F32)
        outs = epilogue(acc, [e[...] for e in e_refs], [r[...] for r in r_refs])
        for o_ref, o in zip(o_refs, outs[:no]):
            o_ref[...] = o.astype(o_ref.dtype)
        if n_sums:
            @pl.when(pl.program_id(0) == 0)
            def _():
                for s_ref in s_refs:
                    s_ref[...] = jnp.zeros_like(s_ref)

            for s_ref, val in zip(s_refs, outs[no:]):
                s_ref[...] += val

    tile = pl.BlockSpec((tm, tk), lambda i, j: (i, j))
    vec = pl.BlockSpec((1, tk), lambda i, j: (0, j))
    sem = ("arbitrary", "parallel") if n_sums else ("parallel", "parallel")
    return pl.pallas_call(
        body, grid=(m // tm, k // tk),
        in_specs=[pl.BlockSpec((tm, n), lambda i, j: (i, 0)), pl.BlockSpec((tk, n), lambda i, j: (j, 0))]
        + [tile] * ne + [vec] * nr,
        out_specs=[tile] * no + [vec] * n_sums,
        out_shape=[_sds((m, k), dt) for dt in out_dtypes] + [_sds((1, k), F32)] * n_sums, name=name,
        compiler_params=_params(sem))(g, w, *extras, *rowvecs)


def mm_tn(name, a, g, tk=512, tn=512):
    m, k = a.shape
    n = g.shape[1]
    tk, tn = min(tk, k), min(tn, n)

    def body(a_ref, g_ref, o_ref):
        acc = lax.dot_general(a_ref[...], g_ref[...], (((0,), (0,)), ((), ())), preferred_element_type=F32)
        o_ref[...] = acc.astype(o_ref.dtype)

    return pl.pallas_call(
        body, grid=(k // tk, n // tn),
        in_specs=[pl.BlockSpec((m, tk), lambda i, j: (0, i)), pl.BlockSpec((m, tn), lambda i, j: (0, j))],
        out_specs=pl.BlockSpec((tk, tn), lambda i, j: (i, j)), out_shape=_sds((k, n), BF16), name=name,
        compiler_params=_params(("parallel", "parallel")))(a, g)


def _row_mask(tc):
    row = lax.broadcasted_iota(jnp.int32, (8 * tc, 256), 0) % 8
    col = lax.broadcasted_iota(jnp.int32, (8 * tc, 256), 1) // 32
    return row == col


def _expand_rows(expand_ref, val, mask):
    rep = jnp.dot(expand_ref[...], val.astype(BF16), preferred_element_type=F32)
    return jnp.where(mask, rep, 0.0).astype(BF16)


def _staged(ref):
    return jnp.concatenate([ref[0], ref[1]], axis=1)


def _stage(ref, val):
    ref[0] = val[:, 0:128]
    ref[1] = val[:, 128:256]


def _gather_rows(src_ref, tc):
    halves = []
    for half in range(2):
        col = lax.broadcasted_iota(jnp.int32, (tc, 128), 1) // 32 + 4 * half
        out = jnp.zeros((tc, 128), F32)
        for s8 in range(4 * half, 4 * half + 4):
            out = jnp.where(col == s8, src_ref.at[half][pl.ds(s8, tc, stride=8), :], out)
        halves.append(out)
    return jnp.concatenate(halves, axis=1)


def _gelu(x):
    c = math.sqrt(2.0 / math.pi)
    return 0.5 * x * (1.0 + jnp.tanh(c * (x + 0.044715 * x * x * x)))


def _gelu_grad(x):
    c = math.sqrt(2.0 / math.pi)
    t = jnp.tanh(c * (x + 0.044715 * x * x * x))
    return 0.5 * (1.0 + t) + 0.5 * x * (1.0 - t * t) * c * (1.0 + 3.0 * 0.044715 * x * x)


def _expansion(tc):
    return (jnp.arange(8 * tc)[:, None] // 8 == jnp.arange(tc)[None, :]).astype(BF16)


def s5_fwd(u, d_skip, rb, rc, lam_r, lam_i):
    n_rows = u.shape[0]
    tc = min(S5_CHUNK, n_rows)
    nc = n_rows // tc

    def body(u_ref, d_ref, ex_ref, rb_ref, rc_ref, lr_ref, li_ref, ge_ref, y2_ref, cs_ref, bux, yrows, carry):
        i = pl.program_id(0)

        @pl.when(i == 0)
        def _():
            carry[...] = jnp.zeros_like(carry)

        cs_ref[0] = carry[...]
        mask = _row_mask(tc)
        for blk in range(S5_BLOCKS):
            lhs = _expand_rows(ex_ref, u_ref[:, blk * 256:(blk + 1) * 256], mask)
            bux[blk] = jnp.dot(lhs, rb_ref[blk], preferred_element_type=F32)
        lam = [(lr_ref[blk], li_ref[blk]) for blk in range(S5_BLOCKS)]

        def step(t, c):
            r0 = pl.multiple_of(t * 8, 8)
            new = []
            for blk in range(S5_BLOCKS):
                xr, xi = c[2 * blk], c[2 * blk + 1]
                lr, li = lam[blk]
                nr = lr * xr - li * xi + bux[blk, pl.ds(r0, 8), 0:128]
                ni = lr * xi + li * xr + bux[blk, pl.ds(r0, 8), 128:256]
                bux[blk, pl.ds(r0, 8), 0:128] = nr
                bux[blk, pl.ds(r0, 8), 128:256] = ni
                new += [nr, ni]
            return tuple(new)

        c0 = []
        for blk in range(S5_BLOCKS):
            c0 += [carry[blk, :, 0:128], carry[blk, :, 128:256]]
        cn = lax.fori_loop(0, tc, step, tuple(c0), unroll=4)
        for blk in range(S5_BLOCKS):
            carry[blk, :, 0:128] = cn[2 * blk]
            carry[blk, :, 128:256] = cn[2 * blk + 1]
        for blk in range(S5_BLOCKS):
            _stage(yrows, jnp.dot(bux[blk].astype(BF16), rc_ref[blk], preferred_element_type=F32))
            sl = slice(blk * 256, (blk + 1) * 256)
            y2 = _gather_rows(yrows, tc) + d_ref[:, sl] * u_ref[:, sl]
            y2_ref[:, sl] = y2
            ge_ref[:, sl] = _gelu(y2).astype(BF16)

    row = pl.BlockSpec((tc, D_MODEL), lambda i: (i, 0))
    mat = pl.BlockSpec((S5_BLOCKS, 256, 256), lambda i: (0, 0, 0))
    lamspec = pl.BlockSpec((S5_BLOCKS, 8, 128), lambda i: (0, 0, 0))
    return pl.pallas_call(
        body, grid=(nc,),
        in_specs=[row, pl.BlockSpec((1, D_MODEL), lambda i: (0, 0)), pl.BlockSpec((8 * tc, tc), lambda i: (0, 0)),
                  mat, mat, lamspec, lamspec],
        out_specs=[row, row, pl.BlockSpec((1, S5_BLOCKS, 8, 256), lambda i: (i, 0, 0, 0))],
        out_shape=[_sds((n_rows, D_MODEL), BF16), _sds((n_rows, D_MODEL), F32), _sds((nc, S5_BLOCKS, 8, 256), F32)],
        scratch_shapes=[pltpu.VMEM((S5_BLOCKS, 8 * tc, 256), F32), pltpu.VMEM((2, 8 * tc, 128), F32),
                        pltpu.VMEM((S5_BLOCKS, 8, 256), F32)],
        name="s5_fwd", compiler_params=_params(("arbitrary",)))(u, d_skip, _expansion(tc), rb, rc, lam_r, lam_i)


def s5_bwd(u, dy2, d_skip, cs, rb, rbt, rct, lam_r, lam_i):
    n_rows = u.shape[0]
    tc = min(S5_CHUNK, n_rows)
    nc = n_rows // tc

    def body(u_ref, dy_ref, d_ref, cs_ref, ex_ref, rb_ref, rbt_ref, rct_ref, lr_ref, li_ref,
             du_ref, dd_ref, drb_ref, drc_ref, dlr_ref, dli_ref, tmp, lhsu, lhsd, xs, adj, acarry):
        i = pl.program_id(0)

        @pl.when(i == 0)
        def _():
            acarry[...] = jnp.zeros_like(acarry)
            dd_ref[...] = jnp.zeros_like(dd_ref)
            drb_ref[...] = jnp.zeros_like(drb_ref)
            drc_ref[...] = jnp.zeros_like(drc_ref)
            dlr_ref[...] = jnp.zeros_like(dlr_ref)
            dli_ref[...] = jnp.zeros_like(dli_ref)

        dd_ref[...] += jnp.sum(dy_ref[...] * u_ref[...], axis=0, keepdims=True)
        mask = _row_mask(tc)
        for blk in range(S5_BLOCKS):
            sl = slice(blk * 256, (blk + 1) * 256)
            lhsu[blk] = _expand_rows(ex_ref, u_ref[:, sl], mask)
            xs[blk] = jnp.dot(lhsu[blk], rb_ref[blk], preferred_element_type=F32)
            lhsd[blk] = _expand_rows(ex_ref, dy_ref[:, sl], mask)
            adj[blk] = jnp.dot(lhsd[blk], rct_ref[blk], preferred_element_type=F32)
        lam = [(lr_ref[blk], li_ref[blk]) for blk in range(S5_BLOCKS)]

        def fstep(t, c):
            r0 = pl.multiple_of(t * 8, 8)
            new = []
            for blk in range(S5_BLOCKS):
                xr, xi = c[2 * blk], c[2 * blk + 1]
                lr, li = lam[blk]
                nr = lr * xr - li * xi + xs[blk, pl.ds(r0, 8), 0:128]
                ni = lr * xi + li * xr + xs[blk, pl.ds(r0, 8), 128:256]
                xs[blk, pl.ds(r0, 8), 0:128] = nr
                xs[blk, pl.ds(r0, 8), 128:256] = ni
                new += [nr, ni]
            return tuple(new)

        c0 = []
        for blk in range(S5_BLOCKS):
            c0 += [cs_ref[0, blk, :, 0:128], cs_ref[0, blk, :, 128:256]]
        lax.fori_loop(0, tc, fstep, tuple(c0), unroll=4)

        def bstep(k, c):
            t = tc - 1 - k
            r0 = pl.multiple_of(t * 8, 8)
            rp = pl.multiple_of(jnp.maximum(t - 1, 0) * 8, 8)
            first = t == 0
            new_a, new_g = [], []
            for blk in range(S5_BLOCKS):
                ar, ai = c[0][2 * blk], c[0][2 * blk + 1]
                glr, gli = c[1][2 * blk], c[1][2 * blk + 1]
                lr, li = lam[blk]
                nr = lr * ar + li * ai + adj[blk, pl.ds(r0, 8), 0:128]
                ni = lr * ai - li * ar + adj[blk, pl.ds(r0, 8), 128:256]
                adj[blk, pl.ds(r0, 8), 0:128] = nr
                adj[blk, pl.ds(r0, 8), 128:256] = ni
                pr = jnp.where(first, cs_ref[0, blk, :, 0:128], xs[blk, pl.ds(rp, 8), 0:128])
                pi = jnp.where(first, cs_ref[0, blk, :, 128:256], xs[blk, pl.ds(rp, 8), 128:256])
                new_a += [nr, ni]
                new_g += [glr + nr * pr + ni * pi, gli + ni * pr - nr * pi]
            return tuple(new_a), tuple(new_g)

        a0, g0 = [], []
        for blk in range(S5_BLOCKS):
            a0 += [acarry[blk, :, 0:128], acarry[blk, :, 128:256]]
            g0 += [dlr_ref[blk], dli_ref[blk]]
        an, gn = lax.fori_loop(0, tc, bstep, (tuple(a0), tuple(g0)), unroll=2)
        for blk in range(S5_BLOCKS):
            acarry[blk, :, 0:128] = an[2 * blk]
            acarry[blk, :, 128:256] = an[2 * blk + 1]
            dlr_ref[blk] = gn[2 * blk]
            dli_ref[blk] = gn[2 * blk + 1]
        for blk in range(S5_BLOCKS):
            sl = slice(blk * 256, (blk + 1) * 256)
            ab = adj[blk].astype(BF16)
            _stage(tmp, jnp.dot(ab, rbt_ref[blk], preferred_element_type=F32))
            du_ref[:, sl] = _gather_rows(tmp, tc) + d_ref[:, sl] * dy_ref[:, sl]
            drb_ref[blk] += lax.dot_general(lhsu[blk], ab, (((0,), (0,)), ((), ())), preferred_element_type=F32)
            drc_ref[blk] += lax.dot_general(lhsd[blk], xs[blk].astype(BF16), (((0,), (0,)), ((), ())),
                                            preferred_element_type=F32)

    rev = pl.BlockSpec((tc, D_MODEL), lambda i: (nc - 1 - i, 0))
    vec = pl.BlockSpec((1, D_MODEL), lambda i: (0, 0))
    mat = pl.BlockSpec((S5_BLOCKS, 256, 256), lambda i: (0, 0, 0))
    lamspec = pl.BlockSpec((S5_BLOCKS, 8, 128), lambda i: (0, 0, 0))
    big = pltpu.VMEM((S5_BLOCKS, 8 * tc, 256), F32)
    bigb = pltpu.VMEM((S5_BLOCKS, 8 * tc, 256), BF16)
    return pl.pallas_call(
        body, grid=(nc,),
        in_specs=[rev, rev, vec, pl.BlockSpec((1, S5_BLOCKS, 8, 256), lambda i: (nc - 1 - i, 0, 0, 0)),
                  pl.BlockSpec((8 * tc, tc), lambda i: (0, 0)), mat, mat, mat, lamspec, lamspec],
        out_specs=[rev, vec, mat, mat, lamspec, lamspec],
        out_shape=[_sds((n_rows, D_MODEL), F32), _sds((1, D_MODEL), F32), _sds((S5_BLOCKS, 256, 256), F32),
                   _sds((S5_BLOCKS, 256, 256), F32), _sds((S5_BLOCKS, 8, 128), F32), _sds((S5_BLOCKS, 8, 128), F32)],
        scratch_shapes=[pltpu.VMEM((2, 8 * tc, 128), F32), bigb, bigb, big, big, pltpu.VMEM((S5_BLOCKS, 8, 256), F32)],
        name="s5_bwd", compiler_params=_params(("arbitrary",)))(u, dy2, d_skip, cs, _expansion(tc), rb, rbt, rct, lam_r, lam_i)


def _s5_discretise(a_re, a_im, log_dt, b_re, b_im):
    lam = lax.complex(jnp.minimum(a_re, LAMBDA_RE_MAX), a_im)
    dt = jnp.exp(log_dt)[:, None]
    lam_bar = jnp.exp(lam * dt)
    b_bar = ((lam_bar - 1.0) / lam)[:, :, None] * lax.complex(b_re, b_im)
    return jnp.real(lam_bar), jnp.imag(lam_bar), jnp.real(b_bar), jnp.imag(b_bar)


def _s5_matrices(bbar_re, bbar_im, c_re, c_im):
    eye2 = jnp.eye(2, dtype=F32)
    bst = jnp.stack([bbar_re, bbar_im]).reshape(2, S5_BLOCKS, 8, 2, S5_STATE, S5_GROUP)
    bt = jnp.transpose(bst, (1, 2, 3, 5, 0, 4))
    rb = (bt[:, :, :, :, :, None, :] * eye2[None, None, :, None, None, :, None]).reshape(S5_BLOCKS, 256, 256)
    cst = jnp.stack([c_re, -c_im]).reshape(2, S5_BLOCKS, 8, 2, S5_GROUP, S5_STATE)
    ct = jnp.transpose(cst, (1, 0, 5, 2, 3, 4))
    rc = (ct[:, :, None, :, :, :, :] * eye2[None, None, :, None, None, :, None]).reshape(S5_BLOCKS, 256, 256)
    return rb, rc


def s5_compact(drb, drct, dlr, dli):
    def body(drb_ref, drct_ref, dlr_ref, dli_ref, o_ref):
        even = (lax.broadcasted_iota(jnp.int32, (256, 64), 0) // S5_GROUP) % 2 == 0
        for blk in range(S5_BLOCKS):
            for k, ref in enumerate((drb_ref, drct_ref)):
                m = ref[blk]
                re = jnp.where(even, m[:, 0:64], m[:, 64:128])
                im = jnp.where(even, m[:, 128:192], m[:, 192:256])
                o_ref[pl.ds(k * 1024 + blk * 256, 256), :] = jnp.concatenate([re, im], axis=1)
            o_ref[pl.ds(2048 + blk * 8, 8), :] = dlr_ref[blk]
            o_ref[pl.ds(2080 + blk * 8, 8), :] = dli_ref[blk]

    vm = pl.BlockSpec(memory_space=pltpu.VMEM)
    return pl.pallas_call(body, in_specs=[vm] * 4, out_specs=vm, out_shape=_sds((2112, 128), F32), name="s5_compact",
                          compiler_params=_params())(drb, drct, dlr, dli)


def _s5_unpack(mats):
    bm = mats[0:1024].reshape(S5_GROUPS, S5_GROUP, 128)
    cm = mats[1024:2048].reshape(S5_GROUPS, S5_GROUP, 128)
    swap = lambda t: jnp.transpose(t, (0, 2, 1))
    return (swap(bm[:, :, 0:64]), swap(bm[:, :, 64:128]), cm[:, :, 0:64], -cm[:, :, 64:128],
            mats[2048:2080].reshape(S5_GROUPS, S5_STATE), mats[2080:2112].reshape(S5_GROUPS, S5_STATE))


NEG = -1e30


GROUP = N_Q // N_KV


def _attn_masks(n):
    qi = lax.broadcasted_iota(jnp.int32, (GROUP * BLOCK, BLOCK), 0) % BLOCK
    kj = lax.broadcasted_iota(jnp.int32, (GROUP * BLOCK, BLOCK), 1)
    return jnp.logical_and(kj > qi, n > 0), kj <= qi


def _stack_heads(ref, kh):
    return jnp.concatenate([ref[:, (GROUP * kh + g) * HEAD_DIM:(GROUP * kh + g + 1) * HEAD_DIM] for g in range(GROUP)], axis=0)


def _unstack_heads(val):
    return jnp.concatenate([val[g * BLOCK:(g + 1) * BLOCK] for g in range(GROUP)], axis=1)


def _sink_column(sink_ref, kh):
    grp = lax.broadcasted_iota(jnp.int32, (GROUP * BLOCK, 1), 0) // BLOCK
    col = jnp.zeros((GROUP * BLOCK, 1), F32)
    for g in range(GROUP):
        col = jnp.where(grp == g, sink_ref[GROUP * kh + g], col)
    return col, grp


def _attn_exp(q4, kp, kc, sink, mask_p, mask_c):
    scale = 1.0 / math.sqrt(HEAD_DIM)
    nt = (((1,), (1,)), ((), ()))
    sp = jnp.where(mask_p, lax.dot_general(q4, kp, nt, preferred_element_type=F32) * scale, NEG)
    sc = jnp.where(mask_c, lax.dot_general(q4, kc, nt, preferred_element_type=F32) * scale, NEG)
    m = jnp.maximum(jnp.maximum(jnp.max(sp, axis=-1, keepdims=True), jnp.max(sc, axis=-1, keepdims=True)), sink)
    pp = jnp.exp(sp - m)
    pc = jnp.exp(sc - m)
    ps = jnp.exp(sink - m)
    inv = 1.0 / (jnp.sum(pp, axis=-1, keepdims=True) + jnp.sum(pc, axis=-1, keepdims=True) + ps)
    return pp, pc, ps, inv


def attn_fwd(q, kv, sinks):
    n_rows = q.shape[0]
    nb = n_rows // BLOCK

    def body(sink_ref, q_ref, kvp_ref, kvc_ref, o_ref):
        n = pl.program_id(0)
        mask_p, mask_c = _attn_masks(n)
        outs = []
        for kh in range(N_KV):
            ks, vs = slice(kh * HEAD_DIM, (kh + 1) * HEAD_DIM), slice((N_KV + kh) * HEAD_DIM, (N_KV + kh + 1) * HEAD_DIM)
            sink, _ = _sink_column(sink_ref, kh)
            pp, pc, _, inv = _attn_exp(_stack_heads(q_ref, kh), kvp_ref[:, ks], kvc_ref[:, ks], sink, mask_p, mask_c)
            o4 = (jnp.dot(pp.astype(BF16), kvp_ref[:, vs], preferred_element_type=F32)
                  + jnp.dot(pc.astype(BF16), kvc_ref[:, vs], preferred_element_type=F32)) * inv
            outs.append(_unstack_heads(o4))
        o_ref[...] = jnp.concatenate(outs, axis=1).astype(BF16)

    kvw = 2 * N_KV * HEAD_DIM
    return pl.pallas_call(
        body, grid=(nb,),
        in_specs=[pl.BlockSpec(memory_space=pltpu.SMEM), pl.BlockSpec((BLOCK, D_MODEL), lambda n: (n, 0)),
                  pl.BlockSpec((BLOCK, kvw), lambda n: (jnp.maximum(n - 1, 0), 0)), pl.BlockSpec((BLOCK, kvw), lambda n: (n, 0))],
        out_specs=pl.BlockSpec((BLOCK, D_MODEL), lambda n: (n, 0)), out_shape=_sds((n_rows, D_MODEL), BF16),
        name="attn_fwd", compiler_params=_params(("parallel",)))(sinks, q, kv, kv)


def attn_bwd(q, kv, do, sinks):
    n_rows = q.shape[0]
    nb = n_rows // BLOCK
    kvw = 2 * N_KV * HEAD_DIM
    tn = (((0,), (0,)), ((), ()))
    nt = (((1,), (1,)), ((), ()))
    scale = 1.0 / math.sqrt(HEAD_DIM)

    def body(sink_ref, q_ref, kvp_ref, kvc_ref, do_ref, dq_ref, dbq_ref, dprev_ref, dcur_ref, dsink_ref):
        n = pl.program_id(0)
        mask_p, mask_c = _attn_masks(n)
        lane = lax.broadcasted_iota(jnp.int32, (1, D_MODEL), 1)
        dqs, dsink = [], jnp.zeros((1, D_MODEL), F32)
        dkp, dkc, dvp, dvc = [], [], [], []
        for kh in range(N_KV):
            ks, vs = slice(kh * HEAD_DIM, (kh + 1) * HEAD_DIM), slice((N_KV + kh) * HEAD_DIM, (N_KV + kh + 1) * HEAD_DIM)
            q4, do4 = _stack_heads(q_ref, kh), _stack_heads(do_ref, kh)
            kp, kc, vp, vc = kvp_ref[:, ks], kvc_ref[:, ks], kvp_ref[:, vs], kvc_ref[:, vs]
            sink, grp = _sink_column(sink_ref, kh)
            pp, pc, ps, inv = _attn_exp(q4, kp, kc, sink, mask_p, mask_c)
            pp, pc = pp * inv, pc * inv
            dpp = lax.dot_general(do4, vp, nt, preferred_element_type=F32)
            dpc = lax.dot_general(do4, vc, nt, preferred_element_type=F32)
            delta = jnp.sum(pp * dpp, axis=-1, keepdims=True) + jnp.sum(pc * dpc, axis=-1, keepdims=True)
            dsp = (pp * (dpp - delta) * scale).astype(BF16)
            dsc = (pc * (dpc - delta) * scale).astype(BF16)
            dsk = ps * inv * delta
            for g in range(GROUP):
                dsink = dsink + jnp.where(lane == GROUP * kh + g, -jnp.sum(jnp.where(grp == g, dsk, 0.0)), 0.0)
            dqs.append(_unstack_heads(jnp.dot(dsp, kp, preferred_element_type=F32)
                                      + jnp.dot(dsc, kc, preferred_element_type=F32)))
            dkp.append(lax.dot_general(dsp, q4, tn, preferred_element_type=F32))
            dkc.append(lax.dot_general(dsc, q4, tn, preferred_element_type=F32))
            dvp.append(lax.dot_general(pp.astype(BF16), do4, tn, preferred_element_type=F32))
            dvc.append(lax.dot_general(pc.astype(BF16), do4, tn, preferred_element_type=F32))
        dq = jnp.concatenate(dqs, axis=1)
        dq_ref[...] = dq.astype(BF16)
        dprev_ref[0] = jnp.concatenate(dkp + dvp, axis=1)
        dcur_ref[0] = jnp.concatenate(dkc + dvc, axis=1)

        @pl.when(n == 0)
        def _():
            dbq_ref[...] = jnp.zeros_like(dbq_ref)
            dsink_ref[...] = jnp.zeros_like(dsink_ref)

        dbq_ref[...] += jnp.sum(dq, axis=0, keepdims=True)
        dsink_ref[...] += dsink

    blk = pl.BlockSpec((BLOCK, D_MODEL), lambda n: (n, 0))
    part = pl.BlockSpec((1, BLOCK, kvw), lambda n: (n, 0, 0))
    return pl.pallas_call(
        body, grid=(nb,),
        in_specs=[pl.BlockSpec(memory_space=pltpu.SMEM), blk,
                  pl.BlockSpec((BLOCK, kvw), lambda n: (jnp.maximum(n - 1, 0), 0)), pl.BlockSpec((BLOCK, kvw), lambda n: (n, 0)), blk],
        out_specs=[blk, pl.BlockSpec((1, D_MODEL), lambda n: (0, 0)), part, part, pl.BlockSpec((1, D_MODEL), lambda n: (0, 0))],
        out_shape=[_sds((n_rows, D_MODEL), BF16), _sds((1, D_MODEL), F32), _sds((nb, BLOCK, kvw), F32),
                   _sds((nb, BLOCK, kvw), F32), _sds((1, D_MODEL), F32)],
        name="attn_bwd", compiler_params=_params(("arbitrary",)))(sinks, q, kv, kv, do)


def kv_combine(dprev, dcur):
    nb, _, kvw = dprev.shape

    def body(dcur_ref, dnext_ref, dkv_ref, db_ref):
        m = pl.program_id(0)
        dkv = dcur_ref[0] + jnp.where(m + 1 < nb, dnext_ref[0], 0.0)
        dkv_ref[...] = dkv.astype(BF16)

        @pl.when(m == 0)
        def _():
            db_ref[...] = jnp.zeros_like(db_ref)

        db_ref[:, 0:kvw] += jnp.sum(dkv, axis=0, keepdims=True)

    return pl.pallas_call(
        body, grid=(nb,),
        in_specs=[pl.BlockSpec((1, BLOCK, kvw), lambda m: (m, 0, 0)),
                  pl.BlockSpec((1, BLOCK, kvw), lambda m: (jnp.minimum(m + 1, nb - 1), 0, 0))],
        out_specs=[pl.BlockSpec((BLOCK, kvw), lambda m: (m, 0)), pl.BlockSpec((1, D_MODEL), lambda m: (0, 0))],
        out_shape=[_sds((nb * BLOCK, kvw), BF16), _sds((1, D_MODEL), F32)],
        name="kv_combine", compiler_params=_params(("arbitrary",)))(dcur, dprev)


def glu_bwd(dout, val, gate, tm=256):
    n_rows, d = dout.shape

    def body(do_ref, v_ref, g_ref, dz_ref, db_ref):
        i = pl.program_id(0)
        sg = jax.nn.sigmoid(g_ref[...])
        dval = do_ref[...] * sg
        dgate = do_ref[...] * v_ref[...] * sg * (1.0 - sg)
        dz_ref[...] = jnp.concatenate([dval, dgate], axis=1).astype(BF16)

        @pl.when(i == 0)
        def _():
            db_ref[...] = jnp.zeros_like(db_ref)

        db_ref[0:1, :] += jnp.sum(dval, axis=0, keepdims=True)
        db_ref[1:2, :] += jnp.sum(dgate, axis=0, keepdims=True)

    row = pl.BlockSpec((tm, d), lambda i: (i, 0))
    return pl.pallas_call(
        body, grid=(n_rows // tm,), in_specs=[row, row, row],
        out_specs=[pl.BlockSpec((tm, 2 * d), lambda i: (i, 0)), pl.BlockSpec((2, d), lambda i: (0, 0))],
        out_shape=[_sds((n_rows, 2 * d), BF16), _sds((2, d), F32)],
        name="glu_bwd", compiler_params=_params(("arbitrary",)))(dout, val, gate)


def final_loss(h, target, gain, tm=256):
    n_rows, d = h.shape

    def body(h_ref, t_ref, g_ref, loss_ref, dh_ref, dhb_ref, dg_ref):
        i = pl.program_id(0)
        xh, r = _rms_hat(h_ref[...])
        err = xh * g_ref[...] - t_ref[...]
        dy = err * (1.0 / d)
        dxh = dy * g_ref[...]
        dx = r * (dxh - xh * jnp.mean(dxh * xh, axis=-1, keepdims=True))
        dh_ref[...] = dx
        dhb_ref[...] = dx.astype(BF16)

        @pl.when(i == 0)
        def _():
            loss_ref[...] = jnp.zeros_like(loss_ref)
            dg_ref[...] = jnp.zeros_like(dg_ref)

        loss_ref[...] += jnp.full((1, d), 0.5 * jnp.sum(jnp.mean(err * err, axis=-1, keepdims=True)), F32)
        dg_ref[...] += jnp.sum(dy * xh, axis=0, keepdims=True)

    row = pl.BlockSpec((tm, d), lambda i: (i, 0))
    vec = pl.BlockSpec((1, d), lambda i: (0, 0))
    return pl.pallas_call(
        body, grid=(n_rows // tm,), in_specs=[row, row, vec],
        out_specs=[vec, row, row, vec],
        out_shape=[_sds((1, d), F32), _sds((n_rows, d), F32), _sds((n_rows, d), BF16), _sds((1, d), F32)],
        name="final_loss", compiler_params=_params(("arbitrary",)))(h, target, gain)


def adamw(name, w, g, m, v, tm=256):
    n_rows, d = w.shape
    tm = tm if n_rows % tm == 0 else n_rows

    def body(w_ref, g_ref, m_ref, v_ref, d_ref, nm_ref, nv_ref):
        gv = g_ref[...]
        nm = ADAM_B1 * m_ref[...] + (1.0 - ADAM_B1) * gv
        nv = ADAM_B2 * v_ref[...] + (1.0 - ADAM_B2) * (gv * gv)
        m_hat = nm / (1.0 - ADAM_B1 ** ADAM_STEP)
        v_hat = nv / (1.0 - ADAM_B2 ** ADAM_STEP)
        d_ref[...] = -ADAM_LR * (m_hat / (jnp.sqrt(v_hat) + ADAM_EPS) + ADAM_WD * w_ref[...])
        nm_ref[...] = nm
        nv_ref[...] = nv

    row = pl.BlockSpec((tm, d), lambda i: (i, 0))
    return pl.pallas_call(
        body, grid=(n_rows // tm,), in_specs=[row] * 4, out_specs=[row] * 3,
        out_shape=[_sds((n_rows, d), F32)] * 3, name=name, compiler_params=_params(("parallel",)))(w, g, m, v)


def _adam_update(w, g, m, v):
    nm = ADAM_B1 * m + (1.0 - ADAM_B1) * g
    nv = ADAM_B2 * v + (1.0 - ADAM_B2) * (g * g)
    m_hat = nm / (1.0 - ADAM_B1 ** ADAM_STEP)
    v_hat = nv / (1.0 - ADAM_B2 ** ADAM_STEP)
    return -ADAM_LR * (m_hat / (jnp.sqrt(v_hat) + ADAM_EPS) + ADAM_WD * w), nm, nv


def adamw_native(name, ws, gs, ms, vs):
    n = len(ws)

    def body(*refs):
        w_refs, g_refs, m_refs, v_refs = refs[:n], refs[n:2 * n], refs[2 * n:3 * n], refs[3 * n:4 * n]
        d_refs, nm_refs, nv_refs = refs[4 * n:5 * n], refs[5 * n:6 * n], refs[6 * n:7 * n]
        for k in range(n):
            dl, nm, nv = _adam_update(w_refs[k][...], g_refs[k][...], m_refs[k][...], v_refs[k][...])
            d_refs[k][...] = dl
            nm_refs[k][...] = nm
            nv_refs[k][...] = nv

    vm = pl.BlockSpec(memory_space=pltpu.VMEM)
    shapes = [_sds(w.shape, F32) for w in ws]
    out = pl.pallas_call(body, in_specs=[vm] * (4 * n), out_specs=[vm] * (3 * n), out_shape=shapes * 3, name=name,
                         compiler_params=_params())(*ws, *gs, *ms, *vs)
    return list(out[:n]), list(out[n:2 * n]), list(out[2 * n:])


VEC_ROWS = {"norm_mix": 0, "norm_mlp": 2, "norm_kv": 4, "norm_final": 5, "s5_d": 6, "b_q": 7, "b_o": 8, "s5_b_glu": 9,
            "b_kv": 11, "sinks": 12, "loss": 13}


def split_vectors(where, vecs, d_shard, glu_shard):
    kvw = 2 * N_KV * HEAD_DIM
    shapes = {"norm_mix": (2, D_MODEL), "norm_mlp": (2, D_MODEL), "norm_kv": (1, D_MODEL), "norm_final": (1, D_MODEL),
              "s5_d": (1, d_shard), "b_q": (1, D_MODEL), "b_o": (1, D_MODEL), "s5_b_glu": (1, glu_shard), "b_kv": (1, kvw),
              "sinks": (1, N_Q), "loss": (1, 128)}
    names = list(shapes)

    def body(where_ref, v_ref, *o_refs):
        chip = where_ref[1]
        for name, o_ref in zip(names, o_refs):
            r0, (r, n) = VEC_ROWS[name], shapes[name]
            if name == "s5_d":
                g = jnp.zeros((1, n), F32)
                for j in range(4):
                    g = jnp.where(chip == j, v_ref[r0:r0 + 1, j * n:(j + 1) * n], g)
            elif name == "s5_b_glu":
                g = jnp.zeros((1, n), F32)
                for j in range(4):
                    row, col = r0 + (j * n) // D_MODEL, (j * n) % D_MODEL
                    g = jnp.where(chip == j, v_ref[row:row + 1, col:col + n], g)
            else:
                g = v_ref[r0:r0 + r, 0:n]
            o_ref[...] = g

    vm = pl.BlockSpec(memory_space=pltpu.VMEM)
    out = pl.pallas_call(body, in_specs=[pl.BlockSpec(memory_space=pltpu.SMEM), vm], out_specs=[vm] * len(names),
                         out_shape=[_sds(shapes[n], F32) for n in names], name="split_vectors",
                         compiler_params=_params())(where, vecs)
    return dict(zip(names, out))


def _position():
    x, y, c = lax.axis_index("x"), lax.axis_index("y"), lax.axis_index("c")
    others = [(1 - x, y), (x, 1 - y), (1 - x, 1 - y)]
    return x, y, c, others


def _window(ref, kind, chip, half, shard_shape):
    r, n = shard_shape
    if kind == "col":
        return ref.at[pl.ds(pl.multiple_of(half * (r // 2), 16), r // 2), pl.ds(pl.multiple_of(chip * n, 128), n)]
    return ref.at[pl.ds(pl.multiple_of(chip * r, 16), r), pl.ds(pl.multiple_of(half * (n // 2), 128), n // 2)]


def _half(ref, kind, half, shape):
    r, n = shape
    if kind == "col":
        return ref.at[pl.ds(pl.multiple_of(half * (r // 2), 16), r // 2), :]
    return ref.at[:, pl.ds(pl.multiple_of(half * (n // 2), 128), n // 2)]


def swap_halves(name, grads, kinds):
    nt = len(grads)
    shapes = [tuple(g.shape) for g in grads]

    def body(*refs):
        in_refs, out_refs = refs[:nt], refs[nt:2 * nt]
        send_sems, recv_sems = refs[2 * nt:]
        x, y, c, _ = _position()
        cps = []
        for t in range(nt):
            cp = pltpu.make_async_remote_copy(
                src_ref=_half(in_refs[t], kinds[t], 1 - c, shapes[t]), dst_ref=_half(out_refs[t], kinds[t], 1 - c, shapes[t]),
                send_sem=send_sems.at[t], recv_sem=recv_sems.at[t], device_id=(x, y, 1 - c), device_id_type=MESH)
            cp.start()
            cps.append(cp)
        for t in range(nt):
            mine = _half(out_refs[t], kinds[t], c, shapes[t])
            pltpu.make_async_remote_copy(
                src_ref=mine, dst_ref=mine, send_sem=send_sems.at[t], recv_sem=recv_sems.at[t],
                device_id=(x, y, 1 - c), device_id_type=MESH).wait_recv()
        for cp in cps:
            cp.wait_send()

    hbm = pl.BlockSpec(memory_space=pl.ANY)
    return pl.pallas_call(
        body, in_specs=[hbm] * nt, out_specs=[hbm] * nt, out_shape=[_sds(s, BF16) for s in shapes],
        scratch_shapes=[pltpu.SemaphoreType.DMA((nt,)), pltpu.SemaphoreType.DMA((nt,))],
        name=name, compiler_params=_params())(*grads)


def _half_spec(kind, shape, tiles):
    r, n = shape
    if kind == "col":
        tn = n // tiles
        return pl.BlockSpec((r // 2, tn), lambda i, s: (s[0], i))
    tm = r // tiles
    return pl.BlockSpec((tm, n // 2), lambda i, s: (i, s[0]))


def add_halves(name, mine, landed, kinds, where, tiles=4):
    nt = len(mine)
    shapes = [tuple(a.shape) for a in mine]

    def compact(t):
        r, n = shapes[t]
        if kinds[t] == "col":
            return (r // 2, n), pl.BlockSpec((r // 2, n // tiles), lambda i, s: (0, i))
        return (r, n // 2), pl.BlockSpec((r // tiles, n // 2), lambda i, s: (i, 0))

    def body(s_ref, *refs):
        for a_ref, b_ref, o_ref in zip(refs[:nt], refs[nt:2 * nt], refs[2 * nt:]):
            o_ref[...] = (a_ref[...].astype(F32) + b_ref[...].astype(F32)).astype(BF16)

    specs = [_half_spec(kinds[t], shapes[t], tiles) for t in range(nt)]
    return pl.pallas_call(
        body, grid_spec=pltpu.PrefetchScalarGridSpec(num_scalar_prefetch=1, grid=(tiles,), in_specs=specs + specs,
                                                     out_specs=[compact(t)[1] for t in range(nt)]),
        out_shape=[_sds(compact(t)[0], BF16) for t in range(nt)], name=name,
        compiler_params=_params(("parallel",)))(where, *mine, *landed)


def sum_shards(name, parts, landed, kinds, shard_shapes, where, layers, n_layers, intos, tiles=2):
    nt = len(parts)
    in_specs, out_specs = [], []
    for t in range(nt):
        (r, n), layer = shard_shapes[t], layers[t]
        if kinds[t] == "col":
            tm, width = r // 2 // tiles, n
            own = pl.BlockSpec((tm, n), lambda i, s: (i, s[1]))
            out = pl.BlockSpec((None, tm, n), lambda i, s, layer=layer: (layer, s[0] * tiles + i, 0))
        else:
            tm, width = r // tiles, n // 2
            own = pl.BlockSpec((tm, n // 2), lambda i, s: (s[1] * tiles + i, 0))
            out = pl.BlockSpec((None, tm, n // 2), lambda i, s, layer=layer: (layer, i, s[0]))
        in_specs += [own, pl.BlockSpec((3, tm, width), lambda i, s: (0, i, 0))]
        out_specs.append(out)
    args, aliases = [where] + [a for pair in zip(parts, landed) for a in pair], {}
    for t in range(nt):
        if intos[t] is not None:
            aliases[len(args)] = t
            in_specs.append(pl.BlockSpec(memory_space=pl.ANY))
            args.append(intos[t])

    def body(s_ref, *refs):
        for t in range(nt):
            a_ref, l_ref, o_ref = refs[2 * t], refs[2 * t + 1], refs[len(in_specs) + t]
            o_ref[...] = ((a_ref[...].astype(F32) + l_ref[0].astype(F32)) + l_ref[1].astype(F32)) + l_ref[2].astype(F32)

    return pl.pallas_call(
        body, grid_spec=pltpu.PrefetchScalarGridSpec(num_scalar_prefetch=1, grid=(tiles,), in_specs=in_specs,
                                                     out_specs=out_specs),
        out_shape=[_sds((n_layers[t],) + tuple(shard_shapes[t]), F32) for t in range(nt)], input_output_aliases=aliases,
        name=name, compiler_params=_params(("parallel",)))(*args)


def share_halves(arrays, entries):
    na, nt = len(arrays), len(entries)

    def body(*refs):
        out_refs = refs[na:2 * na]
        send_sems, recv_sems = refs[2 * na:]
        x, y, c, _ = _position()
        cps = []
        for t, (a, layer, kind) in enumerate(entries):
            shape = tuple(arrays[a].shape[1:])
            mine = _half(out_refs[a].at[layer], kind, c, shape)
            cp = pltpu.make_async_remote_copy(
                src_ref=mine, dst_ref=mine, send_sem=send_sems.at[t], recv_sem=recv_sems.at[t],
                device_id=(x, y, 1 - c), device_id_type=MESH)
            cp.start()
            cps.append(cp)
        for t, (a, layer, kind) in enumerate(entries):
            shape = tuple(arrays[a].shape[1:])
            other = _half(out_refs[a].at[layer], kind, 1 - c, shape)
            pltpu.make_async_remote_copy(
                src_ref=other, dst_ref=other, send_sem=send_sems.at[t], recv_sem=recv_sems.at[t],
                device_id=(x, y, 1 - c), device_id_type=MESH).wait_recv()
        for cp in cps:
            cp.wait_send()

    hbm = pl.BlockSpec(memory_space=pl.ANY)
    return pl.pallas_call(
        body, in_specs=[hbm] * na, out_specs=[hbm] * na, out_shape=[_sds(a.shape, F32) for a in arrays],
        input_output_aliases={i: i for i in range(na)},
        scratch_shapes=[pltpu.SemaphoreType.DMA((nt,)), pltpu.SemaphoreType.DMA((nt,))],
        name="share_halves", compiler_params=_params())(*arrays)


HBM_SPEC = pl.BlockSpec(memory_space=pltpu.HBM)
SEM_SPEC = pl.BlockSpec(memory_space=pltpu.SEMAPHORE)
ANY_SPEC = pl.BlockSpec(memory_space=pl.ANY)


def _split_params():
    return pltpu.CompilerParams(has_side_effects=pltpu.SideEffectType.DATAFLOW_SIDE_EFFECTING,
                                vmem_limit_bytes=VMEM_LIMIT_BYTES)


def _in_hbm(a):
    return pltpu.with_memory_space_constraint(a, pltpu.HBM)


def cast_place(arrays, entries, where, tiles=2):
    in_specs, out_specs, fulls = [], [], []
    for a, layer, kind in entries:
        _, r, n = arrays[a].shape
        tm = r // tiles
        in_specs.append(pl.BlockSpec((None, tm, n), lambda i, s, layer=layer: (layer, i, 0)))
        if kind == "col":
            fulls.append((r, 4 * n))
            out_specs.append(pl.BlockSpec((tm, n), lambda i, s: (i, s[1])))
        else:
            fulls.append((4 * r, n))
            out_specs.append(pl.BlockSpec((tm, n), lambda i, s: (s[1] * tiles + i, 0)))
    nt = len(entries)

    def body(s_ref, *refs):
        for w_ref, o_ref in zip(refs[:nt], refs[nt:]):
            o_ref[...] = w_ref[...].astype(BF16)

    return pl.pallas_call(
        body, grid_spec=pltpu.PrefetchScalarGridSpec(num_scalar_prefetch=1, grid=(tiles,), in_specs=in_specs,
                                                     out_specs=out_specs),
        out_shape=[_sds(f, BF16) for f in fulls], name="cast_place",
        compiler_params=_params(("parallel",)))(where, *[arrays[a] for a, _, _ in entries])


def gather_start(name, fulls, kinds, shard_shapes, after):
    nt = len(fulls)
    na = 0 if after is None else 1

    def body(*refs):
        full_refs = refs[:nt]
        send_sems, recv_sems, token = refs[nt + na], refs[nt + na + 1], refs[-1]
        x, y, c, others = _position()
        for t in range(nt):
            mine = _window(full_refs[t], kinds[t], 2 * x + y, c, shard_shapes[t])
            for j, (ox, oy) in enumerate(others):
                pltpu.make_async_remote_copy(
                    src_ref=mine, dst_ref=mine, send_sem=send_sems.at[3 * t + j], recv_sem=recv_sems.at[3 * t + j],
                    device_id=(ox, oy, c), device_id_type=MESH).start()
        token[...] = jnp.zeros_like(token)

    sems = pltpu.SemaphoreType.DMA((3 * nt,))
    out = pl.pallas_call(
        body, name=name, in_specs=[HBM_SPEC] * nt + [ANY_SPEC] * na,
        out_specs=(SEM_SPEC, SEM_SPEC, *[HBM_SPEC] * nt, pl.BlockSpec(memory_space=pltpu.VMEM)),
        out_shape=(sems, sems, *[pltpu.HBM(f.shape, f.dtype) for f in fulls], _sds((8, 128), F32)),
        input_output_aliases={t: 2 + t for t in range(nt)}, compiler_params=_split_params(),
    )(*[_in_hbm(f) for f in fulls], *([] if after is None else [after]))
    return out[0], out[1], list(out[2:2 + nt]), out[-1]


def gather_wait(name, send_sems, recv_sems, fulls, kinds, shard_shapes, after):
    nt = len(fulls)

    def body(*refs):
        full_refs, send_ref, recv_ref = refs[:nt], refs[nt], refs[nt + 1]
        x, y, c, others = _position()
        for t in range(nt):
            mine = _window(full_refs[t], kinds[t], 2 * x + y, c, shard_shapes[t])
            for j, (ox, oy) in enumerate(others):
                cp = pltpu.make_async_remote_copy(
                    src_ref=mine, dst_ref=_window(full_refs[t], kinds[t], 2 * ox + oy, c, shard_shapes[t]),
                    send_sem=send_ref.at[3 * t + j], recv_sem=recv_ref.at[3 * t + j],
                    device_id=(ox, oy, c), device_id_type=MESH)
                cp.wait_send()
                cp.wait_recv()

    out = pl.pallas_call(
        body, name=name, in_specs=[HBM_SPEC] * nt + [SEM_SPEC, SEM_SPEC, HBM_SPEC], out_specs=[HBM_SPEC] * nt,
        out_shape=[pltpu.HBM(f.shape, f.dtype) for f in fulls], input_output_aliases={t: t for t in range(nt)},
        compiler_params=_split_params())(*fulls, send_sems, recv_sems, _in_hbm(after))
    return list(out)


def forward_halves(name, fulls, kinds, shard_shapes):
    nt = len(fulls)

    def body(*refs):
        out_refs = refs[nt:2 * nt]
        send_sems, recv_sems = refs[2 * nt:]
        x, y, c, others = _position()
        cps = []
        for t in range(nt):
            for j, (ox, oy) in enumerate(others):
                landed = _window(out_refs[t], kinds[t], 2 * ox + oy, c, shard_shapes[t])
                cp = pltpu.make_async_remote_copy(
                    src_ref=landed, dst_ref=landed, send_sem=send_sems.at[3 * t + j], recv_sem=recv_sems.at[3 * t + j],
                    device_id=(x, y, 1 - c), device_id_type=MESH)
                cp.start()
                cps.append(cp)
        for t in range(nt):
            for j, (ox, oy) in enumerate(others):
                got = _window(out_refs[t], kinds[t], 2 * ox + oy, 1 - c, shard_shapes[t])
                pltpu.make_async_remote_copy(
                    src_ref=got, dst_ref=got, send_sem=send_sems.at[3 * t + j], recv_sem=recv_sems.at[3 * t + j],
                    device_id=(x, y, 1 - c), device_id_type=MESH).wait_recv()
        for cp in cps:
            cp.wait_send()

    out = pl.pallas_call(
        body, in_specs=[ANY_SPEC] * nt, out_specs=[ANY_SPEC] * nt, out_shape=[_sds(f.shape, f.dtype) for f in fulls],
        input_output_aliases={t: t for t in range(nt)},
        scratch_shapes=[pltpu.SemaphoreType.DMA((3 * nt,)), pltpu.SemaphoreType.DMA((3 * nt,))],
        name=name, compiler_params=_params())(*fulls)
    return list(out)


def _piece(ref, kind, chip, shard_shape):
    r, n = shard_shape
    if kind == "col":
        return ref.at[:, pl.ds(pl.multiple_of(chip * n, 128), n)]
    return ref.at[pl.ds(pl.multiple_of(chip * r, 16), r), :]


def _piece_shape(kind, shard_shape):
    r, n = shard_shape
    return (r // 2, n) if kind == "col" else (r, n // 2)


def exchange_start(name, parts, kinds, shard_shapes):
    nt = len(parts)
    lands = [lax.empty((3,) + _piece_shape(kinds[t], shard_shapes[t]), BF16) for t in range(nt)]

    def body(*refs):
        part_refs, land_refs = refs[:nt], refs[nt:2 * nt]
        send_sems, recv_sems, token = refs[2 * nt], refs[2 * nt + 1], refs[-1]
        x, y, c, others = _position()
        for t in range(nt):
            for j, (ox, oy) in enumerate(others):
                pltpu.make_async_remote_copy(
                    src_ref=_piece(part_refs[t], kinds[t], 2 * ox + oy, shard_shapes[t]), dst_ref=land_refs[t].at[j],
                    send_sem=send_sems.at[3 * t + j], recv_sem=recv_sems.at[3 * t + j],
                    device_id=(ox, oy, c), device_id_type=MESH).start()
        token[...] = jnp.zeros_like(token)

    sems = pltpu.SemaphoreType.DMA((3 * nt,))
    both = list(parts) + lands
    out = pl.pallas_call(
        body, name=name, in_specs=[HBM_SPEC] * (2 * nt),
        out_specs=(SEM_SPEC, SEM_SPEC, *[HBM_SPEC] * (2 * nt), pl.BlockSpec(memory_space=pltpu.VMEM)),
        out_shape=(sems, sems, *[pltpu.HBM(a.shape, a.dtype) for a in both], _sds((8, 128), F32)),
        input_output_aliases={t: 2 + t for t in range(2 * nt)}, compiler_params=_split_params(),
    )(*[_in_hbm(a) for a in both])
    return out[0], out[1], list(out[2:2 + nt]), list(out[2 + nt:2 + 2 * nt]), out[-1]


def exchange_wait(name, send_sems, recv_sems, parts, lands, kinds, shard_shapes, after):
    nt = len(parts)

    def body(*refs):
        part_refs, land_refs = refs[:nt], refs[nt:2 * nt]
        send_ref, recv_ref = refs[2 * nt], refs[2 * nt + 1]
        x, y, c, others = _position()
        for t in range(nt):
            for j, (ox, oy) in enumerate(others):
                cp = pltpu.make_async_remote_copy(
                    src_ref=_piece(part_refs[t], kinds[t], 2 * ox + oy, shard_shapes[t]), dst_ref=land_refs[t].at[j],
                    send_sem=send_ref.at[3 * t + j], recv_sem=recv_ref.at[3 * t + j],
                    device_id=(ox, oy, c), device_id_type=MESH)
                cp.wait_send()
                cp.wait_recv()

    both = list(parts) + list(lands)
    out = pl.pallas_call(
        body, name=name, in_specs=[HBM_SPEC] * (2 * nt) + [SEM_SPEC, SEM_SPEC, HBM_SPEC], out_specs=[HBM_SPEC] * (2 * nt),
        out_shape=[pltpu.HBM(a.shape, a.dtype) for a in both], input_output_aliases={t: t for t in range(2 * nt)},
        compiler_params=_split_params())(*both, send_sems, recv_sems, _in_hbm(after))
    return list(out[:nt]), list(out[nt:])


def all_reduce_small(name, bufs):
    n = len(bufs)
    halves = [b.shape[0] // 2 for b in bufs]

    def body(*refs):
        in_refs, out_refs, lands = refs[:n], refs[n:2 * n], refs[2 * n:3 * n]
        send_sems, recv_sems = refs[3 * n:]
        x, y, c, _ = _position()
        mine = [pl.ds(pl.multiple_of(c * h, 8), h) for h in halves]
        other = [pl.ds(pl.multiple_of((1 - c) * h, 8), h) for h in halves]
        for k in range(n):
            out_refs[k][mine[k], :] = in_refs[k][mine[k], :]
        for s, peer in enumerate([(x, y, 1 - c), (1 - x, y, c), (x, 1 - y, c)]):
            cps = []
            for k in range(n):
                src = in_refs[k].at[other[k]] if s == 0 else out_refs[k].at[mine[k]]
                cp = pltpu.make_async_remote_copy(
                    src_ref=src, dst_ref=lands[k].at[s], send_sem=send_sems.at[4 * k + s], recv_sem=recv_sems.at[4 * k + s],
                    device_id=peer, device_id_type=MESH)
                cp.start()
                cps.append(cp)
            for k, cp in enumerate(cps):
                cp.wait()
                out_refs[k][mine[k], :] = out_refs[k][mine[k], :] + lands[k][s]
        cps = []
        for k in range(n):
            cp = pltpu.make_async_remote_copy(
                src_ref=out_refs[k].at[mine[k]], dst_ref=out_refs[k].at[mine[k]], send_sem=send_sems.at[4 * k + 3],
                recv_sem=recv_sems.at[4 * k + 3], device_id=(x, y, 1 - c), device_id_type=MESH)
            cp.start()
            cps.append(cp)
        for cp in cps:
            cp.wait()

    vm = pl.BlockSpec(memory_space=pltpu.VMEM)
    out = pl.pallas_call(
        body, in_specs=[vm] * n, out_specs=[vm] * n, out_shape=[_sds(b.shape, F32) for b in bufs],
        scratch_shapes=[pltpu.VMEM((3, h, b.shape[1]), F32) for h, b in zip(halves, bufs)]
        + [pltpu.SemaphoreType.DMA((4 * n,)), pltpu.SemaphoreType.DMA((4 * n,))],
        name=name, compiler_params=_params())(*bufs)
    return list(out)


def _local_step(x, target, small, need, emit):
    d = D_MODEL
    full = {}

    def after_token(vec, token):
        return vec if token is None else vec + token[0:1, 0:1]

    lam_r, lam_i, bbar_re, bbar_im = small["s5_disc"]
    rb, rc = _s5_matrices(bbar_re, bbar_im, small["s5_c_re"], small["s5_c_im"])
    rb16, rc16 = rb.astype(BF16), rc.astype(BF16)
    lr_t, li_t = lam_r.reshape(S5_BLOCKS, 8, 128), lam_i.reshape(S5_BLOCKS, 8, 128)
    (u,) = rms_fwd("norm_mix0", x, [small["norm_mix0"]], [F32])
    ge, y2, cs = s5_fwd(u, small["s5_d"], rb16, rc16, lr_t, li_t)
    full.update(need("glu", ge))

    def norm_rows(h, gains):
        xh, _ = _rms_hat(h)
        return [xh * g for g in gains]

    def glu_epilogue(accs, e, r):
        v, gt = accs[0] + r[0], accs[1] + r[1]
        h = e[0] + v * jax.nn.sigmoid(gt)
        return [h, v, gt] + norm_rows(h, r[2:])

    h1, val, gate, n1 = mm_nn(
        "glu", ge, full["w_glu"], [0, d], d, glu_epilogue, [F32, F32, F32, BF16], extras=[x],
        rowvecs=[(small["s5_b_glu"], 0), (small["s5_b_glu"], d), (small["norm_mlp0"], 0)], tm=512, tn=d)

    def mlp_fwd(tag, h, n, w_in, w_out, next_gains):
        def in_epilogue(accs, e, rv):
            pos = jnp.maximum(accs[0], 0.0)
            return [pos * pos, 2.0 * pos]

        r, slope = mm_nn("mlp_in" + tag, n, w_in, [0], w_in.shape[1], in_epilogue, [BF16, BF16], tm=2048)

        def epilogue(accs, e, rv):
            h_out = e[0] + accs[0]
            return [h_out] + norm_rows(h_out, rv)

        outs = mm_nn("mlp_out" + tag, r, w_out, [0], d, epilogue, [F32] + [BF16] * len(next_gains), extras=[h],
                     rowvecs=[(g, 0) for g in next_gains], tm=512, tn=d)
        return outs[0], outs[1:], (n, r, slope)

    full.update(need("mlp0", h1))
    h2, (nkv, n2), mlp0 = mlp_fwd("0", h1, n1, full["w_in0"], full["w_out0"], [small["norm_kv"], small["norm_mix1"]])

    full.update(need("rest", h2))
    kvw = 2 * N_KV * HEAD_DIM
    (kv,) = mm_nn("kv_proj", nkv, full["w_kv"], [0], kvw, lambda accs, e, r: [accs[0] + r[0]], [BF16],
                  rowvecs=[(small["b_kv"], 0)])
    (q,) = mm_nn("q_proj", n2, full["w_q"], [0], d, lambda accs, e, r: [accs[0] + r[0]], [BF16],
                 rowvecs=[(small["b_q"], 0)])
    sinks = small["sinks"].reshape(N_Q)
    o = attn_fwd(q, kv, sinks)
    def o_epilogue(accs, e, r):
        h_out = e[0] + accs[0] + r[0]
        return [h_out] + norm_rows(h_out, r[1:])

    h3, n3 = mm_nn("o_proj", o, full["w_o"], [0], d, o_epilogue, [F32, BF16], extras=[h2],
                   rowvecs=[(small["b_o"], 0), (small["norm_mlp1"], 0)], tm=512, tn=d)
    h4, _, mlp1 = mlp_fwd("1", h3, n3, full["w_in1"], full["w_out1"], [])
    loss_tile, dh, dhb, dg_final = final_loss(h4, target, small["norm_final"])

    grads_small, grads_full = {"norm_final": dg_final}, {}
    ident = lambda acc, e, r: [acc]
    layer1 = ["w_out1", "w_in1", "w_o", "w_q", "w_kv"]
    layer0 = ["w_out0", "w_in0", "w_glu"]

    def norm_bwd_rows(x_rows, res, dys, gains):
        xh, r = _rms_hat(x_rows)
        dxh = sum(dy * g for dy, g in zip(dys, gains))
        dx = r * (dxh - xh * jnp.mean(dxh * xh, axis=-1, keepdims=True)) + res
        return dx, [jnp.sum(dy * xh, axis=0, keepdims=True) for dy in dys]

    def mlp_bwd(tag, dh, dhb, h_in, gain, w_in, w_out, saved):
        n, r, slope = saved
        grads_full["w_out" + tag] = mm_tn("dw_out" + tag, r, dhb, tn=1024)
        (da,) = mm_nt("mlp_da" + tag, dhb, w_out, lambda acc, e, rv: [acc * e[0].astype(F32)], [BF16], extras=[slope],
                      tm=2048)
        grads_full["w_in" + tag] = mm_tn("dw_in" + tag, n, da, tn=1024)

        def epilogue(acc, e, rv):
            dx, dgs = norm_bwd_rows(e[0], e[1], [acc], rv)
            return [dx, dx, jnp.sum(dx, axis=0, keepdims=True)] + dgs

        dx, dxb, colsum, dg = mm_nt("mlp_dn" + tag, da, w_in, epilogue, [F32, BF16], extras=[h_in, dh], rowvecs=[gain],
                                    n_sums=2, tm=512, tk=d)
        grads_small["norm_mlp" + tag] = dg
        return dx, dxb, colsum

    dh3, dh3b, colsum3 = mlp_bwd("1", dh, dhb, h3, small["norm_mlp1"], full["w_in1"], full["w_out1"], mlp1)
    grads_small["b_o"] = colsum3
    grads_full["w_o"] = mm_tn("dw_o", o, dh3b)
    (do,) = mm_nt("attn_do", dh3b, full["w_o"], ident, [BF16])
    dq, dbq, dprev, dcur, dsink = attn_bwd(q, kv, do, sinks)
    dkv, dbkv = kv_combine(dprev, dcur)
    grads_small["b_q"], grads_small["b_kv"], grads_small["sinks"] = dbq, dbkv, dsink
    grads_full["w_q"] = mm_tn("dw_q", n2, dq)
    grads_full["w_kv"] = mm_tn("dw_kv", nkv, dkv)
    (dnkv,) = mm_nt("kv_dn", dkv, full["w_kv"], ident, [F32])
    token = emit("layer1", {n: grads_full[n] for n in layer1})

    def attn_dn_epilogue(acc, e, rv):
        dx, dgs = norm_bwd_rows(e[0], e[1], [acc, e[2]], rv)
        return [dx, dx] + dgs

    dh2, dh2b, dg_mix1, dg_kv = mm_nt("attn_dn", dq, full["w_q"], attn_dn_epilogue, [F32, BF16], extras=[h2, dh3, dnkv],
                                      rowvecs=[after_token(small["norm_mix1"], token), small["norm_kv"]], n_sums=2,
                                      tm=512, tk=d)
    grads_small["norm_mix1"], grads_small["norm_kv"] = dg_mix1, dg_kv
    dh1, _, _ = mlp_bwd("0", dh2, dh2b, h1, small["norm_mlp0"], full["w_in0"], full["w_out0"], mlp0)

    dz, db_glu = glu_bwd(dh1, val, gate)
    grads_small["s5_b_glu"] = db_glu
    grads_full["w_glu"] = mm_tn("dw_glu", ge, dz, tn=1024)
    token = emit("layer0", {n: grads_full[n] for n in layer0})
    (dy2,) = mm_nt("glu_dy", dz, full["w_glu"], lambda acc, e, rv: [acc * _gelu_grad(e[0])], [F32], extras=[y2])
    rbt16, rct16 = jnp.swapaxes(rb16, 1, 2), jnp.swapaxes(rc16, 1, 2)
    du, dd, drb, drc, dlr, dli = s5_bwd(u, dy2, after_token(small["s5_d"], token), cs, rb16, rbt16, rct16, lr_t, li_t)
    grads_small["s5_d"] = dd
    grads_small["s5_mats"] = (drb, drc, dlr, dli)
    grad_x, _, _, dg_mix0 = rms_bwd("norm_mix0_bwd", x, [du], [small["norm_mix0"]], dh1)
    grads_small["norm_mix0"] = dg_mix0
    return loss_tile, grad_x, grads_small


SMALL_NAMES = ["norm_mix", "norm_mlp", "norm_kv", "norm_final", "s5_a_re", "s5_a_im", "s5_log_dt", "s5_b_re", "s5_b_im",
               "s5_c_re", "s5_c_im", "s5_d", "s5_b_glu", "b_kv", "b_q", "sinks", "b_o"]
BIG_NAMES = ["s5_w_glu", "w_kv", "w_q", "w_o", "w_mlp_in", "w_mlp_out"]
WEIGHT_ORDER = ["norm_mix", "norm_mlp", "norm_kv", "norm_final", "s5_a_re", "s5_a_im", "s5_log_dt", "s5_b_re", "s5_b_im",
                "s5_c_re", "s5_c_im", "s5_d", "s5_w_glu", "s5_b_glu", "w_kv", "b_kv", "w_q", "b_q", "sinks", "w_o", "b_o",
                "w_mlp_in", "w_mlp_out"]


def kernel(x, norm_mix, norm_mlp, norm_kv, norm_final, s5_a_re, s5_a_im, s5_log_dt, s5_b_re, s5_b_im, s5_c_re, s5_c_im, s5_d, s5_w_glu, s5_b_glu, w_kv, b_kv, w_q, b_q, sinks, w_o, b_o, w_mlp_in, w_mlp_out, loss_target, m_norm_mix, m_norm_mlp, m_norm_kv, m_norm_final, m_s5_a_re, m_s5_a_im, m_s5_log_dt, m_s5_b_re, m_s5_b_im, m_s5_c_re, m_s5_c_im, m_s5_d, m_s5_w_glu, m_s5_b_glu, m_w_kv, m_b_kv, m_w_q, m_b_q, m_sinks, m_w_o, m_b_o, m_w_mlp_in, m_w_mlp_out, v_norm_mix, v_norm_mlp, v_norm_kv, v_norm_final, v_s5_a_re, v_s5_a_im, v_s5_log_dt, v_s5_b_re, v_s5_b_im, v_s5_c_re, v_s5_c_im, v_s5_d, v_s5_w_glu, v_s5_b_glu, v_w_kv, v_b_kv, v_w_q, v_b_q, v_sinks, v_w_o, v_b_o, v_w_mlp_in, v_w_mlp_out):
    env = dict(locals())
    w = {n: env[n] for n in WEIGHT_ORDER}
    mom = {n: env["m_" + n] for n in WEIGHT_ORDER}
    var = {n: env["v_" + n] for n in WEIGHT_ORDER}
    d = D_MODEL
    xi, yi, ci = lax.axis_index("x"), lax.axis_index("y"), lax.axis_index("c")
    chip = 2 * xi + yi
    where = jnp.stack([ci, chip]).astype(jnp.int32)

    dsh, bsh = s5_d.shape[1], s5_b_glu.shape[1]
    placed = jnp.concatenate([
        lax.dynamic_update_slice(jnp.zeros((4 * dsh,), F32), s5_d[0], (chip * dsh,)),
        lax.dynamic_update_slice(jnp.zeros((4 * bsh,), F32), s5_b_glu[0], (chip * bsh,))])
    placed = jnp.pad(placed, (0, (-placed.shape[0]) % 2048))
    placed = jnp.where(ci == 0, placed, 0.0).reshape(-1, 128)
    (gathered_rows,) = all_reduce_small("gather_vectors", [placed])
    gathered = gathered_rows.reshape(-1)
    d_full, bglu_full = gathered[:4 * dsh].reshape(1, -1), gathered[4 * dsh:].reshape(1, -1)

    big = [s5_w_glu, w_kv[None], w_q, w_o, w_mlp_in, w_mlp_out]
    entries = [(0, 0, "col"), (1, 0, "row"), (2, 0, "row"), (3, 0, "row"), (4, 0, "col"), (4, 1, "col"),
               (5, 0, "row"), (5, 1, "row")]
    names = ["w_glu", "w_kv", "w_q", "w_o", "w_in0", "w_in1", "w_out0", "w_out1"]
    kinds = dict(zip(names, [k for _, _, k in entries]))
    shard_shapes = dict(zip(names, [tuple(big[a].shape[1:]) for a, _, _ in entries]))

    placed_w = dict(zip(names, cast_place(big, entries, where)))
    gather_groups = {"glu": ["w_glu"], "mlp0": ["w_in0", "w_out0"], "rest": ["w_kv", "w_q", "w_o", "w_in1", "w_out1"]}
    started, token = {}, gathered_rows
    for group, members in gather_groups.items():
        send, recv, thru, token = gather_start(
            "gather_start_" + group, [placed_w[n] for n in members], [kinds[n] for n in members],
            [shard_shapes[n] for n in members], token)
        started[group] = (send, recv, thru)

    def need(group, after):
        members = gather_groups[group]
        ks, shapes = [kinds[n] for n in members], [shard_shapes[n] for n in members]
        send, recv, thru = started[group]
        landed = gather_wait("gather_wait_" + group, send, recv, thru, ks, shapes, after)
        return dict(zip(members, forward_halves("forward_halves_" + group, landed, ks, shapes)))

    exchanging = {}

    def emit(group, partial):
        members = list(partial)
        ks, shapes = [kinds[n] for n in members], [shard_shapes[n] for n in members]
        landed = swap_halves("swap_halves_" + group, [partial[n] for n in members], ks)
        sums = add_halves("add_halves_" + group, [partial[n] for n in members], landed, ks, where)
        send, recv, parts, lands, tok = exchange_start("exchange_start_" + group, sums, ks, shapes)
        exchanging[group] = (members, send, recv, parts, lands)
        return tok

    disc = lambda *p: _s5_discretise(p[0], p[1], p[2], p[3], p[4])
    disc_args = (s5_a_re[0], s5_a_im[0], s5_log_dt[0], s5_b_re[0], s5_b_im[0])
    disc_out, disc_vjp = jax.vjp(disc, *disc_args)
    small = {
        "norm_mix0": norm_mix[0:1] + token[0:1, 0:1], "norm_mix1": norm_mix[1:2], "norm_mlp0": norm_mlp[0:1], "norm_mlp1": norm_mlp[1:2],
        "norm_kv": norm_kv.reshape(1, d), "norm_final": norm_final.reshape(1, d), "s5_disc": disc_out,
        "s5_c_re": s5_c_re[0], "s5_c_im": s5_c_im[0], "s5_d": d_full, "s5_b_glu": bglu_full,
        "b_kv": b_kv.reshape(1, -1), "b_q": b_q, "sinks": sinks, "b_o": b_o,
    }
    loss_row, grad_x, gs = _local_step(x[0], loss_target[0], small, need, emit)

    mats = s5_compact(*gs["s5_mats"])
    rows = [gs["norm_mix0"], gs["norm_mix1"], gs["norm_mlp0"], gs["norm_mlp1"], gs["norm_kv"], gs["norm_final"], gs["s5_d"],
            gs["b_q"], gs["b_o"], gs["s5_b_glu"], gs["b_kv"], gs["sinks"], loss_row, jnp.zeros((2, d), F32)]
    vecs, mats = all_reduce_small("reduce_small", [jnp.concatenate(rows, axis=0), mats])
    grads = split_vectors(where, vecs, dsh, bsh)
    loss = grads.pop("loss")[0, 0]
    dbbar_re, dbbar_im, dc_re, dc_im, dlr, dli = _s5_unpack(mats)
    g_are, g_aim, g_dt, g_bre, g_bim = disc_vjp((dlr, dli, dbbar_re, dbbar_im))
    grads.update({"s5_a_re": g_are[None], "s5_a_im": g_aim[None], "s5_log_dt": g_dt[None], "s5_b_re": g_bre[None],
                  "s5_b_im": g_bim[None], "s5_c_re": dc_re[None], "s5_c_im": dc_im[None]})

    reduced = [None] * len(big)
    where_of = dict(zip(names, entries))
    for group, after in (("layer1", grad_x), ("layer0", mats)):
        members, send, recv, parts, lands = exchanging[group]
        ks, shapes = [kinds[n] for n in members], [shard_shapes[n] for n in members]
        parts, lands = exchange_wait("exchange_wait_" + group, send, recv, parts, lands, ks, shapes, after)
        targets = [where_of[n][0] for n in members]
        sums = sum_shards("sum_shards_" + group, parts, lands, ks, shapes, where, [where_of[n][1] for n in members],
                          [big[a].shape[0] for a in targets], [reduced[a] for a in targets])
        for a, arr in zip(targets, sums):
            reduced[a] = arr
    reduced = share_halves(reduced, entries)
    for n, g in zip(BIG_NAMES, reduced):
        grads[n] = g.reshape(w[n].shape)

    delta, new_m, new_v = {}, {}, {}
    for n in BIG_NAMES:
        flat = lambda a: a.reshape(-1, a.shape[-1])
        dl, nm, nv = adamw("adamw_" + n, flat(w[n]), flat(grads[n]), flat(mom[n]), flat(var[n]))
        delta[n], new_m[n], new_v[n] = dl.reshape(w[n].shape), nm.reshape(w[n].shape), nv.reshape(w[n].shape)
    as_rows = lambda a: a.reshape(1, -1) if a.ndim == 1 else a
    sw, sg, sm, sv = ([as_rows(t[n]) for n in SMALL_NAMES] for t in (w, grads, mom, var))
    for n, a, b, c_ in zip(SMALL_NAMES, *adamw_native("adamw_small", sw, sg, sm, sv)):
        delta[n], new_m[n], new_v[n] = a, b, c_

    out = [loss.reshape(()), grad_x[None]]
    for table in (grads, delta, new_m, new_v):
        out += [table[n].reshape(w[n].shape) for n in WEIGHT_ORDER]
    return tuple(out)
```

```python
import functools
import math

import jax
import jax.numpy as jnp
from jax import lax
from jax.experimental import pallas as pl
from jax.experimental.pallas import tpu as pltpu

F32 = jnp.float32
BF16 = jnp.bfloat16

D_MODEL = 1024
S5_GROUPS = 64
S5_GROUP = 16
S5_STATE = 64
N_KV = 4
N_Q = 16
HEAD_DIM = 64
BLOCK = 128
NORM_EPS = 1e-5
LAMBDA_RE_MAX = -1e-4
ADAM_LR, ADAM_B1, ADAM_B2, ADAM_EPS, ADAM_WD, ADAM_STEP = 0.001, 0.9, 0.999, 1e-08, 0.01, 10

VMEM_LIMIT_BYTES = 56 * 1024 * 1024
S5_CHUNK = 256
S5_BLOCKS = 4
MESH = pl.DeviceIdType.MESH


def _params(sem=None):
    return pltpu.CompilerParams(dimension_semantics=sem, vmem_limit_bytes=VMEM_LIMIT_BYTES)


def _sds(shape, dtype):
    return jax.ShapeDtypeStruct(shape, dtype)


def _rms_hat(xv):
    r = lax.rsqrt(jnp.mean(xv * xv, axis=-1, keepdims=True) + NORM_EPS)
    return xv * r, r


def rms_fwd(name, x, gains, out_dtypes, tm=256):
    n_rows, d = x.shape
    ng = len(gains)

    def body(x_ref, *refs):
        xh, _ = _rms_hat(x_ref[...])
        for g_ref, o_ref in zip(refs[:ng], refs[ng:]):
            o_ref[...] = (xh * g_ref[...]).astype(o_ref.dtype)

    row = pl.BlockSpec((tm, d), lambda i: (i, 0))
    vec = pl.BlockSpec((1, d), lambda i: (0, 0))
    return pl.pallas_call(
        body, grid=(n_rows // tm,), in_specs=[row] + [vec] * ng, out_specs=[row] * ng,
        out_shape=[_sds((n_rows, d), dt) for dt in out_dtypes], name=name,
        compiler_params=_params(("parallel",)))(x, *gains)


def rms_bwd(name, x, dys, gains, res, tm=256):
    n_rows, d = x.shape
    ng = len(gains)

    def body(x_ref, res_ref, *refs):
        dy_refs, g_refs = refs[:ng], refs[ng:2 * ng]
        dx_ref, dxb_ref, cs_ref = refs[2 * ng:2 * ng + 3]
        dg_refs = refs[2 * ng + 3:]
        i = pl.program_id(0)
        xh, r = _rms_hat(x_ref[...])
        dxh = jnp.zeros_like(xh)
        dgs = []
        for dy_ref, g_ref in zip(dy_refs, g_refs):
            dy = dy_ref[...].astype(F32)
            dxh = dxh + dy * g_ref[...]
            dgs.append(jnp.sum(dy * xh, axis=0, keepdims=True))
        dx = r * (dxh - xh * jnp.mean(dxh * xh, axis=-1, keepdims=True)) + res_ref[...]
        dx_ref[...] = dx
        dxb_ref[...] = dx.astype(BF16)
        cs = jnp.sum(dx, axis=0, keepdims=True)

        @pl.when(i == 0)
        def _():
            cs_ref[...] = jnp.zeros_like(cs_ref)
            for dg_ref in dg_refs:
                dg_ref[...] = jnp.zeros_like(dg_ref)

        cs_ref[...] += cs
        for dg_ref, dg in zip(dg_refs, dgs):
            dg_ref[...] += dg

    row = pl.BlockSpec((tm, d), lambda i: (i, 0))
    vec = pl.BlockSpec((1, d), lambda i: (0, 0))
    return pl.pallas_call(
        body, grid=(n_rows // tm,), in_specs=[row, row] + [row] * ng + [vec] * ng,
        out_specs=[row, row, vec] + [vec] * ng,
        out_shape=[_sds((n_rows, d), F32), _sds((n_rows, d), BF16), _sds((1, d), F32)] + [_sds((1, d), F32)] * ng,
        name=name, compiler_params=_params(("arbitrary",)))(x, res, *dys, *gains)


def mm_nn(name, a, w, col_offsets, n_out, epilogue, out_dtypes, extras=(), rowvecs=(), tm=1024, tn=512):
    m, k = a.shape
    tm, tn = min(tm, m), min(tn, n_out)
    nw, ne, nr = len(col_offsets), len(extras), len(rowvecs)

    def body(a_ref, *refs):
        w_refs, e_refs, r_refs = refs[:nw], refs[nw:nw + ne], refs[nw + ne:nw + ne + nr]
        o_refs = refs[nw + ne + nr:]
        av = a_ref[...]
        accs = [jnp.dot(av, w_ref[...], preferred_element_type=F32) for w_ref in w_refs]
        outs = epilogue(accs, [e[...] for e in e_refs], [r[...] for r in r_refs])
        for o_ref, o in zip(o_refs, outs):
            o_ref[...] = o.astype(o_ref.dtype)

    def wspec(off):
        return pl.BlockSpec((k, tn), lambda j, i, off=off: (0, off // tn + j))

    def rspec(off):
        return pl.BlockSpec((1, tn), lambda j, i, off=off: (0, off // tn + j))

    tile = pl.BlockSpec((tm, tn), lambda j, i: (i, j))
    in_specs = ([pl.BlockSpec((tm, k), lambda j, i: (i, 0))] + [wspec(o) for o in col_offsets]
                + [tile] * ne + [rspec(o) for _, o in rowvecs])
    return pl.pallas_call(
        body, grid=(n_out // tn, m // tm), in_specs=in_specs, out_specs=[tile] * len(out_dtypes),
        out_shape=[_sds((m, n_out), dt) for dt in out_dtypes], name=name,
        compiler_params=_params(("parallel", "parallel")))(a, *([w] * nw), *extras, *[r for r, _ in rowvecs])


def mm_nt(name, g, w, epilogue, out_dtypes, extras=(), rowvecs=(), n_sums=0, tm=512, tk=512):
    m, n = g.shape
    k = w.shape[0]
    tm, tk = min(tm, m), min(tk, k)
    ne, nr, no = len(extras), len(rowvecs), len(out_dtypes)

    def body(g_ref, w_ref, *refs):
        e_refs, r_refs, o_refs, s_refs = refs[:ne], refs[ne:ne + nr], refs[ne + nr:ne + nr + no], refs[ne + nr + no:]
        acc = lax.dot_general(g_ref[...], w_ref[...], (((1,), (1,)), ((), ())), preferred_element_type=F32)
        outs = epilogue(acc, [e[...] for e in e_refs], [r[...] for r in r_refs])
        for o_ref, o in zip(o_refs, outs[:no]):
            o_ref[...] = o.astype(o_ref.dtype)
        if n_sums:
            @pl.when(pl.program_id(0) == 0)
            def _():
                for s_ref in s_refs:
                    s_ref[...] = jnp.zeros_like(s_ref)

            for s_ref, val in zip(s_refs, outs[no:]):
                s_ref[...] += val

    tile = pl.BlockSpec((tm, tk), lambda i, j: (i, j))
    vec = pl.BlockSpec((1, tk), lambda i, j: (0, j))
    sem = ("arbitrary", "parallel") if n_sums else ("parallel", "parallel")
    return pl.pallas_call(
        body, grid=(m // tm, k // tk),
        in_specs=[pl.BlockSpec((tm, n), lambda i, j: (i, 0)), pl.BlockSpec((tk, n), lambda i, j: (j, 0))]
        + [tile] * ne + [vec] * nr,
        out_specs=[tile] * no + [vec] * n_sums,
        out_shape=[_sds((m, k), dt) for dt in out_dtypes] + [_sds((1, k), F32)] * n_sums, name=name,
        compiler_params=_params(sem))(g, w, *extras, *rowvecs)


def mm_tn(name, a, g, tk=512, tn=512):
    m, k = a.shape
    n = g.shape[1]
    tk, tn = min(tk, k), min(tn, n)

    def body(a_ref, g_ref, o_ref):
        acc = lax.dot_general(a_ref[...], g_ref[...], (((0,), (0,)), ((), ())), preferred_element_type=F32)
        o_ref[...] = acc.astype(o_ref.dtype)

    return pl.pallas_call(
        body, grid=(k // tk, n // tn),
        in_specs=[pl.BlockSpec((m, tk), lambda i, j: (0, i)), pl.BlockSpec((m, tn), lambda i, j: (0, j))],
        out_specs=pl.BlockSpec((tk, tn), lambda i, j: (i, j)), out_shape=_sds((k, n), BF16), name=name,
        compiler_params=_params(("parallel", "parallel")))(a, g)


def _row_mask(tc):
    row = lax.broadcasted_iota(jnp.int32, (8 * tc, 256), 0) % 8
    col = lax.broadcasted_iota(jnp.int32, (8 * tc, 256), 1) // 32
    return row == col


def _expand_rows(expand_ref, val, mask):
    rep = jnp.dot(expand_ref[...], val.astype(BF16), preferred_element_type=F32)
    return jnp.where(mask, rep, 0.0).astype(BF16)


def _staged(ref):
    return jnp.concatenate([ref[0], ref[1]], axis=1)


def _stage(ref, val):
    ref[0] = val[:, 0:128]
    ref[1] = val[:, 128:256]


def _gather_rows(src_ref, tc):
    halves = []
    for half in range(2):
        col = lax.broadcasted_iota(jnp.int32, (tc, 128), 1) // 32 + 4 * half
        out = jnp.zeros((tc, 128), F32)
        for s8 in range(4 * half, 4 * half + 4):
            out = jnp.where(col == s8, src_ref.at[half][pl.ds(s8, tc, stride=8), :], out)
        halves.append(out)
    return jnp.concatenate(halves, axis=1)


def _gelu(x):
    c = math.sqrt(2.0 / math.pi)
    return 0.5 * x * (1.0 + jnp.tanh(c * (x + 0.044715 * x * x * x)))


def _gelu_grad(x):
    c = math.sqrt(2.0 / math.pi)
    t = jnp.tanh(c * (x + 0.044715 * x * x * x))
    return 0.5 * (1.0 + t) + 0.5 * x * (1.0 - t * t) * c * (1.0 + 3.0 * 0.044715 * x * x)


def _expansion(tc):
    return (jnp.arange(8 * tc)[:, None] // 8 == jnp.arange(tc)[None, :]).astype(BF16)


def s5_fwd(u, d_skip, rb, rc, lam_r, lam_i):
    n_rows = u.shape[0]
    tc = min(S5_CHUNK, n_rows)
    nc = n_rows // tc

    def body(u_ref, d_ref, ex_ref, rb_ref, rc_ref, lr_ref, li_ref, ge_ref, y2_ref, cs_ref, bux, yrows, carry):
        i = pl.program_id(0)

        @pl.when(i == 0)
        def _():
            carry[...] = jnp.zeros_like(carry)

        cs_ref[0] = carry[...]
        mask = _row_mask(tc)
        for blk in range(S5_BLOCKS):
            lhs = _expand_rows(ex_ref, u_ref[:, blk * 256:(blk + 1) * 256], mask)
            bux[blk] = jnp.dot(lhs, rb_ref[blk], preferred_element_type=F32)
        lam = [(lr_ref[blk], li_ref[blk]) for blk in range(S5_BLOCKS)]

        def step(t, c):
            r0 = pl.multiple_of(t * 8, 8)
            new = []
            for blk in range(S5_BLOCKS):
                xr, xi = c[2 * blk], c[2 * blk + 1]
                lr, li = lam[blk]
                nr = lr * xr - li * xi + bux[blk, pl.ds(r0, 8), 0:128]
                ni = lr * xi + li * xr + bux[blk, pl.ds(r0, 8), 128:256]
                bux[blk, pl.ds(r0, 8), 0:128] = nr
                bux[blk, pl.ds(r0, 8), 128:256] = ni
                new += [nr, ni]
            return tuple(new)

        c0 = []
        for blk in range(S5_BLOCKS):
            c0 += [carry[blk, :, 0:128], carry[blk, :, 128:256]]
        cn = lax.fori_loop(0, tc, step, tuple(c0), unroll=4)
        for blk in range(S5_BLOCKS):
            carry[blk, :, 0:128] = cn[2 * blk]
            carry[blk, :, 128:256] = cn[2 * blk + 1]
        for blk in range(S5_BLOCKS):
            _stage(yrows, jnp.dot(bux[blk].astype(BF16), rc_ref[blk], preferred_element_type=F32))
            sl = slice(blk * 256, (blk + 1) * 256)
            y2 = _gather_rows(yrows, tc) + d_ref[:, sl] * u_ref[:, sl]
            y2_ref[:, sl] = y2
            ge_ref[:, sl] = _gelu(y2).astype(BF16)

    row = pl.BlockSpec((tc, D_MODEL), lambda i: (i, 0))
    mat = pl.BlockSpec((S5_BLOCKS, 256, 256), lambda i: (0, 0, 0))
    lamspec = pl.BlockSpec((S5_BLOCKS, 8, 128), lambda i: (0, 0, 0))
    return pl.pallas_call(
        body, grid=(nc,),
        in_specs=[row, pl.BlockSpec((1, D_MODEL), lambda i: (0, 0)), pl.BlockSpec((8 * tc, tc), lambda i: (0, 0)),
                  mat, mat, lamspec, lamspec],
        out_specs=[row, row, pl.BlockSpec((1, S5_BLOCKS, 8, 256), lambda i: (i, 0, 0, 0))],
        out_shape=[_sds((n_rows, D_MODEL), BF16), _sds((n_rows, D_MODEL), F32), _sds((nc, S5_BLOCKS, 8, 256), F32)],
        scratch_shapes=[pltpu.VMEM((S5_BLOCKS, 8 * tc, 256), F32), pltpu.VMEM((2, 8 * tc, 128), F32),
                        pltpu.VMEM((S5_BLOCKS, 8, 256), F32)],
        name="s5_fwd", compiler_params=_params(("arbitrary",)))(u, d_skip, _expansion(tc), rb, rc, lam_r, lam_i)


def s5_bwd(u, dy2, d_skip, cs, rb, rbt, rct, lam_r, lam_i):
    n_rows = u.shape[0]
    tc = min(S5_CHUNK, n_rows)
    nc = n_rows // tc

    def body(u_ref, dy_ref, d_ref, cs_ref, ex_ref, rb_ref, rbt_ref, rct_ref, lr_ref, li_ref,
             du_ref, dd_ref, drb_ref, drc_ref, dlr_ref, dli_ref, tmp, lhsu, lhsd, xs, adj, acarry):
        i = pl.program_id(0)

        @pl.when(i == 0)
        def _():
            acarry[...] = jnp.zeros_like(acarry)
            dd_ref[...] = jnp.zeros_like(dd_ref)
            drb_ref[...] = jnp.zeros_like(drb_ref)
            drc_ref[...] = jnp.zeros_like(drc_ref)
            dlr_ref[...] = jnp.zeros_like(dlr_ref)
            dli_ref[...] = jnp.zeros_like(dli_ref)

        dd_ref[...] += jnp.sum(dy_ref[...] * u_ref[...], axis=0, keepdims=True)
        mask = _row_mask(tc)
        for blk in range(S5_BLOCKS):
            sl = slice(blk * 256, (blk + 1) * 256)
            lhsu[blk] = _expand_rows(ex_ref, u_ref[:, sl], mask)
            xs[blk] = jnp.dot(lhsu[blk], rb_ref[blk], preferred_element_type=F32)
            lhsd[blk] = _expand_rows(ex_ref, dy_ref[:, sl], mask)
            adj[blk] = jnp.dot(lhsd[blk], rct_ref[blk], preferred_element_type=F32)
        lam = [(lr_ref[blk], li_ref[blk]) for blk in range(S5_BLOCKS)]

        def fstep(t, c):
            r0 = pl.multiple_of(t * 8, 8)
            new = []
            for blk in range(S5_BLOCKS):
                xr, xi = c[2 * blk], c[2 * blk + 1]
                lr, li = lam[blk]
                nr = lr * xr - li * xi + xs[blk, pl.ds(r0, 8), 0:128]
                ni = lr * xi + li * xr + xs[blk, pl.ds(r0, 8), 128:256]
                xs[blk, pl.ds(r0, 8), 0:128] = nr
                xs[blk, pl.ds(r0, 8), 128:256] = ni
                new += [nr, ni]
            return tuple(new)

        c0 = []
        for blk in range(S5_BLOCKS):
            c0 += [cs_ref[0, blk, :, 0:128], cs_ref[0, blk, :, 128:256]]
        lax.fori_loop(0, tc, fstep, tuple(c0), unroll=4)

        def bstep(k, c):
            t = tc - 1 - k
            r0 = pl.multiple_of(t * 8, 8)
            rp = pl.multiple_of(jnp.maximum(t - 1, 0) * 8, 8)
            first = t == 0
            new_a, new_g = [], []
            for blk in range(S5_BLOCKS):
                ar, ai = c[0][2 * blk], c[0][2 * blk + 1]
                glr, gli = c[1][2 * blk], c[1][2 * blk + 1]
                lr, li = lam[blk]
                nr = lr * ar + li * ai + adj[blk, pl.ds(r0, 8), 0:128]
                ni = lr * ai - li * ar + adj[blk, pl.ds(r0, 8), 128:256]
                adj[blk, pl.ds(r0, 8), 0:128] = nr
                adj[blk, pl.ds(r0, 8), 128:256] = ni
                pr = jnp.where(first, cs_ref[0, blk, :, 0:128], xs[blk, pl.ds(rp, 8), 0:128])
                pi = jnp.where(first, cs_ref[0, blk, :, 128:256], xs[blk, pl.ds(rp, 8), 128:256])
                new_a += [nr, ni]
                new_g += [glr + nr * pr + ni * pi, gli + ni * pr - nr * pi]
            return tuple(new_a), tuple(new_g)

        a0, g0 = [], []
        for blk in range(S5_BLOCKS):
            a0 += [acarry[blk, :, 0:128], acarry[blk, :, 128:256]]
            g0 += [dlr_ref[blk], dli_ref[blk]]
        an, gn = lax.fori_loop(0, tc, bstep, (tuple(a0), tuple(g0)), unroll=2)
        for blk in range(S5_BLOCKS):
            acarry[blk, :, 0:128] = an[2 * blk]
            acarry[blk, :, 128:256] = an[2 * blk + 1]
            dlr_ref[blk] = gn[2 * blk]
            dli_ref[blk] = gn[2 * blk + 1]
        for blk in range(S5_BLOCKS):
            sl = slice(blk * 256, (blk + 1) * 256)
            ab = adj[blk].astype(BF16)
            _stage(tmp, jnp.dot(ab, rbt_ref[blk], preferred_element_type=F32))
            du_ref[:, sl] = _gather_rows(tmp, tc) + d_ref[:, sl] * dy_ref[:, sl]
            drb_ref[blk] += lax.dot_general(lhsu[blk], ab, (((0,), (0,)), ((), ())), preferred_element_type=F32)
            drc_ref[blk] += lax.dot_general(lhsd[blk], xs[blk].astype(BF16), (((0,), (0,)), ((), ())),
                                            preferred_element_type=F32)

    rev = pl.BlockSpec((tc, D_MODEL), lambda i: (nc - 1 - i, 0))
    vec = pl.BlockSpec((1, D_MODEL), lambda i: (0, 0))
    mat = pl.BlockSpec((S5_BLOCKS, 256, 256), lambda i: (0, 0, 0))
    lamspec = pl.BlockSpec((S5_BLOCKS, 8, 128), lambda i: (0, 0, 0))
    big = pltpu.VMEM((S5_BLOCKS, 8 * tc, 256), F32)
    bigb = pltpu.VMEM((S5_BLOCKS, 8 * tc, 256), BF16)
    return pl.pallas_call(
        body, grid=(nc,),
        in_specs=[rev, rev, vec, pl.BlockSpec((1, S5_BLOCKS, 8, 256), lambda i: (nc - 1 - i, 0, 0, 0)),
                  pl.BlockSpec((8 * tc, tc), lambda i: (0, 0)), mat, mat, mat, lamspec, lamspec],
        out_specs=[rev, vec, mat, mat, lamspec, lamspec],
        out_shape=[_sds((n_rows, D_MODEL), F32), _sds((1, D_MODEL), F32), _sds((S5_BLOCKS, 256, 256), F32),
                   _sds((S5_BLOCKS, 256, 256), F32), _sds((S5_BLOCKS, 8, 128), F32), _sds((S5_BLOCKS, 8, 128), F32)],
        scratch_shapes=[pltpu.VMEM((2, 8 * tc, 128), F32), bigb, bigb, big, big, pltpu.VMEM((S5_BLOCKS, 8, 256), F32)],
        name="s5_bwd", compiler_params=_params(("arbitrary",)))(u, dy2, d_skip, cs, _expansion(tc), rb, rbt, rct, lam_r, lam_i)


def _s5_discretise(a_re, a_im, log_dt, b_re, b_im):
    lam = lax.complex(jnp.minimum(a_re, LAMBDA_RE_MAX), a_im)
    dt = jnp.exp(log_dt)[:, None]
    lam_bar = jnp.exp(lam * dt)
    b_bar = ((lam_bar - 1.0) / lam)[:, :, None] * lax.complex(b_re, b_im)
    return jnp.real(lam_bar), jnp.imag(lam_bar), jnp.real(b_bar), jnp.imag(b_bar)


def _s5_matrices(bbar_re, bbar_im, c_re, c_im):
    eye2 = jnp.eye(2, dtype=F32)
    bst = jnp.stack([bbar_re, bbar_im]).reshape(2, S5_BLOCKS, 8, 2, S5_STATE, S5_GROUP)
    bt = jnp.transpose(bst, (1, 2, 3, 5, 0, 4))
    rb = (bt[:, :, :, :, :, None, :] * eye2[None, None, :, None, None, :, None]).reshape(S5_BLOCKS, 256, 256)
    cst = jnp.stack([c_re, -c_im]).reshape(2, S5_BLOCKS, 8, 2, S5_GROUP, S5_STATE)
    ct = jnp.transpose(cst, (1, 0, 5, 2, 3, 4))
    rc = (ct[:, :, None, :, :, :, :] * eye2[None, None, :, None, None, :, None]).reshape(S5_BLOCKS, 256, 256)
    return rb, rc


def s5_compact(drb, drct, dlr, dli):
    def body(drb_ref, drct_ref, dlr_ref, dli_ref, o_ref):
        even = (lax.broadcasted_iota(jnp.int32, (256, 64), 0) // S5_GROUP) % 2 == 0
        for blk in range(S5_BLOCKS):
            for k, ref in enumerate((drb_ref, drct_ref)):
                m = ref[blk]
                re = jnp.where(even, m[:, 0:64], m[:, 64:128])
                im = jnp.where(even, m[:, 128:192], m[:, 192:256])
                o_ref[pl.ds(k * 1024 + blk * 256, 256), :] = jnp.concatenate([re, im], axis=1)
            o_ref[pl.ds(2048 + blk * 8, 8), :] = dlr_ref[blk]
            o_ref[pl.ds(2080 + blk * 8, 8), :] = dli_ref[blk]

    vm = pl.BlockSpec(memory_space=pltpu.VMEM)
    return pl.pallas_call(body, in_specs=[vm] * 4, out_specs=vm, out_shape=_sds((2112, 128), F32), name="s5_compact",
                          compiler_params=_params())(drb, drct, dlr, dli)


def _s5_unpack(mats):
    bm = mats[0:1024].reshape(S5_GROUPS, S5_GROUP, 128)
    cm = mats[1024:2048].reshape(S5_GROUPS, S5_GROUP, 128)
    swap = lambda t: jnp.transpose(t, (0, 2, 1))
    return (swap(bm[:, :, 0:64]), swap(bm[:, :, 64:128]), cm[:, :, 0:64], -cm[:, :, 64:128],
            mats[2048:2080].reshape(S5_GROUPS, S5_STATE), mats[2080:2112].reshape(S5_GROUPS, S5_STATE))


NEG = -1e30


GROUP = N_Q // N_KV


def _attn_masks(n):
    qi = lax.broadcasted_iota(jnp.int32, (GROUP * BLOCK, BLOCK), 0) % BLOCK
    kj = lax.broadcasted_iota(jnp.int32, (GROUP * BLOCK, BLOCK), 1)
    return jnp.logical_and(kj > qi, n > 0), kj <= qi


def _stack_heads(ref, kh):
    return jnp.concatenate([ref[:, (GROUP * kh + g) * HEAD_DIM:(GROUP * kh + g + 1) * HEAD_DIM] for g in range(GROUP)], axis=0)


def _unstack_heads(val):
    return jnp.concatenate([val[g * BLOCK:(g + 1) * BLOCK] for g in range(GROUP)], axis=1)


def _sink_column(sink_ref, kh):
    grp = lax.broadcasted_iota(jnp.int32, (GROUP * BLOCK, 1), 0) // BLOCK
    col = jnp.zeros((GROUP * BLOCK, 1), F32)
    for g in range(GROUP):
        col = jnp.where(grp == g, sink_ref[GROUP * kh + g], col)
    return col, grp


def _attn_exp(q4, kp, kc, sink, mask_p, mask_c):
    scale = 1.0 / math.sqrt(HEAD_DIM)
    nt = (((1,), (1,)), ((), ()))
    sp = jnp.where(mask_p, lax.dot_general(q4, kp, nt, preferred_element_type=F32) * scale, NEG)
    sc = jnp.where(mask_c, lax.dot_general(q4, kc, nt, preferred_element_type=F32) * scale, NEG)
    m = jnp.maximum(jnp.maximum(jnp.max(sp, axis=-1, keepdims=True), jnp.max(sc, axis=-1, keepdims=True)), sink)
    pp = jnp.exp(sp - m)
    pc = jnp.exp(sc - m)
    ps = jnp.exp(sink - m)
    inv = 1.0 / (jnp.sum(pp, axis=-1, keepdims=True) + jnp.sum(pc, axis=-1, keepdims=True) + ps)
    return pp, pc, ps, inv


def attn_fwd(q, kv, sinks):
    n_rows = q.shape[0]
    nb = n_rows // BLOCK

    def body(sink_ref, q_ref, kvp_ref, kvc_ref, o_ref):
        n = pl.program_id(0)
        mask_p, mask_c = _attn_masks(n)
        outs = []
        for kh in range(N_KV):
            ks, vs = slice(kh * HEAD_DIM, (kh + 1) * HEAD_DIM), slice((N_KV + kh) * HEAD_DIM, (N_KV + kh + 1) * HEAD_DIM)
            sink, _ = _sink_column(sink_ref, kh)
            pp, pc, _, inv = _attn_exp(_stack_heads(q_ref, kh), kvp_ref[:, ks], kvc_ref[:, ks], sink, mask_p, mask_c)
            o4 = (jnp.dot(pp.astype(BF16), kvp_ref[:, vs], preferred_element_type=F32)
                  + jnp.dot(pc.astype(BF16), kvc_ref[:, vs], preferred_element_type=F32)) * inv
            outs.append(_unstack_heads(o4))
        o_ref[...] = jnp.concatenate(outs, axis=1).astype(BF16)

    kvw = 2 * N_KV * HEAD_DIM
    return pl.pallas_call(
        body, grid=(nb,),
        in_specs=[pl.BlockSpec(memory_space=pltpu.SMEM), pl.BlockSpec((BLOCK, D_MODEL), lambda n: (n, 0)),
                  pl.BlockSpec((BLOCK, kvw), lambda n: (jnp.maximum(n - 1, 0), 0)), pl.BlockSpec((BLOCK, kvw), lambda n: (n, 0))],
        out_specs=pl.BlockSpec((BLOCK, D_MODEL), lambda n: (n, 0)), out_shape=_sds((n_rows, D_MODEL), BF16),
        name="attn_fwd", compiler_params=_params(("parallel",)))(sinks, q, kv, kv)


def attn_bwd(q, kv, do, sinks):
    n_rows = q.shape[0]
    nb = n_rows // BLOCK
    kvw = 2 * N_KV * HEAD_DIM
    tn = (((0,), (0,)), ((), ()))
    nt = (((1,), (1,)), ((), ()))
    scale = 1.0 / math.sqrt(HEAD_DIM)

    def body(sink_ref, q_ref, kvp_ref, kvc_ref, do_ref, dq_ref, dbq_ref, dprev_ref, dcur_ref, dsink_ref):
        n = pl.program_id(0)
        mask_p, mask_c = _attn_masks(n)
        lane = lax.broadcasted_iota(jnp.int32, (1, D_MODEL), 1)
        dqs, dsink = [], jnp.zeros((1, D_MODEL), F32)
        dkp, dkc, dvp, dvc = [], [], [], []
        for kh in range(N_KV):
            ks, vs = slice(kh * HEAD_DIM, (kh + 1) * HEAD_DIM), slice((N_KV + kh) * HEAD_DIM, (N_KV + kh + 1) * HEAD_DIM)
            q4, do4 = _stack_heads(q_ref, kh), _stack_heads(do_ref, kh)
            kp, kc, vp, vc = kvp_ref[:, ks], kvc_ref[:, ks], kvp_ref[:, vs], kvc_ref[:, vs]
            sink, grp = _sink_column(sink_ref, kh)
            pp, pc, ps, inv = _attn_exp(q4, kp, kc, sink, mask_p, mask_c)
            pp, pc = pp * inv, pc * inv
            dpp = lax.dot_general(do4, vp, nt, preferred_element_type=F32)
            dpc = lax.dot_general(do4, vc, nt, preferred_element_type=F32)
            delta = jnp.sum(pp * dpp, axis=-1, keepdims=True) + jnp.sum(pc * dpc, axis=-1, keepdims=True)
            dsp = (pp * (dpp - delta) * scale).astype(BF16)
            dsc = (pc * (dpc - delta) * scale).astype(BF16)
            dsk = ps * inv * delta
            for g in range(GROUP):
                dsink = dsink + jnp.where(lane == GROUP * kh + g, -jnp.sum(jnp.where(grp == g, dsk, 0.0)), 0.0)
            dqs.append(_unstack_heads(jnp.dot(dsp, kp, preferred_element_type=F32)
                                      + jnp.dot(dsc, kc, preferred_element_type=F32)))
            dkp.append(lax.dot_general(dsp, q4, tn, preferred_element_type=F32))
            dkc.append(lax.dot_general(dsc, q4, tn, preferred_element_type=F32))
            dvp.append(lax.dot_general(pp.astype(BF16), do4, tn, preferred_element_type=F32))
            dvc.append(lax.dot_general(pc.astype(BF16), do4, tn, preferred_element_type=F32))
        dq = jnp.concatenate(dqs, axis=1)
        dq_ref[...] = dq.astype(BF16)
        dprev_ref[0] = jnp.concatenate(dkp + dvp, axis=1)
        dcur_ref[0] = jnp.concatenate(dkc + dvc, axis=1)

        @pl.when(n == 0)
        def _():
            dbq_ref[...] = jnp.zeros_like(dbq_ref)
            dsink_ref[...] = jnp.zeros_like(dsink_ref)

        dbq_ref[...] += jnp.sum(dq, axis=0, keepdims=True)
        dsink_ref[...] += dsink

    blk = pl.BlockSpec((BLOCK, D_MODEL), lambda n: (n, 0))
    part = pl.BlockSpec((1, BLOCK, kvw), lambda n: (n, 0, 0))
    return pl.pallas_call(
        body, grid=(nb,),
        in_specs=[pl.BlockSpec(memory_space=pltpu.SMEM), blk,
                  pl.BlockSpec((BLOCK, kvw), lambda n: (jnp.maximum(n - 1, 0), 0)), pl.BlockSpec((BLOCK, kvw), lambda n: (n, 0)), blk],
        out_specs=[blk, pl.BlockSpec((1, D_MODEL), lambda n: (0, 0)), part, part, pl.BlockSpec((1, D_MODEL), lambda n: (0, 0))],
        out_shape=[_sds((n_rows, D_MODEL), BF16), _sds((1, D_MODEL), F32), _sds((nb, BLOCK, kvw), F32),
                   _sds((nb, BLOCK, kvw), F32), _sds((1, D_MODEL), F32)],
        name="attn_bwd", compiler_params=_params(("arbitrary",)))(sinks, q, kv, kv, do)


def kv_combine(dprev, dcur):
    nb, _, kvw = dprev.shape

    def body(dcur_ref, dnext_ref, dkv_ref, db_ref):
        m = pl.program_id(0)
        dkv = dcur_ref[0] + jnp.where(m + 1 < nb, dnext_ref[0], 0.0)
        dkv_ref[...] = dkv.astype(BF16)

        @pl.when(m == 0)
        def _():
            db_ref[...] = jnp.zeros_like(db_ref)

        db_ref[:, 0:kvw] += jnp.sum(dkv, axis=0, keepdims=True)

    return pl.pallas_call(
        body, grid=(nb,),
        in_specs=[pl.BlockSpec((1, BLOCK, kvw), lambda m: (m, 0, 0)),
                  pl.BlockSpec((1, BLOCK, kvw), lambda m: (jnp.minimum(m + 1, nb - 1), 0, 0))],
        out_specs=[pl.BlockSpec((BLOCK, kvw), lambda m: (m, 0)), pl.BlockSpec((1, D_MODEL), lambda m: (0, 0))],
        out_shape=[_sds((nb * BLOCK, kvw), BF16), _sds((1, D_MODEL), F32)],
        name="kv_combine", compiler_params=_params(("arbitrary",)))(dcur, dprev)


def glu_bwd(dout, val, gate, tm=256):
    n_rows, d = dout.shape

    def body(do_ref, v_ref, g_ref, dz_ref, db_ref):
        i = pl.program_id(0)
        sg = jax.nn.sigmoid(g_ref[...])
        dval = do_ref[...] * sg
        dgate = do_ref[...] * v_ref[...] * sg * (1.0 - sg)
        dz_ref[...] = jnp.concatenate([dval, dgate], axis=1).astype(BF16)

        @pl.when(i == 0)
        def _():
            db_ref[...] = jnp.zeros_like(db_ref)

        db_ref[0:1, :] += jnp.sum(dval, axis=0, keepdims=True)
        db_ref[1:2, :] += jnp.sum(dgate, axis=0, keepdims=True)

    row = pl.BlockSpec((tm, d), lambda i: (i, 0))
    return pl.pallas_call(
        body, grid=(n_rows // tm,), in_specs=[row, row, row],
        out_specs=[pl.BlockSpec((tm, 2 * d), lambda i: (i, 0)), pl.BlockSpec((2, d), lambda i: (0, 0))],
        out_shape=[_sds((n_rows, 2 * d), BF16), _sds((2, d), F32)],
        name="glu_bwd", compiler_params=_params(("arbitrary",)))(dout, val, gate)


def final_loss(h, target, gain, tm=256):
    n_rows, d = h.shape

    def body(h_ref, t_ref, g_ref, loss_ref, dh_ref, dhb_ref, dg_ref):
        i = pl.program_id(0)
        xh, r = _rms_hat(h_ref[...])
        err = xh * g_ref[...] - t_ref[...]
        dy = err * (1.0 / d)
        dxh = dy * g_ref[...]
        dx = r * (dxh - xh * jnp.mean(dxh * xh, axis=-1, keepdims=True))
        dh_ref[...] = dx
        dhb_ref[...] = dx.astype(BF16)

        @pl.when(i == 0)
        def _():
            loss_ref[...] = jnp.zeros_like(loss_ref)
            dg_ref[...] = jnp.zeros_like(dg_ref)

        loss_ref[...] += jnp.full((1, d), 0.5 * jnp.sum(jnp.mean(err * err, axis=-1, keepdims=True)), F32)
        dg_ref[...] += jnp.sum(dy * xh, axis=0, keepdims=True)

    row = pl.BlockSpec((tm, d), lambda i: (i, 0))
    vec = pl.BlockSpec((1, d), lambda i: (0, 0))
    return pl.pallas_call(
        body, grid=(n_rows // tm,), in_specs=[row, row, vec],
        out_specs=[vec, row, row, vec],
        out_shape=[_sds((1, d), F32), _sds((n_rows, d), F32), _sds((n_rows, d), BF16), _sds((1, d), F32)],
        name="final_loss", compiler_params=_params(("arbitrary",)))(h, target, gain)


def _adam_update(w, g, m, v):
    nm = ADAM_B1 * m + (1.0 - ADAM_B1) * g
    nv = ADAM_B2 * v + (1.0 - ADAM_B2) * (g * g)
    m_hat = nm / (1.0 - ADAM_B1 ** ADAM_STEP)
    v_hat = nv / (1.0 - ADAM_B2 ** ADAM_STEP)
    return -ADAM_LR * (m_hat / (jnp.sqrt(v_hat) + ADAM_EPS) + ADAM_WD * w), nm, nv


def adamw(name, w, g, m, v, tm=256):
    n_rows, d = w.shape
    tm = tm if n_rows % tm == 0 else n_rows

    def body(w_ref, g_ref, m_ref, v_ref, go_ref, d_ref, nm_ref, nv_ref):
        gv = g_ref[...]
        go_ref[...] = gv
        d_ref[...], nm_ref[...], nv_ref[...] = _adam_update(w_ref[...], gv, m_ref[...], v_ref[...])

    row = pl.BlockSpec((tm, d), lambda i: (i, 0))
    return pl.pallas_call(
        body, grid=(n_rows // tm,), in_specs=[row] * 4, out_specs=[row] * 4,
        out_shape=[_sds((n_rows, d), F32)] * 4, name=name, compiler_params=_params(("parallel",)))(w, g, m, v)


def adamw_native(name, ws, gs, ms, vs):
    n = len(ws)

    def body(*refs):
        w_refs, g_refs, m_refs, v_refs = refs[:n], refs[n:2 * n], refs[2 * n:3 * n], refs[3 * n:4 * n]
        d_refs, nm_refs, nv_refs = refs[4 * n:5 * n], refs[5 * n:6 * n], refs[6 * n:7 * n]
        for k in range(n):
            dl, nm, nv = _adam_update(w_refs[k][...], g_refs[k][...], m_refs[k][...], v_refs[k][...])
            d_refs[k][...] = dl
            nm_refs[k][...] = nm
            nv_refs[k][...] = nv

    vm = pl.BlockSpec(memory_space=pltpu.VMEM)
    shapes = [_sds(w.shape, F32) for w in ws]
    out = pl.pallas_call(body, in_specs=[vm] * (4 * n), out_specs=[vm] * (3 * n), out_shape=shapes * 3, name=name,
                         compiler_params=_params())(*ws, *gs, *ms, *vs)
    return list(out[:n]), list(out[n:2 * n]), list(out[2 * n:])


VEC_ROWS = {"norm_mix": 0, "norm_mlp": 2, "norm_kv": 4, "norm_final": 5, "s5_d": 6, "b_q": 7, "b_o": 8, "s5_b_glu": 9,
            "b_kv": 11, "sinks": 12, "loss": 13}


def split_vectors(where, vecs, d_shard, glu_shard):
    kvw = 2 * N_KV * HEAD_DIM
    shapes = {"norm_mix": (2, D_MODEL), "norm_mlp": (2, D_MODEL), "norm_kv": (1, D_MODEL), "norm_final": (1, D_MODEL),
              "s5_d": (1, d_shard), "b_q": (1, D_MODEL), "b_o": (1, D_MODEL), "s5_b_glu": (1, glu_shard), "b_kv": (1, kvw),
              "sinks": (1, N_Q), "loss": (1, 128)}
    names = list(shapes)

    def body(where_ref, v_ref, *o_refs):
        chip = where_ref[1]
        for name, o_ref in zip(names, o_refs):
            r0, (r, n) = VEC_ROWS[name], shapes[name]
            if name == "s5_d":
                g = jnp.zeros((1, n), F32)
                for j in range(4):
                    g = jnp.where(chip == j, v_ref[r0:r0 + 1, j * n:(j + 1) * n], g)
            elif name == "s5_b_glu":
                g = jnp.zeros((1, n), F32)
                for j in range(4):
                    row, col = r0 + (j * n) // D_MODEL, (j * n) % D_MODEL
                    g = jnp.where(chip == j, v_ref[row:row + 1, col:col + n], g)
            else:
                g = v_ref[r0:r0 + r, 0:n]
            o_ref[...] = g

    vm = pl.BlockSpec(memory_space=pltpu.VMEM)
    out = pl.pallas_call(body, in_specs=[pl.BlockSpec(memory_space=pltpu.SMEM), vm], out_specs=[vm] * len(names),
                         out_shape=[_sds(shapes[n], F32) for n in names], name="split_vectors",
                         compiler_params=_params())(where, vecs)
    return dict(zip(names, out))


def _position():
    x, y, c = lax.axis_index("x"), lax.axis_index("y"), lax.axis_index("c")
    others = [(1 - x, y), (x, 1 - y), (1 - x, 1 - y)]
    return x, y, c, others


def _window(ref, kind, chip, half, shard_shape):
    r, n = shard_shape
    if kind == "col":
        return ref.at[pl.ds(pl.multiple_of(half * (r // 2), 16), r // 2), pl.ds(pl.multiple_of(chip * n, 128), n)]
    return ref.at[pl.ds(pl.multiple_of(chip * r, 16), r), pl.ds(pl.multiple_of(half * (n // 2), 128), n // 2)]


def _half(ref, kind, half, shape):
    r, n = shape
    if kind == "col":
        return ref.at[pl.ds(pl.multiple_of(half * (r // 2), 16), r // 2), :]
    return ref.at[:, pl.ds(pl.multiple_of(half * (n // 2), 128), n // 2)]


def swap_halves(name, grads, kinds):
    nt = len(grads)
    shapes = [tuple(g.shape) for g in grads]

    def body(*refs):
        in_refs, out_refs = refs[:nt], refs[nt:2 * nt]
        send_sems, recv_sems = refs[2 * nt:]
        x, y, c, _ = _position()
        cps = []
        for t in range(nt):
            cp = pltpu.make_async_remote_copy(
                src_ref=_half(in_refs[t], kinds[t], 1 - c, shapes[t]), dst_ref=_half(out_refs[t], kinds[t], 1 - c, shapes[t]),
                send_sem=send_sems.at[t], recv_sem=recv_sems.at[t], device_id=(x, y, 1 - c), device_id_type=MESH)
            cp.start()
            cps.append(cp)
        for t in range(nt):
            mine = _half(out_refs[t], kinds[t], c, shapes[t])
            pltpu.make_async_remote_copy(
                src_ref=mine, dst_ref=mine, send_sem=send_sems.at[t], recv_sem=recv_sems.at[t],
                device_id=(x, y, 1 - c), device_id_type=MESH).wait_recv()
        for cp in cps:
            cp.wait_send()

    hbm = pl.BlockSpec(memory_space=pl.ANY)
    return pl.pallas_call(
        body, in_specs=[hbm] * nt, out_specs=[hbm] * nt, out_shape=[_sds(s, BF16) for s in shapes],
        scratch_shapes=[pltpu.SemaphoreType.DMA((nt,)), pltpu.SemaphoreType.DMA((nt,))],
        name=name, compiler_params=_params())(*grads)


def _half_spec(kind, shape, tiles):
    r, n = shape
    if kind == "col":
        tn = n // tiles
        return pl.BlockSpec((r // 2, tn), lambda i, s: (s[0], i))
    tm = r // tiles
    return pl.BlockSpec((tm, n // 2), lambda i, s: (i, s[0]))


def add_halves(name, mine, landed, kinds, where, tiles=4):
    nt = len(mine)
    shapes = [tuple(a.shape) for a in mine]

    def compact(t):
        r, n = shapes[t]
        if kinds[t] == "col":
            return (r // 2, n), pl.BlockSpec((r // 2, n // tiles), lambda i, s: (0, i))
        return (r, n // 2), pl.BlockSpec((r // tiles, n // 2), lambda i, s: (i, 0))

    def body(s_ref, *refs):
        for a_ref, b_ref, o_ref in zip(refs[:nt], refs[nt:2 * nt], refs[2 * nt:]):
            o_ref[...] = (a_ref[...].astype(F32) + b_ref[...].astype(F32)).astype(BF16)

    specs = [_half_spec(kinds[t], shapes[t], tiles) for t in range(nt)]
    return pl.pallas_call(
        body, grid_spec=pltpu.PrefetchScalarGridSpec(num_scalar_prefetch=1, grid=(tiles,), in_specs=specs + specs,
                                                     out_specs=[compact(t)[1] for t in range(nt)]),
        out_shape=[_sds(compact(t)[0], BF16) for t in range(nt)], name=name,
        compiler_params=_params(("parallel",)))(where, *mine, *landed)


def sum_shards(name, parts, landed, kinds, shard_shapes, where, layers, n_layers, intos, tiles=2):
    nt = len(parts)
    in_specs, out_specs = [], []
    for t in range(nt):
        (r, n), layer = shard_shapes[t], layers[t]
        if kinds[t] == "col":
            tm, width = r // 2 // tiles, n
            own = pl.BlockSpec((tm, n), lambda i, s: (i, s[1]))
            out = pl.BlockSpec((None, tm, n), lambda i, s, layer=layer: (layer, s[0] * tiles + i, 0))
        else:
            tm, width = r // tiles, n // 2
            own = pl.BlockSpec((tm, n // 2), lambda i, s: (s[1] * tiles + i, 0))
            out = pl.BlockSpec((None, tm, n // 2), lambda i, s, layer=layer: (layer, i, s[0]))
        in_specs += [own, pl.BlockSpec((3, tm, width), lambda i, s: (0, i, 0))]
        out_specs.append(out)
    args, aliases = [where] + [a for pair in zip(parts, landed) for a in pair], {}
    for t in range(nt):
        if intos[t] is not None:
            aliases[len(args)] = t
            in_specs.append(pl.BlockSpec(memory_space=pl.ANY))
            args.append(intos[t])

    def body(s_ref, *refs):
        for t in range(nt):
            a_ref, l_ref, o_ref = refs[2 * t], refs[2 * t + 1], refs[len(in_specs) + t]
            o_ref[...] = ((a_ref[...].astype(F32) + l_ref[0].astype(F32)) + l_ref[1].astype(F32)) + l_ref[2].astype(F32)

    return pl.pallas_call(
        body, grid_spec=pltpu.PrefetchScalarGridSpec(num_scalar_prefetch=1, grid=(tiles,), in_specs=in_specs,
                                                     out_specs=out_specs),
        out_shape=[_sds((n_layers[t],) + tuple(shard_shapes[t]), F32) for t in range(nt)], input_output_aliases=aliases,
        name=name, compiler_params=_params(("parallel",)))(*args)


def share_halves(arrays, entries):
    na, nt = len(arrays), len(entries)

    def body(*refs):
        out_refs = refs[na:2 * na]
        send_sems, recv_sems = refs[2 * na:]
        x, y, c, _ = _position()
        cps = []
        for t, (a, layer, kind) in enumerate(entries):
            shape = tuple(arrays[a].shape[1:])
            mine = _half(out_refs[a].at[layer], kind, c, shape)
            cp = pltpu.make_async_remote_copy(
                src_ref=mine, dst_ref=mine, send_sem=send_sems.at[t], recv_sem=recv_sems.at[t],
                device_id=(x, y, 1 - c), device_id_type=MESH)
            cp.start()
            cps.append(cp)
        for t, (a, layer, kind) in enumerate(entries):
            shape = tuple(arrays[a].shape[1:])
            other = _half(out_refs[a].at[layer], kind, 1 - c, shape)
            pltpu.make_async_remote_copy(
                src_ref=other, dst_ref=other, send_sem=send_sems.at[t], recv_sem=recv_sems.at[t],
                device_id=(x, y, 1 - c), device_id_type=MESH).wait_recv()
        for cp in cps:
            cp.wait_send()

    hbm = pl.BlockSpec(memory_space=pl.ANY)
    return pl.pallas_call(
        body, in_specs=[hbm] * na, out_specs=[hbm] * na, out_shape=[_sds(a.shape, F32) for a in arrays],
        input_output_aliases={i: i for i in range(na)},
        scratch_shapes=[pltpu.SemaphoreType.DMA((nt,)), pltpu.SemaphoreType.DMA((nt,))],
        name="share_halves", compiler_params=_params())(*arrays)


HBM_SPEC = pl.BlockSpec(memory_space=pltpu.HBM)
SEM_SPEC = pl.BlockSpec(memory_space=pltpu.SEMAPHORE)
ANY_SPEC = pl.BlockSpec(memory_space=pl.ANY)


def _split_params():
    return pltpu.CompilerParams(has_side_effects=pltpu.SideEffectType.DATAFLOW_SIDE_EFFECTING,
                                vmem_limit_bytes=VMEM_LIMIT_BYTES)


def _in_hbm(a):
    return pltpu.with_memory_space_constraint(a, pltpu.HBM)


def cast_place(arrays, entries, where, tiles=2):
    in_specs, out_specs, fulls = [], [], []
    for a, layer, kind in entries:
        _, r, n = arrays[a].shape
        tm = r // tiles
        in_specs.append(pl.BlockSpec((None, tm, n), lambda i, s, layer=layer: (layer, i, 0)))
        if kind == "col":
            fulls.append((r, 4 * n))
            out_specs.append(pl.BlockSpec((tm, n), lambda i, s: (i, s[1])))
        else:
            fulls.append((4 * r, n))
            out_specs.append(pl.BlockSpec((tm, n), lambda i, s: (s[1] * tiles + i, 0)))
    nt = len(entries)

    def body(s_ref, *refs):
        for w_ref, o_ref in zip(refs[:nt], refs[nt:]):
            o_ref[...] = w_ref[...].astype(BF16)

    return pl.pallas_call(
        body, grid_spec=pltpu.PrefetchScalarGridSpec(num_scalar_prefetch=1, grid=(tiles,), in_specs=in_specs,
                                                     out_specs=out_specs),
        out_shape=[_sds(f, BF16) for f in fulls], name="cast_place",
        compiler_params=_params(("parallel",)))(where, *[arrays[a] for a, _, _ in entries])


def gather_start(name, fulls, kinds, shard_shapes, after):
    nt = len(fulls)
    na = 0 if after is None else 1

    def body(*refs):
        full_refs = refs[:nt]
        send_sems, recv_sems, token = refs[nt + na], refs[nt + na + 1], refs[-1]
        x, y, c, others = _position()
        for t in range(nt):
            mine = _window(full_refs[t], kinds[t], 2 * x + y, c, shard_shapes[t])
            for j, (ox, oy) in enumerate(others):
                pltpu.make_async_remote_copy(
                    src_ref=mine, dst_ref=mine, send_sem=send_sems.at[3 * t + j], recv_sem=recv_sems.at[3 * t + j],
                    device_id=(ox, oy, c), device_id_type=MESH).start()
        token[...] = jnp.zeros_like(token)

    sems = pltpu.SemaphoreType.DMA((3 * nt,))
    out = pl.pallas_call(
        body, name=name, in_specs=[HBM_SPEC] * nt + [ANY_SPEC] * na,
        out_specs=(SEM_SPEC, SEM_SPEC, *[HBM_SPEC] * nt, pl.BlockSpec(memory_space=pltpu.VMEM)),
        out_shape=(sems, sems, *[pltpu.HBM(f.shape, f.dtype) for f in fulls], _sds((8, 128), F32)),
        input_output_aliases={t: 2 + t for t in range(nt)}, compiler_params=_split_params(),
    )(*[_in_hbm(f) for f in fulls], *([] if after is None else [after]))
    return out[0], out[1], list(out[2:2 + nt]), out[-1]


def gather_wait(name, send_sems, recv_sems, fulls, kinds, shard_shapes, after):
    nt = len(fulls)

    def body(*refs):
        full_refs, send_ref, recv_ref = refs[:nt], refs[nt], refs[nt + 1]
        x, y, c, others = _position()
        for t in range(nt):
            mine = _window(full_refs[t], kinds[t], 2 * x + y, c, shard_shapes[t])
            for j, (ox, oy) in enumerate(others):
                cp = pltpu.make_async_remote_copy(
                    src_ref=mine, dst_ref=_window(full_refs[t], kinds[t], 2 * ox + oy, c, shard_shapes[t]),
                    send_sem=send_ref.at[3 * t + j], recv_sem=recv_ref.at[3 * t + j],
                    device_id=(ox, oy, c), device_id_type=MESH)
                cp.wait_send()
                cp.wait_recv()

    out = pl.pallas_call(
        body, name=name, in_specs=[HBM_SPEC] * nt + [SEM_SPEC, SEM_SPEC, HBM_SPEC], out_specs=[HBM_SPEC] * nt,
        out_shape=[pltpu.HBM(f.shape, f.dtype) for f in fulls], input_output_aliases={t: t for t in range(nt)},
        compiler_params=_split_params())(*fulls, send_sems, recv_sems, _in_hbm(after))
    return list(out)


def forward_halves(name, fulls, kinds, shard_shapes):
    nt = len(fulls)

    def body(*refs):
        out_refs = refs[nt:2 * nt]
        send_sems, recv_sems = refs[2 * nt:]
        x, y, c, others = _position()
        cps = []
        for t in range(nt):
            for j, (ox, oy) in enumerate(others):
                landed = _window(out_refs[t], kinds[t], 2 * ox + oy, c, shard_shapes[t])
                cp = pltpu.make_async_remote_copy(
                    src_ref=landed, dst_ref=landed, send_sem=send_sems.at[3 * t + j], recv_sem=recv_sems.at[3 * t + j],
                    device_id=(x, y, 1 - c), device_id_type=MESH)
                cp.start()
                cps.append(cp)
        for t in range(nt):
            for j, (ox, oy) in enumerate(others):
                got = _window(out_refs[t], kinds[t], 2 * ox + oy, 1 - c, shard_shapes[t])
                pltpu.make_async_remote_copy(
                    src_ref=got, dst_ref=got, send_sem=send_sems.at[3 * t + j], recv_sem=recv_sems.at[3 * t + j],
                    device_id=(x, y, 1 - c), device_id_type=MESH).wait_recv()
        for cp in cps:
            cp.wait_send()

    out = pl.pallas_call(
        body, in_specs=[ANY_SPEC] * nt, out_specs=[ANY_SPEC] * nt, out_shape=[_sds(f.shape, f.dtype) for f in fulls],
        input_output_aliases={t: t for t in range(nt)},
        scratch_shapes=[pltpu.SemaphoreType.DMA((3 * nt,)), pltpu.SemaphoreType.DMA((3 * nt,))],
        name=name, compiler_params=_params())(*fulls)
    return list(out)


def _piece(ref, kind, chip, shard_shape):
    r, n = shard_shape
    if kind == "col":
        return ref.at[:, pl.ds(pl.multiple_of(chip * n, 128), n)]
    return ref.at[pl.ds(pl.multiple_of(chip * r, 16), r), :]


def _piece_shape(kind, shard_shape):
    r, n = shard_shape
    return (r // 2, n) if kind == "col" else (r, n // 2)


def exchange_start(name, parts, kinds, shard_shapes):
    nt = len(parts)
    lands = [lax.empty((3,) + _piece_shape(kinds[t], shard_shapes[t]), BF16) for t in range(nt)]

    def body(*refs):
        part_refs, land_refs = refs[:nt], refs[nt:2 * nt]
        send_sems, recv_sems, token = refs[2 * nt], refs[2 * nt + 1], refs[-1]
        x, y, c, others = _position()
        for t in range(nt):
            for j, (ox, oy) in enumerate(others):
                pltpu.make_async_remote_copy(
                    src_ref=_piece(part_refs[t], kinds[t], 2 * ox + oy, shard_shapes[t]), dst_ref=land_refs[t].at[j],
                    send_sem=send_sems.at[3 * t + j], recv_sem=recv_sems.at[3 * t + j],
                    device_id=(ox, oy, c), device_id_type=MESH).start()
        token[...] = jnp.zeros_like(token)

    sems = pltpu.SemaphoreType.DMA((3 * nt,))
    both = list(parts) + lands
    out = pl.pallas_call(
        body, name=name, in_specs=[HBM_SPEC] * (2 * nt),
        out_specs=(SEM_SPEC, SEM_SPEC, *[HBM_SPEC] * (2 * nt), pl.BlockSpec(memory_space=pltpu.VMEM)),
        out_shape=(sems, sems, *[pltpu.HBM(a.shape, a.dtype) for a in both], _sds((8, 128), F32)),
        input_output_aliases={t: 2 + t for t in range(2 * nt)}, compiler_params=_split_params(),
    )(*[_in_hbm(a) for a in both])
    return out[0], out[1], list(out[2:2 + nt]), list(out[2 + nt:2 + 2 * nt]), out[-1]


def exchange_wait(name, send_sems, recv_sems, parts, lands, kinds, shard_shapes, after):
    nt = len(parts)

    def body(*refs):
        part_refs, land_refs = refs[:nt], refs[nt:2 * nt]
        send_ref, recv_ref = refs[2 * nt], refs[2 * nt + 1]
        x, y, c, others = _position()
        for t in range(nt):
            for j, (ox, oy) in enumerate(others):
                cp = pltpu.make_async_remote_copy(
                    src_ref=_piece(part_refs[t], kinds[t], 2 * ox + oy, shard_shapes[t]), dst_ref=land_refs[t].at[j],
                    send_sem=send_ref.at[3 * t + j], recv_sem=recv_ref.at[3 * t + j],
                    device_id=(ox, oy, c), device_id_type=MESH)
                cp.wait_send()
                cp.wait_recv()

    both = list(parts) + list(lands)
    out = pl.pallas_call(
        body, name=name, in_specs=[HBM_SPEC] * (2 * nt) + [SEM_SPEC, SEM_SPEC, HBM_SPEC], out_specs=[HBM_SPEC] * (2 * nt),
        out_shape=[pltpu.HBM(a.shape, a.dtype) for a in both], input_output_aliases={t: t for t in range(2 * nt)},
        compiler_params=_split_params())(*both, send_sems, recv_sems, _in_hbm(after))
    return list(out[:nt]), list(out[nt:])


def all_reduce_small(name, bufs):
    n = len(bufs)
    halves = [b.shape[0] // 2 for b in bufs]

    def body(*refs):
        in_refs, out_refs, lands = refs[:n], refs[n:2 * n], refs[2 * n:3 * n]
        send_sems, recv_sems = refs[3 * n:]
        x, y, c, _ = _position()
        mine = [pl.ds(pl.multiple_of(c * h, 8), h) for h in halves]
        other = [pl.ds(pl.multiple_of((1 - c) * h, 8), h) for h in halves]
        for k in range(n):
            out_refs[k][mine[k], :] = in_refs[k][mine[k], :]
        for s, peer in enumerate([(x, y, 1 - c), (1 - x, y, c), (x, 1 - y, c)]):
            cps = []
            for k in range(n):
                src = in_refs[k].at[other[k]] if s == 0 else out_refs[k].at[mine[k]]
                cp = pltpu.make_async_remote_copy(
                    src_ref=src, dst_ref=lands[k].at[s], send_sem=send_sems.at[4 * k + s], recv_sem=recv_sems.at[4 * k + s],
                    device_id=peer, device_id_type=MESH)
                cp.start()
                cps.append(cp)
            for k, cp in enumerate(cps):
                cp.wait()
                out_refs[k][mine[k], :] = out_refs[k][mine[k], :] + lands[k][s]
        cps = []
        for k in range(n):
            cp = pltpu.make_async_remote_copy(
                src_ref=out_refs[k].at[mine[k]], dst_ref=out_refs[k].at[mine[k]], send_sem=send_sems.at[4 * k + 3],
                recv_sem=recv_sems.at[4 * k + 3], device_id=(x, y, 1 - c), device_id_type=MESH)
            cp.start()
            cps.append(cp)
        for cp in cps:
            cp.wait()

    vm = pl.BlockSpec(memory_space=pltpu.VMEM)
    out = pl.pallas_call(
        body, in_specs=[vm] * n, out_specs=[vm] * n, out_shape=[_sds(b.shape, F32) for b in bufs],
        scratch_shapes=[pltpu.VMEM((3, h, b.shape[1]), F32) for h, b in zip(halves, bufs)]
        + [pltpu.SemaphoreType.DMA((4 * n,)), pltpu.SemaphoreType.DMA((4 * n,))],
        name=name, compiler_params=_params())(*bufs)
    return list(out)


def _local_step(x, target, small, need, emit):
    d = D_MODEL
    full = {}

    def after_token(vec, token):
        return vec if token is None else vec + token[0:1, 0:1]

    lam_r, lam_i, bbar_re, bbar_im = small["s5_disc"]
    rb, rc = _s5_matrices(bbar_re, bbar_im, small["s5_c_re"], small["s5_c_im"])
    rb16, rc16 = rb.astype(BF16), rc.astype(BF16)
    lr_t, li_t = lam_r.reshape(S5_BLOCKS, 8, 128), lam_i.reshape(S5_BLOCKS, 8, 128)
    (u,) = rms_fwd("norm_mix0", x, [small["norm_mix0"]], [F32])
    ge, y2, cs = s5_fwd(u, small["s5_d"], rb16, rc16, lr_t, li_t)
    full.update(need("glu", ge))

    def norm_rows(h, gains):
        xh, _ = _rms_hat(h)
        return [xh * g for g in gains]

    def glu_epilogue(accs, e, r):
        v, gt = accs[0] + r[0], accs[1] + r[1]
        h = e[0] + v * jax.nn.sigmoid(gt)
        return [h, v, gt] + norm_rows(h, r[2:])

    h1, val, gate, n1 = mm_nn(
        "glu", ge, full["w_glu"], [0, d], d, glu_epilogue, [F32, F32, F32, BF16], extras=[x],
        rowvecs=[(small["s5_b_glu"], 0), (small["s5_b_glu"], d), (small["norm_mlp0"], 0)], tm=512, tn=d)

    def mlp_fwd(tag, h, n, w_in, w_out, next_gains):
        def in_epilogue(accs, e, rv):
            pos = jnp.maximum(accs[0], 0.0)
            return [pos * pos, 2.0 * pos]

        r, slope = mm_nn("mlp_in" + tag, n, w_in, [0], w_in.shape[1], in_epilogue, [BF16, BF16], tm=2048)

        def epilogue(accs, e, rv):
            h_out = e[0] + accs[0]
            return [h_out] + norm_rows(h_out, rv)

        outs = mm_nn("mlp_out" + tag, r, w_out, [0], d, epilogue, [F32] + [BF16] * len(next_gains), extras=[h],
                     rowvecs=[(g, 0) for g in next_gains], tm=512, tn=d)
        return outs[0], outs[1:], (n, r, slope)

    full.update(need("mlp0", h1))
    h2, (nkv, n2), mlp0 = mlp_fwd("0", h1, n1, full["w_in0"], full["w_out0"], [small["norm_kv"], small["norm_mix1"]])

    full.update(need("attn", h2))
    kvw = 2 * N_KV * HEAD_DIM
    (kv,) = mm_nn("kv_proj", nkv, full["w_kv"], [0], kvw, lambda accs, e, r: [accs[0] + r[0]], [BF16],
                  rowvecs=[(small["b_kv"], 0)])
    (q,) = mm_nn("q_proj", n2, full["w_q"], [0], d, lambda accs, e, r: [accs[0] + r[0]], [BF16],
                 rowvecs=[(small["b_q"], 0)])
    sinks = small["sinks"].reshape(N_Q)
    o = attn_fwd(q, kv, sinks)
    def o_epilogue(accs, e, r):
        h_out = e[0] + accs[0] + r[0]
        return [h_out] + norm_rows(h_out, r[1:])

    h3, n3 = mm_nn("o_proj", o, full["w_o"], [0], d, o_epilogue, [F32, BF16], extras=[h2],
                   rowvecs=[(small["b_o"], 0), (small["norm_mlp1"], 0)], tm=512, tn=d)
    full.update(need("mlp1", h3))
    h4, _, mlp1 = mlp_fwd("1", h3, n3, full["w_in1"], full["w_out1"], [])
    loss_tile, dh, dhb, dg_final = final_loss(h4, target, small["norm_final"])

    grads_small, grads_full = {"norm_final": dg_final}, {}
    ident = lambda acc, e, r: [acc]
    layer1 = ["w_out1", "w_in1", "w_o", "w_q", "w_kv"]
    layer0 = ["w_out0", "w_in0", "w_glu"]

    def norm_bwd_rows(x_rows, res, dys, gains):
        xh, r = _rms_hat(x_rows)
        dxh = sum(dy * g for dy, g in zip(dys, gains))
        dx = r * (dxh - xh * jnp.mean(dxh * xh, axis=-1, keepdims=True)) + res
        return dx, [jnp.sum(dy * xh, axis=0, keepdims=True) for dy in dys]

    def mlp_bwd(tag, dh, dhb, h_in, gain, w_in, w_out, saved):
        n, r, slope = saved
        grads_full["w_out" + tag] = mm_tn("dw_out" + tag, r, dhb, tn=1024)
        (da,) = mm_nt("mlp_da" + tag, dhb, w_out, lambda acc, e, rv: [acc * e[0].astype(F32)], [BF16], extras=[slope],
                      tm=2048)
        grads_full["w_in" + tag] = mm_tn("dw_in" + tag, n, da, tn=1024)

        def epilogue(acc, e, rv):
            dx, dgs = norm_bwd_rows(e[0], e[1], [acc], rv)
            return [dx, dx, jnp.sum(dx, axis=0, keepdims=True)] + dgs

        dx, dxb, colsum, dg = mm_nt("mlp_dn" + tag, da, w_in, epilogue, [F32, BF16], extras=[h_in, dh], rowvecs=[gain],
                                    n_sums=2, tm=512, tk=d)
        grads_small["norm_mlp" + tag] = dg
        return dx, dxb, colsum

    dh3, dh3b, colsum3 = mlp_bwd("1", dh, dhb, h3, small["norm_mlp1"], full["w_in1"], full["w_out1"], mlp1)
    grads_small["b_o"] = colsum3
    grads_full["w_o"] = mm_tn("dw_o", o, dh3b)
    (do,) = mm_nt("attn_do", dh3b, full["w_o"], ident, [BF16])
    dq, dbq, dprev, dcur, dsink = attn_bwd(q, kv, do, sinks)
    dkv, dbkv = kv_combine(dprev, dcur)
    grads_small["b_q"], grads_small["b_kv"], grads_small["sinks"] = dbq, dbkv, dsink
    grads_full["w_q"] = mm_tn("dw_q", n2, dq)
    grads_full["w_kv"] = mm_tn("dw_kv", nkv, dkv)
    (dnkv,) = mm_nt("kv_dn", dkv, full["w_kv"], ident, [F32])
    token = emit("layer1", {n: grads_full[n] for n in layer1})

    def attn_dn_epilogue(acc, e, rv):
        dx, dgs = norm_bwd_rows(e[0], e[1], [acc, e[2]], rv)
        return [dx, dx] + dgs

    dh2, dh2b, dg_mix1, dg_kv = mm_nt("attn_dn", dq, full["w_q"], attn_dn_epilogue, [F32, BF16], extras=[h2, dh3, dnkv],
                                      rowvecs=[after_token(small["norm_mix1"], token), small["norm_kv"]], n_sums=2,
                                      tm=512, tk=d)
    grads_small["norm_mix1"], grads_small["norm_kv"] = dg_mix1, dg_kv
    dh1, _, _ = mlp_bwd("0", dh2, dh2b, h1, small["norm_mlp0"], full["w_in0"], full["w_out0"], mlp0)

    dz, db_glu = glu_bwd(dh1, val, gate)
    grads_small["s5_b_glu"] = db_glu
    grads_full["w_glu"] = mm_tn("dw_glu", ge, dz, tn=1024)
    token = emit("layer0", {n: grads_full[n] for n in layer0})
    (dy2,) = mm_nt("glu_dy", dz, full["w_glu"], lambda acc, e, rv: [acc * _gelu_grad(e[0])], [F32], extras=[y2])
    rbt16, rct16 = jnp.swapaxes(rb16, 1, 2), jnp.swapaxes(rc16, 1, 2)
    du, dd, drb, drc, dlr, dli = s5_bwd(u, dy2, after_token(small["s5_d"], token), cs, rb16, rbt16, rct16, lr_t, li_t)
    grads_small["s5_d"] = dd
    grads_small["s5_mats"] = (drb, drc, dlr, dli)
    grad_x, _, _, dg_mix0 = rms_bwd("norm_mix0_bwd", x, [du], [small["norm_mix0"]], dh1)
    grads_small["norm_mix0"] = dg_mix0
    return loss_tile, grad_x, grads_small


SMALL_NAMES = ["norm_mix", "norm_mlp", "norm_kv", "norm_final", "s5_a_re", "s5_a_im", "s5_log_dt", "s5_b_re", "s5_b_im",
               "s5_c_re", "s5_c_im", "s5_d", "s5_b_glu", "b_kv", "b_q", "sinks", "b_o"]
BIG_NAMES = ["s5_w_glu", "w_kv", "w_q", "w_o", "w_mlp_in", "w_mlp_out"]
WEIGHT_ORDER = ["norm_mix", "norm_mlp", "norm_kv", "norm_final", "s5_a_re", "s5_a_im", "s5_log_dt", "s5_b_re", "s5_b_im",
                "s5_c_re", "s5_c_im", "s5_d", "s5_w_glu", "s5_b_glu", "w_kv", "b_kv", "w_q", "b_q", "sinks", "w_o", "b_o",
                "w_mlp_in", "w_mlp_out"]


def kernel(x, norm_mix, norm_mlp, norm_kv, norm_final, s5_a_re, s5_a_im, s5_log_dt, s5_b_re, s5_b_im, s5_c_re, s5_c_im, s5_d, s5_w_glu, s5_b_glu, w_kv, b_kv, w_q, b_q, sinks, w_o, b_o, w_mlp_in, w_mlp_out, loss_target, m_norm_mix, m_norm_mlp, m_norm_kv, m_norm_final, m_s5_a_re, m_s5_a_im, m_s5_log_dt, m_s5_b_re, m_s5_b_im, m_s5_c_re, m_s5_c_im, m_s5_d, m_s5_w_glu, m_s5_b_glu, m_w_kv, m_b_kv, m_w_q, m_b_q, m_sinks, m_w_o, m_b_o, m_w_mlp_in, m_w_mlp_out, v_norm_mix, v_norm_mlp, v_norm_kv, v_norm_final, v_s5_a_re, v_s5_a_im, v_s5_log_dt, v_s5_b_re, v_s5_b_im, v_s5_c_re, v_s5_c_im, v_s5_d, v_s5_w_glu, v_s5_b_glu, v_w_kv, v_b_kv, v_w_q, v_b_q, v_sinks, v_w_o, v_b_o, v_w_mlp_in, v_w_mlp_out):
    env = dict(locals())
    w = {n: env[n] for n in WEIGHT_ORDER}
    mom = {n: env["m_" + n] for n in WEIGHT_ORDER}
    var = {n: env["v_" + n] for n in WEIGHT_ORDER}
    d = D_MODEL
    xi, yi, ci = lax.axis_index("x"), lax.axis_index("y"), lax.axis_index("c")
    chip = 2 * xi + yi
    where = jnp.stack([ci, chip]).astype(jnp.int32)

    dsh, bsh = s5_d.shape[1], s5_b_glu.shape[1]
    placed = jnp.concatenate([
        lax.dynamic_update_slice(jnp.zeros((4 * dsh,), F32), s5_d[0], (chip * dsh,)),
        lax.dynamic_update_slice(jnp.zeros((4 * bsh,), F32), s5_b_glu[0], (chip * bsh,))])
    placed = jnp.pad(placed, (0, (-placed.shape[0]) % 2048))
    placed = jnp.where(ci == 0, placed, 0.0).reshape(-1, 128)
    (gathered_rows,) = all_reduce_small("gather_vectors", [placed])
    gathered = gathered_rows.reshape(-1)
    d_full, bglu_full = gathered[:4 * dsh].reshape(1, -1), gathered[4 * dsh:].reshape(1, -1)

    big = [s5_w_glu, w_kv[None], w_q, w_o, w_mlp_in, w_mlp_out]
    entries = [(0, 0, "col"), (1, 0, "row"), (2, 0, "row"), (3, 0, "row"), (4, 0, "col"), (4, 1, "col"),
               (5, 0, "row"), (5, 1, "row")]
    names = ["w_glu", "w_kv", "w_q", "w_o", "w_in0", "w_in1", "w_out0", "w_out1"]
    kinds = dict(zip(names, [k for _, _, k in entries]))
    shard_shapes = dict(zip(names, [tuple(big[a].shape[1:]) for a, _, _ in entries]))

    placed_w = dict(zip(names, cast_place(big, entries, where)))
    gather_groups = {"glu": ["w_glu"], "mlp0": ["w_in0", "w_out0"], "attn": ["w_kv", "w_q", "w_o"],
                     "mlp1": ["w_in1", "w_out1"]}
    started, token = {}, gathered_rows
    for group, members in gather_groups.items():
        send, recv, thru, token = gather_start(
            "gather_start_" + group, [placed_w[n] for n in members], [kinds[n] for n in members],
            [shard_shapes[n] for n in members], token)
        started[group] = (send, recv, thru)

    def need(group, after):
        members = gather_groups[group]
        ks, shapes = [kinds[n] for n in members], [shard_shapes[n] for n in members]
        send, recv, thru = started[group]
        landed = gather_wait("gather_wait_" + group, send, recv, thru, ks, shapes, after)
        return dict(zip(members, forward_halves("forward_halves_" + group, landed, ks, shapes)))

    exchanging = {}

    def emit(group, partial):
        members = list(partial)
        ks, shapes = [kinds[n] for n in members], [shard_shapes[n] for n in members]
        landed = swap_halves("swap_halves_" + group, [partial[n] for n in members], ks)
        sums = add_halves("add_halves_" + group, [partial[n] for n in members], landed, ks, where)
        send, recv, parts, lands, tok = exchange_start("exchange_start_" + group, sums, ks, shapes)
        exchanging[group] = (members, send, recv, parts, lands)
        return tok

    disc = lambda *p: _s5_discretise(p[0], p[1], p[2], p[3], p[4])
    disc_args = (s5_a_re[0], s5_a_im[0], s5_log_dt[0], s5_b_re[0], s5_b_im[0])
    disc_out, disc_vjp = jax.vjp(disc, *disc_args)
    small = {
        "norm_mix0": norm_mix[0:1] + token[0:1, 0:1], "norm_mix1": norm_mix[1:2], "norm_mlp0": norm_mlp[0:1], "norm_mlp1": norm_mlp[1:2],
        "norm_kv": norm_kv.reshape(1, d), "norm_final": norm_final.reshape(1, d), "s5_disc": disc_out,
        "s5_c_re": s5_c_re[0], "s5_c_im": s5_c_im[0], "s5_d": d_full, "s5_b_glu": bglu_full,
        "b_kv": b_kv.reshape(1, -1), "b_q": b_q, "sinks": sinks, "b_o": b_o,
    }
    loss_row, grad_x, gs = _local_step(x[0], loss_target[0], small, need, emit)

    mats = s5_compact(*gs["s5_mats"])
    rows = [gs["norm_mix0"], gs["norm_mix1"], gs["norm_mlp0"], gs["norm_mlp1"], gs["norm_kv"], gs["norm_final"], gs["s5_d"],
            gs["b_q"], gs["b_o"], gs["s5_b_glu"], gs["b_kv"], gs["sinks"], loss_row, jnp.zeros((2, d), F32)]
    vecs, mats = all_reduce_small("reduce_small", [jnp.concatenate(rows, axis=0), mats])
    grads = split_vectors(where, vecs, dsh, bsh)
    loss = grads.pop("loss")[0, 0]
    dbbar_re, dbbar_im, dc_re, dc_im, dlr, dli = _s5_unpack(mats)
    g_are, g_aim, g_dt, g_bre, g_bim = disc_vjp((dlr, dli, dbbar_re, dbbar_im))
    grads.update({"s5_a_re": g_are[None], "s5_a_im": g_aim[None], "s5_log_dt": g_dt[None], "s5_b_re": g_bre[None],
                  "s5_b_im": g_bim[None], "s5_c_re": dc_re[None], "s5_c_im": dc_im[None]})

    reduced = [None] * len(big)
    where_of = dict(zip(names, entries))
    for group, after in (("layer1", grad_x), ("layer0", mats)):
        members, send, recv, parts, lands = exchanging[group]
        ks, shapes = [kinds[n] for n in members], [shard_shapes[n] for n in members]
        parts, lands = exchange_wait("exchange_wait_" + group, send, recv, parts, lands, ks, shapes, after)
        targets = [where_of[n][0] for n in members]
        sums = sum_shards("sum_shards_" + group, parts, lands, ks, shapes, where, [where_of[n][1] for n in members],
                          [big[a].shape[0] for a in targets], [reduced[a] for a in targets])
        for a, arr in zip(targets, sums):
            reduced[a] = arr
    reduced = share_halves(reduced, entries)
    for n, g in zip(BIG_NAMES, reduced):
        grads[n] = g.reshape(w[n].shape)

    delta, new_m, new_v = {}, {}, {}
    for n in BIG_NAMES:
        flat = lambda a: a.reshape(-1, a.shape[-1])
        go, dl, nm, nv = adamw("adamw_" + n, flat(w[n]), flat(grads[n]), flat(mom[n]), flat(var[n]))
        grads[n], delta[n], new_m[n], new_v[n] = (t.reshape(w[n].shape) for t in (go, dl, nm, nv))

    def view(n, a):
        return a.reshape(1, -1) if a.ndim == 1 else jnp.swapaxes(a, -1, -2) if n in ("s5_b_re", "s5_b_im") else a

    sw, sg, sm, sv = ([view(n, t[n]) for n in SMALL_NAMES] for t in (w, grads, mom, var))
    for n, a, b, c_ in zip(SMALL_NAMES, *adamw_native("adamw_small", sw, sg, sm, sv)):
        delta[n], new_m[n], new_v[n] = (view(n, t) if t.ndim == 4 else t for t in (a, b, c_))

    out = [loss.reshape(()), grad_x[None]]
    for table in (grads, delta, new_m, new_v):
        out += [table[n].reshape(w[n].shape) for n in WEIGHT_ORDER]
    return tuple(out)
```

```python
import functools
import math

import jax
import jax.numpy as jnp
from jax import lax
from jax.experimental import pallas as pl
from jax.experimental.pallas import tpu as pltpu

F32 = jnp.float32
BF16 = jnp.bfloat16

D_MODEL = 1024
S5_GROUPS = 64
S5_GROUP = 16
S5_STATE = 64
N_KV = 4
N_Q = 16
HEAD_DIM = 64
BLOCK = 128
NORM_EPS = 1e-5
LAMBDA_RE_MAX = -1e-4
ADAM_LR, ADAM_B1, ADAM_B2, ADAM_EPS, ADAM_WD, ADAM_STEP = 0.001, 0.9, 0.999, 1e-08, 0.01, 10

VMEM_LIMIT_BYTES = 56 * 1024 * 1024
S5_CHUNK = 256
S5_BLOCKS = 4
MESH = pl.DeviceIdType.MESH


def _params(sem=None):
    return pltpu.CompilerParams(dimension_semantics=sem, vmem_limit_bytes=VMEM_LIMIT_BYTES)


def _sds(shape, dtype):
    return jax.ShapeDtypeStruct(shape, dtype)


def _rms_hat(xv):
    r = lax.rsqrt(jnp.mean(xv * xv, axis=-1, keepdims=True) + NORM_EPS)
    return xv * r, r


def mm_nn(name, a, w, col_offsets, n_out, epilogue, out_dtypes, extras=(), rowvecs=(), tm=1024, tn=512):
    m, k = a.shape
    tm, tn = min(tm, m), min(tn, n_out)
    nw, ne, nr = len(col_offsets), len(extras), len(rowvecs)

    def body(a_ref, *refs):
        w_refs, e_refs, r_refs = refs[:nw], refs[nw:nw + ne], refs[nw + ne:nw + ne + nr]
        o_refs = refs[nw + ne + nr:]
        av = a_ref[...]
        accs = [jnp.dot(av, w_ref[...], preferred_element_type=F32) for w_ref in w_refs]
        outs = epilogue(accs, [e[...] for e in e_refs], [r[...] for r in r_refs])
        for o_ref, o in zip(o_refs, outs):
            o_ref[...] = o.astype(o_ref.dtype)

    def wspec(off):
        return pl.BlockSpec((k, tn), lambda j, i, off=off: (0, off // tn + j))

    def rspec(off):
        return pl.BlockSpec((1, tn), lambda j, i, off=off: (0, off // tn + j))

    tile = pl.BlockSpec((tm, tn), lambda j, i: (i, j))
    in_specs = ([pl.BlockSpec((tm, k), lambda j, i: (i, 0))] + [wspec(o) for o in col_offsets]
                + [tile] * ne + [rspec(o) for _, o in rowvecs])
    return pl.pallas_call(
        body, grid=(n_out // tn, m // tm), in_specs=in_specs, out_specs=[tile] * len(out_dtypes),
        out_shape=[_sds((m, n_out), dt) for dt in out_dtypes], name=name,
        compiler_params=_params(("parallel", "parallel")))(a, *([w] * nw), *extras, *[r for r, _ in rowvecs])


def mm_nt(name, g, w, epilogue, out_dtypes, extras=(), rowvecs=(), n_sums=0, tm=512, tk=512):
    m, n = g.shape
    k = w.shape[0]
    tm, tk = min(tm, m), min(tk, k)
    ne, nr, no = len(extras), len(rowvecs), len(out_dtypes)

    def body(g_ref, w_ref, *refs):
        e_refs, r_refs, o_refs, s_refs = refs[:ne], refs[ne:ne + nr], refs[ne + nr:ne + nr + no], refs[ne + nr + no:]
        acc = lax.dot_general(g_ref[...], w_ref[...], (((1,), (1,)), ((), ())), preferred_element_type=F32)
        outs = epilogue(acc, [e[...] for e in e_refs], [r[...] for r in r_refs])
        for o_ref, o in zip(o_refs, outs[:no]):
            o_ref[...] = o.astype(o_ref.dtype)
        if n_sums:
            @pl.when(pl.program_id(0) == 0)
            def _():
                for s_ref in s_refs:
                    s_ref[...] = jnp.zeros_like(s_ref)

            for s_ref, val in zip(s_refs, outs[no:]):
                s_ref[...] += val

    tile = pl.BlockSpec((tm, tk), lambda i, j: (i, j))
    vec = pl.BlockSpec((1, tk), lambda i, j: (0, j))
    sem = ("arbitrary", "parallel") if n_sums else ("parallel", "parallel")
    return pl.pallas_call(
        body, grid=(m // tm, k // tk),
        in_specs=[pl.BlockSpec((tm, n), lambda i, j: (i, 0)), pl.BlockSpec((tk, n), lambda i, j: (j, 0))]
        + [tile] * ne + [vec] * nr,
        out_specs=[tile] * no + [vec] * n_sums,
        out_shape=[_sds((m, k), dt) for dt in out_dtypes] + [_sds((1, k), F32)] * n_sums, name=name,
        compiler_params=_params(sem))(g, w, *extras, *rowvecs)


def mm_tn(name, a, g, tk=512, tn=512):
    m, k = a.shape
    n = g.shape[1]
    tk, tn = min(tk, k), min(tn, n)

    def body(a_ref, g_ref, o_ref):
        acc = lax.dot_general(a_ref[...], g_ref[...], (((0,), (0,)), ((), ())), preferred_element_type=F32)
        o_ref[...] = acc.astype(o_ref.dtype)

    return pl.pallas_call(
        body, grid=(k // tk, n // tn),
        in_specs=[pl.BlockSpec((m, tk), lambda i, j: (0, i)), pl.BlockSpec((m, tn), lambda i, j: (0, j))],
        out_specs=pl.BlockSpec((tk, tn), lambda i, j: (i, j)), out_shape=_sds((k, n), BF16), name=name,
        compiler_params=_params(("parallel", "parallel")))(a, g)


def _row_mask(tc):
    row = lax.broadcasted_iota(jnp.int32, (8 * tc, 256), 0) % 8
    col = lax.broadcasted_iota(jnp.int32, (8 * tc, 256), 1) // 32
    return row == col


def _expand_rows(expand_ref, val, mask):
    rep = jnp.dot(expand_ref[...], val.astype(BF16), preferred_element_type=F32)
    return jnp.where(mask, rep, 0.0).astype(BF16)


def _staged(ref):
    return jnp.concatenate([ref[0], ref[1]], axis=1)


def _stage(ref, val):
    ref[0] = val[:, 0:128]
    ref[1] = val[:, 128:256]


def _gather_rows(src_ref, tc):
    halves = []
    for half in range(2):
        col = lax.broadcasted_iota(jnp.int32, (tc, 128), 1) // 32 + 4 * half
        out = jnp.zeros((tc, 128), F32)
        for s8 in range(4 * half, 4 * half + 4):
            out = jnp.where(col == s8, src_ref.at[half][pl.ds(s8, tc, stride=8), :], out)
        halves.append(out)
    return jnp.concatenate(halves, axis=1)


def _gelu(x):
    c = math.sqrt(2.0 / math.pi)
    return 0.5 * x * (1.0 + jnp.tanh(c * (x + 0.044715 * x * x * x)))


def _gelu_grad(x):
    c = math.sqrt(2.0 / math.pi)
    t = jnp.tanh(c * (x + 0.044715 * x * x * x))
    return 0.5 * (1.0 + t) + 0.5 * x * (1.0 - t * t) * c * (1.0 + 3.0 * 0.044715 * x * x)


def _expansion(tc):
    return (jnp.arange(8 * tc)[:, None] // 8 == jnp.arange(tc)[None, :]).astype(BF16)


def s5_fwd(x, gain, d_skip, rb, rc, lam_r, lam_i):
    n_rows = x.shape[0]
    tc = min(S5_CHUNK, n_rows)
    nc = n_rows // tc

    def body(x_ref, g_ref, d_ref, ex_ref, rb_ref, rc_ref, lr_ref, li_ref, ge_ref, y2_ref, cs_ref, bux, yrows, carry):
        i = pl.program_id(0)
        u = _rms_hat(x_ref[...])[0] * g_ref[...]

        @pl.when(i == 0)
        def _():
            carry[...] = jnp.zeros_like(carry)

        cs_ref[0] = carry[...]
        mask = _row_mask(tc)
        for blk in range(S5_BLOCKS):
            lhs = _expand_rows(ex_ref, u[:, blk * 256:(blk + 1) * 256], mask)
            bux[blk] = jnp.dot(lhs, rb_ref[blk], preferred_element_type=F32)
        lam = [(lr_ref[blk], li_ref[blk]) for blk in range(S5_BLOCKS)]

        def step(t, c):
            r0 = pl.multiple_of(t * 8, 8)
            new = []
            for blk in range(S5_BLOCKS):
                xr, xi = c[2 * blk], c[2 * blk + 1]
                lr, li = lam[blk]
                nr = lr * xr - li * xi + bux[blk, pl.ds(r0, 8), 0:128]
                ni = lr * xi + li * xr + bux[blk, pl.ds(r0, 8), 128:256]
                bux[blk, pl.ds(r0, 8), 0:128] = nr
                bux[blk, pl.ds(r0, 8), 128:256] = ni
                new += [nr, ni]
            return tuple(new)

        c0 = []
        for blk in range(S5_BLOCKS):
            c0 += [carry[blk, :, 0:128], carry[blk, :, 128:256]]
        cn = lax.fori_loop(0, tc, step, tuple(c0), unroll=4)
        for blk in range(S5_BLOCKS):
            carry[blk, :, 0:128] = cn[2 * blk]
            carry[blk, :, 128:256] = cn[2 * blk + 1]
        for blk in range(S5_BLOCKS):
            _stage(yrows, jnp.dot(bux[blk].astype(BF16), rc_ref[blk], preferred_element_type=F32))
            sl = slice(blk * 256, (blk + 1) * 256)
            y2 = _gather_rows(yrows, tc) + d_ref[:, sl] * u[:, sl]
            y2_ref[:, sl] = y2
            ge_ref[:, sl] = _gelu(y2).astype(BF16)

    row = pl.BlockSpec((tc, D_MODEL), lambda i: (i, 0))
    vec = pl.BlockSpec((1, D_MODEL), lambda i: (0, 0))
    mat = pl.BlockSpec((S5_BLOCKS, 256, 256), lambda i: (0, 0, 0))
    lamspec = pl.BlockSpec((S5_BLOCKS, 8, 128), lambda i: (0, 0, 0))
    return pl.pallas_call(
        body, grid=(nc,),
        in_specs=[row, vec, vec, pl.BlockSpec((8 * tc, tc), lambda i: (0, 0)), mat, mat, lamspec, lamspec],
        out_specs=[row, row, pl.BlockSpec((1, S5_BLOCKS, 8, 256), lambda i: (i, 0, 0, 0))],
        out_shape=[_sds((n_rows, D_MODEL), BF16), _sds((n_rows, D_MODEL), F32), _sds((nc, S5_BLOCKS, 8, 256), F32)],
        scratch_shapes=[pltpu.VMEM((S5_BLOCKS, 8 * tc, 256), F32), pltpu.VMEM((2, 8 * tc, 128), F32),
                        pltpu.VMEM((S5_BLOCKS, 8, 256), F32)],
        name="s5_fwd", compiler_params=_params(("arbitrary",)))(x, gain, d_skip, _expansion(tc), rb, rc, lam_r, lam_i)


def s5_bwd(x, gain, dy2, res, d_skip, cs, rb, rbt, rct, lam_r, lam_i):
    n_rows = x.shape[0]
    tc = min(S5_CHUNK, n_rows)
    nc = n_rows // tc

    def body(x_ref, g_ref, dy_ref, res_ref, d_ref, cs_ref, ex_ref, rb_ref, rbt_ref, rct_ref, lr_ref, li_ref,
             dx_ref, dd_ref, drb_ref, drc_ref, dlr_ref, dli_ref, dg_ref, tmp, du, lhsu, lhsd, xs, adj, acarry):
        i = pl.program_id(0)
        u = _rms_hat(x_ref[...])[0] * g_ref[...]

        @pl.when(i == 0)
        def _():
            acarry[...] = jnp.zeros_like(acarry)
            dd_ref[...] = jnp.zeros_like(dd_ref)
            drb_ref[...] = jnp.zeros_like(drb_ref)
            drc_ref[...] = jnp.zeros_like(drc_ref)
            dlr_ref[...] = jnp.zeros_like(dlr_ref)
            dli_ref[...] = jnp.zeros_like(dli_ref)
            dg_ref[...] = jnp.zeros_like(dg_ref)

        dd_ref[...] += jnp.sum(dy_ref[...] * u, axis=0, keepdims=True)
        mask = _row_mask(tc)
        for blk in range(S5_BLOCKS):
            sl = slice(blk * 256, (blk + 1) * 256)
            lhsu[blk] = _expand_rows(ex_ref, u[:, sl], mask)
            xs[blk] = jnp.dot(lhsu[blk], rb_ref[blk], preferred_element_type=F32)
            lhsd[blk] = _expand_rows(ex_ref, dy_ref[:, sl], mask)
            adj[blk] = jnp.dot(lhsd[blk], rct_ref[blk], preferred_element_type=F32)
        lam = [(lr_ref[blk], li_ref[blk]) for blk in range(S5_BLOCKS)]

        def fstep(t, c):
            r0 = pl.multiple_of(t * 8, 8)
            new = []
            for blk in range(S5_BLOCKS):
                xr, xi = c[2 * blk], c[2 * blk + 1]
                lr, li = lam[blk]
                nr = lr * xr - li * xi + xs[blk, pl.ds(r0, 8), 0:128]
                ni = lr * xi + li * xr + xs[blk, pl.ds(r0, 8), 128:256]
                xs[blk, pl.ds(r0, 8), 0:128] = nr
                xs[blk, pl.ds(r0, 8), 128:256] = ni
                new += [nr, ni]
            return tuple(new)

        c0 = []
        for blk in range(S5_BLOCKS):
            c0 += [cs_ref[0, blk, :, 0:128], cs_ref[0, blk, :, 128:256]]
        lax.fori_loop(0, tc, fstep, tuple(c0), unroll=4)

        def bstep(k, c):
            t = tc - 1 - k
            r0 = pl.multiple_of(t * 8, 8)
            rp = pl.multiple_of(jnp.maximum(t - 1, 0) * 8, 8)
            first = t == 0
            new_a, new_g = [], []
            for blk in range(S5_BLOCKS):
                ar, ai = c[0][2 * blk], c[0][2 * blk + 1]
                glr, gli = c[1][2 * blk], c[1][2 * blk + 1]
                lr, li = lam[blk]
                nr = lr * ar + li * ai + adj[blk, pl.ds(r0, 8), 0:128]
                ni = lr * ai - li * ar + adj[blk, pl.ds(r0, 8), 128:256]
                adj[blk, pl.ds(r0, 8), 0:128] = nr
                adj[blk, pl.ds(r0, 8), 128:256] = ni
                pr = jnp.where(first, cs_ref[0, blk, :, 0:128], xs[blk, pl.ds(rp, 8), 0:128])
                pi = jnp.where(first, cs_ref[0, blk, :, 128:256], xs[blk, pl.ds(rp, 8), 128:256])
                new_a += [nr, ni]
                new_g += [glr + nr * pr + ni * pi, gli + ni * pr - nr * pi]
            return tuple(new_a), tuple(new_g)

        a0, g0 = [], []
        for blk in range(S5_BLOCKS):
            a0 += [acarry[blk, :, 0:128], acarry[blk, :, 128:256]]
            g0 += [dlr_ref[blk], dli_ref[blk]]
        an, gn = lax.fori_loop(0, tc, bstep, (tuple(a0), tuple(g0)), unroll=2)
        for blk in range(S5_BLOCKS):
            acarry[blk, :, 0:128] = an[2 * blk]
            acarry[blk, :, 128:256] = an[2 * blk + 1]
            dlr_ref[blk] = gn[2 * blk]
            dli_ref[blk] = gn[2 * blk + 1]
        for blk in range(S5_BLOCKS):
            sl = slice(blk * 256, (blk + 1) * 256)
            ab = adj[blk].astype(BF16)
            _stage(tmp, jnp.dot(ab, rbt_ref[blk], preferred_element_type=F32))
            du[:, sl] = _gather_rows(tmp, tc) + d_ref[:, sl] * dy_ref[:, sl]
            drb_ref[blk] += lax.dot_general(lhsu[blk], ab, (((0,), (0,)), ((), ())), preferred_element_type=F32)
            drc_ref[blk] += lax.dot_general(lhsd[blk], xs[blk].astype(BF16), (((0,), (0,)), ((), ())),
                                            preferred_element_type=F32)
        xh, r = _rms_hat(x_ref[...])
        dg_ref[...] += jnp.sum(du[...] * xh, axis=0, keepdims=True)
        dxh = du[...] * g_ref[...]
        dx_ref[...] = r * (dxh - xh * jnp.mean(dxh * xh, axis=-1, keepdims=True)) + res_ref[...]

    rev = pl.BlockSpec((tc, D_MODEL), lambda i: (nc - 1 - i, 0))
    vec = pl.BlockSpec((1, D_MODEL), lambda i: (0, 0))
    mat = pl.BlockSpec((S5_BLOCKS, 256, 256), lambda i: (0, 0, 0))
    lamspec = pl.BlockSpec((S5_BLOCKS, 8, 128), lambda i: (0, 0, 0))
    big = pltpu.VMEM((S5_BLOCKS, 8 * tc, 256), F32)
    bigb = pltpu.VMEM((S5_BLOCKS, 8 * tc, 256), BF16)
    return pl.pallas_call(
        body, grid=(nc,),
        in_specs=[rev, vec, rev, rev, vec, pl.BlockSpec((1, S5_BLOCKS, 8, 256), lambda i: (nc - 1 - i, 0, 0, 0)),
                  pl.BlockSpec((8 * tc, tc), lambda i: (0, 0)), mat, mat, mat, lamspec, lamspec],
        out_specs=[rev, vec, mat, mat, lamspec, lamspec, vec],
        out_shape=[_sds((n_rows, D_MODEL), F32), _sds((1, D_MODEL), F32), _sds((S5_BLOCKS, 256, 256), F32),
                   _sds((S5_BLOCKS, 256, 256), F32), _sds((S5_BLOCKS, 8, 128), F32), _sds((S5_BLOCKS, 8, 128), F32),
                   _sds((1, D_MODEL), F32)],
        scratch_shapes=[pltpu.VMEM((2, 8 * tc, 128), F32), pltpu.VMEM((tc, D_MODEL), F32), bigb, bigb, big, big,
                        pltpu.VMEM((S5_BLOCKS, 8, 256), F32)],
        name="s5_bwd", compiler_params=_params(("arbitrary",)))(
            x, gain, dy2, res, d_skip, cs, _expansion(tc), rb, rbt, rct, lam_r, lam_i)


def _s5_discretise(a_re, a_im, log_dt, b_re, b_im):
    lam = lax.complex(jnp.minimum(a_re, LAMBDA_RE_MAX), a_im)
    dt = jnp.exp(log_dt)[:, None]
    lam_bar = jnp.exp(lam * dt)
    b_bar = ((lam_bar - 1.0) / lam)[:, :, None] * lax.complex(b_re, b_im)
    return jnp.real(lam_bar), jnp.imag(lam_bar), jnp.real(b_bar), jnp.imag(b_bar)


def _s5_matrices(bbar_re, bbar_im, c_re, c_im):
    eye2 = jnp.eye(2, dtype=F32)
    bst = jnp.stack([bbar_re, bbar_im]).reshape(2, S5_BLOCKS, 8, 2, S5_STATE, S5_GROUP)
    bt = jnp.transpose(bst, (1, 2, 3, 5, 0, 4))
    rb = (bt[:, :, :, :, :, None, :] * eye2[None, None, :, None, None, :, None]).reshape(S5_BLOCKS, 256, 256)
    cst = jnp.stack([c_re, -c_im]).reshape(2, S5_BLOCKS, 8, 2, S5_GROUP, S5_STATE)
    ct = jnp.transpose(cst, (1, 0, 5, 2, 3, 4))
    rc = (ct[:, :, None, :, :, :, :] * eye2[None, None, :, None, None, :, None]).reshape(S5_BLOCKS, 256, 256)
    return rb, rc


def s5_compact(drb, drct, dlr, dli):
    def body(drb_ref, drct_ref, dlr_ref, dli_ref, o_ref):
        even = (lax.broadcasted_iota(jnp.int32, (256, 64), 0) // S5_GROUP) % 2 == 0
        for blk in range(S5_BLOCKS):
            for k, ref in enumerate((drb_ref, drct_ref)):
                m = ref[blk]
                re = jnp.where(even, m[:, 0:64], m[:, 64:128])
                im = jnp.where(even, m[:, 128:192], m[:, 192:256])
                o_ref[pl.ds(k * 1024 + blk * 256, 256), :] = jnp.concatenate([re, im], axis=1)
            o_ref[pl.ds(2048 + blk * 8, 8), :] = dlr_ref[blk]
            o_ref[pl.ds(2080 + blk * 8, 8), :] = dli_ref[blk]

    vm = pl.BlockSpec(memory_space=pltpu.VMEM)
    return pl.pallas_call(body, in_specs=[vm] * 4, out_specs=vm, out_shape=_sds((2112, 128), F32), name="s5_compact",
                          compiler_params=_params())(drb, drct, dlr, dli)


def _s5_unpack(mats):
    bm = mats[0:1024].reshape(S5_GROUPS, S5_GROUP, 128)
    cm = mats[1024:2048].reshape(S5_GROUPS, S5_GROUP, 128)
    swap = lambda t: jnp.transpose(t, (0, 2, 1))
    return (swap(bm[:, :, 0:64]), swap(bm[:, :, 64:128]), cm[:, :, 0:64], -cm[:, :, 64:128],
            mats[2048:2080].reshape(S5_GROUPS, S5_STATE), mats[2080:2112].reshape(S5_GROUPS, S5_STATE))


NEG = -1e30


GROUP = N_Q // N_KV


def _attn_masks(n):
    qi = lax.broadcasted_iota(jnp.int32, (GROUP * BLOCK, BLOCK), 0) % BLOCK
    kj = lax.broadcasted_iota(jnp.int32, (GROUP * BLOCK, BLOCK), 1)
    return jnp.logical_and(kj > qi, n > 0), kj <= qi


def _stack_heads(ref, kh):
    return jnp.concatenate([ref[:, (GROUP * kh + g) * HEAD_DIM:(GROUP * kh + g + 1) * HEAD_DIM] for g in range(GROUP)], axis=0)


def _unstack_heads(val):
    return jnp.concatenate([val[g * BLOCK:(g + 1) * BLOCK] for g in range(GROUP)], axis=1)


def _sink_column(sink_ref, kh):
    grp = lax.broadcasted_iota(jnp.int32, (GROUP * BLOCK, 1), 0) // BLOCK
    col = jnp.zeros((GROUP * BLOCK, 1), F32)
    for g in range(GROUP):
        col = jnp.where(grp == g, sink_ref[GROUP * kh + g], col)
    return col, grp


def _attn_exp(q4, kp, kc, sink, mask_p, mask_c):
    scale = 1.0 / math.sqrt(HEAD_DIM)
    nt = (((1,), (1,)), ((), ()))
    sp = jnp.where(mask_p, lax.dot_general(q4, kp, nt, preferred_element_type=F32) * scale, NEG)
    sc = jnp.where(mask_c, lax.dot_general(q4, kc, nt, preferred_element_type=F32) * scale, NEG)
    m = jnp.maximum(jnp.maximum(jnp.max(sp, axis=-1, keepdims=True), jnp.max(sc, axis=-1, keepdims=True)), sink)
    pp = jnp.exp(sp - m)
    pc = jnp.exp(sc - m)
    ps = jnp.exp(sink - m)
    inv = 1.0 / (jnp.sum(pp, axis=-1, keepdims=True) + jnp.sum(pc, axis=-1, keepdims=True) + ps)
    return pp, pc, ps, inv


def attn_fwd(q, kv, sinks):
    n_rows = q.shape[0]
    nb = n_rows // BLOCK

    def body(sink_ref, q_ref, kvp_ref, kvc_ref, o_ref):
        n = pl.program_id(0)
        mask_p, mask_c = _attn_masks(n)
        outs = []
        for kh in range(N_KV):
            ks, vs = slice(kh * HEAD_DIM, (kh + 1) * HEAD_DIM), slice((N_KV + kh) * HEAD_DIM, (N_KV + kh + 1) * HEAD_DIM)
            sink, _ = _sink_column(sink_ref, kh)
            pp, pc, _, inv = _attn_exp(_stack_heads(q_ref, kh), kvp_ref[:, ks], kvc_ref[:, ks], sink, mask_p, mask_c)
            o4 = (jnp.dot(pp.astype(BF16), kvp_ref[:, vs], preferred_element_type=F32)
                  + jnp.dot(pc.astype(BF16), kvc_ref[:, vs], preferred_element_type=F32)) * inv
            outs.append(_unstack_heads(o4))
        o_ref[...] = jnp.concatenate(outs, axis=1).astype(BF16)

    kvw = 2 * N_KV * HEAD_DIM
    return pl.pallas_call(
        body, grid=(nb,),
        in_specs=[pl.BlockSpec(memory_space=pltpu.SMEM), pl.BlockSpec((BLOCK, D_MODEL), lambda n: (n, 0)),
                  pl.BlockSpec((BLOCK, kvw), lambda n: (jnp.maximum(n - 1, 0), 0)), pl.BlockSpec((BLOCK, kvw), lambda n: (n, 0))],
        out_specs=pl.BlockSpec((BLOCK, D_MODEL), lambda n: (n, 0)), out_shape=_sds((n_rows, D_MODEL), BF16),
        name="attn_fwd", compiler_params=_params(("parallel",)))(sinks, q, kv, kv)


def attn_bwd(q, kv, do, sinks):
    n_rows = q.shape[0]
    nb = n_rows // BLOCK
    kvw = 2 * N_KV * HEAD_DIM
    tn = (((0,), (0,)), ((), ()))
    nt = (((1,), (1,)), ((), ()))
    scale = 1.0 / math.sqrt(HEAD_DIM)

    def body(sink_ref, q_ref, kvp_ref, kvc_ref, do_ref, dq_ref, dbq_ref, dprev_ref, dcur_ref, dsink_ref):
        n = pl.program_id(0)
        mask_p, mask_c = _attn_masks(n)
        lane = lax.broadcasted_iota(jnp.int32, (1, D_MODEL), 1)
        dqs, dsink = [], jnp.zeros((1, D_MODEL), F32)
        dkp, dkc, dvp, dvc = [], [], [], []
        for kh in range(N_KV):
            ks, vs = slice(kh * HEAD_DIM, (kh + 1) * HEAD_DIM), slice((N_KV + kh) * HEAD_DIM, (N_KV + kh + 1) * HEAD_DIM)
            q4, do4 = _stack_heads(q_ref, kh), _stack_heads(do_ref, kh)
            kp, kc, vp, vc = kvp_ref[:, ks], kvc_ref[:, ks], kvp_ref[:, vs], kvc_ref[:, vs]
            sink, grp = _sink_column(sink_ref, kh)
            pp, pc, ps, inv = _attn_exp(q4, kp, kc, sink, mask_p, mask_c)
            pp, pc = pp * inv, pc * inv
            dpp = lax.dot_general(do4, vp, nt, preferred_element_type=F32)
            dpc = lax.dot_general(do4, vc, nt, preferred_element_type=F32)
            delta = jnp.sum(pp * dpp, axis=-1, keepdims=True) + jnp.sum(pc * dpc, axis=-1, keepdims=True)
            dsp = (pp * (dpp - delta) * scale).astype(BF16)
            dsc = (pc * (dpc - delta) * scale).astype(BF16)
            dsk = ps * inv * delta
            for g in range(GROUP):
                dsink = dsink + jnp.where(lane == GROUP * kh + g, -jnp.sum(jnp.where(grp == g, dsk, 0.0)), 0.0)
            dqs.append(_unstack_heads(jnp.dot(dsp, kp, preferred_element_type=F32)
                                      + jnp.dot(dsc, kc, preferred_element_type=F32)))
            dkp.append(lax.dot_general(dsp, q4, tn, preferred_element_type=F32))
            dkc.append(lax.dot_general(dsc, q4, tn, preferred_element_type=F32))
            dvp.append(lax.dot_general(pp.astype(BF16), do4, tn, preferred_element_type=F32))
            dvc.append(lax.dot_general(pc.astype(BF16), do4, tn, preferred_element_type=F32))
        dq = jnp.concatenate(dqs, axis=1)
        dq_ref[...] = dq.astype(BF16)
        dprev_ref[0] = jnp.concatenate(dkp + dvp, axis=1)
        dcur_ref[0] = jnp.concatenate(dkc + dvc, axis=1)

        @pl.when(n == 0)
        def _():
            dbq_ref[...] = jnp.zeros_like(dbq_ref)
            dsink_ref[...] = jnp.zeros_like(dsink_ref)

        dbq_ref[...] += jnp.sum(dq, axis=0, keepdims=True)
        dsink_ref[...] += dsink

    blk = pl.BlockSpec((BLOCK, D_MODEL), lambda n: (n, 0))
    part = pl.BlockSpec((1, BLOCK, kvw), lambda n: (n, 0, 0))
    return pl.pallas_call(
        body, grid=(nb,),
        in_specs=[pl.BlockSpec(memory_space=pltpu.SMEM), blk,
                  pl.BlockSpec((BLOCK, kvw), lambda n: (jnp.maximum(n - 1, 0), 0)), pl.BlockSpec((BLOCK, kvw), lambda n: (n, 0)), blk],
        out_specs=[blk, pl.BlockSpec((1, D_MODEL), lambda n: (0, 0)), part, part, pl.BlockSpec((1, D_MODEL), lambda n: (0, 0))],
        out_shape=[_sds((n_rows, D_MODEL), BF16), _sds((1, D_MODEL), F32), _sds((nb, BLOCK, kvw), F32),
                   _sds((nb, BLOCK, kvw), F32), _sds((1, D_MODEL), F32)],
        name="attn_bwd", compiler_params=_params(("arbitrary",)))(sinks, q, kv, kv, do)


def kv_combine(dprev, dcur):
    nb, _, kvw = dprev.shape

    def body(dcur_ref, dprev_ref, dkv_ref, db_ref):
        total = jnp.zeros((1, kvw), F32)
        for m in range(nb):
            dkv = dcur_ref[m] + dprev_ref[m + 1] if m + 1 < nb else dcur_ref[m]
            dkv_ref[m * BLOCK:(m + 1) * BLOCK, :] = dkv.astype(BF16)
            total = total + jnp.sum(dkv, axis=0, keepdims=True)
        db_ref[...] = jnp.concatenate([total, jnp.zeros((1, D_MODEL - kvw), F32)], axis=1)

    vm = pl.BlockSpec(memory_space=pltpu.VMEM)
    return pl.pallas_call(body, in_specs=[vm, vm], out_specs=[vm, vm],
                          out_shape=[_sds((nb * BLOCK, kvw), BF16), _sds((1, D_MODEL), F32)], name="kv_combine",
                          compiler_params=_params())(dcur, dprev)


def glu_bwd(dout, val, gate, tm=256):
    n_rows, d = dout.shape

    def body(do_ref, v_ref, g_ref, dz_ref, db_ref):
        i = pl.program_id(0)
        sg = jax.nn.sigmoid(g_ref[...])
        dval = do_ref[...] * sg
        dgate = do_ref[...] * v_ref[...] * sg * (1.0 - sg)
        dz_ref[...] = jnp.concatenate([dval, dgate], axis=1).astype(BF16)

        @pl.when(i == 0)
        def _():
            db_ref[...] = jnp.zeros_like(db_ref)

        db_ref[0:1, :] += jnp.sum(dval, axis=0, keepdims=True)
        db_ref[1:2, :] += jnp.sum(dgate, axis=0, keepdims=True)

    row = pl.BlockSpec((tm, d), lambda i: (i, 0))
    return pl.pallas_call(
        body, grid=(n_rows // tm,), in_specs=[row, row, row],
        out_specs=[pl.BlockSpec((tm, 2 * d), lambda i: (i, 0)), pl.BlockSpec((2, d), lambda i: (0, 0))],
        out_shape=[_sds((n_rows, 2 * d), BF16), _sds((2, d), F32)],
        name="glu_bwd", compiler_params=_params(("arbitrary",)))(dout, val, gate)


def final_loss(h, target, gain, tm=256):
    n_rows, d = h.shape

    def body(h_ref, t_ref, g_ref, loss_ref, dh_ref, dhb_ref, dg_ref):
        i = pl.program_id(0)
        xh, r = _rms_hat(h_ref[...])
        err = xh * g_ref[...] - t_ref[...]
        dy = err * (1.0 / d)
        dxh = dy * g_ref[...]
        dx = r * (dxh - xh * jnp.mean(dxh * xh, axis=-1, keepdims=True))
        dh_ref[...] = dx
        dhb_ref[...] = dx.astype(BF16)

        @pl.when(i == 0)
        def _():
            loss_ref[...] = jnp.zeros_like(loss_ref)
            dg_ref[...] = jnp.zeros_like(dg_ref)

        loss_ref[...] += jnp.full((1, d), 0.5 * jnp.sum(jnp.mean(err * err, axis=-1, keepdims=True)), F32)
        dg_ref[...] += jnp.sum(dy * xh, axis=0, keepdims=True)

    row = pl.BlockSpec((tm, d), lambda i: (i, 0))
    vec = pl.BlockSpec((1, d), lambda i: (0, 0))
    return pl.pallas_call(
        body, grid=(n_rows // tm,), in_specs=[row, row, vec],
        out_specs=[vec, row, row, vec],
        out_shape=[_sds((1, d), F32), _sds((n_rows, d), F32), _sds((n_rows, d), BF16), _sds((1, d), F32)],
        name="final_loss", compiler_params=_params(("arbitrary",)))(h, target, gain)


def _adam_update(w, g, m, v):
    nm = ADAM_B1 * m + (1.0 - ADAM_B1) * g
    nv = ADAM_B2 * v + (1.0 - ADAM_B2) * (g * g)
    m_hat = nm / (1.0 - ADAM_B1 ** ADAM_STEP)
    v_hat = nv / (1.0 - ADAM_B2 ** ADAM_STEP)
    return -ADAM_LR * (m_hat / (jnp.sqrt(v_hat) + ADAM_EPS) + ADAM_WD * w), nm, nv


def adamw(name, w, g, m, v, tm=256):
    n_rows, d = w.shape
    tm = tm if n_rows % tm == 0 else n_rows

    def body(w_ref, g_ref, m_ref, v_ref, go_ref, d_ref, nm_ref, nv_ref):
        gv = g_ref[...]
        go_ref[...] = gv
        d_ref[...], nm_ref[...], nv_ref[...] = _adam_update(w_ref[...], gv, m_ref[...], v_ref[...])

    row = pl.BlockSpec((tm, d), lambda i: (i, 0))
    return pl.pallas_call(
        body, grid=(n_rows // tm,), in_specs=[row] * 4, out_specs=[row] * 4,
        out_shape=[_sds((n_rows, d), F32)] * 4, name=name, compiler_params=_params(("parallel",)))(w, g, m, v)


def adamw_native(name, ws, gs, ms, vs):
    n = len(ws)

    def body(*refs):
        w_refs, g_refs, m_refs, v_refs = refs[:n], refs[n:2 * n], refs[2 * n:3 * n], refs[3 * n:4 * n]
        d_refs, nm_refs, nv_refs = refs[4 * n:5 * n], refs[5 * n:6 * n], refs[6 * n:7 * n]
        for k in range(n):
            dl, nm, nv = _adam_update(w_refs[k][...], g_refs[k][...], m_refs[k][...], v_refs[k][...])
            d_refs[k][...] = dl
            nm_refs[k][...] = nm
            nv_refs[k][...] = nv

    vm = pl.BlockSpec(memory_space=pltpu.VMEM)
    shapes = [_sds(w.shape, F32) for w in ws]
    out = pl.pallas_call(body, in_specs=[vm] * (4 * n), out_specs=[vm] * (3 * n), out_shape=shapes * 3, name=name,
                         compiler_params=_params())(*ws, *gs, *ms, *vs)
    return list(out[:n]), list(out[n:2 * n]), list(out[2 * n:])


VEC_ROWS = {"norm_mix": 0, "norm_mlp": 2, "norm_kv": 4, "norm_final": 5, "s5_d": 6, "b_q": 7, "b_o": 8, "s5_b_glu": 9,
            "b_kv": 11, "sinks": 12, "loss": 13}


def split_vectors(where, vecs, d_shard, glu_shard):
    kvw = 2 * N_KV * HEAD_DIM
    shapes = {"norm_mix": (2, D_MODEL), "norm_mlp": (2, D_MODEL), "norm_kv": (1, D_MODEL), "norm_final": (1, D_MODEL),
              "s5_d": (1, d_shard), "b_q": (1, D_MODEL), "b_o": (1, D_MODEL), "s5_b_glu": (1, glu_shard), "b_kv": (1, kvw),
              "sinks": (1, N_Q), "loss": (1, 128)}
    names = list(shapes)

    def body(where_ref, v_ref, *o_refs):
        chip = where_ref[1]
        for name, o_ref in zip(names, o_refs):
            r0, (r, n) = VEC_ROWS[name], shapes[name]
            if name == "s5_d":
                g = jnp.zeros((1, n), F32)
                for j in range(4):
                    g = jnp.where(chip == j, v_ref[r0:r0 + 1, j * n:(j + 1) * n], g)
            elif name == "s5_b_glu":
                g = jnp.zeros((1, n), F32)
                for j in range(4):
                    row, col = r0 + (j * n) // D_MODEL, (j * n) % D_MODEL
                    g = jnp.where(chip == j, v_ref[row:row + 1, col:col + n], g)
            else:
                g = v_ref[r0:r0 + r, 0:n]
            o_ref[...] = g

    vm = pl.BlockSpec(memory_space=pltpu.VMEM)
    out = pl.pallas_call(body, in_specs=[pl.BlockSpec(memory_space=pltpu.SMEM), vm], out_specs=[vm] * len(names),
                         out_shape=[_sds(shapes[n], F32) for n in names], name="split_vectors",
                         compiler_params=_params())(where, vecs)
    return dict(zip(names, out))


def _position():
    x, y, c = lax.axis_index("x"), lax.axis_index("y"), lax.axis_index("c")
    others = [(1 - x, y), (x, 1 - y), (1 - x, 1 - y)]
    return x, y, c, others


def _window(ref, kind, chip, half, shard_shape):
    r, n = shard_shape
    if kind == "col":
        return ref.at[pl.ds(pl.multiple_of(half * (r // 2), 16), r // 2), pl.ds(pl.multiple_of(chip * n, 128), n)]
    return ref.at[pl.ds(pl.multiple_of(chip * r, 16), r), pl.ds(pl.multiple_of(half * (n // 2), 128), n // 2)]


def _half(ref, kind, half, shape):
    r, n = shape
    if kind == "col":
        return ref.at[pl.ds(pl.multiple_of(half * (r // 2), 16), r // 2), :]
    return ref.at[:, pl.ds(pl.multiple_of(half * (n // 2), 128), n // 2)]


def swap_halves(name, grads, kinds):
    nt = len(grads)
    shapes = [tuple(g.shape) for g in grads]

    def body(*refs):
        in_refs, out_refs = refs[:nt], refs[nt:2 * nt]
        send_sems, recv_sems = refs[2 * nt:]
        x, y, c, _ = _position()
        cps = []
        for t in range(nt):
            cp = pltpu.make_async_remote_copy(
                src_ref=_half(in_refs[t], kinds[t], 1 - c, shapes[t]), dst_ref=_half(out_refs[t], kinds[t], 1 - c, shapes[t]),
                send_sem=send_sems.at[t], recv_sem=recv_sems.at[t], device_id=(x, y, 1 - c), device_id_type=MESH)
            cp.start()
            cps.append(cp)
        for t in range(nt):
            mine = _half(out_refs[t], kinds[t], c, shapes[t])
            pltpu.make_async_remote_copy(
                src_ref=mine, dst_ref=mine, send_sem=send_sems.at[t], recv_sem=recv_sems.at[t],
                device_id=(x, y, 1 - c), device_id_type=MESH).wait_recv()
        for cp in cps:
            cp.wait_send()

    hbm = pl.BlockSpec(memory_space=pl.ANY)
    return pl.pallas_call(
        body, in_specs=[hbm] * nt, out_specs=[hbm] * nt, out_shape=[_sds(s, BF16) for s in shapes],
        scratch_shapes=[pltpu.SemaphoreType.DMA((nt,)), pltpu.SemaphoreType.DMA((nt,))],
        name=name, compiler_params=_params())(*grads)


def _half_spec(kind, shape, tiles):
    r, n = shape
    if kind == "col":
        tn = n // tiles
        return pl.BlockSpec((r // 2, tn), lambda i, s: (s[0], i))
    tm = r // tiles
    return pl.BlockSpec((tm, n // 2), lambda i, s: (i, s[0]))


def add_halves(name, mine, landed, kinds, where, tiles=4):
    nt = len(mine)
    shapes = [tuple(a.shape) for a in mine]

    def compact(t):
        r, n = shapes[t]
        if kinds[t] == "col":
            return (r // 2, n), pl.BlockSpec((r // 2, n // tiles), lambda i, s: (0, i))
        return (r, n // 2), pl.BlockSpec((r // tiles, n // 2), lambda i, s: (i, 0))

    def body(s_ref, *refs):
        for a_ref, b_ref, o_ref in zip(refs[:nt], refs[nt:2 * nt], refs[2 * nt:]):
            o_ref[...] = (a_ref[...].astype(F32) + b_ref[...].astype(F32)).astype(BF16)

    specs = [_half_spec(kinds[t], shapes[t], tiles) for t in range(nt)]
    return pl.pallas_call(
        body, grid_spec=pltpu.PrefetchScalarGridSpec(num_scalar_prefetch=1, grid=(tiles,), in_specs=specs + specs,
                                                     out_specs=[compact(t)[1] for t in range(nt)]),
        out_shape=[_sds(compact(t)[0], BF16) for t in range(nt)], name=name,
        compiler_params=_params(("parallel",)))(where, *mine, *landed)


def sum_shards(name, parts, landed, kinds, shard_shapes, where, layers, n_layers, intos, tiles=2):
    nt = len(parts)
    in_specs, out_specs = [], []
    for t in range(nt):
        (r, n), layer = shard_shapes[t], layers[t]
        if kinds[t] == "col":
            tm, width = r // 2 // tiles, n
            own = pl.BlockSpec((tm, n), lambda i, s: (i, s[1]))
            out = pl.BlockSpec((None, tm, n), lambda i, s, layer=layer: (layer, s[0] * tiles + i, 0))
        else:
            tm, width = r // tiles, n // 2
            own = pl.BlockSpec((tm, n // 2), lambda i, s: (s[1] * tiles + i, 0))
            out = pl.BlockSpec((None, tm, n // 2), lambda i, s, layer=layer: (layer, i, s[0]))
        in_specs += [own, pl.BlockSpec((3, tm, width), lambda i, s: (0, i, 0))]
        out_specs.append(out)
    args, aliases = [where] + [a for pair in zip(parts, landed) for a in pair], {}
    for t in range(nt):
        if intos[t] is not None:
            aliases[len(args)] = t
            in_specs.append(pl.BlockSpec(memory_space=pl.ANY))
            args.append(intos[t])

    def body(s_ref, *refs):
        for t in range(nt):
            a_ref, l_ref, o_ref = refs[2 * t], refs[2 * t + 1], refs[len(in_specs) + t]
            o_ref[...] = ((a_ref[...].astype(F32) + l_ref[0].astype(F32)) + l_ref[1].astype(F32)) + l_ref[2].astype(F32)

    return pl.pallas_call(
        body, grid_spec=pltpu.PrefetchScalarGridSpec(num_scalar_prefetch=1, grid=(tiles,), in_specs=in_specs,
                                                     out_specs=out_specs),
        out_shape=[_sds((n_layers[t],) + tuple(shard_shapes[t]), F32) for t in range(nt)], input_output_aliases=aliases,
        name=name, compiler_params=_params(("parallel",)))(*args)


def share_halves(arrays, entries):
    na, nt = len(arrays), len(entries)

    def body(*refs):
        out_refs = refs[na:2 * na]
        send_sems, recv_sems = refs[2 * na:]
        x, y, c, _ = _position()
        cps = []
        for t, (a, layer, kind) in enumerate(entries):
            shape = tuple(arrays[a].shape[1:])
            mine = _half(out_refs[a].at[layer], kind, c, shape)
            cp = pltpu.make_async_remote_copy(
                src_ref=mine, dst_ref=mine, send_sem=send_sems.at[t], recv_sem=recv_sems.at[t],
                device_id=(x, y, 1 - c), device_id_type=MESH)
            cp.start()
            cps.append(cp)
        for t, (a, layer, kind) in enumerate(entries):
            shape = tuple(arrays[a].shape[1:])
            other = _half(out_refs[a].at[layer], kind, 1 - c, shape)
            pltpu.make_async_remote_copy(
                src_ref=other, dst_ref=other, send_sem=send_sems.at[t], recv_sem=recv_sems.at[t],
                device_id=(x, y, 1 - c), device_id_type=MESH).wait_recv()
        for cp in cps:
            cp.wait_send()

    hbm = pl.BlockSpec(memory_space=pl.ANY)
    return pl.pallas_call(
        body, in_specs=[hbm] * na, out_specs=[hbm] * na, out_shape=[_sds(a.shape, F32) for a in arrays],
        input_output_aliases={i: i for i in range(na)},
        scratch_shapes=[pltpu.SemaphoreType.DMA((nt,)), pltpu.SemaphoreType.DMA((nt,))],
        name="share_halves", compiler_params=_params())(*arrays)


HBM_SPEC = pl.BlockSpec(memory_space=pltpu.HBM)
SEM_SPEC = pl.BlockSpec(memory_space=pltpu.SEMAPHORE)
ANY_SPEC = pl.BlockSpec(memory_space=pl.ANY)


def _split_params():
    return pltpu.CompilerParams(has_side_effects=pltpu.SideEffectType.DATAFLOW_SIDE_EFFECTING,
                                vmem_limit_bytes=VMEM_LIMIT_BYTES)


def _in_hbm(a):
    return pltpu.with_memory_space_constraint(a, pltpu.HBM)


def cast_place(arrays, entries, where, tiles=2):
    in_specs, out_specs, fulls = [], [], []
    for a, layer, kind in entries:
        _, r, n = arrays[a].shape
        tm = r // tiles
        in_specs.append(pl.BlockSpec((None, tm, n), lambda i, s, layer=layer: (layer, i, 0)))
        if kind == "col":
            fulls.append((r, 4 * n))
            out_specs.append(pl.BlockSpec((tm, n), lambda i, s: (i, s[1])))
        else:
            fulls.append((4 * r, n))
            out_specs.append(pl.BlockSpec((tm, n), lambda i, s: (s[1] * tiles + i, 0)))
    nt = len(entries)

    def body(s_ref, *refs):
        for w_ref, o_ref in zip(refs[:nt], refs[nt:]):
            o_ref[...] = w_ref[...].astype(BF16)

    return pl.pallas_call(
        body, grid_spec=pltpu.PrefetchScalarGridSpec(num_scalar_prefetch=1, grid=(tiles,), in_specs=in_specs,
                                                     out_specs=out_specs),
        out_shape=[_sds(f, BF16) for f in fulls], name="cast_place",
        compiler_params=_params(("parallel",)))(where, *[arrays[a] for a, _, _ in entries])


def gather_start(name, fulls, kinds, shard_shapes, after):
    nt = len(fulls)
    na = 0 if after is None else 1

    def body(*refs):
        full_refs = refs[:nt]
        send_sems, recv_sems, token = refs[nt + na], refs[nt + na + 1], refs[-1]
        x, y, c, others = _position()
        for t in range(nt):
            mine = _window(full_refs[t], kinds[t], 2 * x + y, c, shard_shapes[t])
            for j, (ox, oy) in enumerate(others):
                pltpu.make_async_remote_copy(
                    src_ref=mine, dst_ref=mine, send_sem=send_sems.at[3 * t + j], recv_sem=recv_sems.at[3 * t + j],
                    device_id=(ox, oy, c), device_id_type=MESH).start()
        token[...] = jnp.zeros_like(token)

    sems = pltpu.SemaphoreType.DMA((3 * nt,))
    out = pl.pallas_call(
        body, name=name, in_specs=[HBM_SPEC] * nt + [ANY_SPEC] * na,
        out_specs=(SEM_SPEC, SEM_SPEC, *[HBM_SPEC] * nt, pl.BlockSpec(memory_space=pltpu.VMEM)),
        out_shape=(sems, sems, *[pltpu.HBM(f.shape, f.dtype) for f in fulls], _sds((8, 128), F32)),
        input_output_aliases={t: 2 + t for t in range(nt)}, compiler_params=_split_params(),
    )(*[_in_hbm(f) for f in fulls], *([] if after is None else [after]))
    return out[0], out[1], list(out[2:2 + nt]), out[-1]


def gather_wait(name, send_sems, recv_sems, fulls, kinds, shard_shapes, after):
    nt = len(fulls)

    def body(*refs):
        full_refs, send_ref, recv_ref = refs[:nt], refs[nt], refs[nt + 1]
        x, y, c, others = _position()
        for t in range(nt):
            mine = _window(full_refs[t], kinds[t], 2 * x + y, c, shard_shapes[t])
            for j, (ox, oy) in enumerate(others):
                cp = pltpu.make_async_remote_copy(
                    src_ref=mine, dst_ref=_window(full_refs[t], kinds[t], 2 * ox + oy, c, shard_shapes[t]),
                    send_sem=send_ref.at[3 * t + j], recv_sem=recv_ref.at[3 * t + j],
                    device_id=(ox, oy, c), device_id_type=MESH)
                cp.wait_send()
                cp.wait_recv()

    out = pl.pallas_call(
        body, name=name, in_specs=[HBM_SPEC] * nt + [SEM_SPEC, SEM_SPEC, HBM_SPEC], out_specs=[HBM_SPEC] * nt,
        out_shape=[pltpu.HBM(f.shape, f.dtype) for f in fulls], input_output_aliases={t: t for t in range(nt)},
        compiler_params=_split_params())(*fulls, send_sems, recv_sems, _in_hbm(after))
    return list(out)


def forward_halves(name, fulls, kinds, shard_shapes):
    nt = len(fulls)

    def body(*refs):
        out_refs = refs[nt:2 * nt]
        send_sems, recv_sems = refs[2 * nt:]
        x, y, c, others = _position()
        cps = []
        for t in range(nt):
            for j, (ox, oy) in enumerate(others):
                landed = _window(out_refs[t], kinds[t], 2 * ox + oy, c, shard_shapes[t])
                cp = pltpu.make_async_remote_copy(
                    src_ref=landed, dst_ref=landed, send_sem=send_sems.at[3 * t + j], recv_sem=recv_sems.at[3 * t + j],
                    device_id=(x, y, 1 - c), device_id_type=MESH)
                cp.start()
                cps.append(cp)
        for t in range(nt):
            for j, (ox, oy) in enumerate(others):
                got = _window(out_refs[t], kinds[t], 2 * ox + oy, 1 - c, shard_shapes[t])
                pltpu.make_async_remote_copy(
                    src_ref=got, dst_ref=got, send_sem=send_sems.at[3 * t + j], recv_sem=recv_sems.at[3 * t + j],
                    device_id=(x, y, 1 - c), device_id_type=MESH).wait_recv()
        for cp in cps:
            cp.wait_send()

    out = pl.pallas_call(
        body, in_specs=[ANY_SPEC] * nt, out_specs=[ANY_SPEC] * nt, out_shape=[_sds(f.shape, f.dtype) for f in fulls],
        input_output_aliases={t: t for t in range(nt)},
        scratch_shapes=[pltpu.SemaphoreType.DMA((3 * nt,)), pltpu.SemaphoreType.DMA((3 * nt,))],
        name=name, compiler_params=_params())(*fulls)
    return list(out)


def _piece(ref, kind, chip, shard_shape):
    r, n = shard_shape
    if kind == "col":
        return ref.at[:, pl.ds(pl.multiple_of(chip * n, 128), n)]
    return ref.at[pl.ds(pl.multiple_of(chip * r, 16), r), :]


def _piece_shape(kind, shard_shape):
    r, n = shard_shape
    return (r // 2, n) if kind == "col" else (r, n // 2)


def exchange_start(name, parts, kinds, shard_shapes):
    nt = len(parts)
    lands = [lax.empty((3,) + _piece_shape(kinds[t], shard_shapes[t]), BF16) for t in range(nt)]

    def body(*refs):
        part_refs, land_refs = refs[:nt], refs[nt:2 * nt]
        send_sems, recv_sems, token = refs[2 * nt], refs[2 * nt + 1], refs[-1]
        x, y, c, others = _position()
        for t in range(nt):
            for j, (ox, oy) in enumerate(others):
                pltpu.make_async_remote_copy(
                    src_ref=_piece(part_refs[t], kinds[t], 2 * ox + oy, shard_shapes[t]), dst_ref=land_refs[t].at[j],
                    send_sem=send_sems.at[3 * t + j], recv_sem=recv_sems.at[3 * t + j],
                    device_id=(ox, oy, c), device_id_type=MESH).start()
        token[...] = jnp.zeros_like(token)

    sems = pltpu.SemaphoreType.DMA((3 * nt,))
    both = list(parts) + lands
    out = pl.pallas_call(
        body, name=name, in_specs=[HBM_SPEC] * (2 * nt),
        out_specs=(SEM_SPEC, SEM_SPEC, *[HBM_SPEC] * (2 * nt), pl.BlockSpec(memory_space=pltpu.VMEM)),
        out_shape=(sems, sems, *[pltpu.HBM(a.shape, a.dtype) for a in both], _sds((8, 128), F32)),
        input_output_aliases={t: 2 + t for t in range(2 * nt)}, compiler_params=_split_params(),
    )(*[_in_hbm(a) for a in both])
    return out[0], out[1], list(out[2:2 + nt]), list(out[2 + nt:2 + 2 * nt]), out[-1]


def exchange_wait(name, send_sems, recv_sems, parts, lands, kinds, shard_shapes, after):
    nt = len(parts)

    def body(*refs):
        part_refs, land_refs = refs[:nt], refs[nt:2 * nt]
        send_ref, recv_ref = refs[2 * nt], refs[2 * nt + 1]
        x, y, c, others = _position()
        for t in range(nt):
            for j, (ox, oy) in enumerate(others):
                cp = pltpu.make_async_remote_copy(
                    src_ref=_piece(part_refs[t], kinds[t], 2 * ox + oy, shard_shapes[t]), dst_ref=land_refs[t].at[j],
                    send_sem=send_ref.at[3 * t + j], recv_sem=recv_ref.at[3 * t + j],
                    device_id=(ox, oy, c), device_id_type=MESH)
                cp.wait_send()
                cp.wait_recv()

    both = list(parts) + list(lands)
    out = pl.pallas_call(
        body, name=name, in_specs=[HBM_SPEC] * (2 * nt) + [SEM_SPEC, SEM_SPEC, HBM_SPEC], out_specs=[HBM_SPEC] * (2 * nt),
        out_shape=[pltpu.HBM(a.shape, a.dtype) for a in both], input_output_aliases={t: t for t in range(2 * nt)},
        compiler_params=_split_params())(*both, send_sems, recv_sems, _in_hbm(after))
    return list(out[:nt]), list(out[nt:])


def all_reduce_small(name, bufs):
    n = len(bufs)
    halves = [b.shape[0] // 2 for b in bufs]

    def body(*refs):
        in_refs, out_refs, lands = refs[:n], refs[n:2 * n], refs[2 * n:3 * n]
        send_sems, recv_sems = refs[3 * n:]
        x, y, c, _ = _position()
        mine = [pl.ds(pl.multiple_of(c * h, 8), h) for h in halves]
        other = [pl.ds(pl.multiple_of((1 - c) * h, 8), h) for h in halves]
        for k in range(n):
            out_refs[k][mine[k], :] = in_refs[k][mine[k], :]
        for s, peer in enumerate([(x, y, 1 - c), (1 - x, y, c), (x, 1 - y, c)]):
            cps = []
            for k in range(n):
                src = in_refs[k].at[other[k]] if s == 0 else out_refs[k].at[mine[k]]
                cp = pltpu.make_async_remote_copy(
                    src_ref=src, dst_ref=lands[k].at[s], send_sem=send_sems.at[4 * k + s], recv_sem=recv_sems.at[4 * k + s],
                    device_id=peer, device_id_type=MESH)
                cp.start()
                cps.append(cp)
            for k, cp in enumerate(cps):
                cp.wait()
                out_refs[k][mine[k], :] = out_refs[k][mine[k], :] + lands[k][s]
        cps = []
        for k in range(n):
            cp = pltpu.make_async_remote_copy(
                src_ref=out_refs[k].at[mine[k]], dst_ref=out_refs[k].at[mine[k]], send_sem=send_sems.at[4 * k + 3],
                recv_sem=recv_sems.at[4 * k + 3], device_id=(x, y, 1 - c), device_id_type=MESH)
            cp.start()
            cps.append(cp)
        for cp in cps:
            cp.wait()

    vm = pl.BlockSpec(memory_space=pltpu.VMEM)
    out = pl.pallas_call(
        body, in_specs=[vm] * n, out_specs=[vm] * n, out_shape=[_sds(b.shape, F32) for b in bufs],
        scratch_shapes=[pltpu.VMEM((3, h, b.shape[1]), F32) for h, b in zip(halves, bufs)]
        + [pltpu.SemaphoreType.DMA((4 * n,)), pltpu.SemaphoreType.DMA((4 * n,))],
        name=name, compiler_params=_params())(*bufs)
    return list(out)


def _local_step(x, target, small, need, emit):
    d = D_MODEL
    full = {}

    def after_token(vec, token):
        return vec if token is None else vec + token[0:1, 0:1]

    lam_r, lam_i, bbar_re, bbar_im = small["s5_disc"]
    rb, rc = _s5_matrices(bbar_re, bbar_im, small["s5_c_re"], small["s5_c_im"])
    rb16, rc16 = rb.astype(BF16), rc.astype(BF16)
    lr_t, li_t = lam_r.reshape(S5_BLOCKS, 8, 128), lam_i.reshape(S5_BLOCKS, 8, 128)
    ge, y2, cs = s5_fwd(x, small["norm_mix0"], small["s5_d"], rb16, rc16, lr_t, li_t)
    full.update(need("glu", ge))

    def norm_rows(h, gains):
        xh, _ = _rms_hat(h)
        return [xh * g for g in gains]

    def glu_epilogue(accs, e, r):
        v, gt = accs[0] + r[0], accs[1] + r[1]
        h = e[0] + v * jax.nn.sigmoid(gt)
        return [h, v, gt] + norm_rows(h, r[2:])

    h1, val, gate, n1 = mm_nn(
        "glu", ge, full["w_glu"], [0, d], d, glu_epilogue, [F32, F32, F32, BF16], extras=[x],
        rowvecs=[(small["s5_b_glu"], 0), (small["s5_b_glu"], d), (small["norm_mlp0"], 0)], tm=512, tn=d)

    def mlp_fwd(tag, h, n, w_in, w_out, next_gains):
        def in_epilogue(accs, e, rv):
            pos = jnp.maximum(accs[0], 0.0)
            return [pos * pos, 2.0 * pos]

        r, slope = mm_nn("mlp_in" + tag, n, w_in, [0], w_in.shape[1], in_epilogue, [BF16, BF16], tm=2048)

        def epilogue(accs, e, rv):
            h_out = e[0] + accs[0]
            return [h_out] + norm_rows(h_out, rv)

        outs = mm_nn("mlp_out" + tag, r, w_out, [0], d, epilogue, [F32] + [BF16] * len(next_gains), extras=[h],
                     rowvecs=[(g, 0) for g in next_gains], tm=512, tn=d)
        return outs[0], outs[1:], (n, r, slope)

    full.update(need("mlp0", h1))
    h2, (nkv, n2), mlp0 = mlp_fwd("0", h1, n1, full["w_in0"], full["w_out0"], [small["norm_kv"], small["norm_mix1"]])

    full.update(need("attn", h2))
    kvw = 2 * N_KV * HEAD_DIM
    (kv,) = mm_nn("kv_proj", nkv, full["w_kv"], [0], kvw, lambda accs, e, r: [accs[0] + r[0]], [BF16],
                  rowvecs=[(small["b_kv"], 0)], tm=2048)
    (q,) = mm_nn("q_proj", n2, full["w_q"], [0], d, lambda accs, e, r: [accs[0] + r[0]], [BF16],
                 rowvecs=[(small["b_q"], 0)], tm=2048)
    sinks = small["sinks"].reshape(N_Q)
    o = attn_fwd(q, kv, sinks)
    def o_epilogue(accs, e, r):
        h_out = e[0] + accs[0] + r[0]
        return [h_out] + norm_rows(h_out, r[1:])

    h3, n3 = mm_nn("o_proj", o, full["w_o"], [0], d, o_epilogue, [F32, BF16], extras=[h2],
                   rowvecs=[(small["b_o"], 0), (small["norm_mlp1"], 0)], tm=512, tn=d)
    full.update(need("mlp1", h3))
    h4, _, mlp1 = mlp_fwd("1", h3, n3, full["w_in1"], full["w_out1"], [])
    loss_tile, dh, dhb, dg_final = final_loss(h4, target, small["norm_final"])

    grads_small, grads_full = {"norm_final": dg_final}, {}
    ident = lambda acc, e, r: [acc]
    layer1 = ["w_out1", "w_in1", "w_o", "w_q", "w_kv"]
    layer0 = ["w_out0", "w_in0", "w_glu"]

    def norm_bwd_rows(x_rows, res, dys, gains):
        xh, r = _rms_hat(x_rows)
        dxh = sum(dy * g for dy, g in zip(dys, gains))
        dx = r * (dxh - xh * jnp.mean(dxh * xh, axis=-1, keepdims=True)) + res
        return dx, [jnp.sum(dy * xh, axis=0, keepdims=True) for dy in dys]

    def mlp_bwd(tag, dh, dhb, h_in, gain, w_in, w_out, saved):
        n, r, slope = saved
        grads_full["w_out" + tag] = mm_tn("dw_out" + tag, r, dhb, tn=1024)
        (da,) = mm_nt("mlp_da" + tag, dhb, w_out, lambda acc, e, rv: [acc * e[0].astype(F32)], [BF16], extras=[slope],
                      tm=2048)
        grads_full["w_in" + tag] = mm_tn("dw_in" + tag, n, da, tn=1024)

        def epilogue(acc, e, rv):
            dx, dgs = norm_bwd_rows(e[0], e[1], [acc], rv)
            return [dx, dx, jnp.sum(dx, axis=0, keepdims=True)] + dgs

        dx, dxb, colsum, dg = mm_nt("mlp_dn" + tag, da, w_in, epilogue, [F32, BF16], extras=[h_in, dh], rowvecs=[gain],
                                    n_sums=2, tm=512, tk=d)
        grads_small["norm_mlp" + tag] = dg
        return dx, dxb, colsum

    dh3, dh3b, colsum3 = mlp_bwd("1", dh, dhb, h3, small["norm_mlp1"], full["w_in1"], full["w_out1"], mlp1)
    grads_small["b_o"] = colsum3
    grads_full["w_o"] = mm_tn("dw_o", o, dh3b, tn=1024)
    (do,) = mm_nt("attn_do", dh3b, full["w_o"], ident, [BF16], tm=2048)
    dq, dbq, dprev, dcur, dsink = attn_bwd(q, kv, do, sinks)
    dkv, dbkv = kv_combine(dprev, dcur)
    grads_small["b_q"], grads_small["b_kv"], grads_small["sinks"] = dbq, dbkv, dsink
    grads_full["w_q"] = mm_tn("dw_q", n2, dq, tn=1024)
    grads_full["w_kv"] = mm_tn("dw_kv", nkv, dkv, tk=1024)
    (dnkv,) = mm_nt("kv_dn", dkv, full["w_kv"], ident, [F32], tm=2048, tk=1024)
    token = emit("layer1", {n: grads_full[n] for n in layer1})

    def attn_dn_epilogue(acc, e, rv):
        dx, dgs = norm_bwd_rows(e[0], e[1], [acc, e[2]], rv)
        return [dx, dx] + dgs

    dh2, dh2b, dg_mix1, dg_kv = mm_nt("attn_dn", dq, full["w_q"], attn_dn_epilogue, [F32, BF16], extras=[h2, dh3, dnkv],
                                      rowvecs=[after_token(small["norm_mix1"], token), small["norm_kv"]], n_sums=2,
                                      tm=512, tk=d)
    grads_small["norm_mix1"], grads_small["norm_kv"] = dg_mix1, dg_kv
    dh1, _, _ = mlp_bwd("0", dh2, dh2b, h1, small["norm_mlp0"], full["w_in0"], full["w_out0"], mlp0)

    dz, db_glu = glu_bwd(dh1, val, gate)
    grads_small["s5_b_glu"] = db_glu
    grads_full["w_glu"] = mm_tn("dw_glu", ge, dz, tn=1024)
    token = emit("layer0", {n: grads_full[n] for n in layer0})
    (dy2,) = mm_nt("glu_dy", dz, full["w_glu"], lambda acc, e, rv: [acc * _gelu_grad(e[0])], [F32], extras=[y2],
                   tm=1024, tk=1024)
    rbt16, rct16 = jnp.swapaxes(rb16, 1, 2), jnp.swapaxes(rc16, 1, 2)
    grad_x, dd, drb, drc, dlr, dli, dg_mix0 = s5_bwd(x, small["norm_mix0"], dy2, dh1, after_token(small["s5_d"], token), cs,
                                                     rb16, rbt16, rct16, lr_t, li_t)
    grads_small["s5_d"] = dd
    grads_small["s5_mats"] = (drb, drc, dlr, dli)
    grads_small["norm_mix0"] = dg_mix0
    return loss_tile, grad_x, grads_small


SMALL_NAMES = ["norm_mix", "norm_mlp", "norm_kv", "norm_final", "s5_a_re", "s5_a_im", "s5_log_dt", "s5_b_re", "s5_b_im",
               "s5_c_re", "s5_c_im", "s5_d", "s5_b_glu", "b_kv", "b_q", "sinks", "b_o"]
BIG_NAMES = ["s5_w_glu", "w_kv", "w_q", "w_o", "w_mlp_in", "w_mlp_out"]
WEIGHT_ORDER = ["norm_mix", "norm_mlp", "norm_kv", "norm_final", "s5_a_re", "s5_a_im", "s5_log_dt", "s5_b_re", "s5_b_im",
                "s5_c_re", "s5_c_im", "s5_d", "s5_w_glu", "s5_b_glu", "w_kv", "b_kv", "w_q", "b_q", "sinks", "w_o", "b_o",
                "w_mlp_in", "w_mlp_out"]


def kernel(x, norm_mix, norm_mlp, norm_kv, norm_final, s5_a_re, s5_a_im, s5_log_dt, s5_b_re, s5_b_im, s5_c_re, s5_c_im, s5_d, s5_w_glu, s5_b_glu, w_kv, b_kv, w_q, b_q, sinks, w_o, b_o, w_mlp_in, w_mlp_out, loss_target, m_norm_mix, m_norm_mlp, m_norm_kv, m_norm_final, m_s5_a_re, m_s5_a_im, m_s5_log_dt, m_s5_b_re, m_s5_b_im, m_s5_c_re, m_s5_c_im, m_s5_d, m_s5_w_glu, m_s5_b_glu, m_w_kv, m_b_kv, m_w_q, m_b_q, m_sinks, m_w_o, m_b_o, m_w_mlp_in, m_w_mlp_out, v_norm_mix, v_norm_mlp, v_norm_kv, v_norm_final, v_s5_a_re, v_s5_a_im, v_s5_log_dt, v_s5_b_re, v_s5_b_im, v_s5_c_re, v_s5_c_im, v_s5_d, v_s5_w_glu, v_s5_b_glu, v_w_kv, v_b_kv, v_w_q, v_b_q, v_sinks, v_w_o, v_b_o, v_w_mlp_in, v_w_mlp_out):
    env = dict(locals())
    w = {n: env[n] for n in WEIGHT_ORDER}
    mom = {n: env["m_" + n] for n in WEIGHT_ORDER}
    var = {n: env["v_" + n] for n in WEIGHT_ORDER}
    d = D_MODEL
    xi, yi, ci = lax.axis_index("x"), lax.axis_index("y"), lax.axis_index("c")
    chip = 2 * xi + yi
    where = jnp.stack([ci, chip]).astype(jnp.int32)

    dsh, bsh = s5_d.shape[1], s5_b_glu.shape[1]
    placed = jnp.concatenate([
        lax.dynamic_update_slice(jnp.zeros((4 * dsh,), F32), s5_d[0], (chip * dsh,)),
        lax.dynamic_update_slice(jnp.zeros((4 * bsh,), F32), s5_b_glu[0], (chip * bsh,))])
    placed = jnp.pad(placed, (0, (-placed.shape[0]) % 2048))
    placed = jnp.where(ci == 0, placed, 0.0).reshape(-1, 128)
    (gathered_rows,) = all_reduce_small("gather_vectors", [placed])
    gathered = gathered_rows.reshape(-1)
    d_full, bglu_full = gathered[:4 * dsh].reshape(1, -1), gathered[4 * dsh:].reshape(1, -1)

    big = [s5_w_glu, w_kv[None], w_q, w_o, w_mlp_in, w_mlp_out]
    entries = [(0, 0, "col"), (1, 0, "row"), (2, 0, "row"), (3, 0, "row"), (4, 0, "col"), (4, 1, "col"),
               (5, 0, "row"), (5, 1, "row")]
    names = ["w_glu", "w_kv", "w_q", "w_o", "w_in0", "w_in1", "w_out0", "w_out1"]
    kinds = dict(zip(names, [k for _, _, k in entries]))
    shard_shapes = dict(zip(names, [tuple(big[a].shape[1:]) for a, _, _ in entries]))

    placed_w = dict(zip(names, cast_place(big, entries, where)))
    gather_groups = {"glu": ["w_glu"], "mlp0": ["w_in0", "w_out0"], "attn": ["w_kv", "w_q", "w_o"],
                     "mlp1": ["w_in1", "w_out1"]}
    started, token = {}, gathered_rows
    for group, members in gather_groups.items():
        send, recv, thru, token = gather_start(
            "gather_start_" + group, [placed_w[n] for n in members], [kinds[n] for n in members],
            [shard_shapes[n] for n in members], token)
        started[group] = (send, recv, thru)

    def need(group, after):
        members = gather_groups[group]
        ks, shapes = [kinds[n] for n in members], [shard_shapes[n] for n in members]
        send, recv, thru = started[group]
        landed = gather_wait("gather_wait_" + group, send, recv, thru, ks, shapes, after)
        return dict(zip(members, forward_halves("forward_halves_" + group, landed, ks, shapes)))

    exchanging = {}

    def emit(group, partial):
        members = list(partial)
        ks, shapes = [kinds[n] for n in members], [shard_shapes[n] for n in members]
        landed = swap_halves("swap_halves_" + group, [partial[n] for n in members], ks)
        sums = add_halves("add_halves_" + group, [partial[n] for n in members], landed, ks, where)
        send, recv, parts, lands, tok = exchange_start("exchange_start_" + group, sums, ks, shapes)
        exchanging[group] = (members, send, recv, parts, lands)
        return tok

    disc = lambda *p: _s5_discretise(p[0], p[1], p[2], p[3], p[4])
    disc_args = (s5_a_re[0], s5_a_im[0], s5_log_dt[0], s5_b_re[0], s5_b_im[0])
    disc_out, disc_vjp = jax.vjp(disc, *disc_args)
    small = {
        "norm_mix0": norm_mix[0:1] + token[0:1, 0:1], "norm_mix1": norm_mix[1:2], "norm_mlp0": norm_mlp[0:1], "norm_mlp1": norm_mlp[1:2],
        "norm_kv": norm_kv.reshape(1, d), "norm_final": norm_final.reshape(1, d), "s5_disc": disc_out,
        "s5_c_re": s5_c_re[0], "s5_c_im": s5_c_im[0], "s5_d": d_full, "s5_b_glu": bglu_full,
        "b_kv": b_kv.reshape(1, -1), "b_q": b_q, "sinks": sinks, "b_o": b_o,
    }
    loss_row, grad_x, gs = _local_step(x[0], loss_target[0], small, need, emit)

    mats = s5_compact(*gs["s5_mats"])
    rows = [gs["norm_mix0"], gs["norm_mix1"], gs["norm_mlp0"], gs["norm_mlp1"], gs["norm_kv"], gs["norm_final"], gs["s5_d"],
            gs["b_q"], gs["b_o"], gs["s5_b_glu"], gs["b_kv"], gs["sinks"], loss_row, jnp.zeros((2, d), F32)]
    vecs, mats = all_reduce_small("reduce_small", [jnp.concatenate(rows, axis=0), mats])
    grads = split_vectors(where, vecs, dsh, bsh)
    loss = grads.pop("loss")[0, 0]
    dbbar_re, dbbar_im, dc_re, dc_im, dlr, dli = _s5_unpack(mats)
    g_are, g_aim, g_dt, g_bre, g_bim = disc_vjp((dlr, dli, dbbar_re, dbbar_im))
    grads.update({"s5_a_re": g_are[None], "s5_a_im": g_aim[None], "s5_log_dt": g_dt[None], "s5_b_re": g_bre[None],
                  "s5_b_im": g_bim[None], "s5_c_re": dc_re[None], "s5_c_im": dc_im[None]})

    reduced = [None] * len(big)
    where_of = dict(zip(names, entries))
    for group, after in (("layer1", grad_x), ("layer0", mats)):
        members, send, recv, parts, lands = exchanging[group]
        ks, shapes = [kinds[n] for n in members], [shard_shapes[n] for n in members]
        parts, lands = exchange_wait("exchange_wait_" + group, send, recv, parts, lands, ks, shapes, after)
        targets = [where_of[n][0] for n in members]
        sums = sum_shards("sum_shards_" + group, parts, lands, ks, shapes, where, [where_of[n][1] for n in members],
                          [big[a].shape[0] for a in targets], [reduced[a] for a in targets])
        for a, arr in zip(targets, sums):
            reduced[a] = arr
    reduced = share_halves(reduced, entries)
    for n, g in zip(BIG_NAMES, reduced):
        grads[n] = g.reshape(w[n].shape)

    delta, new_m, new_v = {}, {}, {}
    for n in BIG_NAMES:
        flat = lambda a: a.reshape(-1, a.shape[-1])
        go, dl, nm, nv = adamw("adamw_" + n, flat(w[n]), flat(grads[n]), flat(mom[n]), flat(var[n]))
        grads[n], delta[n], new_m[n], new_v[n] = (t.reshape(w[n].shape) for t in (go, dl, nm, nv))

    def view(n, a):
        return a.reshape(1, -1) if a.ndim == 1 else jnp.swapaxes(a, -1, -2) if n in ("s5_b_re", "s5_b_im") else a

    sw, sg, sm, sv = ([view(n, t[n]) for n in SMALL_NAMES] for t in (w, grads, mom, var))
    for n, a, b, c_ in zip(SMALL_NAMES, *adamw_native("adamw_small", sw, sg, sm, sv)):
        delta[n], new_m[n], new_v[n] = (view(n, t) if t.ndim == 4 else t for t in (a, b, c_))

    out = [loss.reshape(()), grad_x[None]]
    for table in (grads, delta, new_m, new_v):
        out += [table[n].reshape(w[n].shape) for n in WEIGHT_ORDER]
    return tuple(out)
```

```python
import functools
import math

import jax
import jax.numpy as jnp
from jax import lax
from jax.experimental import pallas as pl
from jax.experimental.pallas import tpu as pltpu

F32 = jnp.float32
BF16 = jnp.bfloat16

D_MODEL = 1024
S5_GROUPS = 64
S5_GROUP = 16
S5_STATE = 64
N_KV = 4
N_Q = 16
HEAD_DIM = 64
BLOCK = 128
NORM_EPS = 1e-5
LAMBDA_RE_MAX = -1e-4
ADAM_LR, ADAM_B1, ADAM_B2, ADAM_EPS, ADAM_WD, ADAM_STEP = 0.001, 0.9, 0.999, 1e-08, 0.01, 10

VMEM_LIMIT_BYTES = 56 * 1024 * 1024
S5_CHUNK = 256
S5_BLOCKS = 4
MESH = pl.DeviceIdType.MESH


def _params(sem=None):
    return pltpu.CompilerParams(dimension_semantics=sem, vmem_limit_bytes=VMEM_LIMIT_BYTES)


def _sds(shape, dtype):
    return jax.ShapeDtypeStruct(shape, dtype)


def _rms_hat(xv):
    r = lax.rsqrt(jnp.mean(xv * xv, axis=-1, keepdims=True) + NORM_EPS)
    return xv * r, r


def mm_nn(name, a, w, col_offsets, n_out, epilogue, out_dtypes, extras=(), rowvecs=(), n_sums=0, tm=1024, tn=512):
    m, k = a.shape
    tm, tn = min(tm, m), min(tn, n_out)
    nw, ne, nr, no = len(col_offsets), len(extras), len(rowvecs), len(out_dtypes)

    def body(a_ref, *refs):
        w_refs, e_refs, r_refs = refs[:nw], refs[nw:nw + ne], refs[nw + ne:nw + ne + nr]
        o_refs, s_refs = refs[nw + ne + nr:nw + ne + nr + no], refs[nw + ne + nr + no:]
        av = a_ref[...]
        accs = [jnp.dot(av, w_ref[...], preferred_element_type=F32) for w_ref in w_refs]
        outs = epilogue(accs, [e[...] for e in e_refs], [r[...] for r in r_refs])
        for o_ref, o in zip(o_refs, outs[:no]):
            o_ref[...] = o.astype(o_ref.dtype)
        if n_sums:
            @pl.when(pl.program_id(1) == 0)
            def _():
                for s_ref in s_refs:
                    s_ref[...] = jnp.zeros_like(s_ref)

            for s_ref, val in zip(s_refs, outs[no:]):
                s_ref[...] += val

    def wspec(off):
        return pl.BlockSpec((k, tn), lambda j, i, off=off: (0, off // tn + j))

    def rspec(off):
        return pl.BlockSpec((1, tn), lambda j, i, off=off: (0, off // tn + j))

    tile = pl.BlockSpec((tm, tn), lambda j, i: (i, j))
    in_specs = ([pl.BlockSpec((tm, k), lambda j, i: (i, 0))] + [wspec(o) for o in col_offsets]
                + [tile] * ne + [rspec(o) for _, o in rowvecs])
    sem = ("parallel", "arbitrary") if n_sums else ("parallel", "parallel")
    return pl.pallas_call(
        body, grid=(n_out // tn, m // tm), in_specs=in_specs,
        out_specs=[tile] * no + [pl.BlockSpec((1, tn), lambda j, i: (0, j))] * n_sums,
        out_shape=[_sds((m, n_out), dt) for dt in out_dtypes] + [_sds((1, n_out), F32)] * n_sums, name=name,
        compiler_params=_params(sem))(a, *([w] * nw), *extras, *[r for r, _ in rowvecs])


def mm_nt(name, g, w, epilogue, out_dtypes, extras=(), rowvecs=(), n_sums=0, tm=512, tk=512):
    m, n = g.shape
    k = w.shape[0]
    tm, tk = min(tm, m), min(tk, k)
    ne, nr, no = len(extras), len(rowvecs), len(out_dtypes)

    def body(g_ref, w_ref, *refs):
        e_refs, r_refs, o_refs, s_refs = refs[:ne], refs[ne:ne + nr], refs[ne + nr:ne + nr + no], refs[ne + nr + no:]
        acc = lax.dot_general(g_ref[...], w_ref[...], (((1,), (1,)), ((), ())), preferred_element_type=F32)
        outs = epilogue(acc, [e[...] for e in e_refs], [r[...] for r in r_refs])
        for o_ref, o in zip(o_refs, outs[:no]):
            o_ref[...] = o.astype(o_ref.dtype)
        if n_sums:
            @pl.when(pl.program_id(0) == 0)
            def _():
                for s_ref in s_refs:
                    s_ref[...] = jnp.zeros_like(s_ref)

            for s_ref, val in zip(s_refs, outs[no:]):
                s_ref[...] += val

    tile = pl.BlockSpec((tm, tk), lambda i, j: (i, j))
    vec = pl.BlockSpec((1, tk), lambda i, j: (0, j))
    sem = ("arbitrary", "parallel") if n_sums else ("parallel", "parallel")
    return pl.pallas_call(
        body, grid=(m // tm, k // tk),
        in_specs=[pl.BlockSpec((tm, n), lambda i, j: (i, 0)), pl.BlockSpec((tk, n), lambda i, j: (j, 0))]
        + [tile] * ne + [vec] * nr,
        out_specs=[tile] * no + [vec] * n_sums,
        out_shape=[_sds((m, k), dt) for dt in out_dtypes] + [_sds((1, k), F32)] * n_sums, name=name,
        compiler_params=_params(sem))(g, w, *extras, *rowvecs)


def mm_tn(name, a, g, tk=512, tn=512):
    m, k = a.shape
    n = g.shape[1]
    tk, tn = min(tk, k), min(tn, n)

    def body(a_ref, g_ref, o_ref):
        acc = lax.dot_general(a_ref[...], g_ref[...], (((0,), (0,)), ((), ())), preferred_element_type=F32)
        o_ref[...] = acc.astype(o_ref.dtype)

    return pl.pallas_call(
        body, grid=(k // tk, n // tn),
        in_specs=[pl.BlockSpec((m, tk), lambda i, j: (0, i)), pl.BlockSpec((m, tn), lambda i, j: (0, j))],
        out_specs=pl.BlockSpec((tk, tn), lambda i, j: (i, j)), out_shape=_sds((k, n), BF16), name=name,
        compiler_params=_params(("parallel", "parallel")))(a, g)


def _row_mask(tc):
    row = lax.broadcasted_iota(jnp.int32, (8 * tc, 256), 0) % 8
    col = lax.broadcasted_iota(jnp.int32, (8 * tc, 256), 1) // 32
    return row == col


def _expand_rows(expand_ref, val, mask):
    rep = jnp.dot(expand_ref[...], val.astype(BF16), preferred_element_type=F32)
    return jnp.where(mask, rep, 0.0).astype(BF16)


def _staged(ref):
    return jnp.concatenate([ref[0], ref[1]], axis=1)


def _stage(ref, val):
    ref[0] = val[:, 0:128]
    ref[1] = val[:, 128:256]


def _gather_rows(src_ref, tc):
    halves = []
    for half in range(2):
        col = lax.broadcasted_iota(jnp.int32, (tc, 128), 1) // 32 + 4 * half
        out = jnp.zeros((tc, 128), F32)
        for s8 in range(4 * half, 4 * half + 4):
            out = jnp.where(col == s8, src_ref.at[half][pl.ds(s8, tc, stride=8), :], out)
        halves.append(out)
    return jnp.concatenate(halves, axis=1)


def _gelu(x):
    c = math.sqrt(2.0 / math.pi)
    return 0.5 * x * (1.0 + jnp.tanh(c * (x + 0.044715 * x * x * x)))


def _gelu_grad(x):
    c = math.sqrt(2.0 / math.pi)
    t = jnp.tanh(c * (x + 0.044715 * x * x * x))
    return 0.5 * (1.0 + t) + 0.5 * x * (1.0 - t * t) * c * (1.0 + 3.0 * 0.044715 * x * x)


def _expansion(tc):
    return (jnp.arange(8 * tc)[:, None] // 8 == jnp.arange(tc)[None, :]).astype(BF16)


def s5_fwd(x, gain, d_skip, rb, rc, lam_r, lam_i):
    n_rows = x.shape[0]
    tc = min(S5_CHUNK, n_rows)
    nc = n_rows // tc

    def body(x_ref, g_ref, d_ref, ex_ref, rb_ref, rc_ref, lr_ref, li_ref, ge_ref, y2_ref, cs_ref, bux, yrows, carry):
        i = pl.program_id(0)
        u = _rms_hat(x_ref[...])[0] * g_ref[...]

        @pl.when(i == 0)
        def _():
            carry[...] = jnp.zeros_like(carry)

        cs_ref[0] = carry[...]
        mask = _row_mask(tc)
        for blk in range(S5_BLOCKS):
            lhs = _expand_rows(ex_ref, u[:, blk * 256:(blk + 1) * 256], mask)
            bux[blk] = jnp.dot(lhs, rb_ref[blk], preferred_element_type=F32)
        lam = [(lr_ref[blk], li_ref[blk]) for blk in range(S5_BLOCKS)]

        def step(t, c):
            r0 = pl.multiple_of(t * 8, 8)
            new = []
            for blk in range(S5_BLOCKS):
                xr, xi = c[2 * blk], c[2 * blk + 1]
                lr, li = lam[blk]
                nr = lr * xr - li * xi + bux[blk, pl.ds(r0, 8), 0:128]
                ni = lr * xi + li * xr + bux[blk, pl.ds(r0, 8), 128:256]
                bux[blk, pl.ds(r0, 8), 0:128] = nr
                bux[blk, pl.ds(r0, 8), 128:256] = ni
                new += [nr, ni]
            return tuple(new)

        c0 = []
        for blk in range(S5_BLOCKS):
            c0 += [carry[blk, :, 0:128], carry[blk, :, 128:256]]
        cn = lax.fori_loop(0, tc, step, tuple(c0), unroll=4)
        for blk in range(S5_BLOCKS):
            carry[blk, :, 0:128] = cn[2 * blk]
            carry[blk, :, 128:256] = cn[2 * blk + 1]
        for blk in range(S5_BLOCKS):
            _stage(yrows, jnp.dot(bux[blk].astype(BF16), rc_ref[blk], preferred_element_type=F32))
            sl = slice(blk * 256, (blk + 1) * 256)
            y2 = _gather_rows(yrows, tc) + d_ref[:, sl] * u[:, sl]
            y2_ref[:, sl] = y2
            ge_ref[:, sl] = _gelu(y2).astype(BF16)

    row = pl.BlockSpec((tc, D_MODEL), lambda i: (i, 0))
    vec = pl.BlockSpec((1, D_MODEL), lambda i: (0, 0))
    mat = pl.BlockSpec((S5_BLOCKS, 256, 256), lambda i: (0, 0, 0))
    lamspec = pl.BlockSpec((S5_BLOCKS, 8, 128), lambda i: (0, 0, 0))
    return pl.pallas_call(
        body, grid=(nc,),
        in_specs=[row, vec, vec, pl.BlockSpec((8 * tc, tc), lambda i: (0, 0)), mat, mat, lamspec, lamspec],
        out_specs=[row, row, pl.BlockSpec((1, S5_BLOCKS, 8, 256), lambda i: (i, 0, 0, 0))],
        out_shape=[_sds((n_rows, D_MODEL), BF16), _sds((n_rows, D_MODEL), F32), _sds((nc, S5_BLOCKS, 8, 256), F32)],
        scratch_shapes=[pltpu.VMEM((S5_BLOCKS, 8 * tc, 256), F32), pltpu.VMEM((2, 8 * tc, 128), F32),
                        pltpu.VMEM((S5_BLOCKS, 8, 256), F32)],
        name="s5_fwd", compiler_params=_params(("arbitrary",)))(x, gain, d_skip, _expansion(tc), rb, rc, lam_r, lam_i)


def s5_bwd(x, gain, dy2, res, d_skip, cs, rb, rbt, rct, lam_r, lam_i):
    n_rows = x.shape[0]
    tc = min(S5_CHUNK, n_rows)
    nc = n_rows // tc

    def body(x_ref, g_ref, dy_ref, res_ref, d_ref, cs_ref, ex_ref, rb_ref, rbt_ref, rct_ref, lr_ref, li_ref,
             dx_ref, dd_ref, drb_ref, drc_ref, dlr_ref, dli_ref, dg_ref, tmp, du, lhsu, lhsd, xs, adj, acarry):
        i = pl.program_id(0)
        u = _rms_hat(x_ref[...])[0] * g_ref[...]

        @pl.when(i == 0)
        def _():
            acarry[...] = jnp.zeros_like(acarry)
            dd_ref[...] = jnp.zeros_like(dd_ref)
            drb_ref[...] = jnp.zeros_like(drb_ref)
            drc_ref[...] = jnp.zeros_like(drc_ref)
            dlr_ref[...] = jnp.zeros_like(dlr_ref)
            dli_ref[...] = jnp.zeros_like(dli_ref)
            dg_ref[...] = jnp.zeros_like(dg_ref)

        dd_ref[...] += jnp.sum(dy_ref[...] * u, axis=0, keepdims=True)
        mask = _row_mask(tc)
        for blk in range(S5_BLOCKS):
            sl = slice(blk * 256, (blk + 1) * 256)
            lhsu[blk] = _expand_rows(ex_ref, u[:, sl], mask)
            xs[blk] = jnp.dot(lhsu[blk], rb_ref[blk], preferred_element_type=F32)
            lhsd[blk] = _expand_rows(ex_ref, dy_ref[:, sl], mask)
            adj[blk] = jnp.dot(lhsd[blk], rct_ref[blk], preferred_element_type=F32)
        lam = [(lr_ref[blk], li_ref[blk]) for blk in range(S5_BLOCKS)]

        def fstep(t, c):
            r0 = pl.multiple_of(t * 8, 8)
            new = []
            for blk in range(S5_BLOCKS):
                xr, xi = c[2 * blk], c[2 * blk + 1]
                lr, li = lam[blk]
                nr = lr * xr - li * xi + xs[blk, pl.ds(r0, 8), 0:128]
                ni = lr * xi + li * xr + xs[blk, pl.ds(r0, 8), 128:256]
                xs[blk, pl.ds(r0, 8), 0:128] = nr
                xs[blk, pl.ds(r0, 8), 128:256] = ni
                new += [nr, ni]
            return tuple(new)

        c0 = []
        for blk in range(S5_BLOCKS):
            c0 += [cs_ref[0, blk, :, 0:128], cs_ref[0, blk, :, 128:256]]
        lax.fori_loop(0, tc, fstep, tuple(c0), unroll=4)

        def bstep(k, c):
            t = tc - 1 - k
            r0 = pl.multiple_of(t * 8, 8)
            rp = pl.multiple_of(jnp.maximum(t - 1, 0) * 8, 8)
            first = t == 0
            new_a, new_g = [], []
            for blk in range(S5_BLOCKS):
                ar, ai = c[0][2 * blk], c[0][2 * blk + 1]
                glr, gli = c[1][2 * blk], c[1][2 * blk + 1]
                lr, li = lam[blk]
                nr = lr * ar + li * ai + adj[blk, pl.ds(r0, 8), 0:128]
                ni = lr * ai - li * ar + adj[blk, pl.ds(r0, 8), 128:256]
                adj[blk, pl.ds(r0, 8), 0:128] = nr
                adj[blk, pl.ds(r0, 8), 128:256] = ni
                pr = jnp.where(first, cs_ref[0, blk, :, 0:128], xs[blk, pl.ds(rp, 8), 0:128])
                pi = jnp.where(first, cs_ref[0, blk, :, 128:256], xs[blk, pl.ds(rp, 8), 128:256])
                new_a += [nr, ni]
                new_g += [glr + nr * pr + ni * pi, gli + ni * pr - nr * pi]
            return tuple(new_a), tuple(new_g)

        a0, g0 = [], []
        for blk in range(S5_BLOCKS):
            a0 += [acarry[blk, :, 0:128], acarry[blk, :, 128:256]]
            g0 += [dlr_ref[blk], dli_ref[blk]]
        an, gn = lax.fori_loop(0, tc, bstep, (tuple(a0), tuple(g0)), unroll=2)
        for blk in range(S5_BLOCKS):
            acarry[blk, :, 0:128] = an[2 * blk]
            acarry[blk, :, 128:256] = an[2 * blk + 1]
            dlr_ref[blk] = gn[2 * blk]
            dli_ref[blk] = gn[2 * blk + 1]
        for blk in range(S5_BLOCKS):
            sl = slice(blk * 256, (blk + 1) * 256)
            ab = adj[blk].astype(BF16)
            _stage(tmp, jnp.dot(ab, rbt_ref[blk], preferred_element_type=F32))
            du[:, sl] = _gather_rows(tmp, tc) + d_ref[:, sl] * dy_ref[:, sl]
            drb_ref[blk] += lax.dot_general(lhsu[blk], ab, (((0,), (0,)), ((), ())), preferred_element_type=F32)
            drc_ref[blk] += lax.dot_general(lhsd[blk], xs[blk].astype(BF16), (((0,), (0,)), ((), ())),
                                            preferred_element_type=F32)
        xh, r = _rms_hat(x_ref[...])
        dg_ref[...] += jnp.sum(du[...] * xh, axis=0, keepdims=True)
        dxh = du[...] * g_ref[...]
        dx_ref[...] = r * (dxh - xh * jnp.mean(dxh * xh, axis=-1, keepdims=True)) + res_ref[...]

    rev = pl.BlockSpec((tc, D_MODEL), lambda i: (nc - 1 - i, 0))
    vec = pl.BlockSpec((1, D_MODEL), lambda i: (0, 0))
    mat = pl.BlockSpec((S5_BLOCKS, 256, 256), lambda i: (0, 0, 0))
    lamspec = pl.BlockSpec((S5_BLOCKS, 8, 128), lambda i: (0, 0, 0))
    big = pltpu.VMEM((S5_BLOCKS, 8 * tc, 256), F32)
    bigb = pltpu.VMEM((S5_BLOCKS, 8 * tc, 256), BF16)
    return pl.pallas_call(
        body, grid=(nc,),
        in_specs=[rev, vec, rev, rev, vec, pl.BlockSpec((1, S5_BLOCKS, 8, 256), lambda i: (nc - 1 - i, 0, 0, 0)),
                  pl.BlockSpec((8 * tc, tc), lambda i: (0, 0)), mat, mat, mat, lamspec, lamspec],
        out_specs=[rev, vec, mat, mat, lamspec, lamspec, vec],
        out_shape=[_sds((n_rows, D_MODEL), F32), _sds((1, D_MODEL), F32), _sds((S5_BLOCKS, 256, 256), F32),
                   _sds((S5_BLOCKS, 256, 256), F32), _sds((S5_BLOCKS, 8, 128), F32), _sds((S5_BLOCKS, 8, 128), F32),
                   _sds((1, D_MODEL), F32)],
        scratch_shapes=[pltpu.VMEM((2, 8 * tc, 128), F32), pltpu.VMEM((tc, D_MODEL), F32), bigb, bigb, big, big,
                        pltpu.VMEM((S5_BLOCKS, 8, 256), F32)],
        name="s5_bwd", compiler_params=_params(("arbitrary",)))(
            x, gain, dy2, res, d_skip, cs, _expansion(tc), rb, rbt, rct, lam_r, lam_i)


def _s5_discretise(a_re, a_im, log_dt, b_re, b_im):
    lam = lax.complex(jnp.minimum(a_re, LAMBDA_RE_MAX), a_im)
    dt = jnp.exp(log_dt)[:, None]
    lam_bar = jnp.exp(lam * dt)
    b_bar = ((lam_bar - 1.0) / lam)[:, :, None] * lax.complex(b_re, b_im)
    return jnp.real(lam_bar), jnp.imag(lam_bar), jnp.real(b_bar), jnp.imag(b_bar)


def _s5_matrices(bbar_re, bbar_im, c_re, c_im):
    eye2 = jnp.eye(2, dtype=F32)
    bst = jnp.stack([bbar_re, bbar_im]).reshape(2, S5_BLOCKS, 8, 2, S5_STATE, S5_GROUP)
    bt = jnp.transpose(bst, (1, 2, 3, 5, 0, 4))
    rb = (bt[:, :, :, :, :, None, :] * eye2[None, None, :, None, None, :, None]).reshape(S5_BLOCKS, 256, 256)
    cst = jnp.stack([c_re, -c_im]).reshape(2, S5_BLOCKS, 8, 2, S5_GROUP, S5_STATE)
    ct = jnp.transpose(cst, (1, 0, 5, 2, 3, 4))
    rc = (ct[:, :, None, :, :, :, :] * eye2[None, None, :, None, None, :, None]).reshape(S5_BLOCKS, 256, 256)
    return rb, rc


def s5_compact(drb, drct, dlr, dli):
    def body(drb_ref, drct_ref, dlr_ref, dli_ref, o_ref):
        even = (lax.broadcasted_iota(jnp.int32, (256, 64), 0) // S5_GROUP) % 2 == 0
        for blk in range(S5_BLOCKS):
            for k, ref in enumerate((drb_ref, drct_ref)):
                m = ref[blk]
                re = jnp.where(even, m[:, 0:64], m[:, 64:128])
                im = jnp.where(even, m[:, 128:192], m[:, 192:256])
                o_ref[pl.ds(k * 1024 + blk * 256, 256), :] = jnp.concatenate([re, im], axis=1)
            o_ref[pl.ds(2048 + blk * 8, 8), :] = dlr_ref[blk]
            o_ref[pl.ds(2080 + blk * 8, 8), :] = dli_ref[blk]

    vm = pl.BlockSpec(memory_space=pltpu.VMEM)
    return pl.pallas_call(body, in_specs=[vm] * 4, out_specs=vm, out_shape=_sds((2112, 128), F32), name="s5_compact",
                          compiler_params=_params())(drb, drct, dlr, dli)


def _s5_unpack(mats):
    bm = mats[0:1024].reshape(S5_GROUPS, S5_GROUP, 128)
    cm = mats[1024:2048].reshape(S5_GROUPS, S5_GROUP, 128)
    swap = lambda t: jnp.transpose(t, (0, 2, 1))
    return (swap(bm[:, :, 0:64]), swap(bm[:, :, 64:128]), cm[:, :, 0:64], -cm[:, :, 64:128],
            mats[2048:2080].reshape(S5_GROUPS, S5_STATE), mats[2080:2112].reshape(S5_GROUPS, S5_STATE))


NEG = -1e30


GROUP = N_Q // N_KV


def _attn_masks(n):
    qi = lax.broadcasted_iota(jnp.int32, (GROUP * BLOCK, BLOCK), 0) % BLOCK
    kj = lax.broadcasted_iota(jnp.int32, (GROUP * BLOCK, BLOCK), 1)
    return jnp.logical_and(kj > qi, n > 0), kj <= qi


def _stack_heads(ref, kh):
    return jnp.concatenate([ref[:, (GROUP * kh + g) * HEAD_DIM:(GROUP * kh + g + 1) * HEAD_DIM] for g in range(GROUP)], axis=0)


def _unstack_heads(val):
    return jnp.concatenate([val[g * BLOCK:(g + 1) * BLOCK] for g in range(GROUP)], axis=1)


def _sink_column(sink_ref, kh):
    grp = lax.broadcasted_iota(jnp.int32, (GROUP * BLOCK, 1), 0) // BLOCK
    col = jnp.zeros((GROUP * BLOCK, 1), F32)
    for g in range(GROUP):
        col = jnp.where(grp == g, sink_ref[GROUP * kh + g], col)
    return col, grp


def _attn_exp(q4, kp, kc, sink, mask_p, mask_c):
    scale = 1.0 / math.sqrt(HEAD_DIM)
    nt = (((1,), (1,)), ((), ()))
    sp = jnp.where(mask_p, lax.dot_general(q4, kp, nt, preferred_element_type=F32) * scale, NEG)
    sc = jnp.where(mask_c, lax.dot_general(q4, kc, nt, preferred_element_type=F32) * scale, NEG)
    m = jnp.maximum(jnp.maximum(jnp.max(sp, axis=-1, keepdims=True), jnp.max(sc, axis=-1, keepdims=True)), sink)
    pp = jnp.exp(sp - m)
    pc = jnp.exp(sc - m)
    ps = jnp.exp(sink - m)
    inv = 1.0 / (jnp.sum(pp, axis=-1, keepdims=True) + jnp.sum(pc, axis=-1, keepdims=True) + ps)
    return pp, pc, ps, inv


def attn_fwd(q, kv, sinks):
    n_rows = q.shape[0]
    nb = n_rows // BLOCK

    def body(sink_ref, q_ref, kvp_ref, kvc_ref, o_ref):
        n = pl.program_id(0)
        mask_p, mask_c = _attn_masks(n)
        outs = []
        for kh in range(N_KV):
            ks, vs = slice(kh * HEAD_DIM, (kh + 1) * HEAD_DIM), slice((N_KV + kh) * HEAD_DIM, (N_KV + kh + 1) * HEAD_DIM)
            sink, _ = _sink_column(sink_ref, kh)
            pp, pc, _, inv = _attn_exp(_stack_heads(q_ref, kh), kvp_ref[:, ks], kvc_ref[:, ks], sink, mask_p, mask_c)
            o4 = (jnp.dot(pp.astype(BF16), kvp_ref[:, vs], preferred_element_type=F32)
                  + jnp.dot(pc.astype(BF16), kvc_ref[:, vs], preferred_element_type=F32)) * inv
            outs.append(_unstack_heads(o4))
        o_ref[...] = jnp.concatenate(outs, axis=1).astype(BF16)

    kvw = 2 * N_KV * HEAD_DIM
    return pl.pallas_call(
        body, grid=(nb,),
        in_specs=[pl.BlockSpec(memory_space=pltpu.SMEM), pl.BlockSpec((BLOCK, D_MODEL), lambda n: (n, 0)),
                  pl.BlockSpec((BLOCK, kvw), lambda n: (jnp.maximum(n - 1, 0), 0)), pl.BlockSpec((BLOCK, kvw), lambda n: (n, 0))],
        out_specs=pl.BlockSpec((BLOCK, D_MODEL), lambda n: (n, 0)), out_shape=_sds((n_rows, D_MODEL), BF16),
        name="attn_fwd", compiler_params=_params(("parallel",)))(sinks, q, kv, kv)


def attn_bwd(q, kv, do, sinks):
    n_rows = q.shape[0]
    nb = n_rows // BLOCK
    kvw = 2 * N_KV * HEAD_DIM
    tn = (((0,), (0,)), ((), ()))
    nt = (((1,), (1,)), ((), ()))
    scale = 1.0 / math.sqrt(HEAD_DIM)

    def body(sink_ref, q_ref, kvp_ref, kvc_ref, do_ref, dq_ref, dbq_ref, dprev_ref, dcur_ref, dsink_ref):
        n = pl.program_id(0)
        mask_p, mask_c = _attn_masks(n)
        lane = lax.broadcasted_iota(jnp.int32, (1, D_MODEL), 1)
        dqs, dsink = [], jnp.zeros((1, D_MODEL), F32)
        dkp, dkc, dvp, dvc = [], [], [], []
        for kh in range(N_KV):
            ks, vs = slice(kh * HEAD_DIM, (kh + 1) * HEAD_DIM), slice((N_KV + kh) * HEAD_DIM, (N_KV + kh + 1) * HEAD_DIM)
            q4, do4 = _stack_heads(q_ref, kh), _stack_heads(do_ref, kh)
            kp, kc, vp, vc = kvp_ref[:, ks], kvc_ref[:, ks], kvp_ref[:, vs], kvc_ref[:, vs]
            sink, grp = _sink_column(sink_ref, kh)
            pp, pc, ps, inv = _attn_exp(q4, kp, kc, sink, mask_p, mask_c)
            pp, pc = pp * inv, pc * inv
            dpp = lax.dot_general(do4, vp, nt, preferred_element_type=F32)
            dpc = lax.dot_general(do4, vc, nt, preferred_element_type=F32)
            delta = jnp.sum(pp * dpp, axis=-1, keepdims=True) + jnp.sum(pc * dpc, axis=-1, keepdims=True)
            dsp = (pp * (dpp - delta) * scale).astype(BF16)
            dsc = (pc * (dpc - delta) * scale).astype(BF16)
            dsk = ps * inv * delta
            for g in range(GROUP):
                dsink = dsink + jnp.where(lane == GROUP * kh + g, -jnp.sum(jnp.where(grp == g, dsk, 0.0)), 0.0)
            dqs.append(_unstack_heads(jnp.dot(dsp, kp, preferred_element_type=F32)
                                      + jnp.dot(dsc, kc, preferred_element_type=F32)))
            dkp.append(lax.dot_general(dsp, q4, tn, preferred_element_type=F32))
            dkc.append(lax.dot_general(dsc, q4, tn, preferred_element_type=F32))
            dvp.append(lax.dot_general(pp.astype(BF16), do4, tn, preferred_element_type=F32))
            dvc.append(lax.dot_general(pc.astype(BF16), do4, tn, preferred_element_type=F32))
        dq = jnp.concatenate(dqs, axis=1)
        dq_ref[...] = dq.astype(BF16)
        dprev_ref[0] = jnp.concatenate(dkp + dvp, axis=1)
        dcur_ref[0] = jnp.concatenate(dkc + dvc, axis=1)

        @pl.when(n == 0)
        def _():
            dbq_ref[...] = jnp.zeros_like(dbq_ref)
            dsink_ref[...] = jnp.zeros_like(dsink_ref)

        dbq_ref[...] += jnp.sum(dq, axis=0, keepdims=True)
        dsink_ref[...] += dsink

    blk = pl.BlockSpec((BLOCK, D_MODEL), lambda n: (n, 0))
    part = pl.BlockSpec((1, BLOCK, kvw), lambda n: (n, 0, 0))
    return pl.pallas_call(
        body, grid=(nb,),
        in_specs=[pl.BlockSpec(memory_space=pltpu.SMEM), blk,
                  pl.BlockSpec((BLOCK, kvw), lambda n: (jnp.maximum(n - 1, 0), 0)), pl.BlockSpec((BLOCK, kvw), lambda n: (n, 0)), blk],
        out_specs=[blk, pl.BlockSpec((1, D_MODEL), lambda n: (0, 0)), part, part, pl.BlockSpec((1, D_MODEL), lambda n: (0, 0))],
        out_shape=[_sds((n_rows, D_MODEL), BF16), _sds((1, D_MODEL), F32), _sds((nb, BLOCK, kvw), F32),
                   _sds((nb, BLOCK, kvw), F32), _sds((1, D_MODEL), F32)],
        name="attn_bwd", compiler_params=_params(("arbitrary",)))(sinks, q, kv, kv, do)


def kv_combine(dprev, dcur):
    nb, _, kvw = dprev.shape

    def body(dcur_ref, dprev_ref, dkv_ref, db_ref):
        total = jnp.zeros((1, kvw), F32)
        for m in range(nb):
            dkv = dcur_ref[m] + dprev_ref[m + 1] if m + 1 < nb else dcur_ref[m]
            dkv_ref[m * BLOCK:(m + 1) * BLOCK, :] = dkv.astype(BF16)
            total = total + jnp.sum(dkv, axis=0, keepdims=True)
        db_ref[...] = jnp.concatenate([total, jnp.zeros((1, D_MODEL - kvw), F32)], axis=1)

    vm = pl.BlockSpec(memory_space=pltpu.VMEM)
    return pl.pallas_call(body, in_specs=[vm, vm], out_specs=[vm, vm],
                          out_shape=[_sds((nb * BLOCK, kvw), BF16), _sds((1, D_MODEL), F32)], name="kv_combine",
                          compiler_params=_params())(dcur, dprev)


def glu_bwd(dout, val, gate, tm=256):
    n_rows, d = dout.shape

    def body(do_ref, v_ref, g_ref, dz_ref, db_ref):
        i = pl.program_id(0)
        sg = jax.nn.sigmoid(g_ref[...])
        dval = do_ref[...] * sg
        dgate = do_ref[...] * v_ref[...] * sg * (1.0 - sg)
        dz_ref[...] = jnp.concatenate([dval, dgate], axis=1).astype(BF16)

        @pl.when(i == 0)
        def _():
            db_ref[...] = jnp.zeros_like(db_ref)

        db_ref[0:1, :] += jnp.sum(dval, axis=0, keepdims=True)
        db_ref[1:2, :] += jnp.sum(dgate, axis=0, keepdims=True)

    row = pl.BlockSpec((tm, d), lambda i: (i, 0))
    return pl.pallas_call(
        body, grid=(n_rows // tm,), in_specs=[row, row, row],
        out_specs=[pl.BlockSpec((tm, 2 * d), lambda i: (i, 0)), pl.BlockSpec((2, d), lambda i: (0, 0))],
        out_shape=[_sds((n_rows, 2 * d), BF16), _sds((2, d), F32)],
        name="glu_bwd", compiler_params=_params(("arbitrary",)))(dout, val, gate)


def _adam_update(w, g, m, v):
    nm = ADAM_B1 * m + (1.0 - ADAM_B1) * g
    nv = ADAM_B2 * v + (1.0 - ADAM_B2) * (g * g)
    m_hat = nm / (1.0 - ADAM_B1 ** ADAM_STEP)
    v_hat = nv / (1.0 - ADAM_B2 ** ADAM_STEP)
    return -ADAM_LR * (m_hat / (jnp.sqrt(v_hat) + ADAM_EPS) + ADAM_WD * w), nm, nv


def adamw(name, w, g, m, v, tm=256):
    n_rows, d = w.shape
    tm = tm if n_rows % tm == 0 else n_rows

    def body(w_ref, g_ref, m_ref, v_ref, go_ref, d_ref, nm_ref, nv_ref):
        gv = g_ref[...]
        go_ref[...] = gv
        d_ref[...], nm_ref[...], nv_ref[...] = _adam_update(w_ref[...], gv, m_ref[...], v_ref[...])

    row = pl.BlockSpec((tm, d), lambda i: (i, 0))
    return pl.pallas_call(
        body, grid=(n_rows // tm,), in_specs=[row] * 4, out_specs=[row] * 4,
        out_shape=[_sds((n_rows, d), F32)] * 4, name=name, compiler_params=_params(("parallel",)))(w, g, m, v)


def adamw_native(name, ws, gs, ms, vs):
    n = len(ws)

    def body(*refs):
        w_refs, g_refs, m_refs, v_refs = refs[:n], refs[n:2 * n], refs[2 * n:3 * n], refs[3 * n:4 * n]
        d_refs, nm_refs, nv_refs = refs[4 * n:5 * n], refs[5 * n:6 * n], refs[6 * n:7 * n]
        for k in range(n):
            dl, nm, nv = _adam_update(w_refs[k][...], g_refs[k][...], m_refs[k][...], v_refs[k][...])
            d_refs[k][...] = dl
            nm_refs[k][...] = nm
            nv_refs[k][...] = nv

    vm = pl.BlockSpec(memory_space=pltpu.VMEM)
    shapes = [_sds(w.shape, F32) for w in ws]
    out = pl.pallas_call(body, in_specs=[vm] * (4 * n), out_specs=[vm] * (3 * n), out_shape=shapes * 3, name=name,
                         compiler_params=_params())(*ws, *gs, *ms, *vs)
    return list(out[:n]), list(out[n:2 * n]), list(out[2 * n:])


VEC_ROWS = {"norm_mix": 0, "norm_mlp": 2, "norm_kv": 4, "norm_final": 5, "s5_d": 6, "b_q": 7, "b_o": 8, "s5_b_glu": 9,
            "b_kv": 11, "sinks": 12, "loss": 13}


def split_vectors(where, vecs, d_shard, glu_shard):
    kvw = 2 * N_KV * HEAD_DIM
    shapes = {"norm_mix": (2, D_MODEL), "norm_mlp": (2, D_MODEL), "norm_kv": (1, D_MODEL), "norm_final": (1, D_MODEL),
              "s5_d": (1, d_shard), "b_q": (1, D_MODEL), "b_o": (1, D_MODEL), "s5_b_glu": (1, glu_shard), "b_kv": (1, kvw),
              "sinks": (1, N_Q), "loss": (1, 128)}
    names = list(shapes)

    def body(where_ref, v_ref, *o_refs):
        chip = where_ref[1]
        for name, o_ref in zip(names, o_refs):
            r0, (r, n) = VEC_ROWS[name], shapes[name]
            if name == "s5_d":
                g = jnp.zeros((1, n), F32)
                for j in range(4):
                    g = jnp.where(chip == j, v_ref[r0:r0 + 1, j * n:(j + 1) * n], g)
            elif name == "s5_b_glu":
                g = jnp.zeros((1, n), F32)
                for j in range(4):
                    row, col = r0 + (j * n) // D_MODEL, (j * n) % D_MODEL
                    g = jnp.where(chip == j, v_ref[row:row + 1, col:col + n], g)
            else:
                g = v_ref[r0:r0 + r, 0:n]
            o_ref[...] = g

    vm = pl.BlockSpec(memory_space=pltpu.VMEM)
    out = pl.pallas_call(body, in_specs=[pl.BlockSpec(memory_space=pltpu.SMEM), vm], out_specs=[vm] * len(names),
                         out_shape=[_sds(shapes[n], F32) for n in names], name="split_vectors",
                         compiler_params=_params())(where, vecs)
    return dict(zip(names, out))


def _position():
    x, y, c = lax.axis_index("x"), lax.axis_index("y"), lax.axis_index("c")
    others = [(1 - x, y), (x, 1 - y), (1 - x, 1 - y)]
    return x, y, c, others


def _window(ref, kind, chip, half, shard_shape):
    r, n = shard_shape
    if kind == "col":
        return ref.at[pl.ds(pl.multiple_of(half * (r // 2), 16), r // 2), pl.ds(pl.multiple_of(chip * n, 128), n)]
    return ref.at[pl.ds(pl.multiple_of(chip * r, 16), r), pl.ds(pl.multiple_of(half * (n // 2), 128), n // 2)]


def _half(ref, kind, half, shape):
    r, n = shape
    if kind == "col":
        return ref.at[pl.ds(pl.multiple_of(half * (r // 2), 16), r // 2), :]
    return ref.at[:, pl.ds(pl.multiple_of(half * (n // 2), 128), n // 2)]


def swap_halves(name, grads, kinds):
    nt = len(grads)
    shapes = [tuple(g.shape) for g in grads]

    def body(*refs):
        in_refs, out_refs = refs[:nt], refs[nt:2 * nt]
        send_sems, recv_sems = refs[2 * nt:]
        x, y, c, _ = _position()
        cps = []
        for t in range(nt):
            cp = pltpu.make_async_remote_copy(
                src_ref=_half(in_refs[t], kinds[t], 1 - c, shapes[t]), dst_ref=_half(out_refs[t], kinds[t], 1 - c, shapes[t]),
                send_sem=send_sems.at[t], recv_sem=recv_sems.at[t], device_id=(x, y, 1 - c), device_id_type=MESH)
            cp.start()
            cps.append(cp)
        for t in range(nt):
            mine = _half(out_refs[t], kinds[t], c, shapes[t])
            pltpu.make_async_remote_copy(
                src_ref=mine, dst_ref=mine, send_sem=send_sems.at[t], recv_sem=recv_sems.at[t],
                device_id=(x, y, 1 - c), device_id_type=MESH).wait_recv()
        for cp in cps:
            cp.wait_send()

    hbm = pl.BlockSpec(memory_space=pl.ANY)
    return pl.pallas_call(
        body, in_specs=[hbm] * nt, out_specs=[hbm] * nt, out_shape=[_sds(s, BF16) for s in shapes],
        scratch_shapes=[pltpu.SemaphoreType.DMA((nt,)), pltpu.SemaphoreType.DMA((nt,))],
        name=name, compiler_params=_params())(*grads)


def _half_spec(kind, shape, tiles):
    r, n = shape
    if kind == "col":
        tn = n // tiles
        return pl.BlockSpec((r // 2, tn), lambda i, s: (s[0], i))
    tm = r // tiles
    return pl.BlockSpec((tm, n // 2), lambda i, s: (i, s[0]))


def add_halves(name, mine, landed, kinds, where, tiles=4):
    nt = len(mine)
    shapes = [tuple(a.shape) for a in mine]

    def compact(t):
        r, n = shapes[t]
        if kinds[t] == "col":
            return (r // 2, n), pl.BlockSpec((r // 2, n // tiles), lambda i, s: (0, i))
        return (r, n // 2), pl.BlockSpec((r // tiles, n // 2), lambda i, s: (i, 0))

    def body(s_ref, *refs):
        for a_ref, b_ref, o_ref in zip(refs[:nt], refs[nt:2 * nt], refs[2 * nt:]):
            o_ref[...] = (a_ref[...].astype(F32) + b_ref[...].astype(F32)).astype(BF16)

    specs = [_half_spec(kinds[t], shapes[t], tiles) for t in range(nt)]
    return pl.pallas_call(
        body, grid_spec=pltpu.PrefetchScalarGridSpec(num_scalar_prefetch=1, grid=(tiles,), in_specs=specs + specs,
                                                     out_specs=[compact(t)[1] for t in range(nt)]),
        out_shape=[_sds(compact(t)[0], BF16) for t in range(nt)], name=name,
        compiler_params=_params(("parallel",)))(where, *mine, *landed)


def sum_shards(name, parts, landed, kinds, shard_shapes, where, layers, n_layers, intos, tiles=2):
    nt = len(parts)
    in_specs, out_specs = [], []
    for t in range(nt):
        (r, n), layer = shard_shapes[t], layers[t]
        if kinds[t] == "col":
            tm, width = r // 2 // tiles, n
            own = pl.BlockSpec((tm, n), lambda i, s: (i, s[1]))
            out = pl.BlockSpec((None, tm, n), lambda i, s, layer=layer: (layer, s[0] * tiles + i, 0))
        else:
            tm, width = r // tiles, n // 2
            own = pl.BlockSpec((tm, n // 2), lambda i, s: (s[1] * tiles + i, 0))
            out = pl.BlockSpec((None, tm, n // 2), lambda i, s, layer=layer: (layer, i, s[0]))
        in_specs += [own, pl.BlockSpec((3, tm, width), lambda i, s: (0, i, 0))]
        out_specs.append(out)
    args, aliases = [where] + [a for pair in zip(parts, landed) for a in pair], {}
    for t in range(nt):
        if intos[t] is not None:
            aliases[len(args)] = t
            in_specs.append(pl.BlockSpec(memory_space=pl.ANY))
            args.append(intos[t])

    def body(s_ref, *refs):
        for t in range(nt):
            a_ref, l_ref, o_ref = refs[2 * t], refs[2 * t + 1], refs[len(in_specs) + t]
            o_ref[...] = ((a_ref[...].astype(F32) + l_ref[0].astype(F32)) + l_ref[1].astype(F32)) + l_ref[2].astype(F32)

    return pl.pallas_call(
        body, grid_spec=pltpu.PrefetchScalarGridSpec(num_scalar_prefetch=1, grid=(tiles,), in_specs=in_specs,
                                                     out_specs=out_specs),
        out_shape=[_sds((n_layers[t],) + tuple(shard_shapes[t]), F32) for t in range(nt)], input_output_aliases=aliases,
        name=name, compiler_params=_params(("parallel",)))(*args)


def share_halves(arrays, entries):
    na, nt = len(arrays), len(entries)

    def body(*refs):
        out_refs = refs[na:2 * na]
        send_sems, recv_sems = refs[2 * na:]
        x, y, c, _ = _position()
        cps = []
        for t, (a, layer, kind) in enumerate(entries):
            shape = tuple(arrays[a].shape[1:])
            mine = _half(out_refs[a].at[layer], kind, c, shape)
            cp = pltpu.make_async_remote_copy(
                src_ref=mine, dst_ref=mine, send_sem=send_sems.at[t], recv_sem=recv_sems.at[t],
                device_id=(x, y, 1 - c), device_id_type=MESH)
            cp.start()
            cps.append(cp)
        for t, (a, layer, kind) in enumerate(entries):
            shape = tuple(arrays[a].shape[1:])
            other = _half(out_refs[a].at[layer], kind, 1 - c, shape)
            pltpu.make_async_remote_copy(
                src_ref=other, dst_ref=other, send_sem=send_sems.at[t], recv_sem=recv_sems.at[t],
                device_id=(x, y, 1 - c), device_id_type=MESH).wait_recv()
        for cp in cps:
            cp.wait_send()

    hbm = pl.BlockSpec(memory_space=pl.ANY)
    return pl.pallas_call(
        body, in_specs=[hbm] * na, out_specs=[hbm] * na, out_shape=[_sds(a.shape, F32) for a in arrays],
        input_output_aliases={i: i for i in range(na)},
        scratch_shapes=[pltpu.SemaphoreType.DMA((nt,)), pltpu.SemaphoreType.DMA((nt,))],
        name="share_halves", compiler_params=_params())(*arrays)


HBM_SPEC = pl.BlockSpec(memory_space=pltpu.HBM)
SEM_SPEC = pl.BlockSpec(memory_space=pltpu.SEMAPHORE)
ANY_SPEC = pl.BlockSpec(memory_space=pl.ANY)


def _split_params():
    return pltpu.CompilerParams(has_side_effects=pltpu.SideEffectType.DATAFLOW_SIDE_EFFECTING,
                                vmem_limit_bytes=VMEM_LIMIT_BYTES)


def _in_hbm(a):
    return pltpu.with_memory_space_constraint(a, pltpu.HBM)


def cast_place(arrays, entries, where, tiles=2):
    in_specs, out_specs, fulls = [], [], []
    for a, layer, kind in entries:
        _, r, n = arrays[a].shape
        tm = r // tiles
        in_specs.append(pl.BlockSpec((None, tm, n), lambda i, s, layer=layer: (layer, i, 0)))
        if kind == "col":
            fulls.append((r, 4 * n))
            out_specs.append(pl.BlockSpec((tm, n), lambda i, s: (i, s[1])))
        else:
            fulls.append((4 * r, n))
            out_specs.append(pl.BlockSpec((tm, n), lambda i, s: (s[1] * tiles + i, 0)))
    nt = len(entries)

    def body(s_ref, *refs):
        for w_ref, o_ref in zip(refs[:nt], refs[nt:]):
            o_ref[...] = w_ref[...].astype(BF16)

    return pl.pallas_call(
        body, grid_spec=pltpu.PrefetchScalarGridSpec(num_scalar_prefetch=1, grid=(tiles,), in_specs=in_specs,
                                                     out_specs=out_specs),
        out_shape=[_sds(f, BF16) for f in fulls], name="cast_place",
        compiler_params=_params(("parallel",)))(where, *[arrays[a] for a, _, _ in entries])


def gather_start(fulls, kinds, shard_shapes, after):
    nt = len(fulls)
    na = 0 if after is None else 1

    def body(*refs):
        full_refs = refs[:nt]
        send_sems, recv_sems, token = refs[nt + na], refs[nt + na + 1], refs[-1]
        x, y, c, others = _position()
        for t in range(nt):
            mine = _window(full_refs[t], kinds[t], 2 * x + y, c, shard_shapes[t])
            for j, (ox, oy) in enumerate(others):
                pltpu.make_async_remote_copy(
                    src_ref=mine, dst_ref=mine, send_sem=send_sems.at[3 * t + j], recv_sem=recv_sems.at[3 * t + j],
                    device_id=(ox, oy, c), device_id_type=MESH).start()
        token[...] = jnp.zeros_like(token)

    sems = pltpu.SemaphoreType.DMA((3 * nt,))
    out = pl.pallas_call(
        body, name="gather_start", in_specs=[HBM_SPEC] * nt + [ANY_SPEC] * na,
        out_specs=(SEM_SPEC, SEM_SPEC, *[HBM_SPEC] * nt, pl.BlockSpec(memory_space=pltpu.VMEM)),
        out_shape=(sems, sems, *[pltpu.HBM(f.shape, f.dtype) for f in fulls], _sds((8, 128), F32)),
        input_output_aliases={t: 2 + t for t in range(nt)}, compiler_params=_split_params(),
    )(*[_in_hbm(f) for f in fulls], *([] if after is None else [after]))
    return out[0], out[1], list(out[2:2 + nt]), out[-1]


def gather_wait(name, send_sems, recv_sems, fulls, kinds, shard_shapes, after, first):
    nt = len(fulls)

    def body(*refs):
        full_refs, send_ref, recv_ref = refs[:nt], refs[nt], refs[nt + 1]
        x, y, c, others = _position()
        for t in range(nt):
            mine = _window(full_refs[t], kinds[t], 2 * x + y, c, shard_shapes[t])
            for j, (ox, oy) in enumerate(others):
                cp = pltpu.make_async_remote_copy(
                    src_ref=mine, dst_ref=_window(full_refs[t], kinds[t], 2 * ox + oy, c, shard_shapes[t]),
                    send_sem=send_ref.at[3 * (first + t) + j], recv_sem=recv_ref.at[3 * (first + t) + j],
                    device_id=(ox, oy, c), device_id_type=MESH)
                cp.wait_send()
                cp.wait_recv()

    out = pl.pallas_call(
        body, name=name, in_specs=[HBM_SPEC] * nt + [SEM_SPEC, SEM_SPEC, HBM_SPEC], out_specs=[HBM_SPEC] * nt,
        out_shape=[pltpu.HBM(f.shape, f.dtype) for f in fulls], input_output_aliases={t: t for t in range(nt)},
        compiler_params=_split_params())(*fulls, send_sems, recv_sems, _in_hbm(after))
    return list(out)


def forward_halves(name, fulls, kinds, shard_shapes):
    nt = len(fulls)

    def body(*refs):
        out_refs = refs[nt:2 * nt]
        send_sems, recv_sems = refs[2 * nt:]
        x, y, c, others = _position()
        cps = []
        for t in range(nt):
            for j, (ox, oy) in enumerate(others):
                landed = _window(out_refs[t], kinds[t], 2 * ox + oy, c, shard_shapes[t])
                cp = pltpu.make_async_remote_copy(
                    src_ref=landed, dst_ref=landed, send_sem=send_sems.at[3 * t + j], recv_sem=recv_sems.at[3 * t + j],
                    device_id=(x, y, 1 - c), device_id_type=MESH)
                cp.start()
                cps.append(cp)
        for t in range(nt):
            for j, (ox, oy) in enumerate(others):
                got = _window(out_refs[t], kinds[t], 2 * ox + oy, 1 - c, shard_shapes[t])
                pltpu.make_async_remote_copy(
                    src_ref=got, dst_ref=got, send_sem=send_sems.at[3 * t + j], recv_sem=recv_sems.at[3 * t + j],
                    device_id=(x, y, 1 - c), device_id_type=MESH).wait_recv()
        for cp in cps:
            cp.wait_send()

    out = pl.pallas_call(
        body, in_specs=[ANY_SPEC] * nt, out_specs=[ANY_SPEC] * nt, out_shape=[_sds(f.shape, f.dtype) for f in fulls],
        input_output_aliases={t: t for t in range(nt)},
        scratch_shapes=[pltpu.SemaphoreType.DMA((3 * nt,)), pltpu.SemaphoreType.DMA((3 * nt,))],
        name=name, compiler_params=_params())(*fulls)
    return list(out)


def _piece(ref, kind, chip, shard_shape):
    r, n = shard_shape
    if kind == "col":
        return ref.at[:, pl.ds(pl.multiple_of(chip * n, 128), n)]
    return ref.at[pl.ds(pl.multiple_of(chip * r, 16), r), :]


def _piece_shape(kind, shard_shape):
    r, n = shard_shape
    return (r // 2, n) if kind == "col" else (r, n // 2)


def exchange_start(name, parts, kinds, shard_shapes):
    nt = len(parts)
    lands = [lax.empty((3,) + _piece_shape(kinds[t], shard_shapes[t]), BF16) for t in range(nt)]

    def body(*refs):
        part_refs, land_refs = refs[:nt], refs[nt:2 * nt]
        send_sems, recv_sems, token = refs[2 * nt], refs[2 * nt + 1], refs[-1]
        x, y, c, others = _position()
        for t in range(nt):
            for j, (ox, oy) in enumerate(others):
                pltpu.make_async_remote_copy(
                    src_ref=_piece(part_refs[t], kinds[t], 2 * ox + oy, shard_shapes[t]), dst_ref=land_refs[t].at[j],
                    send_sem=send_sems.at[3 * t + j], recv_sem=recv_sems.at[3 * t + j],
                    device_id=(ox, oy, c), device_id_type=MESH).start()
        token[...] = jnp.zeros_like(token)

    sems = pltpu.SemaphoreType.DMA((3 * nt,))
    both = list(parts) + lands
    out = pl.pallas_call(
        body, name=name, in_specs=[HBM_SPEC] * (2 * nt),
        out_specs=(SEM_SPEC, SEM_SPEC, *[HBM_SPEC] * (2 * nt), pl.BlockSpec(memory_space=pltpu.VMEM)),
        out_shape=(sems, sems, *[pltpu.HBM(a.shape, a.dtype) for a in both], _sds((8, 128), F32)),
        input_output_aliases={t: 2 + t for t in range(2 * nt)}, compiler_params=_split_params(),
    )(*[_in_hbm(a) for a in both])
    return out[0], out[1], list(out[2:2 + nt]), list(out[2 + nt:2 + 2 * nt]), out[-1]


def exchange_wait(name, send_sems, recv_sems, parts, lands, kinds, shard_shapes, after):
    nt = len(parts)

    def body(*refs):
        part_refs, land_refs = refs[:nt], refs[nt:2 * nt]
        send_ref, recv_ref = refs[2 * nt], refs[2 * nt + 1]
        x, y, c, others = _position()
        for t in range(nt):
            for j, (ox, oy) in enumerate(others):
                cp = pltpu.make_async_remote_copy(
                    src_ref=_piece(part_refs[t], kinds[t], 2 * ox + oy, shard_shapes[t]), dst_ref=land_refs[t].at[j],
                    send_sem=send_ref.at[3 * t + j], recv_sem=recv_ref.at[3 * t + j],
                    device_id=(ox, oy, c), device_id_type=MESH)
                cp.wait_send()
                cp.wait_recv()

    both = list(parts) + list(lands)
    out = pl.pallas_call(
        body, name=name, in_specs=[HBM_SPEC] * (2 * nt) + [SEM_SPEC, SEM_SPEC, HBM_SPEC], out_specs=[HBM_SPEC] * (2 * nt),
        out_shape=[pltpu.HBM(a.shape, a.dtype) for a in both], input_output_aliases={t: t for t in range(2 * nt)},
        compiler_params=_split_params())(*both, send_sems, recv_sems, _in_hbm(after))
    return list(out[:nt]), list(out[nt:])


def all_reduce_small(name, bufs):
    n = len(bufs)
    halves = [b.shape[0] // 2 for b in bufs]

    def body(*refs):
        in_refs, out_refs, lands = refs[:n], refs[n:2 * n], refs[2 * n:3 * n]
        send_sems, recv_sems = refs[3 * n:]
        x, y, c, _ = _position()
        mine = [pl.ds(pl.multiple_of(c * h, 8), h) for h in halves]
        other = [pl.ds(pl.multiple_of((1 - c) * h, 8), h) for h in halves]
        for k in range(n):
            out_refs[k][mine[k], :] = in_refs[k][mine[k], :]
        for s, peer in enumerate([(x, y, 1 - c), (1 - x, y, c), (x, 1 - y, c)]):
            cps = []
            for k in range(n):
                src = in_refs[k].at[other[k]] if s == 0 else out_refs[k].at[mine[k]]
                cp = pltpu.make_async_remote_copy(
                    src_ref=src, dst_ref=lands[k].at[s], send_sem=send_sems.at[4 * k + s], recv_sem=recv_sems.at[4 * k + s],
                    device_id=peer, device_id_type=MESH)
                cp.start()
                cps.append(cp)
            for k, cp in enumerate(cps):
                cp.wait()
                out_refs[k][mine[k], :] = out_refs[k][mine[k], :] + lands[k][s]
        cps = []
        for k in range(n):
            cp = pltpu.make_async_remote_copy(
                src_ref=out_refs[k].at[mine[k]], dst_ref=out_refs[k].at[mine[k]], send_sem=send_sems.at[4 * k + 3],
                recv_sem=recv_sems.at[4 * k + 3], device_id=(x, y, 1 - c), device_id_type=MESH)
            cp.start()
            cps.append(cp)
        for cp in cps:
            cp.wait()

    vm = pl.BlockSpec(memory_space=pltpu.VMEM)
    out = pl.pallas_call(
        body, in_specs=[vm] * n, out_specs=[vm] * n, out_shape=[_sds(b.shape, F32) for b in bufs],
        scratch_shapes=[pltpu.VMEM((3, h, b.shape[1]), F32) for h, b in zip(halves, bufs)]
        + [pltpu.SemaphoreType.DMA((4 * n,)), pltpu.SemaphoreType.DMA((4 * n,))],
        name=name, compiler_params=_params())(*bufs)
    return list(out)


def _local_step(x, target, small, need, emit):
    d = D_MODEL
    full = {}

    def after_token(vec, token):
        return vec if token is None else vec + token[0:1, 0:1]

    lam_r, lam_i, bbar_re, bbar_im = small["s5_disc"]
    rb, rc = _s5_matrices(bbar_re, bbar_im, small["s5_c_re"], small["s5_c_im"])
    rb16, rc16 = rb.astype(BF16), rc.astype(BF16)
    lr_t, li_t = lam_r.reshape(S5_BLOCKS, 8, 128), lam_i.reshape(S5_BLOCKS, 8, 128)
    ge, y2, cs = s5_fwd(x, small["norm_mix0"], small["s5_d"], rb16, rc16, lr_t, li_t)
    full.update(need("glu", ge))

    def norm_rows(h, gains):
        xh, _ = _rms_hat(h)
        return [xh * g for g in gains]

    def glu_epilogue(accs, e, r):
        v, gt = accs[0] + r[0], accs[1] + r[1]
        h = e[0] + v * jax.nn.sigmoid(gt)
        return [h, v, gt] + norm_rows(h, r[2:])

    h1, val, gate, n1 = mm_nn(
        "glu", ge, full["w_glu"], [0, d], d, glu_epilogue, [F32, F32, F32, BF16], extras=[x],
        rowvecs=[(small["s5_b_glu"], 0), (small["s5_b_glu"], d), (small["norm_mlp0"], 0)], tm=512, tn=d)

    def mlp_fwd(tag, h, n, w_in, w_out, next_gains, head=None):
        def in_epilogue(accs, e, rv):
            pos = jnp.maximum(accs[0], 0.0)
            return [pos * pos, 2.0 * pos]

        r, slope = mm_nn("mlp_in" + tag, n, w_in, [0], w_in.shape[1], in_epilogue, [BF16, BF16], tm=2048)

        def epilogue(accs, e, rv):
            h_out = e[0] + accs[0]
            return [h_out] + norm_rows(h_out, rv)

        if head is not None:
            return head(r, w_out, h), (n, r, slope)
        outs = mm_nn("mlp_out" + tag, r, w_out, [0], d, epilogue, [F32] + [BF16] * len(next_gains), extras=[h],
                     rowvecs=[(g, 0) for g in next_gains], tm=512, tn=d)
        return outs[0], outs[1:], (n, r, slope)

    full.update(need("mlp0", h1))
    h2, (nkv, n2), mlp0 = mlp_fwd("0", h1, n1, full["w_in0"], full["w_out0"], [small["norm_kv"], small["norm_mix1"]])

    full.update(need("attn", h2))
    kvw = 2 * N_KV * HEAD_DIM
    (kv,) = mm_nn("kv_proj", nkv, full["w_kv"], [0], kvw, lambda accs, e, r: [accs[0] + r[0]], [BF16],
                  rowvecs=[(small["b_kv"], 0)], tm=2048)
    (q,) = mm_nn("q_proj", n2, full["w_q"], [0], d, lambda accs, e, r: [accs[0] + r[0]], [BF16],
                 rowvecs=[(small["b_q"], 0)], tm=2048)
    sinks = small["sinks"].reshape(N_Q)
    o = attn_fwd(q, kv, sinks)
    def o_epilogue(accs, e, r):
        h_out = e[0] + accs[0] + r[0]
        return [h_out] + norm_rows(h_out, r[1:])

    h3, n3 = mm_nn("o_proj", o, full["w_o"], [0], d, o_epilogue, [F32, BF16], extras=[h2],
                   rowvecs=[(small["b_o"], 0), (small["norm_mlp1"], 0)], tm=512, tn=d)
    full.update(need("mlp1", h3))

    def loss_head(r, w_out, h):
        def epilogue(accs, e, rv):
            xh, rr = _rms_hat(e[0] + accs[0])
            err = xh * rv[0] - e[1]
            dy = err * (1.0 / d)
            dxh = dy * rv[0]
            dx = rr * (dxh - xh * jnp.mean(dxh * xh, axis=-1, keepdims=True))
            loss = jnp.full((1, d), 0.5 * jnp.sum(jnp.mean(err * err, axis=-1, keepdims=True)), F32)
            return [dx, dx, loss, jnp.sum(dy * xh, axis=0, keepdims=True)]

        return mm_nn("mlp_out1", r, w_out, [0], d, epilogue, [F32, BF16], extras=[h, target],
                     rowvecs=[(small["norm_final"], 0)], n_sums=2, tm=512, tn=d)

    (dh, dhb, loss_tile, dg_final), mlp1 = mlp_fwd("1", h3, n3, full["w_in1"], full["w_out1"], [], head=loss_head)

    grads_small, grads_full = {"norm_final": dg_final}, {}
    ident = lambda acc, e, r: [acc]
    layer1 = ["w_out1", "w_in1", "w_o", "w_q", "w_kv"]
    layer0 = ["w_out0", "w_in0", "w_glu"]

    def norm_bwd_rows(x_rows, res, dys, gains):
        xh, r = _rms_hat(x_rows)
        dxh = sum(dy * g for dy, g in zip(dys, gains))
        dx = r * (dxh - xh * jnp.mean(dxh * xh, axis=-1, keepdims=True)) + res
        return dx, [jnp.sum(dy * xh, axis=0, keepdims=True) for dy in dys]

    def mlp_bwd(tag, dh, dhb, h_in, gain, w_in, w_out, saved):
        n, r, slope = saved
        grads_full["w_out" + tag] = mm_tn("dw_out" + tag, r, dhb, tn=1024)
        (da,) = mm_nt("mlp_da" + tag, dhb, w_out, lambda acc, e, rv: [acc * e[0].astype(F32)], [BF16], extras=[slope],
                      tm=2048)
        grads_full["w_in" + tag] = mm_tn("dw_in" + tag, n, da, tn=1024)

        def epilogue(acc, e, rv):
            dx, dgs = norm_bwd_rows(e[0], e[1], [acc], rv)
            return [dx, dx, jnp.sum(dx, axis=0, keepdims=True)] + dgs

        dx, dxb, colsum, dg = mm_nt("mlp_dn" + tag, da, w_in, epilogue, [F32, BF16], extras=[h_in, dh], rowvecs=[gain],
                                    n_sums=2, tm=512, tk=d)
        grads_small["norm_mlp" + tag] = dg
        return dx, dxb, colsum

    dh3, dh3b, colsum3 = mlp_bwd("1", dh, dhb, h3, small["norm_mlp1"], full["w_in1"], full["w_out1"], mlp1)
    grads_small["b_o"] = colsum3
    grads_full["w_o"] = mm_tn("dw_o", o, dh3b, tn=1024)
    (do,) = mm_nt("attn_do", dh3b, full["w_o"], ident, [BF16], tm=2048)
    dq, dbq, dprev, dcur, dsink = attn_bwd(q, kv, do, sinks)
    dkv, dbkv = kv_combine(dprev, dcur)
    grads_small["b_q"], grads_small["b_kv"], grads_small["sinks"] = dbq, dbkv, dsink
    grads_full["w_q"] = mm_tn("dw_q", n2, dq, tn=1024)
    grads_full["w_kv"] = mm_tn("dw_kv", nkv, dkv, tk=1024)
    (dnkv,) = mm_nt("kv_dn", dkv, full["w_kv"], ident, [F32], tm=2048, tk=1024)
    token = emit("layer1", {n: grads_full[n] for n in layer1})

    def attn_dn_epilogue(acc, e, rv):
        dx, dgs = norm_bwd_rows(e[0], e[1], [acc, e[2]], rv)
        return [dx, dx] + dgs

    dh2, dh2b, dg_mix1, dg_kv = mm_nt("attn_dn", dq, full["w_q"], attn_dn_epilogue, [F32, BF16], extras=[h2, dh3, dnkv],
                                      rowvecs=[after_token(small["norm_mix1"], token), small["norm_kv"]], n_sums=2,
                                      tm=512, tk=d)
    grads_small["norm_mix1"], grads_small["norm_kv"] = dg_mix1, dg_kv
    dh1, _, _ = mlp_bwd("0", dh2, dh2b, h1, small["norm_mlp0"], full["w_in0"], full["w_out0"], mlp0)

    dz, db_glu = glu_bwd(dh1, val, gate)
    grads_small["s5_b_glu"] = db_glu
    grads_full["w_glu"] = mm_tn("dw_glu", ge, dz, tn=1024)
    token = emit("layer0", {n: grads_full[n] for n in layer0})
    (dy2,) = mm_nt("glu_dy", dz, full["w_glu"], lambda acc, e, rv: [acc * _gelu_grad(e[0])], [F32], extras=[y2],
                   tm=1024, tk=1024)
    rbt16, rct16 = jnp.swapaxes(rb16, 1, 2), jnp.swapaxes(rc16, 1, 2)
    grad_x, dd, drb, drc, dlr, dli, dg_mix0 = s5_bwd(x, small["norm_mix0"], dy2, dh1, after_token(small["s5_d"], token), cs,
                                                     rb16, rbt16, rct16, lr_t, li_t)
    grads_small["s5_d"] = dd
    grads_small["s5_mats"] = (drb, drc, dlr, dli)
    grads_small["norm_mix0"] = dg_mix0
    return loss_tile, grad_x, grads_small


SMALL_NAMES = ["norm_mix", "norm_mlp", "norm_kv", "norm_final", "s5_a_re", "s5_a_im", "s5_log_dt", "s5_b_re", "s5_b_im",
               "s5_c_re", "s5_c_im", "s5_d", "s5_b_glu", "b_kv", "b_q", "sinks", "b_o"]
BIG_NAMES = ["s5_w_glu", "w_kv", "w_q", "w_o", "w_mlp_in", "w_mlp_out"]
WEIGHT_ORDER = ["norm_mix", "norm_mlp", "norm_kv", "norm_final", "s5_a_re", "s5_a_im", "s5_log_dt", "s5_b_re", "s5_b_im",
                "s5_c_re", "s5_c_im", "s5_d", "s5_w_glu", "s5_b_glu", "w_kv", "b_kv", "w_q", "b_q", "sinks", "w_o", "b_o",
                "w_mlp_in", "w_mlp_out"]


def kernel(x, norm_mix, norm_mlp, norm_kv, norm_final, s5_a_re, s5_a_im, s5_log_dt, s5_b_re, s5_b_im, s5_c_re, s5_c_im, s5_d, s5_w_glu, s5_b_glu, w_kv, b_kv, w_q, b_q, sinks, w_o, b_o, w_mlp_in, w_mlp_out, loss_target, m_norm_mix, m_norm_mlp, m_norm_kv, m_norm_final, m_s5_a_re, m_s5_a_im, m_s5_log_dt, m_s5_b_re, m_s5_b_im, m_s5_c_re, m_s5_c_im, m_s5_d, m_s5_w_glu, m_s5_b_glu, m_w_kv, m_b_kv, m_w_q, m_b_q, m_sinks, m_w_o, m_b_o, m_w_mlp_in, m_w_mlp_out, v_norm_mix, v_norm_mlp, v_norm_kv, v_norm_final, v_s5_a_re, v_s5_a_im, v_s5_log_dt, v_s5_b_re, v_s5_b_im, v_s5_c_re, v_s5_c_im, v_s5_d, v_s5_w_glu, v_s5_b_glu, v_w_kv, v_b_kv, v_w_q, v_b_q, v_sinks, v_w_o, v_b_o, v_w_mlp_in, v_w_mlp_out):
    env = dict(locals())
    w = {n: env[n] for n in WEIGHT_ORDER}
    mom = {n: env["m_" + n] for n in WEIGHT_ORDER}
    var = {n: env["v_" + n] for n in WEIGHT_ORDER}
    d = D_MODEL
    xi, yi, ci = lax.axis_index("x"), lax.axis_index("y"), lax.axis_index("c")
    chip = 2 * xi + yi
    where = jnp.stack([ci, chip]).astype(jnp.int32)

    dsh, bsh = s5_d.shape[1], s5_b_glu.shape[1]
    placed = jnp.concatenate([
        lax.dynamic_update_slice(jnp.zeros((4 * dsh,), F32), s5_d[0], (chip * dsh,)),
        lax.dynamic_update_slice(jnp.zeros((4 * bsh,), F32), s5_b_glu[0], (chip * bsh,))])
    placed = jnp.pad(placed, (0, (-placed.shape[0]) % 2048))
    placed = jnp.where(ci == 0, placed, 0.0).reshape(-1, 128)
    (gathered_rows,) = all_reduce_small("gather_vectors", [placed])
    gathered = gathered_rows.reshape(-1)
    d_full, bglu_full = gathered[:4 * dsh].reshape(1, -1), gathered[4 * dsh:].reshape(1, -1)

    big = [s5_w_glu, w_kv[None], w_q, w_o, w_mlp_in, w_mlp_out]
    entries = [(0, 0, "col"), (1, 0, "row"), (2, 0, "row"), (3, 0, "row"), (4, 0, "col"), (4, 1, "col"),
               (5, 0, "row"), (5, 1, "row")]
    names = ["w_glu", "w_kv", "w_q", "w_o", "w_in0", "w_in1", "w_out0", "w_out1"]
    kinds = dict(zip(names, [k for _, _, k in entries]))
    shard_shapes = dict(zip(names, [tuple(big[a].shape[1:]) for a, _, _ in entries]))

    placed_w = dict(zip(names, cast_place(big, entries, where)))
    gather_groups = {"glu": ["w_glu"], "mlp0": ["w_in0", "w_out0"], "attn": ["w_kv", "w_q", "w_o"],
                     "mlp1": ["w_in1", "w_out1"]}
    order = [n for members in gather_groups.values() for n in members]
    send, recv, thru, token = gather_start([placed_w[n] for n in order], [kinds[n] for n in order],
                                           [shard_shapes[n] for n in order], gathered_rows)
    started = dict(zip(order, thru))

    def need(group, after):
        members = gather_groups[group]
        ks, shapes = [kinds[n] for n in members], [shard_shapes[n] for n in members]
        landed = gather_wait("gather_wait_" + group, send, recv, [started[n] for n in members], ks, shapes, after,
                             order.index(members[0]))
        return dict(zip(members, forward_halves("forward_halves_" + group, landed, ks, shapes)))

    exchanging = {}

    def emit(group, partial):
        members = list(partial)
        ks, shapes = [kinds[n] for n in members], [shard_shapes[n] for n in members]
        landed = swap_halves("swap_halves_" + group, [partial[n] for n in members], ks)
        sums = add_halves("add_halves_" + group, [partial[n] for n in members], landed, ks, where)
        send, recv, parts, lands, tok = exchange_start("exchange_start_" + group, sums, ks, shapes)
        exchanging[group] = (members, send, recv, parts, lands)
        return tok

    disc = lambda *p: _s5_discretise(p[0], p[1], p[2], p[3], p[4])
    disc_args = (s5_a_re[0], s5_a_im[0], s5_log_dt[0], s5_b_re[0], s5_b_im[0])
    disc_out, disc_vjp = jax.vjp(disc, *disc_args)
    small = {
        "norm_mix0": norm_mix[0:1] + token[0:1, 0:1], "norm_mix1": norm_mix[1:2], "norm_mlp0": norm_mlp[0:1], "norm_mlp1": norm_mlp[1:2],
        "norm_kv": norm_kv.reshape(1, d), "norm_final": norm_final.reshape(1, d), "s5_disc": disc_out,
        "s5_c_re": s5_c_re[0], "s5_c_im": s5_c_im[0], "s5_d": d_full, "s5_b_glu": bglu_full,
        "b_kv": b_kv.reshape(1, -1), "b_q": b_q, "sinks": sinks, "b_o": b_o,
    }
    loss_row, grad_x, gs = _local_step(x[0], loss_target[0], small, need, emit)

    mats = s5_compact(*gs["s5_mats"])
    rows = [gs["norm_mix0"], gs["norm_mix1"], gs["norm_mlp0"], gs["norm_mlp1"], gs["norm_kv"], gs["norm_final"], gs["s5_d"],
            gs["b_q"], gs["b_o"], gs["s5_b_glu"], gs["b_kv"], gs["sinks"], loss_row, jnp.zeros((2, d), F32)]
    vecs, mats = all_reduce_small("reduce_small", [jnp.concatenate(rows, axis=0), mats])
    grads = split_vectors(where, vecs, dsh, bsh)
    loss = grads.pop("loss")[0, 0]
    dbbar_re, dbbar_im, dc_re, dc_im, dlr, dli = _s5_unpack(mats)
    g_are, g_aim, g_dt, g_bre, g_bim = disc_vjp((dlr, dli, dbbar_re, dbbar_im))
    grads.update({"s5_a_re": g_are[None], "s5_a_im": g_aim[None], "s5_log_dt": g_dt[None], "s5_b_re": g_bre[None],
                  "s5_b_im": g_bim[None], "s5_c_re": dc_re[None], "s5_c_im": dc_im[None]})

    reduced = [None] * len(big)
    where_of = dict(zip(names, entries))
    for group, after in (("layer1", grad_x), ("layer0", mats)):
        members, send, recv, parts, lands = exchanging[group]
        ks, shapes = [kinds[n] for n in members], [shard_shapes[n] for n in members]
        parts, lands = exchange_wait("exchange_wait_" + group, send, recv, parts, lands, ks, shapes, after)
        targets = [where_of[n][0] for n in members]
        sums = sum_shards("sum_shards_" + group, parts, lands, ks, shapes, where, [where_of[n][1] for n in members],
                          [big[a].shape[0] for a in targets], [reduced[a] for a in targets])
        for a, arr in zip(targets, sums):
            reduced[a] = arr
    reduced = share_halves(reduced, entries)
    for n, g in zip(BIG_NAMES, reduced):
        grads[n] = g.reshape(w[n].shape)

    delta, new_m, new_v = {}, {}, {}
    for n in BIG_NAMES:
        flat = lambda a: a.reshape(-1, a.shape[-1])
        go, dl, nm, nv = adamw("adamw_" + n, flat(w[n]), flat(grads[n]), flat(mom[n]), flat(var[n]))
        grads[n], delta[n], new_m[n], new_v[n] = (t.reshape(w[n].shape) for t in (go, dl, nm, nv))

    def view(n, a):
        return a.reshape(1, -1) if a.ndim == 1 else jnp.swapaxes(a, -1, -2) if n in ("s5_b_re", "s5_b_im") else a

    sw, sg, sm, sv = ([view(n, t[n]) for n in SMALL_NAMES] for t in (w, grads, mom, var))
    for n, a, b, c_ in zip(SMALL_NAMES, *adamw_native("adamw_small", sw, sg, sm, sv)):
        delta[n], new_m[n], new_v[n] = (view(n, t) if t.ndim == 4 else t for t in (a, b, c_))

    out = [loss.reshape(()), grad_x[None]]
    for table in (grads, delta, new_m, new_v):
        out += [table[n].reshape(w[n].shape) for n in WEIGHT_ORDER]
    return tuple(out)
```

```python
import math

import jax
import jax.numpy as jnp
from jax import lax
from jax.experimental import pallas as pl
from jax.experimental.pallas import tpu as pltpu

F32 = jnp.float32
BF16 = jnp.bfloat16

D_MODEL = 1024
S5_GROUPS = 64
S5_GROUP = 16
S5_STATE = 64
N_KV = 4
N_Q = 16
HEAD_DIM = 64
BLOCK = 128
NORM_EPS = 1e-5
LAMBDA_RE_MAX = -1e-4
ADAM_LR, ADAM_B1, ADAM_B2, ADAM_EPS, ADAM_WD, ADAM_STEP = 0.001, 0.9, 0.999, 1e-08, 0.01, 10

VMEM_LIMIT_BYTES = 56 * 1024 * 1024
S5_CHUNK = 256
S5_BLOCKS = 4
MESH = pl.DeviceIdType.MESH


def _params(sem=None):
    return pltpu.CompilerParams(dimension_semantics=sem, vmem_limit_bytes=VMEM_LIMIT_BYTES)


def _sds(shape, dtype):
    return jax.ShapeDtypeStruct(shape, dtype)


def _rms_hat(xv):
    r = lax.rsqrt(jnp.mean(xv * xv, axis=-1, keepdims=True) + NORM_EPS)
    return xv * r, r


def mm_nn(name, a, w, col_offsets, n_out, epilogue, out_dtypes, extras=(), rowvecs=(), n_sums=0, tm=1024, tn=512):
    m, k = a.shape
    tm, tn = min(tm, m), min(tn, n_out)
    nw, ne, nr, no = len(col_offsets), len(extras), len(rowvecs), len(out_dtypes)

    def body(a_ref, *refs):
        w_refs, e_refs, r_refs = refs[:nw], refs[nw:nw + ne], refs[nw + ne:nw + ne + nr]
        o_refs, s_refs = refs[nw + ne + nr:nw + ne + nr + no], refs[nw + ne + nr + no:]
        av = a_ref[...]
        accs = [jnp.dot(av, w_ref[...], preferred_element_type=F32) for w_ref in w_refs]
        outs = epilogue(accs, [e[...] for e in e_refs], [r[...] for r in r_refs])
        for o_ref, o in zip(o_refs, outs[:no]):
            o_ref[...] = o.astype(o_ref.dtype)
        if n_sums:
            @pl.when(pl.program_id(1) == 0)
            def _():
                for s_ref in s_refs:
                    s_ref[...] = jnp.zeros_like(s_ref)

            for s_ref, val in zip(s_refs, outs[no:]):
                s_ref[...] += val

    def wspec(off):
        return pl.BlockSpec((k, tn), lambda j, i, off=off: (0, off // tn + j))

    def rspec(off):
        return pl.BlockSpec((1, tn), lambda j, i, off=off: (0, off // tn + j))

    tile = pl.BlockSpec((tm, tn), lambda j, i: (i, j))
    in_specs = ([pl.BlockSpec((tm, k), lambda j, i: (i, 0))] + [wspec(o) for o in col_offsets]
                + [tile] * ne + [rspec(o) for _, o in rowvecs])
    sem = ("parallel", "arbitrary") if n_sums else ("parallel", "parallel")
    return pl.pallas_call(
        body, grid=(n_out // tn, m // tm), in_specs=in_specs,
        out_specs=[tile] * no + [pl.BlockSpec((1, tn), lambda j, i: (0, j))] * n_sums,
        out_shape=[_sds((m, n_out), dt) for dt in out_dtypes] + [_sds((1, n_out), F32)] * n_sums, name=name,
        compiler_params=_params(sem))(a, *([w] * nw), *extras, *[r for r, _ in rowvecs])


def mm_nt(name, g, w, epilogue, out_dtypes, extras=(), rowvecs=(), n_sums=0, tm=512, tk=512):
    m, n = g.shape
    k = w.shape[0]
    tm, tk = min(tm, m), min(tk, k)
    ne, nr, no = len(extras), len(rowvecs), len(out_dtypes)

    def body(g_ref, w_ref, *refs):
        e_refs, r_refs, o_refs, s_refs = refs[:ne], refs[ne:ne + nr], refs[ne + nr:ne + nr + no], refs[ne + nr + no:]
        acc = lax.dot_general(g_ref[...], w_ref[...], (((1,), (1,)), ((), ())), preferred_element_type=F32)
        outs = epilogue(acc, [e[...] for e in e_refs], [r[...] for r in r_refs])
        for o_ref, o in zip(o_refs, outs[:no]):
            o_ref[...] = o.astype(o_ref.dtype)
        if n_sums:
            @pl.when(pl.program_id(0) == 0)
            def _():
                for s_ref in s_refs:
                    s_ref[...] = jnp.zeros_like(s_ref)

            for s_ref, val in zip(s_refs, outs[no:]):
                s_ref[...] += val

    tile = pl.BlockSpec((tm, tk), lambda i, j: (i, j))
    vec = pl.BlockSpec((1, tk), lambda i, j: (0, j))
    sem = ("arbitrary", "parallel") if n_sums else ("parallel", "parallel")
    return pl.pallas_call(
        body, grid=(m // tm, k // tk),
        in_specs=[pl.BlockSpec((tm, n), lambda i, j: (i, 0)), pl.BlockSpec((tk, n), lambda i, j: (j, 0))]
        + [tile] * ne + [vec] * nr,
        out_specs=[tile] * no + [vec] * n_sums,
        out_shape=[_sds((m, k), dt) for dt in out_dtypes] + [_sds((1, k), F32)] * n_sums, name=name,
        compiler_params=_params(sem))(g, w, *extras, *rowvecs)


def mm_tn(name, a, g, tk=512, tn=512):
    m, k = a.shape
    n = g.shape[1]
    tk, tn = min(tk, k), min(tn, n)

    def body(a_ref, g_ref, o_ref):
        acc = lax.dot_general(a_ref[...], g_ref[...], (((0,), (0,)), ((), ())), preferred_element_type=F32)
        o_ref[...] = acc.astype(o_ref.dtype)

    return pl.pallas_call(
        body, grid=(k // tk, n // tn),
        in_specs=[pl.BlockSpec((m, tk), lambda i, j: (0, i)), pl.BlockSpec((m, tn), lambda i, j: (0, j))],
        out_specs=pl.BlockSpec((tk, tn), lambda i, j: (i, j)), out_shape=_sds((k, n), BF16), name=name,
        compiler_params=_params(("parallel", "parallel")))(a, g)


def _row_mask(tc):
    row = lax.broadcasted_iota(jnp.int32, (8 * tc, 256), 0) % 8
    col = lax.broadcasted_iota(jnp.int32, (8 * tc, 256), 1) // 32
    return row == col


def _expand_rows(expand_ref, val, mask):
    rep = jnp.dot(expand_ref[...], val.astype(BF16), preferred_element_type=F32)
    return jnp.where(mask, rep, 0.0).astype(BF16)


def _stage(ref, val):
    ref[0] = val[:, 0:128]
    ref[1] = val[:, 128:256]


def _gather_rows(src_ref, tc):
    halves = []
    for half in range(2):
        col = lax.broadcasted_iota(jnp.int32, (tc, 128), 1) // 32 + 4 * half
        out = jnp.zeros((tc, 128), F32)
        for s8 in range(4 * half, 4 * half + 4):
            out = jnp.where(col == s8, src_ref.at[half][pl.ds(s8, tc, stride=8), :], out)
        halves.append(out)
    return jnp.concatenate(halves, axis=1)


def _gelu(x):
    c = math.sqrt(2.0 / math.pi)
    return 0.5 * x * (1.0 + jnp.tanh(c * (x + 0.044715 * x * x * x)))


def _gelu_grad(x):
    c = math.sqrt(2.0 / math.pi)
    t = jnp.tanh(c * (x + 0.044715 * x * x * x))
    return 0.5 * (1.0 + t) + 0.5 * x * (1.0 - t * t) * c * (1.0 + 3.0 * 0.044715 * x * x)


def _expansion(tc):
    return (jnp.arange(8 * tc)[:, None] // 8 == jnp.arange(tc)[None, :]).astype(BF16)


def s5_fwd(x, gain, d_skip, rb, rc, lam_r, lam_i):
    n_rows = x.shape[0]
    tc = min(S5_CHUNK, n_rows)
    nc = n_rows // tc

    def body(x_ref, g_ref, d_ref, ex_ref, rb_ref, rc_ref, lr_ref, li_ref, ge_ref, y2_ref, cs_ref, bux, yrows, carry):
        i = pl.program_id(0)
        u = _rms_hat(x_ref[...])[0] * g_ref[...]

        @pl.when(i == 0)
        def _():
            carry[...] = jnp.zeros_like(carry)

        cs_ref[0] = carry[...]
        mask = _row_mask(tc)
        for blk in range(S5_BLOCKS):
            lhs = _expand_rows(ex_ref, u[:, blk * 256:(blk + 1) * 256], mask)
            bux[blk] = jnp.dot(lhs, rb_ref[blk], preferred_element_type=F32)
        lam = [(lr_ref[blk], li_ref[blk]) for blk in range(S5_BLOCKS)]

        def step(t, c):
            r0 = pl.multiple_of(t * 8, 8)
            new = []
            for blk in range(S5_BLOCKS):
                xr, xi = c[2 * blk], c[2 * blk + 1]
                lr, li = lam[blk]
                nr = lr * xr - li * xi + bux[blk, pl.ds(r0, 8), 0:128]
                ni = lr * xi + li * xr + bux[blk, pl.ds(r0, 8), 128:256]
                bux[blk, pl.ds(r0, 8), 0:128] = nr
                bux[blk, pl.ds(r0, 8), 128:256] = ni
                new += [nr, ni]
            return tuple(new)

        c0 = []
        for blk in range(S5_BLOCKS):
            c0 += [carry[blk, :, 0:128], carry[blk, :, 128:256]]
        cn = lax.fori_loop(0, tc, step, tuple(c0), unroll=4)
        for blk in range(S5_BLOCKS):
            carry[blk, :, 0:128] = cn[2 * blk]
            carry[blk, :, 128:256] = cn[2 * blk + 1]
        for blk in range(S5_BLOCKS):
            _stage(yrows, jnp.dot(bux[blk].astype(BF16), rc_ref[blk], preferred_element_type=F32))
            sl = slice(blk * 256, (blk + 1) * 256)
            y2 = _gather_rows(yrows, tc) + d_ref[:, sl] * u[:, sl]
            y2_ref[:, sl] = y2
            ge_ref[:, sl] = _gelu(y2).astype(BF16)

    row = pl.BlockSpec((tc, D_MODEL), lambda i: (i, 0))
    vec = pl.BlockSpec((1, D_MODEL), lambda i: (0, 0))
    mat = pl.BlockSpec((S5_BLOCKS, 256, 256), lambda i: (0, 0, 0))
    lamspec = pl.BlockSpec((S5_BLOCKS, 8, 128), lambda i: (0, 0, 0))
    return pl.pallas_call(
        body, grid=(nc,),
        in_specs=[row, vec, vec, pl.BlockSpec((8 * tc, tc), lambda i: (0, 0)), mat, mat, lamspec, lamspec],
        out_specs=[row, row, pl.BlockSpec((1, S5_BLOCKS, 8, 256), lambda i: (i, 0, 0, 0))],
        out_shape=[_sds((n_rows, D_MODEL), BF16), _sds((n_rows, D_MODEL), F32), _sds((nc, S5_BLOCKS, 8, 256), F32)],
        scratch_shapes=[pltpu.VMEM((S5_BLOCKS, 8 * tc, 256), F32), pltpu.VMEM((2, 8 * tc, 128), F32),
                        pltpu.VMEM((S5_BLOCKS, 8, 256), F32)],
        name="s5_fwd", compiler_params=_params(("arbitrary",)))(x, gain, d_skip, _expansion(tc), rb, rc, lam_r, lam_i)


def s5_bwd(x, gain, dy2, res, d_skip, cs, rb, rbt, rct, lam_r, lam_i):
    n_rows = x.shape[0]
    tc = min(S5_CHUNK, n_rows)
    nc = n_rows // tc

    def body(x_ref, g_ref, dy_ref, res_ref, d_ref, cs_ref, ex_ref, rb_ref, rbt_ref, rct_ref, lr_ref, li_ref,
             dx_ref, dd_ref, drb_ref, drc_ref, dlr_ref, dli_ref, dg_ref, tmp, du, lhsu, lhsd, xs, adj, acarry):
        i = pl.program_id(0)
        u = _rms_hat(x_ref[...])[0] * g_ref[...]

        @pl.when(i == 0)
        def _():
            acarry[...] = jnp.zeros_like(acarry)
            dd_ref[...] = jnp.zeros_like(dd_ref)
            drb_ref[...] = jnp.zeros_like(drb_ref)
            drc_ref[...] = jnp.zeros_like(drc_ref)
            dlr_ref[...] = jnp.zeros_like(dlr_ref)
            dli_ref[...] = jnp.zeros_like(dli_ref)
            dg_ref[...] = jnp.zeros_like(dg_ref)

        dd_ref[...] += jnp.sum(dy_ref[...] * u, axis=0, keepdims=True)
        mask = _row_mask(tc)
        for blk in range(S5_BLOCKS):
            sl = slice(blk * 256, (blk + 1) * 256)
            lhsu[blk] = _expand_rows(ex_ref, u[:, sl], mask)
            xs[blk] = jnp.dot(lhsu[blk], rb_ref[blk], preferred_element_type=F32)
            lhsd[blk] = _expand_rows(ex_ref, dy_ref[:, sl], mask)
            adj[blk] = jnp.dot(lhsd[blk], rct_ref[blk], preferred_element_type=F32)
        lam = [(lr_ref[blk], li_ref[blk]) for blk in range(S5_BLOCKS)]

        def fstep(t, c):
            r0 = pl.multiple_of(t * 8, 8)
            new = []
            for blk in range(S5_BLOCKS):
                xr, xi = c[2 * blk], c[2 * blk + 1]
                lr, li = lam[blk]
                nr = lr * xr - li * xi + xs[blk, pl.ds(r0, 8), 0:128]
                ni = lr * xi + li * xr + xs[blk, pl.ds(r0, 8), 128:256]
                xs[blk, pl.ds(r0, 8), 0:128] = nr
                xs[blk, pl.ds(r0, 8), 128:256] = ni
                new += [nr, ni]
            return tuple(new)

        c0 = []
        for blk in range(S5_BLOCKS):
            c0 += [cs_ref[0, blk, :, 0:128], cs_ref[0, blk, :, 128:256]]
        lax.fori_loop(0, tc, fstep, tuple(c0), unroll=4)

        def bstep(k, c):
            t = tc - 1 - k
            r0 = pl.multiple_of(t * 8, 8)
            rp = pl.multiple_of(jnp.maximum(t - 1, 0) * 8, 8)
            first = t == 0
            new_a, new_g = [], []
            for blk in range(S5_BLOCKS):
                ar, ai = c[0][2 * blk], c[0][2 * blk + 1]
                glr, gli = c[1][2 * blk], c[1][2 * blk + 1]
                lr, li = lam[blk]
                nr = lr * ar + li * ai + adj[blk, pl.ds(r0, 8), 0:128]
                ni = lr * ai - li * ar + adj[blk, pl.ds(r0, 8), 128:256]
                adj[blk, pl.ds(r0, 8), 0:128] = nr
                adj[blk, pl.ds(r0, 8), 128:256] = ni
                pr = jnp.where(first, cs_ref[0, blk, :, 0:128], xs[blk, pl.ds(rp, 8), 0:128])
                pi = jnp.where(first, cs_ref[0, blk, :, 128:256], xs[blk, pl.ds(rp, 8), 128:256])
                new_a += [nr, ni]
                new_g += [glr + nr * pr + ni * pi, gli + ni * pr - nr * pi]
            return tuple(new_a), tuple(new_g)

        a0, g0 = [], []
        for blk in range(S5_BLOCKS):
            a0 += [acarry[blk, :, 0:128], acarry[blk, :, 128:256]]
            g0 += [dlr_ref[blk], dli_ref[blk]]
        an, gn = lax.fori_loop(0, tc, bstep, (tuple(a0), tuple(g0)), unroll=2)
        for blk in range(S5_BLOCKS):
            acarry[blk, :, 0:128] = an[2 * blk]
            acarry[blk, :, 128:256] = an[2 * blk + 1]
            dlr_ref[blk] = gn[2 * blk]
            dli_ref[blk] = gn[2 * blk + 1]
        for blk in range(S5_BLOCKS):
            sl = slice(blk * 256, (blk + 1) * 256)
            ab = adj[blk].astype(BF16)
            _stage(tmp, jnp.dot(ab, rbt_ref[blk], preferred_element_type=F32))
            du[:, sl] = _gather_rows(tmp, tc) + d_ref[:, sl] * dy_ref[:, sl]
            drb_ref[blk] += lax.dot_general(lhsu[blk], ab, (((0,), (0,)), ((), ())), preferred_element_type=F32)
            drc_ref[blk] += lax.dot_general(lhsd[blk], xs[blk].astype(BF16), (((0,), (0,)), ((), ())),
                                            preferred_element_type=F32)
        xh, r = _rms_hat(x_ref[...])
        dg_ref[...] += jnp.sum(du[...] * xh, axis=0, keepdims=True)
        dxh = du[...] * g_ref[...]
        dx_ref[...] = r * (dxh - xh * jnp.mean(dxh * xh, axis=-1, keepdims=True)) + res_ref[...]

    rev = pl.BlockSpec((tc, D_MODEL), lambda i: (nc - 1 - i, 0))
    vec = pl.BlockSpec((1, D_MODEL), lambda i: (0, 0))
    mat = pl.BlockSpec((S5_BLOCKS, 256, 256), lambda i: (0, 0, 0))
    lamspec = pl.BlockSpec((S5_BLOCKS, 8, 128), lambda i: (0, 0, 0))
    big = pltpu.VMEM((S5_BLOCKS, 8 * tc, 256), F32)
    bigb = pltpu.VMEM((S5_BLOCKS, 8 * tc, 256), BF16)
    return pl.pallas_call(
        body, grid=(nc,),
        in_specs=[rev, vec, rev, rev, vec, pl.BlockSpec((1, S5_BLOCKS, 8, 256), lambda i: (nc - 1 - i, 0, 0, 0)),
                  pl.BlockSpec((8 * tc, tc), lambda i: (0, 0)), mat, mat, mat, lamspec, lamspec],
        out_specs=[rev, vec, mat, mat, lamspec, lamspec, vec],
        out_shape=[_sds((n_rows, D_MODEL), F32), _sds((1, D_MODEL), F32), _sds((S5_BLOCKS, 256, 256), F32),
                   _sds((S5_BLOCKS, 256, 256), F32), _sds((S5_BLOCKS, 8, 128), F32), _sds((S5_BLOCKS, 8, 128), F32),
                   _sds((1, D_MODEL), F32)],
        scratch_shapes=[pltpu.VMEM((2, 8 * tc, 128), F32), pltpu.VMEM((tc, D_MODEL), F32), bigb, bigb, big, big,
                        pltpu.VMEM((S5_BLOCKS, 8, 256), F32)],
        name="s5_bwd", compiler_params=_params(("arbitrary",)))(
            x, gain, dy2, res, d_skip, cs, _expansion(tc), rb, rbt, rct, lam_r, lam_i)


def _s5_views(a_re, a_im, log_dt, b_re, b_im):
    return a_re[:, None, :], a_im[:, None, :], log_dt[:, None, None], jnp.swapaxes(b_re, 1, 2), jnp.swapaxes(b_im, 1, 2)


def _s5_factors(a_re, a_im, log_dt):
    lr, li, dt = jnp.minimum(a_re, LAMBDA_RE_MAX), a_im, jnp.exp(log_dt)
    mag, ang = jnp.exp(lr * dt), li * dt
    lbr, lbi = mag * jnp.cos(ang), mag * jnp.sin(ang)
    den = lr * lr + li * li
    fr, fi = ((lbr - 1.0) * lr + lbi * li) / den, (lbi * lr - (lbr - 1.0) * li) / den
    return lr, li, dt, lbr, lbi, fr, fi, den


def s5_prep(a_re, a_im, log_dt, b_re, b_im, c_re, c_im):
    def body(ar_ref, ai_ref, t_ref, br_ref, bi_ref, cr_ref, ci_ref, rb_ref, rbt_ref, rc_ref, rct_ref, lr_ref, li_ref):
        _, _, _, lbr, lbi, fr, fi, _ = _s5_factors(ar_ref[...], ai_ref[...], t_ref[...])
        lr_ref[...] = lbr
        li_ref[...] = lbi
        bre = fr * br_ref[...] - fi * bi_ref[...]
        bim = fr * bi_ref[...] + fi * br_ref[...]
        even = (lax.broadcasted_iota(jnp.int32, (256, S5_STATE), 0) // S5_GROUP) % 2 == 0

        def assemble(re, im):
            re, im = re.reshape(256, S5_STATE), im.reshape(256, S5_STATE)
            return jnp.concatenate([jnp.where(even, re, 0.0), jnp.where(even, 0.0, re), jnp.where(even, im, 0.0),
                                    jnp.where(even, 0.0, im)], axis=1)

        for blk in range(S5_BLOCKS):
            sl = slice(16 * blk, 16 * blk + 16)
            rb = assemble(bre[sl], bim[sl])
            rct = assemble(cr_ref[sl], -ci_ref[sl])
            rb_ref[blk] = rb.astype(BF16)
            rbt_ref[blk] = rb.T.astype(BF16)
            rct_ref[blk] = rct.astype(BF16)
            rc_ref[blk] = rct.T.astype(BF16)

    vm = pl.BlockSpec(memory_space=pltpu.VMEM)
    mat = _sds((S5_BLOCKS, 256, 256), BF16)
    lam = _sds((S5_GROUPS, 1, S5_STATE), F32)
    rb, rbt, rc, rct, lam_r, lam_i = pl.pallas_call(
        body, in_specs=[vm] * 7, out_specs=[vm] * 6, out_shape=[mat, mat, mat, mat, lam, lam], name="s5_prep",
        compiler_params=_params())(*_s5_views(a_re, a_im, log_dt, b_re, b_im), c_re, c_im)
    return rb, rbt, rc, rct, lam_r.reshape(S5_BLOCKS, 8, 128), lam_i.reshape(S5_BLOCKS, 8, 128)


def s5_param_bwd(mats, a_re, a_im, log_dt, b_re, b_im):
    def body(m_ref, glr_ref, gli_ref, ar_ref, ai_ref, t_ref, br_ref, bi_ref,
             dar_ref, dai_ref, dt_ref, dbr_ref, dbi_ref, dcr_ref, dci_ref):
        lr, li, dt, lbr, lbi, fr, fi, den = _s5_factors(ar_ref[...], ai_ref[...], t_ref[...])
        shape = (S5_GROUPS, S5_GROUP, S5_STATE)
        gbr, gbi = m_ref[0:1024, 0:64].reshape(shape), m_ref[0:1024, 64:128].reshape(shape)
        dcr_ref[...] = m_ref[1024:2048, 0:64].reshape(shape)
        dci_ref[...] = -m_ref[1024:2048, 64:128].reshape(shape)
        br, bi = br_ref[...], bi_ref[...]
        dbr_ref[...] = fr * gbr + fi * gbi
        dbi_ref[...] = fr * gbi - fi * gbr
        dfr = jnp.sum(gbr * br + gbi * bi, axis=1, keepdims=True)
        dfi = jnp.sum(gbi * br - gbr * bi, axis=1, keepdims=True)
        nr, ni = (dfr * lr - dfi * li) / den, (dfr * li + dfi * lr) / den
        qr, qi = (fr * lr + fi * li) / den, (fi * lr - fr * li) / den
        lam_r, lam_i = -(dfr * qr + dfi * qi), -(dfi * qr - dfr * qi)
        gr, gi = glr_ref[...] + nr, gli_ref[...] + ni
        zr, zi = gr * lbr + gi * lbi, gi * lbr - gr * lbi
        a = ar_ref[...]
        dar_ref[...] = (lam_r + zr * dt) * jnp.where(a < LAMBDA_RE_MAX, 1.0, jnp.where(a == LAMBDA_RE_MAX, 0.5, 0.0))
        dai_ref[...] = lam_i + zi * dt
        dt_ref[...] = jnp.sum(zr * lr + zi * li, axis=2, keepdims=True) * dt

    vm = pl.BlockSpec(memory_space=pltpu.VMEM)
    state = _sds((S5_GROUPS, 1, S5_STATE), F32)
    wide = _sds((S5_GROUPS, S5_GROUP, S5_STATE), F32)
    glr = mats[2048:2080].reshape(S5_GROUPS, 1, S5_STATE)
    gli = mats[2080:2112].reshape(S5_GROUPS, 1, S5_STATE)
    dar, dai, ddt, dbr, dbi, dcr, dci = pl.pallas_call(
        body, in_specs=[vm] * 8, out_specs=[vm] * 7,
        out_shape=[state, state, _sds((S5_GROUPS, 1, 1), F32), wide, wide, wide, wide], name="s5_param_bwd",
        compiler_params=_params())(mats, glr, gli, *_s5_views(a_re, a_im, log_dt, b_re, b_im))
    return (dar.reshape(S5_GROUPS, S5_STATE), dai.reshape(S5_GROUPS, S5_STATE), ddt.reshape(S5_GROUPS),
            jnp.swapaxes(dbr, 1, 2), jnp.swapaxes(dbi, 1, 2), dcr, dci)


def s5_compact(drb, drct, dlr, dli):
    def body(drb_ref, drct_ref, dlr_ref, dli_ref, o_ref):
        even = (lax.broadcasted_iota(jnp.int32, (256, 64), 0) // S5_GROUP) % 2 == 0
        for blk in range(S5_BLOCKS):
            for k, ref in enumerate((drb_ref, drct_ref)):
                m = ref[blk]
                re = jnp.where(even, m[:, 0:64], m[:, 64:128])
                im = jnp.where(even, m[:, 128:192], m[:, 192:256])
                o_ref[pl.ds(k * 1024 + blk * 256, 256), :] = jnp.concatenate([re, im], axis=1)
            o_ref[pl.ds(2048 + blk * 8, 8), :] = dlr_ref[blk]
            o_ref[pl.ds(2080 + blk * 8, 8), :] = dli_ref[blk]

    vm = pl.BlockSpec(memory_space=pltpu.VMEM)
    return pl.pallas_call(body, in_specs=[vm] * 4, out_specs=vm, out_shape=_sds((2112, 128), F32), name="s5_compact",
                          compiler_params=_params())(drb, drct, dlr, dli)


NEG = -1e30


GROUP = N_Q // N_KV


def _attn_masks(n):
    qi = lax.broadcasted_iota(jnp.int32, (GROUP * BLOCK, BLOCK), 0) % BLOCK
    kj = lax.broadcasted_iota(jnp.int32, (GROUP * BLOCK, BLOCK), 1)
    return jnp.logical_and(kj > qi, n > 0), kj <= qi


def _stack_heads(ref, kh):
    return jnp.concatenate([ref[:, (GROUP * kh + g) * HEAD_DIM:(GROUP * kh + g + 1) * HEAD_DIM] for g in range(GROUP)], axis=0)


def _unstack_heads(val):
    return jnp.concatenate([val[g * BLOCK:(g + 1) * BLOCK] for g in range(GROUP)], axis=1)


def _sink_column(sink_ref, kh):
    grp = lax.broadcasted_iota(jnp.int32, (GROUP * BLOCK, 1), 0) // BLOCK
    col = jnp.zeros((GROUP * BLOCK, 1), F32)
    for g in range(GROUP):
        col = jnp.where(grp == g, sink_ref[GROUP * kh + g], col)
    return col, grp


def _attn_exp(q4, kp, kc, sink, mask_p, mask_c):
    scale = 1.0 / math.sqrt(HEAD_DIM)
    nt = (((1,), (1,)), ((), ()))
    sp = jnp.where(mask_p, lax.dot_general(q4, kp, nt, preferred_element_type=F32) * scale, NEG)
    sc = jnp.where(mask_c, lax.dot_general(q4, kc, nt, preferred_element_type=F32) * scale, NEG)
    m = jnp.maximum(jnp.maximum(jnp.max(sp, axis=-1, keepdims=True), jnp.max(sc, axis=-1, keepdims=True)), sink)
    pp = jnp.exp(sp - m)
    pc = jnp.exp(sc - m)
    ps = jnp.exp(sink - m)
    inv = 1.0 / (jnp.sum(pp, axis=-1, keepdims=True) + jnp.sum(pc, axis=-1, keepdims=True) + ps)
    return pp, pc, ps, inv


def attn_fwd(q, kv, sinks):
    n_rows = q.shape[0]
    nb = n_rows // BLOCK

    def body(sink_ref, q_ref, kvp_ref, kvc_ref, o_ref):
        n = pl.program_id(0)
        mask_p, mask_c = _attn_masks(n)
        outs = []
        for kh in range(N_KV):
            ks, vs = slice(kh * HEAD_DIM, (kh + 1) * HEAD_DIM), slice((N_KV + kh) * HEAD_DIM, (N_KV + kh + 1) * HEAD_DIM)
            sink, _ = _sink_column(sink_ref, kh)
            pp, pc, _, inv = _attn_exp(_stack_heads(q_ref, kh), kvp_ref[:, ks], kvc_ref[:, ks], sink, mask_p, mask_c)
            o4 = (jnp.dot(pp.astype(BF16), kvp_ref[:, vs], preferred_element_type=F32)
                  + jnp.dot(pc.astype(BF16), kvc_ref[:, vs], preferred_element_type=F32)) * inv
            outs.append(_unstack_heads(o4))
        o_ref[...] = jnp.concatenate(outs, axis=1).astype(BF16)

    kvw = 2 * N_KV * HEAD_DIM
    return pl.pallas_call(
        body, grid=(nb,),
        in_specs=[pl.BlockSpec(memory_space=pltpu.SMEM), pl.BlockSpec((BLOCK, D_MODEL), lambda n: (n, 0)),
                  pl.BlockSpec((BLOCK, kvw), lambda n: (jnp.maximum(n - 1, 0), 0)), pl.BlockSpec((BLOCK, kvw), lambda n: (n, 0))],
        out_specs=pl.BlockSpec((BLOCK, D_MODEL), lambda n: (n, 0)), out_shape=_sds((n_rows, D_MODEL), BF16),
        name="attn_fwd", compiler_params=_params(("parallel",)))(sinks, q, kv, kv)


def attn_bwd(q, kv, do, sinks):
    n_rows = q.shape[0]
    nb = n_rows // BLOCK
    kvw = 2 * N_KV * HEAD_DIM
    tn = (((0,), (0,)), ((), ()))
    nt = (((1,), (1,)), ((), ()))
    scale = 1.0 / math.sqrt(HEAD_DIM)

    def body(sink_ref, q_ref, kvp_ref, kvc_ref, do_ref, dq_ref, dbq_ref, dprev_ref, dcur_ref, dsink_ref):
        n = pl.program_id(0)
        mask_p, mask_c = _attn_masks(n)
        lane = lax.broadcasted_iota(jnp.int32, (1, D_MODEL), 1)
        dqs, dsink = [], jnp.zeros((1, D_MODEL), F32)
        dkp, dkc, dvp, dvc = [], [], [], []
        for kh in range(N_KV):
            ks, vs = slice(kh * HEAD_DIM, (kh + 1) * HEAD_DIM), slice((N_KV + kh) * HEAD_DIM, (N_KV + kh + 1) * HEAD_DIM)
            q4, do4 = _stack_heads(q_ref, kh), _stack_heads(do_ref, kh)
            kp, kc, vp, vc = kvp_ref[:, ks], kvc_ref[:, ks], kvp_ref[:, vs], kvc_ref[:, vs]
            sink, grp = _sink_column(sink_ref, kh)
            pp, pc, ps, inv = _attn_exp(q4, kp, kc, sink, mask_p, mask_c)
            pp, pc = pp * inv, pc * inv
            dpp = lax.dot_general(do4, vp, nt, preferred_element_type=F32)
            dpc = lax.dot_general(do4, vc, nt, preferred_element_type=F32)
            delta = jnp.sum(pp * dpp, axis=-1, keepdims=True) + jnp.sum(pc * dpc, axis=-1, keepdims=True)
            dsp = (pp * (dpp - delta) * scale).astype(BF16)
            dsc = (pc * (dpc - delta) * scale).astype(BF16)
            dsk = ps * inv * delta
            for g in range(GROUP):
                dsink = dsink + jnp.where(lane == GROUP * kh + g, -jnp.sum(jnp.where(grp == g, dsk, 0.0)), 0.0)
            dqs.append(_unstack_heads(jnp.dot(dsp, kp, preferred_element_type=F32)
                                      + jnp.dot(dsc, kc, preferred_element_type=F32)))
            dkp.append(lax.dot_general(dsp, q4, tn, preferred_element_type=F32))
            dkc.append(lax.dot_general(dsc, q4, tn, preferred_element_type=F32))
            dvp.append(lax.dot_general(pp.astype(BF16), do4, tn, preferred_element_type=F32))
            dvc.append(lax.dot_general(pc.astype(BF16), do4, tn, preferred_element_type=F32))
        dq = jnp.concatenate(dqs, axis=1)
        dq_ref[...] = dq.astype(BF16)
        dprev_ref[0] = jnp.concatenate(dkp + dvp, axis=1)
        dcur_ref[0] = jnp.concatenate(dkc + dvc, axis=1)

        @pl.when(n == 0)
        def _():
            dbq_ref[...] = jnp.zeros_like(dbq_ref)
            dsink_ref[...] = jnp.zeros_like(dsink_ref)

        dbq_ref[...] += jnp.sum(dq, axis=0, keepdims=True)
        dsink_ref[...] += dsink

    blk = pl.BlockSpec((BLOCK, D_MODEL), lambda n: (n, 0))
    part = pl.BlockSpec((1, BLOCK, kvw), lambda n: (n, 0, 0))
    return pl.pallas_call(
        body, grid=(nb,),
        in_specs=[pl.BlockSpec(memory_space=pltpu.SMEM), blk,
                  pl.BlockSpec((BLOCK, kvw), lambda n: (jnp.maximum(n - 1, 0), 0)), pl.BlockSpec((BLOCK, kvw), lambda n: (n, 0)), blk],
        out_specs=[blk, pl.BlockSpec((1, D_MODEL), lambda n: (0, 0)), part, part, pl.BlockSpec((1, D_MODEL), lambda n: (0, 0))],
        out_shape=[_sds((n_rows, D_MODEL), BF16), _sds((1, D_MODEL), F32), _sds((nb, BLOCK, kvw), F32),
                   _sds((nb, BLOCK, kvw), F32), _sds((1, D_MODEL), F32)],
        name="attn_bwd", compiler_params=_params(("arbitrary",)))(sinks, q, kv, kv, do)


def kv_combine(dprev, dcur):
    nb, _, kvw = dprev.shape

    def body(dcur_ref, dprev_ref, dkv_ref, db_ref):
        total = jnp.zeros((1, kvw), F32)
        for m in range(nb):
            dkv = dcur_ref[m] + dprev_ref[m + 1] if m + 1 < nb else dcur_ref[m]
            dkv_ref[m * BLOCK:(m + 1) * BLOCK, :] = dkv.astype(BF16)
            total = total + jnp.sum(dkv, axis=0, keepdims=True)
        db_ref[...] = jnp.concatenate([total, jnp.zeros((1, D_MODEL - kvw), F32)], axis=1)

    vm = pl.BlockSpec(memory_space=pltpu.VMEM)
    return pl.pallas_call(body, in_specs=[vm, vm], out_specs=[vm, vm],
                          out_shape=[_sds((nb * BLOCK, kvw), BF16), _sds((1, D_MODEL), F32)], name="kv_combine",
                          compiler_params=_params())(dcur, dprev)


def glu_bwd(dout, val, gate, tm=256):
    n_rows, d = dout.shape

    def body(do_ref, v_ref, g_ref, dz_ref, db_ref):
        i = pl.program_id(0)
        sg = jax.nn.sigmoid(g_ref[...])
        dval = do_ref[...] * sg
        dgate = do_ref[...] * v_ref[...] * sg * (1.0 - sg)
        dz_ref[...] = jnp.concatenate([dval, dgate], axis=1).astype(BF16)

        @pl.when(i == 0)
        def _():
            db_ref[...] = jnp.zeros_like(db_ref)

        db_ref[0:1, :] += jnp.sum(dval, axis=0, keepdims=True)
        db_ref[1:2, :] += jnp.sum(dgate, axis=0, keepdims=True)

    row = pl.BlockSpec((tm, d), lambda i: (i, 0))
    return pl.pallas_call(
        body, grid=(n_rows // tm,), in_specs=[row, row, row],
        out_specs=[pl.BlockSpec((tm, 2 * d), lambda i: (i, 0)), pl.BlockSpec((2, d), lambda i: (0, 0))],
        out_shape=[_sds((n_rows, 2 * d), BF16), _sds((2, d), F32)],
        name="glu_bwd", compiler_params=_params(("arbitrary",)))(dout, val, gate)


def _adam_update(w, g, m, v):
    nm = ADAM_B1 * m + (1.0 - ADAM_B1) * g
    nv = ADAM_B2 * v + (1.0 - ADAM_B2) * (g * g)
    m_hat = nm / (1.0 - ADAM_B1 ** ADAM_STEP)
    v_hat = nv / (1.0 - ADAM_B2 ** ADAM_STEP)
    return -ADAM_LR * (m_hat / (jnp.sqrt(v_hat) + ADAM_EPS) + ADAM_WD * w), nm, nv


def adamw(name, w, g, m, v, tm=256):
    n_rows, d = w.shape
    tm = tm if n_rows % tm == 0 else n_rows

    def body(w_ref, g_ref, m_ref, v_ref, go_ref, d_ref, nm_ref, nv_ref):
        gv = g_ref[...]
        go_ref[...] = gv
        d_ref[...], nm_ref[...], nv_ref[...] = _adam_update(w_ref[...], gv, m_ref[...], v_ref[...])

    row = pl.BlockSpec((tm, d), lambda i: (i, 0))
    return pl.pallas_call(
        body, grid=(n_rows // tm,), in_specs=[row] * 4, out_specs=[row] * 4,
        out_shape=[_sds((n_rows, d), F32)] * 4, name=name, compiler_params=_params(("parallel",)))(w, g, m, v)


def adamw_native(name, ws, gs, ms, vs):
    n = len(ws)

    def body(*refs):
        w_refs, g_refs, m_refs, v_refs = refs[:n], refs[n:2 * n], refs[2 * n:3 * n], refs[3 * n:4 * n]
        d_refs, nm_refs, nv_refs = refs[4 * n:5 * n], refs[5 * n:6 * n], refs[6 * n:7 * n]
        for k in range(n):
            dl, nm, nv = _adam_update(w_refs[k][...], g_refs[k][...], m_refs[k][...], v_refs[k][...])
            d_refs[k][...] = dl
            nm_refs[k][...] = nm
            nv_refs[k][...] = nv

    vm = pl.BlockSpec(memory_space=pltpu.VMEM)
    shapes = [_sds(w.shape, F32) for w in ws]
    out = pl.pallas_call(body, in_specs=[vm] * (4 * n), out_specs=[vm] * (3 * n), out_shape=shapes * 3, name=name,
                         compiler_params=_params())(*ws, *gs, *ms, *vs)
    return list(out[:n]), list(out[n:2 * n]), list(out[2 * n:])


VEC_ROWS = {"norm_mix": 0, "norm_mlp": 2, "norm_kv": 4, "norm_final": 5, "s5_d": 6, "b_q": 7, "b_o": 8, "s5_b_glu": 9,
            "b_kv": 11, "sinks": 12, "loss": 13}


def split_vectors(where, vecs, d_shard, glu_shard):
    kvw = 2 * N_KV * HEAD_DIM
    shapes = {"norm_mix": (2, D_MODEL), "norm_mlp": (2, D_MODEL), "norm_kv": (1, D_MODEL), "norm_final": (1, D_MODEL),
              "s5_d": (1, d_shard), "b_q": (1, D_MODEL), "b_o": (1, D_MODEL), "s5_b_glu": (1, glu_shard), "b_kv": (1, kvw),
              "sinks": (1, N_Q), "loss": (1, 128)}
    names = list(shapes)

    def body(where_ref, v_ref, *o_refs):
        chip = where_ref[1]
        for name, o_ref in zip(names, o_refs):
            r0, (r, n) = VEC_ROWS[name], shapes[name]
            if name == "s5_d":
                g = jnp.zeros((1, n), F32)
                for j in range(4):
                    g = jnp.where(chip == j, v_ref[r0:r0 + 1, j * n:(j + 1) * n], g)
            elif name == "s5_b_glu":
                g = jnp.zeros((1, n), F32)
                for j in range(4):
                    row, col = r0 + (j * n) // D_MODEL, (j * n) % D_MODEL
                    g = jnp.where(chip == j, v_ref[row:row + 1, col:col + n], g)
            else:
                g = v_ref[r0:r0 + r, 0:n]
            o_ref[...] = g

    vm = pl.BlockSpec(memory_space=pltpu.VMEM)
    out = pl.pallas_call(body, in_specs=[pl.BlockSpec(memory_space=pltpu.SMEM), vm], out_specs=[vm] * len(names),
                         out_shape=[_sds(shapes[n], F32) for n in names], name="split_vectors",
                         compiler_params=_params())(where, vecs)
    return dict(zip(names, out))


def _position():
    x, y, c = lax.axis_index("x"), lax.axis_index("y"), lax.axis_index("c")
    others = [(1 - x, y), (x, 1 - y), (1 - x, 1 - y)]
    return x, y, c, others


def _window(ref, kind, chip, half, shard_shape):
    r, n = shard_shape
    if kind == "col":
        return ref.at[pl.ds(pl.multiple_of(half * (r // 2), 16), r // 2), pl.ds(pl.multiple_of(chip * n, 128), n)]
    return ref.at[pl.ds(pl.multiple_of(chip * r, 16), r), pl.ds(pl.multiple_of(half * (n // 2), 128), n // 2)]


def _half(ref, kind, half, shape):
    r, n = shape
    if kind == "col":
        return ref.at[pl.ds(pl.multiple_of(half * (r // 2), 16), r // 2), :]
    return ref.at[:, pl.ds(pl.multiple_of(half * (n // 2), 128), n // 2)]


def swap_halves(name, grads, kinds):
    nt = len(grads)
    shapes = [tuple(g.shape) for g in grads]

    def body(*refs):
        in_refs, out_refs = refs[:nt], refs[nt:2 * nt]
        send_sems, recv_sems = refs[2 * nt:]
        x, y, c, _ = _position()
        cps = []
        for t in range(nt):
            cp = pltpu.make_async_remote_copy(
                src_ref=_half(in_refs[t], kinds[t], 1 - c, shapes[t]), dst_ref=_half(out_refs[t], kinds[t], 1 - c, shapes[t]),
                send_sem=send_sems.at[t], recv_sem=recv_sems.at[t], device_id=(x, y, 1 - c), device_id_type=MESH)
            cp.start()
            cps.append(cp)
        for t in range(nt):
            mine = _half(out_refs[t], kinds[t], c, shapes[t])
            pltpu.make_async_remote_copy(
                src_ref=mine, dst_ref=mine, send_sem=send_sems.at[t], recv_sem=recv_sems.at[t],
                device_id=(x, y, 1 - c), device_id_type=MESH).wait_recv()
        for cp in cps:
            cp.wait_send()

    hbm = pl.BlockSpec(memory_space=pl.ANY)
    return pl.pallas_call(
        body, in_specs=[hbm] * nt, out_specs=[hbm] * nt, out_shape=[_sds(s, BF16) for s in shapes],
        scratch_shapes=[pltpu.SemaphoreType.DMA((nt,)), pltpu.SemaphoreType.DMA((nt,))],
        name=name, compiler_params=_params())(*grads)


def _half_spec(kind, shape, tiles):
    r, n = shape
    if kind == "col":
        tn = n // tiles
        return pl.BlockSpec((r // 2, tn), lambda i, s: (s[0], i))
    tm = r // tiles
    return pl.BlockSpec((tm, n // 2), lambda i, s: (i, s[0]))


def add_halves(name, mine, landed, kinds, where, tiles=4):
    nt = len(mine)
    shapes = [tuple(a.shape) for a in mine]

    def compact(t):
        r, n = shapes[t]
        if kinds[t] == "col":
            return (r // 2, n), pl.BlockSpec((r // 2, n // tiles), lambda i, s: (0, i))
        return (r, n // 2), pl.BlockSpec((r // tiles, n // 2), lambda i, s: (i, 0))

    def body(s_ref, *refs):
        for a_ref, b_ref, o_ref in zip(refs[:nt], refs[nt:2 * nt], refs[2 * nt:]):
            o_ref[...] = (a_ref[...].astype(F32) + b_ref[...].astype(F32)).astype(BF16)

    specs = [_half_spec(kinds[t], shapes[t], tiles) for t in range(nt)]
    return pl.pallas_call(
        body, grid_spec=pltpu.PrefetchScalarGridSpec(num_scalar_prefetch=1, grid=(tiles,), in_specs=specs + specs,
                                                     out_specs=[compact(t)[1] for t in range(nt)]),
        out_shape=[_sds(compact(t)[0], BF16) for t in range(nt)], name=name,
        compiler_params=_params(("parallel",)))(where, *mine, *landed)


def sum_shards(name, parts, landed, kinds, shard_shapes, where, layers, n_layers, intos, tiles=2):
    nt = len(parts)
    in_specs, out_specs = [], []
    for t in range(nt):
        (r, n), layer = shard_shapes[t], layers[t]
        if kinds[t] == "col":
            tm, width = r // 2 // tiles, n
            own = pl.BlockSpec((tm, n), lambda i, s: (i, s[1]))
            out = pl.BlockSpec((None, tm, n), lambda i, s, layer=layer: (layer, s[0] * tiles + i, 0))
        else:
            tm, width = r // tiles, n // 2
            own = pl.BlockSpec((tm, n // 2), lambda i, s: (s[1] * tiles + i, 0))
            out = pl.BlockSpec((None, tm, n // 2), lambda i, s, layer=layer: (layer, i, s[0]))
        in_specs += [own, pl.BlockSpec((3, tm, width), lambda i, s: (0, i, 0))]
        out_specs.append(out)
    args, aliases = [where] + [a for pair in zip(parts, landed) for a in pair], {}
    for t in range(nt):
        if intos[t] is not None:
            aliases[len(args)] = t
            in_specs.append(pl.BlockSpec(memory_space=pl.ANY))
            args.append(intos[t])

    def body(s_ref, *refs):
        for t in range(nt):
            a_ref, l_ref, o_ref = refs[2 * t], refs[2 * t + 1], refs[len(in_specs) + t]
            o_ref[...] = ((a_ref[...].astype(F32) + l_ref[0].astype(F32)) + l_ref[1].astype(F32)) + l_ref[2].astype(F32)

    return pl.pallas_call(
        body, grid_spec=pltpu.PrefetchScalarGridSpec(num_scalar_prefetch=1, grid=(tiles,), in_specs=in_specs,
                                                     out_specs=out_specs),
        out_shape=[_sds((n_layers[t],) + tuple(shard_shapes[t]), F32) for t in range(nt)], input_output_aliases=aliases,
        name=name, compiler_params=_params(("parallel",)))(*args)


def share_halves(arrays, entries):
    na, nt = len(arrays), len(entries)

    def body(*refs):
        out_refs = refs[na:2 * na]
        send_sems, recv_sems = refs[2 * na:]
        x, y, c, _ = _position()
        cps = []
        for t, (a, layer, kind) in enumerate(entries):
            shape = tuple(arrays[a].shape[1:])
            mine = _half(out_refs[a].at[layer], kind, c, shape)
            cp = pltpu.make_async_remote_copy(
                src_ref=mine, dst_ref=mine, send_sem=send_sems.at[t], recv_sem=recv_sems.at[t],
                device_id=(x, y, 1 - c), device_id_type=MESH)
            cp.start()
            cps.append(cp)
        for t, (a, layer, kind) in enumerate(entries):
            shape = tuple(arrays[a].shape[1:])
            other = _half(out_refs[a].at[layer], kind, 1 - c, shape)
            pltpu.make_async_remote_copy(
                src_ref=other, dst_ref=other, send_sem=send_sems.at[t], recv_sem=recv_sems.at[t],
                device_id=(x, y, 1 - c), device_id_type=MESH).wait_recv()
        for cp in cps:
            cp.wait_send()

    hbm = pl.BlockSpec(memory_space=pl.ANY)
    return pl.pallas_call(
        body, in_specs=[hbm] * na, out_specs=[hbm] * na, out_shape=[_sds(a.shape, F32) for a in arrays],
        input_output_aliases={i: i for i in range(na)},
        scratch_shapes=[pltpu.SemaphoreType.DMA((nt,)), pltpu.SemaphoreType.DMA((nt,))],
        name="share_halves", compiler_params=_params())(*arrays)


HBM_SPEC = pl.BlockSpec(memory_space=pltpu.HBM)
SEM_SPEC = pl.BlockSpec(memory_space=pltpu.SEMAPHORE)
ANY_SPEC = pl.BlockSpec(memory_space=pl.ANY)


def _split_params():
    return pltpu.CompilerParams(has_side_effects=pltpu.SideEffectType.DATAFLOW_SIDE_EFFECTING,
                                vmem_limit_bytes=VMEM_LIMIT_BYTES)


def _in_hbm(a):
    return pltpu.with_memory_space_constraint(a, pltpu.HBM)


def cast_place(arrays, entries, where, tiles=2):
    in_specs, out_specs, fulls = [], [], []
    for a, layer, kind in entries:
        _, r, n = arrays[a].shape
        tm = r // tiles
        in_specs.append(pl.BlockSpec((None, tm, n), lambda i, s, layer=layer: (layer, i, 0)))
        if kind == "col":
            fulls.append((r, 4 * n))
            out_specs.append(pl.BlockSpec((tm, n), lambda i, s: (i, s[1])))
        else:
            fulls.append((4 * r, n))
            out_specs.append(pl.BlockSpec((tm, n), lambda i, s: (s[1] * tiles + i, 0)))
    nt = len(entries)

    def body(s_ref, *refs):
        for w_ref, o_ref in zip(refs[:nt], refs[nt:]):
            o_ref[...] = w_ref[...].astype(BF16)

    return pl.pallas_call(
        body, grid_spec=pltpu.PrefetchScalarGridSpec(num_scalar_prefetch=1, grid=(tiles,), in_specs=in_specs,
                                                     out_specs=out_specs),
        out_shape=[_sds(f, BF16) for f in fulls], name="cast_place",
        compiler_params=_params(("parallel",)))(where, *[arrays[a] for a, _, _ in entries])


def gather_start(fulls, kinds, shard_shapes, after):
    nt = len(fulls)
    na = 0 if after is None else 1

    def body(*refs):
        full_refs = refs[:nt]
        send_sems, recv_sems, token = refs[nt + na], refs[nt + na + 1], refs[-1]
        x, y, c, others = _position()
        for t in range(nt):
            mine = _window(full_refs[t], kinds[t], 2 * x + y, c, shard_shapes[t])
            for j, (ox, oy) in enumerate(others):
                pltpu.make_async_remote_copy(
                    src_ref=mine, dst_ref=mine, send_sem=send_sems.at[3 * t + j], recv_sem=recv_sems.at[3 * t + j],
                    device_id=(ox, oy, c), device_id_type=MESH).start()
        token[...] = jnp.zeros_like(token)

    sems = pltpu.SemaphoreType.DMA((3 * nt,))
    out = pl.pallas_call(
        body, name="gather_start", in_specs=[HBM_SPEC] * nt + [ANY_SPEC] * na,
        out_specs=(SEM_SPEC, SEM_SPEC, *[HBM_SPEC] * nt, pl.BlockSpec(memory_space=pltpu.VMEM)),
        out_shape=(sems, sems, *[pltpu.HBM(f.shape, f.dtype) for f in fulls], _sds((8, 128), F32)),
        input_output_aliases={t: 2 + t for t in range(nt)}, compiler_params=_split_params(),
    )(*[_in_hbm(f) for f in fulls], *([] if after is None else [after]))
    return out[0], out[1], list(out[2:2 + nt]), out[-1]


def gather_wait(name, send_sems, recv_sems, fulls, kinds, shard_shapes, after, first):
    nt = len(fulls)

    def body(*refs):
        full_refs, send_ref, recv_ref = refs[:nt], refs[nt], refs[nt + 1]
        x, y, c, others = _position()
        for t in range(nt):
            mine = _window(full_refs[t], kinds[t], 2 * x + y, c, shard_shapes[t])
            for j, (ox, oy) in enumerate(others):
                cp = pltpu.make_async_remote_copy(
                    src_ref=mine, dst_ref=_window(full_refs[t], kinds[t], 2 * ox + oy, c, shard_shapes[t]),
                    send_sem=send_ref.at[3 * (first + t) + j], recv_sem=recv_ref.at[3 * (first + t) + j],
                    device_id=(ox, oy, c), device_id_type=MESH)
                cp.wait_send()
                cp.wait_recv()

    out = pl.pallas_call(
        body, name=name, in_specs=[HBM_SPEC] * nt + [SEM_SPEC, SEM_SPEC, HBM_SPEC], out_specs=[HBM_SPEC] * nt,
        out_shape=[pltpu.HBM(f.shape, f.dtype) for f in fulls], input_output_aliases={t: t for t in range(nt)},
        compiler_params=_split_params())(*fulls, send_sems, recv_sems, _in_hbm(after))
    return list(out)


def forward_halves(name, fulls, kinds, shard_shapes):
    nt = len(fulls)

    def body(*refs):
        out_refs = refs[nt:2 * nt]
        send_sems, recv_sems = refs[2 * nt:]
        x, y, c, others = _position()
        cps = []
        for t in range(nt):
            for j, (ox, oy) in enumerate(others):
                landed = _window(out_refs[t], kinds[t], 2 * ox + oy, c, shard_shapes[t])
                cp = pltpu.make_async_remote_copy(
                    src_ref=landed, dst_ref=landed, send_sem=send_sems.at[3 * t + j], recv_sem=recv_sems.at[3 * t + j],
                    device_id=(x, y, 1 - c), device_id_type=MESH)
                cp.start()
                cps.append(cp)
        for t in range(nt):
            for j, (ox, oy) in enumerate(others):
                got = _window(out_refs[t], kinds[t], 2 * ox + oy, 1 - c, shard_shapes[t])
                pltpu.make_async_remote_copy(
                    src_ref=got, dst_ref=got, send_sem=send_sems.at[3 * t + j], recv_sem=recv_sems.at[3 * t + j],
                    device_id=(x, y, 1 - c), device_id_type=MESH).wait_recv()
        for cp in cps:
            cp.wait_send()

    out = pl.pallas_call(
        body, in_specs=[ANY_SPEC] * nt, out_specs=[ANY_SPEC] * nt, out_shape=[_sds(f.shape, f.dtype) for f in fulls],
        input_output_aliases={t: t for t in range(nt)},
        scratch_shapes=[pltpu.SemaphoreType.DMA((3 * nt,)), pltpu.SemaphoreType.DMA((3 * nt,))],
        name=name, compiler_params=_params())(*fulls)
    return list(out)


def _piece(ref, kind, chip, shard_shape):
    r, n = shard_shape
    if kind == "col":
        return ref.at[:, pl.ds(pl.multiple_of(chip * n, 128), n)]
    return ref.at[pl.ds(pl.multiple_of(chip * r, 16), r), :]


def _piece_shape(kind, shard_shape):
    r, n = shard_shape
    return (r // 2, n) if kind == "col" else (r, n // 2)


def exchange_start(name, parts, kinds, shard_shapes):
    nt = len(parts)
    lands = [lax.empty((3,) + _piece_shape(kinds[t], shard_shapes[t]), BF16) for t in range(nt)]

    def body(*refs):
        part_refs, land_refs = refs[:nt], refs[nt:2 * nt]
        send_sems, recv_sems, token = refs[2 * nt], refs[2 * nt + 1], refs[-1]
        x, y, c, others = _position()
        for t in range(nt):
            for j, (ox, oy) in enumerate(others):
                pltpu.make_async_remote_copy(
                    src_ref=_piece(part_refs[t], kinds[t], 2 * ox + oy, shard_shapes[t]), dst_ref=land_refs[t].at[j],
                    send_sem=send_sems.at[3 * t + j], recv_sem=recv_sems.at[3 * t + j],
                    device_id=(ox, oy, c), device_id_type=MESH).start()
        token[...] = jnp.zeros_like(token)

    sems = pltpu.SemaphoreType.DMA((3 * nt,))
    both = list(parts) + lands
    out = pl.pallas_call(
        body, name=name, in_specs=[HBM_SPEC] * (2 * nt),
        out_specs=(SEM_SPEC, SEM_SPEC, *[HBM_SPEC] * (2 * nt), pl.BlockSpec(memory_space=pltpu.VMEM)),
        out_shape=(sems, sems, *[pltpu.HBM(a.shape, a.dtype) for a in both], _sds((8, 128), F32)),
        input_output_aliases={t: 2 + t for t in range(2 * nt)}, compiler_params=_split_params(),
    )(*[_in_hbm(a) for a in both])
    return out[0], out[1], list(out[2:2 + nt]), list(out[2 + nt:2 + 2 * nt]), out[-1]


def exchange_wait(name, send_sems, recv_sems, parts, lands, kinds, shard_shapes, after):
    nt = len(parts)

    def body(*refs):
        part_refs, land_refs = refs[:nt], refs[nt:2 * nt]
        send_ref, recv_ref = refs[2 * nt], refs[2 * nt + 1]
        x, y, c, others = _position()
        for t in range(nt):
            for j, (ox, oy) in enumerate(others):
                cp = pltpu.make_async_remote_copy(
                    src_ref=_piece(part_refs[t], kinds[t], 2 * ox + oy, shard_shapes[t]), dst_ref=land_refs[t].at[j],
                    send_sem=send_ref.at[3 * t + j], recv_sem=recv_ref.at[3 * t + j],
                    device_id=(ox, oy, c), device_id_type=MESH)
                cp.wait_send()
                cp.wait_recv()

    both = list(parts) + list(lands)
    out = pl.pallas_call(
        body, name=name, in_specs=[HBM_SPEC] * (2 * nt) + [SEM_SPEC, SEM_SPEC, HBM_SPEC], out_specs=[HBM_SPEC] * (2 * nt),
        out_shape=[pltpu.HBM(a.shape, a.dtype) for a in both], input_output_aliases={t: t for t in range(2 * nt)},
        compiler_params=_split_params())(*both, send_sems, recv_sems, _in_hbm(after))
    return list(out[:nt]), list(out[nt:])


def all_reduce_small(name, bufs):
    n = len(bufs)
    halves = [b.shape[0] // 2 for b in bufs]

    def body(*refs):
        in_refs, out_refs, lands = refs[:n], refs[n:2 * n], refs[2 * n:3 * n]
        send_sems, recv_sems = refs[3 * n:]
        x, y, c, _ = _position()
        mine = [pl.ds(pl.multiple_of(c * h, 8), h) for h in halves]
        other = [pl.ds(pl.multiple_of((1 - c) * h, 8), h) for h in halves]
        for k in range(n):
            out_refs[k][mine[k], :] = in_refs[k][mine[k], :]
        for s, peer in enumerate([(x, y, 1 - c), (1 - x, y, c), (x, 1 - y, c)]):
            cps = []
            for k in range(n):
                src = in_refs[k].at[other[k]] if s == 0 else out_refs[k].at[mine[k]]
                cp = pltpu.make_async_remote_copy(
                    src_ref=src, dst_ref=lands[k].at[s], send_sem=send_sems.at[4 * k + s], recv_sem=recv_sems.at[4 * k + s],
                    device_id=peer, device_id_type=MESH)
                cp.start()
                cps.append(cp)
            for k, cp in enumerate(cps):
                cp.wait()
                out_refs[k][mine[k], :] = out_refs[k][mine[k], :] + lands[k][s]
        cps = []
        for k in range(n):
            cp = pltpu.make_async_remote_copy(
                src_ref=out_refs[k].at[mine[k]], dst_ref=out_refs[k].at[mine[k]], send_sem=send_sems.at[4 * k + 3],
                recv_sem=recv_sems.at[4 * k + 3], device_id=(x, y, 1 - c), device_id_type=MESH)
            cp.start()
            cps.append(cp)
        for cp in cps:
            cp.wait()

    vm = pl.BlockSpec(memory_space=pltpu.VMEM)
    out = pl.pallas_call(
        body, in_specs=[vm] * n, out_specs=[vm] * n, out_shape=[_sds(b.shape, F32) for b in bufs],
        scratch_shapes=[pltpu.VMEM((3, h, b.shape[1]), F32) for h, b in zip(halves, bufs)]
        + [pltpu.SemaphoreType.DMA((4 * n,)), pltpu.SemaphoreType.DMA((4 * n,))],
        name=name, compiler_params=_params())(*bufs)
    return list(out)


def _local_step(x, target, small, need, emit):
    d = D_MODEL
    full = {}

    def after_token(vec, token):
        return vec if token is None else vec + token[0:1, 0:1]

    rb16, rbt16, rc16, rct16, lr_t, li_t = small["s5_operands"]
    ge, y2, cs = s5_fwd(x, small["norm_mix0"], small["s5_d"], rb16, rc16, lr_t, li_t)
    full.update(need("glu", ge))

    def norm_rows(h, gains):
        xh, _ = _rms_hat(h)
        return [xh * g for g in gains]

    def glu_epilogue(accs, e, r):
        v, gt = accs[0] + r[0], accs[1] + r[1]
        h = e[0] + v * jax.nn.sigmoid(gt)
        return [h, v, gt] + norm_rows(h, r[2:])

    h1, val, gate, n1 = mm_nn(
        "glu", ge, full["w_glu"], [0, d], d, glu_epilogue, [F32, F32, F32, BF16], extras=[x],
        rowvecs=[(small["s5_b_glu"], 0), (small["s5_b_glu"], d), (small["norm_mlp0"], 0)], tm=512, tn=d)

    def mlp_fwd(tag, h, n, w_in, get_w_out, next_gains, head=None):
        def in_epilogue(accs, e, rv):
            pos = jnp.maximum(accs[0], 0.0)
            return [pos * pos, 2.0 * pos]

        r, slope = mm_nn("mlp_in" + tag, n, w_in, [0], w_in.shape[1], in_epilogue, [BF16, BF16], tm=2048)
        w_out = get_w_out(r)

        def epilogue(accs, e, rv):
            h_out = e[0] + accs[0]
            return [h_out] + norm_rows(h_out, rv)

        if head is not None:
            return head(r, w_out, h), (n, r, slope)
        outs = mm_nn("mlp_out" + tag, r, w_out, [0], d, epilogue, [F32] + [BF16] * len(next_gains), extras=[h],
                     rowvecs=[(g, 0) for g in next_gains], tm=512, tn=d)
        return outs[0], outs[1:], (n, r, slope)

    full.update(need("mlp_in0", h1))

    def w_out0(after):
        full.update(need("mlp_out0", after))
        return full["w_out0"]

    h2, (nkv, n2), mlp0 = mlp_fwd("0", h1, n1, full["w_in0"], w_out0, [small["norm_kv"], small["norm_mix1"]])

    full.update(need("attn", h2))
    kvw = 2 * N_KV * HEAD_DIM
    (kv,) = mm_nn("kv_proj", nkv, full["w_kv"], [0], kvw, lambda accs, e, r: [accs[0] + r[0]], [BF16],
                  rowvecs=[(small["b_kv"], 0)], tm=2048)
    (q,) = mm_nn("q_proj", n2, full["w_q"], [0], d, lambda accs, e, r: [accs[0] + r[0]], [BF16],
                 rowvecs=[(small["b_q"], 0)], tm=2048)
    sinks = small["sinks"].reshape(N_Q)
    o = attn_fwd(q, kv, sinks)
    def o_epilogue(accs, e, r):
        h_out = e[0] + accs[0] + r[0]
        return [h_out] + norm_rows(h_out, r[1:])

    h3, n3 = mm_nn("o_proj", o, full["w_o"], [0], d, o_epilogue, [F32, BF16], extras=[h2],
                   rowvecs=[(small["b_o"], 0), (small["norm_mlp1"], 0)], tm=512, tn=d)
    full.update(need("mlp1", h3))

    def loss_head(r, w_out, h):
        def epilogue(accs, e, rv):
            xh, rr = _rms_hat(e[0] + accs[0])
            err = xh * rv[0] - e[1]
            dy = err * (1.0 / d)
            dxh = dy * rv[0]
            dx = rr * (dxh - xh * jnp.mean(dxh * xh, axis=-1, keepdims=True))
            loss = jnp.full((1, d), 0.5 * jnp.sum(jnp.mean(err * err, axis=-1, keepdims=True)), F32)
            return [dx, dx, loss, jnp.sum(dy * xh, axis=0, keepdims=True)]

        return mm_nn("mlp_out1", r, w_out, [0], d, epilogue, [F32, BF16], extras=[h, target],
                     rowvecs=[(small["norm_final"], 0)], n_sums=2, tm=512, tn=d)

    (dh, dhb, loss_tile, dg_final), mlp1 = mlp_fwd("1", h3, n3, full["w_in1"], lambda after: full["w_out1"], [], head=loss_head)

    grads_small, grads_full = {"norm_final": dg_final}, {}
    ident = lambda acc, e, r: [acc]
    layer1 = ["w_out1", "w_in1", "w_o", "w_q", "w_kv"]
    layer0 = ["w_out0", "w_in0", "w_glu"]

    def norm_bwd_rows(x_rows, res, dys, gains):
        xh, r = _rms_hat(x_rows)
        dxh = sum(dy * g for dy, g in zip(dys, gains))
        dx = r * (dxh - xh * jnp.mean(dxh * xh, axis=-1, keepdims=True)) + res
        return dx, [jnp.sum(dy * xh, axis=0, keepdims=True) for dy in dys]

    def mlp_bwd(tag, dh, dhb, h_in, gain, w_in, w_out, saved):
        n, r, slope = saved
        grads_full["w_out" + tag] = mm_tn("dw_out" + tag, r, dhb, tn=1024)
        (da,) = mm_nt("mlp_da" + tag, dhb, w_out, lambda acc, e, rv: [acc * e[0].astype(F32)], [BF16], extras=[slope],
                      tm=2048)
        grads_full["w_in" + tag] = mm_tn("dw_in" + tag, n, da, tn=1024)

        def epilogue(acc, e, rv):
            dx, dgs = norm_bwd_rows(e[0], e[1], [acc], rv)
            return [dx, dx, jnp.sum(dx, axis=0, keepdims=True)] + dgs

        dx, dxb, colsum, dg = mm_nt("mlp_dn" + tag, da, w_in, epilogue, [F32, BF16], extras=[h_in, dh], rowvecs=[gain],
                                    n_sums=2, tm=512, tk=d)
        grads_small["norm_mlp" + tag] = dg
        return dx, dxb, colsum

    dh3, dh3b, colsum3 = mlp_bwd("1", dh, dhb, h3, small["norm_mlp1"], full["w_in1"], full["w_out1"], mlp1)
    grads_small["b_o"] = colsum3
    grads_full["w_o"] = mm_tn("dw_o", o, dh3b, tn=1024)
    (do,) = mm_nt("attn_do", dh3b, full["w_o"], ident, [BF16], tm=2048)
    dq, dbq, dprev, dcur, dsink = attn_bwd(q, kv, do, sinks)
    dkv, dbkv = kv_combine(dprev, dcur)
    grads_small["b_q"], grads_small["b_kv"], grads_small["sinks"] = dbq, dbkv, dsink
    grads_full["w_q"] = mm_tn("dw_q", n2, dq, tn=1024)
    grads_full["w_kv"] = mm_tn("dw_kv", nkv, dkv, tk=1024)
    (dnkv,) = mm_nt("kv_dn", dkv, full["w_kv"], ident, [F32], tm=2048, tk=1024)
    token = emit("layer1", {n: grads_full[n] for n in layer1})

    def attn_dn_epilogue(acc, e, rv):
        dx, dgs = norm_bwd_rows(e[0], e[1], [acc, e[2]], rv)
        return [dx, dx] + dgs

    dh2, dh2b, dg_mix1, dg_kv = mm_nt("attn_dn", dq, full["w_q"], attn_dn_epilogue, [F32, BF16], extras=[h2, dh3, dnkv],
                                      rowvecs=[after_token(small["norm_mix1"], token), small["norm_kv"]], n_sums=2,
                                      tm=512, tk=d)
    grads_small["norm_mix1"], grads_small["norm_kv"] = dg_mix1, dg_kv
    dh1, _, _ = mlp_bwd("0", dh2, dh2b, h1, small["norm_mlp0"], full["w_in0"], full["w_out0"], mlp0)

    dz, db_glu = glu_bwd(dh1, val, gate)
    grads_small["s5_b_glu"] = db_glu
    grads_full["w_glu"] = mm_tn("dw_glu", ge, dz, tn=1024)
    token = emit("layer0", {n: grads_full[n] for n in layer0})
    (dy2,) = mm_nt("glu_dy", dz, full["w_glu"], lambda acc, e, rv: [acc * _gelu_grad(e[0])], [F32], extras=[y2],
                   tm=1024, tk=1024)
    grad_x, dd, drb, drc, dlr, dli, dg_mix0 = s5_bwd(x, small["norm_mix0"], dy2, dh1, after_token(small["s5_d"], token), cs,
                                                     rb16, rbt16, rct16, lr_t, li_t)
    grads_small["s5_d"] = dd
    grads_small["s5_mats"] = (drb, drc, dlr, dli)
    grads_small["norm_mix0"] = dg_mix0
    return loss_tile, grad_x, grads_small


SMALL_NAMES = ["norm_mix", "norm_mlp", "norm_kv", "norm_final", "s5_a_re", "s5_a_im", "s5_log_dt", "s5_b_re", "s5_b_im",
               "s5_c_re", "s5_c_im", "s5_d", "s5_b_glu", "b_kv", "b_q", "sinks", "b_o"]
BIG_NAMES = ["s5_w_glu", "w_kv", "w_q", "w_o", "w_mlp_in", "w_mlp_out"]
WEIGHT_ORDER = ["norm_mix", "norm_mlp", "norm_kv", "norm_final", "s5_a_re", "s5_a_im", "s5_log_dt", "s5_b_re", "s5_b_im",
                "s5_c_re", "s5_c_im", "s5_d", "s5_w_glu", "s5_b_glu", "w_kv", "b_kv", "w_q", "b_q", "sinks", "w_o", "b_o",
                "w_mlp_in", "w_mlp_out"]


def kernel(x, norm_mix, norm_mlp, norm_kv, norm_final, s5_a_re, s5_a_im, s5_log_dt, s5_b_re, s5_b_im, s5_c_re, s5_c_im, s5_d, s5_w_glu, s5_b_glu, w_kv, b_kv, w_q, b_q, sinks, w_o, b_o, w_mlp_in, w_mlp_out, loss_target, m_norm_mix, m_norm_mlp, m_norm_kv, m_norm_final, m_s5_a_re, m_s5_a_im, m_s5_log_dt, m_s5_b_re, m_s5_b_im, m_s5_c_re, m_s5_c_im, m_s5_d, m_s5_w_glu, m_s5_b_glu, m_w_kv, m_b_kv, m_w_q, m_b_q, m_sinks, m_w_o, m_b_o, m_w_mlp_in, m_w_mlp_out, v_norm_mix, v_norm_mlp, v_norm_kv, v_norm_final, v_s5_a_re, v_s5_a_im, v_s5_log_dt, v_s5_b_re, v_s5_b_im, v_s5_c_re, v_s5_c_im, v_s5_d, v_s5_w_glu, v_s5_b_glu, v_w_kv, v_b_kv, v_w_q, v_b_q, v_sinks, v_w_o, v_b_o, v_w_mlp_in, v_w_mlp_out):
    env = dict(locals())
    w = {n: env[n] for n in WEIGHT_ORDER}
    mom = {n: env["m_" + n] for n in WEIGHT_ORDER}
    var = {n: env["v_" + n] for n in WEIGHT_ORDER}
    d = D_MODEL
    xi, yi, ci = lax.axis_index("x"), lax.axis_index("y"), lax.axis_index("c")
    chip = 2 * xi + yi
    where = jnp.stack([ci, chip]).astype(jnp.int32)

    dsh, bsh = s5_d.shape[1], s5_b_glu.shape[1]
    placed = jnp.concatenate([
        lax.dynamic_update_slice(jnp.zeros((4 * dsh,), F32), s5_d[0], (chip * dsh,)),
        lax.dynamic_update_slice(jnp.zeros((4 * bsh,), F32), s5_b_glu[0], (chip * bsh,))])
    placed = jnp.pad(placed, (0, (-placed.shape[0]) % 2048))
    placed = jnp.where(ci == 0, placed, 0.0).reshape(-1, 128)
    (gathered_rows,) = all_reduce_small("gather_vectors", [placed])
    gathered = gathered_rows.reshape(-1)
    d_full, bglu_full = gathered[:4 * dsh].reshape(1, -1), gathered[4 * dsh:].reshape(1, -1)

    big = [s5_w_glu, w_kv[None], w_q, w_o, w_mlp_in, w_mlp_out]
    entries = [(0, 0, "col"), (1, 0, "row"), (2, 0, "row"), (3, 0, "row"), (4, 0, "col"), (4, 1, "col"),
               (5, 0, "row"), (5, 1, "row")]
    names = ["w_glu", "w_kv", "w_q", "w_o", "w_in0", "w_in1", "w_out0", "w_out1"]
    kinds = dict(zip(names, [k for _, _, k in entries]))
    shard_shapes = dict(zip(names, [tuple(big[a].shape[1:]) for a, _, _ in entries]))

    placed_w = dict(zip(names, cast_place(big, entries, where)))
    gather_groups = {"glu": ["w_glu"], "mlp_in0": ["w_in0"], "mlp_out0": ["w_out0"], "attn": ["w_kv", "w_q", "w_o"],
                     "mlp1": ["w_in1", "w_out1"]}
    order = [n for members in gather_groups.values() for n in members]
    send, recv, thru, token = gather_start([placed_w[n] for n in order], [kinds[n] for n in order],
                                           [shard_shapes[n] for n in order], gathered_rows)
    started = dict(zip(order, thru))

    def need(group, after):
        members = gather_groups[group]
        ks, shapes = [kinds[n] for n in members], [shard_shapes[n] for n in members]
        landed = gather_wait("gather_wait_" + group, send, recv, [started[n] for n in members], ks, shapes, after,
                             order.index(members[0]))
        return dict(zip(members, forward_halves("forward_halves_" + group, landed, ks, shapes)))

    exchanging = {}

    def emit(group, partial):
        members = list(partial)
        ks, shapes = [kinds[n] for n in members], [shard_shapes[n] for n in members]
        landed = swap_halves("swap_halves_" + group, [partial[n] for n in members], ks)
        sums = add_halves("add_halves_" + group, [partial[n] for n in members], landed, ks, where)
        send, recv, parts, lands, tok = exchange_start("exchange_start_" + group, sums, ks, shapes)
        exchanging[group] = (members, send, recv, parts, lands)
        return tok

    s5_args = (s5_a_re[0], s5_a_im[0], s5_log_dt[0], s5_b_re[0], s5_b_im[0])
    small = {
        "norm_mix0": norm_mix[0:1] + token[0:1, 0:1], "norm_mix1": norm_mix[1:2], "norm_mlp0": norm_mlp[0:1], "norm_mlp1": norm_mlp[1:2],
        "norm_kv": norm_kv.reshape(1, d), "norm_final": norm_final.reshape(1, d), "s5_operands": s5_prep(*s5_args, s5_c_re[0], s5_c_im[0]),
        "s5_d": d_full, "s5_b_glu": bglu_full,
        "b_kv": b_kv.reshape(1, -1), "b_q": b_q, "sinks": sinks, "b_o": b_o,
    }
    loss_row, grad_x, gs = _local_step(x[0], loss_target[0], small, need, emit)

    mats = s5_compact(*gs["s5_mats"])
    rows = [gs["norm_mix0"], gs["norm_mix1"], gs["norm_mlp0"], gs["norm_mlp1"], gs["norm_kv"], gs["norm_final"], gs["s5_d"],
            gs["b_q"], gs["b_o"], gs["s5_b_glu"], gs["b_kv"], gs["sinks"], loss_row, jnp.zeros((2, d), F32)]
    vecs, mats = all_reduce_small("reduce_small", [jnp.concatenate(rows, axis=0), mats])
    grads = split_vectors(where, vecs, dsh, bsh)
    loss = grads.pop("loss")[0, 0]
    g_are, g_aim, g_dt, g_bre, g_bim, dc_re, dc_im = s5_param_bwd(mats, *s5_args)
    grads.update({"s5_a_re": g_are[None], "s5_a_im": g_aim[None], "s5_log_dt": g_dt[None], "s5_b_re": g_bre[None],
                  "s5_b_im": g_bim[None], "s5_c_re": dc_re[None], "s5_c_im": dc_im[None]})

    reduced = [None] * len(big)
    where_of = dict(zip(names, entries))
    for group, after in (("layer1", grad_x), ("layer0", mats)):
        members, send, recv, parts, lands = exchanging[group]
        ks, shapes = [kinds[n] for n in members], [shard_shapes[n] for n in members]
        parts, lands = exchange_wait("exchange_wait_" + group, send, recv, parts, lands, ks, shapes, after)
        targets = [where_of[n][0] for n in members]
        sums = sum_shards("sum_shards_" + group, parts, lands, ks, shapes, where, [where_of[n][1] for n in members],
                          [big[a].shape[0] for a in targets], [reduced[a] for a in targets])
        for a, arr in zip(targets, sums):
            reduced[a] = arr
    reduced = share_halves(reduced, entries)
    for n, g in zip(BIG_NAMES, reduced):
        grads[n] = g.reshape(w[n].shape)

    delta, new_m, new_v = {}, {}, {}
    for n in BIG_NAMES:
        flat = lambda a: a.reshape(-1, a.shape[-1])
        go, dl, nm, nv = adamw("adamw_" + n, flat(w[n]), flat(grads[n]), flat(mom[n]), flat(var[n]))
        grads[n], delta[n], new_m[n], new_v[n] = (t.reshape(w[n].shape) for t in (go, dl, nm, nv))

    def view(n, a):
        return a.reshape(1, -1) if a.ndim == 1 else jnp.swapaxes(a, -1, -2) if n in ("s5_b_re", "s5_b_im") else a

    sw, sg, sm, sv = ([view(n, t[n]) for n in SMALL_NAMES] for t in (w, grads, mom, var))
    for n, a, b, c_ in zip(SMALL_NAMES, *adamw_native("adamw_small", sw, sg, sm, sv)):
        delta[n], new_m[n], new_v[n] = (view(n, t) if t.ndim == 4 else t for t in (a, b, c_))

    out = [loss.reshape(()), grad_x[None]]
    for table in (grads, delta, new_m, new_v):
        out += [table[n].reshape(w[n].shape) for n in WEIGHT_ORDER]
    return tuple(out)
```

```python
import math

import jax
import jax.numpy as jnp
from jax import lax
from jax.experimental import pallas as pl
from jax.experimental.pallas import tpu as pltpu

F32 = jnp.float32
BF16 = jnp.bfloat16

D_MODEL = 1024
S5_GROUPS = 64
S5_GROUP = 16
S5_STATE = 64
N_KV = 4
N_Q = 16
HEAD_DIM = 64
BLOCK = 128
NORM_EPS = 1e-5
LAMBDA_RE_MAX = -1e-4
ADAM_LR, ADAM_B1, ADAM_B2, ADAM_EPS, ADAM_WD, ADAM_STEP = 0.001, 0.9, 0.999, 1e-08, 0.01, 10

VMEM_LIMIT_BYTES = 56 * 1024 * 1024
S5_CHUNK = 256
S5_BLOCKS = 4
MESH = pl.DeviceIdType.MESH


def _params(sem=None):
    return pltpu.CompilerParams(dimension_semantics=sem, vmem_limit_bytes=VMEM_LIMIT_BYTES)


def _sds(shape, dtype):
    return jax.ShapeDtypeStruct(shape, dtype)


def _rms_hat(xv):
    r = lax.rsqrt(jnp.mean(xv * xv, axis=-1, keepdims=True) + NORM_EPS)
    return xv * r, r


def mm_nn(name, a, w, col_offsets, n_out, epilogue, out_dtypes, extras=(), rowvecs=(), n_sums=0, tm=1024, tn=512):
    m, k = a.shape
    tm, tn = min(tm, m), min(tn, n_out)
    nw, ne, nr, no = len(col_offsets), len(extras), len(rowvecs), len(out_dtypes)

    def body(a_ref, *refs):
        w_refs, e_refs, r_refs = refs[:nw], refs[nw:nw + ne], refs[nw + ne:nw + ne + nr]
        o_refs, s_refs = refs[nw + ne + nr:nw + ne + nr + no], refs[nw + ne + nr + no:]
        av = a_ref[...]
        accs = [jnp.dot(av, w_ref[...], preferred_element_type=F32) for w_ref in w_refs]
        outs = epilogue(accs, [e[...] for e in e_refs], [r[...] for r in r_refs])
        for o_ref, o in zip(o_refs, outs[:no]):
            o_ref[...] = o.astype(o_ref.dtype)
        if n_sums:
            @pl.when(pl.program_id(1) == 0)
            def _():
                for s_ref in s_refs:
                    s_ref[...] = jnp.zeros_like(s_ref)

            for s_ref, val in zip(s_refs, outs[no:]):
                s_ref[...] += val

    def wspec(off):
        return pl.BlockSpec((k, tn), lambda j, i, off=off: (0, off // tn + j))

    def rspec(off):
        return pl.BlockSpec((1, tn), lambda j, i, off=off: (0, off // tn + j))

    tile = pl.BlockSpec((tm, tn), lambda j, i: (i, j))
    in_specs = ([pl.BlockSpec((tm, k), lambda j, i: (i, 0))] + [wspec(o) for o in col_offsets]
                + [tile] * ne + [rspec(o) for _, o in rowvecs])
    sem = ("parallel", "arbitrary") if n_sums else ("parallel", "parallel")
    return pl.pallas_call(
        body, grid=(n_out // tn, m // tm), in_specs=in_specs,
        out_specs=[tile] * no + [pl.BlockSpec((1, tn), lambda j, i: (0, j))] * n_sums,
        out_shape=[_sds((m, n_out), dt) for dt in out_dtypes] + [_sds((1, n_out), F32)] * n_sums, name=name,
        compiler_params=_params(sem))(a, *([w] * nw), *extras, *[r for r, _ in rowvecs])


def mm_nt(name, g, w, epilogue, out_dtypes, extras=(), rowvecs=(), n_sums=0, tm=512, tk=512):
    m, n = g.shape
    k = w.shape[0]
    tm, tk = min(tm, m), min(tk, k)
    ne, nr, no = len(extras), len(rowvecs), len(out_dtypes)

    def body(g_ref, w_ref, *refs):
        e_refs, r_refs, o_refs, s_refs = refs[:ne], refs[ne:ne + nr], refs[ne + nr:ne + nr + no], refs[ne + nr + no:]
        acc = lax.dot_general(g_ref[...], w_ref[...], (((1,), (1,)), ((), ())), preferred_element_type=F32)
        outs = epilogue(acc, [e[...] for e in e_refs], [r[...] for r in r_refs])
        for o_ref, o in zip(o_refs, outs[:no]):
            o_ref[...] = o.astype(o_ref.dtype)
        if n_sums:
            @pl.when(pl.program_id(0) == 0)
            def _():
                for s_ref in s_refs:
                    s_ref[...] = jnp.zeros_like(s_ref)

            for s_ref, val in zip(s_refs, outs[no:]):
                s_ref[...] += val

    tile = pl.BlockSpec((tm, tk), lambda i, j: (i, j))
    vec = pl.BlockSpec((1, tk), lambda i, j: (0, j))
    sem = ("arbitrary", "parallel") if n_sums else ("parallel", "parallel")
    return pl.pallas_call(
        body, grid=(m // tm, k // tk),
        in_specs=[pl.BlockSpec((tm, n), lambda i, j: (i, 0)), pl.BlockSpec((tk, n), lambda i, j: (j, 0))]
        + [tile] * ne + [vec] * nr,
        out_specs=[tile] * no + [vec] * n_sums,
        out_shape=[_sds((m, k), dt) for dt in out_dtypes] + [_sds((1, k), F32)] * n_sums, name=name,
        compiler_params=_params(sem))(g, w, *extras, *rowvecs)


def mm_tn(name, a, g, tk=512, tn=512):
    m, k = a.shape
    n = g.shape[1]
    tk, tn = min(tk, k), min(tn, n)

    def body(a_ref, g_ref, o_ref):
        acc = lax.dot_general(a_ref[...], g_ref[...], (((0,), (0,)), ((), ())), preferred_element_type=F32)
        o_ref[...] = acc.astype(o_ref.dtype)

    return pl.pallas_call(
        body, grid=(k // tk, n // tn),
        in_specs=[pl.BlockSpec((m, tk), lambda i, j: (0, i)), pl.BlockSpec((m, tn), lambda i, j: (0, j))],
        out_specs=pl.BlockSpec((tk, tn), lambda i, j: (i, j)), out_shape=_sds((k, n), BF16), name=name,
        compiler_params=_params(("parallel", "parallel")))(a, g)


def _row_mask(tc):
    row = lax.broadcasted_iota(jnp.int32, (8 * tc, 256), 0) % 8
    col = lax.broadcasted_iota(jnp.int32, (8 * tc, 256), 1) // 32
    return row == col


def _expand_rows(val, mask):
    tc, width = val.shape
    rep = jnp.broadcast_to(val[:, None, :], (tc, 8, width)).reshape(8 * tc, width)
    return jnp.where(mask, rep, 0.0).astype(BF16)


def _stage(ref, val):
    ref[0] = val[:, 0:128]
    ref[1] = val[:, 128:256]


def _gather_rows(src_ref, tc):
    halves = []
    for half in range(2):
        col = lax.broadcasted_iota(jnp.int32, (tc, 128), 1) // 32 + 4 * half
        out = jnp.zeros((tc, 128), F32)
        for s8 in range(4 * half, 4 * half + 4):
            out = jnp.where(col == s8, src_ref.at[half][pl.ds(s8, tc, stride=8), :], out)
        halves.append(out)
    return jnp.concatenate(halves, axis=1)


def _gelu(x):
    c = math.sqrt(2.0 / math.pi)
    return 0.5 * x * (1.0 + jnp.tanh(c * (x + 0.044715 * x * x * x)))


def _gelu_grad(x):
    c = math.sqrt(2.0 / math.pi)
    t = jnp.tanh(c * (x + 0.044715 * x * x * x))
    return 0.5 * (1.0 + t) + 0.5 * x * (1.0 - t * t) * c * (1.0 + 3.0 * 0.044715 * x * x)


def s5_fwd(x, gain, d_skip, rb, rc, lam_r, lam_i):
    n_rows = x.shape[0]
    tc = min(S5_CHUNK, n_rows)
    nc = n_rows // tc

    def body(x_ref, g_ref, d_ref, rb_ref, rc_ref, lr_ref, li_ref, ge_ref, y2_ref, cs_ref, bux, yrows, carry):
        i = pl.program_id(0)
        u = _rms_hat(x_ref[...])[0] * g_ref[...]

        @pl.when(i == 0)
        def _():
            carry[...] = jnp.zeros_like(carry)

        cs_ref[0] = carry[...]
        mask = _row_mask(tc)
        for blk in range(S5_BLOCKS):
            lhs = _expand_rows(u[:, blk * 256:(blk + 1) * 256], mask)
            bux[blk] = jnp.dot(lhs, rb_ref[blk], preferred_element_type=F32)
        lam = [(lr_ref[blk], li_ref[blk]) for blk in range(S5_BLOCKS)]

        def step(t, c):
            r0 = pl.multiple_of(t * 8, 8)
            new = []
            for blk in range(S5_BLOCKS):
                xr, xi = c[2 * blk], c[2 * blk + 1]
                lr, li = lam[blk]
                nr = lr * xr - li * xi + bux[blk, pl.ds(r0, 8), 0:128]
                ni = lr * xi + li * xr + bux[blk, pl.ds(r0, 8), 128:256]
                bux[blk, pl.ds(r0, 8), 0:128] = nr
                bux[blk, pl.ds(r0, 8), 128:256] = ni
                new += [nr, ni]
            return tuple(new)

        c0 = []
        for blk in range(S5_BLOCKS):
            c0 += [carry[blk, :, 0:128], carry[blk, :, 128:256]]
        cn = lax.fori_loop(0, tc, step, tuple(c0), unroll=4)
        for blk in range(S5_BLOCKS):
            carry[blk, :, 0:128] = cn[2 * blk]
            carry[blk, :, 128:256] = cn[2 * blk + 1]
        for blk in range(S5_BLOCKS):
            _stage(yrows, jnp.dot(bux[blk].astype(BF16), rc_ref[blk], preferred_element_type=F32))
            sl = slice(blk * 256, (blk + 1) * 256)
            y2 = _gather_rows(yrows, tc) + d_ref[:, sl] * u[:, sl]
            y2_ref[:, sl] = y2
            ge_ref[:, sl] = _gelu(y2).astype(BF16)

    row = pl.BlockSpec((tc, D_MODEL), lambda i: (i, 0))
    vec = pl.BlockSpec((1, D_MODEL), lambda i: (0, 0))
    mat = pl.BlockSpec((S5_BLOCKS, 256, 256), lambda i: (0, 0, 0))
    lamspec = pl.BlockSpec((S5_BLOCKS, 8, 128), lambda i: (0, 0, 0))
    return pl.pallas_call(
        body, grid=(nc,),
        in_specs=[row, vec, vec, mat, mat, lamspec, lamspec],
        out_specs=[row, row, pl.BlockSpec((1, S5_BLOCKS, 8, 256), lambda i: (i, 0, 0, 0))],
        out_shape=[_sds((n_rows, D_MODEL), BF16), _sds((n_rows, D_MODEL), F32), _sds((nc, S5_BLOCKS, 8, 256), F32)],
        scratch_shapes=[pltpu.VMEM((S5_BLOCKS, 8 * tc, 256), F32), pltpu.VMEM((2, 8 * tc, 128), F32),
                        pltpu.VMEM((S5_BLOCKS, 8, 256), F32)],
        name="s5_fwd", compiler_params=_params(("arbitrary",)))(x, gain, d_skip, rb, rc, lam_r, lam_i)


def s5_bwd(x, gain, dy2, res, d_skip, cs, rb, rbt, rct, lam_r, lam_i):
    n_rows = x.shape[0]
    tc = min(S5_CHUNK, n_rows)
    nc = n_rows // tc

    def body(x_ref, g_ref, dy_ref, res_ref, d_ref, cs_ref, rb_ref, rbt_ref, rct_ref, lr_ref, li_ref,
             dx_ref, dd_ref, drb_ref, drc_ref, dlr_ref, dli_ref, dg_ref, tmp, du, lhsu, lhsd, xs, adj, acarry):
        i = pl.program_id(0)
        u = _rms_hat(x_ref[...])[0] * g_ref[...]

        @pl.when(i == 0)
        def _():
            acarry[...] = jnp.zeros_like(acarry)
            dd_ref[...] = jnp.zeros_like(dd_ref)
            drb_ref[...] = jnp.zeros_like(drb_ref)
            drc_ref[...] = jnp.zeros_like(drc_ref)
            dlr_ref[...] = jnp.zeros_like(dlr_ref)
            dli_ref[...] = jnp.zeros_like(dli_ref)
            dg_ref[...] = jnp.zeros_like(dg_ref)

        dd_ref[...] += jnp.sum(dy_ref[...] * u, axis=0, keepdims=True)
        mask = _row_mask(tc)
        for blk in range(S5_BLOCKS):
            sl = slice(blk * 256, (blk + 1) * 256)
            lhsu[blk] = _expand_rows(u[:, sl], mask)
            xs[blk] = jnp.dot(lhsu[blk], rb_ref[blk], preferred_element_type=F32)
            lhsd[blk] = _expand_rows(dy_ref[:, sl], mask)
            adj[blk] = jnp.dot(lhsd[blk], rct_ref[blk], preferred_element_type=F32)
        lam = [(lr_ref[blk], li_ref[blk]) for blk in range(S5_BLOCKS)]

        def fstep(t, c):
            r0 = pl.multiple_of(t * 8, 8)
            new = []
            for blk in range(S5_BLOCKS):
                xr, xi = c[2 * blk], c[2 * blk + 1]
                lr, li = lam[blk]
                nr = lr * xr - li * xi + xs[blk, pl.ds(r0, 8), 0:128]
                ni = lr * xi + li * xr + xs[blk, pl.ds(r0, 8), 128:256]
                xs[blk, pl.ds(r0, 8), 0:128] = nr
                xs[blk, pl.ds(r0, 8), 128:256] = ni
                new += [nr, ni]
            return tuple(new)

        c0 = []
        for blk in range(S5_BLOCKS):
            c0 += [cs_ref[0, blk, :, 0:128], cs_ref[0, blk, :, 128:256]]
        lax.fori_loop(0, tc, fstep, tuple(c0), unroll=4)

        def bstep(k, c):
            t = tc - 1 - k
            r0 = pl.multiple_of(t * 8, 8)
            rp = pl.multiple_of(jnp.maximum(t - 1, 0) * 8, 8)
            first = t == 0
            new_a, new_g = [], []
            for blk in range(S5_BLOCKS):
                ar, ai = c[0][2 * blk], c[0][2 * blk + 1]
                glr, gli = c[1][2 * blk], c[1][2 * blk + 1]
                lr, li = lam[blk]
                nr = lr * ar + li * ai + adj[blk, pl.ds(r0, 8), 0:128]
                ni = lr * ai - li * ar + adj[blk, pl.ds(r0, 8), 128:256]
                adj[blk, pl.ds(r0, 8), 0:128] = nr
                adj[blk, pl.ds(r0, 8), 128:256] = ni
                pr = jnp.where(first, cs_ref[0, blk, :, 0:128], xs[blk, pl.ds(rp, 8), 0:128])
                pi = jnp.where(first, cs_ref[0, blk, :, 128:256], xs[blk, pl.ds(rp, 8), 128:256])
                new_a += [nr, ni]
                new_g += [glr + nr * pr + ni * pi, gli + ni * pr - nr * pi]
            return tuple(new_a), tuple(new_g)

        a0, g0 = [], []
        for blk in range(S5_BLOCKS):
            a0 += [acarry[blk, :, 0:128], acarry[blk, :, 128:256]]
            g0 += [dlr_ref[blk], dli_ref[blk]]
        an, gn = lax.fori_loop(0, tc, bstep, (tuple(a0), tuple(g0)), unroll=2)
        for blk in range(S5_BLOCKS):
            acarry[blk, :, 0:128] = an[2 * blk]
            acarry[blk, :, 128:256] = an[2 * blk + 1]
            dlr_ref[blk] = gn[2 * blk]
            dli_ref[blk] = gn[2 * blk + 1]
        for blk in range(S5_BLOCKS):
            sl = slice(blk * 256, (blk + 1) * 256)
            ab = adj[blk].astype(BF16)
            _stage(tmp, jnp.dot(ab, rbt_ref[blk], preferred_element_type=F32))
            du[:, sl] = _gather_rows(tmp, tc) + d_ref[:, sl] * dy_ref[:, sl]
            drb_ref[blk] += lax.dot_general(lhsu[blk], ab, (((0,), (0,)), ((), ())), preferred_element_type=F32)
            drc_ref[blk] += lax.dot_general(lhsd[blk], xs[blk].astype(BF16), (((0,), (0,)), ((), ())),
                                            preferred_element_type=F32)
        xh, r = _rms_hat(x_ref[...])
        dg_ref[...] += jnp.sum(du[...] * xh, axis=0, keepdims=True)
        dxh = du[...] * g_ref[...]
        dx_ref[...] = r * (dxh - xh * jnp.mean(dxh * xh, axis=-1, keepdims=True)) + res_ref[...]

    rev = pl.BlockSpec((tc, D_MODEL), lambda i: (nc - 1 - i, 0))
    vec = pl.BlockSpec((1, D_MODEL), lambda i: (0, 0))
    mat = pl.BlockSpec((S5_BLOCKS, 256, 256), lambda i: (0, 0, 0))
    lamspec = pl.BlockSpec((S5_BLOCKS, 8, 128), lambda i: (0, 0, 0))
    big = pltpu.VMEM((S5_BLOCKS, 8 * tc, 256), F32)
    bigb = pltpu.VMEM((S5_BLOCKS, 8 * tc, 256), BF16)
    return pl.pallas_call(
        body, grid=(nc,),
        in_specs=[rev, vec, rev, rev, vec, pl.BlockSpec((1, S5_BLOCKS, 8, 256), lambda i: (nc - 1 - i, 0, 0, 0)),
                  mat, mat, mat, lamspec, lamspec],
        out_specs=[rev, vec, mat, mat, lamspec, lamspec, vec],
        out_shape=[_sds((n_rows, D_MODEL), F32), _sds((1, D_MODEL), F32), _sds((S5_BLOCKS, 256, 256), F32),
                   _sds((S5_BLOCKS, 256, 256), F32), _sds((S5_BLOCKS, 8, 128), F32), _sds((S5_BLOCKS, 8, 128), F32),
                   _sds((1, D_MODEL), F32)],
        scratch_shapes=[pltpu.VMEM((2, 8 * tc, 128), F32), pltpu.VMEM((tc, D_MODEL), F32), bigb, bigb, big, big,
                        pltpu.VMEM((S5_BLOCKS, 8, 256), F32)],
        name="s5_bwd", compiler_params=_params(("arbitrary",)))(
            x, gain, dy2, res, d_skip, cs, rb, rbt, rct, lam_r, lam_i)


def _s5_views(a_re, a_im, log_dt, b_re, b_im):
    return a_re[:, None, :], a_im[:, None, :], log_dt[:, None, None], jnp.swapaxes(b_re, 1, 2), jnp.swapaxes(b_im, 1, 2)


def _s5_factors(a_re, a_im, log_dt):
    lr, li, dt = jnp.minimum(a_re, LAMBDA_RE_MAX), a_im, jnp.exp(log_dt)
    mag, ang = jnp.exp(lr * dt), li * dt
    lbr, lbi = mag * jnp.cos(ang), mag * jnp.sin(ang)
    den = lr * lr + li * li
    fr, fi = ((lbr - 1.0) * lr + lbi * li) / den, (lbi * lr - (lbr - 1.0) * li) / den
    return lr, li, dt, lbr, lbi, fr, fi, den


def s5_prep(a_re, a_im, log_dt, b_re, b_im, c_re, c_im):
    def body(ar_ref, ai_ref, t_ref, br_ref, bi_ref, cr_ref, ci_ref, rb_ref, rbt_ref, rc_ref, rct_ref, lr_ref, li_ref):
        _, _, _, lbr, lbi, fr, fi, _ = _s5_factors(ar_ref[...], ai_ref[...], t_ref[...])
        lr_ref[...] = lbr
        li_ref[...] = lbi
        bre = fr * br_ref[...] - fi * bi_ref[...]
        bim = fr * bi_ref[...] + fi * br_ref[...]
        even = (lax.broadcasted_iota(jnp.int32, (256, S5_STATE), 0) // S5_GROUP) % 2 == 0

        def assemble(re, im):
            re, im = re.reshape(256, S5_STATE), im.reshape(256, S5_STATE)
            return jnp.concatenate([jnp.where(even, re, 0.0), jnp.where(even, 0.0, re), jnp.where(even, im, 0.0),
                                    jnp.where(even, 0.0, im)], axis=1)

        for blk in range(S5_BLOCKS):
            sl = slice(16 * blk, 16 * blk + 16)
            rb = assemble(bre[sl], bim[sl])
            rct = assemble(cr_ref[sl], -ci_ref[sl])
            rb_ref[blk] = rb.astype(BF16)
            rbt_ref[blk] = rb.T.astype(BF16)
            rct_ref[blk] = rct.astype(BF16)
            rc_ref[blk] = rct.T.astype(BF16)

    vm = pl.BlockSpec(memory_space=pltpu.VMEM)
    mat = _sds((S5_BLOCKS, 256, 256), BF16)
    lam = _sds((S5_GROUPS, 1, S5_STATE), F32)
    rb, rbt, rc, rct, lam_r, lam_i = pl.pallas_call(
        body, in_specs=[vm] * 7, out_specs=[vm] * 6, out_shape=[mat, mat, mat, mat, lam, lam], name="s5_prep",
        compiler_params=_params())(*_s5_views(a_re, a_im, log_dt, b_re, b_im), c_re, c_im)
    return rb, rbt, rc, rct, lam_r.reshape(S5_BLOCKS, 8, 128), lam_i.reshape(S5_BLOCKS, 8, 128)


def s5_param_bwd(mats, a_re, a_im, log_dt, b_re, b_im):
    def body(m_ref, glr_ref, gli_ref, ar_ref, ai_ref, t_ref, br_ref, bi_ref,
             dar_ref, dai_ref, dt_ref, dbr_ref, dbi_ref, dcr_ref, dci_ref):
        lr, li, dt, lbr, lbi, fr, fi, den = _s5_factors(ar_ref[...], ai_ref[...], t_ref[...])
        shape = (S5_GROUPS, S5_GROUP, S5_STATE)
        gbr, gbi = m_ref[0:1024, 0:64].reshape(shape), m_ref[0:1024, 64:128].reshape(shape)
        dcr_ref[...] = m_ref[1024:2048, 0:64].reshape(shape)
        dci_ref[...] = -m_ref[1024:2048, 64:128].reshape(shape)
        br, bi = br_ref[...], bi_ref[...]
        dbr_ref[...] = fr * gbr + fi * gbi
        dbi_ref[...] = fr * gbi - fi * gbr
        dfr = jnp.sum(gbr * br + gbi * bi, axis=1, keepdims=True)
        dfi = jnp.sum(gbi * br - gbr * bi, axis=1, keepdims=True)
        nr, ni = (dfr * lr - dfi * li) / den, (dfr * li + dfi * lr) / den
        qr, qi = (fr * lr + fi * li) / den, (fi * lr - fr * li) / den
        lam_r, lam_i = -(dfr * qr + dfi * qi), -(dfi * qr - dfr * qi)
        gr, gi = glr_ref[...] + nr, gli_ref[...] + ni
        zr, zi = gr * lbr + gi * lbi, gi * lbr - gr * lbi
        a = ar_ref[...]
        dar_ref[...] = (lam_r + zr * dt) * jnp.where(a < LAMBDA_RE_MAX, 1.0, jnp.where(a == LAMBDA_RE_MAX, 0.5, 0.0))
        dai_ref[...] = lam_i + zi * dt
        dt_ref[...] = jnp.sum(zr * lr + zi * li, axis=2, keepdims=True) * dt

    vm = pl.BlockSpec(memory_space=pltpu.VMEM)
    state = _sds((S5_GROUPS, 1, S5_STATE), F32)
    wide = _sds((S5_GROUPS, S5_GROUP, S5_STATE), F32)
    glr = mats[2048:2080].reshape(S5_GROUPS, 1, S5_STATE)
    gli = mats[2080:2112].reshape(S5_GROUPS, 1, S5_STATE)
    dar, dai, ddt, dbr, dbi, dcr, dci = pl.pallas_call(
        body, in_specs=[vm] * 8, out_specs=[vm] * 7,
        out_shape=[state, state, _sds((S5_GROUPS, 1, 1), F32), wide, wide, wide, wide], name="s5_param_bwd",
        compiler_params=_params())(mats, glr, gli, *_s5_views(a_re, a_im, log_dt, b_re, b_im))
    return (dar.reshape(S5_GROUPS, S5_STATE), dai.reshape(S5_GROUPS, S5_STATE), ddt.reshape(S5_GROUPS),
            jnp.swapaxes(dbr, 1, 2), jnp.swapaxes(dbi, 1, 2), dcr, dci)


def s5_compact(drb, drct, dlr, dli):
    def body(drb_ref, drct_ref, dlr_ref, dli_ref, o_ref):
        even = (lax.broadcasted_iota(jnp.int32, (256, 64), 0) // S5_GROUP) % 2 == 0
        for blk in range(S5_BLOCKS):
            for k, ref in enumerate((drb_ref, drct_ref)):
                m = ref[blk]
                re = jnp.where(even, m[:, 0:64], m[:, 64:128])
                im = jnp.where(even, m[:, 128:192], m[:, 192:256])
                o_ref[pl.ds(k * 1024 + blk * 256, 256), :] = jnp.concatenate([re, im], axis=1)
            o_ref[pl.ds(2048 + blk * 8, 8), :] = dlr_ref[blk]
            o_ref[pl.ds(2080 + blk * 8, 8), :] = dli_ref[blk]

    vm = pl.BlockSpec(memory_space=pltpu.VMEM)
    return pl.pallas_call(body, in_specs=[vm] * 4, out_specs=vm, out_shape=_sds((2112, 128), F32), name="s5_compact",
                          compiler_params=_params())(drb, drct, dlr, dli)


NEG = -1e30


GROUP = N_Q // N_KV


def _attn_masks(n):
    qi = lax.broadcasted_iota(jnp.int32, (GROUP * BLOCK, BLOCK), 0) % BLOCK
    kj = lax.broadcasted_iota(jnp.int32, (GROUP * BLOCK, BLOCK), 1)
    return jnp.logical_and(kj > qi, n > 0), kj <= qi


def _stack_heads(ref, kh):
    return jnp.concatenate([ref[:, (GROUP * kh + g) * HEAD_DIM:(GROUP * kh + g + 1) * HEAD_DIM] for g in range(GROUP)], axis=0)


def _unstack_heads(val):
    return jnp.concatenate([val[g * BLOCK:(g + 1) * BLOCK] for g in range(GROUP)], axis=1)


def _sink_column(sink_ref, kh):
    grp = lax.broadcasted_iota(jnp.int32, (GROUP * BLOCK, 1), 0) // BLOCK
    col = jnp.zeros((GROUP * BLOCK, 1), F32)
    for g in range(GROUP):
        col = jnp.where(grp == g, sink_ref[GROUP * kh + g], col)
    return col, grp


def _attn_exp(q4, kp, kc, sink, mask_p, mask_c):
    scale = 1.0 / math.sqrt(HEAD_DIM)
    nt = (((1,), (1,)), ((), ()))
    sp = jnp.where(mask_p, lax.dot_general(q4, kp, nt, preferred_element_type=F32) * scale, NEG)
    sc = jnp.where(mask_c, lax.dot_general(q4, kc, nt, preferred_element_type=F32) * scale, NEG)
    m = jnp.maximum(jnp.maximum(jnp.max(sp, axis=-1, keepdims=True), jnp.max(sc, axis=-1, keepdims=True)), sink)
    pp = jnp.exp(sp - m)
    pc = jnp.exp(sc - m)
    ps = jnp.exp(sink - m)
    inv = 1.0 / (jnp.sum(pp, axis=-1, keepdims=True) + jnp.sum(pc, axis=-1, keepdims=True) + ps)
    return pp, pc, ps, inv


def attn_fwd(q, kv, sinks):
    n_rows = q.shape[0]
    nb = n_rows // BLOCK

    def body(sink_ref, q_ref, kvp_ref, kvc_ref, o_ref):
        n = pl.program_id(0)
        mask_p, mask_c = _attn_masks(n)
        outs = []
        for kh in range(N_KV):
            ks, vs = slice(kh * HEAD_DIM, (kh + 1) * HEAD_DIM), slice((N_KV + kh) * HEAD_DIM, (N_KV + kh + 1) * HEAD_DIM)
            sink, _ = _sink_column(sink_ref, kh)
            pp, pc, _, inv = _attn_exp(_stack_heads(q_ref, kh), kvp_ref[:, ks], kvc_ref[:, ks], sink, mask_p, mask_c)
            o4 = (jnp.dot(pp.astype(BF16), kvp_ref[:, vs], preferred_element_type=F32)
                  + jnp.dot(pc.astype(BF16), kvc_ref[:, vs], preferred_element_type=F32)) * inv
            outs.append(_unstack_heads(o4))
        o_ref[...] = jnp.concatenate(outs, axis=1).astype(BF16)

    kvw = 2 * N_KV * HEAD_DIM
    return pl.pallas_call(
        body, grid=(nb,),
        in_specs=[pl.BlockSpec(memory_space=pltpu.SMEM), pl.BlockSpec((BLOCK, D_MODEL), lambda n: (n, 0)),
                  pl.BlockSpec((BLOCK, kvw), lambda n: (jnp.maximum(n - 1, 0), 0)), pl.BlockSpec((BLOCK, kvw), lambda n: (n, 0))],
        out_specs=pl.BlockSpec((BLOCK, D_MODEL), lambda n: (n, 0)), out_shape=_sds((n_rows, D_MODEL), BF16),
        name="attn_fwd", compiler_params=_params(("parallel",)))(sinks, q, kv, kv)


def attn_bwd(q, kv, do, sinks):
    n_rows = q.shape[0]
    nb = n_rows // BLOCK
    kvw = 2 * N_KV * HEAD_DIM
    tn = (((0,), (0,)), ((), ()))
    nt = (((1,), (1,)), ((), ()))
    scale = 1.0 / math.sqrt(HEAD_DIM)

    def body(sink_ref, q_ref, kvp_ref, kvc_ref, do_ref, dq_ref, dbq_ref, dprev_ref, dcur_ref, dsink_ref):
        n = pl.program_id(0)
        mask_p, mask_c = _attn_masks(n)
        lane = lax.broadcasted_iota(jnp.int32, (1, D_MODEL), 1)
        dqs, dsink = [], jnp.zeros((1, D_MODEL), F32)
        dkp, dkc, dvp, dvc = [], [], [], []
        for kh in range(N_KV):
            ks, vs = slice(kh * HEAD_DIM, (kh + 1) * HEAD_DIM), slice((N_KV + kh) * HEAD_DIM, (N_KV + kh + 1) * HEAD_DIM)
            q4, do4 = _stack_heads(q_ref, kh), _stack_heads(do_ref, kh)
            kp, kc, vp, vc = kvp_ref[:, ks], kvc_ref[:, ks], kvp_ref[:, vs], kvc_ref[:, vs]
            sink, grp = _sink_column(sink_ref, kh)
            pp, pc, ps, inv = _attn_exp(q4, kp, kc, sink, mask_p, mask_c)
            pp, pc = pp * inv, pc * inv
            dpp = lax.dot_general(do4, vp, nt, preferred_element_type=F32)
            dpc = lax.dot_general(do4, vc, nt, preferred_element_type=F32)
            delta = jnp.sum(pp * dpp, axis=-1, keepdims=True) + jnp.sum(pc * dpc, axis=-1, keepdims=True)
            dsp = (pp * (dpp - delta) * scale).astype(BF16)
            dsc = (pc * (dpc - delta) * scale).astype(BF16)
            dsk = ps * inv * delta
            for g in range(GROUP):
                dsink = dsink + jnp.where(lane == GROUP * kh + g, -jnp.sum(jnp.where(grp == g, dsk, 0.0)), 0.0)
            dqs.append(_unstack_heads(jnp.dot(dsp, kp, preferred_element_type=F32)
                                      + jnp.dot(dsc, kc, preferred_element_type=F32)))
            dkp.append(lax.dot_general(dsp, q4, tn, preferred_element_type=F32))
            dkc.append(lax.dot_general(dsc, q4, tn, preferred_element_type=F32))
            dvp.append(lax.dot_general(pp.astype(BF16), do4, tn, preferred_element_type=F32))
            dvc.append(lax.dot_general(pc.astype(BF16), do4, tn, preferred_element_type=F32))
        dq = jnp.concatenate(dqs, axis=1)
        dq_ref[...] = dq.astype(BF16)
        dprev_ref[0] = jnp.concatenate(dkp + dvp, axis=1)
        dcur_ref[0] = jnp.concatenate(dkc + dvc, axis=1)

        @pl.when(n == 0)
        def _():
            dbq_ref[...] = jnp.zeros_like(dbq_ref)
            dsink_ref[...] = jnp.zeros_like(dsink_ref)

        dbq_ref[...] += jnp.sum(dq, axis=0, keepdims=True)
        dsink_ref[...] += dsink

    blk = pl.BlockSpec((BLOCK, D_MODEL), lambda n: (n, 0))
    part = pl.BlockSpec((1, BLOCK, kvw), lambda n: (n, 0, 0))
    return pl.pallas_call(
        body, grid=(nb,),
        in_specs=[pl.BlockSpec(memory_space=pltpu.SMEM), blk,
                  pl.BlockSpec((BLOCK, kvw), lambda n: (jnp.maximum(n - 1, 0), 0)), pl.BlockSpec((BLOCK, kvw), lambda n: (n, 0)), blk],
        out_specs=[blk, pl.BlockSpec((1, D_MODEL), lambda n: (0, 0)), part, part, pl.BlockSpec((1, D_MODEL), lambda n: (0, 0))],
        out_shape=[_sds((n_rows, D_MODEL), BF16), _sds((1, D_MODEL), F32), _sds((nb, BLOCK, kvw), F32),
                   _sds((nb, BLOCK, kvw), F32), _sds((1, D_MODEL), F32)],
        name="attn_bwd", compiler_params=_params(("arbitrary",)))(sinks, q, kv, kv, do)


def kv_combine(dprev, dcur):
    nb, _, kvw = dprev.shape

    def body(dcur_ref, dprev_ref, dkv_ref, db_ref):
        total = jnp.zeros((1, kvw), F32)
        for m in range(nb):
            dkv = dcur_ref[m] + dprev_ref[m + 1] if m + 1 < nb else dcur_ref[m]
            dkv_ref[m * BLOCK:(m + 1) * BLOCK, :] = dkv.astype(BF16)
            total = total + jnp.sum(dkv, axis=0, keepdims=True)
        db_ref[...] = jnp.concatenate([total, jnp.zeros((1, D_MODEL - kvw), F32)], axis=1)

    vm = pl.BlockSpec(memory_space=pltpu.VMEM)
    return pl.pallas_call(body, in_specs=[vm, vm], out_specs=[vm, vm],
                          out_shape=[_sds((nb * BLOCK, kvw), BF16), _sds((1, D_MODEL), F32)], name="kv_combine",
                          compiler_params=_params())(dcur, dprev)


def glu_bwd(dout, val, gate, tm=256):
    n_rows, d = dout.shape

    def body(do_ref, v_ref, g_ref, dz_ref, db_ref):
        i = pl.program_id(0)
        sg = jax.nn.sigmoid(g_ref[...])
        dval = do_ref[...] * sg
        dgate = do_ref[...] * v_ref[...] * sg * (1.0 - sg)
        dz_ref[...] = jnp.concatenate([dval, dgate], axis=1).astype(BF16)

        @pl.when(i == 0)
        def _():
            db_ref[...] = jnp.zeros_like(db_ref)

        db_ref[0:1, :] += jnp.sum(dval, axis=0, keepdims=True)
        db_ref[1:2, :] += jnp.sum(dgate, axis=0, keepdims=True)

    row = pl.BlockSpec((tm, d), lambda i: (i, 0))
    return pl.pallas_call(
        body, grid=(n_rows // tm,), in_specs=[row, row, row],
        out_specs=[pl.BlockSpec((tm, 2 * d), lambda i: (i, 0)), pl.BlockSpec((2, d), lambda i: (0, 0))],
        out_shape=[_sds((n_rows, 2 * d), BF16), _sds((2, d), F32)],
        name="glu_bwd", compiler_params=_params(("arbitrary",)))(dout, val, gate)


def _adam_update(w, g, m, v):
    nm = ADAM_B1 * m + (1.0 - ADAM_B1) * g
    nv = ADAM_B2 * v + (1.0 - ADAM_B2) * (g * g)
    m_hat = nm / (1.0 - ADAM_B1 ** ADAM_STEP)
    v_hat = nv / (1.0 - ADAM_B2 ** ADAM_STEP)
    return -ADAM_LR * (m_hat / (jnp.sqrt(v_hat) + ADAM_EPS) + ADAM_WD * w), nm, nv


def adamw(name, w, g, m, v, tm=256):
    n_rows, d = w.shape
    tm = tm if n_rows % tm == 0 else n_rows

    def body(w_ref, g_ref, m_ref, v_ref, go_ref, d_ref, nm_ref, nv_ref):
        gv = g_ref[...]
        go_ref[...] = gv
        d_ref[...], nm_ref[...], nv_ref[...] = _adam_update(w_ref[...], gv, m_ref[...], v_ref[...])

    row = pl.BlockSpec((tm, d), lambda i: (i, 0))
    return pl.pallas_call(
        body, grid=(n_rows // tm,), in_specs=[row] * 4, out_specs=[row] * 4,
        out_shape=[_sds((n_rows, d), F32)] * 4, name=name, compiler_params=_params(("parallel",)))(w, g, m, v)


def adamw_native(name, ws, gs, ms, vs):
    n = len(ws)

    def body(*refs):
        w_refs, g_refs, m_refs, v_refs = refs[:n], refs[n:2 * n], refs[2 * n:3 * n], refs[3 * n:4 * n]
        d_refs, nm_refs, nv_refs = refs[4 * n:5 * n], refs[5 * n:6 * n], refs[6 * n:7 * n]
        for k in range(n):
            dl, nm, nv = _adam_update(w_refs[k][...], g_refs[k][...], m_refs[k][...], v_refs[k][...])
            d_refs[k][...] = dl
            nm_refs[k][...] = nm
            nv_refs[k][...] = nv

    vm = pl.BlockSpec(memory_space=pltpu.VMEM)
    shapes = [_sds(w.shape, F32) for w in ws]
    out = pl.pallas_call(body, in_specs=[vm] * (4 * n), out_specs=[vm] * (3 * n), out_shape=shapes * 3, name=name,
                         compiler_params=_params())(*ws, *gs, *ms, *vs)
    return list(out[:n]), list(out[n:2 * n]), list(out[2 * n:])


VEC_ROWS = {"norm_mix": 0, "norm_mlp": 2, "norm_kv": 4, "norm_final": 5, "s5_d": 6, "b_q": 7, "b_o": 8, "s5_b_glu": 9,
            "b_kv": 11, "sinks": 12, "loss": 13}


def split_vectors(where, vecs, d_shard, glu_shard):
    kvw = 2 * N_KV * HEAD_DIM
    shapes = {"norm_mix": (2, D_MODEL), "norm_mlp": (2, D_MODEL), "norm_kv": (1, D_MODEL), "norm_final": (1, D_MODEL),
              "s5_d": (1, d_shard), "b_q": (1, D_MODEL), "b_o": (1, D_MODEL), "s5_b_glu": (1, glu_shard), "b_kv": (1, kvw),
              "sinks": (1, N_Q), "loss": (1, 128)}
    names = list(shapes)

    def body(where_ref, v_ref, *o_refs):
        chip = where_ref[1]
        for name, o_ref in zip(names, o_refs):
            r0, (r, n) = VEC_ROWS[name], shapes[name]
            if name == "s5_d":
                g = jnp.zeros((1, n), F32)
                for j in range(4):
                    g = jnp.where(chip == j, v_ref[r0:r0 + 1, j * n:(j + 1) * n], g)
            elif name == "s5_b_glu":
                g = jnp.zeros((1, n), F32)
                for j in range(4):
                    row, col = r0 + (j * n) // D_MODEL, (j * n) % D_MODEL
                    g = jnp.where(chip == j, v_ref[row:row + 1, col:col + n], g)
            else:
                g = v_ref[r0:r0 + r, 0:n]
            o_ref[...] = g

    vm = pl.BlockSpec(memory_space=pltpu.VMEM)
    out = pl.pallas_call(body, in_specs=[pl.BlockSpec(memory_space=pltpu.SMEM), vm], out_specs=[vm] * len(names),
                         out_shape=[_sds(shapes[n], F32) for n in names], name="split_vectors",
                         compiler_params=_params())(where, vecs)
    return dict(zip(names, out))


def _position():
    x, y, c = lax.axis_index("x"), lax.axis_index("y"), lax.axis_index("c")
    others = [(1 - x, y), (x, 1 - y), (1 - x, 1 - y)]
    return x, y, c, others


def _window(ref, kind, chip, half, shard_shape):
    r, n = shard_shape
    if kind == "col":
        return ref.at[pl.ds(pl.multiple_of(half * (r // 2), 16), r // 2), pl.ds(pl.multiple_of(chip * n, 128), n)]
    return ref.at[pl.ds(pl.multiple_of(chip * r, 16), r), pl.ds(pl.multiple_of(half * (n // 2), 128), n // 2)]


def _half(ref, kind, half, shape):
    r, n = shape
    if kind == "col":
        return ref.at[pl.ds(pl.multiple_of(half * (r // 2), 16), r // 2), :]
    return ref.at[:, pl.ds(pl.multiple_of(half * (n // 2), 128), n // 2)]


def swap_halves(name, grads, kinds):
    nt = len(grads)
    shapes = [tuple(g.shape) for g in grads]

    def body(*refs):
        in_refs, out_refs = refs[:nt], refs[nt:2 * nt]
        send_sems, recv_sems = refs[2 * nt:]
        x, y, c, _ = _position()
        cps = []
        for t in range(nt):
            cp = pltpu.make_async_remote_copy(
                src_ref=_half(in_refs[t], kinds[t], 1 - c, shapes[t]), dst_ref=_half(out_refs[t], kinds[t], 1 - c, shapes[t]),
                send_sem=send_sems.at[t], recv_sem=recv_sems.at[t], device_id=(x, y, 1 - c), device_id_type=MESH)
            cp.start()
            cps.append(cp)
        for t in range(nt):
            mine = _half(out_refs[t], kinds[t], c, shapes[t])
            pltpu.make_async_remote_copy(
                src_ref=mine, dst_ref=mine, send_sem=send_sems.at[t], recv_sem=recv_sems.at[t],
                device_id=(x, y, 1 - c), device_id_type=MESH).wait_recv()
        for cp in cps:
            cp.wait_send()

    hbm = pl.BlockSpec(memory_space=pl.ANY)
    return pl.pallas_call(
        body, in_specs=[hbm] * nt, out_specs=[hbm] * nt, out_shape=[_sds(s, BF16) for s in shapes],
        scratch_shapes=[pltpu.SemaphoreType.DMA((nt,)), pltpu.SemaphoreType.DMA((nt,))],
        name=name, compiler_params=_params())(*grads)


def _half_spec(kind, shape, tiles):
    r, n = shape
    if kind == "col":
        tn = n // tiles
        return pl.BlockSpec((r // 2, tn), lambda i, s: (s[0], i))
    tm = r // tiles
    return pl.BlockSpec((tm, n // 2), lambda i, s: (i, s[0]))


def add_halves(name, mine, landed, kinds, where, tiles=4):
    nt = len(mine)
    shapes = [tuple(a.shape) for a in mine]

    def compact(t):
        r, n = shapes[t]
        if kinds[t] == "col":
            return (r // 2, n), pl.BlockSpec((r // 2, n // tiles), lambda i, s: (0, i))
        return (r, n // 2), pl.BlockSpec((r // tiles, n // 2), lambda i, s: (i, 0))

    def body(s_ref, *refs):
        for a_ref, b_ref, o_ref in zip(refs[:nt], refs[nt:2 * nt], refs[2 * nt:]):
            o_ref[...] = (a_ref[...].astype(F32) + b_ref[...].astype(F32)).astype(BF16)

    specs = [_half_spec(kinds[t], shapes[t], tiles) for t in range(nt)]
    return pl.pallas_call(
        body, grid_spec=pltpu.PrefetchScalarGridSpec(num_scalar_prefetch=1, grid=(tiles,), in_specs=specs + specs,
                                                     out_specs=[compact(t)[1] for t in range(nt)]),
        out_shape=[_sds(compact(t)[0], BF16) for t in range(nt)], name=name,
        compiler_params=_params(("parallel",)))(where, *mine, *landed)


def sum_shards(name, parts, landed, kinds, shard_shapes, where, layers, n_layers, intos, tiles=2):
    nt = len(parts)
    in_specs, out_specs = [], []
    for t in range(nt):
        (r, n), layer = shard_shapes[t], layers[t]
        if kinds[t] == "col":
            tm, width = r // 2 // tiles, n
            own = pl.BlockSpec((tm, n), lambda i, s: (i, s[1]))
            out = pl.BlockSpec((None, tm, n), lambda i, s, layer=layer: (layer, s[0] * tiles + i, 0))
        else:
            tm, width = r // tiles, n // 2
            own = pl.BlockSpec((tm, n // 2), lambda i, s: (s[1] * tiles + i, 0))
            out = pl.BlockSpec((None, tm, n // 2), lambda i, s, layer=layer: (layer, i, s[0]))
        in_specs += [own, pl.BlockSpec((3, tm, width), lambda i, s: (0, i, 0))]
        out_specs.append(out)
    args, aliases = [where] + [a for pair in zip(parts, landed) for a in pair], {}
    for t in range(nt):
        if intos[t] is not None:
            aliases[len(args)] = t
            in_specs.append(pl.BlockSpec(memory_space=pl.ANY))
            args.append(intos[t])

    def body(s_ref, *refs):
        for t in range(nt):
            a_ref, l_ref, o_ref = refs[2 * t], refs[2 * t + 1], refs[len(in_specs) + t]
            o_ref[...] = ((a_ref[...].astype(F32) + l_ref[0].astype(F32)) + l_ref[1].astype(F32)) + l_ref[2].astype(F32)

    return pl.pallas_call(
        body, grid_spec=pltpu.PrefetchScalarGridSpec(num_scalar_prefetch=1, grid=(tiles,), in_specs=in_specs,
                                                     out_specs=out_specs),
        out_shape=[_sds((n_layers[t],) + tuple(shard_shapes[t]), F32) for t in range(nt)], input_output_aliases=aliases,
        name=name, compiler_params=_params(("parallel",)))(*args)


def share_halves(arrays, entries):
    na, nt = len(arrays), len(entries)

    def body(*refs):
        out_refs = refs[na:2 * na]
        send_sems, recv_sems = refs[2 * na:]
        x, y, c, _ = _position()
        cps = []
        for t, (a, layer, kind) in enumerate(entries):
            shape = tuple(arrays[a].shape[1:])
            mine = _half(out_refs[a].at[layer], kind, c, shape)
            cp = pltpu.make_async_remote_copy(
                src_ref=mine, dst_ref=mine, send_sem=send_sems.at[t], recv_sem=recv_sems.at[t],
                device_id=(x, y, 1 - c), device_id_type=MESH)
            cp.start()
            cps.append(cp)
        for t, (a, layer, kind) in enumerate(entries):
            shape = tuple(arrays[a].shape[1:])
            other = _half(out_refs[a].at[layer], kind, 1 - c, shape)
            pltpu.make_async_remote_copy(
                src_ref=other, dst_ref=other, send_sem=send_sems.at[t], recv_sem=recv_sems.at[t],
                device_id=(x, y, 1 - c), device_id_type=MESH).wait_recv()
        for cp in cps:
            cp.wait_send()

    hbm = pl.BlockSpec(memory_space=pl.ANY)
    return pl.pallas_call(
        body, in_specs=[hbm] * na, out_specs=[hbm] * na, out_shape=[_sds(a.shape, F32) for a in arrays],
        input_output_aliases={i: i for i in range(na)},
        scratch_shapes=[pltpu.SemaphoreType.DMA((nt,)), pltpu.SemaphoreType.DMA((nt,))],
        name="share_halves", compiler_params=_params())(*arrays)


HBM_SPEC = pl.BlockSpec(memory_space=pltpu.HBM)
SEM_SPEC = pl.BlockSpec(memory_space=pltpu.SEMAPHORE)
ANY_SPEC = pl.BlockSpec(memory_space=pl.ANY)


def _split_params():
    return pltpu.CompilerParams(has_side_effects=pltpu.SideEffectType.DATAFLOW_SIDE_EFFECTING,
                                vmem_limit_bytes=VMEM_LIMIT_BYTES)


def _in_hbm(a):
    return pltpu.with_memory_space_constraint(a, pltpu.HBM)


def cast_place(arrays, entries, where, tiles=2):
    in_specs, out_specs, fulls = [], [], []
    for a, layer, kind in entries:
        _, r, n = arrays[a].shape
        tm = r // tiles
        in_specs.append(pl.BlockSpec((None, tm, n), lambda i, s, layer=layer: (layer, i, 0)))
        if kind == "col":
            fulls.append((r, 4 * n))
            out_specs.append(pl.BlockSpec((tm, n), lambda i, s: (i, s[1])))
        else:
            fulls.append((4 * r, n))
            out_specs.append(pl.BlockSpec((tm, n), lambda i, s: (s[1] * tiles + i, 0)))
    nt = len(entries)

    def body(s_ref, *refs):
        for w_ref, o_ref in zip(refs[:nt], refs[nt:]):
            o_ref[...] = w_ref[...].astype(BF16)

    return pl.pallas_call(
        body, grid_spec=pltpu.PrefetchScalarGridSpec(num_scalar_prefetch=1, grid=(tiles,), in_specs=in_specs,
                                                     out_specs=out_specs),
        out_shape=[_sds(f, BF16) for f in fulls], name="cast_place",
        compiler_params=_params(("parallel",)))(where, *[arrays[a] for a, _, _ in entries])


def gather_start(fulls, kinds, shard_shapes, after):
    nt = len(fulls)
    na = 0 if after is None else 1

    def body(*refs):
        full_refs = refs[:nt]
        send_sems, recv_sems, token = refs[nt + na], refs[nt + na + 1], refs[-1]
        x, y, c, others = _position()
        for t in range(nt):
            mine = _window(full_refs[t], kinds[t], 2 * x + y, c, shard_shapes[t])
            for j, (ox, oy) in enumerate(others):
                pltpu.make_async_remote_copy(
                    src_ref=mine, dst_ref=mine, send_sem=send_sems.at[3 * t + j], recv_sem=recv_sems.at[3 * t + j],
                    device_id=(ox, oy, c), device_id_type=MESH).start()
        token[...] = jnp.zeros_like(token)

    sems = pltpu.SemaphoreType.DMA((3 * nt,))
    out = pl.pallas_call(
        body, name="gather_start", in_specs=[HBM_SPEC] * nt + [ANY_SPEC] * na,
        out_specs=(SEM_SPEC, SEM_SPEC, *[HBM_SPEC] * nt, pl.BlockSpec(memory_space=pltpu.VMEM)),
        out_shape=(sems, sems, *[pltpu.HBM(f.shape, f.dtype) for f in fulls], _sds((8, 128), F32)),
        input_output_aliases={t: 2 + t for t in range(nt)}, compiler_params=_split_params(),
    )(*[_in_hbm(f) for f in fulls], *([] if after is None else [after]))
    return out[0], out[1], list(out[2:2 + nt]), out[-1]


def gather_wait(name, send_sems, recv_sems, fulls, kinds, shard_shapes, after, first):
    nt = len(fulls)

    def body(*refs):
        full_refs, send_ref, recv_ref = refs[:nt], refs[nt], refs[nt + 1]
        x, y, c, others = _position()
        for t in range(nt):
            mine = _window(full_refs[t], kinds[t], 2 * x + y, c, shard_shapes[t])
            for j, (ox, oy) in enumerate(others):
                cp = pltpu.make_async_remote_copy(
                    src_ref=mine, dst_ref=_window(full_refs[t], kinds[t], 2 * ox + oy, c, shard_shapes[t]),
                    send_sem=send_ref.at[3 * (first + t) + j], recv_sem=recv_ref.at[3 * (first + t) + j],
                    device_id=(ox, oy, c), device_id_type=MESH)
                cp.wait_send()
                cp.wait_recv()

    out = pl.pallas_call(
        body, name=name, in_specs=[HBM_SPEC] * nt + [SEM_SPEC, SEM_SPEC, HBM_SPEC], out_specs=[HBM_SPEC] * nt,
        out_shape=[pltpu.HBM(f.shape, f.dtype) for f in fulls], input_output_aliases={t: t for t in range(nt)},
        compiler_params=_split_params())(*fulls, send_sems, recv_sems, _in_hbm(after))
    return list(out)


def forward_halves(name, fulls, kinds, shard_shapes):
    nt = len(fulls)

    def body(*refs):
        out_refs = refs[nt:2 * nt]
        send_sems, recv_sems = refs[2 * nt:]
        x, y, c, others = _position()
        cps = []
        for t in range(nt):
            for j, (ox, oy) in enumerate(others):
                landed = _window(out_refs[t], kinds[t], 2 * ox + oy, c, shard_shapes[t])
                cp = pltpu.make_async_remote_copy(
                    src_ref=landed, dst_ref=landed, send_sem=send_sems.at[3 * t + j], recv_sem=recv_sems.at[3 * t + j],
                    device_id=(x, y, 1 - c), device_id_type=MESH)
                cp.start()
                cps.append(cp)
        for t in range(nt):
            for j, (ox, oy) in enumerate(others):
                got = _window(out_refs[t], kinds[t], 2 * ox + oy, 1 - c, shard_shapes[t])
                pltpu.make_async_remote_copy(
                    src_ref=got, dst_ref=got, send_sem=send_sems.at[3 * t + j], recv_sem=recv_sems.at[3 * t + j],
                    device_id=(x, y, 1 - c), device_id_type=MESH).wait_recv()
        for cp in cps:
            cp.wait_send()

    out = pl.pallas_call(
        body, in_specs=[ANY_SPEC] * nt, out_specs=[ANY_SPEC] * nt, out_shape=[_sds(f.shape, f.dtype) for f in fulls],
        input_output_aliases={t: t for t in range(nt)},
        scratch_shapes=[pltpu.SemaphoreType.DMA((3 * nt,)), pltpu.SemaphoreType.DMA((3 * nt,))],
        name=name, compiler_params=_params())(*fulls)
    return list(out)


def _piece(ref, kind, chip, shard_shape):
    r, n = shard_shape
    if kind == "col":
        return ref.at[:, pl.ds(pl.multiple_of(chip * n, 128), n)]
    return ref.at[pl.ds(pl.multiple_of(chip * r, 16), r), :]


def _piece_shape(kind, shard_shape):
    r, n = shard_shape
    return (r // 2, n) if kind == "col" else (r, n // 2)


def exchange_start(name, parts, kinds, shard_shapes):
    nt = len(parts)
    lands = [lax.empty((3,) + _piece_shape(kinds[t], shard_shapes[t]), BF16) for t in range(nt)]

    def body(*refs):
        part_refs, land_refs = refs[:nt], refs[nt:2 * nt]
        send_sems, recv_sems, token = refs[2 * nt], refs[2 * nt + 1], refs[-1]
        x, y, c, others = _position()
        for t in range(nt):
            for j, (ox, oy) in enumerate(others):
                pltpu.make_async_remote_copy(
                    src_ref=_piece(part_refs[t], kinds[t], 2 * ox + oy, shard_shapes[t]), dst_ref=land_refs[t].at[j],
                    send_sem=send_sems.at[3 * t + j], recv_sem=recv_sems.at[3 * t + j],
                    device_id=(ox, oy, c), device_id_type=MESH).start()
        token[...] = jnp.zeros_like(token)

    sems = pltpu.SemaphoreType.DMA((3 * nt,))
    both = list(parts) + lands
    out = pl.pallas_call(
        body, name=name, in_specs=[HBM_SPEC] * (2 * nt),
        out_specs=(SEM_SPEC, SEM_SPEC, *[HBM_SPEC] * (2 * nt), pl.BlockSpec(memory_space=pltpu.VMEM)),
        out_shape=(sems, sems, *[pltpu.HBM(a.shape, a.dtype) for a in both], _sds((8, 128), F32)),
        input_output_aliases={t: 2 + t for t in range(2 * nt)}, compiler_params=_split_params(),
    )(*[_in_hbm(a) for a in both])
    return out[0], out[1], list(out[2:2 + nt]), list(out[2 + nt:2 + 2 * nt]), out[-1]


def exchange_wait(name, send_sems, recv_sems, parts, lands, kinds, shard_shapes, after):
    nt = len(parts)

    def body(*refs):
        part_refs, land_refs = refs[:nt], refs[nt:2 * nt]
        send_ref, recv_ref = refs[2 * nt], refs[2 * nt + 1]
        x, y, c, others = _position()
        for t in range(nt):
            for j, (ox, oy) in enumerate(others):
                cp = pltpu.make_async_remote_copy(
                    src_ref=_piece(part_refs[t], kinds[t], 2 * ox + oy, shard_shapes[t]), dst_ref=land_refs[t].at[j],
                    send_sem=send_ref.at[3 * t + j], recv_sem=recv_ref.at[3 * t + j],
                    device_id=(ox, oy, c), device_id_type=MESH)
                cp.wait_send()
                cp.wait_recv()

    both = list(parts) + list(lands)
    out = pl.pallas_call(
        body, name=name, in_specs=[HBM_SPEC] * (2 * nt) + [SEM_SPEC, SEM_SPEC, HBM_SPEC], out_specs=[HBM_SPEC] * (2 * nt),
        out_shape=[pltpu.HBM(a.shape, a.dtype) for a in both], input_output_aliases={t: t for t in range(2 * nt)},
        compiler_params=_split_params())(*both, send_sems, recv_sems, _in_hbm(after))
    return list(out[:nt]), list(out[nt:])


def all_reduce_small(name, bufs):
    n = len(bufs)
    halves = [b.shape[0] // 2 for b in bufs]

    def body(*refs):
        in_refs, out_refs, lands = refs[:n], refs[n:2 * n], refs[2 * n:3 * n]
        send_sems, recv_sems = refs[3 * n:]
        x, y, c, _ = _position()
        mine = [pl.ds(pl.multiple_of(c * h, 8), h) for h in halves]
        other = [pl.ds(pl.multiple_of((1 - c) * h, 8), h) for h in halves]
        for k in range(n):
            out_refs[k][mine[k], :] = in_refs[k][mine[k], :]
        for s, peer in enumerate([(x, y, 1 - c), (1 - x, y, c), (x, 1 - y, c)]):
            cps = []
            for k in range(n):
                src = in_refs[k].at[other[k]] if s == 0 else out_refs[k].at[mine[k]]
                cp = pltpu.make_async_remote_copy(
                    src_ref=src, dst_ref=lands[k].at[s], send_sem=send_sems.at[4 * k + s], recv_sem=recv_sems.at[4 * k + s],
                    device_id=peer, device_id_type=MESH)
                cp.start()
                cps.append(cp)
            for k, cp in enumerate(cps):
                cp.wait()
                out_refs[k][mine[k], :] = out_refs[k][mine[k], :] + lands[k][s]
        cps = []
        for k in range(n):
            cp = pltpu.make_async_remote_copy(
                src_ref=out_refs[k].at[mine[k]], dst_ref=out_refs[k].at[mine[k]], send_sem=send_sems.at[4 * k + 3],
                recv_sem=recv_sems.at[4 * k + 3], device_id=(x, y, 1 - c), device_id_type=MESH)
            cp.start()
            cps.append(cp)
        for cp in cps:
            cp.wait()

    vm = pl.BlockSpec(memory_space=pltpu.VMEM)
    out = pl.pallas_call(
        body, in_specs=[vm] * n, out_specs=[vm] * n, out_shape=[_sds(b.shape, F32) for b in bufs],
        scratch_shapes=[pltpu.VMEM((3, h, b.shape[1]), F32) for h, b in zip(halves, bufs)]
        + [pltpu.SemaphoreType.DMA((4 * n,)), pltpu.SemaphoreType.DMA((4 * n,))],
        name=name, compiler_params=_params())(*bufs)
    return list(out)


def _local_step(x, target, small, need, emit):
    d = D_MODEL
    full = {}

    def after_token(vec, token):
        return vec if token is None else vec + token[0:1, 0:1]

    rb16, rbt16, rc16, rct16, lr_t, li_t = small["s5_operands"]
    ge, y2, cs = s5_fwd(x, small["norm_mix0"], small["s5_d"], rb16, rc16, lr_t, li_t)
    full.update(need("glu", ge))

    def norm_rows(h, gains):
        xh, _ = _rms_hat(h)
        return [xh * g for g in gains]

    def glu_epilogue(accs, e, r):
        v, gt = accs[0] + r[0], accs[1] + r[1]
        h = e[0] + v * jax.nn.sigmoid(gt)
        return [h, v, gt] + norm_rows(h, r[2:])

    h1, val, gate, n1 = mm_nn(
        "glu", ge, full["w_glu"], [0, d], d, glu_epilogue, [F32, F32, F32, BF16], extras=[x],
        rowvecs=[(small["s5_b_glu"], 0), (small["s5_b_glu"], d), (small["norm_mlp0"], 0)], tm=512, tn=d)

    def mlp_fwd(tag, h, n, w_in, get_w_out, next_gains, head=None):
        def in_epilogue(accs, e, rv):
            pos = jnp.maximum(accs[0], 0.0)
            return [pos * pos, 2.0 * pos]

        r, slope = mm_nn("mlp_in" + tag, n, w_in, [0], w_in.shape[1], in_epilogue, [BF16, BF16], tm=2048)
        w_out = get_w_out(r)

        def epilogue(accs, e, rv):
            h_out = e[0] + accs[0]
            return [h_out] + norm_rows(h_out, rv)

        if head is not None:
            return head(r, w_out, h), (n, r, slope)
        outs = mm_nn("mlp_out" + tag, r, w_out, [0], d, epilogue, [F32] + [BF16] * len(next_gains), extras=[h],
                     rowvecs=[(g, 0) for g in next_gains], tm=512, tn=d)
        return outs[0], outs[1:], (n, r, slope)

    full.update(need("mlp_in0", h1))

    def w_out0(after):
        full.update(need("mlp_out0", after))
        return full["w_out0"]

    h2, (nkv, n2), mlp0 = mlp_fwd("0", h1, n1, full["w_in0"], w_out0, [small["norm_kv"], small["norm_mix1"]])

    full.update(need("attn", h2))
    kvw = 2 * N_KV * HEAD_DIM
    (kv,) = mm_nn("kv_proj", nkv, full["w_kv"], [0], kvw, lambda accs, e, r: [accs[0] + r[0]], [BF16],
                  rowvecs=[(small["b_kv"], 0)], tm=2048)
    (q,) = mm_nn("q_proj", n2, full["w_q"], [0], d, lambda accs, e, r: [accs[0] + r[0]], [BF16],
                 rowvecs=[(small["b_q"], 0)], tm=2048)
    sinks = small["sinks"].reshape(N_Q)
    o = attn_fwd(q, kv, sinks)
    def o_epilogue(accs, e, r):
        h_out = e[0] + accs[0] + r[0]
        return [h_out] + norm_rows(h_out, r[1:])

    h3, n3 = mm_nn("o_proj", o, full["w_o"], [0], d, o_epilogue, [F32, BF16], extras=[h2],
                   rowvecs=[(small["b_o"], 0), (small["norm_mlp1"], 0)], tm=512, tn=d)
    full.update(need("mlp1", h3))

    def loss_head(r, w_out, h):
        def epilogue(accs, e, rv):
            xh, rr = _rms_hat(e[0] + accs[0])
            err = xh * rv[0] - e[1]
            dy = err * (1.0 / d)
            dxh = dy * rv[0]
            dx = rr * (dxh - xh * jnp.mean(dxh * xh, axis=-1, keepdims=True))
            loss = jnp.full((1, d), 0.5 * jnp.sum(jnp.mean(err * err, axis=-1, keepdims=True)), F32)
            return [dx, dx, loss, jnp.sum(dy * xh, axis=0, keepdims=True)]

        return mm_nn("mlp_out1", r, w_out, [0], d, epilogue, [F32, BF16], extras=[h, target],
                     rowvecs=[(small["norm_final"], 0)], n_sums=2, tm=512, tn=d)

    (dh, dhb, loss_tile, dg_final), mlp1 = mlp_fwd("1", h3, n3, full["w_in1"], lambda after: full["w_out1"], [], head=loss_head)

    grads_small, grads_full = {"norm_final": dg_final}, {}
    ident = lambda acc, e, r: [acc]
    layer1 = ["w_out1", "w_in1", "w_o", "w_q", "w_kv"]
    layer0 = ["w_out0", "w_in0", "w_glu"]

    def norm_bwd_rows(x_rows, res, dys, gains):
        xh, r = _rms_hat(x_rows)
        dxh = sum(dy * g for dy, g in zip(dys, gains))
        dx = r * (dxh - xh * jnp.mean(dxh * xh, axis=-1, keepdims=True)) + res
        return dx, [jnp.sum(dy * xh, axis=0, keepdims=True) for dy in dys]

    def mlp_bwd(tag, dh, dhb, h_in, gain, w_in, w_out, saved):
        n, r, slope = saved
        grads_full["w_out" + tag] = mm_tn("dw_out" + tag, r, dhb, tn=1024)
        (da,) = mm_nt("mlp_da" + tag, dhb, w_out, lambda acc, e, rv: [acc * e[0].astype(F32)], [BF16], extras=[slope],
                      tm=2048)
        grads_full["w_in" + tag] = mm_tn("dw_in" + tag, n, da, tn=1024)

        def epilogue(acc, e, rv):
            dx, dgs = norm_bwd_rows(e[0], e[1], [acc], rv)
            return [dx, dx, jnp.sum(dx, axis=0, keepdims=True)] + dgs

        dx, dxb, colsum, dg = mm_nt("mlp_dn" + tag, da, w_in, epilogue, [F32, BF16], extras=[h_in, dh], rowvecs=[gain],
                                    n_sums=2, tm=512, tk=d)
        grads_small["norm_mlp" + tag] = dg
        return dx, dxb, colsum

    dh3, dh3b, colsum3 = mlp_bwd("1", dh, dhb, h3, small["norm_mlp1"], full["w_in1"], full["w_out1"], mlp1)
    grads_small["b_o"] = colsum3
    grads_full["w_o"] = mm_tn("dw_o", o, dh3b, tn=1024)
    (do,) = mm_nt("attn_do", dh3b, full["w_o"], ident, [BF16], tm=2048)
    dq, dbq, dprev, dcur, dsink = attn_bwd(q, kv, do, sinks)
    dkv, dbkv = kv_combine(dprev, dcur)
    grads_small["b_q"], grads_small["b_kv"], grads_small["sinks"] = dbq, dbkv, dsink
    grads_full["w_q"] = mm_tn("dw_q", n2, dq, tn=1024)
    grads_full["w_kv"] = mm_tn("dw_kv", nkv, dkv, tk=1024)
    (dnkv,) = mm_nt("kv_dn", dkv, full["w_kv"], ident, [F32], tm=2048, tk=1024)
    token = emit("layer1", {n: grads_full[n] for n in layer1})

    def attn_dn_epilogue(acc, e, rv):
        dx, dgs = norm_bwd_rows(e[0], e[1], [acc, e[2]], rv)
        return [dx, dx] + dgs

    dh2, dh2b, dg_mix1, dg_kv = mm_nt("attn_dn", dq, full["w_q"], attn_dn_epilogue, [F32, BF16], extras=[h2, dh3, dnkv],
                                      rowvecs=[after_token(small["norm_mix1"], token), small["norm_kv"]], n_sums=2,
                                      tm=512, tk=d)
    grads_small["norm_mix1"], grads_small["norm_kv"] = dg_mix1, dg_kv
    dh1, _, _ = mlp_bwd("0", dh2, dh2b, h1, small["norm_mlp0"], full["w_in0"], full["w_out0"], mlp0)

    dz, db_glu = glu_bwd(dh1, val, gate)
    grads_small["s5_b_glu"] = db_glu
    grads_full["w_glu"] = mm_tn("dw_glu", ge, dz, tn=1024)
    token = emit("layer0", {n: grads_full[n] for n in layer0})
    (dy2,) = mm_nt("glu_dy", dz, full["w_glu"], lambda acc, e, rv: [acc * _gelu_grad(e[0])], [F32], extras=[y2],
                   tm=1024, tk=1024)
    grad_x, dd, drb, drc, dlr, dli, dg_mix0 = s5_bwd(x, small["norm_mix0"], dy2, dh1, after_token(small["s5_d"], token), cs,
                                                     rb16, rbt16, rct16, lr_t, li_t)
    grads_small["s5_d"] = dd
    grads_small["s5_mats"] = (drb, drc, dlr, dli)
    grads_small["norm_mix0"] = dg_mix0
    return loss_tile, grad_x, grads_small


SMALL_NAMES = ["norm_mix", "norm_mlp", "norm_kv", "norm_final", "s5_a_re", "s5_a_im", "s5_log_dt", "s5_b_re", "s5_b_im",
               "s5_c_re", "s5_c_im", "s5_d", "s5_b_glu", "b_kv", "b_q", "sinks", "b_o"]
BIG_NAMES = ["s5_w_glu", "w_kv", "w_q", "w_o", "w_mlp_in", "w_mlp_out"]
WEIGHT_ORDER = ["norm_mix", "norm_mlp", "norm_kv", "norm_final", "s5_a_re", "s5_a_im", "s5_log_dt", "s5_b_re", "s5_b_im",
                "s5_c_re", "s5_c_im", "s5_d", "s5_w_glu", "s5_b_glu", "w_kv", "b_kv", "w_q", "b_q", "sinks", "w_o", "b_o",
                "w_mlp_in", "w_mlp_out"]


def kernel(x, norm_mix, norm_mlp, norm_kv, norm_final, s5_a_re, s5_a_im, s5_log_dt, s5_b_re, s5_b_im, s5_c_re, s5_c_im, s5_d, s5_w_glu, s5_b_glu, w_kv, b_kv, w_q, b_q, sinks, w_o, b_o, w_mlp_in, w_mlp_out, loss_target, m_norm_mix, m_norm_mlp, m_norm_kv, m_norm_final, m_s5_a_re, m_s5_a_im, m_s5_log_dt, m_s5_b_re, m_s5_b_im, m_s5_c_re, m_s5_c_im, m_s5_d, m_s5_w_glu, m_s5_b_glu, m_w_kv, m_b_kv, m_w_q, m_b_q, m_sinks, m_w_o, m_b_o, m_w_mlp_in, m_w_mlp_out, v_norm_mix, v_norm_mlp, v_norm_kv, v_norm_final, v_s5_a_re, v_s5_a_im, v_s5_log_dt, v_s5_b_re, v_s5_b_im, v_s5_c_re, v_s5_c_im, v_s5_d, v_s5_w_glu, v_s5_b_glu, v_w_kv, v_b_kv, v_w_q, v_b_q, v_sinks, v_w_o, v_b_o, v_w_mlp_in, v_w_mlp_out):
    env = dict(locals())
    w = {n: env[n] for n in WEIGHT_ORDER}
    mom = {n: env["m_" + n] for n in WEIGHT_ORDER}
    var = {n: env["v_" + n] for n in WEIGHT_ORDER}
    d = D_MODEL
    xi, yi, ci = lax.axis_index("x"), lax.axis_index("y"), lax.axis_index("c")
    chip = 2 * xi + yi
    where = jnp.stack([ci, chip]).astype(jnp.int32)

    dsh, bsh = s5_d.shape[1], s5_b_glu.shape[1]
    placed = jnp.concatenate([
        lax.dynamic_update_slice(jnp.zeros((4 * dsh,), F32), s5_d[0], (chip * dsh,)),
        lax.dynamic_update_slice(jnp.zeros((4 * bsh,), F32), s5_b_glu[0], (chip * bsh,))])
    placed = jnp.pad(placed, (0, (-placed.shape[0]) % 2048))
    placed = jnp.where(ci == 0, placed, 0.0).reshape(-1, 128)
    (gathered_rows,) = all_reduce_small("gather_vectors", [placed])
    gathered = gathered_rows.reshape(-1)
    d_full, bglu_full = gathered[:4 * dsh].reshape(1, -1), gathered[4 * dsh:].reshape(1, -1)

    big = [s5_w_glu, w_kv[None], w_q, w_o, w_mlp_in, w_mlp_out]
    entries = [(0, 0, "col"), (1, 0, "row"), (2, 0, "row"), (3, 0, "row"), (4, 0, "col"), (4, 1, "col"),
               (5, 0, "row"), (5, 1, "row")]
    names = ["w_glu", "w_kv", "w_q", "w_o", "w_in0", "w_in1", "w_out0", "w_out1"]
    kinds = dict(zip(names, [k for _, _, k in entries]))
    shard_shapes = dict(zip(names, [tuple(big[a].shape[1:]) for a, _, _ in entries]))

    placed_w = dict(zip(names, cast_place(big, entries, where)))
    gather_groups = {"glu": ["w_glu"], "mlp_in0": ["w_in0"], "mlp_out0": ["w_out0"], "attn": ["w_kv", "w_q", "w_o"],
                     "mlp1": ["w_in1", "w_out1"]}
    order = [n for members in gather_groups.values() for n in members]
    send, recv, thru, token = gather_start([placed_w[n] for n in order], [kinds[n] for n in order],
                                           [shard_shapes[n] for n in order], gathered_rows)
    started = dict(zip(order, thru))

    def need(group, after):
        members = gather_groups[group]
        ks, shapes = [kinds[n] for n in members], [shard_shapes[n] for n in members]
        landed = gather_wait("gather_wait_" + group, send, recv, [started[n] for n in members], ks, shapes, after,
                             order.index(members[0]))
        return dict(zip(members, forward_halves("forward_halves_" + group, landed, ks, shapes)))

    exchanging = {}

    def emit(group, partial):
        members = list(partial)
        ks, shapes = [kinds[n] for n in members], [shard_shapes[n] for n in members]
        landed = swap_halves("swap_halves_" + group, [partial[n] for n in members], ks)
        sums = add_halves("add_halves_" + group, [partial[n] for n in members], landed, ks, where)
        send, recv, parts, lands, tok = exchange_start("exchange_start_" + group, sums, ks, shapes)
        exchanging[group] = (members, send, recv, parts, lands)
        return tok

    s5_args = (s5_a_re[0], s5_a_im[0], s5_log_dt[0], s5_b_re[0], s5_b_im[0])
    small = {
        "norm_mix0": norm_mix[0:1] + token[0:1, 0:1], "norm_mix1": norm_mix[1:2], "norm_mlp0": norm_mlp[0:1], "norm_mlp1": norm_mlp[1:2],
        "norm_kv": norm_kv.reshape(1, d), "norm_final": norm_final.reshape(1, d), "s5_operands": s5_prep(*s5_args, s5_c_re[0], s5_c_im[0]),
        "s5_d": d_full, "s5_b_glu": bglu_full,
        "b_kv": b_kv.reshape(1, -1), "b_q": b_q, "sinks": sinks, "b_o": b_o,
    }
    loss_row, grad_x, gs = _local_step(x[0], loss_target[0], small, need, emit)

    mats = s5_compact(*gs["s5_mats"])
    rows = [gs["norm_mix0"], gs["norm_mix1"], gs["norm_mlp0"], gs["norm_mlp1"], gs["norm_kv"], gs["norm_final"], gs["s5_d"],
            gs["b_q"], gs["b_o"], gs["s5_b_glu"], gs["b_kv"], gs["sinks"], loss_row, jnp.zeros((2, d), F32)]
    vecs, mats = all_reduce_small("reduce_small", [jnp.concatenate(rows, axis=0), mats])
    grads = split_vectors(where, vecs, dsh, bsh)
    loss = grads.pop("loss")[0, 0]
    g_are, g_aim, g_dt, g_bre, g_bim, dc_re, dc_im = s5_param_bwd(mats, *s5_args)
    grads.update({"s5_a_re": g_are[None], "s5_a_im": g_aim[None], "s5_log_dt": g_dt[None], "s5_b_re": g_bre[None],
                  "s5_b_im": g_bim[None], "s5_c_re": dc_re[None], "s5_c_im": dc_im[None]})

    reduced = [None] * len(big)
    where_of = dict(zip(names, entries))
    for group, after in (("layer1", grad_x), ("layer0", mats)):
        members, send, recv, parts, lands = exchanging[group]
        ks, shapes = [kinds[n] for n in members], [shard_shapes[n] for n in members]
        parts, lands = exchange_wait("exchange_wait_" + group, send, recv, parts, lands, ks, shapes, after)
        targets = [where_of[n][0] for n in members]
        sums = sum_shards("sum_shards_" + group, parts, lands, ks, shapes, where, [where_of[n][1] for n in members],
                          [big[a].shape[0] for a in targets], [reduced[a] for a in targets])
        for a, arr in zip(targets, sums):
            reduced[a] = arr
    reduced = share_halves(reduced, entries)
    for n, g in zip(BIG_NAMES, reduced):
        grads[n] = g.reshape(w[n].shape)

    delta, new_m, new_v = {}, {}, {}
    for n in BIG_NAMES:
        flat = lambda a: a.reshape(-1, a.shape[-1])
        go, dl, nm, nv = adamw("adamw_" + n, flat(w[n]), flat(grads[n]), flat(mom[n]), flat(var[n]))
        grads[n], delta[n], new_m[n], new_v[n] = (t.reshape(w[n].shape) for t in (go, dl, nm, nv))

    def view(n, a):
        return a.reshape(1, -1) if a.ndim == 1 else jnp.swapaxes(a, -1, -2) if n in ("s5_b_re", "s5_b_im") else a

    sw, sg, sm, sv = ([view(n, t[n]) for n in SMALL_NAMES] for t in (w, grads, mom, var))
    for n, a, b, c_ in zip(SMALL_NAMES, *adamw_native("adamw_small", sw, sg, sm, sv)):
        delta[n], new_m[n], new_v[n] = (view(n, t) if t.ndim == 4 else t for t in (a, b, c_))

    out = [loss.reshape(()), grad_x[None]]
    for table in (grads, delta, new_m, new_v):
        out += [table[n].reshape(w[n].shape) for n in WEIGHT_ORDER]
    return tuple(out)
```

```python
import math

import jax
import jax.numpy as jnp
from jax import lax
from jax.experimental import pallas as pl
from jax.experimental.pallas import tpu as pltpu

F32 = jnp.float32
BF16 = jnp.bfloat16

D_MODEL = 1024
S5_GROUPS = 64
S5_GROUP = 16
S5_STATE = 64
N_KV = 4
N_Q = 16
HEAD_DIM = 64
BLOCK = 128
NORM_EPS = 1e-5
LAMBDA_RE_MAX = -1e-4
ADAM_LR, ADAM_B1, ADAM_B2, ADAM_EPS, ADAM_WD, ADAM_STEP = 0.001, 0.9, 0.999, 1e-08, 0.01, 10

VMEM_LIMIT_BYTES = 56 * 1024 * 1024
S5_CHUNK = 256
S5_BLOCKS = 4
MESH = pl.DeviceIdType.MESH


def _params(sem=None):
    return pltpu.CompilerParams(dimension_semantics=sem, vmem_limit_bytes=VMEM_LIMIT_BYTES)


def _sds(shape, dtype):
    return jax.ShapeDtypeStruct(shape, dtype)


def _rms_hat(xv):
    r = lax.rsqrt(jnp.mean(xv * xv, axis=-1, keepdims=True) + NORM_EPS)
    return xv * r, r


def mm_nn(name, a, w, col_offsets, n_out, epilogue, out_dtypes, extras=(), rowvecs=(), n_sums=0, tm=1024, tn=512):
    m, k = a.shape
    tm, tn = min(tm, m), min(tn, n_out)
    nw, ne, nr, no = len(col_offsets), len(extras), len(rowvecs), len(out_dtypes)

    def body(a_ref, *refs):
        w_refs, e_refs, r_refs = refs[:nw], refs[nw:nw + ne], refs[nw + ne:nw + ne + nr]
        o_refs, s_refs = refs[nw + ne + nr:nw + ne + nr + no], refs[nw + ne + nr + no:]
        av = a_ref[...]
        accs = [jnp.dot(av, w_ref[...], preferred_element_type=F32) for w_ref in w_refs]
        outs = epilogue(accs, [e[...] for e in e_refs], [r[...] for r in r_refs])
        for o_ref, o in zip(o_refs, outs[:no]):
            o_ref[...] = o.astype(o_ref.dtype)
        if n_sums:
            @pl.when(pl.program_id(1) == 0)
            def _():
                for s_ref in s_refs:
                    s_ref[...] = jnp.zeros_like(s_ref)

            for s_ref, val in zip(s_refs, outs[no:]):
                s_ref[...] += val

    def wspec(off):
        return pl.BlockSpec((k, tn), lambda j, i, off=off: (0, off // tn + j))

    def rspec(off):
        return pl.BlockSpec((1, tn), lambda j, i, off=off: (0, off // tn + j))

    tile = pl.BlockSpec((tm, tn), lambda j, i: (i, j))
    in_specs = ([pl.BlockSpec((tm, k), lambda j, i: (i, 0))] + [wspec(o) for o in col_offsets]
                + [tile] * ne + [rspec(o) for _, o in rowvecs])
    sem = ("parallel", "arbitrary") if n_sums else ("parallel", "parallel")
    return pl.pallas_call(
        body, grid=(n_out // tn, m // tm), in_specs=in_specs,
        out_specs=[tile] * no + [pl.BlockSpec((1, tn), lambda j, i: (0, j))] * n_sums,
        out_shape=[_sds((m, n_out), dt) for dt in out_dtypes] + [_sds((1, n_out), F32)] * n_sums, name=name,
        compiler_params=_params(sem))(a, *([w] * nw), *extras, *[r for r, _ in rowvecs])


def mm_nt(name, g, w, epilogue, out_dtypes, extras=(), rowvecs=(), n_sums=0, tm=512, tk=512):
    m, n = g.shape
    k = w.shape[0]
    tm, tk = min(tm, m), min(tk, k)
    ne, nr, no = len(extras), len(rowvecs), len(out_dtypes)

    def body(g_ref, w_ref, *refs):
        e_refs, r_refs, o_refs, s_refs = refs[:ne], refs[ne:ne + nr], refs[ne + nr:ne + nr + no], refs[ne + nr + no:]
        acc = lax.dot_general(g_ref[...], w_ref[...], (((1,), (1,)), ((), ())), preferred_element_type=F32)
        outs = epilogue(acc, [e[...] for e in e_refs], [r[...] for r in r_refs])
        for o_ref, o in zip(o_refs, outs[:no]):
            o_ref[...] = o.astype(o_ref.dtype)
        if n_sums:
            @pl.when(pl.program_id(0) == 0)
            def _():
                for s_ref in s_refs:
                    s_ref[...] = jnp.zeros_like(s_ref)

            for s_ref, val in zip(s_refs, outs[no:]):
                s_ref[...] += val

    tile = pl.BlockSpec((tm, tk), lambda i, j: (i, j))
    vec = pl.BlockSpec((1, tk), lambda i, j: (0, j))
    sem = ("arbitrary", "parallel") if n_sums else ("parallel", "parallel")
    return pl.pallas_call(
        body, grid=(m // tm, k // tk),
        in_specs=[pl.BlockSpec((tm, n), lambda i, j: (i, 0)), pl.BlockSpec((tk, n), lambda i, j: (j, 0))]
        + [tile] * ne + [vec] * nr,
        out_specs=[tile] * no + [vec] * n_sums,
        out_shape=[_sds((m, k), dt) for dt in out_dtypes] + [_sds((1, k), F32)] * n_sums, name=name,
        compiler_params=_params(sem))(g, w, *extras, *rowvecs)


def mm_tn(name, a, g, tk=512, tn=512):
    m, k = a.shape
    n = g.shape[1]
    tk, tn = min(tk, k), min(tn, n)

    def body(a_ref, g_ref, o_ref):
        acc = lax.dot_general(a_ref[...], g_ref[...], (((0,), (0,)), ((), ())), preferred_element_type=F32)
        o_ref[...] = acc.astype(o_ref.dtype)

    return pl.pallas_call(
        body, grid=(k // tk, n // tn),
        in_specs=[pl.BlockSpec((m, tk), lambda i, j: (0, i)), pl.BlockSpec((m, tn), lambda i, j: (0, j))],
        out_specs=pl.BlockSpec((tk, tn), lambda i, j: (i, j)), out_shape=_sds((k, n), BF16), name=name,
        compiler_params=_params(("parallel", "parallel")))(a, g)


def _row_mask(tc):
    row = lax.broadcasted_iota(jnp.int32, (8 * tc, 256), 0) % 8
    col = lax.broadcasted_iota(jnp.int32, (8 * tc, 256), 1) // 32
    return row == col


def _expand_rows(val, mask):
    tc, width = val.shape
    rep = jnp.broadcast_to(val[:, None, :], (tc, 8, width)).reshape(8 * tc, width)
    return jnp.where(mask, rep, 0.0).astype(BF16)


def _stage(ref, val):
    ref[0] = val[:, 0:128]
    ref[1] = val[:, 128:256]


def _gather_rows(src_ref, tc):
    halves = []
    for half in range(2):
        col = lax.broadcasted_iota(jnp.int32, (tc, 128), 1) // 32 + 4 * half
        out = jnp.zeros((tc, 128), F32)
        for s8 in range(4 * half, 4 * half + 4):
            out = jnp.where(col == s8, src_ref.at[half][pl.ds(s8, tc, stride=8), :], out)
        halves.append(out)
    return jnp.concatenate(halves, axis=1)


def _gelu(x):
    c = math.sqrt(2.0 / math.pi)
    return 0.5 * x * (1.0 + jnp.tanh(c * (x + 0.044715 * x * x * x)))


def _gelu_grad(x):
    c = math.sqrt(2.0 / math.pi)
    t = jnp.tanh(c * (x + 0.044715 * x * x * x))
    return 0.5 * (1.0 + t) + 0.5 * x * (1.0 - t * t) * c * (1.0 + 3.0 * 0.044715 * x * x)


def s5_fwd(x, gain, d_skip, rb, rc, lam_r, lam_i):
    n_rows = x.shape[0]
    tc = min(S5_CHUNK, n_rows)
    nc = n_rows // tc

    def body(x_ref, g_ref, d_ref, rb_ref, rc_ref, lr_ref, li_ref, ge_ref, y2_ref, cs_ref, bux, yrows, carry):
        i = pl.program_id(0)
        u = _rms_hat(x_ref[...])[0] * g_ref[...]

        @pl.when(i == 0)
        def _():
            carry[...] = jnp.zeros_like(carry)

        cs_ref[0] = carry[...]
        mask = _row_mask(tc)
        for blk in range(S5_BLOCKS):
            lhs = _expand_rows(u[:, blk * 256:(blk + 1) * 256], mask)
            bux[blk] = jnp.dot(lhs, rb_ref[blk], preferred_element_type=F32)
        lam = [(lr_ref[blk], li_ref[blk]) for blk in range(S5_BLOCKS)]

        def step(t, c):
            r0 = pl.multiple_of(t * 8, 8)
            new = []
            for blk in range(S5_BLOCKS):
                xr, xi = c[2 * blk], c[2 * blk + 1]
                lr, li = lam[blk]
                nr = lr * xr - li * xi + bux[blk, pl.ds(r0, 8), 0:128]
                ni = lr * xi + li * xr + bux[blk, pl.ds(r0, 8), 128:256]
                bux[blk, pl.ds(r0, 8), 0:128] = nr
                bux[blk, pl.ds(r0, 8), 128:256] = ni
                new += [nr, ni]
            return tuple(new)

        c0 = []
        for blk in range(S5_BLOCKS):
            c0 += [carry[blk, :, 0:128], carry[blk, :, 128:256]]
        cn = lax.fori_loop(0, tc, step, tuple(c0), unroll=4)
        for blk in range(S5_BLOCKS):
            carry[blk, :, 0:128] = cn[2 * blk]
            carry[blk, :, 128:256] = cn[2 * blk + 1]
        for blk in range(S5_BLOCKS):
            _stage(yrows, jnp.dot(bux[blk].astype(BF16), rc_ref[blk], preferred_element_type=F32))
            sl = slice(blk * 256, (blk + 1) * 256)
            y2 = _gather_rows(yrows, tc) + d_ref[:, sl] * u[:, sl]
            y2_ref[:, sl] = y2
            ge_ref[:, sl] = _gelu(y2).astype(BF16)

    row = pl.BlockSpec((tc, D_MODEL), lambda i: (i, 0))
    vec = pl.BlockSpec((1, D_MODEL), lambda i: (0, 0))
    mat = pl.BlockSpec((S5_BLOCKS, 256, 256), lambda i: (0, 0, 0))
    lamspec = pl.BlockSpec((S5_BLOCKS, 8, 128), lambda i: (0, 0, 0))
    return pl.pallas_call(
        body, grid=(nc,),
        in_specs=[row, vec, vec, mat, mat, lamspec, lamspec],
        out_specs=[row, row, pl.BlockSpec((1, S5_BLOCKS, 8, 256), lambda i: (i, 0, 0, 0))],
        out_shape=[_sds((n_rows, D_MODEL), BF16), _sds((n_rows, D_MODEL), F32), _sds((nc, S5_BLOCKS, 8, 256), F32)],
        scratch_shapes=[pltpu.VMEM((S5_BLOCKS, 8 * tc, 256), F32), pltpu.VMEM((2, 8 * tc, 128), F32),
                        pltpu.VMEM((S5_BLOCKS, 8, 256), F32)],
        name="s5_fwd", compiler_params=_params(("arbitrary",)))(x, gain, d_skip, rb, rc, lam_r, lam_i)


def s5_bwd(x, gain, dy2, res, d_skip, cs, rb, rbt, rct, lam_r, lam_i):
    n_rows = x.shape[0]
    tc = min(S5_CHUNK, n_rows)
    nc = n_rows // tc

    def body(x_ref, g_ref, dy_ref, res_ref, d_ref, cs_ref, rb_ref, rbt_ref, rct_ref, lr_ref, li_ref,
             dx_ref, dd_ref, drb_ref, drc_ref, dlr_ref, dli_ref, dg_ref, tmp, du, lhsu, lhsd, xs, adj, acarry):
        i = pl.program_id(0)
        u = _rms_hat(x_ref[...])[0] * g_ref[...]

        @pl.when(i == 0)
        def _():
            acarry[...] = jnp.zeros_like(acarry)
            dd_ref[...] = jnp.zeros_like(dd_ref)
            drb_ref[...] = jnp.zeros_like(drb_ref)
            drc_ref[...] = jnp.zeros_like(drc_ref)
            dlr_ref[...] = jnp.zeros_like(dlr_ref)
            dli_ref[...] = jnp.zeros_like(dli_ref)
            dg_ref[...] = jnp.zeros_like(dg_ref)

        dd_ref[...] += jnp.sum(dy_ref[...] * u, axis=0, keepdims=True)
        mask = _row_mask(tc)
        for blk in range(S5_BLOCKS):
            sl = slice(blk * 256, (blk + 1) * 256)
            lhsu[blk] = _expand_rows(u[:, sl], mask)
            xs[blk] = jnp.dot(lhsu[blk], rb_ref[blk], preferred_element_type=F32)
            lhsd[blk] = _expand_rows(dy_ref[:, sl], mask)
            adj[blk] = jnp.dot(lhsd[blk], rct_ref[blk], preferred_element_type=F32)
        lam = [(lr_ref[blk], li_ref[blk]) for blk in range(S5_BLOCKS)]

        def fstep(t, c):
            r0 = pl.multiple_of(t * 8, 8)
            new = []
            for blk in range(S5_BLOCKS):
                xr, xi = c[2 * blk], c[2 * blk + 1]
                lr, li = lam[blk]
                nr = lr * xr - li * xi + xs[blk, pl.ds(r0, 8), 0:128]
                ni = lr * xi + li * xr + xs[blk, pl.ds(r0, 8), 128:256]
                xs[blk, pl.ds(r0, 8), 0:128] = nr
                xs[blk, pl.ds(r0, 8), 128:256] = ni
                new += [nr, ni]
            return tuple(new)

        c0 = []
        for blk in range(S5_BLOCKS):
            c0 += [cs_ref[0, blk, :, 0:128], cs_ref[0, blk, :, 128:256]]
        lax.fori_loop(0, tc, fstep, tuple(c0), unroll=4)

        def bstep(k, c):
            t = tc - 1 - k
            r0 = pl.multiple_of(t * 8, 8)
            rp = pl.multiple_of(jnp.maximum(t - 1, 0) * 8, 8)
            first = t == 0
            new_a, new_g = [], []
            for blk in range(S5_BLOCKS):
                ar, ai = c[0][2 * blk], c[0][2 * blk + 1]
                glr, gli = c[1][2 * blk], c[1][2 * blk + 1]
                lr, li = lam[blk]
                nr = lr * ar + li * ai + adj[blk, pl.ds(r0, 8), 0:128]
                ni = lr * ai - li * ar + adj[blk, pl.ds(r0, 8), 128:256]
                adj[blk, pl.ds(r0, 8), 0:128] = nr
                adj[blk, pl.ds(r0, 8), 128:256] = ni
                pr = jnp.where(first, cs_ref[0, blk, :, 0:128], xs[blk, pl.ds(rp, 8), 0:128])
                pi = jnp.where(first, cs_ref[0, blk, :, 128:256], xs[blk, pl.ds(rp, 8), 128:256])
                new_a += [nr, ni]
                new_g += [glr + nr * pr + ni * pi, gli + ni * pr - nr * pi]
            return tuple(new_a), tuple(new_g)

        a0, g0 = [], []
        for blk in range(S5_BLOCKS):
            a0 += [acarry[blk, :, 0:128], acarry[blk, :, 128:256]]
            g0 += [dlr_ref[blk], dli_ref[blk]]
        an, gn = lax.fori_loop(0, tc, bstep, (tuple(a0), tuple(g0)), unroll=2)
        for blk in range(S5_BLOCKS):
            acarry[blk, :, 0:128] = an[2 * blk]
            acarry[blk, :, 128:256] = an[2 * blk + 1]
            dlr_ref[blk] = gn[2 * blk]
            dli_ref[blk] = gn[2 * blk + 1]
        for blk in range(S5_BLOCKS):
            sl = slice(blk * 256, (blk + 1) * 256)
            ab = adj[blk].astype(BF16)
            _stage(tmp, jnp.dot(ab, rbt_ref[blk], preferred_element_type=F32))
            du[:, sl] = _gather_rows(tmp, tc) + d_ref[:, sl] * dy_ref[:, sl]
            drb_ref[blk] += lax.dot_general(lhsu[blk], ab, (((0,), (0,)), ((), ())), preferred_element_type=F32)
            drc_ref[blk] += lax.dot_general(lhsd[blk], xs[blk].astype(BF16), (((0,), (0,)), ((), ())),
                                            preferred_element_type=F32)
        xh, r = _rms_hat(x_ref[...])
        dg_ref[...] += jnp.sum(du[...] * xh, axis=0, keepdims=True)
        dxh = du[...] * g_ref[...]
        dx_ref[...] = r * (dxh - xh * jnp.mean(dxh * xh, axis=-1, keepdims=True)) + res_ref[...]

    rev = pl.BlockSpec((tc, D_MODEL), lambda i: (nc - 1 - i, 0))
    vec = pl.BlockSpec((1, D_MODEL), lambda i: (0, 0))
    mat = pl.BlockSpec((S5_BLOCKS, 256, 256), lambda i: (0, 0, 0))
    lamspec = pl.BlockSpec((S5_BLOCKS, 8, 128), lambda i: (0, 0, 0))
    big = pltpu.VMEM((S5_BLOCKS, 8 * tc, 256), F32)
    bigb = pltpu.VMEM((S5_BLOCKS, 8 * tc, 256), BF16)
    return pl.pallas_call(
        body, grid=(nc,),
        in_specs=[rev, vec, rev, rev, vec, pl.BlockSpec((1, S5_BLOCKS, 8, 256), lambda i: (nc - 1 - i, 0, 0, 0)),
                  mat, mat, mat, lamspec, lamspec],
        out_specs=[rev, vec, mat, mat, lamspec, lamspec, vec],
        out_shape=[_sds((n_rows, D_MODEL), F32), _sds((1, D_MODEL), F32), _sds((S5_BLOCKS, 256, 256), F32),
                   _sds((S5_BLOCKS, 256, 256), F32), _sds((S5_BLOCKS, 8, 128), F32), _sds((S5_BLOCKS, 8, 128), F32),
                   _sds((1, D_MODEL), F32)],
        scratch_shapes=[pltpu.VMEM((2, 8 * tc, 128), F32), pltpu.VMEM((tc, D_MODEL), F32), bigb, bigb, big, big,
                        pltpu.VMEM((S5_BLOCKS, 8, 256), F32)],
        name="s5_bwd", compiler_params=_params(("arbitrary",)))(
            x, gain, dy2, res, d_skip, cs, rb, rbt, rct, lam_r, lam_i)


def _s5_views(a_re, a_im, log_dt, b_re, b_im):
    return a_re[:, None, :], a_im[:, None, :], log_dt[:, None, None], jnp.swapaxes(b_re, 1, 2), jnp.swapaxes(b_im, 1, 2)


def _s5_factors(a_re, a_im, log_dt):
    lr, li, dt = jnp.minimum(a_re, LAMBDA_RE_MAX), a_im, jnp.exp(log_dt)
    mag, ang = jnp.exp(lr * dt), li * dt
    lbr, lbi = mag * jnp.cos(ang), mag * jnp.sin(ang)
    den = lr * lr + li * li
    fr, fi = ((lbr - 1.0) * lr + lbi * li) / den, (lbi * lr - (lbr - 1.0) * li) / den
    return lr, li, dt, lbr, lbi, fr, fi, den


def s5_prep(a_re, a_im, log_dt, b_re, b_im, c_re, c_im):
    def body(ar_ref, ai_ref, t_ref, br_ref, bi_ref, cr_ref, ci_ref, rb_ref, rbt_ref, rc_ref, rct_ref, lr_ref, li_ref):
        _, _, _, lbr, lbi, fr, fi, _ = _s5_factors(ar_ref[...], ai_ref[...], t_ref[...])
        lr_ref[...] = lbr
        li_ref[...] = lbi
        bre = fr * br_ref[...] - fi * bi_ref[...]
        bim = fr * bi_ref[...] + fi * br_ref[...]
        even = (lax.broadcasted_iota(jnp.int32, (256, S5_STATE), 0) // S5_GROUP) % 2 == 0

        def assemble(re, im):
            re, im = re.reshape(256, S5_STATE), im.reshape(256, S5_STATE)
            return jnp.concatenate([jnp.where(even, re, 0.0), jnp.where(even, 0.0, re), jnp.where(even, im, 0.0),
                                    jnp.where(even, 0.0, im)], axis=1)

        for blk in range(S5_BLOCKS):
            sl = slice(16 * blk, 16 * blk + 16)
            rb = assemble(bre[sl], bim[sl])
            rct = assemble(cr_ref[sl], -ci_ref[sl])
            rb_ref[blk] = rb.astype(BF16)
            rbt_ref[blk] = rb.T.astype(BF16)
            rct_ref[blk] = rct.astype(BF16)
            rc_ref[blk] = rct.T.astype(BF16)

    vm = pl.BlockSpec(memory_space=pltpu.VMEM)
    mat = _sds((S5_BLOCKS, 256, 256), BF16)
    lam = _sds((S5_GROUPS, 1, S5_STATE), F32)
    rb, rbt, rc, rct, lam_r, lam_i = pl.pallas_call(
        body, in_specs=[vm] * 7, out_specs=[vm] * 6, out_shape=[mat, mat, mat, mat, lam, lam], name="s5_prep",
        compiler_params=_params())(*_s5_views(a_re, a_im, log_dt, b_re, b_im), c_re, c_im)
    return rb, rbt, rc, rct, lam_r.reshape(S5_BLOCKS, 8, 128), lam_i.reshape(S5_BLOCKS, 8, 128)


def s5_param_bwd(mats, a_re, a_im, log_dt, b_re, b_im):
    def body(m_ref, glr_ref, gli_ref, ar_ref, ai_ref, t_ref, br_ref, bi_ref,
             dar_ref, dai_ref, dt_ref, dbr_ref, dbi_ref, dcr_ref, dci_ref):
        lr, li, dt, lbr, lbi, fr, fi, den = _s5_factors(ar_ref[...], ai_ref[...], t_ref[...])
        shape = (S5_GROUPS, S5_GROUP, S5_STATE)
        gbr, gbi = m_ref[0:1024, 0:64].reshape(shape), m_ref[0:1024, 64:128].reshape(shape)
        dcr_ref[...] = m_ref[1024:2048, 0:64].reshape(shape)
        dci_ref[...] = -m_ref[1024:2048, 64:128].reshape(shape)
        br, bi = br_ref[...], bi_ref[...]
        dbr_ref[...] = fr * gbr + fi * gbi
        dbi_ref[...] = fr * gbi - fi * gbr
        dfr = jnp.sum(gbr * br + gbi * bi, axis=1, keepdims=True)
        dfi = jnp.sum(gbi * br - gbr * bi, axis=1, keepdims=True)
        nr, ni = (dfr * lr - dfi * li) / den, (dfr * li + dfi * lr) / den
        qr, qi = (fr * lr + fi * li) / den, (fi * lr - fr * li) / den
        lam_r, lam_i = -(dfr * qr + dfi * qi), -(dfi * qr - dfr * qi)
        gr, gi = glr_ref[...] + nr, gli_ref[...] + ni
        zr, zi = gr * lbr + gi * lbi, gi * lbr - gr * lbi
        a = ar_ref[...]
        dar_ref[...] = (lam_r + zr * dt) * jnp.where(a < LAMBDA_RE_MAX, 1.0, jnp.where(a == LAMBDA_RE_MAX, 0.5, 0.0))
        dai_ref[...] = lam_i + zi * dt
        dt_ref[...] = jnp.sum(zr * lr + zi * li, axis=2, keepdims=True) * dt

    vm = pl.BlockSpec(memory_space=pltpu.VMEM)
    state = _sds((S5_GROUPS, 1, S5_STATE), F32)
    wide = _sds((S5_GROUPS, S5_GROUP, S5_STATE), F32)
    glr = mats[2048:2080].reshape(S5_GROUPS, 1, S5_STATE)
    gli = mats[2080:2112].reshape(S5_GROUPS, 1, S5_STATE)
    dar, dai, ddt, dbr, dbi, dcr, dci = pl.pallas_call(
        body, in_specs=[vm] * 8, out_specs=[vm] * 7,
        out_shape=[state, state, _sds((S5_GROUPS, 1, 1), F32), wide, wide, wide, wide], name="s5_param_bwd",
        compiler_params=_params())(mats, glr, gli, *_s5_views(a_re, a_im, log_dt, b_re, b_im))
    return (dar.reshape(S5_GROUPS, S5_STATE), dai.reshape(S5_GROUPS, S5_STATE), ddt.reshape(S5_GROUPS),
            jnp.swapaxes(dbr, 1, 2), jnp.swapaxes(dbi, 1, 2), dcr, dci)


def s5_compact(drb, drct, dlr, dli):
    def body(drb_ref, drct_ref, dlr_ref, dli_ref, o_ref):
        even = (lax.broadcasted_iota(jnp.int32, (256, 64), 0) // S5_GROUP) % 2 == 0
        for blk in range(S5_BLOCKS):
            for k, ref in enumerate((drb_ref, drct_ref)):
                m = ref[blk]
                re = jnp.where(even, m[:, 0:64], m[:, 64:128])
                im = jnp.where(even, m[:, 128:192], m[:, 192:256])
                o_ref[pl.ds(k * 1024 + blk * 256, 256), :] = jnp.concatenate([re, im], axis=1)
            o_ref[pl.ds(2048 + blk * 8, 8), :] = dlr_ref[blk]
            o_ref[pl.ds(2080 + blk * 8, 8), :] = dli_ref[blk]

    vm = pl.BlockSpec(memory_space=pltpu.VMEM)
    return pl.pallas_call(body, in_specs=[vm] * 4, out_specs=vm, out_shape=_sds((2112, 128), F32), name="s5_compact",
                          compiler_params=_params())(drb, drct, dlr, dli)


NEG = -1e30


GROUP = N_Q // N_KV


def _attn_masks(n):
    qi = lax.broadcasted_iota(jnp.int32, (GROUP * BLOCK, BLOCK), 0) % BLOCK
    kj = lax.broadcasted_iota(jnp.int32, (GROUP * BLOCK, BLOCK), 1)
    return jnp.logical_and(kj > qi, n > 0), kj <= qi


def _stack_heads(ref, kh):
    return jnp.concatenate([ref[:, (GROUP * kh + g) * HEAD_DIM:(GROUP * kh + g + 1) * HEAD_DIM] for g in range(GROUP)], axis=0)


def _unstack_heads(val):
    return jnp.concatenate([val[g * BLOCK:(g + 1) * BLOCK] for g in range(GROUP)], axis=1)


def _sink_column(sink_ref, kh):
    grp = lax.broadcasted_iota(jnp.int32, (GROUP * BLOCK, 1), 0) // BLOCK
    col = jnp.zeros((GROUP * BLOCK, 1), F32)
    for g in range(GROUP):
        col = jnp.where(grp == g, sink_ref[GROUP * kh + g], col)
    return col, grp


def _attn_exp(q4, kp, kc, sink, mask_p, mask_c):
    scale = 1.0 / math.sqrt(HEAD_DIM)
    nt = (((1,), (1,)), ((), ()))
    sp = jnp.where(mask_p, lax.dot_general(q4, kp, nt, preferred_element_type=F32) * scale, NEG)
    sc = jnp.where(mask_c, lax.dot_general(q4, kc, nt, preferred_element_type=F32) * scale, NEG)
    m = jnp.maximum(jnp.maximum(jnp.max(sp, axis=-1, keepdims=True), jnp.max(sc, axis=-1, keepdims=True)), sink)
    pp = jnp.exp(sp - m)
    pc = jnp.exp(sc - m)
    ps = jnp.exp(sink - m)
    inv = 1.0 / (jnp.sum(pp, axis=-1, keepdims=True) + jnp.sum(pc, axis=-1, keepdims=True) + ps)
    return pp, pc, ps, inv


def attn_fwd(q, kv, sinks):
    n_rows = q.shape[0]
    nb = n_rows // BLOCK

    def body(sink_ref, q_ref, kvp_ref, kvc_ref, o_ref):
        n = pl.program_id(0)
        mask_p, mask_c = _attn_masks(n)
        outs = []
        for kh in range(N_KV):
            ks, vs = slice(kh * HEAD_DIM, (kh + 1) * HEAD_DIM), slice((N_KV + kh) * HEAD_DIM, (N_KV + kh + 1) * HEAD_DIM)
            sink, _ = _sink_column(sink_ref, kh)
            pp, pc, _, inv = _attn_exp(_stack_heads(q_ref, kh), kvp_ref[:, ks], kvc_ref[:, ks], sink, mask_p, mask_c)
            o4 = (jnp.dot(pp.astype(BF16), kvp_ref[:, vs], preferred_element_type=F32)
                  + jnp.dot(pc.astype(BF16), kvc_ref[:, vs], preferred_element_type=F32)) * inv
            outs.append(_unstack_heads(o4))
        o_ref[...] = jnp.concatenate(outs, axis=1).astype(BF16)

    kvw = 2 * N_KV * HEAD_DIM
    return pl.pallas_call(
        body, grid=(nb,),
        in_specs=[pl.BlockSpec(memory_space=pltpu.SMEM), pl.BlockSpec((BLOCK, D_MODEL), lambda n: (n, 0)),
                  pl.BlockSpec((BLOCK, kvw), lambda n: (jnp.maximum(n - 1, 0), 0)), pl.BlockSpec((BLOCK, kvw), lambda n: (n, 0))],
        out_specs=pl.BlockSpec((BLOCK, D_MODEL), lambda n: (n, 0)), out_shape=_sds((n_rows, D_MODEL), BF16),
        name="attn_fwd", compiler_params=_params(("parallel",)))(sinks, q, kv, kv)


def attn_bwd(q, kv, do, sinks):
    n_rows = q.shape[0]
    nb = n_rows // BLOCK
    kvw = 2 * N_KV * HEAD_DIM
    tn = (((0,), (0,)), ((), ()))
    nt = (((1,), (1,)), ((), ()))
    scale = 1.0 / math.sqrt(HEAD_DIM)

    def body(sink_ref, q_ref, kvp_ref, kvc_ref, do_ref, dq_ref, dbq_ref, dprev_ref, dcur_ref, dsink_ref):
        n = pl.program_id(0)
        mask_p, mask_c = _attn_masks(n)
        lane = lax.broadcasted_iota(jnp.int32, (1, D_MODEL), 1)
        dqs, dsink = [], jnp.zeros((1, D_MODEL), F32)
        dkp, dkc, dvp, dvc = [], [], [], []
        for kh in range(N_KV):
            ks, vs = slice(kh * HEAD_DIM, (kh + 1) * HEAD_DIM), slice((N_KV + kh) * HEAD_DIM, (N_KV + kh + 1) * HEAD_DIM)
            q4, do4 = _stack_heads(q_ref, kh), _stack_heads(do_ref, kh)
            kp, kc, vp, vc = kvp_ref[:, ks], kvc_ref[:, ks], kvp_ref[:, vs], kvc_ref[:, vs]
            sink, grp = _sink_column(sink_ref, kh)
            pp, pc, ps, inv = _attn_exp(q4, kp, kc, sink, mask_p, mask_c)
            pp, pc = pp * inv, pc * inv
            dpp = lax.dot_general(do4, vp, nt, preferred_element_type=F32)
            dpc = lax.dot_general(do4, vc, nt, preferred_element_type=F32)
            delta = jnp.sum(pp * dpp, axis=-1, keepdims=True) + jnp.sum(pc * dpc, axis=-1, keepdims=True)
            dsp = (pp * (dpp - delta) * scale).astype(BF16)
            dsc = (pc * (dpc - delta) * scale).astype(BF16)
            dsk = ps * inv * delta
            for g in range(GROUP):
                dsink = dsink + jnp.where(lane == GROUP * kh + g, -jnp.sum(jnp.where(grp == g, dsk, 0.0)), 0.0)
            dqs.append(_unstack_heads(jnp.dot(dsp, kp, preferred_element_type=F32)
                                      + jnp.dot(dsc, kc, preferred_element_type=F32)))
            dkp.append(lax.dot_general(dsp, q4, tn, preferred_element_type=F32))
            dkc.append(lax.dot_general(dsc, q4, tn, preferred_element_type=F32))
            dvp.append(lax.dot_general(pp.astype(BF16), do4, tn, preferred_element_type=F32))
            dvc.append(lax.dot_general(pc.astype(BF16), do4, tn, preferred_element_type=F32))
        dq = jnp.concatenate(dqs, axis=1)
        dq_ref[...] = dq.astype(BF16)
        dprev_ref[0] = jnp.concatenate(dkp + dvp, axis=1)
        dcur_ref[0] = jnp.concatenate(dkc + dvc, axis=1)

        @pl.when(n == 0)
        def _():
            dbq_ref[...] = jnp.zeros_like(dbq_ref)
            dsink_ref[...] = jnp.zeros_like(dsink_ref)

        dbq_ref[...] += jnp.sum(dq, axis=0, keepdims=True)
        dsink_ref[...] += dsink

    blk = pl.BlockSpec((BLOCK, D_MODEL), lambda n: (n, 0))
    part = pl.BlockSpec((1, BLOCK, kvw), lambda n: (n, 0, 0))
    return pl.pallas_call(
        body, grid=(nb,),
        in_specs=[pl.BlockSpec(memory_space=pltpu.SMEM), blk,
                  pl.BlockSpec((BLOCK, kvw), lambda n: (jnp.maximum(n - 1, 0), 0)), pl.BlockSpec((BLOCK, kvw), lambda n: (n, 0)), blk],
        out_specs=[blk, pl.BlockSpec((1, D_MODEL), lambda n: (0, 0)), part, part, pl.BlockSpec((1, D_MODEL), lambda n: (0, 0))],
        out_shape=[_sds((n_rows, D_MODEL), BF16), _sds((1, D_MODEL), F32), _sds((nb, BLOCK, kvw), F32),
                   _sds((nb, BLOCK, kvw), F32), _sds((1, D_MODEL), F32)],
        name="attn_bwd", compiler_params=_params(("arbitrary",)))(sinks, q, kv, kv, do)


def kv_combine(dprev, dcur):
    nb, _, kvw = dprev.shape

    def body(dcur_ref, dprev_ref, dkv_ref, db_ref):
        total = jnp.zeros((1, kvw), F32)
        for m in range(nb):
            dkv = dcur_ref[m] + dprev_ref[m + 1] if m + 1 < nb else dcur_ref[m]
            dkv_ref[m * BLOCK:(m + 1) * BLOCK, :] = dkv.astype(BF16)
            total = total + jnp.sum(dkv, axis=0, keepdims=True)
        db_ref[...] = jnp.concatenate([total, jnp.zeros((1, D_MODEL - kvw), F32)], axis=1)

    vm = pl.BlockSpec(memory_space=pltpu.VMEM)
    return pl.pallas_call(body, in_specs=[vm, vm], out_specs=[vm, vm],
                          out_shape=[_sds((nb * BLOCK, kvw), BF16), _sds((1, D_MODEL), F32)], name="kv_combine",
                          compiler_params=_params())(dcur, dprev)


def glu_bwd(dout, val, gate, tm=256):
    n_rows, d = dout.shape

    def body(do_ref, v_ref, g_ref, dz_ref, db_ref):
        i = pl.program_id(0)
        sg = jax.nn.sigmoid(g_ref[...])
        dval = do_ref[...] * sg
        dgate = do_ref[...] * v_ref[...] * sg * (1.0 - sg)
        dz_ref[...] = jnp.concatenate([dval, dgate], axis=1).astype(BF16)

        @pl.when(i == 0)
        def _():
            db_ref[...] = jnp.zeros_like(db_ref)

        db_ref[0:1, :] += jnp.sum(dval, axis=0, keepdims=True)
        db_ref[1:2, :] += jnp.sum(dgate, axis=0, keepdims=True)

    row = pl.BlockSpec((tm, d), lambda i: (i, 0))
    return pl.pallas_call(
        body, grid=(n_rows // tm,), in_specs=[row, row, row],
        out_specs=[pl.BlockSpec((tm, 2 * d), lambda i: (i, 0)), pl.BlockSpec((2, d), lambda i: (0, 0))],
        out_shape=[_sds((n_rows, 2 * d), BF16), _sds((2, d), F32)],
        name="glu_bwd", compiler_params=_params(("arbitrary",)))(dout, val, gate)


def _adam_update(w, g, m, v):
    nm = ADAM_B1 * m + (1.0 - ADAM_B1) * g
    nv = ADAM_B2 * v + (1.0 - ADAM_B2) * (g * g)
    m_hat = nm / (1.0 - ADAM_B1 ** ADAM_STEP)
    v_hat = nv / (1.0 - ADAM_B2 ** ADAM_STEP)
    return -ADAM_LR * (m_hat / (jnp.sqrt(v_hat) + ADAM_EPS) + ADAM_WD * w), nm, nv


def adamw(name, w, g, m, v, tm=256):
    n_rows, d = w.shape
    tm = tm if n_rows % tm == 0 else n_rows

    def body(w_ref, g_ref, m_ref, v_ref, go_ref, d_ref, nm_ref, nv_ref):
        gv = g_ref[...]
        go_ref[...] = gv
        d_ref[...], nm_ref[...], nv_ref[...] = _adam_update(w_ref[...], gv, m_ref[...], v_ref[...])

    row = pl.BlockSpec((tm, d), lambda i: (i, 0))
    return pl.pallas_call(
        body, grid=(n_rows // tm,), in_specs=[row] * 4, out_specs=[row] * 4,
        out_shape=[_sds((n_rows, d), F32)] * 4, name=name, compiler_params=_params(("parallel",)))(w, g, m, v)


def adamw_native(name, ws, gs, ms, vs):
    n = len(ws)

    def body(*refs):
        w_refs, g_refs, m_refs, v_refs = refs[:n], refs[n:2 * n], refs[2 * n:3 * n], refs[3 * n:4 * n]
        d_refs, nm_refs, nv_refs = refs[4 * n:5 * n], refs[5 * n:6 * n], refs[6 * n:7 * n]
        for k in range(n):
            dl, nm, nv = _adam_update(w_refs[k][...], g_refs[k][...], m_refs[k][...], v_refs[k][...])
            d_refs[k][...] = dl
            nm_refs[k][...] = nm
            nv_refs[k][...] = nv

    vm = pl.BlockSpec(memory_space=pltpu.VMEM)
    shapes = [_sds(w.shape, F32) for w in ws]
    out = pl.pallas_call(body, in_specs=[vm] * (4 * n), out_specs=[vm] * (3 * n), out_shape=shapes * 3, name=name,
                         compiler_params=_params())(*ws, *gs, *ms, *vs)
    return list(out[:n]), list(out[n:2 * n]), list(out[2 * n:])


VEC_ROWS = {"norm_mix": 0, "norm_mlp": 2, "norm_kv": 4, "norm_final": 5, "s5_d": 6, "b_q": 7, "b_o": 8, "s5_b_glu": 9,
            "b_kv": 11, "sinks": 12, "loss": 13}


def split_vectors(where, vecs, d_shard, glu_shard):
    kvw = 2 * N_KV * HEAD_DIM
    shapes = {"norm_mix": (2, D_MODEL), "norm_mlp": (2, D_MODEL), "norm_kv": (1, D_MODEL), "norm_final": (1, D_MODEL),
              "s5_d": (1, d_shard), "b_q": (1, D_MODEL), "b_o": (1, D_MODEL), "s5_b_glu": (1, glu_shard), "b_kv": (1, kvw),
              "sinks": (1, N_Q), "loss": (1, 128)}
    names = list(shapes)

    def body(where_ref, v_ref, *o_refs):
        chip = where_ref[1]
        for name, o_ref in zip(names, o_refs):
            r0, (r, n) = VEC_ROWS[name], shapes[name]
            if name == "s5_d":
                g = jnp.zeros((1, n), F32)
                for j in range(4):
                    g = jnp.where(chip == j, v_ref[r0:r0 + 1, j * n:(j + 1) * n], g)
            elif name == "s5_b_glu":
                g = jnp.zeros((1, n), F32)
                for j in range(4):
                    row, col = r0 + (j * n) // D_MODEL, (j * n) % D_MODEL
                    g = jnp.where(chip == j, v_ref[row:row + 1, col:col + n], g)
            else:
                g = v_ref[r0:r0 + r, 0:n]
            o_ref[...] = g

    vm = pl.BlockSpec(memory_space=pltpu.VMEM)
    out = pl.pallas_call(body, in_specs=[pl.BlockSpec(memory_space=pltpu.SMEM), vm], out_specs=[vm] * len(names),
                         out_shape=[_sds(shapes[n], F32) for n in names], name="split_vectors",
                         compiler_params=_params())(where, vecs)
    return dict(zip(names, out))


def _position():
    x, y, c = lax.axis_index("x"), lax.axis_index("y"), lax.axis_index("c")
    others = [(1 - x, y), (x, 1 - y), (1 - x, 1 - y)]
    return x, y, c, others


def _window(ref, kind, chip, half, shard_shape):
    r, n = shard_shape
    if kind == "col":
        return ref.at[pl.ds(pl.multiple_of(half * (r // 2), 16), r // 2), pl.ds(pl.multiple_of(chip * n, 128), n)]
    return ref.at[pl.ds(pl.multiple_of(chip * r, 16), r), pl.ds(pl.multiple_of(half * (n // 2), 128), n // 2)]


def _half(ref, kind, half, shape):
    r, n = shape
    if kind == "col":
        return ref.at[pl.ds(pl.multiple_of(half * (r // 2), 16), r // 2), :]
    return ref.at[:, pl.ds(pl.multiple_of(half * (n // 2), 128), n // 2)]


def swap_start(name, grads, kinds):
    nt = len(grads)
    shapes = [tuple(g.shape) for g in grads]
    lands = [lax.empty(sh, BF16) for sh in shapes]

    def body(*refs):
        in_refs, land_refs = refs[:nt], refs[nt:2 * nt]
        send_sems, recv_sems, token = refs[2 * nt], refs[2 * nt + 1], refs[-1]
        x, y, c, _ = _position()
        for t in range(nt):
            pltpu.make_async_remote_copy(
                src_ref=_half(in_refs[t], kinds[t], 1 - c, shapes[t]), dst_ref=_half(land_refs[t], kinds[t], 1 - c, shapes[t]),
                send_sem=send_sems.at[t], recv_sem=recv_sems.at[t], device_id=(x, y, 1 - c), device_id_type=MESH).start()
        token[...] = jnp.zeros_like(token)

    sems = pltpu.SemaphoreType.DMA((nt,))
    both = list(grads) + lands
    out = pl.pallas_call(
        body, name=name, in_specs=[HBM_SPEC] * (2 * nt),
        out_specs=(SEM_SPEC, SEM_SPEC, *[HBM_SPEC] * (2 * nt), pl.BlockSpec(memory_space=pltpu.VMEM)),
        out_shape=(sems, sems, *[pltpu.HBM(a.shape, a.dtype) for a in both], _sds((8, 128), F32)),
        input_output_aliases={t: 2 + t for t in range(2 * nt)}, compiler_params=_split_params(),
    )(*[_in_hbm(a) for a in both])
    return out[0], out[1], list(out[2:2 + nt]), list(out[2 + nt:2 + 2 * nt]), out[-1]


def swap_wait(name, send_sems, recv_sems, grads, lands, kinds, after):
    nt = len(grads)
    shapes = [tuple(g.shape) for g in grads]

    def body(*refs):
        in_refs, land_refs = refs[:nt], refs[nt:2 * nt]
        send_ref, recv_ref = refs[2 * nt], refs[2 * nt + 1]
        x, y, c, _ = _position()
        for t in range(nt):
            cp = pltpu.make_async_remote_copy(
                src_ref=_half(in_refs[t], kinds[t], 1 - c, shapes[t]), dst_ref=_half(land_refs[t], kinds[t], c, shapes[t]),
                send_sem=send_ref.at[t], recv_sem=recv_ref.at[t], device_id=(x, y, 1 - c), device_id_type=MESH)
            cp.wait_send()
            cp.wait_recv()

    both = list(grads) + list(lands)
    out = pl.pallas_call(
        body, name=name, in_specs=[HBM_SPEC] * (2 * nt) + [SEM_SPEC, SEM_SPEC, HBM_SPEC], out_specs=[HBM_SPEC] * (2 * nt),
        out_shape=[pltpu.HBM(a.shape, a.dtype) for a in both], input_output_aliases={t: t for t in range(2 * nt)},
        compiler_params=_split_params())(*both, send_sems, recv_sems, _in_hbm(after))
    return list(out[:nt]), list(out[nt:])


def _half_spec(kind, shape, tiles):
    r, n = shape
    if kind == "col":
        tn = n // tiles
        return pl.BlockSpec((r // 2, tn), lambda i, s: (s[0], i))
    tm = r // tiles
    return pl.BlockSpec((tm, n // 2), lambda i, s: (i, s[0]))


def add_halves(name, mine, landed, kinds, where, tiles=4):
    nt = len(mine)
    shapes = [tuple(a.shape) for a in mine]

    def compact(t):
        r, n = shapes[t]
        if kinds[t] == "col":
            return (r // 2, n), pl.BlockSpec((r // 2, n // tiles), lambda i, s: (0, i))
        return (r, n // 2), pl.BlockSpec((r // tiles, n // 2), lambda i, s: (i, 0))

    def body(s_ref, *refs):
        for a_ref, b_ref, o_ref in zip(refs[:nt], refs[nt:2 * nt], refs[2 * nt:]):
            o_ref[...] = (a_ref[...].astype(F32) + b_ref[...].astype(F32)).astype(BF16)

    specs = [_half_spec(kinds[t], shapes[t], tiles) for t in range(nt)]
    return pl.pallas_call(
        body, grid_spec=pltpu.PrefetchScalarGridSpec(num_scalar_prefetch=1, grid=(tiles,), in_specs=specs + specs,
                                                     out_specs=[compact(t)[1] for t in range(nt)]),
        out_shape=[_sds(compact(t)[0], BF16) for t in range(nt)], name=name,
        compiler_params=_params(("parallel",)))(where, *mine, *landed)


def sum_shards(name, parts, landed, kinds, shard_shapes, where, layers, n_layers, intos, tiles=2):
    nt = len(parts)
    in_specs, out_specs = [], []
    for t in range(nt):
        (r, n), layer = shard_shapes[t], layers[t]
        if kinds[t] == "col":
            tm, width = r // 2 // tiles, n
            own = pl.BlockSpec((tm, n), lambda i, s: (i, s[1]))
            out = pl.BlockSpec((None, tm, n), lambda i, s, layer=layer: (layer, s[0] * tiles + i, 0))
        else:
            tm, width = r // tiles, n // 2
            own = pl.BlockSpec((tm, n // 2), lambda i, s: (s[1] * tiles + i, 0))
            out = pl.BlockSpec((None, tm, n // 2), lambda i, s, layer=layer: (layer, i, s[0]))
        in_specs += [own, pl.BlockSpec((3, tm, width), lambda i, s: (0, i, 0))]
        out_specs.append(out)
    args, aliases = [where] + [a for pair in zip(parts, landed) for a in pair], {}
    for t in range(nt):
        if intos[t] is not None:
            aliases[len(args)] = t
            in_specs.append(pl.BlockSpec(memory_space=pl.ANY))
            args.append(intos[t])

    def body(s_ref, *refs):
        for t in range(nt):
            a_ref, l_ref, o_ref = refs[2 * t], refs[2 * t + 1], refs[len(in_specs) + t]
            o_ref[...] = ((a_ref[...].astype(F32) + l_ref[0].astype(F32)) + l_ref[1].astype(F32)) + l_ref[2].astype(F32)

    return pl.pallas_call(
        body, grid_spec=pltpu.PrefetchScalarGridSpec(num_scalar_prefetch=1, grid=(tiles,), in_specs=in_specs,
                                                     out_specs=out_specs),
        out_shape=[_sds((n_layers[t],) + tuple(shard_shapes[t]), F32) for t in range(nt)], input_output_aliases=aliases,
        name=name, compiler_params=_params(("parallel",)))(*args)


def share_halves(arrays, entries):
    na, nt = len(arrays), len(entries)

    def body(*refs):
        out_refs = refs[na:2 * na]
        send_sems, recv_sems = refs[2 * na:]
        x, y, c, _ = _position()
        cps = []
        for t, (a, layer, kind) in enumerate(entries):
            shape = tuple(arrays[a].shape[1:])
            mine = _half(out_refs[a].at[layer], kind, c, shape)
            cp = pltpu.make_async_remote_copy(
                src_ref=mine, dst_ref=mine, send_sem=send_sems.at[t], recv_sem=recv_sems.at[t],
                device_id=(x, y, 1 - c), device_id_type=MESH)
            cp.start()
            cps.append(cp)
        for t, (a, layer, kind) in enumerate(entries):
            shape = tuple(arrays[a].shape[1:])
            other = _half(out_refs[a].at[layer], kind, 1 - c, shape)
            pltpu.make_async_remote_copy(
                src_ref=other, dst_ref=other, send_sem=send_sems.at[t], recv_sem=recv_sems.at[t],
                device_id=(x, y, 1 - c), device_id_type=MESH).wait_recv()
        for cp in cps:
            cp.wait_send()

    hbm = pl.BlockSpec(memory_space=pl.ANY)
    return pl.pallas_call(
        body, in_specs=[hbm] * na, out_specs=[hbm] * na, out_shape=[_sds(a.shape, F32) for a in arrays],
        input_output_aliases={i: i for i in range(na)},
        scratch_shapes=[pltpu.SemaphoreType.DMA((nt,)), pltpu.SemaphoreType.DMA((nt,))],
        name="share_halves", compiler_params=_params())(*arrays)


HBM_SPEC = pl.BlockSpec(memory_space=pltpu.HBM)
SEM_SPEC = pl.BlockSpec(memory_space=pltpu.SEMAPHORE)
ANY_SPEC = pl.BlockSpec(memory_space=pl.ANY)


def _split_params():
    return pltpu.CompilerParams(has_side_effects=pltpu.SideEffectType.DATAFLOW_SIDE_EFFECTING,
                                vmem_limit_bytes=VMEM_LIMIT_BYTES)


def _in_hbm(a):
    return pltpu.with_memory_space_constraint(a, pltpu.HBM)


def cast_place(arrays, entries, where, tiles=2):
    in_specs, out_specs, fulls = [], [], []
    for a, layer, kind in entries:
        _, r, n = arrays[a].shape
        tm = r // tiles
        in_specs.append(pl.BlockSpec((None, tm, n), lambda i, s, layer=layer: (layer, i, 0)))
        if kind == "col":
            fulls.append((r, 4 * n))
            out_specs.append(pl.BlockSpec((tm, n), lambda i, s: (i, s[1])))
        else:
            fulls.append((4 * r, n))
            out_specs.append(pl.BlockSpec((tm, n), lambda i, s: (s[1] * tiles + i, 0)))
    nt = len(entries)

    def body(s_ref, *refs):
        for w_ref, o_ref in zip(refs[:nt], refs[nt:]):
            o_ref[...] = w_ref[...].astype(BF16)

    return pl.pallas_call(
        body, grid_spec=pltpu.PrefetchScalarGridSpec(num_scalar_prefetch=1, grid=(tiles,), in_specs=in_specs,
                                                     out_specs=out_specs),
        out_shape=[_sds(f, BF16) for f in fulls], name="cast_place",
        compiler_params=_params(("parallel",)))(where, *[arrays[a] for a, _, _ in entries])


def gather_start(fulls, kinds, shard_shapes, after):
    nt = len(fulls)
    na = 0 if after is None else 1

    def body(*refs):
        full_refs = refs[:nt]
        send_sems, recv_sems, token = refs[nt + na], refs[nt + na + 1], refs[-1]
        x, y, c, others = _position()
        for t in range(nt):
            mine = _window(full_refs[t], kinds[t], 2 * x + y, c, shard_shapes[t])
            for j, (ox, oy) in enumerate(others):
                pltpu.make_async_remote_copy(
                    src_ref=mine, dst_ref=mine, send_sem=send_sems.at[3 * t + j], recv_sem=recv_sems.at[3 * t + j],
                    device_id=(ox, oy, c), device_id_type=MESH).start()
        token[...] = jnp.zeros_like(token)

    sems = pltpu.SemaphoreType.DMA((3 * nt,))
    out = pl.pallas_call(
        body, name="gather_start", in_specs=[HBM_SPEC] * nt + [ANY_SPEC] * na,
        out_specs=(SEM_SPEC, SEM_SPEC, *[HBM_SPEC] * nt, pl.BlockSpec(memory_space=pltpu.VMEM)),
        out_shape=(sems, sems, *[pltpu.HBM(f.shape, f.dtype) for f in fulls], _sds((8, 128), F32)),
        input_output_aliases={t: 2 + t for t in range(nt)}, compiler_params=_split_params(),
    )(*[_in_hbm(f) for f in fulls], *([] if after is None else [after]))
    return out[0], out[1], list(out[2:2 + nt]), out[-1]


def gather_wait(name, send_sems, recv_sems, fulls, kinds, shard_shapes, after, first):
    nt = len(fulls)

    def body(*refs):
        full_refs, send_ref, recv_ref = refs[:nt], refs[nt], refs[nt + 1]
        x, y, c, others = _position()
        for t in range(nt):
            mine = _window(full_refs[t], kinds[t], 2 * x + y, c, shard_shapes[t])
            for j, (ox, oy) in enumerate(others):
                cp = pltpu.make_async_remote_copy(
                    src_ref=mine, dst_ref=_window(full_refs[t], kinds[t], 2 * ox + oy, c, shard_shapes[t]),
                    send_sem=send_ref.at[3 * (first + t) + j], recv_sem=recv_ref.at[3 * (first + t) + j],
                    device_id=(ox, oy, c), device_id_type=MESH)
                cp.wait_send()
                cp.wait_recv()

    out = pl.pallas_call(
        body, name=name, in_specs=[HBM_SPEC] * nt + [SEM_SPEC, SEM_SPEC, HBM_SPEC], out_specs=[HBM_SPEC] * nt,
        out_shape=[pltpu.HBM(f.shape, f.dtype) for f in fulls], input_output_aliases={t: t for t in range(nt)},
        compiler_params=_split_params())(*fulls, send_sems, recv_sems, _in_hbm(after))
    return list(out)


def forward_halves(name, fulls, kinds, shard_shapes):
    nt = len(fulls)

    def body(*refs):
        out_refs = refs[nt:2 * nt]
        send_sems, recv_sems = refs[2 * nt:]
        x, y, c, others = _position()
        cps = []
        for t in range(nt):
            for j, (ox, oy) in enumerate(others):
                landed = _window(out_refs[t], kinds[t], 2 * ox + oy, c, shard_shapes[t])
                cp = pltpu.make_async_remote_copy(
                    src_ref=landed, dst_ref=landed, send_sem=send_sems.at[3 * t + j], recv_sem=recv_sems.at[3 * t + j],
                    device_id=(x, y, 1 - c), device_id_type=MESH)
                cp.start()
                cps.append(cp)
        for t in range(nt):
            for j, (ox, oy) in enumerate(others):
                got = _window(out_refs[t], kinds[t], 2 * ox + oy, 1 - c, shard_shapes[t])
                pltpu.make_async_remote_copy(
                    src_ref=got, dst_ref=got, send_sem=send_sems.at[3 * t + j], recv_sem=recv_sems.at[3 * t + j],
                    device_id=(x, y, 1 - c), device_id_type=MESH).wait_recv()
        for cp in cps:
            cp.wait_send()

    out = pl.pallas_call(
        body, in_specs=[ANY_SPEC] * nt, out_specs=[ANY_SPEC] * nt, out_shape=[_sds(f.shape, f.dtype) for f in fulls],
        input_output_aliases={t: t for t in range(nt)},
        scratch_shapes=[pltpu.SemaphoreType.DMA((3 * nt,)), pltpu.SemaphoreType.DMA((3 * nt,))],
        name=name, compiler_params=_params())(*fulls)
    return list(out)


def _piece(ref, kind, chip, shard_shape):
    r, n = shard_shape
    if kind == "col":
        return ref.at[:, pl.ds(pl.multiple_of(chip * n, 128), n)]
    return ref.at[pl.ds(pl.multiple_of(chip * r, 16), r), :]


def _piece_shape(kind, shard_shape):
    r, n = shard_shape
    return (r // 2, n) if kind == "col" else (r, n // 2)


def exchange_start(name, parts, kinds, shard_shapes):
    nt = len(parts)
    lands = [lax.empty((3,) + _piece_shape(kinds[t], shard_shapes[t]), BF16) for t in range(nt)]

    def body(*refs):
        part_refs, land_refs = refs[:nt], refs[nt:2 * nt]
        send_sems, recv_sems, token = refs[2 * nt], refs[2 * nt + 1], refs[-1]
        x, y, c, others = _position()
        for t in range(nt):
            for j, (ox, oy) in enumerate(others):
                pltpu.make_async_remote_copy(
                    src_ref=_piece(part_refs[t], kinds[t], 2 * ox + oy, shard_shapes[t]), dst_ref=land_refs[t].at[j],
                    send_sem=send_sems.at[3 * t + j], recv_sem=recv_sems.at[3 * t + j],
                    device_id=(ox, oy, c), device_id_type=MESH).start()
        token[...] = jnp.zeros_like(token)

    sems = pltpu.SemaphoreType.DMA((3 * nt,))
    both = list(parts) + lands
    out = pl.pallas_call(
        body, name=name, in_specs=[HBM_SPEC] * (2 * nt),
        out_specs=(SEM_SPEC, SEM_SPEC, *[HBM_SPEC] * (2 * nt), pl.BlockSpec(memory_space=pltpu.VMEM)),
        out_shape=(sems, sems, *[pltpu.HBM(a.shape, a.dtype) for a in both], _sds((8, 128), F32)),
        input_output_aliases={t: 2 + t for t in range(2 * nt)}, compiler_params=_split_params(),
    )(*[_in_hbm(a) for a in both])
    return out[0], out[1], list(out[2:2 + nt]), list(out[2 + nt:2 + 2 * nt]), out[-1]


def exchange_wait(name, send_sems, recv_sems, parts, lands, kinds, shard_shapes, after):
    nt = len(parts)

    def body(*refs):
        part_refs, land_refs = refs[:nt], refs[nt:2 * nt]
        send_ref, recv_ref = refs[2 * nt], refs[2 * nt + 1]
        x, y, c, others = _position()
        for t in range(nt):
            for j, (ox, oy) in enumerate(others):
                cp = pltpu.make_async_remote_copy(
                    src_ref=_piece(part_refs[t], kinds[t], 2 * ox + oy, shard_shapes[t]), dst_ref=land_refs[t].at[j],
                    send_sem=send_ref.at[3 * t + j], recv_sem=recv_ref.at[3 * t + j],
                    device_id=(ox, oy, c), device_id_type=MESH)
                cp.wait_send()
                cp.wait_recv()

    both = list(parts) + list(lands)
    out = pl.pallas_call(
        body, name=name, in_specs=[HBM_SPEC] * (2 * nt) + [SEM_SPEC, SEM_SPEC, HBM_SPEC], out_specs=[HBM_SPEC] * (2 * nt),
        out_shape=[pltpu.HBM(a.shape, a.dtype) for a in both], input_output_aliases={t: t for t in range(2 * nt)},
        compiler_params=_split_params())(*both, send_sems, recv_sems, _in_hbm(after))
    return list(out[:nt]), list(out[nt:])


def all_reduce_small(name, bufs):
    n = len(bufs)
    halves = [b.shape[0] // 2 for b in bufs]

    def body(*refs):
        in_refs, out_refs, lands = refs[:n], refs[n:2 * n], refs[2 * n:3 * n]
        send_sems, recv_sems = refs[3 * n:]
        x, y, c, _ = _position()
        mine = [pl.ds(pl.multiple_of(c * h, 8), h) for h in halves]
        other = [pl.ds(pl.multiple_of((1 - c) * h, 8), h) for h in halves]
        for k in range(n):
            out_refs[k][mine[k], :] = in_refs[k][mine[k], :]
        for s, peer in enumerate([(x, y, 1 - c), (1 - x, y, c), (x, 1 - y, c)]):
            cps = []
            for k in range(n):
                src = in_refs[k].at[other[k]] if s == 0 else out_refs[k].at[mine[k]]
                cp = pltpu.make_async_remote_copy(
                    src_ref=src, dst_ref=lands[k].at[s], send_sem=send_sems.at[4 * k + s], recv_sem=recv_sems.at[4 * k + s],
                    device_id=peer, device_id_type=MESH)
                cp.start()
                cps.append(cp)
            for k, cp in enumerate(cps):
                cp.wait()
                out_refs[k][mine[k], :] = out_refs[k][mine[k], :] + lands[k][s]
        cps = []
        for k in range(n):
            cp = pltpu.make_async_remote_copy(
                src_ref=out_refs[k].at[mine[k]], dst_ref=out_refs[k].at[mine[k]], send_sem=send_sems.at[4 * k + 3],
                recv_sem=recv_sems.at[4 * k + 3], device_id=(x, y, 1 - c), device_id_type=MESH)
            cp.start()
            cps.append(cp)
        for cp in cps:
            cp.wait()

    vm = pl.BlockSpec(memory_space=pltpu.VMEM)
    out = pl.pallas_call(
        body, in_specs=[vm] * n, out_specs=[vm] * n, out_shape=[_sds(b.shape, F32) for b in bufs],
        scratch_shapes=[pltpu.VMEM((3, h, b.shape[1]), F32) for h, b in zip(halves, bufs)]
        + [pltpu.SemaphoreType.DMA((4 * n,)), pltpu.SemaphoreType.DMA((4 * n,))],
        name=name, compiler_params=_params())(*bufs)
    return list(out)


def _local_step(x, target, small, need, emit_swap, emit_exchange):
    d = D_MODEL
    full = {}

    def after_token(vec, token):
        return vec if token is None else vec + token[0:1, 0:1]

    def token_rows(token, width):
        return [] if token is None else [after_token(jnp.zeros((1, width), F32), token)]

    def plus(acc, rows):
        return acc + rows[0] if rows else acc

    rb16, rbt16, rc16, rct16, lr_t, li_t = small["s5_operands"]
    ge, y2, cs = s5_fwd(x, small["norm_mix0"], small["s5_d"], rb16, rc16, lr_t, li_t)
    full.update(need("glu", ge))

    def norm_rows(h, gains):
        xh, _ = _rms_hat(h)
        return [xh * g for g in gains]

    def glu_epilogue(accs, e, r):
        v, gt = accs[0] + r[0], accs[1] + r[1]
        h = e[0] + v * jax.nn.sigmoid(gt)
        return [h, v, gt] + norm_rows(h, r[2:])

    h1, val, gate, n1 = mm_nn(
        "glu", ge, full["w_glu"], [0, d], d, glu_epilogue, [F32, F32, F32, BF16], extras=[x],
        rowvecs=[(small["s5_b_glu"], 0), (small["s5_b_glu"], d), (small["norm_mlp0"], 0)], tm=512, tn=d)

    def mlp_fwd(tag, h, n, w_in, get_w_out, next_gains, head=None):
        def in_epilogue(accs, e, rv):
            pos = jnp.maximum(accs[0], 0.0)
            return [pos * pos, 2.0 * pos]

        r, slope = mm_nn("mlp_in" + tag, n, w_in, [0], w_in.shape[1], in_epilogue, [BF16, BF16], tm=2048)
        w_out = get_w_out(r)

        def epilogue(accs, e, rv):
            h_out = e[0] + accs[0]
            return [h_out] + norm_rows(h_out, rv)

        if head is not None:
            return head(r, w_out, h), (n, r, slope)
        outs = mm_nn("mlp_out" + tag, r, w_out, [0], d, epilogue, [F32] + [BF16] * len(next_gains), extras=[h],
                     rowvecs=[(g, 0) for g in next_gains], tm=512, tn=d)
        return outs[0], outs[1:], (n, r, slope)

    full.update(need("mlp_in0", h1))

    def w_out0(after):
        full.update(need("mlp_out0", after))
        return full["w_out0"]

    h2, (nkv, n2), mlp0 = mlp_fwd("0", h1, n1, full["w_in0"], w_out0, [small["norm_kv"], small["norm_mix1"]])

    full.update(need("attn", h2))
    kvw = 2 * N_KV * HEAD_DIM
    (kv,) = mm_nn("kv_proj", nkv, full["w_kv"], [0], kvw, lambda accs, e, r: [accs[0] + r[0]], [BF16],
                  rowvecs=[(small["b_kv"], 0)], tm=2048)
    (q,) = mm_nn("q_proj", n2, full["w_q"], [0], d, lambda accs, e, r: [accs[0] + r[0]], [BF16],
                 rowvecs=[(small["b_q"], 0)], tm=2048)
    sinks = small["sinks"].reshape(N_Q)
    o = attn_fwd(q, kv, sinks)
    def o_epilogue(accs, e, r):
        h_out = e[0] + accs[0] + r[0]
        return [h_out] + norm_rows(h_out, r[1:])

    h3, n3 = mm_nn("o_proj", o, full["w_o"], [0], d, o_epilogue, [F32, BF16], extras=[h2],
                   rowvecs=[(small["b_o"], 0), (small["norm_mlp1"], 0)], tm=512, tn=d)
    full.update(need("mlp1", h3))

    def loss_head(r, w_out, h):
        def epilogue(accs, e, rv):
            xh, rr = _rms_hat(e[0] + accs[0])
            err = xh * rv[0] - e[1]
            dy = err * (1.0 / d)
            dxh = dy * rv[0]
            dx = rr * (dxh - xh * jnp.mean(dxh * xh, axis=-1, keepdims=True))
            loss = jnp.full((1, d), 0.5 * jnp.sum(jnp.mean(err * err, axis=-1, keepdims=True)), F32)
            return [dx, dx, loss, jnp.sum(dy * xh, axis=0, keepdims=True)]

        return mm_nn("mlp_out1", r, w_out, [0], d, epilogue, [F32, BF16], extras=[h, target],
                     rowvecs=[(small["norm_final"], 0)], n_sums=2, tm=512, tn=d)

    (dh, dhb, loss_tile, dg_final), mlp1 = mlp_fwd("1", h3, n3, full["w_in1"], lambda after: full["w_out1"], [], head=loss_head)

    grads_small, grads_full = {"norm_final": dg_final}, {}
    ident = lambda acc, e, r: [plus(acc, r)]
    layer1 = ["w_out1", "w_in1", "w_o", "w_q", "w_kv"]
    layer0 = ["w_out0", "w_in0", "w_glu"]

    def norm_bwd_rows(x_rows, res, dys, gains):
        xh, r = _rms_hat(x_rows)
        dxh = sum(dy * g for dy, g in zip(dys, gains))
        dx = r * (dxh - xh * jnp.mean(dxh * xh, axis=-1, keepdims=True)) + res
        return dx, [jnp.sum(dy * xh, axis=0, keepdims=True) for dy in dys]

    def mlp_bwd(tag, dh, dhb, h_in, gain, w_in, w_out, saved, token=None):
        n, r, slope = saved
        grads_full["w_out" + tag] = mm_tn("dw_out" + tag, r, dhb, tn=1024)
        (da,) = mm_nt("mlp_da" + tag, dhb, w_out, lambda acc, e, rv: [plus(acc * e[0].astype(F32), rv)], [BF16],
                      extras=[slope], rowvecs=token_rows(token, w_out.shape[0]), tm=2048)
        grads_full["w_in" + tag] = mm_tn("dw_in" + tag, n, da, tn=1024)

        def epilogue(acc, e, rv):
            dx, dgs = norm_bwd_rows(e[0], e[1], [acc], rv)
            return [dx, dx, jnp.sum(dx, axis=0, keepdims=True)] + dgs

        dx, dxb, colsum, dg = mm_nt("mlp_dn" + tag, da, w_in, epilogue, [F32, BF16], extras=[h_in, dh], rowvecs=[gain],
                                    n_sums=2, tm=512, tk=d)
        grads_small["norm_mlp" + tag] = dg
        return dx, dxb, colsum

    dh3, dh3b, colsum3 = mlp_bwd("1", dh, dhb, h3, small["norm_mlp1"], full["w_in1"], full["w_out1"], mlp1)
    grads_small["b_o"] = colsum3
    grads_full["w_o"] = mm_tn("dw_o", o, dh3b, tn=1024)
    (do,) = mm_nt("attn_do", dh3b, full["w_o"], ident, [BF16], tm=2048)
    dq, dbq, dprev, dcur, dsink = attn_bwd(q, kv, do, sinks)
    dkv, dbkv = kv_combine(dprev, dcur)
    grads_small["b_q"], grads_small["b_kv"], grads_small["sinks"] = dbq, dbkv, dsink
    grads_full["w_q"] = mm_tn("dw_q", n2, dq, tn=1024)
    grads_full["w_kv"] = mm_tn("dw_kv", nkv, dkv, tk=1024)
    token = emit_swap("layer1", {n: grads_full[n] for n in layer1})
    (dnkv,) = mm_nt("kv_dn", dkv, full["w_kv"], ident, [F32], rowvecs=token_rows(token, d), tm=2048, tk=1024)

    def attn_dn_epilogue(acc, e, rv):
        dx, dgs = norm_bwd_rows(e[0], e[1], [acc, e[2]], rv)
        return [dx, dx] + dgs

    dh2, dh2b, dg_mix1, dg_kv = mm_nt("attn_dn", dq, full["w_q"], attn_dn_epilogue, [F32, BF16], extras=[h2, dh3, dnkv],
                                      rowvecs=[small["norm_mix1"], small["norm_kv"]], n_sums=2, tm=512, tk=d)
    grads_small["norm_mix1"], grads_small["norm_kv"] = dg_mix1, dg_kv
    token = emit_exchange("layer1", dh2b)
    dh1, _, _ = mlp_bwd("0", dh2, dh2b, h1, small["norm_mlp0"], full["w_in0"], full["w_out0"], mlp0, token)

    dz, db_glu = glu_bwd(dh1, val, gate)
    grads_small["s5_b_glu"] = db_glu
    grads_full["w_glu"] = mm_tn("dw_glu", ge, dz, tn=1024)
    token = emit_swap("layer0", {n: grads_full[n] for n in layer0})
    (dy2,) = mm_nt("glu_dy", dz, full["w_glu"], lambda acc, e, rv: [plus(acc, rv) * _gelu_grad(e[0])], [F32], extras=[y2],
                   rowvecs=token_rows(token, d), tm=1024, tk=1024)
    token = emit_exchange("layer0", dy2)
    grad_x, dd, drb, drc, dlr, dli, dg_mix0 = s5_bwd(x, small["norm_mix0"], dy2, dh1, after_token(small["s5_d"], token), cs,
                                                     rb16, rbt16, rct16, lr_t, li_t)
    grads_small["s5_d"] = dd
    grads_small["s5_mats"] = (drb, drc, dlr, dli)
    grads_small["norm_mix0"] = dg_mix0
    return loss_tile, grad_x, grads_small


SMALL_NAMES = ["norm_mix", "norm_mlp", "norm_kv", "norm_final", "s5_a_re", "s5_a_im", "s5_log_dt", "s5_b_re", "s5_b_im",
               "s5_c_re", "s5_c_im", "s5_d", "s5_b_glu", "b_kv", "b_q", "sinks", "b_o"]
BIG_NAMES = ["s5_w_glu", "w_kv", "w_q", "w_o", "w_mlp_in", "w_mlp_out"]
WEIGHT_ORDER = ["norm_mix", "norm_mlp", "norm_kv", "norm_final", "s5_a_re", "s5_a_im", "s5_log_dt", "s5_b_re", "s5_b_im",
                "s5_c_re", "s5_c_im", "s5_d", "s5_w_glu", "s5_b_glu", "w_kv", "b_kv", "w_q", "b_q", "sinks", "w_o", "b_o",
                "w_mlp_in", "w_mlp_out"]


def kernel(x, norm_mix, norm_mlp, norm_kv, norm_final, s5_a_re, s5_a_im, s5_log_dt, s5_b_re, s5_b_im, s5_c_re, s5_c_im, s5_d, s5_w_glu, s5_b_glu, w_kv, b_kv, w_q, b_q, sinks, w_o, b_o, w_mlp_in, w_mlp_out, loss_target, m_norm_mix, m_norm_mlp, m_norm_kv, m_norm_final, m_s5_a_re, m_s5_a_im, m_s5_log_dt, m_s5_b_re, m_s5_b_im, m_s5_c_re, m_s5_c_im, m_s5_d, m_s5_w_glu, m_s5_b_glu, m_w_kv, m_b_kv, m_w_q, m_b_q, m_sinks, m_w_o, m_b_o, m_w_mlp_in, m_w_mlp_out, v_norm_mix, v_norm_mlp, v_norm_kv, v_norm_final, v_s5_a_re, v_s5_a_im, v_s5_log_dt, v_s5_b_re, v_s5_b_im, v_s5_c_re, v_s5_c_im, v_s5_d, v_s5_w_glu, v_s5_b_glu, v_w_kv, v_b_kv, v_w_q, v_b_q, v_sinks, v_w_o, v_b_o, v_w_mlp_in, v_w_mlp_out):
    env = dict(locals())
    w = {n: env[n] for n in WEIGHT_ORDER}
    mom = {n: env["m_" + n] for n in WEIGHT_ORDER}
    var = {n: env["v_" + n] for n in WEIGHT_ORDER}
    d = D_MODEL
    xi, yi, ci = lax.axis_index("x"), lax.axis_index("y"), lax.axis_index("c")
    chip = 2 * xi + yi
    where = jnp.stack([ci, chip]).astype(jnp.int32)

    dsh, bsh = s5_d.shape[1], s5_b_glu.shape[1]
    placed = jnp.concatenate([
        lax.dynamic_update_slice(jnp.zeros((4 * dsh,), F32), s5_d[0], (chip * dsh,)),
        lax.dynamic_update_slice(jnp.zeros((4 * bsh,), F32), s5_b_glu[0], (chip * bsh,))])
    placed = jnp.pad(placed, (0, (-placed.shape[0]) % 2048))
    placed = jnp.where(ci == 0, placed, 0.0).reshape(-1, 128)
    (gathered_rows,) = all_reduce_small("gather_vectors", [placed])
    gathered = gathered_rows.reshape(-1)
    d_full, bglu_full = gathered[:4 * dsh].reshape(1, -1), gathered[4 * dsh:].reshape(1, -1)

    big = [s5_w_glu, w_kv[None], w_q, w_o, w_mlp_in, w_mlp_out]
    entries = [(0, 0, "col"), (1, 0, "row"), (2, 0, "row"), (3, 0, "row"), (4, 0, "col"), (4, 1, "col"),
               (5, 0, "row"), (5, 1, "row")]
    names = ["w_glu", "w_kv", "w_q", "w_o", "w_in0", "w_in1", "w_out0", "w_out1"]
    kinds = dict(zip(names, [k for _, _, k in entries]))
    shard_shapes = dict(zip(names, [tuple(big[a].shape[1:]) for a, _, _ in entries]))

    placed_w = dict(zip(names, cast_place(big, entries, where)))
    gather_groups = {"glu": ["w_glu"], "mlp_in0": ["w_in0"], "mlp_out0": ["w_out0"], "attn": ["w_kv", "w_q", "w_o"],
                     "mlp1": ["w_in1", "w_out1"]}
    order = [n for members in gather_groups.values() for n in members]
    send, recv, thru, token = gather_start([placed_w[n] for n in order], [kinds[n] for n in order],
                                           [shard_shapes[n] for n in order], gathered_rows)
    started = dict(zip(order, thru))

    def need(group, after):
        members = gather_groups[group]
        ks, shapes = [kinds[n] for n in members], [shard_shapes[n] for n in members]
        landed = gather_wait("gather_wait_" + group, send, recv, [started[n] for n in members], ks, shapes, after,
                             order.index(members[0]))
        return dict(zip(members, forward_halves("forward_halves_" + group, landed, ks, shapes)))

    swapping, exchanging = {}, {}

    def emit_swap(group, partial):
        members = list(partial)
        send, recv, mine, lands, tok = swap_start("swap_start_" + group, [partial[n] for n in members],
                                                  [kinds[n] for n in members])
        swapping[group] = (members, send, recv, mine, lands)
        return tok

    def emit_exchange(group, after):
        members, send, recv, mine, lands = swapping[group]
        ks, shapes = [kinds[n] for n in members], [shard_shapes[n] for n in members]
        mine, landed = swap_wait("swap_wait_" + group, send, recv, mine, lands, ks, after)
        sums = add_halves("add_halves_" + group, mine, landed, ks, where)
        send, recv, parts, lands, tok = exchange_start("exchange_start_" + group, sums, ks, shapes)
        exchanging[group] = (members, send, recv, parts, lands)
        return tok

    s5_args = (s5_a_re[0], s5_a_im[0], s5_log_dt[0], s5_b_re[0], s5_b_im[0])
    small = {
        "norm_mix0": norm_mix[0:1] + token[0:1, 0:1], "norm_mix1": norm_mix[1:2], "norm_mlp0": norm_mlp[0:1], "norm_mlp1": norm_mlp[1:2],
        "norm_kv": norm_kv.reshape(1, d), "norm_final": norm_final.reshape(1, d), "s5_operands": s5_prep(*s5_args, s5_c_re[0], s5_c_im[0]),
        "s5_d": d_full, "s5_b_glu": bglu_full,
        "b_kv": b_kv.reshape(1, -1), "b_q": b_q, "sinks": sinks, "b_o": b_o,
    }
    loss_row, grad_x, gs = _local_step(x[0], loss_target[0], small, need, emit_swap, emit_exchange)

    mats = s5_compact(*gs["s5_mats"])
    rows = [gs["norm_mix0"], gs["norm_mix1"], gs["norm_mlp0"], gs["norm_mlp1"], gs["norm_kv"], gs["norm_final"], gs["s5_d"],
            gs["b_q"], gs["b_o"], gs["s5_b_glu"], gs["b_kv"], gs["sinks"], loss_row, jnp.zeros((2, d), F32)]
    vecs, mats = all_reduce_small("reduce_small", [jnp.concatenate(rows, axis=0), mats])
    grads = split_vectors(where, vecs, dsh, bsh)
    loss = grads.pop("loss")[0, 0]
    g_are, g_aim, g_dt, g_bre, g_bim, dc_re, dc_im = s5_param_bwd(mats, *s5_args)
    grads.update({"s5_a_re": g_are[None], "s5_a_im": g_aim[None], "s5_log_dt": g_dt[None], "s5_b_re": g_bre[None],
                  "s5_b_im": g_bim[None], "s5_c_re": dc_re[None], "s5_c_im": dc_im[None]})

    reduced = [None] * len(big)
    where_of = dict(zip(names, entries))
    for group, after in (("layer1", grad_x), ("layer0", mats)):
        members, send, recv, parts, lands = exchanging[group]
        ks, shapes = [kinds[n] for n in members], [shard_shapes[n] for n in members]
        parts, lands = exchange_wait("exchange_wait_" + group, send, recv, parts, lands, ks, shapes, after)
        targets = [where_of[n][0] for n in members]
        sums = sum_shards("sum_shards_" + group, parts, lands, ks, shapes, where, [where_of[n][1] for n in members],
                          [big[a].shape[0] for a in targets], [reduced[a] for a in targets])
        for a, arr in zip(targets, sums):
            reduced[a] = arr
    reduced = share_halves(reduced, entries)
    for n, g in zip(BIG_NAMES, reduced):
        grads[n] = g.reshape(w[n].shape)

    delta, new_m, new_v = {}, {}, {}
    for n in BIG_NAMES:
        flat = lambda a: a.reshape(-1, a.shape[-1])
        go, dl, nm, nv = adamw("adamw_" + n, flat(w[n]), flat(grads[n]), flat(mom[n]), flat(var[n]))
        grads[n], delta[n], new_m[n], new_v[n] = (t.reshape(w[n].shape) for t in (go, dl, nm, nv))

    def view(n, a):
        return a.reshape(1, -1) if a.ndim == 1 else jnp.swapaxes(a, -1, -2) if n in ("s5_b_re", "s5_b_im") else a

    sw, sg, sm, sv = ([view(n, t[n]) for n in SMALL_NAMES] for t in (w, grads, mom, var))
    for n, a, b, c_ in zip(SMALL_NAMES, *adamw_native("adamw_small", sw, sg, sm, sv)):
        delta[n], new_m[n], new_v[n] = (view(n, t) if t.ndim == 4 else t for t in (a, b, c_))

    out = [loss.reshape(()), grad_x[None]]
    for table in (grads, delta, new_m, new_v):
        out += [table[n].reshape(w[n].shape) for n in WEIGHT_ORDER]
    return tuple(out)
```

```python
import math

import jax
import jax.numpy as jnp
from jax import lax
from jax.experimental import pallas as pl
from jax.experimental.pallas import tpu as pltpu

F32 = jnp.float32
BF16 = jnp.bfloat16

D_MODEL = 1024
S5_GROUPS = 64
S5_GROUP = 16
S5_STATE = 64
N_KV = 4
N_Q = 16
HEAD_DIM = 64
BLOCK = 128
NORM_EPS = 1e-5
LAMBDA_RE_MAX = -1e-4
ADAM_LR, ADAM_B1, ADAM_B2, ADAM_EPS, ADAM_WD, ADAM_STEP = 0.001, 0.9, 0.999, 1e-08, 0.01, 10

VMEM_LIMIT_BYTES = 56 * 1024 * 1024
S5_CHUNK = 256
S5_BLOCKS = 4
MESH = pl.DeviceIdType.MESH


def _params(sem=None):
    return pltpu.CompilerParams(dimension_semantics=sem, vmem_limit_bytes=VMEM_LIMIT_BYTES)


def _sds(shape, dtype):
    return jax.ShapeDtypeStruct(shape, dtype)


def _rms_hat(xv):
    r = lax.rsqrt(jnp.mean(xv * xv, axis=-1, keepdims=True) + NORM_EPS)
    return xv * r, r


def mm_nn(name, a, w, col_offsets, n_out, epilogue, out_dtypes, extras=(), rowvecs=(), n_sums=0, tm=1024, tn=512):
    m, k = a.shape
    tm, tn = min(tm, m), min(tn, n_out)
    nw, ne, nr, no = len(col_offsets), len(extras), len(rowvecs), len(out_dtypes)

    def body(a_ref, *refs):
        w_refs, e_refs, r_refs = refs[:nw], refs[nw:nw + ne], refs[nw + ne:nw + ne + nr]
        o_refs, s_refs = refs[nw + ne + nr:nw + ne + nr + no], refs[nw + ne + nr + no:]
        av = a_ref[...]
        accs = [jnp.dot(av, w_ref[...], preferred_element_type=F32) for w_ref in w_refs]
        outs = epilogue(accs, [e[...] for e in e_refs], [r[...] for r in r_refs])
        for o_ref, o in zip(o_refs, outs[:no]):
            o_ref[...] = o.astype(o_ref.dtype)
        if n_sums:
            @pl.when(pl.program_id(1) == 0)
            def _():
                for s_ref in s_refs:
                    s_ref[...] = jnp.zeros_like(s_ref)

            for s_ref, val in zip(s_refs, outs[no:]):
                s_ref[...] += val

    def wspec(off):
        return pl.BlockSpec((k, tn), lambda j, i, off=off: (0, off // tn + j))

    def rspec(off):
        return pl.BlockSpec((1, tn), lambda j, i, off=off: (0, off // tn + j))

    tile = pl.BlockSpec((tm, tn), lambda j, i: (i, j))
    in_specs = ([pl.BlockSpec((tm, k), lambda j, i: (i, 0))] + [wspec(o) for o in col_offsets]
                + [tile] * ne + [rspec(o) for _, o in rowvecs])
    sem = ("parallel", "arbitrary") if n_sums else ("parallel", "parallel")
    return pl.pallas_call(
        body, grid=(n_out // tn, m // tm), in_specs=in_specs,
        out_specs=[tile] * no + [pl.BlockSpec((1, tn), lambda j, i: (0, j))] * n_sums,
        out_shape=[_sds((m, n_out), dt) for dt in out_dtypes] + [_sds((1, n_out), F32)] * n_sums, name=name,
        compiler_params=_params(sem))(a, *([w] * nw), *extras, *[r for r, _ in rowvecs])


def mm_nt(name, g, w, epilogue, out_dtypes, extras=(), rowvecs=(), n_sums=0, tm=512, tk=512):
    m, n = g.shape
    k = w.shape[0]
    tm, tk = min(tm, m), min(tk, k)
    ne, nr, no = len(extras), len(rowvecs), len(out_dtypes)

    def body(g_ref, w_ref, *refs):
        e_refs, r_refs, o_refs, s_refs = refs[:ne], refs[ne:ne + nr], refs[ne + nr:ne + nr + no], refs[ne + nr + no:]
        acc = lax.dot_general(g_ref[...], w_ref[...], (((1,), (1,)), ((), ())), preferred_element_type=F32)
        outs = epilogue(acc, [e[...] for e in e_refs], [r[...] for r in r_refs])
        for o_ref, o in zip(o_refs, outs[:no]):
            o_ref[...] = o.astype(o_ref.dtype)
        if n_sums:
            @pl.when(pl.program_id(0) == 0)
            def _():
                for s_ref in s_refs:
                    s_ref[...] = jnp.zeros_like(s_ref)

            for s_ref, val in zip(s_refs, outs[no:]):
                s_ref[...] += val

    tile = pl.BlockSpec((tm, tk), lambda i, j: (i, j))
    vec = pl.BlockSpec((1, tk), lambda i, j: (0, j))
    sem = ("arbitrary", "parallel") if n_sums else ("parallel", "parallel")
    return pl.pallas_call(
        body, grid=(m // tm, k // tk),
        in_specs=[pl.BlockSpec((tm, n), lambda i, j: (i, 0)), pl.BlockSpec((tk, n), lambda i, j: (j, 0))]
        + [tile] * ne + [vec] * nr,
        out_specs=[tile] * no + [vec] * n_sums,
        out_shape=[_sds((m, k), dt) for dt in out_dtypes] + [_sds((1, k), F32)] * n_sums, name=name,
        compiler_params=_params(sem))(g, w, *extras, *rowvecs)


def mm_tn(name, a, g, tk=512, tn=512):
    m, k = a.shape
    n = g.shape[1]
    tk, tn = min(tk, k), min(tn, n)

    def body(a_ref, g_ref, o_ref):
        acc = lax.dot_general(a_ref[...], g_ref[...], (((0,), (0,)), ((), ())), preferred_element_type=F32)
        o_ref[...] = acc.astype(o_ref.dtype)

    return pl.pallas_call(
        body, grid=(k // tk, n // tn),
        in_specs=[pl.BlockSpec((m, tk), lambda i, j: (0, i)), pl.BlockSpec((m, tn), lambda i, j: (0, j))],
        out_specs=pl.BlockSpec((tk, tn), lambda i, j: (i, j)), out_shape=_sds((k, n), BF16), name=name,
        compiler_params=_params(("parallel", "parallel")))(a, g)


def _row_mask(tc):
    row = lax.broadcasted_iota(jnp.int32, (8 * tc, 256), 0) % 8
    col = lax.broadcasted_iota(jnp.int32, (8 * tc, 256), 1) // 32
    return row == col


def _expand_rows(val, mask):
    tc, width = val.shape
    rep = jnp.broadcast_to(val[:, None, :], (tc, 8, width)).reshape(8 * tc, width)
    return jnp.where(mask, rep, 0.0).astype(BF16)


def _stage(ref, val):
    ref[0] = val[:, 0:128]
    ref[1] = val[:, 128:256]


def _gather_rows(src_ref, tc):
    halves = []
    for half in range(2):
        col = lax.broadcasted_iota(jnp.int32, (tc, 128), 1) // 32 + 4 * half
        out = jnp.zeros((tc, 128), F32)
        for s8 in range(4 * half, 4 * half + 4):
            out = jnp.where(col == s8, src_ref.at[half][pl.ds(s8, tc, stride=8), :], out)
        halves.append(out)
    return jnp.concatenate(halves, axis=1)


def _gelu(x):
    c = math.sqrt(2.0 / math.pi)
    return 0.5 * x * (1.0 + jnp.tanh(c * (x + 0.044715 * x * x * x)))


def _gelu_grad(x):
    c = math.sqrt(2.0 / math.pi)
    t = jnp.tanh(c * (x + 0.044715 * x * x * x))
    return 0.5 * (1.0 + t) + 0.5 * x * (1.0 - t * t) * c * (1.0 + 3.0 * 0.044715 * x * x)


def s5_fwd(x, gain, d_skip, rb, rc, lam_r, lam_i):
    n_rows = x.shape[0]
    tc = min(S5_CHUNK, n_rows)
    nc = n_rows // tc

    def body(x_ref, g_ref, d_ref, rb_ref, rc_ref, lr_ref, li_ref, ge_ref, y2_ref, cs_ref, bux, yrows, carry):
        i = pl.program_id(0)
        u = _rms_hat(x_ref[...])[0] * g_ref[...]

        @pl.when(i == 0)
        def _():
            carry[...] = jnp.zeros_like(carry)

        cs_ref[0] = carry[...]
        mask = _row_mask(tc)
        for blk in range(S5_BLOCKS):
            lhs = _expand_rows(u[:, blk * 256:(blk + 1) * 256], mask)
            bux[blk] = jnp.dot(lhs, rb_ref[blk], preferred_element_type=F32)
        lam = [(lr_ref[blk], li_ref[blk]) for blk in range(S5_BLOCKS)]

        def step(t, c):
            r0 = pl.multiple_of(t * 8, 8)
            new = []
            for blk in range(S5_BLOCKS):
                xr, xi = c[2 * blk], c[2 * blk + 1]
                lr, li = lam[blk]
                nr = lr * xr - li * xi + bux[blk, pl.ds(r0, 8), 0:128]
                ni = lr * xi + li * xr + bux[blk, pl.ds(r0, 8), 128:256]
                bux[blk, pl.ds(r0, 8), 0:128] = nr
                bux[blk, pl.ds(r0, 8), 128:256] = ni
                new += [nr, ni]
            return tuple(new)

        c0 = []
        for blk in range(S5_BLOCKS):
            c0 += [carry[blk, :, 0:128], carry[blk, :, 128:256]]
        cn = lax.fori_loop(0, tc, step, tuple(c0), unroll=4)
        for blk in range(S5_BLOCKS):
            carry[blk, :, 0:128] = cn[2 * blk]
            carry[blk, :, 128:256] = cn[2 * blk + 1]
        for blk in range(S5_BLOCKS):
            _stage(yrows, jnp.dot(bux[blk].astype(BF16), rc_ref[blk], preferred_element_type=F32))
            sl = slice(blk * 256, (blk + 1) * 256)
            y2 = _gather_rows(yrows, tc) + d_ref[:, sl] * u[:, sl]
            y2_ref[:, sl] = y2
            ge_ref[:, sl] = _gelu(y2).astype(BF16)

    row = pl.BlockSpec((tc, D_MODEL), lambda i: (i, 0))
    vec = pl.BlockSpec((1, D_MODEL), lambda i: (0, 0))
    mat = pl.BlockSpec((S5_BLOCKS, 256, 256), lambda i: (0, 0, 0))
    lamspec = pl.BlockSpec((S5_BLOCKS, 8, 128), lambda i: (0, 0, 0))
    return pl.pallas_call(
        body, grid=(nc,),
        in_specs=[row, vec, vec, mat, mat, lamspec, lamspec],
        out_specs=[row, row, pl.BlockSpec((1, S5_BLOCKS, 8, 256), lambda i: (i, 0, 0, 0))],
        out_shape=[_sds((n_rows, D_MODEL), BF16), _sds((n_rows, D_MODEL), F32), _sds((nc, S5_BLOCKS, 8, 256), F32)],
        scratch_shapes=[pltpu.VMEM((S5_BLOCKS, 8 * tc, 256), F32), pltpu.VMEM((2, 8 * tc, 128), F32),
                        pltpu.VMEM((S5_BLOCKS, 8, 256), F32)],
        name="s5_fwd", compiler_params=_params(("arbitrary",)))(x, gain, d_skip, rb, rc, lam_r, lam_i)


def s5_bwd(x, gain, dy2, res, d_skip, cs, rb, rbt, rct, lam_r, lam_i):
    n_rows = x.shape[0]
    tc = min(S5_CHUNK, n_rows)
    nc = n_rows // tc

    def body(x_ref, g_ref, dy_ref, res_ref, d_ref, cs_ref, rb_ref, rbt_ref, rct_ref, lr_ref, li_ref,
             dx_ref, dd_ref, drb_ref, drc_ref, dlr_ref, dli_ref, dg_ref, tmp, du, lhsu, lhsd, xs, adj, acarry):
        i = pl.program_id(0)
        u = _rms_hat(x_ref[...])[0] * g_ref[...]

        @pl.when(i == 0)
        def _():
            acarry[...] = jnp.zeros_like(acarry)
            dd_ref[...] = jnp.zeros_like(dd_ref)
            drb_ref[...] = jnp.zeros_like(drb_ref)
            drc_ref[...] = jnp.zeros_like(drc_ref)
            dlr_ref[...] = jnp.zeros_like(dlr_ref)
            dli_ref[...] = jnp.zeros_like(dli_ref)
            dg_ref[...] = jnp.zeros_like(dg_ref)

        dd_ref[...] += jnp.sum(dy_ref[...] * u, axis=0, keepdims=True)
        mask = _row_mask(tc)
        for blk in range(S5_BLOCKS):
            sl = slice(blk * 256, (blk + 1) * 256)
            lhsu[blk] = _expand_rows(u[:, sl], mask)
            xs[blk] = jnp.dot(lhsu[blk], rb_ref[blk], preferred_element_type=F32)
            lhsd[blk] = _expand_rows(dy_ref[:, sl], mask)
            adj[blk] = jnp.dot(lhsd[blk], rct_ref[blk], preferred_element_type=F32)
        lam = [(lr_ref[blk], li_ref[blk]) for blk in range(S5_BLOCKS)]

        def fstep(t, c):
            r0 = pl.multiple_of(t * 8, 8)
            new = []
            for blk in range(S5_BLOCKS):
                xr, xi = c[2 * blk], c[2 * blk + 1]
                lr, li = lam[blk]
                nr = lr * xr - li * xi + xs[blk, pl.ds(r0, 8), 0:128]
                ni = lr * xi + li * xr + xs[blk, pl.ds(r0, 8), 128:256]
                xs[blk, pl.ds(r0, 8), 0:128] = nr
                xs[blk, pl.ds(r0, 8), 128:256] = ni
                new += [nr, ni]
            return tuple(new)

        c0 = []
        for blk in range(S5_BLOCKS):
            c0 += [cs_ref[0, blk, :, 0:128], cs_ref[0, blk, :, 128:256]]
        lax.fori_loop(0, tc, fstep, tuple(c0), unroll=4)

        def bstep(k, c):
            t = tc - 1 - k
            r0 = pl.multiple_of(t * 8, 8)
            rp = pl.multiple_of(jnp.maximum(t - 1, 0) * 8, 8)
            first = t == 0
            new_a, new_g = [], []
            for blk in range(S5_BLOCKS):
                ar, ai = c[0][2 * blk], c[0][2 * blk + 1]
                glr, gli = c[1][2 * blk], c[1][2 * blk + 1]
                lr, li = lam[blk]
                nr = lr * ar + li * ai + adj[blk, pl.ds(r0, 8), 0:128]
                ni = lr * ai - li * ar + adj[blk, pl.ds(r0, 8), 128:256]
                adj[blk, pl.ds(r0, 8), 0:128] = nr
                adj[blk, pl.ds(r0, 8), 128:256] = ni
                pr = jnp.where(first, cs_ref[0, blk, :, 0:128], xs[blk, pl.ds(rp, 8), 0:128])
                pi = jnp.where(first, cs_ref[0, blk, :, 128:256], xs[blk, pl.ds(rp, 8), 128:256])
                new_a += [nr, ni]
                new_g += [glr + nr * pr + ni * pi, gli + ni * pr - nr * pi]
            return tuple(new_a), tuple(new_g)

        a0, g0 = [], []
        for blk in range(S5_BLOCKS):
            a0 += [acarry[blk, :, 0:128], acarry[blk, :, 128:256]]
            g0 += [dlr_ref[blk], dli_ref[blk]]
        an, gn = lax.fori_loop(0, tc, bstep, (tuple(a0), tuple(g0)), unroll=2)
        for blk in range(S5_BLOCKS):
            acarry[blk, :, 0:128] = an[2 * blk]
            acarry[blk, :, 128:256] = an[2 * blk + 1]
            dlr_ref[blk] = gn[2 * blk]
            dli_ref[blk] = gn[2 * blk + 1]
        for blk in range(S5_BLOCKS):
            sl = slice(blk * 256, (blk + 1) * 256)
            ab = adj[blk].astype(BF16)
            _stage(tmp, jnp.dot(ab, rbt_ref[blk], preferred_element_type=F32))
            du[:, sl] = _gather_rows(tmp, tc) + d_ref[:, sl] * dy_ref[:, sl]
            drb_ref[blk] += lax.dot_general(lhsu[blk], ab, (((0,), (0,)), ((), ())), preferred_element_type=F32)
            drc_ref[blk] += lax.dot_general(lhsd[blk], xs[blk].astype(BF16), (((0,), (0,)), ((), ())),
                                            preferred_element_type=F32)
        xh, r = _rms_hat(x_ref[...])
        dg_ref[...] += jnp.sum(du[...] * xh, axis=0, keepdims=True)
        dxh = du[...] * g_ref[...]
        dx_ref[...] = r * (dxh - xh * jnp.mean(dxh * xh, axis=-1, keepdims=True)) + res_ref[...]

    rev = pl.BlockSpec((tc, D_MODEL), lambda i: (nc - 1 - i, 0))
    vec = pl.BlockSpec((1, D_MODEL), lambda i: (0, 0))
    mat = pl.BlockSpec((S5_BLOCKS, 256, 256), lambda i: (0, 0, 0))
    lamspec = pl.BlockSpec((S5_BLOCKS, 8, 128), lambda i: (0, 0, 0))
    big = pltpu.VMEM((S5_BLOCKS, 8 * tc, 256), F32)
    bigb = pltpu.VMEM((S5_BLOCKS, 8 * tc, 256), BF16)
    return pl.pallas_call(
        body, grid=(nc,),
        in_specs=[rev, vec, rev, rev, vec, pl.BlockSpec((1, S5_BLOCKS, 8, 256), lambda i: (nc - 1 - i, 0, 0, 0)),
                  mat, mat, mat, lamspec, lamspec],
        out_specs=[rev, vec, mat, mat, lamspec, lamspec, vec],
        out_shape=[_sds((n_rows, D_MODEL), F32), _sds((1, D_MODEL), F32), _sds((S5_BLOCKS, 256, 256), F32),
                   _sds((S5_BLOCKS, 256, 256), F32), _sds((S5_BLOCKS, 8, 128), F32), _sds((S5_BLOCKS, 8, 128), F32),
                   _sds((1, D_MODEL), F32)],
        scratch_shapes=[pltpu.VMEM((2, 8 * tc, 128), F32), pltpu.VMEM((tc, D_MODEL), F32), bigb, bigb, big, big,
                        pltpu.VMEM((S5_BLOCKS, 8, 256), F32)],
        name="s5_bwd", compiler_params=_params(("arbitrary",)))(
            x, gain, dy2, res, d_skip, cs, rb, rbt, rct, lam_r, lam_i)


def _s5_views(a_re, a_im, log_dt, b_re, b_im):
    return a_re[:, None, :], a_im[:, None, :], log_dt[:, None, None], jnp.swapaxes(b_re, 1, 2), jnp.swapaxes(b_im, 1, 2)


def _s5_factors(a_re, a_im, log_dt):
    lr, li, dt = jnp.minimum(a_re, LAMBDA_RE_MAX), a_im, jnp.exp(log_dt)
    mag, ang = jnp.exp(lr * dt), li * dt
    lbr, lbi = mag * jnp.cos(ang), mag * jnp.sin(ang)
    den = lr * lr + li * li
    fr, fi = ((lbr - 1.0) * lr + lbi * li) / den, (lbi * lr - (lbr - 1.0) * li) / den
    return lr, li, dt, lbr, lbi, fr, fi, den


def s5_prep(a_re, a_im, log_dt, b_re, b_im, c_re, c_im):
    def body(ar_ref, ai_ref, t_ref, br_ref, bi_ref, cr_ref, ci_ref, rb_ref, rbt_ref, rc_ref, rct_ref, lr_ref, li_ref):
        _, _, _, lbr, lbi, fr, fi, _ = _s5_factors(ar_ref[...], ai_ref[...], t_ref[...])
        lr_ref[...] = lbr
        li_ref[...] = lbi
        bre = fr * br_ref[...] - fi * bi_ref[...]
        bim = fr * bi_ref[...] + fi * br_ref[...]
        even = (lax.broadcasted_iota(jnp.int32, (256, S5_STATE), 0) // S5_GROUP) % 2 == 0

        def assemble(re, im):
            re, im = re.reshape(256, S5_STATE), im.reshape(256, S5_STATE)
            return jnp.concatenate([jnp.where(even, re, 0.0), jnp.where(even, 0.0, re), jnp.where(even, im, 0.0),
                                    jnp.where(even, 0.0, im)], axis=1)

        for blk in range(S5_BLOCKS):
            sl = slice(16 * blk, 16 * blk + 16)
            rb = assemble(bre[sl], bim[sl])
            rct = assemble(cr_ref[sl], -ci_ref[sl])
            rb_ref[blk] = rb.astype(BF16)
            rbt_ref[blk] = rb.T.astype(BF16)
            rct_ref[blk] = rct.astype(BF16)
            rc_ref[blk] = rct.T.astype(BF16)

    vm = pl.BlockSpec(memory_space=pltpu.VMEM)
    mat = _sds((S5_BLOCKS, 256, 256), BF16)
    lam = _sds((S5_GROUPS, 1, S5_STATE), F32)
    rb, rbt, rc, rct, lam_r, lam_i = pl.pallas_call(
        body, in_specs=[vm] * 7, out_specs=[vm] * 6, out_shape=[mat, mat, mat, mat, lam, lam], name="s5_prep",
        compiler_params=_params())(*_s5_views(a_re, a_im, log_dt, b_re, b_im), c_re, c_im)
    return rb, rbt, rc, rct, lam_r.reshape(S5_BLOCKS, 8, 128), lam_i.reshape(S5_BLOCKS, 8, 128)


def s5_param_bwd(mats, a_re, a_im, log_dt, b_re, b_im):
    def body(m_ref, glr_ref, gli_ref, ar_ref, ai_ref, t_ref, br_ref, bi_ref,
             dar_ref, dai_ref, dt_ref, dbr_ref, dbi_ref, dcr_ref, dci_ref):
        lr, li, dt, lbr, lbi, fr, fi, den = _s5_factors(ar_ref[...], ai_ref[...], t_ref[...])
        shape = (S5_GROUPS, S5_GROUP, S5_STATE)
        gbr, gbi = m_ref[0:1024, 0:64].reshape(shape), m_ref[0:1024, 64:128].reshape(shape)
        dcr_ref[...] = m_ref[1024:2048, 0:64].reshape(shape)
        dci_ref[...] = -m_ref[1024:2048, 64:128].reshape(shape)
        br, bi = br_ref[...], bi_ref[...]
        dbr_ref[...] = fr * gbr + fi * gbi
        dbi_ref[...] = fr * gbi - fi * gbr
        dfr = jnp.sum(gbr * br + gbi * bi, axis=1, keepdims=True)
        dfi = jnp.sum(gbi * br - gbr * bi, axis=1, keepdims=True)
        nr, ni = (dfr * lr - dfi * li) / den, (dfr * li + dfi * lr) / den
        qr, qi = (fr * lr + fi * li) / den, (fi * lr - fr * li) / den
        lam_r, lam_i = -(dfr * qr + dfi * qi), -(dfi * qr - dfr * qi)
        gr, gi = glr_ref[...] + nr, gli_ref[...] + ni
        zr, zi = gr * lbr + gi * lbi, gi * lbr - gr * lbi
        a = ar_ref[...]
        dar_ref[...] = (lam_r + zr * dt) * jnp.where(a < LAMBDA_RE_MAX, 1.0, jnp.where(a == LAMBDA_RE_MAX, 0.5, 0.0))
        dai_ref[...] = lam_i + zi * dt
        dt_ref[...] = jnp.sum(zr * lr + zi * li, axis=2, keepdims=True) * dt

    vm = pl.BlockSpec(memory_space=pltpu.VMEM)
    state = _sds((S5_GROUPS, 1, S5_STATE), F32)
    wide = _sds((S5_GROUPS, S5_GROUP, S5_STATE), F32)
    glr = mats[2048:2080].reshape(S5_GROUPS, 1, S5_STATE)
    gli = mats[2080:2112].reshape(S5_GROUPS, 1, S5_STATE)
    dar, dai, ddt, dbr, dbi, dcr, dci = pl.pallas_call(
        body, in_specs=[vm] * 8, out_specs=[vm] * 7,
        out_shape=[state, state, _sds((S5_GROUPS, 1, 1), F32), wide, wide, wide, wide], name="s5_param_bwd",
        compiler_params=_params())(mats, glr, gli, *_s5_views(a_re, a_im, log_dt, b_re, b_im))
    return (dar.reshape(S5_GROUPS, S5_STATE), dai.reshape(S5_GROUPS, S5_STATE), ddt.reshape(S5_GROUPS),
            jnp.swapaxes(dbr, 1, 2), jnp.swapaxes(dbi, 1, 2), dcr, dci)


def s5_compact(drb, drct, dlr, dli):
    def body(drb_ref, drct_ref, dlr_ref, dli_ref, o_ref):
        even = (lax.broadcasted_iota(jnp.int32, (256, 64), 0) // S5_GROUP) % 2 == 0
        for blk in range(S5_BLOCKS):
            for k, ref in enumerate((drb_ref, drct_ref)):
                m = ref[blk]
                re = jnp.where(even, m[:, 0:64], m[:, 64:128])
                im = jnp.where(even, m[:, 128:192], m[:, 192:256])
                o_ref[pl.ds(k * 1024 + blk * 256, 256), :] = jnp.concatenate([re, im], axis=1)
            o_ref[pl.ds(2048 + blk * 8, 8), :] = dlr_ref[blk]
            o_ref[pl.ds(2080 + blk * 8, 8), :] = dli_ref[blk]

    vm = pl.BlockSpec(memory_space=pltpu.VMEM)
    return pl.pallas_call(body, in_specs=[vm] * 4, out_specs=vm, out_shape=_sds((2112, 128), F32), name="s5_compact",
                          compiler_params=_params())(drb, drct, dlr, dli)


NEG = -1e30


GROUP = N_Q // N_KV


def _attn_masks(n):
    qi = lax.broadcasted_iota(jnp.int32, (GROUP * BLOCK, BLOCK), 0) % BLOCK
    kj = lax.broadcasted_iota(jnp.int32, (GROUP * BLOCK, BLOCK), 1)
    return jnp.logical_and(kj > qi, n > 0), kj <= qi


def _stack_heads(ref, kh):
    return jnp.concatenate([ref[:, (GROUP * kh + g) * HEAD_DIM:(GROUP * kh + g + 1) * HEAD_DIM] for g in range(GROUP)], axis=0)


def _unstack_heads(val):
    return jnp.concatenate([val[g * BLOCK:(g + 1) * BLOCK] for g in range(GROUP)], axis=1)


def _sink_column(sink_ref, kh):
    grp = lax.broadcasted_iota(jnp.int32, (GROUP * BLOCK, 1), 0) // BLOCK
    col = jnp.zeros((GROUP * BLOCK, 1), F32)
    for g in range(GROUP):
        col = jnp.where(grp == g, sink_ref[GROUP * kh + g], col)
    return col, grp


def _attn_exp(q4, kp, kc, sink, mask_p, mask_c):
    scale = 1.0 / math.sqrt(HEAD_DIM)
    nt = (((1,), (1,)), ((), ()))
    sp = jnp.where(mask_p, lax.dot_general(q4, kp, nt, preferred_element_type=F32) * scale, NEG)
    sc = jnp.where(mask_c, lax.dot_general(q4, kc, nt, preferred_element_type=F32) * scale, NEG)
    m = jnp.maximum(jnp.maximum(jnp.max(sp, axis=-1, keepdims=True), jnp.max(sc, axis=-1, keepdims=True)), sink)
    pp = jnp.exp(sp - m)
    pc = jnp.exp(sc - m)
    ps = jnp.exp(sink - m)
    inv = 1.0 / (jnp.sum(pp, axis=-1, keepdims=True) + jnp.sum(pc, axis=-1, keepdims=True) + ps)
    return pp, pc, ps, inv


def attn_fwd(q, kv, sinks):
    n_rows = q.shape[0]
    nb = n_rows // BLOCK

    def body(sink_ref, q_ref, kvp_ref, kvc_ref, o_ref):
        n = pl.program_id(0)
        mask_p, mask_c = _attn_masks(n)
        outs = []
        for kh in range(N_KV):
            ks, vs = slice(kh * HEAD_DIM, (kh + 1) * HEAD_DIM), slice((N_KV + kh) * HEAD_DIM, (N_KV + kh + 1) * HEAD_DIM)
            sink, _ = _sink_column(sink_ref, kh)
            pp, pc, _, inv = _attn_exp(_stack_heads(q_ref, kh), kvp_ref[:, ks], kvc_ref[:, ks], sink, mask_p, mask_c)
            o4 = (jnp.dot(pp.astype(BF16), kvp_ref[:, vs], preferred_element_type=F32)
                  + jnp.dot(pc.astype(BF16), kvc_ref[:, vs], preferred_element_type=F32)) * inv
            outs.append(_unstack_heads(o4))
        o_ref[...] = jnp.concatenate(outs, axis=1).astype(BF16)

    kvw = 2 * N_KV * HEAD_DIM
    return pl.pallas_call(
        body, grid=(nb,),
        in_specs=[pl.BlockSpec(memory_space=pltpu.SMEM), pl.BlockSpec((BLOCK, D_MODEL), lambda n: (n, 0)),
                  pl.BlockSpec((BLOCK, kvw), lambda n: (jnp.maximum(n - 1, 0), 0)), pl.BlockSpec((BLOCK, kvw), lambda n: (n, 0))],
        out_specs=pl.BlockSpec((BLOCK, D_MODEL), lambda n: (n, 0)), out_shape=_sds((n_rows, D_MODEL), BF16),
        name="attn_fwd", compiler_params=_params(("parallel",)))(sinks, q, kv, kv)


def attn_bwd(q, kv, do, sinks):
    n_rows = q.shape[0]
    nb = n_rows // BLOCK
    kvw = 2 * N_KV * HEAD_DIM
    tn = (((0,), (0,)), ((), ()))
    nt = (((1,), (1,)), ((), ()))
    scale = 1.0 / math.sqrt(HEAD_DIM)

    def body(sink_ref, q_ref, kvp_ref, kvc_ref, do_ref, dq_ref, dbq_ref, dprev_ref, dcur_ref, dsink_ref):
        n = pl.program_id(0)
        mask_p, mask_c = _attn_masks(n)
        lane = lax.broadcasted_iota(jnp.int32, (1, D_MODEL), 1)
        dqs, dsink = [], jnp.zeros((1, D_MODEL), F32)
        dkp, dkc, dvp, dvc = [], [], [], []
        for kh in range(N_KV):
            ks, vs = slice(kh * HEAD_DIM, (kh + 1) * HEAD_DIM), slice((N_KV + kh) * HEAD_DIM, (N_KV + kh + 1) * HEAD_DIM)
            q4, do4 = _stack_heads(q_ref, kh), _stack_heads(do_ref, kh)
            kp, kc, vp, vc = kvp_ref[:, ks], kvc_ref[:, ks], kvp_ref[:, vs], kvc_ref[:, vs]
            sink, grp = _sink_column(sink_ref, kh)
            pp, pc, ps, inv = _attn_exp(q4, kp, kc, sink, mask_p, mask_c)
            pp, pc = pp * inv, pc * inv
            dpp = lax.dot_general(do4, vp, nt, preferred_element_type=F32)
            dpc = lax.dot_general(do4, vc, nt, preferred_element_type=F32)
            delta = jnp.sum(pp * dpp, axis=-1, keepdims=True) + jnp.sum(pc * dpc, axis=-1, keepdims=True)
            dsp = (pp * (dpp - delta) * scale).astype(BF16)
            dsc = (pc * (dpc - delta) * scale).astype(BF16)
            dsk = ps * inv * delta
            for g in range(GROUP):
                dsink = dsink + jnp.where(lane == GROUP * kh + g, -jnp.sum(jnp.where(grp == g, dsk, 0.0)), 0.0)
            dqs.append(_unstack_heads(jnp.dot(dsp, kp, preferred_element_type=F32)
                                      + jnp.dot(dsc, kc, preferred_element_type=F32)))
            dkp.append(lax.dot_general(dsp, q4, tn, preferred_element_type=F32))
            dkc.append(lax.dot_general(dsc, q4, tn, preferred_element_type=F32))
            dvp.append(lax.dot_general(pp.astype(BF16), do4, tn, preferred_element_type=F32))
            dvc.append(lax.dot_general(pc.astype(BF16), do4, tn, preferred_element_type=F32))
        dq = jnp.concatenate(dqs, axis=1)
        dq_ref[...] = dq.astype(BF16)
        dprev_ref[0] = jnp.concatenate(dkp + dvp, axis=1)
        dcur_ref[0] = jnp.concatenate(dkc + dvc, axis=1)

        @pl.when(n == 0)
        def _():
            dbq_ref[...] = jnp.zeros_like(dbq_ref)
            dsink_ref[...] = jnp.zeros_like(dsink_ref)

        dbq_ref[...] += jnp.sum(dq, axis=0, keepdims=True)
        dsink_ref[...] += dsink

    blk = pl.BlockSpec((BLOCK, D_MODEL), lambda n: (n, 0))
    part = pl.BlockSpec((1, BLOCK, kvw), lambda n: (n, 0, 0))
    return pl.pallas_call(
        body, grid=(nb,),
        in_specs=[pl.BlockSpec(memory_space=pltpu.SMEM), blk,
                  pl.BlockSpec((BLOCK, kvw), lambda n: (jnp.maximum(n - 1, 0), 0)), pl.BlockSpec((BLOCK, kvw), lambda n: (n, 0)), blk],
        out_specs=[blk, pl.BlockSpec((1, D_MODEL), lambda n: (0, 0)), part, part, pl.BlockSpec((1, D_MODEL), lambda n: (0, 0))],
        out_shape=[_sds((n_rows, D_MODEL), BF16), _sds((1, D_MODEL), F32), _sds((nb, BLOCK, kvw), F32),
                   _sds((nb, BLOCK, kvw), F32), _sds((1, D_MODEL), F32)],
        name="attn_bwd", compiler_params=_params(("arbitrary",)))(sinks, q, kv, kv, do)


def kv_combine(dprev, dcur):
    nb, _, kvw = dprev.shape

    def body(dcur_ref, dprev_ref, dkv_ref, db_ref):
        total = jnp.zeros((1, kvw), F32)
        for m in range(nb):
            dkv = dcur_ref[m] + dprev_ref[m + 1] if m + 1 < nb else dcur_ref[m]
            dkv_ref[m * BLOCK:(m + 1) * BLOCK, :] = dkv.astype(BF16)
            total = total + jnp.sum(dkv, axis=0, keepdims=True)
        db_ref[...] = jnp.concatenate([total, jnp.zeros((1, D_MODEL - kvw), F32)], axis=1)

    vm = pl.BlockSpec(memory_space=pltpu.VMEM)
    return pl.pallas_call(body, in_specs=[vm, vm], out_specs=[vm, vm],
                          out_shape=[_sds((nb * BLOCK, kvw), BF16), _sds((1, D_MODEL), F32)], name="kv_combine",
                          compiler_params=_params())(dcur, dprev)


def glu_bwd(dout, val, gate, tm=256):
    n_rows, d = dout.shape

    def body(do_ref, v_ref, g_ref, dz_ref, db_ref):
        i = pl.program_id(0)
        sg = jax.nn.sigmoid(g_ref[...])
        dval = do_ref[...] * sg
        dgate = do_ref[...] * v_ref[...] * sg * (1.0 - sg)
        dz_ref[...] = jnp.concatenate([dval, dgate], axis=1).astype(BF16)

        @pl.when(i == 0)
        def _():
            db_ref[...] = jnp.zeros_like(db_ref)

        db_ref[0:1, :] += jnp.sum(dval, axis=0, keepdims=True)
        db_ref[1:2, :] += jnp.sum(dgate, axis=0, keepdims=True)

    row = pl.BlockSpec((tm, d), lambda i: (i, 0))
    return pl.pallas_call(
        body, grid=(n_rows // tm,), in_specs=[row, row, row],
        out_specs=[pl.BlockSpec((tm, 2 * d), lambda i: (i, 0)), pl.BlockSpec((2, d), lambda i: (0, 0))],
        out_shape=[_sds((n_rows, 2 * d), BF16), _sds((2, d), F32)],
        name="glu_bwd", compiler_params=_params(("arbitrary",)))(dout, val, gate)


def _adam_update(w, g, m, v):
    nm = ADAM_B1 * m + (1.0 - ADAM_B1) * g
    nv = ADAM_B2 * v + (1.0 - ADAM_B2) * (g * g)
    m_hat = nm / (1.0 - ADAM_B1 ** ADAM_STEP)
    v_hat = nv / (1.0 - ADAM_B2 ** ADAM_STEP)
    return -ADAM_LR * (m_hat / (jnp.sqrt(v_hat) + ADAM_EPS) + ADAM_WD * w), nm, nv


def adamw(name, w, g, m, v, tm=256):
    n_rows, d = w.shape
    tm = tm if n_rows % tm == 0 else n_rows

    def body(w_ref, g_ref, m_ref, v_ref, go_ref, d_ref, nm_ref, nv_ref):
        gv = g_ref[...]
        go_ref[...] = gv
        d_ref[...], nm_ref[...], nv_ref[...] = _adam_update(w_ref[...], gv, m_ref[...], v_ref[...])

    row = pl.BlockSpec((tm, d), lambda i: (i, 0))
    return pl.pallas_call(
        body, grid=(n_rows // tm,), in_specs=[row] * 4, out_specs=[row] * 4,
        out_shape=[_sds((n_rows, d), F32)] * 4, name=name, compiler_params=_params(("parallel",)))(w, g, m, v)


def adamw_native(name, ws, gs, ms, vs):
    n = len(ws)

    def body(*refs):
        w_refs, g_refs, m_refs, v_refs = refs[:n], refs[n:2 * n], refs[2 * n:3 * n], refs[3 * n:4 * n]
        d_refs, nm_refs, nv_refs = refs[4 * n:5 * n], refs[5 * n:6 * n], refs[6 * n:7 * n]
        for k in range(n):
            dl, nm, nv = _adam_update(w_refs[k][...], g_refs[k][...], m_refs[k][...], v_refs[k][...])
            d_refs[k][...] = dl
            nm_refs[k][...] = nm
            nv_refs[k][...] = nv

    vm = pl.BlockSpec(memory_space=pltpu.VMEM)
    shapes = [_sds(w.shape, F32) for w in ws]
    out = pl.pallas_call(body, in_specs=[vm] * (4 * n), out_specs=[vm] * (3 * n), out_shape=shapes * 3, name=name,
                         compiler_params=_params())(*ws, *gs, *ms, *vs)
    return list(out[:n]), list(out[n:2 * n]), list(out[2 * n:])


VEC_ROWS = {"norm_mix": 0, "norm_mlp": 2, "norm_kv": 4, "norm_final": 5, "s5_d": 6, "b_q": 7, "b_o": 8, "s5_b_glu": 9,
            "b_kv": 11, "sinks": 12, "loss": 13}


def split_vectors(where, vecs, d_shard, glu_shard):
    kvw = 2 * N_KV * HEAD_DIM
    shapes = {"norm_mix": (2, D_MODEL), "norm_mlp": (2, D_MODEL), "norm_kv": (1, D_MODEL), "norm_final": (1, D_MODEL),
              "s5_d": (1, d_shard), "b_q": (1, D_MODEL), "b_o": (1, D_MODEL), "s5_b_glu": (1, glu_shard), "b_kv": (1, kvw),
              "sinks": (1, N_Q), "loss": (1, 128)}
    names = list(shapes)

    def body(where_ref, v_ref, *o_refs):
        chip = where_ref[1]
        for name, o_ref in zip(names, o_refs):
            r0, (r, n) = VEC_ROWS[name], shapes[name]
            if name == "s5_d":
                g = jnp.zeros((1, n), F32)
                for j in range(4):
                    g = jnp.where(chip == j, v_ref[r0:r0 + 1, j * n:(j + 1) * n], g)
            elif name == "s5_b_glu":
                g = jnp.zeros((1, n), F32)
                for j in range(4):
                    row, col = r0 + (j * n) // D_MODEL, (j * n) % D_MODEL
                    g = jnp.where(chip == j, v_ref[row:row + 1, col:col + n], g)
            else:
                g = v_ref[r0:r0 + r, 0:n]
            o_ref[...] = g

    vm = pl.BlockSpec(memory_space=pltpu.VMEM)
    out = pl.pallas_call(body, in_specs=[pl.BlockSpec(memory_space=pltpu.SMEM), vm], out_specs=[vm] * len(names),
                         out_shape=[_sds(shapes[n], F32) for n in names], name="split_vectors",
                         compiler_params=_params())(where, vecs)
    return dict(zip(names, out))


def _position():
    x, y, c = lax.axis_index("x"), lax.axis_index("y"), lax.axis_index("c")
    others = [(1 - x, y), (x, 1 - y), (1 - x, 1 - y)]
    return x, y, c, others


def _window(ref, kind, chip, half, shard_shape):
    r, n = shard_shape
    if kind == "col":
        return ref.at[pl.ds(pl.multiple_of(half * (r // 2), 16), r // 2), pl.ds(pl.multiple_of(chip * n, 128), n)]
    return ref.at[pl.ds(pl.multiple_of(chip * r, 16), r), pl.ds(pl.multiple_of(half * (n // 2), 128), n // 2)]


def _half(ref, kind, half, shape):
    r, n = shape
    if kind == "col":
        return ref.at[pl.ds(pl.multiple_of(half * (r // 2), 16), r // 2), :]
    return ref.at[:, pl.ds(pl.multiple_of(half * (n // 2), 128), n // 2)]


def swap_start(name, grads, kinds):
    nt = len(grads)
    shapes = [tuple(g.shape) for g in grads]
    lands = [lax.empty(sh, BF16) for sh in shapes]

    def body(*refs):
        in_refs, land_refs = refs[:nt], refs[nt:2 * nt]
        send_sems, recv_sems, token = refs[2 * nt], refs[2 * nt + 1], refs[-1]
        x, y, c, _ = _position()
        for t in range(nt):
            pltpu.make_async_remote_copy(
                src_ref=_half(in_refs[t], kinds[t], 1 - c, shapes[t]), dst_ref=_half(land_refs[t], kinds[t], 1 - c, shapes[t]),
                send_sem=send_sems.at[t], recv_sem=recv_sems.at[t], device_id=(x, y, 1 - c), device_id_type=MESH).start()
        token[...] = jnp.zeros_like(token)

    sems = pltpu.SemaphoreType.DMA((nt,))
    both = list(grads) + lands
    out = pl.pallas_call(
        body, name=name, in_specs=[HBM_SPEC] * (2 * nt),
        out_specs=(SEM_SPEC, SEM_SPEC, *[HBM_SPEC] * (2 * nt), pl.BlockSpec(memory_space=pltpu.VMEM)),
        out_shape=(sems, sems, *[pltpu.HBM(a.shape, a.dtype) for a in both], _sds((8, 128), F32)),
        input_output_aliases={t: 2 + t for t in range(2 * nt)}, compiler_params=_split_params(),
    )(*[_in_hbm(a) for a in both])
    return out[0], out[1], list(out[2:2 + nt]), list(out[2 + nt:2 + 2 * nt]), out[-1]


def swap_wait(name, send_sems, recv_sems, grads, lands, kinds, after):
    nt = len(grads)
    shapes = [tuple(g.shape) for g in grads]

    def body(*refs):
        in_refs, land_refs = refs[:nt], refs[nt:2 * nt]
        send_ref, recv_ref = refs[2 * nt], refs[2 * nt + 1]
        x, y, c, _ = _position()
        for t in range(nt):
            cp = pltpu.make_async_remote_copy(
                src_ref=_half(in_refs[t], kinds[t], 1 - c, shapes[t]), dst_ref=_half(land_refs[t], kinds[t], c, shapes[t]),
                send_sem=send_ref.at[t], recv_sem=recv_ref.at[t], device_id=(x, y, 1 - c), device_id_type=MESH)
            cp.wait_send()
            cp.wait_recv()

    both = list(grads) + list(lands)
    out = pl.pallas_call(
        body, name=name, in_specs=[HBM_SPEC] * (2 * nt) + [SEM_SPEC, SEM_SPEC, HBM_SPEC], out_specs=[HBM_SPEC] * (2 * nt),
        out_shape=[pltpu.HBM(a.shape, a.dtype) for a in both], input_output_aliases={t: t for t in range(2 * nt)},
        compiler_params=_split_params())(*both, send_sems, recv_sems, _in_hbm(after))
    return list(out[:nt]), list(out[nt:])


def _half_spec(kind, shape, tiles):
    r, n = shape
    if kind == "col":
        tn = n // tiles
        return pl.BlockSpec((r // 2, tn), lambda i, s: (s[0], i))
    tm = r // tiles
    return pl.BlockSpec((tm, n // 2), lambda i, s: (i, s[0]))


def add_halves(name, mine, landed, kinds, where, tiles=4):
    nt = len(mine)
    shapes = [tuple(a.shape) for a in mine]

    def compact(t):
        r, n = shapes[t]
        if kinds[t] == "col":
            return (r // 2, n), pl.BlockSpec((r // 2, n // tiles), lambda i, s: (0, i))
        return (r, n // 2), pl.BlockSpec((r // tiles, n // 2), lambda i, s: (i, 0))

    def body(s_ref, *refs):
        for a_ref, b_ref, o_ref in zip(refs[:nt], refs[nt:2 * nt], refs[2 * nt:]):
            o_ref[...] = (a_ref[...].astype(F32) + b_ref[...].astype(F32)).astype(BF16)

    specs = [_half_spec(kinds[t], shapes[t], tiles) for t in range(nt)]
    return pl.pallas_call(
        body, grid_spec=pltpu.PrefetchScalarGridSpec(num_scalar_prefetch=1, grid=(tiles,), in_specs=specs + specs,
                                                     out_specs=[compact(t)[1] for t in range(nt)]),
        out_shape=[_sds(compact(t)[0], BF16) for t in range(nt)], name=name,
        compiler_params=_params(("parallel",)))(where, *mine, *landed)


def sum_shards(name, parts, landed, kinds, shard_shapes, where, layers, n_layers, intos, tiles=2):
    nt = len(parts)
    in_specs, out_specs = [], []
    for t in range(nt):
        (r, n), layer = shard_shapes[t], layers[t]
        if kinds[t] == "col":
            tm, width = r // 2 // tiles, n
            own = pl.BlockSpec((tm, n), lambda i, s: (i, s[1]))
            out = pl.BlockSpec((None, tm, n), lambda i, s, layer=layer: (layer, s[0] * tiles + i, 0))
        else:
            tm, width = r // tiles, n // 2
            own = pl.BlockSpec((tm, n // 2), lambda i, s: (s[1] * tiles + i, 0))
            out = pl.BlockSpec((None, tm, n // 2), lambda i, s, layer=layer: (layer, i, s[0]))
        in_specs += [own, pl.BlockSpec((3, tm, width), lambda i, s: (0, i, 0))]
        out_specs.append(out)
    args, aliases = [where] + [a for pair in zip(parts, landed) for a in pair], {}
    for t in range(nt):
        if intos[t] is not None:
            aliases[len(args)] = t
            in_specs.append(pl.BlockSpec(memory_space=pl.ANY))
            args.append(intos[t])

    def body(s_ref, *refs):
        for t in range(nt):
            a_ref, l_ref, o_ref = refs[2 * t], refs[2 * t + 1], refs[len(in_specs) + t]
            o_ref[...] = ((a_ref[...].astype(F32) + l_ref[0].astype(F32)) + l_ref[1].astype(F32)) + l_ref[2].astype(F32)

    return pl.pallas_call(
        body, grid_spec=pltpu.PrefetchScalarGridSpec(num_scalar_prefetch=1, grid=(tiles,), in_specs=in_specs,
                                                     out_specs=out_specs),
        out_shape=[_sds((n_layers[t],) + tuple(shard_shapes[t]), F32) for t in range(nt)], input_output_aliases=aliases,
        name=name, compiler_params=_params(("parallel",)))(*args)


def share_halves(arrays, entries):
    na, nt = len(arrays), len(entries)

    def body(*refs):
        out_refs = refs[na:2 * na]
        send_sems, recv_sems = refs[2 * na:]
        x, y, c, _ = _position()
        cps = []
        for t, (a, layer, kind) in enumerate(entries):
            shape = tuple(arrays[a].shape[1:])
            mine = _half(out_refs[a].at[layer], kind, c, shape)
            cp = pltpu.make_async_remote_copy(
                src_ref=mine, dst_ref=mine, send_sem=send_sems.at[t], recv_sem=recv_sems.at[t],
                device_id=(x, y, 1 - c), device_id_type=MESH)
            cp.start()
            cps.append(cp)
        for t, (a, layer, kind) in enumerate(entries):
            shape = tuple(arrays[a].shape[1:])
            other = _half(out_refs[a].at[layer], kind, 1 - c, shape)
            pltpu.make_async_remote_copy(
                src_ref=other, dst_ref=other, send_sem=send_sems.at[t], recv_sem=recv_sems.at[t],
                device_id=(x, y, 1 - c), device_id_type=MESH).wait_recv()
        for cp in cps:
            cp.wait_send()

    hbm = pl.BlockSpec(memory_space=pl.ANY)
    return pl.pallas_call(
        body, in_specs=[hbm] * na, out_specs=[hbm] * na, out_shape=[_sds(a.shape, F32) for a in arrays],
        input_output_aliases={i: i for i in range(na)},
        scratch_shapes=[pltpu.SemaphoreType.DMA((nt,)), pltpu.SemaphoreType.DMA((nt,))],
        name="share_halves", compiler_params=_params())(*arrays)


HBM_SPEC = pl.BlockSpec(memory_space=pltpu.HBM)
SEM_SPEC = pl.BlockSpec(memory_space=pltpu.SEMAPHORE)
ANY_SPEC = pl.BlockSpec(memory_space=pl.ANY)


def _split_params():
    return pltpu.CompilerParams(has_side_effects=pltpu.SideEffectType.DATAFLOW_SIDE_EFFECTING,
                                vmem_limit_bytes=VMEM_LIMIT_BYTES)


def _in_hbm(a):
    return pltpu.with_memory_space_constraint(a, pltpu.HBM)


def cast_place(arrays, entries, where, tiles=2):
    in_specs, out_specs, fulls = [], [], []
    for a, layer, kind in entries:
        _, r, n = arrays[a].shape
        tm = r // tiles
        in_specs.append(pl.BlockSpec((None, tm, n), lambda i, s, layer=layer: (layer, i, 0)))
        if kind == "col":
            fulls.append((r, 4 * n))
            out_specs.append(pl.BlockSpec((tm, n), lambda i, s: (i, s[1])))
        else:
            fulls.append((4 * r, n))
            out_specs.append(pl.BlockSpec((tm, n), lambda i, s: (s[1] * tiles + i, 0)))
    nt = len(entries)

    def body(s_ref, *refs):
        for w_ref, o_ref in zip(refs[:nt], refs[nt:]):
            o_ref[...] = w_ref[...].astype(BF16)

    return pl.pallas_call(
        body, grid_spec=pltpu.PrefetchScalarGridSpec(num_scalar_prefetch=1, grid=(tiles,), in_specs=in_specs,
                                                     out_specs=out_specs),
        out_shape=[_sds(f, BF16) for f in fulls], name="cast_place",
        compiler_params=_params(("parallel",)))(where, *[arrays[a] for a, _, _ in entries])


def gather_start(fulls, kinds, shard_shapes, after):
    nt = len(fulls)
    na = 0 if after is None else 1

    def body(*refs):
        full_refs = refs[:nt]
        send_sems, recv_sems, token = refs[nt + na], refs[nt + na + 1], refs[-1]
        x, y, c, others = _position()
        for t in range(nt):
            mine = _window(full_refs[t], kinds[t], 2 * x + y, c, shard_shapes[t])
            for j, (ox, oy) in enumerate(others):
                pltpu.make_async_remote_copy(
                    src_ref=mine, dst_ref=mine, send_sem=send_sems.at[3 * t + j], recv_sem=recv_sems.at[3 * t + j],
                    device_id=(ox, oy, c), device_id_type=MESH).start()
        token[...] = jnp.zeros_like(token)

    sems = pltpu.SemaphoreType.DMA((3 * nt,))
    out = pl.pallas_call(
        body, name="gather_start", in_specs=[HBM_SPEC] * nt + [ANY_SPEC] * na,
        out_specs=(SEM_SPEC, SEM_SPEC, *[HBM_SPEC] * nt, pl.BlockSpec(memory_space=pltpu.VMEM)),
        out_shape=(sems, sems, *[pltpu.HBM(f.shape, f.dtype) for f in fulls], _sds((8, 128), F32)),
        input_output_aliases={t: 2 + t for t in range(nt)}, compiler_params=_split_params(),
    )(*[_in_hbm(f) for f in fulls], *([] if after is None else [after]))
    return out[0], out[1], list(out[2:2 + nt]), out[-1]


def gather_wait(name, send_sems, recv_sems, fulls, kinds, shard_shapes, after, first):
    nt = len(fulls)

    def body(*refs):
        full_refs, send_ref, recv_ref = refs[:nt], refs[nt], refs[nt + 1]
        x, y, c, others = _position()
        for t in range(nt):
            mine = _window(full_refs[t], kinds[t], 2 * x + y, c, shard_shapes[t])
            for j, (ox, oy) in enumerate(others):
                cp = pltpu.make_async_remote_copy(
                    src_ref=mine, dst_ref=_window(full_refs[t], kinds[t], 2 * ox + oy, c, shard_shapes[t]),
                    send_sem=send_ref.at[3 * (first + t) + j], recv_sem=recv_ref.at[3 * (first + t) + j],
                    device_id=(ox, oy, c), device_id_type=MESH)
                cp.wait_send()
                cp.wait_recv()

    out = pl.pallas_call(
        body, name=name, in_specs=[HBM_SPEC] * nt + [SEM_SPEC, SEM_SPEC, HBM_SPEC], out_specs=[HBM_SPEC] * nt,
        out_shape=[pltpu.HBM(f.shape, f.dtype) for f in fulls], input_output_aliases={t: t for t in range(nt)},
        compiler_params=_split_params())(*fulls, send_sems, recv_sems, _in_hbm(after))
    return list(out)


def forward_halves(name, fulls, kinds, shard_shapes):
    nt = len(fulls)

    def body(*refs):
        out_refs = refs[nt:2 * nt]
        send_sems, recv_sems = refs[2 * nt:]
        x, y, c, others = _position()
        cps = []
        for t in range(nt):
            for j, (ox, oy) in enumerate(others):
                landed = _window(out_refs[t], kinds[t], 2 * ox + oy, c, shard_shapes[t])
                cp = pltpu.make_async_remote_copy(
                    src_ref=landed, dst_ref=landed, send_sem=send_sems.at[3 * t + j], recv_sem=recv_sems.at[3 * t + j],
                    device_id=(x, y, 1 - c), device_id_type=MESH)
                cp.start()
                cps.append(cp)
        for t in range(nt):
            for j, (ox, oy) in enumerate(others):
                got = _window(out_refs[t], kinds[t], 2 * ox + oy, 1 - c, shard_shapes[t])
                pltpu.make_async_remote_copy(
                    src_ref=got, dst_ref=got, send_sem=send_sems.at[3 * t + j], recv_sem=recv_sems.at[3 * t + j],
                    device_id=(x, y, 1 - c), device_id_type=MESH).wait_recv()
        for cp in cps:
            cp.wait_send()

    out = pl.pallas_call(
        body, in_specs=[ANY_SPEC] * nt, out_specs=[ANY_SPEC] * nt, out_shape=[_sds(f.shape, f.dtype) for f in fulls],
        input_output_aliases={t: t for t in range(nt)},
        scratch_shapes=[pltpu.SemaphoreType.DMA((3 * nt,)), pltpu.SemaphoreType.DMA((3 * nt,))],
        name=name, compiler_params=_params())(*fulls)
    return list(out)


def _piece(ref, kind, chip, shard_shape):
    r, n = shard_shape
    if kind == "col":
        return ref.at[:, pl.ds(pl.multiple_of(chip * n, 128), n)]
    return ref.at[pl.ds(pl.multiple_of(chip * r, 16), r), :]


def _piece_shape(kind, shard_shape):
    r, n = shard_shape
    return (r // 2, n) if kind == "col" else (r, n // 2)


def exchange_start(name, parts, kinds, shard_shapes):
    nt = len(parts)
    lands = [lax.empty((3,) + _piece_shape(kinds[t], shard_shapes[t]), BF16) for t in range(nt)]

    def body(*refs):
        part_refs, land_refs = refs[:nt], refs[nt:2 * nt]
        send_sems, recv_sems, token = refs[2 * nt], refs[2 * nt + 1], refs[-1]
        x, y, c, others = _position()
        for t in range(nt):
            for j, (ox, oy) in enumerate(others):
                pltpu.make_async_remote_copy(
                    src_ref=_piece(part_refs[t], kinds[t], 2 * ox + oy, shard_shapes[t]), dst_ref=land_refs[t].at[j],
                    send_sem=send_sems.at[3 * t + j], recv_sem=recv_sems.at[3 * t + j],
                    device_id=(ox, oy, c), device_id_type=MESH).start()
        token[...] = jnp.zeros_like(token)

    sems = pltpu.SemaphoreType.DMA((3 * nt,))
    both = list(parts) + lands
    out = pl.pallas_call(
        body, name=name, in_specs=[HBM_SPEC] * (2 * nt),
        out_specs=(SEM_SPEC, SEM_SPEC, *[HBM_SPEC] * (2 * nt), pl.BlockSpec(memory_space=pltpu.VMEM)),
        out_shape=(sems, sems, *[pltpu.HBM(a.shape, a.dtype) for a in both], _sds((8, 128), F32)),
        input_output_aliases={t: 2 + t for t in range(2 * nt)}, compiler_params=_split_params(),
    )(*[_in_hbm(a) for a in both])
    return out[0], out[1], list(out[2:2 + nt]), list(out[2 + nt:2 + 2 * nt]), out[-1]


def exchange_wait(name, send_sems, recv_sems, parts, lands, kinds, shard_shapes, after):
    nt = len(parts)

    def body(*refs):
        part_refs, land_refs = refs[:nt], refs[nt:2 * nt]
        send_ref, recv_ref = refs[2 * nt], refs[2 * nt + 1]
        x, y, c, others = _position()
        for t in range(nt):
            for j, (ox, oy) in enumerate(others):
                cp = pltpu.make_async_remote_copy(
                    src_ref=_piece(part_refs[t], kinds[t], 2 * ox + oy, shard_shapes[t]), dst_ref=land_refs[t].at[j],
                    send_sem=send_ref.at[3 * t + j], recv_sem=recv_ref.at[3 * t + j],
                    device_id=(ox, oy, c), device_id_type=MESH)
                cp.wait_send()
                cp.wait_recv()

    both = list(parts) + list(lands)
    out = pl.pallas_call(
        body, name=name, in_specs=[HBM_SPEC] * (2 * nt) + [SEM_SPEC, SEM_SPEC, HBM_SPEC], out_specs=[HBM_SPEC] * (2 * nt),
        out_shape=[pltpu.HBM(a.shape, a.dtype) for a in both], input_output_aliases={t: t for t in range(2 * nt)},
        compiler_params=_split_params())(*both, send_sems, recv_sems, _in_hbm(after))
    return list(out[:nt]), list(out[nt:])


def gather_rows(name, mine):
    def body(in_ref, out_ref, send_sems, recv_sems):
        x, y, c, others = _position()
        me = 2 * x + y
        out_ref[me] = in_ref[...]
        cps = []
        for j, (ox, oy) in enumerate(others):
            cp = pltpu.make_async_remote_copy(
                src_ref=in_ref, dst_ref=out_ref.at[me], send_sem=send_sems.at[j], recv_sem=recv_sems.at[j],
                device_id=(ox, oy, c), device_id_type=MESH)
            cp.start()
            cps.append(cp)
        for j, (ox, oy) in enumerate(others):
            pltpu.make_async_remote_copy(
                src_ref=in_ref, dst_ref=out_ref.at[2 * ox + oy], send_sem=send_sems.at[j], recv_sem=recv_sems.at[j],
                device_id=(ox, oy, c), device_id_type=MESH).wait_recv()
        for cp in cps:
            cp.wait_send()

    vm = pl.BlockSpec(memory_space=pltpu.VMEM)
    return pl.pallas_call(
        body, in_specs=[vm], out_specs=vm, out_shape=_sds((4,) + tuple(mine.shape), F32),
        scratch_shapes=[pltpu.SemaphoreType.DMA((3,)), pltpu.SemaphoreType.DMA((3,))],
        name=name, compiler_params=_params())(mine)


def all_reduce_small(name, bufs):
    n = len(bufs)
    halves = [b.shape[0] // 2 for b in bufs]

    def body(*refs):
        in_refs, out_refs, lands = refs[:n], refs[n:2 * n], refs[2 * n:3 * n]
        send_sems, recv_sems = refs[3 * n:]
        x, y, c, _ = _position()
        mine = [pl.ds(pl.multiple_of(c * h, 8), h) for h in halves]
        other = [pl.ds(pl.multiple_of((1 - c) * h, 8), h) for h in halves]
        for k in range(n):
            out_refs[k][mine[k], :] = in_refs[k][mine[k], :]
        for s, peer in enumerate([(x, y, 1 - c), (1 - x, y, c), (x, 1 - y, c)]):
            cps = []
            for k in range(n):
                src = in_refs[k].at[other[k]] if s == 0 else out_refs[k].at[mine[k]]
                cp = pltpu.make_async_remote_copy(
                    src_ref=src, dst_ref=lands[k].at[s], send_sem=send_sems.at[4 * k + s], recv_sem=recv_sems.at[4 * k + s],
                    device_id=peer, device_id_type=MESH)
                cp.start()
                cps.append(cp)
            for k, cp in enumerate(cps):
                cp.wait()
                out_refs[k][mine[k], :] = out_refs[k][mine[k], :] + lands[k][s]
        cps = []
        for k in range(n):
            cp = pltpu.make_async_remote_copy(
                src_ref=out_refs[k].at[mine[k]], dst_ref=out_refs[k].at[mine[k]], send_sem=send_sems.at[4 * k + 3],
                recv_sem=recv_sems.at[4 * k + 3], device_id=(x, y, 1 - c), device_id_type=MESH)
            cp.start()
            cps.append(cp)
        for cp in cps:
            cp.wait()

    vm = pl.BlockSpec(memory_space=pltpu.VMEM)
    out = pl.pallas_call(
        body, in_specs=[vm] * n, out_specs=[vm] * n, out_shape=[_sds(b.shape, F32) for b in bufs],
        scratch_shapes=[pltpu.VMEM((3, h, b.shape[1]), F32) for h, b in zip(halves, bufs)]
        + [pltpu.SemaphoreType.DMA((4 * n,)), pltpu.SemaphoreType.DMA((4 * n,))],
        name=name, compiler_params=_params())(*bufs)
    return list(out)


def _local_step(x, target, small, need, emit_swap, emit_exchange):
    d = D_MODEL
    full = {}

    def after_token(vec, token):
        return vec if token is None else vec + token[0:1, 0:1]

    def token_rows(token, width):
        return [] if token is None else [after_token(jnp.zeros((1, width), F32), token)]

    def plus(acc, rows):
        return acc + rows[0] if rows else acc

    rb16, rbt16, rc16, rct16, lr_t, li_t = small["s5_operands"]
    ge, y2, cs = s5_fwd(x, small["norm_mix0"], small["s5_d"], rb16, rc16, lr_t, li_t)
    full.update(need("glu", ge))

    def norm_rows(h, gains):
        xh, _ = _rms_hat(h)
        return [xh * g for g in gains]

    def glu_epilogue(accs, e, r):
        v, gt = accs[0] + r[0], accs[1] + r[1]
        h = e[0] + v * jax.nn.sigmoid(gt)
        return [h, v, gt] + norm_rows(h, r[2:])

    h1, val, gate, n1 = mm_nn(
        "glu", ge, full["w_glu"], [0, d], d, glu_epilogue, [F32, F32, F32, BF16], extras=[x],
        rowvecs=[(small["s5_b_glu"], 0), (small["s5_b_glu"], d), (small["norm_mlp0"], 0)], tm=512, tn=d)

    def mlp_fwd(tag, h, n, w_in, get_w_out, next_gains, head=None):
        def in_epilogue(accs, e, rv):
            pos = jnp.maximum(accs[0], 0.0)
            return [pos * pos, 2.0 * pos]

        r, slope = mm_nn("mlp_in" + tag, n, w_in, [0], w_in.shape[1], in_epilogue, [BF16, BF16], tm=2048)
        w_out = get_w_out(r)

        def epilogue(accs, e, rv):
            h_out = e[0] + accs[0]
            return [h_out] + norm_rows(h_out, rv)

        if head is not None:
            return head(r, w_out, h), (n, r, slope)
        outs = mm_nn("mlp_out" + tag, r, w_out, [0], d, epilogue, [F32] + [BF16] * len(next_gains), extras=[h],
                     rowvecs=[(g, 0) for g in next_gains], tm=512, tn=d)
        return outs[0], outs[1:], (n, r, slope)

    full.update(need("mlp_in0", h1))

    def w_out0(after):
        full.update(need("mlp_out0", after))
        return full["w_out0"]

    h2, (nkv, n2), mlp0 = mlp_fwd("0", h1, n1, full["w_in0"], w_out0, [small["norm_kv"], small["norm_mix1"]])

    full.update(need("attn", h2))
    kvw = 2 * N_KV * HEAD_DIM
    (kv,) = mm_nn("kv_proj", nkv, full["w_kv"], [0], kvw, lambda accs, e, r: [accs[0] + r[0]], [BF16],
                  rowvecs=[(small["b_kv"], 0)], tm=2048)
    (q,) = mm_nn("q_proj", n2, full["w_q"], [0], d, lambda accs, e, r: [accs[0] + r[0]], [BF16],
                 rowvecs=[(small["b_q"], 0)], tm=2048)
    sinks = small["sinks"].reshape(N_Q)
    o = attn_fwd(q, kv, sinks)
    def o_epilogue(accs, e, r):
        h_out = e[0] + accs[0] + r[0]
        return [h_out] + norm_rows(h_out, r[1:])

    h3, n3 = mm_nn("o_proj", o, full["w_o"], [0], d, o_epilogue, [F32, BF16], extras=[h2],
                   rowvecs=[(small["b_o"], 0), (small["norm_mlp1"], 0)], tm=512, tn=d)
    full.update(need("mlp_in1", h3))

    def w_out1(after):
        full.update(need("mlp_out1", after))
        return full["w_out1"]

    def loss_head(r, w_out, h):
        def epilogue(accs, e, rv):
            xh, rr = _rms_hat(e[0] + accs[0])
            err = xh * rv[0] - e[1]
            dy = err * (1.0 / d)
            dxh = dy * rv[0]
            dx = rr * (dxh - xh * jnp.mean(dxh * xh, axis=-1, keepdims=True))
            loss = jnp.full((1, d), 0.5 * jnp.sum(jnp.mean(err * err, axis=-1, keepdims=True)), F32)
            return [dx, dx, loss, jnp.sum(dy * xh, axis=0, keepdims=True)]

        return mm_nn("mlp_out1", r, w_out, [0], d, epilogue, [F32, BF16], extras=[h, target],
                     rowvecs=[(small["norm_final"], 0)], n_sums=2, tm=512, tn=d)

    (dh, dhb, loss_tile, dg_final), mlp1 = mlp_fwd("1", h3, n3, full["w_in1"], w_out1, [], head=loss_head)

    grads_small, grads_full = {"norm_final": dg_final}, {}
    ident = lambda acc, e, r: [plus(acc, r)]
    layer1 = ["w_out1", "w_in1", "w_o", "w_q", "w_kv"]
    layer0 = ["w_out0", "w_in0", "w_glu"]

    def norm_bwd_rows(x_rows, res, dys, gains):
        xh, r = _rms_hat(x_rows)
        dxh = sum(dy * g for dy, g in zip(dys, gains))
        dx = r * (dxh - xh * jnp.mean(dxh * xh, axis=-1, keepdims=True)) + res
        return dx, [jnp.sum(dy * xh, axis=0, keepdims=True) for dy in dys]

    def mlp_bwd(tag, dh, dhb, h_in, gain, w_in, w_out, saved, token=None):
        n, r, slope = saved
        grads_full["w_out" + tag] = mm_tn("dw_out" + tag, r, dhb, tn=1024)
        (da,) = mm_nt("mlp_da" + tag, dhb, w_out, lambda acc, e, rv: [plus(acc * e[0].astype(F32), rv)], [BF16],
                      extras=[slope], rowvecs=token_rows(token, w_out.shape[0]), tm=2048)
        grads_full["w_in" + tag] = mm_tn("dw_in" + tag, n, da, tn=1024)

        def epilogue(acc, e, rv):
            dx, dgs = norm_bwd_rows(e[0], e[1], [acc], rv)
            return [dx, dx, jnp.sum(dx, axis=0, keepdims=True)] + dgs

        dx, dxb, colsum, dg = mm_nt("mlp_dn" + tag, da, w_in, epilogue, [F32, BF16], extras=[h_in, dh], rowvecs=[gain],
                                    n_sums=2, tm=512, tk=d)
        grads_small["norm_mlp" + tag] = dg
        return dx, dxb, colsum

    dh3, dh3b, colsum3 = mlp_bwd("1", dh, dhb, h3, small["norm_mlp1"], full["w_in1"], full["w_out1"], mlp1)
    grads_small["b_o"] = colsum3
    grads_full["w_o"] = mm_tn("dw_o", o, dh3b, tn=1024)
    (do,) = mm_nt("attn_do", dh3b, full["w_o"], ident, [BF16], tm=2048)
    dq, dbq, dprev, dcur, dsink = attn_bwd(q, kv, do, sinks)
    dkv, dbkv = kv_combine(dprev, dcur)
    grads_small["b_q"], grads_small["b_kv"], grads_small["sinks"] = dbq, dbkv, dsink
    grads_full["w_q"] = mm_tn("dw_q", n2, dq, tn=1024)
    grads_full["w_kv"] = mm_tn("dw_kv", nkv, dkv, tk=1024)
    token = emit_swap("layer1", {n: grads_full[n] for n in layer1})
    (dnkv,) = mm_nt("kv_dn", dkv, full["w_kv"], ident, [F32], rowvecs=token_rows(token, d), tm=2048, tk=1024)

    def attn_dn_epilogue(acc, e, rv):
        dx, dgs = norm_bwd_rows(e[0], e[1], [acc, e[2]], rv)
        return [dx, dx] + dgs

    dh2, dh2b, dg_mix1, dg_kv = mm_nt("attn_dn", dq, full["w_q"], attn_dn_epilogue, [F32, BF16], extras=[h2, dh3, dnkv],
                                      rowvecs=[small["norm_mix1"], small["norm_kv"]], n_sums=2, tm=512, tk=d)
    grads_small["norm_mix1"], grads_small["norm_kv"] = dg_mix1, dg_kv
    token = emit_exchange("layer1", dh2b)
    dh1, _, _ = mlp_bwd("0", dh2, dh2b, h1, small["norm_mlp0"], full["w_in0"], full["w_out0"], mlp0, token)

    dz, db_glu = glu_bwd(dh1, val, gate)
    grads_small["s5_b_glu"] = db_glu
    grads_full["w_glu"] = mm_tn("dw_glu", ge, dz, tn=1024)
    token = emit_swap("layer0", {n: grads_full[n] for n in layer0})
    (dy2,) = mm_nt("glu_dy", dz, full["w_glu"], lambda acc, e, rv: [plus(acc, rv) * _gelu_grad(e[0])], [F32], extras=[y2],
                   rowvecs=token_rows(token, d), tm=1024, tk=1024)
    token = emit_exchange("layer0", dy2)
    grad_x, dd, drb, drc, dlr, dli, dg_mix0 = s5_bwd(x, small["norm_mix0"], dy2, dh1, after_token(small["s5_d"], token), cs,
                                                     rb16, rbt16, rct16, lr_t, li_t)
    grads_small["s5_d"] = dd
    grads_small["s5_mats"] = (drb, drc, dlr, dli)
    grads_small["norm_mix0"] = dg_mix0
    return loss_tile, grad_x, grads_small


SMALL_NAMES = ["norm_mix", "norm_mlp", "norm_kv", "norm_final", "s5_a_re", "s5_a_im", "s5_log_dt", "s5_b_re", "s5_b_im",
               "s5_c_re", "s5_c_im", "s5_d", "s5_b_glu", "b_kv", "b_q", "sinks", "b_o"]
BIG_NAMES = ["s5_w_glu", "w_kv", "w_q", "w_o", "w_mlp_in", "w_mlp_out"]
WEIGHT_ORDER = ["norm_mix", "norm_mlp", "norm_kv", "norm_final", "s5_a_re", "s5_a_im", "s5_log_dt", "s5_b_re", "s5_b_im",
                "s5_c_re", "s5_c_im", "s5_d", "s5_w_glu", "s5_b_glu", "w_kv", "b_kv", "w_q", "b_q", "sinks", "w_o", "b_o",
                "w_mlp_in", "w_mlp_out"]


def kernel(x, norm_mix, norm_mlp, norm_kv, norm_final, s5_a_re, s5_a_im, s5_log_dt, s5_b_re, s5_b_im, s5_c_re, s5_c_im, s5_d, s5_w_glu, s5_b_glu, w_kv, b_kv, w_q, b_q, sinks, w_o, b_o, w_mlp_in, w_mlp_out, loss_target, m_norm_mix, m_norm_mlp, m_norm_kv, m_norm_final, m_s5_a_re, m_s5_a_im, m_s5_log_dt, m_s5_b_re, m_s5_b_im, m_s5_c_re, m_s5_c_im, m_s5_d, m_s5_w_glu, m_s5_b_glu, m_w_kv, m_b_kv, m_w_q, m_b_q, m_sinks, m_w_o, m_b_o, m_w_mlp_in, m_w_mlp_out, v_norm_mix, v_norm_mlp, v_norm_kv, v_norm_final, v_s5_a_re, v_s5_a_im, v_s5_log_dt, v_s5_b_re, v_s5_b_im, v_s5_c_re, v_s5_c_im, v_s5_d, v_s5_w_glu, v_s5_b_glu, v_w_kv, v_b_kv, v_w_q, v_b_q, v_sinks, v_w_o, v_b_o, v_w_mlp_in, v_w_mlp_out):
    env = dict(locals())
    w = {n: env[n] for n in WEIGHT_ORDER}
    mom = {n: env["m_" + n] for n in WEIGHT_ORDER}
    var = {n: env["v_" + n] for n in WEIGHT_ORDER}
    d = D_MODEL
    xi, yi, ci = lax.axis_index("x"), lax.axis_index("y"), lax.axis_index("c")
    chip = 2 * xi + yi
    where = jnp.stack([ci, chip]).astype(jnp.int32)

    dsh, bsh = s5_d.shape[1], s5_b_glu.shape[1]
    packed = jnp.concatenate([s5_d.reshape(-1, 128), s5_b_glu.reshape(-1, 128)])
    n_d, n_b = dsh // 128, bsh // 128
    gathered_rows = gather_rows("gather_vectors", jnp.pad(packed, ((0, 8 - n_d - n_b), (0, 0))))
    d_full = gathered_rows[:, 0:n_d].reshape(1, -1)
    bglu_full = gathered_rows[:, n_d:n_d + n_b].reshape(1, -1)

    big = [s5_w_glu, w_kv[None], w_q, w_o, w_mlp_in, w_mlp_out]
    entries = [(0, 0, "col"), (1, 0, "row"), (2, 0, "row"), (3, 0, "row"), (4, 0, "col"), (4, 1, "col"),
               (5, 0, "row"), (5, 1, "row")]
    names = ["w_glu", "w_kv", "w_q", "w_o", "w_in0", "w_in1", "w_out0", "w_out1"]
    kinds = dict(zip(names, [k for _, _, k in entries]))
    shard_shapes = dict(zip(names, [tuple(big[a].shape[1:]) for a, _, _ in entries]))

    placed_w = dict(zip(names, cast_place(big, entries, where)))
    gather_groups = {"glu": ["w_glu"], "mlp_in0": ["w_in0"], "mlp_out0": ["w_out0"], "attn": ["w_kv", "w_q", "w_o"],
                     "mlp_in1": ["w_in1"], "mlp_out1": ["w_out1"]}
    order = [n for members in gather_groups.values() for n in members]
    send, recv, thru, token = gather_start([placed_w[n] for n in order], [kinds[n] for n in order],
                                           [shard_shapes[n] for n in order], gathered_rows)
    started = dict(zip(order, thru))

    def need(group, after):
        members = gather_groups[group]
        ks, shapes = [kinds[n] for n in members], [shard_shapes[n] for n in members]
        landed = gather_wait("gather_wait_" + group, send, recv, [started[n] for n in members], ks, shapes, after,
                             order.index(members[0]))
        return dict(zip(members, forward_halves("forward_halves_" + group, landed, ks, shapes)))

    swapping, exchanging = {}, {}

    def emit_swap(group, partial):
        members = list(partial)
        send, recv, mine, lands, tok = swap_start("swap_start_" + group, [partial[n] for n in members],
                                                  [kinds[n] for n in members])
        swapping[group] = (members, send, recv, mine, lands)
        return tok

    def emit_exchange(group, after):
        members, send, recv, mine, lands = swapping[group]
        ks, shapes = [kinds[n] for n in members], [shard_shapes[n] for n in members]
        mine, landed = swap_wait("swap_wait_" + group, send, recv, mine, lands, ks, after)
        sums = add_halves("add_halves_" + group, mine, landed, ks, where)
        send, recv, parts, lands, tok = exchange_start("exchange_start_" + group, sums, ks, shapes)
        exchanging[group] = (members, send, recv, parts, lands)
        return tok

    s5_args = (s5_a_re[0], s5_a_im[0], s5_log_dt[0], s5_b_re[0], s5_b_im[0])
    small = {
        "norm_mix0": norm_mix[0:1] + token[0:1, 0:1], "norm_mix1": norm_mix[1:2], "norm_mlp0": norm_mlp[0:1], "norm_mlp1": norm_mlp[1:2],
        "norm_kv": norm_kv.reshape(1, d), "norm_final": norm_final.reshape(1, d), "s5_operands": s5_prep(*s5_args, s5_c_re[0], s5_c_im[0]),
        "s5_d": d_full, "s5_b_glu": bglu_full,
        "b_kv": b_kv.reshape(1, -1), "b_q": b_q, "sinks": sinks, "b_o": b_o,
    }
    loss_row, grad_x, gs = _local_step(x[0], loss_target[0], small, need, emit_swap, emit_exchange)

    mats = s5_compact(*gs["s5_mats"])
    rows = [gs["norm_mix0"], gs["norm_mix1"], gs["norm_mlp0"], gs["norm_mlp1"], gs["norm_kv"], gs["norm_final"], gs["s5_d"],
            gs["b_q"], gs["b_o"], gs["s5_b_glu"], gs["b_kv"], gs["sinks"], loss_row, jnp.zeros((2, d), F32)]
    vecs, mats = all_reduce_small("reduce_small", [jnp.concatenate(rows, axis=0), mats])
    grads = split_vectors(where, vecs, dsh, bsh)
    loss = grads.pop("loss")[0, 0]
    g_are, g_aim, g_dt, g_bre, g_bim, dc_re, dc_im = s5_param_bwd(mats, *s5_args)
    grads.update({"s5_a_re": g_are[None], "s5_a_im": g_aim[None], "s5_log_dt": g_dt[None], "s5_b_re": g_bre[None],
                  "s5_b_im": g_bim[None], "s5_c_re": dc_re[None], "s5_c_im": dc_im[None]})

    reduced = [None] * len(big)
    where_of = dict(zip(names, entries))
    for group, after in (("layer1", grad_x), ("layer0", mats)):
        members, send, recv, parts, lands = exchanging[group]
        ks, shapes = [kinds[n] for n in members], [shard_shapes[n] for n in members]
        parts, lands = exchange_wait("exchange_wait_" + group, send, recv, parts, lands, ks, shapes, after)
        targets = [where_of[n][0] for n in members]
        sums = sum_shards("sum_shards_" + group, parts, lands, ks, shapes, where, [where_of[n][1] for n in members],
                          [big[a].shape[0] for a in targets], [reduced[a] for a in targets])
        for a, arr in zip(targets, sums):
            reduced[a] = arr
    reduced = share_halves(reduced, entries)
    for n, g in zip(BIG_NAMES, reduced):
        grads[n] = g.reshape(w[n].shape)

    delta, new_m, new_v = {}, {}, {}
    for n in BIG_NAMES:
        flat = lambda a: a.reshape(-1, a.shape[-1])
        go, dl, nm, nv = adamw("adamw_" + n, flat(w[n]), flat(grads[n]), flat(mom[n]), flat(var[n]))
        grads[n], delta[n], new_m[n], new_v[n] = (t.reshape(w[n].shape) for t in (go, dl, nm, nv))

    def view(n, a):
        return a.reshape(1, -1) if a.ndim == 1 else jnp.swapaxes(a, -1, -2) if n in ("s5_b_re", "s5_b_im") else a

    sw, sg, sm, sv = ([view(n, t[n]) for n in SMALL_NAMES] for t in (w, grads, mom, var))
    for n, a, b, c_ in zip(SMALL_NAMES, *adamw_native("adamw_small", sw, sg, sm, sv)):
        delta[n], new_m[n], new_v[n] = (view(n, t) if t.ndim == 4 else t for t in (a, b, c_))

    out = [loss.reshape(()), grad_x[None]]
    for table in (grads, delta, new_m, new_v):
        out += [table[n].reshape(w[n].shape) for n in WEIGHT_ORDER]
    return tuple(out)
```

```python
import math

import jax
import jax.numpy as jnp
from jax import lax
from jax.experimental import pallas as pl
from jax.experimental.pallas import tpu as pltpu

F32 = jnp.float32
BF16 = jnp.bfloat16

D_MODEL = 1024
S5_GROUPS = 64
S5_GROUP = 16
S5_STATE = 64
N_KV = 4
N_Q = 16
HEAD_DIM = 64
BLOCK = 128
NORM_EPS = 1e-5
LAMBDA_RE_MAX = -1e-4
ADAM_LR, ADAM_B1, ADAM_B2, ADAM_EPS, ADAM_WD, ADAM_STEP = 0.001, 0.9, 0.999, 1e-08, 0.01, 10

VMEM_LIMIT_BYTES = 56 * 1024 * 1024
S5_CHUNK = 256
S5_BLOCKS = 4
MESH = pl.DeviceIdType.MESH


def _params(sem=None):
    return pltpu.CompilerParams(dimension_semantics=sem, vmem_limit_bytes=VMEM_LIMIT_BYTES)


def _sds(shape, dtype):
    return jax.ShapeDtypeStruct(shape, dtype)


def _rms_hat(xv):
    r = lax.rsqrt(jnp.mean(xv * xv, axis=-1, keepdims=True) + NORM_EPS)
    return xv * r, r


def mm_nn(name, a, w, col_offsets, n_out, epilogue, out_dtypes, extras=(), rowvecs=(), n_sums=0, tm=1024, tn=512):
    m, k = a.shape
    tm, tn = min(tm, m), min(tn, n_out)
    nw, ne, nr, no = len(col_offsets), len(extras), len(rowvecs), len(out_dtypes)

    def body(a_ref, *refs):
        w_refs, e_refs, r_refs = refs[:nw], refs[nw:nw + ne], refs[nw + ne:nw + ne + nr]
        o_refs, s_refs = refs[nw + ne + nr:nw + ne + nr + no], refs[nw + ne + nr + no:]
        av = a_ref[...]
        accs = [jnp.dot(av, w_ref[...], preferred_element_type=F32) for w_ref in w_refs]
        outs = epilogue(accs, [e[...] for e in e_refs], [r[...] for r in r_refs])
        for o_ref, o in zip(o_refs, outs[:no]):
            o_ref[...] = o.astype(o_ref.dtype)
        if n_sums:
            @pl.when(pl.program_id(1) == 0)
            def _():
                for s_ref in s_refs:
                    s_ref[...] = jnp.zeros_like(s_ref)

            for s_ref, val in zip(s_refs, outs[no:]):
                s_ref[...] += val

    def wspec(off):
        return pl.BlockSpec((k, tn), lambda j, i, off=off: (0, off // tn + j))

    def rspec(off):
        return pl.BlockSpec((1, tn), lambda j, i, off=off: (0, off // tn + j))

    tile = pl.BlockSpec((tm, tn), lambda j, i: (i, j))
    in_specs = ([pl.BlockSpec((tm, k), lambda j, i: (i, 0))] + [wspec(o) for o in col_offsets]
                + [tile] * ne + [rspec(o) for _, o in rowvecs])
    sem = ("parallel", "arbitrary") if n_sums else ("parallel", "parallel")
    return pl.pallas_call(
        body, grid=(n_out // tn, m // tm), in_specs=in_specs,
        out_specs=[tile] * no + [pl.BlockSpec((1, tn), lambda j, i: (0, j))] * n_sums,
        out_shape=[_sds((m, n_out), dt) for dt in out_dtypes] + [_sds((1, n_out), F32)] * n_sums, name=name,
        compiler_params=_params(sem))(a, *([w] * nw), *extras, *[r for r, _ in rowvecs])


def mm_nt(name, g, w, epilogue, out_dtypes, extras=(), rowvecs=(), n_sums=0, tm=512, tk=512):
    m, n = g.shape
    k = w.shape[0]
    tm, tk = min(tm, m), min(tk, k)
    ne, nr, no = len(extras), len(rowvecs), len(out_dtypes)

    def body(g_ref, w_ref, *refs):
        e_refs, r_refs, o_refs, s_refs = refs[:ne], refs[ne:ne + nr], refs[ne + nr:ne + nr + no], refs[ne + nr + no:]
        acc = lax.dot_general(g_ref[...], w_ref[...], (((1,), (1,)), ((), ())), preferred_element_type=F32)
        outs = epilogue(acc, [e[...] for e in e_refs], [r[...] for r in r_refs])
        for o_ref, o in zip(o_refs, outs[:no]):
            o_ref[...] = o.astype(o_ref.dtype)
        if n_sums:
            @pl.when(pl.program_id(0) == 0)
            def _():
                for s_ref in s_refs:
                    s_ref[...] = jnp.zeros_like(s_ref)

            for s_ref, val in zip(s_refs, outs[no:]):
                s_ref[...] += val

    tile = pl.BlockSpec((tm, tk), lambda i, j: (i, j))
    vec = pl.BlockSpec((1, tk), lambda i, j: (0, j))
    sem = ("arbitrary", "parallel") if n_sums else ("parallel", "parallel")
    return pl.pallas_call(
        body, grid=(m // tm, k // tk),
        in_specs=[pl.BlockSpec((tm, n), lambda i, j: (i, 0)), pl.BlockSpec((tk, n), lambda i, j: (j, 0))]
        + [tile] * ne + [vec] * nr,
        out_specs=[tile] * no + [vec] * n_sums,
        out_shape=[_sds((m, k), dt) for dt in out_dtypes] + [_sds((1, k), F32)] * n_sums, name=name,
        compiler_params=_params(sem))(g, w, *extras, *rowvecs)


def mm_tn(name, a, g, tk=512, tn=512):
    m, k = a.shape
    n = g.shape[1]
    tk, tn = min(tk, k), min(tn, n)

    def body(a_ref, g_ref, o_ref):
        acc = lax.dot_general(a_ref[...], g_ref[...], (((0,), (0,)), ((), ())), preferred_element_type=F32)
        o_ref[...] = acc.astype(o_ref.dtype)

    return pl.pallas_call(
        body, grid=(k // tk, n // tn),
        in_specs=[pl.BlockSpec((m, tk), lambda i, j: (0, i)), pl.BlockSpec((m, tn), lambda i, j: (0, j))],
        out_specs=pl.BlockSpec((tk, tn), lambda i, j: (i, j)), out_shape=_sds((k, n), BF16), name=name,
        compiler_params=_params(("parallel", "parallel")))(a, g)


def _row_mask(tc):
    row = lax.broadcasted_iota(jnp.int32, (8 * tc, 256), 0) % 8
    col = lax.broadcasted_iota(jnp.int32, (8 * tc, 256), 1) // 32
    return row == col


def _expand_rows(val, mask):
    tc, width = val.shape
    rep = jnp.broadcast_to(val[:, None, :], (tc, 8, width)).reshape(8 * tc, width)
    return jnp.where(mask, rep, 0.0).astype(BF16)


def _stage(ref, val):
    ref[0] = val[:, 0:128]
    ref[1] = val[:, 128:256]


def _gather_rows(src_ref, tc):
    halves = []
    for half in range(2):
        col = lax.broadcasted_iota(jnp.int32, (tc, 128), 1) // 32 + 4 * half
        out = jnp.zeros((tc, 128), F32)
        for s8 in range(4 * half, 4 * half + 4):
            out = jnp.where(col == s8, src_ref.at[half][pl.ds(s8, tc, stride=8), :], out)
        halves.append(out)
    return jnp.concatenate(halves, axis=1)


def _gelu(x):
    c = math.sqrt(2.0 / math.pi)
    return 0.5 * x * (1.0 + jnp.tanh(c * (x + 0.044715 * x * x * x)))


def _gelu_grad(x):
    c = math.sqrt(2.0 / math.pi)
    t = jnp.tanh(c * (x + 0.044715 * x * x * x))
    return 0.5 * (1.0 + t) + 0.5 * x * (1.0 - t * t) * c * (1.0 + 3.0 * 0.044715 * x * x)


def s5_fwd(x, gain, d_skip, rb, rc, lam_r, lam_i):
    n_rows = x.shape[0]
    tc = min(S5_CHUNK, n_rows)
    nc = n_rows // tc

    def body(x_ref, g_ref, d_ref, rb_ref, rc_ref, lr_ref, li_ref, ge_ref, y2_ref, cs_ref, bux, yrows, carry):
        i = pl.program_id(0)
        u = _rms_hat(x_ref[...])[0] * g_ref[...]

        @pl.when(i == 0)
        def _():
            carry[...] = jnp.zeros_like(carry)

        cs_ref[0] = carry[...]
        mask = _row_mask(tc)
        for blk in range(S5_BLOCKS):
            lhs = _expand_rows(u[:, blk * 256:(blk + 1) * 256], mask)
            bux[blk] = jnp.dot(lhs, rb_ref[blk], preferred_element_type=F32)
        lam = [(lr_ref[blk], li_ref[blk]) for blk in range(S5_BLOCKS)]

        def step(t, c):
            r0 = pl.multiple_of(t * 8, 8)
            new = []
            for blk in range(S5_BLOCKS):
                xr, xi = c[2 * blk], c[2 * blk + 1]
                lr, li = lam[blk]
                nr = lr * xr - li * xi + bux[blk, pl.ds(r0, 8), 0:128]
                ni = lr * xi + li * xr + bux[blk, pl.ds(r0, 8), 128:256]
                bux[blk, pl.ds(r0, 8), 0:128] = nr
                bux[blk, pl.ds(r0, 8), 128:256] = ni
                new += [nr, ni]
            return tuple(new)

        c0 = []
        for blk in range(S5_BLOCKS):
            c0 += [carry[blk, :, 0:128], carry[blk, :, 128:256]]
        cn = lax.fori_loop(0, tc, step, tuple(c0), unroll=4)
        for blk in range(S5_BLOCKS):
            carry[blk, :, 0:128] = cn[2 * blk]
            carry[blk, :, 128:256] = cn[2 * blk + 1]
        for blk in range(S5_BLOCKS):
            _stage(yrows, jnp.dot(bux[blk].astype(BF16), rc_ref[blk], preferred_element_type=F32))
            sl = slice(blk * 256, (blk + 1) * 256)
            y2 = _gather_rows(yrows, tc) + d_ref[:, sl] * u[:, sl]
            y2_ref[:, sl] = y2
            ge_ref[:, sl] = _gelu(y2).astype(BF16)

    row = pl.BlockSpec((tc, D_MODEL), lambda i: (i, 0))
    vec = pl.BlockSpec((1, D_MODEL), lambda i: (0, 0))
    mat = pl.BlockSpec((S5_BLOCKS, 256, 256), lambda i: (0, 0, 0))
    lamspec = pl.BlockSpec((S5_BLOCKS, 8, 128), lambda i: (0, 0, 0))
    return pl.pallas_call(
        body, grid=(nc,),
        in_specs=[row, vec, vec, mat, mat, lamspec, lamspec],
        out_specs=[row, row, pl.BlockSpec((1, S5_BLOCKS, 8, 256), lambda i: (i, 0, 0, 0))],
        out_shape=[_sds((n_rows, D_MODEL), BF16), _sds((n_rows, D_MODEL), F32), _sds((nc, S5_BLOCKS, 8, 256), F32)],
        scratch_shapes=[pltpu.VMEM((S5_BLOCKS, 8 * tc, 256), F32), pltpu.VMEM((2, 8 * tc, 128), F32),
                        pltpu.VMEM((S5_BLOCKS, 8, 256), F32)],
        name="s5_fwd", compiler_params=_params(("arbitrary",)))(x, gain, d_skip, rb, rc, lam_r, lam_i)


def s5_bwd(x, gain, dy2, res, d_skip, cs, rb, rbt, rct, lam_r, lam_i):
    n_rows = x.shape[0]
    tc = min(S5_CHUNK, n_rows)
    nc = n_rows // tc

    def body(x_ref, g_ref, dy_ref, res_ref, d_ref, cs_ref, rb_ref, rbt_ref, rct_ref, lr_ref, li_ref,
             dx_ref, dd_ref, drb_ref, drc_ref, dlr_ref, dli_ref, dg_ref, tmp, du, lhsu, lhsd, xs, adj, acarry):
        i = pl.program_id(0)
        u = _rms_hat(x_ref[...])[0] * g_ref[...]

        @pl.when(i == 0)
        def _():
            acarry[...] = jnp.zeros_like(acarry)
            dd_ref[...] = jnp.zeros_like(dd_ref)
            drb_ref[...] = jnp.zeros_like(drb_ref)
            drc_ref[...] = jnp.zeros_like(drc_ref)
            dlr_ref[...] = jnp.zeros_like(dlr_ref)
            dli_ref[...] = jnp.zeros_like(dli_ref)
            dg_ref[...] = jnp.zeros_like(dg_ref)

        dd_ref[...] += jnp.sum(dy_ref[...] * u, axis=0, keepdims=True)
        mask = _row_mask(tc)
        for blk in range(S5_BLOCKS):
            sl = slice(blk * 256, (blk + 1) * 256)
            lhsu[blk] = _expand_rows(u[:, sl], mask)
            xs[blk] = jnp.dot(lhsu[blk], rb_ref[blk], preferred_element_type=F32)
            lhsd[blk] = _expand_rows(dy_ref[:, sl], mask)
            adj[blk] = jnp.dot(lhsd[blk], rct_ref[blk], preferred_element_type=F32)
        lam = [(lr_ref[blk], li_ref[blk]) for blk in range(S5_BLOCKS)]

        def fstep(t, c):
            r0 = pl.multiple_of(t * 8, 8)
            new = []
            for blk in range(S5_BLOCKS):
                xr, xi = c[2 * blk], c[2 * blk + 1]
                lr, li = lam[blk]
                nr = lr * xr - li * xi + xs[blk, pl.ds(r0, 8), 0:128]
                ni = lr * xi + li * xr + xs[blk, pl.ds(r0, 8), 128:256]
                xs[blk, pl.ds(r0, 8), 0:128] = nr
                xs[blk, pl.ds(r0, 8), 128:256] = ni
                new += [nr, ni]
            return tuple(new)

        c0 = []
        for blk in range(S5_BLOCKS):
            c0 += [cs_ref[0, blk, :, 0:128], cs_ref[0, blk, :, 128:256]]
        lax.fori_loop(0, tc, fstep, tuple(c0), unroll=4)

        def bstep(k, c):
            t = tc - 1 - k
            r0 = pl.multiple_of(t * 8, 8)
            rp = pl.multiple_of(jnp.maximum(t - 1, 0) * 8, 8)
            first = t == 0
            new_a, new_g = [], []
            for blk in range(S5_BLOCKS):
                ar, ai = c[0][2 * blk], c[0][2 * blk + 1]
                glr, gli = c[1][2 * blk], c[1][2 * blk + 1]
                lr, li = lam[blk]
                nr = lr * ar + li * ai + adj[blk, pl.ds(r0, 8), 0:128]
                ni = lr * ai - li * ar + adj[blk, pl.ds(r0, 8), 128:256]
                adj[blk, pl.ds(r0, 8), 0:128] = nr
                adj[blk, pl.ds(r0, 8), 128:256] = ni
                pr = jnp.where(first, cs_ref[0, blk, :, 0:128], xs[blk, pl.ds(rp, 8), 0:128])
                pi = jnp.where(first, cs_ref[0, blk, :, 128:256], xs[blk, pl.ds(rp, 8), 128:256])
                new_a += [nr, ni]
                new_g += [glr + nr * pr + ni * pi, gli + ni * pr - nr * pi]
            return tuple(new_a), tuple(new_g)

        a0, g0 = [], []
        for blk in range(S5_BLOCKS):
            a0 += [acarry[blk, :, 0:128], acarry[blk, :, 128:256]]
            g0 += [dlr_ref[blk], dli_ref[blk]]
        an, gn = lax.fori_loop(0, tc, bstep, (tuple(a0), tuple(g0)), unroll=2)
        for blk in range(S5_BLOCKS):
            acarry[blk, :, 0:128] = an[2 * blk]
            acarry[blk, :, 128:256] = an[2 * blk + 1]
            dlr_ref[blk] = gn[2 * blk]
            dli_ref[blk] = gn[2 * blk + 1]
        for blk in range(S5_BLOCKS):
            sl = slice(blk * 256, (blk + 1) * 256)
            ab = adj[blk].astype(BF16)
            _stage(tmp, jnp.dot(ab, rbt_ref[blk], preferred_element_type=F32))
            du[:, sl] = _gather_rows(tmp, tc) + d_ref[:, sl] * dy_ref[:, sl]
            drb_ref[blk] += lax.dot_general(lhsu[blk], ab, (((0,), (0,)), ((), ())), preferred_element_type=F32)
            drc_ref[blk] += lax.dot_general(lhsd[blk], xs[blk].astype(BF16), (((0,), (0,)), ((), ())),
                                            preferred_element_type=F32)
        xh, r = _rms_hat(x_ref[...])
        dg_ref[...] += jnp.sum(du[...] * xh, axis=0, keepdims=True)
        dxh = du[...] * g_ref[...]
        dx_ref[...] = r * (dxh - xh * jnp.mean(dxh * xh, axis=-1, keepdims=True)) + res_ref[...]

    rev = pl.BlockSpec((tc, D_MODEL), lambda i: (nc - 1 - i, 0))
    vec = pl.BlockSpec((1, D_MODEL), lambda i: (0, 0))
    mat = pl.BlockSpec((S5_BLOCKS, 256, 256), lambda i: (0, 0, 0))
    lamspec = pl.BlockSpec((S5_BLOCKS, 8, 128), lambda i: (0, 0, 0))
    big = pltpu.VMEM((S5_BLOCKS, 8 * tc, 256), F32)
    bigb = pltpu.VMEM((S5_BLOCKS, 8 * tc, 256), BF16)
    return pl.pallas_call(
        body, grid=(nc,),
        in_specs=[rev, vec, rev, rev, vec, pl.BlockSpec((1, S5_BLOCKS, 8, 256), lambda i: (nc - 1 - i, 0, 0, 0)),
                  mat, mat, mat, lamspec, lamspec],
        out_specs=[rev, vec, mat, mat, lamspec, lamspec, vec],
        out_shape=[_sds((n_rows, D_MODEL), F32), _sds((1, D_MODEL), F32), _sds((S5_BLOCKS, 256, 256), F32),
                   _sds((S5_BLOCKS, 256, 256), F32), _sds((S5_BLOCKS, 8, 128), F32), _sds((S5_BLOCKS, 8, 128), F32),
                   _sds((1, D_MODEL), F32)],
        scratch_shapes=[pltpu.VMEM((2, 8 * tc, 128), F32), pltpu.VMEM((tc, D_MODEL), F32), bigb, bigb, big, big,
                        pltpu.VMEM((S5_BLOCKS, 8, 256), F32)],
        name="s5_bwd", compiler_params=_params(("arbitrary",)))(
            x, gain, dy2, res, d_skip, cs, rb, rbt, rct, lam_r, lam_i)


def _s5_views(a_re, a_im, log_dt, b_re, b_im):
    return a_re[:, None, :], a_im[:, None, :], log_dt[:, None, None], jnp.swapaxes(b_re, 1, 2), jnp.swapaxes(b_im, 1, 2)


def _s5_factors(a_re, a_im, log_dt):
    lr, li, dt = jnp.minimum(a_re, LAMBDA_RE_MAX), a_im, jnp.exp(log_dt)
    mag, ang = jnp.exp(lr * dt), li * dt
    lbr, lbi = mag * jnp.cos(ang), mag * jnp.sin(ang)
    den = lr * lr + li * li
    fr, fi = ((lbr - 1.0) * lr + lbi * li) / den, (lbi * lr - (lbr - 1.0) * li) / den
    return lr, li, dt, lbr, lbi, fr, fi, den


def s5_prep(a_re, a_im, log_dt, b_re, b_im, c_re, c_im):
    def body(ar_ref, ai_ref, t_ref, br_ref, bi_ref, cr_ref, ci_ref, rb_ref, rbt_ref, rc_ref, rct_ref, lr_ref, li_ref):
        _, _, _, lbr, lbi, fr, fi, _ = _s5_factors(ar_ref[...], ai_ref[...], t_ref[...])
        lr_ref[...] = lbr
        li_ref[...] = lbi
        bre = fr * br_ref[...] - fi * bi_ref[...]
        bim = fr * bi_ref[...] + fi * br_ref[...]
        even = (lax.broadcasted_iota(jnp.int32, (256, S5_STATE), 0) // S5_GROUP) % 2 == 0

        def assemble(re, im):
            re, im = re.reshape(256, S5_STATE), im.reshape(256, S5_STATE)
            return jnp.concatenate([jnp.where(even, re, 0.0), jnp.where(even, 0.0, re), jnp.where(even, im, 0.0),
                                    jnp.where(even, 0.0, im)], axis=1)

        for blk in range(S5_BLOCKS):
            sl = slice(16 * blk, 16 * blk + 16)
            rb = assemble(bre[sl], bim[sl])
            rct = assemble(cr_ref[sl], -ci_ref[sl])
            rb_ref[blk] = rb.astype(BF16)
            rbt_ref[blk] = rb.T.astype(BF16)
            rct_ref[blk] = rct.astype(BF16)
            rc_ref[blk] = rct.T.astype(BF16)

    vm = pl.BlockSpec(memory_space=pltpu.VMEM)
    mat = _sds((S5_BLOCKS, 256, 256), BF16)
    lam = _sds((S5_GROUPS, 1, S5_STATE), F32)
    rb, rbt, rc, rct, lam_r, lam_i = pl.pallas_call(
        body, in_specs=[vm] * 7, out_specs=[vm] * 6, out_shape=[mat, mat, mat, mat, lam, lam], name="s5_prep",
        compiler_params=_params())(*_s5_views(a_re, a_im, log_dt, b_re, b_im), c_re, c_im)
    return rb, rbt, rc, rct, lam_r.reshape(S5_BLOCKS, 8, 128), lam_i.reshape(S5_BLOCKS, 8, 128)


def s5_param_bwd(mats, lams, a_re, a_im, log_dt, b_re, b_im):
    def body(m_ref, glr_ref, gli_ref, ar_ref, ai_ref, t_ref, br_ref, bi_ref,
             dar_ref, dai_ref, dt_ref, dbr_ref, dbi_ref, dcr_ref, dci_ref):
        lr, li, dt, lbr, lbi, fr, fi, den = _s5_factors(ar_ref[...], ai_ref[...], t_ref[...])
        shape = (S5_GROUPS, S5_GROUP, S5_STATE)
        gbr, gbi = m_ref[0:1024, 0:64].reshape(shape), m_ref[0:1024, 64:128].reshape(shape)
        dcr_ref[...] = m_ref[1024:2048, 0:64].reshape(shape)
        dci_ref[...] = -m_ref[1024:2048, 64:128].reshape(shape)
        br, bi = br_ref[...], bi_ref[...]
        dbr_ref[...] = fr * gbr + fi * gbi
        dbi_ref[...] = fr * gbi - fi * gbr
        dfr = jnp.sum(gbr * br + gbi * bi, axis=1, keepdims=True)
        dfi = jnp.sum(gbi * br - gbr * bi, axis=1, keepdims=True)
        nr, ni = (dfr * lr - dfi * li) / den, (dfr * li + dfi * lr) / den
        qr, qi = (fr * lr + fi * li) / den, (fi * lr - fr * li) / den
        lam_r, lam_i = -(dfr * qr + dfi * qi), -(dfi * qr - dfr * qi)
        gr, gi = glr_ref[...] + nr, gli_ref[...] + ni
        zr, zi = gr * lbr + gi * lbi, gi * lbr - gr * lbi
        a = ar_ref[...]
        dar_ref[...] = (lam_r + zr * dt) * jnp.where(a < LAMBDA_RE_MAX, 1.0, jnp.where(a == LAMBDA_RE_MAX, 0.5, 0.0))
        dai_ref[...] = lam_i + zi * dt
        dt_ref[...] = jnp.sum(zr * lr + zi * li, axis=2, keepdims=True) * dt

    vm = pl.BlockSpec(memory_space=pltpu.VMEM)
    state = _sds((S5_GROUPS, 1, S5_STATE), F32)
    wide = _sds((S5_GROUPS, S5_GROUP, S5_STATE), F32)
    glr = lams[0:32].reshape(S5_GROUPS, 1, S5_STATE)
    gli = lams[32:64].reshape(S5_GROUPS, 1, S5_STATE)
    dar, dai, ddt, dbr, dbi, dcr, dci = pl.pallas_call(
        body, in_specs=[vm] * 8, out_specs=[vm] * 7,
        out_shape=[state, state, _sds((S5_GROUPS, 1, 1), F32), wide, wide, wide, wide], name="s5_param_bwd",
        compiler_params=_params())(mats, glr, gli, *_s5_views(a_re, a_im, log_dt, b_re, b_im))
    return (dar.reshape(S5_GROUPS, S5_STATE), dai.reshape(S5_GROUPS, S5_STATE), ddt.reshape(S5_GROUPS),
            jnp.swapaxes(dbr, 1, 2), jnp.swapaxes(dbi, 1, 2), dcr, dci)


def s5_compact(drb, drct, dlr, dli):
    def body(drb_ref, drct_ref, dlr_ref, dli_ref, o_ref, lam_ref):
        even = (lax.broadcasted_iota(jnp.int32, (256, 64), 0) // S5_GROUP) % 2 == 0
        for blk in range(S5_BLOCKS):
            for k, ref in enumerate((drb_ref, drct_ref)):
                m = ref[blk]
                re = jnp.where(even, m[:, 0:64], m[:, 64:128])
                im = jnp.where(even, m[:, 128:192], m[:, 192:256])
                o_ref[pl.ds(k * 1024 + blk * 256, 256), :] = jnp.concatenate([re, im], axis=1)
            lam_ref[pl.ds(blk * 8, 8), :] = dlr_ref[blk]
            lam_ref[pl.ds(32 + blk * 8, 8), :] = dli_ref[blk]

    vm = pl.BlockSpec(memory_space=pltpu.VMEM)
    return pl.pallas_call(body, in_specs=[vm] * 4, out_specs=[vm, vm], out_shape=[_sds((2048, 128), F32), _sds((64, 128), F32)],
                          name="s5_compact", compiler_params=_params())(drb, drct, dlr, dli)


NEG = -1e30


GROUP = N_Q // N_KV


def _attn_masks(n):
    qi = lax.broadcasted_iota(jnp.int32, (GROUP * BLOCK, BLOCK), 0) % BLOCK
    kj = lax.broadcasted_iota(jnp.int32, (GROUP * BLOCK, BLOCK), 1)
    return jnp.logical_and(kj > qi, n > 0), kj <= qi


def _stack_heads(ref, kh):
    return jnp.concatenate([ref[:, (GROUP * kh + g) * HEAD_DIM:(GROUP * kh + g + 1) * HEAD_DIM] for g in range(GROUP)], axis=0)


def _unstack_heads(val):
    return jnp.concatenate([val[g * BLOCK:(g + 1) * BLOCK] for g in range(GROUP)], axis=1)


def _sink_column(sink_ref, kh):
    grp = lax.broadcasted_iota(jnp.int32, (GROUP * BLOCK, 1), 0) // BLOCK
    col = jnp.zeros((GROUP * BLOCK, 1), F32)
    for g in range(GROUP):
        col = jnp.where(grp == g, sink_ref[GROUP * kh + g], col)
    return col, grp


def _attn_exp(q4, kp, kc, sink, mask_p, mask_c):
    scale = 1.0 / math.sqrt(HEAD_DIM)
    nt = (((1,), (1,)), ((), ()))
    sp = jnp.where(mask_p, lax.dot_general(q4, kp, nt, preferred_element_type=F32) * scale, NEG)
    sc = jnp.where(mask_c, lax.dot_general(q4, kc, nt, preferred_element_type=F32) * scale, NEG)
    m = jnp.maximum(jnp.maximum(jnp.max(sp, axis=-1, keepdims=True), jnp.max(sc, axis=-1, keepdims=True)), sink)
    pp = jnp.exp(sp - m)
    pc = jnp.exp(sc - m)
    ps = jnp.exp(sink - m)
    inv = 1.0 / (jnp.sum(pp, axis=-1, keepdims=True) + jnp.sum(pc, axis=-1, keepdims=True) + ps)
    return pp, pc, ps, inv


def attn_fwd(q, kv, sinks):
    n_rows = q.shape[0]
    nb = n_rows // BLOCK

    def body(sink_ref, q_ref, kvp_ref, kvc_ref, o_ref):
        n = pl.program_id(0)
        mask_p, mask_c = _attn_masks(n)
        outs = []
        for kh in range(N_KV):
            ks, vs = slice(kh * HEAD_DIM, (kh + 1) * HEAD_DIM), slice((N_KV + kh) * HEAD_DIM, (N_KV + kh + 1) * HEAD_DIM)
            sink, _ = _sink_column(sink_ref, kh)
            pp, pc, _, inv = _attn_exp(_stack_heads(q_ref, kh), kvp_ref[:, ks], kvc_ref[:, ks], sink, mask_p, mask_c)
            o4 = (jnp.dot(pp.astype(BF16), kvp_ref[:, vs], preferred_element_type=F32)
                  + jnp.dot(pc.astype(BF16), kvc_ref[:, vs], preferred_element_type=F32)) * inv
            outs.append(_unstack_heads(o4))
        o_ref[...] = jnp.concatenate(outs, axis=1).astype(BF16)

    kvw = 2 * N_KV * HEAD_DIM
    return pl.pallas_call(
        body, grid=(nb,),
        in_specs=[pl.BlockSpec(memory_space=pltpu.SMEM), pl.BlockSpec((BLOCK, D_MODEL), lambda n: (n, 0)),
                  pl.BlockSpec((BLOCK, kvw), lambda n: (jnp.maximum(n - 1, 0), 0)), pl.BlockSpec((BLOCK, kvw), lambda n: (n, 0))],
        out_specs=pl.BlockSpec((BLOCK, D_MODEL), lambda n: (n, 0)), out_shape=_sds((n_rows, D_MODEL), BF16),
        name="attn_fwd", compiler_params=_params(("parallel",)))(sinks, q, kv, kv)


def attn_bwd(q, kv, do, sinks):
    n_rows = q.shape[0]
    nb = n_rows // BLOCK
    kvw = 2 * N_KV * HEAD_DIM
    tn = (((0,), (0,)), ((), ()))
    nt = (((1,), (1,)), ((), ()))
    scale = 1.0 / math.sqrt(HEAD_DIM)

    def body(sink_ref, q_ref, kvp_ref, kvc_ref, do_ref, dq_ref, dbq_ref, dprev_ref, dcur_ref, dsink_ref):
        n = pl.program_id(0)
        mask_p, mask_c = _attn_masks(n)
        lane = lax.broadcasted_iota(jnp.int32, (1, D_MODEL), 1)
        dqs, dsink = [], jnp.zeros((1, D_MODEL), F32)
        dkp, dkc, dvp, dvc = [], [], [], []
        for kh in range(N_KV):
            ks, vs = slice(kh * HEAD_DIM, (kh + 1) * HEAD_DIM), slice((N_KV + kh) * HEAD_DIM, (N_KV + kh + 1) * HEAD_DIM)
            q4, do4 = _stack_heads(q_ref, kh), _stack_heads(do_ref, kh)
            kp, kc, vp, vc = kvp_ref[:, ks], kvc_ref[:, ks], kvp_ref[:, vs], kvc_ref[:, vs]
            sink, grp = _sink_column(sink_ref, kh)
            pp, pc, ps, inv = _attn_exp(q4, kp, kc, sink, mask_p, mask_c)
            pp, pc = pp * inv, pc * inv
            dpp = lax.dot_general(do4, vp, nt, preferred_element_type=F32)
            dpc = lax.dot_general(do4, vc, nt, preferred_element_type=F32)
            delta = jnp.sum(pp * dpp, axis=-1, keepdims=True) + jnp.sum(pc * dpc, axis=-1, keepdims=True)
            dsp = (pp * (dpp - delta) * scale).astype(BF16)
            dsc = (pc * (dpc - delta) * scale).astype(BF16)
            dsk = ps * inv * delta
            for g in range(GROUP):
                dsink = dsink + jnp.where(lane == GROUP * kh + g, -jnp.sum(jnp.where(grp == g, dsk, 0.0)), 0.0)
            dqs.append(_unstack_heads(jnp.dot(dsp, kp, preferred_element_type=F32)
                                      + jnp.dot(dsc, kc, preferred_element_type=F32)))
            dkp.append(lax.dot_general(dsp, q4, tn, preferred_element_type=F32))
            dkc.append(lax.dot_general(dsc, q4, tn, preferred_element_type=F32))
            dvp.append(lax.dot_general(pp.astype(BF16), do4, tn, preferred_element_type=F32))
            dvc.append(lax.dot_general(pc.astype(BF16), do4, tn, preferred_element_type=F32))
        dq = jnp.concatenate(dqs, axis=1)
        dq_ref[...] = dq.astype(BF16)
        dprev_ref[0] = jnp.concatenate(dkp + dvp, axis=1)
        dcur_ref[0] = jnp.concatenate(dkc + dvc, axis=1)

        @pl.when(n == 0)
        def _():
            dbq_ref[...] = jnp.zeros_like(dbq_ref)
            dsink_ref[...] = jnp.zeros_like(dsink_ref)

        dbq_ref[...] += jnp.sum(dq, axis=0, keepdims=True)
        dsink_ref[...] += dsink

    blk = pl.BlockSpec((BLOCK, D_MODEL), lambda n: (n, 0))
    part = pl.BlockSpec((1, BLOCK, kvw), lambda n: (n, 0, 0))
    return pl.pallas_call(
        body, grid=(nb,),
        in_specs=[pl.BlockSpec(memory_space=pltpu.SMEM), blk,
                  pl.BlockSpec((BLOCK, kvw), lambda n: (jnp.maximum(n - 1, 0), 0)), pl.BlockSpec((BLOCK, kvw), lambda n: (n, 0)), blk],
        out_specs=[blk, pl.BlockSpec((1, D_MODEL), lambda n: (0, 0)), part, part, pl.BlockSpec((1, D_MODEL), lambda n: (0, 0))],
        out_shape=[_sds((n_rows, D_MODEL), BF16), _sds((1, D_MODEL), F32), _sds((nb, BLOCK, kvw), F32),
                   _sds((nb, BLOCK, kvw), F32), _sds((1, D_MODEL), F32)],
        name="attn_bwd", compiler_params=_params(("arbitrary",)))(sinks, q, kv, kv, do)


def kv_combine(dprev, dcur):
    nb, _, kvw = dprev.shape

    def body(dcur_ref, dprev_ref, dkv_ref, db_ref):
        total = jnp.zeros((1, kvw), F32)
        for m in range(nb):
            dkv = dcur_ref[m] + dprev_ref[m + 1] if m + 1 < nb else dcur_ref[m]
            dkv_ref[m * BLOCK:(m + 1) * BLOCK, :] = dkv.astype(BF16)
            total = total + jnp.sum(dkv, axis=0, keepdims=True)
        db_ref[...] = jnp.concatenate([total, jnp.zeros((1, D_MODEL - kvw), F32)], axis=1)

    vm = pl.BlockSpec(memory_space=pltpu.VMEM)
    return pl.pallas_call(body, in_specs=[vm, vm], out_specs=[vm, vm],
                          out_shape=[_sds((nb * BLOCK, kvw), BF16), _sds((1, D_MODEL), F32)], name="kv_combine",
                          compiler_params=_params())(dcur, dprev)


def glu_bwd(dout, val, gate, tm=256):
    n_rows, d = dout.shape

    def body(do_ref, v_ref, g_ref, dz_ref, db_ref):
        i = pl.program_id(0)
        sg = jax.nn.sigmoid(g_ref[...])
        dval = do_ref[...] * sg
        dgate = do_ref[...] * v_ref[...] * sg * (1.0 - sg)
        dz_ref[...] = jnp.concatenate([dval, dgate], axis=1).astype(BF16)

        @pl.when(i == 0)
        def _():
            db_ref[...] = jnp.zeros_like(db_ref)

        db_ref[0:1, :] += jnp.sum(dval, axis=0, keepdims=True)
        db_ref[1:2, :] += jnp.sum(dgate, axis=0, keepdims=True)

    row = pl.BlockSpec((tm, d), lambda i: (i, 0))
    return pl.pallas_call(
        body, grid=(n_rows // tm,), in_specs=[row, row, row],
        out_specs=[pl.BlockSpec((tm, 2 * d), lambda i: (i, 0)), pl.BlockSpec((2, d), lambda i: (0, 0))],
        out_shape=[_sds((n_rows, 2 * d), BF16), _sds((2, d), F32)],
        name="glu_bwd", compiler_params=_params(("arbitrary",)))(dout, val, gate)


def _adam_update(w, g, m, v):
    nm = ADAM_B1 * m + (1.0 - ADAM_B1) * g
    nv = ADAM_B2 * v + (1.0 - ADAM_B2) * (g * g)
    m_hat = nm / (1.0 - ADAM_B1 ** ADAM_STEP)
    v_hat = nv / (1.0 - ADAM_B2 ** ADAM_STEP)
    return -ADAM_LR * (m_hat / (jnp.sqrt(v_hat) + ADAM_EPS) + ADAM_WD * w), nm, nv


def adamw(name, w, g, m, v, tm=256):
    n_rows, d = w.shape
    tm = tm if n_rows % tm == 0 else n_rows

    def body(w_ref, g_ref, m_ref, v_ref, go_ref, d_ref, nm_ref, nv_ref):
        gv = g_ref[...]
        go_ref[...] = gv
        d_ref[...], nm_ref[...], nv_ref[...] = _adam_update(w_ref[...], gv, m_ref[...], v_ref[...])

    row = pl.BlockSpec((tm, d), lambda i: (i, 0))
    return pl.pallas_call(
        body, grid=(n_rows // tm,), in_specs=[row] * 4, out_specs=[row] * 4,
        out_shape=[_sds((n_rows, d), F32)] * 4, name=name, compiler_params=_params(("parallel",)))(w, g, m, v)


def adamw_native(name, ws, gs, ms, vs):
    n = len(ws)

    def body(*refs):
        w_refs, g_refs, m_refs, v_refs = refs[:n], refs[n:2 * n], refs[2 * n:3 * n], refs[3 * n:4 * n]
        d_refs, nm_refs, nv_refs = refs[4 * n:5 * n], refs[5 * n:6 * n], refs[6 * n:7 * n]
        for k in range(n):
            dl, nm, nv = _adam_update(w_refs[k][...], g_refs[k][...], m_refs[k][...], v_refs[k][...])
            d_refs[k][...] = dl
            nm_refs[k][...] = nm
            nv_refs[k][...] = nv

    vm = pl.BlockSpec(memory_space=pltpu.VMEM)
    shapes = [_sds(w.shape, F32) for w in ws]
    out = pl.pallas_call(body, in_specs=[vm] * (4 * n), out_specs=[vm] * (3 * n), out_shape=shapes * 3, name=name,
                         compiler_params=_params())(*ws, *gs, *ms, *vs)
    return list(out[:n]), list(out[n:2 * n]), list(out[2 * n:])


VEC_ROWS = {"norm_mix": 0, "norm_mlp": 2, "norm_kv": 4, "norm_final": 5, "s5_d": 6, "b_q": 7, "b_o": 8, "s5_b_glu": 9,
            "b_kv": 11, "sinks": 12, "loss": 13}


def split_vectors(where, vecs, d_shard, glu_shard):
    kvw = 2 * N_KV * HEAD_DIM
    shapes = {"norm_mix": (2, D_MODEL), "norm_mlp": (2, D_MODEL), "norm_kv": (1, D_MODEL), "norm_final": (1, D_MODEL),
              "s5_d": (1, d_shard), "b_q": (1, D_MODEL), "b_o": (1, D_MODEL), "s5_b_glu": (1, glu_shard), "b_kv": (1, kvw),
              "sinks": (1, N_Q), "loss": (1, 128)}
    names = list(shapes)

    def body(where_ref, v_ref, *o_refs):
        chip = where_ref[1]
        for name, o_ref in zip(names, o_refs):
            r0, (r, n) = VEC_ROWS[name], shapes[name]
            if name == "s5_d":
                g = jnp.zeros((1, n), F32)
                for j in range(4):
                    g = jnp.where(chip == j, v_ref[r0:r0 + 1, j * n:(j + 1) * n], g)
            elif name == "s5_b_glu":
                g = jnp.zeros((1, n), F32)
                for j in range(4):
                    row, col = r0 + (j * n) // D_MODEL, (j * n) % D_MODEL
                    g = jnp.where(chip == j, v_ref[row:row + 1, col:col + n], g)
            else:
                g = v_ref[r0:r0 + r, 0:n]
            o_ref[...] = g

    vm = pl.BlockSpec(memory_space=pltpu.VMEM)
    out = pl.pallas_call(body, in_specs=[pl.BlockSpec(memory_space=pltpu.SMEM), vm], out_specs=[vm] * len(names),
                         out_shape=[_sds(shapes[n], F32) for n in names], name="split_vectors",
                         compiler_params=_params())(where, vecs)
    return dict(zip(names, out))


def _position():
    x, y, c = lax.axis_index("x"), lax.axis_index("y"), lax.axis_index("c")
    others = [(1 - x, y), (x, 1 - y), (1 - x, 1 - y)]
    return x, y, c, others


def _window(ref, kind, chip, half, shard_shape):
    r, n = shard_shape
    if kind == "col":
        return ref.at[pl.ds(pl.multiple_of(half * (r // 2), 16), r // 2), pl.ds(pl.multiple_of(chip * n, 128), n)]
    return ref.at[pl.ds(pl.multiple_of(chip * r, 16), r), pl.ds(pl.multiple_of(half * (n // 2), 128), n // 2)]


def _half(ref, kind, half, shape):
    r, n = shape
    if kind == "col":
        return ref.at[pl.ds(pl.multiple_of(half * (r // 2), 16), r // 2), :]
    return ref.at[:, pl.ds(pl.multiple_of(half * (n // 2), 128), n // 2)]


def swap_start(name, grads, kinds):
    nt = len(grads)
    shapes = [tuple(g.shape) for g in grads]
    lands = [lax.empty(sh, BF16) for sh in shapes]

    def body(*refs):
        in_refs, land_refs = refs[:nt], refs[nt:2 * nt]
        send_sems, recv_sems, token = refs[2 * nt], refs[2 * nt + 1], refs[-1]
        x, y, c, _ = _position()
        for t in range(nt):
            pltpu.make_async_remote_copy(
                src_ref=_half(in_refs[t], kinds[t], 1 - c, shapes[t]), dst_ref=_half(land_refs[t], kinds[t], 1 - c, shapes[t]),
                send_sem=send_sems.at[t], recv_sem=recv_sems.at[t], device_id=(x, y, 1 - c), device_id_type=MESH).start()
        token[...] = jnp.zeros_like(token)

    sems = pltpu.SemaphoreType.DMA((nt,))
    both = list(grads) + lands
    out = pl.pallas_call(
        body, name=name, in_specs=[HBM_SPEC] * (2 * nt),
        out_specs=(SEM_SPEC, SEM_SPEC, *[HBM_SPEC] * (2 * nt), pl.BlockSpec(memory_space=pltpu.VMEM)),
        out_shape=(sems, sems, *[pltpu.HBM(a.shape, a.dtype) for a in both], _sds((8, 128), F32)),
        input_output_aliases={t: 2 + t for t in range(2 * nt)}, compiler_params=_split_params(),
    )(*[_in_hbm(a) for a in both])
    return out[0], out[1], list(out[2:2 + nt]), list(out[2 + nt:2 + 2 * nt]), out[-1]


def swap_wait(name, send_sems, recv_sems, grads, lands, kinds, after):
    nt = len(grads)
    shapes = [tuple(g.shape) for g in grads]

    def body(*refs):
        in_refs, land_refs = refs[:nt], refs[nt:2 * nt]
        send_ref, recv_ref = refs[2 * nt], refs[2 * nt + 1]
        x, y, c, _ = _position()
        for t in range(nt):
            cp = pltpu.make_async_remote_copy(
                src_ref=_half(in_refs[t], kinds[t], 1 - c, shapes[t]), dst_ref=_half(land_refs[t], kinds[t], c, shapes[t]),
                send_sem=send_ref.at[t], recv_sem=recv_ref.at[t], device_id=(x, y, 1 - c), device_id_type=MESH)
            cp.wait_send()
            cp.wait_recv()

    both = list(grads) + list(lands)
    out = pl.pallas_call(
        body, name=name, in_specs=[HBM_SPEC] * (2 * nt) + [SEM_SPEC, SEM_SPEC, HBM_SPEC], out_specs=[HBM_SPEC] * (2 * nt),
        out_shape=[pltpu.HBM(a.shape, a.dtype) for a in both], input_output_aliases={t: t for t in range(2 * nt)},
        compiler_params=_split_params())(*both, send_sems, recv_sems, _in_hbm(after))
    return list(out[:nt]), list(out[nt:])


def _half_spec(kind, shape, tiles):
    r, n = shape
    if kind == "col":
        tn = n // tiles
        return pl.BlockSpec((r // 2, tn), lambda i, s: (s[0], i))
    tm = r // tiles
    return pl.BlockSpec((tm, n // 2), lambda i, s: (i, s[0]))


def add_halves(name, mine, landed, kinds, where, tiles=4):
    nt = len(mine)
    shapes = [tuple(a.shape) for a in mine]

    def compact(t):
        r, n = shapes[t]
        if kinds[t] == "col":
            return (r // 2, n), pl.BlockSpec((r // 2, n // tiles), lambda i, s: (0, i))
        return (r, n // 2), pl.BlockSpec((r // tiles, n // 2), lambda i, s: (i, 0))

    def body(s_ref, *refs):
        for a_ref, b_ref, o_ref in zip(refs[:nt], refs[nt:2 * nt], refs[2 * nt:]):
            o_ref[...] = (a_ref[...].astype(F32) + b_ref[...].astype(F32)).astype(BF16)

    specs = [_half_spec(kinds[t], shapes[t], tiles) for t in range(nt)]
    return pl.pallas_call(
        body, grid_spec=pltpu.PrefetchScalarGridSpec(num_scalar_prefetch=1, grid=(tiles,), in_specs=specs + specs,
                                                     out_specs=[compact(t)[1] for t in range(nt)]),
        out_shape=[_sds(compact(t)[0], BF16) for t in range(nt)], name=name,
        compiler_params=_params(("parallel",)))(where, *mine, *landed)


def sum_shards(name, parts, landed, kinds, shard_shapes, where, layers, n_layers, intos, tiles=2):
    nt = len(parts)
    in_specs, out_specs = [], []
    for t in range(nt):
        (r, n), layer = shard_shapes[t], layers[t]
        if kinds[t] == "col":
            tm, width = r // 2 // tiles, n
            own = pl.BlockSpec((tm, n), lambda i, s: (i, s[1]))
            out = pl.BlockSpec((None, tm, n), lambda i, s, layer=layer: (layer, s[0] * tiles + i, 0))
        else:
            tm, width = r // tiles, n // 2
            own = pl.BlockSpec((tm, n // 2), lambda i, s: (s[1] * tiles + i, 0))
            out = pl.BlockSpec((None, tm, n // 2), lambda i, s, layer=layer: (layer, i, s[0]))
        in_specs += [own, pl.BlockSpec((3, tm, width), lambda i, s: (0, i, 0))]
        out_specs.append(out)
    args, aliases = [where] + [a for pair in zip(parts, landed) for a in pair], {}
    for t in range(nt):
        if intos[t] is not None:
            aliases[len(args)] = t
            in_specs.append(pl.BlockSpec(memory_space=pl.ANY))
            args.append(intos[t])

    def body(s_ref, *refs):
        for t in range(nt):
            a_ref, l_ref, o_ref = refs[2 * t], refs[2 * t + 1], refs[len(in_specs) + t]
            o_ref[...] = ((a_ref[...].astype(F32) + l_ref[0].astype(F32)) + l_ref[1].astype(F32)) + l_ref[2].astype(F32)

    return pl.pallas_call(
        body, grid_spec=pltpu.PrefetchScalarGridSpec(num_scalar_prefetch=1, grid=(tiles,), in_specs=in_specs,
                                                     out_specs=out_specs),
        out_shape=[_sds((n_layers[t],) + tuple(shard_shapes[t]), F32) for t in range(nt)], input_output_aliases=aliases,
        name=name, compiler_params=_params(("parallel",)))(*args)


def share_halves(arrays, entries):
    na, nt = len(arrays), len(entries)

    def body(*refs):
        out_refs = refs[na:2 * na]
        send_sems, recv_sems = refs[2 * na:]
        x, y, c, _ = _position()
        cps = []
        for t, (a, layer, kind) in enumerate(entries):
            shape = tuple(arrays[a].shape[1:])
            mine = _half(out_refs[a].at[layer], kind, c, shape)
            cp = pltpu.make_async_remote_copy(
                src_ref=mine, dst_ref=mine, send_sem=send_sems.at[t], recv_sem=recv_sems.at[t],
                device_id=(x, y, 1 - c), device_id_type=MESH)
            cp.start()
            cps.append(cp)
        for t, (a, layer, kind) in enumerate(entries):
            shape = tuple(arrays[a].shape[1:])
            other = _half(out_refs[a].at[layer], kind, 1 - c, shape)
            pltpu.make_async_remote_copy(
                src_ref=other, dst_ref=other, send_sem=send_sems.at[t], recv_sem=recv_sems.at[t],
                device_id=(x, y, 1 - c), device_id_type=MESH).wait_recv()
        for cp in cps:
            cp.wait_send()

    hbm = pl.BlockSpec(memory_space=pl.ANY)
    return pl.pallas_call(
        body, in_specs=[hbm] * na, out_specs=[hbm] * na, out_shape=[_sds(a.shape, F32) for a in arrays],
        input_output_aliases={i: i for i in range(na)},
        scratch_shapes=[pltpu.SemaphoreType.DMA((nt,)), pltpu.SemaphoreType.DMA((nt,))],
        name="share_halves", compiler_params=_params())(*arrays)


HBM_SPEC = pl.BlockSpec(memory_space=pltpu.HBM)
SEM_SPEC = pl.BlockSpec(memory_space=pltpu.SEMAPHORE)
ANY_SPEC = pl.BlockSpec(memory_space=pl.ANY)


def _split_params():
    return pltpu.CompilerParams(has_side_effects=pltpu.SideEffectType.DATAFLOW_SIDE_EFFECTING,
                                vmem_limit_bytes=VMEM_LIMIT_BYTES)


def _in_hbm(a):
    return pltpu.with_memory_space_constraint(a, pltpu.HBM)


def cast_place(arrays, entries, where, tiles=2):
    in_specs, out_specs, fulls = [], [], []
    for a, layer, kind in entries:
        _, r, n = arrays[a].shape
        tm = r // tiles
        in_specs.append(pl.BlockSpec((None, tm, n), lambda i, s, layer=layer: (layer, i, 0)))
        if kind == "col":
            fulls.append((r, 4 * n))
            out_specs.append(pl.BlockSpec((tm, n), lambda i, s: (i, s[1])))
        else:
            fulls.append((4 * r, n))
            out_specs.append(pl.BlockSpec((tm, n), lambda i, s: (s[1] * tiles + i, 0)))
    nt = len(entries)

    def body(s_ref, *refs):
        for w_ref, o_ref in zip(refs[:nt], refs[nt:]):
            o_ref[...] = w_ref[...].astype(BF16)

    return pl.pallas_call(
        body, grid_spec=pltpu.PrefetchScalarGridSpec(num_scalar_prefetch=1, grid=(tiles,), in_specs=in_specs,
                                                     out_specs=out_specs),
        out_shape=[_sds(f, BF16) for f in fulls], name="cast_place",
        compiler_params=_params(("parallel",)))(where, *[arrays[a] for a, _, _ in entries])


def gather_start(fulls, kinds, shard_shapes, after):
    nt = len(fulls)
    na = 0 if after is None else 1

    def body(*refs):
        full_refs = refs[:nt]
        send_sems, recv_sems, token = refs[nt + na], refs[nt + na + 1], refs[-1]
        x, y, c, others = _position()
        for t in range(nt):
            mine = _window(full_refs[t], kinds[t], 2 * x + y, c, shard_shapes[t])
            for j, (ox, oy) in enumerate(others):
                pltpu.make_async_remote_copy(
                    src_ref=mine, dst_ref=mine, send_sem=send_sems.at[3 * t + j], recv_sem=recv_sems.at[3 * t + j],
                    device_id=(ox, oy, c), device_id_type=MESH).start()
        token[...] = jnp.zeros_like(token)

    sems = pltpu.SemaphoreType.DMA((3 * nt,))
    out = pl.pallas_call(
        body, name="gather_start", in_specs=[HBM_SPEC] * nt + [ANY_SPEC] * na,
        out_specs=(SEM_SPEC, SEM_SPEC, *[HBM_SPEC] * nt, pl.BlockSpec(memory_space=pltpu.VMEM)),
        out_shape=(sems, sems, *[pltpu.HBM(f.shape, f.dtype) for f in fulls], _sds((8, 128), F32)),
        input_output_aliases={t: 2 + t for t in range(nt)}, compiler_params=_split_params(),
    )(*[_in_hbm(f) for f in fulls], *([] if after is None else [after]))
    return out[0], out[1], list(out[2:2 + nt]), out[-1]


def gather_wait(name, send_sems, recv_sems, fulls, kinds, shard_shapes, after, first):
    nt = len(fulls)

    def body(*refs):
        full_refs, send_ref, recv_ref = refs[:nt], refs[nt], refs[nt + 1]
        x, y, c, others = _position()
        for t in range(nt):
            mine = _window(full_refs[t], kinds[t], 2 * x + y, c, shard_shapes[t])
            for j, (ox, oy) in enumerate(others):
                cp = pltpu.make_async_remote_copy(
                    src_ref=mine, dst_ref=_window(full_refs[t], kinds[t], 2 * ox + oy, c, shard_shapes[t]),
                    send_sem=send_ref.at[3 * (first + t) + j], recv_sem=recv_ref.at[3 * (first + t) + j],
                    device_id=(ox, oy, c), device_id_type=MESH)
                cp.wait_send()
                cp.wait_recv()

    out = pl.pallas_call(
        body, name=name, in_specs=[HBM_SPEC] * nt + [SEM_SPEC, SEM_SPEC, HBM_SPEC], out_specs=[HBM_SPEC] * nt,
        out_shape=[pltpu.HBM(f.shape, f.dtype) for f in fulls], input_output_aliases={t: t for t in range(nt)},
        compiler_params=_split_params())(*fulls, send_sems, recv_sems, _in_hbm(after))
    return list(out)


def forward_halves(name, fulls, kinds, shard_shapes):
    nt = len(fulls)

    def body(*refs):
        out_refs = refs[nt:2 * nt]
        send_sems, recv_sems = refs[2 * nt:]
        x, y, c, others = _position()
        cps = []
        for t in range(nt):
            for j, (ox, oy) in enumerate(others):
                landed = _window(out_refs[t], kinds[t], 2 * ox + oy, c, shard_shapes[t])
                cp = pltpu.make_async_remote_copy(
                    src_ref=landed, dst_ref=landed, send_sem=send_sems.at[3 * t + j], recv_sem=recv_sems.at[3 * t + j],
                    device_id=(x, y, 1 - c), device_id_type=MESH)
                cp.start()
                cps.append(cp)
        for t in range(nt):
            for j, (ox, oy) in enumerate(others):
                got = _window(out_refs[t], kinds[t], 2 * ox + oy, 1 - c, shard_shapes[t])
                pltpu.make_async_remote_copy(
                    src_ref=got, dst_ref=got, send_sem=send_sems.at[3 * t + j], recv_sem=recv_sems.at[3 * t + j],
                    device_id=(x, y, 1 - c), device_id_type=MESH).wait_recv()
        for cp in cps:
            cp.wait_send()

    out = pl.pallas_call(
        body, in_specs=[ANY_SPEC] * nt, out_specs=[ANY_SPEC] * nt, out_shape=[_sds(f.shape, f.dtype) for f in fulls],
        input_output_aliases={t: t for t in range(nt)},
        scratch_shapes=[pltpu.SemaphoreType.DMA((3 * nt,)), pltpu.SemaphoreType.DMA((3 * nt,))],
        name=name, compiler_params=_params())(*fulls)
    return list(out)


def _piece(ref, kind, chip, shard_shape):
    r, n = shard_shape
    if kind == "col":
        return ref.at[:, pl.ds(pl.multiple_of(chip * n, 128), n)]
    return ref.at[pl.ds(pl.multiple_of(chip * r, 16), r), :]


def _piece_shape(kind, shard_shape):
    r, n = shard_shape
    return (r // 2, n) if kind == "col" else (r, n // 2)


def exchange_start(name, parts, kinds, shard_shapes):
    nt = len(parts)
    lands = [lax.empty((3,) + _piece_shape(kinds[t], shard_shapes[t]), BF16) for t in range(nt)]

    def body(*refs):
        part_refs, land_refs = refs[:nt], refs[nt:2 * nt]
        send_sems, recv_sems, token = refs[2 * nt], refs[2 * nt + 1], refs[-1]
        x, y, c, others = _position()
        for t in range(nt):
            for j, (ox, oy) in enumerate(others):
                pltpu.make_async_remote_copy(
                    src_ref=_piece(part_refs[t], kinds[t], 2 * ox + oy, shard_shapes[t]), dst_ref=land_refs[t].at[j],
                    send_sem=send_sems.at[3 * t + j], recv_sem=recv_sems.at[3 * t + j],
                    device_id=(ox, oy, c), device_id_type=MESH).start()
        token[...] = jnp.zeros_like(token)

    sems = pltpu.SemaphoreType.DMA((3 * nt,))
    both = list(parts) + lands
    out = pl.pallas_call(
        body, name=name, in_specs=[HBM_SPEC] * (2 * nt),
        out_specs=(SEM_SPEC, SEM_SPEC, *[HBM_SPEC] * (2 * nt), pl.BlockSpec(memory_space=pltpu.VMEM)),
        out_shape=(sems, sems, *[pltpu.HBM(a.shape, a.dtype) for a in both], _sds((8, 128), F32)),
        input_output_aliases={t: 2 + t for t in range(2 * nt)}, compiler_params=_split_params(),
    )(*[_in_hbm(a) for a in both])
    return out[0], out[1], list(out[2:2 + nt]), list(out[2 + nt:2 + 2 * nt]), out[-1]


def exchange_wait(name, send_sems, recv_sems, parts, lands, kinds, shard_shapes, after):
    nt = len(parts)

    def body(*refs):
        part_refs, land_refs = refs[:nt], refs[nt:2 * nt]
        send_ref, recv_ref = refs[2 * nt], refs[2 * nt + 1]
        x, y, c, others = _position()
        for t in range(nt):
            for j, (ox, oy) in enumerate(others):
                cp = pltpu.make_async_remote_copy(
                    src_ref=_piece(part_refs[t], kinds[t], 2 * ox + oy, shard_shapes[t]), dst_ref=land_refs[t].at[j],
                    send_sem=send_ref.at[3 * t + j], recv_sem=recv_ref.at[3 * t + j],
                    device_id=(ox, oy, c), device_id_type=MESH)
                cp.wait_send()
                cp.wait_recv()

    both = list(parts) + list(lands)
    out = pl.pallas_call(
        body, name=name, in_specs=[HBM_SPEC] * (2 * nt) + [SEM_SPEC, SEM_SPEC, HBM_SPEC], out_specs=[HBM_SPEC] * (2 * nt),
        out_shape=[pltpu.HBM(a.shape, a.dtype) for a in both], input_output_aliases={t: t for t in range(2 * nt)},
        compiler_params=_split_params())(*both, send_sems, recv_sems, _in_hbm(after))
    return list(out[:nt]), list(out[nt:])


def gather_rows(name, mine):
    def body(in_ref, out_ref, send_sems, recv_sems):
        x, y, c, others = _position()
        me = 2 * x + y
        out_ref[me] = in_ref[...]
        cps = []
        for j, (ox, oy) in enumerate(others):
            cp = pltpu.make_async_remote_copy(
                src_ref=in_ref, dst_ref=out_ref.at[me], send_sem=send_sems.at[j], recv_sem=recv_sems.at[j],
                device_id=(ox, oy, c), device_id_type=MESH)
            cp.start()
            cps.append(cp)
        for j, (ox, oy) in enumerate(others):
            pltpu.make_async_remote_copy(
                src_ref=in_ref, dst_ref=out_ref.at[2 * ox + oy], send_sem=send_sems.at[j], recv_sem=recv_sems.at[j],
                device_id=(ox, oy, c), device_id_type=MESH).wait_recv()
        for cp in cps:
            cp.wait_send()

    vm = pl.BlockSpec(memory_space=pltpu.VMEM)
    return pl.pallas_call(
        body, in_specs=[vm], out_specs=vm, out_shape=_sds((4,) + tuple(mine.shape), F32),
        scratch_shapes=[pltpu.SemaphoreType.DMA((3,)), pltpu.SemaphoreType.DMA((3,))],
        name=name, compiler_params=_params())(mine)


def all_reduce_small(name, bufs, wire):
    n = len(bufs)
    halves = [b.shape[0] // 2 for b in bufs]

    def body(*refs):
        in_refs, out_refs, lands, txs = refs[:n], refs[n:2 * n], refs[2 * n:3 * n], refs[3 * n:4 * n]
        send_sems, recv_sems = refs[4 * n:]
        x, y, c, _ = _position()
        mine = [pl.ds(pl.multiple_of(c * h, 8), h) for h in halves]
        other = [pl.ds(pl.multiple_of((1 - c) * h, 8), h) for h in halves]
        for s, peer in enumerate([(x, y, 1 - c), (1 - x, y, c), (x, 1 - y, c)]):
            cps = []
            for k in range(n):
                txs[k][...] = (in_refs[k][other[k], :] if s == 0 else out_refs[k][mine[k], :]).astype(wire[k])
                cp = pltpu.make_async_remote_copy(
                    src_ref=txs[k], dst_ref=lands[k].at[s], send_sem=send_sems.at[4 * k + s], recv_sem=recv_sems.at[4 * k + s],
                    device_id=peer, device_id_type=MESH)
                cp.start()
                cps.append(cp)
            for k, cp in enumerate(cps):
                cp.wait()
                own = in_refs[k][mine[k], :] if s == 0 else out_refs[k][mine[k], :]
                out_refs[k][mine[k], :] = own.astype(wire[k]).astype(F32) + lands[k][s].astype(F32)
        cps = []
        for k in range(n):
            cp = pltpu.make_async_remote_copy(
                src_ref=out_refs[k].at[mine[k]], dst_ref=out_refs[k].at[mine[k]], send_sem=send_sems.at[4 * k + 3],
                recv_sem=recv_sems.at[4 * k + 3], device_id=(x, y, 1 - c), device_id_type=MESH)
            cp.start()
            cps.append(cp)
        for cp in cps:
            cp.wait()

    vm = pl.BlockSpec(memory_space=pltpu.VMEM)
    out = pl.pallas_call(
        body, in_specs=[vm] * n, out_specs=[vm] * n, out_shape=[_sds(b.shape, F32) for b in bufs],
        scratch_shapes=[pltpu.VMEM((3, h, b.shape[1]), w) for h, b, w in zip(halves, bufs, wire)]
        + [pltpu.VMEM((h, b.shape[1]), w) for h, b, w in zip(halves, bufs, wire)]
        + [pltpu.SemaphoreType.DMA((4 * n,)), pltpu.SemaphoreType.DMA((4 * n,))],
        name=name, compiler_params=_params())(*bufs)
    return list(out)


def _local_step(x, target, small, need, emit_swap, emit_exchange):
    d = D_MODEL
    full = {}

    def after_token(vec, token):
        return vec if token is None else vec + token[0:1, 0:1]

    def token_rows(token, width):
        return [] if token is None else [after_token(jnp.zeros((1, width), F32), token)]

    def plus(acc, rows):
        return acc + rows[0] if rows else acc

    rb16, rbt16, rc16, rct16, lr_t, li_t = small["s5_operands"]
    ge, y2, cs = s5_fwd(x, small["norm_mix0"], small["s5_d"], rb16, rc16, lr_t, li_t)
    full.update(need("glu", ge))

    def norm_rows(h, gains):
        xh, _ = _rms_hat(h)
        return [xh * g for g in gains]

    def glu_epilogue(accs, e, r):
        v, gt = accs[0] + r[0], accs[1] + r[1]
        h = e[0] + v * jax.nn.sigmoid(gt)
        return [h, v, gt] + norm_rows(h, r[2:])

    h1, val, gate, n1 = mm_nn(
        "glu", ge, full["w_glu"], [0, d], d, glu_epilogue, [F32, F32, F32, BF16], extras=[x],
        rowvecs=[(small["s5_b_glu"], 0), (small["s5_b_glu"], d), (small["norm_mlp0"], 0)], tm=512, tn=d)

    def mlp_fwd(tag, h, n, w_in, get_w_out, next_gains, head=None):
        def in_epilogue(accs, e, rv):
            pos = jnp.maximum(accs[0], 0.0)
            return [pos * pos, 2.0 * pos]

        r, slope = mm_nn("mlp_in" + tag, n, w_in, [0], w_in.shape[1], in_epilogue, [BF16, BF16], tm=2048)
        w_out = get_w_out(r)

        def epilogue(accs, e, rv):
            h_out = e[0] + accs[0]
            return [h_out] + norm_rows(h_out, rv)

        if head is not None:
            return head(r, w_out, h), (n, r, slope)
        outs = mm_nn("mlp_out" + tag, r, w_out, [0], d, epilogue, [F32] + [BF16] * len(next_gains), extras=[h],
                     rowvecs=[(g, 0) for g in next_gains], tm=512, tn=d)
        return outs[0], outs[1:], (n, r, slope)

    full.update(need("mlp_in0", h1))

    def w_out0(after):
        full.update(need("mlp_out0", after))
        return full["w_out0"]

    h2, (nkv, n2), mlp0 = mlp_fwd("0", h1, n1, full["w_in0"], w_out0, [small["norm_kv"], small["norm_mix1"]])

    full.update(need("attn", h2))
    kvw = 2 * N_KV * HEAD_DIM
    (kv,) = mm_nn("kv_proj", nkv, full["w_kv"], [0], kvw, lambda accs, e, r: [accs[0] + r[0]], [BF16],
                  rowvecs=[(small["b_kv"], 0)], tm=2048)
    (q,) = mm_nn("q_proj", n2, full["w_q"], [0], d, lambda accs, e, r: [accs[0] + r[0]], [BF16],
                 rowvecs=[(small["b_q"], 0)], tm=2048)
    sinks = small["sinks"].reshape(N_Q)
    o = attn_fwd(q, kv, sinks)
    def o_epilogue(accs, e, r):
        h_out = e[0] + accs[0] + r[0]
        return [h_out] + norm_rows(h_out, r[1:])

    h3, n3 = mm_nn("o_proj", o, full["w_o"], [0], d, o_epilogue, [F32, BF16], extras=[h2],
                   rowvecs=[(small["b_o"], 0), (small["norm_mlp1"], 0)], tm=512, tn=d)
    full.update(need("mlp_in1", h3))

    def w_out1(after):
        full.update(need("mlp_out1", after))
        return full["w_out1"]

    def loss_head(r, w_out, h):
        def epilogue(accs, e, rv):
            xh, rr = _rms_hat(e[0] + accs[0])
            err = xh * rv[0] - e[1]
            dy = err * (1.0 / d)
            dxh = dy * rv[0]
            dx = rr * (dxh - xh * jnp.mean(dxh * xh, axis=-1, keepdims=True))
            loss = jnp.full((1, d), 0.5 * jnp.sum(jnp.mean(err * err, axis=-1, keepdims=True)), F32)
            return [dx, dx, loss, jnp.sum(dy * xh, axis=0, keepdims=True)]

        return mm_nn("mlp_out1", r, w_out, [0], d, epilogue, [F32, BF16], extras=[h, target],
                     rowvecs=[(small["norm_final"], 0)], n_sums=2, tm=512, tn=d)

    (dh, dhb, loss_tile, dg_final), mlp1 = mlp_fwd("1", h3, n3, full["w_in1"], w_out1, [], head=loss_head)

    grads_small, grads_full = {"norm_final": dg_final}, {}
    ident = lambda acc, e, r: [plus(acc, r)]
    layer1 = ["w_out1", "w_in1", "w_o", "w_q", "w_kv"]
    layer0 = ["w_out0", "w_in0", "w_glu"]

    def norm_bwd_rows(x_rows, res, dys, gains):
        xh, r = _rms_hat(x_rows)
        dxh = sum(dy * g for dy, g in zip(dys, gains))
        dx = r * (dxh - xh * jnp.mean(dxh * xh, axis=-1, keepdims=True)) + res
        return dx, [jnp.sum(dy * xh, axis=0, keepdims=True) for dy in dys]

    def mlp_bwd(tag, dh, dhb, h_in, gain, w_in, w_out, saved, token=None):
        n, r, slope = saved
        grads_full["w_out" + tag] = mm_tn("dw_out" + tag, r, dhb, tn=1024)
        (da,) = mm_nt("mlp_da" + tag, dhb, w_out, lambda acc, e, rv: [plus(acc * e[0].astype(F32), rv)], [BF16],
                      extras=[slope], rowvecs=token_rows(token, w_out.shape[0]), tm=2048)
        grads_full["w_in" + tag] = mm_tn("dw_in" + tag, n, da, tn=1024)

        def epilogue(acc, e, rv):
            dx, dgs = norm_bwd_rows(e[0], e[1], [acc], rv)
            return [dx, dx, jnp.sum(dx, axis=0, keepdims=True)] + dgs

        dx, dxb, colsum, dg = mm_nt("mlp_dn" + tag, da, w_in, epilogue, [F32, BF16], extras=[h_in, dh], rowvecs=[gain],
                                    n_sums=2, tm=512, tk=d)
        grads_small["norm_mlp" + tag] = dg
        return dx, dxb, colsum

    dh3, dh3b, colsum3 = mlp_bwd("1", dh, dhb, h3, small["norm_mlp1"], full["w_in1"], full["w_out1"], mlp1)
    grads_small["b_o"] = colsum3
    grads_full["w_o"] = mm_tn("dw_o", o, dh3b, tn=1024)
    (do,) = mm_nt("attn_do", dh3b, full["w_o"], ident, [BF16], tm=2048)
    dq, dbq, dprev, dcur, dsink = attn_bwd(q, kv, do, sinks)
    dkv, dbkv = kv_combine(dprev, dcur)
    grads_small["b_q"], grads_small["b_kv"], grads_small["sinks"] = dbq, dbkv, dsink
    grads_full["w_q"] = mm_tn("dw_q", n2, dq, tn=1024)
    grads_full["w_kv"] = mm_tn("dw_kv", nkv, dkv, tk=1024)
    token = emit_swap("layer1", {n: grads_full[n] for n in layer1})
    (dnkv,) = mm_nt("kv_dn", dkv, full["w_kv"], ident, [F32], rowvecs=token_rows(token, d), tm=2048, tk=1024)

    def attn_dn_epilogue(acc, e, rv):
        dx, dgs = norm_bwd_rows(e[0], e[1], [acc, e[2]], rv)
        return [dx, dx] + dgs

    dh2, dh2b, dg_mix1, dg_kv = mm_nt("attn_dn", dq, full["w_q"], attn_dn_epilogue, [F32, BF16], extras=[h2, dh3, dnkv],
                                      rowvecs=[small["norm_mix1"], small["norm_kv"]], n_sums=2, tm=512, tk=d)
    grads_small["norm_mix1"], grads_small["norm_kv"] = dg_mix1, dg_kv
    token = emit_exchange("layer1", dh2b)
    dh1, _, _ = mlp_bwd("0", dh2, dh2b, h1, small["norm_mlp0"], full["w_in0"], full["w_out0"], mlp0, token)

    dz, db_glu = glu_bwd(dh1, val, gate)
    grads_small["s5_b_glu"] = db_glu
    grads_full["w_glu"] = mm_tn("dw_glu", ge, dz, tn=1024)
    token = emit_swap("layer0", {n: grads_full[n] for n in layer0})
    (dy2,) = mm_nt("glu_dy", dz, full["w_glu"], lambda acc, e, rv: [plus(acc, rv) * _gelu_grad(e[0])], [F32], extras=[y2],
                   rowvecs=token_rows(token, d), tm=1024, tk=1024)
    token = emit_exchange("layer0", dy2)
    grad_x, dd, drb, drc, dlr, dli, dg_mix0 = s5_bwd(x, small["norm_mix0"], dy2, dh1, after_token(small["s5_d"], token), cs,
                                                     rb16, rbt16, rct16, lr_t, li_t)
    grads_small["s5_d"] = dd
    grads_small["s5_mats"] = (drb, drc, dlr, dli)
    grads_small["norm_mix0"] = dg_mix0
    return loss_tile, grad_x, grads_small


SMALL_NAMES = ["norm_mix", "norm_mlp", "norm_kv", "norm_final", "s5_a_re", "s5_a_im", "s5_log_dt", "s5_b_re", "s5_b_im",
               "s5_c_re", "s5_c_im", "s5_d", "s5_b_glu", "b_kv", "b_q", "sinks", "b_o"]
BIG_NAMES = ["s5_w_glu", "w_kv", "w_q", "w_o", "w_mlp_in", "w_mlp_out"]
WEIGHT_ORDER = ["norm_mix", "norm_mlp", "norm_kv", "norm_final", "s5_a_re", "s5_a_im", "s5_log_dt", "s5_b_re", "s5_b_im",
                "s5_c_re", "s5_c_im", "s5_d", "s5_w_glu", "s5_b_glu", "w_kv", "b_kv", "w_q", "b_q", "sinks", "w_o", "b_o",
                "w_mlp_in", "w_mlp_out"]


def kernel(x, norm_mix, norm_mlp, norm_kv, norm_final, s5_a_re, s5_a_im, s5_log_dt, s5_b_re, s5_b_im, s5_c_re, s5_c_im, s5_d, s5_w_glu, s5_b_glu, w_kv, b_kv, w_q, b_q, sinks, w_o, b_o, w_mlp_in, w_mlp_out, loss_target, m_norm_mix, m_norm_mlp, m_norm_kv, m_norm_final, m_s5_a_re, m_s5_a_im, m_s5_log_dt, m_s5_b_re, m_s5_b_im, m_s5_c_re, m_s5_c_im, m_s5_d, m_s5_w_glu, m_s5_b_glu, m_w_kv, m_b_kv, m_w_q, m_b_q, m_sinks, m_w_o, m_b_o, m_w_mlp_in, m_w_mlp_out, v_norm_mix, v_norm_mlp, v_norm_kv, v_norm_final, v_s5_a_re, v_s5_a_im, v_s5_log_dt, v_s5_b_re, v_s5_b_im, v_s5_c_re, v_s5_c_im, v_s5_d, v_s5_w_glu, v_s5_b_glu, v_w_kv, v_b_kv, v_w_q, v_b_q, v_sinks, v_w_o, v_b_o, v_w_mlp_in, v_w_mlp_out):
    env = dict(locals())
    w = {n: env[n] for n in WEIGHT_ORDER}
    mom = {n: env["m_" + n] for n in WEIGHT_ORDER}
    var = {n: env["v_" + n] for n in WEIGHT_ORDER}
    d = D_MODEL
    xi, yi, ci = lax.axis_index("x"), lax.axis_index("y"), lax.axis_index("c")
    chip = 2 * xi + yi
    where = jnp.stack([ci, chip]).astype(jnp.int32)

    dsh, bsh = s5_d.shape[1], s5_b_glu.shape[1]
    packed = jnp.concatenate([s5_d.reshape(-1, 128), s5_b_glu.reshape(-1, 128)])
    n_d, n_b = dsh // 128, bsh // 128
    gathered_rows = gather_rows("gather_vectors", jnp.pad(packed, ((0, 8 - n_d - n_b), (0, 0))))
    d_full = gathered_rows[:, 0:n_d].reshape(1, -1)
    bglu_full = gathered_rows[:, n_d:n_d + n_b].reshape(1, -1)

    big = [s5_w_glu, w_kv[None], w_q, w_o, w_mlp_in, w_mlp_out]
    entries = [(0, 0, "col"), (1, 0, "row"), (2, 0, "row"), (3, 0, "row"), (4, 0, "col"), (4, 1, "col"),
               (5, 0, "row"), (5, 1, "row")]
    names = ["w_glu", "w_kv", "w_q", "w_o", "w_in0", "w_in1", "w_out0", "w_out1"]
    kinds = dict(zip(names, [k for _, _, k in entries]))
    shard_shapes = dict(zip(names, [tuple(big[a].shape[1:]) for a, _, _ in entries]))

    placed_w = dict(zip(names, cast_place(big, entries, where)))
    gather_groups = {"glu": ["w_glu"], "mlp_in0": ["w_in0"], "mlp_out0": ["w_out0"], "attn": ["w_kv", "w_q", "w_o"],
                     "mlp_in1": ["w_in1"], "mlp_out1": ["w_out1"]}
    order = [n for members in gather_groups.values() for n in members]
    send, recv, thru, token = gather_start([placed_w[n] for n in order], [kinds[n] for n in order],
                                           [shard_shapes[n] for n in order], gathered_rows)
    started = dict(zip(order, thru))

    def need(group, after):
        members = gather_groups[group]
        ks, shapes = [kinds[n] for n in members], [shard_shapes[n] for n in members]
        landed = gather_wait("gather_wait_" + group, send, recv, [started[n] for n in members], ks, shapes, after,
                             order.index(members[0]))
        return dict(zip(members, forward_halves("forward_halves_" + group, landed, ks, shapes)))

    swapping, exchanging = {}, {}

    def emit_swap(group, partial):
        members = list(partial)
        send, recv, mine, lands, tok = swap_start("swap_start_" + group, [partial[n] for n in members],
                                                  [kinds[n] for n in members])
        swapping[group] = (members, send, recv, mine, lands)
        return tok

    def emit_exchange(group, after):
        members, send, recv, mine, lands = swapping[group]
        ks, shapes = [kinds[n] for n in members], [shard_shapes[n] for n in members]
        mine, landed = swap_wait("swap_wait_" + group, send, recv, mine, lands, ks, after)
        sums = add_halves("add_halves_" + group, mine, landed, ks, where)
        send, recv, parts, lands, tok = exchange_start("exchange_start_" + group, sums, ks, shapes)
        exchanging[group] = (members, send, recv, parts, lands)
        return tok

    s5_args = (s5_a_re[0], s5_a_im[0], s5_log_dt[0], s5_b_re[0], s5_b_im[0])
    small = {
        "norm_mix0": norm_mix[0:1] + token[0:1, 0:1], "norm_mix1": norm_mix[1:2], "norm_mlp0": norm_mlp[0:1], "norm_mlp1": norm_mlp[1:2],
        "norm_kv": norm_kv.reshape(1, d), "norm_final": norm_final.reshape(1, d), "s5_operands": s5_prep(*s5_args, s5_c_re[0], s5_c_im[0]),
        "s5_d": d_full, "s5_b_glu": bglu_full,
        "b_kv": b_kv.reshape(1, -1), "b_q": b_q, "sinks": sinks, "b_o": b_o,
    }
    loss_row, grad_x, gs = _local_step(x[0], loss_target[0], small, need, emit_swap, emit_exchange)

    mats, lams = s5_compact(*gs["s5_mats"])
    rows = [gs["norm_mix0"], gs["norm_mix1"], gs["norm_mlp0"], gs["norm_mlp1"], gs["norm_kv"], gs["norm_final"], gs["s5_d"],
            gs["b_q"], gs["b_o"], gs["s5_b_glu"], gs["b_kv"], gs["sinks"], loss_row, jnp.zeros((2, d), F32)]
    vecs, lams, mats = all_reduce_small("reduce_small", [jnp.concatenate(rows, axis=0), lams, mats], [F32, F32, BF16])
    grads = split_vectors(where, vecs, dsh, bsh)
    loss = grads.pop("loss")[0, 0]
    g_are, g_aim, g_dt, g_bre, g_bim, dc_re, dc_im = s5_param_bwd(mats, lams, *s5_args)
    grads.update({"s5_a_re": g_are[None], "s5_a_im": g_aim[None], "s5_log_dt": g_dt[None], "s5_b_re": g_bre[None],
                  "s5_b_im": g_bim[None], "s5_c_re": dc_re[None], "s5_c_im": dc_im[None]})

    reduced = [None] * len(big)
    where_of = dict(zip(names, entries))
    for group, after in (("layer1", grad_x), ("layer0", mats)):
        members, send, recv, parts, lands = exchanging[group]
        ks, shapes = [kinds[n] for n in members], [shard_shapes[n] for n in members]
        parts, lands = exchange_wait("exchange_wait_" + group, send, recv, parts, lands, ks, shapes, after)
        targets = [where_of[n][0] for n in members]
        sums = sum_shards("sum_shards_" + group, parts, lands, ks, shapes, where, [where_of[n][1] for n in members],
                          [big[a].shape[0] for a in targets], [reduced[a] for a in targets])
        for a, arr in zip(targets, sums):
            reduced[a] = arr
    reduced = share_halves(reduced, entries)
    for n, g in zip(BIG_NAMES, reduced):
        grads[n] = g.reshape(w[n].shape)

    delta, new_m, new_v = {}, {}, {}
    for n in BIG_NAMES:
        flat = lambda a: a.reshape(-1, a.shape[-1])
        go, dl, nm, nv = adamw("adamw_" + n, flat(w[n]), flat(grads[n]), flat(mom[n]), flat(var[n]))
        grads[n], delta[n], new_m[n], new_v[n] = (t.reshape(w[n].shape) for t in (go, dl, nm, nv))

    def view(n, a):
        return a.reshape(1, -1) if a.ndim == 1 else jnp.swapaxes(a, -1, -2) if n in ("s5_b_re", "s5_b_im") else a

    sw, sg, sm, sv = ([view(n, t[n]) for n in SMALL_NAMES] for t in (w, grads, mom, var))
    for n, a, b, c_ in zip(SMALL_NAMES, *adamw_native("adamw_small", sw, sg, sm, sv)):
        delta[n], new_m[n], new_v[n] = (view(n, t) if t.ndim == 4 else t for t in (a, b, c_))

    out = [loss.reshape(()), grad_x[None]]
    for table in (grads, delta, new_m, new_v):
        out += [table[n].reshape(w[n].shape) for n in WEIGHT_ORDER]
    return tuple(out)
```

```python
import math

import jax
import jax.numpy as jnp
from jax import lax
from jax.experimental import pallas as pl
from jax.experimental.pallas import tpu as pltpu

F32 = jnp.float32
BF16 = jnp.bfloat16

D_MODEL = 1024
S5_GROUPS = 64
S5_GROUP = 16
S5_STATE = 64
N_KV = 4
N_Q = 16
HEAD_DIM = 64
BLOCK = 128
NORM_EPS = 1e-5
LAMBDA_RE_MAX = -1e-4
ADAM_LR, ADAM_B1, ADAM_B2, ADAM_EPS, ADAM_WD, ADAM_STEP = 0.001, 0.9, 0.999, 1e-08, 0.01, 10

VMEM_LIMIT_BYTES = 56 * 1024 * 1024
S5_CHUNK = 256
S5_BLOCKS = 4
MESH = pl.DeviceIdType.MESH


def _params(sem=None):
    return pltpu.CompilerParams(dimension_semantics=sem, vmem_limit_bytes=VMEM_LIMIT_BYTES)


def _sds(shape, dtype):
    return jax.ShapeDtypeStruct(shape, dtype)


def _rms_hat(xv):
    r = lax.rsqrt(jnp.mean(xv * xv, axis=-1, keepdims=True) + NORM_EPS)
    return xv * r, r


def mm_nn(name, a, w, col_offsets, n_out, epilogue, out_dtypes, extras=(), rowvecs=(), n_sums=0, tm=1024, tn=512):
    m, k = a.shape
    tm, tn = min(tm, m), min(tn, n_out)
    nw, ne, nr, no = len(col_offsets), len(extras), len(rowvecs), len(out_dtypes)

    def body(a_ref, *refs):
        w_refs, e_refs, r_refs = refs[:nw], refs[nw:nw + ne], refs[nw + ne:nw + ne + nr]
        o_refs, s_refs = refs[nw + ne + nr:nw + ne + nr + no], refs[nw + ne + nr + no:]
        av = a_ref[...]
        accs = [jnp.dot(av, w_ref[...], preferred_element_type=F32) for w_ref in w_refs]
        outs = epilogue(accs, [e[...] for e in e_refs], [r[...] for r in r_refs])
        for o_ref, o in zip(o_refs, outs[:no]):
            o_ref[...] = o.astype(o_ref.dtype)
        if n_sums:
            @pl.when(pl.program_id(1) == 0)
            def _():
                for s_ref in s_refs:
                    s_ref[...] = jnp.zeros_like(s_ref)

            for s_ref, val in zip(s_refs, outs[no:]):
                s_ref[...] += val

    def wspec(off):
        return pl.BlockSpec((k, tn), lambda j, i, off=off: (0, off // tn + j))

    def rspec(off):
        return pl.BlockSpec((1, tn), lambda j, i, off=off: (0, off // tn + j))

    tile = pl.BlockSpec((tm, tn), lambda j, i: (i, j))
    in_specs = ([pl.BlockSpec((tm, k), lambda j, i: (i, 0))] + [wspec(o) for o in col_offsets]
                + [tile] * ne + [rspec(o) for _, o in rowvecs])
    sem = ("parallel", "arbitrary") if n_sums else ("parallel", "parallel")
    return pl.pallas_call(
        body, grid=(n_out // tn, m // tm), in_specs=in_specs,
        out_specs=[tile] * no + [pl.BlockSpec((1, tn), lambda j, i: (0, j))] * n_sums,
        out_shape=[_sds((m, n_out), dt) for dt in out_dtypes] + [_sds((1, n_out), F32)] * n_sums, name=name,
        compiler_params=_params(sem))(a, *([w] * nw), *extras, *[r for r, _ in rowvecs])


def mm_nt(name, g, w, epilogue, out_dtypes, extras=(), rowvecs=(), n_sums=0, tm=512, tk=512):
    m, n = g.shape
    k = w.shape[0]
    tm, tk = min(tm, m), min(tk, k)
    ne, nr, no = len(extras), len(rowvecs), len(out_dtypes)

    def body(g_ref, w_ref, *refs):
        e_refs, r_refs, o_refs, s_refs = refs[:ne], refs[ne:ne + nr], refs[ne + nr:ne + nr + no], refs[ne + nr + no:]
        acc = lax.dot_general(g_ref[...], w_ref[...], (((1,), (1,)), ((), ())), preferred_element_type=F32)
        outs = epilogue(acc, [e[...] for e in e_refs], [r[...] for r in r_refs])
        for o_ref, o in zip(o_refs, outs[:no]):
            o_ref[...] = o.astype(o_ref.dtype)
        if n_sums:
            @pl.when(pl.program_id(0) == 0)
            def _():
                for s_ref in s_refs:
                    s_ref[...] = jnp.zeros_like(s_ref)

            for s_ref, val in zip(s_refs, outs[no:]):
                s_ref[...] += val

    tile = pl.BlockSpec((tm, tk), lambda i, j: (i, j))
    vec = pl.BlockSpec((1, tk), lambda i, j: (0, j))
    sem = ("arbitrary", "parallel") if n_sums else ("parallel", "parallel")
    return pl.pallas_call(
        body, grid=(m // tm, k // tk),
        in_specs=[pl.BlockSpec((tm, n), lambda i, j: (i, 0)), pl.BlockSpec((tk, n), lambda i, j: (j, 0))]
        + [tile] * ne + [vec] * nr,
        out_specs=[tile] * no + [vec] * n_sums,
        out_shape=[_sds((m, k), dt) for dt in out_dtypes] + [_sds((1, k), F32)] * n_sums, name=name,
        compiler_params=_params(sem))(g, w, *extras, *rowvecs)


def mm_tn(name, a, g, tk=512, tn=512):
    m, k = a.shape
    n = g.shape[1]
    tk, tn = min(tk, k), min(tn, n)

    def body(a_ref, g_ref, o_ref):
        acc = lax.dot_general(a_ref[...], g_ref[...], (((0,), (0,)), ((), ())), preferred_element_type=F32)
        o_ref[...] = acc.astype(o_ref.dtype)

    return pl.pallas_call(
        body, grid=(k // tk, n // tn),
        in_specs=[pl.BlockSpec((m, tk), lambda i, j: (0, i)), pl.BlockSpec((m, tn), lambda i, j: (0, j))],
        out_specs=pl.BlockSpec((tk, tn), lambda i, j: (i, j)), out_shape=_sds((k, n), BF16), name=name,
        compiler_params=_params(("parallel", "parallel")))(a, g)


def _row_mask(tc):
    row = lax.broadcasted_iota(jnp.int32, (8 * tc, 256), 0) % 8
    col = lax.broadcasted_iota(jnp.int32, (8 * tc, 256), 1) // 32
    return row == col


def _expand_rows(val, mask):
    tc, width = val.shape
    rep = jnp.broadcast_to(val[:, None, :], (tc, 8, width)).reshape(8 * tc, width)
    return jnp.where(mask, rep, 0.0).astype(BF16)


def _stage(ref, val):
    ref[0] = val[:, 0:128]
    ref[1] = val[:, 128:256]


def _gather_rows(src_ref, tc):
    halves = []
    for half in range(2):
        col = lax.broadcasted_iota(jnp.int32, (tc, 128), 1) // 32 + 4 * half
        out = jnp.zeros((tc, 128), F32)
        for s8 in range(4 * half, 4 * half + 4):
            out = jnp.where(col == s8, src_ref.at[half][pl.ds(s8, tc, stride=8), :], out)
        halves.append(out)
    return jnp.concatenate(halves, axis=1)


def _gelu(x):
    c = math.sqrt(2.0 / math.pi)
    return 0.5 * x * (1.0 + jnp.tanh(c * (x + 0.044715 * x * x * x)))


def _gelu_grad(x):
    c = math.sqrt(2.0 / math.pi)
    t = jnp.tanh(c * (x + 0.044715 * x * x * x))
    return 0.5 * (1.0 + t) + 0.5 * x * (1.0 - t * t) * c * (1.0 + 3.0 * 0.044715 * x * x)


def s5_fwd(x, gain, d_skip, rb, rc, lam_r, lam_i):
    n_rows = x.shape[0]
    tc = min(S5_CHUNK, n_rows)
    nc = n_rows // tc

    def body(x_ref, g_ref, d_ref, rb_ref, rc_ref, lr_ref, li_ref, ge_ref, y2_ref, cs_ref, bux, yrows, carry):
        i = pl.program_id(0)
        u = _rms_hat(x_ref[...])[0] * g_ref[...]

        @pl.when(i == 0)
        def _():
            carry[...] = jnp.zeros_like(carry)

        cs_ref[0] = carry[...]
        mask = _row_mask(tc)
        for blk in range(S5_BLOCKS):
            lhs = _expand_rows(u[:, blk * 256:(blk + 1) * 256], mask)
            bux[blk] = jnp.dot(lhs, rb_ref[blk], preferred_element_type=F32)
        lam = [(lr_ref[blk], li_ref[blk]) for blk in range(S5_BLOCKS)]

        def step(t, c):
            r0 = pl.multiple_of(t * 8, 8)
            new = []
            for blk in range(S5_BLOCKS):
                xr, xi = c[2 * blk], c[2 * blk + 1]
                lr, li = lam[blk]
                nr = lr * xr - li * xi + bux[blk, pl.ds(r0, 8), 0:128]
                ni = lr * xi + li * xr + bux[blk, pl.ds(r0, 8), 128:256]
                bux[blk, pl.ds(r0, 8), 0:128] = nr
                bux[blk, pl.ds(r0, 8), 128:256] = ni
                new += [nr, ni]
            return tuple(new)

        c0 = []
        for blk in range(S5_BLOCKS):
            c0 += [carry[blk, :, 0:128], carry[blk, :, 128:256]]
        cn = lax.fori_loop(0, tc, step, tuple(c0), unroll=4)
        for blk in range(S5_BLOCKS):
            carry[blk, :, 0:128] = cn[2 * blk]
            carry[blk, :, 128:256] = cn[2 * blk + 1]
        for blk in range(S5_BLOCKS):
            _stage(yrows, jnp.dot(bux[blk].astype(BF16), rc_ref[blk], preferred_element_type=F32))
            sl = slice(blk * 256, (blk + 1) * 256)
            y2 = _gather_rows(yrows, tc) + d_ref[:, sl] * u[:, sl]
            y2_ref[:, sl] = y2
            ge_ref[:, sl] = _gelu(y2).astype(BF16)

    row = pl.BlockSpec((tc, D_MODEL), lambda i: (i, 0))
    vec = pl.BlockSpec((1, D_MODEL), lambda i: (0, 0))
    mat = pl.BlockSpec((S5_BLOCKS, 256, 256), lambda i: (0, 0, 0))
    lamspec = pl.BlockSpec((S5_BLOCKS, 8, 128), lambda i: (0, 0, 0))
    return pl.pallas_call(
        body, grid=(nc,),
        in_specs=[row, vec, vec, mat, mat, lamspec, lamspec],
        out_specs=[row, row, pl.BlockSpec((1, S5_BLOCKS, 8, 256), lambda i: (i, 0, 0, 0))],
        out_shape=[_sds((n_rows, D_MODEL), BF16), _sds((n_rows, D_MODEL), F32), _sds((nc, S5_BLOCKS, 8, 256), F32)],
        scratch_shapes=[pltpu.VMEM((S5_BLOCKS, 8 * tc, 256), F32), pltpu.VMEM((2, 8 * tc, 128), F32),
                        pltpu.VMEM((S5_BLOCKS, 8, 256), F32)],
        name="s5_fwd", compiler_params=_params(("arbitrary",)))(x, gain, d_skip, rb, rc, lam_r, lam_i)


def s5_bwd(x, gain, dy2, res, d_skip, cs, rb, rbt, rct, lam_r, lam_i):
    n_rows = x.shape[0]
    tc = min(S5_CHUNK, n_rows)
    nc = n_rows // tc

    def body(x_ref, g_ref, dy_ref, res_ref, d_ref, cs_ref, rb_ref, rbt_ref, rct_ref, lr_ref, li_ref,
             dx_ref, dd_ref, drb_ref, drc_ref, dlr_ref, dli_ref, dg_ref, tmp, du, lhsu, lhsd, xs, adj, acarry):
        i = pl.program_id(0)
        u = _rms_hat(x_ref[...])[0] * g_ref[...]

        @pl.when(i == 0)
        def _():
            acarry[...] = jnp.zeros_like(acarry)
            dd_ref[...] = jnp.zeros_like(dd_ref)
            drb_ref[...] = jnp.zeros_like(drb_ref)
            drc_ref[...] = jnp.zeros_like(drc_ref)
            dlr_ref[...] = jnp.zeros_like(dlr_ref)
            dli_ref[...] = jnp.zeros_like(dli_ref)
            dg_ref[...] = jnp.zeros_like(dg_ref)

        dd_ref[...] += jnp.sum(dy_ref[...] * u, axis=0, keepdims=True)
        mask = _row_mask(tc)
        for blk in range(S5_BLOCKS):
            sl = slice(blk * 256, (blk + 1) * 256)
            lhsu[blk] = _expand_rows(u[:, sl], mask)
            xs[blk] = jnp.dot(lhsu[blk], rb_ref[blk], preferred_element_type=F32)
            lhsd[blk] = _expand_rows(dy_ref[:, sl], mask)
            adj[blk] = jnp.dot(lhsd[blk], rct_ref[blk], preferred_element_type=F32)
        lam = [(lr_ref[blk], li_ref[blk]) for blk in range(S5_BLOCKS)]

        def fstep(t, c):
            r0 = pl.multiple_of(t * 8, 8)
            new = []
            for blk in range(S5_BLOCKS):
                xr, xi = c[2 * blk], c[2 * blk + 1]
                lr, li = lam[blk]
                nr = lr * xr - li * xi + xs[blk, pl.ds(r0, 8), 0:128]
                ni = lr * xi + li * xr + xs[blk, pl.ds(r0, 8), 128:256]
                xs[blk, pl.ds(r0, 8), 0:128] = nr
                xs[blk, pl.ds(r0, 8), 128:256] = ni
                new += [nr, ni]
            return tuple(new)

        c0 = []
        for blk in range(S5_BLOCKS):
            c0 += [cs_ref[0, blk, :, 0:128], cs_ref[0, blk, :, 128:256]]
        lax.fori_loop(0, tc, fstep, tuple(c0), unroll=4)

        def bstep(k, c):
            t = tc - 1 - k
            r0 = pl.multiple_of(t * 8, 8)
            rp = pl.multiple_of(jnp.maximum(t - 1, 0) * 8, 8)
            first = t == 0
            new_a, new_g = [], []
            for blk in range(S5_BLOCKS):
                ar, ai = c[0][2 * blk], c[0][2 * blk + 1]
                glr, gli = c[1][2 * blk], c[1][2 * blk + 1]
                lr, li = lam[blk]
                nr = lr * ar + li * ai + adj[blk, pl.ds(r0, 8), 0:128]
                ni = lr * ai - li * ar + adj[blk, pl.ds(r0, 8), 128:256]
                adj[blk, pl.ds(r0, 8), 0:128] = nr
                adj[blk, pl.ds(r0, 8), 128:256] = ni
                pr = jnp.where(first, cs_ref[0, blk, :, 0:128], xs[blk, pl.ds(rp, 8), 0:128])
                pi = jnp.where(first, cs_ref[0, blk, :, 128:256], xs[blk, pl.ds(rp, 8), 128:256])
                new_a += [nr, ni]
                new_g += [glr + nr * pr + ni * pi, gli + ni * pr - nr * pi]
            return tuple(new_a), tuple(new_g)

        a0, g0 = [], []
        for blk in range(S5_BLOCKS):
            a0 += [acarry[blk, :, 0:128], acarry[blk, :, 128:256]]
            g0 += [dlr_ref[blk], dli_ref[blk]]
        an, gn = lax.fori_loop(0, tc, bstep, (tuple(a0), tuple(g0)), unroll=2)
        for blk in range(S5_BLOCKS):
            acarry[blk, :, 0:128] = an[2 * blk]
            acarry[blk, :, 128:256] = an[2 * blk + 1]
            dlr_ref[blk] = gn[2 * blk]
            dli_ref[blk] = gn[2 * blk + 1]
        for blk in range(S5_BLOCKS):
            sl = slice(blk * 256, (blk + 1) * 256)
            ab = adj[blk].astype(BF16)
            _stage(tmp, jnp.dot(ab, rbt_ref[blk], preferred_element_type=F32))
            du[:, sl] = _gather_rows(tmp, tc) + d_ref[:, sl] * dy_ref[:, sl]
            drb_ref[blk] += lax.dot_general(lhsu[blk], ab, (((0,), (0,)), ((), ())), preferred_element_type=F32)
            drc_ref[blk] += lax.dot_general(lhsd[blk], xs[blk].astype(BF16), (((0,), (0,)), ((), ())),
                                            preferred_element_type=F32)
        xh, r = _rms_hat(x_ref[...])
        dg_ref[...] += jnp.sum(du[...] * xh, axis=0, keepdims=True)
        dxh = du[...] * g_ref[...]
        dx_ref[...] = r * (dxh - xh * jnp.mean(dxh * xh, axis=-1, keepdims=True)) + res_ref[...]

    rev = pl.BlockSpec((tc, D_MODEL), lambda i: (nc - 1 - i, 0))
    vec = pl.BlockSpec((1, D_MODEL), lambda i: (0, 0))
    mat = pl.BlockSpec((S5_BLOCKS, 256, 256), lambda i: (0, 0, 0))
    lamspec = pl.BlockSpec((S5_BLOCKS, 8, 128), lambda i: (0, 0, 0))
    big = pltpu.VMEM((S5_BLOCKS, 8 * tc, 256), F32)
    bigb = pltpu.VMEM((S5_BLOCKS, 8 * tc, 256), BF16)
    return pl.pallas_call(
        body, grid=(nc,),
        in_specs=[rev, vec, rev, rev, vec, pl.BlockSpec((1, S5_BLOCKS, 8, 256), lambda i: (nc - 1 - i, 0, 0, 0)),
                  mat, mat, mat, lamspec, lamspec],
        out_specs=[rev, vec, mat, mat, lamspec, lamspec, vec],
        out_shape=[_sds((n_rows, D_MODEL), F32), _sds((1, D_MODEL), F32), _sds((S5_BLOCKS, 256, 256), F32),
                   _sds((S5_BLOCKS, 256, 256), F32), _sds((S5_BLOCKS, 8, 128), F32), _sds((S5_BLOCKS, 8, 128), F32),
                   _sds((1, D_MODEL), F32)],
        scratch_shapes=[pltpu.VMEM((2, 8 * tc, 128), F32), pltpu.VMEM((tc, D_MODEL), F32), bigb, bigb, big, big,
                        pltpu.VMEM((S5_BLOCKS, 8, 256), F32)],
        name="s5_bwd", compiler_params=_params(("arbitrary",)))(
            x, gain, dy2, res, d_skip, cs, rb, rbt, rct, lam_r, lam_i)


def _s5_views(a_re, a_im, log_dt, b_re, b_im):
    return a_re[:, None, :], a_im[:, None, :], log_dt[:, None, None], jnp.swapaxes(b_re, 1, 2), jnp.swapaxes(b_im, 1, 2)


def _s5_factors(a_re, a_im, log_dt):
    lr, li, dt = jnp.minimum(a_re, LAMBDA_RE_MAX), a_im, jnp.exp(log_dt)
    mag, ang = jnp.exp(lr * dt), li * dt
    lbr, lbi = mag * jnp.cos(ang), mag * jnp.sin(ang)
    den = lr * lr + li * li
    fr, fi = ((lbr - 1.0) * lr + lbi * li) / den, (lbi * lr - (lbr - 1.0) * li) / den
    return lr, li, dt, lbr, lbi, fr, fi, den


def s5_prep(a_re, a_im, log_dt, b_re, b_im, c_re, c_im):
    def body(ar_ref, ai_ref, t_ref, br_ref, bi_ref, cr_ref, ci_ref, rb_ref, rbt_ref, rc_ref, rct_ref, lr_ref, li_ref):
        _, _, _, lbr, lbi, fr, fi, _ = _s5_factors(ar_ref[...], ai_ref[...], t_ref[...])
        lr_ref[...] = lbr
        li_ref[...] = lbi
        bre = fr * br_ref[...] - fi * bi_ref[...]
        bim = fr * bi_ref[...] + fi * br_ref[...]
        even = (lax.broadcasted_iota(jnp.int32, (256, S5_STATE), 0) // S5_GROUP) % 2 == 0

        def assemble(re, im):
            re, im = re.reshape(256, S5_STATE), im.reshape(256, S5_STATE)
            return jnp.concatenate([jnp.where(even, re, 0.0), jnp.where(even, 0.0, re), jnp.where(even, im, 0.0),
                                    jnp.where(even, 0.0, im)], axis=1)

        for blk in range(S5_BLOCKS):
            sl = slice(16 * blk, 16 * blk + 16)
            rb = assemble(bre[sl], bim[sl])
            rct = assemble(cr_ref[sl], -ci_ref[sl])
            rb_ref[blk] = rb.astype(BF16)
            rbt_ref[blk] = rb.T.astype(BF16)
            rct_ref[blk] = rct.astype(BF16)
            rc_ref[blk] = rct.T.astype(BF16)

    vm = pl.BlockSpec(memory_space=pltpu.VMEM)
    mat = _sds((S5_BLOCKS, 256, 256), BF16)
    lam = _sds((S5_GROUPS, 1, S5_STATE), F32)
    rb, rbt, rc, rct, lam_r, lam_i = pl.pallas_call(
        body, in_specs=[vm] * 7, out_specs=[vm] * 6, out_shape=[mat, mat, mat, mat, lam, lam], name="s5_prep",
        compiler_params=_params())(*_s5_views(a_re, a_im, log_dt, b_re, b_im), c_re, c_im)
    return rb, rbt, rc, rct, lam_r.reshape(S5_BLOCKS, 8, 128), lam_i.reshape(S5_BLOCKS, 8, 128)


def s5_param_bwd(mats, lams, a_re, a_im, log_dt, b_re, b_im):
    def body(m_ref, glr_ref, gli_ref, ar_ref, ai_ref, t_ref, br_ref, bi_ref,
             dar_ref, dai_ref, dt_ref, dbr_ref, dbi_ref, dcr_ref, dci_ref):
        lr, li, dt, lbr, lbi, fr, fi, den = _s5_factors(ar_ref[...], ai_ref[...], t_ref[...])
        shape = (S5_GROUPS, S5_GROUP, S5_STATE)
        gbr, gbi = m_ref[0:1024, 0:64].reshape(shape), m_ref[0:1024, 64:128].reshape(shape)
        dcr_ref[...] = m_ref[1024:2048, 0:64].reshape(shape)
        dci_ref[...] = -m_ref[1024:2048, 64:128].reshape(shape)
        br, bi = br_ref[...], bi_ref[...]
        dbr_ref[...] = fr * gbr + fi * gbi
        dbi_ref[...] = fr * gbi - fi * gbr
        dfr = jnp.sum(gbr * br + gbi * bi, axis=1, keepdims=True)
        dfi = jnp.sum(gbi * br - gbr * bi, axis=1, keepdims=True)
        nr, ni = (dfr * lr - dfi * li) / den, (dfr * li + dfi * lr) / den
        qr, qi = (fr * lr + fi * li) / den, (fi * lr - fr * li) / den
        lam_r, lam_i = -(dfr * qr + dfi * qi), -(dfi * qr - dfr * qi)
        gr, gi = glr_ref[...] + nr, gli_ref[...] + ni
        zr, zi = gr * lbr + gi * lbi, gi * lbr - gr * lbi
        a = ar_ref[...]
        dar_ref[...] = (lam_r + zr * dt) * jnp.where(a < LAMBDA_RE_MAX, 1.0, jnp.where(a == LAMBDA_RE_MAX, 0.5, 0.0))
        dai_ref[...] = lam_i + zi * dt
        dt_ref[...] = jnp.sum(zr * lr + zi * li, axis=2, keepdims=True) * dt

    vm = pl.BlockSpec(memory_space=pltpu.VMEM)
    state = _sds((S5_GROUPS, 1, S5_STATE), F32)
    wide = _sds((S5_GROUPS, S5_GROUP, S5_STATE), F32)
    glr = lams[0:32].reshape(S5_GROUPS, 1, S5_STATE)
    gli = lams[32:64].reshape(S5_GROUPS, 1, S5_STATE)
    dar, dai, ddt, dbr, dbi, dcr, dci = pl.pallas_call(
        body, in_specs=[vm] * 8, out_specs=[vm] * 7,
        out_shape=[state, state, _sds((S5_GROUPS, 1, 1), F32), wide, wide, wide, wide], name="s5_param_bwd",
        compiler_params=_params())(mats, glr, gli, *_s5_views(a_re, a_im, log_dt, b_re, b_im))
    return (dar.reshape(S5_GROUPS, S5_STATE), dai.reshape(S5_GROUPS, S5_STATE), ddt.reshape(S5_GROUPS),
            jnp.swapaxes(dbr, 1, 2), jnp.swapaxes(dbi, 1, 2), dcr, dci)


def s5_compact(drb, drct, dlr, dli):
    def body(drb_ref, drct_ref, dlr_ref, dli_ref, o_ref, lam_ref):
        even = (lax.broadcasted_iota(jnp.int32, (256, 64), 0) // S5_GROUP) % 2 == 0
        for blk in range(S5_BLOCKS):
            for k, ref in enumerate((drb_ref, drct_ref)):
                m = ref[blk]
                re = jnp.where(even, m[:, 0:64], m[:, 64:128])
                im = jnp.where(even, m[:, 128:192], m[:, 192:256])
                o_ref[pl.ds(k * 1024 + blk * 256, 256), :] = jnp.concatenate([re, im], axis=1)
            lam_ref[pl.ds(blk * 8, 8), :] = dlr_ref[blk]
            lam_ref[pl.ds(32 + blk * 8, 8), :] = dli_ref[blk]

    vm = pl.BlockSpec(memory_space=pltpu.VMEM)
    return pl.pallas_call(body, in_specs=[vm] * 4, out_specs=[vm, vm], out_shape=[_sds((2048, 128), F32), _sds((64, 128), F32)],
                          name="s5_compact", compiler_params=_params())(drb, drct, dlr, dli)


NEG = -1e30


GROUP = N_Q // N_KV


def _attn_masks(n):
    qi = lax.broadcasted_iota(jnp.int32, (GROUP * BLOCK, BLOCK), 0) % BLOCK
    kj = lax.broadcasted_iota(jnp.int32, (GROUP * BLOCK, BLOCK), 1)
    return jnp.logical_and(kj > qi, n > 0), kj <= qi


def _stack_heads(ref, kh):
    return jnp.concatenate([ref[:, (GROUP * kh + g) * HEAD_DIM:(GROUP * kh + g + 1) * HEAD_DIM] for g in range(GROUP)], axis=0)


def _unstack_heads(val):
    return jnp.concatenate([val[g * BLOCK:(g + 1) * BLOCK] for g in range(GROUP)], axis=1)


def _sink_column(sink_ref, kh):
    grp = lax.broadcasted_iota(jnp.int32, (GROUP * BLOCK, 1), 0) // BLOCK
    col = jnp.zeros((GROUP * BLOCK, 1), F32)
    for g in range(GROUP):
        col = jnp.where(grp == g, sink_ref[GROUP * kh + g], col)
    return col, grp


def _attn_exp(q4, kp, kc, sink, mask_p, mask_c):
    scale = 1.0 / math.sqrt(HEAD_DIM)
    nt = (((1,), (1,)), ((), ()))
    sp = jnp.where(mask_p, lax.dot_general(q4, kp, nt, preferred_element_type=F32) * scale, NEG)
    sc = jnp.where(mask_c, lax.dot_general(q4, kc, nt, preferred_element_type=F32) * scale, NEG)
    m = jnp.maximum(jnp.maximum(jnp.max(sp, axis=-1, keepdims=True), jnp.max(sc, axis=-1, keepdims=True)), sink)
    pp = jnp.exp(sp - m)
    pc = jnp.exp(sc - m)
    ps = jnp.exp(sink - m)
    inv = 1.0 / (jnp.sum(pp, axis=-1, keepdims=True) + jnp.sum(pc, axis=-1, keepdims=True) + ps)
    return pp, pc, ps, inv


def attn_fwd(q, kv, sinks):
    n_rows = q.shape[0]
    nb = n_rows // BLOCK

    def body(sink_ref, q_ref, kvp_ref, kvc_ref, o_ref):
        n = pl.program_id(0)
        mask_p, mask_c = _attn_masks(n)
        outs = []
        for kh in range(N_KV):
            ks, vs = slice(kh * HEAD_DIM, (kh + 1) * HEAD_DIM), slice((N_KV + kh) * HEAD_DIM, (N_KV + kh + 1) * HEAD_DIM)
            sink, _ = _sink_column(sink_ref, kh)
            pp, pc, _, inv = _attn_exp(_stack_heads(q_ref, kh), kvp_ref[:, ks], kvc_ref[:, ks], sink, mask_p, mask_c)
            o4 = (jnp.dot(pp.astype(BF16), kvp_ref[:, vs], preferred_element_type=F32)
                  + jnp.dot(pc.astype(BF16), kvc_ref[:, vs], preferred_element_type=F32)) * inv
            outs.append(_unstack_heads(o4))
        o_ref[...] = jnp.concatenate(outs, axis=1).astype(BF16)

    kvw = 2 * N_KV * HEAD_DIM
    return pl.pallas_call(
        body, grid=(nb,),
        in_specs=[pl.BlockSpec(memory_space=pltpu.SMEM), pl.BlockSpec((BLOCK, D_MODEL), lambda n: (n, 0)),
                  pl.BlockSpec((BLOCK, kvw), lambda n: (jnp.maximum(n - 1, 0), 0)), pl.BlockSpec((BLOCK, kvw), lambda n: (n, 0))],
        out_specs=pl.BlockSpec((BLOCK, D_MODEL), lambda n: (n, 0)), out_shape=_sds((n_rows, D_MODEL), BF16),
        name="attn_fwd", compiler_params=_params(("parallel",)))(sinks, q, kv, kv)


def attn_bwd(q, kv, do, sinks):
    n_rows = q.shape[0]
    nb = n_rows // BLOCK
    kvw = 2 * N_KV * HEAD_DIM
    tn = (((0,), (0,)), ((), ()))
    nt = (((1,), (1,)), ((), ()))
    scale = 1.0 / math.sqrt(HEAD_DIM)

    def body(sink_ref, q_ref, kvp_ref, kvc_ref, do_ref, dq_ref, dbq_ref, dprev_ref, dcur_ref, dsink_ref):
        n = pl.program_id(0)
        mask_p, mask_c = _attn_masks(n)
        lane = lax.broadcasted_iota(jnp.int32, (1, D_MODEL), 1)
        dqs, dsink = [], jnp.zeros((1, D_MODEL), F32)
        dkp, dkc, dvp, dvc = [], [], [], []
        for kh in range(N_KV):
            ks, vs = slice(kh * HEAD_DIM, (kh + 1) * HEAD_DIM), slice((N_KV + kh) * HEAD_DIM, (N_KV + kh + 1) * HEAD_DIM)
            q4, do4 = _stack_heads(q_ref, kh), _stack_heads(do_ref, kh)
            kp, kc, vp, vc = kvp_ref[:, ks], kvc_ref[:, ks], kvp_ref[:, vs], kvc_ref[:, vs]
            sink, grp = _sink_column(sink_ref, kh)
            pp, pc, ps, inv = _attn_exp(q4, kp, kc, sink, mask_p, mask_c)
            pp, pc = pp * inv, pc * inv
            dpp = lax.dot_general(do4, vp, nt, preferred_element_type=F32)
            dpc = lax.dot_general(do4, vc, nt, preferred_element_type=F32)
            delta = jnp.sum(pp * dpp, axis=-1, keepdims=True) + jnp.sum(pc * dpc, axis=-1, keepdims=True)
            dsp = (pp * (dpp - delta) * scale).astype(BF16)
            dsc = (pc * (dpc - delta) * scale).astype(BF16)
            dsk = ps * inv * delta
            for g in range(GROUP):
                dsink = dsink + jnp.where(lane == GROUP * kh + g, -jnp.sum(jnp.where(grp == g, dsk, 0.0)), 0.0)
            dqs.append(_unstack_heads(jnp.dot(dsp, kp, preferred_element_type=F32)
                                      + jnp.dot(dsc, kc, preferred_element_type=F32)))
            dkp.append(lax.dot_general(dsp, q4, tn, preferred_element_type=F32))
            dkc.append(lax.dot_general(dsc, q4, tn, preferred_element_type=F32))
            dvp.append(lax.dot_general(pp.astype(BF16), do4, tn, preferred_element_type=F32))
            dvc.append(lax.dot_general(pc.astype(BF16), do4, tn, preferred_element_type=F32))
        dq = jnp.concatenate(dqs, axis=1)
        dq_ref[...] = dq.astype(BF16)
        dprev_ref[0] = jnp.concatenate(dkp + dvp, axis=1)
        dcur_ref[0] = jnp.concatenate(dkc + dvc, axis=1)

        @pl.when(n == 0)
        def _():
            dbq_ref[...] = jnp.zeros_like(dbq_ref)
            dsink_ref[...] = jnp.zeros_like(dsink_ref)

        dbq_ref[...] += jnp.sum(dq, axis=0, keepdims=True)
        dsink_ref[...] += dsink

    blk = pl.BlockSpec((BLOCK, D_MODEL), lambda n: (n, 0))
    part = pl.BlockSpec((1, BLOCK, kvw), lambda n: (n, 0, 0))
    return pl.pallas_call(
        body, grid=(nb,),
        in_specs=[pl.BlockSpec(memory_space=pltpu.SMEM), blk,
                  pl.BlockSpec((BLOCK, kvw), lambda n: (jnp.maximum(n - 1, 0), 0)), pl.BlockSpec((BLOCK, kvw), lambda n: (n, 0)), blk],
        out_specs=[blk, pl.BlockSpec((1, D_MODEL), lambda n: (0, 0)), part, part, pl.BlockSpec((1, D_MODEL), lambda n: (0, 0))],
        out_shape=[_sds((n_rows, D_MODEL), BF16), _sds((1, D_MODEL), F32), _sds((nb, BLOCK, kvw), F32),
                   _sds((nb, BLOCK, kvw), F32), _sds((1, D_MODEL), F32)],
        name="attn_bwd", compiler_params=_params(("arbitrary",)))(sinks, q, kv, kv, do)


def kv_combine(dprev, dcur):
    nb, _, kvw = dprev.shape

    def body(dcur_ref, dprev_ref, dkv_ref, db_ref):
        total = jnp.zeros((1, kvw), F32)
        for m in range(nb):
            dkv = dcur_ref[m] + dprev_ref[m + 1] if m + 1 < nb else dcur_ref[m]
            dkv_ref[m * BLOCK:(m + 1) * BLOCK, :] = dkv.astype(BF16)
            total = total + jnp.sum(dkv, axis=0, keepdims=True)
        db_ref[...] = jnp.concatenate([total, jnp.zeros((1, D_MODEL - kvw), F32)], axis=1)

    vm = pl.BlockSpec(memory_space=pltpu.VMEM)
    return pl.pallas_call(body, in_specs=[vm, vm], out_specs=[vm, vm],
                          out_shape=[_sds((nb * BLOCK, kvw), BF16), _sds((1, D_MODEL), F32)], name="kv_combine",
                          compiler_params=_params())(dcur, dprev)


def glu_bwd(dout, val, gate, tm=256):
    n_rows, d = dout.shape

    def body(do_ref, v_ref, g_ref, dz_ref, db_ref):
        i = pl.program_id(0)
        sg = jax.nn.sigmoid(g_ref[...])
        dval = do_ref[...] * sg
        dgate = do_ref[...] * v_ref[...] * sg * (1.0 - sg)
        dz_ref[...] = jnp.concatenate([dval, dgate], axis=1).astype(BF16)

        @pl.when(i == 0)
        def _():
            db_ref[...] = jnp.zeros_like(db_ref)

        db_ref[0:1, :] += jnp.sum(dval, axis=0, keepdims=True)
        db_ref[1:2, :] += jnp.sum(dgate, axis=0, keepdims=True)

    row = pl.BlockSpec((tm, d), lambda i: (i, 0))
    return pl.pallas_call(
        body, grid=(n_rows // tm,), in_specs=[row, row, row],
        out_specs=[pl.BlockSpec((tm, 2 * d), lambda i: (i, 0)), pl.BlockSpec((2, d), lambda i: (0, 0))],
        out_shape=[_sds((n_rows, 2 * d), BF16), _sds((2, d), F32)],
        name="glu_bwd", compiler_params=_params(("arbitrary",)))(dout, val, gate)


def _adam_update(w, g, m, v):
    nm = ADAM_B1 * m + (1.0 - ADAM_B1) * g
    nv = ADAM_B2 * v + (1.0 - ADAM_B2) * (g * g)
    m_hat = nm / (1.0 - ADAM_B1 ** ADAM_STEP)
    v_hat = nv / (1.0 - ADAM_B2 ** ADAM_STEP)
    return -ADAM_LR * (m_hat / (jnp.sqrt(v_hat) + ADAM_EPS) + ADAM_WD * w), nm, nv


def adamw(name, w, g, m, v, tm=256):
    n_rows, d = w.shape
    tm = tm if n_rows % tm == 0 else n_rows

    def body(w_ref, g_ref, m_ref, v_ref, go_ref, d_ref, nm_ref, nv_ref):
        gv = g_ref[...]
        go_ref[...] = gv
        d_ref[...], nm_ref[...], nv_ref[...] = _adam_update(w_ref[...], gv, m_ref[...], v_ref[...])

    row = pl.BlockSpec((tm, d), lambda i: (i, 0))
    return pl.pallas_call(
        body, grid=(n_rows // tm,), in_specs=[row] * 4, out_specs=[row] * 4,
        out_shape=[_sds((n_rows, d), F32)] * 4, name=name, compiler_params=_params(("parallel",)))(w, g, m, v)


def adamw_native(name, ws, gs, ms, vs):
    n = len(ws)

    def body(*refs):
        w_refs, g_refs, m_refs, v_refs = refs[:n], refs[n:2 * n], refs[2 * n:3 * n], refs[3 * n:4 * n]
        d_refs, nm_refs, nv_refs = refs[4 * n:5 * n], refs[5 * n:6 * n], refs[6 * n:7 * n]
        for k in range(n):
            dl, nm, nv = _adam_update(w_refs[k][...], g_refs[k][...], m_refs[k][...], v_refs[k][...])
            d_refs[k][...] = dl
            nm_refs[k][...] = nm
            nv_refs[k][...] = nv

    vm = pl.BlockSpec(memory_space=pltpu.VMEM)
    shapes = [_sds(w.shape, F32) for w in ws]
    out = pl.pallas_call(body, in_specs=[vm] * (4 * n), out_specs=[vm] * (3 * n), out_shape=shapes * 3, name=name,
                         compiler_params=_params())(*ws, *gs, *ms, *vs)
    return list(out[:n]), list(out[n:2 * n]), list(out[2 * n:])


VEC_ROWS = {"norm_mix": 0, "norm_mlp": 2, "norm_kv": 4, "norm_final": 5, "s5_d": 6, "b_q": 7, "b_o": 8, "s5_b_glu": 9,
            "b_kv": 11, "sinks": 12, "loss": 13}


def split_vectors(where, vecs, d_shard, glu_shard):
    kvw = 2 * N_KV * HEAD_DIM
    shapes = {"norm_mix": (2, D_MODEL), "norm_mlp": (2, D_MODEL), "norm_kv": (1, D_MODEL), "norm_final": (1, D_MODEL),
              "s5_d": (1, d_shard), "b_q": (1, D_MODEL), "b_o": (1, D_MODEL), "s5_b_glu": (1, glu_shard), "b_kv": (1, kvw),
              "sinks": (1, N_Q), "loss": (1, 128)}
    names = list(shapes)

    def body(where_ref, v_ref, *o_refs):
        chip = where_ref[1]
        for name, o_ref in zip(names, o_refs):
            r0, (r, n) = VEC_ROWS[name], shapes[name]
            if name == "s5_d":
                g = jnp.zeros((1, n), F32)
                for j in range(4):
                    g = jnp.where(chip == j, v_ref[r0:r0 + 1, j * n:(j + 1) * n], g)
            elif name == "s5_b_glu":
                g = jnp.zeros((1, n), F32)
                for j in range(4):
                    row, col = r0 + (j * n) // D_MODEL, (j * n) % D_MODEL
                    g = jnp.where(chip == j, v_ref[row:row + 1, col:col + n], g)
            else:
                g = v_ref[r0:r0 + r, 0:n]
            o_ref[...] = g

    vm = pl.BlockSpec(memory_space=pltpu.VMEM)
    out = pl.pallas_call(body, in_specs=[pl.BlockSpec(memory_space=pltpu.SMEM), vm], out_specs=[vm] * len(names),
                         out_shape=[_sds(shapes[n], F32) for n in names], name="split_vectors",
                         compiler_params=_params())(where, vecs)
    return dict(zip(names, out))


def _position():
    x, y, c = lax.axis_index("x"), lax.axis_index("y"), lax.axis_index("c")
    others = [(1 - x, y), (x, 1 - y), (1 - x, 1 - y)]
    return x, y, c, others


def _window(ref, kind, chip, half, shard_shape):
    if kind == "slab":
        return ref.at[chip]
    r, n = shard_shape
    if kind == "col":
        return ref.at[pl.ds(pl.multiple_of(half * (r // 2), 16), r // 2), pl.ds(pl.multiple_of(chip * n, 128), n)]
    return ref.at[pl.ds(pl.multiple_of(chip * r, 16), r), pl.ds(pl.multiple_of(half * (n // 2), 128), n // 2)]


def _half(ref, kind, half, shape):
    r, n = shape
    if kind == "col":
        return ref.at[pl.ds(pl.multiple_of(half * (r // 2), 16), r // 2), :]
    return ref.at[:, pl.ds(pl.multiple_of(half * (n // 2), 128), n // 2)]


def swap_start(name, grads, kinds):
    nt = len(grads)
    shapes = [tuple(g.shape) for g in grads]
    lands = [lax.empty(sh, BF16) for sh in shapes]

    def body(*refs):
        in_refs, land_refs = refs[:nt], refs[nt:2 * nt]
        send_sems, recv_sems, token = refs[2 * nt], refs[2 * nt + 1], refs[-1]
        x, y, c, _ = _position()
        for t in range(nt):
            pltpu.make_async_remote_copy(
                src_ref=_half(in_refs[t], kinds[t], 1 - c, shapes[t]), dst_ref=_half(land_refs[t], kinds[t], 1 - c, shapes[t]),
                send_sem=send_sems.at[t], recv_sem=recv_sems.at[t], device_id=(x, y, 1 - c), device_id_type=MESH).start()
        token[...] = jnp.zeros_like(token)

    sems = pltpu.SemaphoreType.DMA((nt,))
    both = list(grads) + lands
    out = pl.pallas_call(
        body, name=name, in_specs=[HBM_SPEC] * (2 * nt),
        out_specs=(SEM_SPEC, SEM_SPEC, *[HBM_SPEC] * (2 * nt), pl.BlockSpec(memory_space=pltpu.VMEM)),
        out_shape=(sems, sems, *[pltpu.HBM(a.shape, a.dtype) for a in both], _sds((8, 128), F32)),
        input_output_aliases={t: 2 + t for t in range(2 * nt)}, compiler_params=_split_params(),
    )(*[_in_hbm(a) for a in both])
    return out[0], out[1], list(out[2:2 + nt]), list(out[2 + nt:2 + 2 * nt]), out[-1]


def swap_wait(name, send_sems, recv_sems, grads, lands, kinds, after):
    nt = len(grads)
    shapes = [tuple(g.shape) for g in grads]

    def body(*refs):
        in_refs, land_refs = refs[:nt], refs[nt:2 * nt]
        send_ref, recv_ref = refs[2 * nt], refs[2 * nt + 1]
        x, y, c, _ = _position()
        for t in range(nt):
            cp = pltpu.make_async_remote_copy(
                src_ref=_half(in_refs[t], kinds[t], 1 - c, shapes[t]), dst_ref=_half(land_refs[t], kinds[t], c, shapes[t]),
                send_sem=send_ref.at[t], recv_sem=recv_ref.at[t], device_id=(x, y, 1 - c), device_id_type=MESH)
            cp.wait_send()
            cp.wait_recv()

    both = list(grads) + list(lands)
    out = pl.pallas_call(
        body, name=name, in_specs=[HBM_SPEC] * (2 * nt) + [SEM_SPEC, SEM_SPEC, HBM_SPEC], out_specs=[HBM_SPEC] * (2 * nt),
        out_shape=[pltpu.HBM(a.shape, a.dtype) for a in both], input_output_aliases={t: t for t in range(2 * nt)},
        compiler_params=_split_params())(*both, send_sems, recv_sems, _in_hbm(after))
    return list(out[:nt]), list(out[nt:])


def _half_spec(kind, shape, tiles):
    r, n = shape
    if kind == "col":
        tn = n // tiles
        return pl.BlockSpec((r // 2, tn), lambda i, s: (s[0], i))
    tm = r // tiles
    return pl.BlockSpec((tm, n // 2), lambda i, s: (i, s[0]))


def add_halves(name, mine, landed, kinds, where, tiles=4):
    nt = len(mine)
    shapes = [tuple(a.shape) for a in mine]

    def compact(t):
        r, n = shapes[t]
        if kinds[t] == "col":
            return (r // 2, n), pl.BlockSpec((r // 2, n // tiles), lambda i, s: (0, i))
        return (r, n // 2), pl.BlockSpec((r // tiles, n // 2), lambda i, s: (i, 0))

    def body(s_ref, *refs):
        for a_ref, b_ref, o_ref in zip(refs[:nt], refs[nt:2 * nt], refs[2 * nt:]):
            o_ref[...] = (a_ref[...].astype(F32) + b_ref[...].astype(F32)).astype(BF16)

    specs = [_half_spec(kinds[t], shapes[t], tiles) for t in range(nt)]
    return pl.pallas_call(
        body, grid_spec=pltpu.PrefetchScalarGridSpec(num_scalar_prefetch=1, grid=(tiles,), in_specs=specs + specs,
                                                     out_specs=[compact(t)[1] for t in range(nt)]),
        out_shape=[_sds(compact(t)[0], BF16) for t in range(nt)], name=name,
        compiler_params=_params(("parallel",)))(where, *mine, *landed)


def sum_shards(name, parts, landed, kinds, shard_shapes, where, layers, n_layers, intos, tiles=2):
    nt = len(parts)
    in_specs, out_specs = [], []
    for t in range(nt):
        (r, n), layer = shard_shapes[t], layers[t]
        if kinds[t] == "col":
            tm, width = r // 2 // tiles, n
            own = pl.BlockSpec((tm, n), lambda i, s: (i, s[1]))
            out = pl.BlockSpec((None, tm, n), lambda i, s, layer=layer: (layer, s[0] * tiles + i, 0))
        else:
            tm, width = r // tiles, n // 2
            own = pl.BlockSpec((tm, n // 2), lambda i, s: (s[1] * tiles + i, 0))
            out = pl.BlockSpec((None, tm, n // 2), lambda i, s, layer=layer: (layer, i, s[0]))
        in_specs += [own, pl.BlockSpec((3, tm, width), lambda i, s: (0, i, 0))]
        out_specs.append(out)
    args, aliases = [where] + [a for pair in zip(parts, landed) for a in pair], {}
    for t in range(nt):
        if intos[t] is not None:
            aliases[len(args)] = t
            in_specs.append(pl.BlockSpec(memory_space=pl.ANY))
            args.append(intos[t])

    def body(s_ref, *refs):
        for t in range(nt):
            a_ref, l_ref, o_ref = refs[2 * t], refs[2 * t + 1], refs[len(in_specs) + t]
            o_ref[...] = ((a_ref[...].astype(F32) + l_ref[0].astype(F32)) + l_ref[1].astype(F32)) + l_ref[2].astype(F32)

    return pl.pallas_call(
        body, grid_spec=pltpu.PrefetchScalarGridSpec(num_scalar_prefetch=1, grid=(tiles,), in_specs=in_specs,
                                                     out_specs=out_specs),
        out_shape=[_sds((n_layers[t],) + tuple(shard_shapes[t]), F32) for t in range(nt)], input_output_aliases=aliases,
        name=name, compiler_params=_params(("parallel",)))(*args)


def share_halves(arrays, entries):
    na, nt = len(arrays), len(entries)

    def body(*refs):
        out_refs = refs[na:2 * na]
        send_sems, recv_sems = refs[2 * na:]
        x, y, c, _ = _position()
        cps = []
        for t, (a, layer, kind) in enumerate(entries):
            shape = tuple(arrays[a].shape[1:])
            mine = _half(out_refs[a].at[layer], kind, c, shape)
            cp = pltpu.make_async_remote_copy(
                src_ref=mine, dst_ref=mine, send_sem=send_sems.at[t], recv_sem=recv_sems.at[t],
                device_id=(x, y, 1 - c), device_id_type=MESH)
            cp.start()
            cps.append(cp)
        for t, (a, layer, kind) in enumerate(entries):
            shape = tuple(arrays[a].shape[1:])
            other = _half(out_refs[a].at[layer], kind, 1 - c, shape)
            pltpu.make_async_remote_copy(
                src_ref=other, dst_ref=other, send_sem=send_sems.at[t], recv_sem=recv_sems.at[t],
                device_id=(x, y, 1 - c), device_id_type=MESH).wait_recv()
        for cp in cps:
            cp.wait_send()

    hbm = pl.BlockSpec(memory_space=pl.ANY)
    return pl.pallas_call(
        body, in_specs=[hbm] * na, out_specs=[hbm] * na, out_shape=[_sds(a.shape, F32) for a in arrays],
        input_output_aliases={i: i for i in range(na)},
        scratch_shapes=[pltpu.SemaphoreType.DMA((nt,)), pltpu.SemaphoreType.DMA((nt,))],
        name="share_halves", compiler_params=_params())(*arrays)


HBM_SPEC = pl.BlockSpec(memory_space=pltpu.HBM)
SEM_SPEC = pl.BlockSpec(memory_space=pltpu.SEMAPHORE)
ANY_SPEC = pl.BlockSpec(memory_space=pl.ANY)


def _split_params():
    return pltpu.CompilerParams(has_side_effects=pltpu.SideEffectType.DATAFLOW_SIDE_EFFECTING,
                                vmem_limit_bytes=VMEM_LIMIT_BYTES)


def _in_hbm(a):
    return pltpu.with_memory_space_constraint(a, pltpu.HBM)


def cast_place(arrays, entries, where, tiles=2):
    in_specs, out_specs, fulls = [], [], []
    for a, layer, kind in entries:
        _, r, n = arrays[a].shape
        tm = r // tiles
        in_specs.append(pl.BlockSpec((None, tm, n), lambda i, s, layer=layer: (layer, i, 0)))
        if kind == "col":
            fulls.append((r, 4 * n))
            out_specs.append(pl.BlockSpec((tm, n), lambda i, s: (i, s[1])))
        else:
            fulls.append((4 * r, n))
            out_specs.append(pl.BlockSpec((tm, n), lambda i, s: (s[1] * tiles + i, 0)))
    nt = len(entries)

    def body(s_ref, *refs):
        for w_ref, o_ref in zip(refs[:nt], refs[nt:]):
            o_ref[...] = w_ref[...].astype(BF16)

    return pl.pallas_call(
        body, grid_spec=pltpu.PrefetchScalarGridSpec(num_scalar_prefetch=1, grid=(tiles,), in_specs=in_specs,
                                                     out_specs=out_specs),
        out_shape=[_sds(f, BF16) for f in fulls], name="cast_place",
        compiler_params=_params(("parallel",)))(where, *[arrays[a] for a, _, _ in entries])


def gather_start(fulls, kinds, shard_shapes):
    nt = len(fulls)

    def body(*refs):
        full_refs = refs[:nt]
        send_sems, recv_sems, token = refs[nt], refs[nt + 1], refs[-1]
        x, y, c, others = _position()
        for t in range(nt):
            mine = _window(full_refs[t], kinds[t], 2 * x + y, c, shard_shapes[t])
            for j, (ox, oy) in enumerate(others):
                pltpu.make_async_remote_copy(
                    src_ref=mine, dst_ref=mine, send_sem=send_sems.at[3 * t + j], recv_sem=recv_sems.at[3 * t + j],
                    device_id=(ox, oy, c), device_id_type=MESH).start()
        token[...] = jnp.zeros_like(token)

    sems = pltpu.SemaphoreType.DMA((3 * nt,))
    out = pl.pallas_call(
        body, name="gather_start", in_specs=[HBM_SPEC] * nt,
        out_specs=(SEM_SPEC, SEM_SPEC, *[HBM_SPEC] * nt, pl.BlockSpec(memory_space=pltpu.VMEM)),
        out_shape=(sems, sems, *[pltpu.HBM(f.shape, f.dtype) for f in fulls], _sds((8, 128), F32)),
        input_output_aliases={t: 2 + t for t in range(nt)}, compiler_params=_split_params(),
    )(*[_in_hbm(f) for f in fulls])
    return out[0], out[1], list(out[2:2 + nt]), out[-1]


def gather_wait(name, send_sems, recv_sems, fulls, kinds, shard_shapes, after, first):
    nt = len(fulls)
    extra = [] if after is None else [_in_hbm(after)]

    def body(*refs):
        full_refs, send_ref, recv_ref = refs[:nt], refs[nt], refs[nt + 1]
        x, y, c, others = _position()
        for t in range(nt):
            mine = _window(full_refs[t], kinds[t], 2 * x + y, c, shard_shapes[t])
            for j, (ox, oy) in enumerate(others):
                cp = pltpu.make_async_remote_copy(
                    src_ref=mine, dst_ref=_window(full_refs[t], kinds[t], 2 * ox + oy, c, shard_shapes[t]),
                    send_sem=send_ref.at[3 * (first + t) + j], recv_sem=recv_ref.at[3 * (first + t) + j],
                    device_id=(ox, oy, c), device_id_type=MESH)
                cp.wait_send()
                cp.wait_recv()

    out = pl.pallas_call(
        body, name=name, in_specs=[HBM_SPEC] * nt + [SEM_SPEC, SEM_SPEC] + [HBM_SPEC] * len(extra),
        out_specs=[HBM_SPEC] * nt, out_shape=[pltpu.HBM(f.shape, f.dtype) for f in fulls],
        input_output_aliases={t: t for t in range(nt)}, compiler_params=_split_params())(*fulls, send_sems, recv_sems, *extra)
    return list(out)


def forward_halves(name, fulls, kinds, shard_shapes):
    nt = len(fulls)

    def body(*refs):
        out_refs = refs[nt:2 * nt]
        send_sems, recv_sems = refs[2 * nt:]
        x, y, c, others = _position()
        cps = []
        for t in range(nt):
            for j, (ox, oy) in enumerate(others):
                landed = _window(out_refs[t], kinds[t], 2 * ox + oy, c, shard_shapes[t])
                cp = pltpu.make_async_remote_copy(
                    src_ref=landed, dst_ref=landed, send_sem=send_sems.at[3 * t + j], recv_sem=recv_sems.at[3 * t + j],
                    device_id=(x, y, 1 - c), device_id_type=MESH)
                cp.start()
                cps.append(cp)
        for t in range(nt):
            for j, (ox, oy) in enumerate(others):
                got = _window(out_refs[t], kinds[t], 2 * ox + oy, 1 - c, shard_shapes[t])
                pltpu.make_async_remote_copy(
                    src_ref=got, dst_ref=got, send_sem=send_sems.at[3 * t + j], recv_sem=recv_sems.at[3 * t + j],
                    device_id=(x, y, 1 - c), device_id_type=MESH).wait_recv()
        for cp in cps:
            cp.wait_send()

    out = pl.pallas_call(
        body, in_specs=[ANY_SPEC] * nt, out_specs=[ANY_SPEC] * nt, out_shape=[_sds(f.shape, f.dtype) for f in fulls],
        input_output_aliases={t: t for t in range(nt)},
        scratch_shapes=[pltpu.SemaphoreType.DMA((3 * nt,)), pltpu.SemaphoreType.DMA((3 * nt,))],
        name=name, compiler_params=_params())(*fulls)
    return list(out)


def _piece(ref, kind, chip, shard_shape):
    r, n = shard_shape
    if kind == "col":
        return ref.at[:, pl.ds(pl.multiple_of(chip * n, 128), n)]
    return ref.at[pl.ds(pl.multiple_of(chip * r, 16), r), :]


def _piece_shape(kind, shard_shape):
    r, n = shard_shape
    return (r // 2, n) if kind == "col" else (r, n // 2)


def exchange_start(name, parts, kinds, shard_shapes):
    nt = len(parts)
    lands = [lax.empty((3,) + _piece_shape(kinds[t], shard_shapes[t]), BF16) for t in range(nt)]

    def body(*refs):
        part_refs, land_refs = refs[:nt], refs[nt:2 * nt]
        send_sems, recv_sems, token = refs[2 * nt], refs[2 * nt + 1], refs[-1]
        x, y, c, others = _position()
        for t in range(nt):
            for j, (ox, oy) in enumerate(others):
                pltpu.make_async_remote_copy(
                    src_ref=_piece(part_refs[t], kinds[t], 2 * ox + oy, shard_shapes[t]), dst_ref=land_refs[t].at[j],
                    send_sem=send_sems.at[3 * t + j], recv_sem=recv_sems.at[3 * t + j],
                    device_id=(ox, oy, c), device_id_type=MESH).start()
        token[...] = jnp.zeros_like(token)

    sems = pltpu.SemaphoreType.DMA((3 * nt,))
    both = list(parts) + lands
    out = pl.pallas_call(
        body, name=name, in_specs=[HBM_SPEC] * (2 * nt),
        out_specs=(SEM_SPEC, SEM_SPEC, *[HBM_SPEC] * (2 * nt), pl.BlockSpec(memory_space=pltpu.VMEM)),
        out_shape=(sems, sems, *[pltpu.HBM(a.shape, a.dtype) for a in both], _sds((8, 128), F32)),
        input_output_aliases={t: 2 + t for t in range(2 * nt)}, compiler_params=_split_params(),
    )(*[_in_hbm(a) for a in both])
    return out[0], out[1], list(out[2:2 + nt]), list(out[2 + nt:2 + 2 * nt]), out[-1]


def exchange_wait(name, send_sems, recv_sems, parts, lands, kinds, shard_shapes, after):
    nt = len(parts)

    def body(*refs):
        part_refs, land_refs = refs[:nt], refs[nt:2 * nt]
        send_ref, recv_ref = refs[2 * nt], refs[2 * nt + 1]
        x, y, c, others = _position()
        for t in range(nt):
            for j, (ox, oy) in enumerate(others):
                cp = pltpu.make_async_remote_copy(
                    src_ref=_piece(part_refs[t], kinds[t], 2 * ox + oy, shard_shapes[t]), dst_ref=land_refs[t].at[j],
                    send_sem=send_ref.at[3 * t + j], recv_sem=recv_ref.at[3 * t + j],
                    device_id=(ox, oy, c), device_id_type=MESH)
                cp.wait_send()
                cp.wait_recv()

    both = list(parts) + list(lands)
    out = pl.pallas_call(
        body, name=name, in_specs=[HBM_SPEC] * (2 * nt) + [SEM_SPEC, SEM_SPEC, HBM_SPEC], out_specs=[HBM_SPEC] * (2 * nt),
        out_shape=[pltpu.HBM(a.shape, a.dtype) for a in both], input_output_aliases={t: t for t in range(2 * nt)},
        compiler_params=_split_params())(*both, send_sems, recv_sems, _in_hbm(after))
    return list(out[:nt]), list(out[nt:])


def all_reduce_small(name, bufs, wire):
    n = len(bufs)
    halves = [b.shape[0] // 2 for b in bufs]

    def body(*refs):
        in_refs, out_refs, lands, txs = refs[:n], refs[n:2 * n], refs[2 * n:3 * n], refs[3 * n:4 * n]
        send_sems, recv_sems = refs[4 * n:]
        x, y, c, _ = _position()
        mine = [pl.ds(pl.multiple_of(c * h, 8), h) for h in halves]
        other = [pl.ds(pl.multiple_of((1 - c) * h, 8), h) for h in halves]
        for s, peer in enumerate([(x, y, 1 - c), (1 - x, y, c), (x, 1 - y, c)]):
            cps = []
            for k in range(n):
                txs[k][...] = (in_refs[k][other[k], :] if s == 0 else out_refs[k][mine[k], :]).astype(wire[k])
                cp = pltpu.make_async_remote_copy(
                    src_ref=txs[k], dst_ref=lands[k].at[s], send_sem=send_sems.at[4 * k + s], recv_sem=recv_sems.at[4 * k + s],
                    device_id=peer, device_id_type=MESH)
                cp.start()
                cps.append(cp)
            for k, cp in enumerate(cps):
                cp.wait()
                own = in_refs[k][mine[k], :] if s == 0 else out_refs[k][mine[k], :]
                out_refs[k][mine[k], :] = own.astype(wire[k]).astype(F32) + lands[k][s].astype(F32)
        cps = []
        for k in range(n):
            cp = pltpu.make_async_remote_copy(
                src_ref=out_refs[k].at[mine[k]], dst_ref=out_refs[k].at[mine[k]], send_sem=send_sems.at[4 * k + 3],
                recv_sem=recv_sems.at[4 * k + 3], device_id=(x, y, 1 - c), device_id_type=MESH)
            cp.start()
            cps.append(cp)
        for cp in cps:
            cp.wait()

    vm = pl.BlockSpec(memory_space=pltpu.VMEM)
    out = pl.pallas_call(
        body, in_specs=[vm] * n, out_specs=[vm] * n, out_shape=[_sds(b.shape, F32) for b in bufs],
        scratch_shapes=[pltpu.VMEM((3, h, b.shape[1]), w) for h, b, w in zip(halves, bufs, wire)]
        + [pltpu.VMEM((h, b.shape[1]), w) for h, b, w in zip(halves, bufs, wire)]
        + [pltpu.SemaphoreType.DMA((4 * n,)), pltpu.SemaphoreType.DMA((4 * n,))],
        name=name, compiler_params=_params())(*bufs)
    return list(out)


def _local_step(x, target, small, need, emit_swap, emit_exchange):
    d = D_MODEL
    full = {}

    def after_token(vec, token):
        return vec if token is None else vec + token[0:1, 0:1]

    def token_rows(token, width):
        return [] if token is None else [after_token(jnp.zeros((1, width), F32), token)]

    def plus(acc, rows):
        return acc + rows[0] if rows else acc

    rb16, rbt16, rc16, rct16, lr_t, li_t = small["s5_operands"]
    ge, y2, cs = s5_fwd(x, small["norm_mix0"], small["s5_d"], rb16, rc16, lr_t, li_t)
    full.update(need("glu", ge))

    def norm_rows(h, gains):
        xh, _ = _rms_hat(h)
        return [xh * g for g in gains]

    def glu_epilogue(accs, e, r):
        v, gt = accs[0] + r[0], accs[1] + r[1]
        h = e[0] + v * jax.nn.sigmoid(gt)
        return [h, v, gt] + norm_rows(h, r[2:])

    h1, val, gate, n1 = mm_nn(
        "glu", ge, full["w_glu"], [0, d], d, glu_epilogue, [F32, F32, F32, BF16], extras=[x],
        rowvecs=[(small["s5_b_glu"], 0), (small["s5_b_glu"], d), (small["norm_mlp0"], 0)], tm=512, tn=d)

    def mlp_fwd(tag, h, n, w_in, get_w_out, next_gains, head=None):
        def in_epilogue(accs, e, rv):
            pos = jnp.maximum(accs[0], 0.0)
            return [pos * pos, 2.0 * pos]

        r, slope = mm_nn("mlp_in" + tag, n, w_in, [0], w_in.shape[1], in_epilogue, [BF16, BF16], tm=2048)
        w_out = get_w_out(r)

        def epilogue(accs, e, rv):
            h_out = e[0] + accs[0]
            return [h_out] + norm_rows(h_out, rv)

        if head is not None:
            return head(r, w_out, h), (n, r, slope)
        outs = mm_nn("mlp_out" + tag, r, w_out, [0], d, epilogue, [F32] + [BF16] * len(next_gains), extras=[h],
                     rowvecs=[(g, 0) for g in next_gains], tm=512, tn=d)
        return outs[0], outs[1:], (n, r, slope)

    full.update(need("mlp_in0", h1))

    def w_out0(after):
        full.update(need("mlp_out0", after))
        return full["w_out0"]

    h2, (nkv, n2), mlp0 = mlp_fwd("0", h1, n1, full["w_in0"], w_out0, [small["norm_kv"], small["norm_mix1"]])

    full.update(need("attn", h2))
    kvw = 2 * N_KV * HEAD_DIM
    (kv,) = mm_nn("kv_proj", nkv, full["w_kv"], [0], kvw, lambda accs, e, r: [accs[0] + r[0]], [BF16],
                  rowvecs=[(small["b_kv"], 0)], tm=2048)
    (q,) = mm_nn("q_proj", n2, full["w_q"], [0], d, lambda accs, e, r: [accs[0] + r[0]], [BF16],
                 rowvecs=[(small["b_q"], 0)], tm=2048)
    sinks = small["sinks"].reshape(N_Q)
    o = attn_fwd(q, kv, sinks)
    def o_epilogue(accs, e, r):
        h_out = e[0] + accs[0] + r[0]
        return [h_out] + norm_rows(h_out, r[1:])

    h3, n3 = mm_nn("o_proj", o, full["w_o"], [0], d, o_epilogue, [F32, BF16], extras=[h2],
                   rowvecs=[(small["b_o"], 0), (small["norm_mlp1"], 0)], tm=512, tn=d)
    full.update(need("mlp_in1", h3))

    def w_out1(after):
        full.update(need("mlp_out1", after))
        return full["w_out1"]

    def loss_head(r, w_out, h):
        def epilogue(accs, e, rv):
            xh, rr = _rms_hat(e[0] + accs[0])
            err = xh * rv[0] - e[1]
            dy = err * (1.0 / d)
            dxh = dy * rv[0]
            dx = rr * (dxh - xh * jnp.mean(dxh * xh, axis=-1, keepdims=True))
            loss = jnp.full((1, d), 0.5 * jnp.sum(jnp.mean(err * err, axis=-1, keepdims=True)), F32)
            return [dx, dx, loss, jnp.sum(dy * xh, axis=0, keepdims=True)]

        return mm_nn("mlp_out1", r, w_out, [0], d, epilogue, [F32, BF16], extras=[h, target],
                     rowvecs=[(small["norm_final"], 0)], n_sums=2, tm=512, tn=d)

    (dh, dhb, loss_tile, dg_final), mlp1 = mlp_fwd("1", h3, n3, full["w_in1"], w_out1, [], head=loss_head)

    grads_small, grads_full = {"norm_final": dg_final}, {}
    ident = lambda acc, e, r: [plus(acc, r)]
    layer1 = ["w_out1", "w_in1", "w_o", "w_q", "w_kv"]
    layer0 = ["w_out0", "w_in0", "w_glu"]

    def norm_bwd_rows(x_rows, res, dys, gains):
        xh, r = _rms_hat(x_rows)
        dxh = sum(dy * g for dy, g in zip(dys, gains))
        dx = r * (dxh - xh * jnp.mean(dxh * xh, axis=-1, keepdims=True)) + res
        return dx, [jnp.sum(dy * xh, axis=0, keepdims=True) for dy in dys]

    def mlp_bwd(tag, dh, dhb, h_in, gain, w_in, w_out, saved, token=None):
        n, r, slope = saved
        grads_full["w_out" + tag] = mm_tn("dw_out" + tag, r, dhb, tn=1024)
        (da,) = mm_nt("mlp_da" + tag, dhb, w_out, lambda acc, e, rv: [plus(acc * e[0].astype(F32), rv)], [BF16],
                      extras=[slope], rowvecs=token_rows(token, w_out.shape[0]), tm=2048)
        grads_full["w_in" + tag] = mm_tn("dw_in" + tag, n, da, tn=1024)

        def epilogue(acc, e, rv):
            dx, dgs = norm_bwd_rows(e[0], e[1], [acc], rv)
            return [dx, dx, jnp.sum(dx, axis=0, keepdims=True)] + dgs

        dx, dxb, colsum, dg = mm_nt("mlp_dn" + tag, da, w_in, epilogue, [F32, BF16], extras=[h_in, dh], rowvecs=[gain],
                                    n_sums=2, tm=512, tk=d)
        grads_small["norm_mlp" + tag] = dg
        return dx, dxb, colsum

    dh3, dh3b, colsum3 = mlp_bwd("1", dh, dhb, h3, small["norm_mlp1"], full["w_in1"], full["w_out1"], mlp1)
    grads_small["b_o"] = colsum3
    grads_full["w_o"] = mm_tn("dw_o", o, dh3b, tn=1024)
    (do,) = mm_nt("attn_do", dh3b, full["w_o"], ident, [BF16], tm=2048)
    dq, dbq, dprev, dcur, dsink = attn_bwd(q, kv, do, sinks)
    dkv, dbkv = kv_combine(dprev, dcur)
    grads_small["b_q"], grads_small["b_kv"], grads_small["sinks"] = dbq, dbkv, dsink
    grads_full["w_q"] = mm_tn("dw_q", n2, dq, tn=1024)
    grads_full["w_kv"] = mm_tn("dw_kv", nkv, dkv, tk=1024)
    token = emit_swap("layer1", {n: grads_full[n] for n in layer1})
    (dnkv,) = mm_nt("kv_dn", dkv, full["w_kv"], ident, [F32], rowvecs=token_rows(token, d), tm=2048, tk=1024)

    def attn_dn_epilogue(acc, e, rv):
        dx, dgs = norm_bwd_rows(e[0], e[1], [acc, e[2]], rv)
        return [dx, dx] + dgs

    dh2, dh2b, dg_mix1, dg_kv = mm_nt("attn_dn", dq, full["w_q"], attn_dn_epilogue, [F32, BF16], extras=[h2, dh3, dnkv],
                                      rowvecs=[small["norm_mix1"], small["norm_kv"]], n_sums=2, tm=512, tk=d)
    grads_small["norm_mix1"], grads_small["norm_kv"] = dg_mix1, dg_kv
    token = emit_exchange("layer1", dh2b)
    dh1, _, _ = mlp_bwd("0", dh2, dh2b, h1, small["norm_mlp0"], full["w_in0"], full["w_out0"], mlp0, token)

    dz, db_glu = glu_bwd(dh1, val, gate)
    grads_small["s5_b_glu"] = db_glu
    grads_full["w_glu"] = mm_tn("dw_glu", ge, dz, tn=1024)
    token = emit_swap("layer0", {n: grads_full[n] for n in layer0})
    (dy2,) = mm_nt("glu_dy", dz, full["w_glu"], lambda acc, e, rv: [plus(acc, rv) * _gelu_grad(e[0])], [F32], extras=[y2],
                   rowvecs=token_rows(token, d), tm=1024, tk=1024)
    token = emit_exchange("layer0", dy2)
    grad_x, dd, drb, drc, dlr, dli, dg_mix0 = s5_bwd(x, small["norm_mix0"], dy2, dh1, after_token(small["s5_d"], token), cs,
                                                     rb16, rbt16, rct16, lr_t, li_t)
    grads_small["s5_d"] = dd
    grads_small["s5_mats"] = (drb, drc, dlr, dli)
    grads_small["norm_mix0"] = dg_mix0
    return loss_tile, grad_x, grads_small


SMALL_NAMES = ["norm_mix", "norm_mlp", "norm_kv", "norm_final", "s5_a_re", "s5_a_im", "s5_log_dt", "s5_b_re", "s5_b_im",
               "s5_c_re", "s5_c_im", "s5_d", "s5_b_glu", "b_kv", "b_q", "sinks", "b_o"]
BIG_NAMES = ["s5_w_glu", "w_kv", "w_q", "w_o", "w_mlp_in", "w_mlp_out"]
WEIGHT_ORDER = ["norm_mix", "norm_mlp", "norm_kv", "norm_final", "s5_a_re", "s5_a_im", "s5_log_dt", "s5_b_re", "s5_b_im",
                "s5_c_re", "s5_c_im", "s5_d", "s5_w_glu", "s5_b_glu", "w_kv", "b_kv", "w_q", "b_q", "sinks", "w_o", "b_o",
                "w_mlp_in", "w_mlp_out"]


def kernel(x, norm_mix, norm_mlp, norm_kv, norm_final, s5_a_re, s5_a_im, s5_log_dt, s5_b_re, s5_b_im, s5_c_re, s5_c_im, s5_d, s5_w_glu, s5_b_glu, w_kv, b_kv, w_q, b_q, sinks, w_o, b_o, w_mlp_in, w_mlp_out, loss_target, m_norm_mix, m_norm_mlp, m_norm_kv, m_norm_final, m_s5_a_re, m_s5_a_im, m_s5_log_dt, m_s5_b_re, m_s5_b_im, m_s5_c_re, m_s5_c_im, m_s5_d, m_s5_w_glu, m_s5_b_glu, m_w_kv, m_b_kv, m_w_q, m_b_q, m_sinks, m_w_o, m_b_o, m_w_mlp_in, m_w_mlp_out, v_norm_mix, v_norm_mlp, v_norm_kv, v_norm_final, v_s5_a_re, v_s5_a_im, v_s5_log_dt, v_s5_b_re, v_s5_b_im, v_s5_c_re, v_s5_c_im, v_s5_d, v_s5_w_glu, v_s5_b_glu, v_w_kv, v_b_kv, v_w_q, v_b_q, v_sinks, v_w_o, v_b_o, v_w_mlp_in, v_w_mlp_out):
    env = dict(locals())
    w = {n: env[n] for n in WEIGHT_ORDER}
    mom = {n: env["m_" + n] for n in WEIGHT_ORDER}
    var = {n: env["v_" + n] for n in WEIGHT_ORDER}
    d = D_MODEL
    xi, yi, ci = lax.axis_index("x"), lax.axis_index("y"), lax.axis_index("c")
    chip = 2 * xi + yi
    where = jnp.stack([ci, chip]).astype(jnp.int32)

    dsh, bsh = s5_d.shape[1], s5_b_glu.shape[1]
    packed = jnp.concatenate([s5_d.reshape(-1, 128), s5_b_glu.reshape(-1, 128)])
    n_d, n_b = dsh // 128, bsh // 128
    slab = lax.dynamic_update_slice(jnp.zeros((4, 8, 128), F32), jnp.pad(packed, ((0, 8 - n_d - n_b), (0, 0)))[None],
                                    (chip, 0, 0))

    big = [s5_w_glu, w_kv[None], w_q, w_o, w_mlp_in, w_mlp_out]
    entries = [(0, 0, "col"), (1, 0, "row"), (2, 0, "row"), (3, 0, "row"), (4, 0, "col"), (4, 1, "col"),
               (5, 0, "row"), (5, 1, "row")]
    names = ["w_glu", "w_kv", "w_q", "w_o", "w_in0", "w_in1", "w_out0", "w_out1"]
    kinds = dict(zip(names, [k for _, _, k in entries]))
    shard_shapes = dict(zip(names, [tuple(big[a].shape[1:]) for a, _, _ in entries]))

    placed_w = dict(zip(names, cast_place(big, entries, where)))
    placed_w["vectors"], kinds["vectors"], shard_shapes["vectors"] = slab, "slab", None
    gather_groups = {"glu": ["w_glu"], "mlp_in0": ["w_in0"], "mlp_out0": ["w_out0"], "attn": ["w_kv", "w_q", "w_o"],
                     "mlp_in1": ["w_in1"], "mlp_out1": ["w_out1"]}
    order = ["vectors"] + [n for members in gather_groups.values() for n in members]
    send, recv, thru, token = gather_start([placed_w[n] for n in order], [kinds[n] for n in order],
                                           [shard_shapes[n] for n in order])
    started = dict(zip(order, thru))
    (gathered_rows,) = gather_wait("gather_wait_vectors", send, recv, [started["vectors"]], ["slab"], [None], None, 0)
    d_full = gathered_rows[:, 0:n_d].reshape(1, -1)
    bglu_full = gathered_rows[:, n_d:n_d + n_b].reshape(1, -1)

    def need(group, after):
        members = gather_groups[group]
        ks, shapes = [kinds[n] for n in members], [shard_shapes[n] for n in members]
        landed = gather_wait("gather_wait_" + group, send, recv, [started[n] for n in members], ks, shapes, after,
                             order.index(members[0]))
        return dict(zip(members, forward_halves("forward_halves_" + group, landed, ks, shapes)))

    swapping, exchanging = {}, {}

    def emit_swap(group, partial):
        members = list(partial)
        send, recv, mine, lands, tok = swap_start("swap_start_" + group, [partial[n] for n in members],
                                                  [kinds[n] for n in members])
        swapping[group] = (members, send, recv, mine, lands)
        return tok

    def emit_exchange(group, after):
        members, send, recv, mine, lands = swapping[group]
        ks, shapes = [kinds[n] for n in members], [shard_shapes[n] for n in members]
        mine, landed = swap_wait("swap_wait_" + group, send, recv, mine, lands, ks, after)
        sums = add_halves("add_halves_" + group, mine, landed, ks, where)
        send, recv, parts, lands, tok = exchange_start("exchange_start_" + group, sums, ks, shapes)
        exchanging[group] = (members, send, recv, parts, lands)
        return tok

    s5_args = (s5_a_re[0], s5_a_im[0], s5_log_dt[0], s5_b_re[0], s5_b_im[0])
    small = {
        "norm_mix0": norm_mix[0:1] + token[0:1, 0:1], "norm_mix1": norm_mix[1:2], "norm_mlp0": norm_mlp[0:1], "norm_mlp1": norm_mlp[1:2],
        "norm_kv": norm_kv.reshape(1, d), "norm_final": norm_final.reshape(1, d), "s5_operands": s5_prep(*s5_args, s5_c_re[0], s5_c_im[0]),
        "s5_d": d_full, "s5_b_glu": bglu_full,
        "b_kv": b_kv.reshape(1, -1), "b_q": b_q, "sinks": sinks, "b_o": b_o,
    }
    loss_row, grad_x, gs = _local_step(x[0], loss_target[0], small, need, emit_swap, emit_exchange)

    mats, lams = s5_compact(*gs["s5_mats"])
    rows = [gs["norm_mix0"], gs["norm_mix1"], gs["norm_mlp0"], gs["norm_mlp1"], gs["norm_kv"], gs["norm_final"], gs["s5_d"],
            gs["b_q"], gs["b_o"], gs["s5_b_glu"], gs["b_kv"], gs["sinks"], loss_row, jnp.zeros((2, d), F32)]
    vecs, lams, mats = all_reduce_small("reduce_small", [jnp.concatenate(rows, axis=0), lams, mats], [F32, F32, BF16])
    grads = split_vectors(where, vecs, dsh, bsh)
    loss = grads.pop("loss")[0, 0]
    g_are, g_aim, g_dt, g_bre, g_bim, dc_re, dc_im = s5_param_bwd(mats, lams, *s5_args)
    grads.update({"s5_a_re": g_are[None], "s5_a_im": g_aim[None], "s5_log_dt": g_dt[None], "s5_b_re": g_bre[None],
                  "s5_b_im": g_bim[None], "s5_c_re": dc_re[None], "s5_c_im": dc_im[None]})

    reduced = [None] * len(big)
    where_of = dict(zip(names, entries))
    for group, after in (("layer1", grad_x), ("layer0", mats)):
        members, send, recv, parts, lands = exchanging[group]
        ks, shapes = [kinds[n] for n in members], [shard_shapes[n] for n in members]
        parts, lands = exchange_wait("exchange_wait_" + group, send, recv, parts, lands, ks, shapes, after)
        targets = [where_of[n][0] for n in members]
        sums = sum_shards("sum_shards_" + group, parts, lands, ks, shapes, where, [where_of[n][1] for n in members],
                          [big[a].shape[0] for a in targets], [reduced[a] for a in targets])
        for a, arr in zip(targets, sums):
            reduced[a] = arr
    reduced = share_halves(reduced, entries)
    for n, g in zip(BIG_NAMES, reduced):
        grads[n] = g.reshape(w[n].shape)

    delta, new_m, new_v = {}, {}, {}
    for n in BIG_NAMES:
        flat = lambda a: a.reshape(-1, a.shape[-1])
        go, dl, nm, nv = adamw("adamw_" + n, flat(w[n]), flat(grads[n]), flat(mom[n]), flat(var[n]))
        grads[n], delta[n], new_m[n], new_v[n] = (t.reshape(w[n].shape) for t in (go, dl, nm, nv))

    def view(n, a):
        return a.reshape(1, -1) if a.ndim == 1 else jnp.swapaxes(a, -1, -2) if n in ("s5_b_re", "s5_b_im") else a

    sw, sg, sm, sv = ([view(n, t[n]) for n in SMALL_NAMES] for t in (w, grads, mom, var))
    for n, a, b, c_ in zip(SMALL_NAMES, *adamw_native("adamw_small", sw, sg, sm, sv)):
        delta[n], new_m[n], new_v[n] = (view(n, t) if t.ndim == 4 else t for t in (a, b, c_))

    out = [loss.reshape(()), grad_x[None]]
    for table in (grads, delta, new_m, new_v):
        out += [table[n].reshape(w[n].shape) for n in WEIGHT_ORDER]
    return tuple(out)
```

```python
import math

import jax
import jax.numpy as jnp
from jax import lax
from jax.experimental import pallas as pl
from jax.experimental.pallas import tpu as pltpu

F32 = jnp.float32
BF16 = jnp.bfloat16

D_MODEL = 1024
S5_GROUPS = 64
S5_GROUP = 16
S5_STATE = 64
N_KV = 4
N_Q = 16
HEAD_DIM = 64
BLOCK = 128
NORM_EPS = 1e-5
LAMBDA_RE_MAX = -1e-4
ADAM_LR, ADAM_B1, ADAM_B2, ADAM_EPS, ADAM_WD, ADAM_STEP = 0.001, 0.9, 0.999, 1e-08, 0.01, 10

VMEM_LIMIT_BYTES = 56 * 1024 * 1024
S5_CHUNK = 256
S5_BLOCKS = 4
MESH = pl.DeviceIdType.MESH


def _params(sem=None):
    return pltpu.CompilerParams(dimension_semantics=sem, vmem_limit_bytes=VMEM_LIMIT_BYTES)


def _sds(shape, dtype):
    return jax.ShapeDtypeStruct(shape, dtype)


def _rms_hat(xv):
    r = lax.rsqrt(jnp.mean(xv * xv, axis=-1, keepdims=True) + NORM_EPS)
    return xv * r, r


def mm_nn(name, a, w, col_offsets, n_out, epilogue, out_dtypes, extras=(), rowvecs=(), n_sums=0, tm=1024, tn=512):
    m, k = a.shape
    tm, tn = min(tm, m), min(tn, n_out)
    nw, ne, nr, no = len(col_offsets), len(extras), len(rowvecs), len(out_dtypes)

    def body(a_ref, *refs):
        w_refs, e_refs, r_refs = refs[:nw], refs[nw:nw + ne], refs[nw + ne:nw + ne + nr]
        o_refs, s_refs = refs[nw + ne + nr:nw + ne + nr + no], refs[nw + ne + nr + no:]
        av = a_ref[...]
        accs = [jnp.dot(av, w_ref[...], preferred_element_type=F32) for w_ref in w_refs]
        outs = epilogue(accs, [e[...] for e in e_refs], [r[...] for r in r_refs])
        for o_ref, o in zip(o_refs, outs[:no]):
            o_ref[...] = o.astype(o_ref.dtype)
        if n_sums:
            @pl.when(pl.program_id(1) == 0)
            def _():
                for s_ref in s_refs:
                    s_ref[...] = jnp.zeros_like(s_ref)

            for s_ref, val in zip(s_refs, outs[no:]):
                s_ref[...] += val

    def wspec(off):
        return pl.BlockSpec((k, tn), lambda j, i, off=off: (0, off // tn + j))

    def rspec(off):
        return pl.BlockSpec((1, tn), lambda j, i, off=off: (0, off // tn + j))

    tile = pl.BlockSpec((tm, tn), lambda j, i: (i, j))
    in_specs = ([pl.BlockSpec((tm, k), lambda j, i: (i, 0))] + [wspec(o) for o in col_offsets]
                + [tile] * ne + [rspec(o) for _, o in rowvecs])
    sem = ("parallel", "arbitrary") if n_sums else ("parallel", "parallel")
    return pl.pallas_call(
        body, grid=(n_out // tn, m // tm), in_specs=in_specs,
        out_specs=[tile] * no + [pl.BlockSpec((1, tn), lambda j, i: (0, j))] * n_sums,
        out_shape=[_sds((m, n_out), dt) for dt in out_dtypes] + [_sds((1, n_out), F32)] * n_sums, name=name,
        compiler_params=_params(sem))(a, *([w] * nw), *extras, *[r for r, _ in rowvecs])


def mm_nt(name, g, w, epilogue, out_dtypes, extras=(), rowvecs=(), n_sums=0, tm=512, tk=512):
    m, n = g.shape
    k = w.shape[0]
    tm, tk = min(tm, m), min(tk, k)
    ne, nr, no = len(extras), len(rowvecs), len(out_dtypes)

    def body(g_ref, w_ref, *refs):
        e_refs, r_refs, o_refs, s_refs = refs[:ne], refs[ne:ne + nr], refs[ne + nr:ne + nr + no], refs[ne + nr + no:]
        acc = lax.dot_general(g_ref[...], w_ref[...], (((1,), (1,)), ((), ())), preferred_element_type=F32)
        outs = epilogue(acc, [e[...] for e in e_refs], [r[...] for r in r_refs])
        for o_ref, o in zip(o_refs, outs[:no]):
            o_ref[...] = o.astype(o_ref.dtype)
        if n_sums:
            @pl.when(pl.program_id(0) == 0)
            def _():
                for s_ref in s_refs:
                    s_ref[...] = jnp.zeros_like(s_ref)

            for s_ref, val in zip(s_refs, outs[no:]):
                s_ref[...] += val

    tile = pl.BlockSpec((tm, tk), lambda i, j: (i, j))
    vec = pl.BlockSpec((1, tk), lambda i, j: (0, j))
    sem = ("arbitrary", "parallel") if n_sums else ("parallel", "parallel")
    return pl.pallas_call(
        body, grid=(m // tm, k // tk),
        in_specs=[pl.BlockSpec((tm, n), lambda i, j: (i, 0)), pl.BlockSpec((tk, n), lambda i, j: (j, 0))]
        + [tile] * ne + [vec] * nr,
        out_specs=[tile] * no + [vec] * n_sums,
        out_shape=[_sds((m, k), dt) for dt in out_dtypes] + [_sds((1, k), F32)] * n_sums, name=name,
        compiler_params=_params(sem))(g, w, *extras, *rowvecs)


def mm_tn(name, a, g, tk=512, tn=512):
    m, k = a.shape
    n = g.shape[1]
    tk, tn = min(tk, k), min(tn, n)

    def body(a_ref, g_ref, o_ref):
        acc = lax.dot_general(a_ref[...], g_ref[...], (((0,), (0,)), ((), ())), preferred_element_type=F32)
        o_ref[...] = acc.astype(o_ref.dtype)

    return pl.pallas_call(
        body, grid=(k // tk, n // tn),
        in_specs=[pl.BlockSpec((m, tk), lambda i, j: (0, i)), pl.BlockSpec((m, tn), lambda i, j: (0, j))],
        out_specs=pl.BlockSpec((tk, tn), lambda i, j: (i, j)), out_shape=_sds((k, n), BF16), name=name,
        compiler_params=_params(("parallel", "parallel")))(a, g)


def _row_mask(tc):
    row = lax.broadcasted_iota(jnp.int32, (8 * tc, 256), 0) % 8
    col = lax.broadcasted_iota(jnp.int32, (8 * tc, 256), 1) // 32
    return row == col


def _expand_rows(val, mask):
    tc, width = val.shape
    rep = jnp.broadcast_to(val[:, None, :], (tc, 8, width)).reshape(8 * tc, width)
    return jnp.where(mask, rep, 0.0).astype(BF16)


def _stage(ref, val):
    ref[0] = val[:, 0:128]
    ref[1] = val[:, 128:256]


def _gather_rows(src_ref, tc):
    halves = []
    for half in range(2):
        col = lax.broadcasted_iota(jnp.int32, (tc, 128), 1) // 32 + 4 * half
        out = jnp.zeros((tc, 128), F32)
        for s8 in range(4 * half, 4 * half + 4):
            out = jnp.where(col == s8, src_ref.at[half][pl.ds(s8, tc, stride=8), :], out)
        halves.append(out)
    return jnp.concatenate(halves, axis=1)


def _gelu(x):
    c = math.sqrt(2.0 / math.pi)
    return 0.5 * x * (1.0 + jnp.tanh(c * (x + 0.044715 * x * x * x)))


def _gelu_grad(x):
    c = math.sqrt(2.0 / math.pi)
    t = jnp.tanh(c * (x + 0.044715 * x * x * x))
    return 0.5 * (1.0 + t) + 0.5 * x * (1.0 - t * t) * c * (1.0 + 3.0 * 0.044715 * x * x)


def s5_fwd(x, gain, d_skip, rb, rc, lam_r, lam_i):
    n_rows = x.shape[0]
    tc = min(S5_CHUNK, n_rows)
    nc = n_rows // tc

    def body(x_ref, g_ref, d_ref, rb_ref, rc_ref, lr_ref, li_ref, ge_ref, y2_ref, cs_ref, bux, yrows, carry):
        i = pl.program_id(0)
        u = _rms_hat(x_ref[...])[0] * g_ref[...]

        @pl.when(i == 0)
        def _():
            carry[...] = jnp.zeros_like(carry)

        cs_ref[0] = carry[...]
        mask = _row_mask(tc)
        for blk in range(S5_BLOCKS):
            lhs = _expand_rows(u[:, blk * 256:(blk + 1) * 256], mask)
            bux[blk] = jnp.dot(lhs, rb_ref[blk], preferred_element_type=F32)
        lam = [(lr_ref[blk], li_ref[blk]) for blk in range(S5_BLOCKS)]

        def step(t, c):
            r0 = pl.multiple_of(t * 8, 8)
            new = []
            for blk in range(S5_BLOCKS):
                xr, xi = c[2 * blk], c[2 * blk + 1]
                lr, li = lam[blk]
                nr = lr * xr - li * xi + bux[blk, pl.ds(r0, 8), 0:128]
                ni = lr * xi + li * xr + bux[blk, pl.ds(r0, 8), 128:256]
                bux[blk, pl.ds(r0, 8), 0:128] = nr
                bux[blk, pl.ds(r0, 8), 128:256] = ni
                new += [nr, ni]
            return tuple(new)

        c0 = []
        for blk in range(S5_BLOCKS):
            c0 += [carry[blk, :, 0:128], carry[blk, :, 128:256]]
        cn = lax.fori_loop(0, tc, step, tuple(c0), unroll=4)
        for blk in range(S5_BLOCKS):
            carry[blk, :, 0:128] = cn[2 * blk]
            carry[blk, :, 128:256] = cn[2 * blk + 1]
        for blk in range(S5_BLOCKS):
            _stage(yrows, jnp.dot(bux[blk].astype(BF16), rc_ref[blk], preferred_element_type=F32))
            sl = slice(blk * 256, (blk + 1) * 256)
            y2 = _gather_rows(yrows, tc) + d_ref[:, sl] * u[:, sl]
            y2_ref[:, sl] = y2
            ge_ref[:, sl] = _gelu(y2).astype(BF16)

    row = pl.BlockSpec((tc, D_MODEL), lambda i: (i, 0))
    vec = pl.BlockSpec((1, D_MODEL), lambda i: (0, 0))
    mat = pl.BlockSpec((S5_BLOCKS, 256, 256), lambda i: (0, 0, 0))
    lamspec = pl.BlockSpec((S5_BLOCKS, 8, 128), lambda i: (0, 0, 0))
    return pl.pallas_call(
        body, grid=(nc,),
        in_specs=[row, vec, vec, mat, mat, lamspec, lamspec],
        out_specs=[row, row, pl.BlockSpec((1, S5_BLOCKS, 8, 256), lambda i: (i, 0, 0, 0))],
        out_shape=[_sds((n_rows, D_MODEL), BF16), _sds((n_rows, D_MODEL), F32), _sds((nc, S5_BLOCKS, 8, 256), F32)],
        scratch_shapes=[pltpu.VMEM((S5_BLOCKS, 8 * tc, 256), F32), pltpu.VMEM((2, 8 * tc, 128), F32),
                        pltpu.VMEM((S5_BLOCKS, 8, 256), F32)],
        name="s5_fwd", compiler_params=_params(("arbitrary",)))(x, gain, d_skip, rb, rc, lam_r, lam_i)


def s5_bwd(x, gain, dy2, res, d_skip, cs, rb, rbt, rct, lam_r, lam_i):
    n_rows = x.shape[0]
    tc = min(S5_CHUNK, n_rows)
    nc = n_rows // tc

    def body(x_ref, g_ref, dy_ref, res_ref, d_ref, cs_ref, rb_ref, rbt_ref, rct_ref, lr_ref, li_ref,
             dx_ref, dd_ref, drb_ref, drc_ref, dlr_ref, dli_ref, dg_ref, tmp, du, lhsu, lhsd, xs, adj, acarry):
        i = pl.program_id(0)
        u = _rms_hat(x_ref[...])[0] * g_ref[...]

        @pl.when(i == 0)
        def _():
            acarry[...] = jnp.zeros_like(acarry)
            dd_ref[...] = jnp.zeros_like(dd_ref)
            drb_ref[...] = jnp.zeros_like(drb_ref)
            drc_ref[...] = jnp.zeros_like(drc_ref)
            dlr_ref[...] = jnp.zeros_like(dlr_ref)
            dli_ref[...] = jnp.zeros_like(dli_ref)
            dg_ref[...] = jnp.zeros_like(dg_ref)

        dd_ref[...] += jnp.sum(dy_ref[...] * u, axis=0, keepdims=True)
        mask = _row_mask(tc)
        for blk in range(S5_BLOCKS):
            sl = slice(blk * 256, (blk + 1) * 256)
            lhsu[blk] = _expand_rows(u[:, sl], mask)
            xs[blk] = jnp.dot(lhsu[blk], rb_ref[blk], preferred_element_type=F32)
            lhsd[blk] = _expand_rows(dy_ref[:, sl], mask)
            adj[blk] = jnp.dot(lhsd[blk], rct_ref[blk], preferred_element_type=F32)
        lam = [(lr_ref[blk], li_ref[blk]) for blk in range(S5_BLOCKS)]

        def fstep(t, c):
            r0 = pl.multiple_of(t * 8, 8)
            new = []
            for blk in range(S5_BLOCKS):
                xr, xi = c[2 * blk], c[2 * blk + 1]
                lr, li = lam[blk]
                nr = lr * xr - li * xi + xs[blk, pl.ds(r0, 8), 0:128]
                ni = lr * xi + li * xr + xs[blk, pl.ds(r0, 8), 128:256]
                xs[blk, pl.ds(r0, 8), 0:128] = nr
                xs[blk, pl.ds(r0, 8), 128:256] = ni
                new += [nr, ni]
            return tuple(new)

        c0 = []
        for blk in range(S5_BLOCKS):
            c0 += [cs_ref[0, blk, :, 0:128], cs_ref[0, blk, :, 128:256]]
        lax.fori_loop(0, tc, fstep, tuple(c0), unroll=4)

        def bstep(k, c):
            t = tc - 1 - k
            r0 = pl.multiple_of(t * 8, 8)
            rp = pl.multiple_of(jnp.maximum(t - 1, 0) * 8, 8)
            first = t == 0
            new_a, new_g = [], []
            for blk in range(S5_BLOCKS):
                ar, ai = c[0][2 * blk], c[0][2 * blk + 1]
                glr, gli = c[1][2 * blk], c[1][2 * blk + 1]
                lr, li = lam[blk]
                nr = lr * ar + li * ai + adj[blk, pl.ds(r0, 8), 0:128]
                ni = lr * ai - li * ar + adj[blk, pl.ds(r0, 8), 128:256]
                adj[blk, pl.ds(r0, 8), 0:128] = nr
                adj[blk, pl.ds(r0, 8), 128:256] = ni
                pr = jnp.where(first, cs_ref[0, blk, :, 0:128], xs[blk, pl.ds(rp, 8), 0:128])
                pi = jnp.where(first, cs_ref[0, blk, :, 128:256], xs[blk, pl.ds(rp, 8), 128:256])
                new_a += [nr, ni]
                new_g += [glr + nr * pr + ni * pi, gli + ni * pr - nr * pi]
            return tuple(new_a), tuple(new_g)

        a0, g0 = [], []
        for blk in range(S5_BLOCKS):
            a0 += [acarry[blk, :, 0:128], acarry[blk, :, 128:256]]
            g0 += [dlr_ref[blk], dli_ref[blk]]
        an, gn = lax.fori_loop(0, tc, bstep, (tuple(a0), tuple(g0)), unroll=2)
        for blk in range(S5_BLOCKS):
            acarry[blk, :, 0:128] = an[2 * blk]
            acarry[blk, :, 128:256] = an[2 * blk + 1]
            dlr_ref[blk] = gn[2 * blk]
            dli_ref[blk] = gn[2 * blk + 1]
        for blk in range(S5_BLOCKS):
            sl = slice(blk * 256, (blk + 1) * 256)
            ab = adj[blk].astype(BF16)
            _stage(tmp, jnp.dot(ab, rbt_ref[blk], preferred_element_type=F32))
            du[:, sl] = _gather_rows(tmp, tc) + d_ref[:, sl] * dy_ref[:, sl]
            drb_ref[blk] += lax.dot_general(lhsu[blk], ab, (((0,), (0,)), ((), ())), preferred_element_type=F32)
            drc_ref[blk] += lax.dot_general(lhsd[blk], xs[blk].astype(BF16), (((0,), (0,)), ((), ())),
                                            preferred_element_type=F32)
        xh, r = _rms_hat(x_ref[...])
        dg_ref[...] += jnp.sum(du[...] * xh, axis=0, keepdims=True)
        dxh = du[...] * g_ref[...]
        dx_ref[...] = r * (dxh - xh * jnp.mean(dxh * xh, axis=-1, keepdims=True)) + res_ref[...]

    rev = pl.BlockSpec((tc, D_MODEL), lambda i: (nc - 1 - i, 0))
    vec = pl.BlockSpec((1, D_MODEL), lambda i: (0, 0))
    mat = pl.BlockSpec((S5_BLOCKS, 256, 256), lambda i: (0, 0, 0))
    lamspec = pl.BlockSpec((S5_BLOCKS, 8, 128), lambda i: (0, 0, 0))
    big = pltpu.VMEM((S5_BLOCKS, 8 * tc, 256), F32)
    bigb = pltpu.VMEM((S5_BLOCKS, 8 * tc, 256), BF16)
    return pl.pallas_call(
        body, grid=(nc,),
        in_specs=[rev, vec, rev, rev, vec, pl.BlockSpec((1, S5_BLOCKS, 8, 256), lambda i: (nc - 1 - i, 0, 0, 0)),
                  mat, mat, mat, lamspec, lamspec],
        out_specs=[rev, vec, mat, mat, lamspec, lamspec, vec],
        out_shape=[_sds((n_rows, D_MODEL), F32), _sds((1, D_MODEL), F32), _sds((S5_BLOCKS, 256, 256), F32),
                   _sds((S5_BLOCKS, 256, 256), F32), _sds((S5_BLOCKS, 8, 128), F32), _sds((S5_BLOCKS, 8, 128), F32),
                   _sds((1, D_MODEL), F32)],
        scratch_shapes=[pltpu.VMEM((2, 8 * tc, 128), F32), pltpu.VMEM((tc, D_MODEL), F32), bigb, bigb, big, big,
                        pltpu.VMEM((S5_BLOCKS, 8, 256), F32)],
        name="s5_bwd", compiler_params=_params(("arbitrary",)))(
            x, gain, dy2, res, d_skip, cs, rb, rbt, rct, lam_r, lam_i)


def _s5_views(a_re, a_im, log_dt, b_re, b_im):
    return a_re[:, None, :], a_im[:, None, :], log_dt[:, None, None], jnp.swapaxes(b_re, 1, 2), jnp.swapaxes(b_im, 1, 2)


def _s5_factors(a_re, a_im, log_dt):
    lr, li, dt = jnp.minimum(a_re, LAMBDA_RE_MAX), a_im, jnp.exp(log_dt)
    mag, ang = jnp.exp(lr * dt), li * dt
    lbr, lbi = mag * jnp.cos(ang), mag * jnp.sin(ang)
    den = lr * lr + li * li
    fr, fi = ((lbr - 1.0) * lr + lbi * li) / den, (lbi * lr - (lbr - 1.0) * li) / den
    return lr, li, dt, lbr, lbi, fr, fi, den


def s5_prep(a_re, a_im, log_dt, b_re, b_im, c_re, c_im):
    def body(ar_ref, ai_ref, t_ref, br_ref, bi_ref, cr_ref, ci_ref, rb_ref, rbt_ref, rc_ref, rct_ref, lr_ref, li_ref):
        _, _, _, lbr, lbi, fr, fi, _ = _s5_factors(ar_ref[...], ai_ref[...], t_ref[...])
        lr_ref[...] = lbr
        li_ref[...] = lbi
        bre = fr * br_ref[...] - fi * bi_ref[...]
        bim = fr * bi_ref[...] + fi * br_ref[...]
        even = (lax.broadcasted_iota(jnp.int32, (256, S5_STATE), 0) // S5_GROUP) % 2 == 0

        def assemble(re, im):
            re, im = re.reshape(256, S5_STATE), im.reshape(256, S5_STATE)
            return jnp.concatenate([jnp.where(even, re, 0.0), jnp.where(even, 0.0, re), jnp.where(even, im, 0.0),
                                    jnp.where(even, 0.0, im)], axis=1)

        for blk in range(S5_BLOCKS):
            sl = slice(16 * blk, 16 * blk + 16)
            rb = assemble(bre[sl], bim[sl])
            rct = assemble(cr_ref[sl], -ci_ref[sl])
            rb_ref[blk] = rb.astype(BF16)
            rbt_ref[blk] = rb.T.astype(BF16)
            rct_ref[blk] = rct.astype(BF16)
            rc_ref[blk] = rct.T.astype(BF16)

    vm = pl.BlockSpec(memory_space=pltpu.VMEM)
    mat = _sds((S5_BLOCKS, 256, 256), BF16)
    lam = _sds((S5_GROUPS, 1, S5_STATE), F32)
    rb, rbt, rc, rct, lam_r, lam_i = pl.pallas_call(
        body, in_specs=[vm] * 7, out_specs=[vm] * 6, out_shape=[mat, mat, mat, mat, lam, lam], name="s5_prep",
        compiler_params=_params())(*_s5_views(a_re, a_im, log_dt, b_re, b_im), c_re, c_im)
    return rb, rbt, rc, rct, lam_r.reshape(S5_BLOCKS, 8, 128), lam_i.reshape(S5_BLOCKS, 8, 128)


def s5_param_bwd(mats, lams, a_re, a_im, log_dt, b_re, b_im):
    def body(m_ref, glr_ref, gli_ref, ar_ref, ai_ref, t_ref, br_ref, bi_ref,
             dar_ref, dai_ref, dt_ref, dbr_ref, dbi_ref, dcr_ref, dci_ref):
        lr, li, dt, lbr, lbi, fr, fi, den = _s5_factors(ar_ref[...], ai_ref[...], t_ref[...])
        shape = (S5_GROUPS, S5_GROUP, S5_STATE)
        gbr, gbi = m_ref[0:1024, 0:64].reshape(shape), m_ref[0:1024, 64:128].reshape(shape)
        dcr_ref[...] = m_ref[1024:2048, 0:64].reshape(shape)
        dci_ref[...] = -m_ref[1024:2048, 64:128].reshape(shape)
        br, bi = br_ref[...], bi_ref[...]
        dbr_ref[...] = fr * gbr + fi * gbi
        dbi_ref[...] = fr * gbi - fi * gbr
        dfr = jnp.sum(gbr * br + gbi * bi, axis=1, keepdims=True)
        dfi = jnp.sum(gbi * br - gbr * bi, axis=1, keepdims=True)
        nr, ni = (dfr * lr - dfi * li) / den, (dfr * li + dfi * lr) / den
        qr, qi = (fr * lr + fi * li) / den, (fi * lr - fr * li) / den
        lam_r, lam_i = -(dfr * qr + dfi * qi), -(dfi * qr - dfr * qi)
        gr, gi = glr_ref[...] + nr, gli_ref[...] + ni
        zr, zi = gr * lbr + gi * lbi, gi * lbr - gr * lbi
        a = ar_ref[...]
        dar_ref[...] = (lam_r + zr * dt) * jnp.where(a < LAMBDA_RE_MAX, 1.0, jnp.where(a == LAMBDA_RE_MAX, 0.5, 0.0))
        dai_ref[...] = lam_i + zi * dt
        dt_ref[...] = jnp.sum(zr * lr + zi * li, axis=2, keepdims=True) * dt

    vm = pl.BlockSpec(memory_space=pltpu.VMEM)
    state = _sds((S5_GROUPS, 1, S5_STATE), F32)
    wide = _sds((S5_GROUPS, S5_GROUP, S5_STATE), F32)
    glr = lams[0:32].reshape(S5_GROUPS, 1, S5_STATE)
    gli = lams[32:64].reshape(S5_GROUPS, 1, S5_STATE)
    dar, dai, ddt, dbr, dbi, dcr, dci = pl.pallas_call(
        body, in_specs=[vm] * 8, out_specs=[vm] * 7,
        out_shape=[state, state, _sds((S5_GROUPS, 1, 1), F32), wide, wide, wide, wide], name="s5_param_bwd",
        compiler_params=_params())(mats, glr, gli, *_s5_views(a_re, a_im, log_dt, b_re, b_im))
    return (dar.reshape(S5_GROUPS, S5_STATE), dai.reshape(S5_GROUPS, S5_STATE), ddt.reshape(S5_GROUPS),
            jnp.swapaxes(dbr, 1, 2), jnp.swapaxes(dbi, 1, 2), dcr, dci)


def s5_compact(drb, drct, dlr, dli):
    def body(drb_ref, drct_ref, dlr_ref, dli_ref, o_ref, lam_ref):
        even = (lax.broadcasted_iota(jnp.int32, (256, 64), 0) // S5_GROUP) % 2 == 0
        for blk in range(S5_BLOCKS):
            for k, ref in enumerate((drb_ref, drct_ref)):
                m = ref[blk]
                re = jnp.where(even, m[:, 0:64], m[:, 64:128])
                im = jnp.where(even, m[:, 128:192], m[:, 192:256])
                o_ref[pl.ds(k * 1024 + blk * 256, 256), :] = jnp.concatenate([re, im], axis=1)
            lam_ref[pl.ds(blk * 8, 8), :] = dlr_ref[blk]
            lam_ref[pl.ds(32 + blk * 8, 8), :] = dli_ref[blk]

    vm = pl.BlockSpec(memory_space=pltpu.VMEM)
    return pl.pallas_call(body, in_specs=[vm] * 4, out_specs=[vm, vm], out_shape=[_sds((2048, 128), F32), _sds((64, 128), F32)],
                          name="s5_compact", compiler_params=_params())(drb, drct, dlr, dli)


NEG = -1e30


GROUP = N_Q // N_KV


def _attn_masks(n):
    qi = lax.broadcasted_iota(jnp.int32, (GROUP * BLOCK, BLOCK), 0) % BLOCK
    kj = lax.broadcasted_iota(jnp.int32, (GROUP * BLOCK, BLOCK), 1)
    return jnp.logical_and(kj > qi, n > 0), kj <= qi


def _stack_heads(ref, kh):
    return jnp.concatenate([ref[:, (GROUP * kh + g) * HEAD_DIM:(GROUP * kh + g + 1) * HEAD_DIM] for g in range(GROUP)], axis=0)


def _unstack_heads(val):
    return jnp.concatenate([val[g * BLOCK:(g + 1) * BLOCK] for g in range(GROUP)], axis=1)


def _sink_column(sink_ref, kh):
    grp = lax.broadcasted_iota(jnp.int32, (GROUP * BLOCK, 1), 0) // BLOCK
    col = jnp.zeros((GROUP * BLOCK, 1), F32)
    for g in range(GROUP):
        col = jnp.where(grp == g, sink_ref[GROUP * kh + g], col)
    return col, grp


def _attn_exp(q4, kp, kc, sink, mask_p, mask_c):
    scale = 1.0 / math.sqrt(HEAD_DIM)
    nt = (((1,), (1,)), ((), ()))
    sp = jnp.where(mask_p, lax.dot_general(q4, kp, nt, preferred_element_type=F32) * scale, NEG)
    sc = jnp.where(mask_c, lax.dot_general(q4, kc, nt, preferred_element_type=F32) * scale, NEG)
    m = jnp.maximum(jnp.maximum(jnp.max(sp, axis=-1, keepdims=True), jnp.max(sc, axis=-1, keepdims=True)), sink)
    pp = jnp.exp(sp - m)
    pc = jnp.exp(sc - m)
    ps = jnp.exp(sink - m)
    inv = 1.0 / (jnp.sum(pp, axis=-1, keepdims=True) + jnp.sum(pc, axis=-1, keepdims=True) + ps)
    return pp, pc, ps, inv


def attn_fwd(q, kv, sinks):
    n_rows = q.shape[0]
    nb = n_rows // BLOCK

    def body(sink_ref, q_ref, kvp_ref, kvc_ref, o_ref):
        n = pl.program_id(0)
        mask_p, mask_c = _attn_masks(n)
        outs = []
        for kh in range(N_KV):
            ks, vs = slice(kh * HEAD_DIM, (kh + 1) * HEAD_DIM), slice((N_KV + kh) * HEAD_DIM, (N_KV + kh + 1) * HEAD_DIM)
            sink, _ = _sink_column(sink_ref, kh)
            pp, pc, _, inv = _attn_exp(_stack_heads(q_ref, kh), kvp_ref[:, ks], kvc_ref[:, ks], sink, mask_p, mask_c)
            o4 = (jnp.dot(pp.astype(BF16), kvp_ref[:, vs], preferred_element_type=F32)
                  + jnp.dot(pc.astype(BF16), kvc_ref[:, vs], preferred_element_type=F32)) * inv
            outs.append(_unstack_heads(o4))
        o_ref[...] = jnp.concatenate(outs, axis=1).astype(BF16)

    kvw = 2 * N_KV * HEAD_DIM
    return pl.pallas_call(
        body, grid=(nb,),
        in_specs=[pl.BlockSpec(memory_space=pltpu.SMEM), pl.BlockSpec((BLOCK, D_MODEL), lambda n: (n, 0)),
                  pl.BlockSpec((BLOCK, kvw), lambda n: (jnp.maximum(n - 1, 0), 0)), pl.BlockSpec((BLOCK, kvw), lambda n: (n, 0))],
        out_specs=pl.BlockSpec((BLOCK, D_MODEL), lambda n: (n, 0)), out_shape=_sds((n_rows, D_MODEL), BF16),
        name="attn_fwd", compiler_params=_params(("parallel",)))(sinks, q, kv, kv)


def attn_bwd(q, kv, do, sinks):
    n_rows = q.shape[0]
    nb = n_rows // BLOCK
    kvw = 2 * N_KV * HEAD_DIM
    tn = (((0,), (0,)), ((), ()))
    nt = (((1,), (1,)), ((), ()))
    scale = 1.0 / math.sqrt(HEAD_DIM)

    def body(sink_ref, q_ref, kvp_ref, kvc_ref, do_ref, dq_ref, dbq_ref, dprev_ref, dcur_ref, dsink_ref):
        n = pl.program_id(0)
        mask_p, mask_c = _attn_masks(n)
        lane = lax.broadcasted_iota(jnp.int32, (1, D_MODEL), 1)
        dqs, dsink = [], jnp.zeros((1, D_MODEL), F32)
        dkp, dkc, dvp, dvc = [], [], [], []
        for kh in range(N_KV):
            ks, vs = slice(kh * HEAD_DIM, (kh + 1) * HEAD_DIM), slice((N_KV + kh) * HEAD_DIM, (N_KV + kh + 1) * HEAD_DIM)
            q4, do4 = _stack_heads(q_ref, kh), _stack_heads(do_ref, kh)
            kp, kc, vp, vc = kvp_ref[:, ks], kvc_ref[:, ks], kvp_ref[:, vs], kvc_ref[:, vs]
            sink, grp = _sink_column(sink_ref, kh)
            pp, pc, ps, inv = _attn_exp(q4, kp, kc, sink, mask_p, mask_c)
            pp, pc = pp * inv, pc * inv
            dpp = lax.dot_general(do4, vp, nt, preferred_element_type=F32)
            dpc = lax.dot_general(do4, vc, nt, preferred_element_type=F32)
            delta = jnp.sum(pp * dpp, axis=-1, keepdims=True) + jnp.sum(pc * dpc, axis=-1, keepdims=True)
            dsp = (pp * (dpp - delta) * scale).astype(BF16)
            dsc = (pc * (dpc - delta) * scale).astype(BF16)
            dsk = ps * inv * delta
            for g in range(GROUP):
                dsink = dsink + jnp.where(lane == GROUP * kh + g, -jnp.sum(jnp.where(grp == g, dsk, 0.0)), 0.0)
            dqs.append(_unstack_heads(jnp.dot(dsp, kp, preferred_element_type=F32)
                                      + jnp.dot(dsc, kc, preferred_element_type=F32)))
            dkp.append(lax.dot_general(dsp, q4, tn, preferred_element_type=F32))
            dkc.append(lax.dot_general(dsc, q4, tn, preferred_element_type=F32))
            dvp.append(lax.dot_general(pp.astype(BF16), do4, tn, preferred_element_type=F32))
            dvc.append(lax.dot_general(pc.astype(BF16), do4, tn, preferred_element_type=F32))
        dq = jnp.concatenate(dqs, axis=1)
        dq_ref[...] = dq.astype(BF16)
        dprev_ref[0] = jnp.concatenate(dkp + dvp, axis=1)
        dcur_ref[0] = jnp.concatenate(dkc + dvc, axis=1)

        @pl.when(n == 0)
        def _():
            dbq_ref[...] = jnp.zeros_like(dbq_ref)
            dsink_ref[...] = jnp.zeros_like(dsink_ref)

        dbq_ref[...] += jnp.sum(dq, axis=0, keepdims=True)
        dsink_ref[...] += dsink

    blk = pl.BlockSpec((BLOCK, D_MODEL), lambda n: (n, 0))
    part = pl.BlockSpec((1, BLOCK, kvw), lambda n: (n, 0, 0))
    return pl.pallas_call(
        body, grid=(nb,),
        in_specs=[pl.BlockSpec(memory_space=pltpu.SMEM), blk,
                  pl.BlockSpec((BLOCK, kvw), lambda n: (jnp.maximum(n - 1, 0), 0)), pl.BlockSpec((BLOCK, kvw), lambda n: (n, 0)), blk],
        out_specs=[blk, pl.BlockSpec((1, D_MODEL), lambda n: (0, 0)), part, part, pl.BlockSpec((1, D_MODEL), lambda n: (0, 0))],
        out_shape=[_sds((n_rows, D_MODEL), BF16), _sds((1, D_MODEL), F32), _sds((nb, BLOCK, kvw), F32),
                   _sds((nb, BLOCK, kvw), F32), _sds((1, D_MODEL), F32)],
        name="attn_bwd", compiler_params=_params(("arbitrary",)))(sinks, q, kv, kv, do)


def kv_combine(dprev, dcur):
    nb, _, kvw = dprev.shape

    def body(dcur_ref, dprev_ref, dkv_ref, db_ref):
        total = jnp.zeros((1, kvw), F32)
        for m in range(nb):
            dkv = dcur_ref[m] + dprev_ref[m + 1] if m + 1 < nb else dcur_ref[m]
            dkv_ref[m * BLOCK:(m + 1) * BLOCK, :] = dkv.astype(BF16)
            total = total + jnp.sum(dkv, axis=0, keepdims=True)
        db_ref[...] = jnp.concatenate([total, jnp.zeros((1, D_MODEL - kvw), F32)], axis=1)

    vm = pl.BlockSpec(memory_space=pltpu.VMEM)
    return pl.pallas_call(body, in_specs=[vm, vm], out_specs=[vm, vm],
                          out_shape=[_sds((nb * BLOCK, kvw), BF16), _sds((1, D_MODEL), F32)], name="kv_combine",
                          compiler_params=_params())(dcur, dprev)


def glu_bwd(dout, val, gate, tm=256):
    n_rows, d = dout.shape

    def body(do_ref, v_ref, g_ref, dz_ref, db_ref):
        i = pl.program_id(0)
        sg = jax.nn.sigmoid(g_ref[...])
        dval = do_ref[...] * sg
        dgate = do_ref[...] * v_ref[...] * sg * (1.0 - sg)
        dz_ref[...] = jnp.concatenate([dval, dgate], axis=1).astype(BF16)

        @pl.when(i == 0)
        def _():
            db_ref[...] = jnp.zeros_like(db_ref)

        db_ref[0:1, :] += jnp.sum(dval, axis=0, keepdims=True)
        db_ref[1:2, :] += jnp.sum(dgate, axis=0, keepdims=True)

    row = pl.BlockSpec((tm, d), lambda i: (i, 0))
    return pl.pallas_call(
        body, grid=(n_rows // tm,), in_specs=[row, row, row],
        out_specs=[pl.BlockSpec((tm, 2 * d), lambda i: (i, 0)), pl.BlockSpec((2, d), lambda i: (0, 0))],
        out_shape=[_sds((n_rows, 2 * d), BF16), _sds((2, d), F32)],
        name="glu_bwd", compiler_params=_params(("arbitrary",)))(dout, val, gate)


def _adam_update(w, g, m, v):
    nm = ADAM_B1 * m + (1.0 - ADAM_B1) * g
    nv = ADAM_B2 * v + (1.0 - ADAM_B2) * (g * g)
    m_hat = nm / (1.0 - ADAM_B1 ** ADAM_STEP)
    v_hat = nv / (1.0 - ADAM_B2 ** ADAM_STEP)
    return -ADAM_LR * (m_hat / (jnp.sqrt(v_hat) + ADAM_EPS) + ADAM_WD * w), nm, nv


def adamw(name, ws, gs, ms, vs, steps=8):
    n = len(ws)

    def body(*refs):
        for k in range(n):
            w_ref, g_ref, m_ref, v_ref = (refs[j * n + k] for j in range(4))
            go_ref, d_ref, nm_ref, nv_ref = (refs[(4 + j) * n + k] for j in range(4))
            gv = g_ref[...]
            go_ref[...] = gv
            d_ref[...], nm_ref[...], nv_ref[...] = _adam_update(w_ref[...], gv, m_ref[...], v_ref[...])

    specs = [pl.BlockSpec((w.shape[0] // steps, w.shape[1]), lambda i: (i, 0)) for w in ws]
    shapes = [_sds(w.shape, F32) for w in ws]
    out = pl.pallas_call(
        body, grid=(steps,), in_specs=specs * 4, out_specs=specs * 4, out_shape=shapes * 4, name=name,
        compiler_params=_params(("parallel",)))(*ws, *gs, *ms, *vs)
    return [list(out[j * n:(j + 1) * n]) for j in range(4)]


def adamw_native(name, ws, gs, ms, vs):
    n = len(ws)

    def body(*refs):
        w_refs, g_refs, m_refs, v_refs = refs[:n], refs[n:2 * n], refs[2 * n:3 * n], refs[3 * n:4 * n]
        d_refs, nm_refs, nv_refs = refs[4 * n:5 * n], refs[5 * n:6 * n], refs[6 * n:7 * n]
        for k in range(n):
            dl, nm, nv = _adam_update(w_refs[k][...], g_refs[k][...], m_refs[k][...], v_refs[k][...])
            d_refs[k][...] = dl
            nm_refs[k][...] = nm
            nv_refs[k][...] = nv

    vm = pl.BlockSpec(memory_space=pltpu.VMEM)
    shapes = [_sds(w.shape, F32) for w in ws]
    out = pl.pallas_call(body, in_specs=[vm] * (4 * n), out_specs=[vm] * (3 * n), out_shape=shapes * 3, name=name,
                         compiler_params=_params())(*ws, *gs, *ms, *vs)
    return list(out[:n]), list(out[n:2 * n]), list(out[2 * n:])


VEC_ROWS = {"norm_mix": 0, "norm_mlp": 2, "norm_kv": 4, "norm_final": 5, "s5_d": 6, "b_q": 7, "b_o": 8, "s5_b_glu": 9,
            "b_kv": 11, "sinks": 12, "loss": 13}


def split_vectors(where, vecs, d_shard, glu_shard):
    kvw = 2 * N_KV * HEAD_DIM
    shapes = {"norm_mix": (2, D_MODEL), "norm_mlp": (2, D_MODEL), "norm_kv": (1, D_MODEL), "norm_final": (1, D_MODEL),
              "s5_d": (1, d_shard), "b_q": (1, D_MODEL), "b_o": (1, D_MODEL), "s5_b_glu": (1, glu_shard), "b_kv": (1, kvw),
              "sinks": (1, N_Q), "loss": (1, 128)}
    names = list(shapes)

    def body(where_ref, v_ref, *o_refs):
        chip = where_ref[1]
        for name, o_ref in zip(names, o_refs):
            r0, (r, n) = VEC_ROWS[name], shapes[name]
            if name == "s5_d":
                g = jnp.zeros((1, n), F32)
                for j in range(4):
                    g = jnp.where(chip == j, v_ref[r0:r0 + 1, j * n:(j + 1) * n], g)
            elif name == "s5_b_glu":
                g = jnp.zeros((1, n), F32)
                for j in range(4):
                    row, col = r0 + (j * n) // D_MODEL, (j * n) % D_MODEL
                    g = jnp.where(chip == j, v_ref[row:row + 1, col:col + n], g)
            else:
                g = v_ref[r0:r0 + r, 0:n]
            o_ref[...] = g

    vm = pl.BlockSpec(memory_space=pltpu.VMEM)
    out = pl.pallas_call(body, in_specs=[pl.BlockSpec(memory_space=pltpu.SMEM), vm], out_specs=[vm] * len(names),
                         out_shape=[_sds(shapes[n], F32) for n in names], name="split_vectors",
                         compiler_params=_params())(where, vecs)
    return dict(zip(names, out))


def _position():
    x, y, c = lax.axis_index("x"), lax.axis_index("y"), lax.axis_index("c")
    others = [(1 - x, y), (x, 1 - y), (1 - x, 1 - y)]
    return x, y, c, others


def _window(ref, kind, chip, half, shard_shape):
    if kind == "slab":
        return ref.at[chip]
    r, n = shard_shape
    if kind == "col":
        return ref.at[pl.ds(pl.multiple_of(half * (r // 2), 16), r // 2), pl.ds(pl.multiple_of(chip * n, 128), n)]
    return ref.at[pl.ds(pl.multiple_of(chip * r, 16), r), pl.ds(pl.multiple_of(half * (n // 2), 128), n // 2)]


def _half(ref, kind, half, shape):
    r, n = shape
    if kind == "col":
        return ref.at[pl.ds(pl.multiple_of(half * (r // 2), 16), r // 2), :]
    return ref.at[:, pl.ds(pl.multiple_of(half * (n // 2), 128), n // 2)]


def swap_start(name, grads, kinds):
    nt = len(grads)
    shapes = [tuple(g.shape) for g in grads]
    lands = [lax.empty(sh, BF16) for sh in shapes]

    def body(*refs):
        in_refs, land_refs = refs[:nt], refs[nt:2 * nt]
        send_sems, recv_sems, token = refs[2 * nt], refs[2 * nt + 1], refs[-1]
        x, y, c, _ = _position()
        for t in range(nt):
            pltpu.make_async_remote_copy(
                src_ref=_half(in_refs[t], kinds[t], 1 - c, shapes[t]), dst_ref=_half(land_refs[t], kinds[t], 1 - c, shapes[t]),
                send_sem=send_sems.at[t], recv_sem=recv_sems.at[t], device_id=(x, y, 1 - c), device_id_type=MESH).start()
        token[...] = jnp.zeros_like(token)

    sems = pltpu.SemaphoreType.DMA((nt,))
    both = list(grads) + lands
    out = pl.pallas_call(
        body, name=name, in_specs=[HBM_SPEC] * (2 * nt),
        out_specs=(SEM_SPEC, SEM_SPEC, *[HBM_SPEC] * (2 * nt), pl.BlockSpec(memory_space=pltpu.VMEM)),
        out_shape=(sems, sems, *[pltpu.HBM(a.shape, a.dtype) for a in both], _sds((8, 128), F32)),
        input_output_aliases={t: 2 + t for t in range(2 * nt)}, compiler_params=_split_params(),
    )(*[_in_hbm(a) for a in both])
    return out[0], out[1], list(out[2:2 + nt]), list(out[2 + nt:2 + 2 * nt]), out[-1]


def swap_wait(name, send_sems, recv_sems, grads, lands, kinds, after):
    nt = len(grads)
    shapes = [tuple(g.shape) for g in grads]

    def body(*refs):
        in_refs, land_refs = refs[:nt], refs[nt:2 * nt]
        send_ref, recv_ref = refs[2 * nt], refs[2 * nt + 1]
        x, y, c, _ = _position()
        for t in range(nt):
            cp = pltpu.make_async_remote_copy(
                src_ref=_half(in_refs[t], kinds[t], 1 - c, shapes[t]), dst_ref=_half(land_refs[t], kinds[t], c, shapes[t]),
                send_sem=send_ref.at[t], recv_sem=recv_ref.at[t], device_id=(x, y, 1 - c), device_id_type=MESH)
            cp.wait_send()
            cp.wait_recv()

    both = list(grads) + list(lands)
    out = pl.pallas_call(
        body, name=name, in_specs=[HBM_SPEC] * (2 * nt) + [SEM_SPEC, SEM_SPEC, HBM_SPEC], out_specs=[HBM_SPEC] * (2 * nt),
        out_shape=[pltpu.HBM(a.shape, a.dtype) for a in both], input_output_aliases={t: t for t in range(2 * nt)},
        compiler_params=_split_params())(*both, send_sems, recv_sems, _in_hbm(after))
    return list(out[:nt]), list(out[nt:])


def _half_spec(kind, shape, tiles):
    r, n = shape
    if kind == "col":
        tn = n // tiles
        return pl.BlockSpec((r // 2, tn), lambda i, s: (s[0], i))
    tm = r // tiles
    return pl.BlockSpec((tm, n // 2), lambda i, s: (i, s[0]))


def add_halves(name, mine, landed, kinds, where, tiles=4):
    nt = len(mine)
    shapes = [tuple(a.shape) for a in mine]

    def compact(t):
        r, n = shapes[t]
        if kinds[t] == "col":
            return (r // 2, n), pl.BlockSpec((r // 2, n // tiles), lambda i, s: (0, i))
        return (r, n // 2), pl.BlockSpec((r // tiles, n // 2), lambda i, s: (i, 0))

    def body(s_ref, *refs):
        for a_ref, b_ref, o_ref in zip(refs[:nt], refs[nt:2 * nt], refs[2 * nt:]):
            o_ref[...] = (a_ref[...].astype(F32) + b_ref[...].astype(F32)).astype(BF16)

    specs = [_half_spec(kinds[t], shapes[t], tiles) for t in range(nt)]
    return pl.pallas_call(
        body, grid_spec=pltpu.PrefetchScalarGridSpec(num_scalar_prefetch=1, grid=(tiles,), in_specs=specs + specs,
                                                     out_specs=[compact(t)[1] for t in range(nt)]),
        out_shape=[_sds(compact(t)[0], BF16) for t in range(nt)], name=name,
        compiler_params=_params(("parallel",)))(where, *mine, *landed)


def sum_shards(name, parts, landed, kinds, shard_shapes, where, layers, n_layers, intos, tiles=2):
    nt = len(parts)
    in_specs, out_specs = [], []
    for t in range(nt):
        (r, n), layer = shard_shapes[t], layers[t]
        if kinds[t] == "col":
            tm, width = r // 2 // tiles, n
            own = pl.BlockSpec((tm, n), lambda i, s: (i, s[1]))
            out = pl.BlockSpec((None, tm, n), lambda i, s, layer=layer: (layer, s[0] * tiles + i, 0))
        else:
            tm, width = r // tiles, n // 2
            own = pl.BlockSpec((tm, n // 2), lambda i, s: (s[1] * tiles + i, 0))
            out = pl.BlockSpec((None, tm, n // 2), lambda i, s, layer=layer: (layer, i, s[0]))
        in_specs += [own, pl.BlockSpec((3, tm, width), lambda i, s: (0, i, 0))]
        out_specs.append(out)
    args, aliases = [where] + [a for pair in zip(parts, landed) for a in pair], {}
    for t in range(nt):
        if intos[t] is not None:
            aliases[len(args)] = t
            in_specs.append(pl.BlockSpec(memory_space=pl.ANY))
            args.append(intos[t])

    def body(s_ref, *refs):
        for t in range(nt):
            a_ref, l_ref, o_ref = refs[2 * t], refs[2 * t + 1], refs[len(in_specs) + t]
            o_ref[...] = ((a_ref[...].astype(F32) + l_ref[0].astype(F32)) + l_ref[1].astype(F32)) + l_ref[2].astype(F32)

    return pl.pallas_call(
        body, grid_spec=pltpu.PrefetchScalarGridSpec(num_scalar_prefetch=1, grid=(tiles,), in_specs=in_specs,
                                                     out_specs=out_specs),
        out_shape=[_sds((n_layers[t],) + tuple(shard_shapes[t]), F32) for t in range(nt)], input_output_aliases=aliases,
        name=name, compiler_params=_params(("parallel",)))(*args)


def share_halves(arrays, entries):
    na, nt = len(arrays), len(entries)

    def body(*refs):
        out_refs = refs[na:2 * na]
        send_sems, recv_sems = refs[2 * na:]
        x, y, c, _ = _position()
        cps = []
        for t, (a, layer, kind) in enumerate(entries):
            shape = tuple(arrays[a].shape[1:])
            mine = _half(out_refs[a].at[layer], kind, c, shape)
            cp = pltpu.make_async_remote_copy(
                src_ref=mine, dst_ref=mine, send_sem=send_sems.at[t], recv_sem=recv_sems.at[t],
                device_id=(x, y, 1 - c), device_id_type=MESH)
            cp.start()
            cps.append(cp)
        for t, (a, layer, kind) in enumerate(entries):
            shape = tuple(arrays[a].shape[1:])
            other = _half(out_refs[a].at[layer], kind, 1 - c, shape)
            pltpu.make_async_remote_copy(
                src_ref=other, dst_ref=other, send_sem=send_sems.at[t], recv_sem=recv_sems.at[t],
                device_id=(x, y, 1 - c), device_id_type=MESH).wait_recv()
        for cp in cps:
            cp.wait_send()

    hbm = pl.BlockSpec(memory_space=pl.ANY)
    return pl.pallas_call(
        body, in_specs=[hbm] * na, out_specs=[hbm] * na, out_shape=[_sds(a.shape, F32) for a in arrays],
        input_output_aliases={i: i for i in range(na)},
        scratch_shapes=[pltpu.SemaphoreType.DMA((nt,)), pltpu.SemaphoreType.DMA((nt,))],
        name="share_halves", compiler_params=_params())(*arrays)


HBM_SPEC = pl.BlockSpec(memory_space=pltpu.HBM)
SEM_SPEC = pl.BlockSpec(memory_space=pltpu.SEMAPHORE)
ANY_SPEC = pl.BlockSpec(memory_space=pl.ANY)


def _split_params():
    return pltpu.CompilerParams(has_side_effects=pltpu.SideEffectType.DATAFLOW_SIDE_EFFECTING,
                                vmem_limit_bytes=VMEM_LIMIT_BYTES)


def _in_hbm(a):
    return pltpu.with_memory_space_constraint(a, pltpu.HBM)


def cast_place(arrays, entries, where, tiles=2):
    in_specs, out_specs, fulls = [], [], []
    for a, layer, kind in entries:
        _, r, n = arrays[a].shape
        tm = r // tiles
        in_specs.append(pl.BlockSpec((None, tm, n), lambda i, s, layer=layer: (layer, i, 0)))
        if kind == "col":
            fulls.append((r, 4 * n))
            out_specs.append(pl.BlockSpec((tm, n), lambda i, s: (i, s[1])))
        else:
            fulls.append((4 * r, n))
            out_specs.append(pl.BlockSpec((tm, n), lambda i, s: (s[1] * tiles + i, 0)))
    nt = len(entries)

    def body(s_ref, *refs):
        for w_ref, o_ref in zip(refs[:nt], refs[nt:]):
            o_ref[...] = w_ref[...].astype(BF16)

    return pl.pallas_call(
        body, grid_spec=pltpu.PrefetchScalarGridSpec(num_scalar_prefetch=1, grid=(tiles,), in_specs=in_specs,
                                                     out_specs=out_specs),
        out_shape=[_sds(f, BF16) for f in fulls], name="cast_place",
        compiler_params=_params(("parallel",)))(where, *[arrays[a] for a, _, _ in entries])


def gather_start(fulls, kinds, shard_shapes):
    nt = len(fulls)

    def body(*refs):
        full_refs = refs[:nt]
        send_sems, recv_sems, token = refs[nt], refs[nt + 1], refs[-1]
        x, y, c, others = _position()
        for t in range(nt):
            mine = _window(full_refs[t], kinds[t], 2 * x + y, c, shard_shapes[t])
            for j, (ox, oy) in enumerate(others):
                pltpu.make_async_remote_copy(
                    src_ref=mine, dst_ref=mine, send_sem=send_sems.at[3 * t + j], recv_sem=recv_sems.at[3 * t + j],
                    device_id=(ox, oy, c), device_id_type=MESH).start()
        token[...] = jnp.zeros_like(token)

    sems = pltpu.SemaphoreType.DMA((3 * nt,))
    out = pl.pallas_call(
        body, name="gather_start", in_specs=[HBM_SPEC] * nt,
        out_specs=(SEM_SPEC, SEM_SPEC, *[HBM_SPEC] * nt, pl.BlockSpec(memory_space=pltpu.VMEM)),
        out_shape=(sems, sems, *[pltpu.HBM(f.shape, f.dtype) for f in fulls], _sds((8, 128), F32)),
        input_output_aliases={t: 2 + t for t in range(nt)}, compiler_params=_split_params(),
    )(*[_in_hbm(f) for f in fulls])
    return out[0], out[1], list(out[2:2 + nt]), out[-1]


def gather_wait(name, send_sems, recv_sems, fulls, kinds, shard_shapes, after, first):
    nt = len(fulls)
    extra = [] if after is None else [_in_hbm(after)]

    def body(*refs):
        full_refs, send_ref, recv_ref = refs[:nt], refs[nt], refs[nt + 1]
        x, y, c, others = _position()
        for t in range(nt):
            mine = _window(full_refs[t], kinds[t], 2 * x + y, c, shard_shapes[t])
            for j, (ox, oy) in enumerate(others):
                cp = pltpu.make_async_remote_copy(
                    src_ref=mine, dst_ref=_window(full_refs[t], kinds[t], 2 * ox + oy, c, shard_shapes[t]),
                    send_sem=send_ref.at[3 * (first + t) + j], recv_sem=recv_ref.at[3 * (first + t) + j],
                    device_id=(ox, oy, c), device_id_type=MESH)
                cp.wait_send()
                cp.wait_recv()

    out = pl.pallas_call(
        body, name=name, in_specs=[HBM_SPEC] * nt + [SEM_SPEC, SEM_SPEC] + [HBM_SPEC] * len(extra),
        out_specs=[HBM_SPEC] * nt, out_shape=[pltpu.HBM(f.shape, f.dtype) for f in fulls],
        input_output_aliases={t: t for t in range(nt)}, compiler_params=_split_params())(*fulls, send_sems, recv_sems, *extra)
    return list(out)


def forward_halves(name, fulls, kinds, shard_shapes):
    nt = len(fulls)

    def body(*refs):
        out_refs = refs[nt:2 * nt]
        send_sems, recv_sems = refs[2 * nt:]
        x, y, c, others = _position()
        cps = []
        for t in range(nt):
            for j, (ox, oy) in enumerate(others):
                landed = _window(out_refs[t], kinds[t], 2 * ox + oy, c, shard_shapes[t])
                cp = pltpu.make_async_remote_copy(
                    src_ref=landed, dst_ref=landed, send_sem=send_sems.at[3 * t + j], recv_sem=recv_sems.at[3 * t + j],
                    device_id=(x, y, 1 - c), device_id_type=MESH)
                cp.start()
                cps.append(cp)
        for t in range(nt):
            for j, (ox, oy) in enumerate(others):
                got = _window(out_refs[t], kinds[t], 2 * ox + oy, 1 - c, shard_shapes[t])
                pltpu.make_async_remote_copy(
                    src_ref=got, dst_ref=got, send_sem=send_sems.at[3 * t + j], recv_sem=recv_sems.at[3 * t + j],
                    device_id=(x, y, 1 - c), device_id_type=MESH).wait_recv()
        for cp in cps:
            cp.wait_send()

    out = pl.pallas_call(
        body, in_specs=[ANY_SPEC] * nt, out_specs=[ANY_SPEC] * nt, out_shape=[_sds(f.shape, f.dtype) for f in fulls],
        input_output_aliases={t: t for t in range(nt)},
        scratch_shapes=[pltpu.SemaphoreType.DMA((3 * nt,)), pltpu.SemaphoreType.DMA((3 * nt,))],
        name=name, compiler_params=_params())(*fulls)
    return list(out)


def _piece(ref, kind, chip, shard_shape):
    r, n = shard_shape
    if kind == "col":
        return ref.at[:, pl.ds(pl.multiple_of(chip * n, 128), n)]
    return ref.at[pl.ds(pl.multiple_of(chip * r, 16), r), :]


def _piece_shape(kind, shard_shape):
    r, n = shard_shape
    return (r // 2, n) if kind == "col" else (r, n // 2)


def exchange_start(name, parts, kinds, shard_shapes):
    nt = len(parts)
    lands = [lax.empty((3,) + _piece_shape(kinds[t], shard_shapes[t]), BF16) for t in range(nt)]

    def body(*refs):
        part_refs, land_refs = refs[:nt], refs[nt:2 * nt]
        send_sems, recv_sems, token = refs[2 * nt], refs[2 * nt + 1], refs[-1]
        x, y, c, others = _position()
        for t in range(nt):
            for j, (ox, oy) in enumerate(others):
                pltpu.make_async_remote_copy(
                    src_ref=_piece(part_refs[t], kinds[t], 2 * ox + oy, shard_shapes[t]), dst_ref=land_refs[t].at[j],
                    send_sem=send_sems.at[3 * t + j], recv_sem=recv_sems.at[3 * t + j],
                    device_id=(ox, oy, c), device_id_type=MESH).start()
        token[...] = jnp.zeros_like(token)

    sems = pltpu.SemaphoreType.DMA((3 * nt,))
    both = list(parts) + lands
    out = pl.pallas_call(
        body, name=name, in_specs=[HBM_SPEC] * (2 * nt),
        out_specs=(SEM_SPEC, SEM_SPEC, *[HBM_SPEC] * (2 * nt), pl.BlockSpec(memory_space=pltpu.VMEM)),
        out_shape=(sems, sems, *[pltpu.HBM(a.shape, a.dtype) for a in both], _sds((8, 128), F32)),
        input_output_aliases={t: 2 + t for t in range(2 * nt)}, compiler_params=_split_params(),
    )(*[_in_hbm(a) for a in both])
    return out[0], out[1], list(out[2:2 + nt]), list(out[2 + nt:2 + 2 * nt]), out[-1]


def exchange_wait(name, send_sems, recv_sems, parts, lands, kinds, shard_shapes, after):
    nt = len(parts)

    def body(*refs):
        part_refs, land_refs = refs[:nt], refs[nt:2 * nt]
        send_ref, recv_ref = refs[2 * nt], refs[2 * nt + 1]
        x, y, c, others = _position()
        for t in range(nt):
            for j, (ox, oy) in enumerate(others):
                cp = pltpu.make_async_remote_copy(
                    src_ref=_piece(part_refs[t], kinds[t], 2 * ox + oy, shard_shapes[t]), dst_ref=land_refs[t].at[j],
                    send_sem=send_ref.at[3 * t + j], recv_sem=recv_ref.at[3 * t + j],
                    device_id=(ox, oy, c), device_id_type=MESH)
                cp.wait_send()
                cp.wait_recv()

    both = list(parts) + list(lands)
    out = pl.pallas_call(
        body, name=name, in_specs=[HBM_SPEC] * (2 * nt) + [SEM_SPEC, SEM_SPEC, HBM_SPEC], out_specs=[HBM_SPEC] * (2 * nt),
        out_shape=[pltpu.HBM(a.shape, a.dtype) for a in both], input_output_aliases={t: t for t in range(2 * nt)},
        compiler_params=_split_params())(*both, send_sems, recv_sems, _in_hbm(after))
    return list(out[:nt]), list(out[nt:])


def all_reduce_small(name, bufs, wire):
    n = len(bufs)
    halves = [b.shape[0] // 2 for b in bufs]

    def body(*refs):
        in_refs, out_refs, lands, txs = refs[:n], refs[n:2 * n], refs[2 * n:3 * n], refs[3 * n:4 * n]
        send_sems, recv_sems = refs[4 * n:]
        x, y, c, _ = _position()
        mine = [pl.ds(pl.multiple_of(c * h, 8), h) for h in halves]
        other = [pl.ds(pl.multiple_of((1 - c) * h, 8), h) for h in halves]
        for s, peer in enumerate([(x, y, 1 - c), (1 - x, y, c), (x, 1 - y, c)]):
            cps = []
            for k in range(n):
                txs[k][...] = (in_refs[k][other[k], :] if s == 0 else out_refs[k][mine[k], :]).astype(wire[k])
                cp = pltpu.make_async_remote_copy(
                    src_ref=txs[k], dst_ref=lands[k].at[s], send_sem=send_sems.at[4 * k + s], recv_sem=recv_sems.at[4 * k + s],
                    device_id=peer, device_id_type=MESH)
                cp.start()
                cps.append(cp)
            for k, cp in enumerate(cps):
                cp.wait()
                own = in_refs[k][mine[k], :] if s == 0 else out_refs[k][mine[k], :]
                out_refs[k][mine[k], :] = own.astype(wire[k]).astype(F32) + lands[k][s].astype(F32)
        cps = []
        for k in range(n):
            cp = pltpu.make_async_remote_copy(
                src_ref=out_refs[k].at[mine[k]], dst_ref=out_refs[k].at[mine[k]], send_sem=send_sems.at[4 * k + 3],
                recv_sem=recv_sems.at[4 * k + 3], device_id=(x, y, 1 - c), device_id_type=MESH)
            cp.start()
            cps.append(cp)
        for cp in cps:
            cp.wait()

    vm = pl.BlockSpec(memory_space=pltpu.VMEM)
    out = pl.pallas_call(
        body, in_specs=[vm] * n, out_specs=[vm] * n, out_shape=[_sds(b.shape, F32) for b in bufs],
        scratch_shapes=[pltpu.VMEM((3, h, b.shape[1]), w) for h, b, w in zip(halves, bufs, wire)]
        + [pltpu.VMEM((h, b.shape[1]), w) for h, b, w in zip(halves, bufs, wire)]
        + [pltpu.SemaphoreType.DMA((4 * n,)), pltpu.SemaphoreType.DMA((4 * n,))],
        name=name, compiler_params=_params())(*bufs)
    return list(out)


def _local_step(x, target, small, need, emit_swap, emit_exchange):
    d = D_MODEL
    full = {}

    def after_token(vec, token):
        return vec if token is None else vec + token[0:1, 0:1]

    def token_rows(token, width):
        return [] if token is None else [after_token(jnp.zeros((1, width), F32), token)]

    def plus(acc, rows):
        return acc + rows[0] if rows else acc

    rb16, rbt16, rc16, rct16, lr_t, li_t = small["s5_operands"]
    ge, y2, cs = s5_fwd(x, small["norm_mix0"], small["s5_d"], rb16, rc16, lr_t, li_t)
    full.update(need("glu", ge))

    def norm_rows(h, gains):
        xh, _ = _rms_hat(h)
        return [xh * g for g in gains]

    def glu_epilogue(accs, e, r):
        v, gt = accs[0] + r[0], accs[1] + r[1]
        h = e[0] + v * jax.nn.sigmoid(gt)
        return [h, v, gt] + norm_rows(h, r[2:])

    h1, val, gate, n1 = mm_nn(
        "glu", ge, full["w_glu"], [0, d], d, glu_epilogue, [F32, F32, F32, BF16], extras=[x],
        rowvecs=[(small["s5_b_glu"], 0), (small["s5_b_glu"], d), (small["norm_mlp0"], 0)], tm=512, tn=d)

    def mlp_fwd(tag, h, n, w_in, get_w_out, next_gains, head=None):
        def in_epilogue(accs, e, rv):
            pos = jnp.maximum(accs[0], 0.0)
            return [pos * pos, 2.0 * pos]

        r, slope = mm_nn("mlp_in" + tag, n, w_in, [0], w_in.shape[1], in_epilogue, [BF16, BF16], tm=2048)
        w_out = get_w_out(r)

        def epilogue(accs, e, rv):
            h_out = e[0] + accs[0]
            return [h_out] + norm_rows(h_out, rv)

        if head is not None:
            return head(r, w_out, h), (n, r, slope)
        outs = mm_nn("mlp_out" + tag, r, w_out, [0], d, epilogue, [F32] + [BF16] * len(next_gains), extras=[h],
                     rowvecs=[(g, 0) for g in next_gains], tm=512, tn=d)
        return outs[0], outs[1:], (n, r, slope)

    full.update(need("mlp_in0", h1))

    def w_out0(after):
        full.update(need("mlp_out0", after))
        return full["w_out0"]

    h2, (nkv, n2), mlp0 = mlp_fwd("0", h1, n1, full["w_in0"], w_out0, [small["norm_kv"], small["norm_mix1"]])

    full.update(need("attn", h2))
    kvw = 2 * N_KV * HEAD_DIM
    (kv,) = mm_nn("kv_proj", nkv, full["w_kv"], [0], kvw, lambda accs, e, r: [accs[0] + r[0]], [BF16],
                  rowvecs=[(small["b_kv"], 0)], tm=2048)
    (q,) = mm_nn("q_proj", n2, full["w_q"], [0], d, lambda accs, e, r: [accs[0] + r[0]], [BF16],
                 rowvecs=[(small["b_q"], 0)], tm=2048)
    sinks = small["sinks"].reshape(N_Q)
    o = attn_fwd(q, kv, sinks)
    def o_epilogue(accs, e, r):
        h_out = e[0] + accs[0] + r[0]
        return [h_out] + norm_rows(h_out, r[1:])

    h3, n3 = mm_nn("o_proj", o, full["w_o"], [0], d, o_epilogue, [F32, BF16], extras=[h2],
                   rowvecs=[(small["b_o"], 0), (small["norm_mlp1"], 0)], tm=512, tn=d)
    full.update(need("mlp_in1", h3))

    def w_out1(after):
        full.update(need("mlp_out1", after))
        return full["w_out1"]

    def loss_head(r, w_out, h):
        def epilogue(accs, e, rv):
            xh, rr = _rms_hat(e[0] + accs[0])
            err = xh * rv[0] - e[1]
            dy = err * (1.0 / d)
            dxh = dy * rv[0]
            dx = rr * (dxh - xh * jnp.mean(dxh * xh, axis=-1, keepdims=True))
            loss = jnp.full((1, d), 0.5 * jnp.sum(jnp.mean(err * err, axis=-1, keepdims=True)), F32)
            return [dx, dx, loss, jnp.sum(dy * xh, axis=0, keepdims=True)]

        return mm_nn("mlp_out1", r, w_out, [0], d, epilogue, [F32, BF16], extras=[h, target],
                     rowvecs=[(small["norm_final"], 0)], n_sums=2, tm=512, tn=d)

    (dh, dhb, loss_tile, dg_final), mlp1 = mlp_fwd("1", h3, n3, full["w_in1"], w_out1, [], head=loss_head)

    grads_small, grads_full = {"norm_final": dg_final}, {}
    ident = lambda acc, e, r: [plus(acc, r)]
    layer1 = ["w_out1", "w_in1", "w_o", "w_q", "w_kv"]
    layer0 = ["w_out0", "w_in0", "w_glu"]

    def norm_bwd_rows(x_rows, res, dys, gains):
        xh, r = _rms_hat(x_rows)
        dxh = sum(dy * g for dy, g in zip(dys, gains))
        dx = r * (dxh - xh * jnp.mean(dxh * xh, axis=-1, keepdims=True)) + res
        return dx, [jnp.sum(dy * xh, axis=0, keepdims=True) for dy in dys]

    def mlp_bwd(tag, dh, dhb, h_in, gain, w_in, w_out, saved, token=None):
        n, r, slope = saved
        grads_full["w_out" + tag] = mm_tn("dw_out" + tag, r, dhb, tn=1024)
        (da,) = mm_nt("mlp_da" + tag, dhb, w_out, lambda acc, e, rv: [plus(acc * e[0].astype(F32), rv)], [BF16],
                      extras=[slope], rowvecs=token_rows(token, w_out.shape[0]), tm=2048)
        grads_full["w_in" + tag] = mm_tn("dw_in" + tag, n, da, tn=1024)

        def epilogue(acc, e, rv):
            dx, dgs = norm_bwd_rows(e[0], e[1], [acc], rv)
            return [dx, dx, jnp.sum(dx, axis=0, keepdims=True)] + dgs

        dx, dxb, colsum, dg = mm_nt("mlp_dn" + tag, da, w_in, epilogue, [F32, BF16], extras=[h_in, dh], rowvecs=[gain],
                                    n_sums=2, tm=512, tk=d)
        grads_small["norm_mlp" + tag] = dg
        return dx, dxb, colsum

    dh3, dh3b, colsum3 = mlp_bwd("1", dh, dhb, h3, small["norm_mlp1"], full["w_in1"], full["w_out1"], mlp1)
    grads_small["b_o"] = colsum3
    grads_full["w_o"] = mm_tn("dw_o", o, dh3b, tn=1024)
    (do,) = mm_nt("attn_do", dh3b, full["w_o"], ident, [BF16], tm=2048)
    dq, dbq, dprev, dcur, dsink = attn_bwd(q, kv, do, sinks)
    dkv, dbkv = kv_combine(dprev, dcur)
    grads_small["b_q"], grads_small["b_kv"], grads_small["sinks"] = dbq, dbkv, dsink
    grads_full["w_q"] = mm_tn("dw_q", n2, dq, tn=1024)
    grads_full["w_kv"] = mm_tn("dw_kv", nkv, dkv, tk=1024)
    token = emit_swap("layer1", {n: grads_full[n] for n in layer1})
    (dnkv,) = mm_nt("kv_dn", dkv, full["w_kv"], ident, [F32], rowvecs=token_rows(token, d), tm=2048, tk=1024)

    def attn_dn_epilogue(acc, e, rv):
        dx, dgs = norm_bwd_rows(e[0], e[1], [acc, e[2]], rv)
        return [dx, dx] + dgs

    dh2, dh2b, dg_mix1, dg_kv = mm_nt("attn_dn", dq, full["w_q"], attn_dn_epilogue, [F32, BF16], extras=[h2, dh3, dnkv],
                                      rowvecs=[small["norm_mix1"], small["norm_kv"]], n_sums=2, tm=512, tk=d)
    grads_small["norm_mix1"], grads_small["norm_kv"] = dg_mix1, dg_kv
    token = emit_exchange("layer1", dh2b)
    dh1, _, _ = mlp_bwd("0", dh2, dh2b, h1, small["norm_mlp0"], full["w_in0"], full["w_out0"], mlp0, token)

    dz, db_glu = glu_bwd(dh1, val, gate)
    grads_small["s5_b_glu"] = db_glu
    grads_full["w_glu"] = mm_tn("dw_glu", ge, dz, tn=1024)
    token = emit_swap("layer0", {n: grads_full[n] for n in layer0})
    (dy2,) = mm_nt("glu_dy", dz, full["w_glu"], lambda acc, e, rv: [plus(acc, rv) * _gelu_grad(e[0])], [F32], extras=[y2],
                   rowvecs=token_rows(token, d), tm=1024, tk=1024)
    token = emit_exchange("layer0", dy2)
    grad_x, dd, drb, drc, dlr, dli, dg_mix0 = s5_bwd(x, small["norm_mix0"], dy2, dh1, after_token(small["s5_d"], token), cs,
                                                     rb16, rbt16, rct16, lr_t, li_t)
    grads_small["s5_d"] = dd
    grads_small["s5_mats"] = (drb, drc, dlr, dli)
    grads_small["norm_mix0"] = dg_mix0
    return loss_tile, grad_x, grads_small


SMALL_NAMES = ["norm_mix", "norm_mlp", "norm_kv", "norm_final", "s5_a_re", "s5_a_im", "s5_log_dt", "s5_b_re", "s5_b_im",
               "s5_c_re", "s5_c_im", "s5_d", "s5_b_glu", "b_kv", "b_q", "sinks", "b_o"]
BIG_NAMES = ["s5_w_glu", "w_kv", "w_q", "w_o", "w_mlp_in", "w_mlp_out"]
WEIGHT_ORDER = ["norm_mix", "norm_mlp", "norm_kv", "norm_final", "s5_a_re", "s5_a_im", "s5_log_dt", "s5_b_re", "s5_b_im",
                "s5_c_re", "s5_c_im", "s5_d", "s5_w_glu", "s5_b_glu", "w_kv", "b_kv", "w_q", "b_q", "sinks", "w_o", "b_o",
                "w_mlp_in", "w_mlp_out"]


def kernel(x, norm_mix, norm_mlp, norm_kv, norm_final, s5_a_re, s5_a_im, s5_log_dt, s5_b_re, s5_b_im, s5_c_re, s5_c_im, s5_d, s5_w_glu, s5_b_glu, w_kv, b_kv, w_q, b_q, sinks, w_o, b_o, w_mlp_in, w_mlp_out, loss_target, m_norm_mix, m_norm_mlp, m_norm_kv, m_norm_final, m_s5_a_re, m_s5_a_im, m_s5_log_dt, m_s5_b_re, m_s5_b_im, m_s5_c_re, m_s5_c_im, m_s5_d, m_s5_w_glu, m_s5_b_glu, m_w_kv, m_b_kv, m_w_q, m_b_q, m_sinks, m_w_o, m_b_o, m_w_mlp_in, m_w_mlp_out, v_norm_mix, v_norm_mlp, v_norm_kv, v_norm_final, v_s5_a_re, v_s5_a_im, v_s5_log_dt, v_s5_b_re, v_s5_b_im, v_s5_c_re, v_s5_c_im, v_s5_d, v_s5_w_glu, v_s5_b_glu, v_w_kv, v_b_kv, v_w_q, v_b_q, v_sinks, v_w_o, v_b_o, v_w_mlp_in, v_w_mlp_out):
    env = dict(locals())
    w = {n: env[n] for n in WEIGHT_ORDER}
    mom = {n: env["m_" + n] for n in WEIGHT_ORDER}
    var = {n: env["v_" + n] for n in WEIGHT_ORDER}
    d = D_MODEL
    xi, yi, ci = lax.axis_index("x"), lax.axis_index("y"), lax.axis_index("c")
    chip = 2 * xi + yi
    where = jnp.stack([ci, chip]).astype(jnp.int32)

    dsh, bsh = s5_d.shape[1], s5_b_glu.shape[1]
    packed = jnp.concatenate([s5_d.reshape(-1, 128), s5_b_glu.reshape(-1, 128)])
    n_d, n_b = dsh // 128, bsh // 128
    slab = lax.dynamic_update_slice(jnp.zeros((4, 8, 128), F32), jnp.pad(packed, ((0, 8 - n_d - n_b), (0, 0)))[None],
                                    (chip, 0, 0))

    big = [s5_w_glu, w_kv[None], w_q, w_o, w_mlp_in, w_mlp_out]
    entries = [(0, 0, "col"), (1, 0, "row"), (2, 0, "row"), (3, 0, "row"), (4, 0, "col"), (4, 1, "col"),
               (5, 0, "row"), (5, 1, "row")]
    names = ["w_glu", "w_kv", "w_q", "w_o", "w_in0", "w_in1", "w_out0", "w_out1"]
    kinds = dict(zip(names, [k for _, _, k in entries]))
    shard_shapes = dict(zip(names, [tuple(big[a].shape[1:]) for a, _, _ in entries]))

    placed_w = dict(zip(names, cast_place(big, entries, where)))
    placed_w["vectors"], kinds["vectors"], shard_shapes["vectors"] = slab, "slab", None
    gather_groups = {"glu": ["w_glu"], "mlp_in0": ["w_in0"], "mlp_out0": ["w_out0"], "attn": ["w_kv", "w_q", "w_o"],
                     "mlp_in1": ["w_in1"], "mlp_out1": ["w_out1"]}
    order = ["vectors"] + [n for members in gather_groups.values() for n in members]
    send, recv, thru, token = gather_start([placed_w[n] for n in order], [kinds[n] for n in order],
                                           [shard_shapes[n] for n in order])
    started = dict(zip(order, thru))
    (gathered_rows,) = gather_wait("gather_wait_vectors", send, recv, [started["vectors"]], ["slab"], [None], None, 0)
    d_full = gathered_rows[:, 0:n_d].reshape(1, -1)
    bglu_full = gathered_rows[:, n_d:n_d + n_b].reshape(1, -1)

    def need(group, after):
        members = gather_groups[group]
        ks, shapes = [kinds[n] for n in members], [shard_shapes[n] for n in members]
        landed = gather_wait("gather_wait_" + group, send, recv, [started[n] for n in members], ks, shapes, after,
                             order.index(members[0]))
        return dict(zip(members, forward_halves("forward_halves_" + group, landed, ks, shapes)))

    swapping, exchanging = {}, {}

    def emit_swap(group, partial):
        members = list(partial)
        send, recv, mine, lands, tok = swap_start("swap_start_" + group, [partial[n] for n in members],
                                                  [kinds[n] for n in members])
        swapping[group] = (members, send, recv, mine, lands)
        return tok

    def emit_exchange(group, after):
        members, send, recv, mine, lands = swapping[group]
        ks, shapes = [kinds[n] for n in members], [shard_shapes[n] for n in members]
        mine, landed = swap_wait("swap_wait_" + group, send, recv, mine, lands, ks, after)
        sums = add_halves("add_halves_" + group, mine, landed, ks, where)
        send, recv, parts, lands, tok = exchange_start("exchange_start_" + group, sums, ks, shapes)
        exchanging[group] = (members, send, recv, parts, lands)
        return tok

    s5_args = (s5_a_re[0], s5_a_im[0], s5_log_dt[0], s5_b_re[0], s5_b_im[0])
    small = {
        "norm_mix0": norm_mix[0:1] + token[0:1, 0:1], "norm_mix1": norm_mix[1:2], "norm_mlp0": norm_mlp[0:1], "norm_mlp1": norm_mlp[1:2],
        "norm_kv": norm_kv.reshape(1, d), "norm_final": norm_final.reshape(1, d), "s5_operands": s5_prep(*s5_args, s5_c_re[0], s5_c_im[0]),
        "s5_d": d_full, "s5_b_glu": bglu_full,
        "b_kv": b_kv.reshape(1, -1), "b_q": b_q, "sinks": sinks, "b_o": b_o,
    }
    loss_row, grad_x, gs = _local_step(x[0], loss_target[0], small, need, emit_swap, emit_exchange)

    mats, lams = s5_compact(*gs["s5_mats"])
    rows = [gs["norm_mix0"], gs["norm_mix1"], gs["norm_mlp0"], gs["norm_mlp1"], gs["norm_kv"], gs["norm_final"], gs["s5_d"],
            gs["b_q"], gs["b_o"], gs["s5_b_glu"], gs["b_kv"], gs["sinks"], loss_row, jnp.zeros((2, d), F32)]
    vecs, lams, mats = all_reduce_small("reduce_small", [jnp.concatenate(rows, axis=0), lams, mats], [F32, F32, BF16])
    grads = split_vectors(where, vecs, dsh, bsh)
    loss = grads.pop("loss")[0, 0]
    g_are, g_aim, g_dt, g_bre, g_bim, dc_re, dc_im = s5_param_bwd(mats, lams, *s5_args)
    grads.update({"s5_a_re": g_are[None], "s5_a_im": g_aim[None], "s5_log_dt": g_dt[None], "s5_b_re": g_bre[None],
                  "s5_b_im": g_bim[None], "s5_c_re": dc_re[None], "s5_c_im": dc_im[None]})

    reduced = [None] * len(big)
    where_of = dict(zip(names, entries))
    for group, after in (("layer1", grad_x), ("layer0", mats)):
        members, send, recv, parts, lands = exchanging[group]
        ks, shapes = [kinds[n] for n in members], [shard_shapes[n] for n in members]
        parts, lands = exchange_wait("exchange_wait_" + group, send, recv, parts, lands, ks, shapes, after)
        targets = [where_of[n][0] for n in members]
        sums = sum_shards("sum_shards_" + group, parts, lands, ks, shapes, where, [where_of[n][1] for n in members],
                          [big[a].shape[0] for a in targets], [reduced[a] for a in targets])
        for a, arr in zip(targets, sums):
            reduced[a] = arr
    reduced = share_halves(reduced, entries)
    for n, g in zip(BIG_NAMES, reduced):
        grads[n] = g.reshape(w[n].shape)

    delta, new_m, new_v = {}, {}, {}
    flat = lambda t: [t[n].reshape(-1, t[n].shape[-1]) for n in BIG_NAMES]
    for table, arrays in zip((grads, delta, new_m, new_v), adamw("adamw_big", flat(w), flat(grads), flat(mom), flat(var))):
        for n, a in zip(BIG_NAMES, arrays):
            table[n] = a.reshape(w[n].shape)

    def view(n, a):
        return a.reshape(1, -1) if a.ndim == 1 else jnp.swapaxes(a, -1, -2) if n in ("s5_b_re", "s5_b_im") else a

    sw, sg, sm, sv = ([view(n, t[n]) for n in SMALL_NAMES] for t in (w, grads, mom, var))
    for n, a, b, c_ in zip(SMALL_NAMES, *adamw_native("adamw_small", sw, sg, sm, sv)):
        delta[n], new_m[n], new_v[n] = (view(n, t) if t.ndim == 4 else t for t in (a, b, c_))

    out = [loss.reshape(()), grad_x[None]]
    for table in (grads, delta, new_m, new_v):
        out += [table[n].reshape(w[n].shape) for n in WEIGHT_ORDER]
    return tuple(out)
```

```python
import math

import jax
import jax.numpy as jnp
from jax import lax
from jax.experimental import pallas as pl
from jax.experimental.pallas import tpu as pltpu

F32 = jnp.float32
BF16 = jnp.bfloat16

D_MODEL = 1024
S5_GROUPS = 64
S5_GROUP = 16
S5_STATE = 64
N_KV = 4
N_Q = 16
HEAD_DIM = 64
BLOCK = 128
NORM_EPS = 1e-5
LAMBDA_RE_MAX = -1e-4
ADAM_LR, ADAM_B1, ADAM_B2, ADAM_EPS, ADAM_WD, ADAM_STEP = 0.001, 0.9, 0.999, 1e-08, 0.01, 10

VMEM_LIMIT_BYTES = 56 * 1024 * 1024
S5_CHUNK = 256
S5_BLOCKS = 4
MESH = pl.DeviceIdType.MESH


def _params(sem=None):
    return pltpu.CompilerParams(dimension_semantics=sem, vmem_limit_bytes=VMEM_LIMIT_BYTES)


def _sds(shape, dtype):
    return jax.ShapeDtypeStruct(shape, dtype)


def _rms_hat(xv):
    r = lax.rsqrt(jnp.mean(xv * xv, axis=-1, keepdims=True) + NORM_EPS)
    return xv * r, r


def mm_nn(name, a, w, col_offsets, n_out, epilogue, out_dtypes, extras=(), rowvecs=(), n_sums=0, tm=1024, tn=512):
    m, k = a.shape
    tm, tn = min(tm, m), min(tn, n_out)
    nw, ne, nr, no = len(col_offsets), len(extras), len(rowvecs), len(out_dtypes)

    def body(a_ref, *refs):
        w_refs, e_refs, r_refs = refs[:nw], refs[nw:nw + ne], refs[nw + ne:nw + ne + nr]
        o_refs, s_refs = refs[nw + ne + nr:nw + ne + nr + no], refs[nw + ne + nr + no:]
        av = a_ref[...]
        accs = [jnp.dot(av, w_ref[...], preferred_element_type=F32) for w_ref in w_refs]
        outs = epilogue(accs, [e[...] for e in e_refs], [r[...] for r in r_refs])
        for o_ref, o in zip(o_refs, outs[:no]):
            o_ref[...] = o.astype(o_ref.dtype)
        if n_sums:
            @pl.when(pl.program_id(1) == 0)
            def _():
                for s_ref in s_refs:
                    s_ref[...] = jnp.zeros_like(s_ref)

            for s_ref, val in zip(s_refs, outs[no:]):
                s_ref[...] += val

    def wspec(off):
        return pl.BlockSpec((k, tn), lambda j, i, off=off: (0, off // tn + j))

    def rspec(off):
        return pl.BlockSpec((1, tn), lambda j, i, off=off: (0, off // tn + j))

    tile = pl.BlockSpec((tm, tn), lambda j, i: (i, j))
    in_specs = ([pl.BlockSpec((tm, k), lambda j, i: (i, 0))] + [wspec(o) for o in col_offsets]
                + [tile] * ne + [rspec(o) for _, o in rowvecs])
    sem = ("parallel", "arbitrary") if n_sums else ("parallel", "parallel")
    return pl.pallas_call(
        body, grid=(n_out // tn, m // tm), in_specs=in_specs,
        out_specs=[tile] * no + [pl.BlockSpec((1, tn), lambda j, i: (0, j))] * n_sums,
        out_shape=[_sds((m, n_out), dt) for dt in out_dtypes] + [_sds((1, n_out), F32)] * n_sums, name=name,
        compiler_params=_params(sem))(a, *([w] * nw), *extras, *[r for r, _ in rowvecs])


def mm_nt(name, g, w, epilogue, out_dtypes, extras=(), rowvecs=(), n_sums=0, tm=512, tk=512):
    m, n = g.shape
    k = w.shape[0]
    tm, tk = min(tm, m), min(tk, k)
    ne, nr, no = len(extras), len(rowvecs), len(out_dtypes)

    def body(g_ref, w_ref, *refs):
        e_refs, r_refs, o_refs, s_refs = refs[:ne], refs[ne:ne + nr], refs[ne + nr:ne + nr + no], refs[ne + nr + no:]
        acc = lax.dot_general(g_ref[...], w_ref[...], (((1,), (1,)), ((), ())), preferred_element_type=F32)
        outs = epilogue(acc, [e[...] for e in e_refs], [r[...] for r in r_refs])
        for o_ref, o in zip(o_refs, outs[:no]):
            o_ref[...] = o.astype(o_ref.dtype)
        if n_sums:
            @pl.when(pl.program_id(0) == 0)
            def _():
                for s_ref in s_refs:
                    s_ref[...] = jnp.zeros_like(s_ref)

            for s_ref, val in zip(s_refs, outs[no:]):
                s_ref[...] += val

    tile = pl.BlockSpec((tm, tk), lambda i, j: (i, j))
    vec = pl.BlockSpec((1, tk), lambda i, j: (0, j))
    sem = ("arbitrary", "parallel") if n_sums else ("parallel", "parallel")
    return pl.pallas_call(
        body, grid=(m // tm, k // tk),
        in_specs=[pl.BlockSpec((tm, n), lambda i, j: (i, 0)), pl.BlockSpec((tk, n), lambda i, j: (j, 0))]
        + [tile] * ne + [vec] * nr,
        out_specs=[tile] * no + [vec] * n_sums,
        out_shape=[_sds((m, k), dt) for dt in out_dtypes] + [_sds((1, k), F32)] * n_sums, name=name,
        compiler_params=_params(sem))(g, w, *extras, *rowvecs)


def mm_tn(name, a, g, tk=512, tn=512):
    m, k = a.shape
    n = g.shape[1]
    tk, tn = min(tk, k), min(tn, n)

    def body(a_ref, g_ref, o_ref):
        acc = lax.dot_general(a_ref[...], g_ref[...], (((0,), (0,)), ((), ())), preferred_element_type=F32)
        o_ref[...] = acc.astype(o_ref.dtype)

    return pl.pallas_call(
        body, grid=(k // tk, n // tn),
        in_specs=[pl.BlockSpec((m, tk), lambda i, j: (0, i)), pl.BlockSpec((m, tn), lambda i, j: (0, j))],
        out_specs=pl.BlockSpec((tk, tn), lambda i, j: (i, j)), out_shape=_sds((k, n), BF16), name=name,
        compiler_params=_params(("parallel", "parallel")))(a, g)


def _row_mask(tc):
    row = lax.broadcasted_iota(jnp.int32, (8 * tc, 256), 0) % 8
    col = lax.broadcasted_iota(jnp.int32, (8 * tc, 256), 1) // 32
    return row == col


def _expand_rows(val, mask):
    tc, width = val.shape
    rep = jnp.broadcast_to(val[:, None, :], (tc, 8, width)).reshape(8 * tc, width)
    return jnp.where(mask, rep, 0.0).astype(BF16)


def _stage(ref, val):
    ref[0] = val[:, 0:128]
    ref[1] = val[:, 128:256]


def _gather_rows(src_ref, tc):
    halves = []
    for half in range(2):
        col = lax.broadcasted_iota(jnp.int32, (tc, 128), 1) // 32 + 4 * half
        out = jnp.zeros((tc, 128), F32)
        for s8 in range(4 * half, 4 * half + 4):
            out = jnp.where(col == s8, src_ref.at[half][pl.ds(s8, tc, stride=8), :], out)
        halves.append(out)
    return jnp.concatenate(halves, axis=1)


def _gelu(x):
    c = math.sqrt(2.0 / math.pi)
    return 0.5 * x * (1.0 + jnp.tanh(c * (x + 0.044715 * x * x * x)))


def _gelu_grad(x):
    c = math.sqrt(2.0 / math.pi)
    t = jnp.tanh(c * (x + 0.044715 * x * x * x))
    return 0.5 * (1.0 + t) + 0.5 * x * (1.0 - t * t) * c * (1.0 + 3.0 * 0.044715 * x * x)


def s5_fwd(x, gain, d_skip, rb, rc, lam_r, lam_i):
    n_rows = x.shape[0]
    tc = min(S5_CHUNK, n_rows)
    nc = n_rows // tc

    def body(x_ref, g_ref, d_ref, rb_ref, rc_ref, lr_ref, li_ref, ge_ref, y2_ref, cs_ref, bux, yrows, carry):
        i = pl.program_id(0)
        u = _rms_hat(x_ref[...])[0] * g_ref[...]

        @pl.when(i == 0)
        def _():
            carry[...] = jnp.zeros_like(carry)

        cs_ref[0] = carry[...]
        mask = _row_mask(tc)
        for blk in range(S5_BLOCKS):
            lhs = _expand_rows(u[:, blk * 256:(blk + 1) * 256], mask)
            bux[blk] = jnp.dot(lhs, rb_ref[blk], preferred_element_type=F32)
        lam = [(lr_ref[blk], li_ref[blk]) for blk in range(S5_BLOCKS)]

        def step(t, c):
            r0 = pl.multiple_of(t * 8, 8)
            new = []
            for blk in range(S5_BLOCKS):
                xr, xi = c[2 * blk], c[2 * blk + 1]
                lr, li = lam[blk]
                nr = lr * xr - li * xi + bux[blk, pl.ds(r0, 8), 0:128]
                ni = lr * xi + li * xr + bux[blk, pl.ds(r0, 8), 128:256]
                bux[blk, pl.ds(r0, 8), 0:128] = nr
                bux[blk, pl.ds(r0, 8), 128:256] = ni
                new += [nr, ni]
            return tuple(new)

        c0 = []
        for blk in range(S5_BLOCKS):
            c0 += [carry[blk, :, 0:128], carry[blk, :, 128:256]]
        cn = lax.fori_loop(0, tc, step, tuple(c0), unroll=4)
        for blk in range(S5_BLOCKS):
            carry[blk, :, 0:128] = cn[2 * blk]
            carry[blk, :, 128:256] = cn[2 * blk + 1]
        for blk in range(S5_BLOCKS):
            _stage(yrows, jnp.dot(bux[blk].astype(BF16), rc_ref[blk], preferred_element_type=F32))
            sl = slice(blk * 256, (blk + 1) * 256)
            y2 = _gather_rows(yrows, tc) + d_ref[:, sl] * u[:, sl]
            y2_ref[:, sl] = y2
            ge_ref[:, sl] = _gelu(y2).astype(BF16)

    row = pl.BlockSpec((tc, D_MODEL), lambda i: (i, 0))
    vec = pl.BlockSpec((1, D_MODEL), lambda i: (0, 0))
    mat = pl.BlockSpec((S5_BLOCKS, 256, 256), lambda i: (0, 0, 0))
    lamspec = pl.BlockSpec((S5_BLOCKS, 8, 128), lambda i: (0, 0, 0))
    return pl.pallas_call(
        body, grid=(nc,),
        in_specs=[row, vec, vec, mat, mat, lamspec, lamspec],
        out_specs=[row, row, pl.BlockSpec((1, S5_BLOCKS, 8, 256), lambda i: (i, 0, 0, 0))],
        out_shape=[_sds((n_rows, D_MODEL), BF16), _sds((n_rows, D_MODEL), F32), _sds((nc, S5_BLOCKS, 8, 256), F32)],
        scratch_shapes=[pltpu.VMEM((S5_BLOCKS, 8 * tc, 256), F32), pltpu.VMEM((2, 8 * tc, 128), F32),
                        pltpu.VMEM((S5_BLOCKS, 8, 256), F32)],
        name="s5_fwd", compiler_params=_params(("arbitrary",)))(x, gain, d_skip, rb, rc, lam_r, lam_i)


def s5_bwd(x, gain, dy2, res, d_skip, cs, rb, rbt, rct, lam_r, lam_i):
    n_rows = x.shape[0]
    tc = min(S5_CHUNK, n_rows)
    nc = n_rows // tc

    def body(x_ref, g_ref, dy_ref, res_ref, d_ref, cs_ref, rb_ref, rbt_ref, rct_ref, lr_ref, li_ref,
             dx_ref, dd_ref, drb_ref, drc_ref, dlr_ref, dli_ref, dg_ref, tmp, du, lhsu, lhsd, xs, adj, acarry):
        i = pl.program_id(0)
        u = _rms_hat(x_ref[...])[0] * g_ref[...]

        @pl.when(i == 0)
        def _():
            acarry[...] = jnp.zeros_like(acarry)
            dd_ref[...] = jnp.zeros_like(dd_ref)
            drb_ref[...] = jnp.zeros_like(drb_ref)
            drc_ref[...] = jnp.zeros_like(drc_ref)
            dlr_ref[...] = jnp.zeros_like(dlr_ref)
            dli_ref[...] = jnp.zeros_like(dli_ref)
            dg_ref[...] = jnp.zeros_like(dg_ref)

        dd_ref[...] += jnp.sum(dy_ref[...] * u, axis=0, keepdims=True)
        mask = _row_mask(tc)
        for blk in range(S5_BLOCKS):
            sl = slice(blk * 256, (blk + 1) * 256)
            lhsu[blk] = _expand_rows(u[:, sl], mask)
            xs[blk] = jnp.dot(lhsu[blk], rb_ref[blk], preferred_element_type=F32)
            lhsd[blk] = _expand_rows(dy_ref[:, sl], mask)
            adj[blk] = jnp.dot(lhsd[blk], rct_ref[blk], preferred_element_type=F32)
        lam = [(lr_ref[blk], li_ref[blk]) for blk in range(S5_BLOCKS)]

        def fstep(t, c):
            r0 = pl.multiple_of(t * 8, 8)
            new = []
            for blk in range(S5_BLOCKS):
                xr, xi = c[2 * blk], c[2 * blk + 1]
                lr, li = lam[blk]
                nr = lr * xr - li * xi + xs[blk, pl.ds(r0, 8), 0:128]
                ni = lr * xi + li * xr + xs[blk, pl.ds(r0, 8), 128:256]
                xs[blk, pl.ds(r0, 8), 0:128] = nr
                xs[blk, pl.ds(r0, 8), 128:256] = ni
                new += [nr, ni]
            return tuple(new)

        c0 = []
        for blk in range(S5_BLOCKS):
            c0 += [cs_ref[0, blk, :, 0:128], cs_ref[0, blk, :, 128:256]]
        lax.fori_loop(0, tc, fstep, tuple(c0), unroll=4)

        def bstep(k, c):
            t = tc - 1 - k
            r0 = pl.multiple_of(t * 8, 8)
            rp = pl.multiple_of(jnp.maximum(t - 1, 0) * 8, 8)
            first = t == 0
            new_a, new_g = [], []
            for blk in range(S5_BLOCKS):
                ar, ai = c[0][2 * blk], c[0][2 * blk + 1]
                glr, gli = c[1][2 * blk], c[1][2 * blk + 1]
                lr, li = lam[blk]
                nr = lr * ar + li * ai + adj[blk, pl.ds(r0, 8), 0:128]
                ni = lr * ai - li * ar + adj[blk, pl.ds(r0, 8), 128:256]
                adj[blk, pl.ds(r0, 8), 0:128] = nr
                adj[blk, pl.ds(r0, 8), 128:256] = ni
                pr = jnp.where(first, cs_ref[0, blk, :, 0:128], xs[blk, pl.ds(rp, 8), 0:128])
                pi = jnp.where(first, cs_ref[0, blk, :, 128:256], xs[blk, pl.ds(rp, 8), 128:256])
                new_a += [nr, ni]
                new_g += [glr + nr * pr + ni * pi, gli + ni * pr - nr * pi]
            return tuple(new_a), tuple(new_g)

        a0, g0 = [], []
        for blk in range(S5_BLOCKS):
            a0 += [acarry[blk, :, 0:128], acarry[blk, :, 128:256]]
            g0 += [dlr_ref[blk], dli_ref[blk]]
        an, gn = lax.fori_loop(0, tc, bstep, (tuple(a0), tuple(g0)), unroll=2)
        for blk in range(S5_BLOCKS):
            acarry[blk, :, 0:128] = an[2 * blk]
            acarry[blk, :, 128:256] = an[2 * blk + 1]
            dlr_ref[blk] = gn[2 * blk]
            dli_ref[blk] = gn[2 * blk + 1]
        for blk in range(S5_BLOCKS):
            sl = slice(blk * 256, (blk + 1) * 256)
            ab = adj[blk].astype(BF16)
            _stage(tmp, jnp.dot(ab, rbt_ref[blk], preferred_element_type=F32))
            du[:, sl] = _gather_rows(tmp, tc) + d_ref[:, sl] * dy_ref[:, sl]
            drb_ref[blk] += lax.dot_general(lhsu[blk], ab, (((0,), (0,)), ((), ())), preferred_element_type=F32)
            drc_ref[blk] += lax.dot_general(lhsd[blk], xs[blk].astype(BF16), (((0,), (0,)), ((), ())),
                                            preferred_element_type=F32)
        xh, r = _rms_hat(x_ref[...])
        dg_ref[...] += jnp.sum(du[...] * xh, axis=0, keepdims=True)
        dxh = du[...] * g_ref[...]
        dx_ref[...] = r * (dxh - xh * jnp.mean(dxh * xh, axis=-1, keepdims=True)) + res_ref[...]

    rev = pl.BlockSpec((tc, D_MODEL), lambda i: (nc - 1 - i, 0))
    vec = pl.BlockSpec((1, D_MODEL), lambda i: (0, 0))
    mat = pl.BlockSpec((S5_BLOCKS, 256, 256), lambda i: (0, 0, 0))
    lamspec = pl.BlockSpec((S5_BLOCKS, 8, 128), lambda i: (0, 0, 0))
    big = pltpu.VMEM((S5_BLOCKS, 8 * tc, 256), F32)
    bigb = pltpu.VMEM((S5_BLOCKS, 8 * tc, 256), BF16)
    return pl.pallas_call(
        body, grid=(nc,),
        in_specs=[rev, vec, rev, rev, vec, pl.BlockSpec((1, S5_BLOCKS, 8, 256), lambda i: (nc - 1 - i, 0, 0, 0)),
                  mat, mat, mat, lamspec, lamspec],
        out_specs=[rev, vec, mat, mat, lamspec, lamspec, vec],
        out_shape=[_sds((n_rows, D_MODEL), F32), _sds((1, D_MODEL), F32), _sds((S5_BLOCKS, 256, 256), F32),
                   _sds((S5_BLOCKS, 256, 256), F32), _sds((S5_BLOCKS, 8, 128), F32), _sds((S5_BLOCKS, 8, 128), F32),
                   _sds((1, D_MODEL), F32)],
        scratch_shapes=[pltpu.VMEM((2, 8 * tc, 128), F32), pltpu.VMEM((tc, D_MODEL), F32), bigb, bigb, big, big,
                        pltpu.VMEM((S5_BLOCKS, 8, 256), F32)],
        name="s5_bwd", compiler_params=_params(("arbitrary",)))(
            x, gain, dy2, res, d_skip, cs, rb, rbt, rct, lam_r, lam_i)


def _s5_views(a_re, a_im, log_dt, b_re, b_im):
    return a_re[:, None, :], a_im[:, None, :], log_dt[:, None, None], jnp.swapaxes(b_re, 1, 2), jnp.swapaxes(b_im, 1, 2)


def _s5_factors(a_re, a_im, log_dt):
    lr, li, dt = jnp.minimum(a_re, LAMBDA_RE_MAX), a_im, jnp.exp(log_dt)
    mag, ang = jnp.exp(lr * dt), li * dt
    lbr, lbi = mag * jnp.cos(ang), mag * jnp.sin(ang)
    den = lr * lr + li * li
    fr, fi = ((lbr - 1.0) * lr + lbi * li) / den, (lbi * lr - (lbr - 1.0) * li) / den
    return lr, li, dt, lbr, lbi, fr, fi, den


def s5_prep(a_re, a_im, log_dt, b_re, b_im, c_re, c_im):
    def body(ar_ref, ai_ref, t_ref, br_ref, bi_ref, cr_ref, ci_ref, rb_ref, rbt_ref, rc_ref, rct_ref, lr_ref, li_ref):
        _, _, _, lbr, lbi, fr, fi, _ = _s5_factors(ar_ref[...], ai_ref[...], t_ref[...])
        lr_ref[...] = lbr
        li_ref[...] = lbi
        bre = fr * br_ref[...] - fi * bi_ref[...]
        bim = fr * bi_ref[...] + fi * br_ref[...]
        even = (lax.broadcasted_iota(jnp.int32, (256, S5_STATE), 0) // S5_GROUP) % 2 == 0

        def assemble(re, im):
            re, im = re.reshape(256, S5_STATE), im.reshape(256, S5_STATE)
            return jnp.concatenate([jnp.where(even, re, 0.0), jnp.where(even, 0.0, re), jnp.where(even, im, 0.0),
                                    jnp.where(even, 0.0, im)], axis=1)

        for blk in range(S5_BLOCKS):
            sl = slice(16 * blk, 16 * blk + 16)
            rb = assemble(bre[sl], bim[sl])
            rct = assemble(cr_ref[sl], -ci_ref[sl])
            rb_ref[blk] = rb.astype(BF16)
            rbt_ref[blk] = rb.T.astype(BF16)
            rct_ref[blk] = rct.astype(BF16)
            rc_ref[blk] = rct.T.astype(BF16)

    vm = pl.BlockSpec(memory_space=pltpu.VMEM)
    mat = _sds((S5_BLOCKS, 256, 256), BF16)
    lam = _sds((S5_GROUPS, 1, S5_STATE), F32)
    rb, rbt, rc, rct, lam_r, lam_i = pl.pallas_call(
        body, in_specs=[vm] * 7, out_specs=[vm] * 6, out_shape=[mat, mat, mat, mat, lam, lam], name="s5_prep",
        compiler_params=_params())(*_s5_views(a_re, a_im, log_dt, b_re, b_im), c_re, c_im)
    return rb, rbt, rc, rct, lam_r.reshape(S5_BLOCKS, 8, 128), lam_i.reshape(S5_BLOCKS, 8, 128)


def s5_param_bwd(mats, lams, a_re, a_im, log_dt, b_re, b_im):
    def body(m_ref, glr_ref, gli_ref, ar_ref, ai_ref, t_ref, br_ref, bi_ref,
             dar_ref, dai_ref, dt_ref, dbr_ref, dbi_ref, dcr_ref, dci_ref):
        lr, li, dt, lbr, lbi, fr, fi, den = _s5_factors(ar_ref[...], ai_ref[...], t_ref[...])
        shape = (S5_GROUPS, S5_GROUP, S5_STATE)
        gbr, gbi = m_ref[0:1024, 0:64].reshape(shape), m_ref[0:1024, 64:128].reshape(shape)
        dcr_ref[...] = m_ref[1024:2048, 0:64].reshape(shape)
        dci_ref[...] = -m_ref[1024:2048, 64:128].reshape(shape)
        br, bi = br_ref[...], bi_ref[...]
        dbr_ref[...] = fr * gbr + fi * gbi
        dbi_ref[...] = fr * gbi - fi * gbr
        dfr = jnp.sum(gbr * br + gbi * bi, axis=1, keepdims=True)
        dfi = jnp.sum(gbi * br - gbr * bi, axis=1, keepdims=True)
        nr, ni = (dfr * lr - dfi * li) / den, (dfr * li + dfi * lr) / den
        qr, qi = (fr * lr + fi * li) / den, (fi * lr - fr * li) / den
        lam_r, lam_i = -(dfr * qr + dfi * qi), -(dfi * qr - dfr * qi)
        gr, gi = glr_ref[...] + nr, gli_ref[...] + ni
        zr, zi = gr * lbr + gi * lbi, gi * lbr - gr * lbi
        a = ar_ref[...]
        dar_ref[...] = (lam_r + zr * dt) * jnp.where(a < LAMBDA_RE_MAX, 1.0, jnp.where(a == LAMBDA_RE_MAX, 0.5, 0.0))
        dai_ref[...] = lam_i + zi * dt
        dt_ref[...] = jnp.sum(zr * lr + zi * li, axis=2, keepdims=True) * dt

    vm = pl.BlockSpec(memory_space=pltpu.VMEM)
    state = _sds((S5_GROUPS, 1, S5_STATE), F32)
    wide = _sds((S5_GROUPS, S5_GROUP, S5_STATE), F32)
    glr = lams[0:32].reshape(S5_GROUPS, 1, S5_STATE)
    gli = lams[32:64].reshape(S5_GROUPS, 1, S5_STATE)
    dar, dai, ddt, dbr, dbi, dcr, dci = pl.pallas_call(
        body, in_specs=[vm] * 8, out_specs=[vm] * 7,
        out_shape=[state, state, _sds((S5_GROUPS, 1, 1), F32), wide, wide, wide, wide], name="s5_param_bwd",
        compiler_params=_params())(mats, glr, gli, *_s5_views(a_re, a_im, log_dt, b_re, b_im))
    return (dar.reshape(S5_GROUPS, S5_STATE), dai.reshape(S5_GROUPS, S5_STATE), ddt.reshape(S5_GROUPS),
            jnp.swapaxes(dbr, 1, 2), jnp.swapaxes(dbi, 1, 2), dcr, dci)


def s5_compact(drb, drct, dlr, dli):
    def body(drb_ref, drct_ref, dlr_ref, dli_ref, o_ref, lam_ref):
        even = (lax.broadcasted_iota(jnp.int32, (256, 64), 0) // S5_GROUP) % 2 == 0
        for blk in range(S5_BLOCKS):
            for k, ref in enumerate((drb_ref, drct_ref)):
                m = ref[blk]
                re = jnp.where(even, m[:, 0:64], m[:, 64:128])
                im = jnp.where(even, m[:, 128:192], m[:, 192:256])
                o_ref[pl.ds(k * 1024 + blk * 256, 256), :] = jnp.concatenate([re, im], axis=1)
            lam_ref[pl.ds(blk * 8, 8), :] = dlr_ref[blk]
            lam_ref[pl.ds(32 + blk * 8, 8), :] = dli_ref[blk]

    vm = pl.BlockSpec(memory_space=pltpu.VMEM)
    return pl.pallas_call(body, in_specs=[vm] * 4, out_specs=[vm, vm], out_shape=[_sds((2048, 128), F32), _sds((64, 128), F32)],
                          name="s5_compact", compiler_params=_params())(drb, drct, dlr, dli)


NEG = -1e30


GROUP = N_Q // N_KV


def _attn_masks(n):
    qi = lax.broadcasted_iota(jnp.int32, (GROUP * BLOCK, BLOCK), 0) % BLOCK
    kj = lax.broadcasted_iota(jnp.int32, (GROUP * BLOCK, BLOCK), 1)
    return jnp.logical_and(kj > qi, n > 0), kj <= qi


def _stack_heads(ref, kh):
    return jnp.concatenate([ref[:, (GROUP * kh + g) * HEAD_DIM:(GROUP * kh + g + 1) * HEAD_DIM] for g in range(GROUP)], axis=0)


def _unstack_heads(val):
    return jnp.concatenate([val[g * BLOCK:(g + 1) * BLOCK] for g in range(GROUP)], axis=1)


def _sink_column(sink_ref, kh):
    grp = lax.broadcasted_iota(jnp.int32, (GROUP * BLOCK, 1), 0) // BLOCK
    col = jnp.zeros((GROUP * BLOCK, 1), F32)
    for g in range(GROUP):
        col = jnp.where(grp == g, sink_ref[GROUP * kh + g], col)
    return col, grp


def _attn_exp(q4, kp, kc, sink, mask_p, mask_c):
    scale = 1.0 / math.sqrt(HEAD_DIM)
    nt = (((1,), (1,)), ((), ()))
    sp = jnp.where(mask_p, lax.dot_general(q4, kp, nt, preferred_element_type=F32) * scale, NEG)
    sc = jnp.where(mask_c, lax.dot_general(q4, kc, nt, preferred_element_type=F32) * scale, NEG)
    m = jnp.maximum(jnp.maximum(jnp.max(sp, axis=-1, keepdims=True), jnp.max(sc, axis=-1, keepdims=True)), sink)
    pp = jnp.exp(sp - m)
    pc = jnp.exp(sc - m)
    ps = jnp.exp(sink - m)
    inv = 1.0 / (jnp.sum(pp, axis=-1, keepdims=True) + jnp.sum(pc, axis=-1, keepdims=True) + ps)
    return pp, pc, ps, inv


def attn_fwd(q, kv, sinks):
    n_rows = q.shape[0]
    nb = n_rows // BLOCK

    def body(sink_ref, q_ref, kvp_ref, kvc_ref, o_ref):
        n = pl.program_id(0)
        mask_p, mask_c = _attn_masks(n)
        outs = []
        for kh in range(N_KV):
            ks, vs = slice(kh * HEAD_DIM, (kh + 1) * HEAD_DIM), slice((N_KV + kh) * HEAD_DIM, (N_KV + kh + 1) * HEAD_DIM)
            sink, _ = _sink_column(sink_ref, kh)
            pp, pc, _, inv = _attn_exp(_stack_heads(q_ref, kh), kvp_ref[:, ks], kvc_ref[:, ks], sink, mask_p, mask_c)
            o4 = (jnp.dot(pp.astype(BF16), kvp_ref[:, vs], preferred_element_type=F32)
                  + jnp.dot(pc.astype(BF16), kvc_ref[:, vs], preferred_element_type=F32)) * inv
            outs.append(_unstack_heads(o4))
        o_ref[...] = jnp.concatenate(outs, axis=1).astype(BF16)

    kvw = 2 * N_KV * HEAD_DIM
    return pl.pallas_call(
        body, grid=(nb,),
        in_specs=[pl.BlockSpec(memory_space=pltpu.SMEM), pl.BlockSpec((BLOCK, D_MODEL), lambda n: (n, 0)),
                  pl.BlockSpec((BLOCK, kvw), lambda n: (jnp.maximum(n - 1, 0), 0)), pl.BlockSpec((BLOCK, kvw), lambda n: (n, 0))],
        out_specs=pl.BlockSpec((BLOCK, D_MODEL), lambda n: (n, 0)), out_shape=_sds((n_rows, D_MODEL), BF16),
        name="attn_fwd", compiler_params=_params(("parallel",)))(sinks, q, kv, kv)


def attn_bwd(q, kv, do, sinks):
    n_rows = q.shape[0]
    nb = n_rows // BLOCK
    kvw = 2 * N_KV * HEAD_DIM
    tn = (((0,), (0,)), ((), ()))
    nt = (((1,), (1,)), ((), ()))
    scale = 1.0 / math.sqrt(HEAD_DIM)

    def body(sink_ref, q_ref, kvp_ref, kvc_ref, do_ref, dq_ref, dbq_ref, dprev_ref, dcur_ref, dsink_ref):
        n = pl.program_id(0)
        mask_p, mask_c = _attn_masks(n)
        lane = lax.broadcasted_iota(jnp.int32, (1, D_MODEL), 1)
        dqs, dsink = [], jnp.zeros((1, D_MODEL), F32)
        dkp, dkc, dvp, dvc = [], [], [], []
        for kh in range(N_KV):
            ks, vs = slice(kh * HEAD_DIM, (kh + 1) * HEAD_DIM), slice((N_KV + kh) * HEAD_DIM, (N_KV + kh + 1) * HEAD_DIM)
            q4, do4 = _stack_heads(q_ref, kh), _stack_heads(do_ref, kh)
            kp, kc, vp, vc = kvp_ref[:, ks], kvc_ref[:, ks], kvp_ref[:, vs], kvc_ref[:, vs]
            sink, grp = _sink_column(sink_ref, kh)
            pp, pc, ps, inv = _attn_exp(q4, kp, kc, sink, mask_p, mask_c)
            pp, pc = pp * inv, pc * inv
            dpp = lax.dot_general(do4, vp, nt, preferred_element_type=F32)
            dpc = lax.dot_general(do4, vc, nt, preferred_element_type=F32)
            delta = jnp.sum(pp * dpp, axis=-1, keepdims=True) + jnp.sum(pc * dpc, axis=-1, keepdims=True)
            dsp = (pp * (dpp - delta) * scale).astype(BF16)
            dsc = (pc * (dpc - delta) * scale).astype(BF16)
            dsk = ps * inv * delta
            for g in range(GROUP):
                dsink = dsink + jnp.where(lane == GROUP * kh + g, -jnp.sum(jnp.where(grp == g, dsk, 0.0)), 0.0)
            dqs.append(_unstack_heads(jnp.dot(dsp, kp, preferred_element_type=F32)
                                      + jnp.dot(dsc, kc, preferred_element_type=F32)))
            dkp.append(lax.dot_general(dsp, q4, tn, preferred_element_type=F32))
            dkc.append(lax.dot_general(dsc, q4, tn, preferred_element_type=F32))
            dvp.append(lax.dot_general(pp.astype(BF16), do4, tn, preferred_element_type=F32))
            dvc.append(lax.dot_general(pc.astype(BF16), do4, tn, preferred_element_type=F32))
        dq = jnp.concatenate(dqs, axis=1)
        dq_ref[...] = dq.astype(BF16)
        dprev_ref[0] = jnp.concatenate(dkp + dvp, axis=1)
        dcur_ref[0] = jnp.concatenate(dkc + dvc, axis=1)

        @pl.when(n == 0)
        def _():
            dbq_ref[...] = jnp.zeros_like(dbq_ref)
            dsink_ref[...] = jnp.zeros_like(dsink_ref)

        dbq_ref[...] += jnp.sum(dq, axis=0, keepdims=True)
        dsink_ref[...] += dsink

    blk = pl.BlockSpec((BLOCK, D_MODEL), lambda n: (n, 0))
    part = pl.BlockSpec((1, BLOCK, kvw), lambda n: (n, 0, 0))
    return pl.pallas_call(
        body, grid=(nb,),
        in_specs=[pl.BlockSpec(memory_space=pltpu.SMEM), blk,
                  pl.BlockSpec((BLOCK, kvw), lambda n: (jnp.maximum(n - 1, 0), 0)), pl.BlockSpec((BLOCK, kvw), lambda n: (n, 0)), blk],
        out_specs=[blk, pl.BlockSpec((1, D_MODEL), lambda n: (0, 0)), part, part, pl.BlockSpec((1, D_MODEL), lambda n: (0, 0))],
        out_shape=[_sds((n_rows, D_MODEL), BF16), _sds((1, D_MODEL), F32), _sds((nb, BLOCK, kvw), F32),
                   _sds((nb, BLOCK, kvw), F32), _sds((1, D_MODEL), F32)],
        name="attn_bwd", compiler_params=_params(("arbitrary",)))(sinks, q, kv, kv, do)


def kv_combine(dprev, dcur):
    nb, _, kvw = dprev.shape

    def body(dcur_ref, dprev_ref, dkv_ref, db_ref):
        total = jnp.zeros((1, kvw), F32)
        for m in range(nb):
            dkv = dcur_ref[m] + dprev_ref[m + 1] if m + 1 < nb else dcur_ref[m]
            dkv_ref[m * BLOCK:(m + 1) * BLOCK, :] = dkv.astype(BF16)
            total = total + jnp.sum(dkv, axis=0, keepdims=True)
        db_ref[...] = jnp.concatenate([total, jnp.zeros((1, D_MODEL - kvw), F32)], axis=1)

    vm = pl.BlockSpec(memory_space=pltpu.VMEM)
    return pl.pallas_call(body, in_specs=[vm, vm], out_specs=[vm, vm],
                          out_shape=[_sds((nb * BLOCK, kvw), BF16), _sds((1, D_MODEL), F32)], name="kv_combine",
                          compiler_params=_params())(dcur, dprev)


def glu_bwd(dout, val, gate, tm=256):
    n_rows, d = dout.shape

    def body(do_ref, v_ref, g_ref, dz_ref, db_ref):
        i = pl.program_id(0)
        sg = jax.nn.sigmoid(g_ref[...])
        dval = do_ref[...] * sg
        dgate = do_ref[...] * v_ref[...] * sg * (1.0 - sg)
        dz_ref[...] = jnp.concatenate([dval, dgate], axis=1).astype(BF16)

        @pl.when(i == 0)
        def _():
            db_ref[...] = jnp.zeros_like(db_ref)

        db_ref[0:1, :] += jnp.sum(dval, axis=0, keepdims=True)
        db_ref[1:2, :] += jnp.sum(dgate, axis=0, keepdims=True)

    row = pl.BlockSpec((tm, d), lambda i: (i, 0))
    return pl.pallas_call(
        body, grid=(n_rows // tm,), in_specs=[row, row, row],
        out_specs=[pl.BlockSpec((tm, 2 * d), lambda i: (i, 0)), pl.BlockSpec((2, d), lambda i: (0, 0))],
        out_shape=[_sds((n_rows, 2 * d), BF16), _sds((2, d), F32)],
        name="glu_bwd", compiler_params=_params(("arbitrary",)))(dout, val, gate)


def _adam_update(w, g, m, v):
    nm = ADAM_B1 * m + (1.0 - ADAM_B1) * g
    nv = ADAM_B2 * v + (1.0 - ADAM_B2) * (g * g)
    m_hat = nm / (1.0 - ADAM_B1 ** ADAM_STEP)
    v_hat = nv / (1.0 - ADAM_B2 ** ADAM_STEP)
    return -ADAM_LR * (m_hat / (jnp.sqrt(v_hat) + ADAM_EPS) + ADAM_WD * w), nm, nv


def adamw(name, ws, gs, ms, vs, steps=8):
    n = len(ws)

    def body(*refs):
        for k in range(n):
            w_ref, g_ref, m_ref, v_ref = (refs[j * n + k] for j in range(4))
            go_ref, d_ref, nm_ref, nv_ref = (refs[(4 + j) * n + k] for j in range(4))
            gv = g_ref[...]
            go_ref[...] = gv
            d_ref[...], nm_ref[...], nv_ref[...] = _adam_update(w_ref[...], gv, m_ref[...], v_ref[...])

    specs = [pl.BlockSpec((w.shape[0] // steps, w.shape[1]), lambda i: (i, 0)) for w in ws]
    shapes = [_sds(w.shape, F32) for w in ws]
    out = pl.pallas_call(
        body, grid=(steps,), in_specs=specs * 4, out_specs=specs * 4, out_shape=shapes * 4, name=name,
        compiler_params=_params(("parallel",)))(*ws, *gs, *ms, *vs)
    return [list(out[j * n:(j + 1) * n]) for j in range(4)]


def adamw_native(name, ws, gs, ms, vs):
    n = len(ws)

    def body(*refs):
        w_refs, g_refs, m_refs, v_refs = refs[:n], refs[n:2 * n], refs[2 * n:3 * n], refs[3 * n:4 * n]
        d_refs, nm_refs, nv_refs = refs[4 * n:5 * n], refs[5 * n:6 * n], refs[6 * n:7 * n]
        for k in range(n):
            dl, nm, nv = _adam_update(w_refs[k][...], g_refs[k][...], m_refs[k][...], v_refs[k][...])
            d_refs[k][...] = dl
            nm_refs[k][...] = nm
            nv_refs[k][...] = nv

    vm = pl.BlockSpec(memory_space=pltpu.VMEM)
    shapes = [_sds(w.shape, F32) for w in ws]
    out = pl.pallas_call(body, in_specs=[vm] * (4 * n), out_specs=[vm] * (3 * n), out_shape=shapes * 3, name=name,
                         compiler_params=_params())(*ws, *gs, *ms, *vs)
    return list(out[:n]), list(out[n:2 * n]), list(out[2 * n:])


VEC_ROWS = {"norm_mix": 0, "norm_mlp": 2, "norm_kv": 4, "norm_final": 5, "s5_d": 6, "b_q": 7, "b_o": 8, "s5_b_glu": 9,
            "b_kv": 11, "sinks": 12, "loss": 13}


def split_vectors(where, vecs, d_shard, glu_shard):
    kvw = 2 * N_KV * HEAD_DIM
    shapes = {"norm_mix": (2, D_MODEL), "norm_mlp": (2, D_MODEL), "norm_kv": (1, D_MODEL), "norm_final": (1, D_MODEL),
              "s5_d": (1, d_shard), "b_q": (1, D_MODEL), "b_o": (1, D_MODEL), "s5_b_glu": (1, glu_shard), "b_kv": (1, kvw),
              "sinks": (1, N_Q), "loss": (1, 128)}
    names = list(shapes)

    def body(where_ref, v_ref, *o_refs):
        chip = where_ref[1]
        for name, o_ref in zip(names, o_refs):
            r0, (r, n) = VEC_ROWS[name], shapes[name]
            if name == "s5_d":
                g = jnp.zeros((1, n), F32)
                for j in range(4):
                    g = jnp.where(chip == j, v_ref[r0:r0 + 1, j * n:(j + 1) * n], g)
            elif name == "s5_b_glu":
                g = jnp.zeros((1, n), F32)
                for j in range(4):
                    row, col = r0 + (j * n) // D_MODEL, (j * n) % D_MODEL
                    g = jnp.where(chip == j, v_ref[row:row + 1, col:col + n], g)
            else:
                g = v_ref[r0:r0 + r, 0:n]
            o_ref[...] = g

    vm = pl.BlockSpec(memory_space=pltpu.VMEM)
    out = pl.pallas_call(body, in_specs=[pl.BlockSpec(memory_space=pltpu.SMEM), vm], out_specs=[vm] * len(names),
                         out_shape=[_sds(shapes[n], F32) for n in names], name="split_vectors",
                         compiler_params=_params())(where, vecs)
    return dict(zip(names, out))


def _position():
    x, y, c = lax.axis_index("x"), lax.axis_index("y"), lax.axis_index("c")
    others = [(1 - x, y), (x, 1 - y), (1 - x, 1 - y)]
    return x, y, c, others


def _window(ref, kind, chip, half, shard_shape):
    if kind == "slab":
        return ref.at[chip]
    r, n = shard_shape
    if kind == "col":
        return ref.at[pl.ds(pl.multiple_of(half * (r // 2), 16), r // 2), pl.ds(pl.multiple_of(chip * n, 128), n)]
    return ref.at[pl.ds(pl.multiple_of(chip * r, 16), r), pl.ds(pl.multiple_of(half * (n // 2), 128), n // 2)]


def _half(ref, kind, half, shape):
    r, n = shape
    if kind == "col":
        return ref.at[pl.ds(pl.multiple_of(half * (r // 2), 16), r // 2), :]
    return ref.at[:, pl.ds(pl.multiple_of(half * (n // 2), 128), n // 2)]


def swap_start(name, grads, kinds):
    nt = len(grads)
    shapes = [tuple(g.shape) for g in grads]
    lands = [lax.empty(sh, BF16) for sh in shapes]

    def body(*refs):
        in_refs, land_refs = refs[:nt], refs[nt:2 * nt]
        send_sems, recv_sems, token = refs[2 * nt], refs[2 * nt + 1], refs[-1]
        x, y, c, _ = _position()
        for t in range(nt):
            pltpu.make_async_remote_copy(
                src_ref=_half(in_refs[t], kinds[t], 1 - c, shapes[t]), dst_ref=_half(land_refs[t], kinds[t], 1 - c, shapes[t]),
                send_sem=send_sems.at[t], recv_sem=recv_sems.at[t], device_id=(x, y, 1 - c), device_id_type=MESH).start()
        token[...] = jnp.zeros_like(token)

    sems = pltpu.SemaphoreType.DMA((nt,))
    both = list(grads) + lands
    out = pl.pallas_call(
        body, name=name, in_specs=[HBM_SPEC] * (2 * nt),
        out_specs=(SEM_SPEC, SEM_SPEC, *[HBM_SPEC] * (2 * nt), pl.BlockSpec(memory_space=pltpu.VMEM)),
        out_shape=(sems, sems, *[pltpu.HBM(a.shape, a.dtype) for a in both], _sds((8, 128), F32)),
        input_output_aliases={t: 2 + t for t in range(2 * nt)}, compiler_params=_split_params(),
    )(*[_in_hbm(a) for a in both])
    return out[0], out[1], list(out[2:2 + nt]), list(out[2 + nt:2 + 2 * nt]), out[-1]


def swap_wait(name, send_sems, recv_sems, grads, lands, kinds, after):
    nt = len(grads)
    shapes = [tuple(g.shape) for g in grads]

    def body(*refs):
        in_refs, land_refs = refs[:nt], refs[nt:2 * nt]
        send_ref, recv_ref = refs[2 * nt], refs[2 * nt + 1]
        x, y, c, _ = _position()
        for t in range(nt):
            cp = pltpu.make_async_remote_copy(
                src_ref=_half(in_refs[t], kinds[t], 1 - c, shapes[t]), dst_ref=_half(land_refs[t], kinds[t], c, shapes[t]),
                send_sem=send_ref.at[t], recv_sem=recv_ref.at[t], device_id=(x, y, 1 - c), device_id_type=MESH)
            cp.wait_send()
            cp.wait_recv()

    both = list(grads) + list(lands)
    out = pl.pallas_call(
        body, name=name, in_specs=[HBM_SPEC] * (2 * nt) + [SEM_SPEC, SEM_SPEC, HBM_SPEC], out_specs=[HBM_SPEC] * (2 * nt),
        out_shape=[pltpu.HBM(a.shape, a.dtype) for a in both], input_output_aliases={t: t for t in range(2 * nt)},
        compiler_params=_split_params())(*both, send_sems, recv_sems, _in_hbm(after))
    return list(out[:nt]), list(out[nt:])


def _half_spec(kind, shape, tiles):
    r, n = shape
    if kind == "col":
        tn = n // tiles
        return pl.BlockSpec((r // 2, tn), lambda i, s: (s[0], i))
    tm = r // tiles
    return pl.BlockSpec((tm, n // 2), lambda i, s: (i, s[0]))


def add_halves(name, mine, landed, kinds, where, tiles=4):
    nt = len(mine)
    shapes = [tuple(a.shape) for a in mine]

    def compact(t):
        r, n = shapes[t]
        if kinds[t] == "col":
            return (r // 2, n), pl.BlockSpec((r // 2, n // tiles), lambda i, s: (0, i))
        return (r, n // 2), pl.BlockSpec((r // tiles, n // 2), lambda i, s: (i, 0))

    def body(s_ref, *refs):
        for a_ref, b_ref, o_ref in zip(refs[:nt], refs[nt:2 * nt], refs[2 * nt:]):
            o_ref[...] = (a_ref[...].astype(F32) + b_ref[...].astype(F32)).astype(BF16)

    specs = [_half_spec(kinds[t], shapes[t], tiles) for t in range(nt)]
    return pl.pallas_call(
        body, grid_spec=pltpu.PrefetchScalarGridSpec(num_scalar_prefetch=1, grid=(tiles,), in_specs=specs + specs,
                                                     out_specs=[compact(t)[1] for t in range(nt)]),
        out_shape=[_sds(compact(t)[0], BF16) for t in range(nt)], name=name,
        compiler_params=_params(("parallel",)))(where, *mine, *landed)


def sum_shards(name, parts, landed, kinds, shard_shapes, where, layers, n_layers, intos, tiles=2):
    nt = len(parts)
    in_specs, out_specs = [], []
    for t in range(nt):
        (r, n), layer = shard_shapes[t], layers[t]
        if kinds[t] == "col":
            tm, width = r // 2 // tiles, n
            own = pl.BlockSpec((tm, n), lambda i, s: (i, s[1]))
            out = pl.BlockSpec((None, tm, n), lambda i, s, layer=layer: (layer, s[0] * tiles + i, 0))
        else:
            tm, width = r // tiles, n // 2
            own = pl.BlockSpec((tm, n // 2), lambda i, s: (s[1] * tiles + i, 0))
            out = pl.BlockSpec((None, tm, n // 2), lambda i, s, layer=layer: (layer, i, s[0]))
        in_specs += [own, pl.BlockSpec((3, tm, width), lambda i, s: (0, i, 0))]
        out_specs.append(out)
    args, aliases = [where] + [a for pair in zip(parts, landed) for a in pair], {}
    for t in range(nt):
        if intos[t] is not None:
            aliases[len(args)] = t
            in_specs.append(pl.BlockSpec(memory_space=pl.ANY))
            args.append(intos[t])

    def body(s_ref, *refs):
        for t in range(nt):
            a_ref, l_ref, o_ref = refs[2 * t], refs[2 * t + 1], refs[len(in_specs) + t]
            o_ref[...] = ((a_ref[...].astype(F32) + l_ref[0].astype(F32)) + l_ref[1].astype(F32)) + l_ref[2].astype(F32)

    return pl.pallas_call(
        body, grid_spec=pltpu.PrefetchScalarGridSpec(num_scalar_prefetch=1, grid=(tiles,), in_specs=in_specs,
                                                     out_specs=out_specs),
        out_shape=[_sds((n_layers[t],) + tuple(shard_shapes[t]), F32) for t in range(nt)], input_output_aliases=aliases,
        name=name, compiler_params=_params(("parallel",)))(*args)


def share_start(arrays, entries):
    na, nt = len(arrays), len(entries)

    def body(*refs):
        in_refs, send_sems, recv_sems, token = refs[:na], refs[na], refs[na + 1], refs[-1]
        x, y, c, _ = _position()
        for t, (a, layer, kind) in enumerate(entries):
            mine = _half(in_refs[a].at[layer], kind, c, tuple(arrays[a].shape[1:]))
            pltpu.make_async_remote_copy(
                src_ref=mine, dst_ref=mine, send_sem=send_sems.at[t], recv_sem=recv_sems.at[t],
                device_id=(x, y, 1 - c), device_id_type=MESH).start()
        token[...] = jnp.zeros_like(token)

    sems = pltpu.SemaphoreType.DMA((nt,))
    out = pl.pallas_call(
        body, name="share_start", in_specs=[HBM_SPEC] * na,
        out_specs=(SEM_SPEC, SEM_SPEC, *[HBM_SPEC] * na, pl.BlockSpec(memory_space=pltpu.VMEM)),
        out_shape=(sems, sems, *[pltpu.HBM(a.shape, a.dtype) for a in arrays], _sds((8, 128), F32)),
        input_output_aliases={t: 2 + t for t in range(na)}, compiler_params=_split_params(),
    )(*[_in_hbm(a) for a in arrays])
    return out[0], out[1], list(out[2:2 + na]), out[-1]


def share_wait(send_sems, recv_sems, arrays, entries, after):
    na = len(arrays)

    def body(*refs):
        in_refs, send_ref, recv_ref = refs[:na], refs[na], refs[na + 1]
        x, y, c, _ = _position()
        for t, (a, layer, kind) in enumerate(entries):
            shape = tuple(arrays[a].shape[1:])
            cp = pltpu.make_async_remote_copy(
                src_ref=_half(in_refs[a].at[layer], kind, c, shape), dst_ref=_half(in_refs[a].at[layer], kind, 1 - c, shape),
                send_sem=send_ref.at[t], recv_sem=recv_ref.at[t], device_id=(x, y, 1 - c), device_id_type=MESH)
            cp.wait_send()
            cp.wait_recv()

    return list(pl.pallas_call(
        body, name="share_wait", in_specs=[HBM_SPEC] * na + [SEM_SPEC, SEM_SPEC, HBM_SPEC], out_specs=[HBM_SPEC] * na,
        out_shape=[pltpu.HBM(a.shape, a.dtype) for a in arrays], input_output_aliases={t: t for t in range(na)},
        compiler_params=_split_params())(*arrays, send_sems, recv_sems, _in_hbm(after)))


HBM_SPEC = pl.BlockSpec(memory_space=pltpu.HBM)
SEM_SPEC = pl.BlockSpec(memory_space=pltpu.SEMAPHORE)
ANY_SPEC = pl.BlockSpec(memory_space=pl.ANY)


def _split_params():
    return pltpu.CompilerParams(has_side_effects=pltpu.SideEffectType.DATAFLOW_SIDE_EFFECTING,
                                vmem_limit_bytes=VMEM_LIMIT_BYTES)


def _in_hbm(a):
    return pltpu.with_memory_space_constraint(a, pltpu.HBM)


def cast_place(arrays, entries, where, tiles=2):
    in_specs, out_specs, fulls = [], [], []
    for a, layer, kind in entries:
        _, r, n = arrays[a].shape
        tm = r // tiles
        in_specs.append(pl.BlockSpec((None, tm, n), lambda i, s, layer=layer: (layer, i, 0)))
        if kind == "col":
            fulls.append((r, 4 * n))
            out_specs.append(pl.BlockSpec((tm, n), lambda i, s: (i, s[1])))
        else:
            fulls.append((4 * r, n))
            out_specs.append(pl.BlockSpec((tm, n), lambda i, s: (s[1] * tiles + i, 0)))
    nt = len(entries)

    def body(s_ref, *refs):
        for w_ref, o_ref in zip(refs[:nt], refs[nt:]):
            o_ref[...] = w_ref[...].astype(BF16)

    return pl.pallas_call(
        body, grid_spec=pltpu.PrefetchScalarGridSpec(num_scalar_prefetch=1, grid=(tiles,), in_specs=in_specs,
                                                     out_specs=out_specs),
        out_shape=[_sds(f, BF16) for f in fulls], name="cast_place",
        compiler_params=_params(("parallel",)))(where, *[arrays[a] for a, _, _ in entries])


def gather_start(fulls, kinds, shard_shapes):
    nt = len(fulls)

    def body(*refs):
        full_refs = refs[:nt]
        send_sems, recv_sems, token = refs[nt], refs[nt + 1], refs[-1]
        x, y, c, others = _position()
        for t in range(nt):
            mine = _window(full_refs[t], kinds[t], 2 * x + y, c, shard_shapes[t])
            for j, (ox, oy) in enumerate(others):
                pltpu.make_async_remote_copy(
                    src_ref=mine, dst_ref=mine, send_sem=send_sems.at[3 * t + j], recv_sem=recv_sems.at[3 * t + j],
                    device_id=(ox, oy, c), device_id_type=MESH).start()
        token[...] = jnp.zeros_like(token)

    sems = pltpu.SemaphoreType.DMA((3 * nt,))
    out = pl.pallas_call(
        body, name="gather_start", in_specs=[HBM_SPEC] * nt,
        out_specs=(SEM_SPEC, SEM_SPEC, *[HBM_SPEC] * nt, pl.BlockSpec(memory_space=pltpu.VMEM)),
        out_shape=(sems, sems, *[pltpu.HBM(f.shape, f.dtype) for f in fulls], _sds((8, 128), F32)),
        input_output_aliases={t: 2 + t for t in range(nt)}, compiler_params=_split_params(),
    )(*[_in_hbm(f) for f in fulls])
    return out[0], out[1], list(out[2:2 + nt]), out[-1]


def gather_wait(name, send_sems, recv_sems, fulls, kinds, shard_shapes, after, first):
    nt = len(fulls)
    extra = [] if after is None else [_in_hbm(after)]

    def body(*refs):
        full_refs, send_ref, recv_ref = refs[:nt], refs[nt], refs[nt + 1]
        x, y, c, others = _position()
        for t in range(nt):
            mine = _window(full_refs[t], kinds[t], 2 * x + y, c, shard_shapes[t])
            for j, (ox, oy) in enumerate(others):
                cp = pltpu.make_async_remote_copy(
                    src_ref=mine, dst_ref=_window(full_refs[t], kinds[t], 2 * ox + oy, c, shard_shapes[t]),
                    send_sem=send_ref.at[3 * (first + t) + j], recv_sem=recv_ref.at[3 * (first + t) + j],
                    device_id=(ox, oy, c), device_id_type=MESH)
                cp.wait_send()
                cp.wait_recv()

    out = pl.pallas_call(
        body, name=name, in_specs=[HBM_SPEC] * nt + [SEM_SPEC, SEM_SPEC] + [HBM_SPEC] * len(extra),
        out_specs=[HBM_SPEC] * nt, out_shape=[pltpu.HBM(f.shape, f.dtype) for f in fulls],
        input_output_aliases={t: t for t in range(nt)}, compiler_params=_split_params())(*fulls, send_sems, recv_sems, *extra)
    return list(out)


def forward_halves(name, fulls, kinds, shard_shapes):
    nt = len(fulls)

    def body(*refs):
        out_refs = refs[nt:2 * nt]
        send_sems, recv_sems = refs[2 * nt:]
        x, y, c, others = _position()
        cps = []
        for t in range(nt):
            for j, (ox, oy) in enumerate(others):
                landed = _window(out_refs[t], kinds[t], 2 * ox + oy, c, shard_shapes[t])
                cp = pltpu.make_async_remote_copy(
                    src_ref=landed, dst_ref=landed, send_sem=send_sems.at[3 * t + j], recv_sem=recv_sems.at[3 * t + j],
                    device_id=(x, y, 1 - c), device_id_type=MESH)
                cp.start()
                cps.append(cp)
        for t in range(nt):
            for j, (ox, oy) in enumerate(others):
                got = _window(out_refs[t], kinds[t], 2 * ox + oy, 1 - c, shard_shapes[t])
                pltpu.make_async_remote_copy(
                    src_ref=got, dst_ref=got, send_sem=send_sems.at[3 * t + j], recv_sem=recv_sems.at[3 * t + j],
                    device_id=(x, y, 1 - c), device_id_type=MESH).wait_recv()
        for cp in cps:
            cp.wait_send()

    out = pl.pallas_call(
        body, in_specs=[ANY_SPEC] * nt, out_specs=[ANY_SPEC] * nt, out_shape=[_sds(f.shape, f.dtype) for f in fulls],
        input_output_aliases={t: t for t in range(nt)},
        scratch_shapes=[pltpu.SemaphoreType.DMA((3 * nt,)), pltpu.SemaphoreType.DMA((3 * nt,))],
        name=name, compiler_params=_params())(*fulls)
    return list(out)


def _piece(ref, kind, chip, shard_shape):
    r, n = shard_shape
    if kind == "col":
        return ref.at[:, pl.ds(pl.multiple_of(chip * n, 128), n)]
    return ref.at[pl.ds(pl.multiple_of(chip * r, 16), r), :]


def _piece_shape(kind, shard_shape):
    r, n = shard_shape
    return (r // 2, n) if kind == "col" else (r, n // 2)


def exchange_start(name, parts, kinds, shard_shapes):
    nt = len(parts)
    lands = [lax.empty((3,) + _piece_shape(kinds[t], shard_shapes[t]), BF16) for t in range(nt)]

    def body(*refs):
        part_refs, land_refs = refs[:nt], refs[nt:2 * nt]
        send_sems, recv_sems, token = refs[2 * nt], refs[2 * nt + 1], refs[-1]
        x, y, c, others = _position()
        for t in range(nt):
            for j, (ox, oy) in enumerate(others):
                pltpu.make_async_remote_copy(
                    src_ref=_piece(part_refs[t], kinds[t], 2 * ox + oy, shard_shapes[t]), dst_ref=land_refs[t].at[j],
                    send_sem=send_sems.at[3 * t + j], recv_sem=recv_sems.at[3 * t + j],
                    device_id=(ox, oy, c), device_id_type=MESH).start()
        token[...] = jnp.zeros_like(token)

    sems = pltpu.SemaphoreType.DMA((3 * nt,))
    both = list(parts) + lands
    out = pl.pallas_call(
        body, name=name, in_specs=[HBM_SPEC] * (2 * nt),
        out_specs=(SEM_SPEC, SEM_SPEC, *[HBM_SPEC] * (2 * nt), pl.BlockSpec(memory_space=pltpu.VMEM)),
        out_shape=(sems, sems, *[pltpu.HBM(a.shape, a.dtype) for a in both], _sds((8, 128), F32)),
        input_output_aliases={t: 2 + t for t in range(2 * nt)}, compiler_params=_split_params(),
    )(*[_in_hbm(a) for a in both])
    return out[0], out[1], list(out[2:2 + nt]), list(out[2 + nt:2 + 2 * nt]), out[-1]


def exchange_wait(name, send_sems, recv_sems, parts, lands, kinds, shard_shapes, after):
    nt = len(parts)

    def body(*refs):
        part_refs, land_refs = refs[:nt], refs[nt:2 * nt]
        send_ref, recv_ref = refs[2 * nt], refs[2 * nt + 1]
        x, y, c, others = _position()
        for t in range(nt):
            for j, (ox, oy) in enumerate(others):
                cp = pltpu.make_async_remote_copy(
                    src_ref=_piece(part_refs[t], kinds[t], 2 * ox + oy, shard_shapes[t]), dst_ref=land_refs[t].at[j],
                    send_sem=send_ref.at[3 * t + j], recv_sem=recv_ref.at[3 * t + j],
                    device_id=(ox, oy, c), device_id_type=MESH)
                cp.wait_send()
                cp.wait_recv()

    both = list(parts) + list(lands)
    out = pl.pallas_call(
        body, name=name, in_specs=[HBM_SPEC] * (2 * nt) + [SEM_SPEC, SEM_SPEC, HBM_SPEC], out_specs=[HBM_SPEC] * (2 * nt),
        out_shape=[pltpu.HBM(a.shape, a.dtype) for a in both], input_output_aliases={t: t for t in range(2 * nt)},
        compiler_params=_split_params())(*both, send_sems, recv_sems, _in_hbm(after))
    return list(out[:nt]), list(out[nt:])


def all_reduce_small(name, bufs, wire):
    n = len(bufs)
    halves = [b.shape[0] // 2 for b in bufs]

    def body(*refs):
        in_refs, out_refs, lands, txs = refs[:n], refs[n:2 * n], refs[2 * n:3 * n], refs[3 * n:4 * n]
        send_sems, recv_sems = refs[4 * n:]
        x, y, c, _ = _position()
        mine = [pl.ds(pl.multiple_of(c * h, 8), h) for h in halves]
        other = [pl.ds(pl.multiple_of((1 - c) * h, 8), h) for h in halves]
        for s, peer in enumerate([(x, y, 1 - c), (1 - x, y, c), (x, 1 - y, c)]):
            cps = []
            for k in range(n):
                txs[k][...] = (in_refs[k][other[k], :] if s == 0 else out_refs[k][mine[k], :]).astype(wire[k])
                cp = pltpu.make_async_remote_copy(
                    src_ref=txs[k], dst_ref=lands[k].at[s], send_sem=send_sems.at[4 * k + s], recv_sem=recv_sems.at[4 * k + s],
                    device_id=peer, device_id_type=MESH)
                cp.start()
                cps.append(cp)
            for k, cp in enumerate(cps):
                cp.wait()
                own = in_refs[k][mine[k], :] if s == 0 else out_refs[k][mine[k], :]
                out_refs[k][mine[k], :] = own.astype(wire[k]).astype(F32) + lands[k][s].astype(F32)
        cps = []
        for k in range(n):
            cp = pltpu.make_async_remote_copy(
                src_ref=out_refs[k].at[mine[k]], dst_ref=out_refs[k].at[mine[k]], send_sem=send_sems.at[4 * k + 3],
                recv_sem=recv_sems.at[4 * k + 3], device_id=(x, y, 1 - c), device_id_type=MESH)
            cp.start()
            cps.append(cp)
        for cp in cps:
            cp.wait()

    vm = pl.BlockSpec(memory_space=pltpu.VMEM)
    out = pl.pallas_call(
        body, in_specs=[vm] * n, out_specs=[vm] * n, out_shape=[_sds(b.shape, F32) for b in bufs],
        scratch_shapes=[pltpu.VMEM((3, h, b.shape[1]), w) for h, b, w in zip(halves, bufs, wire)]
        + [pltpu.VMEM((h, b.shape[1]), w) for h, b, w in zip(halves, bufs, wire)]
        + [pltpu.SemaphoreType.DMA((4 * n,)), pltpu.SemaphoreType.DMA((4 * n,))],
        name=name, compiler_params=_params())(*bufs)
    return list(out)


def _local_step(x, target, small, need, emit_swap, emit_exchange):
    d = D_MODEL
    full = {}

    def after_token(vec, token):
        return vec if token is None else vec + token[0:1, 0:1]

    def token_rows(token, width):
        return [] if token is None else [after_token(jnp.zeros((1, width), F32), token)]

    def plus(acc, rows):
        return acc + rows[0] if rows else acc

    rb16, rbt16, rc16, rct16, lr_t, li_t = small["s5_operands"]
    ge, y2, cs = s5_fwd(x, small["norm_mix0"], small["s5_d"], rb16, rc16, lr_t, li_t)
    full.update(need("glu", ge))

    def norm_rows(h, gains):
        xh, _ = _rms_hat(h)
        return [xh * g for g in gains]

    def glu_epilogue(accs, e, r):
        v, gt = accs[0] + r[0], accs[1] + r[1]
        h = e[0] + v * jax.nn.sigmoid(gt)
        return [h, v, gt] + norm_rows(h, r[2:])

    h1, val, gate, n1 = mm_nn(
        "glu", ge, full["w_glu"], [0, d], d, glu_epilogue, [F32, F32, F32, BF16], extras=[x],
        rowvecs=[(small["s5_b_glu"], 0), (small["s5_b_glu"], d), (small["norm_mlp0"], 0)], tm=512, tn=d)

    def mlp_fwd(tag, h, n, w_in, get_w_out, next_gains, head=None):
        def in_epilogue(accs, e, rv):
            pos = jnp.maximum(accs[0], 0.0)
            return [pos * pos, 2.0 * pos]

        r, slope = mm_nn("mlp_in" + tag, n, w_in, [0], w_in.shape[1], in_epilogue, [BF16, BF16], tm=2048)
        w_out = get_w_out(r)

        def epilogue(accs, e, rv):
            h_out = e[0] + accs[0]
            return [h_out] + norm_rows(h_out, rv)

        if head is not None:
            return head(r, w_out, h), (n, r, slope)
        outs = mm_nn("mlp_out" + tag, r, w_out, [0], d, epilogue, [F32] + [BF16] * len(next_gains), extras=[h],
                     rowvecs=[(g, 0) for g in next_gains], tm=512, tn=d)
        return outs[0], outs[1:], (n, r, slope)

    full.update(need("mlp_in0", h1))

    def w_out0(after):
        full.update(need("mlp_out0", after))
        return full["w_out0"]

    h2, (nkv, n2), mlp0 = mlp_fwd("0", h1, n1, full["w_in0"], w_out0, [small["norm_kv"], small["norm_mix1"]])

    full.update(need("attn", h2))
    kvw = 2 * N_KV * HEAD_DIM
    (kv,) = mm_nn("kv_proj", nkv, full["w_kv"], [0], kvw, lambda accs, e, r: [accs[0] + r[0]], [BF16],
                  rowvecs=[(small["b_kv"], 0)], tm=2048)
    (q,) = mm_nn("q_proj", n2, full["w_q"], [0], d, lambda accs, e, r: [accs[0] + r[0]], [BF16],
                 rowvecs=[(small["b_q"], 0)], tm=2048)
    sinks = small["sinks"].reshape(N_Q)
    o = attn_fwd(q, kv, sinks)
    def o_epilogue(accs, e, r):
        h_out = e[0] + accs[0] + r[0]
        return [h_out] + norm_rows(h_out, r[1:])

    h3, n3 = mm_nn("o_proj", o, full["w_o"], [0], d, o_epilogue, [F32, BF16], extras=[h2],
                   rowvecs=[(small["b_o"], 0), (small["norm_mlp1"], 0)], tm=512, tn=d)
    full.update(need("mlp_in1", h3))

    def w_out1(after):
        full.update(need("mlp_out1", after))
        return full["w_out1"]

    def loss_head(r, w_out, h):
        def epilogue(accs, e, rv):
            xh, rr = _rms_hat(e[0] + accs[0])
            err = xh * rv[0] - e[1]
            dy = err * (1.0 / d)
            dxh = dy * rv[0]
            dx = rr * (dxh - xh * jnp.mean(dxh * xh, axis=-1, keepdims=True))
            loss = jnp.full((1, d), 0.5 * jnp.sum(jnp.mean(err * err, axis=-1, keepdims=True)), F32)
            return [dx, dx, loss, jnp.sum(dy * xh, axis=0, keepdims=True)]

        return mm_nn("mlp_out1", r, w_out, [0], d, epilogue, [F32, BF16], extras=[h, target],
                     rowvecs=[(small["norm_final"], 0)], n_sums=2, tm=512, tn=d)

    (dh, dhb, loss_tile, dg_final), mlp1 = mlp_fwd("1", h3, n3, full["w_in1"], w_out1, [], head=loss_head)

    grads_small, grads_full = {"norm_final": dg_final}, {}
    ident = lambda acc, e, r: [plus(acc, r)]
    layer1 = ["w_out1", "w_in1", "w_o", "w_q", "w_kv"]
    layer0 = ["w_out0", "w_in0", "w_glu"]

    def norm_bwd_rows(x_rows, res, dys, gains):
        xh, r = _rms_hat(x_rows)
        dxh = sum(dy * g for dy, g in zip(dys, gains))
        dx = r * (dxh - xh * jnp.mean(dxh * xh, axis=-1, keepdims=True)) + res
        return dx, [jnp.sum(dy * xh, axis=0, keepdims=True) for dy in dys]

    def mlp_bwd(tag, dh, dhb, h_in, gain, w_in, w_out, saved, token=None):
        n, r, slope = saved
        grads_full["w_out" + tag] = mm_tn("dw_out" + tag, r, dhb, tn=1024)
        (da,) = mm_nt("mlp_da" + tag, dhb, w_out, lambda acc, e, rv: [plus(acc * e[0].astype(F32), rv)], [BF16],
                      extras=[slope], rowvecs=token_rows(token, w_out.shape[0]), tm=2048)
        grads_full["w_in" + tag] = mm_tn("dw_in" + tag, n, da, tn=1024)

        def epilogue(acc, e, rv):
            dx, dgs = norm_bwd_rows(e[0], e[1], [acc], rv)
            return [dx, dx, jnp.sum(dx, axis=0, keepdims=True)] + dgs

        dx, dxb, colsum, dg = mm_nt("mlp_dn" + tag, da, w_in, epilogue, [F32, BF16], extras=[h_in, dh], rowvecs=[gain],
                                    n_sums=2, tm=512, tk=d)
        grads_small["norm_mlp" + tag] = dg
        return dx, dxb, colsum

    dh3, dh3b, colsum3 = mlp_bwd("1", dh, dhb, h3, small["norm_mlp1"], full["w_in1"], full["w_out1"], mlp1)
    grads_small["b_o"] = colsum3
    grads_full["w_o"] = mm_tn("dw_o", o, dh3b, tn=1024)
    (do,) = mm_nt("attn_do", dh3b, full["w_o"], ident, [BF16], tm=2048)
    dq, dbq, dprev, dcur, dsink = attn_bwd(q, kv, do, sinks)
    dkv, dbkv = kv_combine(dprev, dcur)
    grads_small["b_q"], grads_small["b_kv"], grads_small["sinks"] = dbq, dbkv, dsink
    grads_full["w_q"] = mm_tn("dw_q", n2, dq, tn=1024)
    grads_full["w_kv"] = mm_tn("dw_kv", nkv, dkv, tk=1024)
    token = emit_swap("layer1", {n: grads_full[n] for n in layer1})
    (dnkv,) = mm_nt("kv_dn", dkv, full["w_kv"], ident, [F32], rowvecs=token_rows(token, d), tm=2048, tk=1024)

    def attn_dn_epilogue(acc, e, rv):
        dx, dgs = norm_bwd_rows(e[0], e[1], [acc, e[2]], rv)
        return [dx, dx] + dgs

    dh2, dh2b, dg_mix1, dg_kv = mm_nt("attn_dn", dq, full["w_q"], attn_dn_epilogue, [F32, BF16], extras=[h2, dh3, dnkv],
                                      rowvecs=[small["norm_mix1"], small["norm_kv"]], n_sums=2, tm=512, tk=d)
    grads_small["norm_mix1"], grads_small["norm_kv"] = dg_mix1, dg_kv
    token = emit_exchange("layer1", dh2b)
    dh1, _, _ = mlp_bwd("0", dh2, dh2b, h1, small["norm_mlp0"], full["w_in0"], full["w_out0"], mlp0, token)

    dz, db_glu = glu_bwd(dh1, val, gate)
    grads_small["s5_b_glu"] = db_glu
    grads_full["w_glu"] = mm_tn("dw_glu", ge, dz, tn=1024)
    token = emit_swap("layer0", {n: grads_full[n] for n in layer0})
    (dy2,) = mm_nt("glu_dy", dz, full["w_glu"], lambda acc, e, rv: [plus(acc, rv) * _gelu_grad(e[0])], [F32], extras=[y2],
                   rowvecs=token_rows(token, d), tm=1024, tk=1024)
    token = emit_exchange("layer0", dy2)
    grad_x, dd, drb, drc, dlr, dli, dg_mix0 = s5_bwd(x, small["norm_mix0"], dy2, dh1, after_token(small["s5_d"], token), cs,
                                                     rb16, rbt16, rct16, lr_t, li_t)
    grads_small["s5_d"] = dd
    grads_small["s5_mats"] = (drb, drc, dlr, dli)
    grads_small["norm_mix0"] = dg_mix0
    return loss_tile, grad_x, grads_small


SMALL_NAMES = ["norm_mix", "norm_mlp", "norm_kv", "norm_final", "s5_a_re", "s5_a_im", "s5_log_dt", "s5_b_re", "s5_b_im",
               "s5_c_re", "s5_c_im", "s5_d", "s5_b_glu", "b_kv", "b_q", "sinks", "b_o"]
BIG_NAMES = ["s5_w_glu", "w_kv", "w_q", "w_o", "w_mlp_in", "w_mlp_out"]
WEIGHT_ORDER = ["norm_mix", "norm_mlp", "norm_kv", "norm_final", "s5_a_re", "s5_a_im", "s5_log_dt", "s5_b_re", "s5_b_im",
                "s5_c_re", "s5_c_im", "s5_d", "s5_w_glu", "s5_b_glu", "w_kv", "b_kv", "w_q", "b_q", "sinks", "w_o", "b_o",
                "w_mlp_in", "w_mlp_out"]


def kernel(x, norm_mix, norm_mlp, norm_kv, norm_final, s5_a_re, s5_a_im, s5_log_dt, s5_b_re, s5_b_im, s5_c_re, s5_c_im, s5_d, s5_w_glu, s5_b_glu, w_kv, b_kv, w_q, b_q, sinks, w_o, b_o, w_mlp_in, w_mlp_out, loss_target, m_norm_mix, m_norm_mlp, m_norm_kv, m_norm_final, m_s5_a_re, m_s5_a_im, m_s5_log_dt, m_s5_b_re, m_s5_b_im, m_s5_c_re, m_s5_c_im, m_s5_d, m_s5_w_glu, m_s5_b_glu, m_w_kv, m_b_kv, m_w_q, m_b_q, m_sinks, m_w_o, m_b_o, m_w_mlp_in, m_w_mlp_out, v_norm_mix, v_norm_mlp, v_norm_kv, v_norm_final, v_s5_a_re, v_s5_a_im, v_s5_log_dt, v_s5_b_re, v_s5_b_im, v_s5_c_re, v_s5_c_im, v_s5_d, v_s5_w_glu, v_s5_b_glu, v_w_kv, v_b_kv, v_w_q, v_b_q, v_sinks, v_w_o, v_b_o, v_w_mlp_in, v_w_mlp_out):
    env = dict(locals())
    w = {n: env[n] for n in WEIGHT_ORDER}
    mom = {n: env["m_" + n] for n in WEIGHT_ORDER}
    var = {n: env["v_" + n] for n in WEIGHT_ORDER}
    d = D_MODEL
    xi, yi, ci = lax.axis_index("x"), lax.axis_index("y"), lax.axis_index("c")
    chip = 2 * xi + yi
    where = jnp.stack([ci, chip]).astype(jnp.int32)

    dsh, bsh = s5_d.shape[1], s5_b_glu.shape[1]
    packed = jnp.concatenate([s5_d.reshape(-1, 128), s5_b_glu.reshape(-1, 128)])
    n_d, n_b = dsh // 128, bsh // 128
    slab = lax.dynamic_update_slice(jnp.zeros((4, 8, 128), F32), jnp.pad(packed, ((0, 8 - n_d - n_b), (0, 0)))[None],
                                    (chip, 0, 0))

    big = [s5_w_glu, w_kv[None], w_q, w_o, w_mlp_in, w_mlp_out]
    entries = [(0, 0, "col"), (1, 0, "row"), (2, 0, "row"), (3, 0, "row"), (4, 0, "col"), (4, 1, "col"),
               (5, 0, "row"), (5, 1, "row")]
    names = ["w_glu", "w_kv", "w_q", "w_o", "w_in0", "w_in1", "w_out0", "w_out1"]
    kinds = dict(zip(names, [k for _, _, k in entries]))
    shard_shapes = dict(zip(names, [tuple(big[a].shape[1:]) for a, _, _ in entries]))

    placed_w = dict(zip(names, cast_place(big, entries, where)))
    placed_w["vectors"], kinds["vectors"], shard_shapes["vectors"] = slab, "slab", None
    gather_groups = {"glu": ["w_glu"], "mlp_in0": ["w_in0"], "mlp_out0": ["w_out0"], "attn": ["w_kv", "w_q", "w_o"],
                     "mlp_in1": ["w_in1"], "mlp_out1": ["w_out1"]}
    order = ["vectors"] + [n for members in gather_groups.values() for n in members]
    send, recv, thru, token = gather_start([placed_w[n] for n in order], [kinds[n] for n in order],
                                           [shard_shapes[n] for n in order])
    started = dict(zip(order, thru))
    (gathered_rows,) = gather_wait("gather_wait_vectors", send, recv, [started["vectors"]], ["slab"], [None], None, 0)
    d_full = gathered_rows[:, 0:n_d].reshape(1, -1)
    bglu_full = gathered_rows[:, n_d:n_d + n_b].reshape(1, -1)

    def need(group, after):
        members = gather_groups[group]
        ks, shapes = [kinds[n] for n in members], [shard_shapes[n] for n in members]
        landed = gather_wait("gather_wait_" + group, send, recv, [started[n] for n in members], ks, shapes, after,
                             order.index(members[0]))
        return dict(zip(members, forward_halves("forward_halves_" + group, landed, ks, shapes)))

    swapping, exchanging = {}, {}

    def emit_swap(group, partial):
        members = list(partial)
        send, recv, mine, lands, tok = swap_start("swap_start_" + group, [partial[n] for n in members],
                                                  [kinds[n] for n in members])
        swapping[group] = (members, send, recv, mine, lands)
        return tok

    def emit_exchange(group, after):
        members, send, recv, mine, lands = swapping[group]
        ks, shapes = [kinds[n] for n in members], [shard_shapes[n] for n in members]
        mine, landed = swap_wait("swap_wait_" + group, send, recv, mine, lands, ks, after)
        sums = add_halves("add_halves_" + group, mine, landed, ks, where)
        send, recv, parts, lands, tok = exchange_start("exchange_start_" + group, sums, ks, shapes)
        exchanging[group] = (members, send, recv, parts, lands)
        return tok

    s5_args = (s5_a_re[0], s5_a_im[0], s5_log_dt[0], s5_b_re[0], s5_b_im[0])
    small = {
        "norm_mix0": norm_mix[0:1] + token[0:1, 0:1], "norm_mix1": norm_mix[1:2], "norm_mlp0": norm_mlp[0:1], "norm_mlp1": norm_mlp[1:2],
        "norm_kv": norm_kv.reshape(1, d), "norm_final": norm_final.reshape(1, d), "s5_operands": s5_prep(*s5_args, s5_c_re[0], s5_c_im[0]),
        "s5_d": d_full, "s5_b_glu": bglu_full,
        "b_kv": b_kv.reshape(1, -1), "b_q": b_q, "sinks": sinks, "b_o": b_o,
    }
    loss_row, grad_x, gs = _local_step(x[0], loss_target[0], small, need, emit_swap, emit_exchange)

    reduced = [None] * len(big)
    where_of = dict(zip(names, entries))
    for group in ("layer1", "layer0"):
        members, send, recv, parts, lands = exchanging[group]
        ks, shapes = [kinds[n] for n in members], [shard_shapes[n] for n in members]
        parts, lands = exchange_wait("exchange_wait_" + group, send, recv, parts, lands, ks, shapes, grad_x)
        targets = [where_of[n][0] for n in members]
        sums = sum_shards("sum_shards_" + group, parts, lands, ks, shapes, where, [where_of[n][1] for n in members],
                          [big[a].shape[0] for a in targets], [reduced[a] for a in targets])
        for a, arr in zip(targets, sums):
            reduced[a] = arr
    share_send, share_recv, reduced, shared = share_start(reduced, entries)

    mats, lams = s5_compact(*gs["s5_mats"])
    rows = [gs["norm_mix0"], gs["norm_mix1"], gs["norm_mlp0"], gs["norm_mlp1"], gs["norm_kv"], gs["norm_final"], gs["s5_d"],
            gs["b_q"], gs["b_o"], gs["s5_b_glu"], gs["b_kv"], gs["sinks"], loss_row, jnp.zeros((2, d), F32) + shared[0:1, 0:1]]
    vecs, lams, mats = all_reduce_small("reduce_small", [jnp.concatenate(rows, axis=0), lams, mats], [F32, F32, BF16])
    grads = split_vectors(where, vecs, dsh, bsh)
    loss = grads.pop("loss")[0, 0]
    g_are, g_aim, g_dt, g_bre, g_bim, dc_re, dc_im = s5_param_bwd(mats, lams, *s5_args)
    grads.update({"s5_a_re": g_are[None], "s5_a_im": g_aim[None], "s5_log_dt": g_dt[None], "s5_b_re": g_bre[None],
                  "s5_b_im": g_bim[None], "s5_c_re": dc_re[None], "s5_c_im": dc_im[None]})

    delta, new_m, new_v = {}, {}, {}

    def view(n, a):
        return a.reshape(1, -1) if a.ndim == 1 else jnp.swapaxes(a, -1, -2) if n in ("s5_b_re", "s5_b_im") else a

    sw, sg, sm, sv = ([view(n, t[n]) for n in SMALL_NAMES] for t in (w, grads, mom, var))
    for n, a, b, c_ in zip(SMALL_NAMES, *adamw_native("adamw_small", sw, sg, sm, sv)):
        delta[n], new_m[n], new_v[n] = (view(n, t) if t.ndim == 4 else t for t in (a, b, c_))

    reduced = share_wait(share_send, share_recv, reduced, entries, new_v["s5_c_re"])
    for n, g in zip(BIG_NAMES, reduced):
        grads[n] = g.reshape(w[n].shape)
    flat = lambda t: [t[n].reshape(-1, t[n].shape[-1]) for n in BIG_NAMES]
    for table, arrays in zip((grads, delta, new_m, new_v), adamw("adamw_big", flat(w), flat(grads), flat(mom), flat(var))):
        for n, a in zip(BIG_NAMES, arrays):
            table[n] = a.reshape(w[n].shape)

    out = [loss.reshape(()), grad_x[None]]
    for table in (grads, delta, new_m, new_v):
        out += [table[n].reshape(w[n].shape) for n in WEIGHT_ORDER]
    return tuple(out)
```

```python
import math

import jax
import jax.numpy as jnp
from jax import lax
from jax.experimental import pallas as pl
from jax.experimental.pallas import tpu as pltpu

F32 = jnp.float32
BF16 = jnp.bfloat16

D_MODEL = 1024
S5_GROUPS = 64
S5_GROUP = 16
S5_STATE = 64
N_KV = 4
N_Q = 16
HEAD_DIM = 64
BLOCK = 128
NORM_EPS = 1e-5
LAMBDA_RE_MAX = -1e-4
ADAM_LR, ADAM_B1, ADAM_B2, ADAM_EPS, ADAM_WD, ADAM_STEP = 0.001, 0.9, 0.999, 1e-08, 0.01, 10

VMEM_LIMIT_BYTES = 56 * 1024 * 1024
S5_CHUNK = 256
S5_BLOCKS = 4
MESH = pl.DeviceIdType.MESH


def _params(sem=None):
    return pltpu.CompilerParams(dimension_semantics=sem, vmem_limit_bytes=VMEM_LIMIT_BYTES)


def _sds(shape, dtype):
    return jax.ShapeDtypeStruct(shape, dtype)


def _rms_hat(xv):
    r = lax.rsqrt(jnp.mean(xv * xv, axis=-1, keepdims=True) + NORM_EPS)
    return xv * r, r


def mm_nn(name, a, w, col_offsets, n_out, epilogue, out_dtypes, extras=(), rowvecs=(), n_sums=0, tm=1024, tn=512):
    m, k = a.shape
    tm, tn = min(tm, m), min(tn, n_out)
    nw, ne, nr, no = len(col_offsets), len(extras), len(rowvecs), len(out_dtypes)

    def body(a_ref, *refs):
        w_refs, e_refs, r_refs = refs[:nw], refs[nw:nw + ne], refs[nw + ne:nw + ne + nr]
        o_refs, s_refs = refs[nw + ne + nr:nw + ne + nr + no], refs[nw + ne + nr + no:]
        av = a_ref[...]
        accs = [jnp.dot(av, w_ref[...], preferred_element_type=F32) for w_ref in w_refs]
        outs = epilogue(accs, [e[...] for e in e_refs], [r[...] for r in r_refs])
        for o_ref, o in zip(o_refs, outs[:no]):
            o_ref[...] = o.astype(o_ref.dtype)
        if n_sums:
            @pl.when(pl.program_id(1) == 0)
            def _():
                for s_ref in s_refs:
                    s_ref[...] = jnp.zeros_like(s_ref)

            for s_ref, val in zip(s_refs, outs[no:]):
                s_ref[...] += val

    def wspec(off):
        return pl.BlockSpec((k, tn), lambda j, i, off=off: (0, off // tn + j))

    def rspec(off):
        return pl.BlockSpec((1, tn), lambda j, i, off=off: (0, off // tn + j))

    tile = pl.BlockSpec((tm, tn), lambda j, i: (i, j))
    in_specs = ([pl.BlockSpec((tm, k), lambda j, i: (i, 0))] + [wspec(o) for o in col_offsets]
                + [tile] * ne + [rspec(o) for _, o in rowvecs])
    sem = ("parallel", "arbitrary") if n_sums else ("parallel", "parallel")
    return pl.pallas_call(
        body, grid=(n_out // tn, m // tm), in_specs=in_specs,
        out_specs=[tile] * no + [pl.BlockSpec((1, tn), lambda j, i: (0, j))] * n_sums,
        out_shape=[_sds((m, n_out), dt) for dt in out_dtypes] + [_sds((1, n_out), F32)] * n_sums, name=name,
        compiler_params=_params(sem))(a, *([w] * nw), *extras, *[r for r, _ in rowvecs])


def mm_nt(name, g, w, epilogue, out_dtypes, extras=(), rowvecs=(), n_sums=0, tm=512, tk=512):
    m, n = g.shape
    k = w.shape[0]
    tm, tk = min(tm, m), min(tk, k)
    ne, nr, no = len(extras), len(rowvecs), len(out_dtypes)

    def body(g_ref, w_ref, *refs):
        e_refs, r_refs, o_refs, s_refs = refs[:ne], refs[ne:ne + nr], refs[ne + nr:ne + nr + no], refs[ne + nr + no:]
        acc = lax.dot_general(g_ref[...], w_ref[...], (((1,), (1,)), ((), ())), preferred_element_type=F32)
        outs = epilogue(acc, [e[...] for e in e_refs], [r[...] for r in r_refs])
        for o_ref, o in zip(o_refs, outs[:no]):
            o_ref[...] = o.astype(o_ref.dtype)
        if n_sums:
            @pl.when(pl.program_id(0) == 0)
            def _():
                for s_ref in s_refs:
                    s_ref[...] = jnp.zeros_like(s_ref)

            for s_ref, val in zip(s_refs, outs[no:]):
                s_ref[...] += val

    tile = pl.BlockSpec((tm, tk), lambda i, j: (i, j))
    vec = pl.BlockSpec((1, tk), lambda i, j: (0, j))
    sem = ("arbitrary", "parallel") if n_sums else ("parallel", "parallel")
    return pl.pallas_call(
        body, grid=(m // tm, k // tk),
        in_specs=[pl.BlockSpec((tm, n), lambda i, j: (i, 0)), pl.BlockSpec((tk, n), lambda i, j: (j, 0))]
        + [tile] * ne + [vec] * nr,
        out_specs=[tile] * no + [vec] * n_sums,
        out_shape=[_sds((m, k), dt) for dt in out_dtypes] + [_sds((1, k), F32)] * n_sums, name=name,
        compiler_params=_params(sem))(g, w, *extras, *rowvecs)


def mm_tn(name, a, g, tk=512, tn=512):
    m, k = a.shape
    n = g.shape[1]
    tk, tn = min(tk, k), min(tn, n)

    def body(a_ref, g_ref, o_ref):
        acc = lax.dot_general(a_ref[...], g_ref[...], (((0,), (0,)), ((), ())), preferred_element_type=F32)
        o_ref[...] = acc.astype(o_ref.dtype)

    return pl.pallas_call(
        body, grid=(k // tk, n // tn),
        in_specs=[pl.BlockSpec((m, tk), lambda i, j: (0, i)), pl.BlockSpec((m, tn), lambda i, j: (0, j))],
        out_specs=pl.BlockSpec((tk, tn), lambda i, j: (i, j)), out_shape=_sds((k, n), BF16), name=name,
        compiler_params=_params(("parallel", "parallel")))(a, g)


def _row_mask(tc):
    row = lax.broadcasted_iota(jnp.int32, (8 * tc, 256), 0) % 8
    col = lax.broadcasted_iota(jnp.int32, (8 * tc, 256), 1) // 32
    return row == col


def _expand_rows(val, mask):
    tc, width = val.shape
    rep = jnp.broadcast_to(val[:, None, :], (tc, 8, width)).reshape(8 * tc, width)
    return jnp.where(mask, rep, 0.0).astype(BF16)


def _stage(ref, val):
    ref[0] = val[:, 0:128]
    ref[1] = val[:, 128:256]


def _gather_rows(src_ref, tc):
    halves = []
    for half in range(2):
        col = lax.broadcasted_iota(jnp.int32, (tc, 128), 1) // 32 + 4 * half
        out = jnp.zeros((tc, 128), F32)
        for s8 in range(4 * half, 4 * half + 4):
            out = jnp.where(col == s8, src_ref.at[half][pl.ds(s8, tc, stride=8), :], out)
        halves.append(out)
    return jnp.concatenate(halves, axis=1)


def _gelu(x):
    c = math.sqrt(2.0 / math.pi)
    return 0.5 * x * (1.0 + jnp.tanh(c * (x + 0.044715 * x * x * x)))


def _gelu_grad(x):
    c = math.sqrt(2.0 / math.pi)
    t = jnp.tanh(c * (x + 0.044715 * x * x * x))
    return 0.5 * (1.0 + t) + 0.5 * x * (1.0 - t * t) * c * (1.0 + 3.0 * 0.044715 * x * x)


def s5_fwd(x, gain, d_skip, rb, rc, lam_r, lam_i):
    n_rows = x.shape[0]
    tc = min(S5_CHUNK, n_rows)
    nc = n_rows // tc

    def body(x_ref, g_ref, d_ref, rb_ref, rc_ref, lr_ref, li_ref, ge_ref, y2_ref, cs_ref, bux, yrows, carry):
        i = pl.program_id(0)
        u = _rms_hat(x_ref[...])[0] * g_ref[...]

        @pl.when(i == 0)
        def _():
            carry[...] = jnp.zeros_like(carry)

        cs_ref[0] = carry[...]
        mask = _row_mask(tc)
        for blk in range(S5_BLOCKS):
            lhs = _expand_rows(u[:, blk * 256:(blk + 1) * 256], mask)
            bux[blk] = jnp.dot(lhs, rb_ref[blk], preferred_element_type=F32)
        lam = [(lr_ref[blk], li_ref[blk]) for blk in range(S5_BLOCKS)]

        def step(t, c):
            r0 = pl.multiple_of(t * 8, 8)
            new = []
            for blk in range(S5_BLOCKS):
                xr, xi = c[2 * blk], c[2 * blk + 1]
                lr, li = lam[blk]
                nr = lr * xr - li * xi + bux[blk, pl.ds(r0, 8), 0:128]
                ni = lr * xi + li * xr + bux[blk, pl.ds(r0, 8), 128:256]
                bux[blk, pl.ds(r0, 8), 0:128] = nr
                bux[blk, pl.ds(r0, 8), 128:256] = ni
                new += [nr, ni]
            return tuple(new)

        c0 = []
        for blk in range(S5_BLOCKS):
            c0 += [carry[blk, :, 0:128], carry[blk, :, 128:256]]
        cn = lax.fori_loop(0, tc, step, tuple(c0), unroll=4)
        for blk in range(S5_BLOCKS):
            carry[blk, :, 0:128] = cn[2 * blk]
            carry[blk, :, 128:256] = cn[2 * blk + 1]
        for blk in range(S5_BLOCKS):
            _stage(yrows, jnp.dot(bux[blk].astype(BF16), rc_ref[blk], preferred_element_type=F32))
            sl = slice(blk * 256, (blk + 1) * 256)
            y2 = _gather_rows(yrows, tc) + d_ref[:, sl] * u[:, sl]
            y2_ref[:, sl] = y2
            ge_ref[:, sl] = _gelu(y2).astype(BF16)

    row = pl.BlockSpec((tc, D_MODEL), lambda i: (i, 0))
    vec = pl.BlockSpec((1, D_MODEL), lambda i: (0, 0))
    mat = pl.BlockSpec((S5_BLOCKS, 256, 256), lambda i: (0, 0, 0))
    lamspec = pl.BlockSpec((S5_BLOCKS, 8, 128), lambda i: (0, 0, 0))
    return pl.pallas_call(
        body, grid=(nc,),
        in_specs=[row, vec, vec, mat, mat, lamspec, lamspec],
        out_specs=[row, row, pl.BlockSpec((1, S5_BLOCKS, 8, 256), lambda i: (i, 0, 0, 0))],
        out_shape=[_sds((n_rows, D_MODEL), BF16), _sds((n_rows, D_MODEL), F32), _sds((nc, S5_BLOCKS, 8, 256), F32)],
        scratch_shapes=[pltpu.VMEM((S5_BLOCKS, 8 * tc, 256), F32), pltpu.VMEM((2, 8 * tc, 128), F32),
                        pltpu.VMEM((S5_BLOCKS, 8, 256), F32)],
        name="s5_fwd", compiler_params=_params(("arbitrary",)))(x, gain, d_skip, rb, rc, lam_r, lam_i)


def s5_bwd(x, gain, dy2, res, d_skip, cs, rb, rbt, rct, lam_r, lam_i):
    n_rows = x.shape[0]
    tc = min(S5_CHUNK, n_rows)
    nc = n_rows // tc

    def body(x_ref, g_ref, dy_ref, res_ref, d_ref, cs_ref, rb_ref, rbt_ref, rct_ref, lr_ref, li_ref,
             dx_ref, dd_ref, drb_ref, drc_ref, dlr_ref, dli_ref, dg_ref, tmp, du, lhsu, lhsd, xs, adj, acarry):
        i = pl.program_id(0)
        u = _rms_hat(x_ref[...])[0] * g_ref[...]

        @pl.when(i == 0)
        def _():
            acarry[...] = jnp.zeros_like(acarry)
            dd_ref[...] = jnp.zeros_like(dd_ref)
            drb_ref[...] = jnp.zeros_like(drb_ref)
            drc_ref[...] = jnp.zeros_like(drc_ref)
            dlr_ref[...] = jnp.zeros_like(dlr_ref)
            dli_ref[...] = jnp.zeros_like(dli_ref)
            dg_ref[...] = jnp.zeros_like(dg_ref)

        dd_ref[...] += jnp.sum(dy_ref[...] * u, axis=0, keepdims=True)
        mask = _row_mask(tc)
        for blk in range(S5_BLOCKS):
            sl = slice(blk * 256, (blk + 1) * 256)
            lhsu[blk] = _expand_rows(u[:, sl], mask)
            xs[blk] = jnp.dot(lhsu[blk], rb_ref[blk], preferred_element_type=F32)
            lhsd[blk] = _expand_rows(dy_ref[:, sl], mask)
            adj[blk] = jnp.dot(lhsd[blk], rct_ref[blk], preferred_element_type=F32)
        lam = [(lr_ref[blk], li_ref[blk]) for blk in range(S5_BLOCKS)]

        def fstep(t, c):
            r0 = pl.multiple_of(t * 8, 8)
            new = []
            for blk in range(S5_BLOCKS):
                xr, xi = c[2 * blk], c[2 * blk + 1]
                lr, li = lam[blk]
                nr = lr * xr - li * xi + xs[blk, pl.ds(r0, 8), 0:128]
                ni = lr * xi + li * xr + xs[blk, pl.ds(r0, 8), 128:256]
                xs[blk, pl.ds(r0, 8), 0:128] = nr
                xs[blk, pl.ds(r0, 8), 128:256] = ni
                new += [nr, ni]
            return tuple(new)

        c0 = []
        for blk in range(S5_BLOCKS):
            c0 += [cs_ref[0, blk, :, 0:128], cs_ref[0, blk, :, 128:256]]
        lax.fori_loop(0, tc, fstep, tuple(c0), unroll=4)

        def bstep(k, c):
            t = tc - 1 - k
            r0 = pl.multiple_of(t * 8, 8)
            rp = pl.multiple_of(jnp.maximum(t - 1, 0) * 8, 8)
            first = t == 0
            new_a, new_g = [], []
            for blk in range(S5_BLOCKS):
                ar, ai = c[0][2 * blk], c[0][2 * blk + 1]
                glr, gli = c[1][2 * blk], c[1][2 * blk + 1]
                lr, li = lam[blk]
                nr = lr * ar + li * ai + adj[blk, pl.ds(r0, 8), 0:128]
                ni = lr * ai - li * ar + adj[blk, pl.ds(r0, 8), 128:256]
                adj[blk, pl.ds(r0, 8), 0:128] = nr
                adj[blk, pl.ds(r0, 8), 128:256] = ni
                pr = jnp.where(first, cs_ref[0, blk, :, 0:128], xs[blk, pl.ds(rp, 8), 0:128])
                pi = jnp.where(first, cs_ref[0, blk, :, 128:256], xs[blk, pl.ds(rp, 8), 128:256])
                new_a += [nr, ni]
                new_g += [glr + nr * pr + ni * pi, gli + ni * pr - nr * pi]
            return tuple(new_a), tuple(new_g)

        a0, g0 = [], []
        for blk in range(S5_BLOCKS):
            a0 += [acarry[blk, :, 0:128], acarry[blk, :, 128:256]]
            g0 += [dlr_ref[blk], dli_ref[blk]]
        an, gn = lax.fori_loop(0, tc, bstep, (tuple(a0), tuple(g0)), unroll=2)
        for blk in range(S5_BLOCKS):
            acarry[blk, :, 0:128] = an[2 * blk]
            acarry[blk, :, 128:256] = an[2 * blk + 1]
            dlr_ref[blk] = gn[2 * blk]
            dli_ref[blk] = gn[2 * blk + 1]
        for blk in range(S5_BLOCKS):
            sl = slice(blk * 256, (blk + 1) * 256)
            ab = adj[blk].astype(BF16)
            _stage(tmp, jnp.dot(ab, rbt_ref[blk], preferred_element_type=F32))
            du[:, sl] = _gather_rows(tmp, tc) + d_ref[:, sl] * dy_ref[:, sl]
            drb_ref[blk] += lax.dot_general(lhsu[blk], ab, (((0,), (0,)), ((), ())), preferred_element_type=F32)
            drc_ref[blk] += lax.dot_general(lhsd[blk], xs[blk].astype(BF16), (((0,), (0,)), ((), ())),
                                            preferred_element_type=F32)
        xh, r = _rms_hat(x_ref[...])
        dg_ref[...] += jnp.sum(du[...] * xh, axis=0, keepdims=True)
        dxh = du[...] * g_ref[...]
        dx_ref[...] = r * (dxh - xh * jnp.mean(dxh * xh, axis=-1, keepdims=True)) + res_ref[...]

    rev = pl.BlockSpec((tc, D_MODEL), lambda i: (nc - 1 - i, 0))
    vec = pl.BlockSpec((1, D_MODEL), lambda i: (0, 0))
    mat = pl.BlockSpec((S5_BLOCKS, 256, 256), lambda i: (0, 0, 0))
    lamspec = pl.BlockSpec((S5_BLOCKS, 8, 128), lambda i: (0, 0, 0))
    big = pltpu.VMEM((S5_BLOCKS, 8 * tc, 256), F32)
    bigb = pltpu.VMEM((S5_BLOCKS, 8 * tc, 256), BF16)
    return pl.pallas_call(
        body, grid=(nc,),
        in_specs=[rev, vec, rev, rev, vec, pl.BlockSpec((1, S5_BLOCKS, 8, 256), lambda i: (nc - 1 - i, 0, 0, 0)),
                  mat, mat, mat, lamspec, lamspec],
        out_specs=[rev, vec, mat, mat, lamspec, lamspec, vec],
        out_shape=[_sds((n_rows, D_MODEL), F32), _sds((1, D_MODEL), F32), _sds((S5_BLOCKS, 256, 256), F32),
                   _sds((S5_BLOCKS, 256, 256), F32), _sds((S5_BLOCKS, 8, 128), F32), _sds((S5_BLOCKS, 8, 128), F32),
                   _sds((1, D_MODEL), F32)],
        scratch_shapes=[pltpu.VMEM((2, 8 * tc, 128), F32), pltpu.VMEM((tc, D_MODEL), F32), bigb, bigb, big, big,
                        pltpu.VMEM((S5_BLOCKS, 8, 256), F32)],
        name="s5_bwd", compiler_params=_params(("arbitrary",)))(
            x, gain, dy2, res, d_skip, cs, rb, rbt, rct, lam_r, lam_i)


def _s5_views(a_re, a_im, log_dt, b_re, b_im):
    return a_re[:, None, :], a_im[:, None, :], log_dt[:, None, None], jnp.swapaxes(b_re, 1, 2), jnp.swapaxes(b_im, 1, 2)


def _s5_factors(a_re, a_im, log_dt):
    lr, li, dt = jnp.minimum(a_re, LAMBDA_RE_MAX), a_im, jnp.exp(log_dt)
    mag, ang = jnp.exp(lr * dt), li * dt
    lbr, lbi = mag * jnp.cos(ang), mag * jnp.sin(ang)
    den = lr * lr + li * li
    fr, fi = ((lbr - 1.0) * lr + lbi * li) / den, (lbi * lr - (lbr - 1.0) * li) / den
    return lr, li, dt, lbr, lbi, fr, fi, den


def s5_prep(a_re, a_im, log_dt, b_re, b_im, c_re, c_im):
    def body(ar_ref, ai_ref, t_ref, br_ref, bi_ref, cr_ref, ci_ref, rb_ref, rbt_ref, rc_ref, rct_ref, lr_ref, li_ref):
        _, _, _, lbr, lbi, fr, fi, _ = _s5_factors(ar_ref[...], ai_ref[...], t_ref[...])
        lr_ref[...] = lbr
        li_ref[...] = lbi
        bre = fr * br_ref[...] - fi * bi_ref[...]
        bim = fr * bi_ref[...] + fi * br_ref[...]
        even = (lax.broadcasted_iota(jnp.int32, (256, S5_STATE), 0) // S5_GROUP) % 2 == 0

        def assemble(re, im):
            re, im = re.reshape(256, S5_STATE), im.reshape(256, S5_STATE)
            return jnp.concatenate([jnp.where(even, re, 0.0), jnp.where(even, 0.0, re), jnp.where(even, im, 0.0),
                                    jnp.where(even, 0.0, im)], axis=1)

        for blk in range(S5_BLOCKS):
            sl = slice(16 * blk, 16 * blk + 16)
            rb = assemble(bre[sl], bim[sl])
            rct = assemble(cr_ref[sl], -ci_ref[sl])
            rb_ref[blk] = rb.astype(BF16)
            rbt_ref[blk] = rb.T.astype(BF16)
            rct_ref[blk] = rct.astype(BF16)
            rc_ref[blk] = rct.T.astype(BF16)

    vm = pl.BlockSpec(memory_space=pltpu.VMEM)
    mat = _sds((S5_BLOCKS, 256, 256), BF16)
    lam = _sds((S5_GROUPS, 1, S5_STATE), F32)
    rb, rbt, rc, rct, lam_r, lam_i = pl.pallas_call(
        body, in_specs=[vm] * 7, out_specs=[vm] * 6, out_shape=[mat, mat, mat, mat, lam, lam], name="s5_prep",
        compiler_params=_params())(*_s5_views(a_re, a_im, log_dt, b_re, b_im), c_re, c_im)
    return rb, rbt, rc, rct, lam_r.reshape(S5_BLOCKS, 8, 128), lam_i.reshape(S5_BLOCKS, 8, 128)


def s5_param_bwd(mats, lams, a_re, a_im, log_dt, b_re, b_im):
    def body(m_ref, glr_ref, gli_ref, ar_ref, ai_ref, t_ref, br_ref, bi_ref,
             dar_ref, dai_ref, dt_ref, dbr_ref, dbi_ref, dcr_ref, dci_ref):
        lr, li, dt, lbr, lbi, fr, fi, den = _s5_factors(ar_ref[...], ai_ref[...], t_ref[...])
        shape = (S5_GROUPS, S5_GROUP, S5_STATE)
        gbr, gbi = m_ref[0:1024, 0:64].reshape(shape), m_ref[0:1024, 64:128].reshape(shape)
        dcr_ref[...] = m_ref[1024:2048, 0:64].reshape(shape)
        dci_ref[...] = -m_ref[1024:2048, 64:128].reshape(shape)
        br, bi = br_ref[...], bi_ref[...]
        dbr_ref[...] = fr * gbr + fi * gbi
        dbi_ref[...] = fr * gbi - fi * gbr
        dfr = jnp.sum(gbr * br + gbi * bi, axis=1, keepdims=True)
        dfi = jnp.sum(gbi * br - gbr * bi, axis=1, keepdims=True)
        nr, ni = (dfr * lr - dfi * li) / den, (dfr * li + dfi * lr) / den
        qr, qi = (fr * lr + fi * li) / den, (fi * lr - fr * li) / den
        lam_r, lam_i = -(dfr * qr + dfi * qi), -(dfi * qr - dfr * qi)
        gr, gi = glr_ref[...] + nr, gli_ref[...] + ni
        zr, zi = gr * lbr + gi * lbi, gi * lbr - gr * lbi
        a = ar_ref[...]
        dar_ref[...] = (lam_r + zr * dt) * jnp.where(a < LAMBDA_RE_MAX, 1.0, jnp.where(a == LAMBDA_RE_MAX, 0.5, 0.0))
        dai_ref[...] = lam_i + zi * dt
        dt_ref[...] = jnp.sum(zr * lr + zi * li, axis=2, keepdims=True) * dt

    vm = pl.BlockSpec(memory_space=pltpu.VMEM)
    state = _sds((S5_GROUPS, 1, S5_STATE), F32)
    wide = _sds((S5_GROUPS, S5_GROUP, S5_STATE), F32)
    glr = lams[0:32].reshape(S5_GROUPS, 1, S5_STATE)
    gli = lams[32:64].reshape(S5_GROUPS, 1, S5_STATE)
    dar, dai, ddt, dbr, dbi, dcr, dci = pl.pallas_call(
        body, in_specs=[vm] * 8, out_specs=[vm] * 7,
        out_shape=[state, state, _sds((S5_GROUPS, 1, 1), F32), wide, wide, wide, wide], name="s5_param_bwd",
        compiler_params=_params())(mats, glr, gli, *_s5_views(a_re, a_im, log_dt, b_re, b_im))
    return (dar.reshape(S5_GROUPS, S5_STATE), dai.reshape(S5_GROUPS, S5_STATE), ddt.reshape(S5_GROUPS),
            jnp.swapaxes(dbr, 1, 2), jnp.swapaxes(dbi, 1, 2), dcr, dci)


def s5_compact(drb, drct, dlr, dli):
    def body(drb_ref, drct_ref, dlr_ref, dli_ref, o_ref, lam_ref):
        even = (lax.broadcasted_iota(jnp.int32, (256, 64), 0) // S5_GROUP) % 2 == 0
        for blk in range(S5_BLOCKS):
            for k, ref in enumerate((drb_ref, drct_ref)):
                m = ref[blk]
                re = jnp.where(even, m[:, 0:64], m[:, 64:128])
                im = jnp.where(even, m[:, 128:192], m[:, 192:256])
                o_ref[pl.ds(k * 1024 + blk * 256, 256), :] = jnp.concatenate([re, im], axis=1)
            lam_ref[pl.ds(blk * 8, 8), :] = dlr_ref[blk]
            lam_ref[pl.ds(32 + blk * 8, 8), :] = dli_ref[blk]

    vm = pl.BlockSpec(memory_space=pltpu.VMEM)
    return pl.pallas_call(body, in_specs=[vm] * 4, out_specs=[vm, vm], out_shape=[_sds((2048, 128), F32), _sds((64, 128), F32)],
                          name="s5_compact", compiler_params=_params())(drb, drct, dlr, dli)


NEG = -1e30


GROUP = N_Q // N_KV


def _attn_masks(n):
    qi = lax.broadcasted_iota(jnp.int32, (GROUP * BLOCK, BLOCK), 0) % BLOCK
    kj = lax.broadcasted_iota(jnp.int32, (GROUP * BLOCK, BLOCK), 1)
    return jnp.logical_and(kj > qi, n > 0), kj <= qi


def _stack_heads(ref, kh):
    return jnp.concatenate([ref[:, (GROUP * kh + g) * HEAD_DIM:(GROUP * kh + g + 1) * HEAD_DIM] for g in range(GROUP)], axis=0)


def _unstack_heads(val):
    return jnp.concatenate([val[g * BLOCK:(g + 1) * BLOCK] for g in range(GROUP)], axis=1)


def _sink_column(sink_ref, kh):
    grp = lax.broadcasted_iota(jnp.int32, (GROUP * BLOCK, 1), 0) // BLOCK
    col = jnp.zeros((GROUP * BLOCK, 1), F32)
    for g in range(GROUP):
        col = jnp.where(grp == g, sink_ref[GROUP * kh + g], col)
    return col, grp


def _attn_exp(q4, kp, kc, sink, mask_p, mask_c):
    scale = 1.0 / math.sqrt(HEAD_DIM)
    nt = (((1,), (1,)), ((), ()))
    sp = jnp.where(mask_p, lax.dot_general(q4, kp, nt, preferred_element_type=F32) * scale, NEG)
    sc = jnp.where(mask_c, lax.dot_general(q4, kc, nt, preferred_element_type=F32) * scale, NEG)
    m = jnp.maximum(jnp.maximum(jnp.max(sp, axis=-1, keepdims=True), jnp.max(sc, axis=-1, keepdims=True)), sink)
    pp = jnp.exp(sp - m)
    pc = jnp.exp(sc - m)
    ps = jnp.exp(sink - m)
    inv = 1.0 / (jnp.sum(pp, axis=-1, keepdims=True) + jnp.sum(pc, axis=-1, keepdims=True) + ps)
    return pp, pc, ps, inv


def attn_fwd(q, kv, sinks):
    n_rows = q.shape[0]
    nb = n_rows // BLOCK

    def body(sink_ref, q_ref, kvp_ref, kvc_ref, o_ref):
        n = pl.program_id(0)
        mask_p, mask_c = _attn_masks(n)
        outs = []
        for kh in range(N_KV):
            ks, vs = slice(kh * HEAD_DIM, (kh + 1) * HEAD_DIM), slice((N_KV + kh) * HEAD_DIM, (N_KV + kh + 1) * HEAD_DIM)
            sink, _ = _sink_column(sink_ref, kh)
            pp, pc, _, inv = _attn_exp(_stack_heads(q_ref, kh), kvp_ref[:, ks], kvc_ref[:, ks], sink, mask_p, mask_c)
            o4 = (jnp.dot(pp.astype(BF16), kvp_ref[:, vs], preferred_element_type=F32)
                  + jnp.dot(pc.astype(BF16), kvc_ref[:, vs], preferred_element_type=F32)) * inv
            outs.append(_unstack_heads(o4))
        o_ref[...] = jnp.concatenate(outs, axis=1).astype(BF16)

    kvw = 2 * N_KV * HEAD_DIM
    return pl.pallas_call(
        body, grid=(nb,),
        in_specs=[pl.BlockSpec(memory_space=pltpu.SMEM), pl.BlockSpec((BLOCK, D_MODEL), lambda n: (n, 0)),
                  pl.BlockSpec((BLOCK, kvw), lambda n: (jnp.maximum(n - 1, 0), 0)), pl.BlockSpec((BLOCK, kvw), lambda n: (n, 0))],
        out_specs=pl.BlockSpec((BLOCK, D_MODEL), lambda n: (n, 0)), out_shape=_sds((n_rows, D_MODEL), BF16),
        name="attn_fwd", compiler_params=_params(("parallel",)))(sinks, q, kv, kv)


def attn_bwd(q, kv, do, sinks):
    n_rows = q.shape[0]
    nb = n_rows // BLOCK
    kvw = 2 * N_KV * HEAD_DIM
    tn = (((0,), (0,)), ((), ()))
    nt = (((1,), (1,)), ((), ()))
    scale = 1.0 / math.sqrt(HEAD_DIM)

    def body(sink_ref, q_ref, kvp_ref, kvc_ref, do_ref, dq_ref, dbq_ref, dprev_ref, dcur_ref, dsink_ref):
        n = pl.program_id(0)
        mask_p, mask_c = _attn_masks(n)
        lane = lax.broadcasted_iota(jnp.int32, (1, D_MODEL), 1)
        dqs, dsink = [], jnp.zeros((1, D_MODEL), F32)
        dkp, dkc, dvp, dvc = [], [], [], []
        for kh in range(N_KV):
            ks, vs = slice(kh * HEAD_DIM, (kh + 1) * HEAD_DIM), slice((N_KV + kh) * HEAD_DIM, (N_KV + kh + 1) * HEAD_DIM)
            q4, do4 = _stack_heads(q_ref, kh), _stack_heads(do_ref, kh)
            kp, kc, vp, vc = kvp_ref[:, ks], kvc_ref[:, ks], kvp_ref[:, vs], kvc_ref[:, vs]
            sink, grp = _sink_column(sink_ref, kh)
            pp, pc, ps, inv = _attn_exp(q4, kp, kc, sink, mask_p, mask_c)
            pp, pc = pp * inv, pc * inv
            dpp = lax.dot_general(do4, vp, nt, preferred_element_type=F32)
            dpc = lax.dot_general(do4, vc, nt, preferred_element_type=F32)
            delta = jnp.sum(pp * dpp, axis=-1, keepdims=True) + jnp.sum(pc * dpc, axis=-1, keepdims=True)
            dsp = (pp * (dpp - delta) * scale).astype(BF16)
            dsc = (pc * (dpc - delta) * scale).astype(BF16)
            dsk = ps * inv * delta
            for g in range(GROUP):
                dsink = dsink + jnp.where(lane == GROUP * kh + g, -jnp.sum(jnp.where(grp == g, dsk, 0.0)), 0.0)
            dqs.append(_unstack_heads(jnp.dot(dsp, kp, preferred_element_type=F32)
                                      + jnp.dot(dsc, kc, preferred_element_type=F32)))
            dkp.append(lax.dot_general(dsp, q4, tn, preferred_element_type=F32))
            dkc.append(lax.dot_general(dsc, q4, tn, preferred_element_type=F32))
            dvp.append(lax.dot_general(pp.astype(BF16), do4, tn, preferred_element_type=F32))
            dvc.append(lax.dot_general(pc.astype(BF16), do4, tn, preferred_element_type=F32))
        dq = jnp.concatenate(dqs, axis=1)
        dq_ref[...] = dq.astype(BF16)
        dprev_ref[0] = jnp.concatenate(dkp + dvp, axis=1)
        dcur_ref[0] = jnp.concatenate(dkc + dvc, axis=1)

        @pl.when(n == 0)
        def _():
            dbq_ref[...] = jnp.zeros_like(dbq_ref)
            dsink_ref[...] = jnp.zeros_like(dsink_ref)

        dbq_ref[...] += jnp.sum(dq, axis=0, keepdims=True)
        dsink_ref[...] += dsink

    blk = pl.BlockSpec((BLOCK, D_MODEL), lambda n: (n, 0))
    part = pl.BlockSpec((1, BLOCK, kvw), lambda n: (n, 0, 0))
    return pl.pallas_call(
        body, grid=(nb,),
        in_specs=[pl.BlockSpec(memory_space=pltpu.SMEM), blk,
                  pl.BlockSpec((BLOCK, kvw), lambda n: (jnp.maximum(n - 1, 0), 0)), pl.BlockSpec((BLOCK, kvw), lambda n: (n, 0)), blk],
        out_specs=[blk, pl.BlockSpec((1, D_MODEL), lambda n: (0, 0)), part, part, pl.BlockSpec((1, D_MODEL), lambda n: (0, 0))],
        out_shape=[_sds((n_rows, D_MODEL), BF16), _sds((1, D_MODEL), F32), _sds((nb, BLOCK, kvw), F32),
                   _sds((nb, BLOCK, kvw), F32), _sds((1, D_MODEL), F32)],
        name="attn_bwd", compiler_params=_params(("arbitrary",)))(sinks, q, kv, kv, do)


def kv_combine(dprev, dcur):
    nb, _, kvw = dprev.shape

    def body(dcur_ref, dprev_ref, dkv_ref, db_ref):
        total = jnp.zeros((1, kvw), F32)
        for m in range(nb):
            dkv = dcur_ref[m] + dprev_ref[m + 1] if m + 1 < nb else dcur_ref[m]
            dkv_ref[m * BLOCK:(m + 1) * BLOCK, :] = dkv.astype(BF16)
            total = total + jnp.sum(dkv, axis=0, keepdims=True)
        db_ref[...] = jnp.concatenate([total, jnp.zeros((1, D_MODEL - kvw), F32)], axis=1)

    vm = pl.BlockSpec(memory_space=pltpu.VMEM)
    return pl.pallas_call(body, in_specs=[vm, vm], out_specs=[vm, vm],
                          out_shape=[_sds((nb * BLOCK, kvw), BF16), _sds((1, D_MODEL), F32)], name="kv_combine",
                          compiler_params=_params())(dcur, dprev)


def glu_bwd(dout, val, gate, tm=256):
    n_rows, d = dout.shape

    def body(do_ref, v_ref, g_ref, dz_ref, db_ref):
        i = pl.program_id(0)
        sg = jax.nn.sigmoid(g_ref[...])
        dval = do_ref[...] * sg
        dgate = do_ref[...] * v_ref[...] * sg * (1.0 - sg)
        dz_ref[...] = jnp.concatenate([dval, dgate], axis=1).astype(BF16)

        @pl.when(i == 0)
        def _():
            db_ref[...] = jnp.zeros_like(db_ref)

        db_ref[0:1, :] += jnp.sum(dval, axis=0, keepdims=True)
        db_ref[1:2, :] += jnp.sum(dgate, axis=0, keepdims=True)

    row = pl.BlockSpec((tm, d), lambda i: (i, 0))
    return pl.pallas_call(
        body, grid=(n_rows // tm,), in_specs=[row, row, row],
        out_specs=[pl.BlockSpec((tm, 2 * d), lambda i: (i, 0)), pl.BlockSpec((2, d), lambda i: (0, 0))],
        out_shape=[_sds((n_rows, 2 * d), BF16), _sds((2, d), F32)],
        name="glu_bwd", compiler_params=_params(("arbitrary",)))(dout, val, gate)


def _adam_update(w, g, m, v):
    nm = ADAM_B1 * m + (1.0 - ADAM_B1) * g
    nv = ADAM_B2 * v + (1.0 - ADAM_B2) * (g * g)
    m_hat = nm / (1.0 - ADAM_B1 ** ADAM_STEP)
    v_hat = nv / (1.0 - ADAM_B2 ** ADAM_STEP)
    return -ADAM_LR * (m_hat / (jnp.sqrt(v_hat) + ADAM_EPS) + ADAM_WD * w), nm, nv


def adamw(name, ws, gs, ms, vs, steps=8):
    n = len(ws)

    def body(*refs):
        for k in range(n):
            w_ref, g_ref, m_ref, v_ref = (refs[j * n + k] for j in range(4))
            go_ref, d_ref, nm_ref, nv_ref = (refs[(4 + j) * n + k] for j in range(4))
            gv = g_ref[...]
            go_ref[...] = gv
            d_ref[...], nm_ref[...], nv_ref[...] = _adam_update(w_ref[...], gv, m_ref[...], v_ref[...])

    specs = [pl.BlockSpec((w.shape[0] // steps, w.shape[1]), lambda i: (i, 0)) for w in ws]
    shapes = [_sds(w.shape, F32) for w in ws]
    out = pl.pallas_call(
        body, grid=(steps,), in_specs=specs * 4, out_specs=specs * 4, out_shape=shapes * 4, name=name,
        compiler_params=_params(("parallel",)))(*ws, *gs, *ms, *vs)
    return [list(out[j * n:(j + 1) * n]) for j in range(4)]


def adamw_native(name, ws, gs, ms, vs):
    n = len(ws)

    def body(*refs):
        w_refs, g_refs, m_refs, v_refs = refs[:n], refs[n:2 * n], refs[2 * n:3 * n], refs[3 * n:4 * n]
        d_refs, nm_refs, nv_refs = refs[4 * n:5 * n], refs[5 * n:6 * n], refs[6 * n:7 * n]
        for k in range(n):
            dl, nm, nv = _adam_update(w_refs[k][...], g_refs[k][...], m_refs[k][...], v_refs[k][...])
            d_refs[k][...] = dl
            nm_refs[k][...] = nm
            nv_refs[k][...] = nv

    vm = pl.BlockSpec(memory_space=pltpu.VMEM)
    shapes = [_sds(w.shape, F32) for w in ws]
    out = pl.pallas_call(body, in_specs=[vm] * (4 * n), out_specs=[vm] * (3 * n), out_shape=shapes * 3, name=name,
                         compiler_params=_params())(*ws, *gs, *ms, *vs)
    return list(out[:n]), list(out[n:2 * n]), list(out[2 * n:])


VEC_ROWS = {"norm_mix": 0, "norm_mlp": 2, "norm_kv": 4, "norm_final": 5, "s5_d": 6, "b_q": 7, "b_o": 8, "s5_b_glu": 9,
            "b_kv": 11, "sinks": 12, "loss": 13}


def split_vectors(where, vecs, d_shard, glu_shard):
    kvw = 2 * N_KV * HEAD_DIM
    shapes = {"norm_mix": (2, D_MODEL), "norm_mlp": (2, D_MODEL), "norm_kv": (1, D_MODEL), "norm_final": (1, D_MODEL),
              "s5_d": (1, d_shard), "b_q": (1, D_MODEL), "b_o": (1, D_MODEL), "s5_b_glu": (1, glu_shard), "b_kv": (1, kvw),
              "sinks": (1, N_Q), "loss": (1, 128)}
    names = list(shapes)

    def body(where_ref, v_ref, *o_refs):
        chip = where_ref[1]
        for name, o_ref in zip(names, o_refs):
            r0, (r, n) = VEC_ROWS[name], shapes[name]
            if name == "s5_d":
                g = jnp.zeros((1, n), F32)
                for j in range(4):
                    g = jnp.where(chip == j, v_ref[r0:r0 + 1, j * n:(j + 1) * n], g)
            elif name == "s5_b_glu":
                g = jnp.zeros((1, n), F32)
                for j in range(4):
                    row, col = r0 + (j * n) // D_MODEL, (j * n) % D_MODEL
                    g = jnp.where(chip == j, v_ref[row:row + 1, col:col + n], g)
            else:
                g = v_ref[r0:r0 + r, 0:n]
            o_ref[...] = g

    vm = pl.BlockSpec(memory_space=pltpu.VMEM)
    out = pl.pallas_call(body, in_specs=[pl.BlockSpec(memory_space=pltpu.SMEM), vm], out_specs=[vm] * len(names),
                         out_shape=[_sds(shapes[n], F32) for n in names], name="split_vectors",
                         compiler_params=_params())(where, vecs)
    return dict(zip(names, out))


def _position():
    x, y, c = lax.axis_index("x"), lax.axis_index("y"), lax.axis_index("c")
    others = [(1 - x, y), (x, 1 - y), (1 - x, 1 - y)]
    return x, y, c, others


def _window(ref, kind, chip, half, shard_shape):
    if kind == "slab":
        return ref.at[chip]
    r, n = shard_shape
    if kind == "col":
        return ref.at[pl.ds(pl.multiple_of(half * (r // 2), 16), r // 2), pl.ds(pl.multiple_of(chip * n, 128), n)]
    return ref.at[pl.ds(pl.multiple_of(chip * r, 16), r), pl.ds(pl.multiple_of(half * (n // 2), 128), n // 2)]


def _half(ref, kind, half, shape):
    r, n = shape
    if kind == "col":
        return ref.at[pl.ds(pl.multiple_of(half * (r // 2), 16), r // 2), :]
    return ref.at[:, pl.ds(pl.multiple_of(half * (n // 2), 128), n // 2)]


def swap_start(name, grads, kinds):
    nt = len(grads)
    shapes = [tuple(g.shape) for g in grads]
    lands = [lax.empty(sh, BF16) for sh in shapes]

    def body(*refs):
        in_refs, land_refs = refs[:nt], refs[nt:2 * nt]
        send_sems, recv_sems, token = refs[2 * nt], refs[2 * nt + 1], refs[-1]
        x, y, c, _ = _position()
        for t in range(nt):
            pltpu.make_async_remote_copy(
                src_ref=_half(in_refs[t], kinds[t], 1 - c, shapes[t]), dst_ref=_half(land_refs[t], kinds[t], 1 - c, shapes[t]),
                send_sem=send_sems.at[t], recv_sem=recv_sems.at[t], device_id=(x, y, 1 - c), device_id_type=MESH).start()
        token[...] = jnp.zeros_like(token)

    sems = pltpu.SemaphoreType.DMA((nt,))
    both = list(grads) + lands
    out = pl.pallas_call(
        body, name=name, in_specs=[HBM_SPEC] * (2 * nt),
        out_specs=(SEM_SPEC, SEM_SPEC, *[HBM_SPEC] * (2 * nt), pl.BlockSpec(memory_space=pltpu.VMEM)),
        out_shape=(sems, sems, *[pltpu.HBM(a.shape, a.dtype) for a in both], _sds((8, 128), F32)),
        input_output_aliases={t: 2 + t for t in range(2 * nt)}, compiler_params=_split_params(),
    )(*[_in_hbm(a) for a in both])
    return out[0], out[1], list(out[2:2 + nt]), list(out[2 + nt:2 + 2 * nt]), out[-1]


def swap_wait(name, send_sems, recv_sems, grads, lands, kinds, after):
    nt = len(grads)
    shapes = [tuple(g.shape) for g in grads]

    def body(*refs):
        in_refs, land_refs = refs[:nt], refs[nt:2 * nt]
        send_ref, recv_ref = refs[2 * nt], refs[2 * nt + 1]
        x, y, c, _ = _position()
        for t in range(nt):
            cp = pltpu.make_async_remote_copy(
                src_ref=_half(in_refs[t], kinds[t], 1 - c, shapes[t]), dst_ref=_half(land_refs[t], kinds[t], c, shapes[t]),
                send_sem=send_ref.at[t], recv_sem=recv_ref.at[t], device_id=(x, y, 1 - c), device_id_type=MESH)
            cp.wait_send()
            cp.wait_recv()

    both = list(grads) + list(lands)
    out = pl.pallas_call(
        body, name=name, in_specs=[HBM_SPEC] * (2 * nt) + [SEM_SPEC, SEM_SPEC, HBM_SPEC], out_specs=[HBM_SPEC] * (2 * nt),
        out_shape=[pltpu.HBM(a.shape, a.dtype) for a in both], input_output_aliases={t: t for t in range(2 * nt)},
        compiler_params=_split_params())(*both, send_sems, recv_sems, _in_hbm(after))
    return list(out[:nt]), list(out[nt:])


def _half_spec(kind, shape, tiles):
    r, n = shape
    if kind == "col":
        tn = n // tiles
        return pl.BlockSpec((r // 2, tn), lambda i, s: (s[0], i))
    tm = r // tiles
    return pl.BlockSpec((tm, n // 2), lambda i, s: (i, s[0]))


def add_halves(name, mine, landed, kinds, where, tiles=4):
    nt = len(mine)
    shapes = [tuple(a.shape) for a in mine]

    def compact(t):
        r, n = shapes[t]
        if kinds[t] == "col":
            return (r // 2, n), pl.BlockSpec((r // 2, n // tiles), lambda i, s: (0, i))
        return (r, n // 2), pl.BlockSpec((r // tiles, n // 2), lambda i, s: (i, 0))

    def body(s_ref, *refs):
        for a_ref, b_ref, o_ref in zip(refs[:nt], refs[nt:2 * nt], refs[2 * nt:]):
            o_ref[...] = (a_ref[...].astype(F32) + b_ref[...].astype(F32)).astype(BF16)

    specs = [_half_spec(kinds[t], shapes[t], tiles) for t in range(nt)]
    return pl.pallas_call(
        body, grid_spec=pltpu.PrefetchScalarGridSpec(num_scalar_prefetch=1, grid=(tiles,), in_specs=specs + specs,
                                                     out_specs=[compact(t)[1] for t in range(nt)]),
        out_shape=[_sds(compact(t)[0], BF16) for t in range(nt)], name=name,
        compiler_params=_params(("parallel",)))(where, *mine, *landed)


def sum_shards(name, parts, landed, kinds, shard_shapes, where, layers, n_layers, intos, tiles=2):
    nt = len(parts)
    in_specs, out_specs = [], []
    for t in range(nt):
        (r, n), layer = shard_shapes[t], layers[t]
        if kinds[t] == "col":
            tm, width = r // 2 // tiles, n
            own = pl.BlockSpec((tm, n), lambda i, s: (i, s[1]))
            out = pl.BlockSpec((None, tm, n), lambda i, s, layer=layer: (layer, s[0] * tiles + i, 0))
        else:
            tm, width = r // tiles, n // 2
            own = pl.BlockSpec((tm, n // 2), lambda i, s: (s[1] * tiles + i, 0))
            out = pl.BlockSpec((None, tm, n // 2), lambda i, s, layer=layer: (layer, i, s[0]))
        in_specs += [own, pl.BlockSpec((3, tm, width), lambda i, s: (0, i, 0))]
        out_specs.append(out)
    args, aliases = [where] + [a for pair in zip(parts, landed) for a in pair], {}
    for t in range(nt):
        if intos[t] is not None:
            aliases[len(args)] = t
            in_specs.append(pl.BlockSpec(memory_space=pl.ANY))
            args.append(intos[t])

    def body(s_ref, *refs):
        for t in range(nt):
            a_ref, l_ref, o_ref = refs[2 * t], refs[2 * t + 1], refs[len(in_specs) + t]
            o_ref[...] = ((a_ref[...].astype(F32) + l_ref[0].astype(F32)) + l_ref[1].astype(F32)) + l_ref[2].astype(F32)

    return pl.pallas_call(
        body, grid_spec=pltpu.PrefetchScalarGridSpec(num_scalar_prefetch=1, grid=(tiles,), in_specs=in_specs,
                                                     out_specs=out_specs),
        out_shape=[_sds((n_layers[t],) + tuple(shard_shapes[t]), F32) for t in range(nt)], input_output_aliases=aliases,
        name=name, compiler_params=_params(("parallel",)))(*args)


def share_start(arrays, entries):
    na, nt = len(arrays), len(entries)

    def body(*refs):
        in_refs, send_sems, recv_sems, token = refs[:na], refs[na], refs[na + 1], refs[-1]
        x, y, c, _ = _position()
        for t, (a, layer, kind) in enumerate(entries):
            mine = _half(in_refs[a].at[layer], kind, c, tuple(arrays[a].shape[1:]))
            pltpu.make_async_remote_copy(
                src_ref=mine, dst_ref=mine, send_sem=send_sems.at[t], recv_sem=recv_sems.at[t],
                device_id=(x, y, 1 - c), device_id_type=MESH).start()
        token[...] = jnp.zeros_like(token)

    sems = pltpu.SemaphoreType.DMA((nt,))
    out = pl.pallas_call(
        body, name="share_start", in_specs=[HBM_SPEC] * na,
        out_specs=(SEM_SPEC, SEM_SPEC, *[HBM_SPEC] * na, pl.BlockSpec(memory_space=pltpu.VMEM)),
        out_shape=(sems, sems, *[pltpu.HBM(a.shape, a.dtype) for a in arrays], _sds((8, 128), F32)),
        input_output_aliases={t: 2 + t for t in range(na)}, compiler_params=_split_params(),
    )(*[_in_hbm(a) for a in arrays])
    return out[0], out[1], list(out[2:2 + na]), out[-1]


def share_wait(send_sems, recv_sems, arrays, entries, after):
    na = len(arrays)

    def body(*refs):
        in_refs, send_ref, recv_ref = refs[:na], refs[na], refs[na + 1]
        x, y, c, _ = _position()
        for t, (a, layer, kind) in enumerate(entries):
            shape = tuple(arrays[a].shape[1:])
            cp = pltpu.make_async_remote_copy(
                src_ref=_half(in_refs[a].at[layer], kind, c, shape), dst_ref=_half(in_refs[a].at[layer], kind, 1 - c, shape),
                send_sem=send_ref.at[t], recv_sem=recv_ref.at[t], device_id=(x, y, 1 - c), device_id_type=MESH)
            cp.wait_send()
            cp.wait_recv()

    return list(pl.pallas_call(
        body, name="share_wait", in_specs=[HBM_SPEC] * na + [SEM_SPEC, SEM_SPEC, HBM_SPEC], out_specs=[HBM_SPEC] * na,
        out_shape=[pltpu.HBM(a.shape, a.dtype) for a in arrays], input_output_aliases={t: t for t in range(na)},
        compiler_params=_split_params())(*arrays, send_sems, recv_sems, _in_hbm(after)))


HBM_SPEC = pl.BlockSpec(memory_space=pltpu.HBM)
SEM_SPEC = pl.BlockSpec(memory_space=pltpu.SEMAPHORE)
ANY_SPEC = pl.BlockSpec(memory_space=pl.ANY)


def _split_params():
    return pltpu.CompilerParams(has_side_effects=pltpu.SideEffectType.DATAFLOW_SIDE_EFFECTING,
                                vmem_limit_bytes=VMEM_LIMIT_BYTES)


def _in_hbm(a):
    return pltpu.with_memory_space_constraint(a, pltpu.HBM)


def cast_place(arrays, entries, where, tiles=2):
    in_specs, out_specs, fulls = [], [], []
    for a, layer, kind in entries:
        _, r, n = arrays[a].shape
        tm = r // tiles
        in_specs.append(pl.BlockSpec((None, tm, n), lambda i, s, layer=layer: (layer, i, 0)))
        if kind == "col":
            fulls.append((r, 4 * n))
            out_specs.append(pl.BlockSpec((tm, n), lambda i, s: (i, s[1])))
        else:
            fulls.append((4 * r, n))
            out_specs.append(pl.BlockSpec((tm, n), lambda i, s: (s[1] * tiles + i, 0)))
    nt = len(entries)

    def body(s_ref, *refs):
        for w_ref, o_ref in zip(refs[:nt], refs[nt:]):
            o_ref[...] = w_ref[...].astype(BF16)

    return pl.pallas_call(
        body, grid_spec=pltpu.PrefetchScalarGridSpec(num_scalar_prefetch=1, grid=(tiles,), in_specs=in_specs,
                                                     out_specs=out_specs),
        out_shape=[_sds(f, BF16) for f in fulls], name="cast_place",
        compiler_params=_params(("parallel",)))(where, *[arrays[a] for a, _, _ in entries])


def gather_start(fulls, kinds, shard_shapes):
    nt = len(fulls)

    def body(*refs):
        full_refs = refs[:nt]
        send_sems, recv_sems, token = refs[nt], refs[nt + 1], refs[-1]
        x, y, c, others = _position()
        for t in range(nt):
            mine = _window(full_refs[t], kinds[t], 2 * x + y, c, shard_shapes[t])
            for j, (ox, oy) in enumerate(others):
                pltpu.make_async_remote_copy(
                    src_ref=mine, dst_ref=mine, send_sem=send_sems.at[3 * t + j], recv_sem=recv_sems.at[3 * t + j],
                    device_id=(ox, oy, c), device_id_type=MESH).start()
        token[...] = jnp.zeros_like(token)

    sems = pltpu.SemaphoreType.DMA((3 * nt,))
    out = pl.pallas_call(
        body, name="gather_start", in_specs=[HBM_SPEC] * nt,
        out_specs=(SEM_SPEC, SEM_SPEC, *[HBM_SPEC] * nt, pl.BlockSpec(memory_space=pltpu.VMEM)),
        out_shape=(sems, sems, *[pltpu.HBM(f.shape, f.dtype) for f in fulls], _sds((8, 128), F32)),
        input_output_aliases={t: 2 + t for t in range(nt)}, compiler_params=_split_params(),
    )(*[_in_hbm(f) for f in fulls])
    return out[0], out[1], list(out[2:2 + nt]), out[-1]


def gather_wait(name, send_sems, recv_sems, fulls, kinds, shard_shapes, after, first):
    nt = len(fulls)
    extra = [] if after is None else [_in_hbm(after)]

    def body(*refs):
        full_refs, send_ref, recv_ref = refs[:nt], refs[nt], refs[nt + 1]
        x, y, c, others = _position()
        for t in range(nt):
            mine = _window(full_refs[t], kinds[t], 2 * x + y, c, shard_shapes[t])
            for j, (ox, oy) in enumerate(others):
                cp = pltpu.make_async_remote_copy(
                    src_ref=mine, dst_ref=_window(full_refs[t], kinds[t], 2 * ox + oy, c, shard_shapes[t]),
                    send_sem=send_ref.at[3 * (first + t) + j], recv_sem=recv_ref.at[3 * (first + t) + j],
                    device_id=(ox, oy, c), device_id_type=MESH)
                cp.wait_send()
                cp.wait_recv()

    out = pl.pallas_call(
        body, name=name, in_specs=[HBM_SPEC] * nt + [SEM_SPEC, SEM_SPEC] + [HBM_SPEC] * len(extra),
        out_specs=[HBM_SPEC] * nt, out_shape=[pltpu.HBM(f.shape, f.dtype) for f in fulls],
        input_output_aliases={t: t for t in range(nt)}, compiler_params=_split_params())(*fulls, send_sems, recv_sems, *extra)
    return list(out)


def forward_halves(name, fulls, kinds, shard_shapes):
    nt = len(fulls)

    def body(*refs):
        out_refs = refs[nt:2 * nt]
        send_sems, recv_sems = refs[2 * nt:]
        x, y, c, others = _position()
        cps = []
        for t in range(nt):
            for j, (ox, oy) in enumerate(others):
                landed = _window(out_refs[t], kinds[t], 2 * ox + oy, c, shard_shapes[t])
                cp = pltpu.make_async_remote_copy(
                    src_ref=landed, dst_ref=landed, send_sem=send_sems.at[3 * t + j], recv_sem=recv_sems.at[3 * t + j],
                    device_id=(x, y, 1 - c), device_id_type=MESH)
                cp.start()
                cps.append(cp)
        for t in range(nt):
            for j, (ox, oy) in enumerate(others):
                got = _window(out_refs[t], kinds[t], 2 * ox + oy, 1 - c, shard_shapes[t])
                pltpu.make_async_remote_copy(
                    src_ref=got, dst_ref=got, send_sem=send_sems.at[3 * t + j], recv_sem=recv_sems.at[3 * t + j],
                    device_id=(x, y, 1 - c), device_id_type=MESH).wait_recv()
        for cp in cps:
            cp.wait_send()

    out = pl.pallas_call(
        body, in_specs=[ANY_SPEC] * nt, out_specs=[ANY_SPEC] * nt, out_shape=[_sds(f.shape, f.dtype) for f in fulls],
        input_output_aliases={t: t for t in range(nt)},
        scratch_shapes=[pltpu.SemaphoreType.DMA((3 * nt,)), pltpu.SemaphoreType.DMA((3 * nt,))],
        name=name, compiler_params=_params())(*fulls)
    return list(out)


def forward_start(name, send_sems, recv_sems, fulls, kinds, shard_shapes, after, first):
    nt = len(fulls)

    def body(*refs):
        full_refs, ici_send, ici_recv = refs[:nt], refs[nt], refs[nt + 1]
        send_ref, recv_ref, token = refs[nt + 3], refs[nt + 4], refs[-1]
        x, y, c, others = _position()
        for t in range(nt):
            mine = _window(full_refs[t], kinds[t], 2 * x + y, c, shard_shapes[t])
            for j, (ox, oy) in enumerate(others):
                landed = _window(full_refs[t], kinds[t], 2 * ox + oy, c, shard_shapes[t])
                cp = pltpu.make_async_remote_copy(
                    src_ref=mine, dst_ref=landed, send_sem=ici_send.at[3 * (first + t) + j],
                    recv_sem=ici_recv.at[3 * (first + t) + j], device_id=(ox, oy, c), device_id_type=MESH)
                cp.wait_send()
                cp.wait_recv()
                pltpu.make_async_remote_copy(
                    src_ref=landed, dst_ref=landed, send_sem=send_ref.at[3 * t + j], recv_sem=recv_ref.at[3 * t + j],
                    device_id=(x, y, 1 - c), device_id_type=MESH).start()
        token[...] = jnp.zeros_like(token)

    sems = pltpu.SemaphoreType.DMA((3 * nt,))
    out = pl.pallas_call(
        body, name=name, in_specs=[HBM_SPEC] * nt + [SEM_SPEC, SEM_SPEC, HBM_SPEC],
        out_specs=(SEM_SPEC, SEM_SPEC, *[HBM_SPEC] * nt, pl.BlockSpec(memory_space=pltpu.VMEM)),
        out_shape=(sems, sems, *[pltpu.HBM(f.shape, f.dtype) for f in fulls], _sds((8, 128), F32)),
        input_output_aliases={t: 2 + t for t in range(nt)}, compiler_params=_split_params(),
    )(*fulls, send_sems, recv_sems, _in_hbm(after))
    return out[0], out[1], list(out[2:2 + nt]), out[-1]


def forward_wait(name, send_sems, recv_sems, fulls, kinds, shard_shapes, after):
    nt = len(fulls)

    def body(*refs):
        full_refs, send_ref, recv_ref = refs[:nt], refs[nt], refs[nt + 1]
        x, y, c, others = _position()
        for t in range(nt):
            for j, (ox, oy) in enumerate(others):
                cp = pltpu.make_async_remote_copy(
                    src_ref=_window(full_refs[t], kinds[t], 2 * ox + oy, c, shard_shapes[t]),
                    dst_ref=_window(full_refs[t], kinds[t], 2 * ox + oy, 1 - c, shard_shapes[t]),
                    send_sem=send_ref.at[3 * t + j], recv_sem=recv_ref.at[3 * t + j],
                    device_id=(x, y, 1 - c), device_id_type=MESH)
                cp.wait_send()
                cp.wait_recv()

    return list(pl.pallas_call(
        body, name=name, in_specs=[HBM_SPEC] * nt + [SEM_SPEC, SEM_SPEC, HBM_SPEC], out_specs=[HBM_SPEC] * nt,
        out_shape=[pltpu.HBM(f.shape, f.dtype) for f in fulls], input_output_aliases={t: t for t in range(nt)},
        compiler_params=_split_params())(*fulls, send_sems, recv_sems, _in_hbm(after)))


def _piece(ref, kind, chip, shard_shape):
    r, n = shard_shape
    if kind == "col":
        return ref.at[:, pl.ds(pl.multiple_of(chip * n, 128), n)]
    return ref.at[pl.ds(pl.multiple_of(chip * r, 16), r), :]


def _piece_shape(kind, shard_shape):
    r, n = shard_shape
    return (r // 2, n) if kind == "col" else (r, n // 2)


def exchange_start(name, parts, kinds, shard_shapes):
    nt = len(parts)
    lands = [lax.empty((3,) + _piece_shape(kinds[t], shard_shapes[t]), BF16) for t in range(nt)]

    def body(*refs):
        part_refs, land_refs = refs[:nt], refs[nt:2 * nt]
        send_sems, recv_sems, token = refs[2 * nt], refs[2 * nt + 1], refs[-1]
        x, y, c, others = _position()
        for t in range(nt):
            for j, (ox, oy) in enumerate(others):
                pltpu.make_async_remote_copy(
                    src_ref=_piece(part_refs[t], kinds[t], 2 * ox + oy, shard_shapes[t]), dst_ref=land_refs[t].at[j],
                    send_sem=send_sems.at[3 * t + j], recv_sem=recv_sems.at[3 * t + j],
                    device_id=(ox, oy, c), device_id_type=MESH).start()
        token[...] = jnp.zeros_like(token)

    sems = pltpu.SemaphoreType.DMA((3 * nt,))
    both = list(parts) + lands
    out = pl.pallas_call(
        body, name=name, in_specs=[HBM_SPEC] * (2 * nt),
        out_specs=(SEM_SPEC, SEM_SPEC, *[HBM_SPEC] * (2 * nt), pl.BlockSpec(memory_space=pltpu.VMEM)),
        out_shape=(sems, sems, *[pltpu.HBM(a.shape, a.dtype) for a in both], _sds((8, 128), F32)),
        input_output_aliases={t: 2 + t for t in range(2 * nt)}, compiler_params=_split_params(),
    )(*[_in_hbm(a) for a in both])
    return out[0], out[1], list(out[2:2 + nt]), list(out[2 + nt:2 + 2 * nt]), out[-1]


def exchange_wait(name, send_sems, recv_sems, parts, lands, kinds, shard_shapes, after):
    nt = len(parts)

    def body(*refs):
        part_refs, land_refs = refs[:nt], refs[nt:2 * nt]
        send_ref, recv_ref = refs[2 * nt], refs[2 * nt + 1]
        x, y, c, others = _position()
        for t in range(nt):
            for j, (ox, oy) in enumerate(others):
                cp = pltpu.make_async_remote_copy(
                    src_ref=_piece(part_refs[t], kinds[t], 2 * ox + oy, shard_shapes[t]), dst_ref=land_refs[t].at[j],
                    send_sem=send_ref.at[3 * t + j], recv_sem=recv_ref.at[3 * t + j],
                    device_id=(ox, oy, c), device_id_type=MESH)
                cp.wait_send()
                cp.wait_recv()

    both = list(parts) + list(lands)
    out = pl.pallas_call(
        body, name=name, in_specs=[HBM_SPEC] * (2 * nt) + [SEM_SPEC, SEM_SPEC, HBM_SPEC], out_specs=[HBM_SPEC] * (2 * nt),
        out_shape=[pltpu.HBM(a.shape, a.dtype) for a in both], input_output_aliases={t: t for t in range(2 * nt)},
        compiler_params=_split_params())(*both, send_sems, recv_sems, _in_hbm(after))
    return list(out[:nt]), list(out[nt:])


def all_reduce_small(name, bufs, wire):
    n = len(bufs)
    halves = [b.shape[0] // 2 for b in bufs]

    def body(*refs):
        in_refs, out_refs, lands, txs = refs[:n], refs[n:2 * n], refs[2 * n:3 * n], refs[3 * n:4 * n]
        send_sems, recv_sems = refs[4 * n:]
        x, y, c, _ = _position()
        mine = [pl.ds(pl.multiple_of(c * h, 8), h) for h in halves]
        other = [pl.ds(pl.multiple_of((1 - c) * h, 8), h) for h in halves]
        for s, peer in enumerate([(x, y, 1 - c), (1 - x, y, c), (x, 1 - y, c)]):
            cps = []
            for k in range(n):
                txs[k][...] = (in_refs[k][other[k], :] if s == 0 else out_refs[k][mine[k], :]).astype(wire[k])
                cp = pltpu.make_async_remote_copy(
                    src_ref=txs[k], dst_ref=lands[k].at[s], send_sem=send_sems.at[4 * k + s], recv_sem=recv_sems.at[4 * k + s],
                    device_id=peer, device_id_type=MESH)
                cp.start()
                cps.append(cp)
            for k, cp in enumerate(cps):
                cp.wait()
                own = in_refs[k][mine[k], :] if s == 0 else out_refs[k][mine[k], :]
                out_refs[k][mine[k], :] = own.astype(wire[k]).astype(F32) + lands[k][s].astype(F32)
        cps = []
        for k in range(n):
            cp = pltpu.make_async_remote_copy(
                src_ref=out_refs[k].at[mine[k]], dst_ref=out_refs[k].at[mine[k]], send_sem=send_sems.at[4 * k + 3],
                recv_sem=recv_sems.at[4 * k + 3], device_id=(x, y, 1 - c), device_id_type=MESH)
            cp.start()
            cps.append(cp)
        for cp in cps:
            cp.wait()

    vm = pl.BlockSpec(memory_space=pltpu.VMEM)
    out = pl.pallas_call(
        body, in_specs=[vm] * n, out_specs=[vm] * n, out_shape=[_sds(b.shape, F32) for b in bufs],
        scratch_shapes=[pltpu.VMEM((3, h, b.shape[1]), w) for h, b, w in zip(halves, bufs, wire)]
        + [pltpu.VMEM((h, b.shape[1]), w) for h, b, w in zip(halves, bufs, wire)]
        + [pltpu.SemaphoreType.DMA((4 * n,)), pltpu.SemaphoreType.DMA((4 * n,))],
        name=name, compiler_params=_params())(*bufs)
    return list(out)


def _local_step(x, target, small, need, ahead, emit_swap, emit_exchange):
    d = D_MODEL
    full = {}

    def after_token(vec, token):
        return vec if token is None else vec + token[0:1, 0:1]

    def token_rows(token, width):
        return [] if token is None else [after_token(jnp.zeros((1, width), F32), token)]

    def plus(acc, rows):
        return acc + rows[0] if rows else acc

    rb16, rbt16, rc16, rct16, lr_t, li_t = small["s5_operands"]
    ge, y2, cs = s5_fwd(x, small["norm_mix0"], small["s5_d"], rb16, rc16, lr_t, li_t)
    full.update(need("glu", ge))

    def norm_rows(h, gains):
        xh, _ = _rms_hat(h)
        return [xh * g for g in gains]

    def glu_epilogue(accs, e, r):
        v, gt = accs[0] + r[0], accs[1] + r[1]
        h = e[0] + v * jax.nn.sigmoid(gt)
        return [h, v, gt] + norm_rows(h, r[2:])

    gain_mlp0 = after_token(small["norm_mlp0"], ahead("mlp_in0", ge))
    h1, val, gate, n1 = mm_nn(
        "glu", ge, full["w_glu"], [0, d], d, glu_epilogue, [F32, F32, F32, BF16], extras=[x],
        rowvecs=[(small["s5_b_glu"], 0), (small["s5_b_glu"], d), (gain_mlp0, 0)], tm=512, tn=d)

    def mlp_fwd(tag, h, n, w_in, get_w_out, next_gains, head=None):
        def in_epilogue(accs, e, rv):
            pos = jnp.maximum(accs[0], 0.0)
            return [pos * pos, 2.0 * pos]

        r, slope = mm_nn("mlp_in" + tag, n, w_in, [0], w_in.shape[1], in_epilogue, [BF16, BF16], tm=2048)
        w_out = get_w_out(r)

        def epilogue(accs, e, rv):
            h_out = e[0] + accs[0]
            return [h_out] + norm_rows(h_out, rv)

        if head is not None:
            return head(r, w_out, h), (n, r, slope)
        outs = mm_nn("mlp_out" + tag, r, w_out, [0], d, epilogue, [F32] + [BF16] * len(next_gains), extras=[h],
                     rowvecs=[(g, 0) for g in next_gains], tm=512, tn=d)
        return outs[0], outs[1:], (n, r, slope)

    full.update(need("mlp_in0", h1))

    def w_out0(after):
        full.update(need("mlp_out0", after))
        return full["w_out0"]

    h2, (nkv, n2), mlp0 = mlp_fwd("0", h1, n1, full["w_in0"], w_out0, [small["norm_kv"], small["norm_mix1"]])

    full.update(need("attn", h2))
    kvw = 2 * N_KV * HEAD_DIM
    (kv,) = mm_nn("kv_proj", nkv, full["w_kv"], [0], kvw, lambda accs, e, r: [accs[0] + r[0]], [BF16],
                  rowvecs=[(small["b_kv"], 0)], tm=2048)
    (q,) = mm_nn("q_proj", n2, full["w_q"], [0], d, lambda accs, e, r: [accs[0] + r[0]], [BF16],
                 rowvecs=[(small["b_q"], 0)], tm=2048)
    sinks = small["sinks"].reshape(N_Q)
    o = attn_fwd(q, kv, sinks)
    def o_epilogue(accs, e, r):
        h_out = e[0] + accs[0] + r[0]
        return [h_out] + norm_rows(h_out, r[1:])

    bias_o = after_token(small["b_o"], ahead("mlp_in1", o))
    h3, n3 = mm_nn("o_proj", o, full["w_o"], [0], d, o_epilogue, [F32, BF16], extras=[h2],
                   rowvecs=[(bias_o, 0), (small["norm_mlp1"], 0)], tm=512, tn=d)
    full.update(need("mlp_in1", h3))

    def w_out1(after):
        full.update(need("mlp_out1", after))
        return full["w_out1"]

    def loss_head(r, w_out, h):
        def epilogue(accs, e, rv):
            xh, rr = _rms_hat(e[0] + accs[0])
            err = xh * rv[0] - e[1]
            dy = err * (1.0 / d)
            dxh = dy * rv[0]
            dx = rr * (dxh - xh * jnp.mean(dxh * xh, axis=-1, keepdims=True))
            loss = jnp.full((1, d), 0.5 * jnp.sum(jnp.mean(err * err, axis=-1, keepdims=True)), F32)
            return [dx, dx, loss, jnp.sum(dy * xh, axis=0, keepdims=True)]

        return mm_nn("mlp_out1", r, w_out, [0], d, epilogue, [F32, BF16], extras=[h, target],
                     rowvecs=[(small["norm_final"], 0)], n_sums=2, tm=512, tn=d)

    (dh, dhb, loss_tile, dg_final), mlp1 = mlp_fwd("1", h3, n3, full["w_in1"], w_out1, [], head=loss_head)

    grads_small, grads_full = {"norm_final": dg_final}, {}
    ident = lambda acc, e, r: [plus(acc, r)]
    layer1 = ["w_out1", "w_in1", "w_o", "w_q", "w_kv"]
    layer0 = ["w_out0", "w_in0", "w_glu"]

    def norm_bwd_rows(x_rows, res, dys, gains):
        xh, r = _rms_hat(x_rows)
        dxh = sum(dy * g for dy, g in zip(dys, gains))
        dx = r * (dxh - xh * jnp.mean(dxh * xh, axis=-1, keepdims=True)) + res
        return dx, [jnp.sum(dy * xh, axis=0, keepdims=True) for dy in dys]

    def mlp_bwd(tag, dh, dhb, h_in, gain, w_in, w_out, saved, token=None):
        n, r, slope = saved
        grads_full["w_out" + tag] = mm_tn("dw_out" + tag, r, dhb, tn=1024)
        (da,) = mm_nt("mlp_da" + tag, dhb, w_out, lambda acc, e, rv: [plus(acc * e[0].astype(F32), rv)], [BF16],
                      extras=[slope], rowvecs=token_rows(token, w_out.shape[0]), tm=2048)
        grads_full["w_in" + tag] = mm_tn("dw_in" + tag, n, da, tn=1024)

        def epilogue(acc, e, rv):
            dx, dgs = norm_bwd_rows(e[0], e[1], [acc], rv)
            return [dx, dx, jnp.sum(dx, axis=0, keepdims=True)] + dgs

        dx, dxb, colsum, dg = mm_nt("mlp_dn" + tag, da, w_in, epilogue, [F32, BF16], extras=[h_in, dh], rowvecs=[gain],
                                    n_sums=2, tm=512, tk=d)
        grads_small["norm_mlp" + tag] = dg
        return dx, dxb, colsum

    dh3, dh3b, colsum3 = mlp_bwd("1", dh, dhb, h3, small["norm_mlp1"], full["w_in1"], full["w_out1"], mlp1)
    grads_small["b_o"] = colsum3
    grads_full["w_o"] = mm_tn("dw_o", o, dh3b, tn=1024)
    (do,) = mm_nt("attn_do", dh3b, full["w_o"], ident, [BF16], tm=2048)
    dq, dbq, dprev, dcur, dsink = attn_bwd(q, kv, do, sinks)
    dkv, dbkv = kv_combine(dprev, dcur)
    grads_small["b_q"], grads_small["b_kv"], grads_small["sinks"] = dbq, dbkv, dsink
    grads_full["w_q"] = mm_tn("dw_q", n2, dq, tn=1024)
    grads_full["w_kv"] = mm_tn("dw_kv", nkv, dkv, tk=1024)
    token = emit_swap("layer1", {n: grads_full[n] for n in layer1})
    (dnkv,) = mm_nt("kv_dn", dkv, full["w_kv"], ident, [F32], rowvecs=token_rows(token, d), tm=2048, tk=1024)

    def attn_dn_epilogue(acc, e, rv):
        dx, dgs = norm_bwd_rows(e[0], e[1], [acc, e[2]], rv)
        return [dx, dx] + dgs

    dh2, dh2b, dg_mix1, dg_kv = mm_nt("attn_dn", dq, full["w_q"], attn_dn_epilogue, [F32, BF16], extras=[h2, dh3, dnkv],
                                      rowvecs=[small["norm_mix1"], small["norm_kv"]], n_sums=2, tm=512, tk=d)
    grads_small["norm_mix1"], grads_small["norm_kv"] = dg_mix1, dg_kv
    token = emit_exchange("layer1", dh2b)
    dh1, _, _ = mlp_bwd("0", dh2, dh2b, h1, small["norm_mlp0"], full["w_in0"], full["w_out0"], mlp0, token)

    dz, db_glu = glu_bwd(dh1, val, gate)
    grads_small["s5_b_glu"] = db_glu
    grads_full["w_glu"] = mm_tn("dw_glu", ge, dz, tn=1024)
    token = emit_swap("layer0", {n: grads_full[n] for n in layer0})
    (dy2,) = mm_nt("glu_dy", dz, full["w_glu"], lambda acc, e, rv: [plus(acc, rv) * _gelu_grad(e[0])], [F32], extras=[y2],
                   rowvecs=token_rows(token, d), tm=1024, tk=1024)
    token = emit_exchange("layer0", dy2)
    grad_x, dd, drb, drc, dlr, dli, dg_mix0 = s5_bwd(x, small["norm_mix0"], dy2, dh1, after_token(small["s5_d"], token), cs,
                                                     rb16, rbt16, rct16, lr_t, li_t)
    grads_small["s5_d"] = dd
    grads_small["s5_mats"] = (drb, drc, dlr, dli)
    grads_small["norm_mix0"] = dg_mix0
    return loss_tile, grad_x, grads_small


SMALL_NAMES = ["norm_mix", "norm_mlp", "norm_kv", "norm_final", "s5_a_re", "s5_a_im", "s5_log_dt", "s5_b_re", "s5_b_im",
               "s5_c_re", "s5_c_im", "s5_d", "s5_b_glu", "b_kv", "b_q", "sinks", "b_o"]
BIG_NAMES = ["s5_w_glu", "w_kv", "w_q", "w_o", "w_mlp_in", "w_mlp_out"]
WEIGHT_ORDER = ["norm_mix", "norm_mlp", "norm_kv", "norm_final", "s5_a_re", "s5_a_im", "s5_log_dt", "s5_b_re", "s5_b_im",
                "s5_c_re", "s5_c_im", "s5_d", "s5_w_glu", "s5_b_glu", "w_kv", "b_kv", "w_q", "b_q", "sinks", "w_o", "b_o",
                "w_mlp_in", "w_mlp_out"]


def kernel(x, norm_mix, norm_mlp, norm_kv, norm_final, s5_a_re, s5_a_im, s5_log_dt, s5_b_re, s5_b_im, s5_c_re, s5_c_im, s5_d, s5_w_glu, s5_b_glu, w_kv, b_kv, w_q, b_q, sinks, w_o, b_o, w_mlp_in, w_mlp_out, loss_target, m_norm_mix, m_norm_mlp, m_norm_kv, m_norm_final, m_s5_a_re, m_s5_a_im, m_s5_log_dt, m_s5_b_re, m_s5_b_im, m_s5_c_re, m_s5_c_im, m_s5_d, m_s5_w_glu, m_s5_b_glu, m_w_kv, m_b_kv, m_w_q, m_b_q, m_sinks, m_w_o, m_b_o, m_w_mlp_in, m_w_mlp_out, v_norm_mix, v_norm_mlp, v_norm_kv, v_norm_final, v_s5_a_re, v_s5_a_im, v_s5_log_dt, v_s5_b_re, v_s5_b_im, v_s5_c_re, v_s5_c_im, v_s5_d, v_s5_w_glu, v_s5_b_glu, v_w_kv, v_b_kv, v_w_q, v_b_q, v_sinks, v_w_o, v_b_o, v_w_mlp_in, v_w_mlp_out):
    env = dict(locals())
    w = {n: env[n] for n in WEIGHT_ORDER}
    mom = {n: env["m_" + n] for n in WEIGHT_ORDER}
    var = {n: env["v_" + n] for n in WEIGHT_ORDER}
    d = D_MODEL
    xi, yi, ci = lax.axis_index("x"), lax.axis_index("y"), lax.axis_index("c")
    chip = 2 * xi + yi
    where = jnp.stack([ci, chip]).astype(jnp.int32)

    dsh, bsh = s5_d.shape[1], s5_b_glu.shape[1]
    packed = jnp.concatenate([s5_d.reshape(-1, 128), s5_b_glu.reshape(-1, 128)])
    n_d, n_b = dsh // 128, bsh // 128
    slab = lax.dynamic_update_slice(jnp.zeros((4, 8, 128), F32), jnp.pad(packed, ((0, 8 - n_d - n_b), (0, 0)))[None],
                                    (chip, 0, 0))

    big = [s5_w_glu, w_kv[None], w_q, w_o, w_mlp_in, w_mlp_out]
    entries = [(0, 0, "col"), (1, 0, "row"), (2, 0, "row"), (3, 0, "row"), (4, 0, "col"), (4, 1, "col"),
               (5, 0, "row"), (5, 1, "row")]
    names = ["w_glu", "w_kv", "w_q", "w_o", "w_in0", "w_in1", "w_out0", "w_out1"]
    kinds = dict(zip(names, [k for _, _, k in entries]))
    shard_shapes = dict(zip(names, [tuple(big[a].shape[1:]) for a, _, _ in entries]))

    placed_w = dict(zip(names, cast_place(big, entries, where)))
    placed_w["vectors"], kinds["vectors"], shard_shapes["vectors"] = slab, "slab", None
    gather_groups = {"glu": ["w_glu"], "mlp_in0": ["w_in0"], "mlp_out0": ["w_out0"], "attn": ["w_kv", "w_q", "w_o"],
                     "mlp_in1": ["w_in1"], "mlp_out1": ["w_out1"]}
    order = ["vectors"] + [n for members in gather_groups.values() for n in members]
    send, recv, thru, token = gather_start([placed_w[n] for n in order], [kinds[n] for n in order],
                                           [shard_shapes[n] for n in order])
    started = dict(zip(order, thru))
    (gathered_rows,) = gather_wait("gather_wait_vectors", send, recv, [started["vectors"]], ["slab"], [None], None, 0)
    d_full = gathered_rows[:, 0:n_d].reshape(1, -1)
    bglu_full = gathered_rows[:, n_d:n_d + n_b].reshape(1, -1)

    forwarding = {}

    def ahead(group, after):
        members = gather_groups[group]
        ks, shapes = [kinds[n] for n in members], [shard_shapes[n] for n in members]
        d2d_send, d2d_recv, landed, tok = forward_start(
            "forward_start_" + group, send, recv, [started[n] for n in members], ks, shapes, after, order.index(members[0]))
        forwarding[group] = (d2d_send, d2d_recv, landed)
        return tok

    def need(group, after):
        members = gather_groups[group]
        ks, shapes = [kinds[n] for n in members], [shard_shapes[n] for n in members]
        if group in forwarding:
            return dict(zip(members, forward_wait("forward_wait_" + group, *forwarding[group], ks, shapes, after)))
        landed = gather_wait("gather_wait_" + group, send, recv, [started[n] for n in members], ks, shapes, after,
                             order.index(members[0]))
        return dict(zip(members, forward_halves("forward_halves_" + group, landed, ks, shapes)))

    swapping, exchanging = {}, {}

    def emit_swap(group, partial):
        members = list(partial)
        send, recv, mine, lands, tok = swap_start("swap_start_" + group, [partial[n] for n in members],
                                                  [kinds[n] for n in members])
        swapping[group] = (members, send, recv, mine, lands)
        return tok

    def emit_exchange(group, after):
        members, send, recv, mine, lands = swapping[group]
        ks, shapes = [kinds[n] for n in members], [shard_shapes[n] for n in members]
        mine, landed = swap_wait("swap_wait_" + group, send, recv, mine, lands, ks, after)
        sums = add_halves("add_halves_" + group, mine, landed, ks, where)
        send, recv, parts, lands, tok = exchange_start("exchange_start_" + group, sums, ks, shapes)
        exchanging[group] = (members, send, recv, parts, lands)
        return tok

    s5_args = (s5_a_re[0], s5_a_im[0], s5_log_dt[0], s5_b_re[0], s5_b_im[0])
    small = {
        "norm_mix0": norm_mix[0:1] + token[0:1, 0:1], "norm_mix1": norm_mix[1:2], "norm_mlp0": norm_mlp[0:1], "norm_mlp1": norm_mlp[1:2],
        "norm_kv": norm_kv.reshape(1, d), "norm_final": norm_final.reshape(1, d), "s5_operands": s5_prep(*s5_args, s5_c_re[0], s5_c_im[0]),
        "s5_d": d_full, "s5_b_glu": bglu_full,
        "b_kv": b_kv.reshape(1, -1), "b_q": b_q, "sinks": sinks, "b_o": b_o,
    }
    loss_row, grad_x, gs = _local_step(x[0], loss_target[0], small, need, ahead, emit_swap, emit_exchange)

    reduced = [None] * len(big)
    where_of = dict(zip(names, entries))
    for group in ("layer1", "layer0"):
        members, send, recv, parts, lands = exchanging[group]
        ks, shapes = [kinds[n] for n in members], [shard_shapes[n] for n in members]
        parts, lands = exchange_wait("exchange_wait_" + group, send, recv, parts, lands, ks, shapes, grad_x)
        targets = [where_of[n][0] for n in members]
        sums = sum_shards("sum_shards_" + group, parts, lands, ks, shapes, where, [where_of[n][1] for n in members],
                          [big[a].shape[0] for a in targets], [reduced[a] for a in targets])
        for a, arr in zip(targets, sums):
            reduced[a] = arr
    share_send, share_recv, reduced, shared = share_start(reduced, entries)

    mats, lams = s5_compact(*gs["s5_mats"])
    rows = [gs["norm_mix0"], gs["norm_mix1"], gs["norm_mlp0"], gs["norm_mlp1"], gs["norm_kv"], gs["norm_final"], gs["s5_d"],
            gs["b_q"], gs["b_o"], gs["s5_b_glu"], gs["b_kv"], gs["sinks"], loss_row, jnp.zeros((2, d), F32) + shared[0:1, 0:1]]
    vecs, lams, mats = all_reduce_small("reduce_small", [jnp.concatenate(rows, axis=0), lams, mats], [F32, F32, BF16])
    grads = split_vectors(where, vecs, dsh, bsh)
    loss = grads.pop("loss")[0, 0]
    g_are, g_aim, g_dt, g_bre, g_bim, dc_re, dc_im = s5_param_bwd(mats, lams, *s5_args)
    grads.update({"s5_a_re": g_are[None], "s5_a_im": g_aim[None], "s5_log_dt": g_dt[None], "s5_b_re": g_bre[None],
                  "s5_b_im": g_bim[None], "s5_c_re": dc_re[None], "s5_c_im": dc_im[None]})

    delta, new_m, new_v = {}, {}, {}

    def view(n, a):
        return a.reshape(1, -1) if a.ndim == 1 else jnp.swapaxes(a, -1, -2) if n in ("s5_b_re", "s5_b_im") else a

    sw, sg, sm, sv = ([view(n, t[n]) for n in SMALL_NAMES] for t in (w, grads, mom, var))
    for n, a, b, c_ in zip(SMALL_NAMES, *adamw_native("adamw_small", sw, sg, sm, sv)):
        delta[n], new_m[n], new_v[n] = (view(n, t) if t.ndim == 4 else t for t in (a, b, c_))

    reduced = share_wait(share_send, share_recv, reduced, entries, new_v["s5_c_re"])
    for n, g in zip(BIG_NAMES, reduced):
        grads[n] = g.reshape(w[n].shape)
    flat = lambda t: [t[n].reshape(-1, t[n].shape[-1]) for n in BIG_NAMES]
    for table, arrays in zip((grads, delta, new_m, new_v), adamw("adamw_big", flat(w), flat(grads), flat(mom), flat(var))):
        for n, a in zip(BIG_NAMES, arrays):
            table[n] = a.reshape(w[n].shape)

    out = [loss.reshape(()), grad_x[None]]
    for table in (grads, delta, new_m, new_v):
        out += [table[n].reshape(w[n].shape) for n in WEIGHT_ORDER]
    return tuple(out)
```

```python
import math

import jax
import jax.numpy as jnp
from jax import lax
from jax.experimental import pallas as pl
from jax.experimental.pallas import tpu as pltpu

F32 = jnp.float32
BF16 = jnp.bfloat16

D_MODEL = 1024
S5_GROUPS = 64
S5_GROUP = 16
S5_STATE = 64
N_KV = 4
N_Q = 16
HEAD_DIM = 64
BLOCK = 128
NORM_EPS = 1e-5
LAMBDA_RE_MAX = -1e-4
ADAM_LR, ADAM_B1, ADAM_B2, ADAM_EPS, ADAM_WD, ADAM_STEP = 0.001, 0.9, 0.999, 1e-08, 0.01, 10

VMEM_LIMIT_BYTES = 56 * 1024 * 1024
S5_CHUNK = 256
S5_BLOCKS = 4
MESH = pl.DeviceIdType.MESH


def _params(sem=None):
    return pltpu.CompilerParams(dimension_semantics=sem, vmem_limit_bytes=VMEM_LIMIT_BYTES)


def _sds(shape, dtype):
    return jax.ShapeDtypeStruct(shape, dtype)


def _rms_hat(xv):
    r = lax.rsqrt(jnp.mean(xv * xv, axis=-1, keepdims=True) + NORM_EPS)
    return xv * r, r


def mm_nn(name, a, w, col_offsets, n_out, epilogue, out_dtypes, extras=(), rowvecs=(), n_sums=0, tm=1024, tn=512):
    m, k = a.shape
    tm, tn = min(tm, m), min(tn, n_out)
    nw, ne, nr, no = len(col_offsets), len(extras), len(rowvecs), len(out_dtypes)

    def body(a_ref, *refs):
        w_refs, e_refs, r_refs = refs[:nw], refs[nw:nw + ne], refs[nw + ne:nw + ne + nr]
        o_refs, s_refs = refs[nw + ne + nr:nw + ne + nr + no], refs[nw + ne + nr + no:]
        av = a_ref[...]
        accs = [jnp.dot(av, w_ref[...], preferred_element_type=F32) for w_ref in w_refs]
        outs = epilogue(accs, [e[...] for e in e_refs], [r[...] for r in r_refs])
        for o_ref, o in zip(o_refs, outs[:no]):
            o_ref[...] = o.astype(o_ref.dtype)
        if n_sums:
            @pl.when(pl.program_id(1) == 0)
            def _():
                for s_ref in s_refs:
                    s_ref[...] = jnp.zeros_like(s_ref)

            for s_ref, val in zip(s_refs, outs[no:]):
                s_ref[...] += val

    def wspec(off):
        return pl.BlockSpec((k, tn), lambda j, i, off=off: (0, off // tn + j))

    def rspec(off):
        return pl.BlockSpec((1, tn), lambda j, i, off=off: (0, off // tn + j))

    tile = pl.BlockSpec((tm, tn), lambda j, i: (i, j))
    in_specs = ([pl.BlockSpec((tm, k), lambda j, i: (i, 0))] + [wspec(o) for o in col_offsets]
                + [tile] * ne + [rspec(o) for _, o in rowvecs])
    sem = ("parallel", "arbitrary") if n_sums else ("parallel", "parallel")
    return pl.pallas_call(
        body, grid=(n_out // tn, m // tm), in_specs=in_specs,
        out_specs=[tile] * no + [pl.BlockSpec((1, tn), lambda j, i: (0, j))] * n_sums,
        out_shape=[_sds((m, n_out), dt) for dt in out_dtypes] + [_sds((1, n_out), F32)] * n_sums, name=name,
        compiler_params=_params(sem))(a, *([w] * nw), *extras, *[r for r, _ in rowvecs])


def mm_nt(name, g, w, epilogue, out_dtypes, extras=(), rowvecs=(), n_sums=0, tm=512, tk=512):
    m, n = g.shape
    k = w.shape[0]
    tm, tk = min(tm, m), min(tk, k)
    ne, nr, no = len(extras), len(rowvecs), len(out_dtypes)

    def body(g_ref, w_ref, *refs):
        e_refs, r_refs, o_refs, s_refs = refs[:ne], refs[ne:ne + nr], refs[ne + nr:ne + nr + no], refs[ne + nr + no:]
        acc = lax.dot_general(g_ref[...], w_ref[...], (((1,), (1,)), ((), ())), preferred_element_type=F32)
        outs = epilogue(acc, [e[...] for e in e_refs], [r[...] for r in r_refs])
        for o_ref, o in zip(o_refs, outs[:no]):
            o_ref[...] = o.astype(o_ref.dtype)
        if n_sums:
            @pl.when(pl.program_id(0) == 0)
            def _():
                for s_ref in s_refs:
                    s_ref[...] = jnp.zeros_like(s_ref)

            for s_ref, val in zip(s_refs, outs[no:]):
                s_ref[...] += val

    tile = pl.BlockSpec((tm, tk), lambda i, j: (i, j))
    vec = pl.BlockSpec((1, tk), lambda i, j: (0, j))
    sem = ("arbitrary", "parallel") if n_sums else ("parallel", "parallel")
    return pl.pallas_call(
        body, grid=(m // tm, k // tk),
        in_specs=[pl.BlockSpec((tm, n), lambda i, j: (i, 0)), pl.BlockSpec((tk, n), lambda i, j: (j, 0))]
        + [tile] * ne + [vec] * nr,
        out_specs=[tile] * no + [vec] * n_sums,
        out_shape=[_sds((m, k), dt) for dt in out_dtypes] + [_sds((1, k), F32)] * n_sums, name=name,
        compiler_params=_params(sem))(g, w, *extras, *rowvecs)


def mm_tn(name, a, g, tk=512, tn=512):
    m, k = a.shape
    n = g.shape[1]
    tk, tn = min(tk, k), min(tn, n)

    def body(a_ref, g_ref, o_ref):
        acc = lax.dot_general(a_ref[...], g_ref[...], (((0,), (0,)), ((), ())), preferred_element_type=F32)
        o_ref[...] = acc.astype(o_ref.dtype)

    return pl.pallas_call(
        body, grid=(k // tk, n // tn),
        in_specs=[pl.BlockSpec((m, tk), lambda i, j: (0, i)), pl.BlockSpec((m, tn), lambda i, j: (0, j))],
        out_specs=pl.BlockSpec((tk, tn), lambda i, j: (i, j)), out_shape=_sds((k, n), BF16), name=name,
        compiler_params=_params(("parallel", "parallel")))(a, g)


def _row_mask(tc):
    row = lax.broadcasted_iota(jnp.int32, (8 * tc, 256), 0) % 8
    col = lax.broadcasted_iota(jnp.int32, (8 * tc, 256), 1) // 32
    return row == col


def _expand_rows(val, mask):
    tc, width = val.shape
    rep = jnp.broadcast_to(val[:, None, :], (tc, 8, width)).reshape(8 * tc, width)
    return jnp.where(mask, rep, 0.0).astype(BF16)


def _stage(ref, val):
    ref[0] = val[:, 0:128]
    ref[1] = val[:, 128:256]


def _gather_rows(src_ref, tc):
    halves = []
    for half in range(2):
        col = lax.broadcasted_iota(jnp.int32, (tc, 128), 1) // 32 + 4 * half
        out = jnp.zeros((tc, 128), F32)
        for s8 in range(4 * half, 4 * half + 4):
            out = jnp.where(col == s8, src_ref.at[half][pl.ds(s8, tc, stride=8), :], out)
        halves.append(out)
    return jnp.concatenate(halves, axis=1)


def _gelu(x):
    c = math.sqrt(2.0 / math.pi)
    return 0.5 * x * (1.0 + jnp.tanh(c * (x + 0.044715 * x * x * x)))


def _gelu_grad(x):
    c = math.sqrt(2.0 / math.pi)
    t = jnp.tanh(c * (x + 0.044715 * x * x * x))
    return 0.5 * (1.0 + t) + 0.5 * x * (1.0 - t * t) * c * (1.0 + 3.0 * 0.044715 * x * x)


def s5_fwd(x, gain, d_skip, rb, rc, lam_r, lam_i):
    n_rows = x.shape[0]
    tc = min(S5_CHUNK, n_rows)
    nc = n_rows // tc

    def body(x_ref, g_ref, d_ref, rb_ref, rc_ref, lr_ref, li_ref, ge_ref, y2_ref, cs_ref, bux, yrows, carry):
        i = pl.program_id(0)
        u = _rms_hat(x_ref[...])[0] * g_ref[...]

        @pl.when(i == 0)
        def _():
            carry[...] = jnp.zeros_like(carry)

        cs_ref[0] = carry[...]
        mask = _row_mask(tc)
        for blk in range(S5_BLOCKS):
            lhs = _expand_rows(u[:, blk * 256:(blk + 1) * 256], mask)
            bux[blk] = jnp.dot(lhs, rb_ref[blk], preferred_element_type=F32)
        lam = [(lr_ref[blk], li_ref[blk]) for blk in range(S5_BLOCKS)]

        def step(t, c):
            r0 = pl.multiple_of(t * 8, 8)
            new = []
            for blk in range(S5_BLOCKS):
                xr, xi = c[2 * blk], c[2 * blk + 1]
                lr, li = lam[blk]
                nr = lr * xr - li * xi + bux[blk, pl.ds(r0, 8), 0:128]
                ni = lr * xi + li * xr + bux[blk, pl.ds(r0, 8), 128:256]
                bux[blk, pl.ds(r0, 8), 0:128] = nr
                bux[blk, pl.ds(r0, 8), 128:256] = ni
                new += [nr, ni]
            return tuple(new)

        c0 = []
        for blk in range(S5_BLOCKS):
            c0 += [carry[blk, :, 0:128], carry[blk, :, 128:256]]
        cn = lax.fori_loop(0, tc, step, tuple(c0), unroll=4)
        for blk in range(S5_BLOCKS):
            carry[blk, :, 0:128] = cn[2 * blk]
            carry[blk, :, 128:256] = cn[2 * blk + 1]
        for blk in range(S5_BLOCKS):
            _stage(yrows, jnp.dot(bux[blk].astype(BF16), rc_ref[blk], preferred_element_type=F32))
            sl = slice(blk * 256, (blk + 1) * 256)
            y2 = _gather_rows(yrows, tc) + d_ref[:, sl] * u[:, sl]
            y2_ref[:, sl] = y2
            ge_ref[:, sl] = _gelu(y2).astype(BF16)

    row = pl.BlockSpec((tc, D_MODEL), lambda i: (i, 0))
    vec = pl.BlockSpec((1, D_MODEL), lambda i: (0, 0))
    mat = pl.BlockSpec((S5_BLOCKS, 256, 256), lambda i: (0, 0, 0))
    lamspec = pl.BlockSpec((S5_BLOCKS, 8, 128), lambda i: (0, 0, 0))
    return pl.pallas_call(
        body, grid=(nc,),
        in_specs=[row, vec, vec, mat, mat, lamspec, lamspec],
        out_specs=[row, row, pl.BlockSpec((1, S5_BLOCKS, 8, 256), lambda i: (i, 0, 0, 0))],
        out_shape=[_sds((n_rows, D_MODEL), BF16), _sds((n_rows, D_MODEL), F32), _sds((nc, S5_BLOCKS, 8, 256), F32)],
        scratch_shapes=[pltpu.VMEM((S5_BLOCKS, 8 * tc, 256), F32), pltpu.VMEM((2, 8 * tc, 128), F32),
                        pltpu.VMEM((S5_BLOCKS, 8, 256), F32)],
        name="s5_fwd", compiler_params=_params(("arbitrary",)))(x, gain, d_skip, rb, rc, lam_r, lam_i)


def s5_bwd(x, gain, dy2, res, d_skip, cs, rb, rbt, rct, lam_r, lam_i):
    n_rows = x.shape[0]
    tc = min(S5_CHUNK, n_rows)
    nc = n_rows // tc

    def body(x_ref, g_ref, dy_ref, res_ref, d_ref, cs_ref, rb_ref, rbt_ref, rct_ref, lr_ref, li_ref,
             dx_ref, dd_ref, drb_ref, drc_ref, dlr_ref, dli_ref, dg_ref, tmp, du, lhsu, lhsd, xs, adj, acarry):
        i = pl.program_id(0)
        u = _rms_hat(x_ref[...])[0] * g_ref[...]

        @pl.when(i == 0)
        def _():
            acarry[...] = jnp.zeros_like(acarry)
            dd_ref[...] = jnp.zeros_like(dd_ref)
            drb_ref[...] = jnp.zeros_like(drb_ref)
            drc_ref[...] = jnp.zeros_like(drc_ref)
            dlr_ref[...] = jnp.zeros_like(dlr_ref)
            dli_ref[...] = jnp.zeros_like(dli_ref)
            dg_ref[...] = jnp.zeros_like(dg_ref)

        dd_ref[...] += jnp.sum(dy_ref[...] * u, axis=0, keepdims=True)
        mask = _row_mask(tc)
        for blk in range(S5_BLOCKS):
            sl = slice(blk * 256, (blk + 1) * 256)
            lhsu[blk] = _expand_rows(u[:, sl], mask)
            xs[blk] = jnp.dot(lhsu[blk], rb_ref[blk], preferred_element_type=F32)
            lhsd[blk] = _expand_rows(dy_ref[:, sl], mask)
            adj[blk] = jnp.dot(lhsd[blk], rct_ref[blk], preferred_element_type=F32)
        lam = [(lr_ref[blk], li_ref[blk]) for blk in range(S5_BLOCKS)]

        def fstep(t, c):
            r0 = pl.multiple_of(t * 8, 8)
            new = []
            for blk in range(S5_BLOCKS):
                xr, xi = c[2 * blk], c[2 * blk + 1]
                lr, li = lam[blk]
                nr = lr * xr - li * xi + xs[blk, pl.ds(r0, 8), 0:128]
                ni = lr * xi + li * xr + xs[blk, pl.ds(r0, 8), 128:256]
                xs[blk, pl.ds(r0, 8), 0:128] = nr
                xs[blk, pl.ds(r0, 8), 128:256] = ni
                new += [nr, ni]
            return tuple(new)

        c0 = []
        for blk in range(S5_BLOCKS):
            c0 += [cs_ref[0, blk, :, 0:128], cs_ref[0, blk, :, 128:256]]
        lax.fori_loop(0, tc, fstep, tuple(c0), unroll=4)

        def bstep(k, c):
            t = tc - 1 - k
            r0 = pl.multiple_of(t * 8, 8)
            rp = pl.multiple_of(jnp.maximum(t - 1, 0) * 8, 8)
            first = t == 0
            new_a, new_g = [], []
            for blk in range(S5_BLOCKS):
                ar, ai = c[0][2 * blk], c[0][2 * blk + 1]
                glr, gli = c[1][2 * blk], c[1][2 * blk + 1]
                lr, li = lam[blk]
                nr = lr * ar + li * ai + adj[blk, pl.ds(r0, 8), 0:128]
                ni = lr * ai - li * ar + adj[blk, pl.ds(r0, 8), 128:256]
                adj[blk, pl.ds(r0, 8), 0:128] = nr
                adj[blk, pl.ds(r0, 8), 128:256] = ni
                pr = jnp.where(first, cs_ref[0, blk, :, 0:128], xs[blk, pl.ds(rp, 8), 0:128])
                pi = jnp.where(first, cs_ref[0, blk, :, 128:256], xs[blk, pl.ds(rp, 8), 128:256])
                new_a += [nr, ni]
                new_g += [glr + nr * pr + ni * pi, gli + ni * pr - nr * pi]
            return tuple(new_a), tuple(new_g)

        a0, g0 = [], []
        for blk in range(S5_BLOCKS):
            a0 += [acarry[blk, :, 0:128], acarry[blk, :, 128:256]]
            g0 += [dlr_ref[blk], dli_ref[blk]]
        an, gn = lax.fori_loop(0, tc, bstep, (tuple(a0), tuple(g0)), unroll=2)
        for blk in range(S5_BLOCKS):
            acarry[blk, :, 0:128] = an[2 * blk]
            acarry[blk, :, 128:256] = an[2 * blk + 1]
            dlr_ref[blk] = gn[2 * blk]
            dli_ref[blk] = gn[2 * blk + 1]
        for blk in range(S5_BLOCKS):
            sl = slice(blk * 256, (blk + 1) * 256)
            ab = adj[blk].astype(BF16)
            _stage(tmp, jnp.dot(ab, rbt_ref[blk], preferred_element_type=F32))
            du[:, sl] = _gather_rows(tmp, tc) + d_ref[:, sl] * dy_ref[:, sl]
            drb_ref[blk] += lax.dot_general(lhsu[blk], ab, (((0,), (0,)), ((), ())), preferred_element_type=F32)
            drc_ref[blk] += lax.dot_general(lhsd[blk], xs[blk].astype(BF16), (((0,), (0,)), ((), ())),
                                            preferred_element_type=F32)
        xh, r = _rms_hat(x_ref[...])
        dg_ref[...] += jnp.sum(du[...] * xh, axis=0, keepdims=True)
        dxh = du[...] * g_ref[...]
        dx_ref[...] = r * (dxh - xh * jnp.mean(dxh * xh, axis=-1, keepdims=True)) + res_ref[...]

    rev = pl.BlockSpec((tc, D_MODEL), lambda i: (nc - 1 - i, 0))
    vec = pl.BlockSpec((1, D_MODEL), lambda i: (0, 0))
    mat = pl.BlockSpec((S5_BLOCKS, 256, 256), lambda i: (0, 0, 0))
    lamspec = pl.BlockSpec((S5_BLOCKS, 8, 128), lambda i: (0, 0, 0))
    big = pltpu.VMEM((S5_BLOCKS, 8 * tc, 256), F32)
    bigb = pltpu.VMEM((S5_BLOCKS, 8 * tc, 256), BF16)
    return pl.pallas_call(
        body, grid=(nc,),
        in_specs=[rev, vec, rev, rev, vec, pl.BlockSpec((1, S5_BLOCKS, 8, 256), lambda i: (nc - 1 - i, 0, 0, 0)),
                  mat, mat, mat, lamspec, lamspec],
        out_specs=[rev, vec, mat, mat, lamspec, lamspec, vec],
        out_shape=[_sds((n_rows, D_MODEL), F32), _sds((1, D_MODEL), F32), _sds((S5_BLOCKS, 256, 256), F32),
                   _sds((S5_BLOCKS, 256, 256), F32), _sds((S5_BLOCKS, 8, 128), F32), _sds((S5_BLOCKS, 8, 128), F32),
                   _sds((1, D_MODEL), F32)],
        scratch_shapes=[pltpu.VMEM((2, 8 * tc, 128), F32), pltpu.VMEM((tc, D_MODEL), F32), bigb, bigb, big, big,
                        pltpu.VMEM((S5_BLOCKS, 8, 256), F32)],
        name="s5_bwd", compiler_params=_params(("arbitrary",)))(
            x, gain, dy2, res, d_skip, cs, rb, rbt, rct, lam_r, lam_i)


def _s5_views(a_re, a_im, log_dt, b_re, b_im):
    return a_re[:, None, :], a_im[:, None, :], log_dt[:, None, None], jnp.swapaxes(b_re, 1, 2), jnp.swapaxes(b_im, 1, 2)


def _s5_factors(a_re, a_im, log_dt):
    lr, li, dt = jnp.minimum(a_re, LAMBDA_RE_MAX), a_im, jnp.exp(log_dt)
    mag, ang = jnp.exp(lr * dt), li * dt
    lbr, lbi = mag * jnp.cos(ang), mag * jnp.sin(ang)
    den = lr * lr + li * li
    fr, fi = ((lbr - 1.0) * lr + lbi * li) / den, (lbi * lr - (lbr - 1.0) * li) / den
    return lr, li, dt, lbr, lbi, fr, fi, den


def s5_prep(a_re, a_im, log_dt, b_re, b_im, c_re, c_im):
    def body(ar_ref, ai_ref, t_ref, br_ref, bi_ref, cr_ref, ci_ref, rb_ref, rbt_ref, rc_ref, rct_ref, lr_ref, li_ref):
        _, _, _, lbr, lbi, fr, fi, _ = _s5_factors(ar_ref[...], ai_ref[...], t_ref[...])
        lr_ref[...] = lbr
        li_ref[...] = lbi
        bre = fr * br_ref[...] - fi * bi_ref[...]
        bim = fr * bi_ref[...] + fi * br_ref[...]
        even = (lax.broadcasted_iota(jnp.int32, (256, S5_STATE), 0) // S5_GROUP) % 2 == 0

        def assemble(re, im):
            re, im = re.reshape(256, S5_STATE), im.reshape(256, S5_STATE)
            return jnp.concatenate([jnp.where(even, re, 0.0), jnp.where(even, 0.0, re), jnp.where(even, im, 0.0),
                                    jnp.where(even, 0.0, im)], axis=1)

        for blk in range(S5_BLOCKS):
            sl = slice(16 * blk, 16 * blk + 16)
            rb = assemble(bre[sl], bim[sl])
            rct = assemble(cr_ref[sl], -ci_ref[sl])
            rb_ref[blk] = rb.astype(BF16)
            rbt_ref[blk] = rb.T.astype(BF16)
            rct_ref[blk] = rct.astype(BF16)
            rc_ref[blk] = rct.T.astype(BF16)

    vm = pl.BlockSpec(memory_space=pltpu.VMEM)
    mat = _sds((S5_BLOCKS, 256, 256), BF16)
    lam = _sds((S5_GROUPS, 1, S5_STATE), F32)
    rb, rbt, rc, rct, lam_r, lam_i = pl.pallas_call(
        body, in_specs=[vm] * 7, out_specs=[vm] * 6, out_shape=[mat, mat, mat, mat, lam, lam], name="s5_prep",
        compiler_params=_params())(*_s5_views(a_re, a_im, log_dt, b_re, b_im), c_re, c_im)
    return rb, rbt, rc, rct, lam_r.reshape(S5_BLOCKS, 8, 128), lam_i.reshape(S5_BLOCKS, 8, 128)


def s5_param_bwd(mats, lams, a_re, a_im, log_dt, b_re, b_im):
    def body(m_ref, glr_ref, gli_ref, ar_ref, ai_ref, t_ref, br_ref, bi_ref,
             dar_ref, dai_ref, dt_ref, dbr_ref, dbi_ref, dcr_ref, dci_ref):
        lr, li, dt, lbr, lbi, fr, fi, den = _s5_factors(ar_ref[...], ai_ref[...], t_ref[...])
        shape = (S5_GROUPS, S5_GROUP, S5_STATE)
        gbr, gbi = m_ref[0:1024, 0:64].reshape(shape), m_ref[0:1024, 64:128].reshape(shape)
        dcr_ref[...] = m_ref[1024:2048, 0:64].reshape(shape)
        dci_ref[...] = -m_ref[1024:2048, 64:128].reshape(shape)
        br, bi = br_ref[...], bi_ref[...]
        dbr_ref[...] = fr * gbr + fi * gbi
        dbi_ref[...] = fr * gbi - fi * gbr
        dfr = jnp.sum(gbr * br + gbi * bi, axis=1, keepdims=True)
        dfi = jnp.sum(gbi * br - gbr * bi, axis=1, keepdims=True)
        nr, ni = (dfr * lr - dfi * li) / den, (dfr * li + dfi * lr) / den
        qr, qi = (fr * lr + fi * li) / den, (fi * lr - fr * li) / den
        lam_r, lam_i = -(dfr * qr + dfi * qi), -(dfi * qr - dfr * qi)
        gr, gi = glr_ref[...] + nr, gli_ref[...] + ni
        zr, zi = gr * lbr + gi * lbi, gi * lbr - gr * lbi
        a = ar_ref[...]
        dar_ref[...] = (lam_r + zr * dt) * jnp.where(a < LAMBDA_RE_MAX, 1.0, jnp.where(a == LAMBDA_RE_MAX, 0.5, 0.0))
        dai_ref[...] = lam_i + zi * dt
        dt_ref[...] = jnp.sum(zr * lr + zi * li, axis=2, keepdims=True) * dt

    vm = pl.BlockSpec(memory_space=pltpu.VMEM)
    state = _sds((S5_GROUPS, 1, S5_STATE), F32)
    wide = _sds((S5_GROUPS, S5_GROUP, S5_STATE), F32)
    glr = lams[0:32].reshape(S5_GROUPS, 1, S5_STATE)
    gli = lams[32:64].reshape(S5_GROUPS, 1, S5_STATE)
    dar, dai, ddt, dbr, dbi, dcr, dci = pl.pallas_call(
        body, in_specs=[vm] * 8, out_specs=[vm] * 7,
        out_shape=[state, state, _sds((S5_GROUPS, 1, 1), F32), wide, wide, wide, wide], name="s5_param_bwd",
        compiler_params=_params())(mats, glr, gli, *_s5_views(a_re, a_im, log_dt, b_re, b_im))
    return (dar.reshape(S5_GROUPS, S5_STATE), dai.reshape(S5_GROUPS, S5_STATE), ddt.reshape(S5_GROUPS),
            jnp.swapaxes(dbr, 1, 2), jnp.swapaxes(dbi, 1, 2), dcr, dci)


def s5_compact(drb, drct, dlr, dli):
    def body(drb_ref, drct_ref, dlr_ref, dli_ref, o_ref, lam_ref):
        even = (lax.broadcasted_iota(jnp.int32, (256, 64), 0) // S5_GROUP) % 2 == 0
        for blk in range(S5_BLOCKS):
            for k, ref in enumerate((drb_ref, drct_ref)):
                m = ref[blk]
                re = jnp.where(even, m[:, 0:64], m[:, 64:128])
                im = jnp.where(even, m[:, 128:192], m[:, 192:256])
                o_ref[pl.ds(k * 1024 + blk * 256, 256), :] = jnp.concatenate([re, im], axis=1)
            lam_ref[pl.ds(blk * 8, 8), :] = dlr_ref[blk]
            lam_ref[pl.ds(32 + blk * 8, 8), :] = dli_ref[blk]

    vm = pl.BlockSpec(memory_space=pltpu.VMEM)
    return pl.pallas_call(body, in_specs=[vm] * 4, out_specs=[vm, vm], out_shape=[_sds((2048, 128), F32), _sds((64, 128), F32)],
                          name="s5_compact", compiler_params=_params())(drb, drct, dlr, dli)


NEG = -1e30


GROUP = N_Q // N_KV


def _attn_masks(n):
    qi = lax.broadcasted_iota(jnp.int32, (GROUP * BLOCK, BLOCK), 0) % BLOCK
    kj = lax.broadcasted_iota(jnp.int32, (GROUP * BLOCK, BLOCK), 1)
    return jnp.logical_and(kj > qi, n > 0), kj <= qi


def _stack_heads(ref, kh):
    return jnp.concatenate([ref[:, (GROUP * kh + g) * HEAD_DIM:(GROUP * kh + g + 1) * HEAD_DIM] for g in range(GROUP)], axis=0)


def _unstack_heads(val):
    return jnp.concatenate([val[g * BLOCK:(g + 1) * BLOCK] for g in range(GROUP)], axis=1)


def _sink_column(sink_ref, kh):
    grp = lax.broadcasted_iota(jnp.int32, (GROUP * BLOCK, 1), 0) // BLOCK
    col = jnp.zeros((GROUP * BLOCK, 1), F32)
    for g in range(GROUP):
        col = jnp.where(grp == g, sink_ref[GROUP * kh + g], col)
    return col, grp


def _attn_exp(q4, kp, kc, sink, mask_p, mask_c):
    scale = 1.0 / math.sqrt(HEAD_DIM)
    nt = (((1,), (1,)), ((), ()))
    sp = jnp.where(mask_p, lax.dot_general(q4, kp, nt, preferred_element_type=F32) * scale, NEG)
    sc = jnp.where(mask_c, lax.dot_general(q4, kc, nt, preferred_element_type=F32) * scale, NEG)
    m = jnp.maximum(jnp.maximum(jnp.max(sp, axis=-1, keepdims=True), jnp.max(sc, axis=-1, keepdims=True)), sink)
    pp = jnp.exp(sp - m)
    pc = jnp.exp(sc - m)
    ps = jnp.exp(sink - m)
    inv = 1.0 / (jnp.sum(pp, axis=-1, keepdims=True) + jnp.sum(pc, axis=-1, keepdims=True) + ps)
    return pp, pc, ps, inv


def attn_fwd(q, kv, sinks):
    n_rows = q.shape[0]
    nb = n_rows // BLOCK

    def body(sink_ref, q_ref, kvp_ref, kvc_ref, o_ref):
        n = pl.program_id(0)
        mask_p, mask_c = _attn_masks(n)
        outs = []
        for kh in range(N_KV):
            ks, vs = slice(kh * HEAD_DIM, (kh + 1) * HEAD_DIM), slice((N_KV + kh) * HEAD_DIM, (N_KV + kh + 1) * HEAD_DIM)
            sink, _ = _sink_column(sink_ref, kh)
            pp, pc, _, inv = _attn_exp(_stack_heads(q_ref, kh), kvp_ref[:, ks], kvc_ref[:, ks], sink, mask_p, mask_c)
            o4 = (jnp.dot(pp.astype(BF16), kvp_ref[:, vs], preferred_element_type=F32)
                  + jnp.dot(pc.astype(BF16), kvc_ref[:, vs], preferred_element_type=F32)) * inv
            outs.append(_unstack_heads(o4))
        o_ref[...] = jnp.concatenate(outs, axis=1).astype(BF16)

    kvw = 2 * N_KV * HEAD_DIM
    return pl.pallas_call(
        body, grid=(nb,),
        in_specs=[pl.BlockSpec(memory_space=pltpu.SMEM), pl.BlockSpec((BLOCK, D_MODEL), lambda n: (n, 0)),
                  pl.BlockSpec((BLOCK, kvw), lambda n: (jnp.maximum(n - 1, 0), 0)), pl.BlockSpec((BLOCK, kvw), lambda n: (n, 0))],
        out_specs=pl.BlockSpec((BLOCK, D_MODEL), lambda n: (n, 0)), out_shape=_sds((n_rows, D_MODEL), BF16),
        name="attn_fwd", compiler_params=_params(("parallel",)))(sinks, q, kv, kv)


def attn_bwd(q, kv, do, sinks):
    n_rows = q.shape[0]
    nb = n_rows // BLOCK
    kvw = 2 * N_KV * HEAD_DIM
    tn = (((0,), (0,)), ((), ()))
    nt = (((1,), (1,)), ((), ()))
    scale = 1.0 / math.sqrt(HEAD_DIM)

    def body(sink_ref, q_ref, kvp_ref, kvc_ref, do_ref, dq_ref, dbq_ref, dprev_ref, dcur_ref, dsink_ref):
        n = pl.program_id(0)
        mask_p, mask_c = _attn_masks(n)
        lane = lax.broadcasted_iota(jnp.int32, (1, D_MODEL), 1)
        dqs, dsink = [], jnp.zeros((1, D_MODEL), F32)
        dkp, dkc, dvp, dvc = [], [], [], []
        for kh in range(N_KV):
            ks, vs = slice(kh * HEAD_DIM, (kh + 1) * HEAD_DIM), slice((N_KV + kh) * HEAD_DIM, (N_KV + kh + 1) * HEAD_DIM)
            q4, do4 = _stack_heads(q_ref, kh), _stack_heads(do_ref, kh)
            kp, kc, vp, vc = kvp_ref[:, ks], kvc_ref[:, ks], kvp_ref[:, vs], kvc_ref[:, vs]
            sink, grp = _sink_column(sink_ref, kh)
            pp, pc, ps, inv = _attn_exp(q4, kp, kc, sink, mask_p, mask_c)
            pp, pc = pp * inv, pc * inv
            dpp = lax.dot_general(do4, vp, nt, preferred_element_type=F32)
            dpc = lax.dot_general(do4, vc, nt, preferred_element_type=F32)
            delta = jnp.sum(pp * dpp, axis=-1, keepdims=True) + jnp.sum(pc * dpc, axis=-1, keepdims=True)
            dsp = (pp * (dpp - delta) * scale).astype(BF16)
            dsc = (pc * (dpc - delta) * scale).astype(BF16)
            dsk = ps * inv * delta
            for g in range(GROUP):
                dsink = dsink + jnp.where(lane == GROUP * kh + g, -jnp.sum(jnp.where(grp == g, dsk, 0.0)), 0.0)
            dqs.append(_unstack_heads(jnp.dot(dsp, kp, preferred_element_type=F32)
                                      + jnp.dot(dsc, kc, preferred_element_type=F32)))
            dkp.append(lax.dot_general(dsp, q4, tn, preferred_element_type=F32))
            dkc.append(lax.dot_general(dsc, q4, tn, preferred_element_type=F32))
            dvp.append(lax.dot_general(pp.astype(BF16), do4, tn, preferred_element_type=F32))
            dvc.append(lax.dot_general(pc.astype(BF16), do4, tn, preferred_element_type=F32))
        dq = jnp.concatenate(dqs, axis=1)
        dq_ref[...] = dq.astype(BF16)
        dprev_ref[0] = jnp.concatenate(dkp + dvp, axis=1)
        dcur_ref[0] = jnp.concatenate(dkc + dvc, axis=1)

        @pl.when(n == 0)
        def _():
            dbq_ref[...] = jnp.zeros_like(dbq_ref)
            dsink_ref[...] = jnp.zeros_like(dsink_ref)

        dbq_ref[...] += jnp.sum(dq, axis=0, keepdims=True)
        dsink_ref[...] += dsink

    blk = pl.BlockSpec((BLOCK, D_MODEL), lambda n: (n, 0))
    part = pl.BlockSpec((1, BLOCK, kvw), lambda n: (n, 0, 0))
    return pl.pallas_call(
        body, grid=(nb,),
        in_specs=[pl.BlockSpec(memory_space=pltpu.SMEM), blk,
                  pl.BlockSpec((BLOCK, kvw), lambda n: (jnp.maximum(n - 1, 0), 0)), pl.BlockSpec((BLOCK, kvw), lambda n: (n, 0)), blk],
        out_specs=[blk, pl.BlockSpec((1, D_MODEL), lambda n: (0, 0)), part, part, pl.BlockSpec((1, D_MODEL), lambda n: (0, 0))],
        out_shape=[_sds((n_rows, D_MODEL), BF16), _sds((1, D_MODEL), F32), _sds((nb, BLOCK, kvw), F32),
                   _sds((nb, BLOCK, kvw), F32), _sds((1, D_MODEL), F32)],
        name="attn_bwd", compiler_params=_params(("arbitrary",)))(sinks, q, kv, kv, do)


def kv_combine(dprev, dcur):
    nb, _, kvw = dprev.shape

    def body(dcur_ref, dprev_ref, dkv_ref, db_ref):
        total = jnp.zeros((1, kvw), F32)
        for m in range(nb):
            dkv = dcur_ref[m] + dprev_ref[m + 1] if m + 1 < nb else dcur_ref[m]
            dkv_ref[m * BLOCK:(m + 1) * BLOCK, :] = dkv.astype(BF16)
            total = total + jnp.sum(dkv, axis=0, keepdims=True)
        db_ref[...] = jnp.concatenate([total, jnp.zeros((1, D_MODEL - kvw), F32)], axis=1)

    vm = pl.BlockSpec(memory_space=pltpu.VMEM)
    return pl.pallas_call(body, in_specs=[vm, vm], out_specs=[vm, vm],
                          out_shape=[_sds((nb * BLOCK, kvw), BF16), _sds((1, D_MODEL), F32)], name="kv_combine",
                          compiler_params=_params())(dcur, dprev)


def glu_bwd(dout, val, gate, tm=256):
    n_rows, d = dout.shape

    def body(do_ref, v_ref, g_ref, dz_ref, db_ref):
        i = pl.program_id(0)
        sg = jax.nn.sigmoid(g_ref[...])
        dval = do_ref[...] * sg
        dgate = do_ref[...] * v_ref[...] * sg * (1.0 - sg)
        dz_ref[...] = jnp.concatenate([dval, dgate], axis=1).astype(BF16)

        @pl.when(i == 0)
        def _():
            db_ref[...] = jnp.zeros_like(db_ref)

        db_ref[0:1, :] += jnp.sum(dval, axis=0, keepdims=True)
        db_ref[1:2, :] += jnp.sum(dgate, axis=0, keepdims=True)

    row = pl.BlockSpec((tm, d), lambda i: (i, 0))
    return pl.pallas_call(
        body, grid=(n_rows // tm,), in_specs=[row, row, row],
        out_specs=[pl.BlockSpec((tm, 2 * d), lambda i: (i, 0)), pl.BlockSpec((2, d), lambda i: (0, 0))],
        out_shape=[_sds((n_rows, 2 * d), BF16), _sds((2, d), F32)],
        name="glu_bwd", compiler_params=_params(("arbitrary",)))(dout, val, gate)


def _adam_update(w, g, m, v):
    nm = ADAM_B1 * m + (1.0 - ADAM_B1) * g
    nv = ADAM_B2 * v + (1.0 - ADAM_B2) * (g * g)
    m_hat = nm / (1.0 - ADAM_B1 ** ADAM_STEP)
    v_hat = nv / (1.0 - ADAM_B2 ** ADAM_STEP)
    return -ADAM_LR * (m_hat / (jnp.sqrt(v_hat) + ADAM_EPS) + ADAM_WD * w), nm, nv


def adamw(name, ws, gs, ms, vs, steps=8):
    n = len(ws)

    def body(*refs):
        for k in range(n):
            w_ref, g_ref, m_ref, v_ref = (refs[j * n + k] for j in range(4))
            go_ref, d_ref, nm_ref, nv_ref = (refs[(4 + j) * n + k] for j in range(4))
            gv = g_ref[...]
            go_ref[...] = gv
            d_ref[...], nm_ref[...], nv_ref[...] = _adam_update(w_ref[...], gv, m_ref[...], v_ref[...])

    specs = [pl.BlockSpec((w.shape[0] // steps, w.shape[1]), lambda i: (i, 0)) for w in ws]
    shapes = [_sds(w.shape, F32) for w in ws]
    out = pl.pallas_call(
        body, grid=(steps,), in_specs=specs * 4, out_specs=specs * 4, out_shape=shapes * 4, name=name,
        compiler_params=_params(("parallel",)))(*ws, *gs, *ms, *vs)
    return [list(out[j * n:(j + 1) * n]) for j in range(4)]


def adamw_native(name, ws, gs, ms, vs):
    n = len(ws)

    def body(*refs):
        w_refs, g_refs, m_refs, v_refs = refs[:n], refs[n:2 * n], refs[2 * n:3 * n], refs[3 * n:4 * n]
        d_refs, nm_refs, nv_refs = refs[4 * n:5 * n], refs[5 * n:6 * n], refs[6 * n:7 * n]
        for k in range(n):
            dl, nm, nv = _adam_update(w_refs[k][...], g_refs[k][...], m_refs[k][...], v_refs[k][...])
            d_refs[k][...] = dl
            nm_refs[k][...] = nm
            nv_refs[k][...] = nv

    vm = pl.BlockSpec(memory_space=pltpu.VMEM)
    shapes = [_sds(w.shape, F32) for w in ws]
    out = pl.pallas_call(body, in_specs=[vm] * (4 * n), out_specs=[vm] * (3 * n), out_shape=shapes * 3, name=name,
                         compiler_params=_params())(*ws, *gs, *ms, *vs)
    return list(out[:n]), list(out[n:2 * n]), list(out[2 * n:])


VEC_ROWS = {"norm_mix": 0, "norm_mlp": 2, "norm_kv": 4, "norm_final": 5, "s5_d": 6, "b_q": 7, "b_o": 8, "s5_b_glu": 9,
            "b_kv": 11, "sinks": 12, "loss": 13}


def split_vectors(where, vecs, d_shard, glu_shard):
    kvw = 2 * N_KV * HEAD_DIM
    shapes = {"norm_mix": (2, D_MODEL), "norm_mlp": (2, D_MODEL), "norm_kv": (1, D_MODEL), "norm_final": (1, D_MODEL),
              "s5_d": (1, d_shard), "b_q": (1, D_MODEL), "b_o": (1, D_MODEL), "s5_b_glu": (1, glu_shard), "b_kv": (1, kvw),
              "sinks": (1, N_Q), "loss": (1, 128)}
    names = list(shapes)

    def body(where_ref, v_ref, *o_refs):
        chip = where_ref[1]
        for name, o_ref in zip(names, o_refs):
            r0, (r, n) = VEC_ROWS[name], shapes[name]
            if name == "s5_d":
                g = jnp.zeros((1, n), F32)
                for j in range(4):
                    g = jnp.where(chip == j, v_ref[r0:r0 + 1, j * n:(j + 1) * n], g)
            elif name == "s5_b_glu":
                g = jnp.zeros((1, n), F32)
                for j in range(4):
                    row, col = r0 + (j * n) // D_MODEL, (j * n) % D_MODEL
                    g = jnp.where(chip == j, v_ref[row:row + 1, col:col + n], g)
            else:
                g = v_ref[r0:r0 + r, 0:n]
            o_ref[...] = g

    vm = pl.BlockSpec(memory_space=pltpu.VMEM)
    out = pl.pallas_call(body, in_specs=[pl.BlockSpec(memory_space=pltpu.SMEM), vm], out_specs=[vm] * len(names),
                         out_shape=[_sds(shapes[n], F32) for n in names], name="split_vectors",
                         compiler_params=_params())(where, vecs)
    return dict(zip(names, out))


def _position():
    x, y, c = lax.axis_index("x"), lax.axis_index("y"), lax.axis_index("c")
    others = [(1 - x, y), (x, 1 - y), (1 - x, 1 - y)]
    return x, y, c, others


def _window(ref, kind, chip, half, shard_shape):
    if kind == "slab":
        return ref.at[chip]
    r, n = shard_shape
    if kind == "col":
        return ref.at[pl.ds(pl.multiple_of(half * (r // 2), 16), r // 2), pl.ds(pl.multiple_of(chip * n, 128), n)]
    return ref.at[pl.ds(pl.multiple_of(chip * r, 16), r), pl.ds(pl.multiple_of(half * (n // 2), 128), n // 2)]


def _half(ref, kind, half, shape):
    r, n = shape
    if kind == "col":
        return ref.at[pl.ds(pl.multiple_of(half * (r // 2), 16), r // 2), :]
    return ref.at[:, pl.ds(pl.multiple_of(half * (n // 2), 128), n // 2)]


def swap_start(name, grads, kinds, carry):
    nt = len(grads)
    shapes = [tuple(g.shape) for g in grads]
    lands = [lax.empty(sh, BF16) for sh in shapes]
    given, given_specs, token_type, write = _hand_through(carry)
    n_in = 2 * nt + len(given)

    def body(*refs):
        in_refs, land_refs = refs[:nt], refs[nt:2 * nt]
        send_sems, recv_sems, token = refs[n_in], refs[n_in + 1], refs[-1]
        x, y, c, _ = _position()
        for t in range(nt):
            pltpu.make_async_remote_copy(
                src_ref=_half(in_refs[t], kinds[t], 1 - c, shapes[t]), dst_ref=_half(land_refs[t], kinds[t], 1 - c, shapes[t]),
                send_sem=send_sems.at[t], recv_sem=recv_sems.at[t], device_id=(x, y, 1 - c), device_id_type=MESH).start()
        write(token, refs[:n_in])

    sems = pltpu.SemaphoreType.DMA((nt,))
    both = list(grads) + lands
    out = pl.pallas_call(
        body, name=name, in_specs=[HBM_SPEC] * (2 * nt) + given_specs,
        out_specs=(SEM_SPEC, SEM_SPEC, *[HBM_SPEC] * (2 * nt), pl.BlockSpec(memory_space=pltpu.VMEM)),
        out_shape=(sems, sems, *[pltpu.HBM(a.shape, a.dtype) for a in both], token_type),
        input_output_aliases={t: 2 + t for t in range(2 * nt)}, compiler_params=_split_params(),
    )(*[_in_hbm(a) for a in both], *given)
    return out[0], out[1], list(out[2:2 + nt]), list(out[2 + nt:2 + 2 * nt]), out[-1]


def swap_wait(name, send_sems, recv_sems, grads, lands, kinds, after):
    nt = len(grads)
    shapes = [tuple(g.shape) for g in grads]

    def body(*refs):
        in_refs, land_refs = refs[:nt], refs[nt:2 * nt]
        send_ref, recv_ref = refs[2 * nt], refs[2 * nt + 1]
        x, y, c, _ = _position()
        for t in range(nt):
            cp = pltpu.make_async_remote_copy(
                src_ref=_half(in_refs[t], kinds[t], 1 - c, shapes[t]), dst_ref=_half(land_refs[t], kinds[t], c, shapes[t]),
                send_sem=send_ref.at[t], recv_sem=recv_ref.at[t], device_id=(x, y, 1 - c), device_id_type=MESH)
            cp.wait_send()
            cp.wait_recv()

    both = list(grads) + list(lands)
    out = pl.pallas_call(
        body, name=name, in_specs=[HBM_SPEC] * (2 * nt) + [SEM_SPEC, SEM_SPEC, HBM_SPEC], out_specs=[HBM_SPEC] * (2 * nt),
        out_shape=[pltpu.HBM(a.shape, a.dtype) for a in both], input_output_aliases={t: t for t in range(2 * nt)},
        compiler_params=_split_params())(*both, send_sems, recv_sems, _in_hbm(after))
    return list(out[:nt]), list(out[nt:])


def _half_spec(kind, shape, tiles):
    r, n = shape
    if kind == "col":
        tn = n // tiles
        return pl.BlockSpec((r // 2, tn), lambda i, s: (s[0], i))
    tm = r // tiles
    return pl.BlockSpec((tm, n // 2), lambda i, s: (i, s[0]))


def add_halves(name, mine, landed, kinds, where, tiles=4):
    nt = len(mine)
    shapes = [tuple(a.shape) for a in mine]

    def compact(t):
        r, n = shapes[t]
        if kinds[t] == "col":
            return (r // 2, n), pl.BlockSpec((r // 2, n // tiles), lambda i, s: (0, i))
        return (r, n // 2), pl.BlockSpec((r // tiles, n // 2), lambda i, s: (i, 0))

    def body(s_ref, *refs):
        for a_ref, b_ref, o_ref in zip(refs[:nt], refs[nt:2 * nt], refs[2 * nt:]):
            o_ref[...] = (a_ref[...].astype(F32) + b_ref[...].astype(F32)).astype(BF16)

    specs = [_half_spec(kinds[t], shapes[t], tiles) for t in range(nt)]
    return pl.pallas_call(
        body, grid_spec=pltpu.PrefetchScalarGridSpec(num_scalar_prefetch=1, grid=(tiles,), in_specs=specs + specs,
                                                     out_specs=[compact(t)[1] for t in range(nt)]),
        out_shape=[_sds(compact(t)[0], BF16) for t in range(nt)], name=name,
        compiler_params=_params(("parallel",)))(where, *mine, *landed)


def sum_shards(name, parts, landed, kinds, shard_shapes, where, layers, n_layers, intos, tiles=2):
    nt = len(parts)
    in_specs, out_specs = [], []
    for t in range(nt):
        (r, n), layer = shard_shapes[t], layers[t]
        if kinds[t] == "col":
            tm, width = r // 2 // tiles, n
            own = pl.BlockSpec((tm, n), lambda i, s: (i, s[1]))
            out = pl.BlockSpec((None, tm, n), lambda i, s, layer=layer: (layer, s[0] * tiles + i, 0))
        else:
            tm, width = r // tiles, n // 2
            own = pl.BlockSpec((tm, n // 2), lambda i, s: (s[1] * tiles + i, 0))
            out = pl.BlockSpec((None, tm, n // 2), lambda i, s, layer=layer: (layer, i, s[0]))
        in_specs += [own, pl.BlockSpec((3, tm, width), lambda i, s: (0, i, 0))]
        out_specs.append(out)
    args, aliases = [where] + [a for pair in zip(parts, landed) for a in pair], {}
    for t in range(nt):
        if intos[t] is not None:
            aliases[len(args)] = t
            in_specs.append(pl.BlockSpec(memory_space=pl.ANY))
            args.append(intos[t])

    def body(s_ref, *refs):
        for t in range(nt):
            a_ref, l_ref, o_ref = refs[2 * t], refs[2 * t + 1], refs[len(in_specs) + t]
            o_ref[...] = ((a_ref[...].astype(F32) + l_ref[0].astype(F32)) + l_ref[1].astype(F32)) + l_ref[2].astype(F32)

    return pl.pallas_call(
        body, grid_spec=pltpu.PrefetchScalarGridSpec(num_scalar_prefetch=1, grid=(tiles,), in_specs=in_specs,
                                                     out_specs=out_specs),
        out_shape=[_sds((n_layers[t],) + tuple(shard_shapes[t]), F32) for t in range(nt)], input_output_aliases=aliases,
        name=name, compiler_params=_params(("parallel",)))(*args)


def share_start(arrays, entries, carry):
    na, nt = len(arrays), len(entries)
    given, given_specs, token_type, write = _hand_through(carry)
    n_in = na + len(given)

    def body(*refs):
        in_refs, send_sems, recv_sems, token = refs[:na], refs[n_in], refs[n_in + 1], refs[-1]
        x, y, c, _ = _position()
        for t, (a, layer, kind) in enumerate(entries):
            mine = _half(in_refs[a].at[layer], kind, c, tuple(arrays[a].shape[1:]))
            pltpu.make_async_remote_copy(
                src_ref=mine, dst_ref=mine, send_sem=send_sems.at[t], recv_sem=recv_sems.at[t],
                device_id=(x, y, 1 - c), device_id_type=MESH).start()
        write(token, refs[:n_in])

    sems = pltpu.SemaphoreType.DMA((nt,))
    out = pl.pallas_call(
        body, name="share_start", in_specs=[HBM_SPEC] * na + given_specs,
        out_specs=(SEM_SPEC, SEM_SPEC, *[HBM_SPEC] * na, pl.BlockSpec(memory_space=pltpu.VMEM)),
        out_shape=(sems, sems, *[pltpu.HBM(a.shape, a.dtype) for a in arrays], token_type),
        input_output_aliases={t: 2 + t for t in range(na)}, compiler_params=_split_params(),
    )(*[_in_hbm(a) for a in arrays], *given)
    return out[0], out[1], list(out[2:2 + na]), out[-1]


def share_wait(send_sems, recv_sems, arrays, entries, after):
    na = len(arrays)

    def body(*refs):
        in_refs, send_ref, recv_ref = refs[:na], refs[na], refs[na + 1]
        x, y, c, _ = _position()
        for t, (a, layer, kind) in enumerate(entries):
            shape = tuple(arrays[a].shape[1:])
            cp = pltpu.make_async_remote_copy(
                src_ref=_half(in_refs[a].at[layer], kind, c, shape), dst_ref=_half(in_refs[a].at[layer], kind, 1 - c, shape),
                send_sem=send_ref.at[t], recv_sem=recv_ref.at[t], device_id=(x, y, 1 - c), device_id_type=MESH)
            cp.wait_send()
            cp.wait_recv()

    return list(pl.pallas_call(
        body, name="share_wait", in_specs=[HBM_SPEC] * na + [SEM_SPEC, SEM_SPEC, HBM_SPEC], out_specs=[HBM_SPEC] * na,
        out_shape=[pltpu.HBM(a.shape, a.dtype) for a in arrays], input_output_aliases={t: t for t in range(na)},
        compiler_params=_split_params())(*arrays, send_sems, recv_sems, _in_hbm(after)))


HBM_SPEC = pl.BlockSpec(memory_space=pltpu.HBM)
SEM_SPEC = pl.BlockSpec(memory_space=pltpu.SEMAPHORE)
ANY_SPEC = pl.BlockSpec(memory_space=pl.ANY)


def _split_params():
    return pltpu.CompilerParams(has_side_effects=pltpu.SideEffectType.DATAFLOW_SIDE_EFFECTING,
                                vmem_limit_bytes=VMEM_LIMIT_BYTES)


def _in_hbm(a):
    return pltpu.with_memory_space_constraint(a, pltpu.HBM)


def cast_place(arrays, entries, where, tiles=2):
    in_specs, out_specs, fulls = [], [], []
    for a, layer, kind in entries:
        _, r, n = arrays[a].shape
        tm = r // tiles
        in_specs.append(pl.BlockSpec((None, tm, n), lambda i, s, layer=layer: (layer, i, 0)))
        if kind == "col":
            fulls.append((r, 4 * n))
            out_specs.append(pl.BlockSpec((tm, n), lambda i, s: (i, s[1])))
        else:
            fulls.append((4 * r, n))
            out_specs.append(pl.BlockSpec((tm, n), lambda i, s: (s[1] * tiles + i, 0)))
    nt = len(entries)

    def body(s_ref, *refs):
        for w_ref, o_ref in zip(refs[:nt], refs[nt:]):
            o_ref[...] = w_ref[...].astype(BF16)

    return pl.pallas_call(
        body, grid_spec=pltpu.PrefetchScalarGridSpec(num_scalar_prefetch=1, grid=(tiles,), in_specs=in_specs,
                                                     out_specs=out_specs),
        out_shape=[_sds(f, BF16) for f in fulls], name="cast_place",
        compiler_params=_params(("parallel",)))(where, *[arrays[a] for a, _, _ in entries])


def _hand_through(carry):
    given = [] if isinstance(carry, tuple) else [carry]

    def write(token, ins):
        token[...] = ins[-1][...] if given else jnp.zeros_like(token)

    return (given, [pl.BlockSpec(memory_space=pltpu.VMEM)] * len(given),
            _sds(carry if isinstance(carry, tuple) else carry.shape, F32), write)


def gather_start(fulls, kinds, shard_shapes, carry):
    nt = len(fulls)
    given, given_specs, token_type, write = _hand_through(carry)
    n_in = nt + len(given)

    def body(*refs):
        full_refs = refs[:nt]
        send_sems, recv_sems, token = refs[n_in], refs[n_in + 1], refs[-1]
        x, y, c, others = _position()
        for t in range(nt):
            mine = _window(full_refs[t], kinds[t], 2 * x + y, c, shard_shapes[t])
            for j, (ox, oy) in enumerate(others):
                pltpu.make_async_remote_copy(
                    src_ref=mine, dst_ref=mine, send_sem=send_sems.at[3 * t + j], recv_sem=recv_sems.at[3 * t + j],
                    device_id=(ox, oy, c), device_id_type=MESH).start()
        write(token, refs[:n_in])

    sems = pltpu.SemaphoreType.DMA((3 * nt,))
    out = pl.pallas_call(
        body, name="gather_start", in_specs=[HBM_SPEC] * nt + given_specs,
        out_specs=(SEM_SPEC, SEM_SPEC, *[HBM_SPEC] * nt, pl.BlockSpec(memory_space=pltpu.VMEM)),
        out_shape=(sems, sems, *[pltpu.HBM(f.shape, f.dtype) for f in fulls], token_type),
        input_output_aliases={t: 2 + t for t in range(nt)}, compiler_params=_split_params(),
    )(*[_in_hbm(f) for f in fulls], *given)
    return out[0], out[1], list(out[2:2 + nt]), out[-1]


def gather_wait(name, send_sems, recv_sems, fulls, kinds, shard_shapes, after, first):
    nt = len(fulls)
    extra = [] if after is None else [_in_hbm(after)]

    def body(*refs):
        full_refs, send_ref, recv_ref = refs[:nt], refs[nt], refs[nt + 1]
        x, y, c, others = _position()
        for t in range(nt):
            mine = _window(full_refs[t], kinds[t], 2 * x + y, c, shard_shapes[t])
            for j, (ox, oy) in enumerate(others):
                cp = pltpu.make_async_remote_copy(
                    src_ref=mine, dst_ref=_window(full_refs[t], kinds[t], 2 * ox + oy, c, shard_shapes[t]),
                    send_sem=send_ref.at[3 * (first + t) + j], recv_sem=recv_ref.at[3 * (first + t) + j],
                    device_id=(ox, oy, c), device_id_type=MESH)
                cp.wait_send()
                cp.wait_recv()

    out = pl.pallas_call(
        body, name=name, in_specs=[HBM_SPEC] * nt + [SEM_SPEC, SEM_SPEC] + [HBM_SPEC] * len(extra),
        out_specs=[HBM_SPEC] * nt, out_shape=[pltpu.HBM(f.shape, f.dtype) for f in fulls],
        input_output_aliases={t: t for t in range(nt)}, compiler_params=_split_params())(*fulls, send_sems, recv_sems, *extra)
    return list(out)


def forward_halves(name, fulls, kinds, shard_shapes):
    nt = len(fulls)

    def body(*refs):
        out_refs = refs[nt:2 * nt]
        send_sems, recv_sems = refs[2 * nt:]
        x, y, c, others = _position()
        cps = []
        for t in range(nt):
            for j, (ox, oy) in enumerate(others):
                landed = _window(out_refs[t], kinds[t], 2 * ox + oy, c, shard_shapes[t])
                cp = pltpu.make_async_remote_copy(
                    src_ref=landed, dst_ref=landed, send_sem=send_sems.at[3 * t + j], recv_sem=recv_sems.at[3 * t + j],
                    device_id=(x, y, 1 - c), device_id_type=MESH)
                cp.start()
                cps.append(cp)
        for t in range(nt):
            for j, (ox, oy) in enumerate(others):
                got = _window(out_refs[t], kinds[t], 2 * ox + oy, 1 - c, shard_shapes[t])
                pltpu.make_async_remote_copy(
                    src_ref=got, dst_ref=got, send_sem=send_sems.at[3 * t + j], recv_sem=recv_sems.at[3 * t + j],
                    device_id=(x, y, 1 - c), device_id_type=MESH).wait_recv()
        for cp in cps:
            cp.wait_send()

    out = pl.pallas_call(
        body, in_specs=[ANY_SPEC] * nt, out_specs=[ANY_SPEC] * nt, out_shape=[_sds(f.shape, f.dtype) for f in fulls],
        input_output_aliases={t: t for t in range(nt)},
        scratch_shapes=[pltpu.SemaphoreType.DMA((3 * nt,)), pltpu.SemaphoreType.DMA((3 * nt,))],
        name=name, compiler_params=_params())(*fulls)
    return list(out)


def forward_start(name, send_sems, recv_sems, fulls, kinds, shard_shapes, after, first, carry):
    nt = len(fulls)
    given, given_specs, token_type, write = _hand_through(carry)
    n_in = nt + 3 + len(given)

    def body(*refs):
        full_refs, ici_send, ici_recv = refs[:nt], refs[nt], refs[nt + 1]
        send_ref, recv_ref, token = refs[n_in], refs[n_in + 1], refs[-1]
        x, y, c, others = _position()
        for t in range(nt):
            mine = _window(full_refs[t], kinds[t], 2 * x + y, c, shard_shapes[t])
            for j, (ox, oy) in enumerate(others):
                landed = _window(full_refs[t], kinds[t], 2 * ox + oy, c, shard_shapes[t])
                cp = pltpu.make_async_remote_copy(
                    src_ref=mine, dst_ref=landed, send_sem=ici_send.at[3 * (first + t) + j],
                    recv_sem=ici_recv.at[3 * (first + t) + j], device_id=(ox, oy, c), device_id_type=MESH)
                cp.wait_send()
                cp.wait_recv()
                pltpu.make_async_remote_copy(
                    src_ref=landed, dst_ref=landed, send_sem=send_ref.at[3 * t + j], recv_sem=recv_ref.at[3 * t + j],
                    device_id=(x, y, 1 - c), device_id_type=MESH).start()
        write(token, refs[:n_in])

    sems = pltpu.SemaphoreType.DMA((3 * nt,))
    out = pl.pallas_call(
        body, name=name, in_specs=[HBM_SPEC] * nt + [SEM_SPEC, SEM_SPEC, HBM_SPEC] + given_specs,
        out_specs=(SEM_SPEC, SEM_SPEC, *[HBM_SPEC] * nt, pl.BlockSpec(memory_space=pltpu.VMEM)),
        out_shape=(sems, sems, *[pltpu.HBM(f.shape, f.dtype) for f in fulls], token_type),
        input_output_aliases={t: 2 + t for t in range(nt)}, compiler_params=_split_params(),
    )(*fulls, send_sems, recv_sems, _in_hbm(after), *given)
    return out[0], out[1], list(out[2:2 + nt]), out[-1]


def forward_wait(name, send_sems, recv_sems, fulls, kinds, shard_shapes, after):
    nt = len(fulls)

    def body(*refs):
        full_refs, send_ref, recv_ref = refs[:nt], refs[nt], refs[nt + 1]
        x, y, c, others = _position()
        for t in range(nt):
            for j, (ox, oy) in enumerate(others):
                cp = pltpu.make_async_remote_copy(
                    src_ref=_window(full_refs[t], kinds[t], 2 * ox + oy, c, shard_shapes[t]),
                    dst_ref=_window(full_refs[t], kinds[t], 2 * ox + oy, 1 - c, shard_shapes[t]),
                    send_sem=send_ref.at[3 * t + j], recv_sem=recv_ref.at[3 * t + j],
                    device_id=(x, y, 1 - c), device_id_type=MESH)
                cp.wait_send()
                cp.wait_recv()

    return list(pl.pallas_call(
        body, name=name, in_specs=[HBM_SPEC] * nt + [SEM_SPEC, SEM_SPEC, HBM_SPEC], out_specs=[HBM_SPEC] * nt,
        out_shape=[pltpu.HBM(f.shape, f.dtype) for f in fulls], input_output_aliases={t: t for t in range(nt)},
        compiler_params=_split_params())(*fulls, send_sems, recv_sems, _in_hbm(after)))


def _piece(ref, kind, chip, shard_shape):
    r, n = shard_shape
    if kind == "col":
        return ref.at[:, pl.ds(pl.multiple_of(chip * n, 128), n)]
    return ref.at[pl.ds(pl.multiple_of(chip * r, 16), r), :]


def _piece_shape(kind, shard_shape):
    r, n = shard_shape
    return (r // 2, n) if kind == "col" else (r, n // 2)


def exchange_start(name, parts, kinds, shard_shapes, carry):
    nt = len(parts)
    lands = [lax.empty((3,) + _piece_shape(kinds[t], shard_shapes[t]), BF16) for t in range(nt)]
    given, given_specs, token_type, write = _hand_through(carry)
    n_in = 2 * nt + len(given)

    def body(*refs):
        part_refs, land_refs = refs[:nt], refs[nt:2 * nt]
        send_sems, recv_sems, token = refs[n_in], refs[n_in + 1], refs[-1]
        x, y, c, others = _position()
        for t in range(nt):
            for j, (ox, oy) in enumerate(others):
                pltpu.make_async_remote_copy(
                    src_ref=_piece(part_refs[t], kinds[t], 2 * ox + oy, shard_shapes[t]), dst_ref=land_refs[t].at[j],
                    send_sem=send_sems.at[3 * t + j], recv_sem=recv_sems.at[3 * t + j],
                    device_id=(ox, oy, c), device_id_type=MESH).start()
        write(token, refs[:n_in])

    sems = pltpu.SemaphoreType.DMA((3 * nt,))
    both = list(parts) + lands
    out = pl.pallas_call(
        body, name=name, in_specs=[HBM_SPEC] * (2 * nt) + given_specs,
        out_specs=(SEM_SPEC, SEM_SPEC, *[HBM_SPEC] * (2 * nt), pl.BlockSpec(memory_space=pltpu.VMEM)),
        out_shape=(sems, sems, *[pltpu.HBM(a.shape, a.dtype) for a in both], token_type),
        input_output_aliases={t: 2 + t for t in range(2 * nt)}, compiler_params=_split_params(),
    )(*[_in_hbm(a) for a in both], *given)
    return out[0], out[1], list(out[2:2 + nt]), list(out[2 + nt:2 + 2 * nt]), out[-1]


def exchange_wait(name, send_sems, recv_sems, parts, lands, kinds, shard_shapes, after):
    nt = len(parts)

    def body(*refs):
        part_refs, land_refs = refs[:nt], refs[nt:2 * nt]
        send_ref, recv_ref = refs[2 * nt], refs[2 * nt + 1]
        x, y, c, others = _position()
        for t in range(nt):
            for j, (ox, oy) in enumerate(others):
                cp = pltpu.make_async_remote_copy(
                    src_ref=_piece(part_refs[t], kinds[t], 2 * ox + oy, shard_shapes[t]), dst_ref=land_refs[t].at[j],
                    send_sem=send_ref.at[3 * t + j], recv_sem=recv_ref.at[3 * t + j],
                    device_id=(ox, oy, c), device_id_type=MESH)
                cp.wait_send()
                cp.wait_recv()

    both = list(parts) + list(lands)
    out = pl.pallas_call(
        body, name=name, in_specs=[HBM_SPEC] * (2 * nt) + [SEM_SPEC, SEM_SPEC, HBM_SPEC], out_specs=[HBM_SPEC] * (2 * nt),
        out_shape=[pltpu.HBM(a.shape, a.dtype) for a in both], input_output_aliases={t: t for t in range(2 * nt)},
        compiler_params=_split_params())(*both, send_sems, recv_sems, _in_hbm(after))
    return list(out[:nt]), list(out[nt:])


def all_reduce_small(name, bufs, wire):
    n = len(bufs)
    halves = [b.shape[0] // 2 for b in bufs]

    def body(*refs):
        in_refs, out_refs, lands, txs = refs[:n], refs[n:2 * n], refs[2 * n:3 * n], refs[3 * n:4 * n]
        send_sems, recv_sems = refs[4 * n:]
        x, y, c, _ = _position()
        mine = [pl.ds(pl.multiple_of(c * h, 8), h) for h in halves]
        other = [pl.ds(pl.multiple_of((1 - c) * h, 8), h) for h in halves]
        for s, peer in enumerate([(x, y, 1 - c), (1 - x, y, c), (x, 1 - y, c)]):
            cps = []
            for k in range(n):
                txs[k][...] = (in_refs[k][other[k], :] if s == 0 else out_refs[k][mine[k], :]).astype(wire[k])
                cp = pltpu.make_async_remote_copy(
                    src_ref=txs[k], dst_ref=lands[k].at[s], send_sem=send_sems.at[4 * k + s], recv_sem=recv_sems.at[4 * k + s],
                    device_id=peer, device_id_type=MESH)
                cp.start()
                cps.append(cp)
            for k, cp in enumerate(cps):
                cp.wait()
                own = in_refs[k][mine[k], :] if s == 0 else out_refs[k][mine[k], :]
                out_refs[k][mine[k], :] = own.astype(wire[k]).astype(F32) + lands[k][s].astype(F32)
        cps = []
        for k in range(n):
            cp = pltpu.make_async_remote_copy(
                src_ref=out_refs[k].at[mine[k]], dst_ref=out_refs[k].at[mine[k]], send_sem=send_sems.at[4 * k + 3],
                recv_sem=recv_sems.at[4 * k + 3], device_id=(x, y, 1 - c), device_id_type=MESH)
            cp.start()
            cps.append(cp)
        for cp in cps:
            cp.wait()

    vm = pl.BlockSpec(memory_space=pltpu.VMEM)
    out = pl.pallas_call(
        body, in_specs=[vm] * n, out_specs=[vm] * n, out_shape=[_sds(b.shape, F32) for b in bufs],
        scratch_shapes=[pltpu.VMEM((3, h, b.shape[1]), w) for h, b, w in zip(halves, bufs, wire)]
        + [pltpu.VMEM((h, b.shape[1]), w) for h, b, w in zip(halves, bufs, wire)]
        + [pltpu.SemaphoreType.DMA((4 * n,)), pltpu.SemaphoreType.DMA((4 * n,))],
        name=name, compiler_params=_params())(*bufs)
    return list(out)


def _local_step(x, target, small, need, ahead, emit_swap, emit_exchange):
    d = D_MODEL
    full = {}

    def handed(vec, token):
        return vec if token is None else token

    def token_rows(token):
        return [] if token is None else [token]

    def plus(acc, rows):
        return acc + rows[0] if rows else acc

    rb16, rbt16, rc16, rct16, lr_t, li_t = small["s5_operands"]
    ge, y2, cs = s5_fwd(x, small["norm_mix0"], small["s5_d"], rb16, rc16, lr_t, li_t)
    full.update(need("glu", ge))

    def norm_rows(h, gains):
        xh, _ = _rms_hat(h)
        return [xh * g for g in gains]

    def glu_epilogue(accs, e, r):
        v, gt = accs[0] + r[0], accs[1] + r[1]
        h = e[0] + v * jax.nn.sigmoid(gt)
        return [h, v, gt] + norm_rows(h, r[2:])

    gain_mlp0 = handed(small["norm_mlp0"], ahead("mlp_in0", full["w_glu"], small["norm_mlp0"]))
    h1, val, gate, n1 = mm_nn(
        "glu", ge, full["w_glu"], [0, d], d, glu_epilogue, [F32, F32, F32, BF16], extras=[x],
        rowvecs=[(small["s5_b_glu"], 0), (small["s5_b_glu"], d), (gain_mlp0, 0)], tm=512, tn=d)

    def mlp_fwd(tag, h, n, w_in, get_w_out, next_gains, head=None):
        def in_epilogue(accs, e, rv):
            pos = jnp.maximum(accs[0], 0.0)
            return [pos * pos, 2.0 * pos]

        r, slope = mm_nn("mlp_in" + tag, n, w_in, [0], w_in.shape[1], in_epilogue, [BF16, BF16], tm=2048)
        w_out = get_w_out(r)

        def epilogue(accs, e, rv):
            h_out = e[0] + accs[0]
            return [h_out] + norm_rows(h_out, rv)

        if head is not None:
            return head(r, w_out, h), (n, r, slope)
        outs = mm_nn("mlp_out" + tag, r, w_out, [0], d, epilogue, [F32] + [BF16] * len(next_gains), extras=[h],
                     rowvecs=[(g, 0) for g in next_gains], tm=512, tn=d)
        return outs[0], outs[1:], (n, r, slope)

    full.update(need("mlp_in0", h1))

    def w_out0(after):
        full.update(need("mlp_out0", after))
        return full["w_out0"]

    h2, (nkv, n2), mlp0 = mlp_fwd("0", h1, n1, full["w_in0"], w_out0, [small["norm_kv"], small["norm_mix1"]])

    full.update(need("attn", h2))
    kvw = 2 * N_KV * HEAD_DIM
    (kv,) = mm_nn("kv_proj", nkv, full["w_kv"], [0], kvw, lambda accs, e, r: [accs[0] + r[0]], [BF16],
                  rowvecs=[(small["b_kv"], 0)], tm=2048)
    (q,) = mm_nn("q_proj", n2, full["w_q"], [0], d, lambda accs, e, r: [accs[0] + r[0]], [BF16],
                 rowvecs=[(small["b_q"], 0)], tm=2048)
    sinks = small["sinks"].reshape(N_Q)
    o = attn_fwd(q, kv, sinks)
    def o_epilogue(accs, e, r):
        h_out = e[0] + accs[0] + r[0]
        return [h_out] + norm_rows(h_out, r[1:])

    bias_o = handed(small["b_o"], ahead("mlp_in1", o, small["b_o"]))
    h3, n3 = mm_nn("o_proj", o, full["w_o"], [0], d, o_epilogue, [F32, BF16], extras=[h2],
                   rowvecs=[(bias_o, 0), (small["norm_mlp1"], 0)], tm=512, tn=d)
    full.update(need("mlp_in1", h3))

    def w_out1(after):
        full.update(need("mlp_out1", after))
        return full["w_out1"]

    def loss_head(r, w_out, h):
        def epilogue(accs, e, rv):
            xh, rr = _rms_hat(e[0] + accs[0])
            err = xh * rv[0] - e[1]
            dy = err * (1.0 / d)
            dxh = dy * rv[0]
            dx = rr * (dxh - xh * jnp.mean(dxh * xh, axis=-1, keepdims=True))
            loss = jnp.full((1, d), 0.5 * jnp.sum(jnp.mean(err * err, axis=-1, keepdims=True)), F32)
            return [dx, dx, loss, jnp.sum(dy * xh, axis=0, keepdims=True)]

        return mm_nn("mlp_out1", r, w_out, [0], d, epilogue, [F32, BF16], extras=[h, target],
                     rowvecs=[(small["norm_final"], 0)], n_sums=2, tm=512, tn=d)

    (dh, dhb, loss_tile, dg_final), mlp1 = mlp_fwd("1", h3, n3, full["w_in1"], w_out1, [], head=loss_head)

    grads_small, grads_full = {"norm_final": dg_final}, {}
    ident = lambda acc, e, r: [plus(acc, r)]
    layer1 = ["w_out1", "w_in1", "w_o", "w_q", "w_kv"]
    layer0 = ["w_out0", "w_in0", "w_glu"]

    def norm_bwd_rows(x_rows, res, dys, gains):
        xh, r = _rms_hat(x_rows)
        dxh = sum(dy * g for dy, g in zip(dys, gains))
        dx = r * (dxh - xh * jnp.mean(dxh * xh, axis=-1, keepdims=True)) + res
        return dx, [jnp.sum(dy * xh, axis=0, keepdims=True) for dy in dys]

    def mlp_bwd(tag, dh, dhb, h_in, gain, w_in, w_out, saved, token=None):
        n, r, slope = saved
        grads_full["w_out" + tag] = mm_tn("dw_out" + tag, r, dhb, tn=1024)
        (da,) = mm_nt("mlp_da" + tag, dhb, w_out, lambda acc, e, rv: [plus(acc * e[0].astype(F32), rv)], [BF16],
                      extras=[slope], rowvecs=token_rows(token), tm=2048)
        grads_full["w_in" + tag] = mm_tn("dw_in" + tag, n, da, tn=1024)

        def epilogue(acc, e, rv):
            dx, dgs = norm_bwd_rows(e[0], e[1], [acc], rv)
            return [dx, dx, jnp.sum(dx, axis=0, keepdims=True)] + dgs

        dx, dxb, colsum, dg = mm_nt("mlp_dn" + tag, da, w_in, epilogue, [F32, BF16], extras=[h_in, dh], rowvecs=[gain],
                                    n_sums=2, tm=512, tk=d)
        grads_small["norm_mlp" + tag] = dg
        return dx, dxb, colsum

    dh3, dh3b, colsum3 = mlp_bwd("1", dh, dhb, h3, small["norm_mlp1"], full["w_in1"], full["w_out1"], mlp1)
    grads_small["b_o"] = colsum3
    grads_full["w_o"] = mm_tn("dw_o", o, dh3b, tn=1024)
    (do,) = mm_nt("attn_do", dh3b, full["w_o"], ident, [BF16], tm=2048)
    dq, dbq, dprev, dcur, dsink = attn_bwd(q, kv, do, sinks)
    dkv, dbkv = kv_combine(dprev, dcur)
    grads_small["b_q"], grads_small["b_kv"], grads_small["sinks"] = dbq, dbkv, dsink
    grads_full["w_q"] = mm_tn("dw_q", n2, dq, tn=1024)
    grads_full["w_kv"] = mm_tn("dw_kv", nkv, dkv, tk=1024)
    token = emit_swap("layer1", {n: grads_full[n] for n in layer1}, (1, d))
    (dnkv,) = mm_nt("kv_dn", dkv, full["w_kv"], ident, [F32], rowvecs=token_rows(token), tm=2048, tk=1024)

    def attn_dn_epilogue(acc, e, rv):
        dx, dgs = norm_bwd_rows(e[0], e[1], [acc, e[2]], rv)
        return [dx, dx] + dgs

    dh2, dh2b, dg_mix1, dg_kv = mm_nt("attn_dn", dq, full["w_q"], attn_dn_epilogue, [F32, BF16], extras=[h2, dh3, dnkv],
                                      rowvecs=[small["norm_mix1"], small["norm_kv"]], n_sums=2, tm=512, tk=d)
    grads_small["norm_mix1"], grads_small["norm_kv"] = dg_mix1, dg_kv
    token = emit_exchange("layer1", dh2b, (1, full["w_out0"].shape[0]))
    dh1, _, _ = mlp_bwd("0", dh2, dh2b, h1, small["norm_mlp0"], full["w_in0"], full["w_out0"], mlp0, token)

    dz, db_glu = glu_bwd(dh1, val, gate)
    grads_small["s5_b_glu"] = db_glu
    grads_full["w_glu"] = mm_tn("dw_glu", ge, dz, tn=1024)
    token = emit_swap("layer0", {n: grads_full[n] for n in layer0}, (1, d))
    (dy2,) = mm_nt("glu_dy", dz, full["w_glu"], lambda acc, e, rv: [plus(acc, rv) * _gelu_grad(e[0])], [F32], extras=[y2],
                   rowvecs=token_rows(token), tm=1024, tk=1024)
    d_skip = handed(small["s5_d"], emit_exchange("layer0", dy2, small["s5_d"]))
    grad_x, dd, drb, drc, dlr, dli, dg_mix0 = s5_bwd(x, small["norm_mix0"], dy2, dh1, d_skip, cs, rb16, rbt16, rct16, lr_t, li_t)
    grads_small["s5_d"] = dd
    grads_small["s5_mats"] = (drb, drc, dlr, dli)
    grads_small["norm_mix0"] = dg_mix0
    return loss_tile, grad_x, grads_small


SMALL_NAMES = ["norm_mix", "norm_mlp", "norm_kv", "norm_final", "s5_a_re", "s5_a_im", "s5_log_dt", "s5_b_re", "s5_b_im",
               "s5_c_re", "s5_c_im", "s5_d", "s5_b_glu", "b_kv", "b_q", "sinks", "b_o"]
BIG_NAMES = ["s5_w_glu", "w_kv", "w_q", "w_o", "w_mlp_in", "w_mlp_out"]
WEIGHT_ORDER = ["norm_mix", "norm_mlp", "norm_kv", "norm_final", "s5_a_re", "s5_a_im", "s5_log_dt", "s5_b_re", "s5_b_im",
                "s5_c_re", "s5_c_im", "s5_d", "s5_w_glu", "s5_b_glu", "w_kv", "b_kv", "w_q", "b_q", "sinks", "w_o", "b_o",
                "w_mlp_in", "w_mlp_out"]


def kernel(x, norm_mix, norm_mlp, norm_kv, norm_final, s5_a_re, s5_a_im, s5_log_dt, s5_b_re, s5_b_im, s5_c_re, s5_c_im, s5_d, s5_w_glu, s5_b_glu, w_kv, b_kv, w_q, b_q, sinks, w_o, b_o, w_mlp_in, w_mlp_out, loss_target, m_norm_mix, m_norm_mlp, m_norm_kv, m_norm_final, m_s5_a_re, m_s5_a_im, m_s5_log_dt, m_s5_b_re, m_s5_b_im, m_s5_c_re, m_s5_c_im, m_s5_d, m_s5_w_glu, m_s5_b_glu, m_w_kv, m_b_kv, m_w_q, m_b_q, m_sinks, m_w_o, m_b_o, m_w_mlp_in, m_w_mlp_out, v_norm_mix, v_norm_mlp, v_norm_kv, v_norm_final, v_s5_a_re, v_s5_a_im, v_s5_log_dt, v_s5_b_re, v_s5_b_im, v_s5_c_re, v_s5_c_im, v_s5_d, v_s5_w_glu, v_s5_b_glu, v_w_kv, v_b_kv, v_w_q, v_b_q, v_sinks, v_w_o, v_b_o, v_w_mlp_in, v_w_mlp_out):
    env = dict(locals())
    w = {n: env[n] for n in WEIGHT_ORDER}
    mom = {n: env["m_" + n] for n in WEIGHT_ORDER}
    var = {n: env["v_" + n] for n in WEIGHT_ORDER}
    d = D_MODEL
    xi, yi, ci = lax.axis_index("x"), lax.axis_index("y"), lax.axis_index("c")
    chip = 2 * xi + yi
    where = jnp.stack([ci, chip]).astype(jnp.int32)

    dsh, bsh = s5_d.shape[1], s5_b_glu.shape[1]
    packed = jnp.concatenate([s5_d.reshape(-1, 128), s5_b_glu.reshape(-1, 128)])
    n_d, n_b = dsh // 128, bsh // 128
    slab = lax.dynamic_update_slice(jnp.zeros((4, 8, 128), F32), jnp.pad(packed, ((0, 8 - n_d - n_b), (0, 0)))[None],
                                    (chip, 0, 0))

    big = [s5_w_glu, w_kv[None], w_q, w_o, w_mlp_in, w_mlp_out]
    entries = [(0, 0, "col"), (1, 0, "row"), (2, 0, "row"), (3, 0, "row"), (4, 0, "col"), (4, 1, "col"),
               (5, 0, "row"), (5, 1, "row")]
    names = ["w_glu", "w_kv", "w_q", "w_o", "w_in0", "w_in1", "w_out0", "w_out1"]
    kinds = dict(zip(names, [k for _, _, k in entries]))
    shard_shapes = dict(zip(names, [tuple(big[a].shape[1:]) for a, _, _ in entries]))

    placed_w = dict(zip(names, cast_place(big, entries, where)))
    placed_w["vectors"], kinds["vectors"], shard_shapes["vectors"] = slab, "slab", None
    gather_groups = {"glu": ["w_glu"], "mlp_in0": ["w_in0"], "mlp_out0": ["w_out0"], "attn": ["w_kv", "w_q", "w_o"],
                     "mlp_in1": ["w_in1"], "mlp_out1": ["w_out1"]}
    order = ["vectors"] + [n for members in gather_groups.values() for n in members]
    send, recv, thru, log_dt = gather_start([placed_w[n] for n in order], [kinds[n] for n in order],
                                            [shard_shapes[n] for n in order], s5_log_dt)
    started = dict(zip(order, thru))
    (gathered_rows,) = gather_wait("gather_wait_vectors", send, recv, [started["vectors"]], ["slab"], [None], None, 0)
    d_full = gathered_rows[:, 0:n_d].reshape(1, -1)
    bglu_full = gathered_rows[:, n_d:n_d + n_b].reshape(1, -1)

    forwarding = {}

    def ahead(group, after, carry):
        members = gather_groups[group]
        ks, shapes = [kinds[n] for n in members], [shard_shapes[n] for n in members]
        d2d_send, d2d_recv, landed, tok = forward_start(
            "forward_start_" + group, send, recv, [started[n] for n in members], ks, shapes, after,
            order.index(members[0]), carry)
        forwarding[group] = (d2d_send, d2d_recv, landed)
        return tok

    def need(group, after):
        members = gather_groups[group]
        ks, shapes = [kinds[n] for n in members], [shard_shapes[n] for n in members]
        if group in forwarding:
            return dict(zip(members, forward_wait("forward_wait_" + group, *forwarding[group], ks, shapes, after)))
        landed = gather_wait("gather_wait_" + group, send, recv, [started[n] for n in members], ks, shapes, after,
                             order.index(members[0]))
        return dict(zip(members, forward_halves("forward_halves_" + group, landed, ks, shapes)))

    swapping, exchanging = {}, {}

    def emit_swap(group, partial, carry):
        members = list(partial)
        send, recv, mine, lands, tok = swap_start("swap_start_" + group, [partial[n] for n in members],
                                                  [kinds[n] for n in members], carry)
        swapping[group] = (members, send, recv, mine, lands)
        return tok

    def emit_exchange(group, after, carry):
        members, send, recv, mine, lands = swapping[group]
        ks, shapes = [kinds[n] for n in members], [shard_shapes[n] for n in members]
        mine, landed = swap_wait("swap_wait_" + group, send, recv, mine, lands, ks, after)
        sums = add_halves("add_halves_" + group, mine, landed, ks, where)
        send, recv, parts, lands, tok = exchange_start("exchange_start_" + group, sums, ks, shapes, carry)
        exchanging[group] = (members, send, recv, parts, lands)
        return tok

    s5_args = (s5_a_re[0], s5_a_im[0], log_dt[0], s5_b_re[0], s5_b_im[0])
    small = {
        "norm_mix0": norm_mix[0:1], "norm_mix1": norm_mix[1:2], "norm_mlp0": norm_mlp[0:1], "norm_mlp1": norm_mlp[1:2],
        "norm_kv": norm_kv.reshape(1, d), "norm_final": norm_final.reshape(1, d), "s5_operands": s5_prep(*s5_args, s5_c_re[0], s5_c_im[0]),
        "s5_d": d_full, "s5_b_glu": bglu_full,
        "b_kv": b_kv.reshape(1, -1), "b_q": b_q, "sinks": sinks, "b_o": b_o,
    }
    loss_row, grad_x, gs = _local_step(x[0], loss_target[0], small, need, ahead, emit_swap, emit_exchange)

    reduced = [None] * len(big)
    where_of = dict(zip(names, entries))
    for group in ("layer1", "layer0"):
        members, send, recv, parts, lands = exchanging[group]
        ks, shapes = [kinds[n] for n in members], [shard_shapes[n] for n in members]
        parts, lands = exchange_wait("exchange_wait_" + group, send, recv, parts, lands, ks, shapes, grad_x)
        targets = [where_of[n][0] for n in members]
        sums = sum_shards("sum_shards_" + group, parts, lands, ks, shapes, where, [where_of[n][1] for n in members],
                          [big[a].shape[0] for a in targets], [reduced[a] for a in targets])
        for a, arr in zip(targets, sums):
            reduced[a] = arr
    share_send, share_recv, reduced, shared = share_start(reduced, entries, (2, d))

    mats, lams = s5_compact(*gs["s5_mats"])
    rows = [gs["norm_mix0"], gs["norm_mix1"], gs["norm_mlp0"], gs["norm_mlp1"], gs["norm_kv"], gs["norm_final"], gs["s5_d"],
            gs["b_q"], gs["b_o"], gs["s5_b_glu"], gs["b_kv"], gs["sinks"], loss_row, shared]
    vecs, lams, mats = all_reduce_small("reduce_small", [jnp.concatenate(rows, axis=0), lams, mats], [F32, F32, BF16])
    grads = split_vectors(where, vecs, dsh, bsh)
    loss = grads.pop("loss")[0, 0]
    g_are, g_aim, g_dt, g_bre, g_bim, dc_re, dc_im = s5_param_bwd(mats, lams, *s5_args)
    grads.update({"s5_a_re": g_are[None], "s5_a_im": g_aim[None], "s5_log_dt": g_dt[None], "s5_b_re": g_bre[None],
                  "s5_b_im": g_bim[None], "s5_c_re": dc_re[None], "s5_c_im": dc_im[None]})

    delta, new_m, new_v = {}, {}, {}

    def view(n, a):
        return a.reshape(1, -1) if a.ndim == 1 else jnp.swapaxes(a, -1, -2) if n in ("s5_b_re", "s5_b_im") else a

    sw, sg, sm, sv = ([view(n, t[n]) for n in SMALL_NAMES] for t in (w, grads, mom, var))
    for n, a, b, c_ in zip(SMALL_NAMES, *adamw_native("adamw_small", sw, sg, sm, sv)):
        delta[n], new_m[n], new_v[n] = (view(n, t) if t.ndim == 4 else t for t in (a, b, c_))

    reduced = share_wait(share_send, share_recv, reduced, entries, new_v["s5_c_re"])
    for n, g in zip(BIG_NAMES, reduced):
        grads[n] = g.reshape(w[n].shape)
    flat = lambda t: [t[n].reshape(-1, t[n].shape[-1]) for n in BIG_NAMES]
    for table, arrays in zip((grads, delta, new_m, new_v), adamw("adamw_big", flat(w), flat(grads), flat(mom), flat(var))):
        for n, a in zip(BIG_NAMES, arrays):
            table[n] = a.reshape(w[n].shape)

    out = [loss.reshape(()), grad_x[None]]
    for table in (grads, delta, new_m, new_v):
        out += [table[n].reshape(w[n].shape) for n in WEIGHT_ORDER]
    return tuple(out)
```

```python
import math

import jax
import jax.numpy as jnp
from jax import lax
from jax.experimental import pallas as pl
from jax.experimental.pallas import tpu as pltpu

F32 = jnp.float32
BF16 = jnp.bfloat16

D_MODEL = 1024
S5_GROUPS = 64
S5_GROUP = 16
S5_STATE = 64
N_KV = 4
N_Q = 16
HEAD_DIM = 64
BLOCK = 128
NORM_EPS = 1e-5
LAMBDA_RE_MAX = -1e-4
ADAM_LR, ADAM_B1, ADAM_B2, ADAM_EPS, ADAM_WD, ADAM_STEP = 0.001, 0.9, 0.999, 1e-08, 0.01, 10

VMEM_LIMIT_BYTES = 56 * 1024 * 1024
S5_CHUNK = 256
S5_BLOCKS = 4
MESH = pl.DeviceIdType.MESH


def _params(sem=None):
    return pltpu.CompilerParams(dimension_semantics=sem, vmem_limit_bytes=VMEM_LIMIT_BYTES)


def _sds(shape, dtype):
    return jax.ShapeDtypeStruct(shape, dtype)


def _rms_hat(xv):
    r = lax.rsqrt(jnp.mean(xv * xv, axis=-1, keepdims=True) + NORM_EPS)
    return xv * r, r


def mm_nn(name, a, w, col_offsets, n_out, epilogue, out_dtypes, extras=(), rowvecs=(), n_sums=0, tm=1024, tn=512):
    m, k = a.shape
    tm, tn = min(tm, m), min(tn, n_out)
    nw, ne, nr, no = len(col_offsets), len(extras), len(rowvecs), len(out_dtypes)

    def body(a_ref, *refs):
        w_refs, e_refs, r_refs = refs[:nw], refs[nw:nw + ne], refs[nw + ne:nw + ne + nr]
        o_refs, s_refs = refs[nw + ne + nr:nw + ne + nr + no], refs[nw + ne + nr + no:]
        av = a_ref[...]
        accs = [jnp.dot(av, w_ref[...], preferred_element_type=F32) for w_ref in w_refs]
        outs = epilogue(accs, [e[...] for e in e_refs], [r[...] for r in r_refs])
        for o_ref, o in zip(o_refs, outs[:no]):
            o_ref[...] = o.astype(o_ref.dtype)
        if n_sums:
            @pl.when(pl.program_id(1) == 0)
            def _():
                for s_ref in s_refs:
                    s_ref[...] = jnp.zeros_like(s_ref)

            for s_ref, val in zip(s_refs, outs[no:]):
                s_ref[...] += val

    def wspec(off):
        return pl.BlockSpec((k, tn), lambda j, i, off=off: (0, off // tn + j))

    def rspec(off):
        return pl.BlockSpec((1, tn), lambda j, i, off=off: (0, off // tn + j))

    tile = pl.BlockSpec((tm, tn), lambda j, i: (i, j))
    in_specs = ([pl.BlockSpec((tm, k), lambda j, i: (i, 0))] + [wspec(o) for o in col_offsets]
                + [tile] * ne + [rspec(o) for _, o in rowvecs])
    sem = ("parallel", "arbitrary") if n_sums else ("parallel", "parallel")
    return pl.pallas_call(
        body, grid=(n_out // tn, m // tm), in_specs=in_specs,
        out_specs=[tile] * no + [pl.BlockSpec((1, tn), lambda j, i: (0, j))] * n_sums,
        out_shape=[_sds((m, n_out), dt) for dt in out_dtypes] + [_sds((1, n_out), F32)] * n_sums, name=name,
        compiler_params=_params(sem))(a, *([w] * nw), *extras, *[r for r, _ in rowvecs])


def mm_nt(name, g, w, epilogue, out_dtypes, extras=(), rowvecs=(), n_sums=0, tm=512, tk=512):
    m, n = g.shape
    k = w.shape[0]
    tm, tk = min(tm, m), min(tk, k)
    ne, nr, no = len(extras), len(rowvecs), len(out_dtypes)

    def body(g_ref, w_ref, *refs):
        e_refs, r_refs, o_refs, s_refs = refs[:ne], refs[ne:ne + nr], refs[ne + nr:ne + nr + no], refs[ne + nr + no:]
        acc = lax.dot_general(g_ref[...], w_ref[...], (((1,), (1,)), ((), ())), preferred_element_type=F32)
        outs = epilogue(acc, [e[...] for e in e_refs], [r[...] for r in r_refs])
        for o_ref, o in zip(o_refs, outs[:no]):
            o_ref[...] = o.astype(o_ref.dtype)
        if n_sums:
            @pl.when(pl.program_id(0) == 0)
            def _():
                for s_ref in s_refs:
                    s_ref[...] = jnp.zeros_like(s_ref)

            for s_ref, val in zip(s_refs, outs[no:]):
                s_ref[...] += val

    tile = pl.BlockSpec((tm, tk), lambda i, j: (i, j))
    vec = pl.BlockSpec((1, tk), lambda i, j: (0, j))
    sem = ("arbitrary", "parallel") if n_sums else ("parallel", "parallel")
    return pl.pallas_call(
        body, grid=(m // tm, k // tk),
        in_specs=[pl.BlockSpec((tm, n), lambda i, j: (i, 0)), pl.BlockSpec((tk, n), lambda i, j: (j, 0))]
        + [tile] * ne + [vec] * nr,
        out_specs=[tile] * no + [vec] * n_sums,
        out_shape=[_sds((m, k), dt) for dt in out_dtypes] + [_sds((1, k), F32)] * n_sums, name=name,
        compiler_params=_params(sem))(g, w, *extras, *rowvecs)


def mm_tn(name, a, g, tk=512, tn=512):
    m, k = a.shape
    n = g.shape[1]
    tk, tn = min(tk, k), min(tn, n)

    def body(a_ref, g_ref, o_ref):
        acc = lax.dot_general(a_ref[...], g_ref[...], (((0,), (0,)), ((), ())), preferred_element_type=F32)
        o_ref[...] = acc.astype(o_ref.dtype)

    return pl.pallas_call(
        body, grid=(k // tk, n // tn),
        in_specs=[pl.BlockSpec((m, tk), lambda i, j: (0, i)), pl.BlockSpec((m, tn), lambda i, j: (0, j))],
        out_specs=pl.BlockSpec((tk, tn), lambda i, j: (i, j)), out_shape=_sds((k, n), BF16), name=name,
        compiler_params=_params(("parallel", "parallel")))(a, g)


def _row_mask(tc):
    row = lax.broadcasted_iota(jnp.int32, (8 * tc, 256), 0) % 8
    col = lax.broadcasted_iota(jnp.int32, (8 * tc, 256), 1) // 32
    return row == col


def _expand_rows(val, mask):
    tc, width = val.shape
    rep = jnp.broadcast_to(val[:, None, :], (tc, 8, width)).reshape(8 * tc, width)
    return jnp.where(mask, rep, 0.0).astype(BF16)


def _stage(ref, val):
    ref[0] = val[:, 0:128]
    ref[1] = val[:, 128:256]


def _gather_rows(src_ref, tc):
    halves = []
    for half in range(2):
        col = lax.broadcasted_iota(jnp.int32, (tc, 128), 1) // 32 + 4 * half
        out = jnp.zeros((tc, 128), F32)
        for s8 in range(4 * half, 4 * half + 4):
            out = jnp.where(col == s8, src_ref.at[half][pl.ds(s8, tc, stride=8), :], out)
        halves.append(out)
    return jnp.concatenate(halves, axis=1)


def _gelu(x):
    c = math.sqrt(2.0 / math.pi)
    return 0.5 * x * (1.0 + jnp.tanh(c * (x + 0.044715 * x * x * x)))


def _gelu_grad(x):
    c = math.sqrt(2.0 / math.pi)
    t = jnp.tanh(c * (x + 0.044715 * x * x * x))
    return 0.5 * (1.0 + t) + 0.5 * x * (1.0 - t * t) * c * (1.0 + 3.0 * 0.044715 * x * x)


def s5_fwd(x, gain, d_skip, rb, rc, lam_r, lam_i):
    n_rows = x.shape[0]
    tc = min(S5_CHUNK, n_rows)
    nc = n_rows // tc

    def body(x_ref, g_ref, d_ref, rb_ref, rc_ref, lr_ref, li_ref, ge_ref, y2_ref, cs_ref, bux, yrows, carry):
        i = pl.program_id(0)
        u = _rms_hat(x_ref[...])[0] * g_ref[...]

        @pl.when(i == 0)
        def _():
            carry[...] = jnp.zeros_like(carry)

        cs_ref[0] = carry[...]
        mask = _row_mask(tc)
        for blk in range(S5_BLOCKS):
            lhs = _expand_rows(u[:, blk * 256:(blk + 1) * 256], mask)
            bux[blk] = jnp.dot(lhs, rb_ref[blk], preferred_element_type=F32)
        lam = [(lr_ref[blk], li_ref[blk]) for blk in range(S5_BLOCKS)]

        def step(t, c):
            r0 = pl.multiple_of(t * 8, 8)
            new = []
            for blk in range(S5_BLOCKS):
                xr, xi = c[2 * blk], c[2 * blk + 1]
                lr, li = lam[blk]
                nr = lr * xr - li * xi + bux[blk, pl.ds(r0, 8), 0:128]
                ni = lr * xi + li * xr + bux[blk, pl.ds(r0, 8), 128:256]
                bux[blk, pl.ds(r0, 8), 0:128] = nr
                bux[blk, pl.ds(r0, 8), 128:256] = ni
                new += [nr, ni]
            return tuple(new)

        c0 = []
        for blk in range(S5_BLOCKS):
            c0 += [carry[blk, :, 0:128], carry[blk, :, 128:256]]
        cn = lax.fori_loop(0, tc, step, tuple(c0), unroll=4)
        for blk in range(S5_BLOCKS):
            carry[blk, :, 0:128] = cn[2 * blk]
            carry[blk, :, 128:256] = cn[2 * blk + 1]
        for blk in range(S5_BLOCKS):
            _stage(yrows, jnp.dot(bux[blk].astype(BF16), rc_ref[blk], preferred_element_type=F32))
            sl = slice(blk * 256, (blk + 1) * 256)
            y2 = _gather_rows(yrows, tc) + d_ref[:, sl] * u[:, sl]
            y2_ref[:, sl] = y2
            ge_ref[:, sl] = _gelu(y2).astype(BF16)

    row = pl.BlockSpec((tc, D_MODEL), lambda i: (i, 0))
    vec = pl.BlockSpec((1, D_MODEL), lambda i: (0, 0))
    mat = pl.BlockSpec((S5_BLOCKS, 256, 256), lambda i: (0, 0, 0))
    lamspec = pl.BlockSpec((S5_BLOCKS, 8, 128), lambda i: (0, 0, 0))
    return pl.pallas_call(
        body, grid=(nc,),
        in_specs=[row, vec, vec, mat, mat, lamspec, lamspec],
        out_specs=[row, row, pl.BlockSpec((1, S5_BLOCKS, 8, 256), lambda i: (i, 0, 0, 0))],
        out_shape=[_sds((n_rows, D_MODEL), BF16), _sds((n_rows, D_MODEL), F32), _sds((nc, S5_BLOCKS, 8, 256), F32)],
        scratch_shapes=[pltpu.VMEM((S5_BLOCKS, 8 * tc, 256), F32), pltpu.VMEM((2, 8 * tc, 128), F32),
                        pltpu.VMEM((S5_BLOCKS, 8, 256), F32)],
        name="s5_fwd", compiler_params=_params(("arbitrary",)))(x, gain, d_skip, rb, rc, lam_r, lam_i)


def s5_bwd(x, gain, dy2, res, d_skip, cs, rb, rbt, rct, lam_r, lam_i):
    n_rows = x.shape[0]
    tc = min(S5_CHUNK, n_rows)
    nc = n_rows // tc

    def body(x_ref, g_ref, dy_ref, res_ref, d_ref, cs_ref, rb_ref, rbt_ref, rct_ref, lr_ref, li_ref,
             dx_ref, dd_ref, drb_ref, drc_ref, dlr_ref, dli_ref, dg_ref, tmp, du, lhsu, lhsd, xs, adj, acarry):
        i = pl.program_id(0)
        u = _rms_hat(x_ref[...])[0] * g_ref[...]

        @pl.when(i == 0)
        def _():
            acarry[...] = jnp.zeros_like(acarry)
            dd_ref[...] = jnp.zeros_like(dd_ref)
            drb_ref[...] = jnp.zeros_like(drb_ref)
            drc_ref[...] = jnp.zeros_like(drc_ref)
            dlr_ref[...] = jnp.zeros_like(dlr_ref)
            dli_ref[...] = jnp.zeros_like(dli_ref)
            dg_ref[...] = jnp.zeros_like(dg_ref)

        dd_ref[...] += jnp.sum(dy_ref[...] * u, axis=0, keepdims=True)
        mask = _row_mask(tc)
        for blk in range(S5_BLOCKS):
            sl = slice(blk * 256, (blk + 1) * 256)
            lhsu[blk] = _expand_rows(u[:, sl], mask)
            xs[blk] = jnp.dot(lhsu[blk], rb_ref[blk], preferred_element_type=F32)
            lhsd[blk] = _expand_rows(dy_ref[:, sl], mask)
            adj[blk] = jnp.dot(lhsd[blk], rct_ref[blk], preferred_element_type=F32)
        lam = [(lr_ref[blk], li_ref[blk]) for blk in range(S5_BLOCKS)]

        def fstep(t, c):
            r0 = pl.multiple_of(t * 8, 8)
            new = []
            for blk in range(S5_BLOCKS):
                xr, xi = c[2 * blk], c[2 * blk + 1]
                lr, li = lam[blk]
                nr = lr * xr - li * xi + xs[blk, pl.ds(r0, 8), 0:128]
                ni = lr * xi + li * xr + xs[blk, pl.ds(r0, 8), 128:256]
                xs[blk, pl.ds(r0, 8), 0:128] = nr
                xs[blk, pl.ds(r0, 8), 128:256] = ni
                new += [nr, ni]
            return tuple(new)

        c0 = []
        for blk in range(S5_BLOCKS):
            c0 += [cs_ref[0, blk, :, 0:128], cs_ref[0, blk, :, 128:256]]
        lax.fori_loop(0, tc, fstep, tuple(c0), unroll=4)

        def bstep(k, c):
            t = tc - 1 - k
            r0 = pl.multiple_of(t * 8, 8)
            rp = pl.multiple_of(jnp.maximum(t - 1, 0) * 8, 8)
            first = t == 0
            new_a, new_g = [], []
            for blk in range(S5_BLOCKS):
                ar, ai = c[0][2 * blk], c[0][2 * blk + 1]
                glr, gli = c[1][2 * blk], c[1][2 * blk + 1]
                lr, li = lam[blk]
                nr = lr * ar + li * ai + adj[blk, pl.ds(r0, 8), 0:128]
                ni = lr * ai - li * ar + adj[blk, pl.ds(r0, 8), 128:256]
                adj[blk, pl.ds(r0, 8), 0:128] = nr
                adj[blk, pl.ds(r0, 8), 128:256] = ni
                pr = jnp.where(first, cs_ref[0, blk, :, 0:128], xs[blk, pl.ds(rp, 8), 0:128])
                pi = jnp.where(first, cs_ref[0, blk, :, 128:256], xs[blk, pl.ds(rp, 8), 128:256])
                new_a += [nr, ni]
                new_g += [glr + nr * pr + ni * pi, gli + ni * pr - nr * pi]
            return tuple(new_a), tuple(new_g)

        a0, g0 = [], []
        for blk in range(S5_BLOCKS):
            a0 += [acarry[blk, :, 0:128], acarry[blk, :, 128:256]]
            g0 += [dlr_ref[blk], dli_ref[blk]]
        an, gn = lax.fori_loop(0, tc, bstep, (tuple(a0), tuple(g0)), unroll=2)
        for blk in range(S5_BLOCKS):
            acarry[blk, :, 0:128] = an[2 * blk]
            acarry[blk, :, 128:256] = an[2 * blk + 1]
            dlr_ref[blk] = gn[2 * blk]
            dli_ref[blk] = gn[2 * blk + 1]
        for blk in range(S5_BLOCKS):
            sl = slice(blk * 256, (blk + 1) * 256)
            ab = adj[blk].astype(BF16)
            _stage(tmp, jnp.dot(ab, rbt_ref[blk], preferred_element_type=F32))
            du[:, sl] = _gather_rows(tmp, tc) + d_ref[:, sl] * dy_ref[:, sl]
            drb_ref[blk] += lax.dot_general(lhsu[blk], ab, (((0,), (0,)), ((), ())), preferred_element_type=F32)
            drc_ref[blk] += lax.dot_general(lhsd[blk], xs[blk].astype(BF16), (((0,), (0,)), ((), ())),
                                            preferred_element_type=F32)
        xh, r = _rms_hat(x_ref[...])
        dg_ref[...] += jnp.sum(du[...] * xh, axis=0, keepdims=True)
        dxh = du[...] * g_ref[...]
        dx_ref[...] = r * (dxh - xh * jnp.mean(dxh * xh, axis=-1, keepdims=True)) + res_ref[...]

    rev = pl.BlockSpec((tc, D_MODEL), lambda i: (nc - 1 - i, 0))
    vec = pl.BlockSpec((1, D_MODEL), lambda i: (0, 0))
    mat = pl.BlockSpec((S5_BLOCKS, 256, 256), lambda i: (0, 0, 0))
    lamspec = pl.BlockSpec((S5_BLOCKS, 8, 128), lambda i: (0, 0, 0))
    big = pltpu.VMEM((S5_BLOCKS, 8 * tc, 256), F32)
    bigb = pltpu.VMEM((S5_BLOCKS, 8 * tc, 256), BF16)
    return pl.pallas_call(
        body, grid=(nc,),
        in_specs=[rev, vec, rev, rev, vec, pl.BlockSpec((1, S5_BLOCKS, 8, 256), lambda i: (nc - 1 - i, 0, 0, 0)),
                  mat, mat, mat, lamspec, lamspec],
        out_specs=[rev, vec, mat, mat, lamspec, lamspec, vec],
        out_shape=[_sds((n_rows, D_MODEL), F32), _sds((1, D_MODEL), F32), _sds((S5_BLOCKS, 256, 256), F32),
                   _sds((S5_BLOCKS, 256, 256), F32), _sds((S5_BLOCKS, 8, 128), F32), _sds((S5_BLOCKS, 8, 128), F32),
                   _sds((1, D_MODEL), F32)],
        scratch_shapes=[pltpu.VMEM((2, 8 * tc, 128), F32), pltpu.VMEM((tc, D_MODEL), F32), bigb, bigb, big, big,
                        pltpu.VMEM((S5_BLOCKS, 8, 256), F32)],
        name="s5_bwd", compiler_params=_params(("arbitrary",)))(
            x, gain, dy2, res, d_skip, cs, rb, rbt, rct, lam_r, lam_i)


def _s5_views(a_re, a_im, log_dt, b_re, b_im):
    return a_re[:, None, :], a_im[:, None, :], log_dt[:, None, None], jnp.swapaxes(b_re, 1, 2), jnp.swapaxes(b_im, 1, 2)


def _s5_factors(a_re, a_im, log_dt):
    lr, li, dt = jnp.minimum(a_re, LAMBDA_RE_MAX), a_im, jnp.exp(log_dt)
    mag, ang = jnp.exp(lr * dt), li * dt
    lbr, lbi = mag * jnp.cos(ang), mag * jnp.sin(ang)
    den = lr * lr + li * li
    fr, fi = ((lbr - 1.0) * lr + lbi * li) / den, (lbi * lr - (lbr - 1.0) * li) / den
    return lr, li, dt, lbr, lbi, fr, fi, den


def s5_prep(a_re, a_im, log_dt, b_re, b_im, c_re, c_im):
    def body(ar_ref, ai_ref, t_ref, br_ref, bi_ref, cr_ref, ci_ref, ar2_ref, ai2_ref, t_row_ref,
             rb_ref, rbt_ref, rc_ref, rct_ref, lr_ref, li_ref, pair_ref):
        _, _, _, _, _, fr, fi, _ = _s5_factors(ar_ref[...], ai_ref[...], t_ref[...])
        rows = lax.broadcasted_iota(jnp.int32, (S5_GROUPS, S5_STATE), 0)
        cols = lax.broadcasted_iota(jnp.int32, (S5_GROUPS, S5_STATE), 1)
        t_col = jnp.sum(jnp.where(rows == cols, t_row_ref[...], 0.0), axis=1, keepdims=True)
        _, _, _, lbr, lbi, _, _, _ = _s5_factors(ar2_ref[...], ai2_ref[...], t_col)
        left = lax.broadcasted_iota(jnp.int32, (S5_GROUPS // 2, 2 * S5_STATE), 1) < S5_STATE
        for k, (val, out_ref) in enumerate(((lbr, lr_ref), (lbi, li_ref))):
            pair_ref[k] = jnp.concatenate([val, val], axis=1)
            out_ref[...] = jnp.where(left, pair_ref.at[k][pl.ds(0, S5_GROUPS // 2, stride=2), :],
                                     pair_ref.at[k][pl.ds(1, S5_GROUPS // 2, stride=2), :])
        bre = fr * br_ref[...] - fi * bi_ref[...]
        bim = fr * bi_ref[...] + fi * br_ref[...]
        even = (lax.broadcasted_iota(jnp.int32, (256, S5_STATE), 0) // S5_GROUP) % 2 == 0

        def assemble(re, im):
            re, im = re.reshape(256, S5_STATE), im.reshape(256, S5_STATE)
            return jnp.concatenate([jnp.where(even, re, 0.0), jnp.where(even, 0.0, re), jnp.where(even, im, 0.0),
                                    jnp.where(even, 0.0, im)], axis=1)

        for blk in range(S5_BLOCKS):
            sl = slice(16 * blk, 16 * blk + 16)
            rb = assemble(bre[sl], bim[sl])
            rct = assemble(cr_ref[sl], -ci_ref[sl])
            rb_ref[blk] = rb.astype(BF16)
            rbt_ref[blk] = rb.T.astype(BF16)
            rct_ref[blk] = rct.astype(BF16)
            rc_ref[blk] = rct.T.astype(BF16)

    vm = pl.BlockSpec(memory_space=pltpu.VMEM)
    mat = _sds((S5_BLOCKS, 256, 256), BF16)
    lam = _sds((S5_GROUPS // 2, 2 * S5_STATE), F32)
    rb, rbt, rc, rct, lam_r, lam_i = pl.pallas_call(
        body, in_specs=[vm] * 10, out_specs=[vm] * 6, out_shape=[mat, mat, mat, mat, lam, lam], name="s5_prep",
        scratch_shapes=[pltpu.VMEM((2, S5_GROUPS, 2 * S5_STATE), F32)],
        compiler_params=_params())(*_s5_views(a_re, a_im, log_dt, b_re, b_im), c_re, c_im, a_re, a_im, log_dt[None, :])
    return rb, rbt, rc, rct, lam_r.reshape(S5_BLOCKS, 8, 128), lam_i.reshape(S5_BLOCKS, 8, 128)


def s5_param_bwd(mats, lams, a_re, a_im, log_dt, b_re, b_im):
    def body(m_ref, glr_ref, gli_ref, ar_ref, ai_ref, t_ref, br_ref, bi_ref,
             dar_ref, dai_ref, dt_ref, dbr_ref, dbi_ref, dcr_ref, dci_ref):
        lr, li, dt, lbr, lbi, fr, fi, den = _s5_factors(ar_ref[...], ai_ref[...], t_ref[...])
        shape = (S5_GROUPS, S5_GROUP, S5_STATE)
        gbr, gbi = m_ref[0:1024, 0:64].reshape(shape), m_ref[0:1024, 64:128].reshape(shape)
        dcr_ref[...] = m_ref[1024:2048, 0:64].reshape(shape)
        dci_ref[...] = -m_ref[1024:2048, 64:128].reshape(shape)
        br, bi = br_ref[...], bi_ref[...]
        dbr_ref[...] = fr * gbr + fi * gbi
        dbi_ref[...] = fr * gbi - fi * gbr
        dfr = jnp.sum(gbr * br + gbi * bi, axis=1, keepdims=True)
        dfi = jnp.sum(gbi * br - gbr * bi, axis=1, keepdims=True)
        nr, ni = (dfr * lr - dfi * li) / den, (dfr * li + dfi * lr) / den
        qr, qi = (fr * lr + fi * li) / den, (fi * lr - fr * li) / den
        lam_r, lam_i = -(dfr * qr + dfi * qi), -(dfi * qr - dfr * qi)
        gr, gi = glr_ref[...] + nr, gli_ref[...] + ni
        zr, zi = gr * lbr + gi * lbi, gi * lbr - gr * lbi
        a = ar_ref[...]
        dar_ref[...] = (lam_r + zr * dt) * jnp.where(a < LAMBDA_RE_MAX, 1.0, jnp.where(a == LAMBDA_RE_MAX, 0.5, 0.0))
        dai_ref[...] = lam_i + zi * dt
        dt_ref[...] = jnp.sum(zr * lr + zi * li, axis=2, keepdims=True) * dt

    vm = pl.BlockSpec(memory_space=pltpu.VMEM)
    state = _sds((S5_GROUPS, 1, S5_STATE), F32)
    wide = _sds((S5_GROUPS, S5_GROUP, S5_STATE), F32)
    glr = lams[0:32].reshape(S5_GROUPS, 1, S5_STATE)
    gli = lams[32:64].reshape(S5_GROUPS, 1, S5_STATE)
    dar, dai, ddt, dbr, dbi, dcr, dci = pl.pallas_call(
        body, in_specs=[vm] * 8, out_specs=[vm] * 7,
        out_shape=[state, state, _sds((S5_GROUPS, 1, 1), F32), wide, wide, wide, wide], name="s5_param_bwd",
        compiler_params=_params())(mats, glr, gli, *_s5_views(a_re, a_im, log_dt, b_re, b_im))
    return (dar.reshape(S5_GROUPS, S5_STATE), dai.reshape(S5_GROUPS, S5_STATE), ddt.reshape(S5_GROUPS),
            jnp.swapaxes(dbr, 1, 2), jnp.swapaxes(dbi, 1, 2), dcr, dci)


def s5_compact(drb, drct, dlr, dli):
    def body(drb_ref, drct_ref, dlr_ref, dli_ref, o_ref, lam_ref):
        even = (lax.broadcasted_iota(jnp.int32, (256, 64), 0) // S5_GROUP) % 2 == 0
        for blk in range(S5_BLOCKS):
            for k, ref in enumerate((drb_ref, drct_ref)):
                m = ref[blk]
                re = jnp.where(even, m[:, 0:64], m[:, 64:128])
                im = jnp.where(even, m[:, 128:192], m[:, 192:256])
                o_ref[pl.ds(k * 1024 + blk * 256, 256), :] = jnp.concatenate([re, im], axis=1)
            lam_ref[pl.ds(blk * 8, 8), :] = dlr_ref[blk]
            lam_ref[pl.ds(32 + blk * 8, 8), :] = dli_ref[blk]

    vm = pl.BlockSpec(memory_space=pltpu.VMEM)
    return pl.pallas_call(body, in_specs=[vm] * 4, out_specs=[vm, vm], out_shape=[_sds((2048, 128), F32), _sds((64, 128), F32)],
                          name="s5_compact", compiler_params=_params())(drb, drct, dlr, dli)


NEG = -1e30


GROUP = N_Q // N_KV


def _attn_masks(n):
    qi = lax.broadcasted_iota(jnp.int32, (GROUP * BLOCK, BLOCK), 0) % BLOCK
    kj = lax.broadcasted_iota(jnp.int32, (GROUP * BLOCK, BLOCK), 1)
    return jnp.logical_and(kj > qi, n > 0), kj <= qi


def _stack_heads(ref, kh):
    return jnp.concatenate([ref[:, (GROUP * kh + g) * HEAD_DIM:(GROUP * kh + g + 1) * HEAD_DIM] for g in range(GROUP)], axis=0)


def _unstack_heads(val):
    return jnp.concatenate([val[g * BLOCK:(g + 1) * BLOCK] for g in range(GROUP)], axis=1)


def _sink_column(sink_ref, kh):
    grp = lax.broadcasted_iota(jnp.int32, (GROUP * BLOCK, 1), 0) // BLOCK
    col = jnp.zeros((GROUP * BLOCK, 1), F32)
    for g in range(GROUP):
        col = jnp.where(grp == g, sink_ref[GROUP * kh + g], col)
    return col, grp


def _attn_exp(q4, kp, kc, sink, mask_p, mask_c):
    scale = 1.0 / math.sqrt(HEAD_DIM)
    nt = (((1,), (1,)), ((), ()))
    sp = jnp.where(mask_p, lax.dot_general(q4, kp, nt, preferred_element_type=F32) * scale, NEG)
    sc = jnp.where(mask_c, lax.dot_general(q4, kc, nt, preferred_element_type=F32) * scale, NEG)
    m = jnp.maximum(jnp.maximum(jnp.max(sp, axis=-1, keepdims=True), jnp.max(sc, axis=-1, keepdims=True)), sink)
    pp = jnp.exp(sp - m)
    pc = jnp.exp(sc - m)
    ps = jnp.exp(sink - m)
    inv = 1.0 / (jnp.sum(pp, axis=-1, keepdims=True) + jnp.sum(pc, axis=-1, keepdims=True) + ps)
    return pp, pc, ps, inv


def attn_fwd(q, kv, sinks):
    n_rows = q.shape[0]
    nb = n_rows // BLOCK

    def body(sink_ref, q_ref, kvp_ref, kvc_ref, o_ref):
        n = pl.program_id(0)
        mask_p, mask_c = _attn_masks(n)
        outs = []
        for kh in range(N_KV):
            ks, vs = slice(kh * HEAD_DIM, (kh + 1) * HEAD_DIM), slice((N_KV + kh) * HEAD_DIM, (N_KV + kh + 1) * HEAD_DIM)
            sink, _ = _sink_column(sink_ref, kh)
            pp, pc, _, inv = _attn_exp(_stack_heads(q_ref, kh), kvp_ref[:, ks], kvc_ref[:, ks], sink, mask_p, mask_c)
            o4 = (jnp.dot(pp.astype(BF16), kvp_ref[:, vs], preferred_element_type=F32)
                  + jnp.dot(pc.astype(BF16), kvc_ref[:, vs], preferred_element_type=F32)) * inv
            outs.append(_unstack_heads(o4))
        o_ref[...] = jnp.concatenate(outs, axis=1).astype(BF16)

    kvw = 2 * N_KV * HEAD_DIM
    return pl.pallas_call(
        body, grid=(nb,),
        in_specs=[pl.BlockSpec(memory_space=pltpu.SMEM), pl.BlockSpec((BLOCK, D_MODEL), lambda n: (n, 0)),
                  pl.BlockSpec((BLOCK, kvw), lambda n: (jnp.maximum(n - 1, 0), 0)), pl.BlockSpec((BLOCK, kvw), lambda n: (n, 0))],
        out_specs=pl.BlockSpec((BLOCK, D_MODEL), lambda n: (n, 0)), out_shape=_sds((n_rows, D_MODEL), BF16),
        name="attn_fwd", compiler_params=_params(("parallel",)))(sinks, q, kv, kv)


def attn_bwd(q, kv, do, sinks):
    n_rows = q.shape[0]
    nb = n_rows // BLOCK
    kvw = 2 * N_KV * HEAD_DIM
    tn = (((0,), (0,)), ((), ()))
    nt = (((1,), (1,)), ((), ()))
    scale = 1.0 / math.sqrt(HEAD_DIM)

    def body(sink_ref, q_ref, kvp_ref, kvc_ref, do_ref, dq_ref, dbq_ref, dprev_ref, dcur_ref, dsink_ref):
        n = pl.program_id(0)
        mask_p, mask_c = _attn_masks(n)
        lane = lax.broadcasted_iota(jnp.int32, (1, D_MODEL), 1)
        dqs, dsink = [], jnp.zeros((1, D_MODEL), F32)
        dkp, dkc, dvp, dvc = [], [], [], []
        for kh in range(N_KV):
            ks, vs = slice(kh * HEAD_DIM, (kh + 1) * HEAD_DIM), slice((N_KV + kh) * HEAD_DIM, (N_KV + kh + 1) * HEAD_DIM)
            q4, do4 = _stack_heads(q_ref, kh), _stack_heads(do_ref, kh)
            kp, kc, vp, vc = kvp_ref[:, ks], kvc_ref[:, ks], kvp_ref[:, vs], kvc_ref[:, vs]
            sink, grp = _sink_column(sink_ref, kh)
            pp, pc, ps, inv = _attn_exp(q4, kp, kc, sink, mask_p, mask_c)
            pp, pc = pp * inv, pc * inv
            dpp = lax.dot_general(do4, vp, nt, preferred_element_type=F32)
            dpc = lax.dot_general(do4, vc, nt, preferred_element_type=F32)
            delta = jnp.sum(pp * dpp, axis=-1, keepdims=True) + jnp.sum(pc * dpc, axis=-1, keepdims=True)
            dsp = (pp * (dpp - delta) * scale).astype(BF16)
            dsc = (pc * (dpc - delta) * scale).astype(BF16)
            dsk = ps * inv * delta
            for g in range(GROUP):
                dsink = dsink + jnp.where(lane == GROUP * kh + g, -jnp.sum(jnp.where(grp == g, dsk, 0.0)), 0.0)
            dqs.append(_unstack_heads(jnp.dot(dsp, kp, preferred_element_type=F32)
                                      + jnp.dot(dsc, kc, preferred_element_type=F32)))
            dkp.append(lax.dot_general(dsp, q4, tn, preferred_element_type=F32))
            dkc.append(lax.dot_general(dsc, q4, tn, preferred_element_type=F32))
            dvp.append(lax.dot_general(pp.astype(BF16), do4, tn, preferred_element_type=F32))
            dvc.append(lax.dot_general(pc.astype(BF16), do4, tn, preferred_element_type=F32))
        dq = jnp.concatenate(dqs, axis=1)
        dq_ref[...] = dq.astype(BF16)
        dprev_ref[0] = jnp.concatenate(dkp + dvp, axis=1)
        dcur_ref[0] = jnp.concatenate(dkc + dvc, axis=1)

        @pl.when(n == 0)
        def _():
            dbq_ref[...] = jnp.zeros_like(dbq_ref)
            dsink_ref[...] = jnp.zeros_like(dsink_ref)

        dbq_ref[...] += jnp.sum(dq, axis=0, keepdims=True)
        dsink_ref[...] += dsink

    blk = pl.BlockSpec((BLOCK, D_MODEL), lambda n: (n, 0))
    part = pl.BlockSpec((1, BLOCK, kvw), lambda n: (n, 0, 0))
    return pl.pallas_call(
        body, grid=(nb,),
        in_specs=[pl.BlockSpec(memory_space=pltpu.SMEM), blk,
                  pl.BlockSpec((BLOCK, kvw), lambda n: (jnp.maximum(n - 1, 0), 0)), pl.BlockSpec((BLOCK, kvw), lambda n: (n, 0)), blk],
        out_specs=[blk, pl.BlockSpec((1, D_MODEL), lambda n: (0, 0)), part, part, pl.BlockSpec((1, D_MODEL), lambda n: (0, 0))],
        out_shape=[_sds((n_rows, D_MODEL), BF16), _sds((1, D_MODEL), F32), _sds((nb, BLOCK, kvw), F32),
                   _sds((nb, BLOCK, kvw), F32), _sds((1, D_MODEL), F32)],
        name="attn_bwd", compiler_params=_params(("arbitrary",)))(sinks, q, kv, kv, do)


def kv_combine(dprev, dcur):
    nb, _, kvw = dprev.shape

    def body(dcur_ref, dprev_ref, dkv_ref, db_ref):
        total = jnp.zeros((1, kvw), F32)
        for m in range(nb):
            dkv = dcur_ref[m] + dprev_ref[m + 1] if m + 1 < nb else dcur_ref[m]
            dkv_ref[m * BLOCK:(m + 1) * BLOCK, :] = dkv.astype(BF16)
            total = total + jnp.sum(dkv, axis=0, keepdims=True)
        db_ref[...] = jnp.concatenate([total, jnp.zeros((1, D_MODEL - kvw), F32)], axis=1)

    vm = pl.BlockSpec(memory_space=pltpu.VMEM)
    return pl.pallas_call(body, in_specs=[vm, vm], out_specs=[vm, vm],
                          out_shape=[_sds((nb * BLOCK, kvw), BF16), _sds((1, D_MODEL), F32)], name="kv_combine",
                          compiler_params=_params())(dcur, dprev)


def glu_bwd(dout, val, gate, tm=256):
    n_rows, d = dout.shape

    def body(do_ref, v_ref, g_ref, dz_ref, db_ref):
        i = pl.program_id(0)
        sg = jax.nn.sigmoid(g_ref[...])
        dval = do_ref[...] * sg
        dgate = do_ref[...] * v_ref[...] * sg * (1.0 - sg)
        dz_ref[...] = jnp.concatenate([dval, dgate], axis=1).astype(BF16)

        @pl.when(i == 0)
        def _():
            db_ref[...] = jnp.zeros_like(db_ref)

        db_ref[0:1, :] += jnp.sum(dval, axis=0, keepdims=True)
        db_ref[1:2, :] += jnp.sum(dgate, axis=0, keepdims=True)

    row = pl.BlockSpec((tm, d), lambda i: (i, 0))
    return pl.pallas_call(
        body, grid=(n_rows // tm,), in_specs=[row, row, row],
        out_specs=[pl.BlockSpec((tm, 2 * d), lambda i: (i, 0)), pl.BlockSpec((2, d), lambda i: (0, 0))],
        out_shape=[_sds((n_rows, 2 * d), BF16), _sds((2, d), F32)],
        name="glu_bwd", compiler_params=_params(("arbitrary",)))(dout, val, gate)


def _adam_update(w, g, m, v):
    nm = ADAM_B1 * m + (1.0 - ADAM_B1) * g
    nv = ADAM_B2 * v + (1.0 - ADAM_B2) * (g * g)
    m_hat = nm / (1.0 - ADAM_B1 ** ADAM_STEP)
    v_hat = nv / (1.0 - ADAM_B2 ** ADAM_STEP)
    return -ADAM_LR * (m_hat / (jnp.sqrt(v_hat) + ADAM_EPS) + ADAM_WD * w), nm, nv


def adamw(name, ws, gs, ms, vs, steps=8):
    n = len(ws)

    def body(*refs):
        for k in range(n):
            w_ref, g_ref, m_ref, v_ref = (refs[j * n + k] for j in range(4))
            go_ref, d_ref, nm_ref, nv_ref = (refs[(4 + j) * n + k] for j in range(4))
            gv = g_ref[...]
            go_ref[...] = gv
            d_ref[...], nm_ref[...], nv_ref[...] = _adam_update(w_ref[...], gv, m_ref[...], v_ref[...])

    specs = [pl.BlockSpec((w.shape[0] // steps, w.shape[1]), lambda i: (i, 0)) for w in ws]
    shapes = [_sds(w.shape, F32) for w in ws]
    out = pl.pallas_call(
        body, grid=(steps,), in_specs=specs * 4, out_specs=specs * 4, out_shape=shapes * 4, name=name,
        compiler_params=_params(("parallel",)))(*ws, *gs, *ms, *vs)
    return [list(out[j * n:(j + 1) * n]) for j in range(4)]


def adamw_native(name, ws, gs, ms, vs):
    n = len(ws)

    def body(*refs):
        w_refs, g_refs, m_refs, v_refs = refs[:n], refs[n:2 * n], refs[2 * n:3 * n], refs[3 * n:4 * n]
        d_refs, nm_refs, nv_refs = refs[4 * n:5 * n], refs[5 * n:6 * n], refs[6 * n:7 * n]
        for k in range(n):
            dl, nm, nv = _adam_update(w_refs[k][...], g_refs[k][...], m_refs[k][...], v_refs[k][...])
            d_refs[k][...] = dl
            nm_refs[k][...] = nm
            nv_refs[k][...] = nv

    vm = pl.BlockSpec(memory_space=pltpu.VMEM)
    shapes = [_sds(w.shape, F32) for w in ws]
    out = pl.pallas_call(body, in_specs=[vm] * (4 * n), out_specs=[vm] * (3 * n), out_shape=shapes * 3, name=name,
                         compiler_params=_params())(*ws, *gs, *ms, *vs)
    return list(out[:n]), list(out[n:2 * n]), list(out[2 * n:])


VEC_ROWS = {"norm_mix": 0, "norm_mlp": 2, "norm_kv": 4, "norm_final": 5, "s5_d": 6, "b_q": 7, "b_o": 8, "s5_b_glu": 9,
            "b_kv": 11, "sinks": 12, "loss": 13}


def split_vectors(where, vecs, d_shard, glu_shard):
    kvw = 2 * N_KV * HEAD_DIM
    shapes = {"norm_mix": (2, D_MODEL), "norm_mlp": (2, D_MODEL), "norm_kv": (1, D_MODEL), "norm_final": (1, D_MODEL),
              "s5_d": (1, d_shard), "b_q": (1, D_MODEL), "b_o": (1, D_MODEL), "s5_b_glu": (1, glu_shard), "b_kv": (1, kvw),
              "sinks": (1, N_Q), "loss": (1, 128)}
    names = list(shapes)

    def body(where_ref, v_ref, *o_refs):
        chip = where_ref[1]
        for name, o_ref in zip(names, o_refs):
            r0, (r, n) = VEC_ROWS[name], shapes[name]
            if name == "s5_d":
                g = jnp.zeros((1, n), F32)
                for j in range(4):
                    g = jnp.where(chip == j, v_ref[r0:r0 + 1, j * n:(j + 1) * n], g)
            elif name == "s5_b_glu":
                g = jnp.zeros((1, n), F32)
                for j in range(4):
                    row, col = r0 + (j * n) // D_MODEL, (j * n) % D_MODEL
                    g = jnp.where(chip == j, v_ref[row:row + 1, col:col + n], g)
            else:
                g = v_ref[r0:r0 + r, 0:n]
            o_ref[...] = g

    vm = pl.BlockSpec(memory_space=pltpu.VMEM)
    out = pl.pallas_call(body, in_specs=[pl.BlockSpec(memory_space=pltpu.SMEM), vm], out_specs=[vm] * len(names),
                         out_shape=[_sds(shapes[n], F32) for n in names], name="split_vectors",
                         compiler_params=_params())(where, vecs)
    return dict(zip(names, out))


def _position():
    x, y, c = lax.axis_index("x"), lax.axis_index("y"), lax.axis_index("c")
    others = [(1 - x, y), (x, 1 - y), (1 - x, 1 - y)]
    return x, y, c, others


def _window(ref, kind, chip, half, shard_shape):
    if kind == "slab":
        return ref.at[chip]
    r, n = shard_shape
    if kind == "col":
        return ref.at[pl.ds(pl.multiple_of(half * (r // 2), 16), r // 2), pl.ds(pl.multiple_of(chip * n, 128), n)]
    return ref.at[pl.ds(pl.multiple_of(chip * r, 16), r), pl.ds(pl.multiple_of(half * (n // 2), 128), n // 2)]


def _half(ref, kind, half, shape):
    r, n = shape
    if kind == "col":
        return ref.at[pl.ds(pl.multiple_of(half * (r // 2), 16), r // 2), :]
    return ref.at[:, pl.ds(pl.multiple_of(half * (n // 2), 128), n // 2)]


def swap_start(name, grads, kinds, carry):
    nt = len(grads)
    shapes = [tuple(g.shape) for g in grads]
    lands = [lax.empty(sh, BF16) for sh in shapes]
    given, given_specs, token_type, write = _hand_through(carry)
    n_in = 2 * nt + len(given)

    def body(*refs):
        in_refs, land_refs = refs[:nt], refs[nt:2 * nt]
        send_sems, recv_sems, token = refs[n_in], refs[n_in + 1], refs[-1]
        x, y, c, _ = _position()
        for t in range(nt):
            pltpu.make_async_remote_copy(
                src_ref=_half(in_refs[t], kinds[t], 1 - c, shapes[t]), dst_ref=_half(land_refs[t], kinds[t], 1 - c, shapes[t]),
                send_sem=send_sems.at[t], recv_sem=recv_sems.at[t], device_id=(x, y, 1 - c), device_id_type=MESH).start()
        write(token, refs[:n_in])

    sems = pltpu.SemaphoreType.DMA((nt,))
    both = list(grads) + lands
    out = pl.pallas_call(
        body, name=name, in_specs=[HBM_SPEC] * (2 * nt) + given_specs,
        out_specs=(SEM_SPEC, SEM_SPEC, *[HBM_SPEC] * (2 * nt), pl.BlockSpec(memory_space=pltpu.VMEM)),
        out_shape=(sems, sems, *[pltpu.HBM(a.shape, a.dtype) for a in both], token_type),
        input_output_aliases={t: 2 + t for t in range(2 * nt)}, compiler_params=_split_params(),
    )(*[_in_hbm(a) for a in both], *given)
    return out[0], out[1], list(out[2:2 + nt]), list(out[2 + nt:2 + 2 * nt]), out[-1]


def swap_wait(name, send_sems, recv_sems, grads, lands, kinds, after):
    nt = len(grads)
    shapes = [tuple(g.shape) for g in grads]

    def body(*refs):
        in_refs, land_refs = refs[:nt], refs[nt:2 * nt]
        send_ref, recv_ref = refs[2 * nt], refs[2 * nt + 1]
        x, y, c, _ = _position()
        for t in range(nt):
            cp = pltpu.make_async_remote_copy(
                src_ref=_half(in_refs[t], kinds[t], 1 - c, shapes[t]), dst_ref=_half(land_refs[t], kinds[t], c, shapes[t]),
                send_sem=send_ref.at[t], recv_sem=recv_ref.at[t], device_id=(x, y, 1 - c), device_id_type=MESH)
            cp.wait_send()
            cp.wait_recv()

    both = list(grads) + list(lands)
    out = pl.pallas_call(
        body, name=name, in_specs=[HBM_SPEC] * (2 * nt) + [SEM_SPEC, SEM_SPEC, HBM_SPEC], out_specs=[HBM_SPEC] * (2 * nt),
        out_shape=[pltpu.HBM(a.shape, a.dtype) for a in both], input_output_aliases={t: t for t in range(2 * nt)},
        compiler_params=_split_params())(*both, send_sems, recv_sems, _in_hbm(after))
    return list(out[:nt]), list(out[nt:])


def _half_spec(kind, shape, tiles):
    r, n = shape
    if kind == "col":
        tn = n // tiles
        return pl.BlockSpec((r // 2, tn), lambda i, s: (s[0], i))
    tm = r // tiles
    return pl.BlockSpec((tm, n // 2), lambda i, s: (i, s[0]))


def add_halves(name, mine, landed, kinds, where, tiles=4):
    nt = len(mine)
    shapes = [tuple(a.shape) for a in mine]

    def compact(t):
        r, n = shapes[t]
        if kinds[t] == "col":
            return (r // 2, n), pl.BlockSpec((r // 2, n // tiles), lambda i, s: (0, i))
        return (r, n // 2), pl.BlockSpec((r // tiles, n // 2), lambda i, s: (i, 0))

    def body(s_ref, *refs):
        for a_ref, b_ref, o_ref in zip(refs[:nt], refs[nt:2 * nt], refs[2 * nt:]):
            o_ref[...] = (a_ref[...].astype(F32) + b_ref[...].astype(F32)).astype(BF16)

    specs = [_half_spec(kinds[t], shapes[t], tiles) for t in range(nt)]
    return pl.pallas_call(
        body, grid_spec=pltpu.PrefetchScalarGridSpec(num_scalar_prefetch=1, grid=(tiles,), in_specs=specs + specs,
                                                     out_specs=[compact(t)[1] for t in range(nt)]),
        out_shape=[_sds(compact(t)[0], BF16) for t in range(nt)], name=name,
        compiler_params=_params(("parallel",)))(where, *mine, *landed)


def sum_shards(name, parts, landed, kinds, shard_shapes, where, layers, n_layers, intos, tiles=2):
    nt = len(parts)
    in_specs, out_specs = [], []
    for t in range(nt):
        (r, n), layer = shard_shapes[t], layers[t]
        if kinds[t] == "col":
            tm, width = r // 2 // tiles, n
            own = pl.BlockSpec((tm, n), lambda i, s: (i, s[1]))
            out = pl.BlockSpec((None, tm, n), lambda i, s, layer=layer: (layer, s[0] * tiles + i, 0))
        else:
            tm, width = r // tiles, n // 2
            own = pl.BlockSpec((tm, n // 2), lambda i, s: (s[1] * tiles + i, 0))
            out = pl.BlockSpec((None, tm, n // 2), lambda i, s, layer=layer: (layer, i, s[0]))
        in_specs += [own, pl.BlockSpec((3, tm, width), lambda i, s: (0, i, 0))]
        out_specs.append(out)
    args, aliases = [where] + [a for pair in zip(parts, landed) for a in pair], {}
    for t in range(nt):
        if intos[t] is not None:
            aliases[len(args)] = t
            in_specs.append(pl.BlockSpec(memory_space=pl.ANY))
            args.append(intos[t])

    def body(s_ref, *refs):
        for t in range(nt):
            a_ref, l_ref, o_ref = refs[2 * t], refs[2 * t + 1], refs[len(in_specs) + t]
            o_ref[...] = ((a_ref[...].astype(F32) + l_ref[0].astype(F32)) + l_ref[1].astype(F32)) + l_ref[2].astype(F32)

    return pl.pallas_call(
        body, grid_spec=pltpu.PrefetchScalarGridSpec(num_scalar_prefetch=1, grid=(tiles,), in_specs=in_specs,
                                                     out_specs=out_specs),
        out_shape=[_sds((n_layers[t],) + tuple(shard_shapes[t]), F32) for t in range(nt)], input_output_aliases=aliases,
        name=name, compiler_params=_params(("parallel",)))(*args)


def share_start(arrays, entries, carry):
    na, nt = len(arrays), len(entries)
    given, given_specs, token_type, write = _hand_through(carry)
    n_in = na + len(given)

    def body(*refs):
        in_refs, send_sems, recv_sems, token = refs[:na], refs[n_in], refs[n_in + 1], refs[-1]
        x, y, c, _ = _position()
        for t, (a, layer, kind) in enumerate(entries):
            mine = _half(in_refs[a].at[layer], kind, c, tuple(arrays[a].shape[1:]))
            pltpu.make_async_remote_copy(
                src_ref=mine, dst_ref=mine, send_sem=send_sems.at[t], recv_sem=recv_sems.at[t],
                device_id=(x, y, 1 - c), device_id_type=MESH).start()
        write(token, refs[:n_in])

    sems = pltpu.SemaphoreType.DMA((nt,))
    out = pl.pallas_call(
        body, name="share_start", in_specs=[HBM_SPEC] * na + given_specs,
        out_specs=(SEM_SPEC, SEM_SPEC, *[HBM_SPEC] * na, pl.BlockSpec(memory_space=pltpu.VMEM)),
        out_shape=(sems, sems, *[pltpu.HBM(a.shape, a.dtype) for a in arrays], token_type),
        input_output_aliases={t: 2 + t for t in range(na)}, compiler_params=_split_params(),
    )(*[_in_hbm(a) for a in arrays], *given)
    return out[0], out[1], list(out[2:2 + na]), out[-1]


def share_wait(send_sems, recv_sems, arrays, entries, after):
    na = len(arrays)

    def body(*refs):
        in_refs, send_ref, recv_ref = refs[:na], refs[na], refs[na + 1]
        x, y, c, _ = _position()
        for t, (a, layer, kind) in enumerate(entries):
            shape = tuple(arrays[a].shape[1:])
            cp = pltpu.make_async_remote_copy(
                src_ref=_half(in_refs[a].at[layer], kind, c, shape), dst_ref=_half(in_refs[a].at[layer], kind, 1 - c, shape),
                send_sem=send_ref.at[t], recv_sem=recv_ref.at[t], device_id=(x, y, 1 - c), device_id_type=MESH)
            cp.wait_send()
            cp.wait_recv()

    return list(pl.pallas_call(
        body, name="share_wait", in_specs=[HBM_SPEC] * na + [SEM_SPEC, SEM_SPEC, HBM_SPEC], out_specs=[HBM_SPEC] * na,
        out_shape=[pltpu.HBM(a.shape, a.dtype) for a in arrays], input_output_aliases={t: t for t in range(na)},
        compiler_params=_split_params())(*arrays, send_sems, recv_sems, _in_hbm(after)))


HBM_SPEC = pl.BlockSpec(memory_space=pltpu.HBM)
SEM_SPEC = pl.BlockSpec(memory_space=pltpu.SEMAPHORE)
ANY_SPEC = pl.BlockSpec(memory_space=pl.ANY)


def _split_params():
    return pltpu.CompilerParams(has_side_effects=pltpu.SideEffectType.DATAFLOW_SIDE_EFFECTING,
                                vmem_limit_bytes=VMEM_LIMIT_BYTES)


def _in_hbm(a):
    return pltpu.with_memory_space_constraint(a, pltpu.HBM)


def cast_place(arrays, entries, where, tiles=2):
    in_specs, out_specs, fulls = [], [], []
    for a, layer, kind in entries:
        _, r, n = arrays[a].shape
        tm = r // tiles
        in_specs.append(pl.BlockSpec((None, tm, n), lambda i, s, layer=layer: (layer, i, 0)))
        if kind == "col":
            fulls.append((r, 4 * n))
            out_specs.append(pl.BlockSpec((tm, n), lambda i, s: (i, s[1])))
        else:
            fulls.append((4 * r, n))
            out_specs.append(pl.BlockSpec((tm, n), lambda i, s: (s[1] * tiles + i, 0)))
    nt = len(entries)

    def body(s_ref, *refs):
        for w_ref, o_ref in zip(refs[:nt], refs[nt:]):
            o_ref[...] = w_ref[...].astype(BF16)

    return pl.pallas_call(
        body, grid_spec=pltpu.PrefetchScalarGridSpec(num_scalar_prefetch=1, grid=(tiles,), in_specs=in_specs,
                                                     out_specs=out_specs),
        out_shape=[_sds(f, BF16) for f in fulls], name="cast_place",
        compiler_params=_params(("parallel",)))(where, *[arrays[a] for a, _, _ in entries])


def _hand_through(carry):
    given = [] if isinstance(carry, tuple) else [carry]

    def write(token, ins):
        token[...] = ins[-1][...] if given else jnp.zeros_like(token)

    return (given, [pl.BlockSpec(memory_space=pltpu.VMEM)] * len(given),
            _sds(carry if isinstance(carry, tuple) else carry.shape, F32), write)


def gather_start(fulls, kinds, shard_shapes, carry):
    nt = len(fulls)
    given, given_specs, token_type, write = _hand_through(carry)
    n_in = nt + len(given)

    def body(*refs):
        full_refs = refs[:nt]
        send_sems, recv_sems, token = refs[n_in], refs[n_in + 1], refs[-1]
        x, y, c, others = _position()
        for t in range(nt):
            mine = _window(full_refs[t], kinds[t], 2 * x + y, c, shard_shapes[t])
            for j, (ox, oy) in enumerate(others):
                pltpu.make_async_remote_copy(
                    src_ref=mine, dst_ref=mine, send_sem=send_sems.at[3 * t + j], recv_sem=recv_sems.at[3 * t + j],
                    device_id=(ox, oy, c), device_id_type=MESH).start()
        write(token, refs[:n_in])

    sems = pltpu.SemaphoreType.DMA((3 * nt,))
    out = pl.pallas_call(
        body, name="gather_start", in_specs=[HBM_SPEC] * nt + given_specs,
        out_specs=(SEM_SPEC, SEM_SPEC, *[HBM_SPEC] * nt, pl.BlockSpec(memory_space=pltpu.VMEM)),
        out_shape=(sems, sems, *[pltpu.HBM(f.shape, f.dtype) for f in fulls], token_type),
        input_output_aliases={t: 2 + t for t in range(nt)}, compiler_params=_split_params(),
    )(*[_in_hbm(f) for f in fulls], *given)
    return out[0], out[1], list(out[2:2 + nt]), out[-1]


def gather_wait(name, send_sems, recv_sems, fulls, kinds, shard_shapes, after, first):
    nt = len(fulls)
    extra = [] if after is None else [_in_hbm(after)]

    def body(*refs):
        full_refs, send_ref, recv_ref = refs[:nt], refs[nt], refs[nt + 1]
        x, y, c, others = _position()
        for t in range(nt):
            mine = _window(full_refs[t], kinds[t], 2 * x + y, c, shard_shapes[t])
            for j, (ox, oy) in enumerate(others):
                cp = pltpu.make_async_remote_copy(
                    src_ref=mine, dst_ref=_window(full_refs[t], kinds[t], 2 * ox + oy, c, shard_shapes[t]),
                    send_sem=send_ref.at[3 * (first + t) + j], recv_sem=recv_ref.at[3 * (first + t) + j],
                    device_id=(ox, oy, c), device_id_type=MESH)
                cp.wait_send()
                cp.wait_recv()

    out = pl.pallas_call(
        body, name=name, in_specs=[HBM_SPEC] * nt + [SEM_SPEC, SEM_SPEC] + [HBM_SPEC] * len(extra),
        out_specs=[HBM_SPEC] * nt, out_shape=[pltpu.HBM(f.shape, f.dtype) for f in fulls],
        input_output_aliases={t: t for t in range(nt)}, compiler_params=_split_params())(*fulls, send_sems, recv_sems, *extra)
    return list(out)


def forward_halves(name, fulls, kinds, shard_shapes):
    nt = len(fulls)

    def body(*refs):
        out_refs = refs[nt:2 * nt]
        send_sems, recv_sems = refs[2 * nt:]
        x, y, c, others = _position()
        cps = []
        for t in range(nt):
            for j, (ox, oy) in enumerate(others):
                landed = _window(out_refs[t], kinds[t], 2 * ox + oy, c, shard_shapes[t])
                cp = pltpu.make_async_remote_copy(
                    src_ref=landed, dst_ref=landed, send_sem=send_sems.at[3 * t + j], recv_sem=recv_sems.at[3 * t + j],
                    device_id=(x, y, 1 - c), device_id_type=MESH)
                cp.start()
                cps.append(cp)
        for t in range(nt):
            for j, (ox, oy) in enumerate(others):
                got = _window(out_refs[t], kinds[t], 2 * ox + oy, 1 - c, shard_shapes[t])
                pltpu.make_async_remote_copy(
                    src_ref=got, dst_ref=got, send_sem=send_sems.at[3 * t + j], recv_sem=recv_sems.at[3 * t + j],
                    device_id=(x, y, 1 - c), device_id_type=MESH).wait_recv()
        for cp in cps:
            cp.wait_send()

    out = pl.pallas_call(
        body, in_specs=[ANY_SPEC] * nt, out_specs=[ANY_SPEC] * nt, out_shape=[_sds(f.shape, f.dtype) for f in fulls],
        input_output_aliases={t: t for t in range(nt)},
        scratch_shapes=[pltpu.SemaphoreType.DMA((3 * nt,)), pltpu.SemaphoreType.DMA((3 * nt,))],
        name=name, compiler_params=_params())(*fulls)
    return list(out)


def forward_start(name, send_sems, recv_sems, fulls, kinds, shard_shapes, after, first, carry):
    nt = len(fulls)
    given, given_specs, token_type, write = _hand_through(carry)
    n_in = nt + 3 + len(given)

    def body(*refs):
        full_refs, ici_send, ici_recv = refs[:nt], refs[nt], refs[nt + 1]
        send_ref, recv_ref, token = refs[n_in], refs[n_in + 1], refs[-1]
        x, y, c, others = _position()
        for t in range(nt):
            mine = _window(full_refs[t], kinds[t], 2 * x + y, c, shard_shapes[t])
            for j, (ox, oy) in enumerate(others):
                landed = _window(full_refs[t], kinds[t], 2 * ox + oy, c, shard_shapes[t])
                cp = pltpu.make_async_remote_copy(
                    src_ref=mine, dst_ref=landed, send_sem=ici_send.at[3 * (first + t) + j],
                    recv_sem=ici_recv.at[3 * (first + t) + j], device_id=(ox, oy, c), device_id_type=MESH)
                cp.wait_send()
                cp.wait_recv()
                pltpu.make_async_remote_copy(
                    src_ref=landed, dst_ref=landed, send_sem=send_ref.at[3 * t + j], recv_sem=recv_ref.at[3 * t + j],
                    device_id=(x, y, 1 - c), device_id_type=MESH).start()
        write(token, refs[:n_in])

    sems = pltpu.SemaphoreType.DMA((3 * nt,))
    out = pl.pallas_call(
        body, name=name, in_specs=[HBM_SPEC] * nt + [SEM_SPEC, SEM_SPEC, HBM_SPEC] + given_specs,
        out_specs=(SEM_SPEC, SEM_SPEC, *[HBM_SPEC] * nt, pl.BlockSpec(memory_space=pltpu.VMEM)),
        out_shape=(sems, sems, *[pltpu.HBM(f.shape, f.dtype) for f in fulls], token_type),
        input_output_aliases={t: 2 + t for t in range(nt)}, compiler_params=_split_params(),
    )(*fulls, send_sems, recv_sems, _in_hbm(after), *given)
    return out[0], out[1], list(out[2:2 + nt]), out[-1]


def forward_wait(name, send_sems, recv_sems, fulls, kinds, shard_shapes, after):
    nt = len(fulls)

    def body(*refs):
        full_refs, send_ref, recv_ref = refs[:nt], refs[nt], refs[nt + 1]
        x, y, c, others = _position()
        for t in range(nt):
            for j, (ox, oy) in enumerate(others):
                cp = pltpu.make_async_remote_copy(
                    src_ref=_window(full_refs[t], kinds[t], 2 * ox + oy, c, shard_shapes[t]),
                    dst_ref=_window(full_refs[t], kinds[t], 2 * ox + oy, 1 - c, shard_shapes[t]),
                    send_sem=send_ref.at[3 * t + j], recv_sem=recv_ref.at[3 * t + j],
                    device_id=(x, y, 1 - c), device_id_type=MESH)
                cp.wait_send()
                cp.wait_recv()

    return list(pl.pallas_call(
        body, name=name, in_specs=[HBM_SPEC] * nt + [SEM_SPEC, SEM_SPEC, HBM_SPEC], out_specs=[HBM_SPEC] * nt,
        out_shape=[pltpu.HBM(f.shape, f.dtype) for f in fulls], input_output_aliases={t: t for t in range(nt)},
        compiler_params=_split_params())(*fulls, send_sems, recv_sems, _in_hbm(after)))


def _piece(ref, kind, chip, shard_shape):
    r, n = shard_shape
    if kind == "col":
        return ref.at[:, pl.ds(pl.multiple_of(chip * n, 128), n)]
    return ref.at[pl.ds(pl.multiple_of(chip * r, 16), r), :]


def _piece_shape(kind, shard_shape):
    r, n = shard_shape
    return (r // 2, n) if kind == "col" else (r, n // 2)


def exchange_start(name, parts, kinds, shard_shapes, carry):
    nt = len(parts)
    lands = [lax.empty((3,) + _piece_shape(kinds[t], shard_shapes[t]), BF16) for t in range(nt)]
    given, given_specs, token_type, write = _hand_through(carry)
    n_in = 2 * nt + len(given)

    def body(*refs):
        part_refs, land_refs = refs[:nt], refs[nt:2 * nt]
        send_sems, recv_sems, token = refs[n_in], refs[n_in + 1], refs[-1]
        x, y, c, others = _position()
        for t in range(nt):
            for j, (ox, oy) in enumerate(others):
                pltpu.make_async_remote_copy(
                    src_ref=_piece(part_refs[t], kinds[t], 2 * ox + oy, shard_shapes[t]), dst_ref=land_refs[t].at[j],
                    send_sem=send_sems.at[3 * t + j], recv_sem=recv_sems.at[3 * t + j],
                    device_id=(ox, oy, c), device_id_type=MESH).start()
        write(token, refs[:n_in])

    sems = pltpu.SemaphoreType.DMA((3 * nt,))
    both = list(parts) + lands
    out = pl.pallas_call(
        body, name=name, in_specs=[HBM_SPEC] * (2 * nt) + given_specs,
        out_specs=(SEM_SPEC, SEM_SPEC, *[HBM_SPEC] * (2 * nt), pl.BlockSpec(memory_space=pltpu.VMEM)),
        out_shape=(sems, sems, *[pltpu.HBM(a.shape, a.dtype) for a in both], token_type),
        input_output_aliases={t: 2 + t for t in range(2 * nt)}, compiler_params=_split_params(),
    )(*[_in_hbm(a) for a in both], *given)
    return out[0], out[1], list(out[2:2 + nt]), list(out[2 + nt:2 + 2 * nt]), out[-1]


def exchange_wait(name, send_sems, recv_sems, parts, lands, kinds, shard_shapes, after):
    nt = len(parts)

    def body(*refs):
        part_refs, land_refs = refs[:nt], refs[nt:2 * nt]
        send_ref, recv_ref = refs[2 * nt], refs[2 * nt + 1]
        x, y, c, others = _position()
        for t in range(nt):
            for j, (ox, oy) in enumerate(others):
                cp = pltpu.make_async_remote_copy(
                    src_ref=_piece(part_refs[t], kinds[t], 2 * ox + oy, shard_shapes[t]), dst_ref=land_refs[t].at[j],
                    send_sem=send_ref.at[3 * t + j], recv_sem=recv_ref.at[3 * t + j],
                    device_id=(ox, oy, c), device_id_type=MESH)
                cp.wait_send()
                cp.wait_recv()

    both = list(parts) + list(lands)
    out = pl.pallas_call(
        body, name=name, in_specs=[HBM_SPEC] * (2 * nt) + [SEM_SPEC, SEM_SPEC, HBM_SPEC], out_specs=[HBM_SPEC] * (2 * nt),
        out_shape=[pltpu.HBM(a.shape, a.dtype) for a in both], input_output_aliases={t: t for t in range(2 * nt)},
        compiler_params=_split_params())(*both, send_sems, recv_sems, _in_hbm(after))
    return list(out[:nt]), list(out[nt:])


def all_reduce_small(name, bufs, wire):
    n = len(bufs)
    halves = [b.shape[0] // 2 for b in bufs]

    def body(*refs):
        in_refs, out_refs, lands, txs = refs[:n], refs[n:2 * n], refs[2 * n:3 * n], refs[3 * n:4 * n]
        send_sems, recv_sems = refs[4 * n:]
        x, y, c, _ = _position()
        mine = [pl.ds(pl.multiple_of(c * h, 8), h) for h in halves]
        other = [pl.ds(pl.multiple_of((1 - c) * h, 8), h) for h in halves]
        for s, peer in enumerate([(x, y, 1 - c), (1 - x, y, c), (x, 1 - y, c)]):
            cps = []
            for k in range(n):
                txs[k][...] = (in_refs[k][other[k], :] if s == 0 else out_refs[k][mine[k], :]).astype(wire[k])
                cp = pltpu.make_async_remote_copy(
                    src_ref=txs[k], dst_ref=lands[k].at[s], send_sem=send_sems.at[4 * k + s], recv_sem=recv_sems.at[4 * k + s],
                    device_id=peer, device_id_type=MESH)
                cp.start()
                cps.append(cp)
            for k, cp in enumerate(cps):
                cp.wait()
                own = in_refs[k][mine[k], :] if s == 0 else out_refs[k][mine[k], :]
                out_refs[k][mine[k], :] = own.astype(wire[k]).astype(F32) + lands[k][s].astype(F32)
        cps = []
        for k in range(n):
            cp = pltpu.make_async_remote_copy(
                src_ref=out_refs[k].at[mine[k]], dst_ref=out_refs[k].at[mine[k]], send_sem=send_sems.at[4 * k + 3],
                recv_sem=recv_sems.at[4 * k + 3], device_id=(x, y, 1 - c), device_id_type=MESH)
            cp.start()
            cps.append(cp)
        for cp in cps:
            cp.wait()

    vm = pl.BlockSpec(memory_space=pltpu.VMEM)
    out = pl.pallas_call(
        body, in_specs=[vm] * n, out_specs=[vm] * n, out_shape=[_sds(b.shape, F32) for b in bufs],
        scratch_shapes=[pltpu.VMEM((3, h, b.shape[1]), w) for h, b, w in zip(halves, bufs, wire)]
        + [pltpu.VMEM((h, b.shape[1]), w) for h, b, w in zip(halves, bufs, wire)]
        + [pltpu.SemaphoreType.DMA((4 * n,)), pltpu.SemaphoreType.DMA((4 * n,))],
        name=name, compiler_params=_params())(*bufs)
    return list(out)


def _local_step(x, target, small, need, ahead, emit_swap, emit_exchange):
    d = D_MODEL
    full = {}

    def handed(vec, token):
        return vec if token is None else token

    def token_rows(token):
        return [] if token is None else [token]

    def plus(acc, rows):
        return acc + rows[0] if rows else acc

    rb16, rbt16, rc16, rct16, lr_t, li_t = small["s5_operands"]
    ge, y2, cs = s5_fwd(x, small["norm_mix0"], small["s5_d"], rb16, rc16, lr_t, li_t)
    full.update(need("glu", ge))

    def norm_rows(h, gains):
        xh, _ = _rms_hat(h)
        return [xh * g for g in gains]

    def glu_epilogue(accs, e, r):
        v, gt = accs[0] + r[0], accs[1] + r[1]
        h = e[0] + v * jax.nn.sigmoid(gt)
        return [h, v, gt] + norm_rows(h, r[2:])

    gain_mlp0 = handed(small["norm_mlp0"], ahead("mlp_in0", full["w_glu"], small["norm_mlp0"]))
    h1, val, gate, n1 = mm_nn(
        "glu", ge, full["w_glu"], [0, d], d, glu_epilogue, [F32, F32, F32, BF16], extras=[x],
        rowvecs=[(small["s5_b_glu"], 0), (small["s5_b_glu"], d), (gain_mlp0, 0)], tm=512, tn=d)

    def mlp_fwd(tag, h, n, w_in, get_w_out, next_gains, head=None):
        def in_epilogue(accs, e, rv):
            pos = jnp.maximum(accs[0], 0.0)
            return [pos * pos, 2.0 * pos]

        r, slope = mm_nn("mlp_in" + tag, n, w_in, [0], w_in.shape[1], in_epilogue, [BF16, BF16], tm=2048)
        w_out = get_w_out(r)

        def epilogue(accs, e, rv):
            h_out = e[0] + accs[0]
            return [h_out] + norm_rows(h_out, rv)

        if head is not None:
            return head(r, w_out, h), (n, r, slope)
        outs = mm_nn("mlp_out" + tag, r, w_out, [0], d, epilogue, [F32] + [BF16] * len(next_gains), extras=[h],
                     rowvecs=[(g, 0) for g in next_gains], tm=512, tn=d)
        return outs[0], outs[1:], (n, r, slope)

    full.update(need("mlp_in0", h1))

    def w_out0(after):
        full.update(need("mlp_out0", after))
        return full["w_out0"]

    h2, (nkv, n2), mlp0 = mlp_fwd("0", h1, n1, full["w_in0"], w_out0, [small["norm_kv"], small["norm_mix1"]])

    full.update(need("attn", h2))
    kvw = 2 * N_KV * HEAD_DIM
    (kv,) = mm_nn("kv_proj", nkv, full["w_kv"], [0], kvw, lambda accs, e, r: [accs[0] + r[0]], [BF16],
                  rowvecs=[(small["b_kv"], 0)], tm=2048)
    (q,) = mm_nn("q_proj", n2, full["w_q"], [0], d, lambda accs, e, r: [accs[0] + r[0]], [BF16],
                 rowvecs=[(small["b_q"], 0)], tm=2048)
    sinks = small["sinks"].reshape(N_Q)
    o = attn_fwd(q, kv, sinks)
    def o_epilogue(accs, e, r):
        h_out = e[0] + accs[0] + r[0]
        return [h_out] + norm_rows(h_out, r[1:])

    bias_o = handed(small["b_o"], ahead("mlp_in1", o, small["b_o"]))
    h3, n3 = mm_nn("o_proj", o, full["w_o"], [0], d, o_epilogue, [F32, BF16], extras=[h2],
                   rowvecs=[(bias_o, 0), (small["norm_mlp1"], 0)], tm=512, tn=d)
    full.update(need("mlp_in1", h3))

    def w_out1(after):
        full.update(need("mlp_out1", after))
        return full["w_out1"]

    def loss_head(r, w_out, h):
        def epilogue(accs, e, rv):
            xh, rr = _rms_hat(e[0] + accs[0])
            err = xh * rv[0] - e[1]
            dy = err * (1.0 / d)
            dxh = dy * rv[0]
            dx = rr * (dxh - xh * jnp.mean(dxh * xh, axis=-1, keepdims=True))
            loss = jnp.full((1, d), 0.5 * jnp.sum(jnp.mean(err * err, axis=-1, keepdims=True)), F32)
            return [dx, dx, loss, jnp.sum(dy * xh, axis=0, keepdims=True)]

        return mm_nn("mlp_out1", r, w_out, [0], d, epilogue, [F32, BF16], extras=[h, target],
                     rowvecs=[(small["norm_final"], 0)], n_sums=2, tm=512, tn=d)

    (dh, dhb, loss_tile, dg_final), mlp1 = mlp_fwd("1", h3, n3, full["w_in1"], w_out1, [], head=loss_head)

    grads_small, grads_full = {"norm_final": dg_final}, {}
    ident = lambda acc, e, r: [plus(acc, r)]
    layer1 = ["w_out1", "w_in1", "w_o", "w_q", "w_kv"]
    layer0 = ["w_out0", "w_in0", "w_glu"]

    def norm_bwd_rows(x_rows, res, dys, gains):
        xh, r = _rms_hat(x_rows)
        dxh = sum(dy * g for dy, g in zip(dys, gains))
        dx = r * (dxh - xh * jnp.mean(dxh * xh, axis=-1, keepdims=True)) + res
        return dx, [jnp.sum(dy * xh, axis=0, keepdims=True) for dy in dys]

    def mlp_bwd(tag, dh, dhb, h_in, gain, w_in, w_out, saved, token=None):
        n, r, slope = saved
        grads_full["w_out" + tag] = mm_tn("dw_out" + tag, r, dhb, tn=1024)
        (da,) = mm_nt("mlp_da" + tag, dhb, w_out, lambda acc, e, rv: [plus(acc * e[0].astype(F32), rv)], [BF16],
                      extras=[slope], rowvecs=token_rows(token), tm=2048)
        grads_full["w_in" + tag] = mm_tn("dw_in" + tag, n, da, tn=1024)

        def epilogue(acc, e, rv):
            dx, dgs = norm_bwd_rows(e[0], e[1], [acc], rv)
            return [dx, dx, jnp.sum(dx, axis=0, keepdims=True)] + dgs

        dx, dxb, colsum, dg = mm_nt("mlp_dn" + tag, da, w_in, epilogue, [F32, BF16], extras=[h_in, dh], rowvecs=[gain],
                                    n_sums=2, tm=512, tk=d)
        grads_small["norm_mlp" + tag] = dg
        return dx, dxb, colsum

    dh3, dh3b, colsum3 = mlp_bwd("1", dh, dhb, h3, small["norm_mlp1"], full["w_in1"], full["w_out1"], mlp1)
    grads_small["b_o"] = colsum3
    grads_full["w_o"] = mm_tn("dw_o", o, dh3b, tn=1024)
    (do,) = mm_nt("attn_do", dh3b, full["w_o"], ident, [BF16], tm=2048)
    dq, dbq, dprev, dcur, dsink = attn_bwd(q, kv, do, sinks)
    dkv, dbkv = kv_combine(dprev, dcur)
    grads_small["b_q"], grads_small["b_kv"], grads_small["sinks"] = dbq, dbkv, dsink
    grads_full["w_q"] = mm_tn("dw_q", n2, dq, tn=1024)
    grads_full["w_kv"] = mm_tn("dw_kv", nkv, dkv, tk=1024)
    token = emit_swap("layer1", {n: grads_full[n] for n in layer1}, (1, d))
    (dnkv,) = mm_nt("kv_dn", dkv, full["w_kv"], ident, [F32], rowvecs=token_rows(token), tm=2048, tk=1024)

    def attn_dn_epilogue(acc, e, rv):
        dx, dgs = norm_bwd_rows(e[0], e[1], [acc, e[2]], rv)
        return [dx, dx] + dgs

    dh2, dh2b, dg_mix1, dg_kv = mm_nt("attn_dn", dq, full["w_q"], attn_dn_epilogue, [F32, BF16], extras=[h2, dh3, dnkv],
                                      rowvecs=[small["norm_mix1"], small["norm_kv"]], n_sums=2, tm=512, tk=d)
    grads_small["norm_mix1"], grads_small["norm_kv"] = dg_mix1, dg_kv
    token = emit_exchange("layer1", dh2b, (1, full["w_out0"].shape[0]))
    dh1, _, _ = mlp_bwd("0", dh2, dh2b, h1, small["norm_mlp0"], full["w_in0"], full["w_out0"], mlp0, token)

    dz, db_glu = glu_bwd(dh1, val, gate)
    grads_small["s5_b_glu"] = db_glu
    grads_full["w_glu"] = mm_tn("dw_glu", ge, dz, tn=1024)
    token = emit_swap("layer0", {n: grads_full[n] for n in layer0}, (1, d))
    (dy2,) = mm_nt("glu_dy", dz, full["w_glu"], lambda acc, e, rv: [plus(acc, rv) * _gelu_grad(e[0])], [F32], extras=[y2],
                   rowvecs=token_rows(token), tm=1024, tk=1024)
    d_skip = handed(small["s5_d"], emit_exchange("layer0", dy2, small["s5_d"]))
    grad_x, dd, drb, drc, dlr, dli, dg_mix0 = s5_bwd(x, small["norm_mix0"], dy2, dh1, d_skip, cs, rb16, rbt16, rct16, lr_t, li_t)
    grads_small["s5_d"] = dd
    grads_small["s5_mats"] = (drb, drc, dlr, dli)
    grads_small["norm_mix0"] = dg_mix0
    return loss_tile, grad_x, grads_small


SMALL_NAMES = ["norm_mix", "norm_mlp", "norm_kv", "norm_final", "s5_a_re", "s5_a_im", "s5_log_dt", "s5_b_re", "s5_b_im",
               "s5_c_re", "s5_c_im", "s5_d", "s5_b_glu", "b_kv", "b_q", "sinks", "b_o"]
BIG_NAMES = ["s5_w_glu", "w_kv", "w_q", "w_o", "w_mlp_in", "w_mlp_out"]
WEIGHT_ORDER = ["norm_mix", "norm_mlp", "norm_kv", "norm_final", "s5_a_re", "s5_a_im", "s5_log_dt", "s5_b_re", "s5_b_im",
                "s5_c_re", "s5_c_im", "s5_d", "s5_w_glu", "s5_b_glu", "w_kv", "b_kv", "w_q", "b_q", "sinks", "w_o", "b_o",
                "w_mlp_in", "w_mlp_out"]


def kernel(x, norm_mix, norm_mlp, norm_kv, norm_final, s5_a_re, s5_a_im, s5_log_dt, s5_b_re, s5_b_im, s5_c_re, s5_c_im, s5_d, s5_w_glu, s5_b_glu, w_kv, b_kv, w_q, b_q, sinks, w_o, b_o, w_mlp_in, w_mlp_out, loss_target, m_norm_mix, m_norm_mlp, m_norm_kv, m_norm_final, m_s5_a_re, m_s5_a_im, m_s5_log_dt, m_s5_b_re, m_s5_b_im, m_s5_c_re, m_s5_c_im, m_s5_d, m_s5_w_glu, m_s5_b_glu, m_w_kv, m_b_kv, m_w_q, m_b_q, m_sinks, m_w_o, m_b_o, m_w_mlp_in, m_w_mlp_out, v_norm_mix, v_norm_mlp, v_norm_kv, v_norm_final, v_s5_a_re, v_s5_a_im, v_s5_log_dt, v_s5_b_re, v_s5_b_im, v_s5_c_re, v_s5_c_im, v_s5_d, v_s5_w_glu, v_s5_b_glu, v_w_kv, v_b_kv, v_w_q, v_b_q, v_sinks, v_w_o, v_b_o, v_w_mlp_in, v_w_mlp_out):
    env = dict(locals())
    w = {n: env[n] for n in WEIGHT_ORDER}
    mom = {n: env["m_" + n] for n in WEIGHT_ORDER}
    var = {n: env["v_" + n] for n in WEIGHT_ORDER}
    d = D_MODEL
    xi, yi, ci = lax.axis_index("x"), lax.axis_index("y"), lax.axis_index("c")
    chip = 2 * xi + yi
    where = jnp.stack([ci, chip]).astype(jnp.int32)

    dsh, bsh = s5_d.shape[1], s5_b_glu.shape[1]
    packed = jnp.concatenate([s5_d.reshape(-1, 128), s5_b_glu.reshape(-1, 128)])
    n_d, n_b = dsh // 128, bsh // 128
    slab = lax.dynamic_update_slice(jnp.zeros((4, 8, 128), F32), jnp.pad(packed, ((0, 8 - n_d - n_b), (0, 0)))[None],
                                    (chip, 0, 0))

    big = [s5_w_glu, w_kv[None], w_q, w_o, w_mlp_in, w_mlp_out]
    entries = [(0, 0, "col"), (1, 0, "row"), (2, 0, "row"), (3, 0, "row"), (4, 0, "col"), (4, 1, "col"),
               (5, 0, "row"), (5, 1, "row")]
    names = ["w_glu", "w_kv", "w_q", "w_o", "w_in0", "w_in1", "w_out0", "w_out1"]
    kinds = dict(zip(names, [k for _, _, k in entries]))
    shard_shapes = dict(zip(names, [tuple(big[a].shape[1:]) for a, _, _ in entries]))

    placed_w = dict(zip(names, cast_place(big, entries, where)))
    placed_w["vectors"], kinds["vectors"], shard_shapes["vectors"] = slab, "slab", None
    gather_groups = {"glu": ["w_glu"], "mlp_in0": ["w_in0"], "mlp_out0": ["w_out0"], "attn": ["w_kv", "w_q", "w_o"],
                     "mlp_in1": ["w_in1"], "mlp_out1": ["w_out1"]}
    order = ["vectors"] + [n for members in gather_groups.values() for n in members]
    send, recv, thru, log_dt = gather_start([placed_w[n] for n in order], [kinds[n] for n in order],
                                            [shard_shapes[n] for n in order], s5_log_dt)
    started = dict(zip(order, thru))
    (gathered_rows,) = gather_wait("gather_wait_vectors", send, recv, [started["vectors"]], ["slab"], [None], None, 0)
    d_full = gathered_rows[:, 0:n_d].reshape(1, -1)
    bglu_full = gathered_rows[:, n_d:n_d + n_b].reshape(1, -1)

    forwarding = {}

    def ahead(group, after, carry):
        members = gather_groups[group]
        ks, shapes = [kinds[n] for n in members], [shard_shapes[n] for n in members]
        d2d_send, d2d_recv, landed, tok = forward_start(
            "forward_start_" + group, send, recv, [started[n] for n in members], ks, shapes, after,
            order.index(members[0]), carry)
        forwarding[group] = (d2d_send, d2d_recv, landed)
        return tok

    def need(group, after):
        members = gather_groups[group]
        ks, shapes = [kinds[n] for n in members], [shard_shapes[n] for n in members]
        if group in forwarding:
            return dict(zip(members, forward_wait("forward_wait_" + group, *forwarding[group], ks, shapes, after)))
        landed = gather_wait("gather_wait_" + group, send, recv, [started[n] for n in members], ks, shapes, after,
                             order.index(members[0]))
        return dict(zip(members, forward_halves("forward_halves_" + group, landed, ks, shapes)))

    swapping, exchanging = {}, {}

    def emit_swap(group, partial, carry):
        members = list(partial)
        send, recv, mine, lands, tok = swap_start("swap_start_" + group, [partial[n] for n in members],
                                                  [kinds[n] for n in members], carry)
        swapping[group] = (members, send, recv, mine, lands)
        return tok

    def emit_exchange(group, after, carry):
        members, send, recv, mine, lands = swapping[group]
        ks, shapes = [kinds[n] for n in members], [shard_shapes[n] for n in members]
        mine, landed = swap_wait("swap_wait_" + group, send, recv, mine, lands, ks, after)
        sums = add_halves("add_halves_" + group, mine, landed, ks, where)
        send, recv, parts, lands, tok = exchange_start("exchange_start_" + group, sums, ks, shapes, carry)
        exchanging[group] = (members, send, recv, parts, lands)
        return tok

    s5_args = (s5_a_re[0], s5_a_im[0], log_dt[0], s5_b_re[0], s5_b_im[0])
    small = {
        "norm_mix0": norm_mix[0:1], "norm_mix1": norm_mix[1:2], "norm_mlp0": norm_mlp[0:1], "norm_mlp1": norm_mlp[1:2],
        "norm_kv": norm_kv.reshape(1, d), "norm_final": norm_final.reshape(1, d), "s5_operands": s5_prep(*s5_args, s5_c_re[0], s5_c_im[0]),
        "s5_d": d_full, "s5_b_glu": bglu_full,
        "b_kv": b_kv.reshape(1, -1), "b_q": b_q, "sinks": sinks, "b_o": b_o,
    }
    loss_row, grad_x, gs = _local_step(x[0], loss_target[0], small, need, ahead, emit_swap, emit_exchange)

    reduced = [None] * len(big)
    where_of = dict(zip(names, entries))
    for group in ("layer1", "layer0"):
        members, send, recv, parts, lands = exchanging[group]
        ks, shapes = [kinds[n] for n in members], [shard_shapes[n] for n in members]
        parts, lands = exchange_wait("exchange_wait_" + group, send, recv, parts, lands, ks, shapes, grad_x)
        targets = [where_of[n][0] for n in members]
        sums = sum_shards("sum_shards_" + group, parts, lands, ks, shapes, where, [where_of[n][1] for n in members],
                          [big[a].shape[0] for a in targets], [reduced[a] for a in targets])
        for a, arr in zip(targets, sums):
            reduced[a] = arr
    share_send, share_recv, reduced, shared = share_start(reduced, entries, (2, d))

    mats, lams = s5_compact(*gs["s5_mats"])
    rows = [gs["norm_mix0"], gs["norm_mix1"], gs["norm_mlp0"], gs["norm_mlp1"], gs["norm_kv"], gs["norm_final"], gs["s5_d"],
            gs["b_q"], gs["b_o"], gs["s5_b_glu"], gs["b_kv"], gs["sinks"], loss_row, shared]
    vecs, lams, mats = all_reduce_small("reduce_small", [jnp.concatenate(rows, axis=0), lams, mats], [F32, F32, BF16])
    grads = split_vectors(where, vecs, dsh, bsh)
    loss = grads.pop("loss")[0, 0]
    g_are, g_aim, g_dt, g_bre, g_bim, dc_re, dc_im = s5_param_bwd(mats, lams, *s5_args)
    grads.update({"s5_a_re": g_are[None], "s5_a_im": g_aim[None], "s5_log_dt": g_dt[None], "s5_b_re": g_bre[None],
                  "s5_b_im": g_bim[None], "s5_c_re": dc_re[None], "s5_c_im": dc_im[None]})

    delta, new_m, new_v = {}, {}, {}

    def view(n, a):
        return a.reshape(1, -1) if a.ndim == 1 else jnp.swapaxes(a, -1, -2) if n in ("s5_b_re", "s5_b_im") else a

    sw, sg, sm, sv = ([view(n, t[n]) for n in SMALL_NAMES] for t in (w, grads, mom, var))
    for n, a, b, c_ in zip(SMALL_NAMES, *adamw_native("adamw_small", sw, sg, sm, sv)):
        delta[n], new_m[n], new_v[n] = (view(n, t) if t.ndim == 4 else t for t in (a, b, c_))

    reduced = share_wait(share_send, share_recv, reduced, entries, new_v["s5_c_re"])
    for n, g in zip(BIG_NAMES, reduced):
        grads[n] = g.reshape(w[n].shape)
    flat = lambda t: [t[n].reshape(-1, t[n].shape[-1]) for n in BIG_NAMES]
    for table, arrays in zip((grads, delta, new_m, new_v), adamw("adamw_big", flat(w), flat(grads), flat(mom), flat(var))):
        for n, a in zip(BIG_NAMES, arrays):
            table[n] = a.reshape(w[n].shape)

    out = [loss.reshape(()), grad_x[None]]
    for table in (grads, delta, new_m, new_v):
        out += [table[n].reshape(w[n].shape) for n in WEIGHT_ORDER]
    return tuple(out)
```

```python
import math

import jax
import jax.numpy as jnp
from jax import lax
from jax.experimental import pallas as pl
from jax.experimental.pallas import tpu as pltpu

F32 = jnp.float32
BF16 = jnp.bfloat16

D_MODEL = 1024
S5_GROUPS = 64
S5_GROUP = 16
S5_STATE = 64
N_KV = 4
N_Q = 16
HEAD_DIM = 64
BLOCK = 128
NORM_EPS = 1e-5
LAMBDA_RE_MAX = -1e-4
ADAM_LR, ADAM_B1, ADAM_B2, ADAM_EPS, ADAM_WD, ADAM_STEP = 0.001, 0.9, 0.999, 1e-08, 0.01, 10

VMEM_LIMIT_BYTES = 56 * 1024 * 1024
S5_CHUNK = 256
S5_BLOCKS = 4
MESH = pl.DeviceIdType.MESH


def _params(sem=None):
    return pltpu.CompilerParams(dimension_semantics=sem, vmem_limit_bytes=VMEM_LIMIT_BYTES)


def _sds(shape, dtype):
    return jax.ShapeDtypeStruct(shape, dtype)


def _rms_hat(xv):
    r = lax.rsqrt(jnp.mean(xv * xv, axis=-1, keepdims=True) + NORM_EPS)
    return xv * r, r


def mm_nn(name, a, w, col_offsets, n_out, epilogue, out_dtypes, extras=(), rowvecs=(), n_sums=0, tm=1024, tn=512):
    m, k = a.shape
    tm, tn = min(tm, m), min(tn, n_out)
    nw, ne, nr, no = len(col_offsets), len(extras), len(rowvecs), len(out_dtypes)

    def body(a_ref, *refs):
        w_refs, e_refs, r_refs = refs[:nw], refs[nw:nw + ne], refs[nw + ne:nw + ne + nr]
        o_refs, s_refs = refs[nw + ne + nr:nw + ne + nr + no], refs[nw + ne + nr + no:]
        av = a_ref[...]
        accs = [jnp.dot(av, w_ref[...], preferred_element_type=F32) for w_ref in w_refs]
        outs = epilogue(accs, [e[...] for e in e_refs], [r[...] for r in r_refs])
        for o_ref, o in zip(o_refs, outs[:no]):
            o_ref[...] = o.astype(o_ref.dtype)
        if n_sums:
            @pl.when(pl.program_id(1) == 0)
            def _():
                for s_ref in s_refs:
                    s_ref[...] = jnp.zeros_like(s_ref)

            for s_ref, val in zip(s_refs, outs[no:]):
                s_ref[...] += val

    def wspec(off):
        return pl.BlockSpec((k, tn), lambda j, i, off=off: (0, off // tn + j))

    def rspec(off):
        return pl.BlockSpec((1, tn), lambda j, i, off=off: (0, off // tn + j))

    tile = pl.BlockSpec((tm, tn), lambda j, i: (i, j))
    in_specs = ([pl.BlockSpec((tm, k), lambda j, i: (i, 0))] + [wspec(o) for o in col_offsets]
                + [tile] * ne + [rspec(o) for _, o in rowvecs])
    sem = ("parallel", "arbitrary") if n_sums else ("parallel", "parallel")
    return pl.pallas_call(
        body, grid=(n_out // tn, m // tm), in_specs=in_specs,
        out_specs=[tile] * no + [pl.BlockSpec((1, tn), lambda j, i: (0, j))] * n_sums,
        out_shape=[_sds((m, n_out), dt) for dt in out_dtypes] + [_sds((1, n_out), F32)] * n_sums, name=name,
        compiler_params=_params(sem))(a, *([w] * nw), *extras, *[r for r, _ in rowvecs])


def mm_nt(name, g, w, epilogue, out_dtypes, extras=(), rowvecs=(), n_sums=0, tm=512, tk=512):
    m, n = g.shape
    k = w.shape[0]
    tm, tk = min(tm, m), min(tk, k)
    ne, nr, no = len(extras), len(rowvecs), len(out_dtypes)

    def body(g_ref, w_ref, *refs):
        e_refs, r_refs, o_refs, s_refs = refs[:ne], refs[ne:ne + nr], refs[ne + nr:ne + nr + no], refs[ne + nr + no:]
        acc = lax.dot_general(g_ref[...], w_ref[...], (((1,), (1,)), ((), ())), preferred_element_type=F32)
        outs = epilogue(acc, [e[...] for e in e_refs], [r[...] for r in r_refs])
        for o_ref, o in zip(o_refs, outs[:no]):
            o_ref[...] = o.astype(o_ref.dtype)
        if n_sums:
            @pl.when(pl.program_id(0) == 0)
            def _():
                for s_ref in s_refs:
                    s_ref[...] = jnp.zeros_like(s_ref)

            for s_ref, val in zip(s_refs, outs[no:]):
                s_ref[...] += val

    tile = pl.BlockSpec((tm, tk), lambda i, j: (i, j))
    vec = pl.BlockSpec((1, tk), lambda i, j: (0, j))
    sem = ("arbitrary", "parallel") if n_sums else ("parallel", "parallel")
    return pl.pallas_call(
        body, grid=(m // tm, k // tk),
        in_specs=[pl.BlockSpec((tm, n), lambda i, j: (i, 0)), pl.BlockSpec((tk, n), lambda i, j: (j, 0))]
        + [tile] * ne + [vec] * nr,
        out_specs=[tile] * no + [vec] * n_sums,
        out_shape=[_sds((m, k), dt) for dt in out_dtypes] + [_sds((1, k), F32)] * n_sums, name=name,
        compiler_params=_params(sem))(g, w, *extras, *rowvecs)


def mm_tn(name, a, g, tk=512, tn=512):
    m, k = a.shape
    n = g.shape[1]
    tk, tn = min(tk, k), min(tn, n)

    def body(a_ref, g_ref, o_ref):
        acc = lax.dot_general(a_ref[...], g_ref[...], (((0,), (0,)), ((), ())), preferred_element_type=F32)
        o_ref[...] = acc.astype(o_ref.dtype)

    return pl.pallas_call(
        body, grid=(k // tk, n // tn),
        in_specs=[pl.BlockSpec((m, tk), lambda i, j: (0, i)), pl.BlockSpec((m, tn), lambda i, j: (0, j))],
        out_specs=pl.BlockSpec((tk, tn), lambda i, j: (i, j)), out_shape=_sds((k, n), BF16), name=name,
        compiler_params=_params(("parallel", "parallel")))(a, g)


def _row_mask(tc):
    row = lax.broadcasted_iota(jnp.int32, (8 * tc, 256), 0) % 8
    col = lax.broadcasted_iota(jnp.int32, (8 * tc, 256), 1) // 32
    return row == col


def _expand_rows(val, mask):
    tc, width = val.shape
    rep = jnp.broadcast_to(val[:, None, :], (tc, 8, width)).reshape(8 * tc, width)
    return jnp.where(mask, rep, 0.0).astype(BF16)


def _stage(ref, val):
    ref[0] = val[:, 0:128]
    ref[1] = val[:, 128:256]


def _gather_rows(src_ref, tc):
    halves = []
    for half in range(2):
        col = lax.broadcasted_iota(jnp.int32, (tc, 128), 1) // 32 + 4 * half
        out = jnp.zeros((tc, 128), F32)
        for s8 in range(4 * half, 4 * half + 4):
            out = jnp.where(col == s8, src_ref.at[half][pl.ds(s8, tc, stride=8), :], out)
        halves.append(out)
    return jnp.concatenate(halves, axis=1)


def _gelu(x):
    c = math.sqrt(2.0 / math.pi)
    return 0.5 * x * (1.0 + jnp.tanh(c * (x + 0.044715 * x * x * x)))


def _gelu_grad(x):
    c = math.sqrt(2.0 / math.pi)
    t = jnp.tanh(c * (x + 0.044715 * x * x * x))
    return 0.5 * (1.0 + t) + 0.5 * x * (1.0 - t * t) * c * (1.0 + 3.0 * 0.044715 * x * x)


def s5_fwd(x, gain, d_skip, rb, rc, lam_r, lam_i):
    n_rows = x.shape[0]
    tc = min(S5_CHUNK, n_rows)
    nc = n_rows // tc

    def body(x_ref, g_ref, d_ref, rb_ref, rc_ref, lr_ref, li_ref, ge_ref, y2_ref, cs_ref, bux, yrows, carry):
        i = pl.program_id(0)
        u = _rms_hat(x_ref[...])[0] * g_ref[...]

        @pl.when(i == 0)
        def _():
            carry[...] = jnp.zeros_like(carry)

        cs_ref[0] = carry[...]
        mask = _row_mask(tc)
        for blk in range(S5_BLOCKS):
            lhs = _expand_rows(u[:, blk * 256:(blk + 1) * 256], mask)
            bux[blk] = jnp.dot(lhs, rb_ref[blk], preferred_element_type=F32)
        lam = [(lr_ref[blk], li_ref[blk]) for blk in range(S5_BLOCKS)]

        def step(t, c):
            r0 = pl.multiple_of(t * 8, 8)
            new = []
            for blk in range(S5_BLOCKS):
                xr, xi = c[2 * blk], c[2 * blk + 1]
                lr, li = lam[blk]
                nr = lr * xr - li * xi + bux[blk, pl.ds(r0, 8), 0:128]
                ni = lr * xi + li * xr + bux[blk, pl.ds(r0, 8), 128:256]
                bux[blk, pl.ds(r0, 8), 0:128] = nr
                bux[blk, pl.ds(r0, 8), 128:256] = ni
                new += [nr, ni]
            return tuple(new)

        c0 = []
        for blk in range(S5_BLOCKS):
            c0 += [carry[blk, :, 0:128], carry[blk, :, 128:256]]
        cn = lax.fori_loop(0, tc, step, tuple(c0), unroll=4)
        for blk in range(S5_BLOCKS):
            carry[blk, :, 0:128] = cn[2 * blk]
            carry[blk, :, 128:256] = cn[2 * blk + 1]
        for blk in range(S5_BLOCKS):
            _stage(yrows, jnp.dot(bux[blk].astype(BF16), rc_ref[blk], preferred_element_type=F32))
            sl = slice(blk * 256, (blk + 1) * 256)
            y2 = _gather_rows(yrows, tc) + d_ref[:, sl] * u[:, sl]
            y2_ref[:, sl] = y2
            ge_ref[:, sl] = _gelu(y2).astype(BF16)

    row = pl.BlockSpec((tc, D_MODEL), lambda i: (i, 0))
    vec = pl.BlockSpec((1, D_MODEL), lambda i: (0, 0))
    mat = pl.BlockSpec((S5_BLOCKS, 256, 256), lambda i: (0, 0, 0))
    lamspec = pl.BlockSpec((S5_BLOCKS, 8, 128), lambda i: (0, 0, 0))
    return pl.pallas_call(
        body, grid=(nc,),
        in_specs=[row, vec, vec, mat, mat, lamspec, lamspec],
        out_specs=[row, row, pl.BlockSpec((1, S5_BLOCKS, 8, 256), lambda i: (i, 0, 0, 0))],
        out_shape=[_sds((n_rows, D_MODEL), BF16), _sds((n_rows, D_MODEL), F32), _sds((nc, S5_BLOCKS, 8, 256), F32)],
        scratch_shapes=[pltpu.VMEM((S5_BLOCKS, 8 * tc, 256), F32), pltpu.VMEM((2, 8 * tc, 128), F32),
                        pltpu.VMEM((S5_BLOCKS, 8, 256), F32)],
        name="s5_fwd", compiler_params=_params(("arbitrary",)))(x, gain, d_skip, rb, rc, lam_r, lam_i)


def s5_bwd(x, gain, dy2, res, d_skip, cs, rb, rbt, rct, lam_r, lam_i):
    n_rows = x.shape[0]
    tc = min(S5_CHUNK, n_rows)
    nc = n_rows // tc

    def body(x_ref, g_ref, dy_ref, res_ref, d_ref, cs_ref, rb_ref, rbt_ref, rct_ref, lr_ref, li_ref,
             dx_ref, dd_ref, drb_ref, drc_ref, dlr_ref, dli_ref, dg_ref, tmp, du, lhsu, lhsd, xs, adj, acarry):
        i = pl.program_id(0)
        u = _rms_hat(x_ref[...])[0] * g_ref[...]

        @pl.when(i == 0)
        def _():
            acarry[...] = jnp.zeros_like(acarry)
            dd_ref[...] = jnp.zeros_like(dd_ref)
            drb_ref[...] = jnp.zeros_like(drb_ref)
            drc_ref[...] = jnp.zeros_like(drc_ref)
            dlr_ref[...] = jnp.zeros_like(dlr_ref)
            dli_ref[...] = jnp.zeros_like(dli_ref)
            dg_ref[...] = jnp.zeros_like(dg_ref)

        dd_ref[...] += jnp.sum(dy_ref[...] * u, axis=0, keepdims=True)
        mask = _row_mask(tc)
        for blk in range(S5_BLOCKS):
            sl = slice(blk * 256, (blk + 1) * 256)
            lhsu[blk] = _expand_rows(u[:, sl], mask)
            xs[blk] = jnp.dot(lhsu[blk], rb_ref[blk], preferred_element_type=F32)
            lhsd[blk] = _expand_rows(dy_ref[:, sl], mask)
            adj[blk] = jnp.dot(lhsd[blk], rct_ref[blk], preferred_element_type=F32)
        lam = [(lr_ref[blk], li_ref[blk]) for blk in range(S5_BLOCKS)]

        def fstep(t, c):
            r0 = pl.multiple_of(t * 8, 8)
            new = []
            for blk in range(S5_BLOCKS):
                xr, xi = c[2 * blk], c[2 * blk + 1]
                lr, li = lam[blk]
                nr = lr * xr - li * xi + xs[blk, pl.ds(r0, 8), 0:128]
                ni = lr * xi + li * xr + xs[blk, pl.ds(r0, 8), 128:256]
                xs[blk, pl.ds(r0, 8), 0:128] = nr
                xs[blk, pl.ds(r0, 8), 128:256] = ni
                new += [nr, ni]
            return tuple(new)

        c0 = []
        for blk in range(S5_BLOCKS):
            c0 += [cs_ref[0, blk, :, 0:128], cs_ref[0, blk, :, 128:256]]
        lax.fori_loop(0, tc, fstep, tuple(c0), unroll=4)

        def bstep(k, c):
            t = tc - 1 - k
            r0 = pl.multiple_of(t * 8, 8)
            rp = pl.multiple_of(jnp.maximum(t - 1, 0) * 8, 8)
            first = t == 0
            new_a, new_g = [], []
            for blk in range(S5_BLOCKS):
                ar, ai = c[0][2 * blk], c[0][2 * blk + 1]
                glr, gli = c[1][2 * blk], c[1][2 * blk + 1]
                lr, li = lam[blk]
                nr = lr * ar + li * ai + adj[blk, pl.ds(r0, 8), 0:128]
                ni = lr * ai - li * ar + adj[blk, pl.ds(r0, 8), 128:256]
                adj[blk, pl.ds(r0, 8), 0:128] = nr
                adj[blk, pl.ds(r0, 8), 128:256] = ni
                pr = jnp.where(first, cs_ref[0, blk, :, 0:128], xs[blk, pl.ds(rp, 8), 0:128])
                pi = jnp.where(first, cs_ref[0, blk, :, 128:256], xs[blk, pl.ds(rp, 8), 128:256])
                new_a += [nr, ni]
                new_g += [glr + nr * pr + ni * pi, gli + ni * pr - nr * pi]
            return tuple(new_a), tuple(new_g)

        a0, g0 = [], []
        for blk in range(S5_BLOCKS):
            a0 += [acarry[blk, :, 0:128], acarry[blk, :, 128:256]]
            g0 += [dlr_ref[blk], dli_ref[blk]]
        an, gn = lax.fori_loop(0, tc, bstep, (tuple(a0), tuple(g0)), unroll=2)
        for blk in range(S5_BLOCKS):
            acarry[blk, :, 0:128] = an[2 * blk]
            acarry[blk, :, 128:256] = an[2 * blk + 1]
            dlr_ref[blk] = gn[2 * blk]
            dli_ref[blk] = gn[2 * blk + 1]
        for blk in range(S5_BLOCKS):
            sl = slice(blk * 256, (blk + 1) * 256)
            ab = adj[blk].astype(BF16)
            _stage(tmp, jnp.dot(ab, rbt_ref[blk], preferred_element_type=F32))
            du[:, sl] = _gather_rows(tmp, tc) + d_ref[:, sl] * dy_ref[:, sl]
            drb_ref[blk] += lax.dot_general(lhsu[blk], ab, (((0,), (0,)), ((), ())), preferred_element_type=F32)
            drc_ref[blk] += lax.dot_general(lhsd[blk], xs[blk].astype(BF16), (((0,), (0,)), ((), ())),
                                            preferred_element_type=F32)
        xh, r = _rms_hat(x_ref[...])
        dg_ref[...] += jnp.sum(du[...] * xh, axis=0, keepdims=True)
        dxh = du[...] * g_ref[...]
        dx_ref[...] = r * (dxh - xh * jnp.mean(dxh * xh, axis=-1, keepdims=True)) + res_ref[...]

    rev = pl.BlockSpec((tc, D_MODEL), lambda i: (nc - 1 - i, 0))
    vec = pl.BlockSpec((1, D_MODEL), lambda i: (0, 0))
    mat = pl.BlockSpec((S5_BLOCKS, 256, 256), lambda i: (0, 0, 0))
    lamspec = pl.BlockSpec((S5_BLOCKS, 8, 128), lambda i: (0, 0, 0))
    big = pltpu.VMEM((S5_BLOCKS, 8 * tc, 256), F32)
    bigb = pltpu.VMEM((S5_BLOCKS, 8 * tc, 256), BF16)
    return pl.pallas_call(
        body, grid=(nc,),
        in_specs=[rev, vec, rev, rev, vec, pl.BlockSpec((1, S5_BLOCKS, 8, 256), lambda i: (nc - 1 - i, 0, 0, 0)),
                  mat, mat, mat, lamspec, lamspec],
        out_specs=[rev, vec, mat, mat, lamspec, lamspec, vec],
        out_shape=[_sds((n_rows, D_MODEL), F32), _sds((1, D_MODEL), F32), _sds((S5_BLOCKS, 256, 256), F32),
                   _sds((S5_BLOCKS, 256, 256), F32), _sds((S5_BLOCKS, 8, 128), F32), _sds((S5_BLOCKS, 8, 128), F32),
                   _sds((1, D_MODEL), F32)],
        scratch_shapes=[pltpu.VMEM((2, 8 * tc, 128), F32), pltpu.VMEM((tc, D_MODEL), F32), bigb, bigb, big, big,
                        pltpu.VMEM((S5_BLOCKS, 8, 256), F32)],
        name="s5_bwd", compiler_params=_params(("arbitrary",)))(
            x, gain, dy2, res, d_skip, cs, rb, rbt, rct, lam_r, lam_i)


def _s5_views(a_re, a_im, log_dt, b_re, b_im):
    return a_re[:, None, :], a_im[:, None, :], log_dt[:, None, None], jnp.swapaxes(b_re, 1, 2), jnp.swapaxes(b_im, 1, 2)


def _s5_factors(a_re, a_im, log_dt):
    lr, li, dt = jnp.minimum(a_re, LAMBDA_RE_MAX), a_im, jnp.exp(log_dt)
    mag, ang = jnp.exp(lr * dt), li * dt
    lbr, lbi = mag * jnp.cos(ang), mag * jnp.sin(ang)
    den = lr * lr + li * li
    fr, fi = ((lbr - 1.0) * lr + lbi * li) / den, (lbi * lr - (lbr - 1.0) * li) / den
    return lr, li, dt, lbr, lbi, fr, fi, den


def s5_prep(a_re, a_im, log_dt, b_re, b_im, c_re, c_im):
    def body(ar_ref, ai_ref, t_ref, br_ref, bi_ref, cr_ref, ci_ref, rb_ref, rbt_ref, rc_ref, rct_ref, lr_ref, li_ref):
        _, _, _, lbr, lbi, fr, fi, _ = _s5_factors(ar_ref[...], ai_ref[...], t_ref[...])
        lr_ref[...] = lbr
        li_ref[...] = lbi
        bre = fr * br_ref[...] - fi * bi_ref[...]
        bim = fr * bi_ref[...] + fi * br_ref[...]
        even = (lax.broadcasted_iota(jnp.int32, (256, S5_STATE), 0) // S5_GROUP) % 2 == 0

        def assemble(re, im):
            re, im = re.reshape(256, S5_STATE), im.reshape(256, S5_STATE)
            return jnp.concatenate([jnp.where(even, re, 0.0), jnp.where(even, 0.0, re), jnp.where(even, im, 0.0),
                                    jnp.where(even, 0.0, im)], axis=1)

        for blk in range(S5_BLOCKS):
            sl = slice(16 * blk, 16 * blk + 16)
            rb = assemble(bre[sl], bim[sl])
            rct = assemble(cr_ref[sl], -ci_ref[sl])
            rb_ref[blk] = rb.astype(BF16)
            rbt_ref[blk] = rb.T.astype(BF16)
            rct_ref[blk] = rct.astype(BF16)
            rc_ref[blk] = rct.T.astype(BF16)

    vm = pl.BlockSpec(memory_space=pltpu.VMEM)
    mat = _sds((S5_BLOCKS, 256, 256), BF16)
    lam = _sds((S5_GROUPS, 1, S5_STATE), F32)
    rb, rbt, rc, rct, lam_r, lam_i = pl.pallas_call(
        body, in_specs=[vm] * 7, out_specs=[vm] * 6, out_shape=[mat, mat, mat, mat, lam, lam], name="s5_prep",
        compiler_params=_params())(*_s5_views(a_re, a_im, log_dt, b_re, b_im), c_re, c_im)
    return rb, rbt, rc, rct, lam_r.reshape(S5_BLOCKS, 8, 128), lam_i.reshape(S5_BLOCKS, 8, 128)


def s5_param_bwd(mats, lams, a_re, a_im, log_dt, b_re, b_im):
    def body(m_ref, glr_ref, gli_ref, ar_ref, ai_ref, t_ref, br_ref, bi_ref,
             dar_ref, dai_ref, dt_ref, dbr_ref, dbi_ref, dcr_ref, dci_ref):
        lr, li, dt, lbr, lbi, fr, fi, den = _s5_factors(ar_ref[...], ai_ref[...], t_ref[...])
        shape = (S5_GROUPS, S5_GROUP, S5_STATE)
        gbr, gbi = m_ref[0:1024, 0:64].reshape(shape), m_ref[0:1024, 64:128].reshape(shape)
        dcr_ref[...] = m_ref[1024:2048, 0:64].reshape(shape)
        dci_ref[...] = -m_ref[1024:2048, 64:128].reshape(shape)
        br, bi = br_ref[...], bi_ref[...]
        dbr_ref[...] = fr * gbr + fi * gbi
        dbi_ref[...] = fr * gbi - fi * gbr
        dfr = jnp.sum(gbr * br + gbi * bi, axis=1, keepdims=True)
        dfi = jnp.sum(gbi * br - gbr * bi, axis=1, keepdims=True)
        nr, ni = (dfr * lr - dfi * li) / den, (dfr * li + dfi * lr) / den
        qr, qi = (fr * lr + fi * li) / den, (fi * lr - fr * li) / den
        lam_r, lam_i = -(dfr * qr + dfi * qi), -(dfi * qr - dfr * qi)
        gr, gi = glr_ref[...] + nr, gli_ref[...] + ni
        zr, zi = gr * lbr + gi * lbi, gi * lbr - gr * lbi
        a = ar_ref[...]
        dar_ref[...] = (lam_r + zr * dt) * jnp.where(a < LAMBDA_RE_MAX, 1.0, jnp.where(a == LAMBDA_RE_MAX, 0.5, 0.0))
        dai_ref[...] = lam_i + zi * dt
        dt_ref[...] = jnp.sum(zr * lr + zi * li, axis=2, keepdims=True) * dt

    vm = pl.BlockSpec(memory_space=pltpu.VMEM)
    state = _sds((S5_GROUPS, 1, S5_STATE), F32)
    wide = _sds((S5_GROUPS, S5_GROUP, S5_STATE), F32)
    glr = lams[0:32].reshape(S5_GROUPS, 1, S5_STATE)
    gli = lams[32:64].reshape(S5_GROUPS, 1, S5_STATE)
    dar, dai, ddt, dbr, dbi, dcr, dci = pl.pallas_call(
        body, in_specs=[vm] * 8, out_specs=[vm] * 7,
        out_shape=[state, state, _sds((S5_GROUPS, 1, 1), F32), wide, wide, wide, wide], name="s5_param_bwd",
        compiler_params=_params())(mats, glr, gli, *_s5_views(a_re, a_im, log_dt, b_re, b_im))
    return (dar.reshape(S5_GROUPS, S5_STATE), dai.reshape(S5_GROUPS, S5_STATE), ddt.reshape(S5_GROUPS),
            jnp.swapaxes(dbr, 1, 2), jnp.swapaxes(dbi, 1, 2), dcr, dci)


def s5_compact(drb, drct, dlr, dli):
    def body(drb_ref, drct_ref, dlr_ref, dli_ref, o_ref, lam_ref):
        even = (lax.broadcasted_iota(jnp.int32, (256, 64), 0) // S5_GROUP) % 2 == 0
        for blk in range(S5_BLOCKS):
            for k, ref in enumerate((drb_ref, drct_ref)):
                m = ref[blk]
                re = jnp.where(even, m[:, 0:64], m[:, 64:128])
                im = jnp.where(even, m[:, 128:192], m[:, 192:256])
                o_ref[pl.ds(k * 1024 + blk * 256, 256), :] = jnp.concatenate([re, im], axis=1)
            lam_ref[pl.ds(blk * 8, 8), :] = dlr_ref[blk]
            lam_ref[pl.ds(32 + blk * 8, 8), :] = dli_ref[blk]

    vm = pl.BlockSpec(memory_space=pltpu.VMEM)
    return pl.pallas_call(body, in_specs=[vm] * 4, out_specs=[vm, vm], out_shape=[_sds((2048, 128), F32), _sds((64, 128), F32)],
                          name="s5_compact", compiler_params=_params())(drb, drct, dlr, dli)


NEG = -1e30


GROUP = N_Q // N_KV


def _attn_masks(n):
    qi = lax.broadcasted_iota(jnp.int32, (GROUP * BLOCK, BLOCK), 0) % BLOCK
    kj = lax.broadcasted_iota(jnp.int32, (GROUP * BLOCK, BLOCK), 1)
    return jnp.logical_and(kj > qi, n > 0), kj <= qi


def _stack_heads(ref, kh):
    return jnp.concatenate([ref[:, (GROUP * kh + g) * HEAD_DIM:(GROUP * kh + g + 1) * HEAD_DIM] for g in range(GROUP)], axis=0)


def _unstack_heads(val):
    return jnp.concatenate([val[g * BLOCK:(g + 1) * BLOCK] for g in range(GROUP)], axis=1)


def _sink_column(sink_ref, kh):
    grp = lax.broadcasted_iota(jnp.int32, (GROUP * BLOCK, 1), 0) // BLOCK
    col = jnp.zeros((GROUP * BLOCK, 1), F32)
    for g in range(GROUP):
        col = jnp.where(grp == g, sink_ref[GROUP * kh + g], col)
    return col, grp


def _attn_exp(q4, kp, kc, sink, mask_p, mask_c):
    scale = 1.0 / math.sqrt(HEAD_DIM)
    nt = (((1,), (1,)), ((), ()))
    sp = jnp.where(mask_p, lax.dot_general(q4, kp, nt, preferred_element_type=F32) * scale, NEG)
    sc = jnp.where(mask_c, lax.dot_general(q4, kc, nt, preferred_element_type=F32) * scale, NEG)
    m = jnp.maximum(jnp.maximum(jnp.max(sp, axis=-1, keepdims=True), jnp.max(sc, axis=-1, keepdims=True)), sink)
    pp = jnp.exp(sp - m)
    pc = jnp.exp(sc - m)
    ps = jnp.exp(sink - m)
    inv = 1.0 / (jnp.sum(pp, axis=-1, keepdims=True) + jnp.sum(pc, axis=-1, keepdims=True) + ps)
    return pp, pc, ps, inv


def attn_fwd(q, kv, sinks):
    n_rows = q.shape[0]
    nb = n_rows // BLOCK

    def body(sink_ref, q_ref, kvp_ref, kvc_ref, o_ref):
        n = pl.program_id(0)
        mask_p, mask_c = _attn_masks(n)
        outs = []
        for kh in range(N_KV):
            ks, vs = slice(kh * HEAD_DIM, (kh + 1) * HEAD_DIM), slice((N_KV + kh) * HEAD_DIM, (N_KV + kh + 1) * HEAD_DIM)
            sink, _ = _sink_column(sink_ref, kh)
            pp, pc, _, inv = _attn_exp(_stack_heads(q_ref, kh), kvp_ref[:, ks], kvc_ref[:, ks], sink, mask_p, mask_c)
            o4 = (jnp.dot(pp.astype(BF16), kvp_ref[:, vs], preferred_element_type=F32)
                  + jnp.dot(pc.astype(BF16), kvc_ref[:, vs], preferred_element_type=F32)) * inv
            outs.append(_unstack_heads(o4))
        o_ref[...] = jnp.concatenate(outs, axis=1).astype(BF16)

    kvw = 2 * N_KV * HEAD_DIM
    return pl.pallas_call(
        body, grid=(nb,),
        in_specs=[pl.BlockSpec(memory_space=pltpu.SMEM), pl.BlockSpec((BLOCK, D_MODEL), lambda n: (n, 0)),
                  pl.BlockSpec((BLOCK, kvw), lambda n: (jnp.maximum(n - 1, 0), 0)), pl.BlockSpec((BLOCK, kvw), lambda n: (n, 0))],
        out_specs=pl.BlockSpec((BLOCK, D_MODEL), lambda n: (n, 0)), out_shape=_sds((n_rows, D_MODEL), BF16),
        name="attn_fwd", compiler_params=_params(("parallel",)))(sinks, q, kv, kv)


def attn_bwd(q, kv, do, sinks):
    n_rows = q.shape[0]
    nb = n_rows // BLOCK
    kvw = 2 * N_KV * HEAD_DIM
    tn = (((0,), (0,)), ((), ()))
    nt = (((1,), (1,)), ((), ()))
    scale = 1.0 / math.sqrt(HEAD_DIM)

    def body(sink_ref, q_ref, kvp_ref, kvc_ref, do_ref, dq_ref, dbq_ref, dprev_ref, dcur_ref, dsink_ref):
        n = pl.program_id(0)
        mask_p, mask_c = _attn_masks(n)
        lane = lax.broadcasted_iota(jnp.int32, (1, D_MODEL), 1)
        dqs, dsink = [], jnp.zeros((1, D_MODEL), F32)
        dkp, dkc, dvp, dvc = [], [], [], []
        for kh in range(N_KV):
            ks, vs = slice(kh * HEAD_DIM, (kh + 1) * HEAD_DIM), slice((N_KV + kh) * HEAD_DIM, (N_KV + kh + 1) * HEAD_DIM)
            q4, do4 = _stack_heads(q_ref, kh), _stack_heads(do_ref, kh)
            kp, kc, vp, vc = kvp_ref[:, ks], kvc_ref[:, ks], kvp_ref[:, vs], kvc_ref[:, vs]
            sink, grp = _sink_column(sink_ref, kh)
            pp, pc, ps, inv = _attn_exp(q4, kp, kc, sink, mask_p, mask_c)
            pp, pc = pp * inv, pc * inv
            dpp = lax.dot_general(do4, vp, nt, preferred_element_type=F32)
            dpc = lax.dot_general(do4, vc, nt, preferred_element_type=F32)
            delta = jnp.sum(pp * dpp, axis=-1, keepdims=True) + jnp.sum(pc * dpc, axis=-1, keepdims=True)
            dsp = (pp * (dpp - delta) * scale).astype(BF16)
            dsc = (pc * (dpc - delta) * scale).astype(BF16)
            dsk = ps * inv * delta
            for g in range(GROUP):
                dsink = dsink + jnp.where(lane == GROUP * kh + g, -jnp.sum(jnp.where(grp == g, dsk, 0.0)), 0.0)
            dqs.append(_unstack_heads(jnp.dot(dsp, kp, preferred_element_type=F32)
                                      + jnp.dot(dsc, kc, preferred_element_type=F32)))
            dkp.append(lax.dot_general(dsp, q4, tn, preferred_element_type=F32))
            dkc.append(lax.dot_general(dsc, q4, tn, preferred_element_type=F32))
            dvp.append(lax.dot_general(pp.astype(BF16), do4, tn, preferred_element_type=F32))
            dvc.append(lax.dot_general(pc.astype(BF16), do4, tn, preferred_element_type=F32))
        dq = jnp.concatenate(dqs, axis=1)
        dq_ref[...] = dq.astype(BF16)
        dprev_ref[0] = jnp.concatenate(dkp + dvp, axis=1)
        dcur_ref[0] = jnp.concatenate(dkc + dvc, axis=1)

        @pl.when(n == 0)
        def _():
            dbq_ref[...] = jnp.zeros_like(dbq_ref)
            dsink_ref[...] = jnp.zeros_like(dsink_ref)

        dbq_ref[...] += jnp.sum(dq, axis=0, keepdims=True)
        dsink_ref[...] += dsink

    blk = pl.BlockSpec((BLOCK, D_MODEL), lambda n: (n, 0))
    part = pl.BlockSpec((1, BLOCK, kvw), lambda n: (n, 0, 0))
    return pl.pallas_call(
        body, grid=(nb,),
        in_specs=[pl.BlockSpec(memory_space=pltpu.SMEM), blk,
                  pl.BlockSpec((BLOCK, kvw), lambda n: (jnp.maximum(n - 1, 0), 0)), pl.BlockSpec((BLOCK, kvw), lambda n: (n, 0)), blk],
        out_specs=[blk, pl.BlockSpec((1, D_MODEL), lambda n: (0, 0)), part, part, pl.BlockSpec((1, D_MODEL), lambda n: (0, 0))],
        out_shape=[_sds((n_rows, D_MODEL), BF16), _sds((1, D_MODEL), F32), _sds((nb, BLOCK, kvw), F32),
                   _sds((nb, BLOCK, kvw), F32), _sds((1, D_MODEL), F32)],
        name="attn_bwd", compiler_params=_params(("arbitrary",)))(sinks, q, kv, kv, do)


def kv_combine(dprev, dcur):
    nb, _, kvw = dprev.shape

    def body(dcur_ref, dprev_ref, dkv_ref, db_ref):
        total = jnp.zeros((1, kvw), F32)
        for m in range(nb):
            dkv = dcur_ref[m] + dprev_ref[m + 1] if m + 1 < nb else dcur_ref[m]
            dkv_ref[m * BLOCK:(m + 1) * BLOCK, :] = dkv.astype(BF16)
            total = total + jnp.sum(dkv, axis=0, keepdims=True)
        db_ref[...] = jnp.concatenate([total, jnp.zeros((1, D_MODEL - kvw), F32)], axis=1)

    vm = pl.BlockSpec(memory_space=pltpu.VMEM)
    return pl.pallas_call(body, in_specs=[vm, vm], out_specs=[vm, vm],
                          out_shape=[_sds((nb * BLOCK, kvw), BF16), _sds((1, D_MODEL), F32)], name="kv_combine",
                          compiler_params=_params())(dcur, dprev)


def glu_bwd(dout, val, gate, tm=256):
    n_rows, d = dout.shape

    def body(do_ref, v_ref, g_ref, dz_ref, db_ref):
        i = pl.program_id(0)
        sg = jax.nn.sigmoid(g_ref[...])
        dval = do_ref[...] * sg
        dgate = do_ref[...] * v_ref[...] * sg * (1.0 - sg)
        dz_ref[...] = jnp.concatenate([dval, dgate], axis=1).astype(BF16)

        @pl.when(i == 0)
        def _():
            db_ref[...] = jnp.zeros_like(db_ref)

        db_ref[0:1, :] += jnp.sum(dval, axis=0, keepdims=True)
        db_ref[1:2, :] += jnp.sum(dgate, axis=0, keepdims=True)

    row = pl.BlockSpec((tm, d), lambda i: (i, 0))
    return pl.pallas_call(
        body, grid=(n_rows // tm,), in_specs=[row, row, row],
        out_specs=[pl.BlockSpec((tm, 2 * d), lambda i: (i, 0)), pl.BlockSpec((2, d), lambda i: (0, 0))],
        out_shape=[_sds((n_rows, 2 * d), BF16), _sds((2, d), F32)],
        name="glu_bwd", compiler_params=_params(("arbitrary",)))(dout, val, gate)


def _adam_update(w, g, m, v):
    nm = ADAM_B1 * m + (1.0 - ADAM_B1) * g
    nv = ADAM_B2 * v + (1.0 - ADAM_B2) * (g * g)
    m_hat = nm / (1.0 - ADAM_B1 ** ADAM_STEP)
    v_hat = nv / (1.0 - ADAM_B2 ** ADAM_STEP)
    return -ADAM_LR * (m_hat / (jnp.sqrt(v_hat) + ADAM_EPS) + ADAM_WD * w), nm, nv


def adamw(name, ws, gs, ms, vs, steps=16):
    n = len(ws)

    def body(*refs):
        for k in range(n):
            w_ref, g_ref, m_ref, v_ref = (refs[j * n + k] for j in range(4))
            go_ref, d_ref, nm_ref, nv_ref = (refs[(4 + j) * n + k] for j in range(4))
            gv = g_ref[...]
            go_ref[...] = gv
            d_ref[...], nm_ref[...], nv_ref[...] = _adam_update(w_ref[...], gv, m_ref[...], v_ref[...])

    specs = [pl.BlockSpec((w.shape[0] // steps, w.shape[1]), lambda i: (i, 0)) for w in ws]
    shapes = [_sds(w.shape, F32) for w in ws]
    out = pl.pallas_call(
        body, grid=(steps,), in_specs=specs * 4, out_specs=specs * 4, out_shape=shapes * 4, name=name,
        compiler_params=_params(("parallel",)))(*ws, *gs, *ms, *vs)
    return [list(out[j * n:(j + 1) * n]) for j in range(4)]


def adamw_native(name, ws, gs, ms, vs):
    n = len(ws)

    def body(*refs):
        w_refs, g_refs, m_refs, v_refs = refs[:n], refs[n:2 * n], refs[2 * n:3 * n], refs[3 * n:4 * n]
        d_refs, nm_refs, nv_refs = refs[4 * n:5 * n], refs[5 * n:6 * n], refs[6 * n:7 * n]
        for k in range(n):
            dl, nm, nv = _adam_update(w_refs[k][...], g_refs[k][...], m_refs[k][...], v_refs[k][...])
            d_refs[k][...] = dl
            nm_refs[k][...] = nm
            nv_refs[k][...] = nv

    vm = pl.BlockSpec(memory_space=pltpu.VMEM)
    shapes = [_sds(w.shape, F32) for w in ws]
    out = pl.pallas_call(body, in_specs=[vm] * (4 * n), out_specs=[vm] * (3 * n), out_shape=shapes * 3, name=name,
                         compiler_params=_params())(*ws, *gs, *ms, *vs)
    return list(out[:n]), list(out[n:2 * n]), list(out[2 * n:])


VEC_ROWS = {"norm_mix": 0, "norm_mlp": 2, "norm_kv": 4, "norm_final": 5, "s5_d": 6, "b_q": 7, "b_o": 8, "s5_b_glu": 9,
            "b_kv": 11, "sinks": 12, "loss": 13}


def split_vectors(where, vecs, d_shard, glu_shard):
    kvw = 2 * N_KV * HEAD_DIM
    shapes = {"norm_mix": (2, D_MODEL), "norm_mlp": (2, D_MODEL), "norm_kv": (1, D_MODEL), "norm_final": (1, D_MODEL),
              "s5_d": (1, d_shard), "b_q": (1, D_MODEL), "b_o": (1, D_MODEL), "s5_b_glu": (1, glu_shard), "b_kv": (1, kvw),
              "sinks": (1, N_Q), "loss": (1, 128)}
    names = list(shapes)

    def body(where_ref, v_ref, *o_refs):
        chip = where_ref[1]
        for name, o_ref in zip(names, o_refs):
            r0, (r, n) = VEC_ROWS[name], shapes[name]
            if name == "s5_d":
                g = jnp.zeros((1, n), F32)
                for j in range(4):
                    g = jnp.where(chip == j, v_ref[r0:r0 + 1, j * n:(j + 1) * n], g)
            elif name == "s5_b_glu":
                g = jnp.zeros((1, n), F32)
                for j in range(4):
                    row, col = r0 + (j * n) // D_MODEL, (j * n) % D_MODEL
                    g = jnp.where(chip == j, v_ref[row:row + 1, col:col + n], g)
            else:
                g = v_ref[r0:r0 + r, 0:n]
            o_ref[...] = g

    vm = pl.BlockSpec(memory_space=pltpu.VMEM)
    out = pl.pallas_call(body, in_specs=[pl.BlockSpec(memory_space=pltpu.SMEM), vm], out_specs=[vm] * len(names),
                         out_shape=[_sds(shapes[n], F32) for n in names], name="split_vectors",
                         compiler_params=_params())(where, vecs)
    return dict(zip(names, out))


def _position():
    x, y, c = lax.axis_index("x"), lax.axis_index("y"), lax.axis_index("c")
    others = [(1 - x, y), (x, 1 - y), (1 - x, 1 - y)]
    return x, y, c, others


def _window(ref, kind, chip, half, shard_shape):
    if kind == "slab":
        return ref.at[chip]
    r, n = shard_shape
    if kind == "col":
        return ref.at[pl.ds(pl.multiple_of(half * (r // 2), 16), r // 2), pl.ds(pl.multiple_of(chip * n, 128), n)]
    return ref.at[pl.ds(pl.multiple_of(chip * r, 16), r), pl.ds(pl.multiple_of(half * (n // 2), 128), n // 2)]


def _half(ref, kind, half, shape):
    r, n = shape
    if kind == "col":
        return ref.at[pl.ds(pl.multiple_of(half * (r // 2), 16), r // 2), :]
    return ref.at[:, pl.ds(pl.multiple_of(half * (n // 2), 128), n // 2)]


def swap_start(name, grads, kinds, carry):
    nt = len(grads)
    shapes = [tuple(g.shape) for g in grads]
    lands = [lax.empty(sh, BF16) for sh in shapes]
    given, given_specs, token_type, write = _hand_through(carry)
    n_in = 2 * nt + len(given)

    def body(*refs):
        in_refs, land_refs = refs[:nt], refs[nt:2 * nt]
        send_sems, recv_sems, token = refs[n_in], refs[n_in + 1], refs[-1]
        x, y, c, _ = _position()
        for t in range(nt):
            pltpu.make_async_remote_copy(
                src_ref=_half(in_refs[t], kinds[t], 1 - c, shapes[t]), dst_ref=_half(land_refs[t], kinds[t], 1 - c, shapes[t]),
                send_sem=send_sems.at[t], recv_sem=recv_sems.at[t], device_id=(x, y, 1 - c), device_id_type=MESH).start()
        write(token, refs[:n_in])

    sems = pltpu.SemaphoreType.DMA((nt,))
    both = list(grads) + lands
    out = pl.pallas_call(
        body, name=name, in_specs=[HBM_SPEC] * (2 * nt) + given_specs,
        out_specs=(SEM_SPEC, SEM_SPEC, *[HBM_SPEC] * (2 * nt), pl.BlockSpec(memory_space=pltpu.VMEM)),
        out_shape=(sems, sems, *[pltpu.HBM(a.shape, a.dtype) for a in both], token_type),
        input_output_aliases={t: 2 + t for t in range(2 * nt)}, compiler_params=_split_params(),
    )(*[_in_hbm(a) for a in both], *given)
    return out[0], out[1], list(out[2:2 + nt]), list(out[2 + nt:2 + 2 * nt]), out[-1]


def swap_wait(name, send_sems, recv_sems, grads, lands, kinds, after):
    nt = len(grads)
    shapes = [tuple(g.shape) for g in grads]

    def body(*refs):
        in_refs, land_refs = refs[:nt], refs[nt:2 * nt]
        send_ref, recv_ref = refs[2 * nt], refs[2 * nt + 1]
        x, y, c, _ = _position()
        for t in range(nt):
            cp = pltpu.make_async_remote_copy(
                src_ref=_half(in_refs[t], kinds[t], 1 - c, shapes[t]), dst_ref=_half(land_refs[t], kinds[t], c, shapes[t]),
                send_sem=send_ref.at[t], recv_sem=recv_ref.at[t], device_id=(x, y, 1 - c), device_id_type=MESH)
            cp.wait_send()
            cp.wait_recv()

    both = list(grads) + list(lands)
    out = pl.pallas_call(
        body, name=name, in_specs=[HBM_SPEC] * (2 * nt) + [SEM_SPEC, SEM_SPEC, HBM_SPEC], out_specs=[HBM_SPEC] * (2 * nt),
        out_shape=[pltpu.HBM(a.shape, a.dtype) for a in both], input_output_aliases={t: t for t in range(2 * nt)},
        compiler_params=_split_params())(*both, send_sems, recv_sems, _in_hbm(after))
    return list(out[:nt]), list(out[nt:])


def _half_spec(kind, shape, tiles):
    r, n = shape
    if kind == "col":
        tn = n // tiles
        return pl.BlockSpec((r // 2, tn), lambda i, s: (s[0], i))
    tm = r // tiles
    return pl.BlockSpec((tm, n // 2), lambda i, s: (i, s[0]))


def add_halves(name, mine, landed, kinds, where, tiles=4):
    nt = len(mine)
    shapes = [tuple(a.shape) for a in mine]

    def compact(t):
        r, n = shapes[t]
        if kinds[t] == "col":
            return (r // 2, n), pl.BlockSpec((r // 2, n // tiles), lambda i, s: (0, i))
        return (r, n // 2), pl.BlockSpec((r // tiles, n // 2), lambda i, s: (i, 0))

    def body(s_ref, *refs):
        for a_ref, b_ref, o_ref in zip(refs[:nt], refs[nt:2 * nt], refs[2 * nt:]):
            o_ref[...] = (a_ref[...].astype(F32) + b_ref[...].astype(F32)).astype(BF16)

    specs = [_half_spec(kinds[t], shapes[t], tiles) for t in range(nt)]
    return pl.pallas_call(
        body, grid_spec=pltpu.PrefetchScalarGridSpec(num_scalar_prefetch=1, grid=(tiles,), in_specs=specs + specs,
                                                     out_specs=[compact(t)[1] for t in range(nt)]),
        out_shape=[_sds(compact(t)[0], BF16) for t in range(nt)], name=name,
        compiler_params=_params(("parallel",)))(where, *mine, *landed)


def sum_shards(name, parts, landed, kinds, shard_shapes, where, layers, n_layers, intos, tiles=2):
    nt = len(parts)
    in_specs, out_specs = [], []
    for t in range(nt):
        (r, n), layer = shard_shapes[t], layers[t]
        if kinds[t] == "col":
            tm, width = r // 2 // tiles, n
            own = pl.BlockSpec((tm, n), lambda i, s: (i, s[1]))
            out = pl.BlockSpec((None, tm, n), lambda i, s, layer=layer: (layer, s[0] * tiles + i, 0))
        else:
            tm, width = r // tiles, n // 2
            own = pl.BlockSpec((tm, n // 2), lambda i, s: (s[1] * tiles + i, 0))
            out = pl.BlockSpec((None, tm, n // 2), lambda i, s, layer=layer: (layer, i, s[0]))
        in_specs += [own, pl.BlockSpec((3, tm, width), lambda i, s: (0, i, 0))]
        out_specs.append(out)
    args, aliases = [where] + [a for pair in zip(parts, landed) for a in pair], {}
    for t in range(nt):
        if intos[t] is not None:
            aliases[len(args)] = t
            in_specs.append(pl.BlockSpec(memory_space=pl.ANY))
            args.append(intos[t])

    def body(s_ref, *refs):
        for t in range(nt):
            a_ref, l_ref, o_ref = refs[2 * t], refs[2 * t + 1], refs[len(in_specs) + t]
            o_ref[...] = ((a_ref[...].astype(F32) + l_ref[0].astype(F32)) + l_ref[1].astype(F32)) + l_ref[2].astype(F32)

    return pl.pallas_call(
        body, grid_spec=pltpu.PrefetchScalarGridSpec(num_scalar_prefetch=1, grid=(tiles,), in_specs=in_specs,
                                                     out_specs=out_specs),
        out_shape=[_sds((n_layers[t],) + tuple(shard_shapes[t]), F32) for t in range(nt)], input_output_aliases=aliases,
        name=name, compiler_params=_params(("parallel",)))(*args)


def share_start(arrays, entries, carry):
    na, nt = len(arrays), len(entries)
    given, given_specs, token_type, write = _hand_through(carry)
    n_in = na + len(given)

    def body(*refs):
        in_refs, send_sems, recv_sems, token = refs[:na], refs[n_in], refs[n_in + 1], refs[-1]
        x, y, c, _ = _position()
        for t, (a, layer, kind) in enumerate(entries):
            mine = _half(in_refs[a].at[layer], kind, c, tuple(arrays[a].shape[1:]))
            pltpu.make_async_remote_copy(
                src_ref=mine, dst_ref=mine, send_sem=send_sems.at[t], recv_sem=recv_sems.at[t],
                device_id=(x, y, 1 - c), device_id_type=MESH).start()
        write(token, refs[:n_in])

    sems = pltpu.SemaphoreType.DMA((nt,))
    out = pl.pallas_call(
        body, name="share_start", in_specs=[HBM_SPEC] * na + given_specs,
        out_specs=(SEM_SPEC, SEM_SPEC, *[HBM_SPEC] * na, pl.BlockSpec(memory_space=pltpu.VMEM)),
        out_shape=(sems, sems, *[pltpu.HBM(a.shape, a.dtype) for a in arrays], token_type),
        input_output_aliases={t: 2 + t for t in range(na)}, compiler_params=_split_params(),
    )(*[_in_hbm(a) for a in arrays], *given)
    return out[0], out[1], list(out[2:2 + na]), out[-1]


def share_wait(send_sems, recv_sems, arrays, entries, after):
    na = len(arrays)

    def body(*refs):
        in_refs, send_ref, recv_ref = refs[:na], refs[na], refs[na + 1]
        x, y, c, _ = _position()
        for t, (a, layer, kind) in enumerate(entries):
            shape = tuple(arrays[a].shape[1:])
            cp = pltpu.make_async_remote_copy(
                src_ref=_half(in_refs[a].at[layer], kind, c, shape), dst_ref=_half(in_refs[a].at[layer], kind, 1 - c, shape),
                send_sem=send_ref.at[t], recv_sem=recv_ref.at[t], device_id=(x, y, 1 - c), device_id_type=MESH)
            cp.wait_send()
            cp.wait_recv()

    return list(pl.pallas_call(
        body, name="share_wait", in_specs=[HBM_SPEC] * na + [SEM_SPEC, SEM_SPEC, HBM_SPEC], out_specs=[HBM_SPEC] * na,
        out_shape=[pltpu.HBM(a.shape, a.dtype) for a in arrays], input_output_aliases={t: t for t in range(na)},
        compiler_params=_split_params())(*arrays, send_sems, recv_sems, _in_hbm(after)))


HBM_SPEC = pl.BlockSpec(memory_space=pltpu.HBM)
SEM_SPEC = pl.BlockSpec(memory_space=pltpu.SEMAPHORE)
ANY_SPEC = pl.BlockSpec(memory_space=pl.ANY)


def _split_params():
    return pltpu.CompilerParams(has_side_effects=pltpu.SideEffectType.DATAFLOW_SIDE_EFFECTING,
                                vmem_limit_bytes=VMEM_LIMIT_BYTES)


def _in_hbm(a):
    return pltpu.with_memory_space_constraint(a, pltpu.HBM)


def cast_place(arrays, entries, where, tiles=8):
    in_specs, out_specs, fulls = [], [], []
    for a, layer, kind in entries:
        _, r, n = arrays[a].shape
        tm = r // tiles
        in_specs.append(pl.BlockSpec((None, tm, n), lambda i, s, layer=layer: (layer, i, 0)))
        if kind == "col":
            fulls.append((r, 4 * n))
            out_specs.append(pl.BlockSpec((tm, n), lambda i, s: (i, s[1])))
        else:
            fulls.append((4 * r, n))
            out_specs.append(pl.BlockSpec((tm, n), lambda i, s: (s[1] * tiles + i, 0)))
    nt = len(entries)

    def body(s_ref, *refs):
        for w_ref, o_ref in zip(refs[:nt], refs[nt:]):
            o_ref[...] = w_ref[...].astype(BF16)

    return pl.pallas_call(
        body, grid_spec=pltpu.PrefetchScalarGridSpec(num_scalar_prefetch=1, grid=(tiles,), in_specs=in_specs,
                                                     out_specs=out_specs),
        out_shape=[_sds(f, BF16) for f in fulls], name="cast_place",
        compiler_params=_params(("parallel",)))(where, *[arrays[a] for a, _, _ in entries])


def _hand_through(carry):
    given = [] if isinstance(carry, tuple) else [carry]

    def write(token, ins):
        token[...] = ins[-1][...] if given else jnp.zeros_like(token)

    return (given, [pl.BlockSpec(memory_space=pltpu.VMEM)] * len(given),
            _sds(carry if isinstance(carry, tuple) else carry.shape, F32), write)


def gather_start(fulls, kinds, shard_shapes, carry):
    nt = len(fulls)
    given, given_specs, token_type, write = _hand_through(carry)
    n_in = nt + len(given)

    def body(*refs):
        full_refs = refs[:nt]
        send_sems, recv_sems, token = refs[n_in], refs[n_in + 1], refs[-1]
        x, y, c, others = _position()
        for t in range(nt):
            mine = _window(full_refs[t], kinds[t], 2 * x + y, c, shard_shapes[t])
            for j, (ox, oy) in enumerate(others):
                pltpu.make_async_remote_copy(
                    src_ref=mine, dst_ref=mine, send_sem=send_sems.at[3 * t + j], recv_sem=recv_sems.at[3 * t + j],
                    device_id=(ox, oy, c), device_id_type=MESH).start()
        write(token, refs[:n_in])

    sems = pltpu.SemaphoreType.DMA((3 * nt,))
    out = pl.pallas_call(
        body, name="gather_start", in_specs=[HBM_SPEC] * nt + given_specs,
        out_specs=(SEM_SPEC, SEM_SPEC, *[HBM_SPEC] * nt, pl.BlockSpec(memory_space=pltpu.VMEM)),
        out_shape=(sems, sems, *[pltpu.HBM(f.shape, f.dtype) for f in fulls], token_type),
        input_output_aliases={t: 2 + t for t in range(nt)}, compiler_params=_split_params(),
    )(*[_in_hbm(f) for f in fulls], *given)
    return out[0], out[1], list(out[2:2 + nt]), out[-1]


def gather_wait(name, send_sems, recv_sems, fulls, kinds, shard_shapes, after, first):
    nt = len(fulls)
    extra = [] if after is None else [_in_hbm(after)]

    def body(*refs):
        full_refs, send_ref, recv_ref = refs[:nt], refs[nt], refs[nt + 1]
        x, y, c, others = _position()
        for t in range(nt):
            mine = _window(full_refs[t], kinds[t], 2 * x + y, c, shard_shapes[t])
            for j, (ox, oy) in enumerate(others):
                cp = pltpu.make_async_remote_copy(
                    src_ref=mine, dst_ref=_window(full_refs[t], kinds[t], 2 * ox + oy, c, shard_shapes[t]),
                    send_sem=send_ref.at[3 * (first + t) + j], recv_sem=recv_ref.at[3 * (first + t) + j],
                    device_id=(ox, oy, c), device_id_type=MESH)
                cp.wait_send()
                cp.wait_recv()

    out = pl.pallas_call(
        body, name=name, in_specs=[HBM_SPEC] * nt + [SEM_SPEC, SEM_SPEC] + [HBM_SPEC] * len(extra),
        out_specs=[HBM_SPEC] * nt, out_shape=[pltpu.HBM(f.shape, f.dtype) for f in fulls],
        input_output_aliases={t: t for t in range(nt)}, compiler_params=_split_params())(*fulls, send_sems, recv_sems, *extra)
    return list(out)


def forward_halves(name, fulls, kinds, shard_shapes):
    nt = len(fulls)

    def body(*refs):
        out_refs = refs[nt:2 * nt]
        send_sems, recv_sems = refs[2 * nt:]
        x, y, c, others = _position()
        cps = []
        for t in range(nt):
            for j, (ox, oy) in enumerate(others):
                landed = _window(out_refs[t], kinds[t], 2 * ox + oy, c, shard_shapes[t])
                cp = pltpu.make_async_remote_copy(
                    src_ref=landed, dst_ref=landed, send_sem=send_sems.at[3 * t + j], recv_sem=recv_sems.at[3 * t + j],
                    device_id=(x, y, 1 - c), device_id_type=MESH)
                cp.start()
                cps.append(cp)
        for t in range(nt):
            for j, (ox, oy) in enumerate(others):
                got = _window(out_refs[t], kinds[t], 2 * ox + oy, 1 - c, shard_shapes[t])
                pltpu.make_async_remote_copy(
                    src_ref=got, dst_ref=got, send_sem=send_sems.at[3 * t + j], recv_sem=recv_sems.at[3 * t + j],
                    device_id=(x, y, 1 - c), device_id_type=MESH).wait_recv()
        for cp in cps:
            cp.wait_send()

    out = pl.pallas_call(
        body, in_specs=[ANY_SPEC] * nt, out_specs=[ANY_SPEC] * nt, out_shape=[_sds(f.shape, f.dtype) for f in fulls],
        input_output_aliases={t: t for t in range(nt)},
        scratch_shapes=[pltpu.SemaphoreType.DMA((3 * nt,)), pltpu.SemaphoreType.DMA((3 * nt,))],
        name=name, compiler_params=_params())(*fulls)
    return list(out)


def forward_start(name, send_sems, recv_sems, fulls, kinds, shard_shapes, after, first, carry):
    nt = len(fulls)
    given, given_specs, token_type, write = _hand_through(carry)
    n_in = nt + 3 + len(given)

    def body(*refs):
        full_refs, ici_send, ici_recv = refs[:nt], refs[nt], refs[nt + 1]
        send_ref, recv_ref, token = refs[n_in], refs[n_in + 1], refs[-1]
        x, y, c, others = _position()
        for t in range(nt):
            mine = _window(full_refs[t], kinds[t], 2 * x + y, c, shard_shapes[t])
            for j, (ox, oy) in enumerate(others):
                landed = _window(full_refs[t], kinds[t], 2 * ox + oy, c, shard_shapes[t])
                cp = pltpu.make_async_remote_copy(
                    src_ref=mine, dst_ref=landed, send_sem=ici_send.at[3 * (first + t) + j],
                    recv_sem=ici_recv.at[3 * (first + t) + j], device_id=(ox, oy, c), device_id_type=MESH)
                cp.wait_send()
                cp.wait_recv()
                pltpu.make_async_remote_copy(
                    src_ref=landed, dst_ref=landed, send_sem=send_ref.at[3 * t + j], recv_sem=recv_ref.at[3 * t + j],
                    device_id=(x, y, 1 - c), device_id_type=MESH).start()
        write(token, refs[:n_in])

    sems = pltpu.SemaphoreType.DMA((3 * nt,))
    out = pl.pallas_call(
        body, name=name, in_specs=[HBM_SPEC] * nt + [SEM_SPEC, SEM_SPEC, HBM_SPEC] + given_specs,
        out_specs=(SEM_SPEC, SEM_SPEC, *[HBM_SPEC] * nt, pl.BlockSpec(memory_space=pltpu.VMEM)),
        out_shape=(sems, sems, *[pltpu.HBM(f.shape, f.dtype) for f in fulls], token_type),
        input_output_aliases={t: 2 + t for t in range(nt)}, compiler_params=_split_params(),
    )(*fulls, send_sems, recv_sems, _in_hbm(after), *given)
    return out[0], out[1], list(out[2:2 + nt]), out[-1]


def forward_wait(name, send_sems, recv_sems, fulls, kinds, shard_shapes, after):
    nt = len(fulls)

    def body(*refs):
        full_refs, send_ref, recv_ref = refs[:nt], refs[nt], refs[nt + 1]
        x, y, c, others = _position()
        for t in range(nt):
            for j, (ox, oy) in enumerate(others):
                cp = pltpu.make_async_remote_copy(
                    src_ref=_window(full_refs[t], kinds[t], 2 * ox + oy, c, shard_shapes[t]),
                    dst_ref=_window(full_refs[t], kinds[t], 2 * ox + oy, 1 - c, shard_shapes[t]),
                    send_sem=send_ref.at[3 * t + j], recv_sem=recv_ref.at[3 * t + j],
                    device_id=(x, y, 1 - c), device_id_type=MESH)
                cp.wait_send()
                cp.wait_recv()

    return list(pl.pallas_call(
        body, name=name, in_specs=[HBM_SPEC] * nt + [SEM_SPEC, SEM_SPEC, HBM_SPEC], out_specs=[HBM_SPEC] * nt,
        out_shape=[pltpu.HBM(f.shape, f.dtype) for f in fulls], input_output_aliases={t: t for t in range(nt)},
        compiler_params=_split_params())(*fulls, send_sems, recv_sems, _in_hbm(after)))


def _piece(ref, kind, chip, shard_shape):
    r, n = shard_shape
    if kind == "col":
        return ref.at[:, pl.ds(pl.multiple_of(chip * n, 128), n)]
    return ref.at[pl.ds(pl.multiple_of(chip * r, 16), r), :]


def _piece_shape(kind, shard_shape):
    r, n = shard_shape
    return (r // 2, n) if kind == "col" else (r, n // 2)


def exchange_start(name, parts, kinds, shard_shapes, carry):
    nt = len(parts)
    lands = [lax.empty((3,) + _piece_shape(kinds[t], shard_shapes[t]), BF16) for t in range(nt)]
    given, given_specs, token_type, write = _hand_through(carry)
    n_in = 2 * nt + len(given)

    def body(*refs):
        part_refs, land_refs = refs[:nt], refs[nt:2 * nt]
        send_sems, recv_sems, token = refs[n_in], refs[n_in + 1], refs[-1]
        x, y, c, others = _position()
        for t in range(nt):
            for j, (ox, oy) in enumerate(others):
                pltpu.make_async_remote_copy(
                    src_ref=_piece(part_refs[t], kinds[t], 2 * ox + oy, shard_shapes[t]), dst_ref=land_refs[t].at[j],
                    send_sem=send_sems.at[3 * t + j], recv_sem=recv_sems.at[3 * t + j],
                    device_id=(ox, oy, c), device_id_type=MESH).start()
        write(token, refs[:n_in])

    sems = pltpu.SemaphoreType.DMA((3 * nt,))
    both = list(parts) + lands
    out = pl.pallas_call(
        body, name=name, in_specs=[HBM_SPEC] * (2 * nt) + given_specs,
        out_specs=(SEM_SPEC, SEM_SPEC, *[HBM_SPEC] * (2 * nt), pl.BlockSpec(memory_space=pltpu.VMEM)),
        out_shape=(sems, sems, *[pltpu.HBM(a.shape, a.dtype) for a in both], token_type),
        input_output_aliases={t: 2 + t for t in range(2 * nt)}, compiler_params=_split_params(),
    )(*[_in_hbm(a) for a in both], *given)
    return out[0], out[1], list(out[2:2 + nt]), list(out[2 + nt:2 + 2 * nt]), out[-1]


def exchange_wait(name, send_sems, recv_sems, parts, lands, kinds, shard_shapes, after):
    nt = len(parts)

    def body(*refs):
        part_refs, land_refs = refs[:nt], refs[nt:2 * nt]
        send_ref, recv_ref = refs[2 * nt], refs[2 * nt + 1]
        x, y, c, others = _position()
        for t in range(nt):
            for j, (ox, oy) in enumerate(others):
                cp = pltpu.make_async_remote_copy(
                    src_ref=_piece(part_refs[t], kinds[t], 2 * ox + oy, shard_shapes[t]), dst_ref=land_refs[t].at[j],
                    send_sem=send_ref.at[3 * t + j], recv_sem=recv_ref.at[3 * t + j],
                    device_id=(ox, oy, c), device_id_type=MESH)
                cp.wait_send()
                cp.wait_recv()

    both = list(parts) + list(lands)
    out = pl.pallas_call(
        body, name=name, in_specs=[HBM_SPEC] * (2 * nt) + [SEM_SPEC, SEM_SPEC, HBM_SPEC], out_specs=[HBM_SPEC] * (2 * nt),
        out_shape=[pltpu.HBM(a.shape, a.dtype) for a in both], input_output_aliases={t: t for t in range(2 * nt)},
        compiler_params=_split_params())(*both, send_sems, recv_sems, _in_hbm(after))
    return list(out[:nt]), list(out[nt:])


def all_reduce_small(name, bufs, wire):
    n = len(bufs)
    halves = [b.shape[0] // 2 for b in bufs]

    def body(*refs):
        in_refs, out_refs, lands, txs = refs[:n], refs[n:2 * n], refs[2 * n:3 * n], refs[3 * n:4 * n]
        send_sems, recv_sems = refs[4 * n:]
        x, y, c, _ = _position()
        mine = [pl.ds(pl.multiple_of(c * h, 8), h) for h in halves]
        other = [pl.ds(pl.multiple_of((1 - c) * h, 8), h) for h in halves]
        for s, peer in enumerate([(x, y, 1 - c), (1 - x, y, c), (x, 1 - y, c)]):
            cps = []
            for k in range(n):
                txs[k][...] = (in_refs[k][other[k], :] if s == 0 else out_refs[k][mine[k], :]).astype(wire[k])
                cp = pltpu.make_async_remote_copy(
                    src_ref=txs[k], dst_ref=lands[k].at[s], send_sem=send_sems.at[4 * k + s], recv_sem=recv_sems.at[4 * k + s],
                    device_id=peer, device_id_type=MESH)
                cp.start()
                cps.append(cp)
            for k, cp in enumerate(cps):
                cp.wait()
                own = in_refs[k][mine[k], :] if s == 0 else out_refs[k][mine[k], :]
                out_refs[k][mine[k], :] = own.astype(wire[k]).astype(F32) + lands[k][s].astype(F32)
        cps = []
        for k in range(n):
            cp = pltpu.make_async_remote_copy(
                src_ref=out_refs[k].at[mine[k]], dst_ref=out_refs[k].at[mine[k]], send_sem=send_sems.at[4 * k + 3],
                recv_sem=recv_sems.at[4 * k + 3], device_id=(x, y, 1 - c), device_id_type=MESH)
            cp.start()
            cps.append(cp)
        for cp in cps:
            cp.wait()

    vm = pl.BlockSpec(memory_space=pltpu.VMEM)
    out = pl.pallas_call(
        body, in_specs=[vm] * n, out_specs=[vm] * n, out_shape=[_sds(b.shape, F32) for b in bufs],
        scratch_shapes=[pltpu.VMEM((3, h, b.shape[1]), w) for h, b, w in zip(halves, bufs, wire)]
        + [pltpu.VMEM((h, b.shape[1]), w) for h, b, w in zip(halves, bufs, wire)]
        + [pltpu.SemaphoreType.DMA((4 * n,)), pltpu.SemaphoreType.DMA((4 * n,))],
        name=name, compiler_params=_params())(*bufs)
    return list(out)


def _local_step(x, target, small, need, ahead, emit_swap, emit_exchange):
    d = D_MODEL
    full = {}

    def handed(vec, token):
        return vec if token is None else token

    def token_rows(token):
        return [] if token is None else [token]

    def plus(acc, rows):
        return acc + rows[0] if rows else acc

    rb16, rbt16, rc16, rct16, lr_t, li_t = small["s5_operands"]
    ge, y2, cs = s5_fwd(x, small["norm_mix0"], small["s5_d"], rb16, rc16, lr_t, li_t)
    full.update(need("glu", ge))

    def norm_rows(h, gains):
        xh, _ = _rms_hat(h)
        return [xh * g for g in gains]

    def glu_epilogue(accs, e, r):
        v, gt = accs[0] + r[0], accs[1] + r[1]
        h = e[0] + v * jax.nn.sigmoid(gt)
        return [h, v, gt] + norm_rows(h, r[2:])

    gain_mlp0 = handed(small["norm_mlp0"], ahead("mlp_in0", full["w_glu"], small["norm_mlp0"]))
    h1, val, gate, n1 = mm_nn(
        "glu", ge, full["w_glu"], [0, d], d, glu_epilogue, [F32, F32, F32, BF16], extras=[x],
        rowvecs=[(small["s5_b_glu"], 0), (small["s5_b_glu"], d), (gain_mlp0, 0)], tm=512, tn=d)

    def mlp_fwd(tag, h, n, w_in, get_w_out, next_gains, head=None):
        def in_epilogue(accs, e, rv):
            pos = jnp.maximum(accs[0], 0.0)
            return [pos * pos, 2.0 * pos]

        r, slope = mm_nn("mlp_in" + tag, n, w_in, [0], w_in.shape[1], in_epilogue, [BF16, BF16], tm=2048)
        w_out = get_w_out(r)

        def epilogue(accs, e, rv):
            h_out = e[0] + accs[0]
            return [h_out] + norm_rows(h_out, rv)

        if head is not None:
            return head(r, w_out, h), (n, r, slope)
        outs = mm_nn("mlp_out" + tag, r, w_out, [0], d, epilogue, [F32] + [BF16] * len(next_gains), extras=[h],
                     rowvecs=[(g, 0) for g in next_gains], tm=512, tn=d)
        return outs[0], outs[1:], (n, r, slope)

    full.update(need("mlp_in0", h1))

    def w_out0(after):
        full.update(need("mlp_out0", after))
        return full["w_out0"]

    h2, (nkv, n2), mlp0 = mlp_fwd("0", h1, n1, full["w_in0"], w_out0, [small["norm_kv"], small["norm_mix1"]])

    full.update(need("attn", h2))
    kvw = 2 * N_KV * HEAD_DIM
    (kv,) = mm_nn("kv_proj", nkv, full["w_kv"], [0], kvw, lambda accs, e, r: [accs[0] + r[0]], [BF16],
                  rowvecs=[(small["b_kv"], 0)], tm=2048)
    (q,) = mm_nn("q_proj", n2, full["w_q"], [0], d, lambda accs, e, r: [accs[0] + r[0]], [BF16],
                 rowvecs=[(small["b_q"], 0)], tm=2048)
    sinks = small["sinks"].reshape(N_Q)
    o = attn_fwd(q, kv, sinks)
    def o_epilogue(accs, e, r):
        h_out = e[0] + accs[0] + r[0]
        return [h_out] + norm_rows(h_out, r[1:])

    bias_o = handed(small["b_o"], ahead("mlp_in1", o, small["b_o"]))
    h3, n3 = mm_nn("o_proj", o, full["w_o"], [0], d, o_epilogue, [F32, BF16], extras=[h2],
                   rowvecs=[(bias_o, 0), (small["norm_mlp1"], 0)], tm=512, tn=d)
    full.update(need("mlp_in1", h3))

    def w_out1(after):
        full.update(need("mlp_out1", after))
        return full["w_out1"]

    def loss_head(r, w_out, h):
        def epilogue(accs, e, rv):
            xh, rr = _rms_hat(e[0] + accs[0])
            err = xh * rv[0] - e[1]
            dy = err * (1.0 / d)
            dxh = dy * rv[0]
            dx = rr * (dxh - xh * jnp.mean(dxh * xh, axis=-1, keepdims=True))
            loss = jnp.full((1, d), 0.5 * jnp.sum(jnp.mean(err * err, axis=-1, keepdims=True)), F32)
            return [dx, dx, loss, jnp.sum(dy * xh, axis=0, keepdims=True)]

        return mm_nn("mlp_out1", r, w_out, [0], d, epilogue, [F32, BF16], extras=[h, target],
                     rowvecs=[(small["norm_final"], 0)], n_sums=2, tm=512, tn=d)

    (dh, dhb, loss_tile, dg_final), mlp1 = mlp_fwd("1", h3, n3, full["w_in1"], w_out1, [], head=loss_head)

    grads_small, grads_full = {"norm_final": dg_final}, {}
    ident = lambda acc, e, r: [plus(acc, r)]
    layer1 = ["w_out1", "w_in1", "w_o", "w_q", "w_kv"]
    layer0 = ["w_out0", "w_in0", "w_glu"]

    def norm_bwd_rows(x_rows, res, dys, gains):
        xh, r = _rms_hat(x_rows)
        dxh = sum(dy * g for dy, g in zip(dys, gains))
        dx = r * (dxh - xh * jnp.mean(dxh * xh, axis=-1, keepdims=True)) + res
        return dx, [jnp.sum(dy * xh, axis=0, keepdims=True) for dy in dys]

    def mlp_bwd(tag, dh, dhb, h_in, gain, w_in, w_out, saved, token=None):
        n, r, slope = saved
        grads_full["w_out" + tag] = mm_tn("dw_out" + tag, r, dhb, tn=1024)
        (da,) = mm_nt("mlp_da" + tag, dhb, w_out, lambda acc, e, rv: [plus(acc * e[0].astype(F32), rv)], [BF16],
                      extras=[slope], rowvecs=token_rows(token), tm=2048)
        grads_full["w_in" + tag] = mm_tn("dw_in" + tag, n, da, tn=1024)

        def epilogue(acc, e, rv):
            dx, dgs = norm_bwd_rows(e[0], e[1], [acc], rv)
            return [dx, dx, jnp.sum(dx, axis=0, keepdims=True)] + dgs

        dx, dxb, colsum, dg = mm_nt("mlp_dn" + tag, da, w_in, epilogue, [F32, BF16], extras=[h_in, dh], rowvecs=[gain],
                                    n_sums=2, tm=512, tk=d)
        grads_small["norm_mlp" + tag] = dg
        return dx, dxb, colsum

    dh3, dh3b, colsum3 = mlp_bwd("1", dh, dhb, h3, small["norm_mlp1"], full["w_in1"], full["w_out1"], mlp1)
    grads_small["b_o"] = colsum3
    grads_full["w_o"] = mm_tn("dw_o", o, dh3b, tn=1024)
    (do,) = mm_nt("attn_do", dh3b, full["w_o"], ident, [BF16], tm=2048)
    dq, dbq, dprev, dcur, dsink = attn_bwd(q, kv, do, sinks)
    dkv, dbkv = kv_combine(dprev, dcur)
    grads_small["b_q"], grads_small["b_kv"], grads_small["sinks"] = dbq, dbkv, dsink
    grads_full["w_q"] = mm_tn("dw_q", n2, dq, tn=1024)
    grads_full["w_kv"] = mm_tn("dw_kv", nkv, dkv, tk=1024)
    token = emit_swap("layer1", {n: grads_full[n] for n in layer1}, (1, d))
    (dnkv,) = mm_nt("kv_dn", dkv, full["w_kv"], ident, [F32], rowvecs=token_rows(token), tm=2048, tk=1024)

    def attn_dn_epilogue(acc, e, rv):
        dx, dgs = norm_bwd_rows(e[0], e[1], [acc, e[2]], rv)
        return [dx, dx] + dgs

    dh2, dh2b, dg_mix1, dg_kv = mm_nt("attn_dn", dq, full["w_q"], attn_dn_epilogue, [F32, BF16], extras=[h2, dh3, dnkv],
                                      rowvecs=[small["norm_mix1"], small["norm_kv"]], n_sums=2, tm=512, tk=d)
    grads_small["norm_mix1"], grads_small["norm_kv"] = dg_mix1, dg_kv
    token = emit_exchange("layer1", dh2b, (1, full["w_out0"].shape[0]))
    dh1, _, _ = mlp_bwd("0", dh2, dh2b, h1, small["norm_mlp0"], full["w_in0"], full["w_out0"], mlp0, token)

    dz, db_glu = glu_bwd(dh1, val, gate)
    grads_small["s5_b_glu"] = db_glu
    grads_full["w_glu"] = mm_tn("dw_glu", ge, dz, tn=1024)
    token = emit_swap("layer0", {n: grads_full[n] for n in layer0}, (1, d))
    (dy2,) = mm_nt("glu_dy", dz, full["w_glu"], lambda acc, e, rv: [plus(acc, rv) * _gelu_grad(e[0])], [F32], extras=[y2],
                   rowvecs=token_rows(token), tm=1024, tk=1024)
    d_skip = handed(small["s5_d"], emit_exchange("layer0", dy2, small["s5_d"]))
    grad_x, dd, drb, drc, dlr, dli, dg_mix0 = s5_bwd(x, small["norm_mix0"], dy2, dh1, d_skip, cs, rb16, rbt16, rct16, lr_t, li_t)
    grads_small["s5_d"] = dd
    grads_small["s5_mats"] = (drb, drc, dlr, dli)
    grads_small["norm_mix0"] = dg_mix0
    return loss_tile, grad_x, grads_small


SMALL_NAMES = ["norm_mix", "norm_mlp", "norm_kv", "norm_final", "s5_a_re", "s5_a_im", "s5_log_dt", "s5_b_re", "s5_b_im",
               "s5_c_re", "s5_c_im", "s5_d", "s5_b_glu", "b_kv", "b_q", "sinks", "b_o"]
BIG_NAMES = ["s5_w_glu", "w_kv", "w_q", "w_o", "w_mlp_in", "w_mlp_out"]
WEIGHT_ORDER = ["norm_mix", "norm_mlp", "norm_kv", "norm_final", "s5_a_re", "s5_a_im", "s5_log_dt", "s5_b_re", "s5_b_im",
                "s5_c_re", "s5_c_im", "s5_d", "s5_w_glu", "s5_b_glu", "w_kv", "b_kv", "w_q", "b_q", "sinks", "w_o", "b_o",
                "w_mlp_in", "w_mlp_out"]


def kernel(x, norm_mix, norm_mlp, norm_kv, norm_final, s5_a_re, s5_a_im, s5_log_dt, s5_b_re, s5_b_im, s5_c_re, s5_c_im, s5_d, s5_w_glu, s5_b_glu, w_kv, b_kv, w_q, b_q, sinks, w_o, b_o, w_mlp_in, w_mlp_out, loss_target, m_norm_mix, m_norm_mlp, m_norm_kv, m_norm_final, m_s5_a_re, m_s5_a_im, m_s5_log_dt, m_s5_b_re, m_s5_b_im, m_s5_c_re, m_s5_c_im, m_s5_d, m_s5_w_glu, m_s5_b_glu, m_w_kv, m_b_kv, m_w_q, m_b_q, m_sinks, m_w_o, m_b_o, m_w_mlp_in, m_w_mlp_out, v_norm_mix, v_norm_mlp, v_norm_kv, v_norm_final, v_s5_a_re, v_s5_a_im, v_s5_log_dt, v_s5_b_re, v_s5_b_im, v_s5_c_re, v_s5_c_im, v_s5_d, v_s5_w_glu, v_s5_b_glu, v_w_kv, v_b_kv, v_w_q, v_b_q, v_sinks, v_w_o, v_b_o, v_w_mlp_in, v_w_mlp_out):
    env = dict(locals())
    w = {n: env[n] for n in WEIGHT_ORDER}
    mom = {n: env["m_" + n] for n in WEIGHT_ORDER}
    var = {n: env["v_" + n] for n in WEIGHT_ORDER}
    d = D_MODEL
    xi, yi, ci = lax.axis_index("x"), lax.axis_index("y"), lax.axis_index("c")
    chip = 2 * xi + yi
    where = jnp.stack([ci, chip]).astype(jnp.int32)

    dsh, bsh = s5_d.shape[1], s5_b_glu.shape[1]
    packed = jnp.concatenate([s5_d.reshape(-1, 128), s5_b_glu.reshape(-1, 128)])
    n_d, n_b = dsh // 128, bsh // 128
    slab = lax.dynamic_update_slice(jnp.zeros((4, 8, 128), F32), jnp.pad(packed, ((0, 8 - n_d - n_b), (0, 0)))[None],
                                    (chip, 0, 0))

    big = [s5_w_glu, w_kv[None], w_q, w_o, w_mlp_in, w_mlp_out]
    entries = [(0, 0, "col"), (1, 0, "row"), (2, 0, "row"), (3, 0, "row"), (4, 0, "col"), (4, 1, "col"),
               (5, 0, "row"), (5, 1, "row")]
    names = ["w_glu", "w_kv", "w_q", "w_o", "w_in0", "w_in1", "w_out0", "w_out1"]
    kinds = dict(zip(names, [k for _, _, k in entries]))
    shard_shapes = dict(zip(names, [tuple(big[a].shape[1:]) for a, _, _ in entries]))

    placed_w = dict(zip(names, cast_place(big, entries, where)))
    placed_w["vectors"], kinds["vectors"], shard_shapes["vectors"] = slab, "slab", None
    gather_groups = {"glu": ["w_glu"], "mlp_in0": ["w_in0"], "mlp_out0": ["w_out0"], "attn": ["w_kv", "w_q", "w_o"],
                     "mlp_in1": ["w_in1"], "mlp_out1": ["w_out1"]}
    order = ["vectors"] + [n for members in gather_groups.values() for n in members]
    send, recv, thru, log_dt = gather_start([placed_w[n] for n in order], [kinds[n] for n in order],
                                            [shard_shapes[n] for n in order], s5_log_dt)
    started = dict(zip(order, thru))
    (gathered_rows,) = gather_wait("gather_wait_vectors", send, recv, [started["vectors"]], ["slab"], [None], None, 0)
    d_full = gathered_rows[:, 0:n_d].reshape(1, -1)
    bglu_full = gathered_rows[:, n_d:n_d + n_b].reshape(1, -1)

    forwarding = {}

    def ahead(group, after, carry):
        members = gather_groups[group]
        ks, shapes = [kinds[n] for n in members], [shard_shapes[n] for n in members]
        d2d_send, d2d_recv, landed, tok = forward_start(
            "forward_start_" + group, send, recv, [started[n] for n in members], ks, shapes, after,
            order.index(members[0]), carry)
        forwarding[group] = (d2d_send, d2d_recv, landed)
        return tok

    def need(group, after):
        members = gather_groups[group]
        ks, shapes = [kinds[n] for n in members], [shard_shapes[n] for n in members]
        if group in forwarding:
            return dict(zip(members, forward_wait("forward_wait_" + group, *forwarding[group], ks, shapes, after)))
        landed = gather_wait("gather_wait_" + group, send, recv, [started[n] for n in members], ks, shapes, after,
                             order.index(members[0]))
        return dict(zip(members, forward_halves("forward_halves_" + group, landed, ks, shapes)))

    swapping, exchanging = {}, {}

    def emit_swap(group, partial, carry):
        members = list(partial)
        send, recv, mine, lands, tok = swap_start("swap_start_" + group, [partial[n] for n in members],
                                                  [kinds[n] for n in members], carry)
        swapping[group] = (members, send, recv, mine, lands)
        return tok

    def emit_exchange(group, after, carry):
        members, send, recv, mine, lands = swapping[group]
        ks, shapes = [kinds[n] for n in members], [shard_shapes[n] for n in members]
        mine, landed = swap_wait("swap_wait_" + group, send, recv, mine, lands, ks, after)
        sums = add_halves("add_halves_" + group, mine, landed, ks, where)
        send, recv, parts, lands, tok = exchange_start("exchange_start_" + group, sums, ks, shapes, carry)
        exchanging[group] = (members, send, recv, parts, lands)
        return tok

    s5_args = (s5_a_re[0], s5_a_im[0], log_dt[0], s5_b_re[0], s5_b_im[0])
    small = {
        "norm_mix0": norm_mix[0:1], "norm_mix1": norm_mix[1:2], "norm_mlp0": norm_mlp[0:1], "norm_mlp1": norm_mlp[1:2],
        "norm_kv": norm_kv.reshape(1, d), "norm_final": norm_final.reshape(1, d), "s5_operands": s5_prep(*s5_args, s5_c_re[0], s5_c_im[0]),
        "s5_d": d_full, "s5_b_glu": bglu_full,
        "b_kv": b_kv.reshape(1, -1), "b_q": b_q, "sinks": sinks, "b_o": b_o,
    }
    loss_row, grad_x, gs = _local_step(x[0], loss_target[0], small, need, ahead, emit_swap, emit_exchange)

    reduced = [None] * len(big)
    where_of = dict(zip(names, entries))
    for group in ("layer1", "layer0"):
        members, send, recv, parts, lands = exchanging[group]
        ks, shapes = [kinds[n] for n in members], [shard_shapes[n] for n in members]
        parts, lands = exchange_wait("exchange_wait_" + group, send, recv, parts, lands, ks, shapes, grad_x)
        targets = [where_of[n][0] for n in members]
        sums = sum_shards("sum_shards_" + group, parts, lands, ks, shapes, where, [where_of[n][1] for n in members],
                          [big[a].shape[0] for a in targets], [reduced[a] for a in targets])
        for a, arr in zip(targets, sums):
            reduced[a] = arr
    share_send, share_recv, reduced, shared = share_start(reduced, entries, (2, d))

    mats, lams = s5_compact(*gs["s5_mats"])
    rows = [gs["norm_mix0"], gs["norm_mix1"], gs["norm_mlp0"], gs["norm_mlp1"], gs["norm_kv"], gs["norm_final"], gs["s5_d"],
            gs["b_q"], gs["b_o"], gs["s5_b_glu"], gs["b_kv"], gs["sinks"], loss_row, shared]
    vecs, lams, mats = all_reduce_small("reduce_small", [jnp.concatenate(rows, axis=0), lams, mats], [F32, F32, BF16])
    grads = split_vectors(where, vecs, dsh, bsh)
    loss = grads.pop("loss")[0, 0]
    g_are, g_aim, g_dt, g_bre, g_bim, dc_re, dc_im = s5_param_bwd(mats, lams, *s5_args)
    grads.update({"s5_a_re": g_are[None], "s5_a_im": g_aim[None], "s5_log_dt": g_dt[None], "s5_b_re": g_bre[None],
                  "s5_b_im": g_bim[None], "s5_c_re": dc_re[None], "s5_c_im": dc_im[None]})

    delta, new_m, new_v = {}, {}, {}

    def view(n, a):
        return a.reshape(1, -1) if a.ndim == 1 else jnp.swapaxes(a, -1, -2) if n in ("s5_b_re", "s5_b_im") else a

    sw, sg, sm, sv = ([view(n, t[n]) for n in SMALL_NAMES] for t in (w, grads, mom, var))
    for n, a, b, c_ in zip(SMALL_NAMES, *adamw_native("adamw_small", sw, sg, sm, sv)):
        delta[n], new_m[n], new_v[n] = (view(n, t) if t.ndim == 4 else t for t in (a, b, c_))

    reduced = share_wait(share_send, share_recv, reduced, entries, new_v["s5_c_re"])
    for n, g in zip(BIG_NAMES, reduced):
        grads[n] = g.reshape(w[n].shape)
    flat = lambda t: [t[n].reshape(-1, t[n].shape[-1]) for n in BIG_NAMES]
    for table, arrays in zip((grads, delta, new_m, new_v), adamw("adamw_big", flat(w), flat(grads), flat(mom), flat(var))):
        for n, a in zip(BIG_NAMES, arrays):
            table[n] = a.reshape(w[n].shape)

    out = [loss.reshape(()), grad_x[None]]
    for table in (grads, delta, new_m, new_v):
        out += [table[n].reshape(w[n].shape) for n in WEIGHT_ORDER]
    return tuple(out)
```

```python
import math

import jax
import jax.numpy as jnp
from jax import lax
from jax.experimental import pallas as pl
from jax.experimental.pallas import tpu as pltpu

F32 = jnp.float32
BF16 = jnp.bfloat16

D_MODEL = 1024
S5_GROUPS = 64
S5_GROUP = 16
S5_STATE = 64
N_KV = 4
N_Q = 16
HEAD_DIM = 64
BLOCK = 128
NORM_EPS = 1e-5
LAMBDA_RE_MAX = -1e-4
ADAM_LR, ADAM_B1, ADAM_B2, ADAM_EPS, ADAM_WD, ADAM_STEP = 0.001, 0.9, 0.999, 1e-08, 0.01, 10

VMEM_LIMIT_BYTES = 56 * 1024 * 1024
S5_CHUNK = 256
S5_BLOCKS = 4
MESH = pl.DeviceIdType.MESH


def _params(sem=None):
    return pltpu.CompilerParams(dimension_semantics=sem, vmem_limit_bytes=VMEM_LIMIT_BYTES)


def _sds(shape, dtype):
    return jax.ShapeDtypeStruct(shape, dtype)


def _rms_hat(xv):
    r = lax.rsqrt(jnp.mean(xv * xv, axis=-1, keepdims=True) + NORM_EPS)
    return xv * r, r


def mm_nn(name, a, w, col_offsets, n_out, epilogue, out_dtypes, extras=(), rowvecs=(), n_sums=0, tm=1024, tn=512):
    m, k = a.shape
    tm, tn = min(tm, m), min(tn, n_out)
    nw, ne, nr, no = len(col_offsets), len(extras), len(rowvecs), len(out_dtypes)

    def body(a_ref, *refs):
        w_refs, e_refs, r_refs = refs[:nw], refs[nw:nw + ne], refs[nw + ne:nw + ne + nr]
        o_refs, s_refs = refs[nw + ne + nr:nw + ne + nr + no], refs[nw + ne + nr + no:]
        av = a_ref[...]
        accs = [jnp.dot(av, w_ref[...], preferred_element_type=F32) for w_ref in w_refs]
        outs = epilogue(accs, [e[...] for e in e_refs], [r[...] for r in r_refs])
        for o_ref, o in zip(o_refs, outs[:no]):
            o_ref[...] = o.astype(o_ref.dtype)
        if n_sums:
            @pl.when(pl.program_id(1) == 0)
            def _():
                for s_ref in s_refs:
                    s_ref[...] = jnp.zeros_like(s_ref)

            for s_ref, val in zip(s_refs, outs[no:]):
                s_ref[...] += val

    def wspec(off):
        return pl.BlockSpec((k, tn), lambda j, i, off=off: (0, off // tn + j))

    def rspec(off):
        return pl.BlockSpec((1, tn), lambda j, i, off=off: (0, off // tn + j))

    tile = pl.BlockSpec((tm, tn), lambda j, i: (i, j))
    in_specs = ([pl.BlockSpec((tm, k), lambda j, i: (i, 0))] + [wspec(o) for o in col_offsets]
                + [tile] * ne + [rspec(o) for _, o in rowvecs])
    sem = ("parallel", "arbitrary") if n_sums else ("parallel", "parallel")
    return pl.pallas_call(
        body, grid=(n_out // tn, m // tm), in_specs=in_specs,
        out_specs=[tile] * no + [pl.BlockSpec((1, tn), lambda j, i: (0, j))] * n_sums,
        out_shape=[_sds((m, n_out), dt) for dt in out_dtypes] + [_sds((1, n_out), F32)] * n_sums, name=name,
        compiler_params=_params(sem))(a, *([w] * nw), *extras, *[r for r, _ in rowvecs])


def mm_nt(name, g, w, epilogue, out_dtypes, extras=(), rowvecs=(), n_sums=0, tm=512, tk=512):
    m, n = g.shape
    k = w.shape[0]
    tm, tk = min(tm, m), min(tk, k)
    ne, nr, no = len(extras), len(rowvecs), len(out_dtypes)

    def body(g_ref, w_ref, *refs):
        e_refs, r_refs, o_refs, s_refs = refs[:ne], refs[ne:ne + nr], refs[ne + nr:ne + nr + no], refs[ne + nr + no:]
        acc = lax.dot_general(g_ref[...], w_ref[...], (((1,), (1,)), ((), ())), preferred_element_type=F32)
        outs = epilogue(acc, [e[...] for e in e_refs], [r[...] for r in r_refs])
        for o_ref, o in zip(o_refs, outs[:no]):
            o_ref[...] = o.astype(o_ref.dtype)
        if n_sums:
            @pl.when(pl.program_id(0) == 0)
            def _():
                for s_ref in s_refs:
                    s_ref[...] = jnp.zeros_like(s_ref)

            for s_ref, val in zip(s_refs, outs[no:]):
                s_ref[...] += val

    tile = pl.BlockSpec((tm, tk), lambda i, j: (i, j))
    vec = pl.BlockSpec((1, tk), lambda i, j: (0, j))
    sem = ("arbitrary", "parallel") if n_sums else ("parallel", "parallel")
    return pl.pallas_call(
        body, grid=(m // tm, k // tk),
        in_specs=[pl.BlockSpec((tm, n), lambda i, j: (i, 0)), pl.BlockSpec((tk, n), lambda i, j: (j, 0))]
        + [tile] * ne + [vec] * nr,
        out_specs=[tile] * no + [vec] * n_sums,
        out_shape=[_sds((m, k), dt) for dt in out_dtypes] + [_sds((1, k), F32)] * n_sums, name=name,
        compiler_params=_params(sem))(g, w, *extras, *rowvecs)


def mm_tn(name, a, g, tk=512, tn=512):
    m, k = a.shape
    n = g.shape[1]
    tk, tn = min(tk, k), min(tn, n)

    def body(a_ref, g_ref, o_ref):
        acc = lax.dot_general(a_ref[...], g_ref[...], (((0,), (0,)), ((), ())), preferred_element_type=F32)
        o_ref[...] = acc.astype(o_ref.dtype)

    return pl.pallas_call(
        body, grid=(k // tk, n // tn),
        in_specs=[pl.BlockSpec((m, tk), lambda i, j: (0, i)), pl.BlockSpec((m, tn), lambda i, j: (0, j))],
        out_specs=pl.BlockSpec((tk, tn), lambda i, j: (i, j)), out_shape=_sds((k, n), BF16), name=name,
        compiler_params=_params(("parallel", "parallel")))(a, g)


def _row_mask(tc):
    row = lax.broadcasted_iota(jnp.int32, (8 * tc, 256), 0) % 8
    col = lax.broadcasted_iota(jnp.int32, (8 * tc, 256), 1) // 32
    return row == col


def _expand_rows(val, mask):
    tc, width = val.shape
    rep = jnp.broadcast_to(val[:, None, :], (tc, 8, width)).reshape(8 * tc, width)
    return jnp.where(mask, rep, 0.0).astype(BF16)


def _stage(ref, val):
    ref[0] = val[:, 0:128]
    ref[1] = val[:, 128:256]


def _gather_rows(src_ref, tc):
    halves = []
    for half in range(2):
        col = lax.broadcasted_iota(jnp.int32, (tc, 128), 1) // 32 + 4 * half
        out = jnp.zeros((tc, 128), F32)
        for s8 in range(4 * half, 4 * half + 4):
            out = jnp.where(col == s8, src_ref.at[half][pl.ds(s8, tc, stride=8), :], out)
        halves.append(out)
    return jnp.concatenate(halves, axis=1)


def _gelu(x):
    c = math.sqrt(2.0 / math.pi)
    return 0.5 * x * (1.0 + jnp.tanh(c * (x + 0.044715 * x * x * x)))


def _gelu_grad(x):
    c = math.sqrt(2.0 / math.pi)
    t = jnp.tanh(c * (x + 0.044715 * x * x * x))
    return 0.5 * (1.0 + t) + 0.5 * x * (1.0 - t * t) * c * (1.0 + 3.0 * 0.044715 * x * x)


def s5_fwd(x, gain, d_skip, rb, rc, lam_r, lam_i):
    n_rows = x.shape[0]
    tc = min(S5_CHUNK, n_rows)
    nc = n_rows // tc

    def body(x_ref, g_ref, d_ref, rb_ref, rc_ref, lr_ref, li_ref, ge_ref, y2_ref, cs_ref, bux, yrows, carry):
        i = pl.program_id(0)
        u = _rms_hat(x_ref[...])[0] * g_ref[...]

        @pl.when(i == 0)
        def _():
            carry[...] = jnp.zeros_like(carry)

        cs_ref[0] = carry[...]
        mask = _row_mask(tc)
        for blk in range(S5_BLOCKS):
            lhs = _expand_rows(u[:, blk * 256:(blk + 1) * 256], mask)
            bux[blk] = jnp.dot(lhs, rb_ref[blk], preferred_element_type=F32)
        lam = [(lr_ref[blk], li_ref[blk]) for blk in range(S5_BLOCKS)]

        def step(t, c):
            r0 = pl.multiple_of(t * 8, 8)
            new = []
            for blk in range(S5_BLOCKS):
                xr, xi = c[2 * blk], c[2 * blk + 1]
                lr, li = lam[blk]
                nr = lr * xr - li * xi + bux[blk, pl.ds(r0, 8), 0:128]
                ni = lr * xi + li * xr + bux[blk, pl.ds(r0, 8), 128:256]
                bux[blk, pl.ds(r0, 8), 0:128] = nr
                bux[blk, pl.ds(r0, 8), 128:256] = ni
                new += [nr, ni]
            return tuple(new)

        c0 = []
        for blk in range(S5_BLOCKS):
            c0 += [carry[blk, :, 0:128], carry[blk, :, 128:256]]
        cn = lax.fori_loop(0, tc, step, tuple(c0), unroll=4)
        for blk in range(S5_BLOCKS):
            carry[blk, :, 0:128] = cn[2 * blk]
            carry[blk, :, 128:256] = cn[2 * blk + 1]
        for blk in range(S5_BLOCKS):
            _stage(yrows, jnp.dot(bux[blk].astype(BF16), rc_ref[blk], preferred_element_type=F32))
            sl = slice(blk * 256, (blk + 1) * 256)
            y2 = _gather_rows(yrows, tc) + d_ref[:, sl] * u[:, sl]
            y2_ref[:, sl] = y2
            ge_ref[:, sl] = _gelu(y2).astype(BF16)

    row = pl.BlockSpec((tc, D_MODEL), lambda i: (i, 0))
    vec = pl.BlockSpec((1, D_MODEL), lambda i: (0, 0))
    mat = pl.BlockSpec((S5_BLOCKS, 256, 256), lambda i: (0, 0, 0))
    lamspec = pl.BlockSpec((S5_BLOCKS, 8, 128), lambda i: (0, 0, 0))
    return pl.pallas_call(
        body, grid=(nc,),
        in_specs=[row, vec, vec, mat, mat, lamspec, lamspec],
        out_specs=[row, row, pl.BlockSpec((1, S5_BLOCKS, 8, 256), lambda i: (i, 0, 0, 0))],
        out_shape=[_sds((n_rows, D_MODEL), BF16), _sds((n_rows, D_MODEL), F32), _sds((nc, S5_BLOCKS, 8, 256), F32)],
        scratch_shapes=[pltpu.VMEM((S5_BLOCKS, 8 * tc, 256), F32), pltpu.VMEM((2, 8 * tc, 128), F32),
                        pltpu.VMEM((S5_BLOCKS, 8, 256), F32)],
        name="s5_fwd", compiler_params=_params(("arbitrary",)))(x, gain, d_skip, rb, rc, lam_r, lam_i)


def s5_bwd(x, gain, dy2, res, d_skip, cs, rb, rbt, rct, lam_r, lam_i):
    n_rows = x.shape[0]
    tc = min(S5_CHUNK, n_rows)
    nc = n_rows // tc

    def body(x_ref, g_ref, dy_ref, res_ref, d_ref, cs_ref, rb_ref, rbt_ref, rct_ref, lr_ref, li_ref,
             dx_ref, dd_ref, drb_ref, drc_ref, dlr_ref, dli_ref, dg_ref, tmp, du, lhsu, lhsd, xs, adj, acarry):
        i = pl.program_id(0)
        u = _rms_hat(x_ref[...])[0] * g_ref[...]

        @pl.when(i == 0)
        def _():
            acarry[...] = jnp.zeros_like(acarry)
            dd_ref[...] = jnp.zeros_like(dd_ref)
            drb_ref[...] = jnp.zeros_like(drb_ref)
            drc_ref[...] = jnp.zeros_like(drc_ref)
            dlr_ref[...] = jnp.zeros_like(dlr_ref)
            dli_ref[...] = jnp.zeros_like(dli_ref)
            dg_ref[...] = jnp.zeros_like(dg_ref)

        dd_ref[...] += jnp.sum(dy_ref[...] * u, axis=0, keepdims=True)
        mask = _row_mask(tc)
        for blk in range(S5_BLOCKS):
            sl = slice(blk * 256, (blk + 1) * 256)
            lhsu[blk] = _expand_rows(u[:, sl], mask)
            xs[blk] = jnp.dot(lhsu[blk], rb_ref[blk], preferred_element_type=F32)
            lhsd[blk] = _expand_rows(dy_ref[:, sl], mask)
            adj[blk] = jnp.dot(lhsd[blk], rct_ref[blk], preferred_element_type=F32)
        lam = [(lr_ref[blk], li_ref[blk]) for blk in range(S5_BLOCKS)]

        def fstep(t, c):
            r0 = pl.multiple_of(t * 8, 8)
            new = []
            for blk in range(S5_BLOCKS):
                xr, xi = c[2 * blk], c[2 * blk + 1]
                lr, li = lam[blk]
                nr = lr * xr - li * xi + xs[blk, pl.ds(r0, 8), 0:128]
                ni = lr * xi + li * xr + xs[blk, pl.ds(r0, 8), 128:256]
                xs[blk, pl.ds(r0, 8), 0:128] = nr
                xs[blk, pl.ds(r0, 8), 128:256] = ni
                new += [nr, ni]
            return tuple(new)

        c0 = []
        for blk in range(S5_BLOCKS):
            c0 += [cs_ref[0, blk, :, 0:128], cs_ref[0, blk, :, 128:256]]
        lax.fori_loop(0, tc, fstep, tuple(c0), unroll=4)

        def bstep(k, c):
            t = tc - 1 - k
            r0 = pl.multiple_of(t * 8, 8)
            rp = pl.multiple_of(jnp.maximum(t - 1, 0) * 8, 8)
            first = t == 0
            new_a, new_g = [], []
            for blk in range(S5_BLOCKS):
                ar, ai = c[0][2 * blk], c[0][2 * blk + 1]
                glr, gli = c[1][2 * blk], c[1][2 * blk + 1]
                lr, li = lam[blk]
                nr = lr * ar + li * ai + adj[blk, pl.ds(r0, 8), 0:128]
                ni = lr * ai - li * ar + adj[blk, pl.ds(r0, 8), 128:256]
                adj[blk, pl.ds(r0, 8), 0:128] = nr
                adj[blk, pl.ds(r0, 8), 128:256] = ni
                pr = jnp.where(first, cs_ref[0, blk, :, 0:128], xs[blk, pl.ds(rp, 8), 0:128])
                pi = jnp.where(first, cs_ref[0, blk, :, 128:256], xs[blk, pl.ds(rp, 8), 128:256])
                new_a += [nr, ni]
                new_g += [glr + nr * pr + ni * pi, gli + ni * pr - nr * pi]
            return tuple(new_a), tuple(new_g)

        a0, g0 = [], []
        for blk in range(S5_BLOCKS):
            a0 += [acarry[blk, :, 0:128], acarry[blk, :, 128:256]]
            g0 += [dlr_ref[blk], dli_ref[blk]]
        an, gn = lax.fori_loop(0, tc, bstep, (tuple(a0), tuple(g0)), unroll=2)
        for blk in range(S5_BLOCKS):
            acarry[blk, :, 0:128] = an[2 * blk]
            acarry[blk, :, 128:256] = an[2 * blk + 1]
            dlr_ref[blk] = gn[2 * blk]
            dli_ref[blk] = gn[2 * blk + 1]
        for blk in range(S5_BLOCKS):
            sl = slice(blk * 256, (blk + 1) * 256)
            ab = adj[blk].astype(BF16)
            _stage(tmp, jnp.dot(ab, rbt_ref[blk], preferred_element_type=F32))
            du[:, sl] = _gather_rows(tmp, tc) + d_ref[:, sl] * dy_ref[:, sl]
            drb_ref[blk] += lax.dot_general(lhsu[blk], ab, (((0,), (0,)), ((), ())), preferred_element_type=F32)
            drc_ref[blk] += lax.dot_general(lhsd[blk], xs[blk].astype(BF16), (((0,), (0,)), ((), ())),
                                            preferred_element_type=F32)
        xh, r = _rms_hat(x_ref[...])
        dg_ref[...] += jnp.sum(du[...] * xh, axis=0, keepdims=True)
        dxh = du[...] * g_ref[...]
        dx_ref[...] = r * (dxh - xh * jnp.mean(dxh * xh, axis=-1, keepdims=True)) + res_ref[...]

    rev = pl.BlockSpec((tc, D_MODEL), lambda i: (nc - 1 - i, 0))
    vec = pl.BlockSpec((1, D_MODEL), lambda i: (0, 0))
    mat = pl.BlockSpec((S5_BLOCKS, 256, 256), lambda i: (0, 0, 0))
    lamspec = pl.BlockSpec((S5_BLOCKS, 8, 128), lambda i: (0, 0, 0))
    big = pltpu.VMEM((S5_BLOCKS, 8 * tc, 256), F32)
    bigb = pltpu.VMEM((S5_BLOCKS, 8 * tc, 256), BF16)
    return pl.pallas_call(
        body, grid=(nc,),
        in_specs=[rev, vec, rev, rev, vec, pl.BlockSpec((1, S5_BLOCKS, 8, 256), lambda i: (nc - 1 - i, 0, 0, 0)),
                  mat, mat, mat, lamspec, lamspec],
        out_specs=[rev, vec, mat, mat, lamspec, lamspec, vec],
        out_shape=[_sds((n_rows, D_MODEL), F32), _sds((1, D_MODEL), F32), _sds((S5_BLOCKS, 256, 256), F32),
                   _sds((S5_BLOCKS, 256, 256), F32), _sds((S5_BLOCKS, 8, 128), F32), _sds((S5_BLOCKS, 8, 128), F32),
                   _sds((1, D_MODEL), F32)],
        scratch_shapes=[pltpu.VMEM((2, 8 * tc, 128), F32), pltpu.VMEM((tc, D_MODEL), F32), bigb, bigb, big, big,
                        pltpu.VMEM((S5_BLOCKS, 8, 256), F32)],
        name="s5_bwd", compiler_params=_params(("arbitrary",)))(
            x, gain, dy2, res, d_skip, cs, rb, rbt, rct, lam_r, lam_i)


def _s5_views(a_re, a_im, log_dt, b_re, b_im):
    return a_re[:, None, :], a_im[:, None, :], log_dt[:, None, None], jnp.swapaxes(b_re, 1, 2), jnp.swapaxes(b_im, 1, 2)


def _s5_factors(a_re, a_im, log_dt):
    lr, li, dt = jnp.minimum(a_re, LAMBDA_RE_MAX), a_im, jnp.exp(log_dt)
    mag, ang = jnp.exp(lr * dt), li * dt
    lbr, lbi = mag * jnp.cos(ang), mag * jnp.sin(ang)
    den = lr * lr + li * li
    fr, fi = ((lbr - 1.0) * lr + lbi * li) / den, (lbi * lr - (lbr - 1.0) * li) / den
    return lr, li, dt, lbr, lbi, fr, fi, den


def s5_prep(a_re, a_im, log_dt, b_re, b_im, c_re, c_im):
    def body(ar_ref, ai_ref, t_ref, br_ref, bi_ref, cr_ref, ci_ref, rb_ref, rbt_ref, rc_ref, rct_ref, lr_ref, li_ref):
        _, _, _, lbr, lbi, fr, fi, _ = _s5_factors(ar_ref[...], ai_ref[...], t_ref[...])
        lr_ref[...] = lbr
        li_ref[...] = lbi
        bre = fr * br_ref[...] - fi * bi_ref[...]
        bim = fr * bi_ref[...] + fi * br_ref[...]
        even = (lax.broadcasted_iota(jnp.int32, (256, S5_STATE), 0) // S5_GROUP) % 2 == 0

        def assemble(re, im):
            re, im = re.reshape(256, S5_STATE), im.reshape(256, S5_STATE)
            return jnp.concatenate([jnp.where(even, re, 0.0), jnp.where(even, 0.0, re), jnp.where(even, im, 0.0),
                                    jnp.where(even, 0.0, im)], axis=1)

        for blk in range(S5_BLOCKS):
            sl = slice(16 * blk, 16 * blk + 16)
            rb = assemble(bre[sl], bim[sl])
            rct = assemble(cr_ref[sl], -ci_ref[sl])
            rb_ref[blk] = rb.astype(BF16)
            rbt_ref[blk] = rb.T.astype(BF16)
            rct_ref[blk] = rct.astype(BF16)
            rc_ref[blk] = rct.T.astype(BF16)

    vm = pl.BlockSpec(memory_space=pltpu.VMEM)
    mat = _sds((S5_BLOCKS, 256, 256), BF16)
    lam = _sds((S5_GROUPS, 1, S5_STATE), F32)
    rb, rbt, rc, rct, lam_r, lam_i = pl.pallas_call(
        body, in_specs=[vm] * 7, out_specs=[vm] * 6, out_shape=[mat, mat, mat, mat, lam, lam], name="s5_prep",
        compiler_params=_params())(*_s5_views(a_re, a_im, log_dt, b_re, b_im), c_re, c_im)
    return rb, rbt, rc, rct, lam_r.reshape(S5_BLOCKS, 8, 128), lam_i.reshape(S5_BLOCKS, 8, 128)


def s5_param_bwd(mats, lams, a_re, a_im, log_dt, b_re, b_im):
    def body(m_ref, glr_ref, gli_ref, ar_ref, ai_ref, t_ref, br_ref, bi_ref,
             dar_ref, dai_ref, dt_ref, dbr_ref, dbi_ref, dcr_ref, dci_ref):
        lr, li, dt, lbr, lbi, fr, fi, den = _s5_factors(ar_ref[...], ai_ref[...], t_ref[...])
        shape = (S5_GROUPS, S5_GROUP, S5_STATE)
        gbr, gbi = m_ref[0:1024, 0:64].reshape(shape), m_ref[0:1024, 64:128].reshape(shape)
        dcr_ref[...] = m_ref[1024:2048, 0:64].reshape(shape)
        dci_ref[...] = -m_ref[1024:2048, 64:128].reshape(shape)
        br, bi = br_ref[...], bi_ref[...]
        dbr_ref[...] = fr * gbr + fi * gbi
        dbi_ref[...] = fr * gbi - fi * gbr
        dfr = jnp.sum(gbr * br + gbi * bi, axis=1, keepdims=True)
        dfi = jnp.sum(gbi * br - gbr * bi, axis=1, keepdims=True)
        nr, ni = (dfr * lr - dfi * li) / den, (dfr * li + dfi * lr) / den
        qr, qi = (fr * lr + fi * li) / den, (fi * lr - fr * li) / den
        lam_r, lam_i = -(dfr * qr + dfi * qi), -(dfi * qr - dfr * qi)
        gr, gi = glr_ref[...] + nr, gli_ref[...] + ni
        zr, zi = gr * lbr + gi * lbi, gi * lbr - gr * lbi
        a = ar_ref[...]
        dar_ref[...] = (lam_r + zr * dt) * jnp.where(a < LAMBDA_RE_MAX, 1.0, jnp.where(a == LAMBDA_RE_MAX, 0.5, 0.0))
        dai_ref[...] = lam_i + zi * dt
        dt_ref[...] = jnp.sum(zr * lr + zi * li, axis=2, keepdims=True) * dt

    vm = pl.BlockSpec(memory_space=pltpu.VMEM)
    state = _sds((S5_GROUPS, 1, S5_STATE), F32)
    wide = _sds((S5_GROUPS, S5_GROUP, S5_STATE), F32)
    glr = lams[0:32].reshape(S5_GROUPS, 1, S5_STATE)
    gli = lams[32:64].reshape(S5_GROUPS, 1, S5_STATE)
    dar, dai, ddt, dbr, dbi, dcr, dci = pl.pallas_call(
        body, in_specs=[vm] * 8, out_specs=[vm] * 7,
        out_shape=[state, state, _sds((S5_GROUPS, 1, 1), F32), wide, wide, wide, wide], name="s5_param_bwd",
        compiler_params=_params())(mats, glr, gli, *_s5_views(a_re, a_im, log_dt, b_re, b_im))
    return (dar.reshape(S5_GROUPS, S5_STATE), dai.reshape(S5_GROUPS, S5_STATE), ddt.reshape(S5_GROUPS),
            jnp.swapaxes(dbr, 1, 2), jnp.swapaxes(dbi, 1, 2), dcr, dci)


def s5_compact(drb, drct, dlr, dli):
    def body(drb_ref, drct_ref, dlr_ref, dli_ref, o_ref, lam_ref):
        even = (lax.broadcasted_iota(jnp.int32, (256, 64), 0) // S5_GROUP) % 2 == 0
        for blk in range(S5_BLOCKS):
            for k, ref in enumerate((drb_ref, drct_ref)):
                m = ref[blk]
                re = jnp.where(even, m[:, 0:64], m[:, 64:128])
                im = jnp.where(even, m[:, 128:192], m[:, 192:256])
                o_ref[pl.ds(k * 1024 + blk * 256, 256), :] = jnp.concatenate([re, im], axis=1)
            lam_ref[pl.ds(blk * 8, 8), :] = dlr_ref[blk]
            lam_ref[pl.ds(32 + blk * 8, 8), :] = dli_ref[blk]

    vm = pl.BlockSpec(memory_space=pltpu.VMEM)
    return pl.pallas_call(body, in_specs=[vm] * 4, out_specs=[vm, vm], out_shape=[_sds((2048, 128), F32), _sds((64, 128), F32)],
                          name="s5_compact", compiler_params=_params())(drb, drct, dlr, dli)


NEG = -1e30


GROUP = N_Q // N_KV


def _attn_masks(n):
    qi = lax.broadcasted_iota(jnp.int32, (GROUP * BLOCK, BLOCK), 0) % BLOCK
    kj = lax.broadcasted_iota(jnp.int32, (GROUP * BLOCK, BLOCK), 1)
    return jnp.logical_and(kj > qi, n > 0), kj <= qi


def _stack_heads(ref, kh):
    return jnp.concatenate([ref[:, (GROUP * kh + g) * HEAD_DIM:(GROUP * kh + g + 1) * HEAD_DIM] for g in range(GROUP)], axis=0)


def _unstack_heads(val):
    return jnp.concatenate([val[g * BLOCK:(g + 1) * BLOCK] for g in range(GROUP)], axis=1)


def _sink_column(sink_ref, kh):
    grp = lax.broadcasted_iota(jnp.int32, (GROUP * BLOCK, 1), 0) // BLOCK
    col = jnp.zeros((GROUP * BLOCK, 1), F32)
    for g in range(GROUP):
        col = jnp.where(grp == g, sink_ref[GROUP * kh + g], col)
    return col, grp


def _attn_exp(q4, kp, kc, sink, mask_p, mask_c):
    scale = 1.0 / math.sqrt(HEAD_DIM)
    nt = (((1,), (1,)), ((), ()))
    sp = jnp.where(mask_p, lax.dot_general(q4, kp, nt, preferred_element_type=F32) * scale, NEG)
    sc = jnp.where(mask_c, lax.dot_general(q4, kc, nt, preferred_element_type=F32) * scale, NEG)
    m = jnp.maximum(jnp.maximum(jnp.max(sp, axis=-1, keepdims=True), jnp.max(sc, axis=-1, keepdims=True)), sink)
    pp = jnp.exp(sp - m)
    pc = jnp.exp(sc - m)
    ps = jnp.exp(sink - m)
    inv = 1.0 / (jnp.sum(pp, axis=-1, keepdims=True) + jnp.sum(pc, axis=-1, keepdims=True) + ps)
    return pp, pc, ps, inv


def attn_fwd(q, kv, sinks):
    n_rows = q.shape[0]
    nb = n_rows // BLOCK

    def body(sink_ref, q_ref, kvp_ref, kvc_ref, o_ref):
        n = pl.program_id(0)
        mask_p, mask_c = _attn_masks(n)
        outs = []
        for kh in range(N_KV):
            ks, vs = slice(kh * HEAD_DIM, (kh + 1) * HEAD_DIM), slice((N_KV + kh) * HEAD_DIM, (N_KV + kh + 1) * HEAD_DIM)
            sink, _ = _sink_column(sink_ref, kh)
            pp, pc, _, inv = _attn_exp(_stack_heads(q_ref, kh), kvp_ref[:, ks], kvc_ref[:, ks], sink, mask_p, mask_c)
            o4 = (jnp.dot(pp.astype(BF16), kvp_ref[:, vs], preferred_element_type=F32)
                  + jnp.dot(pc.astype(BF16), kvc_ref[:, vs], preferred_element_type=F32)) * inv
            outs.append(_unstack_heads(o4))
        o_ref[...] = jnp.concatenate(outs, axis=1).astype(BF16)

    kvw = 2 * N_KV * HEAD_DIM
    return pl.pallas_call(
        body, grid=(nb,),
        in_specs=[pl.BlockSpec(memory_space=pltpu.SMEM), pl.BlockSpec((BLOCK, D_MODEL), lambda n: (n, 0)),
                  pl.BlockSpec((BLOCK, kvw), lambda n: (jnp.maximum(n - 1, 0), 0)), pl.BlockSpec((BLOCK, kvw), lambda n: (n, 0))],
        out_specs=pl.BlockSpec((BLOCK, D_MODEL), lambda n: (n, 0)), out_shape=_sds((n_rows, D_MODEL), BF16),
        name="attn_fwd", compiler_params=_params(("parallel",)))(sinks, q, kv, kv)


def attn_bwd(q, kv, do, sinks):
    n_rows = q.shape[0]
    nb = n_rows // BLOCK
    kvw = 2 * N_KV * HEAD_DIM
    tn = (((0,), (0,)), ((), ()))
    nt = (((1,), (1,)), ((), ()))
    scale = 1.0 / math.sqrt(HEAD_DIM)

    def body(sink_ref, q_ref, kvp_ref, kvc_ref, do_ref, dq_ref, dbq_ref, dprev_ref, dcur_ref, dsink_ref):
        n = pl.program_id(0)
        mask_p, mask_c = _attn_masks(n)
        lane = lax.broadcasted_iota(jnp.int32, (1, D_MODEL), 1)
        dqs, dsink = [], jnp.zeros((1, D_MODEL), F32)
        dkp, dkc, dvp, dvc = [], [], [], []
        for kh in range(N_KV):
            ks, vs = slice(kh * HEAD_DIM, (kh + 1) * HEAD_DIM), slice((N_KV + kh) * HEAD_DIM, (N_KV + kh + 1) * HEAD_DIM)
            q4, do4 = _stack_heads(q_ref, kh), _stack_heads(do_ref, kh)
            kp, kc, vp, vc = kvp_ref[:, ks], kvc_ref[:, ks], kvp_ref[:, vs], kvc_ref[:, vs]
            sink, grp = _sink_column(sink_ref, kh)
            pp, pc, ps, inv = _attn_exp(q4, kp, kc, sink, mask_p, mask_c)
            pp, pc = pp * inv, pc * inv
            dpp = lax.dot_general(do4, vp, nt, preferred_element_type=F32)
            dpc = lax.dot_general(do4, vc, nt, preferred_element_type=F32)
            delta = jnp.sum(pp * dpp, axis=-1, keepdims=True) + jnp.sum(pc * dpc, axis=-1, keepdims=True)
            dsp = (pp * (dpp - delta) * scale).astype(BF16)
            dsc = (pc * (dpc - delta) * scale).astype(BF16)
            dsk = ps * inv * delta
            for g in range(GROUP):
                dsink = dsink + jnp.where(lane == GROUP * kh + g, -jnp.sum(jnp.where(grp == g, dsk, 0.0)), 0.0)
            dqs.append(_unstack_heads(jnp.dot(dsp, kp, preferred_element_type=F32)
                                      + jnp.dot(dsc, kc, preferred_element_type=F32)))
            dkp.append(lax.dot_general(dsp, q4, tn, preferred_element_type=F32))
            dkc.append(lax.dot_general(dsc, q4, tn, preferred_element_type=F32))
            dvp.append(lax.dot_general(pp.astype(BF16), do4, tn, preferred_element_type=F32))
            dvc.append(lax.dot_general(pc.astype(BF16), do4, tn, preferred_element_type=F32))
        dq = jnp.concatenate(dqs, axis=1)
        dq_ref[...] = dq.astype(BF16)
        dprev_ref[0] = jnp.concatenate(dkp + dvp, axis=1)
        dcur_ref[0] = jnp.concatenate(dkc + dvc, axis=1)

        @pl.when(n == 0)
        def _():
            dbq_ref[...] = jnp.zeros_like(dbq_ref)
            dsink_ref[...] = jnp.zeros_like(dsink_ref)

        dbq_ref[...] += jnp.sum(dq, axis=0, keepdims=True)
        dsink_ref[...] += dsink

    blk = pl.BlockSpec((BLOCK, D_MODEL), lambda n: (n, 0))
    part = pl.BlockSpec((1, BLOCK, kvw), lambda n: (n, 0, 0))
    return pl.pallas_call(
        body, grid=(nb,),
        in_specs=[pl.BlockSpec(memory_space=pltpu.SMEM), blk,
                  pl.BlockSpec((BLOCK, kvw), lambda n: (jnp.maximum(n - 1, 0), 0)), pl.BlockSpec((BLOCK, kvw), lambda n: (n, 0)), blk],
        out_specs=[blk, pl.BlockSpec((1, D_MODEL), lambda n: (0, 0)), part, part, pl.BlockSpec((1, D_MODEL), lambda n: (0, 0))],
        out_shape=[_sds((n_rows, D_MODEL), BF16), _sds((1, D_MODEL), F32), _sds((nb, BLOCK, kvw), F32),
                   _sds((nb, BLOCK, kvw), F32), _sds((1, D_MODEL), F32)],
        name="attn_bwd", compiler_params=_params(("arbitrary",)))(sinks, q, kv, kv, do)


def kv_combine(dprev, dcur):
    nb, _, kvw = dprev.shape

    def body(dcur_ref, dprev_ref, dkv_ref, db_ref):
        total = jnp.zeros((1, kvw), F32)
        for m in range(nb):
            dkv = dcur_ref[m] + dprev_ref[m + 1] if m + 1 < nb else dcur_ref[m]
            dkv_ref[m * BLOCK:(m + 1) * BLOCK, :] = dkv.astype(BF16)
            total = total + jnp.sum(dkv, axis=0, keepdims=True)
        db_ref[...] = jnp.concatenate([total, jnp.zeros((1, D_MODEL - kvw), F32)], axis=1)

    vm = pl.BlockSpec(memory_space=pltpu.VMEM)
    return pl.pallas_call(body, in_specs=[vm, vm], out_specs=[vm, vm],
                          out_shape=[_sds((nb * BLOCK, kvw), BF16), _sds((1, D_MODEL), F32)], name="kv_combine",
                          compiler_params=_params())(dcur, dprev)


def glu_bwd(dout, val, gate, tm=256):
    n_rows, d = dout.shape

    def body(do_ref, v_ref, g_ref, dz_ref, db_ref):
        i = pl.program_id(0)
        sg = jax.nn.sigmoid(g_ref[...])
        dval = do_ref[...] * sg
        dgate = do_ref[...] * v_ref[...] * sg * (1.0 - sg)
        dz_ref[...] = jnp.concatenate([dval, dgate], axis=1).astype(BF16)

        @pl.when(i == 0)
        def _():
            db_ref[...] = jnp.zeros_like(db_ref)

        db_ref[0:1, :] += jnp.sum(dval, axis=0, keepdims=True)
        db_ref[1:2, :] += jnp.sum(dgate, axis=0, keepdims=True)

    row = pl.BlockSpec((tm, d), lambda i: (i, 0))
    return pl.pallas_call(
        body, grid=(n_rows // tm,), in_specs=[row, row, row],
        out_specs=[pl.BlockSpec((tm, 2 * d), lambda i: (i, 0)), pl.BlockSpec((2, d), lambda i: (0, 0))],
        out_shape=[_sds((n_rows, 2 * d), BF16), _sds((2, d), F32)],
        name="glu_bwd", compiler_params=_params(("arbitrary",)))(dout, val, gate)


def _adam_update(w, g, m, v):
    nm = ADAM_B1 * m + (1.0 - ADAM_B1) * g
    nv = ADAM_B2 * v + (1.0 - ADAM_B2) * (g * g)
    m_hat = nm / (1.0 - ADAM_B1 ** ADAM_STEP)
    v_hat = nv / (1.0 - ADAM_B2 ** ADAM_STEP)
    return -ADAM_LR * (m_hat / (jnp.sqrt(v_hat) + ADAM_EPS) + ADAM_WD * w), nm, nv


def adamw(name, ws, gs, ms, vs, steps=8):
    n = len(ws)

    def body(*refs):
        for k in range(n):
            w_ref, g_ref, m_ref, v_ref = (refs[j * n + k] for j in range(4))
            go_ref, d_ref, nm_ref, nv_ref = (refs[(4 + j) * n + k] for j in range(4))
            gv = g_ref[...]
            go_ref[...] = gv
            d_ref[...], nm_ref[...], nv_ref[...] = _adam_update(w_ref[...], gv, m_ref[...], v_ref[...])

    specs = [pl.BlockSpec((w.shape[0] // steps, w.shape[1]), lambda i: (i, 0)) for w in ws]
    shapes = [_sds(w.shape, F32) for w in ws]
    out = pl.pallas_call(
        body, grid=(steps,), in_specs=specs * 4, out_specs=specs * 4, out_shape=shapes * 4, name=name,
        compiler_params=_params(("parallel",)))(*ws, *gs, *ms, *vs)
    return [list(out[j * n:(j + 1) * n]) for j in range(4)]


def adamw_native(name, ws, gs, ms, vs):
    n = len(ws)

    def body(*refs):
        w_refs, g_refs, m_refs, v_refs = refs[:n], refs[n:2 * n], refs[2 * n:3 * n], refs[3 * n:4 * n]
        d_refs, nm_refs, nv_refs = refs[4 * n:5 * n], refs[5 * n:6 * n], refs[6 * n:7 * n]
        for k in range(n):
            dl, nm, nv = _adam_update(w_refs[k][...], g_refs[k][...], m_refs[k][...], v_refs[k][...])
            d_refs[k][...] = dl
            nm_refs[k][...] = nm
            nv_refs[k][...] = nv

    vm = pl.BlockSpec(memory_space=pltpu.VMEM)
    shapes = [_sds(w.shape, F32) for w in ws]
    out = pl.pallas_call(body, in_specs=[vm] * (4 * n), out_specs=[vm] * (3 * n), out_shape=shapes * 3, name=name,
                         compiler_params=_params())(*ws, *gs, *ms, *vs)
    return list(out[:n]), list(out[n:2 * n]), list(out[2 * n:])


VEC_ROWS = {"norm_mix": 0, "norm_mlp": 2, "norm_kv": 4, "norm_final": 5, "s5_d": 6, "b_q": 7, "b_o": 8, "s5_b_glu": 9,
            "b_kv": 11, "sinks": 12, "loss": 13}


def split_vectors(where, vecs, d_shard, glu_shard):
    kvw = 2 * N_KV * HEAD_DIM
    shapes = {"norm_mix": (2, D_MODEL), "norm_mlp": (2, D_MODEL), "norm_kv": (1, D_MODEL), "norm_final": (1, D_MODEL),
              "s5_d": (1, d_shard), "b_q": (1, D_MODEL), "b_o": (1, D_MODEL), "s5_b_glu": (1, glu_shard), "b_kv": (1, kvw),
              "sinks": (1, N_Q), "loss": (1, 128)}
    names = list(shapes)

    def body(where_ref, v_ref, *o_refs):
        chip = where_ref[1]
        for name, o_ref in zip(names, o_refs):
            r0, (r, n) = VEC_ROWS[name], shapes[name]
            if name == "s5_d":
                g = jnp.zeros((1, n), F32)
                for j in range(4):
                    g = jnp.where(chip == j, v_ref[r0:r0 + 1, j * n:(j + 1) * n], g)
            elif name == "s5_b_glu":
                g = jnp.zeros((1, n), F32)
                for j in range(4):
                    row, col = r0 + (j * n) // D_MODEL, (j * n) % D_MODEL
                    g = jnp.where(chip == j, v_ref[row:row + 1, col:col + n], g)
            else:
                g = v_ref[r0:r0 + r, 0:n]
            o_ref[...] = g

    vm = pl.BlockSpec(memory_space=pltpu.VMEM)
    out = pl.pallas_call(body, in_specs=[pl.BlockSpec(memory_space=pltpu.SMEM), vm], out_specs=[vm] * len(names),
                         out_shape=[_sds(shapes[n], F32) for n in names], name="split_vectors",
                         compiler_params=_params())(where, vecs)
    return dict(zip(names, out))


def _position():
    x, y, c = lax.axis_index("x"), lax.axis_index("y"), lax.axis_index("c")
    others = [(1 - x, y), (x, 1 - y), (1 - x, 1 - y)]
    return x, y, c, others


def _window(ref, kind, chip, half, shard_shape):
    if kind == "slab":
        return ref.at[chip]
    r, n = shard_shape
    if kind == "col":
        return ref.at[pl.ds(pl.multiple_of(half * (r // 2), 16), r // 2), pl.ds(pl.multiple_of(chip * n, 128), n)]
    return ref.at[pl.ds(pl.multiple_of(chip * r, 16), r), pl.ds(pl.multiple_of(half * (n // 2), 128), n // 2)]


def _half(ref, kind, half, shape):
    r, n = shape
    if kind == "col":
        return ref.at[pl.ds(pl.multiple_of(half * (r // 2), 16), r // 2), :]
    return ref.at[:, pl.ds(pl.multiple_of(half * (n // 2), 128), n // 2)]


def swap_start(name, grads, kinds, carry):
    nt = len(grads)
    shapes = [tuple(g.shape) for g in grads]
    lands = [lax.empty(sh, BF16) for sh in shapes]
    given, given_specs, token_type, write = _hand_through(carry)
    n_in = 2 * nt + len(given)

    def body(*refs):
        in_refs, land_refs = refs[:nt], refs[nt:2 * nt]
        send_sems, recv_sems, token = refs[n_in], refs[n_in + 1], refs[-1]
        x, y, c, _ = _position()
        for t in range(nt):
            pltpu.make_async_remote_copy(
                src_ref=_half(in_refs[t], kinds[t], 1 - c, shapes[t]), dst_ref=_half(land_refs[t], kinds[t], 1 - c, shapes[t]),
                send_sem=send_sems.at[t], recv_sem=recv_sems.at[t], device_id=(x, y, 1 - c), device_id_type=MESH).start()
        write(token, refs[:n_in])

    sems = pltpu.SemaphoreType.DMA((nt,))
    both = list(grads) + lands
    out = pl.pallas_call(
        body, name=name, in_specs=[HBM_SPEC] * (2 * nt) + given_specs,
        out_specs=(SEM_SPEC, SEM_SPEC, *[HBM_SPEC] * (2 * nt), pl.BlockSpec(memory_space=pltpu.VMEM)),
        out_shape=(sems, sems, *[pltpu.HBM(a.shape, a.dtype) for a in both], token_type),
        input_output_aliases={t: 2 + t for t in range(2 * nt)}, compiler_params=_split_params(),
    )(*[_in_hbm(a) for a in both], *given)
    return out[0], out[1], list(out[2:2 + nt]), list(out[2 + nt:2 + 2 * nt]), out[-1]


def swap_wait(name, send_sems, recv_sems, grads, lands, kinds, after):
    nt = len(grads)
    shapes = [tuple(g.shape) for g in grads]

    def body(*refs):
        in_refs, land_refs = refs[:nt], refs[nt:2 * nt]
        send_ref, recv_ref = refs[2 * nt], refs[2 * nt + 1]
        x, y, c, _ = _position()
        for t in range(nt):
            cp = pltpu.make_async_remote_copy(
                src_ref=_half(in_refs[t], kinds[t], 1 - c, shapes[t]), dst_ref=_half(land_refs[t], kinds[t], c, shapes[t]),
                send_sem=send_ref.at[t], recv_sem=recv_ref.at[t], device_id=(x, y, 1 - c), device_id_type=MESH)
            cp.wait_send()
            cp.wait_recv()

    both = list(grads) + list(lands)
    out = pl.pallas_call(
        body, name=name, in_specs=[HBM_SPEC] * (2 * nt) + [SEM_SPEC, SEM_SPEC, HBM_SPEC], out_specs=[HBM_SPEC] * (2 * nt),
        out_shape=[pltpu.HBM(a.shape, a.dtype) for a in both], input_output_aliases={t: t for t in range(2 * nt)},
        compiler_params=_split_params())(*both, send_sems, recv_sems, _in_hbm(after))
    return list(out[:nt]), list(out[nt:])


def _half_spec(kind, shape, tiles):
    r, n = shape
    if kind == "col":
        tn = n // tiles
        return pl.BlockSpec((r // 2, tn), lambda i, s: (s[0], i))
    tm = r // tiles
    return pl.BlockSpec((tm, n // 2), lambda i, s: (i, s[0]))


def add_halves(name, mine, landed, kinds, where, tiles=4):
    nt = len(mine)
    shapes = [tuple(a.shape) for a in mine]

    def compact(t):
        r, n = shapes[t]
        if kinds[t] == "col":
            return (r // 2, n), pl.BlockSpec((r // 2, n // tiles), lambda i, s: (0, i))
        return (r, n // 2), pl.BlockSpec((r // tiles, n // 2), lambda i, s: (i, 0))

    def body(s_ref, *refs):
        for a_ref, b_ref, o_ref in zip(refs[:nt], refs[nt:2 * nt], refs[2 * nt:]):
            o_ref[...] = (a_ref[...].astype(F32) + b_ref[...].astype(F32)).astype(BF16)

    specs = [_half_spec(kinds[t], shapes[t], tiles) for t in range(nt)]
    return pl.pallas_call(
        body, grid_spec=pltpu.PrefetchScalarGridSpec(num_scalar_prefetch=1, grid=(tiles,), in_specs=specs + specs,
                                                     out_specs=[compact(t)[1] for t in range(nt)]),
        out_shape=[_sds(compact(t)[0], BF16) for t in range(nt)], name=name,
        compiler_params=_params(("parallel",)))(where, *mine, *landed)


def sum_shards(name, parts, landed, kinds, shard_shapes, where, layers, n_layers, intos, tiles=2):
    nt = len(parts)
    in_specs, out_specs = [], []
    for t in range(nt):
        (r, n), layer = shard_shapes[t], layers[t]
        if kinds[t] == "col":
            tm, width = r // 2 // tiles, n
            own = pl.BlockSpec((tm, n), lambda i, s: (i, s[1]))
            out = pl.BlockSpec((None, tm, n), lambda i, s, layer=layer: (layer, s[0] * tiles + i, 0))
        else:
            tm, width = r // tiles, n // 2
            own = pl.BlockSpec((tm, n // 2), lambda i, s: (s[1] * tiles + i, 0))
            out = pl.BlockSpec((None, tm, n // 2), lambda i, s, layer=layer: (layer, i, s[0]))
        in_specs += [own, pl.BlockSpec((3, tm, width), lambda i, s: (0, i, 0))]
        out_specs.append(out)
    args, aliases = [where] + [a for pair in zip(parts, landed) for a in pair], {}
    for t in range(nt):
        if intos[t] is not None:
            aliases[len(args)] = t
            in_specs.append(pl.BlockSpec(memory_space=pl.ANY))
            args.append(intos[t])

    def body(s_ref, *refs):
        for t in range(nt):
            a_ref, l_ref, o_ref = refs[2 * t], refs[2 * t + 1], refs[len(in_specs) + t]
            o_ref[...] = ((a_ref[...].astype(F32) + l_ref[0].astype(F32)) + l_ref[1].astype(F32)) + l_ref[2].astype(F32)

    return pl.pallas_call(
        body, grid_spec=pltpu.PrefetchScalarGridSpec(num_scalar_prefetch=1, grid=(tiles,), in_specs=in_specs,
                                                     out_specs=out_specs),
        out_shape=[_sds((n_layers[t],) + tuple(shard_shapes[t]), F32) for t in range(nt)], input_output_aliases=aliases,
        name=name, compiler_params=_params(("parallel",)))(*args)


def share_start(arrays, entries, carry):
    na, nt = len(arrays), len(entries)
    given, given_specs, token_type, write = _hand_through(carry)
    n_in = na + len(given)

    def body(*refs):
        in_refs, send_sems, recv_sems, token = refs[:na], refs[n_in], refs[n_in + 1], refs[-1]
        x, y, c, _ = _position()
        for t, (a, layer, kind) in enumerate(entries):
            mine = _half(in_refs[a].at[layer], kind, c, tuple(arrays[a].shape[1:]))
            pltpu.make_async_remote_copy(
                src_ref=mine, dst_ref=mine, send_sem=send_sems.at[t], recv_sem=recv_sems.at[t],
                device_id=(x, y, 1 - c), device_id_type=MESH).start()
        write(token, refs[:n_in])

    sems = pltpu.SemaphoreType.DMA((nt,))
    out = pl.pallas_call(
        body, name="share_start", in_specs=[HBM_SPEC] * na + given_specs,
        out_specs=(SEM_SPEC, SEM_SPEC, *[HBM_SPEC] * na, pl.BlockSpec(memory_space=pltpu.VMEM)),
        out_shape=(sems, sems, *[pltpu.HBM(a.shape, a.dtype) for a in arrays], token_type),
        input_output_aliases={t: 2 + t for t in range(na)}, compiler_params=_split_params(),
    )(*[_in_hbm(a) for a in arrays], *given)
    return out[0], out[1], list(out[2:2 + na]), out[-1]


def share_wait(send_sems, recv_sems, arrays, entries, after):
    na = len(arrays)

    def body(*refs):
        in_refs, send_ref, recv_ref = refs[:na], refs[na], refs[na + 1]
        x, y, c, _ = _position()
        for t, (a, layer, kind) in enumerate(entries):
            shape = tuple(arrays[a].shape[1:])
            cp = pltpu.make_async_remote_copy(
                src_ref=_half(in_refs[a].at[layer], kind, c, shape), dst_ref=_half(in_refs[a].at[layer], kind, 1 - c, shape),
                send_sem=send_ref.at[t], recv_sem=recv_ref.at[t], device_id=(x, y, 1 - c), device_id_type=MESH)
            cp.wait_send()
            cp.wait_recv()

    return list(pl.pallas_call(
        body, name="share_wait", in_specs=[HBM_SPEC] * na + [SEM_SPEC, SEM_SPEC, HBM_SPEC], out_specs=[HBM_SPEC] * na,
        out_shape=[pltpu.HBM(a.shape, a.dtype) for a in arrays], input_output_aliases={t: t for t in range(na)},
        compiler_params=_split_params())(*arrays, send_sems, recv_sems, _in_hbm(after)))


HBM_SPEC = pl.BlockSpec(memory_space=pltpu.HBM)
SEM_SPEC = pl.BlockSpec(memory_space=pltpu.SEMAPHORE)
ANY_SPEC = pl.BlockSpec(memory_space=pl.ANY)


def _split_params():
    return pltpu.CompilerParams(has_side_effects=pltpu.SideEffectType.DATAFLOW_SIDE_EFFECTING,
                                vmem_limit_bytes=VMEM_LIMIT_BYTES)


def _in_hbm(a):
    return pltpu.with_memory_space_constraint(a, pltpu.HBM)


def cast_place(arrays, entries, where, tiles=2):
    in_specs, out_specs, fulls = [], [], []
    for a, layer, kind in entries:
        _, r, n = arrays[a].shape
        tm = r // tiles
        in_specs.append(pl.BlockSpec((None, tm, n), lambda i, s, layer=layer: (layer, i, 0)))
        if kind == "col":
            fulls.append((r, 4 * n))
            out_specs.append(pl.BlockSpec((tm, n), lambda i, s: (i, s[1])))
        else:
            fulls.append((4 * r, n))
            out_specs.append(pl.BlockSpec((tm, n), lambda i, s: (s[1] * tiles + i, 0)))
    nt = len(entries)

    def body(s_ref, *refs):
        for w_ref, o_ref in zip(refs[:nt], refs[nt:]):
            o_ref[...] = w_ref[...].astype(BF16)

    return pl.pallas_call(
        body, grid_spec=pltpu.PrefetchScalarGridSpec(num_scalar_prefetch=1, grid=(tiles,), in_specs=in_specs,
                                                     out_specs=out_specs),
        out_shape=[_sds(f, BF16) for f in fulls], name="cast_place",
        compiler_params=_params(("parallel",)))(where, *[arrays[a] for a, _, _ in entries])


def _hand_through(carry):
    given = [] if isinstance(carry, tuple) else [carry]

    def write(token, ins):
        token[...] = ins[-1][...] if given else jnp.zeros_like(token)

    return (given, [pl.BlockSpec(memory_space=pltpu.VMEM)] * len(given),
            _sds(carry if isinstance(carry, tuple) else carry.shape, F32), write)


def gather_start(fulls, kinds, shard_shapes, carry):
    nt = len(fulls)
    given, given_specs, token_type, write = _hand_through(carry)
    n_in = nt + len(given)

    def body(*refs):
        full_refs = refs[:nt]
        send_sems, recv_sems, token = refs[n_in], refs[n_in + 1], refs[-1]
        x, y, c, others = _position()
        for t in range(nt):
            mine = _window(full_refs[t], kinds[t], 2 * x + y, c, shard_shapes[t])
            for j, (ox, oy) in enumerate(others):
                pltpu.make_async_remote_copy(
                    src_ref=mine, dst_ref=mine, send_sem=send_sems.at[3 * t + j], recv_sem=recv_sems.at[3 * t + j],
                    device_id=(ox, oy, c), device_id_type=MESH).start()
        write(token, refs[:n_in])

    sems = pltpu.SemaphoreType.DMA((3 * nt,))
    out = pl.pallas_call(
        body, name="gather_start", in_specs=[HBM_SPEC] * nt + given_specs,
        out_specs=(SEM_SPEC, SEM_SPEC, *[HBM_SPEC] * nt, pl.BlockSpec(memory_space=pltpu.VMEM)),
        out_shape=(sems, sems, *[pltpu.HBM(f.shape, f.dtype) for f in fulls], token_type),
        input_output_aliases={t: 2 + t for t in range(nt)}, compiler_params=_split_params(),
    )(*[_in_hbm(f) for f in fulls], *given)
    return out[0], out[1], list(out[2:2 + nt]), out[-1]


def gather_wait(name, send_sems, recv_sems, fulls, kinds, shard_shapes, after, first):
    nt = len(fulls)
    extra = [] if after is None else [_in_hbm(after)]

    def body(*refs):
        full_refs, send_ref, recv_ref = refs[:nt], refs[nt], refs[nt + 1]
        x, y, c, others = _position()
        for t in range(nt):
            mine = _window(full_refs[t], kinds[t], 2 * x + y, c, shard_shapes[t])
            for j, (ox, oy) in enumerate(others):
                cp = pltpu.make_async_remote_copy(
                    src_ref=mine, dst_ref=_window(full_refs[t], kinds[t], 2 * ox + oy, c, shard_shapes[t]),
                    send_sem=send_ref.at[3 * (first + t) + j], recv_sem=recv_ref.at[3 * (first + t) + j],
                    device_id=(ox, oy, c), device_id_type=MESH)
                cp.wait_send()
                cp.wait_recv()

    out = pl.pallas_call(
        body, name=name, in_specs=[HBM_SPEC] * nt + [SEM_SPEC, SEM_SPEC] + [HBM_SPEC] * len(extra),
        out_specs=[HBM_SPEC] * nt, out_shape=[pltpu.HBM(f.shape, f.dtype) for f in fulls],
        input_output_aliases={t: t for t in range(nt)}, compiler_params=_split_params())(*fulls, send_sems, recv_sems, *extra)
    return list(out)


def forward_halves(name, fulls, kinds, shard_shapes):
    nt = len(fulls)

    def body(*refs):
        out_refs = refs[nt:2 * nt]
        send_sems, recv_sems = refs[2 * nt:]
        x, y, c, others = _position()
        cps = []
        for t in range(nt):
            for j, (ox, oy) in enumerate(others):
                landed = _window(out_refs[t], kinds[t], 2 * ox + oy, c, shard_shapes[t])
                cp = pltpu.make_async_remote_copy(
                    src_ref=landed, dst_ref=landed, send_sem=send_sems.at[3 * t + j], recv_sem=recv_sems.at[3 * t + j],
                    device_id=(x, y, 1 - c), device_id_type=MESH)
                cp.start()
                cps.append(cp)
        for t in range(nt):
            for j, (ox, oy) in enumerate(others):
                got = _window(out_refs[t], kinds[t], 2 * ox + oy, 1 - c, shard_shapes[t])
                pltpu.make_async_remote_copy(
                    src_ref=got, dst_ref=got, send_sem=send_sems.at[3 * t + j], recv_sem=recv_sems.at[3 * t + j],
                    device_id=(x, y, 1 - c), device_id_type=MESH).wait_recv()
        for cp in cps:
            cp.wait_send()

    out = pl.pallas_call(
        body, in_specs=[ANY_SPEC] * nt, out_specs=[ANY_SPEC] * nt, out_shape=[_sds(f.shape, f.dtype) for f in fulls],
        input_output_aliases={t: t for t in range(nt)},
        scratch_shapes=[pltpu.SemaphoreType.DMA((3 * nt,)), pltpu.SemaphoreType.DMA((3 * nt,))],
        name=name, compiler_params=_params())(*fulls)
    return list(out)


def forward_start(name, send_sems, recv_sems, fulls, kinds, shard_shapes, after, first, carry):
    nt = len(fulls)
    given, given_specs, token_type, write = _hand_through(carry)
    n_in = nt + 3 + len(given)

    def body(*refs):
        full_refs, ici_send, ici_recv = refs[:nt], refs[nt], refs[nt + 1]
        send_ref, recv_ref, token = refs[n_in], refs[n_in + 1], refs[-1]
        x, y, c, others = _position()
        for t in range(nt):
            mine = _window(full_refs[t], kinds[t], 2 * x + y, c, shard_shapes[t])
            for j, (ox, oy) in enumerate(others):
                landed = _window(full_refs[t], kinds[t], 2 * ox + oy, c, shard_shapes[t])
                cp = pltpu.make_async_remote_copy(
                    src_ref=mine, dst_ref=landed, send_sem=ici_send.at[3 * (first + t) + j],
                    recv_sem=ici_recv.at[3 * (first + t) + j], device_id=(ox, oy, c), device_id_type=MESH)
                cp.wait_send()
                cp.wait_recv()
                pltpu.make_async_remote_copy(
                    src_ref=landed, dst_ref=landed, send_sem=send_ref.at[3 * t + j], recv_sem=recv_ref.at[3 * t + j],
                    device_id=(x, y, 1 - c), device_id_type=MESH).start()
        write(token, refs[:n_in])

    sems = pltpu.SemaphoreType.DMA((3 * nt,))
    out = pl.pallas_call(
        body, name=name, in_specs=[HBM_SPEC] * nt + [SEM_SPEC, SEM_SPEC, HBM_SPEC] + given_specs,
        out_specs=(SEM_SPEC, SEM_SPEC, *[HBM_SPEC] * nt, pl.BlockSpec(memory_space=pltpu.VMEM)),
        out_shape=(sems, sems, *[pltpu.HBM(f.shape, f.dtype) for f in fulls], token_type),
        input_output_aliases={t: 2 + t for t in range(nt)}, compiler_params=_split_params(),
    )(*fulls, send_sems, recv_sems, _in_hbm(after), *given)
    return out[0], out[1], list(out[2:2 + nt]), out[-1]


def forward_wait(name, send_sems, recv_sems, fulls, kinds, shard_shapes, after):
    nt = len(fulls)

    def body(*refs):
        full_refs, send_ref, recv_ref = refs[:nt], refs[nt], refs[nt + 1]
        x, y, c, others = _position()
        for t in range(nt):
            for j, (ox, oy) in enumerate(others):
                cp = pltpu.make_async_remote_copy(
                    src_ref=_window(full_refs[t], kinds[t], 2 * ox + oy, c, shard_shapes[t]),
                    dst_ref=_window(full_refs[t], kinds[t], 2 * ox + oy, 1 - c, shard_shapes[t]),
                    send_sem=send_ref.at[3 * t + j], recv_sem=recv_ref.at[3 * t + j],
                    device_id=(x, y, 1 - c), device_id_type=MESH)
                cp.wait_send()
                cp.wait_recv()

    return list(pl.pallas_call(
        body, name=name, in_specs=[HBM_SPEC] * nt + [SEM_SPEC, SEM_SPEC, HBM_SPEC], out_specs=[HBM_SPEC] * nt,
        out_shape=[pltpu.HBM(f.shape, f.dtype) for f in fulls], input_output_aliases={t: t for t in range(nt)},
        compiler_params=_split_params())(*fulls, send_sems, recv_sems, _in_hbm(after)))


def _piece(ref, kind, chip, shard_shape):
    r, n = shard_shape
    if kind == "col":
        return ref.at[:, pl.ds(pl.multiple_of(chip * n, 128), n)]
    return ref.at[pl.ds(pl.multiple_of(chip * r, 16), r), :]


def _piece_shape(kind, shard_shape):
    r, n = shard_shape
    return (r // 2, n) if kind == "col" else (r, n // 2)


def exchange_start(name, parts, kinds, shard_shapes, carry):
    nt = len(parts)
    lands = [lax.empty((3,) + _piece_shape(kinds[t], shard_shapes[t]), BF16) for t in range(nt)]
    given, given_specs, token_type, write = _hand_through(carry)
    n_in = 2 * nt + len(given)

    def body(*refs):
        part_refs, land_refs = refs[:nt], refs[nt:2 * nt]
        send_sems, recv_sems, token = refs[n_in], refs[n_in + 1], refs[-1]
        x, y, c, others = _position()
        for t in range(nt):
            for j, (ox, oy) in enumerate(others):
                pltpu.make_async_remote_copy(
                    src_ref=_piece(part_refs[t], kinds[t], 2 * ox + oy, shard_shapes[t]), dst_ref=land_refs[t].at[j],
                    send_sem=send_sems.at[3 * t + j], recv_sem=recv_sems.at[3 * t + j],
                    device_id=(ox, oy, c), device_id_type=MESH).start()
        write(token, refs[:n_in])

    sems = pltpu.SemaphoreType.DMA((3 * nt,))
    both = list(parts) + lands
    out = pl.pallas_call(
        body, name=name, in_specs=[HBM_SPEC] * (2 * nt) + given_specs,
        out_specs=(SEM_SPEC, SEM_SPEC, *[HBM_SPEC] * (2 * nt), pl.BlockSpec(memory_space=pltpu.VMEM)),
        out_shape=(sems, sems, *[pltpu.HBM(a.shape, a.dtype) for a in both], token_type),
        input_output_aliases={t: 2 + t for t in range(2 * nt)}, compiler_params=_split_params(),
    )(*[_in_hbm(a) for a in both], *given)
    return out[0], out[1], list(out[2:2 + nt]), list(out[2 + nt:2 + 2 * nt]), out[-1]


def exchange_wait(name, send_sems, recv_sems, parts, lands, kinds, shard_shapes, after):
    nt = len(parts)

    def body(*refs):
        part_refs, land_refs = refs[:nt], refs[nt:2 * nt]
        send_ref, recv_ref = refs[2 * nt], refs[2 * nt + 1]
        x, y, c, others = _position()
        for t in range(nt):
            for j, (ox, oy) in enumerate(others):
                cp = pltpu.make_async_remote_copy(
                    src_ref=_piece(part_refs[t], kinds[t], 2 * ox + oy, shard_shapes[t]), dst_ref=land_refs[t].at[j],
                    send_sem=send_ref.at[3 * t + j], recv_sem=recv_ref.at[3 * t + j],
                    device_id=(ox, oy, c), device_id_type=MESH)
                cp.wait_send()
                cp.wait_recv()

    both = list(parts) + list(lands)
    out = pl.pallas_call(
        body, name=name, in_specs=[HBM_SPEC] * (2 * nt) + [SEM_SPEC, SEM_SPEC, HBM_SPEC], out_specs=[HBM_SPEC] * (2 * nt),
        out_shape=[pltpu.HBM(a.shape, a.dtype) for a in both], input_output_aliases={t: t for t in range(2 * nt)},
        compiler_params=_split_params())(*both, send_sems, recv_sems, _in_hbm(after))
    return list(out[:nt]), list(out[nt:])


def reduce_swap(bufs, wire):
    n = len(bufs)
    halves = [b.shape[0] // 2 for b in bufs]

    def body(*refs):
        in_refs, out_refs, txs, got = refs[:n], refs[n:2 * n], refs[2 * n:3 * n], refs[3 * n:4 * n]
        send_sems, recv_sems = refs[4 * n:]
        x, y, c, _ = _position()
        cps = []
        for k in range(n):
            txs[k][...] = in_refs[k][pl.ds(pl.multiple_of((1 - c) * halves[k], 8), halves[k]), :].astype(wire[k])
            cp = pltpu.make_async_remote_copy(src_ref=txs[k], dst_ref=got[k], send_sem=send_sems.at[k],
                                              recv_sem=recv_sems.at[k], device_id=(x, y, 1 - c), device_id_type=MESH)
            cp.start()
            cps.append(cp)
        for k, cp in enumerate(cps):
            cp.wait()
            own = in_refs[k][pl.ds(pl.multiple_of(c * halves[k], 8), halves[k]), :]
            out_refs[k][...] = (own.astype(wire[k]).astype(F32) + got[k][...].astype(F32)).astype(wire[k])

    vm = pl.BlockSpec(memory_space=pltpu.VMEM)
    parts = [((h, b.shape[1]), w) for h, b, w in zip(halves, bufs, wire)]
    return list(pl.pallas_call(
        body, name="reduce_swap", in_specs=[vm] * n, out_specs=[vm] * n, out_shape=[_sds(sh, w) for sh, w in parts],
        scratch_shapes=[pltpu.VMEM(sh, w) for sh, w in parts] * 2 + [pltpu.SemaphoreType.DMA((n,))] * 2,
        compiler_params=_params())(*bufs))


def reduce_start(parts):
    n = len(parts)
    lands = [lax.empty((4,) + tuple(p.shape), p.dtype) for p in parts]

    def body(*refs):
        part_refs, land_refs, send_sems, recv_sems = refs[:n], refs[n:2 * n], refs[2 * n], refs[2 * n + 1]
        x, y, c, others = _position()
        for k in range(n):
            for j, (ox, oy) in enumerate(others):
                pltpu.make_async_remote_copy(
                    src_ref=part_refs[k], dst_ref=land_refs[k].at[2 * x + y], send_sem=send_sems.at[3 * k + j],
                    recv_sem=recv_sems.at[3 * k + j], device_id=(ox, oy, c), device_id_type=MESH).start()

    sems = pltpu.SemaphoreType.DMA((3 * n,))
    both = list(parts) + lands
    out = pl.pallas_call(
        body, name="reduce_start", in_specs=[HBM_SPEC] * (2 * n), out_specs=(SEM_SPEC, SEM_SPEC, *[HBM_SPEC] * (2 * n)),
        out_shape=(sems, sems, *[pltpu.HBM(a.shape, a.dtype) for a in both]),
        input_output_aliases={k: 2 + k for k in range(2 * n)}, compiler_params=_split_params(),
    )(*[_in_hbm(a) for a in both])
    return out[0], out[1], list(out[2:2 + n]), list(out[2 + n:])


def reduce_wait(send_sems, recv_sems, parts, lands, after):
    n = len(parts)

    def body(*refs):
        part_refs, land_refs, send_ref, recv_ref = refs[:n], refs[n:2 * n], refs[2 * n], refs[2 * n + 1]
        x, y, c, others = _position()
        for k in range(n):
            for j, (ox, oy) in enumerate(others):
                cp = pltpu.make_async_remote_copy(
                    src_ref=part_refs[k], dst_ref=land_refs[k].at[2 * ox + oy], send_sem=send_ref.at[3 * k + j],
                    recv_sem=recv_ref.at[3 * k + j], device_id=(ox, oy, c), device_id_type=MESH)
                cp.wait_send()
                cp.wait_recv()

    both = list(parts) + list(lands)
    out = pl.pallas_call(
        body, name="reduce_wait", in_specs=[HBM_SPEC] * (2 * n) + [SEM_SPEC, SEM_SPEC, HBM_SPEC],
        out_specs=[HBM_SPEC] * (2 * n), out_shape=[pltpu.HBM(a.shape, a.dtype) for a in both],
        input_output_aliases={k: k for k in range(2 * n)}, compiler_params=_split_params(),
    )(*both, send_sems, recv_sems, _in_hbm(after))
    return list(out[:n]), list(out[n:])


def reduce_share(parts, lands):
    n = len(parts)
    halves = [p.shape[0] for p in parts]

    def body(*refs):
        part_refs, land_refs, out_refs = refs[:n], refs[n:2 * n], refs[2 * n:3 * n]
        send_sems, recv_sems = refs[3 * n:]
        x, y, c, _ = _position()
        chip = 2 * x + y
        cps = []
        for k in range(n):
            mine = pl.ds(pl.multiple_of(c * halves[k], 8), halves[k])
            own = part_refs[k][...].astype(F32)
            total = jnp.where(chip == 0, own, land_refs[k][0].astype(F32))
            for entry in range(1, 4):
                total = total + jnp.where(chip == entry, own, land_refs[k][entry].astype(F32))
            out_refs[k][mine, :] = total
            cp = pltpu.make_async_remote_copy(
                src_ref=out_refs[k].at[mine], dst_ref=out_refs[k].at[mine], send_sem=send_sems.at[k],
                recv_sem=recv_sems.at[k], device_id=(x, y, 1 - c), device_id_type=MESH)
            cp.start()
            cps.append(cp)
        for cp in cps:
            cp.wait()

    vm = pl.BlockSpec(memory_space=pltpu.VMEM)
    return list(pl.pallas_call(
        body, name="reduce_share", in_specs=[vm] * (2 * n), out_specs=[vm] * n,
        out_shape=[_sds((2 * p.shape[0], p.shape[1]), F32) for p in parts],
        scratch_shapes=[pltpu.SemaphoreType.DMA((n,))] * 2, compiler_params=_params())(*parts, *lands))


def _local_step(x, target, small, need, ahead, emit_swap, emit_exchange):
    d = D_MODEL
    full = {}

    def handed(vec, token):
        return vec if token is None else token

    def token_rows(token):
        return [] if token is None else [token]

    def plus(acc, rows):
        return acc + rows[0] if rows else acc

    rb16, rbt16, rc16, rct16, lr_t, li_t = small["s5_operands"]
    ge, y2, cs = s5_fwd(x, small["norm_mix0"], small["s5_d"], rb16, rc16, lr_t, li_t)
    full.update(need("glu", ge))

    def norm_rows(h, gains):
        xh, _ = _rms_hat(h)
        return [xh * g for g in gains]

    def glu_epilogue(accs, e, r):
        v, gt = accs[0] + r[0], accs[1] + r[1]
        h = e[0] + v * jax.nn.sigmoid(gt)
        return [h, v, gt] + norm_rows(h, r[2:])

    gain_mlp0 = handed(small["norm_mlp0"], ahead("mlp_in0", full["w_glu"], small["norm_mlp0"]))
    h1, val, gate, n1 = mm_nn(
        "glu", ge, full["w_glu"], [0, d], d, glu_epilogue, [F32, F32, F32, BF16], extras=[x],
        rowvecs=[(small["s5_b_glu"], 0), (small["s5_b_glu"], d), (gain_mlp0, 0)], tm=512, tn=d)

    def mlp_fwd(tag, h, n, w_in, get_w_out, next_gains, head=None):
        def in_epilogue(accs, e, rv):
            pos = jnp.maximum(accs[0], 0.0)
            return [pos * pos, 2.0 * pos]

        r, slope = mm_nn("mlp_in" + tag, n, w_in, [0], w_in.shape[1], in_epilogue, [BF16, BF16], tm=2048)
        w_out = get_w_out(r)

        def epilogue(accs, e, rv):
            h_out = e[0] + accs[0]
            return [h_out] + norm_rows(h_out, rv)

        if head is not None:
            return head(r, w_out, h), (n, r, slope)
        outs = mm_nn("mlp_out" + tag, r, w_out, [0], d, epilogue, [F32] + [BF16] * len(next_gains), extras=[h],
                     rowvecs=[(g, 0) for g in next_gains], tm=512, tn=d)
        return outs[0], outs[1:], (n, r, slope)

    full.update(need("mlp_in0", h1))

    def w_out0(after):
        full.update(need("mlp_out0", after))
        return full["w_out0"]

    h2, (nkv, n2), mlp0 = mlp_fwd("0", h1, n1, full["w_in0"], w_out0, [small["norm_kv"], small["norm_mix1"]])

    full.update(need("attn", h2))
    kvw = 2 * N_KV * HEAD_DIM
    (kv,) = mm_nn("kv_proj", nkv, full["w_kv"], [0], kvw, lambda accs, e, r: [accs[0] + r[0]], [BF16],
                  rowvecs=[(small["b_kv"], 0)], tm=2048)
    (q,) = mm_nn("q_proj", n2, full["w_q"], [0], d, lambda accs, e, r: [accs[0] + r[0]], [BF16],
                 rowvecs=[(small["b_q"], 0)], tm=2048)
    sinks = small["sinks"].reshape(N_Q)
    o = attn_fwd(q, kv, sinks)
    def o_epilogue(accs, e, r):
        h_out = e[0] + accs[0] + r[0]
        return [h_out] + norm_rows(h_out, r[1:])

    bias_o = handed(small["b_o"], ahead("mlp_in1", o, small["b_o"]))
    h3, n3 = mm_nn("o_proj", o, full["w_o"], [0], d, o_epilogue, [F32, BF16], extras=[h2],
                   rowvecs=[(bias_o, 0), (small["norm_mlp1"], 0)], tm=512, tn=d)
    full.update(need("mlp_in1", h3))

    def w_out1(after):
        full.update(need("mlp_out1", after))
        return full["w_out1"]

    def loss_head(r, w_out, h):
        def epilogue(accs, e, rv):
            xh, rr = _rms_hat(e[0] + accs[0])
            err = xh * rv[0] - e[1]
            dy = err * (1.0 / d)
            dxh = dy * rv[0]
            dx = rr * (dxh - xh * jnp.mean(dxh * xh, axis=-1, keepdims=True))
            loss = jnp.full((1, d), 0.5 * jnp.sum(jnp.mean(err * err, axis=-1, keepdims=True)), F32)
            return [dx, dx, loss, jnp.sum(dy * xh, axis=0, keepdims=True)]

        return mm_nn("mlp_out1", r, w_out, [0], d, epilogue, [F32, BF16], extras=[h, target],
                     rowvecs=[(small["norm_final"], 0)], n_sums=2, tm=512, tn=d)

    (dh, dhb, loss_tile, dg_final), mlp1 = mlp_fwd("1", h3, n3, full["w_in1"], w_out1, [], head=loss_head)

    grads_small, grads_full = {"norm_final": dg_final}, {}
    ident = lambda acc, e, r: [plus(acc, r)]
    layer1 = ["w_out1", "w_in1", "w_o", "w_q", "w_kv"]
    layer0 = ["w_out0", "w_in0", "w_glu"]

    def norm_bwd_rows(x_rows, res, dys, gains):
        xh, r = _rms_hat(x_rows)
        dxh = sum(dy * g for dy, g in zip(dys, gains))
        dx = r * (dxh - xh * jnp.mean(dxh * xh, axis=-1, keepdims=True)) + res
        return dx, [jnp.sum(dy * xh, axis=0, keepdims=True) for dy in dys]

    def mlp_bwd(tag, dh, dhb, h_in, gain, w_in, w_out, saved, token=None):
        n, r, slope = saved
        grads_full["w_out" + tag] = mm_tn("dw_out" + tag, r, dhb, tn=1024)
        (da,) = mm_nt("mlp_da" + tag, dhb, w_out, lambda acc, e, rv: [plus(acc * e[0].astype(F32), rv)], [BF16],
                      extras=[slope], rowvecs=token_rows(token), tm=2048)
        grads_full["w_in" + tag] = mm_tn("dw_in" + tag, n, da, tn=1024)

        def epilogue(acc, e, rv):
            dx, dgs = norm_bwd_rows(e[0], e[1], [acc], rv)
            return [dx, dx, jnp.sum(dx, axis=0, keepdims=True)] + dgs

        dx, dxb, colsum, dg = mm_nt("mlp_dn" + tag, da, w_in, epilogue, [F32, BF16], extras=[h_in, dh], rowvecs=[gain],
                                    n_sums=2, tm=512, tk=d)
        grads_small["norm_mlp" + tag] = dg
        return dx, dxb, colsum

    dh3, dh3b, colsum3 = mlp_bwd("1", dh, dhb, h3, small["norm_mlp1"], full["w_in1"], full["w_out1"], mlp1)
    grads_small["b_o"] = colsum3
    grads_full["w_o"] = mm_tn("dw_o", o, dh3b, tn=1024)
    (do,) = mm_nt("attn_do", dh3b, full["w_o"], ident, [BF16], tm=2048)
    dq, dbq, dprev, dcur, dsink = attn_bwd(q, kv, do, sinks)
    dkv, dbkv = kv_combine(dprev, dcur)
    grads_small["b_q"], grads_small["b_kv"], grads_small["sinks"] = dbq, dbkv, dsink
    grads_full["w_q"] = mm_tn("dw_q", n2, dq, tn=1024)
    grads_full["w_kv"] = mm_tn("dw_kv", nkv, dkv, tk=1024)
    token = emit_swap("layer1", {n: grads_full[n] for n in layer1}, (1, d))
    (dnkv,) = mm_nt("kv_dn", dkv, full["w_kv"], ident, [F32], rowvecs=token_rows(token), tm=2048, tk=1024)

    def attn_dn_epilogue(acc, e, rv):
        dx, dgs = norm_bwd_rows(e[0], e[1], [acc, e[2]], rv)
        return [dx, dx] + dgs

    dh2, dh2b, dg_mix1, dg_kv = mm_nt("attn_dn", dq, full["w_q"], attn_dn_epilogue, [F32, BF16], extras=[h2, dh3, dnkv],
                                      rowvecs=[small["norm_mix1"], small["norm_kv"]], n_sums=2, tm=512, tk=d)
    grads_small["norm_mix1"], grads_small["norm_kv"] = dg_mix1, dg_kv
    token = emit_exchange("layer1", dh2b, (1, full["w_out0"].shape[0]))
    dh1, _, _ = mlp_bwd("0", dh2, dh2b, h1, small["norm_mlp0"], full["w_in0"], full["w_out0"], mlp0, token)

    dz, db_glu = glu_bwd(dh1, val, gate)
    grads_small["s5_b_glu"] = db_glu
    grads_full["w_glu"] = mm_tn("dw_glu", ge, dz, tn=1024)
    token = emit_swap("layer0", {n: grads_full[n] for n in layer0}, (1, d))
    (dy2,) = mm_nt("glu_dy", dz, full["w_glu"], lambda acc, e, rv: [plus(acc, rv) * _gelu_grad(e[0])], [F32], extras=[y2],
                   rowvecs=token_rows(token), tm=1024, tk=1024)
    d_skip = handed(small["s5_d"], emit_exchange("layer0", dy2, small["s5_d"]))
    grad_x, dd, drb, drc, dlr, dli, dg_mix0 = s5_bwd(x, small["norm_mix0"], dy2, dh1, d_skip, cs, rb16, rbt16, rct16, lr_t, li_t)
    grads_small["s5_d"] = dd
    grads_small["s5_mats"] = (drb, drc, dlr, dli)
    grads_small["norm_mix0"] = dg_mix0
    return loss_tile, grad_x, grads_small


SMALL_NAMES = ["norm_mix", "norm_mlp", "norm_kv", "norm_final", "s5_a_re", "s5_a_im", "s5_log_dt", "s5_b_re", "s5_b_im",
               "s5_c_re", "s5_c_im", "s5_d", "s5_b_glu", "b_kv", "b_q", "sinks", "b_o"]
BIG_NAMES = ["s5_w_glu", "w_kv", "w_q", "w_o", "w_mlp_in", "w_mlp_out"]
WEIGHT_ORDER = ["norm_mix", "norm_mlp", "norm_kv", "norm_final", "s5_a_re", "s5_a_im", "s5_log_dt", "s5_b_re", "s5_b_im",
                "s5_c_re", "s5_c_im", "s5_d", "s5_w_glu", "s5_b_glu", "w_kv", "b_kv", "w_q", "b_q", "sinks", "w_o", "b_o",
                "w_mlp_in", "w_mlp_out"]


def kernel(x, norm_mix, norm_mlp, norm_kv, norm_final, s5_a_re, s5_a_im, s5_log_dt, s5_b_re, s5_b_im, s5_c_re, s5_c_im, s5_d, s5_w_glu, s5_b_glu, w_kv, b_kv, w_q, b_q, sinks, w_o, b_o, w_mlp_in, w_mlp_out, loss_target, m_norm_mix, m_norm_mlp, m_norm_kv, m_norm_final, m_s5_a_re, m_s5_a_im, m_s5_log_dt, m_s5_b_re, m_s5_b_im, m_s5_c_re, m_s5_c_im, m_s5_d, m_s5_w_glu, m_s5_b_glu, m_w_kv, m_b_kv, m_w_q, m_b_q, m_sinks, m_w_o, m_b_o, m_w_mlp_in, m_w_mlp_out, v_norm_mix, v_norm_mlp, v_norm_kv, v_norm_final, v_s5_a_re, v_s5_a_im, v_s5_log_dt, v_s5_b_re, v_s5_b_im, v_s5_c_re, v_s5_c_im, v_s5_d, v_s5_w_glu, v_s5_b_glu, v_w_kv, v_b_kv, v_w_q, v_b_q, v_sinks, v_w_o, v_b_o, v_w_mlp_in, v_w_mlp_out):
    env = dict(locals())
    w = {n: env[n] for n in WEIGHT_ORDER}
    mom = {n: env["m_" + n] for n in WEIGHT_ORDER}
    var = {n: env["v_" + n] for n in WEIGHT_ORDER}
    d = D_MODEL
    xi, yi, ci = lax.axis_index("x"), lax.axis_index("y"), lax.axis_index("c")
    chip = 2 * xi + yi
    where = jnp.stack([ci, chip]).astype(jnp.int32)

    dsh, bsh = s5_d.shape[1], s5_b_glu.shape[1]
    packed = jnp.concatenate([s5_d.reshape(-1, 128), s5_b_glu.reshape(-1, 128)])
    n_d, n_b = dsh // 128, bsh // 128
    slab = lax.dynamic_update_slice(jnp.zeros((4, 8, 128), F32), jnp.pad(packed, ((0, 8 - n_d - n_b), (0, 0)))[None],
                                    (chip, 0, 0))

    big = [s5_w_glu, w_kv[None], w_q, w_o, w_mlp_in, w_mlp_out]
    entries = [(0, 0, "col"), (1, 0, "row"), (2, 0, "row"), (3, 0, "row"), (4, 0, "col"), (4, 1, "col"),
               (5, 0, "row"), (5, 1, "row")]
    names = ["w_glu", "w_kv", "w_q", "w_o", "w_in0", "w_in1", "w_out0", "w_out1"]
    kinds = dict(zip(names, [k for _, _, k in entries]))
    shard_shapes = dict(zip(names, [tuple(big[a].shape[1:]) for a, _, _ in entries]))

    placed_w = dict(zip(names, cast_place(big, entries, where)))
    placed_w["vectors"], kinds["vectors"], shard_shapes["vectors"] = slab, "slab", None
    gather_groups = {"glu": ["w_glu"], "mlp_in0": ["w_in0"], "mlp_out0": ["w_out0"], "attn": ["w_kv", "w_q", "w_o"],
                     "mlp_in1": ["w_in1"], "mlp_out1": ["w_out1"]}
    order = ["vectors"] + [n for members in gather_groups.values() for n in members]
    send, recv, thru, log_dt = gather_start([placed_w[n] for n in order], [kinds[n] for n in order],
                                            [shard_shapes[n] for n in order], s5_log_dt)
    started = dict(zip(order, thru))
    (gathered_rows,) = gather_wait("gather_wait_vectors", send, recv, [started["vectors"]], ["slab"], [None], None, 0)
    d_full = gathered_rows[:, 0:n_d].reshape(1, -1)
    bglu_full = gathered_rows[:, n_d:n_d + n_b].reshape(1, -1)

    forwarding = {}

    def ahead(group, after, carry):
        members = gather_groups[group]
        ks, shapes = [kinds[n] for n in members], [shard_shapes[n] for n in members]
        d2d_send, d2d_recv, landed, tok = forward_start(
            "forward_start_" + group, send, recv, [started[n] for n in members], ks, shapes, after,
            order.index(members[0]), carry)
        forwarding[group] = (d2d_send, d2d_recv, landed)
        return tok

    def need(group, after):
        members = gather_groups[group]
        ks, shapes = [kinds[n] for n in members], [shard_shapes[n] for n in members]
        if group in forwarding:
            return dict(zip(members, forward_wait("forward_wait_" + group, *forwarding[group], ks, shapes, after)))
        landed = gather_wait("gather_wait_" + group, send, recv, [started[n] for n in members], ks, shapes, after,
                             order.index(members[0]))
        return dict(zip(members, forward_halves("forward_halves_" + group, landed, ks, shapes)))

    swapping, exchanging = {}, {}

    def emit_swap(group, partial, carry):
        members = list(partial)
        send, recv, mine, lands, tok = swap_start("swap_start_" + group, [partial[n] for n in members],
                                                  [kinds[n] for n in members], carry)
        swapping[group] = (members, send, recv, mine, lands)
        return tok

    def emit_exchange(group, after, carry):
        members, send, recv, mine, lands = swapping[group]
        ks, shapes = [kinds[n] for n in members], [shard_shapes[n] for n in members]
        mine, landed = swap_wait("swap_wait_" + group, send, recv, mine, lands, ks, after)
        sums = add_halves("add_halves_" + group, mine, landed, ks, where)
        send, recv, parts, lands, tok = exchange_start("exchange_start_" + group, sums, ks, shapes, carry)
        exchanging[group] = (members, send, recv, parts, lands)
        return tok

    s5_args = (s5_a_re[0], s5_a_im[0], log_dt[0], s5_b_re[0], s5_b_im[0])
    small = {
        "norm_mix0": norm_mix[0:1], "norm_mix1": norm_mix[1:2], "norm_mlp0": norm_mlp[0:1], "norm_mlp1": norm_mlp[1:2],
        "norm_kv": norm_kv.reshape(1, d), "norm_final": norm_final.reshape(1, d), "s5_operands": s5_prep(*s5_args, s5_c_re[0], s5_c_im[0]),
        "s5_d": d_full, "s5_b_glu": bglu_full,
        "b_kv": b_kv.reshape(1, -1), "b_q": b_q, "sinks": sinks, "b_o": b_o,
    }
    loss_row, grad_x, gs = _local_step(x[0], loss_target[0], small, need, ahead, emit_swap, emit_exchange)

    mats, lams = s5_compact(*gs["s5_mats"])
    rows = [gs["norm_mix0"], gs["norm_mix1"], gs["norm_mlp0"], gs["norm_mlp1"], gs["norm_kv"], gs["norm_final"], gs["s5_d"],
            gs["b_q"], gs["b_o"], gs["s5_b_glu"], gs["b_kv"], gs["sinks"], loss_row, jnp.zeros((2, d), F32)]
    small_send, small_recv, small_parts, small_lands = reduce_start(
        reduce_swap([jnp.concatenate(rows, axis=0), lams, mats], [F32, F32, BF16]))

    reduced = [None] * len(big)
    where_of = dict(zip(names, entries))
    for group in ("layer1", "layer0"):
        members, send, recv, parts, lands = exchanging[group]
        ks, shapes = [kinds[n] for n in members], [shard_shapes[n] for n in members]
        parts, lands = exchange_wait("exchange_wait_" + group, send, recv, parts, lands, ks, shapes, small_lands[-1])
        targets = [where_of[n][0] for n in members]
        sums = sum_shards("sum_shards_" + group, parts, lands, ks, shapes, where, [where_of[n][1] for n in members],
                          [big[a].shape[0] for a in targets], [reduced[a] for a in targets])
        for a, arr in zip(targets, sums):
            reduced[a] = arr
    share_send, share_recv, reduced, _ = share_start(reduced, entries, (8, 128))

    vecs, lams, mats = reduce_share(*reduce_wait(small_send, small_recv, small_parts, small_lands, reduced[0]))
    grads = split_vectors(where, vecs, dsh, bsh)
    loss = grads.pop("loss")[0, 0]
    g_are, g_aim, g_dt, g_bre, g_bim, dc_re, dc_im = s5_param_bwd(mats, lams, *s5_args)
    grads.update({"s5_a_re": g_are[None], "s5_a_im": g_aim[None], "s5_log_dt": g_dt[None], "s5_b_re": g_bre[None],
                  "s5_b_im": g_bim[None], "s5_c_re": dc_re[None], "s5_c_im": dc_im[None]})

    delta, new_m, new_v = {}, {}, {}

    def view(n, a):
        return a.reshape(1, -1) if a.ndim == 1 else jnp.swapaxes(a, -1, -2) if n in ("s5_b_re", "s5_b_im") else a

    sw, sg, sm, sv = ([view(n, t[n]) for n in SMALL_NAMES] for t in (w, grads, mom, var))
    for n, a, b, c_ in zip(SMALL_NAMES, *adamw_native("adamw_small", sw, sg, sm, sv)):
        delta[n], new_m[n], new_v[n] = (view(n, t) if t.ndim == 4 else t for t in (a, b, c_))

    reduced = share_wait(share_send, share_recv, reduced, entries, new_v["s5_c_re"])
    for n, g in zip(BIG_NAMES, reduced):
        grads[n] = g.reshape(w[n].shape)
    flat = lambda t: [t[n].reshape(-1, t[n].shape[-1]) for n in BIG_NAMES]
    for table, arrays in zip((grads, delta, new_m, new_v), adamw("adamw_big", flat(w), flat(grads), flat(mom), flat(var))):
        for n, a in zip(BIG_NAMES, arrays):
            table[n] = a.reshape(w[n].shape)

    out = [loss.reshape(()), grad_x[None]]
    for table in (grads, delta, new_m, new_v):
        out += [table[n].reshape(w[n].shape) for n in WEIGHT_ORDER]
    return tuple(out)
```

```python
import math

import jax
import jax.numpy as jnp
from jax import lax
from jax.experimental import pallas as pl
from jax.experimental.pallas import tpu as pltpu

F32 = jnp.float32
BF16 = jnp.bfloat16

D_MODEL = 1024
S5_GROUPS = 64
S5_GROUP = 16
S5_STATE = 64
N_KV = 4
N_Q = 16
HEAD_DIM = 64
BLOCK = 128
NORM_EPS = 1e-5
LAMBDA_RE_MAX = -1e-4
ADAM_LR, ADAM_B1, ADAM_B2, ADAM_EPS, ADAM_WD, ADAM_STEP = 0.001, 0.9, 0.999, 1e-08, 0.01, 10

VMEM_LIMIT_BYTES = 56 * 1024 * 1024
S5_CHUNK = 256
S5_BLOCKS = 4
MESH = pl.DeviceIdType.MESH


def _params(sem=None):
    return pltpu.CompilerParams(dimension_semantics=sem, vmem_limit_bytes=VMEM_LIMIT_BYTES)


def _sds(shape, dtype):
    return jax.ShapeDtypeStruct(shape, dtype)


def _rms_hat(xv):
    r = lax.rsqrt(jnp.mean(xv * xv, axis=-1, keepdims=True) + NORM_EPS)
    return xv * r, r


def mm_nn(name, a, w, col_offsets, n_out, epilogue, out_dtypes, extras=(), rowvecs=(), n_sums=0, tm=1024, tn=512):
    m, k = a.shape
    tm, tn = min(tm, m), min(tn, n_out)
    nw, ne, nr, no = len(col_offsets), len(extras), len(rowvecs), len(out_dtypes)

    def body(a_ref, *refs):
        w_refs, e_refs, r_refs = refs[:nw], refs[nw:nw + ne], refs[nw + ne:nw + ne + nr]
        o_refs, s_refs = refs[nw + ne + nr:nw + ne + nr + no], refs[nw + ne + nr + no:]
        av = a_ref[...]
        accs = [jnp.dot(av, w_ref[...], preferred_element_type=F32) for w_ref in w_refs]
        outs = epilogue(accs, [e[...] for e in e_refs], [r[...] for r in r_refs])
        for o_ref, o in zip(o_refs, outs[:no]):
            o_ref[...] = o.astype(o_ref.dtype)
        if n_sums:
            @pl.when(pl.program_id(1) == 0)
            def _():
                for s_ref in s_refs:
                    s_ref[...] = jnp.zeros_like(s_ref)

            for s_ref, val in zip(s_refs, outs[no:]):
                s_ref[...] += val

    def wspec(off):
        return pl.BlockSpec((k, tn), lambda j, i, off=off: (0, off // tn + j))

    def rspec(off):
        return pl.BlockSpec((1, tn), lambda j, i, off=off: (0, off // tn + j))

    tile = pl.BlockSpec((tm, tn), lambda j, i: (i, j))
    in_specs = ([pl.BlockSpec((tm, k), lambda j, i: (i, 0))] + [wspec(o) for o in col_offsets]
                + [tile] * ne + [rspec(o) for _, o in rowvecs])
    sem = ("parallel", "arbitrary") if n_sums else ("parallel", "parallel")
    return pl.pallas_call(
        body, grid=(n_out // tn, m // tm), in_specs=in_specs,
        out_specs=[tile] * no + [pl.BlockSpec((1, tn), lambda j, i: (0, j))] * n_sums,
        out_shape=[_sds((m, n_out), dt) for dt in out_dtypes] + [_sds((1, n_out), F32)] * n_sums, name=name,
        compiler_params=_params(sem))(a, *([w] * nw), *extras, *[r for r, _ in rowvecs])


def mm_nt(name, g, w, epilogue, out_dtypes, extras=(), rowvecs=(), n_sums=0, tm=512, tk=512):
    m, n = g.shape
    k = w.shape[0]
    tm, tk = min(tm, m), min(tk, k)
    ne, nr, no = len(extras), len(rowvecs), len(out_dtypes)

    def body(g_ref, w_ref, *refs):
        e_refs, r_refs, o_refs, s_refs = refs[:ne], refs[ne:ne + nr], refs[ne + nr:ne + nr + no], refs[ne + nr + no:]
        acc = lax.dot_general(g_ref[...], w_ref[...], (((1,), (1,)), ((), ())), preferred_element_type=F32)
        outs = epilogue(acc, [e[...] for e in e_refs], [r[...] for r in r_refs])
        for o_ref, o in zip(o_refs, outs[:no]):
            o_ref[...] = o.astype(o_ref.dtype)
        if n_sums:
            @pl.when(pl.program_id(0) == 0)
            def _():
                for s_ref in s_refs:
                    s_ref[...] = jnp.zeros_like(s_ref)

            for s_ref, val in zip(s_refs, outs[no:]):
                s_ref[...] += val

    tile = pl.BlockSpec((tm, tk), lambda i, j: (i, j))
    vec = pl.BlockSpec((1, tk), lambda i, j: (0, j))
    sem = ("arbitrary", "parallel") if n_sums else ("parallel", "parallel")
    return pl.pallas_call(
        body, grid=(m // tm, k // tk),
        in_specs=[pl.BlockSpec((tm, n), lambda i, j: (i, 0)), pl.BlockSpec((tk, n), lambda i, j: (j, 0))]
        + [tile] * ne + [vec] * nr,
        out_specs=[tile] * no + [vec] * n_sums,
        out_shape=[_sds((m, k), dt) for dt in out_dtypes] + [_sds((1, k), F32)] * n_sums, name=name,
        compiler_params=_params(sem))(g, w, *extras, *rowvecs)


def mm_tn(name, a, g, tk=512, tn=512):
    m, k = a.shape
    n = g.shape[1]
    tk, tn = min(tk, k), min(tn, n)

    def body(a_ref, g_ref, o_ref):
        acc = lax.dot_general(a_ref[...], g_ref[...], (((0,), (0,)), ((), ())), preferred_element_type=F32)
        o_ref[...] = acc.astype(o_ref.dtype)

    return pl.pallas_call(
        body, grid=(k // tk, n // tn),
        in_specs=[pl.BlockSpec((m, tk), lambda i, j: (0, i)), pl.BlockSpec((m, tn), lambda i, j: (0, j))],
        out_specs=pl.BlockSpec((tk, tn), lambda i, j: (i, j)), out_shape=_sds((k, n), BF16), name=name,
        compiler_params=_params(("parallel", "parallel")))(a, g)


def _row_mask(tc):
    row = lax.broadcasted_iota(jnp.int32, (8 * tc, 256), 0) % 8
    col = lax.broadcasted_iota(jnp.int32, (8 * tc, 256), 1) // 32
    return row == col


def _expand_rows(val, mask):
    tc, width = val.shape
    rep = jnp.broadcast_to(val[:, None, :], (tc, 8, width)).reshape(8 * tc, width)
    return jnp.where(mask, rep, 0.0).astype(BF16)


def _stage(ref, val):
    ref[0] = val[:, 0:128]
    ref[1] = val[:, 128:256]


def _gather_rows(src_ref, tc):
    halves = []
    for half in range(2):
        col = lax.broadcasted_iota(jnp.int32, (tc, 128), 1) // 32 + 4 * half
        out = jnp.zeros((tc, 128), F32)
        for s8 in range(4 * half, 4 * half + 4):
            out = jnp.where(col == s8, src_ref.at[half][pl.ds(s8, tc, stride=8), :], out)
        halves.append(out)
    return jnp.concatenate(halves, axis=1)


def _gelu(x):
    c = math.sqrt(2.0 / math.pi)
    return 0.5 * x * (1.0 + jnp.tanh(c * (x + 0.044715 * x * x * x)))


def _gelu_grad(x):
    c = math.sqrt(2.0 / math.pi)
    t = jnp.tanh(c * (x + 0.044715 * x * x * x))
    return 0.5 * (1.0 + t) + 0.5 * x * (1.0 - t * t) * c * (1.0 + 3.0 * 0.044715 * x * x)


def s5_fwd(x, gain, d_skip, rb, rc, lam_r, lam_i):
    n_rows = x.shape[0]
    tc = min(S5_CHUNK, n_rows)
    nc = n_rows // tc

    def body(x_ref, g_ref, d_ref, rb_ref, rc_ref, lr_ref, li_ref, ge_ref, y2_ref, cs_ref, bux, yrows, carry):
        i = pl.program_id(0)
        u = _rms_hat(x_ref[...])[0] * g_ref[...]

        @pl.when(i == 0)
        def _():
            carry[...] = jnp.zeros_like(carry)

        cs_ref[0] = carry[...]
        mask = _row_mask(tc)
        for blk in range(S5_BLOCKS):
            lhs = _expand_rows(u[:, blk * 256:(blk + 1) * 256], mask)
            bux[blk] = jnp.dot(lhs, rb_ref[blk], preferred_element_type=F32)
        lam = [(lr_ref[blk], li_ref[blk]) for blk in range(S5_BLOCKS)]

        def step(t, c):
            r0 = pl.multiple_of(t * 8, 8)
            new = []
            for blk in range(S5_BLOCKS):
                xr, xi = c[2 * blk], c[2 * blk + 1]
                lr, li = lam[blk]
                nr = lr * xr - li * xi + bux[blk, pl.ds(r0, 8), 0:128]
                ni = lr * xi + li * xr + bux[blk, pl.ds(r0, 8), 128:256]
                bux[blk, pl.ds(r0, 8), 0:128] = nr
                bux[blk, pl.ds(r0, 8), 128:256] = ni
                new += [nr, ni]
            return tuple(new)

        c0 = []
        for blk in range(S5_BLOCKS):
            c0 += [carry[blk, :, 0:128], carry[blk, :, 128:256]]
        cn = lax.fori_loop(0, tc, step, tuple(c0), unroll=4)
        for blk in range(S5_BLOCKS):
            carry[blk, :, 0:128] = cn[2 * blk]
            carry[blk, :, 128:256] = cn[2 * blk + 1]
        for blk in range(S5_BLOCKS):
            _stage(yrows, jnp.dot(bux[blk].astype(BF16), rc_ref[blk], preferred_element_type=F32))
            sl = slice(blk * 256, (blk + 1) * 256)
            y2 = _gather_rows(yrows, tc) + d_ref[:, sl] * u[:, sl]
            y2_ref[:, sl] = y2
            ge_ref[:, sl] = _gelu(y2).astype(BF16)

    row = pl.BlockSpec((tc, D_MODEL), lambda i: (i, 0))
    vec = pl.BlockSpec((1, D_MODEL), lambda i: (0, 0))
    mat = pl.BlockSpec((S5_BLOCKS, 256, 256), lambda i: (0, 0, 0))
    lamspec = pl.BlockSpec((S5_BLOCKS, 8, 128), lambda i: (0, 0, 0))
    return pl.pallas_call(
        body, grid=(nc,),
        in_specs=[row, vec, vec, mat, mat, lamspec, lamspec],
        out_specs=[row, row, pl.BlockSpec((1, S5_BLOCKS, 8, 256), lambda i: (i, 0, 0, 0))],
        out_shape=[_sds((n_rows, D_MODEL), BF16), _sds((n_rows, D_MODEL), F32), _sds((nc, S5_BLOCKS, 8, 256), F32)],
        scratch_shapes=[pltpu.VMEM((S5_BLOCKS, 8 * tc, 256), F32), pltpu.VMEM((2, 8 * tc, 128), F32),
                        pltpu.VMEM((S5_BLOCKS, 8, 256), F32)],
        name="s5_fwd", compiler_params=_params(("arbitrary",)))(x, gain, d_skip, rb, rc, lam_r, lam_i)


def s5_bwd(x, gain, dy2, res, d_skip, cs, rb, rbt, rct, lam_r, lam_i):
    n_rows = x.shape[0]
    tc = min(S5_CHUNK, n_rows)
    nc = n_rows // tc

    def body(x_ref, g_ref, dy_ref, res_ref, d_ref, cs_ref, rb_ref, rbt_ref, rct_ref, lr_ref, li_ref,
             dx_ref, dd_ref, drb_ref, drc_ref, dlr_ref, dli_ref, dg_ref, tmp, du, lhsu, lhsd, xs, adj, acarry):
        i = pl.program_id(0)
        u = _rms_hat(x_ref[...])[0] * g_ref[...]

        @pl.when(i == 0)
        def _():
            acarry[...] = jnp.zeros_like(acarry)
            dd_ref[...] = jnp.zeros_like(dd_ref)
            drb_ref[...] = jnp.zeros_like(drb_ref)
            drc_ref[...] = jnp.zeros_like(drc_ref)
            dlr_ref[...] = jnp.zeros_like(dlr_ref)
            dli_ref[...] = jnp.zeros_like(dli_ref)
            dg_ref[...] = jnp.zeros_like(dg_ref)

        dd_ref[...] += jnp.sum(dy_ref[...] * u, axis=0, keepdims=True)
        mask = _row_mask(tc)
        for blk in range(S5_BLOCKS):
            sl = slice(blk * 256, (blk + 1) * 256)
            lhsu[blk] = _expand_rows(u[:, sl], mask)
            xs[blk] = jnp.dot(lhsu[blk], rb_ref[blk], preferred_element_type=F32)
            lhsd[blk] = _expand_rows(dy_ref[:, sl], mask)
            adj[blk] = jnp.dot(lhsd[blk], rct_ref[blk], preferred_element_type=F32)
        lam = [(lr_ref[blk], li_ref[blk]) for blk in range(S5_BLOCKS)]

        def fstep(t, c):
            r0 = pl.multiple_of(t * 8, 8)
            new = []
            for blk in range(S5_BLOCKS):
                xr, xi = c[2 * blk], c[2 * blk + 1]
                lr, li = lam[blk]
                nr = lr * xr - li * xi + xs[blk, pl.ds(r0, 8), 0:128]
                ni = lr * xi + li * xr + xs[blk, pl.ds(r0, 8), 128:256]
                xs[blk, pl.ds(r0, 8), 0:128] = nr
                xs[blk, pl.ds(r0, 8), 128:256] = ni
                new += [nr, ni]
            return tuple(new)

        c0 = []
        for blk in range(S5_BLOCKS):
            c0 += [cs_ref[0, blk, :, 0:128], cs_ref[0, blk, :, 128:256]]
        lax.fori_loop(0, tc, fstep, tuple(c0), unroll=4)

        def bstep(k, c):
            t = tc - 1 - k
            r0 = pl.multiple_of(t * 8, 8)
            rp = pl.multiple_of(jnp.maximum(t - 1, 0) * 8, 8)
            first = t == 0
            new_a, new_g = [], []
            for blk in range(S5_BLOCKS):
                ar, ai = c[0][2 * blk], c[0][2 * blk + 1]
                glr, gli = c[1][2 * blk], c[1][2 * blk + 1]
                lr, li = lam[blk]
                nr = lr * ar + li * ai + adj[blk, pl.ds(r0, 8), 0:128]
                ni = lr * ai - li * ar + adj[blk, pl.ds(r0, 8), 128:256]
                adj[blk, pl.ds(r0, 8), 0:128] = nr
                adj[blk, pl.ds(r0, 8), 128:256] = ni
                pr = jnp.where(first, cs_ref[0, blk, :, 0:128], xs[blk, pl.ds(rp, 8), 0:128])
                pi = jnp.where(first, cs_ref[0, blk, :, 128:256], xs[blk, pl.ds(rp, 8), 128:256])
                new_a += [nr, ni]
                new_g += [glr + nr * pr + ni * pi, gli + ni * pr - nr * pi]
            return tuple(new_a), tuple(new_g)

        a0, g0 = [], []
        for blk in range(S5_BLOCKS):
            a0 += [acarry[blk, :, 0:128], acarry[blk, :, 128:256]]
            g0 += [dlr_ref[blk], dli_ref[blk]]
        an, gn = lax.fori_loop(0, tc, bstep, (tuple(a0), tuple(g0)), unroll=2)
        for blk in range(S5_BLOCKS):
            acarry[blk, :, 0:128] = an[2 * blk]
            acarry[blk, :, 128:256] = an[2 * blk + 1]
            dlr_ref[blk] = gn[2 * blk]
            dli_ref[blk] = gn[2 * blk + 1]
        for blk in range(S5_BLOCKS):
            sl = slice(blk * 256, (blk + 1) * 256)
            ab = adj[blk].astype(BF16)
            _stage(tmp, jnp.dot(ab, rbt_ref[blk], preferred_element_type=F32))
            du[:, sl] = _gather_rows(tmp, tc) + d_ref[:, sl] * dy_ref[:, sl]
            drb_ref[blk] += lax.dot_general(lhsu[blk], ab, (((0,), (0,)), ((), ())), preferred_element_type=F32)
            drc_ref[blk] += lax.dot_general(lhsd[blk], xs[blk].astype(BF16), (((0,), (0,)), ((), ())),
                                            preferred_element_type=F32)
        xh, r = _rms_hat(x_ref[...])
        dg_ref[...] += jnp.sum(du[...] * xh, axis=0, keepdims=True)
        dxh = du[...] * g_ref[...]
        dx_ref[...] = r * (dxh - xh * jnp.mean(dxh * xh, axis=-1, keepdims=True)) + res_ref[...]

    rev = pl.BlockSpec((tc, D_MODEL), lambda i: (nc - 1 - i, 0))
    vec = pl.BlockSpec((1, D_MODEL), lambda i: (0, 0))
    mat = pl.BlockSpec((S5_BLOCKS, 256, 256), lambda i: (0, 0, 0))
    lamspec = pl.BlockSpec((S5_BLOCKS, 8, 128), lambda i: (0, 0, 0))
    big = pltpu.VMEM((S5_BLOCKS, 8 * tc, 256), F32)
    bigb = pltpu.VMEM((S5_BLOCKS, 8 * tc, 256), BF16)
    return pl.pallas_call(
        body, grid=(nc,),
        in_specs=[rev, vec, rev, rev, vec, pl.BlockSpec((1, S5_BLOCKS, 8, 256), lambda i: (nc - 1 - i, 0, 0, 0)),
                  mat, mat, mat, lamspec, lamspec],
        out_specs=[rev, vec, mat, mat, lamspec, lamspec, vec],
        out_shape=[_sds((n_rows, D_MODEL), F32), _sds((1, D_MODEL), F32), _sds((S5_BLOCKS, 256, 256), F32),
                   _sds((S5_BLOCKS, 256, 256), F32), _sds((S5_BLOCKS, 8, 128), F32), _sds((S5_BLOCKS, 8, 128), F32),
                   _sds((1, D_MODEL), F32)],
        scratch_shapes=[pltpu.VMEM((2, 8 * tc, 128), F32), pltpu.VMEM((tc, D_MODEL), F32), bigb, bigb, big, big,
                        pltpu.VMEM((S5_BLOCKS, 8, 256), F32)],
        name="s5_bwd", compiler_params=_params(("arbitrary",)))(
            x, gain, dy2, res, d_skip, cs, rb, rbt, rct, lam_r, lam_i)


def _s5_views(a_re, a_im, log_dt, b_re, b_im):
    return a_re[:, None, :], a_im[:, None, :], log_dt[:, None, None], jnp.swapaxes(b_re, 1, 2), jnp.swapaxes(b_im, 1, 2)


def _s5_factors(a_re, a_im, log_dt):
    lr, li, dt = jnp.minimum(a_re, LAMBDA_RE_MAX), a_im, jnp.exp(log_dt)
    mag, ang = jnp.exp(lr * dt), li * dt
    lbr, lbi = mag * jnp.cos(ang), mag * jnp.sin(ang)
    den = lr * lr + li * li
    fr, fi = ((lbr - 1.0) * lr + lbi * li) / den, (lbi * lr - (lbr - 1.0) * li) / den
    return lr, li, dt, lbr, lbi, fr, fi, den


def s5_prep(a_re, a_im, log_dt, b_re, b_im, c_re, c_im):
    def body(ar_ref, ai_ref, t_ref, br_ref, bi_ref, cr_ref, ci_ref, rb_ref, rbt_ref, rc_ref, rct_ref, lr_ref, li_ref):
        _, _, _, lbr, lbi, fr, fi, _ = _s5_factors(ar_ref[...], ai_ref[...], t_ref[...])
        lr_ref[...] = lbr
        li_ref[...] = lbi
        bre = fr * br_ref[...] - fi * bi_ref[...]
        bim = fr * bi_ref[...] + fi * br_ref[...]
        even = (lax.broadcasted_iota(jnp.int32, (256, S5_STATE), 0) // S5_GROUP) % 2 == 0

        def assemble(re, im):
            re, im = re.reshape(256, S5_STATE), im.reshape(256, S5_STATE)
            return jnp.concatenate([jnp.where(even, re, 0.0), jnp.where(even, 0.0, re), jnp.where(even, im, 0.0),
                                    jnp.where(even, 0.0, im)], axis=1)

        for blk in range(S5_BLOCKS):
            sl = slice(16 * blk, 16 * blk + 16)
            rb = assemble(bre[sl], bim[sl])
            rct = assemble(cr_ref[sl], -ci_ref[sl])
            rb_ref[blk] = rb.astype(BF16)
            rbt_ref[blk] = rb.T.astype(BF16)
            rct_ref[blk] = rct.astype(BF16)
            rc_ref[blk] = rct.T.astype(BF16)

    vm = pl.BlockSpec(memory_space=pltpu.VMEM)
    mat = _sds((S5_BLOCKS, 256, 256), BF16)
    lam = _sds((S5_GROUPS, 1, S5_STATE), F32)
    rb, rbt, rc, rct, lam_r, lam_i = pl.pallas_call(
        body, in_specs=[vm] * 7, out_specs=[vm] * 6, out_shape=[mat, mat, mat, mat, lam, lam], name="s5_prep",
        compiler_params=_params())(*_s5_views(a_re, a_im, log_dt, b_re, b_im), c_re, c_im)
    return rb, rbt, rc, rct, lam_r.reshape(S5_BLOCKS, 8, 128), lam_i.reshape(S5_BLOCKS, 8, 128)


def s5_param_bwd(mats, lams, a_re, a_im, log_dt, b_re, b_im):
    def body(m_ref, glr_ref, gli_ref, ar_ref, ai_ref, t_ref, br_ref, bi_ref,
             dar_ref, dai_ref, dt_ref, dbr_ref, dbi_ref, dcr_ref, dci_ref):
        lr, li, dt, lbr, lbi, fr, fi, den = _s5_factors(ar_ref[...], ai_ref[...], t_ref[...])
        shape = (S5_GROUPS, S5_GROUP, S5_STATE)
        gbr, gbi = m_ref[0:1024, 0:64].reshape(shape), m_ref[0:1024, 64:128].reshape(shape)
        dcr_ref[...] = m_ref[1024:2048, 0:64].reshape(shape)
        dci_ref[...] = -m_ref[1024:2048, 64:128].reshape(shape)
        br, bi = br_ref[...], bi_ref[...]
        dbr_ref[...] = fr * gbr + fi * gbi
        dbi_ref[...] = fr * gbi - fi * gbr
        dfr = jnp.sum(gbr * br + gbi * bi, axis=1, keepdims=True)
        dfi = jnp.sum(gbi * br - gbr * bi, axis=1, keepdims=True)
        nr, ni = (dfr * lr - dfi * li) / den, (dfr * li + dfi * lr) / den
        qr, qi = (fr * lr + fi * li) / den, (fi * lr - fr * li) / den
        lam_r, lam_i = -(dfr * qr + dfi * qi), -(dfi * qr - dfr * qi)
        gr, gi = glr_ref[...] + nr, gli_ref[...] + ni
        zr, zi = gr * lbr + gi * lbi, gi * lbr - gr * lbi
        a = ar_ref[...]
        dar_ref[...] = (lam_r + zr * dt) * jnp.where(a < LAMBDA_RE_MAX, 1.0, jnp.where(a == LAMBDA_RE_MAX, 0.5, 0.0))
        dai_ref[...] = lam_i + zi * dt
        dt_ref[...] = jnp.sum(zr * lr + zi * li, axis=2, keepdims=True) * dt

    vm = pl.BlockSpec(memory_space=pltpu.VMEM)
    state = _sds((S5_GROUPS, 1, S5_STATE), F32)
    wide = _sds((S5_GROUPS, S5_GROUP, S5_STATE), F32)
    glr = lams[0:32].reshape(S5_GROUPS, 1, S5_STATE)
    gli = lams[32:64].reshape(S5_GROUPS, 1, S5_STATE)
    dar, dai, ddt, dbr, dbi, dcr, dci = pl.pallas_call(
        body, in_specs=[vm] * 8, out_specs=[vm] * 7,
        out_shape=[state, state, _sds((S5_GROUPS, 1, 1), F32), wide, wide, wide, wide], name="s5_param_bwd",
        compiler_params=_params())(mats, glr, gli, *_s5_views(a_re, a_im, log_dt, b_re, b_im))
    return (dar.reshape(S5_GROUPS, S5_STATE), dai.reshape(S5_GROUPS, S5_STATE), ddt.reshape(S5_GROUPS),
            jnp.swapaxes(dbr, 1, 2), jnp.swapaxes(dbi, 1, 2), dcr, dci)


def s5_compact(drb, drct, dlr, dli):
    def body(drb_ref, drct_ref, dlr_ref, dli_ref, o_ref, lam_ref):
        even = (lax.broadcasted_iota(jnp.int32, (256, 64), 0) // S5_GROUP) % 2 == 0
        for blk in range(S5_BLOCKS):
            for k, ref in enumerate((drb_ref, drct_ref)):
                m = ref[blk]
                re = jnp.where(even, m[:, 0:64], m[:, 64:128])
                im = jnp.where(even, m[:, 128:192], m[:, 192:256])
                o_ref[pl.ds(k * 1024 + blk * 256, 256), :] = jnp.concatenate([re, im], axis=1)
            lam_ref[pl.ds(blk * 8, 8), :] = dlr_ref[blk]
            lam_ref[pl.ds(32 + blk * 8, 8), :] = dli_ref[blk]

    vm = pl.BlockSpec(memory_space=pltpu.VMEM)
    return pl.pallas_call(body, in_specs=[vm] * 4, out_specs=[vm, vm], out_shape=[_sds((2048, 128), F32), _sds((64, 128), F32)],
                          name="s5_compact", compiler_params=_params())(drb, drct, dlr, dli)


NEG = -1e30


GROUP = N_Q // N_KV


def _attn_masks(n):
    qi = lax.broadcasted_iota(jnp.int32, (GROUP * BLOCK, BLOCK), 0) % BLOCK
    kj = lax.broadcasted_iota(jnp.int32, (GROUP * BLOCK, BLOCK), 1)
    return jnp.logical_and(kj > qi, n > 0), kj <= qi


def _stack_heads(ref, kh):
    return jnp.concatenate([ref[:, (GROUP * kh + g) * HEAD_DIM:(GROUP * kh + g + 1) * HEAD_DIM] for g in range(GROUP)], axis=0)


def _unstack_heads(val):
    return jnp.concatenate([val[g * BLOCK:(g + 1) * BLOCK] for g in range(GROUP)], axis=1)


def _sink_column(sink_ref, kh):
    grp = lax.broadcasted_iota(jnp.int32, (GROUP * BLOCK, 1), 0) // BLOCK
    col = jnp.zeros((GROUP * BLOCK, 1), F32)
    for g in range(GROUP):
        col = jnp.where(grp == g, sink_ref[GROUP * kh + g], col)
    return col, grp


def _attn_exp(q4, kp, kc, sink, mask_p, mask_c):
    scale = 1.0 / math.sqrt(HEAD_DIM)
    nt = (((1,), (1,)), ((), ()))
    sp = jnp.where(mask_p, lax.dot_general(q4, kp, nt, preferred_element_type=F32) * scale, NEG)
    sc = jnp.where(mask_c, lax.dot_general(q4, kc, nt, preferred_element_type=F32) * scale, NEG)
    m = jnp.maximum(jnp.maximum(jnp.max(sp, axis=-1, keepdims=True), jnp.max(sc, axis=-1, keepdims=True)), sink)
    pp = jnp.exp(sp - m)
    pc = jnp.exp(sc - m)
    ps = jnp.exp(sink - m)
    inv = 1.0 / (jnp.sum(pp, axis=-1, keepdims=True) + jnp.sum(pc, axis=-1, keepdims=True) + ps)
    return pp, pc, ps, inv


def attn_fwd(q, kv, sinks):
    n_rows = q.shape[0]
    nb = n_rows // BLOCK

    def body(sink_ref, q_ref, kvp_ref, kvc_ref, o_ref):
        n = pl.program_id(0)
        mask_p, mask_c = _attn_masks(n)
        outs = []
        for kh in range(N_KV):
            ks, vs = slice(kh * HEAD_DIM, (kh + 1) * HEAD_DIM), slice((N_KV + kh) * HEAD_DIM, (N_KV + kh + 1) * HEAD_DIM)
            sink, _ = _sink_column(sink_ref, kh)
            pp, pc, _, inv = _attn_exp(_stack_heads(q_ref, kh), kvp_ref[:, ks], kvc_ref[:, ks], sink, mask_p, mask_c)
            o4 = (jnp.dot(pp.astype(BF16), kvp_ref[:, vs], preferred_element_type=F32)
                  + jnp.dot(pc.astype(BF16), kvc_ref[:, vs], preferred_element_type=F32)) * inv
            outs.append(_unstack_heads(o4))
        o_ref[...] = jnp.concatenate(outs, axis=1).astype(BF16)

    kvw = 2 * N_KV * HEAD_DIM
    return pl.pallas_call(
        body, grid=(nb,),
        in_specs=[pl.BlockSpec(memory_space=pltpu.SMEM), pl.BlockSpec((BLOCK, D_MODEL), lambda n: (n, 0)),
                  pl.BlockSpec((BLOCK, kvw), lambda n: (jnp.maximum(n - 1, 0), 0)), pl.BlockSpec((BLOCK, kvw), lambda n: (n, 0))],
        out_specs=pl.BlockSpec((BLOCK, D_MODEL), lambda n: (n, 0)), out_shape=_sds((n_rows, D_MODEL), BF16),
        name="attn_fwd", compiler_params=_params(("parallel",)))(sinks, q, kv, kv)


def attn_bwd(q, kv, do, sinks):
    n_rows = q.shape[0]
    nb = n_rows // BLOCK
    kvw = 2 * N_KV * HEAD_DIM
    tn = (((0,), (0,)), ((), ()))
    nt = (((1,), (1,)), ((), ()))
    scale = 1.0 / math.sqrt(HEAD_DIM)

    def body(sink_ref, q_ref, kvp_ref, kvc_ref, do_ref, dq_ref, dbq_ref, dprev_ref, dcur_ref, dsink_ref):
        n = pl.program_id(0)
        mask_p, mask_c = _attn_masks(n)
        lane = lax.broadcasted_iota(jnp.int32, (1, D_MODEL), 1)
        dqs, dsink = [], jnp.zeros((1, D_MODEL), F32)
        dkp, dkc, dvp, dvc = [], [], [], []
        for kh in range(N_KV):
            ks, vs = slice(kh * HEAD_DIM, (kh + 1) * HEAD_DIM), slice((N_KV + kh) * HEAD_DIM, (N_KV + kh + 1) * HEAD_DIM)
            q4, do4 = _stack_heads(q_ref, kh), _stack_heads(do_ref, kh)
            kp, kc, vp, vc = kvp_ref[:, ks], kvc_ref[:, ks], kvp_ref[:, vs], kvc_ref[:, vs]
            sink, grp = _sink_column(sink_ref, kh)
            pp, pc, ps, inv = _attn_exp(q4, kp, kc, sink, mask_p, mask_c)
            pp, pc = pp * inv, pc * inv
            dpp = lax.dot_general(do4, vp, nt, preferred_element_type=F32)
            dpc = lax.dot_general(do4, vc, nt, preferred_element_type=F32)
            delta = jnp.sum(pp * dpp, axis=-1, keepdims=True) + jnp.sum(pc * dpc, axis=-1, keepdims=True)
            dsp = (pp * (dpp - delta) * scale).astype(BF16)
            dsc = (pc * (dpc - delta) * scale).astype(BF16)
            dsk = ps * inv * delta
            for g in range(GROUP):
                dsink = dsink + jnp.where(lane == GROUP * kh + g, -jnp.sum(jnp.where(grp == g, dsk, 0.0)), 0.0)
            dqs.append(_unstack_heads(jnp.dot(dsp, kp, preferred_element_type=F32)
                                      + jnp.dot(dsc, kc, preferred_element_type=F32)))
            dkp.append(lax.dot_general(dsp, q4, tn, preferred_element_type=F32))
            dkc.append(lax.dot_general(dsc, q4, tn, preferred_element_type=F32))
            dvp.append(lax.dot_general(pp.astype(BF16), do4, tn, preferred_element_type=F32))
            dvc.append(lax.dot_general(pc.astype(BF16), do4, tn, preferred_element_type=F32))
        dq = jnp.concatenate(dqs, axis=1)
        dq_ref[...] = dq.astype(BF16)
        dprev_ref[0] = jnp.concatenate(dkp + dvp, axis=1)
        dcur_ref[0] = jnp.concatenate(dkc + dvc, axis=1)

        @pl.when(n == 0)
        def _():
            dbq_ref[...] = jnp.zeros_like(dbq_ref)
            dsink_ref[...] = jnp.zeros_like(dsink_ref)

        dbq_ref[...] += jnp.sum(dq, axis=0, keepdims=True)
        dsink_ref[...] += dsink

    blk = pl.BlockSpec((BLOCK, D_MODEL), lambda n: (n, 0))
    part = pl.BlockSpec((1, BLOCK, kvw), lambda n: (n, 0, 0))
    return pl.pallas_call(
        body, grid=(nb,),
        in_specs=[pl.BlockSpec(memory_space=pltpu.SMEM), blk,
                  pl.BlockSpec((BLOCK, kvw), lambda n: (jnp.maximum(n - 1, 0), 0)), pl.BlockSpec((BLOCK, kvw), lambda n: (n, 0)), blk],
        out_specs=[blk, pl.BlockSpec((1, D_MODEL), lambda n: (0, 0)), part, part, pl.BlockSpec((1, D_MODEL), lambda n: (0, 0))],
        out_shape=[_sds((n_rows, D_MODEL), BF16), _sds((1, D_MODEL), F32), _sds((nb, BLOCK, kvw), F32),
                   _sds((nb, BLOCK, kvw), F32), _sds((1, D_MODEL), F32)],
        name="attn_bwd", compiler_params=_params(("arbitrary",)))(sinks, q, kv, kv, do)


def kv_combine(dprev, dcur):
    nb, _, kvw = dprev.shape

    def body(dcur_ref, dprev_ref, dkv_ref, db_ref):
        total = jnp.zeros((1, kvw), F32)
        for m in range(nb):
            dkv = dcur_ref[m] + dprev_ref[m + 1] if m + 1 < nb else dcur_ref[m]
            dkv_ref[m * BLOCK:(m + 1) * BLOCK, :] = dkv.astype(BF16)
            total = total + jnp.sum(dkv, axis=0, keepdims=True)
        db_ref[...] = jnp.concatenate([total, jnp.zeros((1, D_MODEL - kvw), F32)], axis=1)

    vm = pl.BlockSpec(memory_space=pltpu.VMEM)
    return pl.pallas_call(body, in_specs=[vm, vm], out_specs=[vm, vm],
                          out_shape=[_sds((nb * BLOCK, kvw), BF16), _sds((1, D_MODEL), F32)], name="kv_combine",
                          compiler_params=_params())(dcur, dprev)


def glu_bwd(dout, val, gate, tm=256):
    n_rows, d = dout.shape

    def body(do_ref, v_ref, g_ref, dz_ref, db_ref):
        i = pl.program_id(0)
        sg = jax.nn.sigmoid(g_ref[...])
        dval = do_ref[...] * sg
        dgate = do_ref[...] * v_ref[...] * sg * (1.0 - sg)
        dz_ref[...] = jnp.concatenate([dval, dgate], axis=1).astype(BF16)

        @pl.when(i == 0)
        def _():
            db_ref[...] = jnp.zeros_like(db_ref)

        db_ref[0:1, :] += jnp.sum(dval, axis=0, keepdims=True)
        db_ref[1:2, :] += jnp.sum(dgate, axis=0, keepdims=True)

    row = pl.BlockSpec((tm, d), lambda i: (i, 0))
    return pl.pallas_call(
        body, grid=(n_rows // tm,), in_specs=[row, row, row],
        out_specs=[pl.BlockSpec((tm, 2 * d), lambda i: (i, 0)), pl.BlockSpec((2, d), lambda i: (0, 0))],
        out_shape=[_sds((n_rows, 2 * d), BF16), _sds((2, d), F32)],
        name="glu_bwd", compiler_params=_params(("arbitrary",)))(dout, val, gate)


def _adam_update(w, g, m, v):
    nm = ADAM_B1 * m + (1.0 - ADAM_B1) * g
    nv = ADAM_B2 * v + (1.0 - ADAM_B2) * (g * g)
    m_hat = nm / (1.0 - ADAM_B1 ** ADAM_STEP)
    v_hat = nv / (1.0 - ADAM_B2 ** ADAM_STEP)
    return -ADAM_LR * (m_hat / (jnp.sqrt(v_hat) + ADAM_EPS) + ADAM_WD * w), nm, nv


def adamw(name, ws, gs, ms, vs, steps=8):
    n = len(ws)

    def body(*refs):
        for k in range(n):
            w_ref, g_ref, m_ref, v_ref = (refs[j * n + k] for j in range(4))
            go_ref, d_ref, nm_ref, nv_ref = (refs[(4 + j) * n + k] for j in range(4))
            gv = g_ref[...]
            go_ref[...] = gv
            d_ref[...], nm_ref[...], nv_ref[...] = _adam_update(w_ref[...], gv, m_ref[...], v_ref[...])

    specs = [pl.BlockSpec((w.shape[0] // steps, w.shape[1]), lambda i: (i, 0)) for w in ws]
    shapes = [_sds(w.shape, F32) for w in ws]
    out = pl.pallas_call(
        body, grid=(steps,), in_specs=specs * 4, out_specs=specs * 4, out_shape=shapes * 4, name=name,
        compiler_params=_params(("parallel",)))(*ws, *gs, *ms, *vs)
    return [list(out[j * n:(j + 1) * n]) for j in range(4)]


def adamw_native(name, ws, gs, ms, vs):
    n = len(ws)

    def body(*refs):
        w_refs, g_refs, m_refs, v_refs = refs[:n], refs[n:2 * n], refs[2 * n:3 * n], refs[3 * n:4 * n]
        d_refs, nm_refs, nv_refs = refs[4 * n:5 * n], refs[5 * n:6 * n], refs[6 * n:7 * n]
        for k in range(n):
            dl, nm, nv = _adam_update(w_refs[k][...], g_refs[k][...], m_refs[k][...], v_refs[k][...])
            d_refs[k][...] = dl
            nm_refs[k][...] = nm
            nv_refs[k][...] = nv

    vm = pl.BlockSpec(memory_space=pltpu.VMEM)
    shapes = [_sds(w.shape, F32) for w in ws]
    out = pl.pallas_call(body, in_specs=[vm] * (4 * n), out_specs=[vm] * (3 * n), out_shape=shapes * 3, name=name,
                         compiler_params=_params())(*ws, *gs, *ms, *vs)
    return list(out[:n]), list(out[n:2 * n]), list(out[2 * n:])


VEC_ROWS = {"norm_mix": 0, "norm_mlp": 2, "norm_kv": 4, "norm_final": 5, "s5_d": 6, "b_q": 7, "b_o": 8, "s5_b_glu": 9,
            "b_kv": 11, "sinks": 12, "loss": 13}


def split_vectors(where, vecs, d_shard, glu_shard):
    kvw = 2 * N_KV * HEAD_DIM
    shapes = {"norm_mix": (2, D_MODEL), "norm_mlp": (2, D_MODEL), "norm_kv": (1, D_MODEL), "norm_final": (1, D_MODEL),
              "s5_d": (1, d_shard), "b_q": (1, D_MODEL), "b_o": (1, D_MODEL), "s5_b_glu": (1, glu_shard), "b_kv": (1, kvw),
              "sinks": (1, N_Q), "loss": (1, 128)}
    names = list(shapes)

    def body(where_ref, v_ref, *o_refs):
        chip = where_ref[1]
        for name, o_ref in zip(names, o_refs):
            r0, (r, n) = VEC_ROWS[name], shapes[name]
            if name == "s5_d":
                g = jnp.zeros((1, n), F32)
                for j in range(4):
                    g = jnp.where(chip == j, v_ref[r0:r0 + 1, j * n:(j + 1) * n], g)
            elif name == "s5_b_glu":
                g = jnp.zeros((1, n), F32)
                for j in range(4):
                    row, col = r0 + (j * n) // D_MODEL, (j * n) % D_MODEL
                    g = jnp.where(chip == j, v_ref[row:row + 1, col:col + n], g)
            else:
                g = v_ref[r0:r0 + r, 0:n]
            o_ref[...] = g

    vm = pl.BlockSpec(memory_space=pltpu.VMEM)
    out = pl.pallas_call(body, in_specs=[pl.BlockSpec(memory_space=pltpu.SMEM), vm], out_specs=[vm] * len(names),
                         out_shape=[_sds(shapes[n], F32) for n in names], name="split_vectors",
                         compiler_params=_params())(where, vecs)
    return dict(zip(names, out))


def _position():
    x, y, c = lax.axis_index("x"), lax.axis_index("y"), lax.axis_index("c")
    others = [(1 - x, y), (x, 1 - y), (1 - x, 1 - y)]
    return x, y, c, others


def _window(ref, kind, chip, half, shard_shape):
    if kind == "slab":
        return ref.at[chip]
    r, n = shard_shape
    if kind == "col":
        return ref.at[pl.ds(pl.multiple_of(half * (r // 2), 16), r // 2), pl.ds(pl.multiple_of(chip * n, 128), n)]
    return ref.at[pl.ds(pl.multiple_of(chip * r, 16), r), pl.ds(pl.multiple_of(half * (n // 2), 128), n // 2)]


def _half(ref, kind, half, shape):
    r, n = shape
    if kind == "col":
        return ref.at[pl.ds(pl.multiple_of(half * (r // 2), 16), r // 2), :]
    return ref.at[:, pl.ds(pl.multiple_of(half * (n // 2), 128), n // 2)]


def swap_start(name, grads, kinds, carry):
    nt = len(grads)
    shapes = [tuple(g.shape) for g in grads]
    lands = [lax.empty(sh, BF16) for sh in shapes]
    given, given_specs, token_type, write = _hand_through(carry)
    n_in = 2 * nt + len(given)

    def body(*refs):
        in_refs, land_refs = refs[:nt], refs[nt:2 * nt]
        send_sems, recv_sems, token = refs[n_in], refs[n_in + 1], refs[-1]
        x, y, c, _ = _position()
        for t in range(nt):
            pltpu.make_async_remote_copy(
                src_ref=_half(in_refs[t], kinds[t], 1 - c, shapes[t]), dst_ref=_half(land_refs[t], kinds[t], 1 - c, shapes[t]),
                send_sem=send_sems.at[t], recv_sem=recv_sems.at[t], device_id=(x, y, 1 - c), device_id_type=MESH).start()
        write(token, refs[:n_in])

    sems = pltpu.SemaphoreType.DMA((nt,))
    both = list(grads) + lands
    out = pl.pallas_call(
        body, name=name, in_specs=[HBM_SPEC] * (2 * nt) + given_specs,
        out_specs=(SEM_SPEC, SEM_SPEC, *[HBM_SPEC] * (2 * nt), pl.BlockSpec(memory_space=pltpu.VMEM)),
        out_shape=(sems, sems, *[pltpu.HBM(a.shape, a.dtype) for a in both], token_type),
        input_output_aliases={t: 2 + t for t in range(2 * nt)}, compiler_params=_split_params(),
    )(*[_in_hbm(a) for a in both], *given)
    return out[0], out[1], list(out[2:2 + nt]), list(out[2 + nt:2 + 2 * nt]), out[-1]


def swap_wait(name, send_sems, recv_sems, grads, lands, kinds, after):
    nt = len(grads)
    shapes = [tuple(g.shape) for g in grads]

    def body(*refs):
        in_refs, land_refs = refs[:nt], refs[nt:2 * nt]
        send_ref, recv_ref = refs[2 * nt], refs[2 * nt + 1]
        x, y, c, _ = _position()
        for t in range(nt):
            cp = pltpu.make_async_remote_copy(
                src_ref=_half(in_refs[t], kinds[t], 1 - c, shapes[t]), dst_ref=_half(land_refs[t], kinds[t], c, shapes[t]),
                send_sem=send_ref.at[t], recv_sem=recv_ref.at[t], device_id=(x, y, 1 - c), device_id_type=MESH)
            cp.wait_send()
            cp.wait_recv()

    both = list(grads) + list(lands)
    out = pl.pallas_call(
        body, name=name, in_specs=[HBM_SPEC] * (2 * nt) + [SEM_SPEC, SEM_SPEC, HBM_SPEC], out_specs=[HBM_SPEC] * (2 * nt),
        out_shape=[pltpu.HBM(a.shape, a.dtype) for a in both], input_output_aliases={t: t for t in range(2 * nt)},
        compiler_params=_split_params())(*both, send_sems, recv_sems, _in_hbm(after))
    return list(out[:nt]), list(out[nt:])


def _half_spec(kind, shape, tiles):
    r, n = shape
    if kind == "col":
        tn = n // tiles
        return pl.BlockSpec((r // 2, tn), lambda i, s: (s[0], i))
    tm = r // tiles
    return pl.BlockSpec((tm, n // 2), lambda i, s: (i, s[0]))


def add_halves(name, mine, landed, kinds, where, tiles=4):
    nt = len(mine)
    shapes = [tuple(a.shape) for a in mine]

    def compact(t):
        r, n = shapes[t]
        if kinds[t] == "col":
            return (r // 2, n), pl.BlockSpec((r // 2, n // tiles), lambda i, s: (0, i))
        return (r, n // 2), pl.BlockSpec((r // tiles, n // 2), lambda i, s: (i, 0))

    def body(s_ref, *refs):
        for a_ref, b_ref, o_ref in zip(refs[:nt], refs[nt:2 * nt], refs[2 * nt:]):
            o_ref[...] = (a_ref[...].astype(F32) + b_ref[...].astype(F32)).astype(BF16)

    specs = [_half_spec(kinds[t], shapes[t], tiles) for t in range(nt)]
    return pl.pallas_call(
        body, grid_spec=pltpu.PrefetchScalarGridSpec(num_scalar_prefetch=1, grid=(tiles,), in_specs=specs + specs,
                                                     out_specs=[compact(t)[1] for t in range(nt)]),
        out_shape=[_sds(compact(t)[0], BF16) for t in range(nt)], name=name,
        compiler_params=_params(("parallel",)))(where, *mine, *landed)


def sum_shards(name, parts, landed, kinds, shard_shapes, where, layers, n_layers, intos, tiles=2):
    nt = len(parts)
    in_specs, out_specs = [], []
    for t in range(nt):
        (r, n), layer = shard_shapes[t], layers[t]
        if kinds[t] == "col":
            tm, width = r // 2 // tiles, n
            own = pl.BlockSpec((tm, n), lambda i, s: (i, s[1]))
            out = pl.BlockSpec((None, tm, n), lambda i, s, layer=layer: (layer, s[0] * tiles + i, 0))
        else:
            tm, width = r // tiles, n // 2
            own = pl.BlockSpec((tm, n // 2), lambda i, s: (s[1] * tiles + i, 0))
            out = pl.BlockSpec((None, tm, n // 2), lambda i, s, layer=layer: (layer, i, s[0]))
        in_specs += [own, pl.BlockSpec((3, tm, width), lambda i, s: (0, i, 0))]
        out_specs.append(out)
    args, aliases = [where] + [a for pair in zip(parts, landed) for a in pair], {}
    for t in range(nt):
        if intos[t] is not None:
            aliases[len(args)] = t
            in_specs.append(pl.BlockSpec(memory_space=pl.ANY))
            args.append(intos[t])

    def body(s_ref, *refs):
        for t in range(nt):
            a_ref, l_ref, o_ref = refs[2 * t], refs[2 * t + 1], refs[len(in_specs) + t]
            o_ref[...] = ((a_ref[...].astype(F32) + l_ref[0].astype(F32)) + l_ref[1].astype(F32)) + l_ref[2].astype(F32)

    return pl.pallas_call(
        body, grid_spec=pltpu.PrefetchScalarGridSpec(num_scalar_prefetch=1, grid=(tiles,), in_specs=in_specs,
                                                     out_specs=out_specs),
        out_shape=[_sds((n_layers[t],) + tuple(shard_shapes[t]), F32) for t in range(nt)], input_output_aliases=aliases,
        name=name, compiler_params=_params(("parallel",)))(*args)


def share_start(arrays, entries, carry):
    na, nt = len(arrays), len(entries)
    given, given_specs, token_type, write = _hand_through(carry)
    n_in = na + len(given)

    def body(*refs):
        in_refs, send_sems, recv_sems, token = refs[:na], refs[n_in], refs[n_in + 1], refs[-1]
        x, y, c, _ = _position()
        for t, (a, layer, kind) in enumerate(entries):
            mine = _half(in_refs[a].at[layer], kind, c, tuple(arrays[a].shape[1:]))
            pltpu.make_async_remote_copy(
                src_ref=mine, dst_ref=mine, send_sem=send_sems.at[t], recv_sem=recv_sems.at[t],
                device_id=(x, y, 1 - c), device_id_type=MESH).start()
        write(token, refs[:n_in])

    sems = pltpu.SemaphoreType.DMA((nt,))
    out = pl.pallas_call(
        body, name="share_start", in_specs=[HBM_SPEC] * na + given_specs,
        out_specs=(SEM_SPEC, SEM_SPEC, *[HBM_SPEC] * na, pl.BlockSpec(memory_space=pltpu.VMEM)),
        out_shape=(sems, sems, *[pltpu.HBM(a.shape, a.dtype) for a in arrays], token_type),
        input_output_aliases={t: 2 + t for t in range(na)}, compiler_params=_split_params(),
    )(*[_in_hbm(a) for a in arrays], *given)
    return out[0], out[1], list(out[2:2 + na]), out[-1]


def share_wait(send_sems, recv_sems, arrays, entries, after):
    na = len(arrays)

    def body(*refs):
        in_refs, send_ref, recv_ref = refs[:na], refs[na], refs[na + 1]
        x, y, c, _ = _position()
        for t, (a, layer, kind) in enumerate(entries):
            shape = tuple(arrays[a].shape[1:])
            cp = pltpu.make_async_remote_copy(
                src_ref=_half(in_refs[a].at[layer], kind, c, shape), dst_ref=_half(in_refs[a].at[layer], kind, 1 - c, shape),
                send_sem=send_ref.at[t], recv_sem=recv_ref.at[t], device_id=(x, y, 1 - c), device_id_type=MESH)
            cp.wait_send()
            cp.wait_recv()

    return list(pl.pallas_call(
        body, name="share_wait", in_specs=[HBM_SPEC] * na + [SEM_SPEC, SEM_SPEC, HBM_SPEC], out_specs=[HBM_SPEC] * na,
        out_shape=[pltpu.HBM(a.shape, a.dtype) for a in arrays], input_output_aliases={t: t for t in range(na)},
        compiler_params=_split_params())(*arrays, send_sems, recv_sems, _in_hbm(after)))


HBM_SPEC = pl.BlockSpec(memory_space=pltpu.HBM)
SEM_SPEC = pl.BlockSpec(memory_space=pltpu.SEMAPHORE)
ANY_SPEC = pl.BlockSpec(memory_space=pl.ANY)


def _split_params():
    return pltpu.CompilerParams(has_side_effects=pltpu.SideEffectType.DATAFLOW_SIDE_EFFECTING,
                                vmem_limit_bytes=VMEM_LIMIT_BYTES)


def _in_hbm(a):
    return pltpu.with_memory_space_constraint(a, pltpu.HBM)


def cast_place(arrays, entries, where, tiles=2):
    in_specs, out_specs, fulls = [], [], []
    for a, layer, kind in entries:
        _, r, n = arrays[a].shape
        tm = r // tiles
        in_specs.append(pl.BlockSpec((None, tm, n), lambda i, s, layer=layer: (layer, i, 0)))
        if kind == "col":
            fulls.append((r, 4 * n))
            out_specs.append(pl.BlockSpec((tm, n), lambda i, s: (i, s[1])))
        else:
            fulls.append((4 * r, n))
            out_specs.append(pl.BlockSpec((tm, n), lambda i, s: (s[1] * tiles + i, 0)))
    nt = len(entries)

    def body(s_ref, *refs):
        for w_ref, o_ref in zip(refs[:nt], refs[nt:]):
            o_ref[...] = w_ref[...].astype(BF16)

    return pl.pallas_call(
        body, grid_spec=pltpu.PrefetchScalarGridSpec(num_scalar_prefetch=1, grid=(tiles,), in_specs=in_specs,
                                                     out_specs=out_specs),
        out_shape=[_sds(f, BF16) for f in fulls], name="cast_place",
        compiler_params=_params(("parallel",)))(where, *[arrays[a] for a, _, _ in entries])


def _hand_through(carry):
    given = [] if isinstance(carry, tuple) else [carry]

    def write(token, ins):
        token[...] = ins[-1][...] if given else jnp.zeros_like(token)

    return (given, [pl.BlockSpec(memory_space=pltpu.VMEM)] * len(given),
            _sds(carry if isinstance(carry, tuple) else carry.shape, F32), write)


def gather_start(fulls, kinds, shard_shapes, carry):
    nt = len(fulls)
    given, given_specs, token_type, write = _hand_through(carry)
    n_in = nt + len(given)

    def body(*refs):
        full_refs = refs[:nt]
        send_sems, recv_sems, token = refs[n_in], refs[n_in + 1], refs[-1]
        x, y, c, others = _position()
        for t in range(nt):
            mine = _window(full_refs[t], kinds[t], 2 * x + y, c, shard_shapes[t])
            for j, (ox, oy) in enumerate(others):
                pltpu.make_async_remote_copy(
                    src_ref=mine, dst_ref=mine, send_sem=send_sems.at[3 * t + j], recv_sem=recv_sems.at[3 * t + j],
                    device_id=(ox, oy, c), device_id_type=MESH).start()
        write(token, refs[:n_in])

    sems = pltpu.SemaphoreType.DMA((3 * nt,))
    out = pl.pallas_call(
        body, name="gather_start", in_specs=[HBM_SPEC] * nt + given_specs,
        out_specs=(SEM_SPEC, SEM_SPEC, *[HBM_SPEC] * nt, pl.BlockSpec(memory_space=pltpu.VMEM)),
        out_shape=(sems, sems, *[pltpu.HBM(f.shape, f.dtype) for f in fulls], token_type),
        input_output_aliases={t: 2 + t for t in range(nt)}, compiler_params=_split_params(),
    )(*[_in_hbm(f) for f in fulls], *given)
    return out[0], out[1], list(out[2:2 + nt]), out[-1]


def gather_wait(name, send_sems, recv_sems, fulls, kinds, shard_shapes, after, first):
    nt = len(fulls)
    extra = [] if after is None else [_in_hbm(after)]

    def body(*refs):
        full_refs, send_ref, recv_ref = refs[:nt], refs[nt], refs[nt + 1]
        x, y, c, others = _position()
        for t in range(nt):
            mine = _window(full_refs[t], kinds[t], 2 * x + y, c, shard_shapes[t])
            for j, (ox, oy) in enumerate(others):
                cp = pltpu.make_async_remote_copy(
                    src_ref=mine, dst_ref=_window(full_refs[t], kinds[t], 2 * ox + oy, c, shard_shapes[t]),
                    send_sem=send_ref.at[3 * (first + t) + j], recv_sem=recv_ref.at[3 * (first + t) + j],
                    device_id=(ox, oy, c), device_id_type=MESH)
                cp.wait_send()
                cp.wait_recv()

    out = pl.pallas_call(
        body, name=name, in_specs=[HBM_SPEC] * nt + [SEM_SPEC, SEM_SPEC] + [HBM_SPEC] * len(extra),
        out_specs=[HBM_SPEC] * nt, out_shape=[pltpu.HBM(f.shape, f.dtype) for f in fulls],
        input_output_aliases={t: t for t in range(nt)}, compiler_params=_split_params())(*fulls, send_sems, recv_sems, *extra)
    return list(out)


def forward_halves(name, fulls, kinds, shard_shapes):
    nt = len(fulls)

    def body(*refs):
        out_refs = refs[nt:2 * nt]
        send_sems, recv_sems = refs[2 * nt:]
        x, y, c, others = _position()
        cps = []
        for t in range(nt):
            for j, (ox, oy) in enumerate(others):
                landed = _window(out_refs[t], kinds[t], 2 * ox + oy, c, shard_shapes[t])
                cp = pltpu.make_async_remote_copy(
                    src_ref=landed, dst_ref=landed, send_sem=send_sems.at[3 * t + j], recv_sem=recv_sems.at[3 * t + j],
                    device_id=(x, y, 1 - c), device_id_type=MESH)
                cp.start()
                cps.append(cp)
        for t in range(nt):
            for j, (ox, oy) in enumerate(others):
                got = _window(out_refs[t], kinds[t], 2 * ox + oy, 1 - c, shard_shapes[t])
                pltpu.make_async_remote_copy(
                    src_ref=got, dst_ref=got, send_sem=send_sems.at[3 * t + j], recv_sem=recv_sems.at[3 * t + j],
                    device_id=(x, y, 1 - c), device_id_type=MESH).wait_recv()
        for cp in cps:
            cp.wait_send()

    out = pl.pallas_call(
        body, in_specs=[ANY_SPEC] * nt, out_specs=[ANY_SPEC] * nt, out_shape=[_sds(f.shape, f.dtype) for f in fulls],
        input_output_aliases={t: t for t in range(nt)},
        scratch_shapes=[pltpu.SemaphoreType.DMA((3 * nt,)), pltpu.SemaphoreType.DMA((3 * nt,))],
        name=name, compiler_params=_params())(*fulls)
    return list(out)


def forward_start(name, send_sems, recv_sems, fulls, kinds, shard_shapes, after, first, carry, passing=()):
    nt, n_pass = len(fulls), len(passing)
    given, given_specs, token_type, write = _hand_through(carry)
    n_in = nt + 3 + n_pass + len(given)

    def body(*refs):
        full_refs, ici_send, ici_recv = refs[:nt], refs[nt], refs[nt + 1]
        send_ref, recv_ref, token = refs[n_in], refs[n_in + 1], refs[-1]
        x, y, c, others = _position()
        for t in range(nt):
            mine = _window(full_refs[t], kinds[t], 2 * x + y, c, shard_shapes[t])
            for j, (ox, oy) in enumerate(others):
                landed = _window(full_refs[t], kinds[t], 2 * ox + oy, c, shard_shapes[t])
                cp = pltpu.make_async_remote_copy(
                    src_ref=mine, dst_ref=landed, send_sem=ici_send.at[3 * (first + t) + j],
                    recv_sem=ici_recv.at[3 * (first + t) + j], device_id=(ox, oy, c), device_id_type=MESH)
                cp.wait_send()
                cp.wait_recv()
                pltpu.make_async_remote_copy(
                    src_ref=landed, dst_ref=landed, send_sem=send_ref.at[3 * t + j], recv_sem=recv_ref.at[3 * t + j],
                    device_id=(x, y, 1 - c), device_id_type=MESH).start()
        write(token, refs[:n_in])

    sems = pltpu.SemaphoreType.DMA((3 * nt,))
    out = pl.pallas_call(
        body, name=name, in_specs=[HBM_SPEC] * nt + [SEM_SPEC, SEM_SPEC, HBM_SPEC] + [HBM_SPEC] * n_pass + given_specs,
        out_specs=(SEM_SPEC, SEM_SPEC, *[HBM_SPEC] * (nt + n_pass), pl.BlockSpec(memory_space=pltpu.VMEM)),
        out_shape=(sems, sems, *[pltpu.HBM(f.shape, f.dtype) for f in [*fulls, *passing]], token_type),
        input_output_aliases={**{t: 2 + t for t in range(nt)}, **{nt + 3 + t: 2 + nt + t for t in range(n_pass)}},
        compiler_params=_split_params(),
    )(*fulls, send_sems, recv_sems, _in_hbm(after), *passing, *given)
    return out[0], out[1], list(out[2:2 + nt]), list(out[2 + nt:2 + nt + n_pass]), out[-1]


def forward_wait(name, send_sems, recv_sems, fulls, kinds, shard_shapes, after):
    nt = len(fulls)

    def body(*refs):
        full_refs, send_ref, recv_ref = refs[:nt], refs[nt], refs[nt + 1]
        x, y, c, others = _position()
        for t in range(nt):
            for j, (ox, oy) in enumerate(others):
                cp = pltpu.make_async_remote_copy(
                    src_ref=_window(full_refs[t], kinds[t], 2 * ox + oy, c, shard_shapes[t]),
                    dst_ref=_window(full_refs[t], kinds[t], 2 * ox + oy, 1 - c, shard_shapes[t]),
                    send_sem=send_ref.at[3 * t + j], recv_sem=recv_ref.at[3 * t + j],
                    device_id=(x, y, 1 - c), device_id_type=MESH)
                cp.wait_send()
                cp.wait_recv()

    return list(pl.pallas_call(
        body, name=name, in_specs=[HBM_SPEC] * nt + [SEM_SPEC, SEM_SPEC, HBM_SPEC], out_specs=[HBM_SPEC] * nt,
        out_shape=[pltpu.HBM(f.shape, f.dtype) for f in fulls], input_output_aliases={t: t for t in range(nt)},
        compiler_params=_split_params())(*fulls, send_sems, recv_sems, _in_hbm(after)))


def _piece(ref, kind, chip, shard_shape):
    r, n = shard_shape
    if kind == "col":
        return ref.at[:, pl.ds(pl.multiple_of(chip * n, 128), n)]
    return ref.at[pl.ds(pl.multiple_of(chip * r, 16), r), :]


def _piece_shape(kind, shard_shape):
    r, n = shard_shape
    return (r // 2, n) if kind == "col" else (r, n // 2)


def exchange_start(name, parts, kinds, shard_shapes, carry):
    nt = len(parts)
    lands = [lax.empty((3,) + _piece_shape(kinds[t], shard_shapes[t]), BF16) for t in range(nt)]
    given, given_specs, token_type, write = _hand_through(carry)
    n_in = 2 * nt + len(given)

    def body(*refs):
        part_refs, land_refs = refs[:nt], refs[nt:2 * nt]
        send_sems, recv_sems, token = refs[n_in], refs[n_in + 1], refs[-1]
        x, y, c, others = _position()
        for t in range(nt):
            for j, (ox, oy) in enumerate(others):
                pltpu.make_async_remote_copy(
                    src_ref=_piece(part_refs[t], kinds[t], 2 * ox + oy, shard_shapes[t]), dst_ref=land_refs[t].at[j],
                    send_sem=send_sems.at[3 * t + j], recv_sem=recv_sems.at[3 * t + j],
                    device_id=(ox, oy, c), device_id_type=MESH).start()
        write(token, refs[:n_in])

    sems = pltpu.SemaphoreType.DMA((3 * nt,))
    both = list(parts) + lands
    out = pl.pallas_call(
        body, name=name, in_specs=[HBM_SPEC] * (2 * nt) + given_specs,
        out_specs=(SEM_SPEC, SEM_SPEC, *[HBM_SPEC] * (2 * nt), pl.BlockSpec(memory_space=pltpu.VMEM)),
        out_shape=(sems, sems, *[pltpu.HBM(a.shape, a.dtype) for a in both], token_type),
        input_output_aliases={t: 2 + t for t in range(2 * nt)}, compiler_params=_split_params(),
    )(*[_in_hbm(a) for a in both], *given)
    return out[0], out[1], list(out[2:2 + nt]), list(out[2 + nt:2 + 2 * nt]), out[-1]


def exchange_wait(name, send_sems, recv_sems, parts, lands, kinds, shard_shapes, after):
    nt = len(parts)

    def body(*refs):
        part_refs, land_refs = refs[:nt], refs[nt:2 * nt]
        send_ref, recv_ref = refs[2 * nt], refs[2 * nt + 1]
        x, y, c, others = _position()
        for t in range(nt):
            for j, (ox, oy) in enumerate(others):
                cp = pltpu.make_async_remote_copy(
                    src_ref=_piece(part_refs[t], kinds[t], 2 * ox + oy, shard_shapes[t]), dst_ref=land_refs[t].at[j],
                    send_sem=send_ref.at[3 * t + j], recv_sem=recv_ref.at[3 * t + j],
                    device_id=(ox, oy, c), device_id_type=MESH)
                cp.wait_send()
                cp.wait_recv()

    both = list(parts) + list(lands)
    out = pl.pallas_call(
        body, name=name, in_specs=[HBM_SPEC] * (2 * nt) + [SEM_SPEC, SEM_SPEC, HBM_SPEC], out_specs=[HBM_SPEC] * (2 * nt),
        out_shape=[pltpu.HBM(a.shape, a.dtype) for a in both], input_output_aliases={t: t for t in range(2 * nt)},
        compiler_params=_split_params())(*both, send_sems, recv_sems, _in_hbm(after))
    return list(out[:nt]), list(out[nt:])


def reduce_swap(bufs, wire):
    n = len(bufs)
    halves = [b.shape[0] // 2 for b in bufs]

    def body(*refs):
        in_refs, out_refs, txs, got = refs[:n], refs[n:2 * n], refs[2 * n:3 * n], refs[3 * n:4 * n]
        send_sems, recv_sems = refs[4 * n:]
        x, y, c, _ = _position()
        cps = []
        for k in range(n):
            txs[k][...] = in_refs[k][pl.ds(pl.multiple_of((1 - c) * halves[k], 8), halves[k]), :].astype(wire[k])
            cp = pltpu.make_async_remote_copy(src_ref=txs[k], dst_ref=got[k], send_sem=send_sems.at[k],
                                              recv_sem=recv_sems.at[k], device_id=(x, y, 1 - c), device_id_type=MESH)
            cp.start()
            cps.append(cp)
        for k, cp in enumerate(cps):
            cp.wait()
            own = in_refs[k][pl.ds(pl.multiple_of(c * halves[k], 8), halves[k]), :]
            out_refs[k][...] = (own.astype(wire[k]).astype(F32) + got[k][...].astype(F32)).astype(wire[k])

    vm = pl.BlockSpec(memory_space=pltpu.VMEM)
    parts = [((h, b.shape[1]), w) for h, b, w in zip(halves, bufs, wire)]
    return list(pl.pallas_call(
        body, name="reduce_swap", in_specs=[vm] * n, out_specs=[vm] * n, out_shape=[_sds(sh, w) for sh, w in parts],
        scratch_shapes=[pltpu.VMEM(sh, w) for sh, w in parts] * 2 + [pltpu.SemaphoreType.DMA((n,))] * 2,
        compiler_params=_params())(*bufs))


def reduce_start(parts):
    n = len(parts)
    lands = [lax.empty((4,) + tuple(p.shape), p.dtype) for p in parts]

    def body(*refs):
        part_refs, land_refs, send_sems, recv_sems = refs[:n], refs[n:2 * n], refs[2 * n], refs[2 * n + 1]
        x, y, c, others = _position()
        for k in range(n):
            for j, (ox, oy) in enumerate(others):
                pltpu.make_async_remote_copy(
                    src_ref=part_refs[k], dst_ref=land_refs[k].at[2 * x + y], send_sem=send_sems.at[3 * k + j],
                    recv_sem=recv_sems.at[3 * k + j], device_id=(ox, oy, c), device_id_type=MESH).start()

    sems = pltpu.SemaphoreType.DMA((3 * n,))
    both = list(parts) + lands
    out = pl.pallas_call(
        body, name="reduce_start", in_specs=[HBM_SPEC] * (2 * n), out_specs=(SEM_SPEC, SEM_SPEC, *[HBM_SPEC] * (2 * n)),
        out_shape=(sems, sems, *[pltpu.HBM(a.shape, a.dtype) for a in both]),
        input_output_aliases={k: 2 + k for k in range(2 * n)}, compiler_params=_split_params(),
    )(*[_in_hbm(a) for a in both])
    return out[0], out[1], list(out[2:2 + n]), list(out[2 + n:])


def reduce_wait(send_sems, recv_sems, parts, lands, after):
    n = len(parts)

    def body(*refs):
        part_refs, land_refs, send_ref, recv_ref = refs[:n], refs[n:2 * n], refs[2 * n], refs[2 * n + 1]
        x, y, c, others = _position()
        for k in range(n):
            for j, (ox, oy) in enumerate(others):
                cp = pltpu.make_async_remote_copy(
                    src_ref=part_refs[k], dst_ref=land_refs[k].at[2 * ox + oy], send_sem=send_ref.at[3 * k + j],
                    recv_sem=recv_ref.at[3 * k + j], device_id=(ox, oy, c), device_id_type=MESH)
                cp.wait_send()
                cp.wait_recv()

    both = list(parts) + list(lands)
    out = pl.pallas_call(
        body, name="reduce_wait", in_specs=[HBM_SPEC] * (2 * n) + [SEM_SPEC, SEM_SPEC, HBM_SPEC],
        out_specs=[HBM_SPEC] * (2 * n), out_shape=[pltpu.HBM(a.shape, a.dtype) for a in both],
        input_output_aliases={k: k for k in range(2 * n)}, compiler_params=_split_params(),
    )(*both, send_sems, recv_sems, _in_hbm(after))
    return list(out[:n]), list(out[n:])


def reduce_share(parts, lands):
    n = len(parts)
    halves = [p.shape[0] for p in parts]

    def body(*refs):
        part_refs, land_refs, out_refs = refs[:n], refs[n:2 * n], refs[2 * n:3 * n]
        send_sems, recv_sems = refs[3 * n:]
        x, y, c, _ = _position()
        chip = 2 * x + y
        cps = []
        for k in range(n):
            mine = pl.ds(pl.multiple_of(c * halves[k], 8), halves[k])
            own = part_refs[k][...].astype(F32)
            total = jnp.where(chip == 0, own, land_refs[k][0].astype(F32))
            for entry in range(1, 4):
                total = total + jnp.where(chip == entry, own, land_refs[k][entry].astype(F32))
            out_refs[k][mine, :] = total
            cp = pltpu.make_async_remote_copy(
                src_ref=out_refs[k].at[mine], dst_ref=out_refs[k].at[mine], send_sem=send_sems.at[k],
                recv_sem=recv_sems.at[k], device_id=(x, y, 1 - c), device_id_type=MESH)
            cp.start()
            cps.append(cp)
        for cp in cps:
            cp.wait()

    vm = pl.BlockSpec(memory_space=pltpu.VMEM)
    return list(pl.pallas_call(
        body, name="reduce_share", in_specs=[vm] * (2 * n), out_specs=[vm] * n,
        out_shape=[_sds((2 * p.shape[0], p.shape[1]), F32) for p in parts],
        scratch_shapes=[pltpu.SemaphoreType.DMA((n,))] * 2, compiler_params=_params())(*parts, *lands))


def _local_step(x, target, small, need, ahead, emit_swap, emit_exchange):
    d = D_MODEL
    full = {}

    def handed(vec, token):
        return vec if token is None else token

    def token_rows(token):
        return [] if token is None else [token]

    def plus(acc, rows):
        return acc + rows[0] if rows else acc

    rb16, rbt16, rc16, rct16, lr_t, li_t = small["s5_operands"]
    ge, y2, cs = s5_fwd(x, small["norm_mix0"], small["s5_d"], rb16, rc16, lr_t, li_t)
    full.update(need("glu", ge))

    def norm_rows(h, gains):
        xh, _ = _rms_hat(h)
        return [xh * g for g in gains]

    def glu_epilogue(accs, e, r):
        v, gt = accs[0] + r[0], accs[1] + r[1]
        h = e[0] + v * jax.nn.sigmoid(gt)
        return [h, v, gt] + norm_rows(h, r[2:])

    gain_mlp0 = handed(small["norm_mlp0"], ahead("mlp_in0", full["w_glu"], small["norm_mlp0"]))
    h1, val, gate, n1 = mm_nn(
        "glu", ge, full["w_glu"], [0, d], d, glu_epilogue, [F32, F32, F32, BF16], extras=[x],
        rowvecs=[(small["s5_b_glu"], 0), (small["s5_b_glu"], d), (gain_mlp0, 0)], tm=512, tn=d)

    def mlp_fwd(tag, h, n, w_in, get_w_out, next_gains, head=None):
        def in_epilogue(accs, e, rv):
            pos = jnp.maximum(accs[0], 0.0)
            return [pos * pos, 2.0 * pos]

        r, slope = mm_nn("mlp_in" + tag, n, w_in, [0], w_in.shape[1], in_epilogue, [BF16, BF16], tm=2048)
        w_out = get_w_out(r)

        def epilogue(accs, e, rv):
            h_out = e[0] + accs[0]
            return [h_out] + norm_rows(h_out, rv)

        if head is not None:
            return head(r, w_out, h), (n, r, slope)
        outs = mm_nn("mlp_out" + tag, r, w_out, [0], d, epilogue, [F32] + [BF16] * len(next_gains), extras=[h],
                     rowvecs=[(g, 0) for g in next_gains], tm=512, tn=d)
        return outs[0], outs[1:], (n, r, slope)

    full.update(need("mlp_in0", h1))

    def w_out0(after):
        full.update(need("mlp_out0", after))
        return full["w_out0"]

    h2, (nkv, n2), mlp0 = mlp_fwd("0", h1, n1, full["w_in0"], w_out0, [small["norm_kv"], small["norm_mix1"]])

    full.update(need("attn", h2))
    kvw = 2 * N_KV * HEAD_DIM
    (kv,) = mm_nn("kv_proj", nkv, full["w_kv"], [0], kvw, lambda accs, e, r: [accs[0] + r[0]], [BF16],
                  rowvecs=[(small["b_kv"], 0)], tm=2048)
    (q,) = mm_nn("q_proj", n2, full["w_q"], [0], d, lambda accs, e, r: [accs[0] + r[0]], [BF16],
                 rowvecs=[(small["b_q"], 0)], tm=2048)
    sinks = small["sinks"].reshape(N_Q)
    o = attn_fwd(q, kv, sinks)
    def o_epilogue(accs, e, r):
        h_out = e[0] + accs[0] + r[0]
        return [h_out] + norm_rows(h_out, r[1:])

    bias_o = handed(small["b_o"], ahead("mlp_in1", o, small["b_o"]))
    h3, n3 = mm_nn("o_proj", o, full["w_o"], [0], d, o_epilogue, [F32, BF16], extras=[h2],
                   rowvecs=[(bias_o, 0), (small["norm_mlp1"], 0)], tm=512, tn=d)
    full.update(need("mlp_in1", h3, then="mlp_out1"))

    def w_out1(after):
        full.update(need("mlp_out1", after))
        return full["w_out1"]

    def loss_head(r, w_out, h):
        def epilogue(accs, e, rv):
            xh, rr = _rms_hat(e[0] + accs[0])
            err = xh * rv[0] - e[1]
            dy = err * (1.0 / d)
            dxh = dy * rv[0]
            dx = rr * (dxh - xh * jnp.mean(dxh * xh, axis=-1, keepdims=True))
            loss = jnp.full((1, d), 0.5 * jnp.sum(jnp.mean(err * err, axis=-1, keepdims=True)), F32)
            return [dx, dx, loss, jnp.sum(dy * xh, axis=0, keepdims=True)]

        return mm_nn("mlp_out1", r, w_out, [0], d, epilogue, [F32, BF16], extras=[h, target],
                     rowvecs=[(small["norm_final"], 0)], n_sums=2, tm=512, tn=d)

    (dh, dhb, loss_tile, dg_final), mlp1 = mlp_fwd("1", h3, n3, full["w_in1"], w_out1, [], head=loss_head)

    grads_small, grads_full = {"norm_final": dg_final}, {}
    ident = lambda acc, e, r: [plus(acc, r)]
    layer1 = ["w_out1", "w_in1", "w_o", "w_q", "w_kv"]
    layer0 = ["w_out0", "w_in0", "w_glu"]

    def norm_bwd_rows(x_rows, res, dys, gains):
        xh, r = _rms_hat(x_rows)
        dxh = sum(dy * g for dy, g in zip(dys, gains))
        dx = r * (dxh - xh * jnp.mean(dxh * xh, axis=-1, keepdims=True)) + res
        return dx, [jnp.sum(dy * xh, axis=0, keepdims=True) for dy in dys]

    def mlp_bwd(tag, dh, dhb, h_in, gain, w_in, w_out, saved, token=None):
        n, r, slope = saved
        grads_full["w_out" + tag] = mm_tn("dw_out" + tag, r, dhb, tn=1024)
        (da,) = mm_nt("mlp_da" + tag, dhb, w_out, lambda acc, e, rv: [plus(acc * e[0].astype(F32), rv)], [BF16],
                      extras=[slope], rowvecs=token_rows(token), tm=2048)
        grads_full["w_in" + tag] = mm_tn("dw_in" + tag, n, da, tn=1024)

        def epilogue(acc, e, rv):
            dx, dgs = norm_bwd_rows(e[0], e[1], [acc], rv)
            return [dx, dx, jnp.sum(dx, axis=0, keepdims=True)] + dgs

        dx, dxb, colsum, dg = mm_nt("mlp_dn" + tag, da, w_in, epilogue, [F32, BF16], extras=[h_in, dh], rowvecs=[gain],
                                    n_sums=2, tm=512, tk=d)
        grads_small["norm_mlp" + tag] = dg
        return dx, dxb, colsum

    dh3, dh3b, colsum3 = mlp_bwd("1", dh, dhb, h3, small["norm_mlp1"], full["w_in1"], full["w_out1"], mlp1)
    grads_small["b_o"] = colsum3
    grads_full["w_o"] = mm_tn("dw_o", o, dh3b, tn=1024)
    (do,) = mm_nt("attn_do", dh3b, full["w_o"], ident, [BF16], tm=2048)
    dq, dbq, dprev, dcur, dsink = attn_bwd(q, kv, do, sinks)
    dkv, dbkv = kv_combine(dprev, dcur)
    grads_small["b_q"], grads_small["b_kv"], grads_small["sinks"] = dbq, dbkv, dsink
    grads_full["w_q"] = mm_tn("dw_q", n2, dq, tn=1024)
    grads_full["w_kv"] = mm_tn("dw_kv", nkv, dkv, tk=1024)
    token = emit_swap("layer1", {n: grads_full[n] for n in layer1}, (1, d))
    (dnkv,) = mm_nt("kv_dn", dkv, full["w_kv"], ident, [F32], rowvecs=token_rows(token), tm=2048, tk=1024)

    def attn_dn_epilogue(acc, e, rv):
        dx, dgs = norm_bwd_rows(e[0], e[1], [acc, e[2]], rv)
        return [dx, dx] + dgs

    dh2, dh2b, dg_mix1, dg_kv = mm_nt("attn_dn", dq, full["w_q"], attn_dn_epilogue, [F32, BF16], extras=[h2, dh3, dnkv],
                                      rowvecs=[small["norm_mix1"], small["norm_kv"]], n_sums=2, tm=512, tk=d)
    grads_small["norm_mix1"], grads_small["norm_kv"] = dg_mix1, dg_kv
    token = emit_exchange("layer1", dh2b, (1, full["w_out0"].shape[0]))
    dh1, _, _ = mlp_bwd("0", dh2, dh2b, h1, small["norm_mlp0"], full["w_in0"], full["w_out0"], mlp0, token)

    dz, db_glu = glu_bwd(dh1, val, gate)
    grads_small["s5_b_glu"] = db_glu
    grads_full["w_glu"] = mm_tn("dw_glu", ge, dz, tn=1024)
    token = emit_swap("layer0", {n: grads_full[n] for n in layer0}, (1, d))
    (dy2,) = mm_nt("glu_dy", dz, full["w_glu"], lambda acc, e, rv: [plus(acc, rv) * _gelu_grad(e[0])], [F32], extras=[y2],
                   rowvecs=token_rows(token), tm=1024, tk=1024)
    d_skip = handed(small["s5_d"], emit_exchange("layer0", dy2, small["s5_d"]))
    grad_x, dd, drb, drc, dlr, dli, dg_mix0 = s5_bwd(x, small["norm_mix0"], dy2, dh1, d_skip, cs, rb16, rbt16, rct16, lr_t, li_t)
    grads_small["s5_d"] = dd
    grads_small["s5_mats"] = (drb, drc, dlr, dli)
    grads_small["norm_mix0"] = dg_mix0
    return loss_tile, grad_x, grads_small


SMALL_NAMES = ["norm_mix", "norm_mlp", "norm_kv", "norm_final", "s5_a_re", "s5_a_im", "s5_log_dt", "s5_b_re", "s5_b_im",
               "s5_c_re", "s5_c_im", "s5_d", "s5_b_glu", "b_kv", "b_q", "sinks", "b_o"]
BIG_NAMES = ["s5_w_glu", "w_kv", "w_q", "w_o", "w_mlp_in", "w_mlp_out"]
WEIGHT_ORDER = ["norm_mix", "norm_mlp", "norm_kv", "norm_final", "s5_a_re", "s5_a_im", "s5_log_dt", "s5_b_re", "s5_b_im",
                "s5_c_re", "s5_c_im", "s5_d", "s5_w_glu", "s5_b_glu", "w_kv", "b_kv", "w_q", "b_q", "sinks", "w_o", "b_o",
                "w_mlp_in", "w_mlp_out"]


def kernel(x, norm_mix, norm_mlp, norm_kv, norm_final, s5_a_re, s5_a_im, s5_log_dt, s5_b_re, s5_b_im, s5_c_re, s5_c_im, s5_d, s5_w_glu, s5_b_glu, w_kv, b_kv, w_q, b_q, sinks, w_o, b_o, w_mlp_in, w_mlp_out, loss_target, m_norm_mix, m_norm_mlp, m_norm_kv, m_norm_final, m_s5_a_re, m_s5_a_im, m_s5_log_dt, m_s5_b_re, m_s5_b_im, m_s5_c_re, m_s5_c_im, m_s5_d, m_s5_w_glu, m_s5_b_glu, m_w_kv, m_b_kv, m_w_q, m_b_q, m_sinks, m_w_o, m_b_o, m_w_mlp_in, m_w_mlp_out, v_norm_mix, v_norm_mlp, v_norm_kv, v_norm_final, v_s5_a_re, v_s5_a_im, v_s5_log_dt, v_s5_b_re, v_s5_b_im, v_s5_c_re, v_s5_c_im, v_s5_d, v_s5_w_glu, v_s5_b_glu, v_w_kv, v_b_kv, v_w_q, v_b_q, v_sinks, v_w_o, v_b_o, v_w_mlp_in, v_w_mlp_out):
    env = dict(locals())
    w = {n: env[n] for n in WEIGHT_ORDER}
    mom = {n: env["m_" + n] for n in WEIGHT_ORDER}
    var = {n: env["v_" + n] for n in WEIGHT_ORDER}
    d = D_MODEL
    xi, yi, ci = lax.axis_index("x"), lax.axis_index("y"), lax.axis_index("c")
    chip = 2 * xi + yi
    where = jnp.stack([ci, chip]).astype(jnp.int32)

    dsh, bsh = s5_d.shape[1], s5_b_glu.shape[1]
    packed = jnp.concatenate([s5_d.reshape(-1, 128), s5_b_glu.reshape(-1, 128)])
    n_d, n_b = dsh // 128, bsh // 128
    slab = lax.dynamic_update_slice(jnp.zeros((4, 8, 128), F32), jnp.pad(packed, ((0, 8 - n_d - n_b), (0, 0)))[None],
                                    (chip, 0, 0))

    big = [s5_w_glu, w_kv[None], w_q, w_o, w_mlp_in, w_mlp_out]
    entries = [(0, 0, "col"), (1, 0, "row"), (2, 0, "row"), (3, 0, "row"), (4, 0, "col"), (4, 1, "col"),
               (5, 0, "row"), (5, 1, "row")]
    names = ["w_glu", "w_kv", "w_q", "w_o", "w_in0", "w_in1", "w_out0", "w_out1"]
    kinds = dict(zip(names, [k for _, _, k in entries]))
    shard_shapes = dict(zip(names, [tuple(big[a].shape[1:]) for a, _, _ in entries]))

    placed_w = dict(zip(names, cast_place(big, entries, where)))
    placed_w["vectors"], kinds["vectors"], shard_shapes["vectors"] = slab, "slab", None
    gather_groups = {"glu": ["w_glu"], "mlp_in0": ["w_in0"], "mlp_out0": ["w_out0"], "attn": ["w_kv", "w_q", "w_o"],
                     "mlp_in1": ["w_in1"], "mlp_out1": ["w_out1"]}
    order = ["vectors"] + [n for members in gather_groups.values() for n in members]
    send, recv, thru, log_dt = gather_start([placed_w[n] for n in order], [kinds[n] for n in order],
                                            [shard_shapes[n] for n in order], s5_log_dt)
    started = dict(zip(order, thru))
    (gathered_rows,) = gather_wait("gather_wait_vectors", send, recv, [started["vectors"]], ["slab"], [None], None, 0)
    d_full = gathered_rows[:, 0:n_d].reshape(1, -1)
    bglu_full = gathered_rows[:, n_d:n_d + n_b].reshape(1, -1)

    forwarding = {}

    def ahead(group, after, carry, passing=()):
        members = gather_groups[group]
        ks, shapes = [kinds[n] for n in members], [shard_shapes[n] for n in members]
        d2d_send, d2d_recv, landed, passed, tok = forward_start(
            "forward_start_" + group, send, recv, [started[n] for n in members], ks, shapes, after,
            order.index(members[0]), carry, passing)
        forwarding[group] = (d2d_send, d2d_recv, landed)
        return passed if passing else tok

    def need(group, after, then=None):
        members = gather_groups[group]
        ks, shapes = [kinds[n] for n in members], [shard_shapes[n] for n in members]
        if group in forwarding:
            arrays = forward_wait("forward_wait_" + group, *forwarding[group], ks, shapes, after)
        else:
            landed = gather_wait("gather_wait_" + group, send, recv, [started[n] for n in members], ks, shapes, after,
                                 order.index(members[0]))
            arrays = forward_halves("forward_halves_" + group, landed, ks, shapes)
        if then is not None:
            arrays = ahead(then, after, (8, 128), arrays)
        return dict(zip(members, arrays))

    swapping, exchanging = {}, {}

    def emit_swap(group, partial, carry):
        members = list(partial)
        send, recv, mine, lands, tok = swap_start("swap_start_" + group, [partial[n] for n in members],
                                                  [kinds[n] for n in members], carry)
        swapping[group] = (members, send, recv, mine, lands)
        return tok

    def emit_exchange(group, after, carry):
        members, send, recv, mine, lands = swapping[group]
        ks, shapes = [kinds[n] for n in members], [shard_shapes[n] for n in members]
        mine, landed = swap_wait("swap_wait_" + group, send, recv, mine, lands, ks, after)
        sums = add_halves("add_halves_" + group, mine, landed, ks, where)
        send, recv, parts, lands, tok = exchange_start("exchange_start_" + group, sums, ks, shapes, carry)
        exchanging[group] = (members, send, recv, parts, lands)
        return tok

    s5_args = (s5_a_re[0], s5_a_im[0], log_dt[0], s5_b_re[0], s5_b_im[0])
    small = {
        "norm_mix0": norm_mix[0:1], "norm_mix1": norm_mix[1:2], "norm_mlp0": norm_mlp[0:1], "norm_mlp1": norm_mlp[1:2],
        "norm_kv": norm_kv.reshape(1, d), "norm_final": norm_final.reshape(1, d), "s5_operands": s5_prep(*s5_args, s5_c_re[0], s5_c_im[0]),
        "s5_d": d_full, "s5_b_glu": bglu_full,
        "b_kv": b_kv.reshape(1, -1), "b_q": b_q, "sinks": sinks, "b_o": b_o,
    }
    loss_row, grad_x, gs = _local_step(x[0], loss_target[0], small, need, ahead, emit_swap, emit_exchange)

    mats, lams = s5_compact(*gs["s5_mats"])
    rows = [gs["norm_mix0"], gs["norm_mix1"], gs["norm_mlp0"], gs["norm_mlp1"], gs["norm_kv"], gs["norm_final"], gs["s5_d"],
            gs["b_q"], gs["b_o"], gs["s5_b_glu"], gs["b_kv"], gs["sinks"], loss_row, jnp.zeros((2, d), F32)]
    small_send, small_recv, small_parts, small_lands = reduce_start(
        reduce_swap([jnp.concatenate(rows, axis=0), lams, mats], [F32, F32, BF16]))

    reduced = [None] * len(big)
    where_of = dict(zip(names, entries))
    for group in ("layer1", "layer0"):
        members, send, recv, parts, lands = exchanging[group]
        ks, shapes = [kinds[n] for n in members], [shard_shapes[n] for n in members]
        parts, lands = exchange_wait("exchange_wait_" + group, send, recv, parts, lands, ks, shapes, small_lands[-1])
        targets = [where_of[n][0] for n in members]
        sums = sum_shards("sum_shards_" + group, parts, lands, ks, shapes, where, [where_of[n][1] for n in members],
                          [big[a].shape[0] for a in targets], [reduced[a] for a in targets])
        for a, arr in zip(targets, sums):
            reduced[a] = arr
    share_send, share_recv, reduced, _ = share_start(reduced, entries, (8, 128))

    vecs, lams, mats = reduce_share(*reduce_wait(small_send, small_recv, small_parts, small_lands, reduced[0]))
    grads = split_vectors(where, vecs, dsh, bsh)
    loss = grads.pop("loss")[0, 0]
    g_are, g_aim, g_dt, g_bre, g_bim, dc_re, dc_im = s5_param_bwd(mats, lams, *s5_args)
    grads.update({"s5_a_re": g_are[None], "s5_a_im": g_aim[None], "s5_log_dt": g_dt[None], "s5_b_re": g_bre[None],
                  "s5_b_im": g_bim[None], "s5_c_re": dc_re[None], "s5_c_im": dc_im[None]})

    delta, new_m, new_v = {}, {}, {}

    def view(n, a):
        return a.reshape(1, -1) if a.ndim == 1 else jnp.swapaxes(a, -1, -2) if n in ("s5_b_re", "s5_b_im") else a

    sw, sg, sm, sv = ([view(n, t[n]) for n in SMALL_NAMES] for t in (w, grads, mom, var))
    for n, a, b, c_ in zip(SMALL_NAMES, *adamw_native("adamw_small", sw, sg, sm, sv)):
        delta[n], new_m[n], new_v[n] = (view(n, t) if t.ndim == 4 else t for t in (a, b, c_))

    reduced = share_wait(share_send, share_recv, reduced, entries, new_v["s5_c_re"])
    for n, g in zip(BIG_NAMES, reduced):
        grads[n] = g.reshape(w[n].shape)
    flat = lambda t: [t[n].reshape(-1, t[n].shape[-1]) for n in BIG_NAMES]
    for table, arrays in zip((grads, delta, new_m, new_v), adamw("adamw_big", flat(w), flat(grads), flat(mom), flat(var))):
        for n, a in zip(BIG_NAMES, arrays):
            table[n] = a.reshape(w[n].shape)

    out = [loss.reshape(()), grad_x[None]]
    for table in (grads, delta, new_m, new_v):
        out += [table[n].reshape(w[n].shape) for n in WEIGHT_ORDER]
    return tuple(out)
```

```python
import math

import jax
import jax.numpy as jnp
from jax import lax
from jax.experimental import pallas as pl
from jax.experimental.pallas import tpu as pltpu

F32 = jnp.float32
BF16 = jnp.bfloat16

D_MODEL = 1024
S5_GROUPS = 64
S5_GROUP = 16
S5_STATE = 64
N_KV = 4
N_Q = 16
HEAD_DIM = 64
BLOCK = 128
NORM_EPS = 1e-5
LAMBDA_RE_MAX = -1e-4
ADAM_LR, ADAM_B1, ADAM_B2, ADAM_EPS, ADAM_WD, ADAM_STEP = 0.001, 0.9, 0.999, 1e-08, 0.01, 10

VMEM_LIMIT_BYTES = 56 * 1024 * 1024
S5_CHUNK = 256
S5_BLOCKS = 4
MESH = pl.DeviceIdType.MESH


def _params(sem=None):
    return pltpu.CompilerParams(dimension_semantics=sem, vmem_limit_bytes=VMEM_LIMIT_BYTES)


def _sds(shape, dtype):
    return jax.ShapeDtypeStruct(shape, dtype)


def _rms_hat(xv):
    r = lax.rsqrt(jnp.mean(xv * xv, axis=-1, keepdims=True) + NORM_EPS)
    return xv * r, r


def mm_nn(name, a, w, col_offsets, n_out, epilogue, out_dtypes, extras=(), rowvecs=(), n_sums=0, tm=1024, tn=512):
    m, k = a.shape
    tm, tn = min(tm, m), min(tn, n_out)
    nw, ne, nr, no = len(col_offsets), len(extras), len(rowvecs), len(out_dtypes)

    def body(a_ref, *refs):
        w_refs, e_refs, r_refs = refs[:nw], refs[nw:nw + ne], refs[nw + ne:nw + ne + nr]
        o_refs, s_refs = refs[nw + ne + nr:nw + ne + nr + no], refs[nw + ne + nr + no:]
        av = a_ref[...]
        accs = [jnp.dot(av, w_ref[...], preferred_element_type=F32) for w_ref in w_refs]
        outs = epilogue(accs, [e[...] for e in e_refs], [r[...] for r in r_refs])
        for o_ref, o in zip(o_refs, outs[:no]):
            o_ref[...] = o.astype(o_ref.dtype)
        if n_sums:
            @pl.when(pl.program_id(1) == 0)
            def _():
                for s_ref in s_refs:
                    s_ref[...] = jnp.zeros_like(s_ref)

            for s_ref, val in zip(s_refs, outs[no:]):
                s_ref[...] += val

    def wspec(off):
        return pl.BlockSpec((k, tn), lambda j, i, off=off: (0, off // tn + j))

    def rspec(off):
        return pl.BlockSpec((1, tn), lambda j, i, off=off: (0, off // tn + j))

    tile = pl.BlockSpec((tm, tn), lambda j, i: (i, j))
    in_specs = ([pl.BlockSpec((tm, k), lambda j, i: (i, 0))] + [wspec(o) for o in col_offsets]
                + [tile] * ne + [rspec(o) for _, o in rowvecs])
    sem = ("parallel", "arbitrary") if n_sums else ("parallel", "parallel")
    return pl.pallas_call(
        body, grid=(n_out // tn, m // tm), in_specs=in_specs,
        out_specs=[tile] * no + [pl.BlockSpec((1, tn), lambda j, i: (0, j))] * n_sums,
        out_shape=[_sds((m, n_out), dt) for dt in out_dtypes] + [_sds((1, n_out), F32)] * n_sums, name=name,
        compiler_params=_params(sem))(a, *([w] * nw), *extras, *[r for r, _ in rowvecs])


def mm_nt(name, g, w, epilogue, out_dtypes, extras=(), rowvecs=(), n_sums=0, tm=512, tk=512):
    m, n = g.shape
    k = w.shape[0]
    tm, tk = min(tm, m), min(tk, k)
    ne, nr, no = len(extras), len(rowvecs), len(out_dtypes)

    def body(g_ref, w_ref, *refs):
        e_refs, r_refs, o_refs, s_refs = refs[:ne], refs[ne:ne + nr], refs[ne + nr:ne + nr + no], refs[ne + nr + no:]
        acc = lax.dot_general(g_ref[...], w_ref[...], (((1,), (1,)), ((), ())), preferred_element_type=F32)
        outs = epilogue(acc, [e[...] for e in e_refs], [r[...] for r in r_refs])
        for o_ref, o in zip(o_refs, outs[:no]):
            o_ref[...] = o.astype(o_ref.dtype)
        if n_sums:
            @pl.when(pl.program_id(0) == 0)
            def _():
                for s_ref in s_refs:
                    s_ref[...] = jnp.zeros_like(s_ref)

            for s_ref, val in zip(s_refs, outs[no:]):
                s_ref[...] += val

    tile = pl.BlockSpec((tm, tk), lambda i, j: (i, j))
    vec = pl.BlockSpec((1, tk), lambda i, j: (0, j))
    sem = ("arbitrary", "parallel") if n_sums else ("parallel", "parallel")
    return pl.pallas_call(
        body, grid=(m // tm, k // tk),
        in_specs=[pl.BlockSpec((tm, n), lambda i, j: (i, 0)), pl.BlockSpec((tk, n), lambda i, j: (j, 0))]
        + [tile] * ne + [vec] * nr,
        out_specs=[tile] * no + [vec] * n_sums,
        out_shape=[_sds((m, k), dt) for dt in out_dtypes] + [_sds((1, k), F32)] * n_sums, name=name,
        compiler_params=_params(sem))(g, w, *extras, *rowvecs)


def mm_tn(name, a, g, tk=512, tn=512):
    m, k = a.shape
    n = g.shape[1]
    tk, tn = min(tk, k), min(tn, n)

    def body(a_ref, g_ref, o_ref):
        acc = lax.dot_general(a_ref[...], g_ref[...], (((0,), (0,)), ((), ())), preferred_element_type=F32)
        o_ref[...] = acc.astype(o_ref.dtype)

    return pl.pallas_call(
        body, grid=(k // tk, n // tn),
        in_specs=[pl.BlockSpec((m, tk), lambda i, j: (0, i)), pl.BlockSpec((m, tn), lambda i, j: (0, j))],
        out_specs=pl.BlockSpec((tk, tn), lambda i, j: (i, j)), out_shape=_sds((k, n), BF16), name=name,
        compiler_params=_params(("parallel", "parallel")))(a, g)


def _row_mask(tc):
    row = lax.broadcasted_iota(jnp.int32, (8 * tc, 256), 0) % 8
    col = lax.broadcasted_iota(jnp.int32, (8 * tc, 256), 1) // 32
    return row == col


def _expand_rows(val, mask):
    tc, width = val.shape
    rep = jnp.broadcast_to(val[:, None, :], (tc, 8, width)).reshape(8 * tc, width)
    return jnp.where(mask, rep, 0.0).astype(BF16)


def _stage(ref, val):
    ref[0] = val[:, 0:128]
    ref[1] = val[:, 128:256]


def _gather_rows(src_ref, tc):
    halves = []
    for half in range(2):
        col = lax.broadcasted_iota(jnp.int32, (tc, 128), 1) // 32 + 4 * half
        out = jnp.zeros((tc, 128), F32)
        for s8 in range(4 * half, 4 * half + 4):
            out = jnp.where(col == s8, src_ref.at[half][pl.ds(s8, tc, stride=8), :], out)
        halves.append(out)
    return jnp.concatenate(halves, axis=1)


def _gelu_and_slope(x):
    c = math.sqrt(2.0 / math.pi)
    t = jnp.tanh(c * (x + 0.044715 * x * x * x))
    return 0.5 * x * (1.0 + t), 0.5 * (1.0 + t) + 0.5 * x * (1.0 - t * t) * c * (1.0 + 3.0 * 0.044715 * x * x)


def s5_fwd(x, gain, d_skip, rb, rc, lam_r, lam_i):
    n_rows = x.shape[0]
    tc = min(S5_CHUNK, n_rows)
    nc = n_rows // tc

    def body(x_ref, g_ref, d_ref, rb_ref, rc_ref, lr_ref, li_ref, ge_ref, slope_ref, cs_ref, bux, yrows, carry):
        i = pl.program_id(0)
        u = _rms_hat(x_ref[...])[0] * g_ref[...]

        @pl.when(i == 0)
        def _():
            carry[...] = jnp.zeros_like(carry)

        cs_ref[0] = carry[...]
        mask = _row_mask(tc)
        for blk in range(S5_BLOCKS):
            lhs = _expand_rows(u[:, blk * 256:(blk + 1) * 256], mask)
            bux[blk] = jnp.dot(lhs, rb_ref[blk], preferred_element_type=F32)
        lam = [(lr_ref[blk], li_ref[blk]) for blk in range(S5_BLOCKS)]

        def step(t, c):
            r0 = pl.multiple_of(t * 8, 8)
            new = []
            for blk in range(S5_BLOCKS):
                xr, xi = c[2 * blk], c[2 * blk + 1]
                lr, li = lam[blk]
                nr = lr * xr - li * xi + bux[blk, pl.ds(r0, 8), 0:128]
                ni = lr * xi + li * xr + bux[blk, pl.ds(r0, 8), 128:256]
                bux[blk, pl.ds(r0, 8), 0:128] = nr
                bux[blk, pl.ds(r0, 8), 128:256] = ni
                new += [nr, ni]
            return tuple(new)

        c0 = []
        for blk in range(S5_BLOCKS):
            c0 += [carry[blk, :, 0:128], carry[blk, :, 128:256]]
        cn = lax.fori_loop(0, tc, step, tuple(c0), unroll=4)
        for blk in range(S5_BLOCKS):
            carry[blk, :, 0:128] = cn[2 * blk]
            carry[blk, :, 128:256] = cn[2 * blk + 1]
        for blk in range(S5_BLOCKS):
            _stage(yrows, jnp.dot(bux[blk].astype(BF16), rc_ref[blk], preferred_element_type=F32))
            sl = slice(blk * 256, (blk + 1) * 256)
            ge, slope = _gelu_and_slope(_gather_rows(yrows, tc) + d_ref[:, sl] * u[:, sl])
            slope_ref[:, sl] = slope
            ge_ref[:, sl] = ge.astype(BF16)

    row = pl.BlockSpec((tc, D_MODEL), lambda i: (i, 0))
    vec = pl.BlockSpec((1, D_MODEL), lambda i: (0, 0))
    mat = pl.BlockSpec((S5_BLOCKS, 256, 256), lambda i: (0, 0, 0))
    lamspec = pl.BlockSpec((S5_BLOCKS, 8, 128), lambda i: (0, 0, 0))
    return pl.pallas_call(
        body, grid=(nc,),
        in_specs=[row, vec, vec, mat, mat, lamspec, lamspec],
        out_specs=[row, row, pl.BlockSpec((1, S5_BLOCKS, 8, 256), lambda i: (i, 0, 0, 0))],
        out_shape=[_sds((n_rows, D_MODEL), BF16), _sds((n_rows, D_MODEL), F32), _sds((nc, S5_BLOCKS, 8, 256), F32)],
        scratch_shapes=[pltpu.VMEM((S5_BLOCKS, 8 * tc, 256), F32), pltpu.VMEM((2, 8 * tc, 128), F32),
                        pltpu.VMEM((S5_BLOCKS, 8, 256), F32)],
        name="s5_fwd", compiler_params=_params(("arbitrary",)))(x, gain, d_skip, rb, rc, lam_r, lam_i)


def s5_bwd(x, gain, dy2, res, d_skip, cs, rb, rbt, rct, lam_r, lam_i):
    n_rows = x.shape[0]
    tc = min(S5_CHUNK, n_rows)
    nc = n_rows // tc

    def body(x_ref, g_ref, dy_ref, res_ref, d_ref, cs_ref, rb_ref, rbt_ref, rct_ref, lr_ref, li_ref,
             dx_ref, dd_ref, drb_ref, drc_ref, dlr_ref, dli_ref, dg_ref, tmp, du, lhsu, lhsd, xs, adj, acarry):
        i = pl.program_id(0)
        u = _rms_hat(x_ref[...])[0] * g_ref[...]

        @pl.when(i == 0)
        def _():
            acarry[...] = jnp.zeros_like(acarry)
            dd_ref[...] = jnp.zeros_like(dd_ref)
            drb_ref[...] = jnp.zeros_like(drb_ref)
            drc_ref[...] = jnp.zeros_like(drc_ref)
            dlr_ref[...] = jnp.zeros_like(dlr_ref)
            dli_ref[...] = jnp.zeros_like(dli_ref)
            dg_ref[...] = jnp.zeros_like(dg_ref)

        dd_ref[...] += jnp.sum(dy_ref[...] * u, axis=0, keepdims=True)
        mask = _row_mask(tc)
        for blk in range(S5_BLOCKS):
            sl = slice(blk * 256, (blk + 1) * 256)
            lhsu[blk] = _expand_rows(u[:, sl], mask)
            xs[blk] = jnp.dot(lhsu[blk], rb_ref[blk], preferred_element_type=F32)
            lhsd[blk] = _expand_rows(dy_ref[:, sl], mask)
            adj[blk] = jnp.dot(lhsd[blk], rct_ref[blk], preferred_element_type=F32)
        lam = [(lr_ref[blk], li_ref[blk]) for blk in range(S5_BLOCKS)]

        def fstep(t, c):
            r0 = pl.multiple_of(t * 8, 8)
            new = []
            for blk in range(S5_BLOCKS):
                xr, xi = c[2 * blk], c[2 * blk + 1]
                lr, li = lam[blk]
                nr = lr * xr - li * xi + xs[blk, pl.ds(r0, 8), 0:128]
                ni = lr * xi + li * xr + xs[blk, pl.ds(r0, 8), 128:256]
                xs[blk, pl.ds(r0, 8), 0:128] = nr
                xs[blk, pl.ds(r0, 8), 128:256] = ni
                new += [nr, ni]
            return tuple(new)

        c0 = []
        for blk in range(S5_BLOCKS):
            c0 += [cs_ref[0, blk, :, 0:128], cs_ref[0, blk, :, 128:256]]
        lax.fori_loop(0, tc, fstep, tuple(c0), unroll=4)

        def bstep(k, c):
            t = tc - 1 - k
            r0 = pl.multiple_of(t * 8, 8)
            rp = pl.multiple_of(jnp.maximum(t - 1, 0) * 8, 8)
            first = t == 0
            new_a, new_g = [], []
            for blk in range(S5_BLOCKS):
                ar, ai = c[0][2 * blk], c[0][2 * blk + 1]
                glr, gli = c[1][2 * blk], c[1][2 * blk + 1]
                lr, li = lam[blk]
                nr = lr * ar + li * ai + adj[blk, pl.ds(r0, 8), 0:128]
                ni = lr * ai - li * ar + adj[blk, pl.ds(r0, 8), 128:256]
                adj[blk, pl.ds(r0, 8), 0:128] = nr
                adj[blk, pl.ds(r0, 8), 128:256] = ni
                pr = jnp.where(first, cs_ref[0, blk, :, 0:128], xs[blk, pl.ds(rp, 8), 0:128])
                pi = jnp.where(first, cs_ref[0, blk, :, 128:256], xs[blk, pl.ds(rp, 8), 128:256])
                new_a += [nr, ni]
                new_g += [glr + nr * pr + ni * pi, gli + ni * pr - nr * pi]
            return tuple(new_a), tuple(new_g)

        a0, g0 = [], []
        for blk in range(S5_BLOCKS):
            a0 += [acarry[blk, :, 0:128], acarry[blk, :, 128:256]]
            g0 += [dlr_ref[blk], dli_ref[blk]]
        an, gn = lax.fori_loop(0, tc, bstep, (tuple(a0), tuple(g0)), unroll=2)
        for blk in range(S5_BLOCKS):
            acarry[blk, :, 0:128] = an[2 * blk]
            acarry[blk, :, 128:256] = an[2 * blk + 1]
            dlr_ref[blk] = gn[2 * blk]
            dli_ref[blk] = gn[2 * blk + 1]
        for blk in range(S5_BLOCKS):
            sl = slice(blk * 256, (blk + 1) * 256)
            ab = adj[blk].astype(BF16)
            _stage(tmp, jnp.dot(ab, rbt_ref[blk], preferred_element_type=F32))
            du[:, sl] = _gather_rows(tmp, tc) + d_ref[:, sl] * dy_ref[:, sl]
            drb_ref[blk] += lax.dot_general(lhsu[blk], ab, (((0,), (0,)), ((), ())), preferred_element_type=F32)
            drc_ref[blk] += lax.dot_general(lhsd[blk], xs[blk].astype(BF16), (((0,), (0,)), ((), ())),
                                            preferred_element_type=F32)
        xh, r = _rms_hat(x_ref[...])
        dg_ref[...] += jnp.sum(du[...] * xh, axis=0, keepdims=True)
        dxh = du[...] * g_ref[...]
        dx_ref[...] = r * (dxh - xh * jnp.mean(dxh * xh, axis=-1, keepdims=True)) + res_ref[...]

    rev = pl.BlockSpec((tc, D_MODEL), lambda i: (nc - 1 - i, 0))
    vec = pl.BlockSpec((1, D_MODEL), lambda i: (0, 0))
    mat = pl.BlockSpec((S5_BLOCKS, 256, 256), lambda i: (0, 0, 0))
    lamspec = pl.BlockSpec((S5_BLOCKS, 8, 128), lambda i: (0, 0, 0))
    big = pltpu.VMEM((S5_BLOCKS, 8 * tc, 256), F32)
    bigb = pltpu.VMEM((S5_BLOCKS, 8 * tc, 256), BF16)
    return pl.pallas_call(
        body, grid=(nc,),
        in_specs=[rev, vec, rev, rev, vec, pl.BlockSpec((1, S5_BLOCKS, 8, 256), lambda i: (nc - 1 - i, 0, 0, 0)),
                  mat, mat, mat, lamspec, lamspec],
        out_specs=[rev, vec, mat, mat, lamspec, lamspec, vec],
        out_shape=[_sds((n_rows, D_MODEL), F32), _sds((1, D_MODEL), F32), _sds((S5_BLOCKS, 256, 256), F32),
                   _sds((S5_BLOCKS, 256, 256), F32), _sds((S5_BLOCKS, 8, 128), F32), _sds((S5_BLOCKS, 8, 128), F32),
                   _sds((1, D_MODEL), F32)],
        scratch_shapes=[pltpu.VMEM((2, 8 * tc, 128), F32), pltpu.VMEM((tc, D_MODEL), F32), bigb, bigb, big, big,
                        pltpu.VMEM((S5_BLOCKS, 8, 256), F32)],
        name="s5_bwd", compiler_params=_params(("arbitrary",)))(
            x, gain, dy2, res, d_skip, cs, rb, rbt, rct, lam_r, lam_i)


def _s5_views(a_re, a_im, log_dt, b_re, b_im):
    return a_re[:, None, :], a_im[:, None, :], log_dt[:, None, None], jnp.swapaxes(b_re, 1, 2), jnp.swapaxes(b_im, 1, 2)


def _s5_factors(a_re, a_im, log_dt):
    lr, li, dt = jnp.minimum(a_re, LAMBDA_RE_MAX), a_im, jnp.exp(log_dt)
    mag, ang = jnp.exp(lr * dt), li * dt
    lbr, lbi = mag * jnp.cos(ang), mag * jnp.sin(ang)
    den = lr * lr + li * li
    fr, fi = ((lbr - 1.0) * lr + lbi * li) / den, (lbi * lr - (lbr - 1.0) * li) / den
    return lr, li, dt, lbr, lbi, fr, fi, den


def s5_prep(a_re, a_im, log_dt, b_re, b_im, c_re, c_im):
    def body(ar_ref, ai_ref, t_ref, br_ref, bi_ref, cr_ref, ci_ref, rb_ref, rbt_ref, rc_ref, rct_ref, lr_ref, li_ref):
        _, _, _, lbr, lbi, fr, fi, _ = _s5_factors(ar_ref[...], ai_ref[...], t_ref[...])
        lr_ref[...] = lbr
        li_ref[...] = lbi
        bre = fr * br_ref[...] - fi * bi_ref[...]
        bim = fr * bi_ref[...] + fi * br_ref[...]
        even = (lax.broadcasted_iota(jnp.int32, (256, S5_STATE), 0) // S5_GROUP) % 2 == 0

        def assemble(re, im):
            re, im = re.reshape(256, S5_STATE), im.reshape(256, S5_STATE)
            return jnp.concatenate([jnp.where(even, re, 0.0), jnp.where(even, 0.0, re), jnp.where(even, im, 0.0),
                                    jnp.where(even, 0.0, im)], axis=1)

        for blk in range(S5_BLOCKS):
            sl = slice(16 * blk, 16 * blk + 16)
            rb = assemble(bre[sl], bim[sl])
            rct = assemble(cr_ref[sl], -ci_ref[sl])
            rb_ref[blk] = rb.astype(BF16)
            rbt_ref[blk] = rb.T.astype(BF16)
            rct_ref[blk] = rct.astype(BF16)
            rc_ref[blk] = rct.T.astype(BF16)

    vm = pl.BlockSpec(memory_space=pltpu.VMEM)
    mat = _sds((S5_BLOCKS, 256, 256), BF16)
    lam = _sds((S5_GROUPS, 1, S5_STATE), F32)
    rb, rbt, rc, rct, lam_r, lam_i = pl.pallas_call(
        body, in_specs=[vm] * 7, out_specs=[vm] * 6, out_shape=[mat, mat, mat, mat, lam, lam], name="s5_prep",
        compiler_params=_params())(*_s5_views(a_re, a_im, log_dt, b_re, b_im), c_re, c_im)
    return rb, rbt, rc, rct, lam_r.reshape(S5_BLOCKS, 8, 128), lam_i.reshape(S5_BLOCKS, 8, 128)


def s5_param_bwd(mats, lams, a_re, a_im, log_dt, b_re, b_im):
    def body(m_ref, glr_ref, gli_ref, ar_ref, ai_ref, t_ref, br_ref, bi_ref,
             dar_ref, dai_ref, dt_ref, dbr_ref, dbi_ref, dcr_ref, dci_ref):
        lr, li, dt, lbr, lbi, fr, fi, den = _s5_factors(ar_ref[...], ai_ref[...], t_ref[...])
        shape = (S5_GROUPS, S5_GROUP, S5_STATE)
        gbr, gbi = m_ref[0:1024, 0:64].reshape(shape), m_ref[0:1024, 64:128].reshape(shape)
        dcr_ref[...] = m_ref[1024:2048, 0:64].reshape(shape)
        dci_ref[...] = -m_ref[1024:2048, 64:128].reshape(shape)
        br, bi = br_ref[...], bi_ref[...]
        dbr_ref[...] = fr * gbr + fi * gbi
        dbi_ref[...] = fr * gbi - fi * gbr
        dfr = jnp.sum(gbr * br + gbi * bi, axis=1, keepdims=True)
        dfi = jnp.sum(gbi * br - gbr * bi, axis=1, keepdims=True)
        nr, ni = (dfr * lr - dfi * li) / den, (dfr * li + dfi * lr) / den
        qr, qi = (fr * lr + fi * li) / den, (fi * lr - fr * li) / den
        lam_r, lam_i = -(dfr * qr + dfi * qi), -(dfi * qr - dfr * qi)
        gr, gi = glr_ref[...] + nr, gli_ref[...] + ni
        zr, zi = gr * lbr + gi * lbi, gi * lbr - gr * lbi
        a = ar_ref[...]
        dar_ref[...] = (lam_r + zr * dt) * jnp.where(a < LAMBDA_RE_MAX, 1.0, jnp.where(a == LAMBDA_RE_MAX, 0.5, 0.0))
        dai_ref[...] = lam_i + zi * dt
        dt_ref[...] = jnp.sum(zr * lr + zi * li, axis=2, keepdims=True) * dt

    vm = pl.BlockSpec(memory_space=pltpu.VMEM)
    state = _sds((S5_GROUPS, 1, S5_STATE), F32)
    wide = _sds((S5_GROUPS, S5_GROUP, S5_STATE), F32)
    glr = lams[0:32].reshape(S5_GROUPS, 1, S5_STATE)
    gli = lams[32:64].reshape(S5_GROUPS, 1, S5_STATE)
    dar, dai, ddt, dbr, dbi, dcr, dci = pl.pallas_call(
        body, in_specs=[vm] * 8, out_specs=[vm] * 7,
        out_shape=[state, state, _sds((S5_GROUPS, 1, 1), F32), wide, wide, wide, wide], name="s5_param_bwd",
        compiler_params=_params())(mats, glr, gli, *_s5_views(a_re, a_im, log_dt, b_re, b_im))
    return (dar.reshape(S5_GROUPS, S5_STATE), dai.reshape(S5_GROUPS, S5_STATE), ddt.reshape(S5_GROUPS),
            jnp.swapaxes(dbr, 1, 2), jnp.swapaxes(dbi, 1, 2), dcr, dci)


def s5_compact(drb, drct, dlr, dli):
    def body(drb_ref, drct_ref, dlr_ref, dli_ref, o_ref, lam_ref):
        even = (lax.broadcasted_iota(jnp.int32, (256, 64), 0) // S5_GROUP) % 2 == 0
        for blk in range(S5_BLOCKS):
            for k, ref in enumerate((drb_ref, drct_ref)):
                m = ref[blk]
                re = jnp.where(even, m[:, 0:64], m[:, 64:128])
                im = jnp.where(even, m[:, 128:192], m[:, 192:256])
                o_ref[pl.ds(k * 1024 + blk * 256, 256), :] = jnp.concatenate([re, im], axis=1)
            lam_ref[pl.ds(blk * 8, 8), :] = dlr_ref[blk]
            lam_ref[pl.ds(32 + blk * 8, 8), :] = dli_ref[blk]

    vm = pl.BlockSpec(memory_space=pltpu.VMEM)
    return pl.pallas_call(body, in_specs=[vm] * 4, out_specs=[vm, vm], out_shape=[_sds((2048, 128), F32), _sds((64, 128), F32)],
                          name="s5_compact", compiler_params=_params())(drb, drct, dlr, dli)


NEG = -1e30


GROUP = N_Q // N_KV


def _attn_masks(n):
    qi = lax.broadcasted_iota(jnp.int32, (GROUP * BLOCK, BLOCK), 0) % BLOCK
    kj = lax.broadcasted_iota(jnp.int32, (GROUP * BLOCK, BLOCK), 1)
    return jnp.logical_and(kj > qi, n > 0), kj <= qi


def _stack_heads(ref, kh):
    return jnp.concatenate([ref[:, (GROUP * kh + g) * HEAD_DIM:(GROUP * kh + g + 1) * HEAD_DIM] for g in range(GROUP)], axis=0)


def _unstack_heads(val):
    return jnp.concatenate([val[g * BLOCK:(g + 1) * BLOCK] for g in range(GROUP)], axis=1)


def _sink_column(sink_ref, kh):
    grp = lax.broadcasted_iota(jnp.int32, (GROUP * BLOCK, 1), 0) // BLOCK
    col = jnp.zeros((GROUP * BLOCK, 1), F32)
    for g in range(GROUP):
        col = jnp.where(grp == g, sink_ref[GROUP * kh + g], col)
    return col, grp


def _attn_exp(q4, kp, kc, sink, mask_p, mask_c):
    scale = 1.0 / math.sqrt(HEAD_DIM)
    nt = (((1,), (1,)), ((), ()))
    sp = jnp.where(mask_p, lax.dot_general(q4, kp, nt, preferred_element_type=F32) * scale, NEG)
    sc = jnp.where(mask_c, lax.dot_general(q4, kc, nt, preferred_element_type=F32) * scale, NEG)
    m = jnp.maximum(jnp.maximum(jnp.max(sp, axis=-1, keepdims=True), jnp.max(sc, axis=-1, keepdims=True)), sink)
    pp = jnp.exp(sp - m)
    pc = jnp.exp(sc - m)
    ps = jnp.exp(sink - m)
    inv = 1.0 / (jnp.sum(pp, axis=-1, keepdims=True) + jnp.sum(pc, axis=-1, keepdims=True) + ps)
    return pp, pc, ps, inv


def attn_fwd(q, kv, sinks):
    n_rows = q.shape[0]
    nb = n_rows // BLOCK

    def body(sink_ref, q_ref, kvp_ref, kvc_ref, o_ref):
        n = pl.program_id(0)
        mask_p, mask_c = _attn_masks(n)
        outs = []
        for kh in range(N_KV):
            ks, vs = slice(kh * HEAD_DIM, (kh + 1) * HEAD_DIM), slice((N_KV + kh) * HEAD_DIM, (N_KV + kh + 1) * HEAD_DIM)
            sink, _ = _sink_column(sink_ref, kh)
            pp, pc, _, inv = _attn_exp(_stack_heads(q_ref, kh), kvp_ref[:, ks], kvc_ref[:, ks], sink, mask_p, mask_c)
            o4 = (jnp.dot(pp.astype(BF16), kvp_ref[:, vs], preferred_element_type=F32)
                  + jnp.dot(pc.astype(BF16), kvc_ref[:, vs], preferred_element_type=F32)) * inv
            outs.append(_unstack_heads(o4))
        o_ref[...] = jnp.concatenate(outs, axis=1).astype(BF16)

    kvw = 2 * N_KV * HEAD_DIM
    return pl.pallas_call(
        body, grid=(nb,),
        in_specs=[pl.BlockSpec(memory_space=pltpu.SMEM), pl.BlockSpec((BLOCK, D_MODEL), lambda n: (n, 0)),
                  pl.BlockSpec((BLOCK, kvw), lambda n: (jnp.maximum(n - 1, 0), 0)), pl.BlockSpec((BLOCK, kvw), lambda n: (n, 0))],
        out_specs=pl.BlockSpec((BLOCK, D_MODEL), lambda n: (n, 0)), out_shape=_sds((n_rows, D_MODEL), BF16),
        name="attn_fwd", compiler_params=_params(("parallel",)))(sinks, q, kv, kv)


def attn_bwd(q, kv, do, sinks):
    n_rows = q.shape[0]
    nb = n_rows // BLOCK
    kvw = 2 * N_KV * HEAD_DIM
    tn = (((0,), (0,)), ((), ()))
    nt = (((1,), (1,)), ((), ()))
    scale = 1.0 / math.sqrt(HEAD_DIM)

    def body(sink_ref, q_ref, kvp_ref, kvc_ref, do_ref, dq_ref, dbq_ref, dprev_ref, dcur_ref, dsink_ref):
        n = pl.program_id(0)
        mask_p, mask_c = _attn_masks(n)
        lane = lax.broadcasted_iota(jnp.int32, (1, D_MODEL), 1)
        dqs, dsink = [], jnp.zeros((1, D_MODEL), F32)
        dkp, dkc, dvp, dvc = [], [], [], []
        for kh in range(N_KV):
            ks, vs = slice(kh * HEAD_DIM, (kh + 1) * HEAD_DIM), slice((N_KV + kh) * HEAD_DIM, (N_KV + kh + 1) * HEAD_DIM)
            q4, do4 = _stack_heads(q_ref, kh), _stack_heads(do_ref, kh)
            kp, kc, vp, vc = kvp_ref[:, ks], kvc_ref[:, ks], kvp_ref[:, vs], kvc_ref[:, vs]
            sink, grp = _sink_column(sink_ref, kh)
            pp, pc, ps, inv = _attn_exp(q4, kp, kc, sink, mask_p, mask_c)
            pp, pc = pp * inv, pc * inv
            dpp = lax.dot_general(do4, vp, nt, preferred_element_type=F32)
            dpc = lax.dot_general(do4, vc, nt, preferred_element_type=F32)
            delta = jnp.sum(pp * dpp, axis=-1, keepdims=True) + jnp.sum(pc * dpc, axis=-1, keepdims=True)
            dsp = (pp * (dpp - delta) * scale).astype(BF16)
            dsc = (pc * (dpc - delta) * scale).astype(BF16)
            dsk = ps * inv * delta
            for g in range(GROUP):
                dsink = dsink + jnp.where(lane == GROUP * kh + g, -jnp.sum(jnp.where(grp == g, dsk, 0.0)), 0.0)
            dqs.append(_unstack_heads(jnp.dot(dsp, kp, preferred_element_type=F32)
                                      + jnp.dot(dsc, kc, preferred_element_type=F32)))
            dkp.append(lax.dot_general(dsp, q4, tn, preferred_element_type=F32))
            dkc.append(lax.dot_general(dsc, q4, tn, preferred_element_type=F32))
            dvp.append(lax.dot_general(pp.astype(BF16), do4, tn, preferred_element_type=F32))
            dvc.append(lax.dot_general(pc.astype(BF16), do4, tn, preferred_element_type=F32))
        dq = jnp.concatenate(dqs, axis=1)
        dq_ref[...] = dq.astype(BF16)
        dprev_ref[0] = jnp.concatenate(dkp + dvp, axis=1)
        dcur_ref[0] = jnp.concatenate(dkc + dvc, axis=1)

        @pl.when(n == 0)
        def _():
            dbq_ref[...] = jnp.zeros_like(dbq_ref)
            dsink_ref[...] = jnp.zeros_like(dsink_ref)

        dbq_ref[...] += jnp.sum(dq, axis=0, keepdims=True)
        dsink_ref[...] += dsink

    blk = pl.BlockSpec((BLOCK, D_MODEL), lambda n: (n, 0))
    part = pl.BlockSpec((1, BLOCK, kvw), lambda n: (n, 0, 0))
    return pl.pallas_call(
        body, grid=(nb,),
        in_specs=[pl.BlockSpec(memory_space=pltpu.SMEM), blk,
                  pl.BlockSpec((BLOCK, kvw), lambda n: (jnp.maximum(n - 1, 0), 0)), pl.BlockSpec((BLOCK, kvw), lambda n: (n, 0)), blk],
        out_specs=[blk, pl.BlockSpec((1, D_MODEL), lambda n: (0, 0)), part, part, pl.BlockSpec((1, D_MODEL), lambda n: (0, 0))],
        out_shape=[_sds((n_rows, D_MODEL), BF16), _sds((1, D_MODEL), F32), _sds((nb, BLOCK, kvw), F32),
                   _sds((nb, BLOCK, kvw), F32), _sds((1, D_MODEL), F32)],
        name="attn_bwd", compiler_params=_params(("arbitrary",)))(sinks, q, kv, kv, do)


def kv_combine(dprev, dcur):
    nb, _, kvw = dprev.shape

    def body(dcur_ref, dprev_ref, dkv_ref, db_ref):
        total = jnp.zeros((1, kvw), F32)
        for m in range(nb):
            dkv = dcur_ref[m] + dprev_ref[m + 1] if m + 1 < nb else dcur_ref[m]
            dkv_ref[m * BLOCK:(m + 1) * BLOCK, :] = dkv.astype(BF16)
            total = total + jnp.sum(dkv, axis=0, keepdims=True)
        db_ref[...] = jnp.concatenate([total, jnp.zeros((1, D_MODEL - kvw), F32)], axis=1)

    vm = pl.BlockSpec(memory_space=pltpu.VMEM)
    return pl.pallas_call(body, in_specs=[vm, vm], out_specs=[vm, vm],
                          out_shape=[_sds((nb * BLOCK, kvw), BF16), _sds((1, D_MODEL), F32)], name="kv_combine",
                          compiler_params=_params())(dcur, dprev)


def glu_bwd(dout, val, gate, tm=256):
    n_rows, d = dout.shape

    def body(do_ref, v_ref, g_ref, dz_ref, db_ref):
        i = pl.program_id(0)
        sg = jax.nn.sigmoid(g_ref[...])
        dval = do_ref[...] * sg
        dgate = do_ref[...] * v_ref[...] * sg * (1.0 - sg)
        dz_ref[...] = jnp.concatenate([dval, dgate], axis=1).astype(BF16)

        @pl.when(i == 0)
        def _():
            db_ref[...] = jnp.zeros_like(db_ref)

        db_ref[0:1, :] += jnp.sum(dval, axis=0, keepdims=True)
        db_ref[1:2, :] += jnp.sum(dgate, axis=0, keepdims=True)

    row = pl.BlockSpec((tm, d), lambda i: (i, 0))
    return pl.pallas_call(
        body, grid=(n_rows // tm,), in_specs=[row, row, row],
        out_specs=[pl.BlockSpec((tm, 2 * d), lambda i: (i, 0)), pl.BlockSpec((2, d), lambda i: (0, 0))],
        out_shape=[_sds((n_rows, 2 * d), BF16), _sds((2, d), F32)],
        name="glu_bwd", compiler_params=_params(("arbitrary",)))(dout, val, gate)


def _adam_update(w, g, m, v):
    nm = ADAM_B1 * m + (1.0 - ADAM_B1) * g
    nv = ADAM_B2 * v + (1.0 - ADAM_B2) * (g * g)
    m_hat = nm / (1.0 - ADAM_B1 ** ADAM_STEP)
    v_hat = nv / (1.0 - ADAM_B2 ** ADAM_STEP)
    return -ADAM_LR * (m_hat / (jnp.sqrt(v_hat) + ADAM_EPS) + ADAM_WD * w), nm, nv


def adamw(name, ws, gs, ms, vs, steps=8):
    n = len(ws)

    def body(*refs):
        for k in range(n):
            w_ref, g_ref, m_ref, v_ref = (refs[j * n + k] for j in range(4))
            go_ref, d_ref, nm_ref, nv_ref = (refs[(4 + j) * n + k] for j in range(4))
            gv = g_ref[...]
            go_ref[...] = gv
            d_ref[...], nm_ref[...], nv_ref[...] = _adam_update(w_ref[...], gv, m_ref[...], v_ref[...])

    specs = [pl.BlockSpec((w.shape[0] // steps, w.shape[1]), lambda i: (i, 0)) for w in ws]
    shapes = [_sds(w.shape, F32) for w in ws]
    out = pl.pallas_call(
        body, grid=(steps,), in_specs=specs * 4, out_specs=specs * 4, out_shape=shapes * 4, name=name,
        compiler_params=_params(("parallel",)))(*ws, *gs, *ms, *vs)
    return [list(out[j * n:(j + 1) * n]) for j in range(4)]


def adamw_native(name, ws, gs, ms, vs):
    n = len(ws)

    def body(*refs):
        w_refs, g_refs, m_refs, v_refs = refs[:n], refs[n:2 * n], refs[2 * n:3 * n], refs[3 * n:4 * n]
        d_refs, nm_refs, nv_refs = refs[4 * n:5 * n], refs[5 * n:6 * n], refs[6 * n:7 * n]
        for k in range(n):
            dl, nm, nv = _adam_update(w_refs[k][...], g_refs[k][...], m_refs[k][...], v_refs[k][...])
            d_refs[k][...] = dl
            nm_refs[k][...] = nm
            nv_refs[k][...] = nv

    vm = pl.BlockSpec(memory_space=pltpu.VMEM)
    shapes = [_sds(w.shape, F32) for w in ws]
    out = pl.pallas_call(body, in_specs=[vm] * (4 * n), out_specs=[vm] * (3 * n), out_shape=shapes * 3, name=name,
                         compiler_params=_params())(*ws, *gs, *ms, *vs)
    return list(out[:n]), list(out[n:2 * n]), list(out[2 * n:])


VEC_ROWS = {"norm_mix": 0, "norm_mlp": 2, "norm_kv": 4, "norm_final": 5, "s5_d": 6, "b_q": 7, "b_o": 8, "s5_b_glu": 9,
            "b_kv": 11, "sinks": 12, "loss": 13}


def split_vectors(where, vecs, d_shard, glu_shard):
    kvw = 2 * N_KV * HEAD_DIM
    shapes = {"norm_mix": (2, D_MODEL), "norm_mlp": (2, D_MODEL), "norm_kv": (1, D_MODEL), "norm_final": (1, D_MODEL),
              "s5_d": (1, d_shard), "b_q": (1, D_MODEL), "b_o": (1, D_MODEL), "s5_b_glu": (1, glu_shard), "b_kv": (1, kvw),
              "sinks": (1, N_Q), "loss": (1, 128)}
    names = list(shapes)

    def body(where_ref, v_ref, *o_refs):
        chip = where_ref[1]
        for name, o_ref in zip(names, o_refs):
            r0, (r, n) = VEC_ROWS[name], shapes[name]
            if name == "s5_d":
                g = jnp.zeros((1, n), F32)
                for j in range(4):
                    g = jnp.where(chip == j, v_ref[r0:r0 + 1, j * n:(j + 1) * n], g)
            elif name == "s5_b_glu":
                g = jnp.zeros((1, n), F32)
                for j in range(4):
                    row, col = r0 + (j * n) // D_MODEL, (j * n) % D_MODEL
                    g = jnp.where(chip == j, v_ref[row:row + 1, col:col + n], g)
            else:
                g = v_ref[r0:r0 + r, 0:n]
            o_ref[...] = g

    vm = pl.BlockSpec(memory_space=pltpu.VMEM)
    out = pl.pallas_call(body, in_specs=[pl.BlockSpec(memory_space=pltpu.SMEM), vm], out_specs=[vm] * len(names),
                         out_shape=[_sds(shapes[n], F32) for n in names], name="split_vectors",
                         compiler_params=_params())(where, vecs)
    return dict(zip(names, out))


def _position():
    x, y, c = lax.axis_index("x"), lax.axis_index("y"), lax.axis_index("c")
    others = [(1 - x, y), (x, 1 - y), (1 - x, 1 - y)]
    return x, y, c, others


def _window(ref, kind, chip, half, shard_shape):
    if kind == "slab":
        return ref.at[chip]
    r, n = shard_shape
    if kind == "col":
        return ref.at[pl.ds(pl.multiple_of(half * (r // 2), 16), r // 2), pl.ds(pl.multiple_of(chip * n, 128), n)]
    return ref.at[pl.ds(pl.multiple_of(chip * r, 16), r), pl.ds(pl.multiple_of(half * (n // 2), 128), n // 2)]


def _half(ref, kind, half, shape):
    r, n = shape
    if kind == "col":
        return ref.at[pl.ds(pl.multiple_of(half * (r // 2), 16), r // 2), :]
    return ref.at[:, pl.ds(pl.multiple_of(half * (n // 2), 128), n // 2)]


def swap_start(name, grads, kinds, carry):
    nt = len(grads)
    shapes = [tuple(g.shape) for g in grads]
    lands = [lax.empty(sh, BF16) for sh in shapes]
    given, given_specs, token_type, write = _hand_through(carry)
    n_in = 2 * nt + len(given)

    def body(*refs):
        in_refs, land_refs = refs[:nt], refs[nt:2 * nt]
        send_sems, recv_sems, token = refs[n_in], refs[n_in + 1], refs[-1]
        x, y, c, _ = _position()
        for t in range(nt):
            pltpu.make_async_remote_copy(
                src_ref=_half(in_refs[t], kinds[t], 1 - c, shapes[t]), dst_ref=_half(land_refs[t], kinds[t], 1 - c, shapes[t]),
                send_sem=send_sems.at[t], recv_sem=recv_sems.at[t], device_id=(x, y, 1 - c), device_id_type=MESH).start()
        write(token, refs[:n_in])

    sems = pltpu.SemaphoreType.DMA((nt,))
    both = list(grads) + lands
    out = pl.pallas_call(
        body, name=name, in_specs=[HBM_SPEC] * (2 * nt) + given_specs,
        out_specs=(SEM_SPEC, SEM_SPEC, *[HBM_SPEC] * (2 * nt), pl.BlockSpec(memory_space=pltpu.VMEM)),
        out_shape=(sems, sems, *[pltpu.HBM(a.shape, a.dtype) for a in both], token_type),
        input_output_aliases={t: 2 + t for t in range(2 * nt)}, compiler_params=_split_params(),
    )(*[_in_hbm(a) for a in both], *given)
    return out[0], out[1], list(out[2:2 + nt]), list(out[2 + nt:2 + 2 * nt]), out[-1]


def swap_wait(name, send_sems, recv_sems, grads, lands, kinds, after):
    nt = len(grads)
    shapes = [tuple(g.shape) for g in grads]

    def body(*refs):
        in_refs, land_refs = refs[:nt], refs[nt:2 * nt]
        send_ref, recv_ref = refs[2 * nt], refs[2 * nt + 1]
        x, y, c, _ = _position()
        for t in range(nt):
            cp = pltpu.make_async_remote_copy(
                src_ref=_half(in_refs[t], kinds[t], 1 - c, shapes[t]), dst_ref=_half(land_refs[t], kinds[t], c, shapes[t]),
                send_sem=send_ref.at[t], recv_sem=recv_ref.at[t], device_id=(x, y, 1 - c), device_id_type=MESH)
            cp.wait_send()
            cp.wait_recv()

    both = list(grads) + list(lands)
    out = pl.pallas_call(
        body, name=name, in_specs=[HBM_SPEC] * (2 * nt) + [SEM_SPEC, SEM_SPEC, HBM_SPEC], out_specs=[HBM_SPEC] * (2 * nt),
        out_shape=[pltpu.HBM(a.shape, a.dtype) for a in both], input_output_aliases={t: t for t in range(2 * nt)},
        compiler_params=_split_params())(*both, send_sems, recv_sems, _in_hbm(after))
    return list(out[:nt]), list(out[nt:])


def _half_spec(kind, shape, tiles):
    r, n = shape
    if kind == "col":
        tn = n // tiles
        return pl.BlockSpec((r // 2, tn), lambda i, s: (s[0], i))
    tm = r // tiles
    return pl.BlockSpec((tm, n // 2), lambda i, s: (i, s[0]))


def add_halves(name, mine, landed, kinds, where, tiles=4):
    nt = len(mine)
    shapes = [tuple(a.shape) for a in mine]

    def compact(t):
        r, n = shapes[t]
        if kinds[t] == "col":
            return (r // 2, n), pl.BlockSpec((r // 2, n // tiles), lambda i, s: (0, i))
        return (r, n // 2), pl.BlockSpec((r // tiles, n // 2), lambda i, s: (i, 0))

    def body(s_ref, *refs):
        for a_ref, b_ref, o_ref in zip(refs[:nt], refs[nt:2 * nt], refs[2 * nt:]):
            o_ref[...] = (a_ref[...].astype(F32) + b_ref[...].astype(F32)).astype(BF16)

    specs = [_half_spec(kinds[t], shapes[t], tiles) for t in range(nt)]
    return pl.pallas_call(
        body, grid_spec=pltpu.PrefetchScalarGridSpec(num_scalar_prefetch=1, grid=(tiles,), in_specs=specs + specs,
                                                     out_specs=[compact(t)[1] for t in range(nt)]),
        out_shape=[_sds(compact(t)[0], BF16) for t in range(nt)], name=name,
        compiler_params=_params(("parallel",)))(where, *mine, *landed)


def sum_shards(name, parts, landed, kinds, shard_shapes, where, layers, n_layers, intos, tiles=2):
    nt = len(parts)
    in_specs, out_specs = [], []
    for t in range(nt):
        (r, n), layer = shard_shapes[t], layers[t]
        if kinds[t] == "col":
            tm, width = r // 2 // tiles, n
            own = pl.BlockSpec((tm, n), lambda i, s: (i, s[1]))
            out = pl.BlockSpec((None, tm, n), lambda i, s, layer=layer: (layer, s[0] * tiles + i, 0))
        else:
            tm, width = r // tiles, n // 2
            own = pl.BlockSpec((tm, n // 2), lambda i, s: (s[1] * tiles + i, 0))
            out = pl.BlockSpec((None, tm, n // 2), lambda i, s, layer=layer: (layer, i, s[0]))
        in_specs += [own, pl.BlockSpec((3, tm, width), lambda i, s: (0, i, 0))]
        out_specs.append(out)
    args, aliases = [where] + [a for pair in zip(parts, landed) for a in pair], {}
    for t in range(nt):
        if intos[t] is not None:
            aliases[len(args)] = t
            in_specs.append(pl.BlockSpec(memory_space=pl.ANY))
            args.append(intos[t])

    def body(s_ref, *refs):
        for t in range(nt):
            a_ref, l_ref, o_ref = refs[2 * t], refs[2 * t + 1], refs[len(in_specs) + t]
            o_ref[...] = ((a_ref[...].astype(F32) + l_ref[0].astype(F32)) + l_ref[1].astype(F32)) + l_ref[2].astype(F32)

    return pl.pallas_call(
        body, grid_spec=pltpu.PrefetchScalarGridSpec(num_scalar_prefetch=1, grid=(tiles,), in_specs=in_specs,
                                                     out_specs=out_specs),
        out_shape=[_sds((n_layers[t],) + tuple(shard_shapes[t]), F32) for t in range(nt)], input_output_aliases=aliases,
        name=name, compiler_params=_params(("parallel",)))(*args)


def share_start(arrays, entries, carry):
    na, nt = len(arrays), len(entries)
    given, given_specs, token_type, write = _hand_through(carry)
    n_in = na + len(given)

    def body(*refs):
        in_refs, send_sems, recv_sems, token = refs[:na], refs[n_in], refs[n_in + 1], refs[-1]
        x, y, c, _ = _position()
        for t, (a, layer, kind) in enumerate(entries):
            mine = _half(in_refs[a].at[layer], kind, c, tuple(arrays[a].shape[1:]))
            pltpu.make_async_remote_copy(
                src_ref=mine, dst_ref=mine, send_sem=send_sems.at[t], recv_sem=recv_sems.at[t],
                device_id=(x, y, 1 - c), device_id_type=MESH).start()
        write(token, refs[:n_in])

    sems = pltpu.SemaphoreType.DMA((nt,))
    out = pl.pallas_call(
        body, name="share_start", in_specs=[HBM_SPEC] * na + given_specs,
        out_specs=(SEM_SPEC, SEM_SPEC, *[HBM_SPEC] * na, pl.BlockSpec(memory_space=pltpu.VMEM)),
        out_shape=(sems, sems, *[pltpu.HBM(a.shape, a.dtype) for a in arrays], token_type),
        input_output_aliases={t: 2 + t for t in range(na)}, compiler_params=_split_params(),
    )(*[_in_hbm(a) for a in arrays], *given)
    return out[0], out[1], list(out[2:2 + na]), out[-1]


def share_wait(send_sems, recv_sems, arrays, entries, after):
    na = len(arrays)

    def body(*refs):
        in_refs, send_ref, recv_ref = refs[:na], refs[na], refs[na + 1]
        x, y, c, _ = _position()
        for t, (a, layer, kind) in enumerate(entries):
            shape = tuple(arrays[a].shape[1:])
            cp = pltpu.make_async_remote_copy(
                src_ref=_half(in_refs[a].at[layer], kind, c, shape), dst_ref=_half(in_refs[a].at[layer], kind, 1 - c, shape),
                send_sem=send_ref.at[t], recv_sem=recv_ref.at[t], device_id=(x, y, 1 - c), device_id_type=MESH)
            cp.wait_send()
            cp.wait_recv()

    return list(pl.pallas_call(
        body, name="share_wait", in_specs=[HBM_SPEC] * na + [SEM_SPEC, SEM_SPEC, HBM_SPEC], out_specs=[HBM_SPEC] * na,
        out_shape=[pltpu.HBM(a.shape, a.dtype) for a in arrays], input_output_aliases={t: t for t in range(na)},
        compiler_params=_split_params())(*arrays, send_sems, recv_sems, _in_hbm(after)))


HBM_SPEC = pl.BlockSpec(memory_space=pltpu.HBM)
SEM_SPEC = pl.BlockSpec(memory_space=pltpu.SEMAPHORE)
ANY_SPEC = pl.BlockSpec(memory_space=pl.ANY)


def _split_params():
    return pltpu.CompilerParams(has_side_effects=pltpu.SideEffectType.DATAFLOW_SIDE_EFFECTING,
                                vmem_limit_bytes=VMEM_LIMIT_BYTES)


def _in_hbm(a):
    return pltpu.with_memory_space_constraint(a, pltpu.HBM)


def cast_place(arrays, entries, where, tiles=2):
    in_specs, out_specs, fulls = [], [], []
    for a, layer, kind in entries:
        _, r, n = arrays[a].shape
        tm = r // tiles
        in_specs.append(pl.BlockSpec((None, tm, n), lambda i, s, layer=layer: (layer, i, 0)))
        if kind == "col":
            fulls.append((r, 4 * n))
            out_specs.append(pl.BlockSpec((tm, n), lambda i, s: (i, s[1])))
        else:
            fulls.append((4 * r, n))
            out_specs.append(pl.BlockSpec((tm, n), lambda i, s: (s[1] * tiles + i, 0)))
    nt = len(entries)

    def body(s_ref, *refs):
        for w_ref, o_ref in zip(refs[:nt], refs[nt:]):
            o_ref[...] = w_ref[...].astype(BF16)

    return pl.pallas_call(
        body, grid_spec=pltpu.PrefetchScalarGridSpec(num_scalar_prefetch=1, grid=(tiles,), in_specs=in_specs,
                                                     out_specs=out_specs),
        out_shape=[_sds(f, BF16) for f in fulls], name="cast_place",
        compiler_params=_params(("parallel",)))(where, *[arrays[a] for a, _, _ in entries])


def _hand_through(carry):
    given = [] if isinstance(carry, tuple) else [carry]

    def write(token, ins):
        token[...] = ins[-1][...] if given else jnp.zeros_like(token)

    return (given, [pl.BlockSpec(memory_space=pltpu.VMEM)] * len(given),
            _sds(carry if isinstance(carry, tuple) else carry.shape, F32), write)


def gather_start(fulls, kinds, shard_shapes, carry):
    nt = len(fulls)
    given, given_specs, token_type, write = _hand_through(carry)
    n_in = nt + len(given)

    def body(*refs):
        full_refs = refs[:nt]
        send_sems, recv_sems, token = refs[n_in], refs[n_in + 1], refs[-1]
        x, y, c, others = _position()
        for t in range(nt):
            mine = _window(full_refs[t], kinds[t], 2 * x + y, c, shard_shapes[t])
            for j, (ox, oy) in enumerate(others):
                pltpu.make_async_remote_copy(
                    src_ref=mine, dst_ref=mine, send_sem=send_sems.at[3 * t + j], recv_sem=recv_sems.at[3 * t + j],
                    device_id=(ox, oy, c), device_id_type=MESH).start()
        write(token, refs[:n_in])

    sems = pltpu.SemaphoreType.DMA((3 * nt,))
    out = pl.pallas_call(
        body, name="gather_start", in_specs=[HBM_SPEC] * nt + given_specs,
        out_specs=(SEM_SPEC, SEM_SPEC, *[HBM_SPEC] * nt, pl.BlockSpec(memory_space=pltpu.VMEM)),
        out_shape=(sems, sems, *[pltpu.HBM(f.shape, f.dtype) for f in fulls], token_type),
        input_output_aliases={t: 2 + t for t in range(nt)}, compiler_params=_split_params(),
    )(*[_in_hbm(f) for f in fulls], *given)
    return out[0], out[1], list(out[2:2 + nt]), out[-1]


def gather_wait(name, send_sems, recv_sems, fulls, kinds, shard_shapes, after, first):
    nt = len(fulls)
    extra = [] if after is None else [_in_hbm(after)]

    def body(*refs):
        full_refs, send_ref, recv_ref = refs[:nt], refs[nt], refs[nt + 1]
        x, y, c, others = _position()
        for t in range(nt):
            mine = _window(full_refs[t], kinds[t], 2 * x + y, c, shard_shapes[t])
            for j, (ox, oy) in enumerate(others):
                cp = pltpu.make_async_remote_copy(
                    src_ref=mine, dst_ref=_window(full_refs[t], kinds[t], 2 * ox + oy, c, shard_shapes[t]),
                    send_sem=send_ref.at[3 * (first + t) + j], recv_sem=recv_ref.at[3 * (first + t) + j],
                    device_id=(ox, oy, c), device_id_type=MESH)
                cp.wait_send()
                cp.wait_recv()

    out = pl.pallas_call(
        body, name=name, in_specs=[HBM_SPEC] * nt + [SEM_SPEC, SEM_SPEC] + [HBM_SPEC] * len(extra),
        out_specs=[HBM_SPEC] * nt, out_shape=[pltpu.HBM(f.shape, f.dtype) for f in fulls],
        input_output_aliases={t: t for t in range(nt)}, compiler_params=_split_params())(*fulls, send_sems, recv_sems, *extra)
    return list(out)


def forward_halves(name, fulls, kinds, shard_shapes):
    nt = len(fulls)

    def body(*refs):
        out_refs = refs[nt:2 * nt]
        send_sems, recv_sems = refs[2 * nt:]
        x, y, c, others = _position()
        cps = []
        for t in range(nt):
            for j, (ox, oy) in enumerate(others):
                landed = _window(out_refs[t], kinds[t], 2 * ox + oy, c, shard_shapes[t])
                cp = pltpu.make_async_remote_copy(
                    src_ref=landed, dst_ref=landed, send_sem=send_sems.at[3 * t + j], recv_sem=recv_sems.at[3 * t + j],
                    device_id=(x, y, 1 - c), device_id_type=MESH)
                cp.start()
                cps.append(cp)
        for t in range(nt):
            for j, (ox, oy) in enumerate(others):
                got = _window(out_refs[t], kinds[t], 2 * ox + oy, 1 - c, shard_shapes[t])
                pltpu.make_async_remote_copy(
                    src_ref=got, dst_ref=got, send_sem=send_sems.at[3 * t + j], recv_sem=recv_sems.at[3 * t + j],
                    device_id=(x, y, 1 - c), device_id_type=MESH).wait_recv()
        for cp in cps:
            cp.wait_send()

    out = pl.pallas_call(
        body, in_specs=[ANY_SPEC] * nt, out_specs=[ANY_SPEC] * nt, out_shape=[_sds(f.shape, f.dtype) for f in fulls],
        input_output_aliases={t: t for t in range(nt)},
        scratch_shapes=[pltpu.SemaphoreType.DMA((3 * nt,)), pltpu.SemaphoreType.DMA((3 * nt,))],
        name=name, compiler_params=_params())(*fulls)
    return list(out)


def forward_start(name, send_sems, recv_sems, fulls, kinds, shard_shapes, after, first, carry, passing=()):
    nt, n_pass = len(fulls), len(passing)
    given, given_specs, token_type, write = _hand_through(carry)
    n_in = nt + 3 + n_pass + len(given)

    def body(*refs):
        full_refs, ici_send, ici_recv = refs[:nt], refs[nt], refs[nt + 1]
        send_ref, recv_ref, token = refs[n_in], refs[n_in + 1], refs[-1]
        x, y, c, others = _position()
        for t in range(nt):
            mine = _window(full_refs[t], kinds[t], 2 * x + y, c, shard_shapes[t])
            for j, (ox, oy) in enumerate(others):
                landed = _window(full_refs[t], kinds[t], 2 * ox + oy, c, shard_shapes[t])
                cp = pltpu.make_async_remote_copy(
                    src_ref=mine, dst_ref=landed, send_sem=ici_send.at[3 * (first + t) + j],
                    recv_sem=ici_recv.at[3 * (first + t) + j], device_id=(ox, oy, c), device_id_type=MESH)
                cp.wait_send()
                cp.wait_recv()
                pltpu.make_async_remote_copy(
                    src_ref=landed, dst_ref=landed, send_sem=send_ref.at[3 * t + j], recv_sem=recv_ref.at[3 * t + j],
                    device_id=(x, y, 1 - c), device_id_type=MESH).start()
        write(token, refs[:n_in])

    sems = pltpu.SemaphoreType.DMA((3 * nt,))
    out = pl.pallas_call(
        body, name=name, in_specs=[HBM_SPEC] * nt + [SEM_SPEC, SEM_SPEC, HBM_SPEC] + [HBM_SPEC] * n_pass + given_specs,
        out_specs=(SEM_SPEC, SEM_SPEC, *[HBM_SPEC] * (nt + n_pass), pl.BlockSpec(memory_space=pltpu.VMEM)),
        out_shape=(sems, sems, *[pltpu.HBM(f.shape, f.dtype) for f in [*fulls, *passing]], token_type),
        input_output_aliases={**{t: 2 + t for t in range(nt)}, **{nt + 3 + t: 2 + nt + t for t in range(n_pass)}},
        compiler_params=_split_params(),
    )(*fulls, send_sems, recv_sems, _in_hbm(after), *passing, *given)
    return out[0], out[1], list(out[2:2 + nt]), list(out[2 + nt:2 + nt + n_pass]), out[-1]


def forward_wait(name, send_sems, recv_sems, fulls, kinds, shard_shapes, after):
    nt = len(fulls)

    def body(*refs):
        full_refs, send_ref, recv_ref = refs[:nt], refs[nt], refs[nt + 1]
        x, y, c, others = _position()
        for t in range(nt):
            for j, (ox, oy) in enumerate(others):
                cp = pltpu.make_async_remote_copy(
                    src_ref=_window(full_refs[t], kinds[t], 2 * ox + oy, c, shard_shapes[t]),
                    dst_ref=_window(full_refs[t], kinds[t], 2 * ox + oy, 1 - c, shard_shapes[t]),
                    send_sem=send_ref.at[3 * t + j], recv_sem=recv_ref.at[3 * t + j],
                    device_id=(x, y, 1 - c), device_id_type=MESH)
                cp.wait_send()
                cp.wait_recv()

    return list(pl.pallas_call(
        body, name=name, in_specs=[HBM_SPEC] * nt + [SEM_SPEC, SEM_SPEC, HBM_SPEC], out_specs=[HBM_SPEC] * nt,
        out_shape=[pltpu.HBM(f.shape, f.dtype) for f in fulls], input_output_aliases={t: t for t in range(nt)},
        compiler_params=_split_params())(*fulls, send_sems, recv_sems, _in_hbm(after)))


def _piece(ref, kind, chip, shard_shape):
    r, n = shard_shape
    if kind == "col":
        return ref.at[:, pl.ds(pl.multiple_of(chip * n, 128), n)]
    return ref.at[pl.ds(pl.multiple_of(chip * r, 16), r), :]


def _piece_shape(kind, shard_shape):
    r, n = shard_shape
    return (r // 2, n) if kind == "col" else (r, n // 2)


def exchange_start(name, parts, kinds, shard_shapes, carry):
    nt = len(parts)
    lands = [lax.empty((3,) + _piece_shape(kinds[t], shard_shapes[t]), BF16) for t in range(nt)]
    given, given_specs, token_type, write = _hand_through(carry)
    n_in = 2 * nt + len(given)

    def body(*refs):
        part_refs, land_refs = refs[:nt], refs[nt:2 * nt]
        send_sems, recv_sems, token = refs[n_in], refs[n_in + 1], refs[-1]
        x, y, c, others = _position()
        for t in range(nt):
            for j, (ox, oy) in enumerate(others):
                pltpu.make_async_remote_copy(
                    src_ref=_piece(part_refs[t], kinds[t], 2 * ox + oy, shard_shapes[t]), dst_ref=land_refs[t].at[j],
                    send_sem=send_sems.at[3 * t + j], recv_sem=recv_sems.at[3 * t + j],
                    device_id=(ox, oy, c), device_id_type=MESH).start()
        write(token, refs[:n_in])

    sems = pltpu.SemaphoreType.DMA((3 * nt,))
    both = list(parts) + lands
    out = pl.pallas_call(
        body, name=name, in_specs=[HBM_SPEC] * (2 * nt) + given_specs,
        out_specs=(SEM_SPEC, SEM_SPEC, *[HBM_SPEC] * (2 * nt), pl.BlockSpec(memory_space=pltpu.VMEM)),
        out_shape=(sems, sems, *[pltpu.HBM(a.shape, a.dtype) for a in both], token_type),
        input_output_aliases={t: 2 + t for t in range(2 * nt)}, compiler_params=_split_params(),
    )(*[_in_hbm(a) for a in both], *given)
    return out[0], out[1], list(out[2:2 + nt]), list(out[2 + nt:2 + 2 * nt]), out[-1]


def exchange_wait(name, send_sems, recv_sems, parts, lands, kinds, shard_shapes, after):
    nt = len(parts)

    def body(*refs):
        part_refs, land_refs = refs[:nt], refs[nt:2 * nt]
        send_ref, recv_ref = refs[2 * nt], refs[2 * nt + 1]
        x, y, c, others = _position()
        for t in range(nt):
            for j, (ox, oy) in enumerate(others):
                cp = pltpu.make_async_remote_copy(
                    src_ref=_piece(part_refs[t], kinds[t], 2 * ox + oy, shard_shapes[t]), dst_ref=land_refs[t].at[j],
                    send_sem=send_ref.at[3 * t + j], recv_sem=recv_ref.at[3 * t + j],
                    device_id=(ox, oy, c), device_id_type=MESH)
                cp.wait_send()
                cp.wait_recv()

    both = list(parts) + list(lands)
    out = pl.pallas_call(
        body, name=name, in_specs=[HBM_SPEC] * (2 * nt) + [SEM_SPEC, SEM_SPEC, HBM_SPEC], out_specs=[HBM_SPEC] * (2 * nt),
        out_shape=[pltpu.HBM(a.shape, a.dtype) for a in both], input_output_aliases={t: t for t in range(2 * nt)},
        compiler_params=_split_params())(*both, send_sems, recv_sems, _in_hbm(after))
    return list(out[:nt]), list(out[nt:])


def reduce_swap(bufs, wire):
    n = len(bufs)
    halves = [b.shape[0] // 2 for b in bufs]

    def body(*refs):
        in_refs, out_refs, txs, got = refs[:n], refs[n:2 * n], refs[2 * n:3 * n], refs[3 * n:4 * n]
        send_sems, recv_sems = refs[4 * n:]
        x, y, c, _ = _position()
        cps = []
        for k in range(n):
            txs[k][...] = in_refs[k][pl.ds(pl.multiple_of((1 - c) * halves[k], 8), halves[k]), :].astype(wire[k])
            cp = pltpu.make_async_remote_copy(src_ref=txs[k], dst_ref=got[k], send_sem=send_sems.at[k],
                                              recv_sem=recv_sems.at[k], device_id=(x, y, 1 - c), device_id_type=MESH)
            cp.start()
            cps.append(cp)
        for k, cp in enumerate(cps):
            cp.wait()
            own = in_refs[k][pl.ds(pl.multiple_of(c * halves[k], 8), halves[k]), :]
            out_refs[k][...] = (own.astype(wire[k]).astype(F32) + got[k][...].astype(F32)).astype(wire[k])

    vm = pl.BlockSpec(memory_space=pltpu.VMEM)
    parts = [((h, b.shape[1]), w) for h, b, w in zip(halves, bufs, wire)]
    return list(pl.pallas_call(
        body, name="reduce_swap", in_specs=[vm] * n, out_specs=[vm] * n, out_shape=[_sds(sh, w) for sh, w in parts],
        scratch_shapes=[pltpu.VMEM(sh, w) for sh, w in parts] * 2 + [pltpu.SemaphoreType.DMA((n,))] * 2,
        compiler_params=_params())(*bufs))


def reduce_start(parts):
    n = len(parts)
    lands = [lax.empty((4,) + tuple(p.shape), p.dtype) for p in parts]

    def body(*refs):
        part_refs, land_refs, send_sems, recv_sems = refs[:n], refs[n:2 * n], refs[2 * n], refs[2 * n + 1]
        x, y, c, others = _position()
        for k in range(n):
            for j, (ox, oy) in enumerate(others):
                pltpu.make_async_remote_copy(
                    src_ref=part_refs[k], dst_ref=land_refs[k].at[2 * x + y], send_sem=send_sems.at[3 * k + j],
                    recv_sem=recv_sems.at[3 * k + j], device_id=(ox, oy, c), device_id_type=MESH).start()

    sems = pltpu.SemaphoreType.DMA((3 * n,))
    both = list(parts) + lands
    out = pl.pallas_call(
        body, name="reduce_start", in_specs=[HBM_SPEC] * (2 * n), out_specs=(SEM_SPEC, SEM_SPEC, *[HBM_SPEC] * (2 * n)),
        out_shape=(sems, sems, *[pltpu.HBM(a.shape, a.dtype) for a in both]),
        input_output_aliases={k: 2 + k for k in range(2 * n)}, compiler_params=_split_params(),
    )(*[_in_hbm(a) for a in both])
    return out[0], out[1], list(out[2:2 + n]), list(out[2 + n:])


def reduce_wait(send_sems, recv_sems, parts, lands, after):
    n = len(parts)

    def body(*refs):
        part_refs, land_refs, send_ref, recv_ref = refs[:n], refs[n:2 * n], refs[2 * n], refs[2 * n + 1]
        x, y, c, others = _position()
        for k in range(n):
            for j, (ox, oy) in enumerate(others):
                cp = pltpu.make_async_remote_copy(
                    src_ref=part_refs[k], dst_ref=land_refs[k].at[2 * ox + oy], send_sem=send_ref.at[3 * k + j],
                    recv_sem=recv_ref.at[3 * k + j], device_id=(ox, oy, c), device_id_type=MESH)
                cp.wait_send()
                cp.wait_recv()

    both = list(parts) + list(lands)
    out = pl.pallas_call(
        body, name="reduce_wait", in_specs=[HBM_SPEC] * (2 * n) + [SEM_SPEC, SEM_SPEC, HBM_SPEC],
        out_specs=[HBM_SPEC] * (2 * n), out_shape=[pltpu.HBM(a.shape, a.dtype) for a in both],
        input_output_aliases={k: k for k in range(2 * n)}, compiler_params=_split_params(),
    )(*both, send_sems, recv_sems, _in_hbm(after))
    return list(out[:n]), list(out[n:])


def reduce_share(parts, lands):
    n = len(parts)
    halves = [p.shape[0] for p in parts]

    def body(*refs):
        part_refs, land_refs, out_refs = refs[:n], refs[n:2 * n], refs[2 * n:3 * n]
        send_sems, recv_sems = refs[3 * n:]
        x, y, c, _ = _position()
        chip = 2 * x + y
        cps = []
        for k in range(n):
            mine = pl.ds(pl.multiple_of(c * halves[k], 8), halves[k])
            own = part_refs[k][...].astype(F32)
            total = jnp.where(chip == 0, own, land_refs[k][0].astype(F32))
            for entry in range(1, 4):
                total = total + jnp.where(chip == entry, own, land_refs[k][entry].astype(F32))
            out_refs[k][mine, :] = total
            cp = pltpu.make_async_remote_copy(
                src_ref=out_refs[k].at[mine], dst_ref=out_refs[k].at[mine], send_sem=send_sems.at[k],
                recv_sem=recv_sems.at[k], device_id=(x, y, 1 - c), device_id_type=MESH)
            cp.start()
            cps.append(cp)
        for cp in cps:
            cp.wait()

    vm = pl.BlockSpec(memory_space=pltpu.VMEM)
    return list(pl.pallas_call(
        body, name="reduce_share", in_specs=[vm] * (2 * n), out_specs=[vm] * n,
        out_shape=[_sds((2 * p.shape[0], p.shape[1]), F32) for p in parts],
        scratch_shapes=[pltpu.SemaphoreType.DMA((n,))] * 2, compiler_params=_params())(*parts, *lands))


def _local_step(x, target, small, need, ahead, emit_swap, emit_exchange):
    d = D_MODEL
    full = {}

    def handed(vec, token):
        return vec if token is None else token

    def token_rows(token):
        return [] if token is None else [token]

    def plus(acc, rows):
        return acc + rows[0] if rows else acc

    rb16, rbt16, rc16, rct16, lr_t, li_t = small["s5_operands"]
    ge, ge_slope, cs = s5_fwd(x, small["norm_mix0"], small["s5_d"], rb16, rc16, lr_t, li_t)
    full.update(need("glu", ge))

    def norm_rows(h, gains):
        xh, _ = _rms_hat(h)
        return [xh * g for g in gains]

    def glu_epilogue(accs, e, r):
        v, gt = accs[0] + r[0], accs[1] + r[1]
        h = e[0] + v * jax.nn.sigmoid(gt)
        return [h, v, gt] + norm_rows(h, r[2:])

    gain_mlp0 = handed(small["norm_mlp0"], ahead("mlp_in0", full["w_glu"], small["norm_mlp0"]))
    h1, val, gate, n1 = mm_nn(
        "glu", ge, full["w_glu"], [0, d], d, glu_epilogue, [F32, F32, F32, BF16], extras=[x],
        rowvecs=[(small["s5_b_glu"], 0), (small["s5_b_glu"], d), (gain_mlp0, 0)], tm=512, tn=d)

    def mlp_fwd(tag, h, n, w_in, get_w_out, next_gains, head=None):
        def in_epilogue(accs, e, rv):
            pos = jnp.maximum(accs[0], 0.0)
            return [pos * pos, 2.0 * pos]

        r, slope = mm_nn("mlp_in" + tag, n, w_in, [0], w_in.shape[1], in_epilogue, [BF16, BF16], tm=2048)
        w_out = get_w_out(r)

        def epilogue(accs, e, rv):
            h_out = e[0] + accs[0]
            return [h_out] + norm_rows(h_out, rv)

        if head is not None:
            return head(r, w_out, h), (n, r, slope)
        outs = mm_nn("mlp_out" + tag, r, w_out, [0], d, epilogue, [F32] + [BF16] * len(next_gains), extras=[h],
                     rowvecs=[(g, 0) for g in next_gains], tm=512, tn=d)
        return outs[0], outs[1:], (n, r, slope)

    full.update(need("mlp_in0", h1))

    def w_out0(after):
        full.update(need("mlp_out0", after))
        return full["w_out0"]

    h2, (nkv, n2), mlp0 = mlp_fwd("0", h1, n1, full["w_in0"], w_out0, [small["norm_kv"], small["norm_mix1"]])

    full.update(need("attn", h2))
    kvw = 2 * N_KV * HEAD_DIM
    (kv,) = mm_nn("kv_proj", nkv, full["w_kv"], [0], kvw, lambda accs, e, r: [accs[0] + r[0]], [BF16],
                  rowvecs=[(small["b_kv"], 0)], tm=2048)
    (q,) = mm_nn("q_proj", n2, full["w_q"], [0], d, lambda accs, e, r: [accs[0] + r[0]], [BF16],
                 rowvecs=[(small["b_q"], 0)], tm=2048)
    sinks = small["sinks"].reshape(N_Q)
    o = attn_fwd(q, kv, sinks)
    def o_epilogue(accs, e, r):
        h_out = e[0] + accs[0] + r[0]
        return [h_out] + norm_rows(h_out, r[1:])

    bias_o = handed(small["b_o"], ahead("mlp_in1", o, small["b_o"]))
    h3, n3 = mm_nn("o_proj", o, full["w_o"], [0], d, o_epilogue, [F32, BF16], extras=[h2],
                   rowvecs=[(bias_o, 0), (small["norm_mlp1"], 0)], tm=512, tn=d)
    full.update(need("mlp_in1", h3, then="mlp_out1"))

    def w_out1(after):
        full.update(need("mlp_out1", after))
        return full["w_out1"]

    def loss_head(r, w_out, h):
        def epilogue(accs, e, rv):
            xh, rr = _rms_hat(e[0] + accs[0])
            err = xh * rv[0] - e[1]
            dy = err * (1.0 / d)
            dxh = dy * rv[0]
            dx = rr * (dxh - xh * jnp.mean(dxh * xh, axis=-1, keepdims=True))
            loss = jnp.full((1, d), 0.5 * jnp.sum(jnp.mean(err * err, axis=-1, keepdims=True)), F32)
            return [dx, dx, loss, jnp.sum(dy * xh, axis=0, keepdims=True)]

        return mm_nn("mlp_out1", r, w_out, [0], d, epilogue, [F32, BF16], extras=[h, target],
                     rowvecs=[(small["norm_final"], 0)], n_sums=2, tm=512, tn=d)

    (dh, dhb, loss_tile, dg_final), mlp1 = mlp_fwd("1", h3, n3, full["w_in1"], w_out1, [], head=loss_head)

    grads_small, grads_full = {"norm_final": dg_final}, {}
    ident = lambda acc, e, r: [plus(acc, r)]
    layer1 = ["w_out1", "w_in1", "w_o", "w_q", "w_kv"]
    layer0 = ["w_out0", "w_in0", "w_glu"]

    def norm_bwd_rows(x_rows, res, dys, gains):
        xh, r = _rms_hat(x_rows)
        dxh = sum(dy * g for dy, g in zip(dys, gains))
        dx = r * (dxh - xh * jnp.mean(dxh * xh, axis=-1, keepdims=True)) + res
        return dx, [jnp.sum(dy * xh, axis=0, keepdims=True) for dy in dys]

    def mlp_bwd(tag, dh, dhb, h_in, gain, w_in, w_out, saved, token=None):
        n, r, slope = saved
        grads_full["w_out" + tag] = mm_tn("dw_out" + tag, r, dhb, tn=1024)
        (da,) = mm_nt("mlp_da" + tag, dhb, w_out, lambda acc, e, rv: [plus(acc * e[0].astype(F32), rv)], [BF16],
                      extras=[slope], rowvecs=token_rows(token), tm=2048)
        grads_full["w_in" + tag] = mm_tn("dw_in" + tag, n, da, tn=1024)

        def epilogue(acc, e, rv):
            dx, dgs = norm_bwd_rows(e[0], e[1], [acc], rv)
            return [dx, dx, jnp.sum(dx, axis=0, keepdims=True)] + dgs

        dx, dxb, colsum, dg = mm_nt("mlp_dn" + tag, da, w_in, epilogue, [F32, BF16], extras=[h_in, dh], rowvecs=[gain],
                                    n_sums=2, tm=512, tk=d)
        grads_small["norm_mlp" + tag] = dg
        return dx, dxb, colsum

    dh3, dh3b, colsum3 = mlp_bwd("1", dh, dhb, h3, small["norm_mlp1"], full["w_in1"], full["w_out1"], mlp1)
    grads_small["b_o"] = colsum3
    grads_full["w_o"] = mm_tn("dw_o", o, dh3b, tn=1024)
    (do,) = mm_nt("attn_do", dh3b, full["w_o"], ident, [BF16], tm=2048)
    dq, dbq, dprev, dcur, dsink = attn_bwd(q, kv, do, sinks)
    dkv, dbkv = kv_combine(dprev, dcur)
    grads_small["b_q"], grads_small["b_kv"], grads_small["sinks"] = dbq, dbkv, dsink
    grads_full["w_q"] = mm_tn("dw_q", n2, dq, tn=1024)
    grads_full["w_kv"] = mm_tn("dw_kv", nkv, dkv, tk=1024)
    token = emit_swap("layer1", {n: grads_full[n] for n in layer1}, (1, d))
    (dnkv,) = mm_nt("kv_dn", dkv, full["w_kv"], ident, [F32], rowvecs=token_rows(token), tm=2048, tk=1024)

    def attn_dn_epilogue(acc, e, rv):
        dx, dgs = norm_bwd_rows(e[0], e[1], [acc, e[2]], rv)
        return [dx, dx] + dgs

    dh2, dh2b, dg_mix1, dg_kv = mm_nt("attn_dn", dq, full["w_q"], attn_dn_epilogue, [F32, BF16], extras=[h2, dh3, dnkv],
                                      rowvecs=[small["norm_mix1"], small["norm_kv"]], n_sums=2, tm=512, tk=d)
    grads_small["norm_mix1"], grads_small["norm_kv"] = dg_mix1, dg_kv
    token = emit_exchange("layer1", dh2b, (1, full["w_out0"].shape[0]))
    dh1, _, _ = mlp_bwd("0", dh2, dh2b, h1, small["norm_mlp0"], full["w_in0"], full["w_out0"], mlp0, token)

    dz, db_glu = glu_bwd(dh1, val, gate)
    grads_small["s5_b_glu"] = db_glu
    grads_full["w_glu"] = mm_tn("dw_glu", ge, dz, tn=1024)
    token = emit_swap("layer0", {n: grads_full[n] for n in layer0}, (1, d))
    (dy2,) = mm_nt("glu_dy", dz, full["w_glu"], lambda acc, e, rv: [plus(acc, rv) * e[0]], [F32], extras=[ge_slope],
                   rowvecs=token_rows(token), tm=1024, tk=1024)
    d_skip = handed(small["s5_d"], emit_exchange("layer0", dy2, small["s5_d"]))
    grad_x, dd, drb, drc, dlr, dli, dg_mix0 = s5_bwd(x, small["norm_mix0"], dy2, dh1, d_skip, cs, rb16, rbt16, rct16, lr_t, li_t)
    grads_small["s5_d"] = dd
    grads_small["s5_mats"] = (drb, drc, dlr, dli)
    grads_small["norm_mix0"] = dg_mix0
    return loss_tile, grad_x, grads_small


SMALL_NAMES = ["norm_mix", "norm_mlp", "norm_kv", "norm_final", "s5_a_re", "s5_a_im", "s5_log_dt", "s5_b_re", "s5_b_im",
               "s5_c_re", "s5_c_im", "s5_d", "s5_b_glu", "b_kv", "b_q", "sinks", "b_o"]
BIG_NAMES = ["s5_w_glu", "w_kv", "w_q", "w_o", "w_mlp_in", "w_mlp_out"]
WEIGHT_ORDER = ["norm_mix", "norm_mlp", "norm_kv", "norm_final", "s5_a_re", "s5_a_im", "s5_log_dt", "s5_b_re", "s5_b_im",
                "s5_c_re", "s5_c_im", "s5_d", "s5_w_glu", "s5_b_glu", "w_kv", "b_kv", "w_q", "b_q", "sinks", "w_o", "b_o",
                "w_mlp_in", "w_mlp_out"]


def kernel(x, norm_mix, norm_mlp, norm_kv, norm_final, s5_a_re, s5_a_im, s5_log_dt, s5_b_re, s5_b_im, s5_c_re, s5_c_im, s5_d, s5_w_glu, s5_b_glu, w_kv, b_kv, w_q, b_q, sinks, w_o, b_o, w_mlp_in, w_mlp_out, loss_target, m_norm_mix, m_norm_mlp, m_norm_kv, m_norm_final, m_s5_a_re, m_s5_a_im, m_s5_log_dt, m_s5_b_re, m_s5_b_im, m_s5_c_re, m_s5_c_im, m_s5_d, m_s5_w_glu, m_s5_b_glu, m_w_kv, m_b_kv, m_w_q, m_b_q, m_sinks, m_w_o, m_b_o, m_w_mlp_in, m_w_mlp_out, v_norm_mix, v_norm_mlp, v_norm_kv, v_norm_final, v_s5_a_re, v_s5_a_im, v_s5_log_dt, v_s5_b_re, v_s5_b_im, v_s5_c_re, v_s5_c_im, v_s5_d, v_s5_w_glu, v_s5_b_glu, v_w_kv, v_b_kv, v_w_q, v_b_q, v_sinks, v_w_o, v_b_o, v_w_mlp_in, v_w_mlp_out):
    env = dict(locals())
    w = {n: env[n] for n in WEIGHT_ORDER}
    mom = {n: env["m_" + n] for n in WEIGHT_ORDER}
    var = {n: env["v_" + n] for n in WEIGHT_ORDER}
    d = D_MODEL
    xi, yi, ci = lax.axis_index("x"), lax.axis_index("y"), lax.axis_index("c")
    chip = 2 * xi + yi
    where = jnp.stack([ci, chip]).astype(jnp.int32)

    dsh, bsh = s5_d.shape[1], s5_b_glu.shape[1]
    packed = jnp.concatenate([s5_d.reshape(-1, 128), s5_b_glu.reshape(-1, 128)])
    n_d, n_b = dsh // 128, bsh // 128
    slab = lax.dynamic_update_slice(jnp.zeros((4, 8, 128), F32), jnp.pad(packed, ((0, 8 - n_d - n_b), (0, 0)))[None],
                                    (chip, 0, 0))

    big = [s5_w_glu, w_kv[None], w_q, w_o, w_mlp_in, w_mlp_out]
    entries = [(0, 0, "col"), (1, 0, "row"), (2, 0, "row"), (3, 0, "row"), (4, 0, "col"), (4, 1, "col"),
               (5, 0, "row"), (5, 1, "row")]
    names = ["w_glu", "w_kv", "w_q", "w_o", "w_in0", "w_in1", "w_out0", "w_out1"]
    kinds = dict(zip(names, [k for _, _, k in entries]))
    shard_shapes = dict(zip(names, [tuple(big[a].shape[1:]) for a, _, _ in entries]))

    placed_w = dict(zip(names, cast_place(big, entries, where)))
    placed_w["vectors"], kinds["vectors"], shard_shapes["vectors"] = slab, "slab", None
    gather_groups = {"glu": ["w_glu"], "mlp_in0": ["w_in0"], "mlp_out0": ["w_out0"], "attn": ["w_kv", "w_q", "w_o"],
                     "mlp_in1": ["w_in1"], "mlp_out1": ["w_out1"]}
    order = ["vectors"] + [n for members in gather_groups.values() for n in members]
    send, recv, thru, log_dt = gather_start([placed_w[n] for n in order], [kinds[n] for n in order],
                                            [shard_shapes[n] for n in order], s5_log_dt)
    started = dict(zip(order, thru))
    (gathered_rows,) = gather_wait("gather_wait_vectors", send, recv, [started["vectors"]], ["slab"], [None], None, 0)
    d_full = gathered_rows[:, 0:n_d].reshape(1, -1)
    bglu_full = gathered_rows[:, n_d:n_d + n_b].reshape(1, -1)

    forwarding = {}

    def ahead(group, after, carry, passing=()):
        members = gather_groups[group]
        ks, shapes = [kinds[n] for n in members], [shard_shapes[n] for n in members]
        d2d_send, d2d_recv, landed, passed, tok = forward_start(
            "forward_start_" + group, send, recv, [started[n] for n in members], ks, shapes, after,
            order.index(members[0]), carry, passing)
        forwarding[group] = (d2d_send, d2d_recv, landed)
        return passed if passing else tok

    def need(group, after, then=None):
        members = gather_groups[group]
        ks, shapes = [kinds[n] for n in members], [shard_shapes[n] for n in members]
        if group in forwarding:
            arrays = forward_wait("forward_wait_" + group, *forwarding[group], ks, shapes, after)
        else:
            landed = gather_wait("gather_wait_" + group, send, recv, [started[n] for n in members], ks, shapes, after,
                                 order.index(members[0]))
            arrays = forward_halves("forward_halves_" + group, landed, ks, shapes)
        if then is not None:
            arrays = ahead(then, after, (8, 128), arrays)
        return dict(zip(members, arrays))

    swapping, exchanging = {}, {}

    def emit_swap(group, partial, carry):
        members = list(partial)
        send, recv, mine, lands, tok = swap_start("swap_start_" + group, [partial[n] for n in members],
                                                  [kinds[n] for n in members], carry)
        swapping[group] = (members, send, recv, mine, lands)
        return tok

    def emit_exchange(group, after, carry):
        members, send, recv, mine, lands = swapping[group]
        ks, shapes = [kinds[n] for n in members], [shard_shapes[n] for n in members]
        mine, landed = swap_wait("swap_wait_" + group, send, recv, mine, lands, ks, after)
        sums = add_halves("add_halves_" + group, mine, landed, ks, where)
        send, recv, parts, lands, tok = exchange_start("exchange_start_" + group, sums, ks, shapes, carry)
        exchanging[group] = (members, send, recv, parts, lands)
        return tok

    s5_args = (s5_a_re[0], s5_a_im[0], log_dt[0], s5_b_re[0], s5_b_im[0])
    small = {
        "norm_mix0": norm_mix[0:1], "norm_mix1": norm_mix[1:2], "norm_mlp0": norm_mlp[0:1], "norm_mlp1": norm_mlp[1:2],
        "norm_kv": norm_kv.reshape(1, d), "norm_final": norm_final.reshape(1, d), "s5_operands": s5_prep(*s5_args, s5_c_re[0], s5_c_im[0]),
        "s5_d": d_full, "s5_b_glu": bglu_full,
        "b_kv": b_kv.reshape(1, -1), "b_q": b_q, "sinks": sinks, "b_o": b_o,
    }
    loss_row, grad_x, gs = _local_step(x[0], loss_target[0], small, need, ahead, emit_swap, emit_exchange)

    mats, lams = s5_compact(*gs["s5_mats"])
    rows = [gs["norm_mix0"], gs["norm_mix1"], gs["norm_mlp0"], gs["norm_mlp1"], gs["norm_kv"], gs["norm_final"], gs["s5_d"],
            gs["b_q"], gs["b_o"], gs["s5_b_glu"], gs["b_kv"], gs["sinks"], loss_row, jnp.zeros((2, d), F32)]
    small_send, small_recv, small_parts, small_lands = reduce_start(
        reduce_swap([jnp.concatenate(rows, axis=0), lams, mats], [F32, F32, BF16]))

    reduced = [None] * len(big)
    where_of = dict(zip(names, entries))
    for group in ("layer1", "layer0"):
        members, send, recv, parts, lands = exchanging[group]
        ks, shapes = [kinds[n] for n in members], [shard_shapes[n] for n in members]
        parts, lands = exchange_wait("exchange_wait_" + group, send, recv, parts, lands, ks, shapes, small_lands[-1])
        targets = [where_of[n][0] for n in members]
        sums = sum_shards("sum_shards_" + group, parts, lands, ks, shapes, where, [where_of[n][1] for n in members],
                          [big[a].shape[0] for a in targets], [reduced[a] for a in targets])
        for a, arr in zip(targets, sums):
            reduced[a] = arr
    share_send, share_recv, reduced, _ = share_start(reduced, entries, (8, 128))

    vecs, lams, mats = reduce_share(*reduce_wait(small_send, small_recv, small_parts, small_lands, reduced[0]))
    grads = split_vectors(where, vecs, dsh, bsh)
    loss = grads.pop("loss")[0, 0]
    g_are, g_aim, g_dt, g_bre, g_bim, dc_re, dc_im = s5_param_bwd(mats, lams, *s5_args)
    grads.update({"s5_a_re": g_are[None], "s5_a_im": g_aim[None], "s5_log_dt": g_dt[None], "s5_b_re": g_bre[None],
                  "s5_b_im": g_bim[None], "s5_c_re": dc_re[None], "s5_c_im": dc_im[None]})

    delta, new_m, new_v = {}, {}, {}

    def view(n, a):
        return a.reshape(1, -1) if a.ndim == 1 else jnp.swapaxes(a, -1, -2) if n in ("s5_b_re", "s5_b_im") else a

    sw, sg, sm, sv = ([view(n, t[n]) for n in SMALL_NAMES] for t in (w, grads, mom, var))
    for n, a, b, c_ in zip(SMALL_NAMES, *adamw_native("adamw_small", sw, sg, sm, sv)):
        delta[n], new_m[n], new_v[n] = (view(n, t) if t.ndim == 4 else t for t in (a, b, c_))

    reduced = share_wait(share_send, share_recv, reduced, entries, new_v["s5_c_re"])
    for n, g in zip(BIG_NAMES, reduced):
        grads[n] = g.reshape(w[n].shape)
    flat = lambda t: [t[n].reshape(-1, t[n].shape[-1]) for n in BIG_NAMES]
    for table, arrays in zip((grads, delta, new_m, new_v), adamw("adamw_big", flat(w), flat(grads), flat(mom), flat(var))):
        for n, a in zip(BIG_NAMES, arrays):
            table[n] = a.reshape(w[n].shape)

    out = [loss.reshape(()), grad_x[None]]
    for table in (grads, delta, new_m, new_v):
        out += [table[n].reshape(w[n].shape) for n in WEIGHT_ORDER]
    return tuple(out)
```

```python
import math

import jax
import jax.numpy as jnp
from jax import lax
from jax.experimental import pallas as pl
from jax.experimental.pallas import tpu as pltpu

F32 = jnp.float32
BF16 = jnp.bfloat16

D_MODEL = 1024
S5_GROUPS = 64
S5_GROUP = 16
S5_STATE = 64
N_KV = 4
N_Q = 16
HEAD_DIM = 64
BLOCK = 128
NORM_EPS = 1e-5
LAMBDA_RE_MAX = -1e-4
ADAM_LR, ADAM_B1, ADAM_B2, ADAM_EPS, ADAM_WD, ADAM_STEP = 0.001, 0.9, 0.999, 1e-08, 0.01, 10

VMEM_LIMIT_BYTES = 56 * 1024 * 1024
S5_CHUNK = 256
S5_BLOCKS = 4
MESH = pl.DeviceIdType.MESH


def _params(sem=None):
    return pltpu.CompilerParams(dimension_semantics=sem, vmem_limit_bytes=VMEM_LIMIT_BYTES)


def _sds(shape, dtype):
    return jax.ShapeDtypeStruct(shape, dtype)


def _rms_hat(xv):
    r = lax.rsqrt(jnp.mean(xv * xv, axis=-1, keepdims=True) + NORM_EPS)
    return xv * r, r


def mm_nn(name, a, w, col_offsets, n_out, epilogue, out_dtypes, extras=(), rowvecs=(), n_sums=0, tm=1024, tn=512):
    m, k = a.shape
    tm, tn = min(tm, m), min(tn, n_out)
    nw, ne, nr, no = len(col_offsets), len(extras), len(rowvecs), len(out_dtypes)

    def body(a_ref, *refs):
        w_refs, e_refs, r_refs = refs[:nw], refs[nw:nw + ne], refs[nw + ne:nw + ne + nr]
        o_refs, s_refs = refs[nw + ne + nr:nw + ne + nr + no], refs[nw + ne + nr + no:]
        av = a_ref[...]
        accs = [jnp.dot(av, w_ref[...], preferred_element_type=F32) for w_ref in w_refs]
        outs = epilogue(accs, [e[...] for e in e_refs], [r[...] for r in r_refs])
        for o_ref, o in zip(o_refs, outs[:no]):
            o_ref[...] = o.astype(o_ref.dtype)
        if n_sums:
            @pl.when(pl.program_id(1) == 0)
            def _():
                for s_ref in s_refs:
                    s_ref[...] = jnp.zeros_like(s_ref)

            for s_ref, val in zip(s_refs, outs[no:]):
                s_ref[...] += val

    def wspec(off):
        return pl.BlockSpec((k, tn), lambda j, i, off=off: (0, off // tn + j))

    def rspec(off):
        return pl.BlockSpec((1, tn), lambda j, i, off=off: (0, off // tn + j))

    tile = pl.BlockSpec((tm, tn), lambda j, i: (i, j))
    in_specs = ([pl.BlockSpec((tm, k), lambda j, i: (i, 0))] + [wspec(o) for o in col_offsets]
                + [tile] * ne + [rspec(o) for _, o in rowvecs])
    sem = ("parallel", "arbitrary") if n_sums else ("parallel", "parallel")
    return pl.pallas_call(
        body, grid=(n_out // tn, m // tm), in_specs=in_specs,
        out_specs=[tile] * no + [pl.BlockSpec((1, tn), lambda j, i: (0, j))] * n_sums,
        out_shape=[_sds((m, n_out), dt) for dt in out_dtypes] + [_sds((1, n_out), F32)] * n_sums, name=name,
        compiler_params=_params(sem))(a, *([w] * nw), *extras, *[r for r, _ in rowvecs])


def mm_nt(name, g, w, epilogue, out_dtypes, extras=(), rowvecs=(), n_sums=0, tm=512, tk=512):
    m, n = g.shape
    k = w.shape[0]
    tm, tk = min(tm, m), min(tk, k)
    ne, nr, no = len(extras), len(rowvecs), len(out_dtypes)

    def body(g_ref, w_ref, *refs):
        e_refs, r_refs, o_refs, s_refs = refs[:ne], refs[ne:ne + nr], refs[ne + nr:ne + nr + no], refs[ne + nr + no:]
        acc = lax.dot_general(g_ref[...], w_ref[...], (((1,), (1,)), ((), ())), preferred_element_type=F32)
        outs = epilogue(acc, [e[...] for e in e_refs], [r[...] for r in r_refs])
        for o_ref, o in zip(o_refs, outs[:no]):
            o_ref[...] = o.astype(o_ref.dtype)
        if n_sums:
            @pl.when(pl.program_id(0) == 0)
            def _():
                for s_ref in s_refs:
                    s_ref[...] = jnp.zeros_like(s_ref)

            for s_ref, val in zip(s_refs, outs[no:]):
                s_ref[...] += val

    tile = pl.BlockSpec((tm, tk), lambda i, j: (i, j))
    vec = pl.BlockSpec((1, tk), lambda i, j: (0, j))
    sem = ("arbitrary", "parallel") if n_sums else ("parallel", "parallel")
    return pl.pallas_call(
        body, grid=(m // tm, k // tk),
        in_specs=[pl.BlockSpec((tm, n), lambda i, j: (i, 0)), pl.BlockSpec((tk, n), lambda i, j: (j, 0))]
        + [tile] * ne + [vec] * nr,
        out_specs=[tile] * no + [vec] * n_sums,
        out_shape=[_sds((m, k), dt) for dt in out_dtypes] + [_sds((1, k), F32)] * n_sums, name=name,
        compiler_params=_params(sem))(g, w, *extras, *rowvecs)


def mm_tn(name, a, g, tk=512, tn=512):
    m, k = a.shape
    n = g.shape[1]
    tk, tn = min(tk, k), min(tn, n)

    def body(a_ref, g_ref, o_ref):
        acc = lax.dot_general(a_ref[...], g_ref[...], (((0,), (0,)), ((), ())), preferred_element_type=F32)
        o_ref[...] = acc.astype(o_ref.dtype)

    return pl.pallas_call(
        body, grid=(k // tk, n // tn),
        in_specs=[pl.BlockSpec((m, tk), lambda i, j: (0, i)), pl.BlockSpec((m, tn), lambda i, j: (0, j))],
        out_specs=pl.BlockSpec((tk, tn), lambda i, j: (i, j)), out_shape=_sds((k, n), BF16), name=name,
        compiler_params=_params(("parallel", "parallel")))(a, g)


def _row_mask(tc):
    row = lax.broadcasted_iota(jnp.int32, (8 * tc, 256), 0) % 8
    col = lax.broadcasted_iota(jnp.int32, (8 * tc, 256), 1) // 32
    return row == col


def _expand_rows(val, mask):
    tc, width = val.shape
    rep = jnp.broadcast_to(val[:, None, :], (tc, 8, width)).reshape(8 * tc, width)
    return jnp.where(mask, rep, 0.0).astype(BF16)


def _stage(ref, val):
    ref[0] = val[:, 0:128]
    ref[1] = val[:, 128:256]


def _gather_rows(src_ref, tc):
    halves = []
    for half in range(2):
        col = lax.broadcasted_iota(jnp.int32, (tc, 128), 1) // 32 + 4 * half
        out = jnp.zeros((tc, 128), F32)
        for s8 in range(4 * half, 4 * half + 4):
            out = jnp.where(col == s8, src_ref.at[half][pl.ds(s8, tc, stride=8), :], out)
        halves.append(out)
    return jnp.concatenate(halves, axis=1)


def _gelu_and_slope(x):
    c = math.sqrt(2.0 / math.pi)
    t = jnp.tanh(c * (x + 0.044715 * x * x * x))
    return 0.5 * x * (1.0 + t), 0.5 * (1.0 + t) + 0.5 * x * (1.0 - t * t) * c * (1.0 + 3.0 * 0.044715 * x * x)


def s5_fwd(x, gain, d_skip, rb, rc, lam_r, lam_i):
    n_rows = x.shape[0]
    tc = min(S5_CHUNK, n_rows)
    nc = n_rows // tc

    def body(x_ref, g_ref, d_ref, rb_ref, rc_ref, lr_ref, li_ref, ge_ref, slope_ref, cs_ref, bux, yrows, carry):
        i = pl.program_id(0)
        u = _rms_hat(x_ref[...])[0] * g_ref[...]

        @pl.when(i == 0)
        def _():
            carry[...] = jnp.zeros_like(carry)

        cs_ref[0] = carry[...]
        mask = _row_mask(tc)
        for blk in range(S5_BLOCKS):
            lhs = _expand_rows(u[:, blk * 256:(blk + 1) * 256], mask)
            bux[blk] = jnp.dot(lhs, rb_ref[blk], preferred_element_type=F32)
        lam = [(lr_ref[blk], li_ref[blk]) for blk in range(S5_BLOCKS)]

        def step(t, c):
            r0 = pl.multiple_of(t * 8, 8)
            new = []
            for blk in range(S5_BLOCKS):
                xr, xi = c[2 * blk], c[2 * blk + 1]
                lr, li = lam[blk]
                nr = lr * xr - li * xi + bux[blk, pl.ds(r0, 8), 0:128]
                ni = lr * xi + li * xr + bux[blk, pl.ds(r0, 8), 128:256]
                bux[blk, pl.ds(r0, 8), 0:128] = nr
                bux[blk, pl.ds(r0, 8), 128:256] = ni
                new += [nr, ni]
            return tuple(new)

        c0 = []
        for blk in range(S5_BLOCKS):
            c0 += [carry[blk, :, 0:128], carry[blk, :, 128:256]]
        cn = lax.fori_loop(0, tc, step, tuple(c0), unroll=4)
        for blk in range(S5_BLOCKS):
            carry[blk, :, 0:128] = cn[2 * blk]
            carry[blk, :, 128:256] = cn[2 * blk + 1]
        for blk in range(S5_BLOCKS):
            _stage(yrows, jnp.dot(bux[blk].astype(BF16), rc_ref[blk], preferred_element_type=F32))
            sl = slice(blk * 256, (blk + 1) * 256)
            ge, slope = _gelu_and_slope(_gather_rows(yrows, tc) + d_ref[:, sl] * u[:, sl])
            slope_ref[:, sl] = slope
            ge_ref[:, sl] = ge.astype(BF16)

    row = pl.BlockSpec((tc, D_MODEL), lambda i: (i, 0))
    vec = pl.BlockSpec((1, D_MODEL), lambda i: (0, 0))
    mat = pl.BlockSpec((S5_BLOCKS, 256, 256), lambda i: (0, 0, 0))
    lamspec = pl.BlockSpec((S5_BLOCKS, 8, 128), lambda i: (0, 0, 0))
    return pl.pallas_call(
        body, grid=(nc,),
        in_specs=[row, vec, vec, mat, mat, lamspec, lamspec],
        out_specs=[row, row, pl.BlockSpec((1, S5_BLOCKS, 8, 256), lambda i: (i, 0, 0, 0))],
        out_shape=[_sds((n_rows, D_MODEL), BF16), _sds((n_rows, D_MODEL), F32), _sds((nc, S5_BLOCKS, 8, 256), F32)],
        scratch_shapes=[pltpu.VMEM((S5_BLOCKS, 8 * tc, 256), F32), pltpu.VMEM((2, 8 * tc, 128), F32),
                        pltpu.VMEM((S5_BLOCKS, 8, 256), F32)],
        name="s5_fwd", compiler_params=_params(("arbitrary",)))(x, gain, d_skip, rb, rc, lam_r, lam_i)


def s5_bwd(x, gain, dy2, res, d_skip, cs, rb, rbt, rct, lam_r, lam_i):
    n_rows = x.shape[0]
    tc = min(S5_CHUNK, n_rows)
    nc = n_rows // tc

    def body(x_ref, g_ref, dy_ref, res_ref, d_ref, cs_ref, rb_ref, rbt_ref, rct_ref, lr_ref, li_ref,
             dx_ref, dd_ref, drb_ref, drc_ref, dlr_ref, dli_ref, dg_ref, tmp, du, lhsu, lhsd, xs, adj, acarry):
        i = pl.program_id(0)
        u = _rms_hat(x_ref[...])[0] * g_ref[...]

        @pl.when(i == 0)
        def _():
            acarry[...] = jnp.zeros_like(acarry)
            dd_ref[...] = jnp.zeros_like(dd_ref)
            drb_ref[...] = jnp.zeros_like(drb_ref)
            drc_ref[...] = jnp.zeros_like(drc_ref)
            dlr_ref[...] = jnp.zeros_like(dlr_ref)
            dli_ref[...] = jnp.zeros_like(dli_ref)
            dg_ref[...] = jnp.zeros_like(dg_ref)

        dd_ref[...] += jnp.sum(dy_ref[...] * u, axis=0, keepdims=True)
        mask = _row_mask(tc)
        for blk in range(S5_BLOCKS):
            sl = slice(blk * 256, (blk + 1) * 256)
            lhsu[blk] = _expand_rows(u[:, sl], mask)
            xs[blk] = jnp.dot(lhsu[blk], rb_ref[blk], preferred_element_type=F32)
            lhsd[blk] = _expand_rows(dy_ref[:, sl], mask)
            adj[blk] = jnp.dot(lhsd[blk], rct_ref[blk], preferred_element_type=F32)
        lam = [(lr_ref[blk], li_ref[blk]) for blk in range(S5_BLOCKS)]

        def fstep(t, c):
            r0 = pl.multiple_of(t * 8, 8)
            new = []
            for blk in range(S5_BLOCKS):
                xr, xi = c[2 * blk], c[2 * blk + 1]
                lr, li = lam[blk]
                nr = lr * xr - li * xi + xs[blk, pl.ds(r0, 8), 0:128]
                ni = lr * xi + li * xr + xs[blk, pl.ds(r0, 8), 128:256]
                xs[blk, pl.ds(r0, 8), 0:128] = nr
                xs[blk, pl.ds(r0, 8), 128:256] = ni
                new += [nr, ni]
            return tuple(new)

        c0 = []
        for blk in range(S5_BLOCKS):
            c0 += [cs_ref[0, blk, :, 0:128], cs_ref[0, blk, :, 128:256]]
        lax.fori_loop(0, tc, fstep, tuple(c0), unroll=4)

        def bstep(k, c):
            t = tc - 1 - k
            r0 = pl.multiple_of(t * 8, 8)
            rp = pl.multiple_of(jnp.maximum(t - 1, 0) * 8, 8)
            first = t == 0
            new_a, new_g = [], []
            for blk in range(S5_BLOCKS):
                ar, ai = c[0][2 * blk], c[0][2 * blk + 1]
                glr, gli = c[1][2 * blk], c[1][2 * blk + 1]
                lr, li = lam[blk]
                nr = lr * ar + li * ai + adj[blk, pl.ds(r0, 8), 0:128]
                ni = lr * ai - li * ar + adj[blk, pl.ds(r0, 8), 128:256]
                adj[blk, pl.ds(r0, 8), 0:128] = nr
                adj[blk, pl.ds(r0, 8), 128:256] = ni
                pr = jnp.where(first, cs_ref[0, blk, :, 0:128], xs[blk, pl.ds(rp, 8), 0:128])
                pi = jnp.where(first, cs_ref[0, blk, :, 128:256], xs[blk, pl.ds(rp, 8), 128:256])
                new_a += [nr, ni]
                new_g += [glr + nr * pr + ni * pi, gli + ni * pr - nr * pi]
            return tuple(new_a), tuple(new_g)

        a0, g0 = [], []
        for blk in range(S5_BLOCKS):
            a0 += [acarry[blk, :, 0:128], acarry[blk, :, 128:256]]
            g0 += [dlr_ref[blk], dli_ref[blk]]
        an, gn = lax.fori_loop(0, tc, bstep, (tuple(a0), tuple(g0)), unroll=2)
        for blk in range(S5_BLOCKS):
            acarry[blk, :, 0:128] = an[2 * blk]
            acarry[blk, :, 128:256] = an[2 * blk + 1]
            dlr_ref[blk] = gn[2 * blk]
            dli_ref[blk] = gn[2 * blk + 1]
        for blk in range(S5_BLOCKS):
            sl = slice(blk * 256, (blk + 1) * 256)
            ab = adj[blk].astype(BF16)
            _stage(tmp, jnp.dot(ab, rbt_ref[blk], preferred_element_type=F32))
            du[:, sl] = _gather_rows(tmp, tc) + d_ref[:, sl] * dy_ref[:, sl]
            drb_ref[blk] += lax.dot_general(lhsu[blk], ab, (((0,), (0,)), ((), ())), preferred_element_type=F32)
            drc_ref[blk] += lax.dot_general(lhsd[blk], xs[blk].astype(BF16), (((0,), (0,)), ((), ())),
                                            preferred_element_type=F32)
        xh, r = _rms_hat(x_ref[...])
        dg_ref[...] += jnp.sum(du[...] * xh, axis=0, keepdims=True)
        dxh = du[...] * g_ref[...]
        dx_ref[...] = r * (dxh - xh * jnp.mean(dxh * xh, axis=-1, keepdims=True)) + res_ref[...]

    rev = pl.BlockSpec((tc, D_MODEL), lambda i: (nc - 1 - i, 0))
    vec = pl.BlockSpec((1, D_MODEL), lambda i: (0, 0))
    mat = pl.BlockSpec((S5_BLOCKS, 256, 256), lambda i: (0, 0, 0))
    lamspec = pl.BlockSpec((S5_BLOCKS, 8, 128), lambda i: (0, 0, 0))
    big = pltpu.VMEM((S5_BLOCKS, 8 * tc, 256), F32)
    bigb = pltpu.VMEM((S5_BLOCKS, 8 * tc, 256), BF16)
    return pl.pallas_call(
        body, grid=(nc,),
        in_specs=[rev, vec, rev, rev, vec, pl.BlockSpec((1, S5_BLOCKS, 8, 256), lambda i: (nc - 1 - i, 0, 0, 0)),
                  mat, mat, mat, lamspec, lamspec],
        out_specs=[rev, vec, mat, mat, lamspec, lamspec, vec],
        out_shape=[_sds((n_rows, D_MODEL), F32), _sds((1, D_MODEL), F32), _sds((S5_BLOCKS, 256, 256), F32),
                   _sds((S5_BLOCKS, 256, 256), F32), _sds((S5_BLOCKS, 8, 128), F32), _sds((S5_BLOCKS, 8, 128), F32),
                   _sds((1, D_MODEL), F32)],
        scratch_shapes=[pltpu.VMEM((2, 8 * tc, 128), F32), pltpu.VMEM((tc, D_MODEL), F32), bigb, bigb, big, big,
                        pltpu.VMEM((S5_BLOCKS, 8, 256), F32)],
        name="s5_bwd", compiler_params=_params(("arbitrary",)))(
            x, gain, dy2, res, d_skip, cs, rb, rbt, rct, lam_r, lam_i)


def _s5_views(a_re, a_im, log_dt, b_re, b_im):
    return a_re[:, None, :], a_im[:, None, :], log_dt[:, None, None], jnp.swapaxes(b_re, 1, 2), jnp.swapaxes(b_im, 1, 2)


def _s5_factors(a_re, a_im, log_dt):
    lr, li, dt = jnp.minimum(a_re, LAMBDA_RE_MAX), a_im, jnp.exp(log_dt)
    mag, ang = jnp.exp(lr * dt), li * dt
    lbr, lbi = mag * jnp.cos(ang), mag * jnp.sin(ang)
    den = lr * lr + li * li
    fr, fi = ((lbr - 1.0) * lr + lbi * li) / den, (lbi * lr - (lbr - 1.0) * li) / den
    return lr, li, dt, lbr, lbi, fr, fi, den


def s5_prep(a_re, a_im, log_dt, b_re, b_im, c_re, c_im):
    def body(ar_ref, ai_ref, t_ref, br_ref, bi_ref, cr_ref, ci_ref, rb_ref, rbt_ref, rc_ref, rct_ref, lr_ref, li_ref):
        _, _, _, lbr, lbi, fr, fi, _ = _s5_factors(ar_ref[...], ai_ref[...], t_ref[...])
        lr_ref[...] = lbr
        li_ref[...] = lbi
        bre = fr * br_ref[...] - fi * bi_ref[...]
        bim = fr * bi_ref[...] + fi * br_ref[...]
        even = (lax.broadcasted_iota(jnp.int32, (256, S5_STATE), 0) // S5_GROUP) % 2 == 0

        def assemble(re, im):
            re, im = re.reshape(256, S5_STATE), im.reshape(256, S5_STATE)
            return jnp.concatenate([jnp.where(even, re, 0.0), jnp.where(even, 0.0, re), jnp.where(even, im, 0.0),
                                    jnp.where(even, 0.0, im)], axis=1)

        for blk in range(S5_BLOCKS):
            sl = slice(16 * blk, 16 * blk + 16)
            rb = assemble(bre[sl], bim[sl])
            rct = assemble(cr_ref[sl], -ci_ref[sl])
            rb_ref[blk] = rb.astype(BF16)
            rbt_ref[blk] = rb.T.astype(BF16)
            rct_ref[blk] = rct.astype(BF16)
            rc_ref[blk] = rct.T.astype(BF16)

    vm = pl.BlockSpec(memory_space=pltpu.VMEM)
    mat = _sds((S5_BLOCKS, 256, 256), BF16)
    lam = _sds((S5_GROUPS, 1, S5_STATE), F32)
    rb, rbt, rc, rct, lam_r, lam_i = pl.pallas_call(
        body, in_specs=[vm] * 7, out_specs=[vm] * 6, out_shape=[mat, mat, mat, mat, lam, lam], name="s5_prep",
        compiler_params=_params())(*_s5_views(a_re, a_im, log_dt, b_re, b_im), c_re, c_im)
    return rb, rbt, rc, rct, lam_r.reshape(S5_BLOCKS, 8, 128), lam_i.reshape(S5_BLOCKS, 8, 128)


def s5_param_bwd(mats, lams, a_re, a_im, log_dt, b_re, b_im):
    def body(m_ref, glr_ref, gli_ref, ar_ref, ai_ref, t_ref, br_ref, bi_ref,
             dar_ref, dai_ref, dt_ref, dbr_ref, dbi_ref, dcr_ref, dci_ref):
        lr, li, dt, lbr, lbi, fr, fi, den = _s5_factors(ar_ref[...], ai_ref[...], t_ref[...])
        shape = (S5_GROUPS, S5_GROUP, S5_STATE)
        gbr, gbi = m_ref[0:1024, 0:64].reshape(shape), m_ref[0:1024, 64:128].reshape(shape)
        dcr_ref[...] = m_ref[1024:2048, 0:64].reshape(shape)
        dci_ref[...] = -m_ref[1024:2048, 64:128].reshape(shape)
        br, bi = br_ref[...], bi_ref[...]
        dbr_ref[...] = fr * gbr + fi * gbi
        dbi_ref[...] = fr * gbi - fi * gbr
        dfr = jnp.sum(gbr * br + gbi * bi, axis=1, keepdims=True)
        dfi = jnp.sum(gbi * br - gbr * bi, axis=1, keepdims=True)
        nr, ni = (dfr * lr - dfi * li) / den, (dfr * li + dfi * lr) / den
        qr, qi = (fr * lr + fi * li) / den, (fi * lr - fr * li) / den
        lam_r, lam_i = -(dfr * qr + dfi * qi), -(dfi * qr - dfr * qi)
        gr, gi = glr_ref[...] + nr, gli_ref[...] + ni
        zr, zi = gr * lbr + gi * lbi, gi * lbr - gr * lbi
        a = ar_ref[...]
        dar_ref[...] = (lam_r + zr * dt) * jnp.where(a < LAMBDA_RE_MAX, 1.0, jnp.where(a == LAMBDA_RE_MAX, 0.5, 0.0))
        dai_ref[...] = lam_i + zi * dt
        dt_ref[...] = jnp.sum(zr * lr + zi * li, axis=2, keepdims=True) * dt

    vm = pl.BlockSpec(memory_space=pltpu.VMEM)
    state = _sds((S5_GROUPS, 1, S5_STATE), F32)
    wide = _sds((S5_GROUPS, S5_GROUP, S5_STATE), F32)
    glr = lams[0:32].reshape(S5_GROUPS, 1, S5_STATE)
    gli = lams[32:64].reshape(S5_GROUPS, 1, S5_STATE)
    dar, dai, ddt, dbr, dbi, dcr, dci = pl.pallas_call(
        body, in_specs=[vm] * 8, out_specs=[vm] * 7,
        out_shape=[state, state, _sds((S5_GROUPS, 1, 1), F32), wide, wide, wide, wide], name="s5_param_bwd",
        compiler_params=_params())(mats, glr, gli, *_s5_views(a_re, a_im, log_dt, b_re, b_im))
    return (dar.reshape(S5_GROUPS, S5_STATE), dai.reshape(S5_GROUPS, S5_STATE), ddt.reshape(S5_GROUPS),
            jnp.swapaxes(dbr, 1, 2), jnp.swapaxes(dbi, 1, 2), dcr, dci)


def s5_compact(drb, drct, dlr, dli):
    def body(drb_ref, drct_ref, dlr_ref, dli_ref, o_ref, lam_ref):
        even = (lax.broadcasted_iota(jnp.int32, (256, 64), 0) // S5_GROUP) % 2 == 0
        for blk in range(S5_BLOCKS):
            for k, ref in enumerate((drb_ref, drct_ref)):
                m = ref[blk]
                re = jnp.where(even, m[:, 0:64], m[:, 64:128])
                im = jnp.where(even, m[:, 128:192], m[:, 192:256])
                o_ref[pl.ds(k * 1024 + blk * 256, 256), :] = jnp.concatenate([re, im], axis=1)
            lam_ref[pl.ds(blk * 8, 8), :] = dlr_ref[blk]
            lam_ref[pl.ds(32 + blk * 8, 8), :] = dli_ref[blk]

    vm = pl.BlockSpec(memory_space=pltpu.VMEM)
    return pl.pallas_call(body, in_specs=[vm] * 4, out_specs=[vm, vm], out_shape=[_sds((2048, 128), F32), _sds((64, 128), F32)],
                          name="s5_compact", compiler_params=_params())(drb, drct, dlr, dli)


NEG = -1e30


GROUP = N_Q // N_KV


def _attn_masks(n):
    qi = lax.broadcasted_iota(jnp.int32, (GROUP * BLOCK, BLOCK), 0) % BLOCK
    kj = lax.broadcasted_iota(jnp.int32, (GROUP * BLOCK, BLOCK), 1)
    return jnp.logical_and(kj > qi, n > 0), kj <= qi


def _stack_heads(ref, kh):
    return jnp.concatenate([ref[:, (GROUP * kh + g) * HEAD_DIM:(GROUP * kh + g + 1) * HEAD_DIM] for g in range(GROUP)], axis=0)


def _unstack_heads(val):
    return jnp.concatenate([val[g * BLOCK:(g + 1) * BLOCK] for g in range(GROUP)], axis=1)


def _sink_column(sink_ref, kh):
    grp = lax.broadcasted_iota(jnp.int32, (GROUP * BLOCK, 1), 0) // BLOCK
    col = jnp.zeros((GROUP * BLOCK, 1), F32)
    for g in range(GROUP):
        col = jnp.where(grp == g, sink_ref[GROUP * kh + g], col)
    return col, grp


def _attn_exp(q4, kp, kc, sink, mask_p, mask_c):
    scale = 1.0 / math.sqrt(HEAD_DIM)
    nt = (((1,), (1,)), ((), ()))
    sp = jnp.where(mask_p, lax.dot_general(q4, kp, nt, preferred_element_type=F32) * scale, NEG)
    sc = jnp.where(mask_c, lax.dot_general(q4, kc, nt, preferred_element_type=F32) * scale, NEG)
    m = jnp.maximum(jnp.maximum(jnp.max(sp, axis=-1, keepdims=True), jnp.max(sc, axis=-1, keepdims=True)), sink)
    pp = jnp.exp(sp - m)
    pc = jnp.exp(sc - m)
    ps = jnp.exp(sink - m)
    inv = 1.0 / (jnp.sum(pp, axis=-1, keepdims=True) + jnp.sum(pc, axis=-1, keepdims=True) + ps)
    return pp, pc, ps, inv


def attn_fwd(q, kv, sinks):
    n_rows = q.shape[0]
    nb = n_rows // BLOCK

    def body(sink_ref, q_ref, kvp_ref, kvc_ref, o_ref):
        n = pl.program_id(0)
        mask_p, mask_c = _attn_masks(n)
        outs = []
        for kh in range(N_KV):
            ks, vs = slice(kh * HEAD_DIM, (kh + 1) * HEAD_DIM), slice((N_KV + kh) * HEAD_DIM, (N_KV + kh + 1) * HEAD_DIM)
            sink, _ = _sink_column(sink_ref, kh)
            pp, pc, _, inv = _attn_exp(_stack_heads(q_ref, kh), kvp_ref[:, ks], kvc_ref[:, ks], sink, mask_p, mask_c)
            o4 = (jnp.dot(pp.astype(BF16), kvp_ref[:, vs], preferred_element_type=F32)
                  + jnp.dot(pc.astype(BF16), kvc_ref[:, vs], preferred_element_type=F32)) * inv
            outs.append(_unstack_heads(o4))
        o_ref[...] = jnp.concatenate(outs, axis=1).astype(BF16)

    kvw = 2 * N_KV * HEAD_DIM
    return pl.pallas_call(
        body, grid=(nb,),
        in_specs=[pl.BlockSpec(memory_space=pltpu.SMEM), pl.BlockSpec((BLOCK, D_MODEL), lambda n: (n, 0)),
                  pl.BlockSpec((BLOCK, kvw), lambda n: (jnp.maximum(n - 1, 0), 0)), pl.BlockSpec((BLOCK, kvw), lambda n: (n, 0))],
        out_specs=pl.BlockSpec((BLOCK, D_MODEL), lambda n: (n, 0)), out_shape=_sds((n_rows, D_MODEL), BF16),
        name="attn_fwd", compiler_params=_params(("parallel",)))(sinks, q, kv, kv)


def attn_bwd(q, kv, do, sinks):
    n_rows = q.shape[0]
    nb = n_rows // BLOCK
    kvw = 2 * N_KV * HEAD_DIM
    tn = (((0,), (0,)), ((), ()))
    nt = (((1,), (1,)), ((), ()))
    scale = 1.0 / math.sqrt(HEAD_DIM)

    def body(sink_ref, q_ref, kvp_ref, kvc_ref, do_ref, dq_ref, dbq_ref, dprev_ref, dcur_ref, dsink_ref):
        n = pl.program_id(0)
        mask_p, mask_c = _attn_masks(n)
        lane = lax.broadcasted_iota(jnp.int32, (1, D_MODEL), 1)
        dqs, dsink = [], jnp.zeros((1, D_MODEL), F32)
        dkp, dkc, dvp, dvc = [], [], [], []
        for kh in range(N_KV):
            ks, vs = slice(kh * HEAD_DIM, (kh + 1) * HEAD_DIM), slice((N_KV + kh) * HEAD_DIM, (N_KV + kh + 1) * HEAD_DIM)
            q4, do4 = _stack_heads(q_ref, kh), _stack_heads(do_ref, kh)
            kp, kc, vp, vc = kvp_ref[:, ks], kvc_ref[:, ks], kvp_ref[:, vs], kvc_ref[:, vs]
            sink, grp = _sink_column(sink_ref, kh)
            pp, pc, ps, inv = _attn_exp(q4, kp, kc, sink, mask_p, mask_c)
            pp, pc = pp * inv, pc * inv
            dpp = lax.dot_general(do4, vp, nt, preferred_element_type=F32)
            dpc = lax.dot_general(do4, vc, nt, preferred_element_type=F32)
            delta = jnp.sum(pp * dpp, axis=-1, keepdims=True) + jnp.sum(pc * dpc, axis=-1, keepdims=True)
            dsp = (pp * (dpp - delta) * scale).astype(BF16)
            dsc = (pc * (dpc - delta) * scale).astype(BF16)
            dsk = ps * inv * delta
            for g in range(GROUP):
                dsink = dsink + jnp.where(lane == GROUP * kh + g, -jnp.sum(jnp.where(grp == g, dsk, 0.0)), 0.0)
            dqs.append(_unstack_heads(jnp.dot(dsp, kp, preferred_element_type=F32)
                                      + jnp.dot(dsc, kc, preferred_element_type=F32)))
            dkp.append(lax.dot_general(dsp, q4, tn, preferred_element_type=F32))
            dkc.append(lax.dot_general(dsc, q4, tn, preferred_element_type=F32))
            dvp.append(lax.dot_general(pp.astype(BF16), do4, tn, preferred_element_type=F32))
            dvc.append(lax.dot_general(pc.astype(BF16), do4, tn, preferred_element_type=F32))
        dq = jnp.concatenate(dqs, axis=1)
        dq_ref[...] = dq.astype(BF16)
        dprev_ref[0] = jnp.concatenate(dkp + dvp, axis=1)
        dcur_ref[0] = jnp.concatenate(dkc + dvc, axis=1)

        @pl.when(n == 0)
        def _():
            dbq_ref[...] = jnp.zeros_like(dbq_ref)
            dsink_ref[...] = jnp.zeros_like(dsink_ref)

        dbq_ref[...] += jnp.sum(dq, axis=0, keepdims=True)
        dsink_ref[...] += dsink

    blk = pl.BlockSpec((BLOCK, D_MODEL), lambda n: (n, 0))
    part = pl.BlockSpec((1, BLOCK, kvw), lambda n: (n, 0, 0))
    return pl.pallas_call(
        body, grid=(nb,),
        in_specs=[pl.BlockSpec(memory_space=pltpu.SMEM), blk,
                  pl.BlockSpec((BLOCK, kvw), lambda n: (jnp.maximum(n - 1, 0), 0)), pl.BlockSpec((BLOCK, kvw), lambda n: (n, 0)), blk],
        out_specs=[blk, pl.BlockSpec((1, D_MODEL), lambda n: (0, 0)), part, part, pl.BlockSpec((1, D_MODEL), lambda n: (0, 0))],
        out_shape=[_sds((n_rows, D_MODEL), BF16), _sds((1, D_MODEL), F32), _sds((nb, BLOCK, kvw), F32),
                   _sds((nb, BLOCK, kvw), F32), _sds((1, D_MODEL), F32)],
        name="attn_bwd", compiler_params=_params(("arbitrary",)))(sinks, q, kv, kv, do)


def kv_combine(dprev, dcur):
    nb, _, kvw = dprev.shape

    def body(dcur_ref, dprev_ref, dkv_ref, db_ref):
        total = jnp.zeros((1, kvw), F32)
        for m in range(nb):
            dkv = dcur_ref[m] + dprev_ref[m + 1] if m + 1 < nb else dcur_ref[m]
            dkv_ref[m * BLOCK:(m + 1) * BLOCK, :] = dkv.astype(BF16)
            total = total + jnp.sum(dkv, axis=0, keepdims=True)
        db_ref[...] = jnp.concatenate([total, jnp.zeros((1, D_MODEL - kvw), F32)], axis=1)

    vm = pl.BlockSpec(memory_space=pltpu.VMEM)
    return pl.pallas_call(body, in_specs=[vm, vm], out_specs=[vm, vm],
                          out_shape=[_sds((nb * BLOCK, kvw), BF16), _sds((1, D_MODEL), F32)], name="kv_combine",
                          compiler_params=_params())(dcur, dprev)


def glu_bwd(dout, val, gate, tm=256):
    n_rows, d = dout.shape

    def body(do_ref, v_ref, g_ref, dz_ref, db_ref):
        i = pl.program_id(0)
        sg = jax.nn.sigmoid(g_ref[...])
        dval = do_ref[...] * sg
        dgate = do_ref[...] * v_ref[...] * sg * (1.0 - sg)
        dz_ref[...] = jnp.concatenate([dval, dgate], axis=1).astype(BF16)

        @pl.when(i == 0)
        def _():
            db_ref[...] = jnp.zeros_like(db_ref)

        db_ref[0:1, :] += jnp.sum(dval, axis=0, keepdims=True)
        db_ref[1:2, :] += jnp.sum(dgate, axis=0, keepdims=True)

    row = pl.BlockSpec((tm, d), lambda i: (i, 0))
    return pl.pallas_call(
        body, grid=(n_rows // tm,), in_specs=[row, row, row],
        out_specs=[pl.BlockSpec((tm, 2 * d), lambda i: (i, 0)), pl.BlockSpec((2, d), lambda i: (0, 0))],
        out_shape=[_sds((n_rows, 2 * d), BF16), _sds((2, d), F32)],
        name="glu_bwd", compiler_params=_params(("arbitrary",)))(dout, val, gate)


def _adam_update(w, g, m, v):
    nm = ADAM_B1 * m + (1.0 - ADAM_B1) * g
    nv = ADAM_B2 * v + (1.0 - ADAM_B2) * (g * g)
    m_hat = nm / (1.0 - ADAM_B1 ** ADAM_STEP)
    v_hat = nv / (1.0 - ADAM_B2 ** ADAM_STEP)
    return -ADAM_LR * (m_hat / (jnp.sqrt(v_hat) + ADAM_EPS) + ADAM_WD * w), nm, nv


def adamw(name, ws, gs, ms, vs, steps=8):
    n = len(ws)

    def body(*refs):
        for k in range(n):
            w_ref, g_ref, m_ref, v_ref = (refs[j * n + k] for j in range(4))
            go_ref, d_ref, nm_ref, nv_ref = (refs[(4 + j) * n + k] for j in range(4))
            gv = g_ref[...]
            go_ref[...] = gv
            d_ref[...], nm_ref[...], nv_ref[...] = _adam_update(w_ref[...], gv, m_ref[...], v_ref[...])

    specs = [pl.BlockSpec((w.shape[0] // steps, w.shape[1]), lambda i: (i, 0)) for w in ws]
    shapes = [_sds(w.shape, F32) for w in ws]
    out = pl.pallas_call(
        body, grid=(steps,), in_specs=specs * 4, out_specs=specs * 4, out_shape=shapes * 4, name=name,
        compiler_params=_params(("parallel",)))(*ws, *gs, *ms, *vs)
    return [list(out[j * n:(j + 1) * n]) for j in range(4)]


def adamw_native(name, ws, gs, ms, vs):
    n = len(ws)

    def body(*refs):
        w_refs, g_refs, m_refs, v_refs = refs[:n], refs[n:2 * n], refs[2 * n:3 * n], refs[3 * n:4 * n]
        d_refs, nm_refs, nv_refs = refs[4 * n:5 * n], refs[5 * n:6 * n], refs[6 * n:7 * n]
        for k in range(n):
            dl, nm, nv = _adam_update(w_refs[k][...], g_refs[k][...], m_refs[k][...], v_refs[k][...])
            d_refs[k][...] = dl
            nm_refs[k][...] = nm
            nv_refs[k][...] = nv

    vm = pl.BlockSpec(memory_space=pltpu.VMEM)
    shapes = [_sds(w.shape, F32) for w in ws]
    out = pl.pallas_call(body, in_specs=[vm] * (4 * n), out_specs=[vm] * (3 * n), out_shape=shapes * 3, name=name,
                         compiler_params=_params())(*ws, *gs, *ms, *vs)
    return list(out[:n]), list(out[n:2 * n]), list(out[2 * n:])


VEC_ROWS = {"norm_mix": 0, "norm_mlp": 2, "norm_kv": 4, "norm_final": 5, "s5_d": 6, "b_q": 7, "b_o": 8, "s5_b_glu": 9,
            "b_kv": 11, "sinks": 12, "loss": 13}


def split_vectors(where, vecs, d_shard, glu_shard):
    kvw = 2 * N_KV * HEAD_DIM
    shapes = {"norm_mix": (2, D_MODEL), "norm_mlp": (2, D_MODEL), "norm_kv": (1, D_MODEL), "norm_final": (1, D_MODEL),
              "s5_d": (1, d_shard), "b_q": (1, D_MODEL), "b_o": (1, D_MODEL), "s5_b_glu": (1, glu_shard), "b_kv": (1, kvw),
              "sinks": (1, N_Q), "loss": (1, 128)}
    names = list(shapes)

    def body(where_ref, v_ref, *o_refs):
        chip = where_ref[1]
        for name, o_ref in zip(names, o_refs):
            r0, (r, n) = VEC_ROWS[name], shapes[name]
            if name == "s5_d":
                g = jnp.zeros((1, n), F32)
                for j in range(4):
                    g = jnp.where(chip == j, v_ref[r0:r0 + 1, j * n:(j + 1) * n], g)
            elif name == "s5_b_glu":
                g = jnp.zeros((1, n), F32)
                for j in range(4):
                    row, col = r0 + (j * n) // D_MODEL, (j * n) % D_MODEL
                    g = jnp.where(chip == j, v_ref[row:row + 1, col:col + n], g)
            else:
                g = v_ref[r0:r0 + r, 0:n]
            o_ref[...] = g

    vm = pl.BlockSpec(memory_space=pltpu.VMEM)
    out = pl.pallas_call(body, in_specs=[pl.BlockSpec(memory_space=pltpu.SMEM), vm], out_specs=[vm] * len(names),
                         out_shape=[_sds(shapes[n], F32) for n in names], name="split_vectors",
                         compiler_params=_params())(where, vecs)
    return dict(zip(names, out))


def _position():
    x, y, c = lax.axis_index("x"), lax.axis_index("y"), lax.axis_index("c")
    others = [(1 - x, y), (x, 1 - y), (1 - x, 1 - y)]
    return x, y, c, others


def _window(ref, kind, chip, half, shard_shape):
    if kind == "slab":
        return ref.at[chip]
    r, n = shard_shape
    if kind == "col":
        return ref.at[pl.ds(pl.multiple_of(half * (r // 2), 16), r // 2), pl.ds(pl.multiple_of(chip * n, 128), n)]
    return ref.at[pl.ds(pl.multiple_of(chip * r, 16), r), pl.ds(pl.multiple_of(half * (n // 2), 128), n // 2)]


def _half(ref, kind, half, shape):
    r, n = shape
    if kind == "col":
        return ref.at[pl.ds(pl.multiple_of(half * (r // 2), 16), r // 2), :]
    return ref.at[:, pl.ds(pl.multiple_of(half * (n // 2), 128), n // 2)]


def swap_start(name, grads, kinds, carry):
    nt = len(grads)
    shapes = [tuple(g.shape) for g in grads]
    lands = [lax.empty(sh, BF16) for sh in shapes]
    given, given_specs, token_type, write = _hand_through(carry)
    n_in = 2 * nt + len(given)

    def body(*refs):
        in_refs, land_refs = refs[:nt], refs[nt:2 * nt]
        send_sems, recv_sems, token = refs[n_in], refs[n_in + 1], refs[-1]
        x, y, c, _ = _position()
        for t in range(nt):
            pltpu.make_async_remote_copy(
                src_ref=_half(in_refs[t], kinds[t], 1 - c, shapes[t]), dst_ref=_half(land_refs[t], kinds[t], 1 - c, shapes[t]),
                send_sem=send_sems.at[t], recv_sem=recv_sems.at[t], device_id=(x, y, 1 - c), device_id_type=MESH).start()
        write(token, refs[:n_in])

    sems = pltpu.SemaphoreType.DMA((nt,))
    both = list(grads) + lands
    out = pl.pallas_call(
        body, name=name, in_specs=[HBM_SPEC] * (2 * nt) + given_specs,
        out_specs=(SEM_SPEC, SEM_SPEC, *[HBM_SPEC] * (2 * nt), pl.BlockSpec(memory_space=pltpu.VMEM)),
        out_shape=(sems, sems, *[pltpu.HBM(a.shape, a.dtype) for a in both], token_type),
        input_output_aliases={t: 2 + t for t in range(2 * nt)}, compiler_params=_split_params(),
    )(*[_in_hbm(a) for a in both], *given)
    return out[0], out[1], list(out[2:2 + nt]), list(out[2 + nt:2 + 2 * nt]), out[-1]


def swap_wait(name, send_sems, recv_sems, grads, lands, kinds, after):
    nt = len(grads)
    shapes = [tuple(g.shape) for g in grads]

    def body(*refs):
        in_refs, land_refs = refs[:nt], refs[nt:2 * nt]
        send_ref, recv_ref = refs[2 * nt], refs[2 * nt + 1]
        x, y, c, _ = _position()
        for t in range(nt):
            cp = pltpu.make_async_remote_copy(
                src_ref=_half(in_refs[t], kinds[t], 1 - c, shapes[t]), dst_ref=_half(land_refs[t], kinds[t], c, shapes[t]),
                send_sem=send_ref.at[t], recv_sem=recv_ref.at[t], device_id=(x, y, 1 - c), device_id_type=MESH)
            cp.wait_send()
            cp.wait_recv()

    both = list(grads) + list(lands)
    out = pl.pallas_call(
        body, name=name, in_specs=[HBM_SPEC] * (2 * nt) + [SEM_SPEC, SEM_SPEC, HBM_SPEC], out_specs=[HBM_SPEC] * (2 * nt),
        out_shape=[pltpu.HBM(a.shape, a.dtype) for a in both], input_output_aliases={t: t for t in range(2 * nt)},
        compiler_params=_split_params())(*both, send_sems, recv_sems, _in_hbm(after))
    return list(out[:nt]), list(out[nt:])


def _half_spec(kind, shape, tiles):
    r, n = shape
    if kind == "col":
        tn = n // tiles
        return pl.BlockSpec((r // 2, tn), lambda i, s: (s[0], i))
    tm = r // tiles
    return pl.BlockSpec((tm, n // 2), lambda i, s: (i, s[0]))


def add_halves(name, mine, landed, kinds, where, tiles=2):
    nt = len(mine)
    shapes = [tuple(a.shape) for a in mine]

    def compact(t):
        r, n = shapes[t]
        if kinds[t] == "col":
            return (r // 2, n), pl.BlockSpec((r // 2, n // tiles), lambda i, s: (0, i))
        return (r, n // 2), pl.BlockSpec((r // tiles, n // 2), lambda i, s: (i, 0))

    def body(s_ref, *refs):
        for a_ref, b_ref, o_ref in zip(refs[:nt], refs[nt:2 * nt], refs[2 * nt:]):
            o_ref[...] = (a_ref[...].astype(F32) + b_ref[...].astype(F32)).astype(BF16)

    specs = [_half_spec(kinds[t], shapes[t], tiles) for t in range(nt)]
    return pl.pallas_call(
        body, grid_spec=pltpu.PrefetchScalarGridSpec(num_scalar_prefetch=1, grid=(tiles,), in_specs=specs + specs,
                                                     out_specs=[compact(t)[1] for t in range(nt)]),
        out_shape=[_sds(compact(t)[0], BF16) for t in range(nt)], name=name,
        compiler_params=_params(("parallel",)))(where, *mine, *landed)


def sum_shards(name, parts, landed, kinds, shard_shapes, where, layers, n_layers, intos, tiles=2):
    nt = len(parts)
    in_specs, out_specs = [], []
    for t in range(nt):
        (r, n), layer = shard_shapes[t], layers[t]
        if kinds[t] == "col":
            tm, width = r // 2 // tiles, n
            own = pl.BlockSpec((tm, n), lambda i, s: (i, s[1]))
            out = pl.BlockSpec((None, tm, n), lambda i, s, layer=layer: (layer, s[0] * tiles + i, 0))
        else:
            tm, width = r // tiles, n // 2
            own = pl.BlockSpec((tm, n // 2), lambda i, s: (s[1] * tiles + i, 0))
            out = pl.BlockSpec((None, tm, n // 2), lambda i, s, layer=layer: (layer, i, s[0]))
        in_specs += [own, pl.BlockSpec((3, tm, width), lambda i, s: (0, i, 0))]
        out_specs.append(out)
    args, aliases = [where] + [a for pair in zip(parts, landed) for a in pair], {}
    for t in range(nt):
        if intos[t] is not None:
            aliases[len(args)] = t
            in_specs.append(pl.BlockSpec(memory_space=pl.ANY))
            args.append(intos[t])

    def body(s_ref, *refs):
        for t in range(nt):
            a_ref, l_ref, o_ref = refs[2 * t], refs[2 * t + 1], refs[len(in_specs) + t]
            o_ref[...] = ((a_ref[...].astype(F32) + l_ref[0].astype(F32)) + l_ref[1].astype(F32)) + l_ref[2].astype(F32)

    return pl.pallas_call(
        body, grid_spec=pltpu.PrefetchScalarGridSpec(num_scalar_prefetch=1, grid=(tiles,), in_specs=in_specs,
                                                     out_specs=out_specs),
        out_shape=[_sds((n_layers[t],) + tuple(shard_shapes[t]), F32) for t in range(nt)], input_output_aliases=aliases,
        name=name, compiler_params=_params(("parallel",)))(*args)


def share_start(arrays, entries, carry):
    na, nt = len(arrays), len(entries)
    given, given_specs, token_type, write = _hand_through(carry)
    n_in = na + len(given)

    def body(*refs):
        in_refs, send_sems, recv_sems, token = refs[:na], refs[n_in], refs[n_in + 1], refs[-1]
        x, y, c, _ = _position()
        for t, (a, layer, kind) in enumerate(entries):
            mine = _half(in_refs[a].at[layer], kind, c, tuple(arrays[a].shape[1:]))
            pltpu.make_async_remote_copy(
                src_ref=mine, dst_ref=mine, send_sem=send_sems.at[t], recv_sem=recv_sems.at[t],
                device_id=(x, y, 1 - c), device_id_type=MESH).start()
        write(token, refs[:n_in])

    sems = pltpu.SemaphoreType.DMA((nt,))
    out = pl.pallas_call(
        body, name="share_start", in_specs=[HBM_SPEC] * na + given_specs,
        out_specs=(SEM_SPEC, SEM_SPEC, *[HBM_SPEC] * na, pl.BlockSpec(memory_space=pltpu.VMEM)),
        out_shape=(sems, sems, *[pltpu.HBM(a.shape, a.dtype) for a in arrays], token_type),
        input_output_aliases={t: 2 + t for t in range(na)}, compiler_params=_split_params(),
    )(*[_in_hbm(a) for a in arrays], *given)
    return out[0], out[1], list(out[2:2 + na]), out[-1]


def share_wait(send_sems, recv_sems, arrays, entries, after):
    na = len(arrays)

    def body(*refs):
        in_refs, send_ref, recv_ref = refs[:na], refs[na], refs[na + 1]
        x, y, c, _ = _position()
        for t, (a, layer, kind) in enumerate(entries):
            shape = tuple(arrays[a].shape[1:])
            cp = pltpu.make_async_remote_copy(
                src_ref=_half(in_refs[a].at[layer], kind, c, shape), dst_ref=_half(in_refs[a].at[layer], kind, 1 - c, shape),
                send_sem=send_ref.at[t], recv_sem=recv_ref.at[t], device_id=(x, y, 1 - c), device_id_type=MESH)
            cp.wait_send()
            cp.wait_recv()

    return list(pl.pallas_call(
        body, name="share_wait", in_specs=[HBM_SPEC] * na + [SEM_SPEC, SEM_SPEC, HBM_SPEC], out_specs=[HBM_SPEC] * na,
        out_shape=[pltpu.HBM(a.shape, a.dtype) for a in arrays], input_output_aliases={t: t for t in range(na)},
        compiler_params=_split_params())(*arrays, send_sems, recv_sems, _in_hbm(after)))


HBM_SPEC = pl.BlockSpec(memory_space=pltpu.HBM)
SEM_SPEC = pl.BlockSpec(memory_space=pltpu.SEMAPHORE)
ANY_SPEC = pl.BlockSpec(memory_space=pl.ANY)


def _split_params():
    return pltpu.CompilerParams(has_side_effects=pltpu.SideEffectType.DATAFLOW_SIDE_EFFECTING,
                                vmem_limit_bytes=VMEM_LIMIT_BYTES)


def _in_hbm(a):
    return pltpu.with_memory_space_constraint(a, pltpu.HBM)


def cast_place(arrays, entries, where, tiles=2):
    in_specs, out_specs, fulls = [], [], []
    for a, layer, kind in entries:
        _, r, n = arrays[a].shape
        tm = r // tiles
        in_specs.append(pl.BlockSpec((None, tm, n), lambda i, s, layer=layer: (layer, i, 0)))
        if kind == "col":
            fulls.append((r, 4 * n))
            out_specs.append(pl.BlockSpec((tm, n), lambda i, s: (i, s[1])))
        else:
            fulls.append((4 * r, n))
            out_specs.append(pl.BlockSpec((tm, n), lambda i, s: (s[1] * tiles + i, 0)))
    nt = len(entries)

    def body(s_ref, *refs):
        for w_ref, o_ref in zip(refs[:nt], refs[nt:]):
            o_ref[...] = w_ref[...].astype(BF16)

    return pl.pallas_call(
        body, grid_spec=pltpu.PrefetchScalarGridSpec(num_scalar_prefetch=1, grid=(tiles,), in_specs=in_specs,
                                                     out_specs=out_specs),
        out_shape=[_sds(f, BF16) for f in fulls], name="cast_place",
        compiler_params=_params(("parallel",)))(where, *[arrays[a] for a, _, _ in entries])


def _hand_through(carry):
    given = [] if isinstance(carry, tuple) else [carry]

    def write(token, ins):
        token[...] = ins[-1][...] if given else jnp.zeros_like(token)

    return (given, [pl.BlockSpec(memory_space=pltpu.VMEM)] * len(given),
            _sds(carry if isinstance(carry, tuple) else carry.shape, F32), write)


def gather_start(fulls, kinds, shard_shapes, carry):
    nt = len(fulls)
    given, given_specs, token_type, write = _hand_through(carry)
    n_in = nt + len(given)

    def body(*refs):
        full_refs = refs[:nt]
        send_sems, recv_sems, token = refs[n_in], refs[n_in + 1], refs[-1]
        x, y, c, others = _position()
        for t in range(nt):
            mine = _window(full_refs[t], kinds[t], 2 * x + y, c, shard_shapes[t])
            for j, (ox, oy) in enumerate(others):
                pltpu.make_async_remote_copy(
                    src_ref=mine, dst_ref=mine, send_sem=send_sems.at[3 * t + j], recv_sem=recv_sems.at[3 * t + j],
                    device_id=(ox, oy, c), device_id_type=MESH).start()
        write(token, refs[:n_in])

    sems = pltpu.SemaphoreType.DMA((3 * nt,))
    out = pl.pallas_call(
        body, name="gather_start", in_specs=[HBM_SPEC] * nt + given_specs,
        out_specs=(SEM_SPEC, SEM_SPEC, *[HBM_SPEC] * nt, pl.BlockSpec(memory_space=pltpu.VMEM)),
        out_shape=(sems, sems, *[pltpu.HBM(f.shape, f.dtype) for f in fulls], token_type),
        input_output_aliases={t: 2 + t for t in range(nt)}, compiler_params=_split_params(),
    )(*[_in_hbm(f) for f in fulls], *given)
    return out[0], out[1], list(out[2:2 + nt]), out[-1]


def gather_wait(name, send_sems, recv_sems, fulls, kinds, shard_shapes, after, first):
    nt = len(fulls)
    extra = [] if after is None else [_in_hbm(after)]

    def body(*refs):
        full_refs, send_ref, recv_ref = refs[:nt], refs[nt], refs[nt + 1]
        x, y, c, others = _position()
        for t in range(nt):
            mine = _window(full_refs[t], kinds[t], 2 * x + y, c, shard_shapes[t])
            for j, (ox, oy) in enumerate(others):
                cp = pltpu.make_async_remote_copy(
                    src_ref=mine, dst_ref=_window(full_refs[t], kinds[t], 2 * ox + oy, c, shard_shapes[t]),
                    send_sem=send_ref.at[3 * (first + t) + j], recv_sem=recv_ref.at[3 * (first + t) + j],
                    device_id=(ox, oy, c), device_id_type=MESH)
                cp.wait_send()
                cp.wait_recv()

    out = pl.pallas_call(
        body, name=name, in_specs=[HBM_SPEC] * nt + [SEM_SPEC, SEM_SPEC] + [HBM_SPEC] * len(extra),
        out_specs=[HBM_SPEC] * nt, out_shape=[pltpu.HBM(f.shape, f.dtype) for f in fulls],
        input_output_aliases={t: t for t in range(nt)}, compiler_params=_split_params())(*fulls, send_sems, recv_sems, *extra)
    return list(out)


def forward_halves(name, fulls, kinds, shard_shapes):
    nt = len(fulls)

    def body(*refs):
        out_refs = refs[nt:2 * nt]
        send_sems, recv_sems = refs[2 * nt:]
        x, y, c, others = _position()
        cps = []
        for t in range(nt):
            for j, (ox, oy) in enumerate(others):
                landed = _window(out_refs[t], kinds[t], 2 * ox + oy, c, shard_shapes[t])
                cp = pltpu.make_async_remote_copy(
                    src_ref=landed, dst_ref=landed, send_sem=send_sems.at[3 * t + j], recv_sem=recv_sems.at[3 * t + j],
                    device_id=(x, y, 1 - c), device_id_type=MESH)
                cp.start()
                cps.append(cp)
        for t in range(nt):
            for j, (ox, oy) in enumerate(others):
                got = _window(out_refs[t], kinds[t], 2 * ox + oy, 1 - c, shard_shapes[t])
                pltpu.make_async_remote_copy(
                    src_ref=got, dst_ref=got, send_sem=send_sems.at[3 * t + j], recv_sem=recv_sems.at[3 * t + j],
                    device_id=(x, y, 1 - c), device_id_type=MESH).wait_recv()
        for cp in cps:
            cp.wait_send()

    out = pl.pallas_call(
        body, in_specs=[ANY_SPEC] * nt, out_specs=[ANY_SPEC] * nt, out_shape=[_sds(f.shape, f.dtype) for f in fulls],
        input_output_aliases={t: t for t in range(nt)},
        scratch_shapes=[pltpu.SemaphoreType.DMA((3 * nt,)), pltpu.SemaphoreType.DMA((3 * nt,))],
        name=name, compiler_params=_params())(*fulls)
    return list(out)


def forward_start(name, send_sems, recv_sems, fulls, kinds, shard_shapes, after, first, carry, passing=()):
    nt, n_pass = len(fulls), len(passing)
    given, given_specs, token_type, write = _hand_through(carry)
    n_in = nt + 3 + n_pass + len(given)

    def body(*refs):
        full_refs, ici_send, ici_recv = refs[:nt], refs[nt], refs[nt + 1]
        send_ref, recv_ref, token = refs[n_in], refs[n_in + 1], refs[-1]
        x, y, c, others = _position()
        for t in range(nt):
            mine = _window(full_refs[t], kinds[t], 2 * x + y, c, shard_shapes[t])
            for j, (ox, oy) in enumerate(others):
                landed = _window(full_refs[t], kinds[t], 2 * ox + oy, c, shard_shapes[t])
                cp = pltpu.make_async_remote_copy(
                    src_ref=mine, dst_ref=landed, send_sem=ici_send.at[3 * (first + t) + j],
                    recv_sem=ici_recv.at[3 * (first + t) + j], device_id=(ox, oy, c), device_id_type=MESH)
                cp.wait_send()
                cp.wait_recv()
                pltpu.make_async_remote_copy(
                    src_ref=landed, dst_ref=landed, send_sem=send_ref.at[3 * t + j], recv_sem=recv_ref.at[3 * t + j],
                    device_id=(x, y, 1 - c), device_id_type=MESH).start()
        write(token, refs[:n_in])

    sems = pltpu.SemaphoreType.DMA((3 * nt,))
    out = pl.pallas_call(
        body, name=name, in_specs=[HBM_SPEC] * nt + [SEM_SPEC, SEM_SPEC, HBM_SPEC] + [HBM_SPEC] * n_pass + given_specs,
        out_specs=(SEM_SPEC, SEM_SPEC, *[HBM_SPEC] * (nt + n_pass), pl.BlockSpec(memory_space=pltpu.VMEM)),
        out_shape=(sems, sems, *[pltpu.HBM(f.shape, f.dtype) for f in [*fulls, *passing]], token_type),
        input_output_aliases={**{t: 2 + t for t in range(nt)}, **{nt + 3 + t: 2 + nt + t for t in range(n_pass)}},
        compiler_params=_split_params(),
    )(*fulls, send_sems, recv_sems, _in_hbm(after), *passing, *given)
    return out[0], out[1], list(out[2:2 + nt]), list(out[2 + nt:2 + nt + n_pass]), out[-1]


def forward_wait(name, send_sems, recv_sems, fulls, kinds, shard_shapes, after):
    nt = len(fulls)

    def body(*refs):
        full_refs, send_ref, recv_ref = refs[:nt], refs[nt], refs[nt + 1]
        x, y, c, others = _position()
        for t in range(nt):
            for j, (ox, oy) in enumerate(others):
                cp = pltpu.make_async_remote_copy(
                    src_ref=_window(full_refs[t], kinds[t], 2 * ox + oy, c, shard_shapes[t]),
                    dst_ref=_window(full_refs[t], kinds[t], 2 * ox + oy, 1 - c, shard_shapes[t]),
                    send_sem=send_ref.at[3 * t + j], recv_sem=recv_ref.at[3 * t + j],
                    device_id=(x, y, 1 - c), device_id_type=MESH)
                cp.wait_send()
                cp.wait_recv()

    return list(pl.pallas_call(
        body, name=name, in_specs=[HBM_SPEC] * nt + [SEM_SPEC, SEM_SPEC, HBM_SPEC], out_specs=[HBM_SPEC] * nt,
        out_shape=[pltpu.HBM(f.shape, f.dtype) for f in fulls], input_output_aliases={t: t for t in range(nt)},
        compiler_params=_split_params())(*fulls, send_sems, recv_sems, _in_hbm(after)))


def _piece(ref, kind, chip, shard_shape):
    r, n = shard_shape
    if kind == "col":
        return ref.at[:, pl.ds(pl.multiple_of(chip * n, 128), n)]
    return ref.at[pl.ds(pl.multiple_of(chip * r, 16), r), :]


def _piece_shape(kind, shard_shape):
    r, n = shard_shape
    return (r // 2, n) if kind == "col" else (r, n // 2)


def exchange_start(name, parts, kinds, shard_shapes, carry):
    nt = len(parts)
    lands = [lax.empty((3,) + _piece_shape(kinds[t], shard_shapes[t]), BF16) for t in range(nt)]
    given, given_specs, token_type, write = _hand_through(carry)
    n_in = 2 * nt + len(given)

    def body(*refs):
        part_refs, land_refs = refs[:nt], refs[nt:2 * nt]
        send_sems, recv_sems, token = refs[n_in], refs[n_in + 1], refs[-1]
        x, y, c, others = _position()
        for t in range(nt):
            for j, (ox, oy) in enumerate(others):
                pltpu.make_async_remote_copy(
                    src_ref=_piece(part_refs[t], kinds[t], 2 * ox + oy, shard_shapes[t]), dst_ref=land_refs[t].at[j],
                    send_sem=send_sems.at[3 * t + j], recv_sem=recv_sems.at[3 * t + j],
                    device_id=(ox, oy, c), device_id_type=MESH).start()
        write(token, refs[:n_in])

    sems = pltpu.SemaphoreType.DMA((3 * nt,))
    both = list(parts) + lands
    out = pl.pallas_call(
        body, name=name, in_specs=[HBM_SPEC] * (2 * nt) + given_specs,
        out_specs=(SEM_SPEC, SEM_SPEC, *[HBM_SPEC] * (2 * nt), pl.BlockSpec(memory_space=pltpu.VMEM)),
        out_shape=(sems, sems, *[pltpu.HBM(a.shape, a.dtype) for a in both], token_type),
        input_output_aliases={t: 2 + t for t in range(2 * nt)}, compiler_params=_split_params(),
    )(*[_in_hbm(a) for a in both], *given)
    return out[0], out[1], list(out[2:2 + nt]), list(out[2 + nt:2 + 2 * nt]), out[-1]


def exchange_wait(name, send_sems, recv_sems, parts, lands, kinds, shard_shapes, after):
    nt = len(parts)

    def body(*refs):
        part_refs, land_refs = refs[:nt], refs[nt:2 * nt]
        send_ref, recv_ref = refs[2 * nt], refs[2 * nt + 1]
        x, y, c, others = _position()
        for t in range(nt):
            for j, (ox, oy) in enumerate(others):
                cp = pltpu.make_async_remote_copy(
                    src_ref=_piece(part_refs[t], kinds[t], 2 * ox + oy, shard_shapes[t]), dst_ref=land_refs[t].at[j],
                    send_sem=send_ref.at[3 * t + j], recv_sem=recv_ref.at[3 * t + j],
                    device_id=(ox, oy, c), device_id_type=MESH)
                cp.wait_send()
                cp.wait_recv()

    both = list(parts) + list(lands)
    out = pl.pallas_call(
        body, name=name, in_specs=[HBM_SPEC] * (2 * nt) + [SEM_SPEC, SEM_SPEC, HBM_SPEC], out_specs=[HBM_SPEC] * (2 * nt),
        out_shape=[pltpu.HBM(a.shape, a.dtype) for a in both], input_output_aliases={t: t for t in range(2 * nt)},
        compiler_params=_split_params())(*both, send_sems, recv_sems, _in_hbm(after))
    return list(out[:nt]), list(out[nt:])


def reduce_swap(bufs, wire):
    n = len(bufs)
    halves = [b.shape[0] // 2 for b in bufs]

    def body(*refs):
        in_refs, out_refs, txs, got = refs[:n], refs[n:2 * n], refs[2 * n:3 * n], refs[3 * n:4 * n]
        send_sems, recv_sems = refs[4 * n:]
        x, y, c, _ = _position()
        cps = []
        for k in range(n):
            txs[k][...] = in_refs[k][pl.ds(pl.multiple_of((1 - c) * halves[k], 8), halves[k]), :].astype(wire[k])
            cp = pltpu.make_async_remote_copy(src_ref=txs[k], dst_ref=got[k], send_sem=send_sems.at[k],
                                              recv_sem=recv_sems.at[k], device_id=(x, y, 1 - c), device_id_type=MESH)
            cp.start()
            cps.append(cp)
        for k, cp in enumerate(cps):
            cp.wait()
            own = in_refs[k][pl.ds(pl.multiple_of(c * halves[k], 8), halves[k]), :]
            out_refs[k][...] = (own.astype(wire[k]).astype(F32) + got[k][...].astype(F32)).astype(wire[k])

    vm = pl.BlockSpec(memory_space=pltpu.VMEM)
    parts = [((h, b.shape[1]), w) for h, b, w in zip(halves, bufs, wire)]
    return list(pl.pallas_call(
        body, name="reduce_swap", in_specs=[vm] * n, out_specs=[vm] * n, out_shape=[_sds(sh, w) for sh, w in parts],
        scratch_shapes=[pltpu.VMEM(sh, w) for sh, w in parts] * 2 + [pltpu.SemaphoreType.DMA((n,))] * 2,
        compiler_params=_params())(*bufs))


def reduce_start(parts):
    n = len(parts)
    lands = [lax.empty((4,) + tuple(p.shape), p.dtype) for p in parts]

    def body(*refs):
        part_refs, land_refs, send_sems, recv_sems = refs[:n], refs[n:2 * n], refs[2 * n], refs[2 * n + 1]
        x, y, c, others = _position()
        for k in range(n):
            for j, (ox, oy) in enumerate(others):
                pltpu.make_async_remote_copy(
                    src_ref=part_refs[k], dst_ref=land_refs[k].at[2 * x + y], send_sem=send_sems.at[3 * k + j],
                    recv_sem=recv_sems.at[3 * k + j], device_id=(ox, oy, c), device_id_type=MESH).start()

    sems = pltpu.SemaphoreType.DMA((3 * n,))
    both = list(parts) + lands
    out = pl.pallas_call(
        body, name="reduce_start", in_specs=[HBM_SPEC] * (2 * n), out_specs=(SEM_SPEC, SEM_SPEC, *[HBM_SPEC] * (2 * n)),
        out_shape=(sems, sems, *[pltpu.HBM(a.shape, a.dtype) for a in both]),
        input_output_aliases={k: 2 + k for k in range(2 * n)}, compiler_params=_split_params(),
    )(*[_in_hbm(a) for a in both])
    return out[0], out[1], list(out[2:2 + n]), list(out[2 + n:])


def reduce_wait(send_sems, recv_sems, parts, lands, after):
    n = len(parts)

    def body(*refs):
        part_refs, land_refs, send_ref, recv_ref = refs[:n], refs[n:2 * n], refs[2 * n], refs[2 * n + 1]
        x, y, c, others = _position()
        for k in range(n):
            for j, (ox, oy) in enumerate(others):
                cp = pltpu.make_async_remote_copy(
                    src_ref=part_refs[k], dst_ref=land_refs[k].at[2 * ox + oy], send_sem=send_ref.at[3 * k + j],
                    recv_sem=recv_ref.at[3 * k + j], device_id=(ox, oy, c), device_id_type=MESH)
                cp.wait_send()
                cp.wait_recv()

    both = list(parts) + list(lands)
    out = pl.pallas_call(
        body, name="reduce_wait", in_specs=[HBM_SPEC] * (2 * n) + [SEM_SPEC, SEM_SPEC, HBM_SPEC],
        out_specs=[HBM_SPEC] * (2 * n), out_shape=[pltpu.HBM(a.shape, a.dtype) for a in both],
        input_output_aliases={k: k for k in range(2 * n)}, compiler_params=_split_params(),
    )(*both, send_sems, recv_sems, _in_hbm(after))
    return list(out[:n]), list(out[n:])


def reduce_share(parts, lands):
    n = len(parts)
    halves = [p.shape[0] for p in parts]

    def body(*refs):
        part_refs, land_refs, out_refs = refs[:n], refs[n:2 * n], refs[2 * n:3 * n]
        send_sems, recv_sems = refs[3 * n:]
        x, y, c, _ = _position()
        chip = 2 * x + y
        cps = []
        for k in range(n):
            mine = pl.ds(pl.multiple_of(c * halves[k], 8), halves[k])
            own = part_refs[k][...].astype(F32)
            total = jnp.where(chip == 0, own, land_refs[k][0].astype(F32))
            for entry in range(1, 4):
                total = total + jnp.where(chip == entry, own, land_refs[k][entry].astype(F32))
            out_refs[k][mine, :] = total
            cp = pltpu.make_async_remote_copy(
                src_ref=out_refs[k].at[mine], dst_ref=out_refs[k].at[mine], send_sem=send_sems.at[k],
                recv_sem=recv_sems.at[k], device_id=(x, y, 1 - c), device_id_type=MESH)
            cp.start()
            cps.append(cp)
        for cp in cps:
            cp.wait()

    vm = pl.BlockSpec(memory_space=pltpu.VMEM)
    return list(pl.pallas_call(
        body, name="reduce_share", in_specs=[vm] * (2 * n), out_specs=[vm] * n,
        out_shape=[_sds((2 * p.shape[0], p.shape[1]), F32) for p in parts],
        scratch_shapes=[pltpu.SemaphoreType.DMA((n,))] * 2, compiler_params=_params())(*parts, *lands))


def _local_step(x, target, small, need, ahead, emit_swap, emit_exchange):
    d = D_MODEL
    full = {}

    def handed(vec, token):
        return vec if token is None else token

    def token_rows(token):
        return [] if token is None else [token]

    def plus(acc, rows):
        return acc + rows[0] if rows else acc

    rb16, rbt16, rc16, rct16, lr_t, li_t = small["s5_operands"]
    ge, ge_slope, cs = s5_fwd(x, small["norm_mix0"], small["s5_d"], rb16, rc16, lr_t, li_t)
    full.update(need("glu", ge))

    def norm_rows(h, gains):
        xh, _ = _rms_hat(h)
        return [xh * g for g in gains]

    def glu_epilogue(accs, e, r):
        v, gt = accs[0] + r[0], accs[1] + r[1]
        h = e[0] + v * jax.nn.sigmoid(gt)
        return [h, v, gt] + norm_rows(h, r[2:])

    gain_mlp0 = handed(small["norm_mlp0"], ahead("mlp_in0", full["w_glu"], small["norm_mlp0"]))
    h1, val, gate, n1 = mm_nn(
        "glu", ge, full["w_glu"], [0, d], d, glu_epilogue, [F32, F32, F32, BF16], extras=[x],
        rowvecs=[(small["s5_b_glu"], 0), (small["s5_b_glu"], d), (gain_mlp0, 0)], tm=512, tn=d)

    def mlp_fwd(tag, h, n, w_in, get_w_out, next_gains, head=None):
        def in_epilogue(accs, e, rv):
            pos = jnp.maximum(accs[0], 0.0)
            return [pos * pos, 2.0 * pos]

        r, slope = mm_nn("mlp_in" + tag, n, w_in, [0], w_in.shape[1], in_epilogue, [BF16, BF16], tm=2048)
        w_out = get_w_out(r)

        def epilogue(accs, e, rv):
            h_out = e[0] + accs[0]
            return [h_out] + norm_rows(h_out, rv)

        if head is not None:
            return head(r, w_out, h), (n, r, slope)
        outs = mm_nn("mlp_out" + tag, r, w_out, [0], d, epilogue, [F32] + [BF16] * len(next_gains), extras=[h],
                     rowvecs=[(g, 0) for g in next_gains], tm=512, tn=d)
        return outs[0], outs[1:], (n, r, slope)

    full.update(need("mlp_in0", h1))

    def w_out0(after):
        full.update(need("mlp_out0", after))
        return full["w_out0"]

    h2, (nkv, n2), mlp0 = mlp_fwd("0", h1, n1, full["w_in0"], w_out0, [small["norm_kv"], small["norm_mix1"]])

    full.update(need("attn", h2))
    kvw = 2 * N_KV * HEAD_DIM
    (kv,) = mm_nn("kv_proj", nkv, full["w_kv"], [0], kvw, lambda accs, e, r: [accs[0] + r[0]], [BF16],
                  rowvecs=[(small["b_kv"], 0)], tm=2048)
    (q,) = mm_nn("q_proj", n2, full["w_q"], [0], d, lambda accs, e, r: [accs[0] + r[0]], [BF16],
                 rowvecs=[(small["b_q"], 0)], tm=2048)
    sinks = small["sinks"].reshape(N_Q)
    o = attn_fwd(q, kv, sinks)
    def o_epilogue(accs, e, r):
        h_out = e[0] + accs[0] + r[0]
        return [h_out] + norm_rows(h_out, r[1:])

    bias_o = handed(small["b_o"], ahead("mlp_in1", o, small["b_o"]))
    h3, n3 = mm_nn("o_proj", o, full["w_o"], [0], d, o_epilogue, [F32, BF16], extras=[h2],
                   rowvecs=[(bias_o, 0), (small["norm_mlp1"], 0)], tm=512, tn=d)
    full.update(need("mlp_in1", h3, then="mlp_out1"))

    def w_out1(after):
        full.update(need("mlp_out1", after))
        return full["w_out1"]

    def loss_head(r, w_out, h):
        def epilogue(accs, e, rv):
            xh, rr = _rms_hat(e[0] + accs[0])
            err = xh * rv[0] - e[1]
            dy = err * (1.0 / d)
            dxh = dy * rv[0]
            dx = rr * (dxh - xh * jnp.mean(dxh * xh, axis=-1, keepdims=True))
            loss = jnp.full((1, d), 0.5 * jnp.sum(jnp.mean(err * err, axis=-1, keepdims=True)), F32)
            return [dx, dx, loss, jnp.sum(dy * xh, axis=0, keepdims=True)]

        return mm_nn("mlp_out1", r, w_out, [0], d, epilogue, [F32, BF16], extras=[h, target],
                     rowvecs=[(small["norm_final"], 0)], n_sums=2, tm=512, tn=d)

    (dh, dhb, loss_tile, dg_final), mlp1 = mlp_fwd("1", h3, n3, full["w_in1"], w_out1, [], head=loss_head)

    grads_small, grads_full = {"norm_final": dg_final}, {}
    ident = lambda acc, e, r: [plus(acc, r)]
    layer1 = ["w_out1", "w_in1", "w_o", "w_q", "w_kv"]
    layer0 = ["w_out0", "w_in0", "w_glu"]

    def norm_bwd_rows(x_rows, res, dys, gains):
        xh, r = _rms_hat(x_rows)
        dxh = sum(dy * g for dy, g in zip(dys, gains))
        dx = r * (dxh - xh * jnp.mean(dxh * xh, axis=-1, keepdims=True)) + res
        return dx, [jnp.sum(dy * xh, axis=0, keepdims=True) for dy in dys]

    def mlp_bwd(tag, dh, dhb, h_in, gain, w_in, w_out, saved, token=None):
        n, r, slope = saved
        grads_full["w_out" + tag] = mm_tn("dw_out" + tag, r, dhb, tn=1024)
        (da,) = mm_nt("mlp_da" + tag, dhb, w_out, lambda acc, e, rv: [plus(acc * e[0].astype(F32), rv)], [BF16],
                      extras=[slope], rowvecs=token_rows(token), tm=2048)
        grads_full["w_in" + tag] = mm_tn("dw_in" + tag, n, da, tn=1024)

        def epilogue(acc, e, rv):
            dx, dgs = norm_bwd_rows(e[0], e[1], [acc], rv)
            return [dx, dx, jnp.sum(dx, axis=0, keepdims=True)] + dgs

        dx, dxb, colsum, dg = mm_nt("mlp_dn" + tag, da, w_in, epilogue, [F32, BF16], extras=[h_in, dh], rowvecs=[gain],
                                    n_sums=2, tm=512, tk=d)
        grads_small["norm_mlp" + tag] = dg
        return dx, dxb, colsum

    dh3, dh3b, colsum3 = mlp_bwd("1", dh, dhb, h3, small["norm_mlp1"], full["w_in1"], full["w_out1"], mlp1)
    grads_small["b_o"] = colsum3
    grads_full["w_o"] = mm_tn("dw_o", o, dh3b, tn=1024)
    (do,) = mm_nt("attn_do", dh3b, full["w_o"], ident, [BF16], tm=2048)
    dq, dbq, dprev, dcur, dsink = attn_bwd(q, kv, do, sinks)
    dkv, dbkv = kv_combine(dprev, dcur)
    grads_small["b_q"], grads_small["b_kv"], grads_small["sinks"] = dbq, dbkv, dsink
    grads_full["w_q"] = mm_tn("dw_q", n2, dq, tn=1024)
    grads_full["w_kv"] = mm_tn("dw_kv", nkv, dkv, tk=1024)
    token = emit_swap("layer1", {n: grads_full[n] for n in layer1}, (1, d))
    (dnkv,) = mm_nt("kv_dn", dkv, full["w_kv"], ident, [F32], rowvecs=token_rows(token), tm=2048, tk=1024)

    def attn_dn_epilogue(acc, e, rv):
        dx, dgs = norm_bwd_rows(e[0], e[1], [acc, e[2]], rv)
        return [dx, dx] + dgs

    dh2, dh2b, dg_mix1, dg_kv = mm_nt("attn_dn", dq, full["w_q"], attn_dn_epilogue, [F32, BF16], extras=[h2, dh3, dnkv],
                                      rowvecs=[small["norm_mix1"], small["norm_kv"]], n_sums=2, tm=512, tk=d)
    grads_small["norm_mix1"], grads_small["norm_kv"] = dg_mix1, dg_kv
    token = emit_exchange("layer1", dh2b, (1, full["w_out0"].shape[0]))
    dh1, _, _ = mlp_bwd("0", dh2, dh2b, h1, small["norm_mlp0"], full["w_in0"], full["w_out0"], mlp0, token)

    dz, db_glu = glu_bwd(dh1, val, gate)
    grads_small["s5_b_glu"] = db_glu
    grads_full["w_glu"] = mm_tn("dw_glu", ge, dz, tn=1024)
    token = emit_swap("layer0", {n: grads_full[n] for n in layer0}, (1, d))
    (dy2,) = mm_nt("glu_dy", dz, full["w_glu"], lambda acc, e, rv: [plus(acc, rv) * e[0]], [F32], extras=[ge_slope],
                   rowvecs=token_rows(token), tm=512, tk=1024)
    d_skip = handed(small["s5_d"], emit_exchange("layer0", dy2, small["s5_d"]))
    grad_x, dd, drb, drc, dlr, dli, dg_mix0 = s5_bwd(x, small["norm_mix0"], dy2, dh1, d_skip, cs, rb16, rbt16, rct16, lr_t, li_t)
    grads_small["s5_d"] = dd
    grads_small["s5_mats"] = (drb, drc, dlr, dli)
    grads_small["norm_mix0"] = dg_mix0
    return loss_tile, grad_x, grads_small


SMALL_NAMES = ["norm_mix", "norm_mlp", "norm_kv", "norm_final", "s5_a_re", "s5_a_im", "s5_log_dt", "s5_b_re", "s5_b_im",
               "s5_c_re", "s5_c_im", "s5_d", "s5_b_glu", "b_kv", "b_q", "sinks", "b_o"]
BIG_NAMES = ["s5_w_glu", "w_kv", "w_q", "w_o", "w_mlp_in", "w_mlp_out"]
WEIGHT_ORDER = ["norm_mix", "norm_mlp", "norm_kv", "norm_final", "s5_a_re", "s5_a_im", "s5_log_dt", "s5_b_re", "s5_b_im",
                "s5_c_re", "s5_c_im", "s5_d", "s5_w_glu", "s5_b_glu", "w_kv", "b_kv", "w_q", "b_q", "sinks", "w_o", "b_o",
                "w_mlp_in", "w_mlp_out"]


def kernel(x, norm_mix, norm_mlp, norm_kv, norm_final, s5_a_re, s5_a_im, s5_log_dt, s5_b_re, s5_b_im, s5_c_re, s5_c_im, s5_d, s5_w_glu, s5_b_glu, w_kv, b_kv, w_q, b_q, sinks, w_o, b_o, w_mlp_in, w_mlp_out, loss_target, m_norm_mix, m_norm_mlp, m_norm_kv, m_norm_final, m_s5_a_re, m_s5_a_im, m_s5_log_dt, m_s5_b_re, m_s5_b_im, m_s5_c_re, m_s5_c_im, m_s5_d, m_s5_w_glu, m_s5_b_glu, m_w_kv, m_b_kv, m_w_q, m_b_q, m_sinks, m_w_o, m_b_o, m_w_mlp_in, m_w_mlp_out, v_norm_mix, v_norm_mlp, v_norm_kv, v_norm_final, v_s5_a_re, v_s5_a_im, v_s5_log_dt, v_s5_b_re, v_s5_b_im, v_s5_c_re, v_s5_c_im, v_s5_d, v_s5_w_glu, v_s5_b_glu, v_w_kv, v_b_kv, v_w_q, v_b_q, v_sinks, v_w_o, v_b_o, v_w_mlp_in, v_w_mlp_out):
    env = dict(locals())
    w = {n: env[n] for n in WEIGHT_ORDER}
    mom = {n: env["m_" + n] for n in WEIGHT_ORDER}
    var = {n: env["v_" + n] for n in WEIGHT_ORDER}
    d = D_MODEL
    xi, yi, ci = lax.axis_index("x"), lax.axis_index("y"), lax.axis_index("c")
    chip = 2 * xi + yi
    where = jnp.stack([ci, chip]).astype(jnp.int32)

    dsh, bsh = s5_d.shape[1], s5_b_glu.shape[1]
    packed = jnp.concatenate([s5_d.reshape(-1, 128), s5_b_glu.reshape(-1, 128)])
    n_d, n_b = dsh // 128, bsh // 128
    slab = lax.dynamic_update_slice(jnp.zeros((4, 8, 128), F32), jnp.pad(packed, ((0, 8 - n_d - n_b), (0, 0)))[None],
                                    (chip, 0, 0))

    big = [s5_w_glu, w_kv[None], w_q, w_o, w_mlp_in, w_mlp_out]
    entries = [(0, 0, "col"), (1, 0, "row"), (2, 0, "row"), (3, 0, "row"), (4, 0, "col"), (4, 1, "col"),
               (5, 0, "row"), (5, 1, "row")]
    names = ["w_glu", "w_kv", "w_q", "w_o", "w_in0", "w_in1", "w_out0", "w_out1"]
    kinds = dict(zip(names, [k for _, _, k in entries]))
    shard_shapes = dict(zip(names, [tuple(big[a].shape[1:]) for a, _, _ in entries]))

    placed_w = dict(zip(names, cast_place(big, entries, where)))
    placed_w["vectors"], kinds["vectors"], shard_shapes["vectors"] = slab, "slab", None
    gather_groups = {"glu": ["w_glu"], "mlp_in0": ["w_in0"], "mlp_out0": ["w_out0"], "attn": ["w_kv", "w_q", "w_o"],
                     "mlp_in1": ["w_in1"], "mlp_out1": ["w_out1"]}
    order = ["vectors"] + [n for members in gather_groups.values() for n in members]
    send, recv, thru, log_dt = gather_start([placed_w[n] for n in order], [kinds[n] for n in order],
                                            [shard_shapes[n] for n in order], s5_log_dt)
    started = dict(zip(order, thru))
    (gathered_rows,) = gather_wait("gather_wait_vectors", send, recv, [started["vectors"]], ["slab"], [None], None, 0)
    d_full = gathered_rows[:, 0:n_d].reshape(1, -1)
    bglu_full = gathered_rows[:, n_d:n_d + n_b].reshape(1, -1)

    forwarding = {}

    def ahead(group, after, carry, passing=()):
        members = gather_groups[group]
        ks, shapes = [kinds[n] for n in members], [shard_shapes[n] for n in members]
        d2d_send, d2d_recv, landed, passed, tok = forward_start(
            "forward_start_" + group, send, recv, [started[n] for n in members], ks, shapes, after,
            order.index(members[0]), carry, passing)
        forwarding[group] = (d2d_send, d2d_recv, landed)
        return passed if passing else tok

    def need(group, after, then=None):
        members = gather_groups[group]
        ks, shapes = [kinds[n] for n in members], [shard_shapes[n] for n in members]
        if group in forwarding:
            arrays = forward_wait("forward_wait_" + group, *forwarding[group], ks, shapes, after)
        else:
            landed = gather_wait("gather_wait_" + group, send, recv, [started[n] for n in members], ks, shapes, after,
                                 order.index(members[0]))
            arrays = forward_halves("forward_halves_" + group, landed, ks, shapes)
        if then is not None:
            arrays = ahead(then, after, (8, 128), arrays)
        return dict(zip(members, arrays))

    swapping, exchanging = {}, {}

    def emit_swap(group, partial, carry):
        members = list(partial)
        send, recv, mine, lands, tok = swap_start("swap_start_" + group, [partial[n] for n in members],
                                                  [kinds[n] for n in members], carry)
        swapping[group] = (members, send, recv, mine, lands)
        return tok

    def emit_exchange(group, after, carry):
        members, send, recv, mine, lands = swapping[group]
        ks, shapes = [kinds[n] for n in members], [shard_shapes[n] for n in members]
        mine, landed = swap_wait("swap_wait_" + group, send, recv, mine, lands, ks, after)
        sums = add_halves("add_halves_" + group, mine, landed, ks, where)
        send, recv, parts, lands, tok = exchange_start("exchange_start_" + group, sums, ks, shapes, carry)
        exchanging[group] = (members, send, recv, parts, lands)
        return tok

    s5_args = (s5_a_re[0], s5_a_im[0], log_dt[0], s5_b_re[0], s5_b_im[0])
    small = {
        "norm_mix0": norm_mix[0:1], "norm_mix1": norm_mix[1:2], "norm_mlp0": norm_mlp[0:1], "norm_mlp1": norm_mlp[1:2],
        "norm_kv": norm_kv.reshape(1, d), "norm_final": norm_final.reshape(1, d), "s5_operands": s5_prep(*s5_args, s5_c_re[0], s5_c_im[0]),
        "s5_d": d_full, "s5_b_glu": bglu_full,
        "b_kv": b_kv.reshape(1, -1), "b_q": b_q, "sinks": sinks, "b_o": b_o,
    }
    loss_row, grad_x, gs = _local_step(x[0], loss_target[0], small, need, ahead, emit_swap, emit_exchange)

    mats, lams = s5_compact(*gs["s5_mats"])
    rows = [gs["norm_mix0"], gs["norm_mix1"], gs["norm_mlp0"], gs["norm_mlp1"], gs["norm_kv"], gs["norm_final"], gs["s5_d"],
            gs["b_q"], gs["b_o"], gs["s5_b_glu"], gs["b_kv"], gs["sinks"], loss_row, jnp.zeros((2, d), F32)]
    small_send, small_recv, small_parts, small_lands = reduce_start(
        reduce_swap([jnp.concatenate(rows, axis=0), lams, mats], [F32, F32, BF16]))

    reduced = [None] * len(big)
    where_of = dict(zip(names, entries))
    for group in ("layer1", "layer0"):
        members, send, recv, parts, lands = exchanging[group]
        ks, shapes = [kinds[n] for n in members], [shard_shapes[n] for n in members]
        parts, lands = exchange_wait("exchange_wait_" + group, send, recv, parts, lands, ks, shapes, small_lands[-1])
        targets = [where_of[n][0] for n in members]
        sums = sum_shards("sum_shards_" + group, parts, lands, ks, shapes, where, [where_of[n][1] for n in members],
                          [big[a].shape[0] for a in targets], [reduced[a] for a in targets])
        for a, arr in zip(targets, sums):
            reduced[a] = arr
    share_send, share_recv, reduced, _ = share_start(reduced, entries, (8, 128))

    vecs, lams, mats = reduce_share(*reduce_wait(small_send, small_recv, small_parts, small_lands, reduced[0]))
    grads = split_vectors(where, vecs, dsh, bsh)
    loss = grads.pop("loss")[0, 0]
    g_are, g_aim, g_dt, g_bre, g_bim, dc_re, dc_im = s5_param_bwd(mats, lams, *s5_args)
    grads.update({"s5_a_re": g_are[None], "s5_a_im": g_aim[None], "s5_log_dt": g_dt[None], "s5_b_re": g_bre[None],
                  "s5_b_im": g_bim[None], "s5_c_re": dc_re[None], "s5_c_im": dc_im[None]})

    delta, new_m, new_v = {}, {}, {}

    def view(n, a):
        return a.reshape(1, -1) if a.ndim == 1 else jnp.swapaxes(a, -1, -2) if n in ("s5_b_re", "s5_b_im") else a

    sw, sg, sm, sv = ([view(n, t[n]) for n in SMALL_NAMES] for t in (w, grads, mom, var))
    for n, a, b, c_ in zip(SMALL_NAMES, *adamw_native("adamw_small", sw, sg, sm, sv)):
        delta[n], new_m[n], new_v[n] = (view(n, t) if t.ndim == 4 else t for t in (a, b, c_))

    reduced = share_wait(share_send, share_recv, reduced, entries, new_v["s5_c_re"])
    for n, g in zip(BIG_NAMES, reduced):
        grads[n] = g.reshape(w[n].shape)
    flat = lambda t: [t[n].reshape(-1, t[n].shape[-1]) for n in BIG_NAMES]
    for table, arrays in zip((grads, delta, new_m, new_v), adamw("adamw_big", flat(w), flat(grads), flat(mom), flat(var))):
        for n, a in zip(BIG_NAMES, arrays):
            table[n] = a.reshape(w[n].shape)

    out = [loss.reshape(()), grad_x[None]]
    for table in (grads, delta, new_m, new_v):
        out += [table[n].reshape(w[n].shape) for n in WEIGHT_ORDER]
    return tuple(out)
```

```python
import math

import jax
import jax.numpy as jnp
from jax import lax
from jax.experimental import pallas as pl
from jax.experimental.pallas import tpu as pltpu

F32 = jnp.float32
BF16 = jnp.bfloat16

D_MODEL = 1024
S5_GROUPS = 64
S5_GROUP = 16
S5_STATE = 64
N_KV = 4
N_Q = 16
HEAD_DIM = 64
BLOCK = 128
NORM_EPS = 1e-5
LAMBDA_RE_MAX = -1e-4
ADAM_LR, ADAM_B1, ADAM_B2, ADAM_EPS, ADAM_WD, ADAM_STEP = 0.001, 0.9, 0.999, 1e-08, 0.01, 10

VMEM_LIMIT_BYTES = 56 * 1024 * 1024
S5_CHUNK = 256
S5_BLOCKS = 4
MESH = pl.DeviceIdType.MESH


def _params(sem=None):
    return pltpu.CompilerParams(dimension_semantics=sem, vmem_limit_bytes=VMEM_LIMIT_BYTES)


def _sds(shape, dtype):
    return jax.ShapeDtypeStruct(shape, dtype)


def _rms_hat(xv):
    r = lax.rsqrt(jnp.mean(xv * xv, axis=-1, keepdims=True) + NORM_EPS)
    return xv * r, r


def mm_nn(name, a, w, col_offsets, n_out, epilogue, out_dtypes, extras=(), rowvecs=(), n_sums=0, tm=1024, tn=512):
    m, k = a.shape
    tm, tn = min(tm, m), min(tn, n_out)
    nw, ne, nr, no = len(col_offsets), len(extras), len(rowvecs), len(out_dtypes)

    def body(a_ref, *refs):
        w_refs, e_refs, r_refs = refs[:nw], refs[nw:nw + ne], refs[nw + ne:nw + ne + nr]
        o_refs, s_refs = refs[nw + ne + nr:nw + ne + nr + no], refs[nw + ne + nr + no:]
        av = a_ref[...]
        accs = [jnp.dot(av, w_ref[...], preferred_element_type=F32) for w_ref in w_refs]
        outs = epilogue(accs, [e[...] for e in e_refs], [r[...] for r in r_refs])
        for o_ref, o in zip(o_refs, outs[:no]):
            o_ref[...] = o.astype(o_ref.dtype)
        if n_sums:
            @pl.when(pl.program_id(1) == 0)
            def _():
                for s_ref in s_refs:
                    s_ref[...] = jnp.zeros_like(s_ref)

            for s_ref, val in zip(s_refs, outs[no:]):
                s_ref[...] += val

    def wspec(off):
        return pl.BlockSpec((k, tn), lambda j, i, off=off: (0, off // tn + j))

    def rspec(off):
        return pl.BlockSpec((1, tn), lambda j, i, off=off: (0, off // tn + j))

    tile = pl.BlockSpec((tm, tn), lambda j, i: (i, j))
    in_specs = ([pl.BlockSpec((tm, k), lambda j, i: (i, 0))] + [wspec(o) for o in col_offsets]
                + [tile] * ne + [rspec(o) for _, o in rowvecs])
    sem = ("parallel", "arbitrary") if n_sums else ("parallel", "parallel")
    return pl.pallas_call(
        body, grid=(n_out // tn, m // tm), in_specs=in_specs,
        out_specs=[tile] * no + [pl.BlockSpec((1, tn), lambda j, i: (0, j))] * n_sums,
        out_shape=[_sds((m, n_out), dt) for dt in out_dtypes] + [_sds((1, n_out), F32)] * n_sums, name=name,
        compiler_params=_params(sem))(a, *([w] * nw), *extras, *[r for r, _ in rowvecs])


def mm_nt(name, g, w, epilogue, out_dtypes, extras=(), rowvecs=(), n_sums=0, tm=512, tk=512):
    m, n = g.shape
    k = w.shape[0]
    tm, tk = min(tm, m), min(tk, k)
    ne, nr, no = len(extras), len(rowvecs), len(out_dtypes)

    def body(g_ref, w_ref, *refs):
        e_refs, r_refs, o_refs, s_refs = refs[:ne], refs[ne:ne + nr], refs[ne + nr:ne + nr + no], refs[ne + nr + no:]
        acc = lax.dot_general(g_ref[...], w_ref[...], (((1,), (1,)), ((), ())), preferred_element_type=F32)
        outs = epilogue(acc, [e[...] for e in e_refs], [r[...] for r in r_refs])
        for o_ref, o in zip(o_refs, outs[:no]):
            o_ref[...] = o.astype(o_ref.dtype)
        if n_sums:
            @pl.when(pl.program_id(0) == 0)
            def _():
                for s_ref in s_refs:
                    s_ref[...] = jnp.zeros_like(s_ref)

            for s_ref, val in zip(s_refs, outs[no:]):
                s_ref[...] += val

    tile = pl.BlockSpec((tm, tk), lambda i, j: (i, j))
    vec = pl.BlockSpec((1, tk), lambda i, j: (0, j))
    sem = ("arbitrary", "parallel") if n_sums else ("parallel", "parallel")
    return pl.pallas_call(
        body, grid=(m // tm, k // tk),
        in_specs=[pl.BlockSpec((tm, n), lambda i, j: (i, 0)), pl.BlockSpec((tk, n), lambda i, j: (j, 0))]
        + [tile] * ne + [vec] * nr,
        out_specs=[tile] * no + [vec] * n_sums,
        out_shape=[_sds((m, k), dt) for dt in out_dtypes] + [_sds((1, k), F32)] * n_sums, name=name,
        compiler_params=_params(sem))(g, w, *extras, *rowvecs)


def mm_tn(name, a, g, tk=512, tn=512):
    m, k = a.shape
    n = g.shape[1]
    tk, tn = min(tk, k), min(tn, n)

    def body(a_ref, g_ref, o_ref):
        acc = lax.dot_general(a_ref[...], g_ref[...], (((0,), (0,)), ((), ())), preferred_element_type=F32)
        o_ref[...] = acc.astype(o_ref.dtype)

    return pl.pallas_call(
        body, grid=(k // tk, n // tn),
        in_specs=[pl.BlockSpec((m, tk), lambda i, j: (0, i)), pl.BlockSpec((m, tn), lambda i, j: (0, j))],
        out_specs=pl.BlockSpec((tk, tn), lambda i, j: (i, j)), out_shape=_sds((k, n), BF16), name=name,
        compiler_params=_params(("parallel", "parallel")))(a, g)


def _row_mask(tc):
    row = lax.broadcasted_iota(jnp.int32, (8 * tc, 256), 0) % 8
    col = lax.broadcasted_iota(jnp.int32, (8 * tc, 256), 1) // 32
    return row == col


def _expand_rows(val, mask):
    tc, width = val.shape
    rep = jnp.broadcast_to(val[:, None, :], (tc, 8, width)).reshape(8 * tc, width)
    return jnp.where(mask, rep, 0.0).astype(BF16)


def _stage(ref, val):
    ref[0] = val[:, 0:128]
    ref[1] = val[:, 128:256]


def _gather_rows(src_ref, tc):
    halves = []
    for half in range(2):
        col = lax.broadcasted_iota(jnp.int32, (tc, 128), 1) // 32 + 4 * half
        out = jnp.zeros((tc, 128), F32)
        for s8 in range(4 * half, 4 * half + 4):
            out = jnp.where(col == s8, src_ref.at[half][pl.ds(s8, tc, stride=8), :], out)
        halves.append(out)
    return jnp.concatenate(halves, axis=1)


def _repeat(n, by, step, carry):
    def trip(i, c):
        for j in range(by):
            c = step(i * by + j, c)
        return c

    return lax.fori_loop(0, n // by, trip, carry)


def _gelu_and_slope(x):
    c = math.sqrt(2.0 / math.pi)
    t = jnp.tanh(c * (x + 0.044715 * x * x * x))
    return 0.5 * x * (1.0 + t), 0.5 * (1.0 + t) + 0.5 * x * (1.0 - t * t) * c * (1.0 + 3.0 * 0.044715 * x * x)


def s5_fwd(x, gain, d_skip, rb, rc, lam_r, lam_i):
    n_rows = x.shape[0]
    tc = min(S5_CHUNK, n_rows)
    nc = n_rows // tc

    def body(x_ref, g_ref, d_ref, rb_ref, rc_ref, lr_ref, li_ref, ge_ref, slope_ref, cs_ref, bux, yrows, carry):
        i = pl.program_id(0)
        u = _rms_hat(x_ref[...])[0] * g_ref[...]

        @pl.when(i == 0)
        def _():
            carry[...] = jnp.zeros_like(carry)

        cs_ref[0] = carry[...]
        mask = _row_mask(tc)
        for blk in range(S5_BLOCKS):
            lhs = _expand_rows(u[:, blk * 256:(blk + 1) * 256], mask)
            bux[blk] = jnp.dot(lhs, rb_ref[blk], preferred_element_type=F32)
        lam = [(lr_ref[blk], li_ref[blk]) for blk in range(S5_BLOCKS)]

        def step(t, c):
            r0 = pl.multiple_of(t * 8, 8)
            new = []
            for blk in range(S5_BLOCKS):
                xr, xi = c[2 * blk], c[2 * blk + 1]
                lr, li = lam[blk]
                nr = lr * xr - li * xi + bux[blk, pl.ds(r0, 8), 0:128]
                ni = lr * xi + li * xr + bux[blk, pl.ds(r0, 8), 128:256]
                bux[blk, pl.ds(r0, 8), 0:128] = nr
                bux[blk, pl.ds(r0, 8), 128:256] = ni
                new += [nr, ni]
            return tuple(new)

        c0 = []
        for blk in range(S5_BLOCKS):
            c0 += [carry[blk, :, 0:128], carry[blk, :, 128:256]]
        cn = _repeat(tc, 4, step, tuple(c0))
        for blk in range(S5_BLOCKS):
            carry[blk, :, 0:128] = cn[2 * blk]
            carry[blk, :, 128:256] = cn[2 * blk + 1]
        for blk in range(S5_BLOCKS):
            _stage(yrows, jnp.dot(bux[blk].astype(BF16), rc_ref[blk], preferred_element_type=F32))
            sl = slice(blk * 256, (blk + 1) * 256)
            ge, slope = _gelu_and_slope(_gather_rows(yrows, tc) + d_ref[:, sl] * u[:, sl])
            slope_ref[:, sl] = slope
            ge_ref[:, sl] = ge.astype(BF16)

    row = pl.BlockSpec((tc, D_MODEL), lambda i: (i, 0))
    vec = pl.BlockSpec((1, D_MODEL), lambda i: (0, 0))
    mat = pl.BlockSpec((S5_BLOCKS, 256, 256), lambda i: (0, 0, 0))
    lamspec = pl.BlockSpec((S5_BLOCKS, 8, 128), lambda i: (0, 0, 0))
    return pl.pallas_call(
        body, grid=(nc,),
        in_specs=[row, vec, vec, mat, mat, lamspec, lamspec],
        out_specs=[row, row, pl.BlockSpec((1, S5_BLOCKS, 8, 256), lambda i: (i, 0, 0, 0))],
        out_shape=[_sds((n_rows, D_MODEL), BF16), _sds((n_rows, D_MODEL), F32), _sds((nc, S5_BLOCKS, 8, 256), F32)],
        scratch_shapes=[pltpu.VMEM((S5_BLOCKS, 8 * tc, 256), F32), pltpu.VMEM((2, 8 * tc, 128), F32),
                        pltpu.VMEM((S5_BLOCKS, 8, 256), F32)],
        name="s5_fwd", compiler_params=_params(("arbitrary",)))(x, gain, d_skip, rb, rc, lam_r, lam_i)


def s5_bwd(x, gain, dy2, res, d_skip, cs, rb, rbt, rct, lam_r, lam_i):
    n_rows = x.shape[0]
    tc = min(S5_CHUNK, n_rows)
    nc = n_rows // tc

    def body(x_ref, g_ref, dy_ref, res_ref, d_ref, cs_ref, rb_ref, rbt_ref, rct_ref, lr_ref, li_ref,
             dx_ref, dd_ref, drb_ref, drc_ref, dlr_ref, dli_ref, dg_ref, tmp, du, lhsu, lhsd, xs, adj, acarry):
        i = pl.program_id(0)
        u = _rms_hat(x_ref[...])[0] * g_ref[...]

        @pl.when(i == 0)
        def _():
            acarry[...] = jnp.zeros_like(acarry)
            dd_ref[...] = jnp.zeros_like(dd_ref)
            drb_ref[...] = jnp.zeros_like(drb_ref)
            drc_ref[...] = jnp.zeros_like(drc_ref)
            dlr_ref[...] = jnp.zeros_like(dlr_ref)
            dli_ref[...] = jnp.zeros_like(dli_ref)
            dg_ref[...] = jnp.zeros_like(dg_ref)

        dd_ref[...] += jnp.sum(dy_ref[...] * u, axis=0, keepdims=True)
        mask = _row_mask(tc)
        for blk in range(S5_BLOCKS):
            sl = slice(blk * 256, (blk + 1) * 256)
            lhsu[blk] = _expand_rows(u[:, sl], mask)
            xs[blk] = jnp.dot(lhsu[blk], rb_ref[blk], preferred_element_type=F32)
            lhsd[blk] = _expand_rows(dy_ref[:, sl], mask)
            adj[blk] = jnp.dot(lhsd[blk], rct_ref[blk], preferred_element_type=F32)
        lam = [(lr_ref[blk], li_ref[blk]) for blk in range(S5_BLOCKS)]

        def fstep(t, c):
            r0 = pl.multiple_of(t * 8, 8)
            new = []
            for blk in range(S5_BLOCKS):
                xr, xi = c[2 * blk], c[2 * blk + 1]
                lr, li = lam[blk]
                nr = lr * xr - li * xi + xs[blk, pl.ds(r0, 8), 0:128]
                ni = lr * xi + li * xr + xs[blk, pl.ds(r0, 8), 128:256]
                xs[blk, pl.ds(r0, 8), 0:128] = nr
                xs[blk, pl.ds(r0, 8), 128:256] = ni
                new += [nr, ni]
            return tuple(new)

        c0 = []
        for blk in range(S5_BLOCKS):
            c0 += [cs_ref[0, blk, :, 0:128], cs_ref[0, blk, :, 128:256]]
        _repeat(tc, 4, fstep, tuple(c0))

        def bstep(k, c):
            t = tc - 1 - k
            r0 = pl.multiple_of(t * 8, 8)
            rp = pl.multiple_of(jnp.maximum(t - 1, 0) * 8, 8)
            first = t == 0
            new_a, new_g = [], []
            for blk in range(S5_BLOCKS):
                ar, ai = c[0][2 * blk], c[0][2 * blk + 1]
                glr, gli = c[1][2 * blk], c[1][2 * blk + 1]
                lr, li = lam[blk]
                nr = lr * ar + li * ai + adj[blk, pl.ds(r0, 8), 0:128]
                ni = lr * ai - li * ar + adj[blk, pl.ds(r0, 8), 128:256]
                adj[blk, pl.ds(r0, 8), 0:128] = nr
                adj[blk, pl.ds(r0, 8), 128:256] = ni
                pr = jnp.where(first, cs_ref[0, blk, :, 0:128], xs[blk, pl.ds(rp, 8), 0:128])
                pi = jnp.where(first, cs_ref[0, blk, :, 128:256], xs[blk, pl.ds(rp, 8), 128:256])
                new_a += [nr, ni]
                new_g += [glr + nr * pr + ni * pi, gli + ni * pr - nr * pi]
            return tuple(new_a), tuple(new_g)

        a0, g0 = [], []
        for blk in range(S5_BLOCKS):
            a0 += [acarry[blk, :, 0:128], acarry[blk, :, 128:256]]
            g0 += [dlr_ref[blk], dli_ref[blk]]
        an, gn = _repeat(tc, 2, bstep, (tuple(a0), tuple(g0)))
        for blk in range(S5_BLOCKS):
            acarry[blk, :, 0:128] = an[2 * blk]
            acarry[blk, :, 128:256] = an[2 * blk + 1]
            dlr_ref[blk] = gn[2 * blk]
            dli_ref[blk] = gn[2 * blk + 1]
        for blk in range(S5_BLOCKS):
            sl = slice(blk * 256, (blk + 1) * 256)
            ab = adj[blk].astype(BF16)
            _stage(tmp, jnp.dot(ab, rbt_ref[blk], preferred_element_type=F32))
            du[:, sl] = _gather_rows(tmp, tc) + d_ref[:, sl] * dy_ref[:, sl]
            drb_ref[blk] += lax.dot_general(lhsu[blk], ab, (((0,), (0,)), ((), ())), preferred_element_type=F32)
            drc_ref[blk] += lax.dot_general(lhsd[blk], xs[blk].astype(BF16), (((0,), (0,)), ((), ())),
                                            preferred_element_type=F32)
        xh, r = _rms_hat(x_ref[...])
        dg_ref[...] += jnp.sum(du[...] * xh, axis=0, keepdims=True)
        dxh = du[...] * g_ref[...]
        dx_ref[...] = r * (dxh - xh * jnp.mean(dxh * xh, axis=-1, keepdims=True)) + res_ref[...]

    rev = pl.BlockSpec((tc, D_MODEL), lambda i: (nc - 1 - i, 0))
    vec = pl.BlockSpec((1, D_MODEL), lambda i: (0, 0))
    mat = pl.BlockSpec((S5_BLOCKS, 256, 256), lambda i: (0, 0, 0))
    lamspec = pl.BlockSpec((S5_BLOCKS, 8, 128), lambda i: (0, 0, 0))
    big = pltpu.VMEM((S5_BLOCKS, 8 * tc, 256), F32)
    bigb = pltpu.VMEM((S5_BLOCKS, 8 * tc, 256), BF16)
    return pl.pallas_call(
        body, grid=(nc,),
        in_specs=[rev, vec, rev, rev, vec, pl.BlockSpec((1, S5_BLOCKS, 8, 256), lambda i: (nc - 1 - i, 0, 0, 0)),
                  mat, mat, mat, lamspec, lamspec],
        out_specs=[rev, vec, mat, mat, lamspec, lamspec, vec],
        out_shape=[_sds((n_rows, D_MODEL), F32), _sds((1, D_MODEL), F32), _sds((S5_BLOCKS, 256, 256), F32),
                   _sds((S5_BLOCKS, 256, 256), F32), _sds((S5_BLOCKS, 8, 128), F32), _sds((S5_BLOCKS, 8, 128), F32),
                   _sds((1, D_MODEL), F32)],
        scratch_shapes=[pltpu.VMEM((2, 8 * tc, 128), F32), pltpu.VMEM((tc, D_MODEL), F32), bigb, bigb, big, big,
                        pltpu.VMEM((S5_BLOCKS, 8, 256), F32)],
        name="s5_bwd", compiler_params=_params(("arbitrary",)))(
            x, gain, dy2, res, d_skip, cs, rb, rbt, rct, lam_r, lam_i)


def _s5_views(a_re, a_im, log_dt, b_re, b_im):
    return a_re[:, None, :], a_im[:, None, :], log_dt[:, None, None], jnp.swapaxes(b_re, 1, 2), jnp.swapaxes(b_im, 1, 2)


def _s5_factors(a_re, a_im, log_dt):
    lr, li, dt = jnp.minimum(a_re, LAMBDA_RE_MAX), a_im, jnp.exp(log_dt)
    mag, ang = jnp.exp(lr * dt), li * dt
    lbr, lbi = mag * jnp.cos(ang), mag * jnp.sin(ang)
    den = lr * lr + li * li
    fr, fi = ((lbr - 1.0) * lr + lbi * li) / den, (lbi * lr - (lbr - 1.0) * li) / den
    return lr, li, dt, lbr, lbi, fr, fi, den


def s5_prep(a_re, a_im, log_dt, b_re, b_im, c_re, c_im):
    def body(ar_ref, ai_ref, t_ref, br_ref, bi_ref, cr_ref, ci_ref, rb_ref, rbt_ref, rc_ref, rct_ref, lr_ref, li_ref):
        _, _, _, lbr, lbi, fr, fi, _ = _s5_factors(ar_ref[...], ai_ref[...], t_ref[...])
        lr_ref[...] = lbr
        li_ref[...] = lbi
        bre = fr * br_ref[...] - fi * bi_ref[...]
        bim = fr * bi_ref[...] + fi * br_ref[...]
        even = (lax.broadcasted_iota(jnp.int32, (256, S5_STATE), 0) // S5_GROUP) % 2 == 0

        def assemble(re, im):
            re, im = re.reshape(256, S5_STATE), im.reshape(256, S5_STATE)
            return jnp.concatenate([jnp.where(even, re, 0.0), jnp.where(even, 0.0, re), jnp.where(even, im, 0.0),
                                    jnp.where(even, 0.0, im)], axis=1)

        for blk in range(S5_BLOCKS):
            sl = slice(16 * blk, 16 * blk + 16)
            rb = assemble(bre[sl], bim[sl])
            rct = assemble(cr_ref[sl], -ci_ref[sl])
            rb_ref[blk] = rb.astype(BF16)
            rbt_ref[blk] = rb.T.astype(BF16)
            rct_ref[blk] = rct.astype(BF16)
            rc_ref[blk] = rct.T.astype(BF16)

    vm = pl.BlockSpec(memory_space=pltpu.VMEM)
    mat = _sds((S5_BLOCKS, 256, 256), BF16)
    lam = _sds((S5_GROUPS, 1, S5_STATE), F32)
    rb, rbt, rc, rct, lam_r, lam_i = pl.pallas_call(
        body, in_specs=[vm] * 7, out_specs=[vm] * 6, out_shape=[mat, mat, mat, mat, lam, lam], name="s5_prep",
        compiler_params=_params())(*_s5_views(a_re, a_im, log_dt, b_re, b_im), c_re, c_im)
    return rb, rbt, rc, rct, lam_r.reshape(S5_BLOCKS, 8, 128), lam_i.reshape(S5_BLOCKS, 8, 128)


def s5_param_bwd(mats, lams, a_re, a_im, log_dt, b_re, b_im):
    def body(m_ref, glr_ref, gli_ref, ar_ref, ai_ref, t_ref, br_ref, bi_ref,
             dar_ref, dai_ref, dt_ref, dbr_ref, dbi_ref, dcr_ref, dci_ref):
        lr, li, dt, lbr, lbi, fr, fi, den = _s5_factors(ar_ref[...], ai_ref[...], t_ref[...])
        shape = (S5_GROUPS, S5_GROUP, S5_STATE)
        gbr, gbi = m_ref[0:1024, 0:64].reshape(shape), m_ref[0:1024, 64:128].reshape(shape)
        dcr_ref[...] = m_ref[1024:2048, 0:64].reshape(shape)
        dci_ref[...] = -m_ref[1024:2048, 64:128].reshape(shape)
        br, bi = br_ref[...], bi_ref[...]
        dbr_ref[...] = fr * gbr + fi * gbi
        dbi_ref[...] = fr * gbi - fi * gbr
        dfr = jnp.sum(gbr * br + gbi * bi, axis=1, keepdims=True)
        dfi = jnp.sum(gbi * br - gbr * bi, axis=1, keepdims=True)
        nr, ni = (dfr * lr - dfi * li) / den, (dfr * li + dfi * lr) / den
        qr, qi = (fr * lr + fi * li) / den, (fi * lr - fr * li) / den
        lam_r, lam_i = -(dfr * qr + dfi * qi), -(dfi * qr - dfr * qi)
        gr, gi = glr_ref[...] + nr, gli_ref[...] + ni
        zr, zi = gr * lbr + gi * lbi, gi * lbr - gr * lbi
        a = ar_ref[...]
        dar_ref[...] = (lam_r + zr * dt) * jnp.where(a < LAMBDA_RE_MAX, 1.0, jnp.where(a == LAMBDA_RE_MAX, 0.5, 0.0))
        dai_ref[...] = lam_i + zi * dt
        dt_ref[...] = jnp.sum(zr * lr + zi * li, axis=2, keepdims=True) * dt

    vm = pl.BlockSpec(memory_space=pltpu.VMEM)
    state = _sds((S5_GROUPS, 1, S5_STATE), F32)
    wide = _sds((S5_GROUPS, S5_GROUP, S5_STATE), F32)
    glr = lams[0:32].reshape(S5_GROUPS, 1, S5_STATE)
    gli = lams[32:64].reshape(S5_GROUPS, 1, S5_STATE)
    dar, dai, ddt, dbr, dbi, dcr, dci = pl.pallas_call(
        body, in_specs=[vm] * 8, out_specs=[vm] * 7,
        out_shape=[state, state, _sds((S5_GROUPS, 1, 1), F32), wide, wide, wide, wide], name="s5_param_bwd",
        compiler_params=_params())(mats, glr, gli, *_s5_views(a_re, a_im, log_dt, b_re, b_im))
    return (dar.reshape(S5_GROUPS, S5_STATE), dai.reshape(S5_GROUPS, S5_STATE), ddt.reshape(S5_GROUPS),
            jnp.swapaxes(dbr, 1, 2), jnp.swapaxes(dbi, 1, 2), dcr, dci)


def s5_compact(drb, drct, dlr, dli):
    def body(drb_ref, drct_ref, dlr_ref, dli_ref, o_ref, lam_ref):
        even = (lax.broadcasted_iota(jnp.int32, (256, 64), 0) // S5_GROUP) % 2 == 0
        for blk in range(S5_BLOCKS):
            for k, ref in enumerate((drb_ref, drct_ref)):
                m = ref[blk]
                re = jnp.where(even, m[:, 0:64], m[:, 64:128])
                im = jnp.where(even, m[:, 128:192], m[:, 192:256])
                o_ref[pl.ds(k * 1024 + blk * 256, 256), :] = jnp.concatenate([re, im], axis=1)
            lam_ref[pl.ds(blk * 8, 8), :] = dlr_ref[blk]
            lam_ref[pl.ds(32 + blk * 8, 8), :] = dli_ref[blk]

    vm = pl.BlockSpec(memory_space=pltpu.VMEM)
    return pl.pallas_call(body, in_specs=[vm] * 4, out_specs=[vm, vm], out_shape=[_sds((2048, 128), F32), _sds((64, 128), F32)],
                          name="s5_compact", compiler_params=_params())(drb, drct, dlr, dli)


NEG = -1e30


GROUP = N_Q // N_KV


def _attn_masks(n):
    qi = lax.broadcasted_iota(jnp.int32, (GROUP * BLOCK, BLOCK), 0) % BLOCK
    kj = lax.broadcasted_iota(jnp.int32, (GROUP * BLOCK, BLOCK), 1)
    return jnp.logical_and(kj > qi, n > 0), kj <= qi


def _stack_heads(ref, kh):
    return jnp.concatenate([ref[:, (GROUP * kh + g) * HEAD_DIM:(GROUP * kh + g + 1) * HEAD_DIM] for g in range(GROUP)], axis=0)


def _unstack_heads(val):
    return jnp.concatenate([val[g * BLOCK:(g + 1) * BLOCK] for g in range(GROUP)], axis=1)


def _sink_column(sink_ref, kh):
    grp = lax.broadcasted_iota(jnp.int32, (GROUP * BLOCK, 1), 0) // BLOCK
    col = jnp.zeros((GROUP * BLOCK, 1), F32)
    for g in range(GROUP):
        col = jnp.where(grp == g, sink_ref[GROUP * kh + g], col)
    return col, grp


def _attn_exp(q4, kp, kc, sink, mask_p, mask_c):
    scale = 1.0 / math.sqrt(HEAD_DIM)
    nt = (((1,), (1,)), ((), ()))
    sp = jnp.where(mask_p, lax.dot_general(q4, kp, nt, preferred_element_type=F32) * scale, NEG)
    sc = jnp.where(mask_c, lax.dot_general(q4, kc, nt, preferred_element_type=F32) * scale, NEG)
    m = jnp.maximum(jnp.maximum(jnp.max(sp, axis=-1, keepdims=True), jnp.max(sc, axis=-1, keepdims=True)), sink)
    pp = jnp.exp(sp - m)
    pc = jnp.exp(sc - m)
    ps = jnp.exp(sink - m)
    inv = 1.0 / (jnp.sum(pp, axis=-1, keepdims=True) + jnp.sum(pc, axis=-1, keepdims=True) + ps)
    return pp, pc, ps, inv


def attn_fwd(q, kv, sinks):
    n_rows = q.shape[0]
    nb = n_rows // BLOCK

    def body(sink_ref, q_ref, kvp_ref, kvc_ref, o_ref):
        n = pl.program_id(0)
        mask_p, mask_c = _attn_masks(n)
        outs = []
        for kh in range(N_KV):
            ks, vs = slice(kh * HEAD_DIM, (kh + 1) * HEAD_DIM), slice((N_KV + kh) * HEAD_DIM, (N_KV + kh + 1) * HEAD_DIM)
            sink, _ = _sink_column(sink_ref, kh)
            pp, pc, _, inv = _attn_exp(_stack_heads(q_ref, kh), kvp_ref[:, ks], kvc_ref[:, ks], sink, mask_p, mask_c)
            o4 = (jnp.dot(pp.astype(BF16), kvp_ref[:, vs], preferred_element_type=F32)
                  + jnp.dot(pc.astype(BF16), kvc_ref[:, vs], preferred_element_type=F32)) * inv
            outs.append(_unstack_heads(o4))
        o_ref[...] = jnp.concatenate(outs, axis=1).astype(BF16)

    kvw = 2 * N_KV * HEAD_DIM
    return pl.pallas_call(
        body, grid=(nb,),
        in_specs=[pl.BlockSpec(memory_space=pltpu.SMEM), pl.BlockSpec((BLOCK, D_MODEL), lambda n: (n, 0)),
                  pl.BlockSpec((BLOCK, kvw), lambda n: (jnp.maximum(n - 1, 0), 0)), pl.BlockSpec((BLOCK, kvw), lambda n: (n, 0))],
        out_specs=pl.BlockSpec((BLOCK, D_MODEL), lambda n: (n, 0)), out_shape=_sds((n_rows, D_MODEL), BF16),
        name="attn_fwd", compiler_params=_params(("parallel",)))(sinks, q, kv, kv)


def attn_bwd(q, kv, do, sinks):
    n_rows = q.shape[0]
    nb = n_rows // BLOCK
    kvw = 2 * N_KV * HEAD_DIM
    tn = (((0,), (0,)), ((), ()))
    nt = (((1,), (1,)), ((), ()))
    scale = 1.0 / math.sqrt(HEAD_DIM)

    def body(sink_ref, q_ref, kvp_ref, kvc_ref, do_ref, dq_ref, dbq_ref, dprev_ref, dcur_ref, dsink_ref):
        n = pl.program_id(0)
        mask_p, mask_c = _attn_masks(n)
        lane = lax.broadcasted_iota(jnp.int32, (1, D_MODEL), 1)
        dqs, dsink = [], jnp.zeros((1, D_MODEL), F32)
        dkp, dkc, dvp, dvc = [], [], [], []
        for kh in range(N_KV):
            ks, vs = slice(kh * HEAD_DIM, (kh + 1) * HEAD_DIM), slice((N_KV + kh) * HEAD_DIM, (N_KV + kh + 1) * HEAD_DIM)
            q4, do4 = _stack_heads(q_ref, kh), _stack_heads(do_ref, kh)
            kp, kc, vp, vc = kvp_ref[:, ks], kvc_ref[:, ks], kvp_ref[:, vs], kvc_ref[:, vs]
            sink, grp = _sink_column(sink_ref, kh)
            pp, pc, ps, inv = _attn_exp(q4, kp, kc, sink, mask_p, mask_c)
            pp, pc = pp * inv, pc * inv
            dpp = lax.dot_general(do4, vp, nt, preferred_element_type=F32)
            dpc = lax.dot_general(do4, vc, nt, preferred_element_type=F32)
            delta = jnp.sum(pp * dpp, axis=-1, keepdims=True) + jnp.sum(pc * dpc, axis=-1, keepdims=True)
            dsp = (pp * (dpp - delta) * scale).astype(BF16)
            dsc = (pc * (dpc - delta) * scale).astype(BF16)
            dsk = ps * inv * delta
            for g in range(GROUP):
                dsink = dsink + jnp.where(lane == GROUP * kh + g, -jnp.sum(jnp.where(grp == g, dsk, 0.0)), 0.0)
            dqs.append(_unstack_heads(jnp.dot(dsp, kp, preferred_element_type=F32)
                                      + jnp.dot(dsc, kc, preferred_element_type=F32)))
            dkp.append(lax.dot_general(dsp, q4, tn, preferred_element_type=F32))
            dkc.append(lax.dot_general(dsc, q4, tn, preferred_element_type=F32))
            dvp.append(lax.dot_general(pp.astype(BF16), do4, tn, preferred_element_type=F32))
            dvc.append(lax.dot_general(pc.astype(BF16), do4, tn, preferred_element_type=F32))
        dq = jnp.concatenate(dqs, axis=1)
        dq_ref[...] = dq.astype(BF16)
        dprev_ref[0] = jnp.concatenate(dkp + dvp, axis=1)
        dcur_ref[0] = jnp.concatenate(dkc + dvc, axis=1)

        @pl.when(n == 0)
        def _():
            dbq_ref[...] = jnp.zeros_like(dbq_ref)
            dsink_ref[...] = jnp.zeros_like(dsink_ref)

        dbq_ref[...] += jnp.sum(dq, axis=0, keepdims=True)
        dsink_ref[...] += dsink

    blk = pl.BlockSpec((BLOCK, D_MODEL), lambda n: (n, 0))
    part = pl.BlockSpec((1, BLOCK, kvw), lambda n: (n, 0, 0))
    return pl.pallas_call(
        body, grid=(nb,),
        in_specs=[pl.BlockSpec(memory_space=pltpu.SMEM), blk,
                  pl.BlockSpec((BLOCK, kvw), lambda n: (jnp.maximum(n - 1, 0), 0)), pl.BlockSpec((BLOCK, kvw), lambda n: (n, 0)), blk],
        out_specs=[blk, pl.BlockSpec((1, D_MODEL), lambda n: (0, 0)), part, part, pl.BlockSpec((1, D_MODEL), lambda n: (0, 0))],
        out_shape=[_sds((n_rows, D_MODEL), BF16), _sds((1, D_MODEL), F32), _sds((nb, BLOCK, kvw), F32),
                   _sds((nb, BLOCK, kvw), F32), _sds((1, D_MODEL), F32)],
        name="attn_bwd", compiler_params=_params(("arbitrary",)))(sinks, q, kv, kv, do)


def kv_combine(dprev, dcur):
    nb, _, kvw = dprev.shape

    def body(dcur_ref, dprev_ref, dkv_ref, db_ref):
        total = jnp.zeros((1, kvw), F32)
        for m in range(nb):
            dkv = dcur_ref[m] + dprev_ref[m + 1] if m + 1 < nb else dcur_ref[m]
            dkv_ref[m * BLOCK:(m + 1) * BLOCK, :] = dkv.astype(BF16)
            total = total + jnp.sum(dkv, axis=0, keepdims=True)
        db_ref[...] = jnp.concatenate([total, jnp.zeros((1, D_MODEL - kvw), F32)], axis=1)

    vm = pl.BlockSpec(memory_space=pltpu.VMEM)
    return pl.pallas_call(body, in_specs=[vm, vm], out_specs=[vm, vm],
                          out_shape=[_sds((nb * BLOCK, kvw), BF16), _sds((1, D_MODEL), F32)], name="kv_combine",
                          compiler_params=_params())(dcur, dprev)


def glu_bwd(dout, val, gate, tm=256):
    n_rows, d = dout.shape

    def body(do_ref, v_ref, g_ref, dz_ref, db_ref):
        i = pl.program_id(0)
        sg = jax.nn.sigmoid(g_ref[...])
        dval = do_ref[...] * sg
        dgate = do_ref[...] * v_ref[...] * sg * (1.0 - sg)
        dz_ref[...] = jnp.concatenate([dval, dgate], axis=1).astype(BF16)

        @pl.when(i == 0)
        def _():
            db_ref[...] = jnp.zeros_like(db_ref)

        db_ref[0:1, :] += jnp.sum(dval, axis=0, keepdims=True)
        db_ref[1:2, :] += jnp.sum(dgate, axis=0, keepdims=True)

    row = pl.BlockSpec((tm, d), lambda i: (i, 0))
    return pl.pallas_call(
        body, grid=(n_rows // tm,), in_specs=[row, row, row],
        out_specs=[pl.BlockSpec((tm, 2 * d), lambda i: (i, 0)), pl.BlockSpec((2, d), lambda i: (0, 0))],
        out_shape=[_sds((n_rows, 2 * d), BF16), _sds((2, d), F32)],
        name="glu_bwd", compiler_params=_params(("arbitrary",)))(dout, val, gate)


def _adam_update(w, g, m, v):
    nm = ADAM_B1 * m + (1.0 - ADAM_B1) * g
    nv = ADAM_B2 * v + (1.0 - ADAM_B2) * (g * g)
    m_hat = nm / (1.0 - ADAM_B1 ** ADAM_STEP)
    v_hat = nv / (1.0 - ADAM_B2 ** ADAM_STEP)
    return -ADAM_LR * (m_hat / (jnp.sqrt(v_hat) + ADAM_EPS) + ADAM_WD * w), nm, nv


def adamw(name, ws, gs, ms, vs, steps=8):
    n = len(ws)

    def body(*refs):
        for k in range(n):
            w_ref, g_ref, m_ref, v_ref = (refs[j * n + k] for j in range(4))
            go_ref, d_ref, nm_ref, nv_ref = (refs[(4 + j) * n + k] for j in range(4))
            gv = g_ref[...]
            go_ref[...] = gv
            d_ref[...], nm_ref[...], nv_ref[...] = _adam_update(w_ref[...], gv, m_ref[...], v_ref[...])

    specs = [pl.BlockSpec((w.shape[0] // steps, w.shape[1]), lambda i: (i, 0)) for w in ws]
    shapes = [_sds(w.shape, F32) for w in ws]
    out = pl.pallas_call(
        body, grid=(steps,), in_specs=specs * 4, out_specs=specs * 4, out_shape=shapes * 4, name=name,
        compiler_params=_params(("parallel",)))(*ws, *gs, *ms, *vs)
    return [list(out[j * n:(j + 1) * n]) for j in range(4)]


def adamw_native(name, ws, gs, ms, vs):
    n = len(ws)

    def body(*refs):
        w_refs, g_refs, m_refs, v_refs = refs[:n], refs[n:2 * n], refs[2 * n:3 * n], refs[3 * n:4 * n]
        d_refs, nm_refs, nv_refs = refs[4 * n:5 * n], refs[5 * n:6 * n], refs[6 * n:7 * n]
        for k in range(n):
            dl, nm, nv = _adam_update(w_refs[k][...], g_refs[k][...], m_refs[k][...], v_refs[k][...])
            d_refs[k][...] = dl
            nm_refs[k][...] = nm
            nv_refs[k][...] = nv

    vm = pl.BlockSpec(memory_space=pltpu.VMEM)
    shapes = [_sds(w.shape, F32) for w in ws]
    out = pl.pallas_call(body, in_specs=[vm] * (4 * n), out_specs=[vm] * (3 * n), out_shape=shapes * 3, name=name,
                         compiler_params=_params())(*ws, *gs, *ms, *vs)
    return list(out[:n]), list(out[n:2 * n]), list(out[2 * n:])


VEC_ROWS = {"norm_mix": 0, "norm_mlp": 2, "norm_kv": 4, "norm_final": 5, "s5_d": 6, "b_q": 7, "b_o": 8, "s5_b_glu": 9,
            "b_kv": 11, "sinks": 12, "loss": 13}


def split_vectors(where, vecs, d_shard, glu_shard):
    kvw = 2 * N_KV * HEAD_DIM
    shapes = {"norm_mix": (2, D_MODEL), "norm_mlp": (2, D_MODEL), "norm_kv": (1, D_MODEL), "norm_final": (1, D_MODEL),
              "s5_d": (1, d_shard), "b_q": (1, D_MODEL), "b_o": (1, D_MODEL), "s5_b_glu": (1, glu_shard), "b_kv": (1, kvw),
              "sinks": (1, N_Q), "loss": (1, 128)}
    names = list(shapes)

    def body(where_ref, v_ref, *o_refs):
        chip = where_ref[1]
        for name, o_ref in zip(names, o_refs):
            r0, (r, n) = VEC_ROWS[name], shapes[name]
            if name == "s5_d":
                g = jnp.zeros((1, n), F32)
                for j in range(4):
                    g = jnp.where(chip == j, v_ref[r0:r0 + 1, j * n:(j + 1) * n], g)
            elif name == "s5_b_glu":
                g = jnp.zeros((1, n), F32)
                for j in range(4):
                    row, col = r0 + (j * n) // D_MODEL, (j * n) % D_MODEL
                    g = jnp.where(chip == j, v_ref[row:row + 1, col:col + n], g)
            else:
                g = v_ref[r0:r0 + r, 0:n]
            o_ref[...] = g

    vm = pl.BlockSpec(memory_space=pltpu.VMEM)
    out = pl.pallas_call(body, in_specs=[pl.BlockSpec(memory_space=pltpu.SMEM), vm], out_specs=[vm] * len(names),
                         out_shape=[_sds(shapes[n], F32) for n in names], name="split_vectors",
                         compiler_params=_params())(where, vecs)
    return dict(zip(names, out))


def _position():
    x, y, c = lax.axis_index("x"), lax.axis_index("y"), lax.axis_index("c")
    others = [(1 - x, y), (x, 1 - y), (1 - x, 1 - y)]
    return x, y, c, others


def _window(ref, kind, chip, half, shard_shape):
    if kind == "slab":
        return ref.at[chip]
    r, n = shard_shape
    if kind == "col":
        return ref.at[pl.ds(pl.multiple_of(half * (r // 2), 16), r // 2), pl.ds(pl.multiple_of(chip * n, 128), n)]
    return ref.at[pl.ds(pl.multiple_of(chip * r, 16), r), pl.ds(pl.multiple_of(half * (n // 2), 128), n // 2)]


def _half(ref, kind, half, shape):
    r, n = shape
    if kind == "col":
        return ref.at[pl.ds(pl.multiple_of(half * (r // 2), 16), r // 2), :]
    return ref.at[:, pl.ds(pl.multiple_of(half * (n // 2), 128), n // 2)]


def swap_start(name, grads, kinds, carry):
    nt = len(grads)
    shapes = [tuple(g.shape) for g in grads]
    lands = [lax.empty(sh, BF16) for sh in shapes]
    given, given_specs, token_type, write = _hand_through(carry)
    n_in = 2 * nt + len(given)

    def body(*refs):
        in_refs, land_refs = refs[:nt], refs[nt:2 * nt]
        send_sems, recv_sems, token = refs[n_in], refs[n_in + 1], refs[-1]
        x, y, c, _ = _position()
        for t in range(nt):
            pltpu.make_async_remote_copy(
                src_ref=_half(in_refs[t], kinds[t], 1 - c, shapes[t]), dst_ref=_half(land_refs[t], kinds[t], 1 - c, shapes[t]),
                send_sem=send_sems.at[t], recv_sem=recv_sems.at[t], device_id=(x, y, 1 - c), device_id_type=MESH).start()
        write(token, refs[:n_in])

    sems = pltpu.SemaphoreType.DMA((nt,))
    both = list(grads) + lands
    out = pl.pallas_call(
        body, name=name, in_specs=[HBM_SPEC] * (2 * nt) + given_specs,
        out_specs=(SEM_SPEC, SEM_SPEC, *[HBM_SPEC] * (2 * nt), pl.BlockSpec(memory_space=pltpu.VMEM)),
        out_shape=(sems, sems, *[pltpu.HBM(a.shape, a.dtype) for a in both], token_type),
        input_output_aliases={t: 2 + t for t in range(2 * nt)}, compiler_params=_split_params(),
    )(*[_in_hbm(a) for a in both], *given)
    return out[0], out[1], list(out[2:2 + nt]), list(out[2 + nt:2 + 2 * nt]), out[-1]


def swap_wait(name, send_sems, recv_sems, grads, lands, kinds, after):
    nt = len(grads)
    shapes = [tuple(g.shape) for g in grads]

    def body(*refs):
        in_refs, land_refs = refs[:nt], refs[nt:2 * nt]
        send_ref, recv_ref = refs[2 * nt], refs[2 * nt + 1]
        x, y, c, _ = _position()
        for t in range(nt):
            cp = pltpu.make_async_remote_copy(
                src_ref=_half(in_refs[t], kinds[t], 1 - c, shapes[t]), dst_ref=_half(land_refs[t], kinds[t], c, shapes[t]),
                send_sem=send_ref.at[t], recv_sem=recv_ref.at[t], device_id=(x, y, 1 - c), device_id_type=MESH)
            cp.wait_send()
            cp.wait_recv()

    both = list(grads) + list(lands)
    out = pl.pallas_call(
        body, name=name, in_specs=[HBM_SPEC] * (2 * nt) + [SEM_SPEC, SEM_SPEC, HBM_SPEC], out_specs=[HBM_SPEC] * (2 * nt),
        out_shape=[pltpu.HBM(a.shape, a.dtype) for a in both], input_output_aliases={t: t for t in range(2 * nt)},
        compiler_params=_split_params())(*both, send_sems, recv_sems, _in_hbm(after))
    return list(out[:nt]), list(out[nt:])


def _half_spec(kind, shape, tiles):
    r, n = shape
    if kind == "col":
        tn = n // tiles
        return pl.BlockSpec((r // 2, tn), lambda i, s: (s[0], i))
    tm = r // tiles
    return pl.BlockSpec((tm, n // 2), lambda i, s: (i, s[0]))


def add_halves(name, mine, landed, kinds, where, tiles=2):
    nt = len(mine)
    shapes = [tuple(a.shape) for a in mine]

    def compact(t):
        r, n = shapes[t]
        if kinds[t] == "col":
            return (r // 2, n), pl.BlockSpec((r // 2, n // tiles), lambda i, s: (0, i))
        return (r, n // 2), pl.BlockSpec((r // tiles, n // 2), lambda i, s: (i, 0))

    def body(s_ref, *refs):
        for a_ref, b_ref, o_ref in zip(refs[:nt], refs[nt:2 * nt], refs[2 * nt:]):
            o_ref[...] = (a_ref[...].astype(F32) + b_ref[...].astype(F32)).astype(BF16)

    specs = [_half_spec(kinds[t], shapes[t], tiles) for t in range(nt)]
    return pl.pallas_call(
        body, grid_spec=pltpu.PrefetchScalarGridSpec(num_scalar_prefetch=1, grid=(tiles,), in_specs=specs + specs,
                                                     out_specs=[compact(t)[1] for t in range(nt)]),
        out_shape=[_sds(compact(t)[0], BF16) for t in range(nt)], name=name,
        compiler_params=_params(("parallel",)))(where, *mine, *landed)


def sum_shards(name, parts, landed, kinds, shard_shapes, where, layers, n_layers, intos, tiles=2):
    nt = len(parts)
    in_specs, out_specs = [], []
    for t in range(nt):
        (r, n), layer = shard_shapes[t], layers[t]
        if kinds[t] == "col":
            tm, width = r // 2 // tiles, n
            own = pl.BlockSpec((tm, n), lambda i, s: (i, s[1]))
            out = pl.BlockSpec((None, tm, n), lambda i, s, layer=layer: (layer, s[0] * tiles + i, 0))
        else:
            tm, width = r // tiles, n // 2
            own = pl.BlockSpec((tm, n // 2), lambda i, s: (s[1] * tiles + i, 0))
            out = pl.BlockSpec((None, tm, n // 2), lambda i, s, layer=layer: (layer, i, s[0]))
        in_specs += [own, pl.BlockSpec((3, tm, width), lambda i, s: (0, i, 0))]
        out_specs.append(out)
    args, aliases = [where] + [a for pair in zip(parts, landed) for a in pair], {}
    for t in range(nt):
        if intos[t] is not None:
            aliases[len(args)] = t
            in_specs.append(pl.BlockSpec(memory_space=pl.ANY))
            args.append(intos[t])

    def body(s_ref, *refs):
        for t in range(nt):
            a_ref, l_ref, o_ref = refs[2 * t], refs[2 * t + 1], refs[len(in_specs) + t]
            o_ref[...] = ((a_ref[...].astype(F32) + l_ref[0].astype(F32)) + l_ref[1].astype(F32)) + l_ref[2].astype(F32)

    return pl.pallas_call(
        body, grid_spec=pltpu.PrefetchScalarGridSpec(num_scalar_prefetch=1, grid=(tiles,), in_specs=in_specs,
                                                     out_specs=out_specs),
        out_shape=[_sds((n_layers[t],) + tuple(shard_shapes[t]), F32) for t in range(nt)], input_output_aliases=aliases,
        name=name, compiler_params=_params(("parallel",)))(*args)


def share_start(arrays, entries, carry):
    na, nt = len(arrays), len(entries)
    given, given_specs, token_type, write = _hand_through(carry)
    n_in = na + len(given)

    def body(*refs):
        in_refs, send_sems, recv_sems, token = refs[:na], refs[n_in], refs[n_in + 1], refs[-1]
        x, y, c, _ = _position()
        for t, (a, layer, kind) in enumerate(entries):
            mine = _half(in_refs[a].at[layer], kind, c, tuple(arrays[a].shape[1:]))
            pltpu.make_async_remote_copy(
                src_ref=mine, dst_ref=mine, send_sem=send_sems.at[t], recv_sem=recv_sems.at[t],
                device_id=(x, y, 1 - c), device_id_type=MESH).start()
        write(token, refs[:n_in])

    sems = pltpu.SemaphoreType.DMA((nt,))
    out = pl.pallas_call(
        body, name="share_start", in_specs=[HBM_SPEC] * na + given_specs,
        out_specs=(SEM_SPEC, SEM_SPEC, *[HBM_SPEC] * na, pl.BlockSpec(memory_space=pltpu.VMEM)),
        out_shape=(sems, sems, *[pltpu.HBM(a.shape, a.dtype) for a in arrays], token_type),
        input_output_aliases={t: 2 + t for t in range(na)}, compiler_params=_split_params(),
    )(*[_in_hbm(a) for a in arrays], *given)
    return out[0], out[1], list(out[2:2 + na]), out[-1]


def share_wait(send_sems, recv_sems, arrays, entries, after):
    na = len(arrays)

    def body(*refs):
        in_refs, send_ref, recv_ref = refs[:na], refs[na], refs[na + 1]
        x, y, c, _ = _position()
        for t, (a, layer, kind) in enumerate(entries):
            shape = tuple(arrays[a].shape[1:])
            cp = pltpu.make_async_remote_copy(
                src_ref=_half(in_refs[a].at[layer], kind, c, shape), dst_ref=_half(in_refs[a].at[layer], kind, 1 - c, shape),
                send_sem=send_ref.at[t], recv_sem=recv_ref.at[t], device_id=(x, y, 1 - c), device_id_type=MESH)
            cp.wait_send()
            cp.wait_recv()

    return list(pl.pallas_call(
        body, name="share_wait", in_specs=[HBM_SPEC] * na + [SEM_SPEC, SEM_SPEC, HBM_SPEC], out_specs=[HBM_SPEC] * na,
        out_shape=[pltpu.HBM(a.shape, a.dtype) for a in arrays], input_output_aliases={t: t for t in range(na)},
        compiler_params=_split_params())(*arrays, send_sems, recv_sems, _in_hbm(after)))


HBM_SPEC = pl.BlockSpec(memory_space=pltpu.HBM)
SEM_SPEC = pl.BlockSpec(memory_space=pltpu.SEMAPHORE)
ANY_SPEC = pl.BlockSpec(memory_space=pl.ANY)


def _split_params():
    return pltpu.CompilerParams(has_side_effects=pltpu.SideEffectType.DATAFLOW_SIDE_EFFECTING,
                                vmem_limit_bytes=VMEM_LIMIT_BYTES)


def _in_hbm(a):
    return pltpu.with_memory_space_constraint(a, pltpu.HBM)


def cast_place(arrays, entries, where, tiles=2):
    in_specs, out_specs, fulls = [], [], []
    for a, layer, kind in entries:
        _, r, n = arrays[a].shape
        tm = r // tiles
        in_specs.append(pl.BlockSpec((None, tm, n), lambda i, s, layer=layer: (layer, i, 0)))
        if kind == "col":
            fulls.append((r, 4 * n))
            out_specs.append(pl.BlockSpec((tm, n), lambda i, s: (i, s[1])))
        else:
            fulls.append((4 * r, n))
            out_specs.append(pl.BlockSpec((tm, n), lambda i, s: (s[1] * tiles + i, 0)))
    nt = len(entries)

    def body(s_ref, *refs):
        for w_ref, o_ref in zip(refs[:nt], refs[nt:]):
            o_ref[...] = w_ref[...].astype(BF16)

    return pl.pallas_call(
        body, grid_spec=pltpu.PrefetchScalarGridSpec(num_scalar_prefetch=1, grid=(tiles,), in_specs=in_specs,
                                                     out_specs=out_specs),
        out_shape=[_sds(f, BF16) for f in fulls], name="cast_place",
        compiler_params=_params(("parallel",)))(where, *[arrays[a] for a, _, _ in entries])


def _hand_through(carry):
    given = [] if isinstance(carry, tuple) else [carry]

    def write(token, ins):
        token[...] = ins[-1][...] if given else jnp.zeros_like(token)

    return (given, [pl.BlockSpec(memory_space=pltpu.VMEM)] * len(given),
            _sds(carry if isinstance(carry, tuple) else carry.shape, F32), write)


def gather_start(fulls, kinds, shard_shapes, carry):
    nt = len(fulls)
    given, given_specs, token_type, write = _hand_through(carry)
    n_in = nt + len(given)

    def body(*refs):
        full_refs = refs[:nt]
        send_sems, recv_sems, token = refs[n_in], refs[n_in + 1], refs[-1]
        x, y, c, others = _position()
        for t in range(nt):
            mine = _window(full_refs[t], kinds[t], 2 * x + y, c, shard_shapes[t])
            for j, (ox, oy) in enumerate(others):
                pltpu.make_async_remote_copy(
                    src_ref=mine, dst_ref=mine, send_sem=send_sems.at[3 * t + j], recv_sem=recv_sems.at[3 * t + j],
                    device_id=(ox, oy, c), device_id_type=MESH).start()
        write(token, refs[:n_in])

    sems = pltpu.SemaphoreType.DMA((3 * nt,))
    out = pl.pallas_call(
        body, name="gather_start", in_specs=[HBM_SPEC] * nt + given_specs,
        out_specs=(SEM_SPEC, SEM_SPEC, *[HBM_SPEC] * nt, pl.BlockSpec(memory_space=pltpu.VMEM)),
        out_shape=(sems, sems, *[pltpu.HBM(f.shape, f.dtype) for f in fulls], token_type),
        input_output_aliases={t: 2 + t for t in range(nt)}, compiler_params=_split_params(),
    )(*[_in_hbm(f) for f in fulls], *given)
    return out[0], out[1], list(out[2:2 + nt]), out[-1]


def gather_wait(name, send_sems, recv_sems, fulls, kinds, shard_shapes, after, first):
    nt = len(fulls)
    extra = [] if after is None else [_in_hbm(after)]

    def body(*refs):
        full_refs, send_ref, recv_ref = refs[:nt], refs[nt], refs[nt + 1]
        x, y, c, others = _position()
        for t in range(nt):
            mine = _window(full_refs[t], kinds[t], 2 * x + y, c, shard_shapes[t])
            for j, (ox, oy) in enumerate(others):
                cp = pltpu.make_async_remote_copy(
                    src_ref=mine, dst_ref=_window(full_refs[t], kinds[t], 2 * ox + oy, c, shard_shapes[t]),
                    send_sem=send_ref.at[3 * (first + t) + j], recv_sem=recv_ref.at[3 * (first + t) + j],
                    device_id=(ox, oy, c), device_id_type=MESH)
                cp.wait_send()
                cp.wait_recv()

    out = pl.pallas_call(
        body, name=name, in_specs=[HBM_SPEC] * nt + [SEM_SPEC, SEM_SPEC] + [HBM_SPEC] * len(extra),
        out_specs=[HBM_SPEC] * nt, out_shape=[pltpu.HBM(f.shape, f.dtype) for f in fulls],
        input_output_aliases={t: t for t in range(nt)}, compiler_params=_split_params())(*fulls, send_sems, recv_sems, *extra)
    return list(out)


def forward_halves(name, fulls, kinds, shard_shapes):
    nt = len(fulls)

    def body(*refs):
        out_refs = refs[nt:2 * nt]
        send_sems, recv_sems = refs[2 * nt:]
        x, y, c, others = _position()
        cps = []
        for t in range(nt):
            for j, (ox, oy) in enumerate(others):
                landed = _window(out_refs[t], kinds[t], 2 * ox + oy, c, shard_shapes[t])
                cp = pltpu.make_async_remote_copy(
                    src_ref=landed, dst_ref=landed, send_sem=send_sems.at[3 * t + j], recv_sem=recv_sems.at[3 * t + j],
                    device_id=(x, y, 1 - c), device_id_type=MESH)
                cp.start()
                cps.append(cp)
        for t in range(nt):
            for j, (ox, oy) in enumerate(others):
                got = _window(out_refs[t], kinds[t], 2 * ox + oy, 1 - c, shard_shapes[t])
                pltpu.make_async_remote_copy(
                    src_ref=got, dst_ref=got, send_sem=send_sems.at[3 * t + j], recv_sem=recv_sems.at[3 * t + j],
                    device_id=(x, y, 1 - c), device_id_type=MESH).wait_recv()
        for cp in cps:
            cp.wait_send()

    out = pl.pallas_call(
        body, in_specs=[ANY_SPEC] * nt, out_specs=[ANY_SPEC] * nt, out_shape=[_sds(f.shape, f.dtype) for f in fulls],
        input_output_aliases={t: t for t in range(nt)},
        scratch_shapes=[pltpu.SemaphoreType.DMA((3 * nt,)), pltpu.SemaphoreType.DMA((3 * nt,))],
        name=name, compiler_params=_params())(*fulls)
    return list(out)


def forward_start(name, send_sems, recv_sems, fulls, kinds, shard_shapes, after, first, carry, passing=()):
    nt, n_pass = len(fulls), len(passing)
    given, given_specs, token_type, write = _hand_through(carry)
    n_in = nt + 3 + n_pass + len(given)

    def body(*refs):
        full_refs, ici_send, ici_recv = refs[:nt], refs[nt], refs[nt + 1]
        send_ref, recv_ref, token = refs[n_in], refs[n_in + 1], refs[-1]
        x, y, c, others = _position()
        for t in range(nt):
            mine = _window(full_refs[t], kinds[t], 2 * x + y, c, shard_shapes[t])
            for j, (ox, oy) in enumerate(others):
                landed = _window(full_refs[t], kinds[t], 2 * ox + oy, c, shard_shapes[t])
                cp = pltpu.make_async_remote_copy(
                    src_ref=mine, dst_ref=landed, send_sem=ici_send.at[3 * (first + t) + j],
                    recv_sem=ici_recv.at[3 * (first + t) + j], device_id=(ox, oy, c), device_id_type=MESH)
                cp.wait_send()
                cp.wait_recv()
                pltpu.make_async_remote_copy(
                    src_ref=landed, dst_ref=landed, send_sem=send_ref.at[3 * t + j], recv_sem=recv_ref.at[3 * t + j],
                    device_id=(x, y, 1 - c), device_id_type=MESH).start()
        write(token, refs[:n_in])

    sems = pltpu.SemaphoreType.DMA((3 * nt,))
    out = pl.pallas_call(
        body, name=name, in_specs=[HBM_SPEC] * nt + [SEM_SPEC, SEM_SPEC, HBM_SPEC] + [HBM_SPEC] * n_pass + given_specs,
        out_specs=(SEM_SPEC, SEM_SPEC, *[HBM_SPEC] * (nt + n_pass), pl.BlockSpec(memory_space=pltpu.VMEM)),
        out_shape=(sems, sems, *[pltpu.HBM(f.shape, f.dtype) for f in [*fulls, *passing]], token_type),
        input_output_aliases={**{t: 2 + t for t in range(nt)}, **{nt + 3 + t: 2 + nt + t for t in range(n_pass)}},
        compiler_params=_split_params(),
    )(*fulls, send_sems, recv_sems, _in_hbm(after), *passing, *given)
    return out[0], out[1], list(out[2:2 + nt]), list(out[2 + nt:2 + nt + n_pass]), out[-1]


def forward_wait(name, send_sems, recv_sems, fulls, kinds, shard_shapes, after):
    nt = len(fulls)

    def body(*refs):
        full_refs, send_ref, recv_ref = refs[:nt], refs[nt], refs[nt + 1]
        x, y, c, others = _position()
        for t in range(nt):
            for j, (ox, oy) in enumerate(others):
                cp = pltpu.make_async_remote_copy(
                    src_ref=_window(full_refs[t], kinds[t], 2 * ox + oy, c, shard_shapes[t]),
                    dst_ref=_window(full_refs[t], kinds[t], 2 * ox + oy, 1 - c, shard_shapes[t]),
                    send_sem=send_ref.at[3 * t + j], recv_sem=recv_ref.at[3 * t + j],
                    device_id=(x, y, 1 - c), device_id_type=MESH)
                cp.wait_send()
                cp.wait_recv()

    return list(pl.pallas_call(
        body, name=name, in_specs=[HBM_SPEC] * nt + [SEM_SPEC, SEM_SPEC, HBM_SPEC], out_specs=[HBM_SPEC] * nt,
        out_shape=[pltpu.HBM(f.shape, f.dtype) for f in fulls], input_output_aliases={t: t for t in range(nt)},
        compiler_params=_split_params())(*fulls, send_sems, recv_sems, _in_hbm(after)))


def _piece(ref, kind, chip, shard_shape):
    r, n = shard_shape
    if kind == "col":
        return ref.at[:, pl.ds(pl.multiple_of(chip * n, 128), n)]
    return ref.at[pl.ds(pl.multiple_of(chip * r, 16), r), :]


def _piece_shape(kind, shard_shape):
    r, n = shard_shape
    return (r // 2, n) if kind == "col" else (r, n // 2)


def exchange_start(name, parts, kinds, shard_shapes, carry):
    nt = len(parts)
    lands = [lax.empty((3,) + _piece_shape(kinds[t], shard_shapes[t]), BF16) for t in range(nt)]
    given, given_specs, token_type, write = _hand_through(carry)
    n_in = 2 * nt + len(given)

    def body(*refs):
        part_refs, land_refs = refs[:nt], refs[nt:2 * nt]
        send_sems, recv_sems, token = refs[n_in], refs[n_in + 1], refs[-1]
        x, y, c, others = _position()
        for t in range(nt):
            for j, (ox, oy) in enumerate(others):
                pltpu.make_async_remote_copy(
                    src_ref=_piece(part_refs[t], kinds[t], 2 * ox + oy, shard_shapes[t]), dst_ref=land_refs[t].at[j],
                    send_sem=send_sems.at[3 * t + j], recv_sem=recv_sems.at[3 * t + j],
                    device_id=(ox, oy, c), device_id_type=MESH).start()
        write(token, refs[:n_in])

    sems = pltpu.SemaphoreType.DMA((3 * nt,))
    both = list(parts) + lands
    out = pl.pallas_call(
        body, name=name, in_specs=[HBM_SPEC] * (2 * nt) + given_specs,
        out_specs=(SEM_SPEC, SEM_SPEC, *[HBM_SPEC] * (2 * nt), pl.BlockSpec(memory_space=pltpu.VMEM)),
        out_shape=(sems, sems, *[pltpu.HBM(a.shape, a.dtype) for a in both], token_type),
        input_output_aliases={t: 2 + t for t in range(2 * nt)}, compiler_params=_split_params(),
    )(*[_in_hbm(a) for a in both], *given)
    return out[0], out[1], list(out[2:2 + nt]), list(out[2 + nt:2 + 2 * nt]), out[-1]


def exchange_wait(name, send_sems, recv_sems, parts, lands, kinds, shard_shapes, after):
    nt = len(parts)

    def body(*refs):
        part_refs, land_refs = refs[:nt], refs[nt:2 * nt]
        send_ref, recv_ref = refs[2 * nt], refs[2 * nt + 1]
        x, y, c, others = _position()
        for t in range(nt):
            for j, (ox, oy) in enumerate(others):
                cp = pltpu.make_async_remote_copy(
                    src_ref=_piece(part_refs[t], kinds[t], 2 * ox + oy, shard_shapes[t]), dst_ref=land_refs[t].at[j],
                    send_sem=send_ref.at[3 * t + j], recv_sem=recv_ref.at[3 * t + j],
                    device_id=(ox, oy, c), device_id_type=MESH)
                cp.wait_send()
                cp.wait_recv()

    both = list(parts) + list(lands)
    out = pl.pallas_call(
        body, name=name, in_specs=[HBM_SPEC] * (2 * nt) + [SEM_SPEC, SEM_SPEC, HBM_SPEC], out_specs=[HBM_SPEC] * (2 * nt),
        out_shape=[pltpu.HBM(a.shape, a.dtype) for a in both], input_output_aliases={t: t for t in range(2 * nt)},
        compiler_params=_split_params())(*both, send_sems, recv_sems, _in_hbm(after))
    return list(out[:nt]), list(out[nt:])


def reduce_swap(bufs, wire):
    n = len(bufs)
    halves = [b.shape[0] // 2 for b in bufs]

    def body(*refs):
        in_refs, out_refs, txs, got = refs[:n], refs[n:2 * n], refs[2 * n:3 * n], refs[3 * n:4 * n]
        send_sems, recv_sems = refs[4 * n:]
        x, y, c, _ = _position()
        cps = []
        for k in range(n):
            txs[k][...] = in_refs[k][pl.ds(pl.multiple_of((1 - c) * halves[k], 8), halves[k]), :].astype(wire[k])
            cp = pltpu.make_async_remote_copy(src_ref=txs[k], dst_ref=got[k], send_sem=send_sems.at[k],
                                              recv_sem=recv_sems.at[k], device_id=(x, y, 1 - c), device_id_type=MESH)
            cp.start()
            cps.append(cp)
        for k, cp in enumerate(cps):
            cp.wait()
            own = in_refs[k][pl.ds(pl.multiple_of(c * halves[k], 8), halves[k]), :]
            out_refs[k][...] = (own.astype(wire[k]).astype(F32) + got[k][...].astype(F32)).astype(wire[k])

    vm = pl.BlockSpec(memory_space=pltpu.VMEM)
    parts = [((h, b.shape[1]), w) for h, b, w in zip(halves, bufs, wire)]
    return list(pl.pallas_call(
        body, name="reduce_swap", in_specs=[vm] * n, out_specs=[vm] * n, out_shape=[_sds(sh, w) for sh, w in parts],
        scratch_shapes=[pltpu.VMEM(sh, w) for sh, w in parts] * 2 + [pltpu.SemaphoreType.DMA((n,))] * 2,
        compiler_params=_params())(*bufs))


def reduce_start(parts):
    n = len(parts)
    lands = [lax.empty((4,) + tuple(p.shape), p.dtype) for p in parts]

    def body(*refs):
        part_refs, land_refs, send_sems, recv_sems = refs[:n], refs[n:2 * n], refs[2 * n], refs[2 * n + 1]
        x, y, c, others = _position()
        for k in range(n):
            for j, (ox, oy) in enumerate(others):
                pltpu.make_async_remote_copy(
                    src_ref=part_refs[k], dst_ref=land_refs[k].at[2 * x + y], send_sem=send_sems.at[3 * k + j],
                    recv_sem=recv_sems.at[3 * k + j], device_id=(ox, oy, c), device_id_type=MESH).start()

    sems = pltpu.SemaphoreType.DMA((3 * n,))
    both = list(parts) + lands
    out = pl.pallas_call(
        body, name="reduce_start", in_specs=[HBM_SPEC] * (2 * n), out_specs=(SEM_SPEC, SEM_SPEC, *[HBM_SPEC] * (2 * n)),
        out_shape=(sems, sems, *[pltpu.HBM(a.shape, a.dtype) for a in both]),
        input_output_aliases={k: 2 + k for k in range(2 * n)}, compiler_params=_split_params(),
    )(*[_in_hbm(a) for a in both])
    return out[0], out[1], list(out[2:2 + n]), list(out[2 + n:])


def reduce_wait(send_sems, recv_sems, parts, lands, after):
    n = len(parts)

    def body(*refs):
        part_refs, land_refs, send_ref, recv_ref = refs[:n], refs[n:2 * n], refs[2 * n], refs[2 * n + 1]
        x, y, c, others = _position()
        for k in range(n):
            for j, (ox, oy) in enumerate(others):
                cp = pltpu.make_async_remote_copy(
                    src_ref=part_refs[k], dst_ref=land_refs[k].at[2 * ox + oy], send_sem=send_ref.at[3 * k + j],
                    recv_sem=recv_ref.at[3 * k + j], device_id=(ox, oy, c), device_id_type=MESH)
                cp.wait_send()
                cp.wait_recv()

    both = list(parts) + list(lands)
    out = pl.pallas_call(
        body, name="reduce_wait", in_specs=[HBM_SPEC] * (2 * n) + [SEM_SPEC, SEM_SPEC, HBM_SPEC],
        out_specs=[HBM_SPEC] * (2 * n), out_shape=[pltpu.HBM(a.shape, a.dtype) for a in both],
        input_output_aliases={k: k for k in range(2 * n)}, compiler_params=_split_params(),
    )(*both, send_sems, recv_sems, _in_hbm(after))
    return list(out[:n]), list(out[n:])


def reduce_share(parts, lands):
    n = len(parts)
    halves = [p.shape[0] for p in parts]

    def body(*refs):
        part_refs, land_refs, out_refs = refs[:n], refs[n:2 * n], refs[2 * n:3 * n]
        send_sems, recv_sems = refs[3 * n:]
        x, y, c, _ = _position()
        chip = 2 * x + y
        cps = []
        for k in range(n):
            mine = pl.ds(pl.multiple_of(c * halves[k], 8), halves[k])
            own = part_refs[k][...].astype(F32)
            total = jnp.where(chip == 0, own, land_refs[k][0].astype(F32))
            for entry in range(1, 4):
                total = total + jnp.where(chip == entry, own, land_refs[k][entry].astype(F32))
            out_refs[k][mine, :] = total
            cp = pltpu.make_async_remote_copy(
                src_ref=out_refs[k].at[mine], dst_ref=out_refs[k].at[mine], send_sem=send_sems.at[k],
                recv_sem=recv_sems.at[k], device_id=(x, y, 1 - c), device_id_type=MESH)
            cp.start()
            cps.append(cp)
        for cp in cps:
            cp.wait()

    vm = pl.BlockSpec(memory_space=pltpu.VMEM)
    return list(pl.pallas_call(
        body, name="reduce_share", in_specs=[vm] * (2 * n), out_specs=[vm] * n,
        out_shape=[_sds((2 * p.shape[0], p.shape[1]), F32) for p in parts],
        scratch_shapes=[pltpu.SemaphoreType.DMA((n,))] * 2, compiler_params=_params())(*parts, *lands))


def _local_step(x, target, small, need, ahead, emit_swap, emit_exchange):
    d = D_MODEL
    full = {}

    def handed(vec, token):
        return vec if token is None else token

    def token_rows(token):
        return [] if token is None else [token]

    def plus(acc, rows):
        return acc + rows[0] if rows else acc

    rb16, rbt16, rc16, rct16, lr_t, li_t = small["s5_operands"]
    ge, ge_slope, cs = s5_fwd(x, small["norm_mix0"], small["s5_d"], rb16, rc16, lr_t, li_t)
    full.update(need("glu", ge))

    def norm_rows(h, gains):
        xh, _ = _rms_hat(h)
        return [xh * g for g in gains]

    def glu_epilogue(accs, e, r):
        v, gt = accs[0] + r[0], accs[1] + r[1]
        h = e[0] + v * jax.nn.sigmoid(gt)
        return [h, v, gt] + norm_rows(h, r[2:])

    gain_mlp0 = handed(small["norm_mlp0"], ahead("mlp_in0", full["w_glu"], small["norm_mlp0"]))
    h1, val, gate, n1 = mm_nn(
        "glu", ge, full["w_glu"], [0, d], d, glu_epilogue, [F32, F32, F32, BF16], extras=[x],
        rowvecs=[(small["s5_b_glu"], 0), (small["s5_b_glu"], d), (gain_mlp0, 0)], tm=512, tn=d)

    def mlp_fwd(tag, h, n, w_in, get_w_out, next_gains, head=None):
        def in_epilogue(accs, e, rv):
            pos = jnp.maximum(accs[0], 0.0)
            return [pos * pos, 2.0 * pos]

        r, slope = mm_nn("mlp_in" + tag, n, w_in, [0], w_in.shape[1], in_epilogue, [BF16, BF16], tm=2048)
        w_out = get_w_out(r)

        def epilogue(accs, e, rv):
            h_out = e[0] + accs[0]
            return [h_out] + norm_rows(h_out, rv)

        if head is not None:
            return head(r, w_out, h), (n, r, slope)
        outs = mm_nn("mlp_out" + tag, r, w_out, [0], d, epilogue, [F32] + [BF16] * len(next_gains), extras=[h],
                     rowvecs=[(g, 0) for g in next_gains], tm=512, tn=d)
        return outs[0], outs[1:], (n, r, slope)

    full.update(need("mlp_in0", h1))

    def w_out0(after):
        full.update(need("mlp_out0", after))
        return full["w_out0"]

    h2, (nkv, n2), mlp0 = mlp_fwd("0", h1, n1, full["w_in0"], w_out0, [small["norm_kv"], small["norm_mix1"]])

    full.update(need("attn", h2))
    kvw = 2 * N_KV * HEAD_DIM
    (kv,) = mm_nn("kv_proj", nkv, full["w_kv"], [0], kvw, lambda accs, e, r: [accs[0] + r[0]], [BF16],
                  rowvecs=[(small["b_kv"], 0)], tm=2048)
    (q,) = mm_nn("q_proj", n2, full["w_q"], [0], d, lambda accs, e, r: [accs[0] + r[0]], [BF16],
                 rowvecs=[(small["b_q"], 0)], tm=2048)
    sinks = small["sinks"].reshape(N_Q)
    o = attn_fwd(q, kv, sinks)
    def o_epilogue(accs, e, r):
        h_out = e[0] + accs[0] + r[0]
        return [h_out] + norm_rows(h_out, r[1:])

    bias_o = handed(small["b_o"], ahead("mlp_in1", o, small["b_o"]))
    h3, n3 = mm_nn("o_proj", o, full["w_o"], [0], d, o_epilogue, [F32, BF16], extras=[h2],
                   rowvecs=[(bias_o, 0), (small["norm_mlp1"], 0)], tm=512, tn=d)
    full.update(need("mlp_in1", h3, then="mlp_out1"))

    def w_out1(after):
        full.update(need("mlp_out1", after))
        return full["w_out1"]

    def loss_head(r, w_out, h):
        def epilogue(accs, e, rv):
            xh, rr = _rms_hat(e[0] + accs[0])
            err = xh * rv[0] - e[1]
            dy = err * (1.0 / d)
            dxh = dy * rv[0]
            dx = rr * (dxh - xh * jnp.mean(dxh * xh, axis=-1, keepdims=True))
            loss = jnp.full((1, d), 0.5 * jnp.sum(jnp.mean(err * err, axis=-1, keepdims=True)), F32)
            return [dx, dx, loss, jnp.sum(dy * xh, axis=0, keepdims=True)]

        return mm_nn("mlp_out1", r, w_out, [0], d, epilogue, [F32, BF16], extras=[h, target],
                     rowvecs=[(small["norm_final"], 0)], n_sums=2, tm=512, tn=d)

    (dh, dhb, loss_tile, dg_final), mlp1 = mlp_fwd("1", h3, n3, full["w_in1"], w_out1, [], head=loss_head)

    grads_small, grads_full = {"norm_final": dg_final}, {}
    ident = lambda acc, e, r: [plus(acc, r)]
    layer1 = ["w_out1", "w_in1", "w_o", "w_q", "w_kv"]
    layer0 = ["w_out0", "w_in0", "w_glu"]

    def norm_bwd_rows(x_rows, res, dys, gains):
        xh, r = _rms_hat(x_rows)
        dxh = sum(dy * g for dy, g in zip(dys, gains))
        dx = r * (dxh - xh * jnp.mean(dxh * xh, axis=-1, keepdims=True)) + res
        return dx, [jnp.sum(dy * xh, axis=0, keepdims=True) for dy in dys]

    def mlp_bwd(tag, dh, dhb, h_in, gain, w_in, w_out, saved, token=None):
        n, r, slope = saved
        grads_full["w_out" + tag] = mm_tn("dw_out" + tag, r, dhb, tn=1024)
        (da,) = mm_nt("mlp_da" + tag, dhb, w_out, lambda acc, e, rv: [plus(acc * e[0].astype(F32), rv)], [BF16],
                      extras=[slope], rowvecs=token_rows(token), tm=2048)
        grads_full["w_in" + tag] = mm_tn("dw_in" + tag, n, da, tn=1024)

        def epilogue(acc, e, rv):
            dx, dgs = norm_bwd_rows(e[0], e[1], [acc], rv)
            return [dx, dx, jnp.sum(dx, axis=0, keepdims=True)] + dgs

        dx, dxb, colsum, dg = mm_nt("mlp_dn" + tag, da, w_in, epilogue, [F32, BF16], extras=[h_in, dh], rowvecs=[gain],
                                    n_sums=2, tm=512, tk=d)
        grads_small["norm_mlp" + tag] = dg
        return dx, dxb, colsum

    dh3, dh3b, colsum3 = mlp_bwd("1", dh, dhb, h3, small["norm_mlp1"], full["w_in1"], full["w_out1"], mlp1)
    grads_small["b_o"] = colsum3
    grads_full["w_o"] = mm_tn("dw_o", o, dh3b, tn=1024)
    (do,) = mm_nt("attn_do", dh3b, full["w_o"], ident, [BF16], tm=2048)
    dq, dbq, dprev, dcur, dsink = attn_bwd(q, kv, do, sinks)
    dkv, dbkv = kv_combine(dprev, dcur)
    grads_small["b_q"], grads_small["b_kv"], grads_small["sinks"] = dbq, dbkv, dsink
    grads_full["w_q"] = mm_tn("dw_q", n2, dq, tn=1024)
    grads_full["w_kv"] = mm_tn("dw_kv", nkv, dkv, tk=1024)
    token = emit_swap("layer1", {n: grads_full[n] for n in layer1}, (1, d))
    (dnkv,) = mm_nt("kv_dn", dkv, full["w_kv"], ident, [F32], rowvecs=token_rows(token), tm=2048, tk=1024)

    def attn_dn_epilogue(acc, e, rv):
        dx, dgs = norm_bwd_rows(e[0], e[1], [acc, e[2]], rv)
        return [dx, dx] + dgs

    dh2, dh2b, dg_mix1, dg_kv = mm_nt("attn_dn", dq, full["w_q"], attn_dn_epilogue, [F32, BF16], extras=[h2, dh3, dnkv],
                                      rowvecs=[small["norm_mix1"], small["norm_kv"]], n_sums=2, tm=512, tk=d)
    grads_small["norm_mix1"], grads_small["norm_kv"] = dg_mix1, dg_kv
    token = emit_exchange("layer1", dh2b, (1, full["w_out0"].shape[0]))
    dh1, _, _ = mlp_bwd("0", dh2, dh2b, h1, small["norm_mlp0"], full["w_in0"], full["w_out0"], mlp0, token)

    dz, db_glu = glu_bwd(dh1, val, gate)
    grads_small["s5_b_glu"] = db_glu
    grads_full["w_glu"] = mm_tn("dw_glu", ge, dz, tn=1024)
    token = emit_swap("layer0", {n: grads_full[n] for n in layer0}, (1, d))
    (dy2,) = mm_nt("glu_dy", dz, full["w_glu"], lambda acc, e, rv: [plus(acc, rv) * e[0]], [F32], extras=[ge_slope],
                   rowvecs=token_rows(token), tm=512, tk=1024)
    d_skip = handed(small["s5_d"], emit_exchange("layer0", dy2, small["s5_d"]))
    grad_x, dd, drb, drc, dlr, dli, dg_mix0 = s5_bwd(x, small["norm_mix0"], dy2, dh1, d_skip, cs, rb16, rbt16, rct16, lr_t, li_t)
    grads_small["s5_d"] = dd
    grads_small["s5_mats"] = (drb, drc, dlr, dli)
    grads_small["norm_mix0"] = dg_mix0
    return loss_tile, grad_x, grads_small


SMALL_NAMES = ["norm_mix", "norm_mlp", "norm_kv", "norm_final", "s5_a_re", "s5_a_im", "s5_log_dt", "s5_b_re", "s5_b_im",
               "s5_c_re", "s5_c_im", "s5_d", "s5_b_glu", "b_kv", "b_q", "sinks", "b_o"]
BIG_NAMES = ["s5_w_glu", "w_kv", "w_q", "w_o", "w_mlp_in", "w_mlp_out"]
WEIGHT_ORDER = ["norm_mix", "norm_mlp", "norm_kv", "norm_final", "s5_a_re", "s5_a_im", "s5_log_dt", "s5_b_re", "s5_b_im",
                "s5_c_re", "s5_c_im", "s5_d", "s5_w_glu", "s5_b_glu", "w_kv", "b_kv", "w_q", "b_q", "sinks", "w_o", "b_o",
                "w_mlp_in", "w_mlp_out"]


def kernel(x, norm_mix, norm_mlp, norm_kv, norm_final, s5_a_re, s5_a_im, s5_log_dt, s5_b_re, s5_b_im, s5_c_re, s5_c_im, s5_d, s5_w_glu, s5_b_glu, w_kv, b_kv, w_q, b_q, sinks, w_o, b_o, w_mlp_in, w_mlp_out, loss_target, m_norm_mix, m_norm_mlp, m_norm_kv, m_norm_final, m_s5_a_re, m_s5_a_im, m_s5_log_dt, m_s5_b_re, m_s5_b_im, m_s5_c_re, m_s5_c_im, m_s5_d, m_s5_w_glu, m_s5_b_glu, m_w_kv, m_b_kv, m_w_q, m_b_q, m_sinks, m_w_o, m_b_o, m_w_mlp_in, m_w_mlp_out, v_norm_mix, v_norm_mlp, v_norm_kv, v_norm_final, v_s5_a_re, v_s5_a_im, v_s5_log_dt, v_s5_b_re, v_s5_b_im, v_s5_c_re, v_s5_c_im, v_s5_d, v_s5_w_glu, v_s5_b_glu, v_w_kv, v_b_kv, v_w_q, v_b_q, v_sinks, v_w_o, v_b_o, v_w_mlp_in, v_w_mlp_out):
    env = dict(locals())
    w = {n: env[n] for n in WEIGHT_ORDER}
    mom = {n: env["m_" + n] for n in WEIGHT_ORDER}
    var = {n: env["v_" + n] for n in WEIGHT_ORDER}
    d = D_MODEL
    xi, yi, ci = lax.axis_index("x"), lax.axis_index("y"), lax.axis_index("c")
    chip = 2 * xi + yi
    where = jnp.stack([ci, chip]).astype(jnp.int32)

    dsh, bsh = s5_d.shape[1], s5_b_glu.shape[1]
    packed = jnp.concatenate([s5_d.reshape(-1, 128), s5_b_glu.reshape(-1, 128)])
    n_d, n_b = dsh // 128, bsh // 128
    slab = lax.dynamic_update_slice(jnp.zeros((4, 8, 128), F32), jnp.pad(packed, ((0, 8 - n_d - n_b), (0, 0)))[None],
                                    (chip, 0, 0))

    big = [s5_w_glu, w_kv[None], w_q, w_o, w_mlp_in, w_mlp_out]
    entries = [(0, 0, "col"), (1, 0, "row"), (2, 0, "row"), (3, 0, "row"), (4, 0, "col"), (4, 1, "col"),
               (5, 0, "row"), (5, 1, "row")]
    names = ["w_glu", "w_kv", "w_q", "w_o", "w_in0", "w_in1", "w_out0", "w_out1"]
    kinds = dict(zip(names, [k for _, _, k in entries]))
    shard_shapes = dict(zip(names, [tuple(big[a].shape[1:]) for a, _, _ in entries]))

    placed_w = dict(zip(names, cast_place(big, entries, where)))
    placed_w["vectors"], kinds["vectors"], shard_shapes["vectors"] = slab, "slab", None
    gather_groups = {"glu": ["w_glu"], "mlp_in0": ["w_in0"], "mlp_out0": ["w_out0"], "attn": ["w_kv", "w_q", "w_o"],
                     "mlp_in1": ["w_in1"], "mlp_out1": ["w_out1"]}
    order = ["vectors"] + [n for members in gather_groups.values() for n in members]
    send, recv, thru, log_dt = gather_start([placed_w[n] for n in order], [kinds[n] for n in order],
                                            [shard_shapes[n] for n in order], s5_log_dt)
    started = dict(zip(order, thru))
    (gathered_rows,) = gather_wait("gather_wait_vectors", send, recv, [started["vectors"]], ["slab"], [None], None, 0)
    d_full = gathered_rows[:, 0:n_d].reshape(1, -1)
    bglu_full = gathered_rows[:, n_d:n_d + n_b].reshape(1, -1)

    forwarding = {}

    def ahead(group, after, carry, passing=()):
        members = gather_groups[group]
        ks, shapes = [kinds[n] for n in members], [shard_shapes[n] for n in members]
        d2d_send, d2d_recv, landed, passed, tok = forward_start(
            "forward_start_" + group, send, recv, [started[n] for n in members], ks, shapes, after,
            order.index(members[0]), carry, passing)
        forwarding[group] = (d2d_send, d2d_recv, landed)
        return passed if passing else tok

    def need(group, after, then=None):
        members = gather_groups[group]
        ks, shapes = [kinds[n] for n in members], [shard_shapes[n] for n in members]
        if group in forwarding:
            arrays = forward_wait("forward_wait_" + group, *forwarding[group], ks, shapes, after)
        else:
            landed = gather_wait("gather_wait_" + group, send, recv, [started[n] for n in members], ks, shapes, after,
                                 order.index(members[0]))
            arrays = forward_halves("forward_halves_" + group, landed, ks, shapes)
        if then is not None:
            arrays = ahead(then, after, (8, 128), arrays)
        return dict(zip(members, arrays))

    swapping, exchanging = {}, {}

    def emit_swap(group, partial, carry):
        members = list(partial)
        send, recv, mine, lands, tok = swap_start("swap_start_" + group, [partial[n] for n in members],
                                                  [kinds[n] for n in members], carry)
        swapping[group] = (members, send, recv, mine, lands)
        return tok

    def emit_exchange(group, after, carry):
        members, send, recv, mine, lands = swapping[group]
        ks, shapes = [kinds[n] for n in members], [shard_shapes[n] for n in members]
        mine, landed = swap_wait("swap_wait_" + group, send, recv, mine, lands, ks, after)
        sums = add_halves("add_halves_" + group, mine, landed, ks, where)
        send, recv, parts, lands, tok = exchange_start("exchange_start_" + group, sums, ks, shapes, carry)
        exchanging[group] = (members, send, recv, parts, lands)
        return tok

    s5_args = (s5_a_re[0], s5_a_im[0], log_dt[0], s5_b_re[0], s5_b_im[0])
    small = {
        "norm_mix0": norm_mix[0:1], "norm_mix1": norm_mix[1:2], "norm_mlp0": norm_mlp[0:1], "norm_mlp1": norm_mlp[1:2],
        "norm_kv": norm_kv.reshape(1, d), "norm_final": norm_final.reshape(1, d), "s5_operands": s5_prep(*s5_args, s5_c_re[0], s5_c_im[0]),
        "s5_d": d_full, "s5_b_glu": bglu_full,
        "b_kv": b_kv.reshape(1, -1), "b_q": b_q, "sinks": sinks, "b_o": b_o,
    }
    loss_row, grad_x, gs = _local_step(x[0], loss_target[0], small, need, ahead, emit_swap, emit_exchange)

    mats, lams = s5_compact(*gs["s5_mats"])
    rows = [gs["norm_mix0"], gs["norm_mix1"], gs["norm_mlp0"], gs["norm_mlp1"], gs["norm_kv"], gs["norm_final"], gs["s5_d"],
            gs["b_q"], gs["b_o"], gs["s5_b_glu"], gs["b_kv"], gs["sinks"], loss_row, jnp.zeros((2, d), F32)]
    small_send, small_recv, small_parts, small_lands = reduce_start(
        reduce_swap([jnp.concatenate(rows, axis=0), lams, mats], [F32, F32, BF16]))

    reduced = [None] * len(big)
    where_of = dict(zip(names, entries))
    for group in ("layer1", "layer0"):
        members, send, recv, parts, lands = exchanging[group]
        ks, shapes = [kinds[n] for n in members], [shard_shapes[n] for n in members]
        parts, lands = exchange_wait("exchange_wait_" + group, send, recv, parts, lands, ks, shapes, small_lands[-1])
        targets = [where_of[n][0] for n in members]
        sums = sum_shards("sum_shards_" + group, parts, lands, ks, shapes, where, [where_of[n][1] for n in members],
                          [big[a].shape[0] for a in targets], [reduced[a] for a in targets])
        for a, arr in zip(targets, sums):
            reduced[a] = arr
    share_send, share_recv, reduced, _ = share_start(reduced, entries, (8, 128))

    vecs, lams, mats = reduce_share(*reduce_wait(small_send, small_recv, small_parts, small_lands, reduced[0]))
    grads = split_vectors(where, vecs, dsh, bsh)
    loss = grads.pop("loss")[0, 0]
    g_are, g_aim, g_dt, g_bre, g_bim, dc_re, dc_im = s5_param_bwd(mats, lams, *s5_args)
    grads.update({"s5_a_re": g_are[None], "s5_a_im": g_aim[None], "s5_log_dt": g_dt[None], "s5_b_re": g_bre[None],
                  "s5_b_im": g_bim[None], "s5_c_re": dc_re[None], "s5_c_im": dc_im[None]})

    delta, new_m, new_v = {}, {}, {}

    def view(n, a):
        return a.reshape(1, -1) if a.ndim == 1 else jnp.swapaxes(a, -1, -2) if n in ("s5_b_re", "s5_b_im") else a

    sw, sg, sm, sv = ([view(n, t[n]) for n in SMALL_NAMES] for t in (w, grads, mom, var))
    for n, a, b, c_ in zip(SMALL_NAMES, *adamw_native("adamw_small", sw, sg, sm, sv)):
        delta[n], new_m[n], new_v[n] = (view(n, t) if t.ndim == 4 else t for t in (a, b, c_))

    reduced = share_wait(share_send, share_recv, reduced, entries, new_v["s5_c_re"])
    for n, g in zip(BIG_NAMES, reduced):
        grads[n] = g.reshape(w[n].shape)
    flat = lambda t: [t[n].reshape(-1, t[n].shape[-1]) for n in BIG_NAMES]
    for table, arrays in zip((grads, delta, new_m, new_v), adamw("adamw_big", flat(w), flat(grads), flat(mom), flat(var))):
        for n, a in zip(BIG_NAMES, arrays):
            table[n] = a.reshape(w[n].shape)

    out = [loss.reshape(()), grad_x[None]]
    for table in (grads, delta, new_m, new_v):
        out += [table[n].reshape(w[n].shape) for n in WEIGHT_ORDER]
    return tuple(out)
```

```python
import math

import jax
import jax.numpy as jnp
from jax import lax
from jax.experimental import pallas as pl
from jax.experimental.pallas import tpu as pltpu

F32 = jnp.float32
BF16 = jnp.bfloat16

D_MODEL = 1024
S5_GROUPS = 64
S5_GROUP = 16
S5_STATE = 64
N_KV = 4
N_Q = 16
HEAD_DIM = 64
BLOCK = 128
NORM_EPS = 1e-5
LAMBDA_RE_MAX = -1e-4
ADAM_LR, ADAM_B1, ADAM_B2, ADAM_EPS, ADAM_WD, ADAM_STEP = 0.001, 0.9, 0.999, 1e-08, 0.01, 10

VMEM_LIMIT_BYTES = 56 * 1024 * 1024
S5_CHUNK = 256
S5_BLOCKS = 4
MESH = pl.DeviceIdType.MESH


def _params(sem=None):
    return pltpu.CompilerParams(dimension_semantics=sem, vmem_limit_bytes=VMEM_LIMIT_BYTES)


def _sds(shape, dtype):
    return jax.ShapeDtypeStruct(shape, dtype)


def _rms_hat(xv):
    r = lax.rsqrt(jnp.mean(xv * xv, axis=-1, keepdims=True) + NORM_EPS)
    return xv * r, r


def mm_nn(name, a, w, col_offsets, n_out, epilogue, out_dtypes, extras=(), rowvecs=(), n_sums=0, tm=1024, tn=512):
    m, k = a.shape
    tm, tn = min(tm, m), min(tn, n_out)
    nw, ne, nr, no = len(col_offsets), len(extras), len(rowvecs), len(out_dtypes)

    def body(a_ref, *refs):
        w_refs, e_refs, r_refs = refs[:nw], refs[nw:nw + ne], refs[nw + ne:nw + ne + nr]
        o_refs, s_refs = refs[nw + ne + nr:nw + ne + nr + no], refs[nw + ne + nr + no:]
        av = a_ref[...]
        accs = [jnp.dot(av, w_ref[...], preferred_element_type=F32) for w_ref in w_refs]
        outs = epilogue(accs, [e[...] for e in e_refs], [r[...] for r in r_refs])
        for o_ref, o in zip(o_refs, outs[:no]):
            o_ref[...] = o.astype(o_ref.dtype)
        if n_sums:
            @pl.when(pl.program_id(1) == 0)
            def _():
                for s_ref in s_refs:
                    s_ref[...] = jnp.zeros_like(s_ref)

            for s_ref, val in zip(s_refs, outs[no:]):
                s_ref[...] += val

    def wspec(off):
        return pl.BlockSpec((k, tn), lambda j, i, off=off: (0, off // tn + j))

    def rspec(off):
        return pl.BlockSpec((1, tn), lambda j, i, off=off: (0, off // tn + j))

    tile = pl.BlockSpec((tm, tn), lambda j, i: (i, j))
    in_specs = ([pl.BlockSpec((tm, k), lambda j, i: (i, 0))] + [wspec(o) for o in col_offsets]
                + [tile] * ne + [rspec(o) for _, o in rowvecs])
    sem = ("parallel", "arbitrary") if n_sums else ("parallel", "parallel")
    return pl.pallas_call(
        body, grid=(n_out // tn, m // tm), in_specs=in_specs,
        out_specs=[tile] * no + [pl.BlockSpec((1, tn), lambda j, i: (0, j))] * n_sums,
        out_shape=[_sds((m, n_out), dt) for dt in out_dtypes] + [_sds((1, n_out), F32)] * n_sums, name=name,
        compiler_params=_params(sem))(a, *([w] * nw), *extras, *[r for r, _ in rowvecs])


def mm_nt(name, g, w, epilogue, out_dtypes, extras=(), rowvecs=(), n_sums=0, tm=512, tk=512):
    m, n = g.shape
    k = w.shape[0]
    tm, tk = min(tm, m), min(tk, k)
    ne, nr, no = len(extras), len(rowvecs), len(out_dtypes)

    def body(g_ref, w_ref, *refs):
        e_refs, r_refs, o_refs, s_refs = refs[:ne], refs[ne:ne + nr], refs[ne + nr:ne + nr + no], refs[ne + nr + no:]
        acc = lax.dot_general(g_ref[...], w_ref[...], (((1,), (1,)), ((), ())), preferred_element_type=F32)
        outs = epilogue(acc, [e[...] for e in e_refs], [r[...] for r in r_refs])
        for o_ref, o in zip(o_refs, outs[:no]):
            o_ref[...] = o.astype(o_ref.dtype)
        if n_sums:
            @pl.when(pl.program_id(0) == 0)
            def _():
                for s_ref in s_refs:
                    s_ref[...] = jnp.zeros_like(s_ref)

            for s_ref, val in zip(s_refs, outs[no:]):
                s_ref[...] += val

    tile = pl.BlockSpec((tm, tk), lambda i, j: (i, j))
    vec = pl.BlockSpec((1, tk), lambda i, j: (0, j))
    sem = ("arbitrary", "parallel") if n_sums else ("parallel", "parallel")
    return pl.pallas_call(
        body, grid=(m // tm, k // tk),
        in_specs=[pl.BlockSpec((tm, n), lambda i, j: (i, 0)), pl.BlockSpec((tk, n), lambda i, j: (j, 0))]
        + [tile] * ne + [vec] * nr,
        out_specs=[tile] * no + [vec] * n_sums,
        out_shape=[_sds((m, k), dt) for dt in out_dtypes] + [_sds((1, k), F32)] * n_sums, name=name,
        compiler_params=_params(sem))(g, w, *extras, *rowvecs)


def mm_tn(name, a, g, tk=512, tn=512):
    m, k = a.shape
    n = g.shape[1]
    tk, tn = min(tk, k), min(tn, n)

    def body(a_ref, g_ref, o_ref):
        acc = lax.dot_general(a_ref[...], g_ref[...], (((0,), (0,)), ((), ())), preferred_element_type=F32)
        o_ref[...] = acc.astype(o_ref.dtype)

    return pl.pallas_call(
        body, grid=(k // tk, n // tn),
        in_specs=[pl.BlockSpec((m, tk), lambda i, j: (0, i)), pl.BlockSpec((m, tn), lambda i, j: (0, j))],
        out_specs=pl.BlockSpec((tk, tn), lambda i, j: (i, j)), out_shape=_sds((k, n), BF16), name=name,
        compiler_params=_params(("parallel", "parallel")))(a, g)


def _row_mask(tc):
    row = lax.broadcasted_iota(jnp.int32, (8 * tc, 256), 0) % 8
    col = lax.broadcasted_iota(jnp.int32, (8 * tc, 256), 1) // 32
    return row == col


def _expand_rows(val, mask):
    tc, width = val.shape
    rep = jnp.broadcast_to(val[:, None, :], (tc, 8, width)).reshape(8 * tc, width)
    return jnp.where(mask, rep, 0.0).astype(BF16)


def _stage(ref, val):
    ref[0] = val[:, 0:128]
    ref[1] = val[:, 128:256]


def _gather_rows(src_ref, tc):
    halves = []
    for half in range(2):
        col = lax.broadcasted_iota(jnp.int32, (tc, 128), 1) // 32 + 4 * half
        out = jnp.zeros((tc, 128), F32)
        for s8 in range(4 * half, 4 * half + 4):
            out = jnp.where(col == s8, src_ref.at[half][pl.ds(s8, tc, stride=8), :], out)
        halves.append(out)
    return jnp.concatenate(halves, axis=1)


def _repeat(n, by, step, carry):
    def trip(i, c):
        for j in range(by):
            c = step(i * by + j, c)
        return c

    return lax.fori_loop(0, n // by, trip, carry)


def _gelu_and_slope(x):
    c = math.sqrt(2.0 / math.pi)
    t = jnp.tanh(c * (x + 0.044715 * x * x * x))
    return 0.5 * x * (1.0 + t), 0.5 * (1.0 + t) + 0.5 * x * (1.0 - t * t) * c * (1.0 + 3.0 * 0.044715 * x * x)


def s5_fwd(x, gain, d_skip, rb, rc, lam_r, lam_i):
    n_rows = x.shape[0]
    tc = min(S5_CHUNK, n_rows)
    nc = n_rows // tc

    def body(x_ref, g_ref, d_ref, rb_ref, rc_ref, lr_ref, li_ref, ge_ref, slope_ref, cs_ref, bux, yrows, carry):
        i = pl.program_id(0)
        u = _rms_hat(x_ref[...])[0] * g_ref[...]

        @pl.when(i == 0)
        def _():
            carry[...] = jnp.zeros_like(carry)

        cs_ref[0] = carry[...]
        mask = _row_mask(tc)
        for blk in range(S5_BLOCKS):
            lhs = _expand_rows(u[:, blk * 256:(blk + 1) * 256], mask)
            bux[blk] = jnp.dot(lhs, rb_ref[blk], preferred_element_type=F32)
        lam = [(lr_ref[blk], li_ref[blk]) for blk in range(S5_BLOCKS)]

        def step(t, c):
            r0 = pl.multiple_of(t * 8, 8)
            new = []
            for blk in range(S5_BLOCKS):
                xr, xi = c[2 * blk], c[2 * blk + 1]
                lr, li = lam[blk]
                nr = lr * xr - li * xi + bux[blk, pl.ds(r0, 8), 0:128]
                ni = lr * xi + li * xr + bux[blk, pl.ds(r0, 8), 128:256]
                bux[blk, pl.ds(r0, 8), 0:128] = nr
                bux[blk, pl.ds(r0, 8), 128:256] = ni
                new += [nr, ni]
            return tuple(new)

        c0 = []
        for blk in range(S5_BLOCKS):
            c0 += [carry[blk, :, 0:128], carry[blk, :, 128:256]]
        cn = _repeat(tc, 8, step, tuple(c0))
        for blk in range(S5_BLOCKS):
            carry[blk, :, 0:128] = cn[2 * blk]
            carry[blk, :, 128:256] = cn[2 * blk + 1]
        for blk in range(S5_BLOCKS):
            _stage(yrows, jnp.dot(bux[blk].astype(BF16), rc_ref[blk], preferred_element_type=F32))
            sl = slice(blk * 256, (blk + 1) * 256)
            ge, slope = _gelu_and_slope(_gather_rows(yrows, tc) + d_ref[:, sl] * u[:, sl])
            slope_ref[:, sl] = slope
            ge_ref[:, sl] = ge.astype(BF16)

    row = pl.BlockSpec((tc, D_MODEL), lambda i: (i, 0))
    vec = pl.BlockSpec((1, D_MODEL), lambda i: (0, 0))
    mat = pl.BlockSpec((S5_BLOCKS, 256, 256), lambda i: (0, 0, 0))
    lamspec = pl.BlockSpec((S5_BLOCKS, 8, 128), lambda i: (0, 0, 0))
    return pl.pallas_call(
        body, grid=(nc,),
        in_specs=[row, vec, vec, mat, mat, lamspec, lamspec],
        out_specs=[row, row, pl.BlockSpec((1, S5_BLOCKS, 8, 256), lambda i: (i, 0, 0, 0))],
        out_shape=[_sds((n_rows, D_MODEL), BF16), _sds((n_rows, D_MODEL), F32), _sds((nc, S5_BLOCKS, 8, 256), F32)],
        scratch_shapes=[pltpu.VMEM((S5_BLOCKS, 8 * tc, 256), F32), pltpu.VMEM((2, 8 * tc, 128), F32),
                        pltpu.VMEM((S5_BLOCKS, 8, 256), F32)],
        name="s5_fwd", compiler_params=_params(("arbitrary",)))(x, gain, d_skip, rb, rc, lam_r, lam_i)


def s5_bwd(x, gain, dy2, res, d_skip, cs, rb, rbt, rct, lam_r, lam_i):
    n_rows = x.shape[0]
    tc = min(S5_CHUNK, n_rows)
    nc = n_rows // tc

    def body(x_ref, g_ref, dy_ref, res_ref, d_ref, cs_ref, rb_ref, rbt_ref, rct_ref, lr_ref, li_ref,
             dx_ref, dd_ref, drb_ref, drc_ref, dlr_ref, dli_ref, dg_ref, tmp, du, lhsu, lhsd, xs, adj, acarry):
        i = pl.program_id(0)
        u = _rms_hat(x_ref[...])[0] * g_ref[...]

        @pl.when(i == 0)
        def _():
            acarry[...] = jnp.zeros_like(acarry)
            dd_ref[...] = jnp.zeros_like(dd_ref)
            drb_ref[...] = jnp.zeros_like(drb_ref)
            drc_ref[...] = jnp.zeros_like(drc_ref)
            dlr_ref[...] = jnp.zeros_like(dlr_ref)
            dli_ref[...] = jnp.zeros_like(dli_ref)
            dg_ref[...] = jnp.zeros_like(dg_ref)

        dd_ref[...] += jnp.sum(dy_ref[...] * u, axis=0, keepdims=True)
        mask = _row_mask(tc)
        for blk in range(S5_BLOCKS):
            sl = slice(blk * 256, (blk + 1) * 256)
            lhsu[blk] = _expand_rows(u[:, sl], mask)
            xs[blk] = jnp.dot(lhsu[blk], rb_ref[blk], preferred_element_type=F32)
            lhsd[blk] = _expand_rows(dy_ref[:, sl], mask)
            adj[blk] = jnp.dot(lhsd[blk], rct_ref[blk], preferred_element_type=F32)
        lam = [(lr_ref[blk], li_ref[blk]) for blk in range(S5_BLOCKS)]

        def fstep(t, c):
            r0 = pl.multiple_of(t * 8, 8)
            new = []
            for blk in range(S5_BLOCKS):
                xr, xi = c[2 * blk], c[2 * blk + 1]
                lr, li = lam[blk]
                nr = lr * xr - li * xi + xs[blk, pl.ds(r0, 8), 0:128]
                ni = lr * xi + li * xr + xs[blk, pl.ds(r0, 8), 128:256]
                xs[blk, pl.ds(r0, 8), 0:128] = nr
                xs[blk, pl.ds(r0, 8), 128:256] = ni
                new += [nr, ni]
            return tuple(new)

        c0 = []
        for blk in range(S5_BLOCKS):
            c0 += [cs_ref[0, blk, :, 0:128], cs_ref[0, blk, :, 128:256]]
        _repeat(tc, 8, fstep, tuple(c0))

        def bstep(k, c):
            t = tc - 1 - k
            r0 = pl.multiple_of(t * 8, 8)
            rp = pl.multiple_of(jnp.maximum(t - 1, 0) * 8, 8)
            first = t == 0
            new_a, new_g = [], []
            for blk in range(S5_BLOCKS):
                ar, ai = c[0][2 * blk], c[0][2 * blk + 1]
                glr, gli = c[1][2 * blk], c[1][2 * blk + 1]
                lr, li = lam[blk]
                nr = lr * ar + li * ai + adj[blk, pl.ds(r0, 8), 0:128]
                ni = lr * ai - li * ar + adj[blk, pl.ds(r0, 8), 128:256]
                adj[blk, pl.ds(r0, 8), 0:128] = nr
                adj[blk, pl.ds(r0, 8), 128:256] = ni
                pr = jnp.where(first, cs_ref[0, blk, :, 0:128], xs[blk, pl.ds(rp, 8), 0:128])
                pi = jnp.where(first, cs_ref[0, blk, :, 128:256], xs[blk, pl.ds(rp, 8), 128:256])
                new_a += [nr, ni]
                new_g += [glr + nr * pr + ni * pi, gli + ni * pr - nr * pi]
            return tuple(new_a), tuple(new_g)

        a0, g0 = [], []
        for blk in range(S5_BLOCKS):
            a0 += [acarry[blk, :, 0:128], acarry[blk, :, 128:256]]
            g0 += [dlr_ref[blk], dli_ref[blk]]
        an, gn = _repeat(tc, 4, bstep, (tuple(a0), tuple(g0)))
        for blk in range(S5_BLOCKS):
            acarry[blk, :, 0:128] = an[2 * blk]
            acarry[blk, :, 128:256] = an[2 * blk + 1]
            dlr_ref[blk] = gn[2 * blk]
            dli_ref[blk] = gn[2 * blk + 1]
        for blk in range(S5_BLOCKS):
            sl = slice(blk * 256, (blk + 1) * 256)
            ab = adj[blk].astype(BF16)
            _stage(tmp, jnp.dot(ab, rbt_ref[blk], preferred_element_type=F32))
            du[:, sl] = _gather_rows(tmp, tc) + d_ref[:, sl] * dy_ref[:, sl]
            drb_ref[blk] += lax.dot_general(lhsu[blk], ab, (((0,), (0,)), ((), ())), preferred_element_type=F32)
            drc_ref[blk] += lax.dot_general(lhsd[blk], xs[blk].astype(BF16), (((0,), (0,)), ((), ())),
                                            preferred_element_type=F32)
        xh, r = _rms_hat(x_ref[...])
        dg_ref[...] += jnp.sum(du[...] * xh, axis=0, keepdims=True)
        dxh = du[...] * g_ref[...]
        dx_ref[...] = r * (dxh - xh * jnp.mean(dxh * xh, axis=-1, keepdims=True)) + res_ref[...]

    rev = pl.BlockSpec((tc, D_MODEL), lambda i: (nc - 1 - i, 0))
    vec = pl.BlockSpec((1, D_MODEL), lambda i: (0, 0))
    mat = pl.BlockSpec((S5_BLOCKS, 256, 256), lambda i: (0, 0, 0))
    lamspec = pl.BlockSpec((S5_BLOCKS, 8, 128), lambda i: (0, 0, 0))
    big = pltpu.VMEM((S5_BLOCKS, 8 * tc, 256), F32)
    bigb = pltpu.VMEM((S5_BLOCKS, 8 * tc, 256), BF16)
    return pl.pallas_call(
        body, grid=(nc,),
        in_specs=[rev, vec, rev, rev, vec, pl.BlockSpec((1, S5_BLOCKS, 8, 256), lambda i: (nc - 1 - i, 0, 0, 0)),
                  mat, mat, mat, lamspec, lamspec],
        out_specs=[rev, vec, mat, mat, lamspec, lamspec, vec],
        out_shape=[_sds((n_rows, D_MODEL), F32), _sds((1, D_MODEL), F32), _sds((S5_BLOCKS, 256, 256), F32),
                   _sds((S5_BLOCKS, 256, 256), F32), _sds((S5_BLOCKS, 8, 128), F32), _sds((S5_BLOCKS, 8, 128), F32),
                   _sds((1, D_MODEL), F32)],
        scratch_shapes=[pltpu.VMEM((2, 8 * tc, 128), F32), pltpu.VMEM((tc, D_MODEL), F32), bigb, bigb, big, big,
                        pltpu.VMEM((S5_BLOCKS, 8, 256), F32)],
        name="s5_bwd", compiler_params=_params(("arbitrary",)))(
            x, gain, dy2, res, d_skip, cs, rb, rbt, rct, lam_r, lam_i)


def _s5_views(a_re, a_im, log_dt, b_re, b_im):
    return a_re[:, None, :], a_im[:, None, :], log_dt[:, None, None], jnp.swapaxes(b_re, 1, 2), jnp.swapaxes(b_im, 1, 2)


def _s5_factors(a_re, a_im, log_dt):
    lr, li, dt = jnp.minimum(a_re, LAMBDA_RE_MAX), a_im, jnp.exp(log_dt)
    mag, ang = jnp.exp(lr * dt), li * dt
    lbr, lbi = mag * jnp.cos(ang), mag * jnp.sin(ang)
    den = lr * lr + li * li
    fr, fi = ((lbr - 1.0) * lr + lbi * li) / den, (lbi * lr - (lbr - 1.0) * li) / den
    return lr, li, dt, lbr, lbi, fr, fi, den


def s5_prep(a_re, a_im, log_dt, b_re, b_im, c_re, c_im):
    def body(ar_ref, ai_ref, t_ref, br_ref, bi_ref, cr_ref, ci_ref, rb_ref, rbt_ref, rc_ref, rct_ref, lr_ref, li_ref):
        _, _, _, lbr, lbi, fr, fi, _ = _s5_factors(ar_ref[...], ai_ref[...], t_ref[...])
        lr_ref[...] = lbr
        li_ref[...] = lbi
        bre = fr * br_ref[...] - fi * bi_ref[...]
        bim = fr * bi_ref[...] + fi * br_ref[...]
        even = (lax.broadcasted_iota(jnp.int32, (256, S5_STATE), 0) // S5_GROUP) % 2 == 0

        def assemble(re, im):
            re, im = re.reshape(256, S5_STATE), im.reshape(256, S5_STATE)
            return jnp.concatenate([jnp.where(even, re, 0.0), jnp.where(even, 0.0, re), jnp.where(even, im, 0.0),
                                    jnp.where(even, 0.0, im)], axis=1)

        for blk in range(S5_BLOCKS):
            sl = slice(16 * blk, 16 * blk + 16)
            rb = assemble(bre[sl], bim[sl])
            rct = assemble(cr_ref[sl], -ci_ref[sl])
            rb_ref[blk] = rb.astype(BF16)
            rbt_ref[blk] = rb.T.astype(BF16)
            rct_ref[blk] = rct.astype(BF16)
            rc_ref[blk] = rct.T.astype(BF16)

    vm = pl.BlockSpec(memory_space=pltpu.VMEM)
    mat = _sds((S5_BLOCKS, 256, 256), BF16)
    lam = _sds((S5_GROUPS, 1, S5_STATE), F32)
    rb, rbt, rc, rct, lam_r, lam_i = pl.pallas_call(
        body, in_specs=[vm] * 7, out_specs=[vm] * 6, out_shape=[mat, mat, mat, mat, lam, lam], name="s5_prep",
        compiler_params=_params())(*_s5_views(a_re, a_im, log_dt, b_re, b_im), c_re, c_im)
    return rb, rbt, rc, rct, lam_r.reshape(S5_BLOCKS, 8, 128), lam_i.reshape(S5_BLOCKS, 8, 128)


def s5_param_bwd(mats, lams, a_re, a_im, log_dt, b_re, b_im):
    def body(m_ref, glr_ref, gli_ref, ar_ref, ai_ref, t_ref, br_ref, bi_ref,
             dar_ref, dai_ref, dt_ref, dbr_ref, dbi_ref, dcr_ref, dci_ref):
        lr, li, dt, lbr, lbi, fr, fi, den = _s5_factors(ar_ref[...], ai_ref[...], t_ref[...])
        shape = (S5_GROUPS, S5_GROUP, S5_STATE)
        gbr, gbi = m_ref[0:1024, 0:64].reshape(shape), m_ref[0:1024, 64:128].reshape(shape)
        dcr_ref[...] = m_ref[1024:2048, 0:64].reshape(shape)
        dci_ref[...] = -m_ref[1024:2048, 64:128].reshape(shape)
        br, bi = br_ref[...], bi_ref[...]
        dbr_ref[...] = fr * gbr + fi * gbi
        dbi_ref[...] = fr * gbi - fi * gbr
        dfr = jnp.sum(gbr * br + gbi * bi, axis=1, keepdims=True)
        dfi = jnp.sum(gbi * br - gbr * bi, axis=1, keepdims=True)
        nr, ni = (dfr * lr - dfi * li) / den, (dfr * li + dfi * lr) / den
        qr, qi = (fr * lr + fi * li) / den, (fi * lr - fr * li) / den
        lam_r, lam_i = -(dfr * qr + dfi * qi), -(dfi * qr - dfr * qi)
        gr, gi = glr_ref[...] + nr, gli_ref[...] + ni
        zr, zi = gr * lbr + gi * lbi, gi * lbr - gr * lbi
        a = ar_ref[...]
        dar_ref[...] = (lam_r + zr * dt) * jnp.where(a < LAMBDA_RE_MAX, 1.0, jnp.where(a == LAMBDA_RE_MAX, 0.5, 0.0))
        dai_ref[...] = lam_i + zi * dt
        dt_ref[...] = jnp.sum(zr * lr + zi * li, axis=2, keepdims=True) * dt

    vm = pl.BlockSpec(memory_space=pltpu.VMEM)
    state = _sds((S5_GROUPS, 1, S5_STATE), F32)
    wide = _sds((S5_GROUPS, S5_GROUP, S5_STATE), F32)
    glr = lams[0:32].reshape(S5_GROUPS, 1, S5_STATE)
    gli = lams[32:64].reshape(S5_GROUPS, 1, S5_STATE)
    dar, dai, ddt, dbr, dbi, dcr, dci = pl.pallas_call(
        body, in_specs=[vm] * 8, out_specs=[vm] * 7,
        out_shape=[state, state, _sds((S5_GROUPS, 1, 1), F32), wide, wide, wide, wide], name="s5_param_bwd",
        compiler_params=_params())(mats, glr, gli, *_s5_views(a_re, a_im, log_dt, b_re, b_im))
    return (dar.reshape(S5_GROUPS, S5_STATE), dai.reshape(S5_GROUPS, S5_STATE), ddt.reshape(S5_GROUPS),
            jnp.swapaxes(dbr, 1, 2), jnp.swapaxes(dbi, 1, 2), dcr, dci)


def s5_compact(drb, drct, dlr, dli):
    def body(drb_ref, drct_ref, dlr_ref, dli_ref, o_ref, lam_ref):
        even = (lax.broadcasted_iota(jnp.int32, (256, 64), 0) // S5_GROUP) % 2 == 0
        for blk in range(S5_BLOCKS):
            for k, ref in enumerate((drb_ref, drct_ref)):
                m = ref[blk]
                re = jnp.where(even, m[:, 0:64], m[:, 64:128])
                im = jnp.where(even, m[:, 128:192], m[:, 192:256])
                o_ref[pl.ds(k * 1024 + blk * 256, 256), :] = jnp.concatenate([re, im], axis=1)
            lam_ref[pl.ds(blk * 8, 8), :] = dlr_ref[blk]
            lam_ref[pl.ds(32 + blk * 8, 8), :] = dli_ref[blk]

    vm = pl.BlockSpec(memory_space=pltpu.VMEM)
    return pl.pallas_call(body, in_specs=[vm] * 4, out_specs=[vm, vm], out_shape=[_sds((2048, 128), F32), _sds((64, 128), F32)],
                          name="s5_compact", compiler_params=_params())(drb, drct, dlr, dli)


NEG = -1e30


GROUP = N_Q // N_KV


def _attn_masks(n):
    qi = lax.broadcasted_iota(jnp.int32, (GROUP * BLOCK, BLOCK), 0) % BLOCK
    kj = lax.broadcasted_iota(jnp.int32, (GROUP * BLOCK, BLOCK), 1)
    return jnp.logical_and(kj > qi, n > 0), kj <= qi


def _stack_heads(ref, kh):
    return jnp.concatenate([ref[:, (GROUP * kh + g) * HEAD_DIM:(GROUP * kh + g + 1) * HEAD_DIM] for g in range(GROUP)], axis=0)


def _unstack_heads(val):
    return jnp.concatenate([val[g * BLOCK:(g + 1) * BLOCK] for g in range(GROUP)], axis=1)


def _sink_column(sink_ref, kh):
    grp = lax.broadcasted_iota(jnp.int32, (GROUP * BLOCK, 1), 0) // BLOCK
    col = jnp.zeros((GROUP * BLOCK, 1), F32)
    for g in range(GROUP):
        col = jnp.where(grp == g, sink_ref[GROUP * kh + g], col)
    return col, grp


def _attn_exp(q4, kp, kc, sink, mask_p, mask_c):
    scale = 1.0 / math.sqrt(HEAD_DIM)
    nt = (((1,), (1,)), ((), ()))
    sp = jnp.where(mask_p, lax.dot_general(q4, kp, nt, preferred_element_type=F32) * scale, NEG)
    sc = jnp.where(mask_c, lax.dot_general(q4, kc, nt, preferred_element_type=F32) * scale, NEG)
    m = jnp.maximum(jnp.maximum(jnp.max(sp, axis=-1, keepdims=True), jnp.max(sc, axis=-1, keepdims=True)), sink)
    pp = jnp.exp(sp - m)
    pc = jnp.exp(sc - m)
    ps = jnp.exp(sink - m)
    inv = 1.0 / (jnp.sum(pp, axis=-1, keepdims=True) + jnp.sum(pc, axis=-1, keepdims=True) + ps)
    return pp, pc, ps, inv


def attn_fwd(q, kv, sinks):
    n_rows = q.shape[0]
    nb = n_rows // BLOCK

    def body(sink_ref, q_ref, kvp_ref, kvc_ref, o_ref):
        n = pl.program_id(0)
        mask_p, mask_c = _attn_masks(n)
        outs = []
        for kh in range(N_KV):
            ks, vs = slice(kh * HEAD_DIM, (kh + 1) * HEAD_DIM), slice((N_KV + kh) * HEAD_DIM, (N_KV + kh + 1) * HEAD_DIM)
            sink, _ = _sink_column(sink_ref, kh)
            pp, pc, _, inv = _attn_exp(_stack_heads(q_ref, kh), kvp_ref[:, ks], kvc_ref[:, ks], sink, mask_p, mask_c)
            o4 = (jnp.dot(pp.astype(BF16), kvp_ref[:, vs], preferred_element_type=F32)
                  + jnp.dot(pc.astype(BF16), kvc_ref[:, vs], preferred_element_type=F32)) * inv
            outs.append(_unstack_heads(o4))
        o_ref[...] = jnp.concatenate(outs, axis=1).astype(BF16)

    kvw = 2 * N_KV * HEAD_DIM
    return pl.pallas_call(
        body, grid=(nb,),
        in_specs=[pl.BlockSpec(memory_space=pltpu.SMEM), pl.BlockSpec((BLOCK, D_MODEL), lambda n: (n, 0)),
                  pl.BlockSpec((BLOCK, kvw), lambda n: (jnp.maximum(n - 1, 0), 0)), pl.BlockSpec((BLOCK, kvw), lambda n: (n, 0))],
        out_specs=pl.BlockSpec((BLOCK, D_MODEL), lambda n: (n, 0)), out_shape=_sds((n_rows, D_MODEL), BF16),
        name="attn_fwd", compiler_params=_params(("parallel",)))(sinks, q, kv, kv)


def attn_bwd(q, kv, do, sinks):
    n_rows = q.shape[0]
    nb = n_rows // BLOCK
    kvw = 2 * N_KV * HEAD_DIM
    tn = (((0,), (0,)), ((), ()))
    nt = (((1,), (1,)), ((), ()))
    scale = 1.0 / math.sqrt(HEAD_DIM)

    def body(sink_ref, q_ref, kvp_ref, kvc_ref, do_ref, dq_ref, dbq_ref, dprev_ref, dcur_ref, dsink_ref):
        n = pl.program_id(0)
        mask_p, mask_c = _attn_masks(n)
        lane = lax.broadcasted_iota(jnp.int32, (1, D_MODEL), 1)
        dqs, dsink = [], jnp.zeros((1, D_MODEL), F32)
        dkp, dkc, dvp, dvc = [], [], [], []
        for kh in range(N_KV):
            ks, vs = slice(kh * HEAD_DIM, (kh + 1) * HEAD_DIM), slice((N_KV + kh) * HEAD_DIM, (N_KV + kh + 1) * HEAD_DIM)
            q4, do4 = _stack_heads(q_ref, kh), _stack_heads(do_ref, kh)
            kp, kc, vp, vc = kvp_ref[:, ks], kvc_ref[:, ks], kvp_ref[:, vs], kvc_ref[:, vs]
            sink, grp = _sink_column(sink_ref, kh)
            pp, pc, ps, inv = _attn_exp(q4, kp, kc, sink, mask_p, mask_c)
            pp, pc = pp * inv, pc * inv
            dpp = lax.dot_general(do4, vp, nt, preferred_element_type=F32)
            dpc = lax.dot_general(do4, vc, nt, preferred_element_type=F32)
            delta = jnp.sum(pp * dpp, axis=-1, keepdims=True) + jnp.sum(pc * dpc, axis=-1, keepdims=True)
            dsp = (pp * (dpp - delta) * scale).astype(BF16)
            dsc = (pc * (dpc - delta) * scale).astype(BF16)
            dsk = ps * inv * delta
            for g in range(GROUP):
                dsink = dsink + jnp.where(lane == GROUP * kh + g, -jnp.sum(jnp.where(grp == g, dsk, 0.0)), 0.0)
            dqs.append(_unstack_heads(jnp.dot(dsp, kp, preferred_element_type=F32)
                                      + jnp.dot(dsc, kc, preferred_element_type=F32)))
            dkp.append(lax.dot_general(dsp, q4, tn, preferred_element_type=F32))
            dkc.append(lax.dot_general(dsc, q4, tn, preferred_element_type=F32))
            dvp.append(lax.dot_general(pp.astype(BF16), do4, tn, preferred_element_type=F32))
            dvc.append(lax.dot_general(pc.astype(BF16), do4, tn, preferred_element_type=F32))
        dq = jnp.concatenate(dqs, axis=1)
        dq_ref[...] = dq.astype(BF16)
        dprev_ref[0] = jnp.concatenate(dkp + dvp, axis=1)
        dcur_ref[0] = jnp.concatenate(dkc + dvc, axis=1)

        @pl.when(n == 0)
        def _():
            dbq_ref[...] = jnp.zeros_like(dbq_ref)
            dsink_ref[...] = jnp.zeros_like(dsink_ref)

        dbq_ref[...] += jnp.sum(dq, axis=0, keepdims=True)
        dsink_ref[...] += dsink

    blk = pl.BlockSpec((BLOCK, D_MODEL), lambda n: (n, 0))
    part = pl.BlockSpec((1, BLOCK, kvw), lambda n: (n, 0, 0))
    return pl.pallas_call(
        body, grid=(nb,),
        in_specs=[pl.BlockSpec(memory_space=pltpu.SMEM), blk,
                  pl.BlockSpec((BLOCK, kvw), lambda n: (jnp.maximum(n - 1, 0), 0)), pl.BlockSpec((BLOCK, kvw), lambda n: (n, 0)), blk],
        out_specs=[blk, pl.BlockSpec((1, D_MODEL), lambda n: (0, 0)), part, part, pl.BlockSpec((1, D_MODEL), lambda n: (0, 0))],
        out_shape=[_sds((n_rows, D_MODEL), BF16), _sds((1, D_MODEL), F32), _sds((nb, BLOCK, kvw), F32),
                   _sds((nb, BLOCK, kvw), F32), _sds((1, D_MODEL), F32)],
        name="attn_bwd", compiler_params=_params(("arbitrary",)))(sinks, q, kv, kv, do)


def kv_combine(dprev, dcur):
    nb, _, kvw = dprev.shape

    def body(dcur_ref, dprev_ref, dkv_ref, db_ref):
        total = jnp.zeros((1, kvw), F32)
        for m in range(nb):
            dkv = dcur_ref[m] + dprev_ref[m + 1] if m + 1 < nb else dcur_ref[m]
            dkv_ref[m * BLOCK:(m + 1) * BLOCK, :] = dkv.astype(BF16)
            total = total + jnp.sum(dkv, axis=0, keepdims=True)
        db_ref[...] = jnp.concatenate([total, jnp.zeros((1, D_MODEL - kvw), F32)], axis=1)

    vm = pl.BlockSpec(memory_space=pltpu.VMEM)
    return pl.pallas_call(body, in_specs=[vm, vm], out_specs=[vm, vm],
                          out_shape=[_sds((nb * BLOCK, kvw), BF16), _sds((1, D_MODEL), F32)], name="kv_combine",
                          compiler_params=_params())(dcur, dprev)


def glu_bwd(dout, val, gate, tm=256):
    n_rows, d = dout.shape

    def body(do_ref, v_ref, g_ref, dz_ref, db_ref):
        i = pl.program_id(0)
        sg = jax.nn.sigmoid(g_ref[...])
        dval = do_ref[...] * sg
        dgate = do_ref[...] * v_ref[...] * sg * (1.0 - sg)
        dz_ref[...] = jnp.concatenate([dval, dgate], axis=1).astype(BF16)

        @pl.when(i == 0)
        def _():
            db_ref[...] = jnp.zeros_like(db_ref)

        db_ref[0:1, :] += jnp.sum(dval, axis=0, keepdims=True)
        db_ref[1:2, :] += jnp.sum(dgate, axis=0, keepdims=True)

    row = pl.BlockSpec((tm, d), lambda i: (i, 0))
    return pl.pallas_call(
        body, grid=(n_rows // tm,), in_specs=[row, row, row],
        out_specs=[pl.BlockSpec((tm, 2 * d), lambda i: (i, 0)), pl.BlockSpec((2, d), lambda i: (0, 0))],
        out_shape=[_sds((n_rows, 2 * d), BF16), _sds((2, d), F32)],
        name="glu_bwd", compiler_params=_params(("arbitrary",)))(dout, val, gate)


def _adam_update(w, g, m, v):
    nm = ADAM_B1 * m + (1.0 - ADAM_B1) * g
    nv = ADAM_B2 * v + (1.0 - ADAM_B2) * (g * g)
    m_hat = nm / (1.0 - ADAM_B1 ** ADAM_STEP)
    v_hat = nv / (1.0 - ADAM_B2 ** ADAM_STEP)
    return -ADAM_LR * (m_hat / (jnp.sqrt(v_hat) + ADAM_EPS) + ADAM_WD * w), nm, nv


def adamw(name, ws, gs, ms, vs, steps=8):
    n = len(ws)

    def body(*refs):
        for k in range(n):
            w_ref, g_ref, m_ref, v_ref = (refs[j * n + k] for j in range(4))
            go_ref, d_ref, nm_ref, nv_ref = (refs[(4 + j) * n + k] for j in range(4))
            gv = g_ref[...]
            go_ref[...] = gv
            d_ref[...], nm_ref[...], nv_ref[...] = _adam_update(w_ref[...], gv, m_ref[...], v_ref[...])

    specs = [pl.BlockSpec((w.shape[0] // steps, w.shape[1]), lambda i: (i, 0)) for w in ws]
    shapes = [_sds(w.shape, F32) for w in ws]
    out = pl.pallas_call(
        body, grid=(steps,), in_specs=specs * 4, out_specs=specs * 4, out_shape=shapes * 4, name=name,
        compiler_params=_params(("parallel",)))(*ws, *gs, *ms, *vs)
    return [list(out[j * n:(j + 1) * n]) for j in range(4)]


def adamw_native(name, ws, gs, ms, vs):
    n = len(ws)

    def body(*refs):
        w_refs, g_refs, m_refs, v_refs = refs[:n], refs[n:2 * n], refs[2 * n:3 * n], refs[3 * n:4 * n]
        d_refs, nm_refs, nv_refs = refs[4 * n:5 * n], refs[5 * n:6 * n], refs[6 * n:7 * n]
        for k in range(n):
            dl, nm, nv = _adam_update(w_refs[k][...], g_refs[k][...], m_refs[k][...], v_refs[k][...])
            d_refs[k][...] = dl
            nm_refs[k][...] = nm
            nv_refs[k][...] = nv

    vm = pl.BlockSpec(memory_space=pltpu.VMEM)
    shapes = [_sds(w.shape, F32) for w in ws]
    out = pl.pallas_call(body, in_specs=[vm] * (4 * n), out_specs=[vm] * (3 * n), out_shape=shapes * 3, name=name,
                         compiler_params=_params())(*ws, *gs, *ms, *vs)
    return list(out[:n]), list(out[n:2 * n]), list(out[2 * n:])


VEC_ROWS = {"norm_mix": 0, "norm_mlp": 2, "norm_kv": 4, "norm_final": 5, "s5_d": 6, "b_q": 7, "b_o": 8, "s5_b_glu": 9,
            "b_kv": 11, "sinks": 12, "loss": 13}


def split_vectors(where, vecs, d_shard, glu_shard):
    kvw = 2 * N_KV * HEAD_DIM
    shapes = {"norm_mix": (2, D_MODEL), "norm_mlp": (2, D_MODEL), "norm_kv": (1, D_MODEL), "norm_final": (1, D_MODEL),
              "s5_d": (1, d_shard), "b_q": (1, D_MODEL), "b_o": (1, D_MODEL), "s5_b_glu": (1, glu_shard), "b_kv": (1, kvw),
              "sinks": (1, N_Q), "loss": (1, 128)}
    names = list(shapes)

    def body(where_ref, v_ref, *o_refs):
        chip = where_ref[1]
        for name, o_ref in zip(names, o_refs):
            r0, (r, n) = VEC_ROWS[name], shapes[name]
            if name == "s5_d":
                g = jnp.zeros((1, n), F32)
                for j in range(4):
                    g = jnp.where(chip == j, v_ref[r0:r0 + 1, j * n:(j + 1) * n], g)
            elif name == "s5_b_glu":
                g = jnp.zeros((1, n), F32)
                for j in range(4):
                    row, col = r0 + (j * n) // D_MODEL, (j * n) % D_MODEL
                    g = jnp.where(chip == j, v_ref[row:row + 1, col:col + n], g)
            else:
                g = v_ref[r0:r0 + r, 0:n]
            o_ref[...] = g

    vm = pl.BlockSpec(memory_space=pltpu.VMEM)
    out = pl.pallas_call(body, in_specs=[pl.BlockSpec(memory_space=pltpu.SMEM), vm], out_specs=[vm] * len(names),
                         out_shape=[_sds(shapes[n], F32) for n in names], name="split_vectors",
                         compiler_params=_params())(where, vecs)
    return dict(zip(names, out))


def _position():
    x, y, c = lax.axis_index("x"), lax.axis_index("y"), lax.axis_index("c")
    others = [(1 - x, y), (x, 1 - y), (1 - x, 1 - y)]
    return x, y, c, others


def _window(ref, kind, chip, half, shard_shape):
    if kind == "slab":
        return ref.at[chip]
    r, n = shard_shape
    if kind == "col":
        return ref.at[pl.ds(pl.multiple_of(half * (r // 2), 16), r // 2), pl.ds(pl.multiple_of(chip * n, 128), n)]
    return ref.at[pl.ds(pl.multiple_of(chip * r, 16), r), pl.ds(pl.multiple_of(half * (n // 2), 128), n // 2)]


def _half(ref, kind, half, shape):
    r, n = shape
    if kind == "col":
        return ref.at[pl.ds(pl.multiple_of(half * (r // 2), 16), r // 2), :]
    return ref.at[:, pl.ds(pl.multiple_of(half * (n // 2), 128), n // 2)]


def swap_start(name, grads, kinds, carry):
    nt = len(grads)
    shapes = [tuple(g.shape) for g in grads]
    lands = [lax.empty(sh, BF16) for sh in shapes]
    given, given_specs, token_type, write = _hand_through(carry)
    n_in = 2 * nt + len(given)

    def body(*refs):
        in_refs, land_refs = refs[:nt], refs[nt:2 * nt]
        send_sems, recv_sems, token = refs[n_in], refs[n_in + 1], refs[-1]
        x, y, c, _ = _position()
        for t in range(nt):
            pltpu.make_async_remote_copy(
                src_ref=_half(in_refs[t], kinds[t], 1 - c, shapes[t]), dst_ref=_half(land_refs[t], kinds[t], 1 - c, shapes[t]),
                send_sem=send_sems.at[t], recv_sem=recv_sems.at[t], device_id=(x, y, 1 - c), device_id_type=MESH).start()
        write(token, refs[:n_in])

    sems = pltpu.SemaphoreType.DMA((nt,))
    both = list(grads) + lands
    out = pl.pallas_call(
        body, name=name, in_specs=[HBM_SPEC] * (2 * nt) + given_specs,
        out_specs=(SEM_SPEC, SEM_SPEC, *[HBM_SPEC] * (2 * nt), pl.BlockSpec(memory_space=pltpu.VMEM)),
        out_shape=(sems, sems, *[pltpu.HBM(a.shape, a.dtype) for a in both], token_type),
        input_output_aliases={t: 2 + t for t in range(2 * nt)}, compiler_params=_split_params(),
    )(*[_in_hbm(a) for a in both], *given)
    return out[0], out[1], list(out[2:2 + nt]), list(out[2 + nt:2 + 2 * nt]), out[-1]


def swap_wait(name, send_sems, recv_sems, grads, lands, kinds, after):
    nt = len(grads)
    shapes = [tuple(g.shape) for g in grads]

    def body(*refs):
        in_refs, land_refs = refs[:nt], refs[nt:2 * nt]
        send_ref, recv_ref = refs[2 * nt], refs[2 * nt + 1]
        x, y, c, _ = _position()
        for t in range(nt):
            cp = pltpu.make_async_remote_copy(
                src_ref=_half(in_refs[t], kinds[t], 1 - c, shapes[t]), dst_ref=_half(land_refs[t], kinds[t], c, shapes[t]),
                send_sem=send_ref.at[t], recv_sem=recv_ref.at[t], device_id=(x, y, 1 - c), device_id_type=MESH)
            cp.wait_send()
            cp.wait_recv()

    both = list(grads) + list(lands)
    out = pl.pallas_call(
        body, name=name, in_specs=[HBM_SPEC] * (2 * nt) + [SEM_SPEC, SEM_SPEC, HBM_SPEC], out_specs=[HBM_SPEC] * (2 * nt),
        out_shape=[pltpu.HBM(a.shape, a.dtype) for a in both], input_output_aliases={t: t for t in range(2 * nt)},
        compiler_params=_split_params())(*both, send_sems, recv_sems, _in_hbm(after))
    return list(out[:nt]), list(out[nt:])


def _half_spec(kind, shape, tiles):
    r, n = shape
    if kind == "col":
        tn = n // tiles
        return pl.BlockSpec((r // 2, tn), lambda i, s: (s[0], i))
    tm = r // tiles
    return pl.BlockSpec((tm, n // 2), lambda i, s: (i, s[0]))


def add_halves(name, mine, landed, kinds, where, tiles=2):
    nt = len(mine)
    shapes = [tuple(a.shape) for a in mine]

    def compact(t):
        r, n = shapes[t]
        if kinds[t] == "col":
            return (r // 2, n), pl.BlockSpec((r // 2, n // tiles), lambda i, s: (0, i))
        return (r, n // 2), pl.BlockSpec((r // tiles, n // 2), lambda i, s: (i, 0))

    def body(s_ref, *refs):
        for a_ref, b_ref, o_ref in zip(refs[:nt], refs[nt:2 * nt], refs[2 * nt:]):
            o_ref[...] = (a_ref[...].astype(F32) + b_ref[...].astype(F32)).astype(BF16)

    specs = [_half_spec(kinds[t], shapes[t], tiles) for t in range(nt)]
    return pl.pallas_call(
        body, grid_spec=pltpu.PrefetchScalarGridSpec(num_scalar_prefetch=1, grid=(tiles,), in_specs=specs + specs,
                                                     out_specs=[compact(t)[1] for t in range(nt)]),
        out_shape=[_sds(compact(t)[0], BF16) for t in range(nt)], name=name,
        compiler_params=_params(("parallel",)))(where, *mine, *landed)


def sum_shards(name, parts, landed, kinds, shard_shapes, where, layers, n_layers, intos, tiles=2):
    nt = len(parts)
    in_specs, out_specs = [], []
    for t in range(nt):
        (r, n), layer = shard_shapes[t], layers[t]
        if kinds[t] == "col":
            tm, width = r // 2 // tiles, n
            own = pl.BlockSpec((tm, n), lambda i, s: (i, s[1]))
            out = pl.BlockSpec((None, tm, n), lambda i, s, layer=layer: (layer, s[0] * tiles + i, 0))
        else:
            tm, width = r // tiles, n // 2
            own = pl.BlockSpec((tm, n // 2), lambda i, s: (s[1] * tiles + i, 0))
            out = pl.BlockSpec((None, tm, n // 2), lambda i, s, layer=layer: (layer, i, s[0]))
        in_specs += [own, pl.BlockSpec((3, tm, width), lambda i, s: (0, i, 0))]
        out_specs.append(out)
    args, aliases = [where] + [a for pair in zip(parts, landed) for a in pair], {}
    for t in range(nt):
        if intos[t] is not None:
            aliases[len(args)] = t
            in_specs.append(pl.BlockSpec(memory_space=pl.ANY))
            args.append(intos[t])

    def body(s_ref, *refs):
        for t in range(nt):
            a_ref, l_ref, o_ref = refs[2 * t], refs[2 * t + 1], refs[len(in_specs) + t]
            o_ref[...] = ((a_ref[...].astype(F32) + l_ref[0].astype(F32)) + l_ref[1].astype(F32)) + l_ref[2].astype(F32)

    return pl.pallas_call(
        body, grid_spec=pltpu.PrefetchScalarGridSpec(num_scalar_prefetch=1, grid=(tiles,), in_specs=in_specs,
                                                     out_specs=out_specs),
        out_shape=[_sds((n_layers[t],) + tuple(shard_shapes[t]), F32) for t in range(nt)], input_output_aliases=aliases,
        name=name, compiler_params=_params(("parallel",)))(*args)


def share_start(arrays, entries, carry):
    na, nt = len(arrays), len(entries)
    given, given_specs, token_type, write = _hand_through(carry)
    n_in = na + len(given)

    def body(*refs):
        in_refs, send_sems, recv_sems, token = refs[:na], refs[n_in], refs[n_in + 1], refs[-1]
        x, y, c, _ = _position()
        for t, (a, layer, kind) in enumerate(entries):
            mine = _half(in_refs[a].at[layer], kind, c, tuple(arrays[a].shape[1:]))
            pltpu.make_async_remote_copy(
                src_ref=mine, dst_ref=mine, send_sem=send_sems.at[t], recv_sem=recv_sems.at[t],
                device_id=(x, y, 1 - c), device_id_type=MESH).start()
        write(token, refs[:n_in])

    sems = pltpu.SemaphoreType.DMA((nt,))
    out = pl.pallas_call(
        body, name="share_start", in_specs=[HBM_SPEC] * na + given_specs,
        out_specs=(SEM_SPEC, SEM_SPEC, *[HBM_SPEC] * na, pl.BlockSpec(memory_space=pltpu.VMEM)),
        out_shape=(sems, sems, *[pltpu.HBM(a.shape, a.dtype) for a in arrays], token_type),
        input_output_aliases={t: 2 + t for t in range(na)}, compiler_params=_split_params(),
    )(*[_in_hbm(a) for a in arrays], *given)
    return out[0], out[1], list(out[2:2 + na]), out[-1]


def share_wait(send_sems, recv_sems, arrays, entries, after):
    na = len(arrays)

    def body(*refs):
        in_refs, send_ref, recv_ref = refs[:na], refs[na], refs[na + 1]
        x, y, c, _ = _position()
        for t, (a, layer, kind) in enumerate(entries):
            shape = tuple(arrays[a].shape[1:])
            cp = pltpu.make_async_remote_copy(
                src_ref=_half(in_refs[a].at[layer], kind, c, shape), dst_ref=_half(in_refs[a].at[layer], kind, 1 - c, shape),
                send_sem=send_ref.at[t], recv_sem=recv_ref.at[t], device_id=(x, y, 1 - c), device_id_type=MESH)
            cp.wait_send()
            cp.wait_recv()

    return list(pl.pallas_call(
        body, name="share_wait", in_specs=[HBM_SPEC] * na + [SEM_SPEC, SEM_SPEC, HBM_SPEC], out_specs=[HBM_SPEC] * na,
        out_shape=[pltpu.HBM(a.shape, a.dtype) for a in arrays], input_output_aliases={t: t for t in range(na)},
        compiler_params=_split_params())(*arrays, send_sems, recv_sems, _in_hbm(after)))


HBM_SPEC = pl.BlockSpec(memory_space=pltpu.HBM)
SEM_SPEC = pl.BlockSpec(memory_space=pltpu.SEMAPHORE)
ANY_SPEC = pl.BlockSpec(memory_space=pl.ANY)


def _split_params():
    return pltpu.CompilerParams(has_side_effects=pltpu.SideEffectType.DATAFLOW_SIDE_EFFECTING,
                                vmem_limit_bytes=VMEM_LIMIT_BYTES)


def _in_hbm(a):
    return pltpu.with_memory_space_constraint(a, pltpu.HBM)


def cast_place(arrays, entries, where, tiles=2):
    in_specs, out_specs, fulls = [], [], []
    for a, layer, kind in entries:
        _, r, n = arrays[a].shape
        tm = r // tiles
        in_specs.append(pl.BlockSpec((None, tm, n), lambda i, s, layer=layer: (layer, i, 0)))
        if kind == "col":
            fulls.append((r, 4 * n))
            out_specs.append(pl.BlockSpec((tm, n), lambda i, s: (i, s[1])))
        else:
            fulls.append((4 * r, n))
            out_specs.append(pl.BlockSpec((tm, n), lambda i, s: (s[1] * tiles + i, 0)))
    nt = len(entries)

    def body(s_ref, *refs):
        for w_ref, o_ref in zip(refs[:nt], refs[nt:]):
            o_ref[...] = w_ref[...].astype(BF16)

    return pl.pallas_call(
        body, grid_spec=pltpu.PrefetchScalarGridSpec(num_scalar_prefetch=1, grid=(tiles,), in_specs=in_specs,
                                                     out_specs=out_specs),
        out_shape=[_sds(f, BF16) for f in fulls], name="cast_place",
        compiler_params=_params(("parallel",)))(where, *[arrays[a] for a, _, _ in entries])


def _hand_through(carry):
    given = [] if isinstance(carry, tuple) else [carry]

    def write(token, ins):
        token[...] = ins[-1][...] if given else jnp.zeros_like(token)

    return (given, [pl.BlockSpec(memory_space=pltpu.VMEM)] * len(given),
            _sds(carry if isinstance(carry, tuple) else carry.shape, F32), write)


def gather_start(fulls, kinds, shard_shapes, carry):
    nt = len(fulls)
    given, given_specs, token_type, write = _hand_through(carry)
    n_in = nt + len(given)

    def body(*refs):
        full_refs = refs[:nt]
        send_sems, recv_sems, token = refs[n_in], refs[n_in + 1], refs[-1]
        x, y, c, others = _position()
        for t in range(nt):
            mine = _window(full_refs[t], kinds[t], 2 * x + y, c, shard_shapes[t])
            for j, (ox, oy) in enumerate(others):
                pltpu.make_async_remote_copy(
                    src_ref=mine, dst_ref=mine, send_sem=send_sems.at[3 * t + j], recv_sem=recv_sems.at[3 * t + j],
                    device_id=(ox, oy, c), device_id_type=MESH).start()
        write(token, refs[:n_in])

    sems = pltpu.SemaphoreType.DMA((3 * nt,))
    out = pl.pallas_call(
        body, name="gather_start", in_specs=[HBM_SPEC] * nt + given_specs,
        out_specs=(SEM_SPEC, SEM_SPEC, *[HBM_SPEC] * nt, pl.BlockSpec(memory_space=pltpu.VMEM)),
        out_shape=(sems, sems, *[pltpu.HBM(f.shape, f.dtype) for f in fulls], token_type),
        input_output_aliases={t: 2 + t for t in range(nt)}, compiler_params=_split_params(),
    )(*[_in_hbm(f) for f in fulls], *given)
    return out[0], out[1], list(out[2:2 + nt]), out[-1]


def gather_wait(name, send_sems, recv_sems, fulls, kinds, shard_shapes, after, first):
    nt = len(fulls)
    extra = [] if after is None else [_in_hbm(after)]

    def body(*refs):
        full_refs, send_ref, recv_ref = refs[:nt], refs[nt], refs[nt + 1]
        x, y, c, others = _position()
        for t in range(nt):
            mine = _window(full_refs[t], kinds[t], 2 * x + y, c, shard_shapes[t])
            for j, (ox, oy) in enumerate(others):
                cp = pltpu.make_async_remote_copy(
                    src_ref=mine, dst_ref=_window(full_refs[t], kinds[t], 2 * ox + oy, c, shard_shapes[t]),
                    send_sem=send_ref.at[3 * (first + t) + j], recv_sem=recv_ref.at[3 * (first + t) + j],
                    device_id=(ox, oy, c), device_id_type=MESH)
                cp.wait_send()
                cp.wait_recv()

    out = pl.pallas_call(
        body, name=name, in_specs=[HBM_SPEC] * nt + [SEM_SPEC, SEM_SPEC] + [HBM_SPEC] * len(extra),
        out_specs=[HBM_SPEC] * nt, out_shape=[pltpu.HBM(f.shape, f.dtype) for f in fulls],
        input_output_aliases={t: t for t in range(nt)}, compiler_params=_split_params())(*fulls, send_sems, recv_sems, *extra)
    return list(out)


def forward_halves(name, fulls, kinds, shard_shapes):
    nt = len(fulls)

    def body(*refs):
        out_refs = refs[nt:2 * nt]
        send_sems, recv_sems = refs[2 * nt:]
        x, y, c, others = _position()
        cps = []
        for t in range(nt):
            for j, (ox, oy) in enumerate(others):
                landed = _window(out_refs[t], kinds[t], 2 * ox + oy, c, shard_shapes[t])
                cp = pltpu.make_async_remote_copy(
                    src_ref=landed, dst_ref=landed, send_sem=send_sems.at[3 * t + j], recv_sem=recv_sems.at[3 * t + j],
                    device_id=(x, y, 1 - c), device_id_type=MESH)
                cp.start()
                cps.append(cp)
        for t in range(nt):
            for j, (ox, oy) in enumerate(others):
                got = _window(out_refs[t], kinds[t], 2 * ox + oy, 1 - c, shard_shapes[t])
                pltpu.make_async_remote_copy(
                    src_ref=got, dst_ref=got, send_sem=send_sems.at[3 * t + j], recv_sem=recv_sems.at[3 * t + j],
                    device_id=(x, y, 1 - c), device_id_type=MESH).wait_recv()
        for cp in cps:
            cp.wait_send()

    out = pl.pallas_call(
        body, in_specs=[ANY_SPEC] * nt, out_specs=[ANY_SPEC] * nt, out_shape=[_sds(f.shape, f.dtype) for f in fulls],
        input_output_aliases={t: t for t in range(nt)},
        scratch_shapes=[pltpu.SemaphoreType.DMA((3 * nt,)), pltpu.SemaphoreType.DMA((3 * nt,))],
        name=name, compiler_params=_params())(*fulls)
    return list(out)


def forward_start(name, send_sems, recv_sems, fulls, kinds, shard_shapes, after, first, carry, passing=()):
    nt, n_pass = len(fulls), len(passing)
    given, given_specs, token_type, write = _hand_through(carry)
    n_in = nt + 3 + n_pass + len(given)

    def body(*refs):
        full_refs, ici_send, ici_recv = refs[:nt], refs[nt], refs[nt + 1]
        send_ref, recv_ref, token = refs[n_in], refs[n_in + 1], refs[-1]
        x, y, c, others = _position()
        for t in range(nt):
            mine = _window(full_refs[t], kinds[t], 2 * x + y, c, shard_shapes[t])
            for j, (ox, oy) in enumerate(others):
                landed = _window(full_refs[t], kinds[t], 2 * ox + oy, c, shard_shapes[t])
                cp = pltpu.make_async_remote_copy(
                    src_ref=mine, dst_ref=landed, send_sem=ici_send.at[3 * (first + t) + j],
                    recv_sem=ici_recv.at[3 * (first + t) + j], device_id=(ox, oy, c), device_id_type=MESH)
                cp.wait_send()
                cp.wait_recv()
                pltpu.make_async_remote_copy(
                    src_ref=landed, dst_ref=landed, send_sem=send_ref.at[3 * t + j], recv_sem=recv_ref.at[3 * t + j],
                    device_id=(x, y, 1 - c), device_id_type=MESH).start()
        write(token, refs[:n_in])

    sems = pltpu.SemaphoreType.DMA((3 * nt,))
    out = pl.pallas_call(
        body, name=name, in_specs=[HBM_SPEC] * nt + [SEM_SPEC, SEM_SPEC, HBM_SPEC] + [HBM_SPEC] * n_pass + given_specs,
        out_specs=(SEM_SPEC, SEM_SPEC, *[HBM_SPEC] * (nt + n_pass), pl.BlockSpec(memory_space=pltpu.VMEM)),
        out_shape=(sems, sems, *[pltpu.HBM(f.shape, f.dtype) for f in [*fulls, *passing]], token_type),
        input_output_aliases={**{t: 2 + t for t in range(nt)}, **{nt + 3 + t: 2 + nt + t for t in range(n_pass)}},
        compiler_params=_split_params(),
    )(*fulls, send_sems, recv_sems, _in_hbm(after), *passing, *given)
    return out[0], out[1], list(out[2:2 + nt]), list(out[2 + nt:2 + nt + n_pass]), out[-1]


def forward_wait(name, send_sems, recv_sems, fulls, kinds, shard_shapes, after):
    nt = len(fulls)

    def body(*refs):
        full_refs, send_ref, recv_ref = refs[:nt], refs[nt], refs[nt + 1]
        x, y, c, others = _position()
        for t in range(nt):
            for j, (ox, oy) in enumerate(others):
                cp = pltpu.make_async_remote_copy(
                    src_ref=_window(full_refs[t], kinds[t], 2 * ox + oy, c, shard_shapes[t]),
                    dst_ref=_window(full_refs[t], kinds[t], 2 * ox + oy, 1 - c, shard_shapes[t]),
                    send_sem=send_ref.at[3 * t + j], recv_sem=recv_ref.at[3 * t + j],
                    device_id=(x, y, 1 - c), device_id_type=MESH)
                cp.wait_send()
                cp.wait_recv()

    return list(pl.pallas_call(
        body, name=name, in_specs=[HBM_SPEC] * nt + [SEM_SPEC, SEM_SPEC, HBM_SPEC], out_specs=[HBM_SPEC] * nt,
        out_shape=[pltpu.HBM(f.shape, f.dtype) for f in fulls], input_output_aliases={t: t for t in range(nt)},
        compiler_params=_split_params())(*fulls, send_sems, recv_sems, _in_hbm(after)))


def _piece(ref, kind, chip, shard_shape):
    r, n = shard_shape
    if kind == "col":
        return ref.at[:, pl.ds(pl.multiple_of(chip * n, 128), n)]
    return ref.at[pl.ds(pl.multiple_of(chip * r, 16), r), :]


def _piece_shape(kind, shard_shape):
    r, n = shard_shape
    return (r // 2, n) if kind == "col" else (r, n // 2)


def exchange_start(name, parts, kinds, shard_shapes, carry):
    nt = len(parts)
    lands = [lax.empty((3,) + _piece_shape(kinds[t], shard_shapes[t]), BF16) for t in range(nt)]
    given, given_specs, token_type, write = _hand_through(carry)
    n_in = 2 * nt + len(given)

    def body(*refs):
        part_refs, land_refs = refs[:nt], refs[nt:2 * nt]
        send_sems, recv_sems, token = refs[n_in], refs[n_in + 1], refs[-1]
        x, y, c, others = _position()
        for t in range(nt):
            for j, (ox, oy) in enumerate(others):
                pltpu.make_async_remote_copy(
                    src_ref=_piece(part_refs[t], kinds[t], 2 * ox + oy, shard_shapes[t]), dst_ref=land_refs[t].at[j],
                    send_sem=send_sems.at[3 * t + j], recv_sem=recv_sems.at[3 * t + j],
                    device_id=(ox, oy, c), device_id_type=MESH).start()
        write(token, refs[:n_in])

    sems = pltpu.SemaphoreType.DMA((3 * nt,))
    both = list(parts) + lands
    out = pl.pallas_call(
        body, name=name, in_specs=[HBM_SPEC] * (2 * nt) + given_specs,
        out_specs=(SEM_SPEC, SEM_SPEC, *[HBM_SPEC] * (2 * nt), pl.BlockSpec(memory_space=pltpu.VMEM)),
        out_shape=(sems, sems, *[pltpu.HBM(a.shape, a.dtype) for a in both], token_type),
        input_output_aliases={t: 2 + t for t in range(2 * nt)}, compiler_params=_split_params(),
    )(*[_in_hbm(a) for a in both], *given)
    return out[0], out[1], list(out[2:2 + nt]), list(out[2 + nt:2 + 2 * nt]), out[-1]


def exchange_wait(name, send_sems, recv_sems, parts, lands, kinds, shard_shapes, after):
    nt = len(parts)

    def body(*refs):
        part_refs, land_refs = refs[:nt], refs[nt:2 * nt]
        send_ref, recv_ref = refs[2 * nt], refs[2 * nt + 1]
        x, y, c, others = _position()
        for t in range(nt):
            for j, (ox, oy) in enumerate(others):
                cp = pltpu.make_async_remote_copy(
                    src_ref=_piece(part_refs[t], kinds[t], 2 * ox + oy, shard_shapes[t]), dst_ref=land_refs[t].at[j],
                    send_sem=send_ref.at[3 * t + j], recv_sem=recv_ref.at[3 * t + j],
                    device_id=(ox, oy, c), device_id_type=MESH)
                cp.wait_send()
                cp.wait_recv()

    both = list(parts) + list(lands)
    out = pl.pallas_call(
        body, name=name, in_specs=[HBM_SPEC] * (2 * nt) + [SEM_SPEC, SEM_SPEC, HBM_SPEC], out_specs=[HBM_SPEC] * (2 * nt),
        out_shape=[pltpu.HBM(a.shape, a.dtype) for a in both], input_output_aliases={t: t for t in range(2 * nt)},
        compiler_params=_split_params())(*both, send_sems, recv_sems, _in_hbm(after))
    return list(out[:nt]), list(out[nt:])


def reduce_swap(bufs, wire):
    n = len(bufs)
    halves = [b.shape[0] // 2 for b in bufs]

    def body(*refs):
        in_refs, out_refs, txs, got = refs[:n], refs[n:2 * n], refs[2 * n:3 * n], refs[3 * n:4 * n]
        send_sems, recv_sems = refs[4 * n:]
        x, y, c, _ = _position()
        cps = []
        for k in range(n):
            txs[k][...] = in_refs[k][pl.ds(pl.multiple_of((1 - c) * halves[k], 8), halves[k]), :].astype(wire[k])
            cp = pltpu.make_async_remote_copy(src_ref=txs[k], dst_ref=got[k], send_sem=send_sems.at[k],
                                              recv_sem=recv_sems.at[k], device_id=(x, y, 1 - c), device_id_type=MESH)
            cp.start()
            cps.append(cp)
        for k, cp in enumerate(cps):
            cp.wait()
            own = in_refs[k][pl.ds(pl.multiple_of(c * halves[k], 8), halves[k]), :]
            out_refs[k][...] = (own.astype(wire[k]).astype(F32) + got[k][...].astype(F32)).astype(wire[k])

    vm = pl.BlockSpec(memory_space=pltpu.VMEM)
    parts = [((h, b.shape[1]), w) for h, b, w in zip(halves, bufs, wire)]
    return list(pl.pallas_call(
        body, name="reduce_swap", in_specs=[vm] * n, out_specs=[vm] * n, out_shape=[_sds(sh, w) for sh, w in parts],
        scratch_shapes=[pltpu.VMEM(sh, w) for sh, w in parts] * 2 + [pltpu.SemaphoreType.DMA((n,))] * 2,
        compiler_params=_params())(*bufs))


def reduce_start(parts):
    n = len(parts)
    lands = [lax.empty((4,) + tuple(p.shape), p.dtype) for p in parts]

    def body(*refs):
        part_refs, land_refs, send_sems, recv_sems = refs[:n], refs[n:2 * n], refs[2 * n], refs[2 * n + 1]
        x, y, c, others = _position()
        for k in range(n):
            for j, (ox, oy) in enumerate(others):
                pltpu.make_async_remote_copy(
                    src_ref=part_refs[k], dst_ref=land_refs[k].at[2 * x + y], send_sem=send_sems.at[3 * k + j],
                    recv_sem=recv_sems.at[3 * k + j], device_id=(ox, oy, c), device_id_type=MESH).start()

    sems = pltpu.SemaphoreType.DMA((3 * n,))
    both = list(parts) + lands
    out = pl.pallas_call(
        body, name="reduce_start", in_specs=[HBM_SPEC] * (2 * n), out_specs=(SEM_SPEC, SEM_SPEC, *[HBM_SPEC] * (2 * n)),
        out_shape=(sems, sems, *[pltpu.HBM(a.shape, a.dtype) for a in both]),
        input_output_aliases={k: 2 + k for k in range(2 * n)}, compiler_params=_split_params(),
    )(*[_in_hbm(a) for a in both])
    return out[0], out[1], list(out[2:2 + n]), list(out[2 + n:])


def reduce_wait(send_sems, recv_sems, parts, lands, after):
    n = len(parts)

    def body(*refs):
        part_refs, land_refs, send_ref, recv_ref = refs[:n], refs[n:2 * n], refs[2 * n], refs[2 * n + 1]
        x, y, c, others = _position()
        for k in range(n):
            for j, (ox, oy) in enumerate(others):
                cp = pltpu.make_async_remote_copy(
                    src_ref=part_refs[k], dst_ref=land_refs[k].at[2 * ox + oy], send_sem=send_ref.at[3 * k + j],
                    recv_sem=recv_ref.at[3 * k + j], device_id=(ox, oy, c), device_id_type=MESH)
                cp.wait_send()
                cp.wait_recv()

    both = list(parts) + list(lands)
    out = pl.pallas_call(
        body, name="reduce_wait", in_specs=[HBM_SPEC] * (2 * n) + [SEM_SPEC, SEM_SPEC, HBM_SPEC],
        out_specs=[HBM_SPEC] * (2 * n), out_shape=[pltpu.HBM(a.shape, a.dtype) for a in both],
        input_output_aliases={k: k for k in range(2 * n)}, compiler_params=_split_params(),
    )(*both, send_sems, recv_sems, _in_hbm(after))
    return list(out[:n]), list(out[n:])


def reduce_share(parts, lands):
    n = len(parts)
    halves = [p.shape[0] for p in parts]

    def body(*refs):
        part_refs, land_refs, out_refs = refs[:n], refs[n:2 * n], refs[2 * n:3 * n]
        send_sems, recv_sems = refs[3 * n:]
        x, y, c, _ = _position()
        chip = 2 * x + y
        cps = []
        for k in range(n):
            mine = pl.ds(pl.multiple_of(c * halves[k], 8), halves[k])
            own = part_refs[k][...].astype(F32)
            total = jnp.where(chip == 0, own, land_refs[k][0].astype(F32))
            for entry in range(1, 4):
                total = total + jnp.where(chip == entry, own, land_refs[k][entry].astype(F32))
            out_refs[k][mine, :] = total
            cp = pltpu.make_async_remote_copy(
                src_ref=out_refs[k].at[mine], dst_ref=out_refs[k].at[mine], send_sem=send_sems.at[k],
                recv_sem=recv_sems.at[k], device_id=(x, y, 1 - c), device_id_type=MESH)
            cp.start()
            cps.append(cp)
        for cp in cps:
            cp.wait()

    vm = pl.BlockSpec(memory_space=pltpu.VMEM)
    return list(pl.pallas_call(
        body, name="reduce_share", in_specs=[vm] * (2 * n), out_specs=[vm] * n,
        out_shape=[_sds((2 * p.shape[0], p.shape[1]), F32) for p in parts],
        scratch_shapes=[pltpu.SemaphoreType.DMA((n,))] * 2, compiler_params=_params())(*parts, *lands))


def _local_step(x, target, small, need, ahead, emit_swap, emit_exchange):
    d = D_MODEL
    full = {}

    def handed(vec, token):
        return vec if token is None else token

    def token_rows(token):
        return [] if token is None else [token]

    def plus(acc, rows):
        return acc + rows[0] if rows else acc

    rb16, rbt16, rc16, rct16, lr_t, li_t = small["s5_operands"]
    ge, ge_slope, cs = s5_fwd(x, small["norm_mix0"], small["s5_d"], rb16, rc16, lr_t, li_t)
    full.update(need("glu", ge))

    def norm_rows(h, gains):
        xh, _ = _rms_hat(h)
        return [xh * g for g in gains]

    def glu_epilogue(accs, e, r):
        v, gt = accs[0] + r[0], accs[1] + r[1]
        h = e[0] + v * jax.nn.sigmoid(gt)
        return [h, v, gt] + norm_rows(h, r[2:])

    gain_mlp0 = handed(small["norm_mlp0"], ahead("mlp_in0", full["w_glu"], small["norm_mlp0"]))
    h1, val, gate, n1 = mm_nn(
        "glu", ge, full["w_glu"], [0, d], d, glu_epilogue, [F32, F32, F32, BF16], extras=[x],
        rowvecs=[(small["s5_b_glu"], 0), (small["s5_b_glu"], d), (gain_mlp0, 0)], tm=512, tn=d)

    def mlp_fwd(tag, h, n, w_in, get_w_out, next_gains, head=None):
        def in_epilogue(accs, e, rv):
            pos = jnp.maximum(accs[0], 0.0)
            return [pos * pos, 2.0 * pos]

        r, slope = mm_nn("mlp_in" + tag, n, w_in, [0], w_in.shape[1], in_epilogue, [BF16, BF16], tm=2048)
        w_out = get_w_out(r)

        def epilogue(accs, e, rv):
            h_out = e[0] + accs[0]
            return [h_out] + norm_rows(h_out, rv)

        if head is not None:
            return head(r, w_out, h), (n, r, slope)
        outs = mm_nn("mlp_out" + tag, r, w_out, [0], d, epilogue, [F32] + [BF16] * len(next_gains), extras=[h],
                     rowvecs=[(g, 0) for g in next_gains], tm=512, tn=d)
        return outs[0], outs[1:], (n, r, slope)

    full.update(need("mlp_in0", h1))

    def w_out0(after):
        full.update(need("mlp_out0", after))
        return full["w_out0"]

    h2, (nkv, n2), mlp0 = mlp_fwd("0", h1, n1, full["w_in0"], w_out0, [small["norm_kv"], small["norm_mix1"]])

    full.update(need("attn", h2))
    kvw = 2 * N_KV * HEAD_DIM
    (kv,) = mm_nn("kv_proj", nkv, full["w_kv"], [0], kvw, lambda accs, e, r: [accs[0] + r[0]], [BF16],
                  rowvecs=[(small["b_kv"], 0)], tm=2048)
    (q,) = mm_nn("q_proj", n2, full["w_q"], [0], d, lambda accs, e, r: [accs[0] + r[0]], [BF16],
                 rowvecs=[(small["b_q"], 0)], tm=2048)
    sinks = small["sinks"].reshape(N_Q)
    o = attn_fwd(q, kv, sinks)
    def o_epilogue(accs, e, r):
        h_out = e[0] + accs[0] + r[0]
        return [h_out] + norm_rows(h_out, r[1:])

    bias_o = handed(small["b_o"], ahead("mlp_in1", o, small["b_o"]))
    h3, n3 = mm_nn("o_proj", o, full["w_o"], [0], d, o_epilogue, [F32, BF16], extras=[h2],
                   rowvecs=[(bias_o, 0), (small["norm_mlp1"], 0)], tm=512, tn=d)
    full.update(need("mlp_in1", h3, then="mlp_out1"))

    def w_out1(after):
        full.update(need("mlp_out1", after))
        return full["w_out1"]

    def loss_head(r, w_out, h):
        def epilogue(accs, e, rv):
            xh, rr = _rms_hat(e[0] + accs[0])
            err = xh * rv[0] - e[1]
            dy = err * (1.0 / d)
            dxh = dy * rv[0]
            dx = rr * (dxh - xh * jnp.mean(dxh * xh, axis=-1, keepdims=True))
            loss = jnp.full((1, d), 0.5 * jnp.sum(jnp.mean(err * err, axis=-1, keepdims=True)), F32)
            return [dx, dx, loss, jnp.sum(dy * xh, axis=0, keepdims=True)]

        return mm_nn("mlp_out1", r, w_out, [0], d, epilogue, [F32, BF16], extras=[h, target],
                     rowvecs=[(small["norm_final"], 0)], n_sums=2, tm=512, tn=d)

    (dh, dhb, loss_tile, dg_final), mlp1 = mlp_fwd("1", h3, n3, full["w_in1"], w_out1, [], head=loss_head)

    grads_small, grads_full = {"norm_final": dg_final}, {}
    ident = lambda acc, e, r: [plus(acc, r)]
    layer1 = ["w_out1", "w_in1", "w_o", "w_q", "w_kv"]
    layer0 = ["w_out0", "w_in0", "w_glu"]

    def norm_bwd_rows(x_rows, res, dys, gains):
        xh, r = _rms_hat(x_rows)
        dxh = sum(dy * g for dy, g in zip(dys, gains))
        dx = r * (dxh - xh * jnp.mean(dxh * xh, axis=-1, keepdims=True)) + res
        return dx, [jnp.sum(dy * xh, axis=0, keepdims=True) for dy in dys]

    def mlp_bwd(tag, dh, dhb, h_in, gain, w_in, w_out, saved, token=None):
        n, r, slope = saved
        grads_full["w_out" + tag] = mm_tn("dw_out" + tag, r, dhb, tn=1024)
        (da,) = mm_nt("mlp_da" + tag, dhb, w_out, lambda acc, e, rv: [plus(acc * e[0].astype(F32), rv)], [BF16],
                      extras=[slope], rowvecs=token_rows(token), tm=2048)
        grads_full["w_in" + tag] = mm_tn("dw_in" + tag, n, da, tn=1024)

        def epilogue(acc, e, rv):
            dx, dgs = norm_bwd_rows(e[0], e[1], [acc], rv)
            return [dx, dx, jnp.sum(dx, axis=0, keepdims=True)] + dgs

        dx, dxb, colsum, dg = mm_nt("mlp_dn" + tag, da, w_in, epilogue, [F32, BF16], extras=[h_in, dh], rowvecs=[gain],
                                    n_sums=2, tm=512, tk=d)
        grads_small["norm_mlp" + tag] = dg
        return dx, dxb, colsum

    dh3, dh3b, colsum3 = mlp_bwd("1", dh, dhb, h3, small["norm_mlp1"], full["w_in1"], full["w_out1"], mlp1)
    grads_small["b_o"] = colsum3
    grads_full["w_o"] = mm_tn("dw_o", o, dh3b, tn=1024)
    (do,) = mm_nt("attn_do", dh3b, full["w_o"], ident, [BF16], tm=2048)
    dq, dbq, dprev, dcur, dsink = attn_bwd(q, kv, do, sinks)
    dkv, dbkv = kv_combine(dprev, dcur)
    grads_small["b_q"], grads_small["b_kv"], grads_small["sinks"] = dbq, dbkv, dsink
    grads_full["w_q"] = mm_tn("dw_q", n2, dq, tn=1024)
    grads_full["w_kv"] = mm_tn("dw_kv", nkv, dkv, tk=1024)
    token = emit_swap("layer1", {n: grads_full[n] for n in layer1}, (1, d))
    (dnkv,) = mm_nt("kv_dn", dkv, full["w_kv"], ident, [F32], rowvecs=token_rows(token), tm=2048, tk=1024)

    def attn_dn_epilogue(acc, e, rv):
        dx, dgs = norm_bwd_rows(e[0], e[1], [acc, e[2]], rv)
        return [dx, dx] + dgs

    dh2, dh2b, dg_mix1, dg_kv = mm_nt("attn_dn", dq, full["w_q"], attn_dn_epilogue, [F32, BF16], extras=[h2, dh3, dnkv],
                                      rowvecs=[small["norm_mix1"], small["norm_kv"]], n_sums=2, tm=512, tk=d)
    grads_small["norm_mix1"], grads_small["norm_kv"] = dg_mix1, dg_kv
    token = emit_exchange("layer1", dh2b, (1, full["w_out0"].shape[0]))
    dh1, _, _ = mlp_bwd("0", dh2, dh2b, h1, small["norm_mlp0"], full["w_in0"], full["w_out0"], mlp0, token)

    dz, db_glu = glu_bwd(dh1, val, gate)
    grads_small["s5_b_glu"] = db_glu
    grads_full["w_glu"] = mm_tn("dw_glu", ge, dz, tn=1024)
    token = emit_swap("layer0", {n: grads_full[n] for n in layer0}, (1, d))
    (dy2,) = mm_nt("glu_dy", dz, full["w_glu"], lambda acc, e, rv: [plus(acc, rv) * e[0]], [F32], extras=[ge_slope],
                   rowvecs=token_rows(token), tm=512, tk=1024)
    d_skip = handed(small["s5_d"], emit_exchange("layer0", dy2, small["s5_d"]))
    grad_x, dd, drb, drc, dlr, dli, dg_mix0 = s5_bwd(x, small["norm_mix0"], dy2, dh1, d_skip, cs, rb16, rbt16, rct16, lr_t, li_t)
    grads_small["s5_d"] = dd
    grads_small["s5_mats"] = (drb, drc, dlr, dli)
    grads_small["norm_mix0"] = dg_mix0
    return loss_tile, grad_x, grads_small


SMALL_NAMES = ["norm_mix", "norm_mlp", "norm_kv", "norm_final", "s5_a_re", "s5_a_im", "s5_log_dt", "s5_b_re", "s5_b_im",
               "s5_c_re", "s5_c_im", "s5_d", "s5_b_glu", "b_kv", "b_q", "sinks", "b_o"]
BIG_NAMES = ["s5_w_glu", "w_kv", "w_q", "w_o", "w_mlp_in", "w_mlp_out"]
WEIGHT_ORDER = ["norm_mix", "norm_mlp", "norm_kv", "norm_final", "s5_a_re", "s5_a_im", "s5_log_dt", "s5_b_re", "s5_b_im",
                "s5_c_re", "s5_c_im", "s5_d", "s5_w_glu", "s5_b_glu", "w_kv", "b_kv", "w_q", "b_q", "sinks", "w_o", "b_o",
                "w_mlp_in", "w_mlp_out"]


def kernel(x, norm_mix, norm_mlp, norm_kv, norm_final, s5_a_re, s5_a_im, s5_log_dt, s5_b_re, s5_b_im, s5_c_re, s5_c_im, s5_d, s5_w_glu, s5_b_glu, w_kv, b_kv, w_q, b_q, sinks, w_o, b_o, w_mlp_in, w_mlp_out, loss_target, m_norm_mix, m_norm_mlp, m_norm_kv, m_norm_final, m_s5_a_re, m_s5_a_im, m_s5_log_dt, m_s5_b_re, m_s5_b_im, m_s5_c_re, m_s5_c_im, m_s5_d, m_s5_w_glu, m_s5_b_glu, m_w_kv, m_b_kv, m_w_q, m_b_q, m_sinks, m_w_o, m_b_o, m_w_mlp_in, m_w_mlp_out, v_norm_mix, v_norm_mlp, v_norm_kv, v_norm_final, v_s5_a_re, v_s5_a_im, v_s5_log_dt, v_s5_b_re, v_s5_b_im, v_s5_c_re, v_s5_c_im, v_s5_d, v_s5_w_glu, v_s5_b_glu, v_w_kv, v_b_kv, v_w_q, v_b_q, v_sinks, v_w_o, v_b_o, v_w_mlp_in, v_w_mlp_out):
    env = dict(locals())
    w = {n: env[n] for n in WEIGHT_ORDER}
    mom = {n: env["m_" + n] for n in WEIGHT_ORDER}
    var = {n: env["v_" + n] for n in WEIGHT_ORDER}
    d = D_MODEL
    xi, yi, ci = lax.axis_index("x"), lax.axis_index("y"), lax.axis_index("c")
    chip = 2 * xi + yi
    where = jnp.stack([ci, chip]).astype(jnp.int32)

    dsh, bsh = s5_d.shape[1], s5_b_glu.shape[1]
    packed = jnp.concatenate([s5_d.reshape(-1, 128), s5_b_glu.reshape(-1, 128)])
    n_d, n_b = dsh // 128, bsh // 128
    slab = lax.dynamic_update_slice(jnp.zeros((4, 8, 128), F32), jnp.pad(packed, ((0, 8 - n_d - n_b), (0, 0)))[None],
                                    (chip, 0, 0))

    big = [s5_w_glu, w_kv[None], w_q, w_o, w_mlp_in, w_mlp_out]
    entries = [(0, 0, "col"), (1, 0, "row"), (2, 0, "row"), (3, 0, "row"), (4, 0, "col"), (4, 1, "col"),
               (5, 0, "row"), (5, 1, "row")]
    names = ["w_glu", "w_kv", "w_q", "w_o", "w_in0", "w_in1", "w_out0", "w_out1"]
    kinds = dict(zip(names, [k for _, _, k in entries]))
    shard_shapes = dict(zip(names, [tuple(big[a].shape[1:]) for a, _, _ in entries]))

    placed_w = dict(zip(names, cast_place(big, entries, where)))
    placed_w["vectors"], kinds["vectors"], shard_shapes["vectors"] = slab, "slab", None
    gather_groups = {"glu": ["w_glu"], "mlp_in0": ["w_in0"], "mlp_out0": ["w_out0"], "attn": ["w_kv", "w_q", "w_o"],
                     "mlp_in1": ["w_in1"], "mlp_out1": ["w_out1"]}
    order = ["vectors"] + [n for members in gather_groups.values() for n in members]
    send, recv, thru, log_dt = gather_start([placed_w[n] for n in order], [kinds[n] for n in order],
                                            [shard_shapes[n] for n in order], s5_log_dt)
    started = dict(zip(order, thru))
    (gathered_rows,) = gather_wait("gather_wait_vectors", send, recv, [started["vectors"]], ["slab"], [None], None, 0)
    d_full = gathered_rows[:, 0:n_d].reshape(1, -1)
    bglu_full = gathered_rows[:, n_d:n_d + n_b].reshape(1, -1)

    forwarding = {}

    def ahead(group, after, carry, passing=()):
        members = gather_groups[group]
        ks, shapes = [kinds[n] for n in members], [shard_shapes[n] for n in members]
        d2d_send, d2d_recv, landed, passed, tok = forward_start(
            "forward_start_" + group, send, recv, [started[n] for n in members], ks, shapes, after,
            order.index(members[0]), carry, passing)
        forwarding[group] = (d2d_send, d2d_recv, landed)
        return passed if passing else tok

    def need(group, after, then=None):
        members = gather_groups[group]
        ks, shapes = [kinds[n] for n in members], [shard_shapes[n] for n in members]
        if group in forwarding:
            arrays = forward_wait("forward_wait_" + group, *forwarding[group], ks, shapes, after)
        else:
            landed = gather_wait("gather_wait_" + group, send, recv, [started[n] for n in members], ks, shapes, after,
                                 order.index(members[0]))
            arrays = forward_halves("forward_halves_" + group, landed, ks, shapes)
        if then is not None:
            arrays = ahead(then, after, (8, 128), arrays)
        return dict(zip(members, arrays))

    swapping, exchanging = {}, {}

    def emit_swap(group, partial, carry):
        members = list(partial)
        send, recv, mine, lands, tok = swap_start("swap_start_" + group, [partial[n] for n in members],
                                                  [kinds[n] for n in members], carry)
        swapping[group] = (members, send, recv, mine, lands)
        return tok

    def emit_exchange(group, after, carry):
        members, send, recv, mine, lands = swapping[group]
        ks, shapes = [kinds[n] for n in members], [shard_shapes[n] for n in members]
        mine, landed = swap_wait("swap_wait_" + group, send, recv, mine, lands, ks, after)
        sums = add_halves("add_halves_" + group, mine, landed, ks, where)
        send, recv, parts, lands, tok = exchange_start("exchange_start_" + group, sums, ks, shapes, carry)
        exchanging[group] = (members, send, recv, parts, lands)
        return tok

    s5_args = (s5_a_re[0], s5_a_im[0], log_dt[0], s5_b_re[0], s5_b_im[0])
    small = {
        "norm_mix0": norm_mix[0:1], "norm_mix1": norm_mix[1:2], "norm_mlp0": norm_mlp[0:1], "norm_mlp1": norm_mlp[1:2],
        "norm_kv": norm_kv.reshape(1, d), "norm_final": norm_final.reshape(1, d), "s5_operands": s5_prep(*s5_args, s5_c_re[0], s5_c_im[0]),
        "s5_d": d_full, "s5_b_glu": bglu_full,
        "b_kv": b_kv.reshape(1, -1), "b_q": b_q, "sinks": sinks, "b_o": b_o,
    }
    loss_row, grad_x, gs = _local_step(x[0], loss_target[0], small, need, ahead, emit_swap, emit_exchange)

    mats, lams = s5_compact(*gs["s5_mats"])
    rows = [gs["norm_mix0"], gs["norm_mix1"], gs["norm_mlp0"], gs["norm_mlp1"], gs["norm_kv"], gs["norm_final"], gs["s5_d"],
            gs["b_q"], gs["b_o"], gs["s5_b_glu"], gs["b_kv"], gs["sinks"], loss_row, jnp.zeros((2, d), F32)]
    small_send, small_recv, small_parts, small_lands = reduce_start(
        reduce_swap([jnp.concatenate(rows, axis=0), lams, mats], [F32, F32, BF16]))

    reduced = [None] * len(big)
    where_of = dict(zip(names, entries))
    for group in ("layer1", "layer0"):
        members, send, recv, parts, lands = exchanging[group]
        ks, shapes = [kinds[n] for n in members], [shard_shapes[n] for n in members]
        parts, lands = exchange_wait("exchange_wait_" + group, send, recv, parts, lands, ks, shapes, small_lands[-1])
        targets = [where_of[n][0] for n in members]
        sums = sum_shards("sum_shards_" + group, parts, lands, ks, shapes, where, [where_of[n][1] for n in members],
                          [big[a].shape[0] for a in targets], [reduced[a] for a in targets])
        for a, arr in zip(targets, sums):
            reduced[a] = arr
    share_send, share_recv, reduced, _ = share_start(reduced, entries, (8, 128))

    vecs, lams, mats = reduce_share(*reduce_wait(small_send, small_recv, small_parts, small_lands, reduced[0]))
    grads = split_vectors(where, vecs, dsh, bsh)
    loss = grads.pop("loss")[0, 0]
    g_are, g_aim, g_dt, g_bre, g_bim, dc_re, dc_im = s5_param_bwd(mats, lams, *s5_args)
    grads.update({"s5_a_re": g_are[None], "s5_a_im": g_aim[None], "s5_log_dt": g_dt[None], "s5_b_re": g_bre[None],
                  "s5_b_im": g_bim[None], "s5_c_re": dc_re[None], "s5_c_im": dc_im[None]})

    delta, new_m, new_v = {}, {}, {}

    def view(n, a):
        return a.reshape(1, -1) if a.ndim == 1 else jnp.swapaxes(a, -1, -2) if n in ("s5_b_re", "s5_b_im") else a

    sw, sg, sm, sv = ([view(n, t[n]) for n in SMALL_NAMES] for t in (w, grads, mom, var))
    for n, a, b, c_ in zip(SMALL_NAMES, *adamw_native("adamw_small", sw, sg, sm, sv)):
        delta[n], new_m[n], new_v[n] = (view(n, t) if t.ndim == 4 else t for t in (a, b, c_))

    reduced = share_wait(share_send, share_recv, reduced, entries, new_v["s5_c_re"])
    for n, g in zip(BIG_NAMES, reduced):
        grads[n] = g.reshape(w[n].shape)
    flat = lambda t: [t[n].reshape(-1, t[n].shape[-1]) for n in BIG_NAMES]
    for table, arrays in zip((grads, delta, new_m, new_v), adamw("adamw_big", flat(w), flat(grads), flat(mom), flat(var))):
        for n, a in zip(BIG_NAMES, arrays):
            table[n] = a.reshape(w[n].shape)

    out = [loss.reshape(()), grad_x[None]]
    for table in (grads, delta, new_m, new_v):
        out += [table[n].reshape(w[n].shape) for n in WEIGHT_ORDER]
    return tuple(out)
```

```python
import math

import jax
import jax.numpy as jnp
from jax import lax
from jax.experimental import pallas as pl
from jax.experimental.pallas import tpu as pltpu

F32 = jnp.float32
BF16 = jnp.bfloat16

D_MODEL = 1024
S5_GROUPS = 64
S5_GROUP = 16
S5_STATE = 64
N_KV = 4
N_Q = 16
HEAD_DIM = 64
BLOCK = 128
NORM_EPS = 1e-5
LAMBDA_RE_MAX = -1e-4
ADAM_LR, ADAM_B1, ADAM_B2, ADAM_EPS, ADAM_WD, ADAM_STEP = 0.001, 0.9, 0.999, 1e-08, 0.01, 10

VMEM_LIMIT_BYTES = 56 * 1024 * 1024
S5_CHUNK = 256
S5_BLOCKS = 4
MESH = pl.DeviceIdType.MESH


def _params(sem=None):
    return pltpu.CompilerParams(dimension_semantics=sem, vmem_limit_bytes=VMEM_LIMIT_BYTES)


def _sds(shape, dtype):
    return jax.ShapeDtypeStruct(shape, dtype)


def _rms_hat(xv):
    r = lax.rsqrt(jnp.mean(xv * xv, axis=-1, keepdims=True) + NORM_EPS)
    return xv * r, r


def mm_nn(name, a, w, col_offsets, n_out, epilogue, out_dtypes, extras=(), rowvecs=(), n_sums=0, tm=1024, tn=512):
    m, k = a.shape
    tm, tn = min(tm, m), min(tn, n_out)
    nw, ne, nr, no = len(col_offsets), len(extras), len(rowvecs), len(out_dtypes)

    def body(a_ref, *refs):
        w_refs, e_refs, r_refs = refs[:nw], refs[nw:nw + ne], refs[nw + ne:nw + ne + nr]
        o_refs, s_refs = refs[nw + ne + nr:nw + ne + nr + no], refs[nw + ne + nr + no:]
        av = a_ref[...]
        accs = [jnp.dot(av, w_ref[...], preferred_element_type=F32) for w_ref in w_refs]
        outs = epilogue(accs, [e[...] for e in e_refs], [r[...] for r in r_refs])
        for o_ref, o in zip(o_refs, outs[:no]):
            o_ref[...] = o.astype(o_ref.dtype)
        if n_sums:
            @pl.when(pl.program_id(1) == 0)
            def _():
                for s_ref in s_refs:
                    s_ref[...] = jnp.zeros_like(s_ref)

            for s_ref, val in zip(s_refs, outs[no:]):
                s_ref[...] += val

    def wspec(off):
        return pl.BlockSpec((k, tn), lambda j, i, off=off: (0, off // tn + j))

    def rspec(off):
        return pl.BlockSpec((1, tn), lambda j, i, off=off: (0, off // tn + j))

    tile = pl.BlockSpec((tm, tn), lambda j, i: (i, j))
    in_specs = ([pl.BlockSpec((tm, k), lambda j, i: (i, 0))] + [wspec(o) for o in col_offsets]
                + [tile] * ne + [rspec(o) for _, o in rowvecs])
    sem = ("parallel", "arbitrary") if n_sums else ("parallel", "parallel")
    return pl.pallas_call(
        body, grid=(n_out // tn, m // tm), in_specs=in_specs,
        out_specs=[tile] * no + [pl.BlockSpec((1, tn), lambda j, i: (0, j))] * n_sums,
        out_shape=[_sds((m, n_out), dt) for dt in out_dtypes] + [_sds((1, n_out), F32)] * n_sums, name=name,
        compiler_params=_params(sem))(a, *([w] * nw), *extras, *[r for r, _ in rowvecs])


def mm_nt(name, g, w, epilogue, out_dtypes, extras=(), rowvecs=(), n_sums=0, tm=512, tk=512):
    m, n = g.shape
    k = w.shape[0]
    tm, tk = min(tm, m), min(tk, k)
    ne, nr, no = len(extras), len(rowvecs), len(out_dtypes)

    def body(g_ref, w_ref, *refs):
        e_refs, r_refs, o_refs, s_refs = refs[:ne], refs[ne:ne + nr], refs[ne + nr:ne + nr + no], refs[ne + nr + no:]
        acc = lax.dot_general(g_ref[...], w_ref[...], (((1,), (1,)), ((), ())), preferred_element_type=F32)
        outs = epilogue(acc, [e[...] for e in e_refs], [r[...] for r in r_refs])
        for o_ref, o in zip(o_refs, outs[:no]):
            o_ref[...] = o.astype(o_ref.dtype)
        if n_sums:
            @pl.when(pl.program_id(0) == 0)
            def _():
                for s_ref in s_refs:
                    s_ref[...] = jnp.zeros_like(s_ref)

            for s_ref, val in zip(s_refs, outs[no:]):
                s_ref[...] += val

    tile = pl.BlockSpec((tm, tk), lambda i, j: (i, j))
    vec = pl.BlockSpec((1, tk), lambda i, j: (0, j))
    sem = ("arbitrary", "parallel") if n_sums else ("parallel", "parallel")
    return pl.pallas_call(
        body, grid=(m // tm, k // tk),
        in_specs=[pl.BlockSpec((tm, n), lambda i, j: (i, 0)), pl.BlockSpec((tk, n), lambda i, j: (j, 0))]
        + [tile] * ne + [vec] * nr,
        out_specs=[tile] * no + [vec] * n_sums,
        out_shape=[_sds((m, k), dt) for dt in out_dtypes] + [_sds((1, k), F32)] * n_sums, name=name,
        compiler_params=_params(sem))(g, w, *extras, *rowvecs)


def mm_tn(name, a, g, tk=512, tn=512):
    m, k = a.shape
    n = g.shape[1]
    tk, tn = min(tk, k), min(tn, n)

    def body(a_ref, g_ref, o_ref):
        acc = lax.dot_general(a_ref[...], g_ref[...], (((0,), (0,)), ((), ())), preferred_element_type=F32)
        o_ref[...] = acc.astype(o_ref.dtype)

    return pl.pallas_call(
        body, grid=(k // tk, n // tn),
        in_specs=[pl.BlockSpec((m, tk), lambda i, j: (0, i)), pl.BlockSpec((m, tn), lambda i, j: (0, j))],
        out_specs=pl.BlockSpec((tk, tn), lambda i, j: (i, j)), out_shape=_sds((k, n), BF16), name=name,
        compiler_params=_params(("parallel", "parallel")))(a, g)


def _row_mask(tc):
    row = lax.broadcasted_iota(jnp.int32, (8 * tc, 256), 0) % 8
    col = lax.broadcasted_iota(jnp.int32, (8 * tc, 256), 1) // 32
    return row == col


def _expand_rows(val, mask):
    tc, width = val.shape
    rep = jnp.broadcast_to(val[:, None, :], (tc, 8, width)).reshape(8 * tc, width)
    return jnp.where(mask, rep, 0.0).astype(BF16)


def _stage(ref, val):
    ref[0] = val[:, 0:128]
    ref[1] = val[:, 128:256]


def _gather_rows(src_ref, tc):
    halves = []
    for half in range(2):
        col = lax.broadcasted_iota(jnp.int32, (tc, 128), 1) // 32 + 4 * half
        out = jnp.zeros((tc, 128), F32)
        for s8 in range(4 * half, 4 * half + 4):
            out = jnp.where(col == s8, src_ref.at[half][pl.ds(s8, tc, stride=8), :], out)
        halves.append(out)
    return jnp.concatenate(halves, axis=1)


def _repeat(n, by, step, carry):
    def trip(i, c):
        for j in range(by):
            c = step(i * by + j, c)
        return c

    return lax.fori_loop(0, n // by, trip, carry)


def _gelu_and_slope(x):
    c = math.sqrt(2.0 / math.pi)
    t = jnp.tanh(c * (x + 0.044715 * x * x * x))
    return 0.5 * x * (1.0 + t), 0.5 * (1.0 + t) + 0.5 * x * (1.0 - t * t) * c * (1.0 + 3.0 * 0.044715 * x * x)


def s5_fwd(x, gain, d_skip, rb, rc, lam_r, lam_i):
    n_rows = x.shape[0]
    tc = min(S5_CHUNK, n_rows)
    nc = n_rows // tc

    def body(x_ref, g_ref, d_ref, rb_ref, rc_ref, lr_ref, li_ref, ge_ref, slope_ref, cs_ref, bux, yrows, carry):
        i = pl.program_id(0)
        u = _rms_hat(x_ref[...])[0] * g_ref[...]

        @pl.when(i == 0)
        def _():
            carry[...] = jnp.zeros_like(carry)

        cs_ref[0] = carry[...]
        mask = _row_mask(tc)
        for blk in range(S5_BLOCKS):
            lhs = _expand_rows(u[:, blk * 256:(blk + 1) * 256], mask)
            bux[blk] = jnp.dot(lhs, rb_ref[blk], preferred_element_type=F32)
        lam = [(lr_ref[blk], li_ref[blk]) for blk in range(S5_BLOCKS)]

        def step(t, c):
            r0 = pl.multiple_of(t * 8, 8)
            new = []
            for blk in range(S5_BLOCKS):
                xr, xi = c[2 * blk], c[2 * blk + 1]
                lr, li = lam[blk]
                nr = lr * xr - li * xi + bux[blk, pl.ds(r0, 8), 0:128]
                ni = lr * xi + li * xr + bux[blk, pl.ds(r0, 8), 128:256]
                bux[blk, pl.ds(r0, 8), 0:128] = nr
                bux[blk, pl.ds(r0, 8), 128:256] = ni
                new += [nr, ni]
            return tuple(new)

        c0 = []
        for blk in range(S5_BLOCKS):
            c0 += [carry[blk, :, 0:128], carry[blk, :, 128:256]]
        cn = _repeat(tc, 8, step, tuple(c0))
        for blk in range(S5_BLOCKS):
            carry[blk, :, 0:128] = cn[2 * blk]
            carry[blk, :, 128:256] = cn[2 * blk + 1]
        for blk in range(S5_BLOCKS):
            _stage(yrows, jnp.dot(bux[blk].astype(BF16), rc_ref[blk], preferred_element_type=F32))
            sl = slice(blk * 256, (blk + 1) * 256)
            ge, slope = _gelu_and_slope(_gather_rows(yrows, tc) + d_ref[:, sl] * u[:, sl])
            slope_ref[:, sl] = slope
            ge_ref[:, sl] = ge.astype(BF16)

    row = pl.BlockSpec((tc, D_MODEL), lambda i: (i, 0))
    vec = pl.BlockSpec((1, D_MODEL), lambda i: (0, 0))
    mat = pl.BlockSpec((S5_BLOCKS, 256, 256), lambda i: (0, 0, 0))
    lamspec = pl.BlockSpec((S5_BLOCKS, 8, 128), lambda i: (0, 0, 0))
    return pl.pallas_call(
        body, grid=(nc,),
        in_specs=[row, vec, vec, mat, mat, lamspec, lamspec],
        out_specs=[row, row, pl.BlockSpec((1, S5_BLOCKS, 8, 256), lambda i: (i, 0, 0, 0))],
        out_shape=[_sds((n_rows, D_MODEL), BF16), _sds((n_rows, D_MODEL), F32), _sds((nc, S5_BLOCKS, 8, 256), F32)],
        scratch_shapes=[pltpu.VMEM((S5_BLOCKS, 8 * tc, 256), F32), pltpu.VMEM((2, 8 * tc, 128), F32),
                        pltpu.VMEM((S5_BLOCKS, 8, 256), F32)],
        name="s5_fwd", compiler_params=_params(("arbitrary",)))(x, gain, d_skip, rb, rc, lam_r, lam_i)


def s5_bwd(x, gain, dy2, res, d_skip, cs, rb, rbt, rct, lam_r, lam_i):
    n_rows = x.shape[0]
    tc = min(S5_CHUNK, n_rows)
    nc = n_rows // tc

    def body(x_ref, g_ref, dy_ref, res_ref, d_ref, cs_ref, rb_ref, rbt_ref, rct_ref, lr_ref, li_ref,
             dx_ref, dd_ref, drb_ref, drc_ref, dlr_ref, dli_ref, dg_ref, tmp, du, lhsu, lhsd, xs, adj, acarry):
        i = pl.program_id(0)
        u = _rms_hat(x_ref[...])[0] * g_ref[...]

        @pl.when(i == 0)
        def _():
            acarry[...] = jnp.zeros_like(acarry)
            dd_ref[...] = jnp.zeros_like(dd_ref)
            drb_ref[...] = jnp.zeros_like(drb_ref)
            drc_ref[...] = jnp.zeros_like(drc_ref)
            dlr_ref[...] = jnp.zeros_like(dlr_ref)
            dli_ref[...] = jnp.zeros_like(dli_ref)
            dg_ref[...] = jnp.zeros_like(dg_ref)

        dd_ref[...] += jnp.sum(dy_ref[...] * u, axis=0, keepdims=True)
        mask = _row_mask(tc)
        for blk in range(S5_BLOCKS):
            sl = slice(blk * 256, (blk + 1) * 256)
            lhsu[blk] = _expand_rows(u[:, sl], mask)
            xs[blk, 0:8] = cs_ref[0, blk]
            xs[blk, 8:8 * tc + 8] = jnp.dot(lhsu[blk], rb_ref[blk], preferred_element_type=F32)
            lhsd[blk] = _expand_rows(dy_ref[:, sl], mask)
            adj[blk] = jnp.dot(lhsd[blk], rct_ref[blk], preferred_element_type=F32)
        lam = [(lr_ref[blk], li_ref[blk]) for blk in range(S5_BLOCKS)]

        def fstep(t, c):
            r0 = pl.multiple_of(t * 8 + 8, 8)
            new = []
            for blk in range(S5_BLOCKS):
                xr, xi = c[2 * blk], c[2 * blk + 1]
                lr, li = lam[blk]
                nr = lr * xr - li * xi + xs[blk, pl.ds(r0, 8), 0:128]
                ni = lr * xi + li * xr + xs[blk, pl.ds(r0, 8), 128:256]
                xs[blk, pl.ds(r0, 8), 0:128] = nr
                xs[blk, pl.ds(r0, 8), 128:256] = ni
                new += [nr, ni]
            return tuple(new)

        c0 = []
        for blk in range(S5_BLOCKS):
            c0 += [cs_ref[0, blk, :, 0:128], cs_ref[0, blk, :, 128:256]]
        _repeat(tc, 8, fstep, tuple(c0))

        def bstep(k, c):
            t = tc - 1 - k
            r0 = pl.multiple_of(t * 8, 8)
            new_a, new_g = [], []
            for blk in range(S5_BLOCKS):
                ar, ai = c[0][2 * blk], c[0][2 * blk + 1]
                glr, gli = c[1][2 * blk], c[1][2 * blk + 1]
                lr, li = lam[blk]
                nr = lr * ar + li * ai + adj[blk, pl.ds(r0, 8), 0:128]
                ni = lr * ai - li * ar + adj[blk, pl.ds(r0, 8), 128:256]
                adj[blk, pl.ds(r0, 8), 0:128] = nr
                adj[blk, pl.ds(r0, 8), 128:256] = ni
                pr, pi = xs[blk, pl.ds(r0, 8), 0:128], xs[blk, pl.ds(r0, 8), 128:256]
                new_a += [nr, ni]
                new_g += [glr + nr * pr + ni * pi, gli + ni * pr - nr * pi]
            return tuple(new_a), tuple(new_g)

        a0, g0 = [], []
        for blk in range(S5_BLOCKS):
            a0 += [acarry[blk, :, 0:128], acarry[blk, :, 128:256]]
            g0 += [dlr_ref[blk], dli_ref[blk]]
        an, gn = _repeat(tc, 4, bstep, (tuple(a0), tuple(g0)))
        for blk in range(S5_BLOCKS):
            acarry[blk, :, 0:128] = an[2 * blk]
            acarry[blk, :, 128:256] = an[2 * blk + 1]
            dlr_ref[blk] = gn[2 * blk]
            dli_ref[blk] = gn[2 * blk + 1]
        for blk in range(S5_BLOCKS):
            sl = slice(blk * 256, (blk + 1) * 256)
            ab = adj[blk].astype(BF16)
            _stage(tmp, jnp.dot(ab, rbt_ref[blk], preferred_element_type=F32))
            du[:, sl] = _gather_rows(tmp, tc) + d_ref[:, sl] * dy_ref[:, sl]
            drb_ref[blk] += lax.dot_general(lhsu[blk], ab, (((0,), (0,)), ((), ())), preferred_element_type=F32)
            drc_ref[blk] += lax.dot_general(lhsd[blk], xs[blk, 8:8 * tc + 8].astype(BF16), (((0,), (0,)), ((), ())),
                                            preferred_element_type=F32)
        xh, r = _rms_hat(x_ref[...])
        dg_ref[...] += jnp.sum(du[...] * xh, axis=0, keepdims=True)
        dxh = du[...] * g_ref[...]
        dx_ref[...] = r * (dxh - xh * jnp.mean(dxh * xh, axis=-1, keepdims=True)) + res_ref[...]

    rev = pl.BlockSpec((tc, D_MODEL), lambda i: (nc - 1 - i, 0))
    vec = pl.BlockSpec((1, D_MODEL), lambda i: (0, 0))
    mat = pl.BlockSpec((S5_BLOCKS, 256, 256), lambda i: (0, 0, 0))
    lamspec = pl.BlockSpec((S5_BLOCKS, 8, 128), lambda i: (0, 0, 0))
    big = pltpu.VMEM((S5_BLOCKS, 8 * tc, 256), F32)
    bigb = pltpu.VMEM((S5_BLOCKS, 8 * tc, 256), BF16)
    return pl.pallas_call(
        body, grid=(nc,),
        in_specs=[rev, vec, rev, rev, vec, pl.BlockSpec((1, S5_BLOCKS, 8, 256), lambda i: (nc - 1 - i, 0, 0, 0)),
                  mat, mat, mat, lamspec, lamspec],
        out_specs=[rev, vec, mat, mat, lamspec, lamspec, vec],
        out_shape=[_sds((n_rows, D_MODEL), F32), _sds((1, D_MODEL), F32), _sds((S5_BLOCKS, 256, 256), F32),
                   _sds((S5_BLOCKS, 256, 256), F32), _sds((S5_BLOCKS, 8, 128), F32), _sds((S5_BLOCKS, 8, 128), F32),
                   _sds((1, D_MODEL), F32)],
        scratch_shapes=[pltpu.VMEM((2, 8 * tc, 128), F32), pltpu.VMEM((tc, D_MODEL), F32), bigb, bigb,
                        pltpu.VMEM((S5_BLOCKS, 8 * tc + 8, 256), F32), big,
                        pltpu.VMEM((S5_BLOCKS, 8, 256), F32)],
        name="s5_bwd", compiler_params=_params(("arbitrary",)))(
            x, gain, dy2, res, d_skip, cs, rb, rbt, rct, lam_r, lam_i)


def _s5_views(a_re, a_im, log_dt, b_re, b_im):
    return a_re[:, None, :], a_im[:, None, :], log_dt[:, None, None], jnp.swapaxes(b_re, 1, 2), jnp.swapaxes(b_im, 1, 2)


def _s5_factors(a_re, a_im, log_dt):
    lr, li, dt = jnp.minimum(a_re, LAMBDA_RE_MAX), a_im, jnp.exp(log_dt)
    mag, ang = jnp.exp(lr * dt), li * dt
    lbr, lbi = mag * jnp.cos(ang), mag * jnp.sin(ang)
    den = lr * lr + li * li
    fr, fi = ((lbr - 1.0) * lr + lbi * li) / den, (lbi * lr - (lbr - 1.0) * li) / den
    return lr, li, dt, lbr, lbi, fr, fi, den


def s5_prep(a_re, a_im, log_dt, b_re, b_im, c_re, c_im):
    def body(ar_ref, ai_ref, t_ref, br_ref, bi_ref, cr_ref, ci_ref, rb_ref, rbt_ref, rc_ref, rct_ref, lr_ref, li_ref):
        _, _, _, lbr, lbi, fr, fi, _ = _s5_factors(ar_ref[...], ai_ref[...], t_ref[...])
        lr_ref[...] = lbr
        li_ref[...] = lbi
        bre = fr * br_ref[...] - fi * bi_ref[...]
        bim = fr * bi_ref[...] + fi * br_ref[...]
        even = (lax.broadcasted_iota(jnp.int32, (256, S5_STATE), 0) // S5_GROUP) % 2 == 0

        def assemble(re, im):
            re, im = re.reshape(256, S5_STATE), im.reshape(256, S5_STATE)
            return jnp.concatenate([jnp.where(even, re, 0.0), jnp.where(even, 0.0, re), jnp.where(even, im, 0.0),
                                    jnp.where(even, 0.0, im)], axis=1)

        for blk in range(S5_BLOCKS):
            sl = slice(16 * blk, 16 * blk + 16)
            rb = assemble(bre[sl], bim[sl])
            rct = assemble(cr_ref[sl], -ci_ref[sl])
            rb_ref[blk] = rb.astype(BF16)
            rbt_ref[blk] = rb.T.astype(BF16)
            rct_ref[blk] = rct.astype(BF16)
            rc_ref[blk] = rct.T.astype(BF16)

    vm = pl.BlockSpec(memory_space=pltpu.VMEM)
    mat = _sds((S5_BLOCKS, 256, 256), BF16)
    lam = _sds((S5_GROUPS, 1, S5_STATE), F32)
    rb, rbt, rc, rct, lam_r, lam_i = pl.pallas_call(
        body, in_specs=[vm] * 7, out_specs=[vm] * 6, out_shape=[mat, mat, mat, mat, lam, lam], name="s5_prep",
        compiler_params=_params())(*_s5_views(a_re, a_im, log_dt, b_re, b_im), c_re, c_im)
    return rb, rbt, rc, rct, lam_r.reshape(S5_BLOCKS, 8, 128), lam_i.reshape(S5_BLOCKS, 8, 128)


def s5_param_bwd(mats, lams, a_re, a_im, log_dt, b_re, b_im):
    def body(m_ref, glr_ref, gli_ref, ar_ref, ai_ref, t_ref, br_ref, bi_ref,
             dar_ref, dai_ref, dt_ref, dbr_ref, dbi_ref, dcr_ref, dci_ref):
        lr, li, dt, lbr, lbi, fr, fi, den = _s5_factors(ar_ref[...], ai_ref[...], t_ref[...])
        shape = (S5_GROUPS, S5_GROUP, S5_STATE)
        gbr, gbi = m_ref[0:1024, 0:64].reshape(shape), m_ref[0:1024, 64:128].reshape(shape)
        dcr_ref[...] = m_ref[1024:2048, 0:64].reshape(shape)
        dci_ref[...] = -m_ref[1024:2048, 64:128].reshape(shape)
        br, bi = br_ref[...], bi_ref[...]
        dbr_ref[...] = fr * gbr + fi * gbi
        dbi_ref[...] = fr * gbi - fi * gbr
        dfr = jnp.sum(gbr * br + gbi * bi, axis=1, keepdims=True)
        dfi = jnp.sum(gbi * br - gbr * bi, axis=1, keepdims=True)
        nr, ni = (dfr * lr - dfi * li) / den, (dfr * li + dfi * lr) / den
        qr, qi = (fr * lr + fi * li) / den, (fi * lr - fr * li) / den
        lam_r, lam_i = -(dfr * qr + dfi * qi), -(dfi * qr - dfr * qi)
        gr, gi = glr_ref[...] + nr, gli_ref[...] + ni
        zr, zi = gr * lbr + gi * lbi, gi * lbr - gr * lbi
        a = ar_ref[...]
        dar_ref[...] = (lam_r + zr * dt) * jnp.where(a < LAMBDA_RE_MAX, 1.0, jnp.where(a == LAMBDA_RE_MAX, 0.5, 0.0))
        dai_ref[...] = lam_i + zi * dt
        dt_ref[...] = jnp.sum(zr * lr + zi * li, axis=2, keepdims=True) * dt

    vm = pl.BlockSpec(memory_space=pltpu.VMEM)
    state = _sds((S5_GROUPS, 1, S5_STATE), F32)
    wide = _sds((S5_GROUPS, S5_GROUP, S5_STATE), F32)
    glr = lams[0:32].reshape(S5_GROUPS, 1, S5_STATE)
    gli = lams[32:64].reshape(S5_GROUPS, 1, S5_STATE)
    dar, dai, ddt, dbr, dbi, dcr, dci = pl.pallas_call(
        body, in_specs=[vm] * 8, out_specs=[vm] * 7,
        out_shape=[state, state, _sds((S5_GROUPS, 1, 1), F32), wide, wide, wide, wide], name="s5_param_bwd",
        compiler_params=_params())(mats, glr, gli, *_s5_views(a_re, a_im, log_dt, b_re, b_im))
    return (dar.reshape(S5_GROUPS, S5_STATE), dai.reshape(S5_GROUPS, S5_STATE), ddt.reshape(S5_GROUPS),
            jnp.swapaxes(dbr, 1, 2), jnp.swapaxes(dbi, 1, 2), dcr, dci)


def s5_compact(drb, drct, dlr, dli):
    def body(drb_ref, drct_ref, dlr_ref, dli_ref, o_ref, lam_ref):
        even = (lax.broadcasted_iota(jnp.int32, (256, 64), 0) // S5_GROUP) % 2 == 0
        for blk in range(S5_BLOCKS):
            for k, ref in enumerate((drb_ref, drct_ref)):
                m = ref[blk]
                re = jnp.where(even, m[:, 0:64], m[:, 64:128])
                im = jnp.where(even, m[:, 128:192], m[:, 192:256])
                o_ref[pl.ds(k * 1024 + blk * 256, 256), :] = jnp.concatenate([re, im], axis=1)
            lam_ref[pl.ds(blk * 8, 8), :] = dlr_ref[blk]
            lam_ref[pl.ds(32 + blk * 8, 8), :] = dli_ref[blk]

    vm = pl.BlockSpec(memory_space=pltpu.VMEM)
    return pl.pallas_call(body, in_specs=[vm] * 4, out_specs=[vm, vm], out_shape=[_sds((2048, 128), F32), _sds((64, 128), F32)],
                          name="s5_compact", compiler_params=_params())(drb, drct, dlr, dli)


NEG = -1e30


GROUP = N_Q // N_KV


def _attn_masks(n):
    qi = lax.broadcasted_iota(jnp.int32, (GROUP * BLOCK, BLOCK), 0) % BLOCK
    kj = lax.broadcasted_iota(jnp.int32, (GROUP * BLOCK, BLOCK), 1)
    return jnp.logical_and(kj > qi, n > 0), kj <= qi


def _stack_heads(ref, kh):
    return jnp.concatenate([ref[:, (GROUP * kh + g) * HEAD_DIM:(GROUP * kh + g + 1) * HEAD_DIM] for g in range(GROUP)], axis=0)


def _unstack_heads(val):
    return jnp.concatenate([val[g * BLOCK:(g + 1) * BLOCK] for g in range(GROUP)], axis=1)


def _sink_column(sink_ref, kh):
    grp = lax.broadcasted_iota(jnp.int32, (GROUP * BLOCK, 1), 0) // BLOCK
    col = jnp.zeros((GROUP * BLOCK, 1), F32)
    for g in range(GROUP):
        col = jnp.where(grp == g, sink_ref[GROUP * kh + g], col)
    return col, grp


def _attn_exp(q4, kp, kc, sink, mask_p, mask_c):
    scale = 1.0 / math.sqrt(HEAD_DIM)
    nt = (((1,), (1,)), ((), ()))
    sp = jnp.where(mask_p, lax.dot_general(q4, kp, nt, preferred_element_type=F32) * scale, NEG)
    sc = jnp.where(mask_c, lax.dot_general(q4, kc, nt, preferred_element_type=F32) * scale, NEG)
    m = jnp.maximum(jnp.maximum(jnp.max(sp, axis=-1, keepdims=True), jnp.max(sc, axis=-1, keepdims=True)), sink)
    pp = jnp.exp(sp - m)
    pc = jnp.exp(sc - m)
    ps = jnp.exp(sink - m)
    inv = 1.0 / (jnp.sum(pp, axis=-1, keepdims=True) + jnp.sum(pc, axis=-1, keepdims=True) + ps)
    return pp, pc, ps, inv


def attn_fwd(q, kv, sinks):
    n_rows = q.shape[0]
    nb = n_rows // BLOCK

    def body(sink_ref, q_ref, kvp_ref, kvc_ref, o_ref):
        n = pl.program_id(0)
        mask_p, mask_c = _attn_masks(n)
        outs = []
        for kh in range(N_KV):
            ks, vs = slice(kh * HEAD_DIM, (kh + 1) * HEAD_DIM), slice((N_KV + kh) * HEAD_DIM, (N_KV + kh + 1) * HEAD_DIM)
            sink, _ = _sink_column(sink_ref, kh)
            pp, pc, _, inv = _attn_exp(_stack_heads(q_ref, kh), kvp_ref[:, ks], kvc_ref[:, ks], sink, mask_p, mask_c)
            o4 = (jnp.dot(pp.astype(BF16), kvp_ref[:, vs], preferred_element_type=F32)
                  + jnp.dot(pc.astype(BF16), kvc_ref[:, vs], preferred_element_type=F32)) * inv
            outs.append(_unstack_heads(o4))
        o_ref[...] = jnp.concatenate(outs, axis=1).astype(BF16)

    kvw = 2 * N_KV * HEAD_DIM
    return pl.pallas_call(
        body, grid=(nb,),
        in_specs=[pl.BlockSpec(memory_space=pltpu.SMEM), pl.BlockSpec((BLOCK, D_MODEL), lambda n: (n, 0)),
                  pl.BlockSpec((BLOCK, kvw), lambda n: (jnp.maximum(n - 1, 0), 0)), pl.BlockSpec((BLOCK, kvw), lambda n: (n, 0))],
        out_specs=pl.BlockSpec((BLOCK, D_MODEL), lambda n: (n, 0)), out_shape=_sds((n_rows, D_MODEL), BF16),
        name="attn_fwd", compiler_params=_params(("parallel",)))(sinks, q, kv, kv)


def attn_bwd(q, kv, do, sinks):
    n_rows = q.shape[0]
    nb = n_rows // BLOCK
    kvw = 2 * N_KV * HEAD_DIM
    tn = (((0,), (0,)), ((), ()))
    nt = (((1,), (1,)), ((), ()))
    scale = 1.0 / math.sqrt(HEAD_DIM)

    def body(sink_ref, q_ref, kvp_ref, kvc_ref, do_ref, dq_ref, dbq_ref, dprev_ref, dcur_ref, dsink_ref):
        n = pl.program_id(0)
        mask_p, mask_c = _attn_masks(n)
        lane = lax.broadcasted_iota(jnp.int32, (1, D_MODEL), 1)
        dqs, dsink = [], jnp.zeros((1, D_MODEL), F32)
        dkp, dkc, dvp, dvc = [], [], [], []
        for kh in range(N_KV):
            ks, vs = slice(kh * HEAD_DIM, (kh + 1) * HEAD_DIM), slice((N_KV + kh) * HEAD_DIM, (N_KV + kh + 1) * HEAD_DIM)
            q4, do4 = _stack_heads(q_ref, kh), _stack_heads(do_ref, kh)
            kp, kc, vp, vc = kvp_ref[:, ks], kvc_ref[:, ks], kvp_ref[:, vs], kvc_ref[:, vs]
            sink, grp = _sink_column(sink_ref, kh)
            pp, pc, ps, inv = _attn_exp(q4, kp, kc, sink, mask_p, mask_c)
            pp, pc = pp * inv, pc * inv
            dpp = lax.dot_general(do4, vp, nt, preferred_element_type=F32)
            dpc = lax.dot_general(do4, vc, nt, preferred_element_type=F32)
            delta = jnp.sum(pp * dpp, axis=-1, keepdims=True) + jnp.sum(pc * dpc, axis=-1, keepdims=True)
            dsp = (pp * (dpp - delta) * scale).astype(BF16)
            dsc = (pc * (dpc - delta) * scale).astype(BF16)
            dsk = ps * inv * delta
            for g in range(GROUP):
                dsink = dsink + jnp.where(lane == GROUP * kh + g, -jnp.sum(jnp.where(grp == g, dsk, 0.0)), 0.0)
            dqs.append(_unstack_heads(jnp.dot(dsp, kp, preferred_element_type=F32)
                                      + jnp.dot(dsc, kc, preferred_element_type=F32)))
            dkp.append(lax.dot_general(dsp, q4, tn, preferred_element_type=F32))
            dkc.append(lax.dot_general(dsc, q4, tn, preferred_element_type=F32))
            dvp.append(lax.dot_general(pp.astype(BF16), do4, tn, preferred_element_type=F32))
            dvc.append(lax.dot_general(pc.astype(BF16), do4, tn, preferred_element_type=F32))
        dq = jnp.concatenate(dqs, axis=1)
        dq_ref[...] = dq.astype(BF16)
        dprev_ref[0] = jnp.concatenate(dkp + dvp, axis=1)
        dcur_ref[0] = jnp.concatenate(dkc + dvc, axis=1)

        @pl.when(n == 0)
        def _():
            dbq_ref[...] = jnp.zeros_like(dbq_ref)
            dsink_ref[...] = jnp.zeros_like(dsink_ref)

        dbq_ref[...] += jnp.sum(dq, axis=0, keepdims=True)
        dsink_ref[...] += dsink

    blk = pl.BlockSpec((BLOCK, D_MODEL), lambda n: (n, 0))
    part = pl.BlockSpec((1, BLOCK, kvw), lambda n: (n, 0, 0))
    return pl.pallas_call(
        body, grid=(nb,),
        in_specs=[pl.BlockSpec(memory_space=pltpu.SMEM), blk,
                  pl.BlockSpec((BLOCK, kvw), lambda n: (jnp.maximum(n - 1, 0), 0)), pl.BlockSpec((BLOCK, kvw), lambda n: (n, 0)), blk],
        out_specs=[blk, pl.BlockSpec((1, D_MODEL), lambda n: (0, 0)), part, part, pl.BlockSpec((1, D_MODEL), lambda n: (0, 0))],
        out_shape=[_sds((n_rows, D_MODEL), BF16), _sds((1, D_MODEL), F32), _sds((nb, BLOCK, kvw), F32),
                   _sds((nb, BLOCK, kvw), F32), _sds((1, D_MODEL), F32)],
        name="attn_bwd", compiler_params=_params(("arbitrary",)))(sinks, q, kv, kv, do)


def kv_combine(dprev, dcur):
    nb, _, kvw = dprev.shape

    def body(dcur_ref, dprev_ref, dkv_ref, db_ref):
        total = jnp.zeros((1, kvw), F32)
        for m in range(nb):
            dkv = dcur_ref[m] + dprev_ref[m + 1] if m + 1 < nb else dcur_ref[m]
            dkv_ref[m * BLOCK:(m + 1) * BLOCK, :] = dkv.astype(BF16)
            total = total + jnp.sum(dkv, axis=0, keepdims=True)
        db_ref[...] = jnp.concatenate([total, jnp.zeros((1, D_MODEL - kvw), F32)], axis=1)

    vm = pl.BlockSpec(memory_space=pltpu.VMEM)
    return pl.pallas_call(body, in_specs=[vm, vm], out_specs=[vm, vm],
                          out_shape=[_sds((nb * BLOCK, kvw), BF16), _sds((1, D_MODEL), F32)], name="kv_combine",
                          compiler_params=_params())(dcur, dprev)


def glu_bwd(dout, val, gate, tm=256):
    n_rows, d = dout.shape

    def body(do_ref, v_ref, g_ref, dz_ref, db_ref):
        i = pl.program_id(0)
        sg = jax.nn.sigmoid(g_ref[...])
        dval = do_ref[...] * sg
        dgate = do_ref[...] * v_ref[...] * sg * (1.0 - sg)
        dz_ref[...] = jnp.concatenate([dval, dgate], axis=1).astype(BF16)

        @pl.when(i == 0)
        def _():
            db_ref[...] = jnp.zeros_like(db_ref)

        db_ref[0:1, :] += jnp.sum(dval, axis=0, keepdims=True)
        db_ref[1:2, :] += jnp.sum(dgate, axis=0, keepdims=True)

    row = pl.BlockSpec((tm, d), lambda i: (i, 0))
    return pl.pallas_call(
        body, grid=(n_rows // tm,), in_specs=[row, row, row],
        out_specs=[pl.BlockSpec((tm, 2 * d), lambda i: (i, 0)), pl.BlockSpec((2, d), lambda i: (0, 0))],
        out_shape=[_sds((n_rows, 2 * d), BF16), _sds((2, d), F32)],
        name="glu_bwd", compiler_params=_params(("arbitrary",)))(dout, val, gate)


def _adam_update(w, g, m, v):
    nm = ADAM_B1 * m + (1.0 - ADAM_B1) * g
    nv = ADAM_B2 * v + (1.0 - ADAM_B2) * (g * g)
    m_hat = nm / (1.0 - ADAM_B1 ** ADAM_STEP)
    v_hat = nv / (1.0 - ADAM_B2 ** ADAM_STEP)
    return -ADAM_LR * (m_hat / (jnp.sqrt(v_hat) + ADAM_EPS) + ADAM_WD * w), nm, nv


def adamw(name, ws, gs, ms, vs, steps=8):
    n = len(ws)

    def body(*refs):
        for k in range(n):
            w_ref, g_ref, m_ref, v_ref = (refs[j * n + k] for j in range(4))
            go_ref, d_ref, nm_ref, nv_ref = (refs[(4 + j) * n + k] for j in range(4))
            gv = g_ref[...]
            go_ref[...] = gv
            d_ref[...], nm_ref[...], nv_ref[...] = _adam_update(w_ref[...], gv, m_ref[...], v_ref[...])

    specs = [pl.BlockSpec((w.shape[0] // steps, w.shape[1]), lambda i: (i, 0)) for w in ws]
    shapes = [_sds(w.shape, F32) for w in ws]
    out = pl.pallas_call(
        body, grid=(steps,), in_specs=specs * 4, out_specs=specs * 4, out_shape=shapes * 4, name=name,
        compiler_params=_params(("parallel",)))(*ws, *gs, *ms, *vs)
    return [list(out[j * n:(j + 1) * n]) for j in range(4)]


def adamw_native(name, ws, gs, ms, vs):
    n = len(ws)

    def body(*refs):
        w_refs, g_refs, m_refs, v_refs = refs[:n], refs[n:2 * n], refs[2 * n:3 * n], refs[3 * n:4 * n]
        d_refs, nm_refs, nv_refs = refs[4 * n:5 * n], refs[5 * n:6 * n], refs[6 * n:7 * n]
        for k in range(n):
            dl, nm, nv = _adam_update(w_refs[k][...], g_refs[k][...], m_refs[k][...], v_refs[k][...])
            d_refs[k][...] = dl
            nm_refs[k][...] = nm
            nv_refs[k][...] = nv

    vm = pl.BlockSpec(memory_space=pltpu.VMEM)
    shapes = [_sds(w.shape, F32) for w in ws]
    out = pl.pallas_call(body, in_specs=[vm] * (4 * n), out_specs=[vm] * (3 * n), out_shape=shapes * 3, name=name,
                         compiler_params=_params())(*ws, *gs, *ms, *vs)
    return list(out[:n]), list(out[n:2 * n]), list(out[2 * n:])


VEC_ROWS = {"norm_mix": 0, "norm_mlp": 2, "norm_kv": 4, "norm_final": 5, "s5_d": 6, "b_q": 7, "b_o": 8, "s5_b_glu": 9,
            "b_kv": 11, "sinks": 12, "loss": 13}


def split_vectors(where, vecs, d_shard, glu_shard):
    kvw = 2 * N_KV * HEAD_DIM
    shapes = {"norm_mix": (2, D_MODEL), "norm_mlp": (2, D_MODEL), "norm_kv": (1, D_MODEL), "norm_final": (1, D_MODEL),
              "s5_d": (1, d_shard), "b_q": (1, D_MODEL), "b_o": (1, D_MODEL), "s5_b_glu": (1, glu_shard), "b_kv": (1, kvw),
              "sinks": (1, N_Q), "loss": (1, 128)}
    names = list(shapes)

    def body(where_ref, v_ref, *o_refs):
        chip = where_ref[1]
        for name, o_ref in zip(names, o_refs):
            r0, (r, n) = VEC_ROWS[name], shapes[name]
            if name == "s5_d":
                g = jnp.zeros((1, n), F32)
                for j in range(4):
                    g = jnp.where(chip == j, v_ref[r0:r0 + 1, j * n:(j + 1) * n], g)
            elif name == "s5_b_glu":
                g = jnp.zeros((1, n), F32)
                for j in range(4):
                    row, col = r0 + (j * n) // D_MODEL, (j * n) % D_MODEL
                    g = jnp.where(chip == j, v_ref[row:row + 1, col:col + n], g)
            else:
                g = v_ref[r0:r0 + r, 0:n]
            o_ref[...] = g

    vm = pl.BlockSpec(memory_space=pltpu.VMEM)
    out = pl.pallas_call(body, in_specs=[pl.BlockSpec(memory_space=pltpu.SMEM), vm], out_specs=[vm] * len(names),
                         out_shape=[_sds(shapes[n], F32) for n in names], name="split_vectors",
                         compiler_params=_params())(where, vecs)
    return dict(zip(names, out))


def _position():
    x, y, c = lax.axis_index("x"), lax.axis_index("y"), lax.axis_index("c")
    others = [(1 - x, y), (x, 1 - y), (1 - x, 1 - y)]
    return x, y, c, others


def _window(ref, kind, chip, half, shard_shape):
    if kind == "slab":
        return ref.at[chip]
    r, n = shard_shape
    if kind == "col":
        return ref.at[pl.ds(pl.multiple_of(half * (r // 2), 16), r // 2), pl.ds(pl.multiple_of(chip * n, 128), n)]
    return ref.at[pl.ds(pl.multiple_of(chip * r, 16), r), pl.ds(pl.multiple_of(half * (n // 2), 128), n // 2)]


def _half(ref, kind, half, shape):
    r, n = shape
    if kind == "col":
        return ref.at[pl.ds(pl.multiple_of(half * (r // 2), 16), r // 2), :]
    return ref.at[:, pl.ds(pl.multiple_of(half * (n // 2), 128), n // 2)]


def swap_start(name, grads, kinds, carry):
    nt = len(grads)
    shapes = [tuple(g.shape) for g in grads]
    lands = [lax.empty(sh, BF16) for sh in shapes]
    given, given_specs, token_type, write = _hand_through(carry)
    n_in = 2 * nt + len(given)

    def body(*refs):
        in_refs, land_refs = refs[:nt], refs[nt:2 * nt]
        send_sems, recv_sems, token = refs[n_in], refs[n_in + 1], refs[-1]
        x, y, c, _ = _position()
        for t in range(nt):
            pltpu.make_async_remote_copy(
                src_ref=_half(in_refs[t], kinds[t], 1 - c, shapes[t]), dst_ref=_half(land_refs[t], kinds[t], 1 - c, shapes[t]),
                send_sem=send_sems.at[t], recv_sem=recv_sems.at[t], device_id=(x, y, 1 - c), device_id_type=MESH).start()
        write(token, refs[:n_in])

    sems = pltpu.SemaphoreType.DMA((nt,))
    both = list(grads) + lands
    out = pl.pallas_call(
        body, name=name, in_specs=[HBM_SPEC] * (2 * nt) + given_specs,
        out_specs=(SEM_SPEC, SEM_SPEC, *[HBM_SPEC] * (2 * nt), pl.BlockSpec(memory_space=pltpu.VMEM)),
        out_shape=(sems, sems, *[pltpu.HBM(a.shape, a.dtype) for a in both], token_type),
        input_output_aliases={t: 2 + t for t in range(2 * nt)}, compiler_params=_split_params(),
    )(*[_in_hbm(a) for a in both], *given)
    return out[0], out[1], list(out[2:2 + nt]), list(out[2 + nt:2 + 2 * nt]), out[-1]


def swap_wait(name, send_sems, recv_sems, grads, lands, kinds, after):
    nt = len(grads)
    shapes = [tuple(g.shape) for g in grads]

    def body(*refs):
        in_refs, land_refs = refs[:nt], refs[nt:2 * nt]
        send_ref, recv_ref = refs[2 * nt], refs[2 * nt + 1]
        x, y, c, _ = _position()
        for t in range(nt):
            cp = pltpu.make_async_remote_copy(
                src_ref=_half(in_refs[t], kinds[t], 1 - c, shapes[t]), dst_ref=_half(land_refs[t], kinds[t], c, shapes[t]),
                send_sem=send_ref.at[t], recv_sem=recv_ref.at[t], device_id=(x, y, 1 - c), device_id_type=MESH)
            cp.wait_send()
            cp.wait_recv()

    both = list(grads) + list(lands)
    out = pl.pallas_call(
        body, name=name, in_specs=[HBM_SPEC] * (2 * nt) + [SEM_SPEC, SEM_SPEC, HBM_SPEC], out_specs=[HBM_SPEC] * (2 * nt),
        out_shape=[pltpu.HBM(a.shape, a.dtype) for a in both], input_output_aliases={t: t for t in range(2 * nt)},
        compiler_params=_split_params())(*both, send_sems, recv_sems, _in_hbm(after))
    return list(out[:nt]), list(out[nt:])


def _half_spec(kind, shape, tiles):
    r, n = shape
    if kind == "col":
        tn = n // tiles
        return pl.BlockSpec((r // 2, tn), lambda i, s: (s[0], i))
    tm = r // tiles
    return pl.BlockSpec((tm, n // 2), lambda i, s: (i, s[0]))


def add_halves(name, mine, landed, kinds, where, tiles=2):
    nt = len(mine)
    shapes = [tuple(a.shape) for a in mine]

    def compact(t):
        r, n = shapes[t]
        if kinds[t] == "col":
            return (r // 2, n), pl.BlockSpec((r // 2, n // tiles), lambda i, s: (0, i))
        return (r, n // 2), pl.BlockSpec((r // tiles, n // 2), lambda i, s: (i, 0))

    def body(s_ref, *refs):
        for a_ref, b_ref, o_ref in zip(refs[:nt], refs[nt:2 * nt], refs[2 * nt:]):
            o_ref[...] = (a_ref[...].astype(F32) + b_ref[...].astype(F32)).astype(BF16)

    specs = [_half_spec(kinds[t], shapes[t], tiles) for t in range(nt)]
    return pl.pallas_call(
        body, grid_spec=pltpu.PrefetchScalarGridSpec(num_scalar_prefetch=1, grid=(tiles,), in_specs=specs + specs,
                                                     out_specs=[compact(t)[1] for t in range(nt)]),
        out_shape=[_sds(compact(t)[0], BF16) for t in range(nt)], name=name,
        compiler_params=_params(("parallel",)))(where, *mine, *landed)


def sum_shards(name, parts, landed, kinds, shard_shapes, where, layers, n_layers, intos, tiles=2):
    nt = len(parts)
    in_specs, out_specs = [], []
    for t in range(nt):
        (r, n), layer = shard_shapes[t], layers[t]
        if kinds[t] == "col":
            tm, width = r // 2 // tiles, n
            own = pl.BlockSpec((tm, n), lambda i, s: (i, s[1]))
            out = pl.BlockSpec((None, tm, n), lambda i, s, layer=layer: (layer, s[0] * tiles + i, 0))
        else:
            tm, width = r // tiles, n // 2
            own = pl.BlockSpec((tm, n // 2), lambda i, s: (s[1] * tiles + i, 0))
            out = pl.BlockSpec((None, tm, n // 2), lambda i, s, layer=layer: (layer, i, s[0]))
        in_specs += [own, pl.BlockSpec((3, tm, width), lambda i, s: (0, i, 0))]
        out_specs.append(out)
    args, aliases = [where] + [a for pair in zip(parts, landed) for a in pair], {}
    for t in range(nt):
        if intos[t] is not None:
            aliases[len(args)] = t
            in_specs.append(pl.BlockSpec(memory_space=pl.ANY))
            args.append(intos[t])

    def body(s_ref, *refs):
        for t in range(nt):
            a_ref, l_ref, o_ref = refs[2 * t], refs[2 * t + 1], refs[len(in_specs) + t]
            o_ref[...] = ((a_ref[...].astype(F32) + l_ref[0].astype(F32)) + l_ref[1].astype(F32)) + l_ref[2].astype(F32)

    return pl.pallas_call(
        body, grid_spec=pltpu.PrefetchScalarGridSpec(num_scalar_prefetch=1, grid=(tiles,), in_specs=in_specs,
                                                     out_specs=out_specs),
        out_shape=[_sds((n_layers[t],) + tuple(shard_shapes[t]), F32) for t in range(nt)], input_output_aliases=aliases,
        name=name, compiler_params=_params(("parallel",)))(*args)


def share_start(arrays, entries, carry):
    na, nt = len(arrays), len(entries)
    given, given_specs, token_type, write = _hand_through(carry)
    n_in = na + len(given)

    def body(*refs):
        in_refs, send_sems, recv_sems, token = refs[:na], refs[n_in], refs[n_in + 1], refs[-1]
        x, y, c, _ = _position()
        for t, (a, layer, kind) in enumerate(entries):
            mine = _half(in_refs[a].at[layer], kind, c, tuple(arrays[a].shape[1:]))
            pltpu.make_async_remote_copy(
                src_ref=mine, dst_ref=mine, send_sem=send_sems.at[t], recv_sem=recv_sems.at[t],
                device_id=(x, y, 1 - c), device_id_type=MESH).start()
        write(token, refs[:n_in])

    sems = pltpu.SemaphoreType.DMA((nt,))
    out = pl.pallas_call(
        body, name="share_start", in_specs=[HBM_SPEC] * na + given_specs,
        out_specs=(SEM_SPEC, SEM_SPEC, *[HBM_SPEC] * na, pl.BlockSpec(memory_space=pltpu.VMEM)),
        out_shape=(sems, sems, *[pltpu.HBM(a.shape, a.dtype) for a in arrays], token_type),
        input_output_aliases={t: 2 + t for t in range(na)}, compiler_params=_split_params(),
    )(*[_in_hbm(a) for a in arrays], *given)
    return out[0], out[1], list(out[2:2 + na]), out[-1]


def share_wait(send_sems, recv_sems, arrays, entries, after):
    na = len(arrays)

    def body(*refs):
        in_refs, send_ref, recv_ref = refs[:na], refs[na], refs[na + 1]
        x, y, c, _ = _position()
        for t, (a, layer, kind) in enumerate(entries):
            shape = tuple(arrays[a].shape[1:])
            cp = pltpu.make_async_remote_copy(
                src_ref=_half(in_refs[a].at[layer], kind, c, shape), dst_ref=_half(in_refs[a].at[layer], kind, 1 - c, shape),
                send_sem=send_ref.at[t], recv_sem=recv_ref.at[t], device_id=(x, y, 1 - c), device_id_type=MESH)
            cp.wait_send()
            cp.wait_recv()

    return list(pl.pallas_call(
        body, name="share_wait", in_specs=[HBM_SPEC] * na + [SEM_SPEC, SEM_SPEC, HBM_SPEC], out_specs=[HBM_SPEC] * na,
        out_shape=[pltpu.HBM(a.shape, a.dtype) for a in arrays], input_output_aliases={t: t for t in range(na)},
        compiler_params=_split_params())(*arrays, send_sems, recv_sems, _in_hbm(after)))


HBM_SPEC = pl.BlockSpec(memory_space=pltpu.HBM)
SEM_SPEC = pl.BlockSpec(memory_space=pltpu.SEMAPHORE)
ANY_SPEC = pl.BlockSpec(memory_space=pl.ANY)


def _split_params():
    return pltpu.CompilerParams(has_side_effects=pltpu.SideEffectType.DATAFLOW_SIDE_EFFECTING,
                                vmem_limit_bytes=VMEM_LIMIT_BYTES)


def _in_hbm(a):
    return pltpu.with_memory_space_constraint(a, pltpu.HBM)


def cast_place(arrays, entries, where, tiles=2):
    in_specs, out_specs, fulls = [], [], []
    for a, layer, kind in entries:
        _, r, n = arrays[a].shape
        tm = r // tiles
        in_specs.append(pl.BlockSpec((None, tm, n), lambda i, s, layer=layer: (layer, i, 0)))
        if kind == "col":
            fulls.append((r, 4 * n))
            out_specs.append(pl.BlockSpec((tm, n), lambda i, s: (i, s[1])))
        else:
            fulls.append((4 * r, n))
            out_specs.append(pl.BlockSpec((tm, n), lambda i, s: (s[1] * tiles + i, 0)))
    nt = len(entries)

    def body(s_ref, *refs):
        for w_ref, o_ref in zip(refs[:nt], refs[nt:]):
            o_ref[...] = w_ref[...].astype(BF16)

    return pl.pallas_call(
        body, grid_spec=pltpu.PrefetchScalarGridSpec(num_scalar_prefetch=1, grid=(tiles,), in_specs=in_specs,
                                                     out_specs=out_specs),
        out_shape=[_sds(f, BF16) for f in fulls], name="cast_place",
        compiler_params=_params(("parallel",)))(where, *[arrays[a] for a, _, _ in entries])


def _hand_through(carry):
    given = [] if isinstance(carry, tuple) else [carry]

    def write(token, ins):
        token[...] = ins[-1][...] if given else jnp.zeros_like(token)

    return (given, [pl.BlockSpec(memory_space=pltpu.VMEM)] * len(given),
            _sds(carry if isinstance(carry, tuple) else carry.shape, F32), write)


def gather_start(fulls, kinds, shard_shapes, carry):
    nt = len(fulls)
    given, given_specs, token_type, write = _hand_through(carry)
    n_in = nt + len(given)

    def body(*refs):
        full_refs = refs[:nt]
        send_sems, recv_sems, token = refs[n_in], refs[n_in + 1], refs[-1]
        x, y, c, others = _position()
        for t in range(nt):
            mine = _window(full_refs[t], kinds[t], 2 * x + y, c, shard_shapes[t])
            for j, (ox, oy) in enumerate(others):
                pltpu.make_async_remote_copy(
                    src_ref=mine, dst_ref=mine, send_sem=send_sems.at[3 * t + j], recv_sem=recv_sems.at[3 * t + j],
                    device_id=(ox, oy, c), device_id_type=MESH).start()
        write(token, refs[:n_in])

    sems = pltpu.SemaphoreType.DMA((3 * nt,))
    out = pl.pallas_call(
        body, name="gather_start", in_specs=[HBM_SPEC] * nt + given_specs,
        out_specs=(SEM_SPEC, SEM_SPEC, *[HBM_SPEC] * nt, pl.BlockSpec(memory_space=pltpu.VMEM)),
        out_shape=(sems, sems, *[pltpu.HBM(f.shape, f.dtype) for f in fulls], token_type),
        input_output_aliases={t: 2 + t for t in range(nt)}, compiler_params=_split_params(),
    )(*[_in_hbm(f) for f in fulls], *given)
    return out[0], out[1], list(out[2:2 + nt]), out[-1]


def gather_wait(name, send_sems, recv_sems, fulls, kinds, shard_shapes, after, first):
    nt = len(fulls)
    extra = [] if after is None else [_in_hbm(after)]

    def body(*refs):
        full_refs, send_ref, recv_ref = refs[:nt], refs[nt], refs[nt + 1]
        x, y, c, others = _position()
        for t in range(nt):
            mine = _window(full_refs[t], kinds[t], 2 * x + y, c, shard_shapes[t])
            for j, (ox, oy) in enumerate(others):
                cp = pltpu.make_async_remote_copy(
                    src_ref=mine, dst_ref=_window(full_refs[t], kinds[t], 2 * ox + oy, c, shard_shapes[t]),
                    send_sem=send_ref.at[3 * (first + t) + j], recv_sem=recv_ref.at[3 * (first + t) + j],
                    device_id=(ox, oy, c), device_id_type=MESH)
                cp.wait_send()
                cp.wait_recv()

    out = pl.pallas_call(
        body, name=name, in_specs=[HBM_SPEC] * nt + [SEM_SPEC, SEM_SPEC] + [HBM_SPEC] * len(extra),
        out_specs=[HBM_SPEC] * nt, out_shape=[pltpu.HBM(f.shape, f.dtype) for f in fulls],
        input_output_aliases={t: t for t in range(nt)}, compiler_params=_split_params())(*fulls, send_sems, recv_sems, *extra)
    return list(out)


def forward_halves(name, fulls, kinds, shard_shapes):
    nt = len(fulls)

    def body(*refs):
        out_refs = refs[nt:2 * nt]
        send_sems, recv_sems = refs[2 * nt:]
        x, y, c, others = _position()
        cps = []
        for t in range(nt):
            for j, (ox, oy) in enumerate(others):
                landed = _window(out_refs[t], kinds[t], 2 * ox + oy, c, shard_shapes[t])
                cp = pltpu.make_async_remote_copy(
                    src_ref=landed, dst_ref=landed, send_sem=send_sems.at[3 * t + j], recv_sem=recv_sems.at[3 * t + j],
                    device_id=(x, y, 1 - c), device_id_type=MESH)
                cp.start()
                cps.append(cp)
        for t in range(nt):
            for j, (ox, oy) in enumerate(others):
                got = _window(out_refs[t], kinds[t], 2 * ox + oy, 1 - c, shard_shapes[t])
                pltpu.make_async_remote_copy(
                    src_ref=got, dst_ref=got, send_sem=send_sems.at[3 * t + j], recv_sem=recv_sems.at[3 * t + j],
                    device_id=(x, y, 1 - c), device_id_type=MESH).wait_recv()
        for cp in cps:
            cp.wait_send()

    out = pl.pallas_call(
        body, in_specs=[ANY_SPEC] * nt, out_specs=[ANY_SPEC] * nt, out_shape=[_sds(f.shape, f.dtype) for f in fulls],
        input_output_aliases={t: t for t in range(nt)},
        scratch_shapes=[pltpu.SemaphoreType.DMA((3 * nt,)), pltpu.SemaphoreType.DMA((3 * nt,))],
        name=name, compiler_params=_params())(*fulls)
    return list(out)


def forward_start(name, send_sems, recv_sems, fulls, kinds, shard_shapes, after, first, carry, passing=()):
    nt, n_pass = len(fulls), len(passing)
    given, given_specs, token_type, write = _hand_through(carry)
    n_in = nt + 3 + n_pass + len(given)

    def body(*refs):
        full_refs, ici_send, ici_recv = refs[:nt], refs[nt], refs[nt + 1]
        send_ref, recv_ref, token = refs[n_in], refs[n_in + 1], refs[-1]
        x, y, c, others = _position()
        for t in range(nt):
            mine = _window(full_refs[t], kinds[t], 2 * x + y, c, shard_shapes[t])
            for j, (ox, oy) in enumerate(others):
                landed = _window(full_refs[t], kinds[t], 2 * ox + oy, c, shard_shapes[t])
                cp = pltpu.make_async_remote_copy(
                    src_ref=mine, dst_ref=landed, send_sem=ici_send.at[3 * (first + t) + j],
                    recv_sem=ici_recv.at[3 * (first + t) + j], device_id=(ox, oy, c), device_id_type=MESH)
                cp.wait_send()
                cp.wait_recv()
                pltpu.make_async_remote_copy(
                    src_ref=landed, dst_ref=landed, send_sem=send_ref.at[3 * t + j], recv_sem=recv_ref.at[3 * t + j],
                    device_id=(x, y, 1 - c), device_id_type=MESH).start()
        write(token, refs[:n_in])

    sems = pltpu.SemaphoreType.DMA((3 * nt,))
    out = pl.pallas_call(
        body, name=name, in_specs=[HBM_SPEC] * nt + [SEM_SPEC, SEM_SPEC, HBM_SPEC] + [HBM_SPEC] * n_pass + given_specs,
        out_specs=(SEM_SPEC, SEM_SPEC, *[HBM_SPEC] * (nt + n_pass), pl.BlockSpec(memory_space=pltpu.VMEM)),
        out_shape=(sems, sems, *[pltpu.HBM(f.shape, f.dtype) for f in [*fulls, *passing]], token_type),
        input_output_aliases={**{t: 2 + t for t in range(nt)}, **{nt + 3 + t: 2 + nt + t for t in range(n_pass)}},
        compiler_params=_split_params(),
    )(*fulls, send_sems, recv_sems, _in_hbm(after), *passing, *given)
    return out[0], out[1], list(out[2:2 + nt]), list(out[2 + nt:2 + nt + n_pass]), out[-1]


def forward_wait(name, send_sems, recv_sems, fulls, kinds, shard_shapes, after):
    nt = len(fulls)

    def body(*refs):
        full_refs, send_ref, recv_ref = refs[:nt], refs[nt], refs[nt + 1]
        x, y, c, others = _position()
        for t in range(nt):
            for j, (ox, oy) in enumerate(others):
                cp = pltpu.make_async_remote_copy(
                    src_ref=_window(full_refs[t], kinds[t], 2 * ox + oy, c, shard_shapes[t]),
                    dst_ref=_window(full_refs[t], kinds[t], 2 * ox + oy, 1 - c, shard_shapes[t]),
                    send_sem=send_ref.at[3 * t + j], recv_sem=recv_ref.at[3 * t + j],
                    device_id=(x, y, 1 - c), device_id_type=MESH)
                cp.wait_send()
                cp.wait_recv()

    return list(pl.pallas_call(
        body, name=name, in_specs=[HBM_SPEC] * nt + [SEM_SPEC, SEM_SPEC, HBM_SPEC], out_specs=[HBM_SPEC] * nt,
        out_shape=[pltpu.HBM(f.shape, f.dtype) for f in fulls], input_output_aliases={t: t for t in range(nt)},
        compiler_params=_split_params())(*fulls, send_sems, recv_sems, _in_hbm(after)))


def _piece(ref, kind, chip, shard_shape):
    r, n = shard_shape
    if kind == "col":
        return ref.at[:, pl.ds(pl.multiple_of(chip * n, 128), n)]
    return ref.at[pl.ds(pl.multiple_of(chip * r, 16), r), :]


def _piece_shape(kind, shard_shape):
    r, n = shard_shape
    return (r // 2, n) if kind == "col" else (r, n // 2)


def exchange_start(name, parts, kinds, shard_shapes, carry):
    nt = len(parts)
    lands = [lax.empty((3,) + _piece_shape(kinds[t], shard_shapes[t]), BF16) for t in range(nt)]
    given, given_specs, token_type, write = _hand_through(carry)
    n_in = 2 * nt + len(given)

    def body(*refs):
        part_refs, land_refs = refs[:nt], refs[nt:2 * nt]
        send_sems, recv_sems, token = refs[n_in], refs[n_in + 1], refs[-1]
        x, y, c, others = _position()
        for t in range(nt):
            for j, (ox, oy) in enumerate(others):
                pltpu.make_async_remote_copy(
                    src_ref=_piece(part_refs[t], kinds[t], 2 * ox + oy, shard_shapes[t]), dst_ref=land_refs[t].at[j],
                    send_sem=send_sems.at[3 * t + j], recv_sem=recv_sems.at[3 * t + j],
                    device_id=(ox, oy, c), device_id_type=MESH).start()
        write(token, refs[:n_in])

    sems = pltpu.SemaphoreType.DMA((3 * nt,))
    both = list(parts) + lands
    out = pl.pallas_call(
        body, name=name, in_specs=[HBM_SPEC] * (2 * nt) + given_specs,
        out_specs=(SEM_SPEC, SEM_SPEC, *[HBM_SPEC] * (2 * nt), pl.BlockSpec(memory_space=pltpu.VMEM)),
        out_shape=(sems, sems, *[pltpu.HBM(a.shape, a.dtype) for a in both], token_type),
        input_output_aliases={t: 2 + t for t in range(2 * nt)}, compiler_params=_split_params(),
    )(*[_in_hbm(a) for a in both], *given)
    return out[0], out[1], list(out[2:2 + nt]), list(out[2 + nt:2 + 2 * nt]), out[-1]


def exchange_wait(name, send_sems, recv_sems, parts, lands, kinds, shard_shapes, after):
    nt = len(parts)

    def body(*refs):
        part_refs, land_refs = refs[:nt], refs[nt:2 * nt]
        send_ref, recv_ref = refs[2 * nt], refs[2 * nt + 1]
        x, y, c, others = _position()
        for t in range(nt):
            for j, (ox, oy) in enumerate(others):
                cp = pltpu.make_async_remote_copy(
                    src_ref=_piece(part_refs[t], kinds[t], 2 * ox + oy, shard_shapes[t]), dst_ref=land_refs[t].at[j],
                    send_sem=send_ref.at[3 * t + j], recv_sem=recv_ref.at[3 * t + j],
                    device_id=(ox, oy, c), device_id_type=MESH)
                cp.wait_send()
                cp.wait_recv()

    both = list(parts) + list(lands)
    out = pl.pallas_call(
        body, name=name, in_specs=[HBM_SPEC] * (2 * nt) + [SEM_SPEC, SEM_SPEC, HBM_SPEC], out_specs=[HBM_SPEC] * (2 * nt),
        out_shape=[pltpu.HBM(a.shape, a.dtype) for a in both], input_output_aliases={t: t for t in range(2 * nt)},
        compiler_params=_split_params())(*both, send_sems, recv_sems, _in_hbm(after))
    return list(out[:nt]), list(out[nt:])


def reduce_swap(bufs, wire):
    n = len(bufs)
    halves = [b.shape[0] // 2 for b in bufs]

    def body(*refs):
        in_refs, out_refs, txs, got = refs[:n], refs[n:2 * n], refs[2 * n:3 * n], refs[3 * n:4 * n]
        send_sems, recv_sems = refs[4 * n:]
        x, y, c, _ = _position()
        cps = []
        for k in range(n):
            txs[k][...] = in_refs[k][pl.ds(pl.multiple_of((1 - c) * halves[k], 8), halves[k]), :].astype(wire[k])
            cp = pltpu.make_async_remote_copy(src_ref=txs[k], dst_ref=got[k], send_sem=send_sems.at[k],
                                              recv_sem=recv_sems.at[k], device_id=(x, y, 1 - c), device_id_type=MESH)
            cp.start()
            cps.append(cp)
        for k, cp in enumerate(cps):
            cp.wait()
            own = in_refs[k][pl.ds(pl.multiple_of(c * halves[k], 8), halves[k]), :]
            out_refs[k][...] = (own.astype(wire[k]).astype(F32) + got[k][...].astype(F32)).astype(wire[k])

    vm = pl.BlockSpec(memory_space=pltpu.VMEM)
    parts = [((h, b.shape[1]), w) for h, b, w in zip(halves, bufs, wire)]
    return list(pl.pallas_call(
        body, name="reduce_swap", in_specs=[vm] * n, out_specs=[vm] * n, out_shape=[_sds(sh, w) for sh, w in parts],
        scratch_shapes=[pltpu.VMEM(sh, w) for sh, w in parts] * 2 + [pltpu.SemaphoreType.DMA((n,))] * 2,
        compiler_params=_params())(*bufs))


def reduce_start(parts):
    n = len(parts)
    lands = [lax.empty((4,) + tuple(p.shape), p.dtype) for p in parts]

    def body(*refs):
        part_refs, land_refs, send_sems, recv_sems = refs[:n], refs[n:2 * n], refs[2 * n], refs[2 * n + 1]
        x, y, c, others = _position()
        for k in range(n):
            for j, (ox, oy) in enumerate(others):
                pltpu.make_async_remote_copy(
                    src_ref=part_refs[k], dst_ref=land_refs[k].at[2 * x + y], send_sem=send_sems.at[3 * k + j],
                    recv_sem=recv_sems.at[3 * k + j], device_id=(ox, oy, c), device_id_type=MESH).start()

    sems = pltpu.SemaphoreType.DMA((3 * n,))
    both = list(parts) + lands
    out = pl.pallas_call(
        body, name="reduce_start", in_specs=[HBM_SPEC] * (2 * n), out_specs=(SEM_SPEC, SEM_SPEC, *[HBM_SPEC] * (2 * n)),
        out_shape=(sems, sems, *[pltpu.HBM(a.shape, a.dtype) for a in both]),
        input_output_aliases={k: 2 + k for k in range(2 * n)}, compiler_params=_split_params(),
    )(*[_in_hbm(a) for a in both])
    return out[0], out[1], list(out[2:2 + n]), list(out[2 + n:])


def reduce_wait(send_sems, recv_sems, parts, lands, after):
    n = len(parts)

    def body(*refs):
        part_refs, land_refs, send_ref, recv_ref = refs[:n], refs[n:2 * n], refs[2 * n], refs[2 * n + 1]
        x, y, c, others = _position()
        for k in range(n):
            for j, (ox, oy) in enumerate(others):
                cp = pltpu.make_async_remote_copy(
                    src_ref=part_refs[k], dst_ref=land_refs[k].at[2 * ox + oy], send_sem=send_ref.at[3 * k + j],
                    recv_sem=recv_ref.at[3 * k + j], device_id=(ox, oy, c), device_id_type=MESH)
                cp.wait_send()
                cp.wait_recv()

    both = list(parts) + list(lands)
    out = pl.pallas_call(
        body, name="reduce_wait", in_specs=[HBM_SPEC] * (2 * n) + [SEM_SPEC, SEM_SPEC, HBM_SPEC],
        out_specs=[HBM_SPEC] * (2 * n), out_shape=[pltpu.HBM(a.shape, a.dtype) for a in both],
        input_output_aliases={k: k for k in range(2 * n)}, compiler_params=_split_params(),
    )(*both, send_sems, recv_sems, _in_hbm(after))
    return list(out[:n]), list(out[n:])


def reduce_share(parts, lands):
    n = len(parts)
    halves = [p.shape[0] for p in parts]

    def body(*refs):
        part_refs, land_refs, out_refs = refs[:n], refs[n:2 * n], refs[2 * n:3 * n]
        send_sems, recv_sems = refs[3 * n:]
        x, y, c, _ = _position()
        chip = 2 * x + y
        cps = []
        for k in range(n):
            mine = pl.ds(pl.multiple_of(c * halves[k], 8), halves[k])
            own = part_refs[k][...].astype(F32)
            total = jnp.where(chip == 0, own, land_refs[k][0].astype(F32))
            for entry in range(1, 4):
                total = total + jnp.where(chip == entry, own, land_refs[k][entry].astype(F32))
            out_refs[k][mine, :] = total
            cp = pltpu.make_async_remote_copy(
                src_ref=out_refs[k].at[mine], dst_ref=out_refs[k].at[mine], send_sem=send_sems.at[k],
                recv_sem=recv_sems.at[k], device_id=(x, y, 1 - c), device_id_type=MESH)
            cp.start()
            cps.append(cp)
        for cp in cps:
            cp.wait()

    vm = pl.BlockSpec(memory_space=pltpu.VMEM)
    return list(pl.pallas_call(
        body, name="reduce_share", in_specs=[vm] * (2 * n), out_specs=[vm] * n,
        out_shape=[_sds((2 * p.shape[0], p.shape[1]), F32) for p in parts],
        scratch_shapes=[pltpu.SemaphoreType.DMA((n,))] * 2, compiler_params=_params())(*parts, *lands))


def _local_step(x, target, small, need, ahead, emit_swap, emit_exchange):
    d = D_MODEL
    full = {}

    def handed(vec, token):
        return vec if token is None else token

    def token_rows(token):
        return [] if token is None else [token]

    def plus(acc, rows):
        return acc + rows[0] if rows else acc

    rb16, rbt16, rc16, rct16, lr_t, li_t = small["s5_operands"]
    ge, ge_slope, cs = s5_fwd(x, small["norm_mix0"], small["s5_d"], rb16, rc16, lr_t, li_t)
    full.update(need("glu", ge))

    def norm_rows(h, gains):
        xh, _ = _rms_hat(h)
        return [xh * g for g in gains]

    def glu_epilogue(accs, e, r):
        v, gt = accs[0] + r[0], accs[1] + r[1]
        h = e[0] + v * jax.nn.sigmoid(gt)
        return [h, v, gt] + norm_rows(h, r[2:])

    gain_mlp0 = handed(small["norm_mlp0"], ahead("mlp_in0", full["w_glu"], small["norm_mlp0"]))
    h1, val, gate, n1 = mm_nn(
        "glu", ge, full["w_glu"], [0, d], d, glu_epilogue, [F32, F32, F32, BF16], extras=[x],
        rowvecs=[(small["s5_b_glu"], 0), (small["s5_b_glu"], d), (gain_mlp0, 0)], tm=512, tn=d)

    def mlp_fwd(tag, h, n, w_in, get_w_out, next_gains, head=None):
        def in_epilogue(accs, e, rv):
            pos = jnp.maximum(accs[0], 0.0)
            return [pos * pos, 2.0 * pos]

        r, slope = mm_nn("mlp_in" + tag, n, w_in, [0], w_in.shape[1], in_epilogue, [BF16, BF16], tm=2048)
        w_out = get_w_out(r)

        def epilogue(accs, e, rv):
            h_out = e[0] + accs[0]
            return [h_out] + norm_rows(h_out, rv)

        if head is not None:
            return head(r, w_out, h), (n, r, slope)
        outs = mm_nn("mlp_out" + tag, r, w_out, [0], d, epilogue, [F32] + [BF16] * len(next_gains), extras=[h],
                     rowvecs=[(g, 0) for g in next_gains], tm=512, tn=d)
        return outs[0], outs[1:], (n, r, slope)

    full.update(need("mlp_in0", h1))

    def w_out0(after):
        full.update(need("mlp_out0", after))
        return full["w_out0"]

    h2, (nkv, n2), mlp0 = mlp_fwd("0", h1, n1, full["w_in0"], w_out0, [small["norm_kv"], small["norm_mix1"]])

    full.update(need("attn", h2))
    kvw = 2 * N_KV * HEAD_DIM
    (kv,) = mm_nn("kv_proj", nkv, full["w_kv"], [0], kvw, lambda accs, e, r: [accs[0] + r[0]], [BF16],
                  rowvecs=[(small["b_kv"], 0)], tm=2048)
    (q,) = mm_nn("q_proj", n2, full["w_q"], [0], d, lambda accs, e, r: [accs[0] + r[0]], [BF16],
                 rowvecs=[(small["b_q"], 0)], tm=2048)
    sinks = small["sinks"].reshape(N_Q)
    o = attn_fwd(q, kv, sinks)
    def o_epilogue(accs, e, r):
        h_out = e[0] + accs[0] + r[0]
        return [h_out] + norm_rows(h_out, r[1:])

    bias_o = handed(small["b_o"], ahead("mlp_in1", o, small["b_o"]))
    h3, n3 = mm_nn("o_proj", o, full["w_o"], [0], d, o_epilogue, [F32, BF16], extras=[h2],
                   rowvecs=[(bias_o, 0), (small["norm_mlp1"], 0)], tm=512, tn=d)
    full.update(need("mlp_in1", h3, then="mlp_out1"))

    def w_out1(after):
        full.update(need("mlp_out1", after))
        return full["w_out1"]

    def loss_head(r, w_out, h):
        def epilogue(accs, e, rv):
            xh, rr = _rms_hat(e[0] + accs[0])
            err = xh * rv[0] - e[1]
            dy = err * (1.0 / d)
            dxh = dy * rv[0]
            dx = rr * (dxh - xh * jnp.mean(dxh * xh, axis=-1, keepdims=True))
            loss = jnp.full((1, d), 0.5 * jnp.sum(jnp.mean(err * err, axis=-1, keepdims=True)), F32)
            return [dx, dx, loss, jnp.sum(dy * xh, axis=0, keepdims=True)]

        return mm_nn("mlp_out1", r, w_out, [0], d, epilogue, [F32, BF16], extras=[h, target],
                     rowvecs=[(small["norm_final"], 0)], n_sums=2, tm=512, tn=d)

    (dh, dhb, loss_tile, dg_final), mlp1 = mlp_fwd("1", h3, n3, full["w_in1"], w_out1, [], head=loss_head)

    grads_small, grads_full = {"norm_final": dg_final}, {}
    ident = lambda acc, e, r: [plus(acc, r)]
    layer1 = ["w_out1", "w_in1", "w_o", "w_q", "w_kv"]
    layer0 = ["w_out0", "w_in0", "w_glu"]

    def norm_bwd_rows(x_rows, res, dys, gains):
        xh, r = _rms_hat(x_rows)
        dxh = sum(dy * g for dy, g in zip(dys, gains))
        dx = r * (dxh - xh * jnp.mean(dxh * xh, axis=-1, keepdims=True)) + res
        return dx, [jnp.sum(dy * xh, axis=0, keepdims=True) for dy in dys]

    def mlp_bwd(tag, dh, dhb, h_in, gain, w_in, w_out, saved, token=None):
        n, r, slope = saved
        grads_full["w_out" + tag] = mm_tn("dw_out" + tag, r, dhb, tn=1024)
        (da,) = mm_nt("mlp_da" + tag, dhb, w_out, lambda acc, e, rv: [plus(acc * e[0].astype(F32), rv)], [BF16],
                      extras=[slope], rowvecs=token_rows(token), tm=2048)
        grads_full["w_in" + tag] = mm_tn("dw_in" + tag, n, da, tn=1024)

        def epilogue(acc, e, rv):
            dx, dgs = norm_bwd_rows(e[0], e[1], [acc], rv)
            return [dx, dx, jnp.sum(dx, axis=0, keepdims=True)] + dgs

        dx, dxb, colsum, dg = mm_nt("mlp_dn" + tag, da, w_in, epilogue, [F32, BF16], extras=[h_in, dh], rowvecs=[gain],
                                    n_sums=2, tm=512, tk=d)
        grads_small["norm_mlp" + tag] = dg
        return dx, dxb, colsum

    dh3, dh3b, colsum3 = mlp_bwd("1", dh, dhb, h3, small["norm_mlp1"], full["w_in1"], full["w_out1"], mlp1)
    grads_small["b_o"] = colsum3
    grads_full["w_o"] = mm_tn("dw_o", o, dh3b, tn=1024)
    (do,) = mm_nt("attn_do", dh3b, full["w_o"], ident, [BF16], tm=2048)
    dq, dbq, dprev, dcur, dsink = attn_bwd(q, kv, do, sinks)
    dkv, dbkv = kv_combine(dprev, dcur)
    grads_small["b_q"], grads_small["b_kv"], grads_small["sinks"] = dbq, dbkv, dsink
    grads_full["w_q"] = mm_tn("dw_q", n2, dq, tn=1024)
    grads_full["w_kv"] = mm_tn("dw_kv", nkv, dkv, tk=1024)
    token = emit_swap("layer1", {n: grads_full[n] for n in layer1}, (1, d))
    (dnkv,) = mm_nt("kv_dn", dkv, full["w_kv"], ident, [F32], rowvecs=token_rows(token), tm=2048, tk=1024)

    def attn_dn_epilogue(acc, e, rv):
        dx, dgs = norm_bwd_rows(e[0], e[1], [acc, e[2]], rv)
        return [dx, dx] + dgs

    dh2, dh2b, dg_mix1, dg_kv = mm_nt("attn_dn", dq, full["w_q"], attn_dn_epilogue, [F32, BF16], extras=[h2, dh3, dnkv],
                                      rowvecs=[small["norm_mix1"], small["norm_kv"]], n_sums=2, tm=512, tk=d)
    grads_small["norm_mix1"], grads_small["norm_kv"] = dg_mix1, dg_kv
    token = emit_exchange("layer1", dh2b, (1, full["w_out0"].shape[0]))
    dh1, _, _ = mlp_bwd("0", dh2, dh2b, h1, small["norm_mlp0"], full["w_in0"], full["w_out0"], mlp0, token)

    dz, db_glu = glu_bwd(dh1, val, gate)
    grads_small["s5_b_glu"] = db_glu
    grads_full["w_glu"] = mm_tn("dw_glu", ge, dz, tn=1024)
    token = emit_swap("layer0", {n: grads_full[n] for n in layer0}, (1, d))
    (dy2,) = mm_nt("glu_dy", dz, full["w_glu"], lambda acc, e, rv: [plus(acc, rv) * e[0]], [F32], extras=[ge_slope],
                   rowvecs=token_rows(token), tm=512, tk=1024)
    d_skip = handed(small["s5_d"], emit_exchange("layer0", dy2, small["s5_d"]))
    grad_x, dd, drb, drc, dlr, dli, dg_mix0 = s5_bwd(x, small["norm_mix0"], dy2, dh1, d_skip, cs, rb16, rbt16, rct16, lr_t, li_t)
    grads_small["s5_d"] = dd
    grads_small["s5_mats"] = (drb, drc, dlr, dli)
    grads_small["norm_mix0"] = dg_mix0
    return loss_tile, grad_x, grads_small


SMALL_NAMES = ["norm_mix", "norm_mlp", "norm_kv", "norm_final", "s5_a_re", "s5_a_im", "s5_log_dt", "s5_b_re", "s5_b_im",
               "s5_c_re", "s5_c_im", "s5_d", "s5_b_glu", "b_kv", "b_q", "sinks", "b_o"]
BIG_NAMES = ["s5_w_glu", "w_kv", "w_q", "w_o", "w_mlp_in", "w_mlp_out"]
WEIGHT_ORDER = ["norm_mix", "norm_mlp", "norm_kv", "norm_final", "s5_a_re", "s5_a_im", "s5_log_dt", "s5_b_re", "s5_b_im",
                "s5_c_re", "s5_c_im", "s5_d", "s5_w_glu", "s5_b_glu", "w_kv", "b_kv", "w_q", "b_q", "sinks", "w_o", "b_o",
                "w_mlp_in", "w_mlp_out"]


def kernel(x, norm_mix, norm_mlp, norm_kv, norm_final, s5_a_re, s5_a_im, s5_log_dt, s5_b_re, s5_b_im, s5_c_re, s5_c_im, s5_d, s5_w_glu, s5_b_glu, w_kv, b_kv, w_q, b_q, sinks, w_o, b_o, w_mlp_in, w_mlp_out, loss_target, m_norm_mix, m_norm_mlp, m_norm_kv, m_norm_final, m_s5_a_re, m_s5_a_im, m_s5_log_dt, m_s5_b_re, m_s5_b_im, m_s5_c_re, m_s5_c_im, m_s5_d, m_s5_w_glu, m_s5_b_glu, m_w_kv, m_b_kv, m_w_q, m_b_q, m_sinks, m_w_o, m_b_o, m_w_mlp_in, m_w_mlp_out, v_norm_mix, v_norm_mlp, v_norm_kv, v_norm_final, v_s5_a_re, v_s5_a_im, v_s5_log_dt, v_s5_b_re, v_s5_b_im, v_s5_c_re, v_s5_c_im, v_s5_d, v_s5_w_glu, v_s5_b_glu, v_w_kv, v_b_kv, v_w_q, v_b_q, v_sinks, v_w_o, v_b_o, v_w_mlp_in, v_w_mlp_out):
    env = dict(locals())
    w = {n: env[n] for n in WEIGHT_ORDER}
    mom = {n: env["m_" + n] for n in WEIGHT_ORDER}
    var = {n: env["v_" + n] for n in WEIGHT_ORDER}
    d = D_MODEL
    xi, yi, ci = lax.axis_index("x"), lax.axis_index("y"), lax.axis_index("c")
    chip = 2 * xi + yi
    where = jnp.stack([ci, chip]).astype(jnp.int32)

    dsh, bsh = s5_d.shape[1], s5_b_glu.shape[1]
    packed = jnp.concatenate([s5_d.reshape(-1, 128), s5_b_glu.reshape(-1, 128)])
    n_d, n_b = dsh // 128, bsh // 128
    slab = lax.dynamic_update_slice(jnp.zeros((4, 8, 128), F32), jnp.pad(packed, ((0, 8 - n_d - n_b), (0, 0)))[None],
                                    (chip, 0, 0))

    big = [s5_w_glu, w_kv[None], w_q, w_o, w_mlp_in, w_mlp_out]
    entries = [(0, 0, "col"), (1, 0, "row"), (2, 0, "row"), (3, 0, "row"), (4, 0, "col"), (4, 1, "col"),
               (5, 0, "row"), (5, 1, "row")]
    names = ["w_glu", "w_kv", "w_q", "w_o", "w_in0", "w_in1", "w_out0", "w_out1"]
    kinds = dict(zip(names, [k for _, _, k in entries]))
    shard_shapes = dict(zip(names, [tuple(big[a].shape[1:]) for a, _, _ in entries]))

    placed_w = dict(zip(names, cast_place(big, entries, where)))
    placed_w["vectors"], kinds["vectors"], shard_shapes["vectors"] = slab, "slab", None
    gather_groups = {"glu": ["w_glu"], "mlp_in0": ["w_in0"], "mlp_out0": ["w_out0"], "attn": ["w_kv", "w_q", "w_o"],
                     "mlp_in1": ["w_in1"], "mlp_out1": ["w_out1"]}
    order = ["vectors"] + [n for members in gather_groups.values() for n in members]
    send, recv, thru, log_dt = gather_start([placed_w[n] for n in order], [kinds[n] for n in order],
                                            [shard_shapes[n] for n in order], s5_log_dt)
    started = dict(zip(order, thru))
    (gathered_rows,) = gather_wait("gather_wait_vectors", send, recv, [started["vectors"]], ["slab"], [None], None, 0)
    d_full = gathered_rows[:, 0:n_d].reshape(1, -1)
    bglu_full = gathered_rows[:, n_d:n_d + n_b].reshape(1, -1)

    forwarding = {}

    def ahead(group, after, carry, passing=()):
        members = gather_groups[group]
        ks, shapes = [kinds[n] for n in members], [shard_shapes[n] for n in members]
        d2d_send, d2d_recv, landed, passed, tok = forward_start(
            "forward_start_" + group, send, recv, [started[n] for n in members], ks, shapes, after,
            order.index(members[0]), carry, passing)
        forwarding[group] = (d2d_send, d2d_recv, landed)
        return passed if passing else tok

    def need(group, after, then=None):
        members = gather_groups[group]
        ks, shapes = [kinds[n] for n in members], [shard_shapes[n] for n in members]
        if group in forwarding:
            arrays = forward_wait("forward_wait_" + group, *forwarding[group], ks, shapes, after)
        else:
            landed = gather_wait("gather_wait_" + group, send, recv, [started[n] for n in members], ks, shapes, after,
                                 order.index(members[0]))
            arrays = forward_halves("forward_halves_" + group, landed, ks, shapes)
        if then is not None:
            arrays = ahead(then, after, (8, 128), arrays)
        return dict(zip(members, arrays))

    swapping, exchanging = {}, {}

    def emit_swap(group, partial, carry):
        members = list(partial)
        send, recv, mine, lands, tok = swap_start("swap_start_" + group, [partial[n] for n in members],
                                                  [kinds[n] for n in members], carry)
        swapping[group] = (members, send, recv, mine, lands)
        return tok

    def emit_exchange(group, after, carry):
        members, send, recv, mine, lands = swapping[group]
        ks, shapes = [kinds[n] for n in members], [shard_shapes[n] for n in members]
        mine, landed = swap_wait("swap_wait_" + group, send, recv, mine, lands, ks, after)
        sums = add_halves("add_halves_" + group, mine, landed, ks, where)
        send, recv, parts, lands, tok = exchange_start("exchange_start_" + group, sums, ks, shapes, carry)
        exchanging[group] = (members, send, recv, parts, lands)
        return tok

    s5_args = (s5_a_re[0], s5_a_im[0], log_dt[0], s5_b_re[0], s5_b_im[0])
    small = {
        "norm_mix0": norm_mix[0:1], "norm_mix1": norm_mix[1:2], "norm_mlp0": norm_mlp[0:1], "norm_mlp1": norm_mlp[1:2],
        "norm_kv": norm_kv.reshape(1, d), "norm_final": norm_final.reshape(1, d), "s5_operands": s5_prep(*s5_args, s5_c_re[0], s5_c_im[0]),
        "s5_d": d_full, "s5_b_glu": bglu_full,
        "b_kv": b_kv.reshape(1, -1), "b_q": b_q, "sinks": sinks, "b_o": b_o,
    }
    loss_row, grad_x, gs = _local_step(x[0], loss_target[0], small, need, ahead, emit_swap, emit_exchange)

    mats, lams = s5_compact(*gs["s5_mats"])
    rows = [gs["norm_mix0"], gs["norm_mix1"], gs["norm_mlp0"], gs["norm_mlp1"], gs["norm_kv"], gs["norm_final"], gs["s5_d"],
            gs["b_q"], gs["b_o"], gs["s5_b_glu"], gs["b_kv"], gs["sinks"], loss_row, jnp.zeros((2, d), F32)]
    small_send, small_recv, small_parts, small_lands = reduce_start(
        reduce_swap([jnp.concatenate(rows, axis=0), lams, mats], [F32, F32, BF16]))

    reduced = [None] * len(big)
    where_of = dict(zip(names, entries))
    for group in ("layer1", "layer0"):
        members, send, recv, parts, lands = exchanging[group]
        ks, shapes = [kinds[n] for n in members], [shard_shapes[n] for n in members]
        parts, lands = exchange_wait("exchange_wait_" + group, send, recv, parts, lands, ks, shapes, small_lands[-1])
        targets = [where_of[n][0] for n in members]
        sums = sum_shards("sum_shards_" + group, parts, lands, ks, shapes, where, [where_of[n][1] for n in members],
                          [big[a].shape[0] for a in targets], [reduced[a] for a in targets])
        for a, arr in zip(targets, sums):
            reduced[a] = arr
    share_send, share_recv, reduced, _ = share_start(reduced, entries, (8, 128))

    vecs, lams, mats = reduce_share(*reduce_wait(small_send, small_recv, small_parts, small_lands, reduced[0]))
    grads = split_vectors(where, vecs, dsh, bsh)
    loss = grads.pop("loss")[0, 0]
    g_are, g_aim, g_dt, g_bre, g_bim, dc_re, dc_im = s5_param_bwd(mats, lams, *s5_args)
    grads.update({"s5_a_re": g_are[None], "s5_a_im": g_aim[None], "s5_log_dt": g_dt[None], "s5_b_re": g_bre[None],
                  "s5_b_im": g_bim[None], "s5_c_re": dc_re[None], "s5_c_im": dc_im[None]})

    delta, new_m, new_v = {}, {}, {}

    def view(n, a):
        return a.reshape(1, -1) if a.ndim == 1 else jnp.swapaxes(a, -1, -2) if n in ("s5_b_re", "s5_b_im") else a

    sw, sg, sm, sv = ([view(n, t[n]) for n in SMALL_NAMES] for t in (w, grads, mom, var))
    for n, a, b, c_ in zip(SMALL_NAMES, *adamw_native("adamw_small", sw, sg, sm, sv)):
        delta[n], new_m[n], new_v[n] = (view(n, t) if t.ndim == 4 else t for t in (a, b, c_))

    reduced = share_wait(share_send, share_recv, reduced, entries, new_v["s5_c_re"])
    for n, g in zip(BIG_NAMES, reduced):
        grads[n] = g.reshape(w[n].shape)
    flat = lambda t: [t[n].reshape(-1, t[n].shape[-1]) for n in BIG_NAMES]
    for table, arrays in zip((grads, delta, new_m, new_v), adamw("adamw_big", flat(w), flat(grads), flat(mom), flat(var))):
        for n, a in zip(BIG_NAMES, arrays):
            table[n] = a.reshape(w[n].shape)

    out = [loss.reshape(()), grad_x[None]]
    for table in (grads, delta, new_m, new_v):
        out += [table[n].reshape(w[n].shape) for n in WEIGHT_ORDER]
    return tuple(out)
```

```python
import math

import jax
import jax.numpy as jnp
from jax import lax
from jax.experimental import pallas as pl
from jax.experimental.pallas import tpu as pltpu

F32 = jnp.float32
BF16 = jnp.bfloat16

D_MODEL = 1024
S5_GROUPS = 64
S5_GROUP = 16
S5_STATE = 64
N_KV = 4
N_Q = 16
HEAD_DIM = 64
BLOCK = 128
NORM_EPS = 1e-5
LAMBDA_RE_MAX = -1e-4
ADAM_LR, ADAM_B1, ADAM_B2, ADAM_EPS, ADAM_WD, ADAM_STEP = 0.001, 0.9, 0.999, 1e-08, 0.01, 10

VMEM_LIMIT_BYTES = 56 * 1024 * 1024
S5_CHUNK = 256
S5_BLOCKS = 4
MESH = pl.DeviceIdType.MESH


def _params(sem=None):
    return pltpu.CompilerParams(dimension_semantics=sem, vmem_limit_bytes=VMEM_LIMIT_BYTES)


def _sds(shape, dtype):
    return jax.ShapeDtypeStruct(shape, dtype)


def _rms_hat(xv):
    r = lax.rsqrt(jnp.mean(xv * xv, axis=-1, keepdims=True) + NORM_EPS)
    return xv * r, r


def mm_nn(name, a, w, col_offsets, n_out, epilogue, out_dtypes, extras=(), rowvecs=(), n_sums=0, tm=1024, tn=512):
    m, k = a.shape
    tm, tn = min(tm, m), min(tn, n_out)
    nw, ne, nr, no = len(col_offsets), len(extras), len(rowvecs), len(out_dtypes)

    def body(a_ref, *refs):
        w_refs, e_refs, r_refs = refs[:nw], refs[nw:nw + ne], refs[nw + ne:nw + ne + nr]
        o_refs, s_refs = refs[nw + ne + nr:nw + ne + nr + no], refs[nw + ne + nr + no:]
        av = a_ref[...]
        accs = [jnp.dot(av, w_ref[...], preferred_element_type=F32) for w_ref in w_refs]
        outs = epilogue(accs, [e[...] for e in e_refs], [r[...] for r in r_refs])
        for o_ref, o in zip(o_refs, outs[:no]):
            o_ref[...] = o.astype(o_ref.dtype)
        if n_sums:
            @pl.when(pl.program_id(1) == 0)
            def _():
                for s_ref in s_refs:
                    s_ref[...] = jnp.zeros_like(s_ref)

            for s_ref, val in zip(s_refs, outs[no:]):
                s_ref[...] += val

    def wspec(off):
        return pl.BlockSpec((k, tn), lambda j, i, off=off: (0, off // tn + j))

    def rspec(off):
        return pl.BlockSpec((1, tn), lambda j, i, off=off: (0, off // tn + j))

    tile = pl.BlockSpec((tm, tn), lambda j, i: (i, j))
    in_specs = ([pl.BlockSpec((tm, k), lambda j, i: (i, 0))] + [wspec(o) for o in col_offsets]
                + [tile] * ne + [rspec(o) for _, o in rowvecs])
    sem = ("parallel", "arbitrary") if n_sums else ("parallel", "parallel")
    return pl.pallas_call(
        body, grid=(n_out // tn, m // tm), in_specs=in_specs,
        out_specs=[tile] * no + [pl.BlockSpec((1, tn), lambda j, i: (0, j))] * n_sums,
        out_shape=[_sds((m, n_out), dt) for dt in out_dtypes] + [_sds((1, n_out), F32)] * n_sums, name=name,
        compiler_params=_params(sem))(a, *([w] * nw), *extras, *[r for r, _ in rowvecs])


def mm_nt(name, g, w, epilogue, out_dtypes, extras=(), rowvecs=(), n_sums=0, tm=512, tk=512):
    m, n = g.shape
    k = w.shape[0]
    tm, tk = min(tm, m), min(tk, k)
    ne, nr, no = len(extras), len(rowvecs), len(out_dtypes)

    def body(g_ref, w_ref, *refs):
        e_refs, r_refs, o_refs, s_refs = refs[:ne], refs[ne:ne + nr], refs[ne + nr:ne + nr + no], refs[ne + nr + no:]
        acc = lax.dot_general(g_ref[...], w_ref[...], (((1,), (1,)), ((), ())), preferred_element_type=F32)
        outs = epilogue(acc, [e[...] for e in e_refs], [r[...] for r in r_refs])
        for o_ref, o in zip(o_refs, outs[:no]):
            o_ref[...] = o.astype(o_ref.dtype)
        if n_sums:
            @pl.when(pl.program_id(0) == 0)
            def _():
                for s_ref in s_refs:
                    s_ref[...] = jnp.zeros_like(s_ref)

            for s_ref, val in zip(s_refs, outs[no:]):
                s_ref[...] += val

    tile = pl.BlockSpec((tm, tk), lambda i, j: (i, j))
    vec = pl.BlockSpec((1, tk), lambda i, j: (0, j))
    sem = ("arbitrary", "parallel") if n_sums else ("parallel", "parallel")
    return pl.pallas_call(
        body, grid=(m // tm, k // tk),
        in_specs=[pl.BlockSpec((tm, n), lambda i, j: (i, 0)), pl.BlockSpec((tk, n), lambda i, j: (j, 0))]
        + [tile] * ne + [vec] * nr,
        out_specs=[tile] * no + [vec] * n_sums,
        out_shape=[_sds((m, k), dt) for dt in out_dtypes] + [_sds((1, k), F32)] * n_sums, name=name,
        compiler_params=_params(sem))(g, w, *extras, *rowvecs)


def mm_tn(name, a, g, tk=512, tn=512):
    m, k = a.shape
    n = g.shape[1]
    tk, tn = min(tk, k), min(tn, n)

    def body(a_ref, g_ref, o_ref):
        acc = lax.dot_general(a_ref[...], g_ref[...], (((0,), (0,)), ((), ())), preferred_element_type=F32)
        o_ref[...] = acc.astype(o_ref.dtype)

    return pl.pallas_call(
        body, grid=(k // tk, n // tn),
        in_specs=[pl.BlockSpec((m, tk), lambda i, j: (0, i)), pl.BlockSpec((m, tn), lambda i, j: (0, j))],
        out_specs=pl.BlockSpec((tk, tn), lambda i, j: (i, j)), out_shape=_sds((k, n), BF16), name=name,
        compiler_params=_params(("parallel", "parallel")))(a, g)


def mm_tn_many(name, pairs, tk=512, tn=1024):
    m = pairs[0][0].shape[0]
    np_, counts = len(pairs), []
    for a, g in pairs:
        counts.append((a.shape[1] // tk, g.shape[1] // tn))
    firsts = [sum(ck * cn for ck, cn in counts[:i]) for i in range(np_)]

    def tile(i, s):
        local = jnp.clip(s - firsts[i], 0, counts[i][0] * counts[i][1] - 1)
        return local // counts[i][1], local % counts[i][1]

    def body(*refs):
        s = pl.program_id(0)
        for i in range(np_):
            @pl.when(jnp.logical_and(s >= firsts[i], s < firsts[i] + counts[i][0] * counts[i][1]))
            def _(i=i):
                acc = lax.dot_general(refs[2 * i][...], refs[2 * i + 1][...], (((0,), (0,)), ((), ())),
                                      preferred_element_type=F32)
                refs[2 * np_ + i][...] = acc.astype(BF16)

    in_specs, out_specs = [], []
    for i in range(np_):
        in_specs += [pl.BlockSpec((m, tk), lambda s, i=i: (0, tile(i, s)[0])),
                     pl.BlockSpec((m, tn), lambda s, i=i: (0, tile(i, s)[1]))]
        out_specs.append(pl.BlockSpec((tk, tn), lambda s, i=i: tile(i, s)))
    return pl.pallas_call(
        body, grid=(firsts[-1] + counts[-1][0] * counts[-1][1],), in_specs=in_specs, out_specs=out_specs,
        out_shape=[_sds((a.shape[1], g.shape[1]), BF16) for a, g in pairs], name=name,
        compiler_params=_params(("arbitrary",)))(*[x for pair in pairs for x in pair])


def _row_mask(tc):
    row = lax.broadcasted_iota(jnp.int32, (8 * tc, 256), 0) % 8
    col = lax.broadcasted_iota(jnp.int32, (8 * tc, 256), 1) // 32
    return row == col


def _expand_rows(val, mask):
    tc, width = val.shape
    rep = jnp.broadcast_to(val[:, None, :], (tc, 8, width)).reshape(8 * tc, width)
    return jnp.where(mask, rep, 0.0).astype(BF16)


def _stage(ref, val):
    ref[0] = val[:, 0:128]
    ref[1] = val[:, 128:256]


def _gather_rows(src_ref, tc):
    halves = []
    for half in range(2):
        col = lax.broadcasted_iota(jnp.int32, (tc, 128), 1) // 32 + 4 * half
        out = jnp.zeros((tc, 128), F32)
        for s8 in range(4 * half, 4 * half + 4):
            out = jnp.where(col == s8, src_ref.at[half][pl.ds(s8, tc, stride=8), :], out)
        halves.append(out)
    return jnp.concatenate(halves, axis=1)


def _repeat(n, by, step, carry):
    def trip(i, c):
        for j in range(by):
            c = step(i * by + j, c)
        return c

    return lax.fori_loop(0, n // by, trip, carry)


def _gelu_and_slope(x):
    c = math.sqrt(2.0 / math.pi)
    t = jnp.tanh(c * (x + 0.044715 * x * x * x))
    return 0.5 * x * (1.0 + t), 0.5 * (1.0 + t) + 0.5 * x * (1.0 - t * t) * c * (1.0 + 3.0 * 0.044715 * x * x)


def s5_fwd(x, gain, d_skip, rb, rc, lam_r, lam_i):
    n_rows = x.shape[0]
    tc = min(S5_CHUNK, n_rows)
    nc = n_rows // tc

    def body(x_ref, g_ref, d_ref, rb_ref, rc_ref, lr_ref, li_ref, ge_ref, slope_ref, cs_ref, bux, yrows, carry):
        i = pl.program_id(0)
        u = _rms_hat(x_ref[...])[0] * g_ref[...]

        @pl.when(i == 0)
        def _():
            carry[...] = jnp.zeros_like(carry)

        cs_ref[0] = carry[...]
        mask = _row_mask(tc)
        for blk in range(S5_BLOCKS):
            lhs = _expand_rows(u[:, blk * 256:(blk + 1) * 256], mask)
            bux[blk] = jnp.dot(lhs, rb_ref[blk], preferred_element_type=F32)
        lam = [(lr_ref[blk], li_ref[blk]) for blk in range(S5_BLOCKS)]

        def step(t, c):
            r0 = pl.multiple_of(t * 8, 8)
            new = []
            for blk in range(S5_BLOCKS):
                xr, xi = c[2 * blk], c[2 * blk + 1]
                lr, li = lam[blk]
                nr = lr * xr - li * xi + bux[blk, pl.ds(r0, 8), 0:128]
                ni = lr * xi + li * xr + bux[blk, pl.ds(r0, 8), 128:256]
                bux[blk, pl.ds(r0, 8), 0:128] = nr
                bux[blk, pl.ds(r0, 8), 128:256] = ni
                new += [nr, ni]
            return tuple(new)

        c0 = []
        for blk in range(S5_BLOCKS):
            c0 += [carry[blk, :, 0:128], carry[blk, :, 128:256]]
        cn = _repeat(tc, 8, step, tuple(c0))
        for blk in range(S5_BLOCKS):
            carry[blk, :, 0:128] = cn[2 * blk]
            carry[blk, :, 128:256] = cn[2 * blk + 1]
        for blk in range(S5_BLOCKS):
            _stage(yrows, jnp.dot(bux[blk].astype(BF16), rc_ref[blk], preferred_element_type=F32))
            sl = slice(blk * 256, (blk + 1) * 256)
            ge, slope = _gelu_and_slope(_gather_rows(yrows, tc) + d_ref[:, sl] * u[:, sl])
            slope_ref[:, sl] = slope
            ge_ref[:, sl] = ge.astype(BF16)

    row = pl.BlockSpec((tc, D_MODEL), lambda i: (i, 0))
    vec = pl.BlockSpec((1, D_MODEL), lambda i: (0, 0))
    mat = pl.BlockSpec((S5_BLOCKS, 256, 256), lambda i: (0, 0, 0))
    lamspec = pl.BlockSpec((S5_BLOCKS, 8, 128), lambda i: (0, 0, 0))
    return pl.pallas_call(
        body, grid=(nc,),
        in_specs=[row, vec, vec, mat, mat, lamspec, lamspec],
        out_specs=[row, row, pl.BlockSpec((1, S5_BLOCKS, 8, 256), lambda i: (i, 0, 0, 0))],
        out_shape=[_sds((n_rows, D_MODEL), BF16), _sds((n_rows, D_MODEL), F32), _sds((nc, S5_BLOCKS, 8, 256), F32)],
        scratch_shapes=[pltpu.VMEM((S5_BLOCKS, 8 * tc, 256), F32), pltpu.VMEM((2, 8 * tc, 128), F32),
                        pltpu.VMEM((S5_BLOCKS, 8, 256), F32)],
        name="s5_fwd", compiler_params=_params(("arbitrary",)))(x, gain, d_skip, rb, rc, lam_r, lam_i)


def s5_bwd(x, gain, dy2, res, d_skip, cs, rb, rbt, rct, lam_r, lam_i):
    n_rows = x.shape[0]
    tc = min(S5_CHUNK, n_rows)
    nc = n_rows // tc

    def body(x_ref, g_ref, dy_ref, res_ref, d_ref, cs_ref, rb_ref, rbt_ref, rct_ref, lr_ref, li_ref,
             dx_ref, dd_ref, drb_ref, drc_ref, dlr_ref, dli_ref, dg_ref, tmp, du, lhsu, lhsd, xs, adj, acarry):
        i = pl.program_id(0)
        u = _rms_hat(x_ref[...])[0] * g_ref[...]

        @pl.when(i == 0)
        def _():
            acarry[...] = jnp.zeros_like(acarry)
            dd_ref[...] = jnp.zeros_like(dd_ref)
            drb_ref[...] = jnp.zeros_like(drb_ref)
            drc_ref[...] = jnp.zeros_like(drc_ref)
            dlr_ref[...] = jnp.zeros_like(dlr_ref)
            dli_ref[...] = jnp.zeros_like(dli_ref)
            dg_ref[...] = jnp.zeros_like(dg_ref)

        dd_ref[...] += jnp.sum(dy_ref[...] * u, axis=0, keepdims=True)
        mask = _row_mask(tc)
        for blk in range(S5_BLOCKS):
            sl = slice(blk * 256, (blk + 1) * 256)
            lhsu[blk] = _expand_rows(u[:, sl], mask)
            xs[blk, 0:8] = cs_ref[0, blk]
            xs[blk, 8:8 * tc + 8] = jnp.dot(lhsu[blk], rb_ref[blk], preferred_element_type=F32)
            lhsd[blk] = _expand_rows(dy_ref[:, sl], mask)
            adj[blk] = jnp.dot(lhsd[blk], rct_ref[blk], preferred_element_type=F32)
        lam = [(lr_ref[blk], li_ref[blk]) for blk in range(S5_BLOCKS)]

        def fstep(t, c):
            r0 = pl.multiple_of(t * 8 + 8, 8)
            new = []
            for blk in range(S5_BLOCKS):
                xr, xi = c[2 * blk], c[2 * blk + 1]
                lr, li = lam[blk]
                nr = lr * xr - li * xi + xs[blk, pl.ds(r0, 8), 0:128]
                ni = lr * xi + li * xr + xs[blk, pl.ds(r0, 8), 128:256]
                xs[blk, pl.ds(r0, 8), 0:128] = nr
                xs[blk, pl.ds(r0, 8), 128:256] = ni
                new += [nr, ni]
            return tuple(new)

        c0 = []
        for blk in range(S5_BLOCKS):
            c0 += [cs_ref[0, blk, :, 0:128], cs_ref[0, blk, :, 128:256]]
        _repeat(tc, 8, fstep, tuple(c0))

        def bstep(k, c):
            t = tc - 1 - k
            r0 = pl.multiple_of(t * 8, 8)
            new_a, new_g = [], []
            for blk in range(S5_BLOCKS):
                ar, ai = c[0][2 * blk], c[0][2 * blk + 1]
                glr, gli = c[1][2 * blk], c[1][2 * blk + 1]
                lr, li = lam[blk]
                nr = lr * ar + li * ai + adj[blk, pl.ds(r0, 8), 0:128]
                ni = lr * ai - li * ar + adj[blk, pl.ds(r0, 8), 128:256]
                adj[blk, pl.ds(r0, 8), 0:128] = nr
                adj[blk, pl.ds(r0, 8), 128:256] = ni
                pr, pi = xs[blk, pl.ds(r0, 8), 0:128], xs[blk, pl.ds(r0, 8), 128:256]
                new_a += [nr, ni]
                new_g += [glr + nr * pr + ni * pi, gli + ni * pr - nr * pi]
            return tuple(new_a), tuple(new_g)

        a0, g0 = [], []
        for blk in range(S5_BLOCKS):
            a0 += [acarry[blk, :, 0:128], acarry[blk, :, 128:256]]
            g0 += [dlr_ref[blk], dli_ref[blk]]
        an, gn = _repeat(tc, 4, bstep, (tuple(a0), tuple(g0)))
        for blk in range(S5_BLOCKS):
            acarry[blk, :, 0:128] = an[2 * blk]
            acarry[blk, :, 128:256] = an[2 * blk + 1]
            dlr_ref[blk] = gn[2 * blk]
            dli_ref[blk] = gn[2 * blk + 1]
        for blk in range(S5_BLOCKS):
            sl = slice(blk * 256, (blk + 1) * 256)
            ab = adj[blk].astype(BF16)
            _stage(tmp, jnp.dot(ab, rbt_ref[blk], preferred_element_type=F32))
            du[:, sl] = _gather_rows(tmp, tc) + d_ref[:, sl] * dy_ref[:, sl]
            drb_ref[blk] += lax.dot_general(lhsu[blk], ab, (((0,), (0,)), ((), ())), preferred_element_type=F32)
            drc_ref[blk] += lax.dot_general(lhsd[blk], xs[blk, 8:8 * tc + 8].astype(BF16), (((0,), (0,)), ((), ())),
                                            preferred_element_type=F32)
        xh, r = _rms_hat(x_ref[...])
        dg_ref[...] += jnp.sum(du[...] * xh, axis=0, keepdims=True)
        dxh = du[...] * g_ref[...]
        dx_ref[...] = r * (dxh - xh * jnp.mean(dxh * xh, axis=-1, keepdims=True)) + res_ref[...]

    rev = pl.BlockSpec((tc, D_MODEL), lambda i: (nc - 1 - i, 0))
    vec = pl.BlockSpec((1, D_MODEL), lambda i: (0, 0))
    mat = pl.BlockSpec((S5_BLOCKS, 256, 256), lambda i: (0, 0, 0))
    lamspec = pl.BlockSpec((S5_BLOCKS, 8, 128), lambda i: (0, 0, 0))
    big = pltpu.VMEM((S5_BLOCKS, 8 * tc, 256), F32)
    bigb = pltpu.VMEM((S5_BLOCKS, 8 * tc, 256), BF16)
    return pl.pallas_call(
        body, grid=(nc,),
        in_specs=[rev, vec, rev, rev, vec, pl.BlockSpec((1, S5_BLOCKS, 8, 256), lambda i: (nc - 1 - i, 0, 0, 0)),
                  mat, mat, mat, lamspec, lamspec],
        out_specs=[rev, vec, mat, mat, lamspec, lamspec, vec],
        out_shape=[_sds((n_rows, D_MODEL), F32), _sds((1, D_MODEL), F32), _sds((S5_BLOCKS, 256, 256), F32),
                   _sds((S5_BLOCKS, 256, 256), F32), _sds((S5_BLOCKS, 8, 128), F32), _sds((S5_BLOCKS, 8, 128), F32),
                   _sds((1, D_MODEL), F32)],
        scratch_shapes=[pltpu.VMEM((2, 8 * tc, 128), F32), pltpu.VMEM((tc, D_MODEL), F32), bigb, bigb,
                        pltpu.VMEM((S5_BLOCKS, 8 * tc + 8, 256), F32), big,
                        pltpu.VMEM((S5_BLOCKS, 8, 256), F32)],
        name="s5_bwd", compiler_params=_params(("arbitrary",)))(
            x, gain, dy2, res, d_skip, cs, rb, rbt, rct, lam_r, lam_i)


def _s5_views(a_re, a_im, log_dt, b_re, b_im):
    return a_re[:, None, :], a_im[:, None, :], log_dt[:, None, None], jnp.swapaxes(b_re, 1, 2), jnp.swapaxes(b_im, 1, 2)


def _s5_factors(a_re, a_im, log_dt):
    lr, li, dt = jnp.minimum(a_re, LAMBDA_RE_MAX), a_im, jnp.exp(log_dt)
    mag, ang = jnp.exp(lr * dt), li * dt
    lbr, lbi = mag * jnp.cos(ang), mag * jnp.sin(ang)
    den = lr * lr + li * li
    fr, fi = ((lbr - 1.0) * lr + lbi * li) / den, (lbi * lr - (lbr - 1.0) * li) / den
    return lr, li, dt, lbr, lbi, fr, fi, den


def s5_prep(a_re, a_im, log_dt, b_re, b_im, c_re, c_im):
    def body(ar_ref, ai_ref, t_ref, br_ref, bi_ref, cr_ref, ci_ref, rb_ref, rbt_ref, rc_ref, rct_ref, lr_ref, li_ref):
        _, _, _, lbr, lbi, fr, fi, _ = _s5_factors(ar_ref[...], ai_ref[...], t_ref[...])
        lr_ref[...] = lbr
        li_ref[...] = lbi
        bre = fr * br_ref[...] - fi * bi_ref[...]
        bim = fr * bi_ref[...] + fi * br_ref[...]
        even = (lax.broadcasted_iota(jnp.int32, (256, S5_STATE), 0) // S5_GROUP) % 2 == 0

        def assemble(re, im):
            re, im = re.reshape(256, S5_STATE), im.reshape(256, S5_STATE)
            return jnp.concatenate([jnp.where(even, re, 0.0), jnp.where(even, 0.0, re), jnp.where(even, im, 0.0),
                                    jnp.where(even, 0.0, im)], axis=1)

        for blk in range(S5_BLOCKS):
            sl = slice(16 * blk, 16 * blk + 16)
            rb = assemble(bre[sl], bim[sl])
            rct = assemble(cr_ref[sl], -ci_ref[sl])
            rb_ref[blk] = rb.astype(BF16)
            rbt_ref[blk] = rb.T.astype(BF16)
            rct_ref[blk] = rct.astype(BF16)
            rc_ref[blk] = rct.T.astype(BF16)

    vm = pl.BlockSpec(memory_space=pltpu.VMEM)
    mat = _sds((S5_BLOCKS, 256, 256), BF16)
    lam = _sds((S5_GROUPS, 1, S5_STATE), F32)
    rb, rbt, rc, rct, lam_r, lam_i = pl.pallas_call(
        body, in_specs=[vm] * 7, out_specs=[vm] * 6, out_shape=[mat, mat, mat, mat, lam, lam], name="s5_prep",
        compiler_params=_params())(*_s5_views(a_re, a_im, log_dt, b_re, b_im), c_re, c_im)
    return rb, rbt, rc, rct, lam_r.reshape(S5_BLOCKS, 8, 128), lam_i.reshape(S5_BLOCKS, 8, 128)


def s5_param_bwd(mats, lams, a_re, a_im, log_dt, b_re, b_im):
    def body(m_ref, glr_ref, gli_ref, ar_ref, ai_ref, t_ref, br_ref, bi_ref,
             dar_ref, dai_ref, dt_ref, dbr_ref, dbi_ref, dcr_ref, dci_ref):
        lr, li, dt, lbr, lbi, fr, fi, den = _s5_factors(ar_ref[...], ai_ref[...], t_ref[...])
        shape = (S5_GROUPS, S5_GROUP, S5_STATE)
        gbr, gbi = m_ref[0:1024, 0:64].reshape(shape), m_ref[0:1024, 64:128].reshape(shape)
        dcr_ref[...] = m_ref[1024:2048, 0:64].reshape(shape)
        dci_ref[...] = -m_ref[1024:2048, 64:128].reshape(shape)
        br, bi = br_ref[...], bi_ref[...]
        dbr_ref[...] = fr * gbr + fi * gbi
        dbi_ref[...] = fr * gbi - fi * gbr
        dfr = jnp.sum(gbr * br + gbi * bi, axis=1, keepdims=True)
        dfi = jnp.sum(gbi * br - gbr * bi, axis=1, keepdims=True)
        nr, ni = (dfr * lr - dfi * li) / den, (dfr * li + dfi * lr) / den
        qr, qi = (fr * lr + fi * li) / den, (fi * lr - fr * li) / den
        lam_r, lam_i = -(dfr * qr + dfi * qi), -(dfi * qr - dfr * qi)
        gr, gi = glr_ref[...] + nr, gli_ref[...] + ni
        zr, zi = gr * lbr + gi * lbi, gi * lbr - gr * lbi
        a = ar_ref[...]
        dar_ref[...] = (lam_r + zr * dt) * jnp.where(a < LAMBDA_RE_MAX, 1.0, jnp.where(a == LAMBDA_RE_MAX, 0.5, 0.0))
        dai_ref[...] = lam_i + zi * dt
        dt_ref[...] = jnp.sum(zr * lr + zi * li, axis=2, keepdims=True) * dt

    vm = pl.BlockSpec(memory_space=pltpu.VMEM)
    state = _sds((S5_GROUPS, 1, S5_STATE), F32)
    wide = _sds((S5_GROUPS, S5_GROUP, S5_STATE), F32)
    glr = lams[0:32].reshape(S5_GROUPS, 1, S5_STATE)
    gli = lams[32:64].reshape(S5_GROUPS, 1, S5_STATE)
    dar, dai, ddt, dbr, dbi, dcr, dci = pl.pallas_call(
        body, in_specs=[vm] * 8, out_specs=[vm] * 7,
        out_shape=[state, state, _sds((S5_GROUPS, 1, 1), F32), wide, wide, wide, wide], name="s5_param_bwd",
        compiler_params=_params())(mats, glr, gli, *_s5_views(a_re, a_im, log_dt, b_re, b_im))
    return (dar.reshape(S5_GROUPS, S5_STATE), dai.reshape(S5_GROUPS, S5_STATE), ddt.reshape(S5_GROUPS),
            jnp.swapaxes(dbr, 1, 2), jnp.swapaxes(dbi, 1, 2), dcr, dci)


def s5_compact(drb, drct, dlr, dli):
    def body(drb_ref, drct_ref, dlr_ref, dli_ref, o_ref, lam_ref):
        even = (lax.broadcasted_iota(jnp.int32, (256, 64), 0) // S5_GROUP) % 2 == 0
        for blk in range(S5_BLOCKS):
            for k, ref in enumerate((drb_ref, drct_ref)):
                m = ref[blk]
                re = jnp.where(even, m[:, 0:64], m[:, 64:128])
                im = jnp.where(even, m[:, 128:192], m[:, 192:256])
                o_ref[pl.ds(k * 1024 + blk * 256, 256), :] = jnp.concatenate([re, im], axis=1)
            lam_ref[pl.ds(blk * 8, 8), :] = dlr_ref[blk]
            lam_ref[pl.ds(32 + blk * 8, 8), :] = dli_ref[blk]

    vm = pl.BlockSpec(memory_space=pltpu.VMEM)
    return pl.pallas_call(body, in_specs=[vm] * 4, out_specs=[vm, vm], out_shape=[_sds((2048, 128), F32), _sds((64, 128), F32)],
                          name="s5_compact", compiler_params=_params())(drb, drct, dlr, dli)


NEG = -1e30


GROUP = N_Q // N_KV


def _attn_masks(n):
    qi = lax.broadcasted_iota(jnp.int32, (GROUP * BLOCK, BLOCK), 0) % BLOCK
    kj = lax.broadcasted_iota(jnp.int32, (GROUP * BLOCK, BLOCK), 1)
    return jnp.logical_and(kj > qi, n > 0), kj <= qi


def _stack_heads(ref, kh):
    return jnp.concatenate([ref[:, (GROUP * kh + g) * HEAD_DIM:(GROUP * kh + g + 1) * HEAD_DIM] for g in range(GROUP)], axis=0)


def _unstack_heads(val):
    return jnp.concatenate([val[g * BLOCK:(g + 1) * BLOCK] for g in range(GROUP)], axis=1)


def _sink_column(sink_ref, kh):
    grp = lax.broadcasted_iota(jnp.int32, (GROUP * BLOCK, 1), 0) // BLOCK
    col = jnp.zeros((GROUP * BLOCK, 1), F32)
    for g in range(GROUP):
        col = jnp.where(grp == g, sink_ref[GROUP * kh + g], col)
    return col, grp


def _attn_exp(q4, kp, kc, sink, mask_p, mask_c):
    scale = 1.0 / math.sqrt(HEAD_DIM)
    nt = (((1,), (1,)), ((), ()))
    sp = jnp.where(mask_p, lax.dot_general(q4, kp, nt, preferred_element_type=F32) * scale, NEG)
    sc = jnp.where(mask_c, lax.dot_general(q4, kc, nt, preferred_element_type=F32) * scale, NEG)
    m = jnp.maximum(jnp.maximum(jnp.max(sp, axis=-1, keepdims=True), jnp.max(sc, axis=-1, keepdims=True)), sink)
    pp = jnp.exp(sp - m)
    pc = jnp.exp(sc - m)
    ps = jnp.exp(sink - m)
    inv = 1.0 / (jnp.sum(pp, axis=-1, keepdims=True) + jnp.sum(pc, axis=-1, keepdims=True) + ps)
    return pp, pc, ps, inv


def attn_fwd(q, kv, sinks):
    n_rows = q.shape[0]
    nb = n_rows // BLOCK

    def body(sink_ref, q_ref, kvp_ref, kvc_ref, o_ref):
        n = pl.program_id(0)
        mask_p, mask_c = _attn_masks(n)
        outs = []
        for kh in range(N_KV):
            ks, vs = slice(kh * HEAD_DIM, (kh + 1) * HEAD_DIM), slice((N_KV + kh) * HEAD_DIM, (N_KV + kh + 1) * HEAD_DIM)
            sink, _ = _sink_column(sink_ref, kh)
            pp, pc, _, inv = _attn_exp(_stack_heads(q_ref, kh), kvp_ref[:, ks], kvc_ref[:, ks], sink, mask_p, mask_c)
            o4 = (jnp.dot(pp.astype(BF16), kvp_ref[:, vs], preferred_element_type=F32)
                  + jnp.dot(pc.astype(BF16), kvc_ref[:, vs], preferred_element_type=F32)) * inv
            outs.append(_unstack_heads(o4))
        o_ref[...] = jnp.concatenate(outs, axis=1).astype(BF16)

    kvw = 2 * N_KV * HEAD_DIM
    return pl.pallas_call(
        body, grid=(nb,),
        in_specs=[pl.BlockSpec(memory_space=pltpu.SMEM), pl.BlockSpec((BLOCK, D_MODEL), lambda n: (n, 0)),
                  pl.BlockSpec((BLOCK, kvw), lambda n: (jnp.maximum(n - 1, 0), 0)), pl.BlockSpec((BLOCK, kvw), lambda n: (n, 0))],
        out_specs=pl.BlockSpec((BLOCK, D_MODEL), lambda n: (n, 0)), out_shape=_sds((n_rows, D_MODEL), BF16),
        name="attn_fwd", compiler_params=_params(("parallel",)))(sinks, q, kv, kv)


def attn_bwd(q, kv, do, sinks):
    n_rows = q.shape[0]
    nb = n_rows // BLOCK
    kvw = 2 * N_KV * HEAD_DIM
    tn = (((0,), (0,)), ((), ()))
    nt = (((1,), (1,)), ((), ()))
    scale = 1.0 / math.sqrt(HEAD_DIM)

    def body(sink_ref, q_ref, kvp_ref, kvc_ref, do_ref, dq_ref, dbq_ref, dprev_ref, dcur_ref, dsink_ref):
        n = pl.program_id(0)
        mask_p, mask_c = _attn_masks(n)
        lane = lax.broadcasted_iota(jnp.int32, (1, D_MODEL), 1)
        dqs, dsink = [], jnp.zeros((1, D_MODEL), F32)
        dkp, dkc, dvp, dvc = [], [], [], []
        for kh in range(N_KV):
            ks, vs = slice(kh * HEAD_DIM, (kh + 1) * HEAD_DIM), slice((N_KV + kh) * HEAD_DIM, (N_KV + kh + 1) * HEAD_DIM)
            q4, do4 = _stack_heads(q_ref, kh), _stack_heads(do_ref, kh)
            kp, kc, vp, vc = kvp_ref[:, ks], kvc_ref[:, ks], kvp_ref[:, vs], kvc_ref[:, vs]
            sink, grp = _sink_column(sink_ref, kh)
            pp, pc, ps, inv = _attn_exp(q4, kp, kc, sink, mask_p, mask_c)
            pp, pc = pp * inv, pc * inv
            dpp = lax.dot_general(do4, vp, nt, preferred_element_type=F32)
            dpc = lax.dot_general(do4, vc, nt, preferred_element_type=F32)
            delta = jnp.sum(pp * dpp, axis=-1, keepdims=True) + jnp.sum(pc * dpc, axis=-1, keepdims=True)
            dsp = (pp * (dpp - delta) * scale).astype(BF16)
            dsc = (pc * (dpc - delta) * scale).astype(BF16)
            dsk = ps * inv * delta
            for g in range(GROUP):
                dsink = dsink + jnp.where(lane == GROUP * kh + g, -jnp.sum(jnp.where(grp == g, dsk, 0.0)), 0.0)
            dqs.append(_unstack_heads(jnp.dot(dsp, kp, preferred_element_type=F32)
                                      + jnp.dot(dsc, kc, preferred_element_type=F32)))
            dkp.append(lax.dot_general(dsp, q4, tn, preferred_element_type=F32))
            dkc.append(lax.dot_general(dsc, q4, tn, preferred_element_type=F32))
            dvp.append(lax.dot_general(pp.astype(BF16), do4, tn, preferred_element_type=F32))
            dvc.append(lax.dot_general(pc.astype(BF16), do4, tn, preferred_element_type=F32))
        dq = jnp.concatenate(dqs, axis=1)
        dq_ref[...] = dq.astype(BF16)
        dprev_ref[0] = jnp.concatenate(dkp + dvp, axis=1)
        dcur_ref[0] = jnp.concatenate(dkc + dvc, axis=1)

        @pl.when(n == 0)
        def _():
            dbq_ref[...] = jnp.zeros_like(dbq_ref)
            dsink_ref[...] = jnp.zeros_like(dsink_ref)

        dbq_ref[...] += jnp.sum(dq, axis=0, keepdims=True)
        dsink_ref[...] += dsink

    blk = pl.BlockSpec((BLOCK, D_MODEL), lambda n: (n, 0))
    part = pl.BlockSpec((1, BLOCK, kvw), lambda n: (n, 0, 0))
    return pl.pallas_call(
        body, grid=(nb,),
        in_specs=[pl.BlockSpec(memory_space=pltpu.SMEM), blk,
                  pl.BlockSpec((BLOCK, kvw), lambda n: (jnp.maximum(n - 1, 0), 0)), pl.BlockSpec((BLOCK, kvw), lambda n: (n, 0)), blk],
        out_specs=[blk, pl.BlockSpec((1, D_MODEL), lambda n: (0, 0)), part, part, pl.BlockSpec((1, D_MODEL), lambda n: (0, 0))],
        out_shape=[_sds((n_rows, D_MODEL), BF16), _sds((1, D_MODEL), F32), _sds((nb, BLOCK, kvw), F32),
                   _sds((nb, BLOCK, kvw), F32), _sds((1, D_MODEL), F32)],
        name="attn_bwd", compiler_params=_params(("arbitrary",)))(sinks, q, kv, kv, do)


def kv_combine(dprev, dcur):
    nb, _, kvw = dprev.shape

    def body(dcur_ref, dprev_ref, dkv_ref, db_ref):
        total = jnp.zeros((1, kvw), F32)
        for m in range(nb):
            dkv = dcur_ref[m] + dprev_ref[m + 1] if m + 1 < nb else dcur_ref[m]
            dkv_ref[m * BLOCK:(m + 1) * BLOCK, :] = dkv.astype(BF16)
            total = total + jnp.sum(dkv, axis=0, keepdims=True)
        db_ref[...] = jnp.concatenate([total, jnp.zeros((1, D_MODEL - kvw), F32)], axis=1)

    vm = pl.BlockSpec(memory_space=pltpu.VMEM)
    return pl.pallas_call(body, in_specs=[vm, vm], out_specs=[vm, vm],
                          out_shape=[_sds((nb * BLOCK, kvw), BF16), _sds((1, D_MODEL), F32)], name="kv_combine",
                          compiler_params=_params())(dcur, dprev)


def glu_bwd(dout, val, gate, tm=256):
    n_rows, d = dout.shape

    def body(do_ref, v_ref, g_ref, dz_ref, db_ref):
        i = pl.program_id(0)
        sg = jax.nn.sigmoid(g_ref[...])
        dval = do_ref[...] * sg
        dgate = do_ref[...] * v_ref[...] * sg * (1.0 - sg)
        dz_ref[...] = jnp.concatenate([dval, dgate], axis=1).astype(BF16)

        @pl.when(i == 0)
        def _():
            db_ref[...] = jnp.zeros_like(db_ref)

        db_ref[0:1, :] += jnp.sum(dval, axis=0, keepdims=True)
        db_ref[1:2, :] += jnp.sum(dgate, axis=0, keepdims=True)

    row = pl.BlockSpec((tm, d), lambda i: (i, 0))
    return pl.pallas_call(
        body, grid=(n_rows // tm,), in_specs=[row, row, row],
        out_specs=[pl.BlockSpec((tm, 2 * d), lambda i: (i, 0)), pl.BlockSpec((2, d), lambda i: (0, 0))],
        out_shape=[_sds((n_rows, 2 * d), BF16), _sds((2, d), F32)],
        name="glu_bwd", compiler_params=_params(("arbitrary",)))(dout, val, gate)


def _adam_update(w, g, m, v):
    nm = ADAM_B1 * m + (1.0 - ADAM_B1) * g
    nv = ADAM_B2 * v + (1.0 - ADAM_B2) * (g * g)
    m_hat = nm / (1.0 - ADAM_B1 ** ADAM_STEP)
    v_hat = nv / (1.0 - ADAM_B2 ** ADAM_STEP)
    return -ADAM_LR * (m_hat / (jnp.sqrt(v_hat) + ADAM_EPS) + ADAM_WD * w), nm, nv


def adamw(name, ws, gs, ms, vs, steps=8):
    n = len(ws)

    def body(*refs):
        for k in range(n):
            w_ref, g_ref, m_ref, v_ref = (refs[j * n + k] for j in range(4))
            go_ref, d_ref, nm_ref, nv_ref = (refs[(4 + j) * n + k] for j in range(4))
            gv = g_ref[...]
            go_ref[...] = gv
            d_ref[...], nm_ref[...], nv_ref[...] = _adam_update(w_ref[...], gv, m_ref[...], v_ref[...])

    specs = [pl.BlockSpec((w.shape[0] // steps, w.shape[1]), lambda i: (i, 0)) for w in ws]
    shapes = [_sds(w.shape, F32) for w in ws]
    out = pl.pallas_call(
        body, grid=(steps,), in_specs=specs * 4, out_specs=specs * 4, out_shape=shapes * 4, name=name,
        compiler_params=_params(("parallel",)))(*ws, *gs, *ms, *vs)
    return [list(out[j * n:(j + 1) * n]) for j in range(4)]


def adamw_native(name, ws, gs, ms, vs):
    n = len(ws)

    def body(*refs):
        w_refs, g_refs, m_refs, v_refs = refs[:n], refs[n:2 * n], refs[2 * n:3 * n], refs[3 * n:4 * n]
        d_refs, nm_refs, nv_refs = refs[4 * n:5 * n], refs[5 * n:6 * n], refs[6 * n:7 * n]
        for k in range(n):
            dl, nm, nv = _adam_update(w_refs[k][...], g_refs[k][...], m_refs[k][...], v_refs[k][...])
            d_refs[k][...] = dl
            nm_refs[k][...] = nm
            nv_refs[k][...] = nv

    vm = pl.BlockSpec(memory_space=pltpu.VMEM)
    shapes = [_sds(w.shape, F32) for w in ws]
    out = pl.pallas_call(body, in_specs=[vm] * (4 * n), out_specs=[vm] * (3 * n), out_shape=shapes * 3, name=name,
                         compiler_params=_params())(*ws, *gs, *ms, *vs)
    return list(out[:n]), list(out[n:2 * n]), list(out[2 * n:])


VEC_ROWS = {"norm_mix": 0, "norm_mlp": 2, "norm_kv": 4, "norm_final": 5, "s5_d": 6, "b_q": 7, "b_o": 8, "s5_b_glu": 9,
            "b_kv": 11, "sinks": 12, "loss": 13}


def split_vectors(where, vecs, d_shard, glu_shard):
    kvw = 2 * N_KV * HEAD_DIM
    shapes = {"norm_mix": (2, D_MODEL), "norm_mlp": (2, D_MODEL), "norm_kv": (1, D_MODEL), "norm_final": (1, D_MODEL),
              "s5_d": (1, d_shard), "b_q": (1, D_MODEL), "b_o": (1, D_MODEL), "s5_b_glu": (1, glu_shard), "b_kv": (1, kvw),
              "sinks": (1, N_Q), "loss": (1, 128)}
    names = list(shapes)

    def body(where_ref, v_ref, *o_refs):
        chip = where_ref[1]
        for name, o_ref in zip(names, o_refs):
            r0, (r, n) = VEC_ROWS[name], shapes[name]
            if name == "s5_d":
                g = jnp.zeros((1, n), F32)
                for j in range(4):
                    g = jnp.where(chip == j, v_ref[r0:r0 + 1, j * n:(j + 1) * n], g)
            elif name == "s5_b_glu":
                g = jnp.zeros((1, n), F32)
                for j in range(4):
                    row, col = r0 + (j * n) // D_MODEL, (j * n) % D_MODEL
                    g = jnp.where(chip == j, v_ref[row:row + 1, col:col + n], g)
            else:
                g = v_ref[r0:r0 + r, 0:n]
            o_ref[...] = g

    vm = pl.BlockSpec(memory_space=pltpu.VMEM)
    out = pl.pallas_call(body, in_specs=[pl.BlockSpec(memory_space=pltpu.SMEM), vm], out_specs=[vm] * len(names),
                         out_shape=[_sds(shapes[n], F32) for n in names], name="split_vectors",
                         compiler_params=_params())(where, vecs)
    return dict(zip(names, out))


def _position():
    x, y, c = lax.axis_index("x"), lax.axis_index("y"), lax.axis_index("c")
    others = [(1 - x, y), (x, 1 - y), (1 - x, 1 - y)]
    return x, y, c, others


def _window(ref, kind, chip, half, shard_shape):
    if kind == "slab":
        return ref.at[chip]
    r, n = shard_shape
    if kind == "col":
        return ref.at[pl.ds(pl.multiple_of(half * (r // 2), 16), r // 2), pl.ds(pl.multiple_of(chip * n, 128), n)]
    return ref.at[pl.ds(pl.multiple_of(chip * r, 16), r), pl.ds(pl.multiple_of(half * (n // 2), 128), n // 2)]


def _half(ref, kind, half, shape):
    r, n = shape
    if kind == "col":
        return ref.at[pl.ds(pl.multiple_of(half * (r // 2), 16), r // 2), :]
    return ref.at[:, pl.ds(pl.multiple_of(half * (n // 2), 128), n // 2)]


def swap_start(name, grads, kinds, carry):
    nt = len(grads)
    shapes = [tuple(g.shape) for g in grads]
    lands = [lax.empty(sh, BF16) for sh in shapes]
    given, given_specs, token_type, write = _hand_through(carry)
    n_in = 2 * nt + len(given)

    def body(*refs):
        in_refs, land_refs = refs[:nt], refs[nt:2 * nt]
        send_sems, recv_sems, token = refs[n_in], refs[n_in + 1], refs[-1]
        x, y, c, _ = _position()
        for t in range(nt):
            pltpu.make_async_remote_copy(
                src_ref=_half(in_refs[t], kinds[t], 1 - c, shapes[t]), dst_ref=_half(land_refs[t], kinds[t], 1 - c, shapes[t]),
                send_sem=send_sems.at[t], recv_sem=recv_sems.at[t], device_id=(x, y, 1 - c), device_id_type=MESH).start()
        write(token, refs[:n_in])

    sems = pltpu.SemaphoreType.DMA((nt,))
    both = list(grads) + lands
    out = pl.pallas_call(
        body, name=name, in_specs=[HBM_SPEC] * (2 * nt) + given_specs,
        out_specs=(SEM_SPEC, SEM_SPEC, *[HBM_SPEC] * (2 * nt), pl.BlockSpec(memory_space=pltpu.VMEM)),
        out_shape=(sems, sems, *[pltpu.HBM(a.shape, a.dtype) for a in both], token_type),
        input_output_aliases={t: 2 + t for t in range(2 * nt)}, compiler_params=_split_params(),
    )(*[_in_hbm(a) for a in both], *given)
    return out[0], out[1], list(out[2:2 + nt]), list(out[2 + nt:2 + 2 * nt]), out[-1]


def swap_wait(name, send_sems, recv_sems, grads, lands, kinds, after):
    nt = len(grads)
    shapes = [tuple(g.shape) for g in grads]

    def body(*refs):
        in_refs, land_refs = refs[:nt], refs[nt:2 * nt]
        send_ref, recv_ref = refs[2 * nt], refs[2 * nt + 1]
        x, y, c, _ = _position()
        for t in range(nt):
            cp = pltpu.make_async_remote_copy(
                src_ref=_half(in_refs[t], kinds[t], 1 - c, shapes[t]), dst_ref=_half(land_refs[t], kinds[t], c, shapes[t]),
                send_sem=send_ref.at[t], recv_sem=recv_ref.at[t], device_id=(x, y, 1 - c), device_id_type=MESH)
            cp.wait_send()
            cp.wait_recv()

    both = list(grads) + list(lands)
    out = pl.pallas_call(
        body, name=name, in_specs=[HBM_SPEC] * (2 * nt) + [SEM_SPEC, SEM_SPEC, HBM_SPEC], out_specs=[HBM_SPEC] * (2 * nt),
        out_shape=[pltpu.HBM(a.shape, a.dtype) for a in both], input_output_aliases={t: t for t in range(2 * nt)},
        compiler_params=_split_params())(*both, send_sems, recv_sems, _in_hbm(after))
    return list(out[:nt]), list(out[nt:])


def _half_spec(kind, shape, tiles):
    r, n = shape
    if kind == "col":
        tn = n // tiles
        return pl.BlockSpec((r // 2, tn), lambda i, s: (s[0], i))
    tm = r // tiles
    return pl.BlockSpec((tm, n // 2), lambda i, s: (i, s[0]))


def add_halves(name, mine, landed, kinds, where, tiles=2):
    nt = len(mine)
    shapes = [tuple(a.shape) for a in mine]

    def compact(t):
        r, n = shapes[t]
        if kinds[t] == "col":
            return (r // 2, n), pl.BlockSpec((r // 2, n // tiles), lambda i, s: (0, i))
        return (r, n // 2), pl.BlockSpec((r // tiles, n // 2), lambda i, s: (i, 0))

    def body(s_ref, *refs):
        for a_ref, b_ref, o_ref in zip(refs[:nt], refs[nt:2 * nt], refs[2 * nt:]):
            o_ref[...] = (a_ref[...].astype(F32) + b_ref[...].astype(F32)).astype(BF16)

    specs = [_half_spec(kinds[t], shapes[t], tiles) for t in range(nt)]
    return pl.pallas_call(
        body, grid_spec=pltpu.PrefetchScalarGridSpec(num_scalar_prefetch=1, grid=(tiles,), in_specs=specs + specs,
                                                     out_specs=[compact(t)[1] for t in range(nt)]),
        out_shape=[_sds(compact(t)[0], BF16) for t in range(nt)], name=name,
        compiler_params=_params(("parallel",)))(where, *mine, *landed)


def sum_shards(name, parts, landed, kinds, shard_shapes, where, layers, n_layers, intos, tiles=2):
    nt = len(parts)
    in_specs, out_specs = [], []
    for t in range(nt):
        (r, n), layer = shard_shapes[t], layers[t]
        if kinds[t] == "col":
            tm, width = r // 2 // tiles, n
            own = pl.BlockSpec((tm, n), lambda i, s: (i, s[1]))
            out = pl.BlockSpec((None, tm, n), lambda i, s, layer=layer: (layer, s[0] * tiles + i, 0))
        else:
            tm, width = r // tiles, n // 2
            own = pl.BlockSpec((tm, n // 2), lambda i, s: (s[1] * tiles + i, 0))
            out = pl.BlockSpec((None, tm, n // 2), lambda i, s, layer=layer: (layer, i, s[0]))
        in_specs += [own, pl.BlockSpec((3, tm, width), lambda i, s: (0, i, 0))]
        out_specs.append(out)
    args, aliases = [where] + [a for pair in zip(parts, landed) for a in pair], {}
    for t in range(nt):
        if intos[t] is not None:
            aliases[len(args)] = t
            in_specs.append(pl.BlockSpec(memory_space=pl.ANY))
            args.append(intos[t])

    def body(s_ref, *refs):
        for t in range(nt):
            a_ref, l_ref, o_ref = refs[2 * t], refs[2 * t + 1], refs[len(in_specs) + t]
            o_ref[...] = ((a_ref[...].astype(F32) + l_ref[0].astype(F32)) + l_ref[1].astype(F32)) + l_ref[2].astype(F32)

    return pl.pallas_call(
        body, grid_spec=pltpu.PrefetchScalarGridSpec(num_scalar_prefetch=1, grid=(tiles,), in_specs=in_specs,
                                                     out_specs=out_specs),
        out_shape=[_sds((n_layers[t],) + tuple(shard_shapes[t]), F32) for t in range(nt)], input_output_aliases=aliases,
        name=name, compiler_params=_params(("parallel",)))(*args)


def share_start(arrays, entries, carry):
    na, nt = len(arrays), len(entries)
    given, given_specs, token_type, write = _hand_through(carry)
    n_in = na + len(given)

    def body(*refs):
        in_refs, send_sems, recv_sems, token = refs[:na], refs[n_in], refs[n_in + 1], refs[-1]
        x, y, c, _ = _position()
        for t, (a, layer, kind) in enumerate(entries):
            mine = _half(in_refs[a].at[layer], kind, c, tuple(arrays[a].shape[1:]))
            pltpu.make_async_remote_copy(
                src_ref=mine, dst_ref=mine, send_sem=send_sems.at[t], recv_sem=recv_sems.at[t],
                device_id=(x, y, 1 - c), device_id_type=MESH).start()
        write(token, refs[:n_in])

    sems = pltpu.SemaphoreType.DMA((nt,))
    out = pl.pallas_call(
        body, name="share_start", in_specs=[HBM_SPEC] * na + given_specs,
        out_specs=(SEM_SPEC, SEM_SPEC, *[HBM_SPEC] * na, pl.BlockSpec(memory_space=pltpu.VMEM)),
        out_shape=(sems, sems, *[pltpu.HBM(a.shape, a.dtype) for a in arrays], token_type),
        input_output_aliases={t: 2 + t for t in range(na)}, compiler_params=_split_params(),
    )(*[_in_hbm(a) for a in arrays], *given)
    return out[0], out[1], list(out[2:2 + na]), out[-1]


def share_wait(send_sems, recv_sems, arrays, entries, after):
    na = len(arrays)

    def body(*refs):
        in_refs, send_ref, recv_ref = refs[:na], refs[na], refs[na + 1]
        x, y, c, _ = _position()
        for t, (a, layer, kind) in enumerate(entries):
            shape = tuple(arrays[a].shape[1:])
            cp = pltpu.make_async_remote_copy(
                src_ref=_half(in_refs[a].at[layer], kind, c, shape), dst_ref=_half(in_refs[a].at[layer], kind, 1 - c, shape),
                send_sem=send_ref.at[t], recv_sem=recv_ref.at[t], device_id=(x, y, 1 - c), device_id_type=MESH)
            cp.wait_send()
            cp.wait_recv()

    return list(pl.pallas_call(
        body, name="share_wait", in_specs=[HBM_SPEC] * na + [SEM_SPEC, SEM_SPEC, HBM_SPEC], out_specs=[HBM_SPEC] * na,
        out_shape=[pltpu.HBM(a.shape, a.dtype) for a in arrays], input_output_aliases={t: t for t in range(na)},
        compiler_params=_split_params())(*arrays, send_sems, recv_sems, _in_hbm(after)))


HBM_SPEC = pl.BlockSpec(memory_space=pltpu.HBM)
SEM_SPEC = pl.BlockSpec(memory_space=pltpu.SEMAPHORE)
ANY_SPEC = pl.BlockSpec(memory_space=pl.ANY)


def _split_params():
    return pltpu.CompilerParams(has_side_effects=pltpu.SideEffectType.DATAFLOW_SIDE_EFFECTING,
                                vmem_limit_bytes=VMEM_LIMIT_BYTES)


def _in_hbm(a):
    return pltpu.with_memory_space_constraint(a, pltpu.HBM)


def cast_place(arrays, entries, where, tiles=2):
    in_specs, out_specs, fulls = [], [], []
    for a, layer, kind in entries:
        _, r, n = arrays[a].shape
        tm = r // tiles
        in_specs.append(pl.BlockSpec((None, tm, n), lambda i, s, layer=layer: (layer, i, 0)))
        if kind == "col":
            fulls.append((r, 4 * n))
            out_specs.append(pl.BlockSpec((tm, n), lambda i, s: (i, s[1])))
        else:
            fulls.append((4 * r, n))
            out_specs.append(pl.BlockSpec((tm, n), lambda i, s: (s[1] * tiles + i, 0)))
    nt = len(entries)

    def body(s_ref, *refs):
        for w_ref, o_ref in zip(refs[:nt], refs[nt:]):
            o_ref[...] = w_ref[...].astype(BF16)

    return pl.pallas_call(
        body, grid_spec=pltpu.PrefetchScalarGridSpec(num_scalar_prefetch=1, grid=(tiles,), in_specs=in_specs,
                                                     out_specs=out_specs),
        out_shape=[_sds(f, BF16) for f in fulls], name="cast_place",
        compiler_params=_params(("parallel",)))(where, *[arrays[a] for a, _, _ in entries])


def _hand_through(carry):
    given = [] if isinstance(carry, tuple) else [carry]

    def write(token, ins):
        token[...] = ins[-1][...] if given else jnp.zeros_like(token)

    return (given, [pl.BlockSpec(memory_space=pltpu.VMEM)] * len(given),
            _sds(carry if isinstance(carry, tuple) else carry.shape, F32), write)


def gather_start(fulls, kinds, shard_shapes, carry):
    nt = len(fulls)
    given, given_specs, token_type, write = _hand_through(carry)
    n_in = nt + len(given)

    def body(*refs):
        full_refs = refs[:nt]
        send_sems, recv_sems, token = refs[n_in], refs[n_in + 1], refs[-1]
        x, y, c, others = _position()
        for t in range(nt):
            mine = _window(full_refs[t], kinds[t], 2 * x + y, c, shard_shapes[t])
            for j, (ox, oy) in enumerate(others):
                pltpu.make_async_remote_copy(
                    src_ref=mine, dst_ref=mine, send_sem=send_sems.at[3 * t + j], recv_sem=recv_sems.at[3 * t + j],
                    device_id=(ox, oy, c), device_id_type=MESH).start()
        write(token, refs[:n_in])

    sems = pltpu.SemaphoreType.DMA((3 * nt,))
    out = pl.pallas_call(
        body, name="gather_start", in_specs=[HBM_SPEC] * nt + given_specs,
        out_specs=(SEM_SPEC, SEM_SPEC, *[HBM_SPEC] * nt, pl.BlockSpec(memory_space=pltpu.VMEM)),
        out_shape=(sems, sems, *[pltpu.HBM(f.shape, f.dtype) for f in fulls], token_type),
        input_output_aliases={t: 2 + t for t in range(nt)}, compiler_params=_split_params(),
    )(*[_in_hbm(f) for f in fulls], *given)
    return out[0], out[1], list(out[2:2 + nt]), out[-1]


def gather_wait(name, send_sems, recv_sems, fulls, kinds, shard_shapes, after, first):
    nt = len(fulls)
    extra = [] if after is None else [_in_hbm(after)]

    def body(*refs):
        full_refs, send_ref, recv_ref = refs[:nt], refs[nt], refs[nt + 1]
        x, y, c, others = _position()
        for t in range(nt):
            mine = _window(full_refs[t], kinds[t], 2 * x + y, c, shard_shapes[t])
            for j, (ox, oy) in enumerate(others):
                cp = pltpu.make_async_remote_copy(
                    src_ref=mine, dst_ref=_window(full_refs[t], kinds[t], 2 * ox + oy, c, shard_shapes[t]),
                    send_sem=send_ref.at[3 * (first + t) + j], recv_sem=recv_ref.at[3 * (first + t) + j],
                    device_id=(ox, oy, c), device_id_type=MESH)
                cp.wait_send()
                cp.wait_recv()

    out = pl.pallas_call(
        body, name=name, in_specs=[HBM_SPEC] * nt + [SEM_SPEC, SEM_SPEC] + [HBM_SPEC] * len(extra),
        out_specs=[HBM_SPEC] * nt, out_shape=[pltpu.HBM(f.shape, f.dtype) for f in fulls],
        input_output_aliases={t: t for t in range(nt)}, compiler_params=_split_params())(*fulls, send_sems, recv_sems, *extra)
    return list(out)


def forward_halves(name, fulls, kinds, shard_shapes):
    nt = len(fulls)

    def body(*refs):
        out_refs = refs[nt:2 * nt]
        send_sems, recv_sems = refs[2 * nt:]
        x, y, c, others = _position()
        cps = []
        for t in range(nt):
            for j, (ox, oy) in enumerate(others):
                landed = _window(out_refs[t], kinds[t], 2 * ox + oy, c, shard_shapes[t])
                cp = pltpu.make_async_remote_copy(
                    src_ref=landed, dst_ref=landed, send_sem=send_sems.at[3 * t + j], recv_sem=recv_sems.at[3 * t + j],
                    device_id=(x, y, 1 - c), device_id_type=MESH)
                cp.start()
                cps.append(cp)
        for t in range(nt):
            for j, (ox, oy) in enumerate(others):
                got = _window(out_refs[t], kinds[t], 2 * ox + oy, 1 - c, shard_shapes[t])
                pltpu.make_async_remote_copy(
                    src_ref=got, dst_ref=got, send_sem=send_sems.at[3 * t + j], recv_sem=recv_sems.at[3 * t + j],
                    device_id=(x, y, 1 - c), device_id_type=MESH).wait_recv()
        for cp in cps:
            cp.wait_send()

    out = pl.pallas_call(
        body, in_specs=[ANY_SPEC] * nt, out_specs=[ANY_SPEC] * nt, out_shape=[_sds(f.shape, f.dtype) for f in fulls],
        input_output_aliases={t: t for t in range(nt)},
        scratch_shapes=[pltpu.SemaphoreType.DMA((3 * nt,)), pltpu.SemaphoreType.DMA((3 * nt,))],
        name=name, compiler_params=_params())(*fulls)
    return list(out)


def forward_start(name, send_sems, recv_sems, fulls, kinds, shard_shapes, after, first, carry, passing=()):
    nt, n_pass = len(fulls), len(passing)
    given, given_specs, token_type, write = _hand_through(carry)
    n_in = nt + 3 + n_pass + len(given)

    def body(*refs):
        full_refs, ici_send, ici_recv = refs[:nt], refs[nt], refs[nt + 1]
        send_ref, recv_ref, token = refs[n_in], refs[n_in + 1], refs[-1]
        x, y, c, others = _position()
        for t in range(nt):
            mine = _window(full_refs[t], kinds[t], 2 * x + y, c, shard_shapes[t])
            for j, (ox, oy) in enumerate(others):
                landed = _window(full_refs[t], kinds[t], 2 * ox + oy, c, shard_shapes[t])
                cp = pltpu.make_async_remote_copy(
                    src_ref=mine, dst_ref=landed, send_sem=ici_send.at[3 * (first + t) + j],
                    recv_sem=ici_recv.at[3 * (first + t) + j], device_id=(ox, oy, c), device_id_type=MESH)
                cp.wait_send()
                cp.wait_recv()
                pltpu.make_async_remote_copy(
                    src_ref=landed, dst_ref=landed, send_sem=send_ref.at[3 * t + j], recv_sem=recv_ref.at[3 * t + j],
                    device_id=(x, y, 1 - c), device_id_type=MESH).start()
        write(token, refs[:n_in])

    sems = pltpu.SemaphoreType.DMA((3 * nt,))
    out = pl.pallas_call(
        body, name=name, in_specs=[HBM_SPEC] * nt + [SEM_SPEC, SEM_SPEC, HBM_SPEC] + [HBM_SPEC] * n_pass + given_specs,
        out_specs=(SEM_SPEC, SEM_SPEC, *[HBM_SPEC] * (nt + n_pass), pl.BlockSpec(memory_space=pltpu.VMEM)),
        out_shape=(sems, sems, *[pltpu.HBM(f.shape, f.dtype) for f in [*fulls, *passing]], token_type),
        input_output_aliases={**{t: 2 + t for t in range(nt)}, **{nt + 3 + t: 2 + nt + t for t in range(n_pass)}},
        compiler_params=_split_params(),
    )(*fulls, send_sems, recv_sems, _in_hbm(after), *passing, *given)
    return out[0], out[1], list(out[2:2 + nt]), list(out[2 + nt:2 + nt + n_pass]), out[-1]


def forward_wait(name, send_sems, recv_sems, fulls, kinds, shard_shapes, after):
    nt = len(fulls)

    def body(*refs):
        full_refs, send_ref, recv_ref = refs[:nt], refs[nt], refs[nt + 1]
        x, y, c, others = _position()
        for t in range(nt):
            for j, (ox, oy) in enumerate(others):
                cp = pltpu.make_async_remote_copy(
                    src_ref=_window(full_refs[t], kinds[t], 2 * ox + oy, c, shard_shapes[t]),
                    dst_ref=_window(full_refs[t], kinds[t], 2 * ox + oy, 1 - c, shard_shapes[t]),
                    send_sem=send_ref.at[3 * t + j], recv_sem=recv_ref.at[3 * t + j],
                    device_id=(x, y, 1 - c), device_id_type=MESH)
                cp.wait_send()
                cp.wait_recv()

    return list(pl.pallas_call(
        body, name=name, in_specs=[HBM_SPEC] * nt + [SEM_SPEC, SEM_SPEC, HBM_SPEC], out_specs=[HBM_SPEC] * nt,
        out_shape=[pltpu.HBM(f.shape, f.dtype) for f in fulls], input_output_aliases={t: t for t in range(nt)},
        compiler_params=_split_params())(*fulls, send_sems, recv_sems, _in_hbm(after)))


def _piece(ref, kind, chip, shard_shape):
    r, n = shard_shape
    if kind == "col":
        return ref.at[:, pl.ds(pl.multiple_of(chip * n, 128), n)]
    return ref.at[pl.ds(pl.multiple_of(chip * r, 16), r), :]


def _piece_shape(kind, shard_shape):
    r, n = shard_shape
    return (r // 2, n) if kind == "col" else (r, n // 2)


def exchange_start(name, parts, kinds, shard_shapes, carry):
    nt = len(parts)
    lands = [lax.empty((3,) + _piece_shape(kinds[t], shard_shapes[t]), BF16) for t in range(nt)]
    given, given_specs, token_type, write = _hand_through(carry)
    n_in = 2 * nt + len(given)

    def body(*refs):
        part_refs, land_refs = refs[:nt], refs[nt:2 * nt]
        send_sems, recv_sems, token = refs[n_in], refs[n_in + 1], refs[-1]
        x, y, c, others = _position()
        for t in range(nt):
            for j, (ox, oy) in enumerate(others):
                pltpu.make_async_remote_copy(
                    src_ref=_piece(part_refs[t], kinds[t], 2 * ox + oy, shard_shapes[t]), dst_ref=land_refs[t].at[j],
                    send_sem=send_sems.at[3 * t + j], recv_sem=recv_sems.at[3 * t + j],
                    device_id=(ox, oy, c), device_id_type=MESH).start()
        write(token, refs[:n_in])

    sems = pltpu.SemaphoreType.DMA((3 * nt,))
    both = list(parts) + lands
    out = pl.pallas_call(
        body, name=name, in_specs=[HBM_SPEC] * (2 * nt) + given_specs,
        out_specs=(SEM_SPEC, SEM_SPEC, *[HBM_SPEC] * (2 * nt), pl.BlockSpec(memory_space=pltpu.VMEM)),
        out_shape=(sems, sems, *[pltpu.HBM(a.shape, a.dtype) for a in both], token_type),
        input_output_aliases={t: 2 + t for t in range(2 * nt)}, compiler_params=_split_params(),
    )(*[_in_hbm(a) for a in both], *given)
    return out[0], out[1], list(out[2:2 + nt]), list(out[2 + nt:2 + 2 * nt]), out[-1]


def exchange_wait(name, send_sems, recv_sems, parts, lands, kinds, shard_shapes, after):
    nt = len(parts)

    def body(*refs):
        part_refs, land_refs = refs[:nt], refs[nt:2 * nt]
        send_ref, recv_ref = refs[2 * nt], refs[2 * nt + 1]
        x, y, c, others = _position()
        for t in range(nt):
            for j, (ox, oy) in enumerate(others):
                cp = pltpu.make_async_remote_copy(
                    src_ref=_piece(part_refs[t], kinds[t], 2 * ox + oy, shard_shapes[t]), dst_ref=land_refs[t].at[j],
                    send_sem=send_ref.at[3 * t + j], recv_sem=recv_ref.at[3 * t + j],
                    device_id=(ox, oy, c), device_id_type=MESH)
                cp.wait_send()
                cp.wait_recv()

    both = list(parts) + list(lands)
    out = pl.pallas_call(
        body, name=name, in_specs=[HBM_SPEC] * (2 * nt) + [SEM_SPEC, SEM_SPEC, HBM_SPEC], out_specs=[HBM_SPEC] * (2 * nt),
        out_shape=[pltpu.HBM(a.shape, a.dtype) for a in both], input_output_aliases={t: t for t in range(2 * nt)},
        compiler_params=_split_params())(*both, send_sems, recv_sems, _in_hbm(after))
    return list(out[:nt]), list(out[nt:])


def reduce_swap(bufs, wire):
    n = len(bufs)
    halves = [b.shape[0] // 2 for b in bufs]

    def body(*refs):
        in_refs, out_refs, txs, got = refs[:n], refs[n:2 * n], refs[2 * n:3 * n], refs[3 * n:4 * n]
        send_sems, recv_sems = refs[4 * n:]
        x, y, c, _ = _position()
        cps = []
        for k in range(n):
            txs[k][...] = in_refs[k][pl.ds(pl.multiple_of((1 - c) * halves[k], 8), halves[k]), :].astype(wire[k])
            cp = pltpu.make_async_remote_copy(src_ref=txs[k], dst_ref=got[k], send_sem=send_sems.at[k],
                                              recv_sem=recv_sems.at[k], device_id=(x, y, 1 - c), device_id_type=MESH)
            cp.start()
            cps.append(cp)
        for k, cp in enumerate(cps):
            cp.wait()
            own = in_refs[k][pl.ds(pl.multiple_of(c * halves[k], 8), halves[k]), :]
            out_refs[k][...] = (own.astype(wire[k]).astype(F32) + got[k][...].astype(F32)).astype(wire[k])

    vm = pl.BlockSpec(memory_space=pltpu.VMEM)
    parts = [((h, b.shape[1]), w) for h, b, w in zip(halves, bufs, wire)]
    return list(pl.pallas_call(
        body, name="reduce_swap", in_specs=[vm] * n, out_specs=[vm] * n, out_shape=[_sds(sh, w) for sh, w in parts],
        scratch_shapes=[pltpu.VMEM(sh, w) for sh, w in parts] * 2 + [pltpu.SemaphoreType.DMA((n,))] * 2,
        compiler_params=_params())(*bufs))


def reduce_start(parts):
    n = len(parts)
    lands = [lax.empty((4,) + tuple(p.shape), p.dtype) for p in parts]

    def body(*refs):
        part_refs, land_refs, send_sems, recv_sems = refs[:n], refs[n:2 * n], refs[2 * n], refs[2 * n + 1]
        x, y, c, others = _position()
        for k in range(n):
            for j, (ox, oy) in enumerate(others):
                pltpu.make_async_remote_copy(
                    src_ref=part_refs[k], dst_ref=land_refs[k].at[2 * x + y], send_sem=send_sems.at[3 * k + j],
                    recv_sem=recv_sems.at[3 * k + j], device_id=(ox, oy, c), device_id_type=MESH).start()

    sems = pltpu.SemaphoreType.DMA((3 * n,))
    both = list(parts) + lands
    out = pl.pallas_call(
        body, name="reduce_start", in_specs=[HBM_SPEC] * (2 * n), out_specs=(SEM_SPEC, SEM_SPEC, *[HBM_SPEC] * (2 * n)),
        out_shape=(sems, sems, *[pltpu.HBM(a.shape, a.dtype) for a in both]),
        input_output_aliases={k: 2 + k for k in range(2 * n)}, compiler_params=_split_params(),
    )(*[_in_hbm(a) for a in both])
    return out[0], out[1], list(out[2:2 + n]), list(out[2 + n:])


def reduce_wait(send_sems, recv_sems, parts, lands, after):
    n = len(parts)

    def body(*refs):
        part_refs, land_refs, send_ref, recv_ref = refs[:n], refs[n:2 * n], refs[2 * n], refs[2 * n + 1]
        x, y, c, others = _position()
        for k in range(n):
            for j, (ox, oy) in enumerate(others):
                cp = pltpu.make_async_remote_copy(
                    src_ref=part_refs[k], dst_ref=land_refs[k].at[2 * ox + oy], send_sem=send_ref.at[3 * k + j],
                    recv_sem=recv_ref.at[3 * k + j], device_id=(ox, oy, c), device_id_type=MESH)
                cp.wait_send()
                cp.wait_recv()

    both = list(parts) + list(lands)
    out = pl.pallas_call(
        body, name="reduce_wait", in_specs=[HBM_SPEC] * (2 * n) + [SEM_SPEC, SEM_SPEC, HBM_SPEC],
        out_specs=[HBM_SPEC] * (2 * n), out_shape=[pltpu.HBM(a.shape, a.dtype) for a in both],
        input_output_aliases={k: k for k in range(2 * n)}, compiler_params=_split_params(),
    )(*both, send_sems, recv_sems, _in_hbm(after))
    return list(out[:n]), list(out[n:])


def reduce_share(parts, lands):
    n = len(parts)
    halves = [p.shape[0] for p in parts]

    def body(*refs):
        part_refs, land_refs, out_refs = refs[:n], refs[n:2 * n], refs[2 * n:3 * n]
        send_sems, recv_sems = refs[3 * n:]
        x, y, c, _ = _position()
        chip = 2 * x + y
        cps = []
        for k in range(n):
            mine = pl.ds(pl.multiple_of(c * halves[k], 8), halves[k])
            own = part_refs[k][...].astype(F32)
            total = jnp.where(chip == 0, own, land_refs[k][0].astype(F32))
            for entry in range(1, 4):
                total = total + jnp.where(chip == entry, own, land_refs[k][entry].astype(F32))
            out_refs[k][mine, :] = total
            cp = pltpu.make_async_remote_copy(
                src_ref=out_refs[k].at[mine], dst_ref=out_refs[k].at[mine], send_sem=send_sems.at[k],
                recv_sem=recv_sems.at[k], device_id=(x, y, 1 - c), device_id_type=MESH)
            cp.start()
            cps.append(cp)
        for cp in cps:
            cp.wait()

    vm = pl.BlockSpec(memory_space=pltpu.VMEM)
    return list(pl.pallas_call(
        body, name="reduce_share", in_specs=[vm] * (2 * n), out_specs=[vm] * n,
        out_shape=[_sds((2 * p.shape[0], p.shape[1]), F32) for p in parts],
        scratch_shapes=[pltpu.SemaphoreType.DMA((n,))] * 2, compiler_params=_params())(*parts, *lands))


def _local_step(x, target, small, need, ahead, emit_swap, emit_exchange):
    d = D_MODEL
    full = {}

    def handed(vec, token):
        return vec if token is None else token

    def token_rows(token):
        return [] if token is None else [token]

    def plus(acc, rows):
        return acc + rows[0] if rows else acc

    rb16, rbt16, rc16, rct16, lr_t, li_t = small["s5_operands"]
    ge, ge_slope, cs = s5_fwd(x, small["norm_mix0"], small["s5_d"], rb16, rc16, lr_t, li_t)
    full.update(need("glu", ge))

    def norm_rows(h, gains):
        xh, _ = _rms_hat(h)
        return [xh * g for g in gains]

    def glu_epilogue(accs, e, r):
        v, gt = accs[0] + r[0], accs[1] + r[1]
        h = e[0] + v * jax.nn.sigmoid(gt)
        return [h, v, gt] + norm_rows(h, r[2:])

    gain_mlp0 = handed(small["norm_mlp0"], ahead("mlp_in0", full["w_glu"], small["norm_mlp0"]))
    h1, val, gate, n1 = mm_nn(
        "glu", ge, full["w_glu"], [0, d], d, glu_epilogue, [F32, F32, F32, BF16], extras=[x],
        rowvecs=[(small["s5_b_glu"], 0), (small["s5_b_glu"], d), (gain_mlp0, 0)], tm=512, tn=d)

    def mlp_fwd(tag, h, n, w_in, get_w_out, next_gains, head=None):
        def in_epilogue(accs, e, rv):
            pos = jnp.maximum(accs[0], 0.0)
            return [pos * pos, 2.0 * pos]

        r, slope = mm_nn("mlp_in" + tag, n, w_in, [0], w_in.shape[1], in_epilogue, [BF16, BF16], tm=2048)
        w_out = get_w_out(r)

        def epilogue(accs, e, rv):
            h_out = e[0] + accs[0]
            return [h_out] + norm_rows(h_out, rv)

        if head is not None:
            return head(r, w_out, h), (n, r, slope)
        outs = mm_nn("mlp_out" + tag, r, w_out, [0], d, epilogue, [F32] + [BF16] * len(next_gains), extras=[h],
                     rowvecs=[(g, 0) for g in next_gains], tm=512, tn=d)
        return outs[0], outs[1:], (n, r, slope)

    full.update(need("mlp_in0", h1))

    def w_out0(after):
        full.update(need("mlp_out0", after))
        return full["w_out0"]

    h2, (nkv, n2), mlp0 = mlp_fwd("0", h1, n1, full["w_in0"], w_out0, [small["norm_kv"], small["norm_mix1"]])

    full.update(need("attn", h2))
    kvw = 2 * N_KV * HEAD_DIM
    (kv,) = mm_nn("kv_proj", nkv, full["w_kv"], [0], kvw, lambda accs, e, r: [accs[0] + r[0]], [BF16],
                  rowvecs=[(small["b_kv"], 0)], tm=2048)
    (q,) = mm_nn("q_proj", n2, full["w_q"], [0], d, lambda accs, e, r: [accs[0] + r[0]], [BF16],
                 rowvecs=[(small["b_q"], 0)], tm=2048)
    sinks = small["sinks"].reshape(N_Q)
    o = attn_fwd(q, kv, sinks)
    def o_epilogue(accs, e, r):
        h_out = e[0] + accs[0] + r[0]
        return [h_out] + norm_rows(h_out, r[1:])

    bias_o = handed(small["b_o"], ahead("mlp_in1", o, small["b_o"]))
    h3, n3 = mm_nn("o_proj", o, full["w_o"], [0], d, o_epilogue, [F32, BF16], extras=[h2],
                   rowvecs=[(bias_o, 0), (small["norm_mlp1"], 0)], tm=512, tn=d)
    full.update(need("mlp_in1", h3, then="mlp_out1"))

    def w_out1(after):
        full.update(need("mlp_out1", after))
        return full["w_out1"]

    def loss_head(r, w_out, h):
        def epilogue(accs, e, rv):
            xh, rr = _rms_hat(e[0] + accs[0])
            err = xh * rv[0] - e[1]
            dy = err * (1.0 / d)
            dxh = dy * rv[0]
            dx = rr * (dxh - xh * jnp.mean(dxh * xh, axis=-1, keepdims=True))
            loss = jnp.full((1, d), 0.5 * jnp.sum(jnp.mean(err * err, axis=-1, keepdims=True)), F32)
            return [dx, dx, loss, jnp.sum(dy * xh, axis=0, keepdims=True)]

        return mm_nn("mlp_out1", r, w_out, [0], d, epilogue, [F32, BF16], extras=[h, target],
                     rowvecs=[(small["norm_final"], 0)], n_sums=2, tm=512, tn=d)

    (dh, dhb, loss_tile, dg_final), mlp1 = mlp_fwd("1", h3, n3, full["w_in1"], w_out1, [], head=loss_head)

    grads_small, grads_full = {"norm_final": dg_final}, {}
    ident = lambda acc, e, r: [plus(acc, r)]
    layer1 = ["w_out1", "w_in1", "w_o", "w_q", "w_kv"]
    layer0 = ["w_out0", "w_in0", "w_glu"]

    def norm_bwd_rows(x_rows, res, dys, gains):
        xh, r = _rms_hat(x_rows)
        dxh = sum(dy * g for dy, g in zip(dys, gains))
        dx = r * (dxh - xh * jnp.mean(dxh * xh, axis=-1, keepdims=True)) + res
        return dx, [jnp.sum(dy * xh, axis=0, keepdims=True) for dy in dys]

    def mlp_bwd(tag, dh, dhb, h_in, gain, w_in, w_out, saved, token=None):
        n, r, slope = saved
        (da,) = mm_nt("mlp_da" + tag, dhb, w_out, lambda acc, e, rv: [plus(acc * e[0].astype(F32), rv)], [BF16],
                      extras=[slope], rowvecs=token_rows(token), tm=2048)
        grads_full["w_in" + tag], grads_full["w_out" + tag] = mm_tn_many("dw_mlp" + tag, [(n, da), (r, dhb)])

        def epilogue(acc, e, rv):
            dx, dgs = norm_bwd_rows(e[0], e[1], [acc], rv)
            return [dx, dx, jnp.sum(dx, axis=0, keepdims=True)] + dgs

        dx, dxb, colsum, dg = mm_nt("mlp_dn" + tag, da, w_in, epilogue, [F32, BF16], extras=[h_in, dh], rowvecs=[gain],
                                    n_sums=2, tm=512, tk=d)
        grads_small["norm_mlp" + tag] = dg
        return dx, dxb, colsum

    dh3, dh3b, colsum3 = mlp_bwd("1", dh, dhb, h3, small["norm_mlp1"], full["w_in1"], full["w_out1"], mlp1)
    grads_small["b_o"] = colsum3
    grads_full["w_o"] = mm_tn("dw_o", o, dh3b, tn=1024)
    (do,) = mm_nt("attn_do", dh3b, full["w_o"], ident, [BF16], tm=2048)
    dq, dbq, dprev, dcur, dsink = attn_bwd(q, kv, do, sinks)
    dkv, dbkv = kv_combine(dprev, dcur)
    grads_small["b_q"], grads_small["b_kv"], grads_small["sinks"] = dbq, dbkv, dsink
    grads_full["w_q"] = mm_tn("dw_q", n2, dq, tn=1024)
    grads_full["w_kv"] = mm_tn("dw_kv", nkv, dkv, tk=1024)
    token = emit_swap("layer1", {n: grads_full[n] for n in layer1}, (1, d))
    (dnkv,) = mm_nt("kv_dn", dkv, full["w_kv"], ident, [F32], rowvecs=token_rows(token), tm=2048, tk=1024)

    def attn_dn_epilogue(acc, e, rv):
        dx, dgs = norm_bwd_rows(e[0], e[1], [acc, e[2]], rv)
        return [dx, dx] + dgs

    dh2, dh2b, dg_mix1, dg_kv = mm_nt("attn_dn", dq, full["w_q"], attn_dn_epilogue, [F32, BF16], extras=[h2, dh3, dnkv],
                                      rowvecs=[small["norm_mix1"], small["norm_kv"]], n_sums=2, tm=512, tk=d)
    grads_small["norm_mix1"], grads_small["norm_kv"] = dg_mix1, dg_kv
    token = emit_exchange("layer1", dh2b, (1, full["w_out0"].shape[0]))
    dh1, _, _ = mlp_bwd("0", dh2, dh2b, h1, small["norm_mlp0"], full["w_in0"], full["w_out0"], mlp0, token)

    dz, db_glu = glu_bwd(dh1, val, gate)
    grads_small["s5_b_glu"] = db_glu
    grads_full["w_glu"] = mm_tn("dw_glu", ge, dz, tn=1024)
    token = emit_swap("layer0", {n: grads_full[n] for n in layer0}, (1, d))
    (dy2,) = mm_nt("glu_dy", dz, full["w_glu"], lambda acc, e, rv: [plus(acc, rv) * e[0]], [F32], extras=[ge_slope],
                   rowvecs=token_rows(token), tm=512, tk=1024)
    d_skip = handed(small["s5_d"], emit_exchange("layer0", dy2, small["s5_d"]))
    grad_x, dd, drb, drc, dlr, dli, dg_mix0 = s5_bwd(x, small["norm_mix0"], dy2, dh1, d_skip, cs, rb16, rbt16, rct16, lr_t, li_t)
    grads_small["s5_d"] = dd
    grads_small["s5_mats"] = (drb, drc, dlr, dli)
    grads_small["norm_mix0"] = dg_mix0
    return loss_tile, grad_x, grads_small


SMALL_NAMES = ["norm_mix", "norm_mlp", "norm_kv", "norm_final", "s5_a_re", "s5_a_im", "s5_log_dt", "s5_b_re", "s5_b_im",
               "s5_c_re", "s5_c_im", "s5_d", "s5_b_glu", "b_kv", "b_q", "sinks", "b_o"]
BIG_NAMES = ["s5_w_glu", "w_kv", "w_q", "w_o", "w_mlp_in", "w_mlp_out"]
WEIGHT_ORDER = ["norm_mix", "norm_mlp", "norm_kv", "norm_final", "s5_a_re", "s5_a_im", "s5_log_dt", "s5_b_re", "s5_b_im",
                "s5_c_re", "s5_c_im", "s5_d", "s5_w_glu", "s5_b_glu", "w_kv", "b_kv", "w_q", "b_q", "sinks", "w_o", "b_o",
                "w_mlp_in", "w_mlp_out"]


def kernel(x, norm_mix, norm_mlp, norm_kv, norm_final, s5_a_re, s5_a_im, s5_log_dt, s5_b_re, s5_b_im, s5_c_re, s5_c_im, s5_d, s5_w_glu, s5_b_glu, w_kv, b_kv, w_q, b_q, sinks, w_o, b_o, w_mlp_in, w_mlp_out, loss_target, m_norm_mix, m_norm_mlp, m_norm_kv, m_norm_final, m_s5_a_re, m_s5_a_im, m_s5_log_dt, m_s5_b_re, m_s5_b_im, m_s5_c_re, m_s5_c_im, m_s5_d, m_s5_w_glu, m_s5_b_glu, m_w_kv, m_b_kv, m_w_q, m_b_q, m_sinks, m_w_o, m_b_o, m_w_mlp_in, m_w_mlp_out, v_norm_mix, v_norm_mlp, v_norm_kv, v_norm_final, v_s5_a_re, v_s5_a_im, v_s5_log_dt, v_s5_b_re, v_s5_b_im, v_s5_c_re, v_s5_c_im, v_s5_d, v_s5_w_glu, v_s5_b_glu, v_w_kv, v_b_kv, v_w_q, v_b_q, v_sinks, v_w_o, v_b_o, v_w_mlp_in, v_w_mlp_out):
    env = dict(locals())
    w = {n: env[n] for n in WEIGHT_ORDER}
    mom = {n: env["m_" + n] for n in WEIGHT_ORDER}
    var = {n: env["v_" + n] for n in WEIGHT_ORDER}
    d = D_MODEL
    xi, yi, ci = lax.axis_index("x"), lax.axis_index("y"), lax.axis_index("c")
    chip = 2 * xi + yi
    where = jnp.stack([ci, chip]).astype(jnp.int32)

    dsh, bsh = s5_d.shape[1], s5_b_glu.shape[1]
    packed = jnp.concatenate([s5_d.reshape(-1, 128), s5_b_glu.reshape(-1, 128)])
    n_d, n_b = dsh // 128, bsh // 128
    slab = lax.dynamic_update_slice(jnp.zeros((4, 8, 128), F32), jnp.pad(packed, ((0, 8 - n_d - n_b), (0, 0)))[None],
                                    (chip, 0, 0))

    big = [s5_w_glu, w_kv[None], w_q, w_o, w_mlp_in, w_mlp_out]
    entries = [(0, 0, "col"), (1, 0, "row"), (2, 0, "row"), (3, 0, "row"), (4, 0, "col"), (4, 1, "col"),
               (5, 0, "row"), (5, 1, "row")]
    names = ["w_glu", "w_kv", "w_q", "w_o", "w_in0", "w_in1", "w_out0", "w_out1"]
    kinds = dict(zip(names, [k for _, _, k in entries]))
    shard_shapes = dict(zip(names, [tuple(big[a].shape[1:]) for a, _, _ in entries]))

    placed_w = dict(zip(names, cast_place(big, entries, where)))
    placed_w["vectors"], kinds["vectors"], shard_shapes["vectors"] = slab, "slab", None
    gather_groups = {"glu": ["w_glu"], "mlp_in0": ["w_in0"], "mlp_out0": ["w_out0"], "attn": ["w_kv", "w_q", "w_o"],
                     "mlp_in1": ["w_in1"], "mlp_out1": ["w_out1"]}
    order = ["vectors"] + [n for members in gather_groups.values() for n in members]
    send, recv, thru, log_dt = gather_start([placed_w[n] for n in order], [kinds[n] for n in order],
                                            [shard_shapes[n] for n in order], s5_log_dt)
    started = dict(zip(order, thru))
    (gathered_rows,) = gather_wait("gather_wait_vectors", send, recv, [started["vectors"]], ["slab"], [None], None, 0)
    d_full = gathered_rows[:, 0:n_d].reshape(1, -1)
    bglu_full = gathered_rows[:, n_d:n_d + n_b].reshape(1, -1)

    forwarding = {}

    def ahead(group, after, carry, passing=()):
        members = gather_groups[group]
        ks, shapes = [kinds[n] for n in members], [shard_shapes[n] for n in members]
        d2d_send, d2d_recv, landed, passed, tok = forward_start(
            "forward_start_" + group, send, recv, [started[n] for n in members], ks, shapes, after,
            order.index(members[0]), carry, passing)
        forwarding[group] = (d2d_send, d2d_recv, landed)
        return passed if passing else tok

    def need(group, after, then=None):
        members = gather_groups[group]
        ks, shapes = [kinds[n] for n in members], [shard_shapes[n] for n in members]
        if group in forwarding:
            arrays = forward_wait("forward_wait_" + group, *forwarding[group], ks, shapes, after)
        else:
            landed = gather_wait("gather_wait_" + group, send, recv, [started[n] for n in members], ks, shapes, after,
                                 order.index(members[0]))
            arrays = forward_halves("forward_halves_" + group, landed, ks, shapes)
        if then is not None:
            arrays = ahead(then, after, (8, 128), arrays)
        return dict(zip(members, arrays))

    swapping, exchanging = {}, {}

    def emit_swap(group, partial, carry):
        members = list(partial)
        send, recv, mine, lands, tok = swap_start("swap_start_" + group, [partial[n] for n in members],
                                                  [kinds[n] for n in members], carry)
        swapping[group] = (members, send, recv, mine, lands)
        return tok

    def emit_exchange(group, after, carry):
        members, send, recv, mine, lands = swapping[group]
        ks, shapes = [kinds[n] for n in members], [shard_shapes[n] for n in members]
        mine, landed = swap_wait("swap_wait_" + group, send, recv, mine, lands, ks, after)
        sums = add_halves("add_halves_" + group, mine, landed, ks, where)
        send, recv, parts, lands, tok = exchange_start("exchange_start_" + group, sums, ks, shapes, carry)
        exchanging[group] = (members, send, recv, parts, lands)
        return tok

    s5_args = (s5_a_re[0], s5_a_im[0], log_dt[0], s5_b_re[0], s5_b_im[0])
    small = {
        "norm_mix0": norm_mix[0:1], "norm_mix1": norm_mix[1:2], "norm_mlp0": norm_mlp[0:1], "norm_mlp1": norm_mlp[1:2],
        "norm_kv": norm_kv.reshape(1, d), "norm_final": norm_final.reshape(1, d), "s5_operands": s5_prep(*s5_args, s5_c_re[0], s5_c_im[0]),
        "s5_d": d_full, "s5_b_glu": bglu_full,
        "b_kv": b_kv.reshape(1, -1), "b_q": b_q, "sinks": sinks, "b_o": b_o,
    }
    loss_row, grad_x, gs = _local_step(x[0], loss_target[0], small, need, ahead, emit_swap, emit_exchange)

    mats, lams = s5_compact(*gs["s5_mats"])
    rows = [gs["norm_mix0"], gs["norm_mix1"], gs["norm_mlp0"], gs["norm_mlp1"], gs["norm_kv"], gs["norm_final"], gs["s5_d"],
            gs["b_q"], gs["b_o"], gs["s5_b_glu"], gs["b_kv"], gs["sinks"], loss_row, jnp.zeros((2, d), F32)]
    small_send, small_recv, small_parts, small_lands = reduce_start(
        reduce_swap([jnp.concatenate(rows, axis=0), lams, mats], [F32, F32, BF16]))

    reduced = [None] * len(big)
    where_of = dict(zip(names, entries))
    for group in ("layer1", "layer0"):
        members, send, recv, parts, lands = exchanging[group]
        ks, shapes = [kinds[n] for n in members], [shard_shapes[n] for n in members]
        parts, lands = exchange_wait("exchange_wait_" + group, send, recv, parts, lands, ks, shapes, small_lands[-1])
        targets = [where_of[n][0] for n in members]
        sums = sum_shards("sum_shards_" + group, parts, lands, ks, shapes, where, [where_of[n][1] for n in members],
                          [big[a].shape[0] for a in targets], [reduced[a] for a in targets])
        for a, arr in zip(targets, sums):
            reduced[a] = arr
    share_send, share_recv, reduced, _ = share_start(reduced, entries, (8, 128))

    vecs, lams, mats = reduce_share(*reduce_wait(small_send, small_recv, small_parts, small_lands, reduced[0]))
    grads = split_vectors(where, vecs, dsh, bsh)
    loss = grads.pop("loss")[0, 0]
    g_are, g_aim, g_dt, g_bre, g_bim, dc_re, dc_im = s5_param_bwd(mats, lams, *s5_args)
    grads.update({"s5_a_re": g_are[None], "s5_a_im": g_aim[None], "s5_log_dt": g_dt[None], "s5_b_re": g_bre[None],
                  "s5_b_im": g_bim[None], "s5_c_re": dc_re[None], "s5_c_im": dc_im[None]})

    delta, new_m, new_v = {}, {}, {}

    def view(n, a):
        return a.reshape(1, -1) if a.ndim == 1 else jnp.swapaxes(a, -1, -2) if n in ("s5_b_re", "s5_b_im") else a

    sw, sg, sm, sv = ([view(n, t[n]) for n in SMALL_NAMES] for t in (w, grads, mom, var))
    for n, a, b, c_ in zip(SMALL_NAMES, *adamw_native("adamw_small", sw, sg, sm, sv)):
        delta[n], new_m[n], new_v[n] = (view(n, t) if t.ndim == 4 else t for t in (a, b, c_))

    reduced = share_wait(share_send, share_recv, reduced, entries, new_v["s5_c_re"])
    for n, g in zip(BIG_NAMES, reduced):
        grads[n] = g.reshape(w[n].shape)
    flat = lambda t: [t[n].reshape(-1, t[n].shape[-1]) for n in BIG_NAMES]
    for table, arrays in zip((grads, delta, new_m, new_v), adamw("adamw_big", flat(w), flat(grads), flat(mom), flat(var))):
        for n, a in zip(BIG_NAMES, arrays):
            table[n] = a.reshape(w[n].shape)

    out = [loss.reshape(()), grad_x[None]]
    for table in (grads, delta, new_m, new_v):
        out += [table[n].reshape(w[n].shape) for n in WEIGHT_ORDER]
    return tuple(out)
```

```python
import math

import jax
import jax.numpy as jnp
from jax import lax
from jax.experimental import pallas as pl
from jax.experimental.pallas import tpu as pltpu

F32 = jnp.float32
BF16 = jnp.bfloat16

D_MODEL = 1024
S5_GROUPS = 64
S5_GROUP = 16
S5_STATE = 64
N_KV = 4
N_Q = 16
HEAD_DIM = 64
BLOCK = 128
NORM_EPS = 1e-5
LAMBDA_RE_MAX = -1e-4
ADAM_LR, ADAM_B1, ADAM_B2, ADAM_EPS, ADAM_WD, ADAM_STEP = 0.001, 0.9, 0.999, 1e-08, 0.01, 10

VMEM_LIMIT_BYTES = 56 * 1024 * 1024
S5_CHUNK = 256
S5_BLOCKS = 4
MESH = pl.DeviceIdType.MESH


def _params(sem=None):
    return pltpu.CompilerParams(dimension_semantics=sem, vmem_limit_bytes=VMEM_LIMIT_BYTES)


def _sds(shape, dtype):
    return jax.ShapeDtypeStruct(shape, dtype)


def _rms_hat(xv):
    r = lax.rsqrt(jnp.mean(xv * xv, axis=-1, keepdims=True) + NORM_EPS)
    return xv * r, r


def mm_nn(name, a, w, col_offsets, n_out, epilogue, out_dtypes, extras=(), rowvecs=(), n_sums=0, tm=1024, tn=512):
    m, k = a.shape
    tm, tn = min(tm, m), min(tn, n_out)
    nw, ne, nr, no = len(col_offsets), len(extras), len(rowvecs), len(out_dtypes)

    def body(a_ref, *refs):
        w_refs, e_refs, r_refs = refs[:nw], refs[nw:nw + ne], refs[nw + ne:nw + ne + nr]
        o_refs, s_refs = refs[nw + ne + nr:nw + ne + nr + no], refs[nw + ne + nr + no:]
        av = a_ref[...]
        accs = [jnp.dot(av, w_ref[...], preferred_element_type=F32) for w_ref in w_refs]
        outs = epilogue(accs, [e[...] for e in e_refs], [r[...] for r in r_refs])
        for o_ref, o in zip(o_refs, outs[:no]):
            o_ref[...] = o.astype(o_ref.dtype)
        if n_sums:
            @pl.when(pl.program_id(1) == 0)
            def _():
                for s_ref in s_refs:
                    s_ref[...] = jnp.zeros_like(s_ref)

            for s_ref, val in zip(s_refs, outs[no:]):
                s_ref[...] += val

    def wspec(off):
        return pl.BlockSpec((k, tn), lambda j, i, off=off: (0, off // tn + j))

    def rspec(off):
        return pl.BlockSpec((1, tn), lambda j, i, off=off: (0, off // tn + j))

    tile = pl.BlockSpec((tm, tn), lambda j, i: (i, j))
    in_specs = ([pl.BlockSpec((tm, k), lambda j, i: (i, 0))] + [wspec(o) for o in col_offsets]
                + [tile] * ne + [rspec(o) for _, o in rowvecs])
    sem = ("parallel", "arbitrary") if n_sums else ("parallel", "parallel")
    return pl.pallas_call(
        body, grid=(n_out // tn, m // tm), in_specs=in_specs,
        out_specs=[tile] * no + [pl.BlockSpec((1, tn), lambda j, i: (0, j))] * n_sums,
        out_shape=[_sds((m, n_out), dt) for dt in out_dtypes] + [_sds((1, n_out), F32)] * n_sums, name=name,
        compiler_params=_params(sem))(a, *([w] * nw), *extras, *[r for r, _ in rowvecs])


def mm_nt(name, g, w, epilogue, out_dtypes, extras=(), rowvecs=(), n_sums=0, tm=512, tk=512):
    m, n = g.shape
    k = w.shape[0]
    tm, tk = min(tm, m), min(tk, k)
    ne, nr, no = len(extras), len(rowvecs), len(out_dtypes)
    steps = m // tm
    ring = k == tk and steps >= 3

    def body(g_ref, w_ref, *refs):
        if ring:
            *refs, g_tiles, g_sems = refs
            i = pl.program_id(0)

            def fetch(row, slot):
                return pltpu.make_async_copy(g_ref.at[pl.ds(row, tm), :], g_tiles.at[slot], g_sems.at[slot])

            @pl.when(i == 0)
            def _():
                fetch(0, 0).start()
                fetch(tm, 1).start()

            @pl.when(i + 2 < steps)
            def _():
                fetch(pl.multiple_of((i + 2) * tm, tm), (i + 2) % 3).start()

            fetch(pl.multiple_of(i * tm, tm), i % 3).wait()
            gv = g_tiles[i % 3]
        else:
            gv = g_ref[...]
        e_refs, r_refs, o_refs, s_refs = refs[:ne], refs[ne:ne + nr], refs[ne + nr:ne + nr + no], refs[ne + nr + no:]
        acc = lax.dot_general(gv, w_ref[...], (((1,), (1,)), ((), ())), preferred_element_type=F32)
        outs = epilogue(acc, [e[...] for e in e_refs], [r[...] for r in r_refs])
        for o_ref, o in zip(o_refs, outs[:no]):
            o_ref[...] = o.astype(o_ref.dtype)
        if n_sums:
            @pl.when(pl.program_id(0) == 0)
            def _():
                for s_ref in s_refs:
                    s_ref[...] = jnp.zeros_like(s_ref)

            for s_ref, val in zip(s_refs, outs[no:]):
                s_ref[...] += val

    tile = pl.BlockSpec((tm, tk), lambda i, j: (i, j))
    vec = pl.BlockSpec((1, tk), lambda i, j: (0, j))
    g_spec = pl.BlockSpec(memory_space=pl.ANY) if ring else pl.BlockSpec((tm, n), lambda i, j: (i, 0))
    sem = ("arbitrary", "arbitrary") if ring else ("arbitrary", "parallel") if n_sums else ("parallel", "parallel")
    return pl.pallas_call(
        body, grid=(steps, k // tk),
        in_specs=[g_spec, pl.BlockSpec((tk, n), lambda i, j: (j, 0))] + [tile] * ne + [vec] * nr,
        out_specs=[tile] * no + [vec] * n_sums,
        out_shape=[_sds((m, k), dt) for dt in out_dtypes] + [_sds((1, k), F32)] * n_sums, name=name,
        scratch_shapes=[pltpu.VMEM((3, tm, n), g.dtype), pltpu.SemaphoreType.DMA((3,))] if ring else [],
        compiler_params=_params(sem))(g, w, *extras, *rowvecs)


def mm_tn(name, a, g, tk=512, tn=512):
    m, k = a.shape
    n = g.shape[1]
    tk, tn = min(tk, k), min(tn, n)

    def body(a_ref, g_ref, o_ref):
        acc = lax.dot_general(a_ref[...], g_ref[...], (((0,), (0,)), ((), ())), preferred_element_type=F32)
        o_ref[...] = acc.astype(o_ref.dtype)

    return pl.pallas_call(
        body, grid=(k // tk, n // tn),
        in_specs=[pl.BlockSpec((m, tk), lambda i, j: (0, i)), pl.BlockSpec((m, tn), lambda i, j: (0, j))],
        out_specs=pl.BlockSpec((tk, tn), lambda i, j: (i, j)), out_shape=_sds((k, n), BF16), name=name,
        compiler_params=_params(("parallel", "parallel")))(a, g)


def _row_mask(tc):
    row = lax.broadcasted_iota(jnp.int32, (8 * tc, 256), 0) % 8
    col = lax.broadcasted_iota(jnp.int32, (8 * tc, 256), 1) // 32
    return row == col


def _expand_rows(val, mask):
    tc, width = val.shape
    rep = jnp.broadcast_to(val[:, None, :], (tc, 8, width)).reshape(8 * tc, width)
    return jnp.where(mask, rep, 0.0).astype(BF16)


def _stage(ref, val):
    ref[0] = val[:, 0:128]
    ref[1] = val[:, 128:256]


def _gather_rows(src_ref, tc):
    halves = []
    for half in range(2):
        col = lax.broadcasted_iota(jnp.int32, (tc, 128), 1) // 32 + 4 * half
        out = jnp.zeros((tc, 128), F32)
        for s8 in range(4 * half, 4 * half + 4):
            out = jnp.where(col == s8, src_ref.at[half][pl.ds(s8, tc, stride=8), :], out)
        halves.append(out)
    return jnp.concatenate(halves, axis=1)


def _repeat(n, by, step, carry):
    def trip(i, c):
        for j in range(by):
            c = step(i * by + j, c)
        return c

    return lax.fori_loop(0, n // by, trip, carry)


def _gelu_and_slope(x):
    c = math.sqrt(2.0 / math.pi)
    t = jnp.tanh(c * (x + 0.044715 * x * x * x))
    return 0.5 * x * (1.0 + t), 0.5 * (1.0 + t) + 0.5 * x * (1.0 - t * t) * c * (1.0 + 3.0 * 0.044715 * x * x)


def s5_fwd(x, gain, d_skip, rb, rc, lam_r, lam_i):
    n_rows = x.shape[0]
    tc = min(S5_CHUNK, n_rows)
    nc = n_rows // tc

    def body(x_ref, g_ref, d_ref, rb_ref, rc_ref, lr_ref, li_ref, ge_ref, slope_ref, cs_ref, bux, yrows, carry):
        i = pl.program_id(0)
        u = _rms_hat(x_ref[...])[0] * g_ref[...]

        @pl.when(i == 0)
        def _():
            carry[...] = jnp.zeros_like(carry)

        cs_ref[0] = carry[...]
        mask = _row_mask(tc)
        for blk in range(S5_BLOCKS):
            lhs = _expand_rows(u[:, blk * 256:(blk + 1) * 256], mask)
            bux[blk] = jnp.dot(lhs, rb_ref[blk], preferred_element_type=F32)
        lam = [(lr_ref[blk], li_ref[blk]) for blk in range(S5_BLOCKS)]

        def step(t, c):
            r0 = pl.multiple_of(t * 8, 8)
            new = []
            for blk in range(S5_BLOCKS):
                xr, xi = c[2 * blk], c[2 * blk + 1]
                lr, li = lam[blk]
                nr = lr * xr - li * xi + bux[blk, pl.ds(r0, 8), 0:128]
                ni = lr * xi + li * xr + bux[blk, pl.ds(r0, 8), 128:256]
                bux[blk, pl.ds(r0, 8), 0:128] = nr
                bux[blk, pl.ds(r0, 8), 128:256] = ni
                new += [nr, ni]
            return tuple(new)

        c0 = []
        for blk in range(S5_BLOCKS):
            c0 += [carry[blk, :, 0:128], carry[blk, :, 128:256]]
        cn = _repeat(tc, 8, step, tuple(c0))
        for blk in range(S5_BLOCKS):
            carry[blk, :, 0:128] = cn[2 * blk]
            carry[blk, :, 128:256] = cn[2 * blk + 1]
        for blk in range(S5_BLOCKS):
            _stage(yrows, jnp.dot(bux[blk].astype(BF16), rc_ref[blk], preferred_element_type=F32))
            sl = slice(blk * 256, (blk + 1) * 256)
            ge, slope = _gelu_and_slope(_gather_rows(yrows, tc) + d_ref[:, sl] * u[:, sl])
            slope_ref[:, sl] = slope
            ge_ref[:, sl] = ge.astype(BF16)

    row = pl.BlockSpec((tc, D_MODEL), lambda i: (i, 0))
    vec = pl.BlockSpec((1, D_MODEL), lambda i: (0, 0))
    mat = pl.BlockSpec((S5_BLOCKS, 256, 256), lambda i: (0, 0, 0))
    lamspec = pl.BlockSpec((S5_BLOCKS, 8, 128), lambda i: (0, 0, 0))
    return pl.pallas_call(
        body, grid=(nc,),
        in_specs=[row, vec, vec, mat, mat, lamspec, lamspec],
        out_specs=[row, row, pl.BlockSpec((1, S5_BLOCKS, 8, 256), lambda i: (i, 0, 0, 0))],
        out_shape=[_sds((n_rows, D_MODEL), BF16), _sds((n_rows, D_MODEL), F32), _sds((nc, S5_BLOCKS, 8, 256), F32)],
        scratch_shapes=[pltpu.VMEM((S5_BLOCKS, 8 * tc, 256), F32), pltpu.VMEM((2, 8 * tc, 128), F32),
                        pltpu.VMEM((S5_BLOCKS, 8, 256), F32)],
        name="s5_fwd", compiler_params=_params(("arbitrary",)))(x, gain, d_skip, rb, rc, lam_r, lam_i)


def s5_bwd(x, gain, dy2, res, d_skip, cs, rb, rbt, rct, lam_r, lam_i):
    n_rows = x.shape[0]
    tc = min(S5_CHUNK, n_rows)
    nc = n_rows // tc

    def body(x_ref, g_ref, dy_ref, res_ref, d_ref, cs_ref, rb_ref, rbt_ref, rct_ref, lr_ref, li_ref,
             dx_ref, dd_ref, drb_ref, drc_ref, dlr_ref, dli_ref, dg_ref, tmp, du, lhsu, lhsd, xs, adj, acarry):
        i = pl.program_id(0)
        u = _rms_hat(x_ref[...])[0] * g_ref[...]

        @pl.when(i == 0)
        def _():
            acarry[...] = jnp.zeros_like(acarry)
            dd_ref[...] = jnp.zeros_like(dd_ref)
            drb_ref[...] = jnp.zeros_like(drb_ref)
            drc_ref[...] = jnp.zeros_like(drc_ref)
            dlr_ref[...] = jnp.zeros_like(dlr_ref)
            dli_ref[...] = jnp.zeros_like(dli_ref)
            dg_ref[...] = jnp.zeros_like(dg_ref)

        dd_ref[...] += jnp.sum(dy_ref[...] * u, axis=0, keepdims=True)
        mask = _row_mask(tc)
        for blk in range(S5_BLOCKS):
            sl = slice(blk * 256, (blk + 1) * 256)
            lhsu[blk] = _expand_rows(u[:, sl], mask)
            xs[blk, 0:8] = cs_ref[0, blk]
            xs[blk, 8:8 * tc + 8] = jnp.dot(lhsu[blk], rb_ref[blk], preferred_element_type=F32)
            lhsd[blk] = _expand_rows(dy_ref[:, sl], mask)
            adj[blk] = jnp.dot(lhsd[blk], rct_ref[blk], preferred_element_type=F32)
        lam = [(lr_ref[blk], li_ref[blk]) for blk in range(S5_BLOCKS)]

        def fstep(t, c):
            r0 = pl.multiple_of(t * 8 + 8, 8)
            new = []
            for blk in range(S5_BLOCKS):
                xr, xi = c[2 * blk], c[2 * blk + 1]
                lr, li = lam[blk]
                nr = lr * xr - li * xi + xs[blk, pl.ds(r0, 8), 0:128]
                ni = lr * xi + li * xr + xs[blk, pl.ds(r0, 8), 128:256]
                xs[blk, pl.ds(r0, 8), 0:128] = nr
                xs[blk, pl.ds(r0, 8), 128:256] = ni
                new += [nr, ni]
            return tuple(new)

        c0 = []
        for blk in range(S5_BLOCKS):
            c0 += [cs_ref[0, blk, :, 0:128], cs_ref[0, blk, :, 128:256]]
        _repeat(tc, 8, fstep, tuple(c0))

        def bstep(k, c):
            t = tc - 1 - k
            r0 = pl.multiple_of(t * 8, 8)
            new_a, new_g = [], []
            for blk in range(S5_BLOCKS):
                ar, ai = c[0][2 * blk], c[0][2 * blk + 1]
                glr, gli = c[1][2 * blk], c[1][2 * blk + 1]
                lr, li = lam[blk]
                nr = lr * ar + li * ai + adj[blk, pl.ds(r0, 8), 0:128]
                ni = lr * ai - li * ar + adj[blk, pl.ds(r0, 8), 128:256]
                adj[blk, pl.ds(r0, 8), 0:128] = nr
                adj[blk, pl.ds(r0, 8), 128:256] = ni
                pr, pi = xs[blk, pl.ds(r0, 8), 0:128], xs[blk, pl.ds(r0, 8), 128:256]
                new_a += [nr, ni]
                new_g += [glr + nr * pr + ni * pi, gli + ni * pr - nr * pi]
            return tuple(new_a), tuple(new_g)

        a0, g0 = [], []
        for blk in range(S5_BLOCKS):
            a0 += [acarry[blk, :, 0:128], acarry[blk, :, 128:256]]
            g0 += [dlr_ref[blk], dli_ref[blk]]
        an, gn = _repeat(tc, 4, bstep, (tuple(a0), tuple(g0)))
        for blk in range(S5_BLOCKS):
            acarry[blk, :, 0:128] = an[2 * blk]
            acarry[blk, :, 128:256] = an[2 * blk + 1]
            dlr_ref[blk] = gn[2 * blk]
            dli_ref[blk] = gn[2 * blk + 1]
        for blk in range(S5_BLOCKS):
            sl = slice(blk * 256, (blk + 1) * 256)
            ab = adj[blk].astype(BF16)
            _stage(tmp, jnp.dot(ab, rbt_ref[blk], preferred_element_type=F32))
            du[:, sl] = _gather_rows(tmp, tc) + d_ref[:, sl] * dy_ref[:, sl]
            drb_ref[blk] += lax.dot_general(lhsu[blk], ab, (((0,), (0,)), ((), ())), preferred_element_type=F32)
            drc_ref[blk] += lax.dot_general(lhsd[blk], xs[blk, 8:8 * tc + 8].astype(BF16), (((0,), (0,)), ((), ())),
                                            preferred_element_type=F32)
        xh, r = _rms_hat(x_ref[...])
        dg_ref[...] += jnp.sum(du[...] * xh, axis=0, keepdims=True)
        dxh = du[...] * g_ref[...]
        dx_ref[...] = r * (dxh - xh * jnp.mean(dxh * xh, axis=-1, keepdims=True)) + res_ref[...]

    rev = pl.BlockSpec((tc, D_MODEL), lambda i: (nc - 1 - i, 0))
    vec = pl.BlockSpec((1, D_MODEL), lambda i: (0, 0))
    mat = pl.BlockSpec((S5_BLOCKS, 256, 256), lambda i: (0, 0, 0))
    lamspec = pl.BlockSpec((S5_BLOCKS, 8, 128), lambda i: (0, 0, 0))
    big = pltpu.VMEM((S5_BLOCKS, 8 * tc, 256), F32)
    bigb = pltpu.VMEM((S5_BLOCKS, 8 * tc, 256), BF16)
    return pl.pallas_call(
        body, grid=(nc,),
        in_specs=[rev, vec, rev, rev, vec, pl.BlockSpec((1, S5_BLOCKS, 8, 256), lambda i: (nc - 1 - i, 0, 0, 0)),
                  mat, mat, mat, lamspec, lamspec],
        out_specs=[rev, vec, mat, mat, lamspec, lamspec, vec],
        out_shape=[_sds((n_rows, D_MODEL), F32), _sds((1, D_MODEL), F32), _sds((S5_BLOCKS, 256, 256), F32),
                   _sds((S5_BLOCKS, 256, 256), F32), _sds((S5_BLOCKS, 8, 128), F32), _sds((S5_BLOCKS, 8, 128), F32),
                   _sds((1, D_MODEL), F32)],
        scratch_shapes=[pltpu.VMEM((2, 8 * tc, 128), F32), pltpu.VMEM((tc, D_MODEL), F32), bigb, bigb,
                        pltpu.VMEM((S5_BLOCKS, 8 * tc + 8, 256), F32), big,
                        pltpu.VMEM((S5_BLOCKS, 8, 256), F32)],
        name="s5_bwd", compiler_params=_params(("arbitrary",)))(
            x, gain, dy2, res, d_skip, cs, rb, rbt, rct, lam_r, lam_i)


def _s5_views(a_re, a_im, log_dt, b_re, b_im):
    return a_re[:, None, :], a_im[:, None, :], log_dt[:, None, None], jnp.swapaxes(b_re, 1, 2), jnp.swapaxes(b_im, 1, 2)


def _s5_factors(a_re, a_im, log_dt):
    lr, li, dt = jnp.minimum(a_re, LAMBDA_RE_MAX), a_im, jnp.exp(log_dt)
    mag, ang = jnp.exp(lr * dt), li * dt
    lbr, lbi = mag * jnp.cos(ang), mag * jnp.sin(ang)
    den = lr * lr + li * li
    fr, fi = ((lbr - 1.0) * lr + lbi * li) / den, (lbi * lr - (lbr - 1.0) * li) / den
    return lr, li, dt, lbr, lbi, fr, fi, den


def s5_prep(a_re, a_im, log_dt, b_re, b_im, c_re, c_im):
    def body(ar_ref, ai_ref, t_ref, br_ref, bi_ref, cr_ref, ci_ref, rb_ref, rbt_ref, rc_ref, rct_ref, lr_ref, li_ref):
        _, _, _, lbr, lbi, fr, fi, _ = _s5_factors(ar_ref[...], ai_ref[...], t_ref[...])
        lr_ref[...] = lbr
        li_ref[...] = lbi
        bre = fr * br_ref[...] - fi * bi_ref[...]
        bim = fr * bi_ref[...] + fi * br_ref[...]
        even = (lax.broadcasted_iota(jnp.int32, (256, S5_STATE), 0) // S5_GROUP) % 2 == 0

        def assemble(re, im):
            re, im = re.reshape(256, S5_STATE), im.reshape(256, S5_STATE)
            return jnp.concatenate([jnp.where(even, re, 0.0), jnp.where(even, 0.0, re), jnp.where(even, im, 0.0),
                                    jnp.where(even, 0.0, im)], axis=1)

        for blk in range(S5_BLOCKS):
            sl = slice(16 * blk, 16 * blk + 16)
            rb = assemble(bre[sl], bim[sl])
            rct = assemble(cr_ref[sl], -ci_ref[sl])
            rb_ref[blk] = rb.astype(BF16)
            rbt_ref[blk] = rb.T.astype(BF16)
            rct_ref[blk] = rct.astype(BF16)
            rc_ref[blk] = rct.T.astype(BF16)

    vm = pl.BlockSpec(memory_space=pltpu.VMEM)
    mat = _sds((S5_BLOCKS, 256, 256), BF16)
    lam = _sds((S5_GROUPS, 1, S5_STATE), F32)
    rb, rbt, rc, rct, lam_r, lam_i = pl.pallas_call(
        body, in_specs=[vm] * 7, out_specs=[vm] * 6, out_shape=[mat, mat, mat, mat, lam, lam], name="s5_prep",
        compiler_params=_params())(*_s5_views(a_re, a_im, log_dt, b_re, b_im), c_re, c_im)
    return rb, rbt, rc, rct, lam_r.reshape(S5_BLOCKS, 8, 128), lam_i.reshape(S5_BLOCKS, 8, 128)


def s5_param_bwd(mats, lams, a_re, a_im, log_dt, b_re, b_im):
    def body(m_ref, glr_ref, gli_ref, ar_ref, ai_ref, t_ref, br_ref, bi_ref,
             dar_ref, dai_ref, dt_ref, dbr_ref, dbi_ref, dcr_ref, dci_ref):
        lr, li, dt, lbr, lbi, fr, fi, den = _s5_factors(ar_ref[...], ai_ref[...], t_ref[...])
        shape = (S5_GROUPS, S5_GROUP, S5_STATE)
        gbr, gbi = m_ref[0:1024, 0:64].reshape(shape), m_ref[0:1024, 64:128].reshape(shape)
        dcr_ref[...] = m_ref[1024:2048, 0:64].reshape(shape)
        dci_ref[...] = -m_ref[1024:2048, 64:128].reshape(shape)
        br, bi = br_ref[...], bi_ref[...]
        dbr_ref[...] = fr * gbr + fi * gbi
        dbi_ref[...] = fr * gbi - fi * gbr
        dfr = jnp.sum(gbr * br + gbi * bi, axis=1, keepdims=True)
        dfi = jnp.sum(gbi * br - gbr * bi, axis=1, keepdims=True)
        nr, ni = (dfr * lr - dfi * li) / den, (dfr * li + dfi * lr) / den
        qr, qi = (fr * lr + fi * li) / den, (fi * lr - fr * li) / den
        lam_r, lam_i = -(dfr * qr + dfi * qi), -(dfi * qr - dfr * qi)
        gr, gi = glr_ref[...] + nr, gli_ref[...] + ni
        zr, zi = gr * lbr + gi * lbi, gi * lbr - gr * lbi
        a = ar_ref[...]
        dar_ref[...] = (lam_r + zr * dt) * jnp.where(a < LAMBDA_RE_MAX, 1.0, jnp.where(a == LAMBDA_RE_MAX, 0.5, 0.0))
        dai_ref[...] = lam_i + zi * dt
        dt_ref[...] = jnp.sum(zr * lr + zi * li, axis=2, keepdims=True) * dt

    vm = pl.BlockSpec(memory_space=pltpu.VMEM)
    state = _sds((S5_GROUPS, 1, S5_STATE), F32)
    wide = _sds((S5_GROUPS, S5_GROUP, S5_STATE), F32)
    glr = lams[0:32].reshape(S5_GROUPS, 1, S5_STATE)
    gli = lams[32:64].reshape(S5_GROUPS, 1, S5_STATE)
    dar, dai, ddt, dbr, dbi, dcr, dci = pl.pallas_call(
        body, in_specs=[vm] * 8, out_specs=[vm] * 7,
        out_shape=[state, state, _sds((S5_GROUPS, 1, 1), F32), wide, wide, wide, wide], name="s5_param_bwd",
        compiler_params=_params())(mats, glr, gli, *_s5_views(a_re, a_im, log_dt, b_re, b_im))
    return (dar.reshape(S5_GROUPS, S5_STATE), dai.reshape(S5_GROUPS, S5_STATE), ddt.reshape(S5_GROUPS),
            jnp.swapaxes(dbr, 1, 2), jnp.swapaxes(dbi, 1, 2), dcr, dci)


def s5_compact(drb, drct, dlr, dli):
    def body(drb_ref, drct_ref, dlr_ref, dli_ref, o_ref, lam_ref):
        even = (lax.broadcasted_iota(jnp.int32, (256, 64), 0) // S5_GROUP) % 2 == 0
        for blk in range(S5_BLOCKS):
            for k, ref in enumerate((drb_ref, drct_ref)):
                m = ref[blk]
                re = jnp.where(even, m[:, 0:64], m[:, 64:128])
                im = jnp.where(even, m[:, 128:192], m[:, 192:256])
                o_ref[pl.ds(k * 1024 + blk * 256, 256), :] = jnp.concatenate([re, im], axis=1)
            lam_ref[pl.ds(blk * 8, 8), :] = dlr_ref[blk]
            lam_ref[pl.ds(32 + blk * 8, 8), :] = dli_ref[blk]

    vm = pl.BlockSpec(memory_space=pltpu.VMEM)
    return pl.pallas_call(body, in_specs=[vm] * 4, out_specs=[vm, vm], out_shape=[_sds((2048, 128), F32), _sds((64, 128), F32)],
                          name="s5_compact", compiler_params=_params())(drb, drct, dlr, dli)


NEG = -1e30


GROUP = N_Q // N_KV


def _attn_masks(n):
    qi = lax.broadcasted_iota(jnp.int32, (GROUP * BLOCK, BLOCK), 0) % BLOCK
    kj = lax.broadcasted_iota(jnp.int32, (GROUP * BLOCK, BLOCK), 1)
    return jnp.logical_and(kj > qi, n > 0), kj <= qi


def _stack_heads(ref, kh):
    return jnp.concatenate([ref[:, (GROUP * kh + g) * HEAD_DIM:(GROUP * kh + g + 1) * HEAD_DIM] for g in range(GROUP)], axis=0)


def _unstack_heads(val):
    return jnp.concatenate([val[g * BLOCK:(g + 1) * BLOCK] for g in range(GROUP)], axis=1)


def _sink_column(sink_ref, kh):
    grp = lax.broadcasted_iota(jnp.int32, (GROUP * BLOCK, 1), 0) // BLOCK
    col = jnp.zeros((GROUP * BLOCK, 1), F32)
    for g in range(GROUP):
        col = jnp.where(grp == g, sink_ref[GROUP * kh + g], col)
    return col, grp


def _attn_exp(q4, kp, kc, sink, mask_p, mask_c):
    scale = 1.0 / math.sqrt(HEAD_DIM)
    nt = (((1,), (1,)), ((), ()))
    sp = jnp.where(mask_p, lax.dot_general(q4, kp, nt, preferred_element_type=F32) * scale, NEG)
    sc = jnp.where(mask_c, lax.dot_general(q4, kc, nt, preferred_element_type=F32) * scale, NEG)
    m = jnp.maximum(jnp.maximum(jnp.max(sp, axis=-1, keepdims=True), jnp.max(sc, axis=-1, keepdims=True)), sink)
    pp = jnp.exp(sp - m)
    pc = jnp.exp(sc - m)
    ps = jnp.exp(sink - m)
    inv = 1.0 / (jnp.sum(pp, axis=-1, keepdims=True) + jnp.sum(pc, axis=-1, keepdims=True) + ps)
    return pp, pc, ps, inv


def attn_fwd(q, kv, sinks):
    n_rows = q.shape[0]
    nb = n_rows // BLOCK

    def body(sink_ref, q_ref, kvp_ref, kvc_ref, o_ref):
        n = pl.program_id(0)
        mask_p, mask_c = _attn_masks(n)
        outs = []
        for kh in range(N_KV):
            ks, vs = slice(kh * HEAD_DIM, (kh + 1) * HEAD_DIM), slice((N_KV + kh) * HEAD_DIM, (N_KV + kh + 1) * HEAD_DIM)
            sink, _ = _sink_column(sink_ref, kh)
            pp, pc, _, inv = _attn_exp(_stack_heads(q_ref, kh), kvp_ref[:, ks], kvc_ref[:, ks], sink, mask_p, mask_c)
            o4 = (jnp.dot(pp.astype(BF16), kvp_ref[:, vs], preferred_element_type=F32)
                  + jnp.dot(pc.astype(BF16), kvc_ref[:, vs], preferred_element_type=F32)) * inv
            outs.append(_unstack_heads(o4))
        o_ref[...] = jnp.concatenate(outs, axis=1).astype(BF16)

    kvw = 2 * N_KV * HEAD_DIM
    return pl.pallas_call(
        body, grid=(nb,),
        in_specs=[pl.BlockSpec(memory_space=pltpu.SMEM), pl.BlockSpec((BLOCK, D_MODEL), lambda n: (n, 0)),
                  pl.BlockSpec((BLOCK, kvw), lambda n: (jnp.maximum(n - 1, 0), 0)), pl.BlockSpec((BLOCK, kvw), lambda n: (n, 0))],
        out_specs=pl.BlockSpec((BLOCK, D_MODEL), lambda n: (n, 0)), out_shape=_sds((n_rows, D_MODEL), BF16),
        name="attn_fwd", compiler_params=_params(("parallel",)))(sinks, q, kv, kv)


def attn_bwd(q, kv, do, sinks):
    n_rows = q.shape[0]
    nb = n_rows // BLOCK
    kvw = 2 * N_KV * HEAD_DIM
    tn = (((0,), (0,)), ((), ()))
    nt = (((1,), (1,)), ((), ()))
    scale = 1.0 / math.sqrt(HEAD_DIM)

    def body(sink_ref, q_ref, kvp_ref, kvc_ref, do_ref, dq_ref, dbq_ref, dprev_ref, dcur_ref, dsink_ref):
        n = pl.program_id(0)
        mask_p, mask_c = _attn_masks(n)
        lane = lax.broadcasted_iota(jnp.int32, (1, D_MODEL), 1)
        dqs, dsink = [], jnp.zeros((1, D_MODEL), F32)
        dkp, dkc, dvp, dvc = [], [], [], []
        for kh in range(N_KV):
            ks, vs = slice(kh * HEAD_DIM, (kh + 1) * HEAD_DIM), slice((N_KV + kh) * HEAD_DIM, (N_KV + kh + 1) * HEAD_DIM)
            q4, do4 = _stack_heads(q_ref, kh), _stack_heads(do_ref, kh)
            kp, kc, vp, vc = kvp_ref[:, ks], kvc_ref[:, ks], kvp_ref[:, vs], kvc_ref[:, vs]
            sink, grp = _sink_column(sink_ref, kh)
            pp, pc, ps, inv = _attn_exp(q4, kp, kc, sink, mask_p, mask_c)
            pp, pc = pp * inv, pc * inv
            dpp = lax.dot_general(do4, vp, nt, preferred_element_type=F32)
            dpc = lax.dot_general(do4, vc, nt, preferred_element_type=F32)
            delta = jnp.sum(pp * dpp, axis=-1, keepdims=True) + jnp.sum(pc * dpc, axis=-1, keepdims=True)
            dsp = (pp * (dpp - delta) * scale).astype(BF16)
            dsc = (pc * (dpc - delta) * scale).astype(BF16)
            dsk = ps * inv * delta
            for g in range(GROUP):
                dsink = dsink + jnp.where(lane == GROUP * kh + g, -jnp.sum(jnp.where(grp == g, dsk, 0.0)), 0.0)
            dqs.append(_unstack_heads(jnp.dot(dsp, kp, preferred_element_type=F32)
                                      + jnp.dot(dsc, kc, preferred_element_type=F32)))
            dkp.append(lax.dot_general(dsp, q4, tn, preferred_element_type=F32))
            dkc.append(lax.dot_general(dsc, q4, tn, preferred_element_type=F32))
            dvp.append(lax.dot_general(pp.astype(BF16), do4, tn, preferred_element_type=F32))
            dvc.append(lax.dot_general(pc.astype(BF16), do4, tn, preferred_element_type=F32))
        dq = jnp.concatenate(dqs, axis=1)
        dq_ref[...] = dq.astype(BF16)
        dprev_ref[0] = jnp.concatenate(dkp + dvp, axis=1)
        dcur_ref[0] = jnp.concatenate(dkc + dvc, axis=1)

        @pl.when(n == 0)
        def _():
            dbq_ref[...] = jnp.zeros_like(dbq_ref)
            dsink_ref[...] = jnp.zeros_like(dsink_ref)

        dbq_ref[...] += jnp.sum(dq, axis=0, keepdims=True)
        dsink_ref[...] += dsink

    blk = pl.BlockSpec((BLOCK, D_MODEL), lambda n: (n, 0))
    part = pl.BlockSpec((1, BLOCK, kvw), lambda n: (n, 0, 0))
    return pl.pallas_call(
        body, grid=(nb,),
        in_specs=[pl.BlockSpec(memory_space=pltpu.SMEM), blk,
                  pl.BlockSpec((BLOCK, kvw), lambda n: (jnp.maximum(n - 1, 0), 0)), pl.BlockSpec((BLOCK, kvw), lambda n: (n, 0)), blk],
        out_specs=[blk, pl.BlockSpec((1, D_MODEL), lambda n: (0, 0)), part, part, pl.BlockSpec((1, D_MODEL), lambda n: (0, 0))],
        out_shape=[_sds((n_rows, D_MODEL), BF16), _sds((1, D_MODEL), F32), _sds((nb, BLOCK, kvw), F32),
                   _sds((nb, BLOCK, kvw), F32), _sds((1, D_MODEL), F32)],
        name="attn_bwd", compiler_params=_params(("arbitrary",)))(sinks, q, kv, kv, do)


def kv_combine(dprev, dcur):
    nb, _, kvw = dprev.shape

    def body(dcur_ref, dprev_ref, dkv_ref, db_ref):
        total = jnp.zeros((1, kvw), F32)
        for m in range(nb):
            dkv = dcur_ref[m] + dprev_ref[m + 1] if m + 1 < nb else dcur_ref[m]
            dkv_ref[m * BLOCK:(m + 1) * BLOCK, :] = dkv.astype(BF16)
            total = total + jnp.sum(dkv, axis=0, keepdims=True)
        db_ref[...] = jnp.concatenate([total, jnp.zeros((1, D_MODEL - kvw), F32)], axis=1)

    vm = pl.BlockSpec(memory_space=pltpu.VMEM)
    return pl.pallas_call(body, in_specs=[vm, vm], out_specs=[vm, vm],
                          out_shape=[_sds((nb * BLOCK, kvw), BF16), _sds((1, D_MODEL), F32)], name="kv_combine",
                          compiler_params=_params())(dcur, dprev)


def glu_bwd(dout, val, gate, tm=256):
    n_rows, d = dout.shape

    def body(do_ref, v_ref, g_ref, dz_ref, db_ref):
        i = pl.program_id(0)
        sg = jax.nn.sigmoid(g_ref[...])
        dval = do_ref[...] * sg
        dgate = do_ref[...] * v_ref[...] * sg * (1.0 - sg)
        dz_ref[...] = jnp.concatenate([dval, dgate], axis=1).astype(BF16)

        @pl.when(i == 0)
        def _():
            db_ref[...] = jnp.zeros_like(db_ref)

        db_ref[0:1, :] += jnp.sum(dval, axis=0, keepdims=True)
        db_ref[1:2, :] += jnp.sum(dgate, axis=0, keepdims=True)

    row = pl.BlockSpec((tm, d), lambda i: (i, 0))
    return pl.pallas_call(
        body, grid=(n_rows // tm,), in_specs=[row, row, row],
        out_specs=[pl.BlockSpec((tm, 2 * d), lambda i: (i, 0)), pl.BlockSpec((2, d), lambda i: (0, 0))],
        out_shape=[_sds((n_rows, 2 * d), BF16), _sds((2, d), F32)],
        name="glu_bwd", compiler_params=_params(("arbitrary",)))(dout, val, gate)


def _adam_update(w, g, m, v):
    nm = ADAM_B1 * m + (1.0 - ADAM_B1) * g
    nv = ADAM_B2 * v + (1.0 - ADAM_B2) * (g * g)
    m_hat = nm / (1.0 - ADAM_B1 ** ADAM_STEP)
    v_hat = nv / (1.0 - ADAM_B2 ** ADAM_STEP)
    return -ADAM_LR * (m_hat / (jnp.sqrt(v_hat) + ADAM_EPS) + ADAM_WD * w), nm, nv


def adamw(name, ws, gs, ms, vs, steps=8):
    n = len(ws)

    def body(*refs):
        for k in range(n):
            w_ref, g_ref, m_ref, v_ref = (refs[j * n + k] for j in range(4))
            go_ref, d_ref, nm_ref, nv_ref = (refs[(4 + j) * n + k] for j in range(4))
            gv = g_ref[...]
            go_ref[...] = gv
            d_ref[...], nm_ref[...], nv_ref[...] = _adam_update(w_ref[...], gv, m_ref[...], v_ref[...])

    specs = [pl.BlockSpec((w.shape[0] // steps, w.shape[1]), lambda i: (i, 0)) for w in ws]
    shapes = [_sds(w.shape, F32) for w in ws]
    out = pl.pallas_call(
        body, grid=(steps,), in_specs=specs * 4, out_specs=specs * 4, out_shape=shapes * 4, name=name,
        compiler_params=_params(("parallel",)))(*ws, *gs, *ms, *vs)
    return [list(out[j * n:(j + 1) * n]) for j in range(4)]


def adamw_native(name, ws, gs, ms, vs):
    n = len(ws)

    def body(*refs):
        w_refs, g_refs, m_refs, v_refs = refs[:n], refs[n:2 * n], refs[2 * n:3 * n], refs[3 * n:4 * n]
        d_refs, nm_refs, nv_refs = refs[4 * n:5 * n], refs[5 * n:6 * n], refs[6 * n:7 * n]
        for k in range(n):
            dl, nm, nv = _adam_update(w_refs[k][...], g_refs[k][...], m_refs[k][...], v_refs[k][...])
            d_refs[k][...] = dl
            nm_refs[k][...] = nm
            nv_refs[k][...] = nv

    vm = pl.BlockSpec(memory_space=pltpu.VMEM)
    shapes = [_sds(w.shape, F32) for w in ws]
    out = pl.pallas_call(body, in_specs=[vm] * (4 * n), out_specs=[vm] * (3 * n), out_shape=shapes * 3, name=name,
                         compiler_params=_params())(*ws, *gs, *ms, *vs)
    return list(out[:n]), list(out[n:2 * n]), list(out[2 * n:])


VEC_ROWS = {"norm_mix": 0, "norm_mlp": 2, "norm_kv": 4, "norm_final": 5, "s5_d": 6, "b_q": 7, "b_o": 8, "s5_b_glu": 9,
            "b_kv": 11, "sinks": 12, "loss": 13}


def split_vectors(where, vecs, d_shard, glu_shard):
    kvw = 2 * N_KV * HEAD_DIM
    shapes = {"norm_mix": (2, D_MODEL), "norm_mlp": (2, D_MODEL), "norm_kv": (1, D_MODEL), "norm_final": (1, D_MODEL),
              "s5_d": (1, d_shard), "b_q": (1, D_MODEL), "b_o": (1, D_MODEL), "s5_b_glu": (1, glu_shard), "b_kv": (1, kvw),
              "sinks": (1, N_Q), "loss": (1, 128)}
    names = list(shapes)

    def body(where_ref, v_ref, *o_refs):
        chip = where_ref[1]
        for name, o_ref in zip(names, o_refs):
            r0, (r, n) = VEC_ROWS[name], shapes[name]
            if name == "s5_d":
                g = jnp.zeros((1, n), F32)
                for j in range(4):
                    g = jnp.where(chip == j, v_ref[r0:r0 + 1, j * n:(j + 1) * n], g)
            elif name == "s5_b_glu":
                g = jnp.zeros((1, n), F32)
                for j in range(4):
                    row, col = r0 + (j * n) // D_MODEL, (j * n) % D_MODEL
                    g = jnp.where(chip == j, v_ref[row:row + 1, col:col + n], g)
            else:
                g = v_ref[r0:r0 + r, 0:n]
            o_ref[...] = g

    vm = pl.BlockSpec(memory_space=pltpu.VMEM)
    out = pl.pallas_call(body, in_specs=[pl.BlockSpec(memory_space=pltpu.SMEM), vm], out_specs=[vm] * len(names),
                         out_shape=[_sds(shapes[n], F32) for n in names], name="split_vectors",
                         compiler_params=_params())(where, vecs)
    return dict(zip(names, out))


def _position():
    x, y, c = lax.axis_index("x"), lax.axis_index("y"), lax.axis_index("c")
    others = [(1 - x, y), (x, 1 - y), (1 - x, 1 - y)]
    return x, y, c, others


def _window(ref, kind, chip, half, shard_shape):
    if kind == "slab":
        return ref.at[chip]
    r, n = shard_shape
    if kind == "col":
        return ref.at[pl.ds(pl.multiple_of(half * (r // 2), 16), r // 2), pl.ds(pl.multiple_of(chip * n, 128), n)]
    return ref.at[pl.ds(pl.multiple_of(chip * r, 16), r), pl.ds(pl.multiple_of(half * (n // 2), 128), n // 2)]


def _half(ref, kind, half, shape):
    r, n = shape
    if kind == "col":
        return ref.at[pl.ds(pl.multiple_of(half * (r // 2), 16), r // 2), :]
    return ref.at[:, pl.ds(pl.multiple_of(half * (n // 2), 128), n // 2)]


def swap_start(name, grads, kinds, carry):
    nt = len(grads)
    shapes = [tuple(g.shape) for g in grads]
    lands = [lax.empty(sh, BF16) for sh in shapes]
    given, given_specs, token_type, write = _hand_through(carry)
    n_in = 2 * nt + len(given)

    def body(*refs):
        in_refs, land_refs = refs[:nt], refs[nt:2 * nt]
        send_sems, recv_sems, token = refs[n_in], refs[n_in + 1], refs[-1]
        x, y, c, _ = _position()
        for t in range(nt):
            pltpu.make_async_remote_copy(
                src_ref=_half(in_refs[t], kinds[t], 1 - c, shapes[t]), dst_ref=_half(land_refs[t], kinds[t], 1 - c, shapes[t]),
                send_sem=send_sems.at[t], recv_sem=recv_sems.at[t], device_id=(x, y, 1 - c), device_id_type=MESH).start()
        write(token, refs[:n_in])

    sems = pltpu.SemaphoreType.DMA((nt,))
    both = list(grads) + lands
    out = pl.pallas_call(
        body, name=name, in_specs=[HBM_SPEC] * (2 * nt) + given_specs,
        out_specs=(SEM_SPEC, SEM_SPEC, *[HBM_SPEC] * (2 * nt), pl.BlockSpec(memory_space=pltpu.VMEM)),
        out_shape=(sems, sems, *[pltpu.HBM(a.shape, a.dtype) for a in both], token_type),
        input_output_aliases={t: 2 + t for t in range(2 * nt)}, compiler_params=_split_params(),
    )(*[_in_hbm(a) for a in both], *given)
    return out[0], out[1], list(out[2:2 + nt]), list(out[2 + nt:2 + 2 * nt]), out[-1]


def swap_wait(name, send_sems, recv_sems, grads, lands, kinds, after):
    nt = len(grads)
    shapes = [tuple(g.shape) for g in grads]

    def body(*refs):
        in_refs, land_refs = refs[:nt], refs[nt:2 * nt]
        send_ref, recv_ref = refs[2 * nt], refs[2 * nt + 1]
        x, y, c, _ = _position()
        for t in range(nt):
            cp = pltpu.make_async_remote_copy(
                src_ref=_half(in_refs[t], kinds[t], 1 - c, shapes[t]), dst_ref=_half(land_refs[t], kinds[t], c, shapes[t]),
                send_sem=send_ref.at[t], recv_sem=recv_ref.at[t], device_id=(x, y, 1 - c), device_id_type=MESH)
            cp.wait_send()
            cp.wait_recv()

    both = list(grads) + list(lands)
    out = pl.pallas_call(
        body, name=name, in_specs=[HBM_SPEC] * (2 * nt) + [SEM_SPEC, SEM_SPEC, HBM_SPEC], out_specs=[HBM_SPEC] * (2 * nt),
        out_shape=[pltpu.HBM(a.shape, a.dtype) for a in both], input_output_aliases={t: t for t in range(2 * nt)},
        compiler_params=_split_params())(*both, send_sems, recv_sems, _in_hbm(after))
    return list(out[:nt]), list(out[nt:])


def _half_spec(kind, shape, tiles):
    r, n = shape
    if kind == "col":
        tn = n // tiles
        return pl.BlockSpec((r // 2, tn), lambda i, s: (s[0], i))
    tm = r // tiles
    return pl.BlockSpec((tm, n // 2), lambda i, s: (i, s[0]))


def add_halves(name, mine, landed, kinds, where, tiles=2):
    nt = len(mine)
    shapes = [tuple(a.shape) for a in mine]

    def compact(t):
        r, n = shapes[t]
        if kinds[t] == "col":
            return (r // 2, n), pl.BlockSpec((r // 2, n // tiles), lambda i, s: (0, i))
        return (r, n // 2), pl.BlockSpec((r // tiles, n // 2), lambda i, s: (i, 0))

    def body(s_ref, *refs):
        for a_ref, b_ref, o_ref in zip(refs[:nt], refs[nt:2 * nt], refs[2 * nt:]):
            o_ref[...] = (a_ref[...].astype(F32) + b_ref[...].astype(F32)).astype(BF16)

    specs = [_half_spec(kinds[t], shapes[t], tiles) for t in range(nt)]
    return pl.pallas_call(
        body, grid_spec=pltpu.PrefetchScalarGridSpec(num_scalar_prefetch=1, grid=(tiles,), in_specs=specs + specs,
                                                     out_specs=[compact(t)[1] for t in range(nt)]),
        out_shape=[_sds(compact(t)[0], BF16) for t in range(nt)], name=name,
        compiler_params=_params(("parallel",)))(where, *mine, *landed)


def sum_shards(name, parts, landed, kinds, shard_shapes, where, layers, n_layers, intos, tiles=2):
    nt = len(parts)
    in_specs, out_specs = [], []
    for t in range(nt):
        (r, n), layer = shard_shapes[t], layers[t]
        if kinds[t] == "col":
            tm, width = r // 2 // tiles, n
            own = pl.BlockSpec((tm, n), lambda i, s: (i, s[1]))
            out = pl.BlockSpec((None, tm, n), lambda i, s, layer=layer: (layer, s[0] * tiles + i, 0))
        else:
            tm, width = r // tiles, n // 2
            own = pl.BlockSpec((tm, n // 2), lambda i, s: (s[1] * tiles + i, 0))
            out = pl.BlockSpec((None, tm, n // 2), lambda i, s, layer=layer: (layer, i, s[0]))
        in_specs += [own, pl.BlockSpec((3, tm, width), lambda i, s: (0, i, 0))]
        out_specs.append(out)
    args, aliases = [where] + [a for pair in zip(parts, landed) for a in pair], {}
    for t in range(nt):
        if intos[t] is not None:
            aliases[len(args)] = t
            in_specs.append(pl.BlockSpec(memory_space=pl.ANY))
            args.append(intos[t])

    def body(s_ref, *refs):
        for t in range(nt):
            a_ref, l_ref, o_ref = refs[2 * t], refs[2 * t + 1], refs[len(in_specs) + t]
            o_ref[...] = ((a_ref[...].astype(F32) + l_ref[0].astype(F32)) + l_ref[1].astype(F32)) + l_ref[2].astype(F32)

    return pl.pallas_call(
        body, grid_spec=pltpu.PrefetchScalarGridSpec(num_scalar_prefetch=1, grid=(tiles,), in_specs=in_specs,
                                                     out_specs=out_specs),
        out_shape=[_sds((n_layers[t],) + tuple(shard_shapes[t]), F32) for t in range(nt)], input_output_aliases=aliases,
        name=name, compiler_params=_params(("parallel",)))(*args)


def share_start(arrays, entries, carry):
    na, nt = len(arrays), len(entries)
    given, given_specs, token_type, write = _hand_through(carry)
    n_in = na + len(given)

    def body(*refs):
        in_refs, send_sems, recv_sems, token = refs[:na], refs[n_in], refs[n_in + 1], refs[-1]
        x, y, c, _ = _position()
        for t, (a, layer, kind) in enumerate(entries):
            mine = _half(in_refs[a].at[layer], kind, c, tuple(arrays[a].shape[1:]))
            pltpu.make_async_remote_copy(
                src_ref=mine, dst_ref=mine, send_sem=send_sems.at[t], recv_sem=recv_sems.at[t],
                device_id=(x, y, 1 - c), device_id_type=MESH).start()
        write(token, refs[:n_in])

    sems = pltpu.SemaphoreType.DMA((nt,))
    out = pl.pallas_call(
        body, name="share_start", in_specs=[HBM_SPEC] * na + given_specs,
        out_specs=(SEM_SPEC, SEM_SPEC, *[HBM_SPEC] * na, pl.BlockSpec(memory_space=pltpu.VMEM)),
        out_shape=(sems, sems, *[pltpu.HBM(a.shape, a.dtype) for a in arrays], token_type),
        input_output_aliases={t: 2 + t for t in range(na)}, compiler_params=_split_params(),
    )(*[_in_hbm(a) for a in arrays], *given)
    return out[0], out[1], list(out[2:2 + na]), out[-1]


def share_wait(send_sems, recv_sems, arrays, entries, after):
    na = len(arrays)

    def body(*refs):
        in_refs, send_ref, recv_ref = refs[:na], refs[na], refs[na + 1]
        x, y, c, _ = _position()
        for t, (a, layer, kind) in enumerate(entries):
            shape = tuple(arrays[a].shape[1:])
            cp = pltpu.make_async_remote_copy(
                src_ref=_half(in_refs[a].at[layer], kind, c, shape), dst_ref=_half(in_refs[a].at[layer], kind, 1 - c, shape),
                send_sem=send_ref.at[t], recv_sem=recv_ref.at[t], device_id=(x, y, 1 - c), device_id_type=MESH)
            cp.wait_send()
            cp.wait_recv()

    return list(pl.pallas_call(
        body, name="share_wait", in_specs=[HBM_SPEC] * na + [SEM_SPEC, SEM_SPEC, HBM_SPEC], out_specs=[HBM_SPEC] * na,
        out_shape=[pltpu.HBM(a.shape, a.dtype) for a in arrays], input_output_aliases={t: t for t in range(na)},
        compiler_params=_split_params())(*arrays, send_sems, recv_sems, _in_hbm(after)))


HBM_SPEC = pl.BlockSpec(memory_space=pltpu.HBM)
SEM_SPEC = pl.BlockSpec(memory_space=pltpu.SEMAPHORE)
ANY_SPEC = pl.BlockSpec(memory_space=pl.ANY)


def _split_params():
    return pltpu.CompilerParams(has_side_effects=pltpu.SideEffectType.DATAFLOW_SIDE_EFFECTING,
                                vmem_limit_bytes=VMEM_LIMIT_BYTES)


def _in_hbm(a):
    return pltpu.with_memory_space_constraint(a, pltpu.HBM)


def cast_place(arrays, entries, where, tiles=2):
    in_specs, out_specs, fulls = [], [], []
    for a, layer, kind in entries:
        _, r, n = arrays[a].shape
        tm = r // tiles
        in_specs.append(pl.BlockSpec((None, tm, n), lambda i, s, layer=layer: (layer, i, 0)))
        if kind == "col":
            fulls.append((r, 4 * n))
            out_specs.append(pl.BlockSpec((tm, n), lambda i, s: (i, s[1])))
        else:
            fulls.append((4 * r, n))
            out_specs.append(pl.BlockSpec((tm, n), lambda i, s: (s[1] * tiles + i, 0)))
    nt = len(entries)

    def body(s_ref, *refs):
        for w_ref, o_ref in zip(refs[:nt], refs[nt:]):
            o_ref[...] = w_ref[...].astype(BF16)

    return pl.pallas_call(
        body, grid_spec=pltpu.PrefetchScalarGridSpec(num_scalar_prefetch=1, grid=(tiles,), in_specs=in_specs,
                                                     out_specs=out_specs),
        out_shape=[_sds(f, BF16) for f in fulls], name="cast_place",
        compiler_params=_params(("parallel",)))(where, *[arrays[a] for a, _, _ in entries])


def _hand_through(carry):
    given = [] if isinstance(carry, tuple) else [carry]

    def write(token, ins):
        token[...] = ins[-1][...] if given else jnp.zeros_like(token)

    return (given, [pl.BlockSpec(memory_space=pltpu.VMEM)] * len(given),
            _sds(carry if isinstance(carry, tuple) else carry.shape, F32), write)


def gather_start(fulls, kinds, shard_shapes, carry):
    nt = len(fulls)
    given, given_specs, token_type, write = _hand_through(carry)
    n_in = nt + len(given)

    def body(*refs):
        full_refs = refs[:nt]
        send_sems, recv_sems, token = refs[n_in], refs[n_in + 1], refs[-1]
        x, y, c, others = _position()
        for t in range(nt):
            mine = _window(full_refs[t], kinds[t], 2 * x + y, c, shard_shapes[t])
            for j, (ox, oy) in enumerate(others):
                pltpu.make_async_remote_copy(
                    src_ref=mine, dst_ref=mine, send_sem=send_sems.at[3 * t + j], recv_sem=recv_sems.at[3 * t + j],
                    device_id=(ox, oy, c), device_id_type=MESH).start()
        write(token, refs[:n_in])

    sems = pltpu.SemaphoreType.DMA((3 * nt,))
    out = pl.pallas_call(
        body, name="gather_start", in_specs=[HBM_SPEC] * nt + given_specs,
        out_specs=(SEM_SPEC, SEM_SPEC, *[HBM_SPEC] * nt, pl.BlockSpec(memory_space=pltpu.VMEM)),
        out_shape=(sems, sems, *[pltpu.HBM(f.shape, f.dtype) for f in fulls], token_type),
        input_output_aliases={t: 2 + t for t in range(nt)}, compiler_params=_split_params(),
    )(*[_in_hbm(f) for f in fulls], *given)
    return out[0], out[1], list(out[2:2 + nt]), out[-1]


def gather_wait(name, send_sems, recv_sems, fulls, kinds, shard_shapes, after, first):
    nt = len(fulls)
    extra = [] if after is None else [_in_hbm(after)]

    def body(*refs):
        full_refs, send_ref, recv_ref = refs[:nt], refs[nt], refs[nt + 1]
        x, y, c, others = _position()
        for t in range(nt):
            mine = _window(full_refs[t], kinds[t], 2 * x + y, c, shard_shapes[t])
            for j, (ox, oy) in enumerate(others):
                cp = pltpu.make_async_remote_copy(
                    src_ref=mine, dst_ref=_window(full_refs[t], kinds[t], 2 * ox + oy, c, shard_shapes[t]),
                    send_sem=send_ref.at[3 * (first + t) + j], recv_sem=recv_ref.at[3 * (first + t) + j],
                    device_id=(ox, oy, c), device_id_type=MESH)
                cp.wait_send()
                cp.wait_recv()

    out = pl.pallas_call(
        body, name=name, in_specs=[HBM_SPEC] * nt + [SEM_SPEC, SEM_SPEC] + [HBM_SPEC] * len(extra),
        out_specs=[HBM_SPEC] * nt, out_shape=[pltpu.HBM(f.shape, f.dtype) for f in fulls],
        input_output_aliases={t: t for t in range(nt)}, compiler_params=_split_params())(*fulls, send_sems, recv_sems, *extra)
    return list(out)


def forward_halves(name, fulls, kinds, shard_shapes):
    nt = len(fulls)

    def body(*refs):
        out_refs = refs[nt:2 * nt]
        send_sems, recv_sems = refs[2 * nt:]
        x, y, c, others = _position()
        cps = []
        for t in range(nt):
            for j, (ox, oy) in enumerate(others):
                landed = _window(out_refs[t], kinds[t], 2 * ox + oy, c, shard_shapes[t])
                cp = pltpu.make_async_remote_copy(
                    src_ref=landed, dst_ref=landed, send_sem=send_sems.at[3 * t + j], recv_sem=recv_sems.at[3 * t + j],
                    device_id=(x, y, 1 - c), device_id_type=MESH)
                cp.start()
                cps.append(cp)
        for t in range(nt):
            for j, (ox, oy) in enumerate(others):
                got = _window(out_refs[t], kinds[t], 2 * ox + oy, 1 - c, shard_shapes[t])
                pltpu.make_async_remote_copy(
                    src_ref=got, dst_ref=got, send_sem=send_sems.at[3 * t + j], recv_sem=recv_sems.at[3 * t + j],
                    device_id=(x, y, 1 - c), device_id_type=MESH).wait_recv()
        for cp in cps:
            cp.wait_send()

    out = pl.pallas_call(
        body, in_specs=[ANY_SPEC] * nt, out_specs=[ANY_SPEC] * nt, out_shape=[_sds(f.shape, f.dtype) for f in fulls],
        input_output_aliases={t: t for t in range(nt)},
        scratch_shapes=[pltpu.SemaphoreType.DMA((3 * nt,)), pltpu.SemaphoreType.DMA((3 * nt,))],
        name=name, compiler_params=_params())(*fulls)
    return list(out)


def forward_start(name, send_sems, recv_sems, fulls, kinds, shard_shapes, after, first, carry, passing=()):
    nt, n_pass = len(fulls), len(passing)
    given, given_specs, token_type, write = _hand_through(carry)
    n_in = nt + 3 + n_pass + len(given)

    def body(*refs):
        full_refs, ici_send, ici_recv = refs[:nt], refs[nt], refs[nt + 1]
        send_ref, recv_ref, token = refs[n_in], refs[n_in + 1], refs[-1]
        x, y, c, others = _position()
        for t in range(nt):
            mine = _window(full_refs[t], kinds[t], 2 * x + y, c, shard_shapes[t])
            for j, (ox, oy) in enumerate(others):
                landed = _window(full_refs[t], kinds[t], 2 * ox + oy, c, shard_shapes[t])
                cp = pltpu.make_async_remote_copy(
                    src_ref=mine, dst_ref=landed, send_sem=ici_send.at[3 * (first + t) + j],
                    recv_sem=ici_recv.at[3 * (first + t) + j], device_id=(ox, oy, c), device_id_type=MESH)
                cp.wait_send()
                cp.wait_recv()
                pltpu.make_async_remote_copy(
                    src_ref=landed, dst_ref=landed, send_sem=send_ref.at[3 * t + j], recv_sem=recv_ref.at[3 * t + j],
                    device_id=(x, y, 1 - c), device_id_type=MESH).start()
        write(token, refs[:n_in])

    sems = pltpu.SemaphoreType.DMA((3 * nt,))
    out = pl.pallas_call(
        body, name=name, in_specs=[HBM_SPEC] * nt + [SEM_SPEC, SEM_SPEC, HBM_SPEC] + [HBM_SPEC] * n_pass + given_specs,
        out_specs=(SEM_SPEC, SEM_SPEC, *[HBM_SPEC] * (nt + n_pass), pl.BlockSpec(memory_space=pltpu.VMEM)),
        out_shape=(sems, sems, *[pltpu.HBM(f.shape, f.dtype) for f in [*fulls, *passing]], token_type),
        input_output_aliases={**{t: 2 + t for t in range(nt)}, **{nt + 3 + t: 2 + nt + t for t in range(n_pass)}},
        compiler_params=_split_params(),
    )(*fulls, send_sems, recv_sems, _in_hbm(after), *passing, *given)
    return out[0], out[1], list(out[2:2 + nt]), list(out[2 + nt:2 + nt + n_pass]), out[-1]


def forward_wait(name, send_sems, recv_sems, fulls, kinds, shard_shapes, after):
    nt = len(fulls)

    def body(*refs):
        full_refs, send_ref, recv_ref = refs[:nt], refs[nt], refs[nt + 1]
        x, y, c, others = _position()
        for t in range(nt):
            for j, (ox, oy) in enumerate(others):
                cp = pltpu.make_async_remote_copy(
                    src_ref=_window(full_refs[t], kinds[t], 2 * ox + oy, c, shard_shapes[t]),
                    dst_ref=_window(full_refs[t], kinds[t], 2 * ox + oy, 1 - c, shard_shapes[t]),
                    send_sem=send_ref.at[3 * t + j], recv_sem=recv_ref.at[3 * t + j],
                    device_id=(x, y, 1 - c), device_id_type=MESH)
                cp.wait_send()
                cp.wait_recv()

    return list(pl.pallas_call(
        body, name=name, in_specs=[HBM_SPEC] * nt + [SEM_SPEC, SEM_SPEC, HBM_SPEC], out_specs=[HBM_SPEC] * nt,
        out_shape=[pltpu.HBM(f.shape, f.dtype) for f in fulls], input_output_aliases={t: t for t in range(nt)},
        compiler_params=_split_params())(*fulls, send_sems, recv_sems, _in_hbm(after)))


def _piece(ref, kind, chip, shard_shape):
    r, n = shard_shape
    if kind == "col":
        return ref.at[:, pl.ds(pl.multiple_of(chip * n, 128), n)]
    return ref.at[pl.ds(pl.multiple_of(chip * r, 16), r), :]


def _piece_shape(kind, shard_shape):
    r, n = shard_shape
    return (r // 2, n) if kind == "col" else (r, n // 2)


def exchange_start(name, parts, kinds, shard_shapes, carry):
    nt = len(parts)
    lands = [lax.empty((3,) + _piece_shape(kinds[t], shard_shapes[t]), BF16) for t in range(nt)]
    given, given_specs, token_type, write = _hand_through(carry)
    n_in = 2 * nt + len(given)

    def body(*refs):
        part_refs, land_refs = refs[:nt], refs[nt:2 * nt]
        send_sems, recv_sems, token = refs[n_in], refs[n_in + 1], refs[-1]
        x, y, c, others = _position()
        for t in range(nt):
            for j, (ox, oy) in enumerate(others):
                pltpu.make_async_remote_copy(
                    src_ref=_piece(part_refs[t], kinds[t], 2 * ox + oy, shard_shapes[t]), dst_ref=land_refs[t].at[j],
                    send_sem=send_sems.at[3 * t + j], recv_sem=recv_sems.at[3 * t + j],
                    device_id=(ox, oy, c), device_id_type=MESH).start()
        write(token, refs[:n_in])

    sems = pltpu.SemaphoreType.DMA((3 * nt,))
    both = list(parts) + lands
    out = pl.pallas_call(
        body, name=name, in_specs=[HBM_SPEC] * (2 * nt) + given_specs,
        out_specs=(SEM_SPEC, SEM_SPEC, *[HBM_SPEC] * (2 * nt), pl.BlockSpec(memory_space=pltpu.VMEM)),
        out_shape=(sems, sems, *[pltpu.HBM(a.shape, a.dtype) for a in both], token_type),
        input_output_aliases={t: 2 + t for t in range(2 * nt)}, compiler_params=_split_params(),
    )(*[_in_hbm(a) for a in both], *given)
    return out[0], out[1], list(out[2:2 + nt]), list(out[2 + nt:2 + 2 * nt]), out[-1]


def exchange_wait(name, send_sems, recv_sems, parts, lands, kinds, shard_shapes, after):
    nt = len(parts)

    def body(*refs):
        part_refs, land_refs = refs[:nt], refs[nt:2 * nt]
        send_ref, recv_ref = refs[2 * nt], refs[2 * nt + 1]
        x, y, c, others = _position()
        for t in range(nt):
            for j, (ox, oy) in enumerate(others):
                cp = pltpu.make_async_remote_copy(
                    src_ref=_piece(part_refs[t], kinds[t], 2 * ox + oy, shard_shapes[t]), dst_ref=land_refs[t].at[j],
                    send_sem=send_ref.at[3 * t + j], recv_sem=recv_ref.at[3 * t + j],
                    device_id=(ox, oy, c), device_id_type=MESH)
                cp.wait_send()
                cp.wait_recv()

    both = list(parts) + list(lands)
    out = pl.pallas_call(
        body, name=name, in_specs=[HBM_SPEC] * (2 * nt) + [SEM_SPEC, SEM_SPEC, HBM_SPEC], out_specs=[HBM_SPEC] * (2 * nt),
        out_shape=[pltpu.HBM(a.shape, a.dtype) for a in both], input_output_aliases={t: t for t in range(2 * nt)},
        compiler_params=_split_params())(*both, send_sems, recv_sems, _in_hbm(after))
    return list(out[:nt]), list(out[nt:])


def reduce_swap(bufs, wire):
    n = len(bufs)
    halves = [b.shape[0] // 2 for b in bufs]

    def body(*refs):
        in_refs, out_refs, txs, got = refs[:n], refs[n:2 * n], refs[2 * n:3 * n], refs[3 * n:4 * n]
        send_sems, recv_sems = refs[4 * n:]
        x, y, c, _ = _position()
        cps = []
        for k in range(n):
            txs[k][...] = in_refs[k][pl.ds(pl.multiple_of((1 - c) * halves[k], 8), halves[k]), :].astype(wire[k])
            cp = pltpu.make_async_remote_copy(src_ref=txs[k], dst_ref=got[k], send_sem=send_sems.at[k],
                                              recv_sem=recv_sems.at[k], device_id=(x, y, 1 - c), device_id_type=MESH)
            cp.start()
            cps.append(cp)
        for k, cp in enumerate(cps):
            cp.wait()
            own = in_refs[k][pl.ds(pl.multiple_of(c * halves[k], 8), halves[k]), :]
            out_refs[k][...] = (own.astype(wire[k]).astype(F32) + got[k][...].astype(F32)).astype(wire[k])

    vm = pl.BlockSpec(memory_space=pltpu.VMEM)
    parts = [((h, b.shape[1]), w) for h, b, w in zip(halves, bufs, wire)]
    return list(pl.pallas_call(
        body, name="reduce_swap", in_specs=[vm] * n, out_specs=[vm] * n, out_shape=[_sds(sh, w) for sh, w in parts],
        scratch_shapes=[pltpu.VMEM(sh, w) for sh, w in parts] * 2 + [pltpu.SemaphoreType.DMA((n,))] * 2,
        compiler_params=_params())(*bufs))


def reduce_start(parts):
    n = len(parts)
    lands = [lax.empty((4,) + tuple(p.shape), p.dtype) for p in parts]

    def body(*refs):
        part_refs, land_refs, send_sems, recv_sems = refs[:n], refs[n:2 * n], refs[2 * n], refs[2 * n + 1]
        x, y, c, others = _position()
        for k in range(n):
            for j, (ox, oy) in enumerate(others):
                pltpu.make_async_remote_copy(
                    src_ref=part_refs[k], dst_ref=land_refs[k].at[2 * x + y], send_sem=send_sems.at[3 * k + j],
                    recv_sem=recv_sems.at[3 * k + j], device_id=(ox, oy, c), device_id_type=MESH).start()

    sems = pltpu.SemaphoreType.DMA((3 * n,))
    both = list(parts) + lands
    out = pl.pallas_call(
        body, name="reduce_start", in_specs=[HBM_SPEC] * (2 * n), out_specs=(SEM_SPEC, SEM_SPEC, *[HBM_SPEC] * (2 * n)),
        out_shape=(sems, sems, *[pltpu.HBM(a.shape, a.dtype) for a in both]),
        input_output_aliases={k: 2 + k for k in range(2 * n)}, compiler_params=_split_params(),
    )(*[_in_hbm(a) for a in both])
    return out[0], out[1], list(out[2:2 + n]), list(out[2 + n:])


def reduce_wait(send_sems, recv_sems, parts, lands, after):
    n = len(parts)

    def body(*refs):
        part_refs, land_refs, send_ref, recv_ref = refs[:n], refs[n:2 * n], refs[2 * n], refs[2 * n + 1]
        x, y, c, others = _position()
        for k in range(n):
            for j, (ox, oy) in enumerate(others):
                cp = pltpu.make_async_remote_copy(
                    src_ref=part_refs[k], dst_ref=land_refs[k].at[2 * ox + oy], send_sem=send_ref.at[3 * k + j],
                    recv_sem=recv_ref.at[3 * k + j], device_id=(ox, oy, c), device_id_type=MESH)
                cp.wait_send()
                cp.wait_recv()

    both = list(parts) + list(lands)
    out = pl.pallas_call(
        body, name="reduce_wait", in_specs=[HBM_SPEC] * (2 * n) + [SEM_SPEC, SEM_SPEC, HBM_SPEC],
        out_specs=[HBM_SPEC] * (2 * n), out_shape=[pltpu.HBM(a.shape, a.dtype) for a in both],
        input_output_aliases={k: k for k in range(2 * n)}, compiler_params=_split_params(),
    )(*both, send_sems, recv_sems, _in_hbm(after))
    return list(out[:n]), list(out[n:])


def reduce_share(parts, lands):
    n = len(parts)
    halves = [p.shape[0] for p in parts]

    def body(*refs):
        part_refs, land_refs, out_refs = refs[:n], refs[n:2 * n], refs[2 * n:3 * n]
        send_sems, recv_sems = refs[3 * n:]
        x, y, c, _ = _position()
        chip = 2 * x + y
        cps = []
        for k in range(n):
            mine = pl.ds(pl.multiple_of(c * halves[k], 8), halves[k])
            own = part_refs[k][...].astype(F32)
            total = jnp.where(chip == 0, own, land_refs[k][0].astype(F32))
            for entry in range(1, 4):
                total = total + jnp.where(chip == entry, own, land_refs[k][entry].astype(F32))
            out_refs[k][mine, :] = total
            cp = pltpu.make_async_remote_copy(
                src_ref=out_refs[k].at[mine], dst_ref=out_refs[k].at[mine], send_sem=send_sems.at[k],
                recv_sem=recv_sems.at[k], device_id=(x, y, 1 - c), device_id_type=MESH)
            cp.start()
            cps.append(cp)
        for cp in cps:
            cp.wait()

    vm = pl.BlockSpec(memory_space=pltpu.VMEM)
    return list(pl.pallas_call(
        body, name="reduce_share", in_specs=[vm] * (2 * n), out_specs=[vm] * n,
        out_shape=[_sds((2 * p.shape[0], p.shape[1]), F32) for p in parts],
        scratch_shapes=[pltpu.SemaphoreType.DMA((n,))] * 2, compiler_params=_params())(*parts, *lands))


def _local_step(x, target, small, need, ahead, emit_swap, emit_exchange):
    d = D_MODEL
    full = {}

    def handed(vec, token):
        return vec if token is None else token

    def token_rows(token):
        return [] if token is None else [token]

    def plus(acc, rows):
        return acc + rows[0] if rows else acc

    rb16, rbt16, rc16, rct16, lr_t, li_t = small["s5_operands"]
    ge, ge_slope, cs = s5_fwd(x, small["norm_mix0"], small["s5_d"], rb16, rc16, lr_t, li_t)
    full.update(need("glu", ge))

    def norm_rows(h, gains):
        xh, _ = _rms_hat(h)
        return [xh * g for g in gains]

    def glu_epilogue(accs, e, r):
        v, gt = accs[0] + r[0], accs[1] + r[1]
        h = e[0] + v * jax.nn.sigmoid(gt)
        return [h, v, gt] + norm_rows(h, r[2:])

    gain_mlp0 = handed(small["norm_mlp0"], ahead("mlp_in0", full["w_glu"], small["norm_mlp0"]))
    h1, val, gate, n1 = mm_nn(
        "glu", ge, full["w_glu"], [0, d], d, glu_epilogue, [F32, F32, F32, BF16], extras=[x],
        rowvecs=[(small["s5_b_glu"], 0), (small["s5_b_glu"], d), (gain_mlp0, 0)], tm=512, tn=d)

    def mlp_fwd(tag, h, n, w_in, get_w_out, next_gains, head=None):
        def in_epilogue(accs, e, rv):
            pos = jnp.maximum(accs[0], 0.0)
            return [pos * pos, 2.0 * pos]

        r, slope = mm_nn("mlp_in" + tag, n, w_in, [0], w_in.shape[1], in_epilogue, [BF16, BF16], tm=2048)
        w_out = get_w_out(r)

        def epilogue(accs, e, rv):
            h_out = e[0] + accs[0]
            return [h_out] + norm_rows(h_out, rv)

        if head is not None:
            return head(r, w_out, h), (n, r, slope)
        outs = mm_nn("mlp_out" + tag, r, w_out, [0], d, epilogue, [F32] + [BF16] * len(next_gains), extras=[h],
                     rowvecs=[(g, 0) for g in next_gains], tm=512, tn=d)
        return outs[0], outs[1:], (n, r, slope)

    full.update(need("mlp_in0", h1))

    def w_out0(after):
        full.update(need("mlp_out0", after))
        return full["w_out0"]

    h2, (nkv, n2), mlp0 = mlp_fwd("0", h1, n1, full["w_in0"], w_out0, [small["norm_kv"], small["norm_mix1"]])

    full.update(need("attn", h2))
    kvw = 2 * N_KV * HEAD_DIM
    (kv,) = mm_nn("kv_proj", nkv, full["w_kv"], [0], kvw, lambda accs, e, r: [accs[0] + r[0]], [BF16],
                  rowvecs=[(small["b_kv"], 0)], tm=2048)
    (q,) = mm_nn("q_proj", n2, full["w_q"], [0], d, lambda accs, e, r: [accs[0] + r[0]], [BF16],
                 rowvecs=[(small["b_q"], 0)], tm=2048)
    sinks = small["sinks"].reshape(N_Q)
    o = attn_fwd(q, kv, sinks)
    def o_epilogue(accs, e, r):
        h_out = e[0] + accs[0] + r[0]
        return [h_out] + norm_rows(h_out, r[1:])

    bias_o = handed(small["b_o"], ahead("mlp_in1", o, small["b_o"]))
    h3, n3 = mm_nn("o_proj", o, full["w_o"], [0], d, o_epilogue, [F32, BF16], extras=[h2],
                   rowvecs=[(bias_o, 0), (small["norm_mlp1"], 0)], tm=512, tn=d)
    full.update(need("mlp_in1", h3, then="mlp_out1"))

    def w_out1(after):
        full.update(need("mlp_out1", after))
        return full["w_out1"]

    def loss_head(r, w_out, h):
        def epilogue(accs, e, rv):
            xh, rr = _rms_hat(e[0] + accs[0])
            err = xh * rv[0] - e[1]
            dy = err * (1.0 / d)
            dxh = dy * rv[0]
            dx = rr * (dxh - xh * jnp.mean(dxh * xh, axis=-1, keepdims=True))
            loss = jnp.full((1, d), 0.5 * jnp.sum(jnp.mean(err * err, axis=-1, keepdims=True)), F32)
            return [dx, dx, loss, jnp.sum(dy * xh, axis=0, keepdims=True)]

        return mm_nn("mlp_out1", r, w_out, [0], d, epilogue, [F32, BF16], extras=[h, target],
                     rowvecs=[(small["norm_final"], 0)], n_sums=2, tm=512, tn=d)

    (dh, dhb, loss_tile, dg_final), mlp1 = mlp_fwd("1", h3, n3, full["w_in1"], w_out1, [], head=loss_head)

    grads_small, grads_full = {"norm_final": dg_final}, {}
    ident = lambda acc, e, r: [plus(acc, r)]
    layer1 = ["w_out1", "w_in1", "w_o", "w_q", "w_kv"]
    layer0 = ["w_out0", "w_in0", "w_glu"]

    def norm_bwd_rows(x_rows, res, dys, gains):
        xh, r = _rms_hat(x_rows)
        dxh = sum(dy * g for dy, g in zip(dys, gains))
        dx = r * (dxh - xh * jnp.mean(dxh * xh, axis=-1, keepdims=True)) + res
        return dx, [jnp.sum(dy * xh, axis=0, keepdims=True) for dy in dys]

    def mlp_bwd(tag, dh, dhb, h_in, gain, w_in, w_out, saved, token=None):
        n, r, slope = saved
        grads_full["w_out" + tag] = mm_tn("dw_out" + tag, r, dhb, tn=1024)
        (da,) = mm_nt("mlp_da" + tag, dhb, w_out, lambda acc, e, rv: [plus(acc * e[0].astype(F32), rv)], [BF16],
                      extras=[slope], rowvecs=token_rows(token), tm=2048)
        grads_full["w_in" + tag] = mm_tn("dw_in" + tag, n, da, tn=1024)

        def epilogue(acc, e, rv):
            dx, dgs = norm_bwd_rows(e[0], e[1], [acc], rv)
            return [dx, dx, jnp.sum(dx, axis=0, keepdims=True)] + dgs

        dx, dxb, colsum, dg = mm_nt("mlp_dn" + tag, da, w_in, epilogue, [F32, BF16], extras=[h_in, dh], rowvecs=[gain],
                                    n_sums=2, tm=512, tk=d)
        grads_small["norm_mlp" + tag] = dg
        return dx, dxb, colsum

    dh3, dh3b, colsum3 = mlp_bwd("1", dh, dhb, h3, small["norm_mlp1"], full["w_in1"], full["w_out1"], mlp1)
    grads_small["b_o"] = colsum3
    grads_full["w_o"] = mm_tn("dw_o", o, dh3b, tn=1024)
    (do,) = mm_nt("attn_do", dh3b, full["w_o"], ident, [BF16], tm=2048)
    dq, dbq, dprev, dcur, dsink = attn_bwd(q, kv, do, sinks)
    dkv, dbkv = kv_combine(dprev, dcur)
    grads_small["b_q"], grads_small["b_kv"], grads_small["sinks"] = dbq, dbkv, dsink
    grads_full["w_q"] = mm_tn("dw_q", n2, dq, tn=1024)
    grads_full["w_kv"] = mm_tn("dw_kv", nkv, dkv, tk=1024)
    token = emit_swap("layer1", {n: grads_full[n] for n in layer1}, (1, d))
    (dnkv,) = mm_nt("kv_dn", dkv, full["w_kv"], ident, [F32], rowvecs=token_rows(token), tm=2048, tk=1024)

    def attn_dn_epilogue(acc, e, rv):
        dx, dgs = norm_bwd_rows(e[0], e[1], [acc, e[2]], rv)
        return [dx, dx] + dgs

    dh2, dh2b, dg_mix1, dg_kv = mm_nt("attn_dn", dq, full["w_q"], attn_dn_epilogue, [F32, BF16], extras=[h2, dh3, dnkv],
                                      rowvecs=[small["norm_mix1"], small["norm_kv"]], n_sums=2, tm=512, tk=d)
    grads_small["norm_mix1"], grads_small["norm_kv"] = dg_mix1, dg_kv
    token = emit_exchange("layer1", dh2b, (1, full["w_out0"].shape[0]))
    dh1, _, _ = mlp_bwd("0", dh2, dh2b, h1, small["norm_mlp0"], full["w_in0"], full["w_out0"], mlp0, token)

    dz, db_glu = glu_bwd(dh1, val, gate)
    grads_small["s5_b_glu"] = db_glu
    grads_full["w_glu"] = mm_tn("dw_glu", ge, dz, tn=1024)
    token = emit_swap("layer0", {n: grads_full[n] for n in layer0}, (1, d))
    (dy2,) = mm_nt("glu_dy", dz, full["w_glu"], lambda acc, e, rv: [plus(acc, rv) * e[0]], [F32], extras=[ge_slope],
                   rowvecs=token_rows(token), tm=512, tk=1024)
    d_skip = handed(small["s5_d"], emit_exchange("layer0", dy2, small["s5_d"]))
    grad_x, dd, drb, drc, dlr, dli, dg_mix0 = s5_bwd(x, small["norm_mix0"], dy2, dh1, d_skip, cs, rb16, rbt16, rct16, lr_t, li_t)
    grads_small["s5_d"] = dd
    grads_small["s5_mats"] = (drb, drc, dlr, dli)
    grads_small["norm_mix0"] = dg_mix0
    return loss_tile, grad_x, grads_small


SMALL_NAMES = ["norm_mix", "norm_mlp", "norm_kv", "norm_final", "s5_a_re", "s5_a_im", "s5_log_dt", "s5_b_re", "s5_b_im",
               "s5_c_re", "s5_c_im", "s5_d", "s5_b_glu", "b_kv", "b_q", "sinks", "b_o"]
BIG_NAMES = ["s5_w_glu", "w_kv", "w_q", "w_o", "w_mlp_in", "w_mlp_out"]
WEIGHT_ORDER = ["norm_mix", "norm_mlp", "norm_kv", "norm_final", "s5_a_re", "s5_a_im", "s5_log_dt", "s5_b_re", "s5_b_im",
                "s5_c_re", "s5_c_im", "s5_d", "s5_w_glu", "s5_b_glu", "w_kv", "b_kv", "w_q", "b_q", "sinks", "w_o", "b_o",
                "w_mlp_in", "w_mlp_out"]


def kernel(x, norm_mix, norm_mlp, norm_kv, norm_final, s5_a_re, s5_a_im, s5_log_dt, s5_b_re, s5_b_im, s5_c_re, s5_c_im, s5_d, s5_w_glu, s5_b_glu, w_kv, b_kv, w_q, b_q, sinks, w_o, b_o, w_mlp_in, w_mlp_out, loss_target, m_norm_mix, m_norm_mlp, m_norm_kv, m_norm_final, m_s5_a_re, m_s5_a_im, m_s5_log_dt, m_s5_b_re, m_s5_b_im, m_s5_c_re, m_s5_c_im, m_s5_d, m_s5_w_glu, m_s5_b_glu, m_w_kv, m_b_kv, m_w_q, m_b_q, m_sinks, m_w_o, m_b_o, m_w_mlp_in, m_w_mlp_out, v_norm_mix, v_norm_mlp, v_norm_kv, v_norm_final, v_s5_a_re, v_s5_a_im, v_s5_log_dt, v_s5_b_re, v_s5_b_im, v_s5_c_re, v_s5_c_im, v_s5_d, v_s5_w_glu, v_s5_b_glu, v_w_kv, v_b_kv, v_w_q, v_b_q, v_sinks, v_w_o, v_b_o, v_w_mlp_in, v_w_mlp_out):
    env = dict(locals())
    w = {n: env[n] for n in WEIGHT_ORDER}
    mom = {n: env["m_" + n] for n in WEIGHT_ORDER}
    var = {n: env["v_" + n] for n in WEIGHT_ORDER}
    d = D_MODEL
    xi, yi, ci = lax.axis_index("x"), lax.axis_index("y"), lax.axis_index("c")
    chip = 2 * xi + yi
    where = jnp.stack([ci, chip]).astype(jnp.int32)

    dsh, bsh = s5_d.shape[1], s5_b_glu.shape[1]
    packed = jnp.concatenate([s5_d.reshape(-1, 128), s5_b_glu.reshape(-1, 128)])
    n_d, n_b = dsh // 128, bsh // 128
    slab = lax.dynamic_update_slice(jnp.zeros((4, 8, 128), F32), jnp.pad(packed, ((0, 8 - n_d - n_b), (0, 0)))[None],
                                    (chip, 0, 0))

    big = [s5_w_glu, w_kv[None], w_q, w_o, w_mlp_in, w_mlp_out]
    entries = [(0, 0, "col"), (1, 0, "row"), (2, 0, "row"), (3, 0, "row"), (4, 0, "col"), (4, 1, "col"),
               (5, 0, "row"), (5, 1, "row")]
    names = ["w_glu", "w_kv", "w_q", "w_o", "w_in0", "w_in1", "w_out0", "w_out1"]
    kinds = dict(zip(names, [k for _, _, k in entries]))
    shard_shapes = dict(zip(names, [tuple(big[a].shape[1:]) for a, _, _ in entries]))

    placed_w = dict(zip(names, cast_place(big, entries, where)))
    placed_w["vectors"], kinds["vectors"], shard_shapes["vectors"] = slab, "slab", None
    gather_groups = {"glu": ["w_glu"], "mlp_in0": ["w_in0"], "mlp_out0": ["w_out0"], "attn": ["w_kv", "w_q", "w_o"],
                     "mlp_in1": ["w_in1"], "mlp_out1": ["w_out1"]}
    order = ["vectors"] + [n for members in gather_groups.values() for n in members]
    send, recv, thru, log_dt = gather_start([placed_w[n] for n in order], [kinds[n] for n in order],
                                            [shard_shapes[n] for n in order], s5_log_dt)
    started = dict(zip(order, thru))
    (gathered_rows,) = gather_wait("gather_wait_vectors", send, recv, [started["vectors"]], ["slab"], [None], None, 0)
    d_full = gathered_rows[:, 0:n_d].reshape(1, -1)
    bglu_full = gathered_rows[:, n_d:n_d + n_b].reshape(1, -1)

    forwarding = {}

    def ahead(group, after, carry, passing=()):
        members = gather_groups[group]
        ks, shapes = [kinds[n] for n in members], [shard_shapes[n] for n in members]
        d2d_send, d2d_recv, landed, passed, tok = forward_start(
            "forward_start_" + group, send, recv, [started[n] for n in members], ks, shapes, after,
            order.index(members[0]), carry, passing)
        forwarding[group] = (d2d_send, d2d_recv, landed)
        return passed if passing else tok

    def need(group, after, then=None):
        members = gather_groups[group]
        ks, shapes = [kinds[n] for n in members], [shard_shapes[n] for n in members]
        if group in forwarding:
            arrays = forward_wait("forward_wait_" + group, *forwarding[group], ks, shapes, after)
        else:
            landed = gather_wait("gather_wait_" + group, send, recv, [started[n] for n in members], ks, shapes, after,
                                 order.index(members[0]))
            arrays = forward_halves("forward_halves_" + group, landed, ks, shapes)
        if then is not None:
            arrays = ahead(then, after, (8, 128), arrays)
        return dict(zip(members, arrays))

    swapping, exchanging = {}, {}

    def emit_swap(group, partial, carry):
        members = list(partial)
        send, recv, mine, lands, tok = swap_start("swap_start_" + group, [partial[n] for n in members],
                                                  [kinds[n] for n in members], carry)
        swapping[group] = (members, send, recv, mine, lands)
        return tok

    def emit_exchange(group, after, carry):
        members, send, recv, mine, lands = swapping[group]
        ks, shapes = [kinds[n] for n in members], [shard_shapes[n] for n in members]
        mine, landed = swap_wait("swap_wait_" + group, send, recv, mine, lands, ks, after)
        sums = add_halves("add_halves_" + group, mine, landed, ks, where)
        send, recv, parts, lands, tok = exchange_start("exchange_start_" + group, sums, ks, shapes, carry)
        exchanging[group] = (members, send, recv, parts, lands)
        return tok

    s5_args = (s5_a_re[0], s5_a_im[0], log_dt[0], s5_b_re[0], s5_b_im[0])
    small = {
        "norm_mix0": norm_mix[0:1], "norm_mix1": norm_mix[1:2], "norm_mlp0": norm_mlp[0:1], "norm_mlp1": norm_mlp[1:2],
        "norm_kv": norm_kv.reshape(1, d), "norm_final": norm_final.reshape(1, d), "s5_operands": s5_prep(*s5_args, s5_c_re[0], s5_c_im[0]),
        "s5_d": d_full, "s5_b_glu": bglu_full,
        "b_kv": b_kv.reshape(1, -1), "b_q": b_q, "sinks": sinks, "b_o": b_o,
    }
    loss_row, grad_x, gs = _local_step(x[0], loss_target[0], small, need, ahead, emit_swap, emit_exchange)

    mats, lams = s5_compact(*gs["s5_mats"])
    rows = [gs["norm_mix0"], gs["norm_mix1"], gs["norm_mlp0"], gs["norm_mlp1"], gs["norm_kv"], gs["norm_final"], gs["s5_d"],
            gs["b_q"], gs["b_o"], gs["s5_b_glu"], gs["b_kv"], gs["sinks"], loss_row, jnp.zeros((2, d), F32)]
    small_send, small_recv, small_parts, small_lands = reduce_start(
        reduce_swap([jnp.concatenate(rows, axis=0), lams, mats], [F32, F32, BF16]))

    reduced = [None] * len(big)
    where_of = dict(zip(names, entries))
    for group in ("layer1", "layer0"):
        members, send, recv, parts, lands = exchanging[group]
        ks, shapes = [kinds[n] for n in members], [shard_shapes[n] for n in members]
        parts, lands = exchange_wait("exchange_wait_" + group, send, recv, parts, lands, ks, shapes, small_lands[-1])
        targets = [where_of[n][0] for n in members]
        sums = sum_shards("sum_shards_" + group, parts, lands, ks, shapes, where, [where_of[n][1] for n in members],
                          [big[a].shape[0] for a in targets], [reduced[a] for a in targets])
        for a, arr in zip(targets, sums):
            reduced[a] = arr
    share_send, share_recv, reduced, _ = share_start(reduced, entries, (8, 128))

    vecs, lams, mats = reduce_share(*reduce_wait(small_send, small_recv, small_parts, small_lands, reduced[0]))
    grads = split_vectors(where, vecs, dsh, bsh)
    loss = grads.pop("loss")[0, 0]
    g_are, g_aim, g_dt, g_bre, g_bim, dc_re, dc_im = s5_param_bwd(mats, lams, *s5_args)
    grads.update({"s5_a_re": g_are[None], "s5_a_im": g_aim[None], "s5_log_dt": g_dt[None], "s5_b_re": g_bre[None],
                  "s5_b_im": g_bim[None], "s5_c_re": dc_re[None], "s5_c_im": dc_im[None]})

    delta, new_m, new_v = {}, {}, {}

    def view(n, a):
        return a.reshape(1, -1) if a.ndim == 1 else jnp.swapaxes(a, -1, -2) if n in ("s5_b_re", "s5_b_im") else a

    sw, sg, sm, sv = ([view(n, t[n]) for n in SMALL_NAMES] for t in (w, grads, mom, var))
    for n, a, b, c_ in zip(SMALL_NAMES, *adamw_native("adamw_small", sw, sg, sm, sv)):
        delta[n], new_m[n], new_v[n] = (view(n, t) if t.ndim == 4 else t for t in (a, b, c_))

    reduced = share_wait(share_send, share_recv, reduced, entries, new_v["s5_c_re"])
    for n, g in zip(BIG_NAMES, reduced):
        grads[n] = g.reshape(w[n].shape)
    flat = lambda t: [t[n].reshape(-1, t[n].shape[-1]) for n in BIG_NAMES]
    for table, arrays in zip((grads, delta, new_m, new_v), adamw("adamw_big", flat(w), flat(grads), flat(mom), flat(var))):
        for n, a in zip(BIG_NAMES, arrays):
            table[n] = a.reshape(w[n].shape)

    out = [loss.reshape(()), grad_x[None]]
    for table in (grads, delta, new_m, new_v):
        out += [table[n].reshape(w[n].shape) for n in WEIGHT_ORDER]
    return tuple(out)
```

```python
import math

import jax
import jax.numpy as jnp
from jax import lax
from jax.experimental import pallas as pl
from jax.experimental.pallas import tpu as pltpu

F32 = jnp.float32
BF16 = jnp.bfloat16

D_MODEL = 1024
S5_GROUPS = 64
S5_GROUP = 16
S5_STATE = 64
N_KV = 4
N_Q = 16
HEAD_DIM = 64
BLOCK = 128
NORM_EPS = 1e-5
LAMBDA_RE_MAX = -1e-4
ADAM_LR, ADAM_B1, ADAM_B2, ADAM_EPS, ADAM_WD, ADAM_STEP = 0.001, 0.9, 0.999, 1e-08, 0.01, 10

VMEM_LIMIT_BYTES = 56 * 1024 * 1024
S5_CHUNK = 256
S5_BLOCKS = 4
MESH = pl.DeviceIdType.MESH


def _params(sem=None):
    return pltpu.CompilerParams(dimension_semantics=sem, vmem_limit_bytes=VMEM_LIMIT_BYTES)


def _sds(shape, dtype):
    return jax.ShapeDtypeStruct(shape, dtype)


def _rms_hat(xv):
    r = lax.rsqrt(jnp.mean(xv * xv, axis=-1, keepdims=True) + NORM_EPS)
    return xv * r, r


def mm_nn(name, a, w, col_offsets, n_out, epilogue, out_dtypes, extras=(), rowvecs=(), n_sums=0, tm=1024, tn=512):
    m, k = a.shape
    tm, tn = min(tm, m), min(tn, n_out)
    nw, ne, nr, no = len(col_offsets), len(extras), len(rowvecs), len(out_dtypes)

    def body(a_ref, *refs):
        w_refs, e_refs, r_refs = refs[:nw], refs[nw:nw + ne], refs[nw + ne:nw + ne + nr]
        o_refs, s_refs = refs[nw + ne + nr:nw + ne + nr + no], refs[nw + ne + nr + no:]
        av = a_ref[...]
        accs = [jnp.dot(av, w_ref[...], preferred_element_type=F32) for w_ref in w_refs]
        outs = epilogue(accs, [e[...] for e in e_refs], [r[...] for r in r_refs])
        for o_ref, o in zip(o_refs, outs[:no]):
            o_ref[...] = o.astype(o_ref.dtype)
        if n_sums:
            @pl.when(pl.program_id(1) == 0)
            def _():
                for s_ref in s_refs:
                    s_ref[...] = jnp.zeros_like(s_ref)

            for s_ref, val in zip(s_refs, outs[no:]):
                s_ref[...] += val

    def wspec(off):
        return pl.BlockSpec((k, tn), lambda j, i, off=off: (0, off // tn + j))

    def rspec(off):
        return pl.BlockSpec((1, tn), lambda j, i, off=off: (0, off // tn + j))

    tile = pl.BlockSpec((tm, tn), lambda j, i: (i, j))
    in_specs = ([pl.BlockSpec((tm, k), lambda j, i: (i, 0))] + [wspec(o) for o in col_offsets]
                + [tile] * ne + [rspec(o) for _, o in rowvecs])
    sem = ("parallel", "arbitrary") if n_sums else ("parallel", "parallel")
    return pl.pallas_call(
        body, grid=(n_out // tn, m // tm), in_specs=in_specs,
        out_specs=[tile] * no + [pl.BlockSpec((1, tn), lambda j, i: (0, j))] * n_sums,
        out_shape=[_sds((m, n_out), dt) for dt in out_dtypes] + [_sds((1, n_out), F32)] * n_sums, name=name,
        compiler_params=_params(sem))(a, *([w] * nw), *extras, *[r for r, _ in rowvecs])


def mm_nt(name, g, w, epilogue, out_dtypes, extras=(), rowvecs=(), n_sums=0, tm=512, tk=512):
    m, n = g.shape
    k = w.shape[0]
    tm, tk = min(tm, m), min(tk, k)
    ne, nr, no = len(extras), len(rowvecs), len(out_dtypes)

    def body(g_ref, w_ref, *refs):
        e_refs, r_refs, o_refs, s_refs = refs[:ne], refs[ne:ne + nr], refs[ne + nr:ne + nr + no], refs[ne + nr + no:]
        acc = lax.dot_general(g_ref[...], w_ref[...], (((1,), (1,)), ((), ())), preferred_element_type=F32)
        outs = epilogue(acc, [e[...] for e in e_refs], [r[...] for r in r_refs])
        for o_ref, o in zip(o_refs, outs[:no]):
            o_ref[...] = o.astype(o_ref.dtype)
        if n_sums:
            @pl.when(pl.program_id(0) == 0)
            def _():
                for s_ref in s_refs:
                    s_ref[...] = jnp.zeros_like(s_ref)

            for s_ref, val in zip(s_refs, outs[no:]):
                s_ref[...] += val

    tile = pl.BlockSpec((tm, tk), lambda i, j: (i, j))
    vec = pl.BlockSpec((1, tk), lambda i, j: (0, j))
    sem = ("arbitrary", "parallel") if n_sums else ("parallel", "parallel")
    return pl.pallas_call(
        body, grid=(m // tm, k // tk),
        in_specs=[pl.BlockSpec((tm, n), lambda i, j: (i, 0)), pl.BlockSpec((tk, n), lambda i, j: (j, 0))]
        + [tile] * ne + [vec] * nr,
        out_specs=[tile] * no + [vec] * n_sums,
        out_shape=[_sds((m, k), dt) for dt in out_dtypes] + [_sds((1, k), F32)] * n_sums, name=name,
        compiler_params=_params(sem))(g, w, *extras, *rowvecs)


def mm_tn(name, a, g, tk=512, tn=512):
    m, k = a.shape
    n = g.shape[1]
    tk, tn = min(tk, k), min(tn, n)

    def body(a_ref, g_ref, o_ref):
        acc = lax.dot_general(a_ref[...], g_ref[...], (((0,), (0,)), ((), ())), preferred_element_type=F32)
        o_ref[...] = acc.astype(o_ref.dtype)

    return pl.pallas_call(
        body, grid=(k // tk, n // tn),
        in_specs=[pl.BlockSpec((m, tk), lambda i, j: (0, i)), pl.BlockSpec((m, tn), lambda i, j: (0, j))],
        out_specs=pl.BlockSpec((tk, tn), lambda i, j: (i, j)), out_shape=_sds((k, n), BF16), name=name,
        compiler_params=_params(("parallel", "parallel")))(a, g)


def _row_mask(tc):
    row = lax.broadcasted_iota(jnp.int32, (8 * tc, 256), 0) % 8
    col = lax.broadcasted_iota(jnp.int32, (8 * tc, 256), 1) // 32
    return row == col


def _expand_rows(val, mask):
    tc, width = val.shape
    rep = jnp.broadcast_to(val[:, None, :], (tc, 8, width)).reshape(8 * tc, width)
    return jnp.where(mask, rep, 0.0).astype(BF16)


def _stage(ref, val):
    ref[0] = val[:, 0:128]
    ref[1] = val[:, 128:256]


def _gather_rows(src_ref, tc):
    halves = []
    for half in range(2):
        col = lax.broadcasted_iota(jnp.int32, (tc, 128), 1) // 32 + 4 * half
        out = jnp.zeros((tc, 128), F32)
        for s8 in range(4 * half, 4 * half + 4):
            out = jnp.where(col == s8, src_ref.at[half][pl.ds(s8, tc, stride=8), :], out)
        halves.append(out)
    return jnp.concatenate(halves, axis=1)


def _repeat(n, by, step, carry):
    def trip(i, c):
        for j in range(by):
            c = step(i * by + j, c)
        return c

    return lax.fori_loop(0, n // by, trip, carry)


def _gelu_and_slope(x):
    c = math.sqrt(2.0 / math.pi)
    t = jnp.tanh(c * (x + 0.044715 * x * x * x))
    return 0.5 * x * (1.0 + t), 0.5 * (1.0 + t) + 0.5 * x * (1.0 - t * t) * c * (1.0 + 3.0 * 0.044715 * x * x)


def s5_fwd(x, gain, d_skip, rb, rc, lam_r, lam_i):
    n_rows = x.shape[0]
    tc = min(S5_CHUNK, n_rows)
    nc = n_rows // tc

    def body(x_ref, g_ref, d_ref, rb_ref, rc_ref, lr_ref, li_ref, ge_ref, slope_ref, cs_ref, bux, yrows, carry):
        i = pl.program_id(0)
        u = _rms_hat(x_ref[...])[0] * g_ref[...]

        @pl.when(i == 0)
        def _():
            carry[...] = jnp.zeros_like(carry)

        cs_ref[0] = carry[...]
        mask = _row_mask(tc)
        for blk in range(S5_BLOCKS):
            lhs = _expand_rows(u[:, blk * 256:(blk + 1) * 256], mask)
            bux[blk] = jnp.dot(lhs, rb_ref[blk], preferred_element_type=F32)
        lam = [(lr_ref[blk], li_ref[blk]) for blk in range(S5_BLOCKS)]

        def step(t, c):
            r0 = pl.multiple_of(t * 8, 8)
            new = []
            for blk in range(S5_BLOCKS):
                xr, xi = c[2 * blk], c[2 * blk + 1]
                lr, li = lam[blk]
                nr = lr * xr - li * xi + bux[blk, pl.ds(r0, 8), 0:128]
                ni = lr * xi + li * xr + bux[blk, pl.ds(r0, 8), 128:256]
                bux[blk, pl.ds(r0, 8), 0:128] = nr
                bux[blk, pl.ds(r0, 8), 128:256] = ni
                new += [nr, ni]
            return tuple(new)

        c0 = []
        for blk in range(S5_BLOCKS):
            c0 += [carry[blk, :, 0:128], carry[blk, :, 128:256]]
        cn = _repeat(tc, 8, step, tuple(c0))
        for blk in range(S5_BLOCKS):
            carry[blk, :, 0:128] = cn[2 * blk]
            carry[blk, :, 128:256] = cn[2 * blk + 1]
        for blk in range(S5_BLOCKS):
            _stage(yrows, jnp.dot(bux[blk].astype(BF16), rc_ref[blk], preferred_element_type=F32))
            sl = slice(blk * 256, (blk + 1) * 256)
            ge, slope = _gelu_and_slope(_gather_rows(yrows, tc) + d_ref[:, sl] * u[:, sl])
            slope_ref[:, sl] = slope
            ge_ref[:, sl] = ge.astype(BF16)

    row = pl.BlockSpec((tc, D_MODEL), lambda i: (i, 0))
    vec = pl.BlockSpec((1, D_MODEL), lambda i: (0, 0))
    mat = pl.BlockSpec((S5_BLOCKS, 256, 256), lambda i: (0, 0, 0))
    lamspec = pl.BlockSpec((S5_BLOCKS, 8, 128), lambda i: (0, 0, 0))
    return pl.pallas_call(
        body, grid=(nc,),
        in_specs=[row, vec, vec, mat, mat, lamspec, lamspec],
        out_specs=[row, row, pl.BlockSpec((1, S5_BLOCKS, 8, 256), lambda i: (i, 0, 0, 0))],
        out_shape=[_sds((n_rows, D_MODEL), BF16), _sds((n_rows, D_MODEL), F32), _sds((nc, S5_BLOCKS, 8, 256), F32)],
        scratch_shapes=[pltpu.VMEM((S5_BLOCKS, 8 * tc, 256), F32), pltpu.VMEM((2, 8 * tc, 128), F32),
                        pltpu.VMEM((S5_BLOCKS, 8, 256), F32)],
        name="s5_fwd", compiler_params=_params(("arbitrary",)))(x, gain, d_skip, rb, rc, lam_r, lam_i)


def s5_bwd(x, gain, dy2, res, d_skip, cs, rb, rbt, rct, lam_r, lam_i):
    n_rows = x.shape[0]
    tc = min(S5_CHUNK, n_rows)
    nc = n_rows // tc

    def body(x_ref, g_ref, dy_ref, res_ref, d_ref, cs_ref, rb_ref, rbt_ref, rct_ref, lr_ref, li_ref,
             dx_ref, dd_ref, drb_ref, drc_ref, dlr_ref, dli_ref, dg_ref, tmp, du, lhsu, lhsd, xs, adj, acarry):
        i = pl.program_id(0)
        u = _rms_hat(x_ref[...])[0] * g_ref[...]

        @pl.when(i == 0)
        def _():
            acarry[...] = jnp.zeros_like(acarry)
            dd_ref[...] = jnp.zeros_like(dd_ref)
            drb_ref[...] = jnp.zeros_like(drb_ref)
            drc_ref[...] = jnp.zeros_like(drc_ref)
            dlr_ref[...] = jnp.zeros_like(dlr_ref)
            dli_ref[...] = jnp.zeros_like(dli_ref)
            dg_ref[...] = jnp.zeros_like(dg_ref)

        dd_ref[...] += jnp.sum(dy_ref[...] * u, axis=0, keepdims=True)
        mask = _row_mask(tc)
        for blk in range(S5_BLOCKS):
            sl = slice(blk * 256, (blk + 1) * 256)
            lhsu[blk] = _expand_rows(u[:, sl], mask)
            xs[blk, 0:8] = cs_ref[0, blk]
            xs[blk, 8:8 * tc + 8] = jnp.dot(lhsu[blk], rb_ref[blk], preferred_element_type=F32)
            lhsd[blk] = _expand_rows(dy_ref[:, sl], mask)
            adj[blk] = jnp.dot(lhsd[blk], rct_ref[blk], preferred_element_type=F32)
        lam = [(lr_ref[blk], li_ref[blk]) for blk in range(S5_BLOCKS)]

        def fstep(t, c):
            r0 = pl.multiple_of(t * 8 + 8, 8)
            new = []
            for blk in range(S5_BLOCKS):
                xr, xi = c[2 * blk], c[2 * blk + 1]
                lr, li = lam[blk]
                nr = lr * xr - li * xi + xs[blk, pl.ds(r0, 8), 0:128]
                ni = lr * xi + li * xr + xs[blk, pl.ds(r0, 8), 128:256]
                xs[blk, pl.ds(r0, 8), 0:128] = nr
                xs[blk, pl.ds(r0, 8), 128:256] = ni
                new += [nr, ni]
            return tuple(new)

        c0 = []
        for blk in range(S5_BLOCKS):
            c0 += [cs_ref[0, blk, :, 0:128], cs_ref[0, blk, :, 128:256]]
        _repeat(tc, 8, fstep, tuple(c0))

        def bstep(k, c):
            t = tc - 1 - k
            r0 = pl.multiple_of(t * 8, 8)
            new_a, new_g = [], []
            for blk in range(S5_BLOCKS):
                ar, ai = c[0][2 * blk], c[0][2 * blk + 1]
                glr, gli = c[1][2 * blk], c[1][2 * blk + 1]
                lr, li = lam[blk]
                nr = lr * ar + li * ai + adj[blk, pl.ds(r0, 8), 0:128]
                ni = lr * ai - li * ar + adj[blk, pl.ds(r0, 8), 128:256]
                adj[blk, pl.ds(r0, 8), 0:128] = nr
                adj[blk, pl.ds(r0, 8), 128:256] = ni
                pr, pi = xs[blk, pl.ds(r0, 8), 0:128], xs[blk, pl.ds(r0, 8), 128:256]
                new_a += [nr, ni]
                new_g += [glr + nr * pr + ni * pi, gli + ni * pr - nr * pi]
            return tuple(new_a), tuple(new_g)

        a0, g0 = [], []
        for blk in range(S5_BLOCKS):
            a0 += [acarry[blk, :, 0:128], acarry[blk, :, 128:256]]
            g0 += [dlr_ref[blk], dli_ref[blk]]
        an, gn = _repeat(tc, 4, bstep, (tuple(a0), tuple(g0)))
        for blk in range(S5_BLOCKS):
            acarry[blk, :, 0:128] = an[2 * blk]
            acarry[blk, :, 128:256] = an[2 * blk + 1]
            dlr_ref[blk] = gn[2 * blk]
            dli_ref[blk] = gn[2 * blk + 1]
        for blk in range(S5_BLOCKS):
            sl = slice(blk * 256, (blk + 1) * 256)
            ab = adj[blk].astype(BF16)
            _stage(tmp, jnp.dot(ab, rbt_ref[blk], preferred_element_type=F32))
            du[:, sl] = _gather_rows(tmp, tc) + d_ref[:, sl] * dy_ref[:, sl]
            drb_ref[blk] += lax.dot_general(lhsu[blk], ab, (((0,), (0,)), ((), ())), preferred_element_type=F32)
            drc_ref[blk] += lax.dot_general(lhsd[blk], xs[blk, 8:8 * tc + 8].astype(BF16), (((0,), (0,)), ((), ())),
                                            preferred_element_type=F32)
        xh, r = _rms_hat(x_ref[...])
        dg_ref[...] += jnp.sum(du[...] * xh, axis=0, keepdims=True)
        dxh = du[...] * g_ref[...]
        dx_ref[...] = r * (dxh - xh * jnp.mean(dxh * xh, axis=-1, keepdims=True)) + res_ref[...]

    rev = pl.BlockSpec((tc, D_MODEL), lambda i: (nc - 1 - i, 0))
    vec = pl.BlockSpec((1, D_MODEL), lambda i: (0, 0))
    mat = pl.BlockSpec((S5_BLOCKS, 256, 256), lambda i: (0, 0, 0))
    lamspec = pl.BlockSpec((S5_BLOCKS, 8, 128), lambda i: (0, 0, 0))
    big = pltpu.VMEM((S5_BLOCKS, 8 * tc, 256), F32)
    bigb = pltpu.VMEM((S5_BLOCKS, 8 * tc, 256), BF16)
    return pl.pallas_call(
        body, grid=(nc,),
        in_specs=[rev, vec, rev, rev, vec, pl.BlockSpec((1, S5_BLOCKS, 8, 256), lambda i: (nc - 1 - i, 0, 0, 0)),
                  mat, mat, mat, lamspec, lamspec],
        out_specs=[rev, vec, mat, mat, lamspec, lamspec, vec],
        out_shape=[_sds((n_rows, D_MODEL), F32), _sds((1, D_MODEL), F32), _sds((S5_BLOCKS, 256, 256), F32),
                   _sds((S5_BLOCKS, 256, 256), F32), _sds((S5_BLOCKS, 8, 128), F32), _sds((S5_BLOCKS, 8, 128), F32),
                   _sds((1, D_MODEL), F32)],
        scratch_shapes=[pltpu.VMEM((2, 8 * tc, 128), F32), pltpu.VMEM((tc, D_MODEL), F32), bigb, bigb,
                        pltpu.VMEM((S5_BLOCKS, 8 * tc + 8, 256), F32), big,
                        pltpu.VMEM((S5_BLOCKS, 8, 256), F32)],
        name="s5_bwd", compiler_params=_params(("arbitrary",)))(
            x, gain, dy2, res, d_skip, cs, rb, rbt, rct, lam_r, lam_i)


def _s5_views(a_re, a_im, log_dt, b_re, b_im):
    return a_re[:, None, :], a_im[:, None, :], log_dt[:, None, None], jnp.swapaxes(b_re, 1, 2), jnp.swapaxes(b_im, 1, 2)


def _s5_factors(a_re, a_im, log_dt):
    lr, li, dt = jnp.minimum(a_re, LAMBDA_RE_MAX), a_im, jnp.exp(log_dt)
    mag, ang = jnp.exp(lr * dt), li * dt
    lbr, lbi = mag * jnp.cos(ang), mag * jnp.sin(ang)
    den = lr * lr + li * li
    fr, fi = ((lbr - 1.0) * lr + lbi * li) / den, (lbi * lr - (lbr - 1.0) * li) / den
    return lr, li, dt, lbr, lbi, fr, fi, den


def s5_prep(a_re, a_im, log_dt, b_re, b_im, c_re, c_im):
    def body(ar_ref, ai_ref, t_ref, br_ref, bi_ref, cr_ref, ci_ref, rb_ref, rbt_ref, rc_ref, rct_ref, lr_ref, li_ref):
        _, _, _, lbr, lbi, fr, fi, _ = _s5_factors(ar_ref[...], ai_ref[...], t_ref[...])
        lr_ref[...] = lbr
        li_ref[...] = lbi
        bre = fr * br_ref[...] - fi * bi_ref[...]
        bim = fr * bi_ref[...] + fi * br_ref[...]
        even = (lax.broadcasted_iota(jnp.int32, (256, S5_STATE), 0) // S5_GROUP) % 2 == 0

        def assemble(re, im):
            re, im = re.reshape(256, S5_STATE), im.reshape(256, S5_STATE)
            return jnp.concatenate([jnp.where(even, re, 0.0), jnp.where(even, 0.0, re), jnp.where(even, im, 0.0),
                                    jnp.where(even, 0.0, im)], axis=1)

        for blk in range(S5_BLOCKS):
            sl = slice(16 * blk, 16 * blk + 16)
            rb = assemble(bre[sl], bim[sl])
            rct = assemble(cr_ref[sl], -ci_ref[sl])
            rb_ref[blk] = rb.astype(BF16)
            rbt_ref[blk] = rb.T.astype(BF16)
            rct_ref[blk] = rct.astype(BF16)
            rc_ref[blk] = rct.T.astype(BF16)

    vm = pl.BlockSpec(memory_space=pltpu.VMEM)
    mat = _sds((S5_BLOCKS, 256, 256), BF16)
    lam = _sds((S5_GROUPS, 1, S5_STATE), F32)
    rb, rbt, rc, rct, lam_r, lam_i = pl.pallas_call(
        body, in_specs=[vm] * 7, out_specs=[vm] * 6, out_shape=[mat, mat, mat, mat, lam, lam], name="s5_prep",
        compiler_params=_params())(*_s5_views(a_re, a_im, log_dt, b_re, b_im), c_re, c_im)
    return rb, rbt, rc, rct, lam_r.reshape(S5_BLOCKS, 8, 128), lam_i.reshape(S5_BLOCKS, 8, 128)


def s5_param_bwd(mats, lams, a_re, a_im, log_dt, b_re, b_im):
    def body(m_ref, glr_ref, gli_ref, ar_ref, ai_ref, t_ref, br_ref, bi_ref,
             dar_ref, dai_ref, dt_ref, dbr_ref, dbi_ref, dcr_ref, dci_ref):
        lr, li, dt, lbr, lbi, fr, fi, den = _s5_factors(ar_ref[...], ai_ref[...], t_ref[...])
        shape = (S5_GROUPS, S5_GROUP, S5_STATE)
        gbr, gbi = m_ref[0:1024, 0:64].reshape(shape), m_ref[0:1024, 64:128].reshape(shape)
        dcr_ref[...] = m_ref[1024:2048, 0:64].reshape(shape)
        dci_ref[...] = -m_ref[1024:2048, 64:128].reshape(shape)
        br, bi = br_ref[...], bi_ref[...]
        dbr_ref[...] = fr * gbr + fi * gbi
        dbi_ref[...] = fr * gbi - fi * gbr
        dfr = jnp.sum(gbr * br + gbi * bi, axis=1, keepdims=True)
        dfi = jnp.sum(gbi * br - gbr * bi, axis=1, keepdims=True)
        nr, ni = (dfr * lr - dfi * li) / den, (dfr * li + dfi * lr) / den
        qr, qi = (fr * lr + fi * li) / den, (fi * lr - fr * li) / den
        lam_r, lam_i = -(dfr * qr + dfi * qi), -(dfi * qr - dfr * qi)
        gr, gi = glr_ref[...] + nr, gli_ref[...] + ni
        zr, zi = gr * lbr + gi * lbi, gi * lbr - gr * lbi
        a = ar_ref[...]
        dar_ref[...] = (lam_r + zr * dt) * jnp.where(a < LAMBDA_RE_MAX, 1.0, jnp.where(a == LAMBDA_RE_MAX, 0.5, 0.0))
        dai_ref[...] = lam_i + zi * dt
        dt_ref[...] = jnp.sum(zr * lr + zi * li, axis=2, keepdims=True) * dt

    vm = pl.BlockSpec(memory_space=pltpu.VMEM)
    state = _sds((S5_GROUPS, 1, S5_STATE), F32)
    wide = _sds((S5_GROUPS, S5_GROUP, S5_STATE), F32)
    glr = lams[0:32].reshape(S5_GROUPS, 1, S5_STATE)
    gli = lams[32:64].reshape(S5_GROUPS, 1, S5_STATE)
    dar, dai, ddt, dbr, dbi, dcr, dci = pl.pallas_call(
        body, in_specs=[vm] * 8, out_specs=[vm] * 7,
        out_shape=[state, state, _sds((S5_GROUPS, 1, 1), F32), wide, wide, wide, wide], name="s5_param_bwd",
        compiler_params=_params())(mats, glr, gli, *_s5_views(a_re, a_im, log_dt, b_re, b_im))
    return (dar.reshape(S5_GROUPS, S5_STATE), dai.reshape(S5_GROUPS, S5_STATE), ddt.reshape(S5_GROUPS),
            jnp.swapaxes(dbr, 1, 2), jnp.swapaxes(dbi, 1, 2), dcr, dci)


def s5_compact(drb, drct, dlr, dli):
    def body(drb_ref, drct_ref, dlr_ref, dli_ref, o_ref, lam_ref):
        even = (lax.broadcasted_iota(jnp.int32, (256, 64), 0) // S5_GROUP) % 2 == 0
        for blk in range(S5_BLOCKS):
            for k, ref in enumerate((drb_ref, drct_ref)):
                m = ref[blk]
                re = jnp.where(even, m[:, 0:64], m[:, 64:128])
                im = jnp.where(even, m[:, 128:192], m[:, 192:256])
                o_ref[pl.ds(k * 1024 + blk * 256, 256), :] = jnp.concatenate([re, im], axis=1)
            lam_ref[pl.ds(blk * 8, 8), :] = dlr_ref[blk]
            lam_ref[pl.ds(32 + blk * 8, 8), :] = dli_ref[blk]

    vm = pl.BlockSpec(memory_space=pltpu.VMEM)
    return pl.pallas_call(body, in_specs=[vm] * 4, out_specs=[vm, vm], out_shape=[_sds((2048, 128), F32), _sds((64, 128), F32)],
                          name="s5_compact", compiler_params=_params())(drb, drct, dlr, dli)


NEG = -1e30


GROUP = N_Q // N_KV


def _attn_masks(n):
    qi = lax.broadcasted_iota(jnp.int32, (GROUP * BLOCK, BLOCK), 0) % BLOCK
    kj = lax.broadcasted_iota(jnp.int32, (GROUP * BLOCK, BLOCK), 1)
    return jnp.logical_and(kj > qi, n > 0), kj <= qi


def _stack_heads(ref, kh):
    return jnp.concatenate([ref[:, (GROUP * kh + g) * HEAD_DIM:(GROUP * kh + g + 1) * HEAD_DIM] for g in range(GROUP)], axis=0)


def _unstack_heads(val):
    return jnp.concatenate([val[g * BLOCK:(g + 1) * BLOCK] for g in range(GROUP)], axis=1)


def _sink_column(sink_ref, kh):
    grp = lax.broadcasted_iota(jnp.int32, (GROUP * BLOCK, 1), 0) // BLOCK
    col = jnp.zeros((GROUP * BLOCK, 1), F32)
    for g in range(GROUP):
        col = jnp.where(grp == g, sink_ref[GROUP * kh + g], col)
    return col, grp


def _attn_exp(q4, kp, kc, sink, mask_p, mask_c):
    scale = 1.0 / math.sqrt(HEAD_DIM)
    nt = (((1,), (1,)), ((), ()))
    sp = jnp.where(mask_p, lax.dot_general(q4, kp, nt, preferred_element_type=F32) * scale, NEG)
    sc = jnp.where(mask_c, lax.dot_general(q4, kc, nt, preferred_element_type=F32) * scale, NEG)
    m = jnp.maximum(jnp.maximum(jnp.max(sp, axis=-1, keepdims=True), jnp.max(sc, axis=-1, keepdims=True)), sink)
    pp = jnp.exp(sp - m)
    pc = jnp.exp(sc - m)
    ps = jnp.exp(sink - m)
    inv = 1.0 / (jnp.sum(pp, axis=-1, keepdims=True) + jnp.sum(pc, axis=-1, keepdims=True) + ps)
    return pp, pc, ps, inv


def attn_fwd(q, kv, sinks):
    n_rows = q.shape[0]
    nb = n_rows // BLOCK

    def body(sink_ref, q_ref, kvp_ref, kvc_ref, o_ref):
        n = pl.program_id(0)
        mask_p, mask_c = _attn_masks(n)
        outs = []
        for kh in range(N_KV):
            ks, vs = slice(kh * HEAD_DIM, (kh + 1) * HEAD_DIM), slice((N_KV + kh) * HEAD_DIM, (N_KV + kh + 1) * HEAD_DIM)
            sink, _ = _sink_column(sink_ref, kh)
            q4, halves = _stack_heads(q_ref, kh), []
            for rows in (slice(0, 2 * BLOCK), slice(2 * BLOCK, 4 * BLOCK)):
                pp, pc, _, inv = _attn_exp(q4[rows], kvp_ref[:, ks], kvc_ref[:, ks], sink[rows], mask_p[rows], mask_c[rows])
                halves.append((jnp.dot(pp.astype(BF16), kvp_ref[:, vs], preferred_element_type=F32)
                               + jnp.dot(pc.astype(BF16), kvc_ref[:, vs], preferred_element_type=F32)) * inv)
            outs.append(_unstack_heads(jnp.concatenate(halves, axis=0)))
        o_ref[...] = jnp.concatenate(outs, axis=1).astype(BF16)

    kvw = 2 * N_KV * HEAD_DIM
    return pl.pallas_call(
        body, grid=(nb,),
        in_specs=[pl.BlockSpec(memory_space=pltpu.SMEM), pl.BlockSpec((BLOCK, D_MODEL), lambda n: (n, 0)),
                  pl.BlockSpec((BLOCK, kvw), lambda n: (jnp.maximum(n - 1, 0), 0)), pl.BlockSpec((BLOCK, kvw), lambda n: (n, 0))],
        out_specs=pl.BlockSpec((BLOCK, D_MODEL), lambda n: (n, 0)), out_shape=_sds((n_rows, D_MODEL), BF16),
        name="attn_fwd", compiler_params=_params(("parallel",)))(sinks, q, kv, kv)


def attn_bwd(q, kv, do, sinks):
    n_rows = q.shape[0]
    nb = n_rows // BLOCK
    kvw = 2 * N_KV * HEAD_DIM
    tn = (((0,), (0,)), ((), ()))
    nt = (((1,), (1,)), ((), ()))
    scale = 1.0 / math.sqrt(HEAD_DIM)

    def body(sink_ref, q_ref, kvp_ref, kvc_ref, do_ref, dq_ref, dbq_ref, dprev_ref, dcur_ref, dsink_ref):
        n = pl.program_id(0)
        mask_p, mask_c = _attn_masks(n)
        lane = lax.broadcasted_iota(jnp.int32, (1, D_MODEL), 1)
        dqs, dsink = [], jnp.zeros((1, D_MODEL), F32)
        dkp, dkc, dvp, dvc = [], [], [], []
        for kh in range(N_KV):
            ks, vs = slice(kh * HEAD_DIM, (kh + 1) * HEAD_DIM), slice((N_KV + kh) * HEAD_DIM, (N_KV + kh + 1) * HEAD_DIM)
            q4, do4 = _stack_heads(q_ref, kh), _stack_heads(do_ref, kh)
            kp, kc, vp, vc = kvp_ref[:, ks], kvc_ref[:, ks], kvp_ref[:, vs], kvc_ref[:, vs]
            sink, grp = _sink_column(sink_ref, kh)
            pp, pc, ps, inv = _attn_exp(q4, kp, kc, sink, mask_p, mask_c)
            pp, pc = pp * inv, pc * inv
            dpp = lax.dot_general(do4, vp, nt, preferred_element_type=F32)
            dpc = lax.dot_general(do4, vc, nt, preferred_element_type=F32)
            delta = jnp.sum(pp * dpp, axis=-1, keepdims=True) + jnp.sum(pc * dpc, axis=-1, keepdims=True)
            dsp = (pp * (dpp - delta) * scale).astype(BF16)
            dsc = (pc * (dpc - delta) * scale).astype(BF16)
            dsk = ps * inv * delta
            for g in range(GROUP):
                dsink = dsink + jnp.where(lane == GROUP * kh + g, -jnp.sum(jnp.where(grp == g, dsk, 0.0)), 0.0)
            dqs.append(_unstack_heads(jnp.dot(dsp, kp, preferred_element_type=F32)
                                      + jnp.dot(dsc, kc, preferred_element_type=F32)))
            dkp.append(lax.dot_general(dsp, q4, tn, preferred_element_type=F32))
            dkc.append(lax.dot_general(dsc, q4, tn, preferred_element_type=F32))
            dvp.append(lax.dot_general(pp.astype(BF16), do4, tn, preferred_element_type=F32))
            dvc.append(lax.dot_general(pc.astype(BF16), do4, tn, preferred_element_type=F32))
        dq = jnp.concatenate(dqs, axis=1)
        dq_ref[...] = dq.astype(BF16)
        dprev_ref[0] = jnp.concatenate(dkp + dvp, axis=1)
        dcur_ref[0] = jnp.concatenate(dkc + dvc, axis=1)

        @pl.when(n == 0)
        def _():
            dbq_ref[...] = jnp.zeros_like(dbq_ref)
            dsink_ref[...] = jnp.zeros_like(dsink_ref)

        dbq_ref[...] += jnp.sum(dq, axis=0, keepdims=True)
        dsink_ref[...] += dsink

    blk = pl.BlockSpec((BLOCK, D_MODEL), lambda n: (n, 0))
    part = pl.BlockSpec((1, BLOCK, kvw), lambda n: (n, 0, 0))
    return pl.pallas_call(
        body, grid=(nb,),
        in_specs=[pl.BlockSpec(memory_space=pltpu.SMEM), blk,
                  pl.BlockSpec((BLOCK, kvw), lambda n: (jnp.maximum(n - 1, 0), 0)), pl.BlockSpec((BLOCK, kvw), lambda n: (n, 0)), blk],
        out_specs=[blk, pl.BlockSpec((1, D_MODEL), lambda n: (0, 0)), part, part, pl.BlockSpec((1, D_MODEL), lambda n: (0, 0))],
        out_shape=[_sds((n_rows, D_MODEL), BF16), _sds((1, D_MODEL), F32), _sds((nb, BLOCK, kvw), F32),
                   _sds((nb, BLOCK, kvw), F32), _sds((1, D_MODEL), F32)],
        name="attn_bwd", compiler_params=_params(("arbitrary",)))(sinks, q, kv, kv, do)


def kv_combine(dprev, dcur):
    nb, _, kvw = dprev.shape

    def body(dcur_ref, dprev_ref, dkv_ref, db_ref):
        total = jnp.zeros((1, kvw), F32)
        for m in range(nb):
            dkv = dcur_ref[m] + dprev_ref[m + 1] if m + 1 < nb else dcur_ref[m]
            dkv_ref[m * BLOCK:(m + 1) * BLOCK, :] = dkv.astype(BF16)
            total = total + jnp.sum(dkv, axis=0, keepdims=True)
        db_ref[...] = jnp.concatenate([total, jnp.zeros((1, D_MODEL - kvw), F32)], axis=1)

    vm = pl.BlockSpec(memory_space=pltpu.VMEM)
    return pl.pallas_call(body, in_specs=[vm, vm], out_specs=[vm, vm],
                          out_shape=[_sds((nb * BLOCK, kvw), BF16), _sds((1, D_MODEL), F32)], name="kv_combine",
                          compiler_params=_params())(dcur, dprev)


def glu_bwd(dout, val, gate, tm=256):
    n_rows, d = dout.shape

    def body(do_ref, v_ref, g_ref, dz_ref, db_ref):
        i = pl.program_id(0)
        sg = jax.nn.sigmoid(g_ref[...])
        dval = do_ref[...] * sg
        dgate = do_ref[...] * v_ref[...] * sg * (1.0 - sg)
        dz_ref[...] = jnp.concatenate([dval, dgate], axis=1).astype(BF16)

        @pl.when(i == 0)
        def _():
            db_ref[...] = jnp.zeros_like(db_ref)

        db_ref[0:1, :] += jnp.sum(dval, axis=0, keepdims=True)
        db_ref[1:2, :] += jnp.sum(dgate, axis=0, keepdims=True)

    row = pl.BlockSpec((tm, d), lambda i: (i, 0))
    return pl.pallas_call(
        body, grid=(n_rows // tm,), in_specs=[row, row, row],
        out_specs=[pl.BlockSpec((tm, 2 * d), lambda i: (i, 0)), pl.BlockSpec((2, d), lambda i: (0, 0))],
        out_shape=[_sds((n_rows, 2 * d), BF16), _sds((2, d), F32)],
        name="glu_bwd", compiler_params=_params(("arbitrary",)))(dout, val, gate)


def _adam_update(w, g, m, v):
    nm = ADAM_B1 * m + (1.0 - ADAM_B1) * g
    nv = ADAM_B2 * v + (1.0 - ADAM_B2) * (g * g)
    m_hat = nm / (1.0 - ADAM_B1 ** ADAM_STEP)
    v_hat = nv / (1.0 - ADAM_B2 ** ADAM_STEP)
    return -ADAM_LR * (m_hat / (jnp.sqrt(v_hat) + ADAM_EPS) + ADAM_WD * w), nm, nv


def adamw(name, ws, gs, ms, vs, steps=8):
    n = len(ws)

    def body(*refs):
        for k in range(n):
            w_ref, g_ref, m_ref, v_ref = (refs[j * n + k] for j in range(4))
            go_ref, d_ref, nm_ref, nv_ref = (refs[(4 + j) * n + k] for j in range(4))
            gv = g_ref[...]
            go_ref[...] = gv
            d_ref[...], nm_ref[...], nv_ref[...] = _adam_update(w_ref[...], gv, m_ref[...], v_ref[...])

    specs = [pl.BlockSpec((w.shape[0] // steps, w.shape[1]), lambda i: (i, 0)) for w in ws]
    shapes = [_sds(w.shape, F32) for w in ws]
    out = pl.pallas_call(
        body, grid=(steps,), in_specs=specs * 4, out_specs=specs * 4, out_shape=shapes * 4, name=name,
        compiler_params=_params(("parallel",)))(*ws, *gs, *ms, *vs)
    return [list(out[j * n:(j + 1) * n]) for j in range(4)]


def adamw_native(name, ws, gs, ms, vs):
    n = len(ws)

    def body(*refs):
        w_refs, g_refs, m_refs, v_refs = refs[:n], refs[n:2 * n], refs[2 * n:3 * n], refs[3 * n:4 * n]
        d_refs, nm_refs, nv_refs = refs[4 * n:5 * n], refs[5 * n:6 * n], refs[6 * n:7 * n]
        for k in range(n):
            dl, nm, nv = _adam_update(w_refs[k][...], g_refs[k][...], m_refs[k][...], v_refs[k][...])
            d_refs[k][...] = dl
            nm_refs[k][...] = nm
            nv_refs[k][...] = nv

    vm = pl.BlockSpec(memory_space=pltpu.VMEM)
    shapes = [_sds(w.shape, F32) for w in ws]
    out = pl.pallas_call(body, in_specs=[vm] * (4 * n), out_specs=[vm] * (3 * n), out_shape=shapes * 3, name=name,
                         compiler_params=_params())(*ws, *gs, *ms, *vs)
    return list(out[:n]), list(out[n:2 * n]), list(out[2 * n:])


VEC_ROWS = {"norm_mix": 0, "norm_mlp": 2, "norm_kv": 4, "norm_final": 5, "s5_d": 6, "b_q": 7, "b_o": 8, "s5_b_glu": 9,
            "b_kv": 11, "sinks": 12, "loss": 13}


def split_vectors(where, vecs, d_shard, glu_shard):
    kvw = 2 * N_KV * HEAD_DIM
    shapes = {"norm_mix": (2, D_MODEL), "norm_mlp": (2, D_MODEL), "norm_kv": (1, D_MODEL), "norm_final": (1, D_MODEL),
              "s5_d": (1, d_shard), "b_q": (1, D_MODEL), "b_o": (1, D_MODEL), "s5_b_glu": (1, glu_shard), "b_kv": (1, kvw),
              "sinks": (1, N_Q), "loss": (1, 128)}
    names = list(shapes)

    def body(where_ref, v_ref, *o_refs):
        chip = where_ref[1]
        for name, o_ref in zip(names, o_refs):
            r0, (r, n) = VEC_ROWS[name], shapes[name]
            if name == "s5_d":
                g = jnp.zeros((1, n), F32)
                for j in range(4):
                    g = jnp.where(chip == j, v_ref[r0:r0 + 1, j * n:(j + 1) * n], g)
            elif name == "s5_b_glu":
                g = jnp.zeros((1, n), F32)
                for j in range(4):
                    row, col = r0 + (j * n) // D_MODEL, (j * n) % D_MODEL
                    g = jnp.where(chip == j, v_ref[row:row + 1, col:col + n], g)
            else:
                g = v_ref[r0:r0 + r, 0:n]
            o_ref[...] = g

    vm = pl.BlockSpec(memory_space=pltpu.VMEM)
    out = pl.pallas_call(body, in_specs=[pl.BlockSpec(memory_space=pltpu.SMEM), vm], out_specs=[vm] * len(names),
                         out_shape=[_sds(shapes[n], F32) for n in names], name="split_vectors",
                         compiler_params=_params())(where, vecs)
    return dict(zip(names, out))


def _position():
    x, y, c = lax.axis_index("x"), lax.axis_index("y"), lax.axis_index("c")
    others = [(1 - x, y), (x, 1 - y), (1 - x, 1 - y)]
    return x, y, c, others


def _window(ref, kind, chip, half, shard_shape):
    if kind == "slab":
        return ref.at[chip]
    r, n = shard_shape
    if kind == "col":
        return ref.at[pl.ds(pl.multiple_of(half * (r // 2), 16), r // 2), pl.ds(pl.multiple_of(chip * n, 128), n)]
    return ref.at[pl.ds(pl.multiple_of(chip * r, 16), r), pl.ds(pl.multiple_of(half * (n // 2), 128), n // 2)]


def _half(ref, kind, half, shape):
    r, n = shape
    if kind == "col":
        return ref.at[pl.ds(pl.multiple_of(half * (r // 2), 16), r // 2), :]
    return ref.at[:, pl.ds(pl.multiple_of(half * (n // 2), 128), n // 2)]


def swap_start(name, grads, kinds, carry):
    nt = len(grads)
    shapes = [tuple(g.shape) for g in grads]
    lands = [lax.empty(sh, BF16) for sh in shapes]
    given, given_specs, token_type, write = _hand_through(carry)
    n_in = 2 * nt + len(given)

    def body(*refs):
        in_refs, land_refs = refs[:nt], refs[nt:2 * nt]
        send_sems, recv_sems, token = refs[n_in], refs[n_in + 1], refs[-1]
        x, y, c, _ = _position()
        for t in range(nt):
            pltpu.make_async_remote_copy(
                src_ref=_half(in_refs[t], kinds[t], 1 - c, shapes[t]), dst_ref=_half(land_refs[t], kinds[t], 1 - c, shapes[t]),
                send_sem=send_sems.at[t], recv_sem=recv_sems.at[t], device_id=(x, y, 1 - c), device_id_type=MESH).start()
        write(token, refs[:n_in])

    sems = pltpu.SemaphoreType.DMA((nt,))
    both = list(grads) + lands
    out = pl.pallas_call(
        body, name=name, in_specs=[HBM_SPEC] * (2 * nt) + given_specs,
        out_specs=(SEM_SPEC, SEM_SPEC, *[HBM_SPEC] * (2 * nt), pl.BlockSpec(memory_space=pltpu.VMEM)),
        out_shape=(sems, sems, *[pltpu.HBM(a.shape, a.dtype) for a in both], token_type),
        input_output_aliases={t: 2 + t for t in range(2 * nt)}, compiler_params=_split_params(),
    )(*[_in_hbm(a) for a in both], *given)
    return out[0], out[1], list(out[2:2 + nt]), list(out[2 + nt:2 + 2 * nt]), out[-1]


def swap_wait(name, send_sems, recv_sems, grads, lands, kinds, after):
    nt = len(grads)
    shapes = [tuple(g.shape) for g in grads]

    def body(*refs):
        in_refs, land_refs = refs[:nt], refs[nt:2 * nt]
        send_ref, recv_ref = refs[2 * nt], refs[2 * nt + 1]
        x, y, c, _ = _position()
        for t in range(nt):
            cp = pltpu.make_async_remote_copy(
                src_ref=_half(in_refs[t], kinds[t], 1 - c, shapes[t]), dst_ref=_half(land_refs[t], kinds[t], c, shapes[t]),
                send_sem=send_ref.at[t], recv_sem=recv_ref.at[t], device_id=(x, y, 1 - c), device_id_type=MESH)
            cp.wait_send()
            cp.wait_recv()

    both = list(grads) + list(lands)
    out = pl.pallas_call(
        body, name=name, in_specs=[HBM_SPEC] * (2 * nt) + [SEM_SPEC, SEM_SPEC, HBM_SPEC], out_specs=[HBM_SPEC] * (2 * nt),
        out_shape=[pltpu.HBM(a.shape, a.dtype) for a in both], input_output_aliases={t: t for t in range(2 * nt)},
        compiler_params=_split_params())(*both, send_sems, recv_sems, _in_hbm(after))
    return list(out[:nt]), list(out[nt:])


def _half_spec(kind, shape, tiles):
    r, n = shape
    if kind == "col":
        tn = n // tiles
        return pl.BlockSpec((r // 2, tn), lambda i, s: (s[0], i))
    tm = r // tiles
    return pl.BlockSpec((tm, n // 2), lambda i, s: (i, s[0]))


def add_halves(name, mine, landed, kinds, where, tiles=2):
    nt = len(mine)
    shapes = [tuple(a.shape) for a in mine]

    def compact(t):
        r, n = shapes[t]
        if kinds[t] == "col":
            return (r // 2, n), pl.BlockSpec((r // 2, n // tiles), lambda i, s: (0, i))
        return (r, n // 2), pl.BlockSpec((r // tiles, n // 2), lambda i, s: (i, 0))

    def body(s_ref, *refs):
        for a_ref, b_ref, o_ref in zip(refs[:nt], refs[nt:2 * nt], refs[2 * nt:]):
            o_ref[...] = (a_ref[...].astype(F32) + b_ref[...].astype(F32)).astype(BF16)

    specs = [_half_spec(kinds[t], shapes[t], tiles) for t in range(nt)]
    return pl.pallas_call(
        body, grid_spec=pltpu.PrefetchScalarGridSpec(num_scalar_prefetch=1, grid=(tiles,), in_specs=specs + specs,
                                                     out_specs=[compact(t)[1] for t in range(nt)]),
        out_shape=[_sds(compact(t)[0], BF16) for t in range(nt)], name=name,
        compiler_params=_params(("parallel",)))(where, *mine, *landed)


def sum_shards(name, parts, landed, kinds, shard_shapes, where, layers, n_layers, intos, tiles=2):
    nt = len(parts)
    in_specs, out_specs = [], []
    for t in range(nt):
        (r, n), layer = shard_shapes[t], layers[t]
        if kinds[t] == "col":
            tm, width = r // 2 // tiles, n
            own = pl.BlockSpec((tm, n), lambda i, s: (i, s[1]))
            out = pl.BlockSpec((None, tm, n), lambda i, s, layer=layer: (layer, s[0] * tiles + i, 0))
        else:
            tm, width = r // tiles, n // 2
            own = pl.BlockSpec((tm, n // 2), lambda i, s: (s[1] * tiles + i, 0))
            out = pl.BlockSpec((None, tm, n // 2), lambda i, s, layer=layer: (layer, i, s[0]))
        in_specs += [own, pl.BlockSpec((3, tm, width), lambda i, s: (0, i, 0))]
        out_specs.append(out)
    args, aliases = [where] + [a for pair in zip(parts, landed) for a in pair], {}
    for t in range(nt):
        if intos[t] is not None:
            aliases[len(args)] = t
            in_specs.append(pl.BlockSpec(memory_space=pl.ANY))
            args.append(intos[t])

    def body(s_ref, *refs):
        for t in range(nt):
            a_ref, l_ref, o_ref = refs[2 * t], refs[2 * t + 1], refs[len(in_specs) + t]
            o_ref[...] = ((a_ref[...].astype(F32) + l_ref[0].astype(F32)) + l_ref[1].astype(F32)) + l_ref[2].astype(F32)

    return pl.pallas_call(
        body, grid_spec=pltpu.PrefetchScalarGridSpec(num_scalar_prefetch=1, grid=(tiles,), in_specs=in_specs,
                                                     out_specs=out_specs),
        out_shape=[_sds((n_layers[t],) + tuple(shard_shapes[t]), F32) for t in range(nt)], input_output_aliases=aliases,
        name=name, compiler_params=_params(("parallel",)))(*args)


def share_start(arrays, entries, carry):
    na, nt = len(arrays), len(entries)
    given, given_specs, token_type, write = _hand_through(carry)
    n_in = na + len(given)

    def body(*refs):
        in_refs, send_sems, recv_sems, token = refs[:na], refs[n_in], refs[n_in + 1], refs[-1]
        x, y, c, _ = _position()
        for t, (a, layer, kind) in enumerate(entries):
            mine = _half(in_refs[a].at[layer], kind, c, tuple(arrays[a].shape[1:]))
            pltpu.make_async_remote_copy(
                src_ref=mine, dst_ref=mine, send_sem=send_sems.at[t], recv_sem=recv_sems.at[t],
                device_id=(x, y, 1 - c), device_id_type=MESH).start()
        write(token, refs[:n_in])

    sems = pltpu.SemaphoreType.DMA((nt,))
    out = pl.pallas_call(
        body, name="share_start", in_specs=[HBM_SPEC] * na + given_specs,
        out_specs=(SEM_SPEC, SEM_SPEC, *[HBM_SPEC] * na, pl.BlockSpec(memory_space=pltpu.VMEM)),
        out_shape=(sems, sems, *[pltpu.HBM(a.shape, a.dtype) for a in arrays], token_type),
        input_output_aliases={t: 2 + t for t in range(na)}, compiler_params=_split_params(),
    )(*[_in_hbm(a) for a in arrays], *given)
    return out[0], out[1], list(out[2:2 + na]), out[-1]


def share_wait(send_sems, recv_sems, arrays, entries, after):
    na = len(arrays)

    def body(*refs):
        in_refs, send_ref, recv_ref = refs[:na], refs[na], refs[na + 1]
        x, y, c, _ = _position()
        for t, (a, layer, kind) in enumerate(entries):
            shape = tuple(arrays[a].shape[1:])
            cp = pltpu.make_async_remote_copy(
                src_ref=_half(in_refs[a].at[layer], kind, c, shape), dst_ref=_half(in_refs[a].at[layer], kind, 1 - c, shape),
                send_sem=send_ref.at[t], recv_sem=recv_ref.at[t], device_id=(x, y, 1 - c), device_id_type=MESH)
            cp.wait_send()
            cp.wait_recv()

    return list(pl.pallas_call(
        body, name="share_wait", in_specs=[HBM_SPEC] * na + [SEM_SPEC, SEM_SPEC, HBM_SPEC], out_specs=[HBM_SPEC] * na,
        out_shape=[pltpu.HBM(a.shape, a.dtype) for a in arrays], input_output_aliases={t: t for t in range(na)},
        compiler_params=_split_params())(*arrays, send_sems, recv_sems, _in_hbm(after)))


HBM_SPEC = pl.BlockSpec(memory_space=pltpu.HBM)
SEM_SPEC = pl.BlockSpec(memory_space=pltpu.SEMAPHORE)
ANY_SPEC = pl.BlockSpec(memory_space=pl.ANY)


def _split_params():
    return pltpu.CompilerParams(has_side_effects=pltpu.SideEffectType.DATAFLOW_SIDE_EFFECTING,
                                vmem_limit_bytes=VMEM_LIMIT_BYTES)


def _in_hbm(a):
    return pltpu.with_memory_space_constraint(a, pltpu.HBM)


def cast_place(arrays, entries, where, tiles=2):
    in_specs, out_specs, fulls = [], [], []
    for a, layer, kind in entries:
        _, r, n = arrays[a].shape
        tm = r // tiles
        in_specs.append(pl.BlockSpec((None, tm, n), lambda i, s, layer=layer: (layer, i, 0)))
        if kind == "col":
            fulls.append((r, 4 * n))
            out_specs.append(pl.BlockSpec((tm, n), lambda i, s: (i, s[1])))
        else:
            fulls.append((4 * r, n))
            out_specs.append(pl.BlockSpec((tm, n), lambda i, s: (s[1] * tiles + i, 0)))
    nt = len(entries)

    def body(s_ref, *refs):
        for w_ref, o_ref in zip(refs[:nt], refs[nt:]):
            o_ref[...] = w_ref[...].astype(BF16)

    return pl.pallas_call(
        body, grid_spec=pltpu.PrefetchScalarGridSpec(num_scalar_prefetch=1, grid=(tiles,), in_specs=in_specs,
                                                     out_specs=out_specs),
        out_shape=[_sds(f, BF16) for f in fulls], name="cast_place",
        compiler_params=_params(("parallel",)))(where, *[arrays[a] for a, _, _ in entries])


def _hand_through(carry):
    given = [] if isinstance(carry, tuple) else [carry]

    def write(token, ins):
        token[...] = ins[-1][...] if given else jnp.zeros_like(token)

    return (given, [pl.BlockSpec(memory_space=pltpu.VMEM)] * len(given),
            _sds(carry if isinstance(carry, tuple) else carry.shape, F32), write)


def gather_start(fulls, kinds, shard_shapes, carry):
    nt = len(fulls)
    given, given_specs, token_type, write = _hand_through(carry)
    n_in = nt + len(given)

    def body(*refs):
        full_refs = refs[:nt]
        send_sems, recv_sems, token = refs[n_in], refs[n_in + 1], refs[-1]
        x, y, c, others = _position()
        for t in range(nt):
            mine = _window(full_refs[t], kinds[t], 2 * x + y, c, shard_shapes[t])
            for j, (ox, oy) in enumerate(others):
                pltpu.make_async_remote_copy(
                    src_ref=mine, dst_ref=mine, send_sem=send_sems.at[3 * t + j], recv_sem=recv_sems.at[3 * t + j],
                    device_id=(ox, oy, c), device_id_type=MESH).start()
        write(token, refs[:n_in])

    sems = pltpu.SemaphoreType.DMA((3 * nt,))
    out = pl.pallas_call(
        body, name="gather_start", in_specs=[HBM_SPEC] * nt + given_specs,
        out_specs=(SEM_SPEC, SEM_SPEC, *[HBM_SPEC] * nt, pl.BlockSpec(memory_space=pltpu.VMEM)),
        out_shape=(sems, sems, *[pltpu.HBM(f.shape, f.dtype) for f in fulls], token_type),
        input_output_aliases={t: 2 + t for t in range(nt)}, compiler_params=_split_params(),
    )(*[_in_hbm(f) for f in fulls], *given)
    return out[0], out[1], list(out[2:2 + nt]), out[-1]


def gather_wait(name, send_sems, recv_sems, fulls, kinds, shard_shapes, after, first):
    nt = len(fulls)
    extra = [] if after is None else [_in_hbm(after)]

    def body(*refs):
        full_refs, send_ref, recv_ref = refs[:nt], refs[nt], refs[nt + 1]
        x, y, c, others = _position()
        for t in range(nt):
            mine = _window(full_refs[t], kinds[t], 2 * x + y, c, shard_shapes[t])
            for j, (ox, oy) in enumerate(others):
                cp = pltpu.make_async_remote_copy(
                    src_ref=mine, dst_ref=_window(full_refs[t], kinds[t], 2 * ox + oy, c, shard_shapes[t]),
                    send_sem=send_ref.at[3 * (first + t) + j], recv_sem=recv_ref.at[3 * (first + t) + j],
                    device_id=(ox, oy, c), device_id_type=MESH)
                cp.wait_send()
                cp.wait_recv()

    out = pl.pallas_call(
        body, name=name, in_specs=[HBM_SPEC] * nt + [SEM_SPEC, SEM_SPEC] + [HBM_SPEC] * len(extra),
        out_specs=[HBM_SPEC] * nt, out_shape=[pltpu.HBM(f.shape, f.dtype) for f in fulls],
        input_output_aliases={t: t for t in range(nt)}, compiler_params=_split_params())(*fulls, send_sems, recv_sems, *extra)
    return list(out)


def forward_halves(name, fulls, kinds, shard_shapes):
    nt = len(fulls)

    def body(*refs):
        out_refs = refs[nt:2 * nt]
        send_sems, recv_sems = refs[2 * nt:]
        x, y, c, others = _position()
        cps = []
        for t in range(nt):
            for j, (ox, oy) in enumerate(others):
                landed = _window(out_refs[t], kinds[t], 2 * ox + oy, c, shard_shapes[t])
                cp = pltpu.make_async_remote_copy(
                    src_ref=landed, dst_ref=landed, send_sem=send_sems.at[3 * t + j], recv_sem=recv_sems.at[3 * t + j],
                    device_id=(x, y, 1 - c), device_id_type=MESH)
                cp.start()
                cps.append(cp)
        for t in range(nt):
            for j, (ox, oy) in enumerate(others):
                got = _window(out_refs[t], kinds[t], 2 * ox + oy, 1 - c, shard_shapes[t])
                pltpu.make_async_remote_copy(
                    src_ref=got, dst_ref=got, send_sem=send_sems.at[3 * t + j], recv_sem=recv_sems.at[3 * t + j],
                    device_id=(x, y, 1 - c), device_id_type=MESH).wait_recv()
        for cp in cps:
            cp.wait_send()

    out = pl.pallas_call(
        body, in_specs=[ANY_SPEC] * nt, out_specs=[ANY_SPEC] * nt, out_shape=[_sds(f.shape, f.dtype) for f in fulls],
        input_output_aliases={t: t for t in range(nt)},
        scratch_shapes=[pltpu.SemaphoreType.DMA((3 * nt,)), pltpu.SemaphoreType.DMA((3 * nt,))],
        name=name, compiler_params=_params())(*fulls)
    return list(out)


def forward_start(name, send_sems, recv_sems, fulls, kinds, shard_shapes, after, first, carry, passing=()):
    nt, n_pass = len(fulls), len(passing)
    given, given_specs, token_type, write = _hand_through(carry)
    n_in = nt + 3 + n_pass + len(given)

    def body(*refs):
        full_refs, ici_send, ici_recv = refs[:nt], refs[nt], refs[nt + 1]
        send_ref, recv_ref, token = refs[n_in], refs[n_in + 1], refs[-1]
        x, y, c, others = _position()
        for t in range(nt):
            mine = _window(full_refs[t], kinds[t], 2 * x + y, c, shard_shapes[t])
            for j, (ox, oy) in enumerate(others):
                landed = _window(full_refs[t], kinds[t], 2 * ox + oy, c, shard_shapes[t])
                cp = pltpu.make_async_remote_copy(
                    src_ref=mine, dst_ref=landed, send_sem=ici_send.at[3 * (first + t) + j],
                    recv_sem=ici_recv.at[3 * (first + t) + j], device_id=(ox, oy, c), device_id_type=MESH)
                cp.wait_send()
                cp.wait_recv()
                pltpu.make_async_remote_copy(
                    src_ref=landed, dst_ref=landed, send_sem=send_ref.at[3 * t + j], recv_sem=recv_ref.at[3 * t + j],
                    device_id=(x, y, 1 - c), device_id_type=MESH).start()
        write(token, refs[:n_in])

    sems = pltpu.SemaphoreType.DMA((3 * nt,))
    out = pl.pallas_call(
        body, name=name, in_specs=[HBM_SPEC] * nt + [SEM_SPEC, SEM_SPEC, HBM_SPEC] + [HBM_SPEC] * n_pass + given_specs,
        out_specs=(SEM_SPEC, SEM_SPEC, *[HBM_SPEC] * (nt + n_pass), pl.BlockSpec(memory_space=pltpu.VMEM)),
        out_shape=(sems, sems, *[pltpu.HBM(f.shape, f.dtype) for f in [*fulls, *passing]], token_type),
        input_output_aliases={**{t: 2 + t for t in range(nt)}, **{nt + 3 + t: 2 + nt + t for t in range(n_pass)}},
        compiler_params=_split_params(),
    )(*fulls, send_sems, recv_sems, _in_hbm(after), *passing, *given)
    return out[0], out[1], list(out[2:2 + nt]), list(out[2 + nt:2 + nt + n_pass]), out[-1]


def forward_wait(name, send_sems, recv_sems, fulls, kinds, shard_shapes, after):
    nt = len(fulls)

    def body(*refs):
        full_refs, send_ref, recv_ref = refs[:nt], refs[nt], refs[nt + 1]
        x, y, c, others = _position()
        for t in range(nt):
            for j, (ox, oy) in enumerate(others):
                cp = pltpu.make_async_remote_copy(
                    src_ref=_window(full_refs[t], kinds[t], 2 * ox + oy, c, shard_shapes[t]),
                    dst_ref=_window(full_refs[t], kinds[t], 2 * ox + oy, 1 - c, shard_shapes[t]),
                    send_sem=send_ref.at[3 * t + j], recv_sem=recv_ref.at[3 * t + j],
                    device_id=(x, y, 1 - c), device_id_type=MESH)
                cp.wait_send()
                cp.wait_recv()

    return list(pl.pallas_call(
        body, name=name, in_specs=[HBM_SPEC] * nt + [SEM_SPEC, SEM_SPEC, HBM_SPEC], out_specs=[HBM_SPEC] * nt,
        out_shape=[pltpu.HBM(f.shape, f.dtype) for f in fulls], input_output_aliases={t: t for t in range(nt)},
        compiler_params=_split_params())(*fulls, send_sems, recv_sems, _in_hbm(after)))


def _piece(ref, kind, chip, shard_shape):
    r, n = shard_shape
    if kind == "col":
        return ref.at[:, pl.ds(pl.multiple_of(chip * n, 128), n)]
    return ref.at[pl.ds(pl.multiple_of(chip * r, 16), r), :]


def _piece_shape(kind, shard_shape):
    r, n = shard_shape
    return (r // 2, n) if kind == "col" else (r, n // 2)


def exchange_start(name, parts, kinds, shard_shapes, carry):
    nt = len(parts)
    lands = [lax.empty((3,) + _piece_shape(kinds[t], shard_shapes[t]), BF16) for t in range(nt)]
    given, given_specs, token_type, write = _hand_through(carry)
    n_in = 2 * nt + len(given)

    def body(*refs):
        part_refs, land_refs = refs[:nt], refs[nt:2 * nt]
        send_sems, recv_sems, token = refs[n_in], refs[n_in + 1], refs[-1]
        x, y, c, others = _position()
        for t in range(nt):
            for j, (ox, oy) in enumerate(others):
                pltpu.make_async_remote_copy(
                    src_ref=_piece(part_refs[t], kinds[t], 2 * ox + oy, shard_shapes[t]), dst_ref=land_refs[t].at[j],
                    send_sem=send_sems.at[3 * t + j], recv_sem=recv_sems.at[3 * t + j],
                    device_id=(ox, oy, c), device_id_type=MESH).start()
        write(token, refs[:n_in])

    sems = pltpu.SemaphoreType.DMA((3 * nt,))
    both = list(parts) + lands
    out = pl.pallas_call(
        body, name=name, in_specs=[HBM_SPEC] * (2 * nt) + given_specs,
        out_specs=(SEM_SPEC, SEM_SPEC, *[HBM_SPEC] * (2 * nt), pl.BlockSpec(memory_space=pltpu.VMEM)),
        out_shape=(sems, sems, *[pltpu.HBM(a.shape, a.dtype) for a in both], token_type),
        input_output_aliases={t: 2 + t for t in range(2 * nt)}, compiler_params=_split_params(),
    )(*[_in_hbm(a) for a in both], *given)
    return out[0], out[1], list(out[2:2 + nt]), list(out[2 + nt:2 + 2 * nt]), out[-1]


def exchange_wait(name, send_sems, recv_sems, parts, lands, kinds, shard_shapes, after):
    nt = len(parts)

    def body(*refs):
        part_refs, land_refs = refs[:nt], refs[nt:2 * nt]
        send_ref, recv_ref = refs[2 * nt], refs[2 * nt + 1]
        x, y, c, others = _position()
        for t in range(nt):
            for j, (ox, oy) in enumerate(others):
                cp = pltpu.make_async_remote_copy(
                    src_ref=_piece(part_refs[t], kinds[t], 2 * ox + oy, shard_shapes[t]), dst_ref=land_refs[t].at[j],
                    send_sem=send_ref.at[3 * t + j], recv_sem=recv_ref.at[3 * t + j],
                    device_id=(ox, oy, c), device_id_type=MESH)
                cp.wait_send()
                cp.wait_recv()

    both = list(parts) + list(lands)
    out = pl.pallas_call(
        body, name=name, in_specs=[HBM_SPEC] * (2 * nt) + [SEM_SPEC, SEM_SPEC, HBM_SPEC], out_specs=[HBM_SPEC] * (2 * nt),
        out_shape=[pltpu.HBM(a.shape, a.dtype) for a in both], input_output_aliases={t: t for t in range(2 * nt)},
        compiler_params=_split_params())(*both, send_sems, recv_sems, _in_hbm(after))
    return list(out[:nt]), list(out[nt:])


def reduce_swap(bufs, wire):
    n = len(bufs)
    halves = [b.shape[0] // 2 for b in bufs]

    def body(*refs):
        in_refs, out_refs, txs, got = refs[:n], refs[n:2 * n], refs[2 * n:3 * n], refs[3 * n:4 * n]
        send_sems, recv_sems = refs[4 * n:]
        x, y, c, _ = _position()
        cps = []
        for k in range(n):
            txs[k][...] = in_refs[k][pl.ds(pl.multiple_of((1 - c) * halves[k], 8), halves[k]), :].astype(wire[k])
            cp = pltpu.make_async_remote_copy(src_ref=txs[k], dst_ref=got[k], send_sem=send_sems.at[k],
                                              recv_sem=recv_sems.at[k], device_id=(x, y, 1 - c), device_id_type=MESH)
            cp.start()
            cps.append(cp)
        for k, cp in enumerate(cps):
            cp.wait()
            own = in_refs[k][pl.ds(pl.multiple_of(c * halves[k], 8), halves[k]), :]
            out_refs[k][...] = (own.astype(wire[k]).astype(F32) + got[k][...].astype(F32)).astype(wire[k])

    vm = pl.BlockSpec(memory_space=pltpu.VMEM)
    parts = [((h, b.shape[1]), w) for h, b, w in zip(halves, bufs, wire)]
    return list(pl.pallas_call(
        body, name="reduce_swap", in_specs=[vm] * n, out_specs=[vm] * n, out_shape=[_sds(sh, w) for sh, w in parts],
        scratch_shapes=[pltpu.VMEM(sh, w) for sh, w in parts] * 2 + [pltpu.SemaphoreType.DMA((n,))] * 2,
        compiler_params=_params())(*bufs))


def reduce_start(parts):
    n = len(parts)
    lands = [lax.empty((4,) + tuple(p.shape), p.dtype) for p in parts]

    def body(*refs):
        part_refs, land_refs, send_sems, recv_sems = refs[:n], refs[n:2 * n], refs[2 * n], refs[2 * n + 1]
        x, y, c, others = _position()
        for k in range(n):
            for j, (ox, oy) in enumerate(others):
                pltpu.make_async_remote_copy(
                    src_ref=part_refs[k], dst_ref=land_refs[k].at[2 * x + y], send_sem=send_sems.at[3 * k + j],
                    recv_sem=recv_sems.at[3 * k + j], device_id=(ox, oy, c), device_id_type=MESH).start()

    sems = pltpu.SemaphoreType.DMA((3 * n,))
    both = list(parts) + lands
    out = pl.pallas_call(
        body, name="reduce_start", in_specs=[HBM_SPEC] * (2 * n), out_specs=(SEM_SPEC, SEM_SPEC, *[HBM_SPEC] * (2 * n)),
        out_shape=(sems, sems, *[pltpu.HBM(a.shape, a.dtype) for a in both]),
        input_output_aliases={k: 2 + k for k in range(2 * n)}, compiler_params=_split_params(),
    )(*[_in_hbm(a) for a in both])
    return out[0], out[1], list(out[2:2 + n]), list(out[2 + n:])


def reduce_wait(send_sems, recv_sems, parts, lands, after):
    n = len(parts)

    def body(*refs):
        part_refs, land_refs, send_ref, recv_ref = refs[:n], refs[n:2 * n], refs[2 * n], refs[2 * n + 1]
        x, y, c, others = _position()
        for k in range(n):
            for j, (ox, oy) in enumerate(others):
                cp = pltpu.make_async_remote_copy(
                    src_ref=part_refs[k], dst_ref=land_refs[k].at[2 * ox + oy], send_sem=send_ref.at[3 * k + j],
                    recv_sem=recv_ref.at[3 * k + j], device_id=(ox, oy, c), device_id_type=MESH)
                cp.wait_send()
                cp.wait_recv()

    both = list(parts) + list(lands)
    out = pl.pallas_call(
        body, name="reduce_wait", in_specs=[HBM_SPEC] * (2 * n) + [SEM_SPEC, SEM_SPEC, HBM_SPEC],
        out_specs=[HBM_SPEC] * (2 * n), out_shape=[pltpu.HBM(a.shape, a.dtype) for a in both],
        input_output_aliases={k: k for k in range(2 * n)}, compiler_params=_split_params(),
    )(*both, send_sems, recv_sems, _in_hbm(after))
    return list(out[:n]), list(out[n:])


def reduce_share(parts, lands):
    n = len(parts)
    halves = [p.shape[0] for p in parts]

    def body(*refs):
        part_refs, land_refs, out_refs = refs[:n], refs[n:2 * n], refs[2 * n:3 * n]
        send_sems, recv_sems = refs[3 * n:]
        x, y, c, _ = _position()
        chip = 2 * x + y
        cps = []
        for k in range(n):
            mine = pl.ds(pl.multiple_of(c * halves[k], 8), halves[k])
            own = part_refs[k][...].astype(F32)
            total = jnp.where(chip == 0, own, land_refs[k][0].astype(F32))
            for entry in range(1, 4):
                total = total + jnp.where(chip == entry, own, land_refs[k][entry].astype(F32))
            out_refs[k][mine, :] = total
            cp = pltpu.make_async_remote_copy(
                src_ref=out_refs[k].at[mine], dst_ref=out_refs[k].at[mine], send_sem=send_sems.at[k],
                recv_sem=recv_sems.at[k], device_id=(x, y, 1 - c), device_id_type=MESH)
            cp.start()
            cps.append(cp)
        for cp in cps:
            cp.wait()

    vm = pl.BlockSpec(memory_space=pltpu.VMEM)
    return list(pl.pallas_call(
        body, name="reduce_share", in_specs=[vm] * (2 * n), out_specs=[vm] * n,
        out_shape=[_sds((2 * p.shape[0], p.shape[1]), F32) for p in parts],
        scratch_shapes=[pltpu.SemaphoreType.DMA((n,))] * 2, compiler_params=_params())(*parts, *lands))


def _local_step(x, target, small, need, ahead, emit_swap, emit_exchange):
    d = D_MODEL
    full = {}

    def handed(vec, token):
        return vec if token is None else token

    def token_rows(token):
        return [] if token is None else [token]

    def plus(acc, rows):
        return acc + rows[0] if rows else acc

    rb16, rbt16, rc16, rct16, lr_t, li_t = small["s5_operands"]
    ge, ge_slope, cs = s5_fwd(x, small["norm_mix0"], small["s5_d"], rb16, rc16, lr_t, li_t)
    full.update(need("glu", ge))

    def norm_rows(h, gains):
        xh, _ = _rms_hat(h)
        return [xh * g for g in gains]

    def glu_epilogue(accs, e, r):
        v, gt = accs[0] + r[0], accs[1] + r[1]
        h = e[0] + v * jax.nn.sigmoid(gt)
        return [h, v, gt] + norm_rows(h, r[2:])

    gain_mlp0 = handed(small["norm_mlp0"], ahead("mlp_in0", full["w_glu"], small["norm_mlp0"]))
    h1, val, gate, n1 = mm_nn(
        "glu", ge, full["w_glu"], [0, d], d, glu_epilogue, [F32, F32, F32, BF16], extras=[x],
        rowvecs=[(small["s5_b_glu"], 0), (small["s5_b_glu"], d), (gain_mlp0, 0)], tm=512, tn=d)

    def mlp_fwd(tag, h, n, w_in, get_w_out, next_gains, head=None):
        def in_epilogue(accs, e, rv):
            pos = jnp.maximum(accs[0], 0.0)
            return [pos * pos, 2.0 * pos]

        r, slope = mm_nn("mlp_in" + tag, n, w_in, [0], w_in.shape[1], in_epilogue, [BF16, BF16], tm=2048)
        w_out = get_w_out(r)

        def epilogue(accs, e, rv):
            h_out = e[0] + accs[0]
            return [h_out] + norm_rows(h_out, rv)

        if head is not None:
            return head(r, w_out, h), (n, r, slope)
        outs = mm_nn("mlp_out" + tag, r, w_out, [0], d, epilogue, [F32] + [BF16] * len(next_gains), extras=[h],
                     rowvecs=[(g, 0) for g in next_gains], tm=512, tn=d)
        return outs[0], outs[1:], (n, r, slope)

    full.update(need("mlp_in0", h1))

    def w_out0(after):
        full.update(need("mlp_out0", after))
        return full["w_out0"]

    h2, (nkv, n2), mlp0 = mlp_fwd("0", h1, n1, full["w_in0"], w_out0, [small["norm_kv"], small["norm_mix1"]])

    full.update(need("attn", h2))
    kvw = 2 * N_KV * HEAD_DIM
    (kv,) = mm_nn("kv_proj", nkv, full["w_kv"], [0], kvw, lambda accs, e, r: [accs[0] + r[0]], [BF16],
                  rowvecs=[(small["b_kv"], 0)], tm=2048)
    (q,) = mm_nn("q_proj", n2, full["w_q"], [0], d, lambda accs, e, r: [accs[0] + r[0]], [BF16],
                 rowvecs=[(small["b_q"], 0)], tm=2048)
    sinks = small["sinks"].reshape(N_Q)
    o = attn_fwd(q, kv, sinks)
    def o_epilogue(accs, e, r):
        h_out = e[0] + accs[0] + r[0]
        return [h_out] + norm_rows(h_out, r[1:])

    bias_o = handed(small["b_o"], ahead("mlp_in1", o, small["b_o"]))
    h3, n3 = mm_nn("o_proj", o, full["w_o"], [0], d, o_epilogue, [F32, BF16], extras=[h2],
                   rowvecs=[(bias_o, 0), (small["norm_mlp1"], 0)], tm=512, tn=d)
    full.update(need("mlp_in1", h3, then="mlp_out1"))

    def w_out1(after):
        full.update(need("mlp_out1", after))
        return full["w_out1"]

    def loss_head(r, w_out, h):
        def epilogue(accs, e, rv):
            xh, rr = _rms_hat(e[0] + accs[0])
            err = xh * rv[0] - e[1]
            dy = err * (1.0 / d)
            dxh = dy * rv[0]
            dx = rr * (dxh - xh * jnp.mean(dxh * xh, axis=-1, keepdims=True))
            loss = jnp.full((1, d), 0.5 * jnp.sum(jnp.mean(err * err, axis=-1, keepdims=True)), F32)
            return [dx, dx, loss, jnp.sum(dy * xh, axis=0, keepdims=True)]

        return mm_nn("mlp_out1", r, w_out, [0], d, epilogue, [F32, BF16], extras=[h, target],
                     rowvecs=[(small["norm_final"], 0)], n_sums=2, tm=512, tn=d)

    (dh, dhb, loss_tile, dg_final), mlp1 = mlp_fwd("1", h3, n3, full["w_in1"], w_out1, [], head=loss_head)

    grads_small, grads_full = {"norm_final": dg_final}, {}
    ident = lambda acc, e, r: [plus(acc, r)]
    layer1 = ["w_out1", "w_in1", "w_o", "w_q", "w_kv"]
    layer0 = ["w_out0", "w_in0", "w_glu"]

    def norm_bwd_rows(x_rows, res, dys, gains):
        xh, r = _rms_hat(x_rows)
        dxh = sum(dy * g for dy, g in zip(dys, gains))
        dx = r * (dxh - xh * jnp.mean(dxh * xh, axis=-1, keepdims=True)) + res
        return dx, [jnp.sum(dy * xh, axis=0, keepdims=True) for dy in dys]

    def mlp_bwd(tag, dh, dhb, h_in, gain, w_in, w_out, saved, token=None):
        n, r, slope = saved
        grads_full["w_out" + tag] = mm_tn("dw_out" + tag, r, dhb, tn=1024)
        (da,) = mm_nt("mlp_da" + tag, dhb, w_out, lambda acc, e, rv: [plus(acc * e[0].astype(F32), rv)], [BF16],
                      extras=[slope], rowvecs=token_rows(token), tm=2048)
        grads_full["w_in" + tag] = mm_tn("dw_in" + tag, n, da, tn=1024)

        def epilogue(acc, e, rv):
            dx, dgs = norm_bwd_rows(e[0], e[1], [acc], rv)
            return [dx, dx, jnp.sum(dx, axis=0, keepdims=True)] + dgs

        dx, dxb, colsum, dg = mm_nt("mlp_dn" + tag, da, w_in, epilogue, [F32, BF16], extras=[h_in, dh], rowvecs=[gain],
                                    n_sums=2, tm=512, tk=d)
        grads_small["norm_mlp" + tag] = dg
        return dx, dxb, colsum

    dh3, dh3b, colsum3 = mlp_bwd("1", dh, dhb, h3, small["norm_mlp1"], full["w_in1"], full["w_out1"], mlp1)
    grads_small["b_o"] = colsum3
    grads_full["w_o"] = mm_tn("dw_o", o, dh3b, tn=1024)
    (do,) = mm_nt("attn_do", dh3b, full["w_o"], ident, [BF16], tm=2048)
    dq, dbq, dprev, dcur, dsink = attn_bwd(q, kv, do, sinks)
    dkv, dbkv = kv_combine(dprev, dcur)
    grads_small["b_q"], grads_small["b_kv"], grads_small["sinks"] = dbq, dbkv, dsink
    grads_full["w_q"] = mm_tn("dw_q", n2, dq, tn=1024)
    grads_full["w_kv"] = mm_tn("dw_kv", nkv, dkv, tk=1024)
    token = emit_swap("layer1", {n: grads_full[n] for n in layer1}, (1, d))
    (dnkv,) = mm_nt("kv_dn", dkv, full["w_kv"], ident, [F32], rowvecs=token_rows(token), tm=2048, tk=1024)

    def attn_dn_epilogue(acc, e, rv):
        dx, dgs = norm_bwd_rows(e[0], e[1], [acc, e[2]], rv)
        return [dx, dx] + dgs

    dh2, dh2b, dg_mix1, dg_kv = mm_nt("attn_dn", dq, full["w_q"], attn_dn_epilogue, [F32, BF16], extras=[h2, dh3, dnkv],
                                      rowvecs=[small["norm_mix1"], small["norm_kv"]], n_sums=2, tm=512, tk=d)
    grads_small["norm_mix1"], grads_small["norm_kv"] = dg_mix1, dg_kv
    token = emit_exchange("layer1", dh2b, (1, full["w_out0"].shape[0]))
    dh1, _, _ = mlp_bwd("0", dh2, dh2b, h1, small["norm_mlp0"], full["w_in0"], full["w_out0"], mlp0, token)

    dz, db_glu = glu_bwd(dh1, val, gate)
    grads_small["s5_b_glu"] = db_glu
    grads_full["w_glu"] = mm_tn("dw_glu", ge, dz, tn=1024)
    token = emit_swap("layer0", {n: grads_full[n] for n in layer0}, (1, d))
    (dy2,) = mm_nt("glu_dy", dz, full["w_glu"], lambda acc, e, rv: [plus(acc, rv) * e[0]], [F32], extras=[ge_slope],
                   rowvecs=token_rows(token), tm=512, tk=1024)
    d_skip = handed(small["s5_d"], emit_exchange("layer0", dy2, small["s5_d"]))
    grad_x, dd, drb, drc, dlr, dli, dg_mix0 = s5_bwd(x, small["norm_mix0"], dy2, dh1, d_skip, cs, rb16, rbt16, rct16, lr_t, li_t)
    grads_small["s5_d"] = dd
    grads_small["s5_mats"] = (drb, drc, dlr, dli)
    grads_small["norm_mix0"] = dg_mix0
    return loss_tile, grad_x, grads_small


SMALL_NAMES = ["norm_mix", "norm_mlp", "norm_kv", "norm_final", "s5_a_re", "s5_a_im", "s5_log_dt", "s5_b_re", "s5_b_im",
               "s5_c_re", "s5_c_im", "s5_d", "s5_b_glu", "b_kv", "b_q", "sinks", "b_o"]
BIG_NAMES = ["s5_w_glu", "w_kv", "w_q", "w_o", "w_mlp_in", "w_mlp_out"]
WEIGHT_ORDER = ["norm_mix", "norm_mlp", "norm_kv", "norm_final", "s5_a_re", "s5_a_im", "s5_log_dt", "s5_b_re", "s5_b_im",
                "s5_c_re", "s5_c_im", "s5_d", "s5_w_glu", "s5_b_glu", "w_kv", "b_kv", "w_q", "b_q", "sinks", "w_o", "b_o",
                "w_mlp_in", "w_mlp_out"]


def kernel(x, norm_mix, norm_mlp, norm_kv, norm_final, s5_a_re, s5_a_im, s5_log_dt, s5_b_re, s5_b_im, s5_c_re, s5_c_im, s5_d, s5_w_glu, s5_b_glu, w_kv, b_kv, w_q, b_q, sinks, w_o, b_o, w_mlp_in, w_mlp_out, loss_target, m_norm_mix, m_norm_mlp, m_norm_kv, m_norm_final, m_s5_a_re, m_s5_a_im, m_s5_log_dt, m_s5_b_re, m_s5_b_im, m_s5_c_re, m_s5_c_im, m_s5_d, m_s5_w_glu, m_s5_b_glu, m_w_kv, m_b_kv, m_w_q, m_b_q, m_sinks, m_w_o, m_b_o, m_w_mlp_in, m_w_mlp_out, v_norm_mix, v_norm_mlp, v_norm_kv, v_norm_final, v_s5_a_re, v_s5_a_im, v_s5_log_dt, v_s5_b_re, v_s5_b_im, v_s5_c_re, v_s5_c_im, v_s5_d, v_s5_w_glu, v_s5_b_glu, v_w_kv, v_b_kv, v_w_q, v_b_q, v_sinks, v_w_o, v_b_o, v_w_mlp_in, v_w_mlp_out):
    env = dict(locals())
    w = {n: env[n] for n in WEIGHT_ORDER}
    mom = {n: env["m_" + n] for n in WEIGHT_ORDER}
    var = {n: env["v_" + n] for n in WEIGHT_ORDER}
    d = D_MODEL
    xi, yi, ci = lax.axis_index("x"), lax.axis_index("y"), lax.axis_index("c")
    chip = 2 * xi + yi
    where = jnp.stack([ci, chip]).astype(jnp.int32)

    dsh, bsh = s5_d.shape[1], s5_b_glu.shape[1]
    packed = jnp.concatenate([s5_d.reshape(-1, 128), s5_b_glu.reshape(-1, 128)])
    n_d, n_b = dsh // 128, bsh // 128
    slab = lax.dynamic_update_slice(jnp.zeros((4, 8, 128), F32), jnp.pad(packed, ((0, 8 - n_d - n_b), (0, 0)))[None],
                                    (chip, 0, 0))

    big = [s5_w_glu, w_kv[None], w_q, w_o, w_mlp_in, w_mlp_out]
    entries = [(0, 0, "col"), (1, 0, "row"), (2, 0, "row"), (3, 0, "row"), (4, 0, "col"), (4, 1, "col"),
               (5, 0, "row"), (5, 1, "row")]
    names = ["w_glu", "w_kv", "w_q", "w_o", "w_in0", "w_in1", "w_out0", "w_out1"]
    kinds = dict(zip(names, [k for _, _, k in entries]))
    shard_shapes = dict(zip(names, [tuple(big[a].shape[1:]) for a, _, _ in entries]))

    placed_w = dict(zip(names, cast_place(big, entries, where)))
    placed_w["vectors"], kinds["vectors"], shard_shapes["vectors"] = slab, "slab", None
    gather_groups = {"glu": ["w_glu"], "mlp_in0": ["w_in0"], "mlp_out0": ["w_out0"], "attn": ["w_kv", "w_q", "w_o"],
                     "mlp_in1": ["w_in1"], "mlp_out1": ["w_out1"]}
    order = ["vectors"] + [n for members in gather_groups.values() for n in members]
    send, recv, thru, log_dt = gather_start([placed_w[n] for n in order], [kinds[n] for n in order],
                                            [shard_shapes[n] for n in order], s5_log_dt)
    started = dict(zip(order, thru))
    (gathered_rows,) = gather_wait("gather_wait_vectors", send, recv, [started["vectors"]], ["slab"], [None], None, 0)
    d_full = gathered_rows[:, 0:n_d].reshape(1, -1)
    bglu_full = gathered_rows[:, n_d:n_d + n_b].reshape(1, -1)

    forwarding = {}

    def ahead(group, after, carry, passing=()):
        members = gather_groups[group]
        ks, shapes = [kinds[n] for n in members], [shard_shapes[n] for n in members]
        d2d_send, d2d_recv, landed, passed, tok = forward_start(
            "forward_start_" + group, send, recv, [started[n] for n in members], ks, shapes, after,
            order.index(members[0]), carry, passing)
        forwarding[group] = (d2d_send, d2d_recv, landed)
        return passed if passing else tok

    def need(group, after, then=None):
        members = gather_groups[group]
        ks, shapes = [kinds[n] for n in members], [shard_shapes[n] for n in members]
        if group in forwarding:
            arrays = forward_wait("forward_wait_" + group, *forwarding[group], ks, shapes, after)
        else:
            landed = gather_wait("gather_wait_" + group, send, recv, [started[n] for n in members], ks, shapes, after,
                                 order.index(members[0]))
            arrays = forward_halves("forward_halves_" + group, landed, ks, shapes)
        if then is not None:
            arrays = ahead(then, after, (8, 128), arrays)
        return dict(zip(members, arrays))

    swapping, exchanging = {}, {}

    def emit_swap(group, partial, carry):
        members = list(partial)
        send, recv, mine, lands, tok = swap_start("swap_start_" + group, [partial[n] for n in members],
                                                  [kinds[n] for n in members], carry)
        swapping[group] = (members, send, recv, mine, lands)
        return tok

    def emit_exchange(group, after, carry):
        members, send, recv, mine, lands = swapping[group]
        ks, shapes = [kinds[n] for n in members], [shard_shapes[n] for n in members]
        mine, landed = swap_wait("swap_wait_" + group, send, recv, mine, lands, ks, after)
        sums = add_halves("add_halves_" + group, mine, landed, ks, where)
        send, recv, parts, lands, tok = exchange_start("exchange_start_" + group, sums, ks, shapes, carry)
        exchanging[group] = (members, send, recv, parts, lands)
        return tok

    s5_args = (s5_a_re[0], s5_a_im[0], log_dt[0], s5_b_re[0], s5_b_im[0])
    small = {
        "norm_mix0": norm_mix[0:1], "norm_mix1": norm_mix[1:2], "norm_mlp0": norm_mlp[0:1], "norm_mlp1": norm_mlp[1:2],
        "norm_kv": norm_kv.reshape(1, d), "norm_final": norm_final.reshape(1, d), "s5_operands": s5_prep(*s5_args, s5_c_re[0], s5_c_im[0]),
        "s5_d": d_full, "s5_b_glu": bglu_full,
        "b_kv": b_kv.reshape(1, -1), "b_q": b_q, "sinks": sinks, "b_o": b_o,
    }
    loss_row, grad_x, gs = _local_step(x[0], loss_target[0], small, need, ahead, emit_swap, emit_exchange)

    mats, lams = s5_compact(*gs["s5_mats"])
    rows = [gs["norm_mix0"], gs["norm_mix1"], gs["norm_mlp0"], gs["norm_mlp1"], gs["norm_kv"], gs["norm_final"], gs["s5_d"],
            gs["b_q"], gs["b_o"], gs["s5_b_glu"], gs["b_kv"], gs["sinks"], loss_row, jnp.zeros((2, d), F32)]
    small_send, small_recv, small_parts, small_lands = reduce_start(
        reduce_swap([jnp.concatenate(rows, axis=0), lams, mats], [F32, F32, BF16]))

    reduced = [None] * len(big)
    where_of = dict(zip(names, entries))
    for group in ("layer1", "layer0"):
        members, send, recv, parts, lands = exchanging[group]
        ks, shapes = [kinds[n] for n in members], [shard_shapes[n] for n in members]
        parts, lands = exchange_wait("exchange_wait_" + group, send, recv, parts, lands, ks, shapes, small_lands[-1])
        targets = [where_of[n][0] for n in members]
        sums = sum_shards("sum_shards_" + group, parts, lands, ks, shapes, where, [where_of[n][1] for n in members],
                          [big[a].shape[0] for a in targets], [reduced[a] for a in targets])
        for a, arr in zip(targets, sums):
            reduced[a] = arr
    share_send, share_recv, reduced, _ = share_start(reduced, entries, (8, 128))

    vecs, lams, mats = reduce_share(*reduce_wait(small_send, small_recv, small_parts, small_lands, reduced[0]))
    grads = split_vectors(where, vecs, dsh, bsh)
    loss = grads.pop("loss")[0, 0]
    g_are, g_aim, g_dt, g_bre, g_bim, dc_re, dc_im = s5_param_bwd(mats, lams, *s5_args)
    grads.update({"s5_a_re": g_are[None], "s5_a_im": g_aim[None], "s5_log_dt": g_dt[None], "s5_b_re": g_bre[None],
                  "s5_b_im": g_bim[None], "s5_c_re": dc_re[None], "s5_c_im": dc_im[None]})

    delta, new_m, new_v = {}, {}, {}

    def view(n, a):
        return a.reshape(1, -1) if a.ndim == 1 else jnp.swapaxes(a, -1, -2) if n in ("s5_b_re", "s5_b_im") else a

    sw, sg, sm, sv = ([view(n, t[n]) for n in SMALL_NAMES] for t in (w, grads, mom, var))
    for n, a, b, c_ in zip(SMALL_NAMES, *adamw_native("adamw_small", sw, sg, sm, sv)):
        delta[n], new_m[n], new_v[n] = (view(n, t) if t.ndim == 4 else t for t in (a, b, c_))

    reduced = share_wait(share_send, share_recv, reduced, entries, new_v["s5_c_re"])
    for n, g in zip(BIG_NAMES, reduced):
        grads[n] = g.reshape(w[n].shape)
    flat = lambda t: [t[n].reshape(-1, t[n].shape[-1]) for n in BIG_NAMES]
    for table, arrays in zip((grads, delta, new_m, new_v), adamw("adamw_big", flat(w), flat(grads), flat(mom), flat(var))):
        for n, a in zip(BIG_NAMES, arrays):
            table[n] = a.reshape(w[n].shape)

    out = [loss.reshape(()), grad_x[None]]
    for table in (grads, delta, new_m, new_v):
        out += [table[n].reshape(w[n].shape) for n in WEIGHT_ORDER]
    return tuple(out)
```

```python
import math

import jax
import jax.numpy as jnp
from jax import lax
from jax.experimental import pallas as pl
from jax.experimental.pallas import tpu as pltpu

F32 = jnp.float32
BF16 = jnp.bfloat16

D_MODEL = 1024
S5_GROUPS = 64
S5_GROUP = 16
S5_STATE = 64
N_KV = 4
N_Q = 16
HEAD_DIM = 64
BLOCK = 128
NORM_EPS = 1e-5
LAMBDA_RE_MAX = -1e-4
ADAM_LR, ADAM_B1, ADAM_B2, ADAM_EPS, ADAM_WD, ADAM_STEP = 0.001, 0.9, 0.999, 1e-08, 0.01, 10

VMEM_LIMIT_BYTES = 56 * 1024 * 1024
S5_CHUNK = 256
S5_BLOCKS = 4
MESH = pl.DeviceIdType.MESH


def _params(sem=None):
    return pltpu.CompilerParams(dimension_semantics=sem, vmem_limit_bytes=VMEM_LIMIT_BYTES)


def _sds(shape, dtype):
    return jax.ShapeDtypeStruct(shape, dtype)


def _rms_hat(xv):
    r = lax.rsqrt(jnp.mean(xv * xv, axis=-1, keepdims=True) + NORM_EPS)
    return xv * r, r


def mm_nn(name, a, w, col_offsets, n_out, epilogue, out_dtypes, extras=(), rowvecs=(), n_sums=0, tm=1024, tn=512):
    m, k = a.shape
    tm, tn = min(tm, m), min(tn, n_out)
    nw, ne, nr, no = len(col_offsets), len(extras), len(rowvecs), len(out_dtypes)

    def body(a_ref, *refs):
        w_refs, e_refs, r_refs = refs[:nw], refs[nw:nw + ne], refs[nw + ne:nw + ne + nr]
        o_refs, s_refs = refs[nw + ne + nr:nw + ne + nr + no], refs[nw + ne + nr + no:]
        av = a_ref[...]
        accs = [jnp.dot(av, w_ref[...], preferred_element_type=F32) for w_ref in w_refs]
        outs = epilogue(accs, [e[...] for e in e_refs], [r[...] for r in r_refs])
        for o_ref, o in zip(o_refs, outs[:no]):
            o_ref[...] = o.astype(o_ref.dtype)
        if n_sums:
            @pl.when(pl.program_id(1) == 0)
            def _():
                for s_ref in s_refs:
                    s_ref[...] = jnp.zeros_like(s_ref)

            for s_ref, val in zip(s_refs, outs[no:]):
                s_ref[...] += val

    def wspec(off):
        return pl.BlockSpec((k, tn), lambda j, i, off=off: (0, off // tn + j))

    def rspec(off):
        return pl.BlockSpec((1, tn), lambda j, i, off=off: (0, off // tn + j))

    tile = pl.BlockSpec((tm, tn), lambda j, i: (i, j))
    in_specs = ([pl.BlockSpec((tm, k), lambda j, i: (i, 0))] + [wspec(o) for o in col_offsets]
                + [tile] * ne + [rspec(o) for _, o in rowvecs])
    sem = ("parallel", "arbitrary") if n_sums else ("parallel", "parallel")
    return pl.pallas_call(
        body, grid=(n_out // tn, m // tm), in_specs=in_specs,
        out_specs=[tile] * no + [pl.BlockSpec((1, tn), lambda j, i: (0, j))] * n_sums,
        out_shape=[_sds((m, n_out), dt) for dt in out_dtypes] + [_sds((1, n_out), F32)] * n_sums, name=name,
        compiler_params=_params(sem))(a, *([w] * nw), *extras, *[r for r, _ in rowvecs])


def mm_nt(name, g, w, epilogue, out_dtypes, extras=(), rowvecs=(), n_sums=0, tm=512, tk=512):
    m, n = g.shape
    k = w.shape[0]
    tm, tk = min(tm, m), min(tk, k)
    ne, nr, no = len(extras), len(rowvecs), len(out_dtypes)

    def body(g_ref, w_ref, *refs):
        e_refs, r_refs, o_refs, s_refs = refs[:ne], refs[ne:ne + nr], refs[ne + nr:ne + nr + no], refs[ne + nr + no:]
        acc = lax.dot_general(g_ref[...], w_ref[...], (((1,), (1,)), ((), ())), preferred_element_type=F32)
        outs = epilogue(acc, [e[...] for e in e_refs], [r[...] for r in r_refs])
        for o_ref, o in zip(o_refs, outs[:no]):
            o_ref[...] = o.astype(o_ref.dtype)
        if n_sums:
            @pl.when(pl.program_id(0) == 0)
            def _():
                for s_ref in s_refs:
                    s_ref[...] = jnp.zeros_like(s_ref)

            for s_ref, val in zip(s_refs, outs[no:]):
                s_ref[...] += val

    tile = pl.BlockSpec((tm, tk), lambda i, j: (i, j))
    vec = pl.BlockSpec((1, tk), lambda i, j: (0, j))
    sem = ("arbitrary", "parallel") if n_sums else ("parallel", "parallel")
    return pl.pallas_call(
        body, grid=(m // tm, k // tk),
        in_specs=[pl.BlockSpec((tm, n), lambda i, j: (i, 0)), pl.BlockSpec((tk, n), lambda i, j: (j, 0))]
        + [tile] * ne + [vec] * nr,
        out_specs=[tile] * no + [vec] * n_sums,
        out_shape=[_sds((m, k), dt) for dt in out_dtypes] + [_sds((1, k), F32)] * n_sums, name=name,
        compiler_params=_params(sem))(g, w, *extras, *rowvecs)


def mm_tn(name, a, g, tk=512, tn=512):
    m, k = a.shape
    n = g.shape[1]
    tk, tn = min(tk, k), min(tn, n)

    def body(a_ref, g_ref, o_ref):
        acc = lax.dot_general(a_ref[...], g_ref[...], (((0,), (0,)), ((), ())), preferred_element_type=F32)
        o_ref[...] = acc.astype(o_ref.dtype)

    return pl.pallas_call(
        body, grid=(k // tk, n // tn),
        in_specs=[pl.BlockSpec((m, tk), lambda i, j: (0, i)), pl.BlockSpec((m, tn), lambda i, j: (0, j))],
        out_specs=pl.BlockSpec((tk, tn), lambda i, j: (i, j)), out_shape=_sds((k, n), BF16), name=name,
        compiler_params=_params(("parallel", "parallel")))(a, g)


def _row_mask(tc):
    row = lax.broadcasted_iota(jnp.int32, (8 * tc, 256), 0) % 8
    col = lax.broadcasted_iota(jnp.int32, (8 * tc, 256), 1) // 32
    return row == col


def _expand_rows(val, mask):
    tc, width = val.shape
    rep = jnp.broadcast_to(val[:, None, :], (tc, 8, width)).reshape(8 * tc, width)
    return jnp.where(mask, rep, 0.0).astype(BF16)


def _stage(ref, val):
    ref[0] = val[:, 0:128]
    ref[1] = val[:, 128:256]


def _gather_rows(src_ref, tc):
    halves = []
    for half in range(2):
        col = lax.broadcasted_iota(jnp.int32, (tc, 128), 1) // 32 + 4 * half
        out = jnp.zeros((tc, 128), F32)
        for s8 in range(4 * half, 4 * half + 4):
            out = jnp.where(col == s8, src_ref.at[half][pl.ds(s8, tc, stride=8), :], out)
        halves.append(out)
    return jnp.concatenate(halves, axis=1)


def _repeat(n, by, step, carry):
    def trip(i, c):
        for j in range(by):
            c = step(i * by + j, c)
        return c

    return lax.fori_loop(0, n // by, trip, carry)


def _gelu_and_slope(x):
    c = math.sqrt(2.0 / math.pi)
    t = jnp.tanh(c * (x + 0.044715 * x * x * x))
    return 0.5 * x * (1.0 + t), 0.5 * (1.0 + t) + 0.5 * x * (1.0 - t * t) * c * (1.0 + 3.0 * 0.044715 * x * x)


def s5_fwd(x, gain, d_skip, rb, rc, lam_r, lam_i):
    n_rows = x.shape[0]
    tc = min(S5_CHUNK, n_rows)
    nc = n_rows // tc

    def body(x_ref, g_ref, d_ref, rb_ref, rc_ref, lr_ref, li_ref, ge_ref, slope_ref, cs_ref, bux, yrows, carry):
        i = pl.program_id(0)
        u = _rms_hat(x_ref[...])[0] * g_ref[...]

        @pl.when(i == 0)
        def _():
            carry[...] = jnp.zeros_like(carry)

        cs_ref[0] = carry[...]
        mask = _row_mask(tc)
        for blk in range(S5_BLOCKS):
            lhs = _expand_rows(u[:, blk * 256:(blk + 1) * 256], mask)
            bux[blk] = jnp.dot(lhs, rb_ref[blk], preferred_element_type=F32)
        lam = [(lr_ref[blk], li_ref[blk]) for blk in range(S5_BLOCKS)]

        def step(t, c):
            r0 = pl.multiple_of(t * 8, 8)
            new = []
            for blk in range(S5_BLOCKS):
                xr, xi = c[2 * blk], c[2 * blk + 1]
                lr, li = lam[blk]
                nr = lr * xr - li * xi + bux[blk, pl.ds(r0, 8), 0:128]
                ni = lr * xi + li * xr + bux[blk, pl.ds(r0, 8), 128:256]
                bux[blk, pl.ds(r0, 8), 0:128] = nr
                bux[blk, pl.ds(r0, 8), 128:256] = ni
                new += [nr, ni]
            return tuple(new)

        c0 = []
        for blk in range(S5_BLOCKS):
            c0 += [carry[blk, :, 0:128], carry[blk, :, 128:256]]
        cn = _repeat(tc, 8, step, tuple(c0))
        for blk in range(S5_BLOCKS):
            carry[blk, :, 0:128] = cn[2 * blk]
            carry[blk, :, 128:256] = cn[2 * blk + 1]
        for blk in range(S5_BLOCKS):
            _stage(yrows, jnp.dot(bux[blk].astype(BF16), rc_ref[blk], preferred_element_type=F32))
            sl = slice(blk * 256, (blk + 1) * 256)
            ge, slope = _gelu_and_slope(_gather_rows(yrows, tc) + d_ref[:, sl] * u[:, sl])
            slope_ref[:, sl] = slope
            ge_ref[:, sl] = ge.astype(BF16)

    row = pl.BlockSpec((tc, D_MODEL), lambda i: (i, 0))
    vec = pl.BlockSpec((1, D_MODEL), lambda i: (0, 0))
    mat = pl.BlockSpec((S5_BLOCKS, 256, 256), lambda i: (0, 0, 0))
    lamspec = pl.BlockSpec((S5_BLOCKS, 8, 128), lambda i: (0, 0, 0))
    return pl.pallas_call(
        body, grid=(nc,),
        in_specs=[row, vec, vec, mat, mat, lamspec, lamspec],
        out_specs=[row, row, pl.BlockSpec((1, S5_BLOCKS, 8, 256), lambda i: (i, 0, 0, 0))],
        out_shape=[_sds((n_rows, D_MODEL), BF16), _sds((n_rows, D_MODEL), F32), _sds((nc, S5_BLOCKS, 8, 256), F32)],
        scratch_shapes=[pltpu.VMEM((S5_BLOCKS, 8 * tc, 256), F32), pltpu.VMEM((2, 8 * tc, 128), F32),
                        pltpu.VMEM((S5_BLOCKS, 8, 256), F32)],
        name="s5_fwd", compiler_params=_params(("arbitrary",)))(x, gain, d_skip, rb, rc, lam_r, lam_i)


def s5_bwd(x, gain, dy2, res, d_skip, cs, rb, rbt, rct, lam_r, lam_i):
    n_rows = x.shape[0]
    tc = min(S5_CHUNK, n_rows)
    nc = n_rows // tc

    def body(x_ref, g_ref, dy_ref, res_ref, d_ref, cs_ref, rb_ref, rbt_ref, rct_ref, lr_ref, li_ref,
             dx_ref, dd_ref, drb_ref, drc_ref, dlr_ref, dli_ref, dg_ref, tmp, du, lhsu, lhsd, xs, adj, acarry):
        i = pl.program_id(0)
        u = _rms_hat(x_ref[...])[0] * g_ref[...]

        @pl.when(i == 0)
        def _():
            acarry[...] = jnp.zeros_like(acarry)
            dd_ref[...] = jnp.zeros_like(dd_ref)
            drb_ref[...] = jnp.zeros_like(drb_ref)
            drc_ref[...] = jnp.zeros_like(drc_ref)
            dlr_ref[...] = jnp.zeros_like(dlr_ref)
            dli_ref[...] = jnp.zeros_like(dli_ref)
            dg_ref[...] = jnp.zeros_like(dg_ref)

        dd_ref[...] += jnp.sum(dy_ref[...] * u, axis=0, keepdims=True)
        mask = _row_mask(tc)
        for blk in range(S5_BLOCKS):
            sl = slice(blk * 256, (blk + 1) * 256)
            lhsu[blk] = _expand_rows(u[:, sl], mask)
            xs[blk, 0:8] = cs_ref[0, blk]
            xs[blk, 8:8 * tc + 8] = jnp.dot(lhsu[blk], rb_ref[blk], preferred_element_type=F32)
            lhsd[blk] = _expand_rows(dy_ref[:, sl], mask)
            adj[blk] = jnp.dot(lhsd[blk], rct_ref[blk], preferred_element_type=F32)
        lam = [(lr_ref[blk], li_ref[blk]) for blk in range(S5_BLOCKS)]

        def fstep(t, c):
            r0 = pl.multiple_of(t * 8 + 8, 8)
            new = []
            for blk in range(S5_BLOCKS):
                xr, xi = c[2 * blk], c[2 * blk + 1]
                lr, li = lam[blk]
                nr = lr * xr - li * xi + xs[blk, pl.ds(r0, 8), 0:128]
                ni = lr * xi + li * xr + xs[blk, pl.ds(r0, 8), 128:256]
                xs[blk, pl.ds(r0, 8), 0:128] = nr
                xs[blk, pl.ds(r0, 8), 128:256] = ni
                new += [nr, ni]
            return tuple(new)

        c0 = []
        for blk in range(S5_BLOCKS):
            c0 += [cs_ref[0, blk, :, 0:128], cs_ref[0, blk, :, 128:256]]
        _repeat(tc, 8, fstep, tuple(c0))

        def bstep(k, c):
            t = tc - 1 - k
            r0 = pl.multiple_of(t * 8, 8)
            new_a, new_g = [], []
            for blk in range(S5_BLOCKS):
                ar, ai = c[0][2 * blk], c[0][2 * blk + 1]
                glr, gli = c[1][2 * blk], c[1][2 * blk + 1]
                lr, li = lam[blk]
                nr = lr * ar + li * ai + adj[blk, pl.ds(r0, 8), 0:128]
                ni = lr * ai - li * ar + adj[blk, pl.ds(r0, 8), 128:256]
                adj[blk, pl.ds(r0, 8), 0:128] = nr
                adj[blk, pl.ds(r0, 8), 128:256] = ni
                pr, pi = xs[blk, pl.ds(r0, 8), 0:128], xs[blk, pl.ds(r0, 8), 128:256]
                new_a += [nr, ni]
                new_g += [glr + nr * pr + ni * pi, gli + ni * pr - nr * pi]
            return tuple(new_a), tuple(new_g)

        a0, g0 = [], []
        for blk in range(S5_BLOCKS):
            a0 += [acarry[blk, :, 0:128], acarry[blk, :, 128:256]]
            g0 += [dlr_ref[blk], dli_ref[blk]]
        an, gn = _repeat(tc, 4, bstep, (tuple(a0), tuple(g0)))
        for blk in range(S5_BLOCKS):
            acarry[blk, :, 0:128] = an[2 * blk]
            acarry[blk, :, 128:256] = an[2 * blk + 1]
            dlr_ref[blk] = gn[2 * blk]
            dli_ref[blk] = gn[2 * blk + 1]
        for blk in range(S5_BLOCKS):
            sl = slice(blk * 256, (blk + 1) * 256)
            ab = adj[blk].astype(BF16)
            _stage(tmp, jnp.dot(ab, rbt_ref[blk], preferred_element_type=F32))
            du[:, sl] = _gather_rows(tmp, tc) + d_ref[:, sl] * dy_ref[:, sl]
            drb_ref[blk] += lax.dot_general(lhsu[blk], ab, (((0,), (0,)), ((), ())), preferred_element_type=F32)
            drc_ref[blk] += lax.dot_general(lhsd[blk], xs[blk, 8:8 * tc + 8].astype(BF16), (((0,), (0,)), ((), ())),
                                            preferred_element_type=F32)
        xh, r = _rms_hat(x_ref[...])
        dg_ref[...] += jnp.sum(du[...] * xh, axis=0, keepdims=True)
        dxh = du[...] * g_ref[...]
        dx_ref[...] = r * (dxh - xh * jnp.mean(dxh * xh, axis=-1, keepdims=True)) + res_ref[...]

    rev = pl.BlockSpec((tc, D_MODEL), lambda i: (nc - 1 - i, 0))
    vec = pl.BlockSpec((1, D_MODEL), lambda i: (0, 0))
    mat = pl.BlockSpec((S5_BLOCKS, 256, 256), lambda i: (0, 0, 0))
    lamspec = pl.BlockSpec((S5_BLOCKS, 8, 128), lambda i: (0, 0, 0))
    big = pltpu.VMEM((S5_BLOCKS, 8 * tc, 256), F32)
    bigb = pltpu.VMEM((S5_BLOCKS, 8 * tc, 256), BF16)
    return pl.pallas_call(
        body, grid=(nc,),
        in_specs=[rev, vec, rev, rev, vec, pl.BlockSpec((1, S5_BLOCKS, 8, 256), lambda i: (nc - 1 - i, 0, 0, 0)),
                  mat, mat, mat, lamspec, lamspec],
        out_specs=[rev, vec, mat, mat, lamspec, lamspec, vec],
        out_shape=[_sds((n_rows, D_MODEL), F32), _sds((1, D_MODEL), F32), _sds((S5_BLOCKS, 256, 256), F32),
                   _sds((S5_BLOCKS, 256, 256), F32), _sds((S5_BLOCKS, 8, 128), F32), _sds((S5_BLOCKS, 8, 128), F32),
                   _sds((1, D_MODEL), F32)],
        scratch_shapes=[pltpu.VMEM((2, 8 * tc, 128), F32), pltpu.VMEM((tc, D_MODEL), F32), bigb, bigb,
                        pltpu.VMEM((S5_BLOCKS, 8 * tc + 8, 256), F32), big,
                        pltpu.VMEM((S5_BLOCKS, 8, 256), F32)],
        name="s5_bwd", compiler_params=_params(("arbitrary",)))(
            x, gain, dy2, res, d_skip, cs, rb, rbt, rct, lam_r, lam_i)


def _s5_views(a_re, a_im, log_dt, b_re, b_im):
    return a_re[:, None, :], a_im[:, None, :], log_dt[:, None, None], jnp.swapaxes(b_re, 1, 2), jnp.swapaxes(b_im, 1, 2)


def _s5_factors(a_re, a_im, log_dt):
    lr, li, dt = jnp.minimum(a_re, LAMBDA_RE_MAX), a_im, jnp.exp(log_dt)
    mag, ang = jnp.exp(lr * dt), li * dt
    lbr, lbi = mag * jnp.cos(ang), mag * jnp.sin(ang)
    den = lr * lr + li * li
    fr, fi = ((lbr - 1.0) * lr + lbi * li) / den, (lbi * lr - (lbr - 1.0) * li) / den
    return lr, li, dt, lbr, lbi, fr, fi, den


def s5_prep(a_re, a_im, log_dt, b_re, b_im, c_re, c_im):
    def body(ar_ref, ai_ref, t_ref, br_ref, bi_ref, cr_ref, ci_ref, rb_ref, rbt_ref, rc_ref, rct_ref, lr_ref, li_ref):
        _, _, _, lbr, lbi, fr, fi, _ = _s5_factors(ar_ref[...], ai_ref[...], t_ref[...])
        lr_ref[...] = lbr
        li_ref[...] = lbi
        bre = fr * br_ref[...] - fi * bi_ref[...]
        bim = fr * bi_ref[...] + fi * br_ref[...]
        even = (lax.broadcasted_iota(jnp.int32, (256, S5_STATE), 0) // S5_GROUP) % 2 == 0

        def assemble(re, im):
            re, im = re.reshape(256, S5_STATE), im.reshape(256, S5_STATE)
            return jnp.concatenate([jnp.where(even, re, 0.0), jnp.where(even, 0.0, re), jnp.where(even, im, 0.0),
                                    jnp.where(even, 0.0, im)], axis=1)

        for blk in range(S5_BLOCKS):
            sl = slice(16 * blk, 16 * blk + 16)
            rb = assemble(bre[sl], bim[sl])
            rct = assemble(cr_ref[sl], -ci_ref[sl])
            rb_ref[blk] = rb.astype(BF16)
            rbt_ref[blk] = rb.T.astype(BF16)
            rct_ref[blk] = rct.astype(BF16)
            rc_ref[blk] = rct.T.astype(BF16)

    vm = pl.BlockSpec(memory_space=pltpu.VMEM)
    mat = _sds((S5_BLOCKS, 256, 256), BF16)
    lam = _sds((S5_GROUPS, 1, S5_STATE), F32)
    rb, rbt, rc, rct, lam_r, lam_i = pl.pallas_call(
        body, in_specs=[vm] * 7, out_specs=[vm] * 6, out_shape=[mat, mat, mat, mat, lam, lam], name="s5_prep",
        compiler_params=_params())(*_s5_views(a_re, a_im, log_dt, b_re, b_im), c_re, c_im)
    return rb, rbt, rc, rct, lam_r.reshape(S5_BLOCKS, 8, 128), lam_i.reshape(S5_BLOCKS, 8, 128)


def s5_param_bwd(mats, lams, a_re, a_im, log_dt, b_re, b_im):
    def body(m_ref, glr_ref, gli_ref, ar_ref, ai_ref, t_ref, br_ref, bi_ref,
             dar_ref, dai_ref, dt_ref, dbr_ref, dbi_ref, dcr_ref, dci_ref):
        lr, li, dt, lbr, lbi, fr, fi, den = _s5_factors(ar_ref[...], ai_ref[...], t_ref[...])
        shape = (S5_GROUPS, S5_GROUP, S5_STATE)
        gbr, gbi = m_ref[0:1024, 0:64].reshape(shape), m_ref[0:1024, 64:128].reshape(shape)
        dcr_ref[...] = m_ref[1024:2048, 0:64].reshape(shape)
        dci_ref[...] = -m_ref[1024:2048, 64:128].reshape(shape)
        br, bi = br_ref[...], bi_ref[...]
        dbr_ref[...] = fr * gbr + fi * gbi
        dbi_ref[...] = fr * gbi - fi * gbr
        dfr = jnp.sum(gbr * br + gbi * bi, axis=1, keepdims=True)
        dfi = jnp.sum(gbi * br - gbr * bi, axis=1, keepdims=True)
        nr, ni = (dfr * lr - dfi * li) / den, (dfr * li + dfi * lr) / den
        qr, qi = (fr * lr + fi * li) / den, (fi * lr - fr * li) / den
        lam_r, lam_i = -(dfr * qr + dfi * qi), -(dfi * qr - dfr * qi)
        gr, gi = glr_ref[...] + nr, gli_ref[...] + ni
        zr, zi = gr * lbr + gi * lbi, gi * lbr - gr * lbi
        a = ar_ref[...]
        dar_ref[...] = (lam_r + zr * dt) * jnp.where(a < LAMBDA_RE_MAX, 1.0, jnp.where(a == LAMBDA_RE_MAX, 0.5, 0.0))
        dai_ref[...] = lam_i + zi * dt
        dt_ref[...] = jnp.sum(zr * lr + zi * li, axis=2, keepdims=True) * dt

    vm = pl.BlockSpec(memory_space=pltpu.VMEM)
    state = _sds((S5_GROUPS, 1, S5_STATE), F32)
    wide = _sds((S5_GROUPS, S5_GROUP, S5_STATE), F32)
    glr = lams[0:32].reshape(S5_GROUPS, 1, S5_STATE)
    gli = lams[32:64].reshape(S5_GROUPS, 1, S5_STATE)
    dar, dai, ddt, dbr, dbi, dcr, dci = pl.pallas_call(
        body, in_specs=[vm] * 8, out_specs=[vm] * 7,
        out_shape=[state, state, _sds((S5_GROUPS, 1, 1), F32), wide, wide, wide, wide], name="s5_param_bwd",
        compiler_params=_params())(mats, glr, gli, *_s5_views(a_re, a_im, log_dt, b_re, b_im))
    return (dar.reshape(S5_GROUPS, S5_STATE), dai.reshape(S5_GROUPS, S5_STATE), ddt.reshape(S5_GROUPS),
            jnp.swapaxes(dbr, 1, 2), jnp.swapaxes(dbi, 1, 2), dcr, dci)


def s5_compact(drb, drct, dlr, dli):
    def body(drb_ref, drct_ref, dlr_ref, dli_ref, o_ref, lam_ref):
        even = (lax.broadcasted_iota(jnp.int32, (256, 64), 0) // S5_GROUP) % 2 == 0
        for blk in range(S5_BLOCKS):
            for k, ref in enumerate((drb_ref, drct_ref)):
                m = ref[blk]
                re = jnp.where(even, m[:, 0:64], m[:, 64:128])
                im = jnp.where(even, m[:, 128:192], m[:, 192:256])
                o_ref[pl.ds(k * 1024 + blk * 256, 256), :] = jnp.concatenate([re, im], axis=1)
            lam_ref[pl.ds(blk * 8, 8), :] = dlr_ref[blk]
            lam_ref[pl.ds(32 + blk * 8, 8), :] = dli_ref[blk]

    vm = pl.BlockSpec(memory_space=pltpu.VMEM)
    return pl.pallas_call(body, in_specs=[vm] * 4, out_specs=[vm, vm], out_shape=[_sds((2048, 128), F32), _sds((64, 128), F32)],
                          name="s5_compact", compiler_params=_params())(drb, drct, dlr, dli)


NEG = -1e30


GROUP = N_Q // N_KV


def _attn_masks(n):
    qi = lax.broadcasted_iota(jnp.int32, (GROUP * BLOCK, BLOCK), 0) % BLOCK
    kj = lax.broadcasted_iota(jnp.int32, (GROUP * BLOCK, BLOCK), 1)
    return jnp.logical_and(kj > qi, n > 0), kj <= qi


def _stack_heads(ref, kh):
    return jnp.concatenate([ref[:, (GROUP * kh + g) * HEAD_DIM:(GROUP * kh + g + 1) * HEAD_DIM] for g in range(GROUP)], axis=0)


def _unstack_heads(val):
    return jnp.concatenate([val[g * BLOCK:(g + 1) * BLOCK] for g in range(GROUP)], axis=1)


def _sink_column(sink_ref, kh):
    grp = lax.broadcasted_iota(jnp.int32, (GROUP * BLOCK, 1), 0) // BLOCK
    col = jnp.zeros((GROUP * BLOCK, 1), F32)
    for g in range(GROUP):
        col = jnp.where(grp == g, sink_ref[GROUP * kh + g], col)
    return col, grp


def _attn_terms(q4, kp, kc, sink, mask_p, mask_c):
    scale = 1.0 / math.sqrt(HEAD_DIM)
    nt = (((1,), (1,)), ((), ()))
    sp = jnp.where(mask_p, lax.dot_general(q4, kp, nt, preferred_element_type=F32) * scale, NEG)
    sc = jnp.where(mask_c, lax.dot_general(q4, kc, nt, preferred_element_type=F32) * scale, NEG)
    m = jnp.maximum(jnp.maximum(jnp.max(sp, axis=-1, keepdims=True), jnp.max(sc, axis=-1, keepdims=True)), sink)
    return jnp.exp(sp - m), jnp.exp(sc - m), jnp.exp(sink - m)


def _attn_exp(q4, kp, kc, sink, mask_p, mask_c):
    pp, pc, ps = _attn_terms(q4, kp, kc, sink, mask_p, mask_c)
    inv = 1.0 / (jnp.sum(pp, axis=-1, keepdims=True) + jnp.sum(pc, axis=-1, keepdims=True) + ps)
    return pp, pc, ps, inv


def attn_fwd(q, kv, sinks):
    n_rows = q.shape[0]
    nb = n_rows // BLOCK

    def body(sink_ref, q_ref, kvp_ref, kvc_ref, o_ref):
        n = pl.program_id(0)
        mask_p, mask_c = _attn_masks(n)
        outs = []
        for kh in range(N_KV):
            ks, vs = slice(kh * HEAD_DIM, (kh + 1) * HEAD_DIM), slice((N_KV + kh) * HEAD_DIM, (N_KV + kh + 1) * HEAD_DIM)
            sink, _ = _sink_column(sink_ref, kh)
            pp, pc, ps = _attn_terms(_stack_heads(q_ref, kh), kvp_ref[:, ks], kvc_ref[:, ks], sink, mask_p, mask_c)
            ones = jnp.ones((BLOCK, HEAD_DIM), BF16)
            o4 = (jnp.dot(pp.astype(BF16), jnp.concatenate([kvp_ref[:, vs], ones], axis=1), preferred_element_type=F32)
                  + jnp.dot(pc.astype(BF16), jnp.concatenate([kvc_ref[:, vs], ones], axis=1), preferred_element_type=F32))
            outs.append(_unstack_heads(o4[:, 0:HEAD_DIM] / (o4[:, HEAD_DIM:HEAD_DIM + 1] + ps)))
        o_ref[...] = jnp.concatenate(outs, axis=1).astype(BF16)

    kvw = 2 * N_KV * HEAD_DIM
    return pl.pallas_call(
        body, grid=(nb,),
        in_specs=[pl.BlockSpec(memory_space=pltpu.SMEM), pl.BlockSpec((BLOCK, D_MODEL), lambda n: (n, 0)),
                  pl.BlockSpec((BLOCK, kvw), lambda n: (jnp.maximum(n - 1, 0), 0)), pl.BlockSpec((BLOCK, kvw), lambda n: (n, 0))],
        out_specs=pl.BlockSpec((BLOCK, D_MODEL), lambda n: (n, 0)), out_shape=_sds((n_rows, D_MODEL), BF16),
        name="attn_fwd", compiler_params=_params(("parallel",)))(sinks, q, kv, kv)


def attn_bwd(q, kv, do, sinks):
    n_rows = q.shape[0]
    nb = n_rows // BLOCK
    kvw = 2 * N_KV * HEAD_DIM
    tn = (((0,), (0,)), ((), ()))
    nt = (((1,), (1,)), ((), ()))
    scale = 1.0 / math.sqrt(HEAD_DIM)

    def body(sink_ref, q_ref, kvp_ref, kvc_ref, do_ref, dq_ref, dbq_ref, dprev_ref, dcur_ref, dsink_ref):
        n = pl.program_id(0)
        mask_p, mask_c = _attn_masks(n)
        lane = lax.broadcasted_iota(jnp.int32, (1, D_MODEL), 1)
        dqs, dsink = [], jnp.zeros((1, D_MODEL), F32)
        dkp, dkc, dvp, dvc = [], [], [], []
        for kh in range(N_KV):
            ks, vs = slice(kh * HEAD_DIM, (kh + 1) * HEAD_DIM), slice((N_KV + kh) * HEAD_DIM, (N_KV + kh + 1) * HEAD_DIM)
            q4, do4 = _stack_heads(q_ref, kh), _stack_heads(do_ref, kh)
            kp, kc, vp, vc = kvp_ref[:, ks], kvc_ref[:, ks], kvp_ref[:, vs], kvc_ref[:, vs]
            sink, grp = _sink_column(sink_ref, kh)
            pp, pc, ps, inv = _attn_exp(q4, kp, kc, sink, mask_p, mask_c)
            pp, pc = pp * inv, pc * inv
            dpp = lax.dot_general(do4, vp, nt, preferred_element_type=F32)
            dpc = lax.dot_general(do4, vc, nt, preferred_element_type=F32)
            delta = jnp.sum(pp * dpp, axis=-1, keepdims=True) + jnp.sum(pc * dpc, axis=-1, keepdims=True)
            dsp = (pp * (dpp - delta) * scale).astype(BF16)
            dsc = (pc * (dpc - delta) * scale).astype(BF16)
            dsk = ps * inv * delta
            for g in range(GROUP):
                dsink = dsink + jnp.where(lane == GROUP * kh + g, -jnp.sum(jnp.where(grp == g, dsk, 0.0)), 0.0)
            dqs.append(_unstack_heads(jnp.dot(dsp, kp, preferred_element_type=F32)
                                      + jnp.dot(dsc, kc, preferred_element_type=F32)))
            dkp.append(lax.dot_general(dsp, q4, tn, preferred_element_type=F32))
            dkc.append(lax.dot_general(dsc, q4, tn, preferred_element_type=F32))
            dvp.append(lax.dot_general(pp.astype(BF16), do4, tn, preferred_element_type=F32))
            dvc.append(lax.dot_general(pc.astype(BF16), do4, tn, preferred_element_type=F32))
        dq = jnp.concatenate(dqs, axis=1)
        dq_ref[...] = dq.astype(BF16)
        dprev_ref[0] = jnp.concatenate(dkp + dvp, axis=1)
        dcur_ref[0] = jnp.concatenate(dkc + dvc, axis=1)

        @pl.when(n == 0)
        def _():
            dbq_ref[...] = jnp.zeros_like(dbq_ref)
            dsink_ref[...] = jnp.zeros_like(dsink_ref)

        dbq_ref[...] += jnp.sum(dq, axis=0, keepdims=True)
        dsink_ref[...] += dsink

    blk = pl.BlockSpec((BLOCK, D_MODEL), lambda n: (n, 0))
    part = pl.BlockSpec((1, BLOCK, kvw), lambda n: (n, 0, 0))
    return pl.pallas_call(
        body, grid=(nb,),
        in_specs=[pl.BlockSpec(memory_space=pltpu.SMEM), blk,
                  pl.BlockSpec((BLOCK, kvw), lambda n: (jnp.maximum(n - 1, 0), 0)), pl.BlockSpec((BLOCK, kvw), lambda n: (n, 0)), blk],
        out_specs=[blk, pl.BlockSpec((1, D_MODEL), lambda n: (0, 0)), part, part, pl.BlockSpec((1, D_MODEL), lambda n: (0, 0))],
        out_shape=[_sds((n_rows, D_MODEL), BF16), _sds((1, D_MODEL), F32), _sds((nb, BLOCK, kvw), F32),
                   _sds((nb, BLOCK, kvw), F32), _sds((1, D_MODEL), F32)],
        name="attn_bwd", compiler_params=_params(("arbitrary",)))(sinks, q, kv, kv, do)


def kv_combine(dprev, dcur):
    nb, _, kvw = dprev.shape

    def body(dcur_ref, dprev_ref, dkv_ref, db_ref):
        total = jnp.zeros((1, kvw), F32)
        for m in range(nb):
            dkv = dcur_ref[m] + dprev_ref[m + 1] if m + 1 < nb else dcur_ref[m]
            dkv_ref[m * BLOCK:(m + 1) * BLOCK, :] = dkv.astype(BF16)
            total = total + jnp.sum(dkv, axis=0, keepdims=True)
        db_ref[...] = jnp.concatenate([total, jnp.zeros((1, D_MODEL - kvw), F32)], axis=1)

    vm = pl.BlockSpec(memory_space=pltpu.VMEM)
    return pl.pallas_call(body, in_specs=[vm, vm], out_specs=[vm, vm],
                          out_shape=[_sds((nb * BLOCK, kvw), BF16), _sds((1, D_MODEL), F32)], name="kv_combine",
                          compiler_params=_params())(dcur, dprev)


def glu_bwd(dout, val, gate, tm=256):
    n_rows, d = dout.shape

    def body(do_ref, v_ref, g_ref, dz_ref, db_ref):
        i = pl.program_id(0)
        sg = jax.nn.sigmoid(g_ref[...])
        dval = do_ref[...] * sg
        dgate = do_ref[...] * v_ref[...] * sg * (1.0 - sg)
        dz_ref[...] = jnp.concatenate([dval, dgate], axis=1).astype(BF16)

        @pl.when(i == 0)
        def _():
            db_ref[...] = jnp.zeros_like(db_ref)

        db_ref[0:1, :] += jnp.sum(dval, axis=0, keepdims=True)
        db_ref[1:2, :] += jnp.sum(dgate, axis=0, keepdims=True)

    row = pl.BlockSpec((tm, d), lambda i: (i, 0))
    return pl.pallas_call(
        body, grid=(n_rows // tm,), in_specs=[row, row, row],
        out_specs=[pl.BlockSpec((tm, 2 * d), lambda i: (i, 0)), pl.BlockSpec((2, d), lambda i: (0, 0))],
        out_shape=[_sds((n_rows, 2 * d), BF16), _sds((2, d), F32)],
        name="glu_bwd", compiler_params=_params(("arbitrary",)))(dout, val, gate)


def _adam_update(w, g, m, v):
    nm = ADAM_B1 * m + (1.0 - ADAM_B1) * g
    nv = ADAM_B2 * v + (1.0 - ADAM_B2) * (g * g)
    m_hat = nm / (1.0 - ADAM_B1 ** ADAM_STEP)
    v_hat = nv / (1.0 - ADAM_B2 ** ADAM_STEP)
    return -ADAM_LR * (m_hat / (jnp.sqrt(v_hat) + ADAM_EPS) + ADAM_WD * w), nm, nv


def adamw(name, ws, gs, ms, vs, steps=8):
    n = len(ws)

    def body(*refs):
        for k in range(n):
            w_ref, g_ref, m_ref, v_ref = (refs[j * n + k] for j in range(4))
            go_ref, d_ref, nm_ref, nv_ref = (refs[(4 + j) * n + k] for j in range(4))
            gv = g_ref[...]
            go_ref[...] = gv
            d_ref[...], nm_ref[...], nv_ref[...] = _adam_update(w_ref[...], gv, m_ref[...], v_ref[...])

    specs = [pl.BlockSpec((w.shape[0] // steps, w.shape[1]), lambda i: (i, 0)) for w in ws]
    shapes = [_sds(w.shape, F32) for w in ws]
    out = pl.pallas_call(
        body, grid=(steps,), in_specs=specs * 4, out_specs=specs * 4, out_shape=shapes * 4, name=name,
        compiler_params=_params(("parallel",)))(*ws, *gs, *ms, *vs)
    return [list(out[j * n:(j + 1) * n]) for j in range(4)]


def adamw_native(name, ws, gs, ms, vs):
    n = len(ws)

    def body(*refs):
        w_refs, g_refs, m_refs, v_refs = refs[:n], refs[n:2 * n], refs[2 * n:3 * n], refs[3 * n:4 * n]
        d_refs, nm_refs, nv_refs = refs[4 * n:5 * n], refs[5 * n:6 * n], refs[6 * n:7 * n]
        for k in range(n):
            dl, nm, nv = _adam_update(w_refs[k][...], g_refs[k][...], m_refs[k][...], v_refs[k][...])
            d_refs[k][...] = dl
            nm_refs[k][...] = nm
            nv_refs[k][...] = nv

    vm = pl.BlockSpec(memory_space=pltpu.VMEM)
    shapes = [_sds(w.shape, F32) for w in ws]
    out = pl.pallas_call(body, in_specs=[vm] * (4 * n), out_specs=[vm] * (3 * n), out_shape=shapes * 3, name=name,
                         compiler_params=_params())(*ws, *gs, *ms, *vs)
    return list(out[:n]), list(out[n:2 * n]), list(out[2 * n:])


VEC_ROWS = {"norm_mix": 0, "norm_mlp": 2, "norm_kv": 4, "norm_final": 5, "s5_d": 6, "b_q": 7, "b_o": 8, "s5_b_glu": 9,
            "b_kv": 11, "sinks": 12, "loss": 13}


def split_vectors(where, vecs, d_shard, glu_shard):
    kvw = 2 * N_KV * HEAD_DIM
    shapes = {"norm_mix": (2, D_MODEL), "norm_mlp": (2, D_MODEL), "norm_kv": (1, D_MODEL), "norm_final": (1, D_MODEL),
              "s5_d": (1, d_shard), "b_q": (1, D_MODEL), "b_o": (1, D_MODEL), "s5_b_glu": (1, glu_shard), "b_kv": (1, kvw),
              "sinks": (1, N_Q), "loss": (1, 128)}
    names = list(shapes)

    def body(where_ref, v_ref, *o_refs):
        chip = where_ref[1]
        for name, o_ref in zip(names, o_refs):
            r0, (r, n) = VEC_ROWS[name], shapes[name]
            if name == "s5_d":
                g = jnp.zeros((1, n), F32)
                for j in range(4):
                    g = jnp.where(chip == j, v_ref[r0:r0 + 1, j * n:(j + 1) * n], g)
            elif name == "s5_b_glu":
                g = jnp.zeros((1, n), F32)
                for j in range(4):
                    row, col = r0 + (j * n) // D_MODEL, (j * n) % D_MODEL
                    g = jnp.where(chip == j, v_ref[row:row + 1, col:col + n], g)
            else:
                g = v_ref[r0:r0 + r, 0:n]
            o_ref[...] = g

    vm = pl.BlockSpec(memory_space=pltpu.VMEM)
    out = pl.pallas_call(body, in_specs=[pl.BlockSpec(memory_space=pltpu.SMEM), vm], out_specs=[vm] * len(names),
                         out_shape=[_sds(shapes[n], F32) for n in names], name="split_vectors",
                         compiler_params=_params())(where, vecs)
    return dict(zip(names, out))


def _position():
    x, y, c = lax.axis_index("x"), lax.axis_index("y"), lax.axis_index("c")
    others = [(1 - x, y), (x, 1 - y), (1 - x, 1 - y)]
    return x, y, c, others


def _window(ref, kind, chip, half, shard_shape):
    if kind == "slab":
        return ref.at[chip]
    r, n = shard_shape
    if kind == "col":
        return ref.at[pl.ds(pl.multiple_of(half * (r // 2), 16), r // 2), pl.ds(pl.multiple_of(chip * n, 128), n)]
    return ref.at[pl.ds(pl.multiple_of(chip * r, 16), r), pl.ds(pl.multiple_of(half * (n // 2), 128), n // 2)]


def _half(ref, kind, half, shape):
    r, n = shape
    if kind == "col":
        return ref.at[pl.ds(pl.multiple_of(half * (r // 2), 16), r // 2), :]
    return ref.at[:, pl.ds(pl.multiple_of(half * (n // 2), 128), n // 2)]


def swap_start(name, grads, kinds, carry):
    nt = len(grads)
    shapes = [tuple(g.shape) for g in grads]
    lands = [lax.empty(sh, BF16) for sh in shapes]
    given, given_specs, token_type, write = _hand_through(carry)
    n_in = 2 * nt + len(given)

    def body(*refs):
        in_refs, land_refs = refs[:nt], refs[nt:2 * nt]
        send_sems, recv_sems, token = refs[n_in], refs[n_in + 1], refs[-1]
        x, y, c, _ = _position()
        for t in range(nt):
            pltpu.make_async_remote_copy(
                src_ref=_half(in_refs[t], kinds[t], 1 - c, shapes[t]), dst_ref=_half(land_refs[t], kinds[t], 1 - c, shapes[t]),
                send_sem=send_sems.at[t], recv_sem=recv_sems.at[t], device_id=(x, y, 1 - c), device_id_type=MESH).start()
        write(token, refs[:n_in])

    sems = pltpu.SemaphoreType.DMA((nt,))
    both = list(grads) + lands
    out = pl.pallas_call(
        body, name=name, in_specs=[HBM_SPEC] * (2 * nt) + given_specs,
        out_specs=(SEM_SPEC, SEM_SPEC, *[HBM_SPEC] * (2 * nt), pl.BlockSpec(memory_space=pltpu.VMEM)),
        out_shape=(sems, sems, *[pltpu.HBM(a.shape, a.dtype) for a in both], token_type),
        input_output_aliases={t: 2 + t for t in range(2 * nt)}, compiler_params=_split_params(),
    )(*[_in_hbm(a) for a in both], *given)
    return out[0], out[1], list(out[2:2 + nt]), list(out[2 + nt:2 + 2 * nt]), out[-1]


def swap_wait(name, send_sems, recv_sems, grads, lands, kinds, after):
    nt = len(grads)
    shapes = [tuple(g.shape) for g in grads]

    def body(*refs):
        in_refs, land_refs = refs[:nt], refs[nt:2 * nt]
        send_ref, recv_ref = refs[2 * nt], refs[2 * nt + 1]
        x, y, c, _ = _position()
        for t in range(nt):
            cp = pltpu.make_async_remote_copy(
                src_ref=_half(in_refs[t], kinds[t], 1 - c, shapes[t]), dst_ref=_half(land_refs[t], kinds[t], c, shapes[t]),
                send_sem=send_ref.at[t], recv_sem=recv_ref.at[t], device_id=(x, y, 1 - c), device_id_type=MESH)
            cp.wait_send()
            cp.wait_recv()

    both = list(grads) + list(lands)
    out = pl.pallas_call(
        body, name=name, in_specs=[HBM_SPEC] * (2 * nt) + [SEM_SPEC, SEM_SPEC, HBM_SPEC], out_specs=[HBM_SPEC] * (2 * nt),
        out_shape=[pltpu.HBM(a.shape, a.dtype) for a in both], input_output_aliases={t: t for t in range(2 * nt)},
        compiler_params=_split_params())(*both, send_sems, recv_sems, _in_hbm(after))
    return list(out[:nt]), list(out[nt:])


def _half_spec(kind, shape, tiles):
    r, n = shape
    if kind == "col":
        tn = n // tiles
        return pl.BlockSpec((r // 2, tn), lambda i, s: (s[0], i))
    tm = r // tiles
    return pl.BlockSpec((tm, n // 2), lambda i, s: (i, s[0]))


def add_halves(name, mine, landed, kinds, where, tiles=2):
    nt = len(mine)
    shapes = [tuple(a.shape) for a in mine]

    def compact(t):
        r, n = shapes[t]
        if kinds[t] == "col":
            return (r // 2, n), pl.BlockSpec((r // 2, n // tiles), lambda i, s: (0, i))
        return (r, n // 2), pl.BlockSpec((r // tiles, n // 2), lambda i, s: (i, 0))

    def body(s_ref, *refs):
        for a_ref, b_ref, o_ref in zip(refs[:nt], refs[nt:2 * nt], refs[2 * nt:]):
            o_ref[...] = (a_ref[...].astype(F32) + b_ref[...].astype(F32)).astype(BF16)

    specs = [_half_spec(kinds[t], shapes[t], tiles) for t in range(nt)]
    return pl.pallas_call(
        body, grid_spec=pltpu.PrefetchScalarGridSpec(num_scalar_prefetch=1, grid=(tiles,), in_specs=specs + specs,
                                                     out_specs=[compact(t)[1] for t in range(nt)]),
        out_shape=[_sds(compact(t)[0], BF16) for t in range(nt)], name=name,
        compiler_params=_params(("parallel",)))(where, *mine, *landed)


def sum_shards(name, parts, landed, kinds, shard_shapes, where, layers, n_layers, intos, tiles=2):
    nt = len(parts)
    in_specs, out_specs = [], []
    for t in range(nt):
        (r, n), layer = shard_shapes[t], layers[t]
        if kinds[t] == "col":
            tm, width = r // 2 // tiles, n
            own = pl.BlockSpec((tm, n), lambda i, s: (i, s[1]))
            out = pl.BlockSpec((None, tm, n), lambda i, s, layer=layer: (layer, s[0] * tiles + i, 0))
        else:
            tm, width = r // tiles, n // 2
            own = pl.BlockSpec((tm, n // 2), lambda i, s: (s[1] * tiles + i, 0))
            out = pl.BlockSpec((None, tm, n // 2), lambda i, s, layer=layer: (layer, i, s[0]))
        in_specs += [own, pl.BlockSpec((3, tm, width), lambda i, s: (0, i, 0))]
        out_specs.append(out)
    args, aliases = [where] + [a for pair in zip(parts, landed) for a in pair], {}
    for t in range(nt):
        if intos[t] is not None:
            aliases[len(args)] = t
            in_specs.append(pl.BlockSpec(memory_space=pl.ANY))
            args.append(intos[t])

    def body(s_ref, *refs):
        for t in range(nt):
            a_ref, l_ref, o_ref = refs[2 * t], refs[2 * t + 1], refs[len(in_specs) + t]
            o_ref[...] = ((a_ref[...].astype(F32) + l_ref[0].astype(F32)) + l_ref[1].astype(F32)) + l_ref[2].astype(F32)

    return pl.pallas_call(
        body, grid_spec=pltpu.PrefetchScalarGridSpec(num_scalar_prefetch=1, grid=(tiles,), in_specs=in_specs,
                                                     out_specs=out_specs),
        out_shape=[_sds((n_layers[t],) + tuple(shard_shapes[t]), F32) for t in range(nt)], input_output_aliases=aliases,
        name=name, compiler_params=_params(("parallel",)))(*args)


def share_start(arrays, entries, carry):
    na, nt = len(arrays), len(entries)
    given, given_specs, token_type, write = _hand_through(carry)
    n_in = na + len(given)

    def body(*refs):
        in_refs, send_sems, recv_sems, token = refs[:na], refs[n_in], refs[n_in + 1], refs[-1]
        x, y, c, _ = _position()
        for t, (a, layer, kind) in enumerate(entries):
            mine = _half(in_refs[a].at[layer], kind, c, tuple(arrays[a].shape[1:]))
            pltpu.make_async_remote_copy(
                src_ref=mine, dst_ref=mine, send_sem=send_sems.at[t], recv_sem=recv_sems.at[t],
                device_id=(x, y, 1 - c), device_id_type=MESH).start()
        write(token, refs[:n_in])

    sems = pltpu.SemaphoreType.DMA((nt,))
    out = pl.pallas_call(
        body, name="share_start", in_specs=[HBM_SPEC] * na + given_specs,
        out_specs=(SEM_SPEC, SEM_SPEC, *[HBM_SPEC] * na, pl.BlockSpec(memory_space=pltpu.VMEM)),
        out_shape=(sems, sems, *[pltpu.HBM(a.shape, a.dtype) for a in arrays], token_type),
        input_output_aliases={t: 2 + t for t in range(na)}, compiler_params=_split_params(),
    )(*[_in_hbm(a) for a in arrays], *given)
    return out[0], out[1], list(out[2:2 + na]), out[-1]


def share_wait(send_sems, recv_sems, arrays, entries, after):
    na = len(arrays)

    def body(*refs):
        in_refs, send_ref, recv_ref = refs[:na], refs[na], refs[na + 1]
        x, y, c, _ = _position()
        for t, (a, layer, kind) in enumerate(entries):
            shape = tuple(arrays[a].shape[1:])
            cp = pltpu.make_async_remote_copy(
                src_ref=_half(in_refs[a].at[layer], kind, c, shape), dst_ref=_half(in_refs[a].at[layer], kind, 1 - c, shape),
                send_sem=send_ref.at[t], recv_sem=recv_ref.at[t], device_id=(x, y, 1 - c), device_id_type=MESH)
            cp.wait_send()
            cp.wait_recv()

    return list(pl.pallas_call(
        body, name="share_wait", in_specs=[HBM_SPEC] * na + [SEM_SPEC, SEM_SPEC, HBM_SPEC], out_specs=[HBM_SPEC] * na,
        out_shape=[pltpu.HBM(a.shape, a.dtype) for a in arrays], input_output_aliases={t: t for t in range(na)},
        compiler_params=_split_params())(*arrays, send_sems, recv_sems, _in_hbm(after)))


HBM_SPEC = pl.BlockSpec(memory_space=pltpu.HBM)
SEM_SPEC = pl.BlockSpec(memory_space=pltpu.SEMAPHORE)
ANY_SPEC = pl.BlockSpec(memory_space=pl.ANY)


def _split_params():
    return pltpu.CompilerParams(has_side_effects=pltpu.SideEffectType.DATAFLOW_SIDE_EFFECTING,
                                vmem_limit_bytes=VMEM_LIMIT_BYTES)


def _in_hbm(a):
    return pltpu.with_memory_space_constraint(a, pltpu.HBM)


def cast_place(arrays, entries, where, tiles=2):
    in_specs, out_specs, fulls = [], [], []
    for a, layer, kind in entries:
        _, r, n = arrays[a].shape
        tm = r // tiles
        in_specs.append(pl.BlockSpec((None, tm, n), lambda i, s, layer=layer: (layer, i, 0)))
        if kind == "col":
            fulls.append((r, 4 * n))
            out_specs.append(pl.BlockSpec((tm, n), lambda i, s: (i, s[1])))
        else:
            fulls.append((4 * r, n))
            out_specs.append(pl.BlockSpec((tm, n), lambda i, s: (s[1] * tiles + i, 0)))
    nt = len(entries)

    def body(s_ref, *refs):
        for w_ref, o_ref in zip(refs[:nt], refs[nt:]):
            o_ref[...] = w_ref[...].astype(BF16)

    return pl.pallas_call(
        body, grid_spec=pltpu.PrefetchScalarGridSpec(num_scalar_prefetch=1, grid=(tiles,), in_specs=in_specs,
                                                     out_specs=out_specs),
        out_shape=[_sds(f, BF16) for f in fulls], name="cast_place",
        compiler_params=_params(("parallel",)))(where, *[arrays[a] for a, _, _ in entries])


def _hand_through(carry):
    given = [] if isinstance(carry, tuple) else [carry]

    def write(token, ins):
        token[...] = ins[-1][...] if given else jnp.zeros_like(token)

    return (given, [pl.BlockSpec(memory_space=pltpu.VMEM)] * len(given),
            _sds(carry if isinstance(carry, tuple) else carry.shape, F32), write)


def gather_start(fulls, kinds, shard_shapes, carry):
    nt = len(fulls)
    given, given_specs, token_type, write = _hand_through(carry)
    n_in = nt + len(given)

    def body(*refs):
        full_refs = refs[:nt]
        send_sems, recv_sems, token = refs[n_in], refs[n_in + 1], refs[-1]
        x, y, c, others = _position()
        for t in range(nt):
            mine = _window(full_refs[t], kinds[t], 2 * x + y, c, shard_shapes[t])
            for j, (ox, oy) in enumerate(others):
                pltpu.make_async_remote_copy(
                    src_ref=mine, dst_ref=mine, send_sem=send_sems.at[3 * t + j], recv_sem=recv_sems.at[3 * t + j],
                    device_id=(ox, oy, c), device_id_type=MESH).start()
        write(token, refs[:n_in])

    sems = pltpu.SemaphoreType.DMA((3 * nt,))
    out = pl.pallas_call(
        body, name="gather_start", in_specs=[HBM_SPEC] * nt + given_specs,
        out_specs=(SEM_SPEC, SEM_SPEC, *[HBM_SPEC] * nt, pl.BlockSpec(memory_space=pltpu.VMEM)),
        out_shape=(sems, sems, *[pltpu.HBM(f.shape, f.dtype) for f in fulls], token_type),
        input_output_aliases={t: 2 + t for t in range(nt)}, compiler_params=_split_params(),
    )(*[_in_hbm(f) for f in fulls], *given)
    return out[0], out[1], list(out[2:2 + nt]), out[-1]


def gather_wait(name, send_sems, recv_sems, fulls, kinds, shard_shapes, after, first):
    nt = len(fulls)
    extra = [] if after is None else [_in_hbm(after)]

    def body(*refs):
        full_refs, send_ref, recv_ref = refs[:nt], refs[nt], refs[nt + 1]
        x, y, c, others = _position()
        for t in range(nt):
            mine = _window(full_refs[t], kinds[t], 2 * x + y, c, shard_shapes[t])
            for j, (ox, oy) in enumerate(others):
                cp = pltpu.make_async_remote_copy(
                    src_ref=mine, dst_ref=_window(full_refs[t], kinds[t], 2 * ox + oy, c, shard_shapes[t]),
                    send_sem=send_ref.at[3 * (first + t) + j], recv_sem=recv_ref.at[3 * (first + t) + j],
                    device_id=(ox, oy, c), device_id_type=MESH)
                cp.wait_send()
                cp.wait_recv()

    out = pl.pallas_call(
        body, name=name, in_specs=[HBM_SPEC] * nt + [SEM_SPEC, SEM_SPEC] + [HBM_SPEC] * len(extra),
        out_specs=[HBM_SPEC] * nt, out_shape=[pltpu.HBM(f.shape, f.dtype) for f in fulls],
        input_output_aliases={t: t for t in range(nt)}, compiler_params=_split_params())(*fulls, send_sems, recv_sems, *extra)
    return list(out)


def forward_halves(name, fulls, kinds, shard_shapes):
    nt = len(fulls)

    def body(*refs):
        out_refs = refs[nt:2 * nt]
        send_sems, recv_sems = refs[2 * nt:]
        x, y, c, others = _position()
        cps = []
        for t in range(nt):
            for j, (ox, oy) in enumerate(others):
                landed = _window(out_refs[t], kinds[t], 2 * ox + oy, c, shard_shapes[t])
                cp = pltpu.make_async_remote_copy(
                    src_ref=landed, dst_ref=landed, send_sem=send_sems.at[3 * t + j], recv_sem=recv_sems.at[3 * t + j],
                    device_id=(x, y, 1 - c), device_id_type=MESH)
                cp.start()
                cps.append(cp)
        for t in range(nt):
            for j, (ox, oy) in enumerate(others):
                got = _window(out_refs[t], kinds[t], 2 * ox + oy, 1 - c, shard_shapes[t])
                pltpu.make_async_remote_copy(
                    src_ref=got, dst_ref=got, send_sem=send_sems.at[3 * t + j], recv_sem=recv_sems.at[3 * t + j],
                    device_id=(x, y, 1 - c), device_id_type=MESH).wait_recv()
        for cp in cps:
            cp.wait_send()

    out = pl.pallas_call(
        body, in_specs=[ANY_SPEC] * nt, out_specs=[ANY_SPEC] * nt, out_shape=[_sds(f.shape, f.dtype) for f in fulls],
        input_output_aliases={t: t for t in range(nt)},
        scratch_shapes=[pltpu.SemaphoreType.DMA((3 * nt,)), pltpu.SemaphoreType.DMA((3 * nt,))],
        name=name, compiler_params=_params())(*fulls)
    return list(out)


def forward_start(name, send_sems, recv_sems, fulls, kinds, shard_shapes, after, first, carry, passing=()):
    nt, n_pass = len(fulls), len(passing)
    given, given_specs, token_type, write = _hand_through(carry)
    n_in = nt + 3 + n_pass + len(given)

    def body(*refs):
        full_refs, ici_send, ici_recv = refs[:nt], refs[nt], refs[nt + 1]
        send_ref, recv_ref, token = refs[n_in], refs[n_in + 1], refs[-1]
        x, y, c, others = _position()
        for t in range(nt):
            mine = _window(full_refs[t], kinds[t], 2 * x + y, c, shard_shapes[t])
            for j, (ox, oy) in enumerate(others):
                landed = _window(full_refs[t], kinds[t], 2 * ox + oy, c, shard_shapes[t])
                cp = pltpu.make_async_remote_copy(
                    src_ref=mine, dst_ref=landed, send_sem=ici_send.at[3 * (first + t) + j],
                    recv_sem=ici_recv.at[3 * (first + t) + j], device_id=(ox, oy, c), device_id_type=MESH)
                cp.wait_send()
                cp.wait_recv()
                pltpu.make_async_remote_copy(
                    src_ref=landed, dst_ref=landed, send_sem=send_ref.at[3 * t + j], recv_sem=recv_ref.at[3 * t + j],
                    device_id=(x, y, 1 - c), device_id_type=MESH).start()
        write(token, refs[:n_in])

    sems = pltpu.SemaphoreType.DMA((3 * nt,))
    out = pl.pallas_call(
        body, name=name, in_specs=[HBM_SPEC] * nt + [SEM_SPEC, SEM_SPEC, HBM_SPEC] + [HBM_SPEC] * n_pass + given_specs,
        out_specs=(SEM_SPEC, SEM_SPEC, *[HBM_SPEC] * (nt + n_pass), pl.BlockSpec(memory_space=pltpu.VMEM)),
        out_shape=(sems, sems, *[pltpu.HBM(f.shape, f.dtype) for f in [*fulls, *passing]], token_type),
        input_output_aliases={**{t: 2 + t for t in range(nt)}, **{nt + 3 + t: 2 + nt + t for t in range(n_pass)}},
        compiler_params=_split_params(),
    )(*fulls, send_sems, recv_sems, _in_hbm(after), *passing, *given)
    return out[0], out[1], list(out[2:2 + nt]), list(out[2 + nt:2 + nt + n_pass]), out[-1]


def forward_wait(name, send_sems, recv_sems, fulls, kinds, shard_shapes, after):
    nt = len(fulls)

    def body(*refs):
        full_refs, send_ref, recv_ref = refs[:nt], refs[nt], refs[nt + 1]
        x, y, c, others = _position()
        for t in range(nt):
            for j, (ox, oy) in enumerate(others):
                cp = pltpu.make_async_remote_copy(
                    src_ref=_window(full_refs[t], kinds[t], 2 * ox + oy, c, shard_shapes[t]),
                    dst_ref=_window(full_refs[t], kinds[t], 2 * ox + oy, 1 - c, shard_shapes[t]),
                    send_sem=send_ref.at[3 * t + j], recv_sem=recv_ref.at[3 * t + j],
                    device_id=(x, y, 1 - c), device_id_type=MESH)
                cp.wait_send()
                cp.wait_recv()

    return list(pl.pallas_call(
        body, name=name, in_specs=[HBM_SPEC] * nt + [SEM_SPEC, SEM_SPEC, HBM_SPEC], out_specs=[HBM_SPEC] * nt,
        out_shape=[pltpu.HBM(f.shape, f.dtype) for f in fulls], input_output_aliases={t: t for t in range(nt)},
        compiler_params=_split_params())(*fulls, send_sems, recv_sems, _in_hbm(after)))


def _piece(ref, kind, chip, shard_shape):
    r, n = shard_shape
    if kind == "col":
        return ref.at[:, pl.ds(pl.multiple_of(chip * n, 128), n)]
    return ref.at[pl.ds(pl.multiple_of(chip * r, 16), r), :]


def _piece_shape(kind, shard_shape):
    r, n = shard_shape
    return (r // 2, n) if kind == "col" else (r, n // 2)


def exchange_start(name, parts, kinds, shard_shapes, carry):
    nt = len(parts)
    lands = [lax.empty((3,) + _piece_shape(kinds[t], shard_shapes[t]), BF16) for t in range(nt)]
    given, given_specs, token_type, write = _hand_through(carry)
    n_in = 2 * nt + len(given)

    def body(*refs):
        part_refs, land_refs = refs[:nt], refs[nt:2 * nt]
        send_sems, recv_sems, token = refs[n_in], refs[n_in + 1], refs[-1]
        x, y, c, others = _position()
        for t in range(nt):
            for j, (ox, oy) in enumerate(others):
                pltpu.make_async_remote_copy(
                    src_ref=_piece(part_refs[t], kinds[t], 2 * ox + oy, shard_shapes[t]), dst_ref=land_refs[t].at[j],
                    send_sem=send_sems.at[3 * t + j], recv_sem=recv_sems.at[3 * t + j],
                    device_id=(ox, oy, c), device_id_type=MESH).start()
        write(token, refs[:n_in])

    sems = pltpu.SemaphoreType.DMA((3 * nt,))
    both = list(parts) + lands
    out = pl.pallas_call(
        body, name=name, in_specs=[HBM_SPEC] * (2 * nt) + given_specs,
        out_specs=(SEM_SPEC, SEM_SPEC, *[HBM_SPEC] * (2 * nt), pl.BlockSpec(memory_space=pltpu.VMEM)),
        out_shape=(sems, sems, *[pltpu.HBM(a.shape, a.dtype) for a in both], token_type),
        input_output_aliases={t: 2 + t for t in range(2 * nt)}, compiler_params=_split_params(),
    )(*[_in_hbm(a) for a in both], *given)
    return out[0], out[1], list(out[2:2 + nt]), list(out[2 + nt:2 + 2 * nt]), out[-1]


def exchange_wait(name, send_sems, recv_sems, parts, lands, kinds, shard_shapes, after):
    nt = len(parts)

    def body(*refs):
        part_refs, land_refs = refs[:nt], refs[nt:2 * nt]
        send_ref, recv_ref = refs[2 * nt], refs[2 * nt + 1]
        x, y, c, others = _position()
        for t in range(nt):
            for j, (ox, oy) in enumerate(others):
                cp = pltpu.make_async_remote_copy(
                    src_ref=_piece(part_refs[t], kinds[t], 2 * ox + oy, shard_shapes[t]), dst_ref=land_refs[t].at[j],
                    send_sem=send_ref.at[3 * t + j], recv_sem=recv_ref.at[3 * t + j],
                    device_id=(ox, oy, c), device_id_type=MESH)
                cp.wait_send()
                cp.wait_recv()

    both = list(parts) + list(lands)
    out = pl.pallas_call(
        body, name=name, in_specs=[HBM_SPEC] * (2 * nt) + [SEM_SPEC, SEM_SPEC, HBM_SPEC], out_specs=[HBM_SPEC] * (2 * nt),
        out_shape=[pltpu.HBM(a.shape, a.dtype) for a in both], input_output_aliases={t: t for t in range(2 * nt)},
        compiler_params=_split_params())(*both, send_sems, recv_sems, _in_hbm(after))
    return list(out[:nt]), list(out[nt:])


def reduce_swap(bufs, wire):
    n = len(bufs)
    halves = [b.shape[0] // 2 for b in bufs]

    def body(*refs):
        in_refs, out_refs, txs, got = refs[:n], refs[n:2 * n], refs[2 * n:3 * n], refs[3 * n:4 * n]
        send_sems, recv_sems = refs[4 * n:]
        x, y, c, _ = _position()
        cps = []
        for k in range(n):
            txs[k][...] = in_refs[k][pl.ds(pl.multiple_of((1 - c) * halves[k], 8), halves[k]), :].astype(wire[k])
            cp = pltpu.make_async_remote_copy(src_ref=txs[k], dst_ref=got[k], send_sem=send_sems.at[k],
                                              recv_sem=recv_sems.at[k], device_id=(x, y, 1 - c), device_id_type=MESH)
            cp.start()
            cps.append(cp)
        for k, cp in enumerate(cps):
            cp.wait()
            own = in_refs[k][pl.ds(pl.multiple_of(c * halves[k], 8), halves[k]), :]
            out_refs[k][...] = (own.astype(wire[k]).astype(F32) + got[k][...].astype(F32)).astype(wire[k])

    vm = pl.BlockSpec(memory_space=pltpu.VMEM)
    parts = [((h, b.shape[1]), w) for h, b, w in zip(halves, bufs, wire)]
    return list(pl.pallas_call(
        body, name="reduce_swap", in_specs=[vm] * n, out_specs=[vm] * n, out_shape=[_sds(sh, w) for sh, w in parts],
        scratch_shapes=[pltpu.VMEM(sh, w) for sh, w in parts] * 2 + [pltpu.SemaphoreType.DMA((n,))] * 2,
        compiler_params=_params())(*bufs))


def reduce_start(parts):
    n = len(parts)
    lands = [lax.empty((4,) + tuple(p.shape), p.dtype) for p in parts]

    def body(*refs):
        part_refs, land_refs, send_sems, recv_sems = refs[:n], refs[n:2 * n], refs[2 * n], refs[2 * n + 1]
        x, y, c, others = _position()
        for k in range(n):
            for j, (ox, oy) in enumerate(others):
                pltpu.make_async_remote_copy(
                    src_ref=part_refs[k], dst_ref=land_refs[k].at[2 * x + y], send_sem=send_sems.at[3 * k + j],
                    recv_sem=recv_sems.at[3 * k + j], device_id=(ox, oy, c), device_id_type=MESH).start()

    sems = pltpu.SemaphoreType.DMA((3 * n,))
    both = list(parts) + lands
    out = pl.pallas_call(
        body, name="reduce_start", in_specs=[HBM_SPEC] * (2 * n), out_specs=(SEM_SPEC, SEM_SPEC, *[HBM_SPEC] * (2 * n)),
        out_shape=(sems, sems, *[pltpu.HBM(a.shape, a.dtype) for a in both]),
        input_output_aliases={k: 2 + k for k in range(2 * n)}, compiler_params=_split_params(),
    )(*[_in_hbm(a) for a in both])
    return out[0], out[1], list(out[2:2 + n]), list(out[2 + n:])


def reduce_wait(send_sems, recv_sems, parts, lands, after):
    n = len(parts)

    def body(*refs):
        part_refs, land_refs, send_ref, recv_ref = refs[:n], refs[n:2 * n], refs[2 * n], refs[2 * n + 1]
        x, y, c, others = _position()
        for k in range(n):
            for j, (ox, oy) in enumerate(others):
                cp = pltpu.make_async_remote_copy(
                    src_ref=part_refs[k], dst_ref=land_refs[k].at[2 * ox + oy], send_sem=send_ref.at[3 * k + j],
                    recv_sem=recv_ref.at[3 * k + j], device_id=(ox, oy, c), device_id_type=MESH)
                cp.wait_send()
                cp.wait_recv()

    both = list(parts) + list(lands)
    out = pl.pallas_call(
        body, name="reduce_wait", in_specs=[HBM_SPEC] * (2 * n) + [SEM_SPEC, SEM_SPEC, HBM_SPEC],
        out_specs=[HBM_SPEC] * (2 * n), out_shape=[pltpu.HBM(a.shape, a.dtype) for a in both],
        input_output_aliases={k: k for k in range(2 * n)}, compiler_params=_split_params(),
    )(*both, send_sems, recv_sems, _in_hbm(after))
    return list(out[:n]), list(out[n:])


def reduce_share(parts, lands):
    n = len(parts)
    halves = [p.shape[0] for p in parts]

    def body(*refs):
        part_refs, land_refs, out_refs = refs[:n], refs[n:2 * n], refs[2 * n:3 * n]
        send_sems, recv_sems = refs[3 * n:]
        x, y, c, _ = _position()
        chip = 2 * x + y
        cps = []
        for k in range(n):
            mine = pl.ds(pl.multiple_of(c * halves[k], 8), halves[k])
            own = part_refs[k][...].astype(F32)
            total = jnp.where(chip == 0, own, land_refs[k][0].astype(F32))
            for entry in range(1, 4):
                total = total + jnp.where(chip == entry, own, land_refs[k][entry].astype(F32))
            out_refs[k][mine, :] = total
            cp = pltpu.make_async_remote_copy(
                src_ref=out_refs[k].at[mine], dst_ref=out_refs[k].at[mine], send_sem=send_sems.at[k],
                recv_sem=recv_sems.at[k], device_id=(x, y, 1 - c), device_id_type=MESH)
            cp.start()
            cps.append(cp)
        for cp in cps:
            cp.wait()

    vm = pl.BlockSpec(memory_space=pltpu.VMEM)
    return list(pl.pallas_call(
        body, name="reduce_share", in_specs=[vm] * (2 * n), out_specs=[vm] * n,
        out_shape=[_sds((2 * p.shape[0], p.shape[1]), F32) for p in parts],
        scratch_shapes=[pltpu.SemaphoreType.DMA((n,))] * 2, compiler_params=_params())(*parts, *lands))


def _local_step(x, target, small, need, ahead, emit_swap, emit_exchange):
    d = D_MODEL
    full = {}

    def handed(vec, token):
        return vec if token is None else token

    def token_rows(token):
        return [] if token is None else [token]

    def plus(acc, rows):
        return acc + rows[0] if rows else acc

    rb16, rbt16, rc16, rct16, lr_t, li_t = small["s5_operands"]
    ge, ge_slope, cs = s5_fwd(x, small["norm_mix0"], small["s5_d"], rb16, rc16, lr_t, li_t)
    full.update(need("glu", ge))

    def norm_rows(h, gains):
        xh, _ = _rms_hat(h)
        return [xh * g for g in gains]

    def glu_epilogue(accs, e, r):
        v, gt = accs[0] + r[0], accs[1] + r[1]
        h = e[0] + v * jax.nn.sigmoid(gt)
        return [h, v, gt] + norm_rows(h, r[2:])

    gain_mlp0 = handed(small["norm_mlp0"], ahead("mlp_in0", full["w_glu"], small["norm_mlp0"]))
    h1, val, gate, n1 = mm_nn(
        "glu", ge, full["w_glu"], [0, d], d, glu_epilogue, [F32, F32, F32, BF16], extras=[x],
        rowvecs=[(small["s5_b_glu"], 0), (small["s5_b_glu"], d), (gain_mlp0, 0)], tm=512, tn=d)

    def mlp_fwd(tag, h, n, w_in, get_w_out, next_gains, head=None):
        def in_epilogue(accs, e, rv):
            pos = jnp.maximum(accs[0], 0.0)
            return [pos * pos, 2.0 * pos]

        r, slope = mm_nn("mlp_in" + tag, n, w_in, [0], w_in.shape[1], in_epilogue, [BF16, BF16], tm=2048)
        w_out = get_w_out(r)

        def epilogue(accs, e, rv):
            h_out = e[0] + accs[0]
            return [h_out] + norm_rows(h_out, rv)

        if head is not None:
            return head(r, w_out, h), (n, r, slope)
        outs = mm_nn("mlp_out" + tag, r, w_out, [0], d, epilogue, [F32] + [BF16] * len(next_gains), extras=[h],
                     rowvecs=[(g, 0) for g in next_gains], tm=512, tn=d)
        return outs[0], outs[1:], (n, r, slope)

    full.update(need("mlp_in0", h1))

    def w_out0(after):
        full.update(need("mlp_out0", after))
        return full["w_out0"]

    h2, (nkv, n2), mlp0 = mlp_fwd("0", h1, n1, full["w_in0"], w_out0, [small["norm_kv"], small["norm_mix1"]])

    full.update(need("attn", h2))
    kvw = 2 * N_KV * HEAD_DIM
    (kv,) = mm_nn("kv_proj", nkv, full["w_kv"], [0], kvw, lambda accs, e, r: [accs[0] + r[0]], [BF16],
                  rowvecs=[(small["b_kv"], 0)], tm=2048)
    (q,) = mm_nn("q_proj", n2, full["w_q"], [0], d, lambda accs, e, r: [accs[0] + r[0]], [BF16],
                 rowvecs=[(small["b_q"], 0)], tm=2048)
    sinks = small["sinks"].reshape(N_Q)
    o = attn_fwd(q, kv, sinks)
    def o_epilogue(accs, e, r):
        h_out = e[0] + accs[0] + r[0]
        return [h_out] + norm_rows(h_out, r[1:])

    bias_o = handed(small["b_o"], ahead("mlp_in1", o, small["b_o"]))
    h3, n3 = mm_nn("o_proj", o, full["w_o"], [0], d, o_epilogue, [F32, BF16], extras=[h2],
                   rowvecs=[(bias_o, 0), (small["norm_mlp1"], 0)], tm=512, tn=d)
    full.update(need("mlp_in1", h3, then="mlp_out1"))

    def w_out1(after):
        full.update(need("mlp_out1", after))
        return full["w_out1"]

    def loss_head(r, w_out, h):
        def epilogue(accs, e, rv):
            xh, rr = _rms_hat(e[0] + accs[0])
            err = xh * rv[0] - e[1]
            dy = err * (1.0 / d)
            dxh = dy * rv[0]
            dx = rr * (dxh - xh * jnp.mean(dxh * xh, axis=-1, keepdims=True))
            loss = jnp.full((1, d), 0.5 * jnp.sum(jnp.mean(err * err, axis=-1, keepdims=True)), F32)
            return [dx, dx, loss, jnp.sum(dy * xh, axis=0, keepdims=True)]

        return mm_nn("mlp_out1", r, w_out, [0], d, epilogue, [F32, BF16], extras=[h, target],
                     rowvecs=[(small["norm_final"], 0)], n_sums=2, tm=512, tn=d)

    (dh, dhb, loss_tile, dg_final), mlp1 = mlp_fwd("1", h3, n3, full["w_in1"], w_out1, [], head=loss_head)

    grads_small, grads_full = {"norm_final": dg_final}, {}
    ident = lambda acc, e, r: [plus(acc, r)]
    layer1 = ["w_out1", "w_in1", "w_o", "w_q", "w_kv"]
    layer0 = ["w_out0", "w_in0", "w_glu"]

    def norm_bwd_rows(x_rows, res, dys, gains):
        xh, r = _rms_hat(x_rows)
        dxh = sum(dy * g for dy, g in zip(dys, gains))
        dx = r * (dxh - xh * jnp.mean(dxh * xh, axis=-1, keepdims=True)) + res
        return dx, [jnp.sum(dy * xh, axis=0, keepdims=True) for dy in dys]

    def mlp_bwd(tag, dh, dhb, h_in, gain, w_in, w_out, saved, token=None):
        n, r, slope = saved
        grads_full["w_out" + tag] = mm_tn("dw_out" + tag, r, dhb, tn=1024)
        (da,) = mm_nt("mlp_da" + tag, dhb, w_out, lambda acc, e, rv: [plus(acc * e[0].astype(F32), rv)], [BF16],
                      extras=[slope], rowvecs=token_rows(token), tm=2048)
        grads_full["w_in" + tag] = mm_tn("dw_in" + tag, n, da, tn=1024)

        def epilogue(acc, e, rv):
            dx, dgs = norm_bwd_rows(e[0], e[1], [acc], rv)
            return [dx, dx, jnp.sum(dx, axis=0, keepdims=True)] + dgs

        dx, dxb, colsum, dg = mm_nt("mlp_dn" + tag, da, w_in, epilogue, [F32, BF16], extras=[h_in, dh], rowvecs=[gain],
                                    n_sums=2, tm=512, tk=d)
        grads_small["norm_mlp" + tag] = dg
        return dx, dxb, colsum

    dh3, dh3b, colsum3 = mlp_bwd("1", dh, dhb, h3, small["norm_mlp1"], full["w_in1"], full["w_out1"], mlp1)
    grads_small["b_o"] = colsum3
    grads_full["w_o"] = mm_tn("dw_o", o, dh3b, tn=1024)
    (do,) = mm_nt("attn_do", dh3b, full["w_o"], ident, [BF16], tm=2048)
    dq, dbq, dprev, dcur, dsink = attn_bwd(q, kv, do, sinks)
    dkv, dbkv = kv_combine(dprev, dcur)
    grads_small["b_q"], grads_small["b_kv"], grads_small["sinks"] = dbq, dbkv, dsink
    grads_full["w_q"] = mm_tn("dw_q", n2, dq, tn=1024)
    grads_full["w_kv"] = mm_tn("dw_kv", nkv, dkv, tk=1024)
    token = emit_swap("layer1", {n: grads_full[n] for n in layer1}, (1, d))
    (dnkv,) = mm_nt("kv_dn", dkv, full["w_kv"], ident, [F32], rowvecs=token_rows(token), tm=2048, tk=1024)

    def attn_dn_epilogue(acc, e, rv):
        dx, dgs = norm_bwd_rows(e[0], e[1], [acc, e[2]], rv)
        return [dx, dx] + dgs

    dh2, dh2b, dg_mix1, dg_kv = mm_nt("attn_dn", dq, full["w_q"], attn_dn_epilogue, [F32, BF16], extras=[h2, dh3, dnkv],
                                      rowvecs=[small["norm_mix1"], small["norm_kv"]], n_sums=2, tm=512, tk=d)
    grads_small["norm_mix1"], grads_small["norm_kv"] = dg_mix1, dg_kv
    token = emit_exchange("layer1", dh2b, (1, full["w_out0"].shape[0]))
    dh1, _, _ = mlp_bwd("0", dh2, dh2b, h1, small["norm_mlp0"], full["w_in0"], full["w_out0"], mlp0, token)

    dz, db_glu = glu_bwd(dh1, val, gate)
    grads_small["s5_b_glu"] = db_glu
    grads_full["w_glu"] = mm_tn("dw_glu", ge, dz, tn=1024)
    token = emit_swap("layer0", {n: grads_full[n] for n in layer0}, (1, d))
    (dy2,) = mm_nt("glu_dy", dz, full["w_glu"], lambda acc, e, rv: [plus(acc, rv) * e[0]], [F32], extras=[ge_slope],
                   rowvecs=token_rows(token), tm=512, tk=1024)
    d_skip = handed(small["s5_d"], emit_exchange("layer0", dy2, small["s5_d"]))
    grad_x, dd, drb, drc, dlr, dli, dg_mix0 = s5_bwd(x, small["norm_mix0"], dy2, dh1, d_skip, cs, rb16, rbt16, rct16, lr_t, li_t)
    grads_small["s5_d"] = dd
    grads_small["s5_mats"] = (drb, drc, dlr, dli)
    grads_small["norm_mix0"] = dg_mix0
    return loss_tile, grad_x, grads_small


SMALL_NAMES = ["norm_mix", "norm_mlp", "norm_kv", "norm_final", "s5_a_re", "s5_a_im", "s5_log_dt", "s5_b_re", "s5_b_im",
               "s5_c_re", "s5_c_im", "s5_d", "s5_b_glu", "b_kv", "b_q", "sinks", "b_o"]
BIG_NAMES = ["s5_w_glu", "w_kv", "w_q", "w_o", "w_mlp_in", "w_mlp_out"]
WEIGHT_ORDER = ["norm_mix", "norm_mlp", "norm_kv", "norm_final", "s5_a_re", "s5_a_im", "s5_log_dt", "s5_b_re", "s5_b_im",
                "s5_c_re", "s5_c_im", "s5_d", "s5_w_glu", "s5_b_glu", "w_kv", "b_kv", "w_q", "b_q", "sinks", "w_o", "b_o",
                "w_mlp_in", "w_mlp_out"]


def kernel(x, norm_mix, norm_mlp, norm_kv, norm_final, s5_a_re, s5_a_im, s5_log_dt, s5_b_re, s5_b_im, s5_c_re, s5_c_im, s5_d, s5_w_glu, s5_b_glu, w_kv, b_kv, w_q, b_q, sinks, w_o, b_o, w_mlp_in, w_mlp_out, loss_target, m_norm_mix, m_norm_mlp, m_norm_kv, m_norm_final, m_s5_a_re, m_s5_a_im, m_s5_log_dt, m_s5_b_re, m_s5_b_im, m_s5_c_re, m_s5_c_im, m_s5_d, m_s5_w_glu, m_s5_b_glu, m_w_kv, m_b_kv, m_w_q, m_b_q, m_sinks, m_w_o, m_b_o, m_w_mlp_in, m_w_mlp_out, v_norm_mix, v_norm_mlp, v_norm_kv, v_norm_final, v_s5_a_re, v_s5_a_im, v_s5_log_dt, v_s5_b_re, v_s5_b_im, v_s5_c_re, v_s5_c_im, v_s5_d, v_s5_w_glu, v_s5_b_glu, v_w_kv, v_b_kv, v_w_q, v_b_q, v_sinks, v_w_o, v_b_o, v_w_mlp_in, v_w_mlp_out):
    env = dict(locals())
    w = {n: env[n] for n in WEIGHT_ORDER}
    mom = {n: env["m_" + n] for n in WEIGHT_ORDER}
    var = {n: env["v_" + n] for n in WEIGHT_ORDER}
    d = D_MODEL
    xi, yi, ci = lax.axis_index("x"), lax.axis_index("y"), lax.axis_index("c")
    chip = 2 * xi + yi
    where = jnp.stack([ci, chip]).astype(jnp.int32)

    dsh, bsh = s5_d.shape[1], s5_b_glu.shape[1]
    packed = jnp.concatenate([s5_d.reshape(-1, 128), s5_b_glu.reshape(-1, 128)])
    n_d, n_b = dsh // 128, bsh // 128
    slab = lax.dynamic_update_slice(jnp.zeros((4, 8, 128), F32), jnp.pad(packed, ((0, 8 - n_d - n_b), (0, 0)))[None],
                                    (chip, 0, 0))

    big = [s5_w_glu, w_kv[None], w_q, w_o, w_mlp_in, w_mlp_out]
    entries = [(0, 0, "col"), (1, 0, "row"), (2, 0, "row"), (3, 0, "row"), (4, 0, "col"), (4, 1, "col"),
               (5, 0, "row"), (5, 1, "row")]
    names = ["w_glu", "w_kv", "w_q", "w_o", "w_in0", "w_in1", "w_out0", "w_out1"]
    kinds = dict(zip(names, [k for _, _, k in entries]))
    shard_shapes = dict(zip(names, [tuple(big[a].shape[1:]) for a, _, _ in entries]))

    placed_w = dict(zip(names, cast_place(big, entries, where)))
    placed_w["vectors"], kinds["vectors"], shard_shapes["vectors"] = slab, "slab", None
    gather_groups = {"glu": ["w_glu"], "mlp_in0": ["w_in0"], "mlp_out0": ["w_out0"], "attn": ["w_kv", "w_q", "w_o"],
                     "mlp_in1": ["w_in1"], "mlp_out1": ["w_out1"]}
    order = ["vectors"] + [n for members in gather_groups.values() for n in members]
    send, recv, thru, log_dt = gather_start([placed_w[n] for n in order], [kinds[n] for n in order],
                                            [shard_shapes[n] for n in order], s5_log_dt)
    started = dict(zip(order, thru))
    (gathered_rows,) = gather_wait("gather_wait_vectors", send, recv, [started["vectors"]], ["slab"], [None], None, 0)
    d_full = gathered_rows[:, 0:n_d].reshape(1, -1)
    bglu_full = gathered_rows[:, n_d:n_d + n_b].reshape(1, -1)

    forwarding = {}

    def ahead(group, after, carry, passing=()):
        members = gather_groups[group]
        ks, shapes = [kinds[n] for n in members], [shard_shapes[n] for n in members]
        d2d_send, d2d_recv, landed, passed, tok = forward_start(
            "forward_start_" + group, send, recv, [started[n] for n in members], ks, shapes, after,
            order.index(members[0]), carry, passing)
        forwarding[group] = (d2d_send, d2d_recv, landed)
        return passed if passing else tok

    def need(group, after, then=None):
        members = gather_groups[group]
        ks, shapes = [kinds[n] for n in members], [shard_shapes[n] for n in members]
        if group in forwarding:
            arrays = forward_wait("forward_wait_" + group, *forwarding[group], ks, shapes, after)
        else:
            landed = gather_wait("gather_wait_" + group, send, recv, [started[n] for n in members], ks, shapes, after,
                                 order.index(members[0]))
            arrays = forward_halves("forward_halves_" + group, landed, ks, shapes)
        if then is not None:
            arrays = ahead(then, after, (8, 128), arrays)
        return dict(zip(members, arrays))

    swapping, exchanging = {}, {}

    def emit_swap(group, partial, carry):
        members = list(partial)
        send, recv, mine, lands, tok = swap_start("swap_start_" + group, [partial[n] for n in members],
                                                  [kinds[n] for n in members], carry)
        swapping[group] = (members, send, recv, mine, lands)
        return tok

    def emit_exchange(group, after, carry):
        members, send, recv, mine, lands = swapping[group]
        ks, shapes = [kinds[n] for n in members], [shard_shapes[n] for n in members]
        mine, landed = swap_wait("swap_wait_" + group, send, recv, mine, lands, ks, after)
        sums = add_halves("add_halves_" + group, mine, landed, ks, where)
        send, recv, parts, lands, tok = exchange_start("exchange_start_" + group, sums, ks, shapes, carry)
        exchanging[group] = (members, send, recv, parts, lands)
        return tok

    s5_args = (s5_a_re[0], s5_a_im[0], log_dt[0], s5_b_re[0], s5_b_im[0])
    small = {
        "norm_mix0": norm_mix[0:1], "norm_mix1": norm_mix[1:2], "norm_mlp0": norm_mlp[0:1], "norm_mlp1": norm_mlp[1:2],
        "norm_kv": norm_kv.reshape(1, d), "norm_final": norm_final.reshape(1, d), "s5_operands": s5_prep(*s5_args, s5_c_re[0], s5_c_im[0]),
        "s5_d": d_full, "s5_b_glu": bglu_full,
        "b_kv": b_kv.reshape(1, -1), "b_q": b_q, "sinks": sinks, "b_o": b_o,
    }
    loss_row, grad_x, gs = _local_step(x[0], loss_target[0], small, need, ahead, emit_swap, emit_exchange)

    mats, lams = s5_compact(*gs["s5_mats"])
    rows = [gs["norm_mix0"], gs["norm_mix1"], gs["norm_mlp0"], gs["norm_mlp1"], gs["norm_kv"], gs["norm_final"], gs["s5_d"],
            gs["b_q"], gs["b_o"], gs["s5_b_glu"], gs["b_kv"], gs["sinks"], loss_row, jnp.zeros((2, d), F32)]
    small_send, small_recv, small_parts, small_lands = reduce_start(
        reduce_swap([jnp.concatenate(rows, axis=0), lams, mats], [F32, F32, BF16]))

    reduced = [None] * len(big)
    where_of = dict(zip(names, entries))
    for group in ("layer1", "layer0"):
        members, send, recv, parts, lands = exchanging[group]
        ks, shapes = [kinds[n] for n in members], [shard_shapes[n] for n in members]
        parts, lands = exchange_wait("exchange_wait_" + group, send, recv, parts, lands, ks, shapes, small_lands[-1])
        targets = [where_of[n][0] for n in members]
        sums = sum_shards("sum_shards_" + group, parts, lands, ks, shapes, where, [where_of[n][1] for n in members],
                          [big[a].shape[0] for a in targets], [reduced[a] for a in targets])
        for a, arr in zip(targets, sums):
            reduced[a] = arr
    share_send, share_recv, reduced, _ = share_start(reduced, entries, (8, 128))

    vecs, lams, mats = reduce_share(*reduce_wait(small_send, small_recv, small_parts, small_lands, reduced[0]))
    grads = split_vectors(where, vecs, dsh, bsh)
    loss = grads.pop("loss")[0, 0]
    g_are, g_aim, g_dt, g_bre, g_bim, dc_re, dc_im = s5_param_bwd(mats, lams, *s5_args)
    grads.update({"s5_a_re": g_are[None], "s5_a_im": g_aim[None], "s5_log_dt": g_dt[None], "s5_b_re": g_bre[None],
                  "s5_b_im": g_bim[None], "s5_c_re": dc_re[None], "s5_c_im": dc_im[None]})

    delta, new_m, new_v = {}, {}, {}

    def view(n, a):
        return a.reshape(1, -1) if a.ndim == 1 else jnp.swapaxes(a, -1, -2) if n in ("s5_b_re", "s5_b_im") else a

    sw, sg, sm, sv = ([view(n, t[n]) for n in SMALL_NAMES] for t in (w, grads, mom, var))
    for n, a, b, c_ in zip(SMALL_NAMES, *adamw_native("adamw_small", sw, sg, sm, sv)):
        delta[n], new_m[n], new_v[n] = (view(n, t) if t.ndim == 4 else t for t in (a, b, c_))

    reduced = share_wait(share_send, share_recv, reduced, entries, new_v["s5_c_re"])
    for n, g in zip(BIG_NAMES, reduced):
        grads[n] = g.reshape(w[n].shape)
    flat = lambda t: [t[n].reshape(-1, t[n].shape[-1]) for n in BIG_NAMES]
    for table, arrays in zip((grads, delta, new_m, new_v), adamw("adamw_big", flat(w), flat(grads), flat(mom), flat(var))):
        for n, a in zip(BIG_NAMES, arrays):
            table[n] = a.reshape(w[n].shape)

    out = [loss.reshape(()), grad_x[None]]
    for table in (grads, delta, new_m, new_v):
        out += [table[n].reshape(w[n].shape) for n in WEIGHT_ORDER]
    return tuple(out)
```
